```python
import math
import jax, jax.numpy as jnp
from jax import lax
import numpy as np

D_MODEL = 1024
BATCH = 16
SEQ = 2048
DEPTH = 4

CHUNK = 64
N_MIXERS = 2
N_GDN = (DEPTH + 1) // 2
N_HGRN = DEPTH // 2
EPS = 1e-6

GDN_HEADS = 8
GDN_HEAD_DIM = D_MODEL // GDN_HEADS
GDN_KEY_WIDTH = GDN_HEADS * GDN_HEAD_DIM
GDN_VAL_WIDTH = GDN_HEADS * GDN_HEAD_DIM
GDN_CONV_WIDTH = 2 * GDN_KEY_WIDTH + GDN_VAL_WIDTH
GDN_IN_WIDTH = GDN_CONV_WIDTH + GDN_VAL_WIDTH + 2 * GDN_HEADS
CONV_K = 4

HGRN_FORGET_DIM = 128
HGRN_HEADS = D_MODEL // HGRN_FORGET_DIM
HGRN_VALUE_DIM = D_MODEL // HGRN_HEADS
HGRN_WIDTH = HGRN_HEADS * HGRN_FORGET_DIM
HGRN_IN_WIDTH = 4 * HGRN_WIDTH
SUB = 16
N_SUB = CHUNK // SUB

MLP_HIDDEN = 4 * D_MODEL

kernel_name = "hybrid_gdn_hgrn2_stream_encoder"


def _rmsnorm(x, w):
    xf = x.astype(jnp.float32)
    y = xf * lax.rsqrt(jnp.mean(xf * xf, axis=-1, keepdims=True) + EPS)
    return (y * w.astype(jnp.float32)).astype(x.dtype)


def _l2norm(x):
    return x * lax.rsqrt(jnp.sum(x * x, axis=-1, keepdims=True) + EPS)


def _causal_conv(x, w):
    K = w.shape[0]
    T = x.shape[1]
    xp = jnp.pad(x, ((0, 0), (K - 1, 0), (0, 0)))
    y = xp[:, 0:T] * w[0]
    for kk in range(1, K):
        y = y + xp[:, kk:kk + T] * w[kk]
    return y


def _to_chunks(a, nc):
    B = a.shape[0]
    H = a.shape[2]
    a = a.reshape((B, nc, CHUNK, H) + a.shape[3:])
    return jnp.moveaxis(a, (1, 3), (0, 2))


def _from_chunks(o):
    nc, B, H, C, Dv = o.shape
    return jnp.moveaxis(o, (0, 2), (1, 3)).reshape(B, nc * C, H, Dv)


def _gdn_chunked(q, k, v, beta, g):
    B, T, H, DK = q.shape
    DV = v.shape[-1]
    nc = T // CHUNK
    q = _to_chunks(q * (DK ** -0.5), nc)
    k = _to_chunks(k, nc)
    v = _to_chunks(v, nc)
    beta = _to_chunks(beta, nc)
    gc = jnp.cumsum(_to_chunks(g, nc), axis=-1)
    causal = jnp.tril(jnp.ones((CHUNK, CHUNK), dtype=bool))
    strict = jnp.tril(jnp.ones((CHUNK, CHUNK), dtype=bool), k=-1)
    decay = jnp.exp(jnp.where(causal, gc[..., :, None] - gc[..., None, :], -jnp.inf))
    kb = k * beta[..., None]
    L = jnp.where(strict, jnp.einsum('nbhik,nbhjk->nbhij', kb, k) * decay, 0.0)
    rhs = jnp.concatenate([v * beta[..., None], kb * jnp.exp(gc)[..., None]], axis=-1)
    sol = lax.linalg.triangular_solve(L, rhs, left_side=True, lower=True, unit_diagonal=True)
    u = sol[..., :DV]
    w = sol[..., DV:]
    a_qk = jnp.where(causal, jnp.einsum('nbhik,nbhjk->nbhij', q, k) * decay, 0.0)
    q_dec = q * jnp.exp(gc)[..., None]
    k_dec = k * jnp.exp(gc[..., -1:] - gc)[..., None]
    chunk_decay = jnp.exp(gc[..., -1])

    def step(S, xs):
        qd, a_c, u_c, w_c, kd, dl = xs
        v_new = u_c - jnp.einsum('bhck,bhkv->bhcv', w_c, S)
        o = jnp.einsum('bhck,bhkv->bhcv', qd, S) + jnp.einsum('bhij,bhjv->bhiv', a_c, v_new)
        S = S * dl[..., None, None] + jnp.einsum('bhck,bhcv->bhkv', kd, v_new)
        return S, o

    S0 = jnp.zeros((B, H, DK, DV), jnp.float32)
    _, o = lax.scan(step, S0, (q_dec, a_qk, u, w, k_dec, chunk_decay))
    return _from_chunks(o)


def _hgrn2_chunked(q, k, v, g):
    B, T, H, DK = q.shape
    DV = v.shape[-1]
    nc = T // CHUNK
    q = _to_chunks(q * (DK ** -0.5), nc)
    k = _to_chunks(k, nc)
    v = _to_chunks(v, nc)
    g = _to_chunks(g, nc)
    gc = jnp.cumsum(g, axis=-2)
    gb = (gc - g)[:, :, :, ::SUB]
    pos = jnp.arange(CHUNK)
    off_mask = (pos[None, :] // SUB) < jnp.arange(N_SUB)[:, None]
    diag_mask = jnp.tril(jnp.ones((SUB, SUB), dtype=bool))

    def step(S, xs):
        q_c, k_c, v_c, gc_c, gb_c = xs
        qs = q_c.reshape(B, H, N_SUB, SUB, DK)
        ks = k_c.reshape(B, H, N_SUB, SUB, DK)
        vs = v_c.reshape(B, H, N_SUB, SUB, DV)
        gs = gc_c.reshape(B, H, N_SUB, SUB, DK)
        q_off = qs * jnp.exp(gs - gb_c[:, :, :, None])
        k_off = k_c[:, :, None] * jnp.exp(
            jnp.where(off_mask[:, :, None], gb_c[:, :, :, None] - gc_c[:, :, None], -jnp.inf))
        a_off = jnp.einsum('bhsik,bhsjk->bhsij', q_off, k_off)
        dec = jnp.exp(jnp.where(diag_mask[:, :, None],
                                gs[..., :, None, :] - gs[..., None, :, :], -jnp.inf))
        a_diag = jnp.einsum('bhsik,bhsjk,bhsijk->bhsij', qs, ks, dec)
        o = (jnp.einsum('bhsij,bhjv->bhsiv', a_off, v_c)
             + jnp.einsum('bhsij,bhsjv->bhsiv', a_diag, vs)).reshape(B, H, CHUNK, DV)
        o = o + jnp.einsum('bhck,bhkv->bhcv', q_c * jnp.exp(gc_c), S)
        g_last = gc_c[:, :, -1]
        S = S * jnp.exp(g_last)[..., None] + jnp.einsum(
            'bhck,bhcv->bhkv', k_c * jnp.exp(g_last[:, :, None] - gc_c), v_c)
        return S, o

    S0 = jnp.zeros((B, H, DK, DV), jnp.float32)
    _, o = lax.scan(step, S0, (q, k, v, gc, gb))
    return _from_chunks(o)


def _gated_deltanet(y, w_in, conv_w, a_log, dt_bias, onorm_w, w_out):
    B, T, _ = y.shape
    f32 = jnp.float32
    proj = y @ w_in
    qkv = proj[..., :GDN_CONV_WIDTH]
    gate = proj[..., GDN_CONV_WIDTH:GDN_CONV_WIDTH + GDN_VAL_WIDTH]
    a = proj[..., GDN_CONV_WIDTH + GDN_VAL_WIDTH:GDN_CONV_WIDTH + GDN_VAL_WIDTH + GDN_HEADS]
    b = proj[..., GDN_CONV_WIDTH + GDN_VAL_WIDTH + GDN_HEADS:]
    qkv = jax.nn.silu(_causal_conv(qkv, conv_w)).astype(f32)
    q = _l2norm(qkv[..., :GDN_KEY_WIDTH].reshape(B, T, GDN_HEADS, GDN_HEAD_DIM))
    k = _l2norm(qkv[..., GDN_KEY_WIDTH:2 * GDN_KEY_WIDTH].reshape(B, T, GDN_HEADS, GDN_HEAD_DIM))
    v = qkv[..., 2 * GDN_KEY_WIDTH:].reshape(B, T, GDN_HEADS, GDN_HEAD_DIM)
    beta = jax.nn.sigmoid(b.astype(f32))
    g = -jnp.exp(a_log.astype(f32)) * jax.nn.softplus(a.astype(f32) + dt_bias.astype(f32))
    o = _gdn_chunked(q, k, v, beta, g)
    o = _rmsnorm(o, onorm_w) * jax.nn.silu(gate.astype(f32)).reshape(B, T, GDN_HEADS, GDN_HEAD_DIM)
    return o.reshape(B, T, GDN_VAL_WIDTH).astype(y.dtype) @ w_out


def _hgrn2(y, w_in, lb, gnorm_w, w_out):
    B, T, _ = y.shape
    f32 = jnp.float32
    proj = y @ w_in
    q, f, i, gate = jnp.split(proj, 4, axis=-1)
    f = f.astype(f32)
    log_forget = jnp.logaddexp(jnp.log(lb), jnp.log1p(-lb) + jax.nn.log_sigmoid(f))
    k = (1.0 - lb) * jax.nn.sigmoid(-f)
    q = jax.nn.silu(q.astype(f32))
    heads = lambda t: t.reshape(B, T, HGRN_HEADS, -1)
    o = _hgrn2_chunked(heads(q), heads(k), heads(i.astype(f32)), heads(log_forget))
    o = _rmsnorm(o.reshape(B, T, HGRN_WIDTH), gnorm_w) * jax.nn.silu(gate.astype(f32))
    return o.astype(y.dtype) @ w_out


def _sq_relu_mlp(h, w_up, w_down):
    return jnp.square(jax.nn.relu(h @ w_up)) @ w_down


def _fwd_setup_inputs(seed: int = 0) -> dict:
    key = jax.random.key(seed)
    ks = jax.random.split(key, 17)
    f32 = jnp.float32
    nrm = lambda kk, shape: jax.random.normal(kk, shape, f32)
    x = nrm(ks[0], (BATCH, SEQ, D_MODEL))
    gdn_w_in = nrm(ks[1], (N_GDN, D_MODEL, GDN_IN_WIDTH)) * D_MODEL ** -0.5
    gdn_conv = nrm(ks[2], (N_GDN, CONV_K, GDN_CONV_WIDTH)) * CONV_K ** -0.5
    gdn_a_log = jnp.log(jax.random.uniform(ks[3], (N_GDN, GDN_HEADS), f32, 1.0, 16.0))
    dt = jnp.exp(jax.random.uniform(ks[4], (N_GDN, GDN_HEADS), f32, math.log(1e-3), math.log(1e-1)))
    gdn_dt_bias = dt + jnp.log(-jnp.expm1(-dt))
    gdn_onorm = 1.0 + 0.02 * nrm(ks[5], (N_GDN, GDN_HEAD_DIM))
    gdn_w_out = nrm(ks[6], (N_GDN, GDN_VAL_WIDTH, D_MODEL)) * GDN_VAL_WIDTH ** -0.5
    hgrn_w_in = nrm(ks[7], (N_HGRN, D_MODEL, HGRN_IN_WIDTH)) * D_MODEL ** -0.5
    hgrn_lb_logits = 0.1 * nrm(ks[8], (DEPTH, HGRN_WIDTH))
    hgrn_gnorm = 1.0 + 0.02 * nrm(ks[9], (N_HGRN, HGRN_WIDTH))
    hgrn_w_out = nrm(ks[10], (N_HGRN, HGRN_WIDTH, D_MODEL)) * HGRN_WIDTH ** -0.5
    norm_mix = 1.0 + 0.02 * nrm(ks[11], (DEPTH, D_MODEL))
    norm_mlp = 1.0 + 0.02 * nrm(ks[12], (DEPTH, D_MODEL))
    mlp_w_up = nrm(ks[13], (DEPTH, D_MODEL, MLP_HIDDEN)) * D_MODEL ** -0.5
    mlp_w_down = nrm(ks[14], (DEPTH, MLP_HIDDEN, D_MODEL)) * MLP_HIDDEN ** -0.5
    norm_final = 1.0 + 0.02 * nrm(ks[15], (D_MODEL,))
    return {"x": x, "gdn_w_in": gdn_w_in, "gdn_conv": gdn_conv, "gdn_a_log": gdn_a_log,
            "gdn_dt_bias": gdn_dt_bias, "gdn_onorm": gdn_onorm, "gdn_w_out": gdn_w_out,
            "hgrn_w_in": hgrn_w_in, "hgrn_lb_logits": hgrn_lb_logits, "hgrn_gnorm": hgrn_gnorm,
            "hgrn_w_out": hgrn_w_out, "norm_mix": norm_mix, "norm_mlp": norm_mlp,
            "mlp_w_up": mlp_w_up, "mlp_w_down": mlp_w_down, "norm_final": norm_final}


def _fwd_reference(x, gdn_w_in, gdn_conv, gdn_a_log, gdn_dt_bias, gdn_onorm, gdn_w_out,
              hgrn_w_in, hgrn_lb_logits, hgrn_gnorm, hgrn_w_out, norm_mix, norm_mlp,
              mlp_w_up, mlp_w_down, norm_final):
    sm = jax.nn.softmax(hgrn_lb_logits.astype(jnp.float32), axis=0)
    lower_bounds = jnp.cumsum(sm, axis=0) - sm[0]
    h = x
    for i in range(DEPTH):
        j = i // N_MIXERS
        y = _rmsnorm(h, norm_mix[i])
        if i % N_MIXERS == 0:
            y = _gated_deltanet(y, gdn_w_in[j], gdn_conv[j], gdn_a_log[j], gdn_dt_bias[j],
                                gdn_onorm[j], gdn_w_out[j])
        else:
            y = _hgrn2(y, hgrn_w_in[j], lower_bounds[i], hgrn_gnorm[j], hgrn_w_out[j])
        h = h + y.astype(h.dtype)
        h = h + _sq_relu_mlp(_rmsnorm(h, norm_mlp[i]), mlp_w_up[i], mlp_w_down[i]).astype(h.dtype)
    return _rmsnorm(h, norm_final)


import jax as _jax
import jax.numpy as _jnp

TWIN_FORMAT = 'train_step'
FWD_PARAMS = ['x', 'gdn_w_in', 'gdn_conv', 'gdn_a_log', 'gdn_dt_bias', 'gdn_onorm', 'gdn_w_out', 'hgrn_w_in', 'hgrn_lb_logits', 'hgrn_gnorm', 'hgrn_w_out', 'norm_mix', 'norm_mlp', 'mlp_w_up', 'mlp_w_down', 'norm_final']
TWIN_WEIGHTS = ['gdn_w_in', 'gdn_conv', 'gdn_a_log', 'gdn_dt_bias', 'gdn_onorm', 'gdn_w_out', 'hgrn_w_in', 'hgrn_lb_logits', 'hgrn_gnorm', 'hgrn_w_out', 'norm_mix', 'norm_mlp', 'mlp_w_up', 'mlp_w_down', 'norm_final']
TWIN_DIFF_INPUT = 'x'
TWIN_INPUTS = ['x', 'gdn_w_in', 'gdn_conv', 'gdn_a_log', 'gdn_dt_bias', 'gdn_onorm', 'gdn_w_out', 'hgrn_w_in', 'hgrn_lb_logits', 'hgrn_gnorm', 'hgrn_w_out', 'norm_mix', 'norm_mlp', 'mlp_w_up', 'mlp_w_down', 'norm_final', 'loss_target', 'm_gdn_w_in', 'm_gdn_conv', 'm_gdn_a_log', 'm_gdn_dt_bias', 'm_gdn_onorm', 'm_gdn_w_out', 'm_hgrn_w_in', 'm_hgrn_lb_logits', 'm_hgrn_gnorm', 'm_hgrn_w_out', 'm_norm_mix', 'm_norm_mlp', 'm_mlp_w_up', 'm_mlp_w_down', 'm_norm_final', 'v_gdn_w_in', 'v_gdn_conv', 'v_gdn_a_log', 'v_gdn_dt_bias', 'v_gdn_onorm', 'v_gdn_w_out', 'v_hgrn_w_in', 'v_hgrn_lb_logits', 'v_hgrn_gnorm', 'v_hgrn_w_out', 'v_norm_mix', 'v_norm_mlp', 'v_mlp_w_up', 'v_mlp_w_down', 'v_norm_final']
TWIN_OUTPUTS = ['loss', 'grad_x', 'grad_gdn_w_in', 'grad_gdn_conv', 'grad_gdn_a_log', 'grad_gdn_dt_bias', 'grad_gdn_onorm', 'grad_gdn_w_out', 'grad_hgrn_w_in', 'grad_hgrn_lb_logits', 'grad_hgrn_gnorm', 'grad_hgrn_w_out', 'grad_norm_mix', 'grad_norm_mlp', 'grad_mlp_w_up', 'grad_mlp_w_down', 'grad_norm_final', 'delta_gdn_w_in', 'delta_gdn_conv', 'delta_gdn_a_log', 'delta_gdn_dt_bias', 'delta_gdn_onorm', 'delta_gdn_w_out', 'delta_hgrn_w_in', 'delta_hgrn_lb_logits', 'delta_hgrn_gnorm', 'delta_hgrn_w_out', 'delta_norm_mix', 'delta_norm_mlp', 'delta_mlp_w_up', 'delta_mlp_w_down', 'delta_norm_final', 'new_m_gdn_w_in', 'new_m_gdn_conv', 'new_m_gdn_a_log', 'new_m_gdn_dt_bias', 'new_m_gdn_onorm', 'new_m_gdn_w_out', 'new_m_hgrn_w_in', 'new_m_hgrn_lb_logits', 'new_m_hgrn_gnorm', 'new_m_hgrn_w_out', 'new_m_norm_mix', 'new_m_norm_mlp', 'new_m_mlp_w_up', 'new_m_mlp_w_down', 'new_m_norm_final', 'new_v_gdn_w_in', 'new_v_gdn_conv', 'new_v_gdn_a_log', 'new_v_gdn_dt_bias', 'new_v_gdn_onorm', 'new_v_gdn_w_out', 'new_v_hgrn_w_in', 'new_v_hgrn_lb_logits', 'new_v_hgrn_gnorm', 'new_v_hgrn_w_out', 'new_v_norm_mix', 'new_v_norm_mlp', 'new_v_mlp_w_up', 'new_v_mlp_w_down', 'new_v_norm_final']
TWIN_LEAF_KINDS = {'loss': 'loss', 'grad_x': 'grad_x', 'grad_gdn_w_in': 'grad_w', 'grad_gdn_conv': 'grad_w', 'grad_gdn_a_log': 'grad_w', 'grad_gdn_dt_bias': 'grad_w', 'grad_gdn_onorm': 'grad_w', 'grad_gdn_w_out': 'grad_w', 'grad_hgrn_w_in': 'grad_w', 'grad_hgrn_lb_logits': 'grad_w', 'grad_hgrn_gnorm': 'grad_w', 'grad_hgrn_w_out': 'grad_w', 'grad_norm_mix': 'grad_w', 'grad_norm_mlp': 'grad_w', 'grad_mlp_w_up': 'grad_w', 'grad_mlp_w_down': 'grad_w', 'grad_norm_final': 'grad_w', 'delta_gdn_w_in': 'delta_w', 'delta_gdn_conv': 'delta_w', 'delta_gdn_a_log': 'delta_w', 'delta_gdn_dt_bias': 'delta_w', 'delta_gdn_onorm': 'delta_w', 'delta_gdn_w_out': 'delta_w', 'delta_hgrn_w_in': 'delta_w', 'delta_hgrn_lb_logits': 'delta_w', 'delta_hgrn_gnorm': 'delta_w', 'delta_hgrn_w_out': 'delta_w', 'delta_norm_mix': 'delta_w', 'delta_norm_mlp': 'delta_w', 'delta_mlp_w_up': 'delta_w', 'delta_mlp_w_down': 'delta_w', 'delta_norm_final': 'delta_w', 'new_m_gdn_w_in': 'new_m', 'new_m_gdn_conv': 'new_m', 'new_m_gdn_a_log': 'new_m', 'new_m_gdn_dt_bias': 'new_m', 'new_m_gdn_onorm': 'new_m', 'new_m_gdn_w_out': 'new_m', 'new_m_hgrn_w_in': 'new_m', 'new_m_hgrn_lb_logits': 'new_m', 'new_m_hgrn_gnorm': 'new_m', 'new_m_hgrn_w_out': 'new_m', 'new_m_norm_mix': 'new_m', 'new_m_norm_mlp': 'new_m', 'new_m_mlp_w_up': 'new_m', 'new_m_mlp_w_down': 'new_m', 'new_m_norm_final': 'new_m', 'new_v_gdn_w_in': 'new_v', 'new_v_gdn_conv': 'new_v', 'new_v_gdn_a_log': 'new_v', 'new_v_gdn_dt_bias': 'new_v', 'new_v_gdn_onorm': 'new_v', 'new_v_gdn_w_out': 'new_v', 'new_v_hgrn_w_in': 'new_v', 'new_v_hgrn_lb_logits': 'new_v', 'new_v_hgrn_gnorm': 'new_v', 'new_v_hgrn_w_out': 'new_v', 'new_v_norm_mix': 'new_v', 'new_v_norm_mlp': 'new_v', 'new_v_mlp_w_up': 'new_v', 'new_v_mlp_w_down': 'new_v', 'new_v_norm_final': 'new_v'}


def _forward(args):
    return _fwd_reference(*[args[k] for k in FWD_PARAMS])


def _output_shape():
    out = _jax.eval_shape(lambda: _forward(_fwd_setup_inputs(0)))
    return out.shape, out.dtype

N_MICROBATCH = 1
ADAM_LR = 0.001
ADAM_B1 = 0.9
ADAM_B2 = 0.999
ADAM_EPS = 1e-08
ADAM_WD = 0.01
ADAM_STEP = 10
PER_EXAMPLE_BATCH_AXIS = {'x': 0, 'loss_target': 0}
SHARED_INPUTS = []
_WEIGHT_DTYPES = {'gdn_w_in': _jnp.float32, 'gdn_conv': _jnp.float32, 'gdn_a_log': _jnp.float32, 'gdn_dt_bias': _jnp.float32, 'gdn_onorm': _jnp.float32, 'gdn_w_out': _jnp.float32, 'hgrn_w_in': _jnp.float32, 'hgrn_lb_logits': _jnp.float32, 'hgrn_gnorm': _jnp.float32, 'hgrn_w_out': _jnp.float32, 'norm_mix': _jnp.float32, 'norm_mlp': _jnp.float32, 'mlp_w_up': _jnp.float32, 'mlp_w_down': _jnp.float32, 'norm_final': _jnp.float32}
MOMENT_SCALE = {'gdn_w_in': 8.199741e-02, 'gdn_conv': 7.569127e-02, 'gdn_a_log': 8.922085e-01, 'gdn_dt_bias': 8.337814e-01, 'gdn_onorm': 2.856936e-01, 'gdn_w_out': 9.717848e-02, 'hgrn_w_in': 4.449546e-02, 'hgrn_lb_logits': 3.013119e-03, 'hgrn_gnorm': 6.041339e-02, 'hgrn_w_out': 6.253359e-02, 'norm_mix': 1.319635e-01, 'norm_mlp': 1.455090e-01, 'mlp_w_up': 6.863879e-02, 'mlp_w_down': 1.328966e-01, 'norm_final': 3.268797e+01}


def _to_microbatches(a, axis):
    t = _jnp.moveaxis(a, axis, 0)
    t = t.reshape((N_MICROBATCH, t.shape[0] // N_MICROBATCH) + t.shape[1:])
    return _jnp.moveaxis(t, 1, axis + 1)


def setup_inputs(seed: int = 0) -> dict:
    inp = _fwd_setup_inputs(seed)
    key = _jax.random.fold_in(_jax.random.key(seed), 7919)
    shape, _ = _output_shape()
    out = dict(inp)
    out["loss_target"] = _jax.random.normal(_jax.random.fold_in(key, 0), shape, _jnp.float32)
    for i, name in enumerate(TWIN_WEIGHTS):
        w = inp[name].astype(_jnp.float32)
        if MOMENT_SCALE is None:
            s = _jnp.sqrt(_jnp.mean(_jnp.square(w)) + 1e-30)
        else:
            s = MOMENT_SCALE[name]
        km, kv = _jax.random.split(_jax.random.fold_in(key, i + 1))
        out[name] = w
        out["m_" + name] = s * _jax.random.normal(km, w.shape, _jnp.float32)
        out["v_" + name] = (s * s) * _jax.random.uniform(kv, w.shape, _jnp.float32, 0.5, 1.5)
    if N_MICROBATCH > 1:
        for name, axis in PER_EXAMPLE_BATCH_AXIS.items():
            out[name] = _to_microbatches(out[name], axis)
    return {'x': out['x'], 'gdn_w_in': out['gdn_w_in'], 'gdn_conv': out['gdn_conv'], 'gdn_a_log': out['gdn_a_log'], 'gdn_dt_bias': out['gdn_dt_bias'], 'gdn_onorm': out['gdn_onorm'], 'gdn_w_out': out['gdn_w_out'], 'hgrn_w_in': out['hgrn_w_in'], 'hgrn_lb_logits': out['hgrn_lb_logits'], 'hgrn_gnorm': out['hgrn_gnorm'], 'hgrn_w_out': out['hgrn_w_out'], 'norm_mix': out['norm_mix'], 'norm_mlp': out['norm_mlp'], 'mlp_w_up': out['mlp_w_up'], 'mlp_w_down': out['mlp_w_down'], 'norm_final': out['norm_final'], 'loss_target': out['loss_target'], 'm_gdn_w_in': out['m_gdn_w_in'], 'm_gdn_conv': out['m_gdn_conv'], 'm_gdn_a_log': out['m_gdn_a_log'], 'm_gdn_dt_bias': out['m_gdn_dt_bias'], 'm_gdn_onorm': out['m_gdn_onorm'], 'm_gdn_w_out': out['m_gdn_w_out'], 'm_hgrn_w_in': out['m_hgrn_w_in'], 'm_hgrn_lb_logits': out['m_hgrn_lb_logits'], 'm_hgrn_gnorm': out['m_hgrn_gnorm'], 'm_hgrn_w_out': out['m_hgrn_w_out'], 'm_norm_mix': out['m_norm_mix'], 'm_norm_mlp': out['m_norm_mlp'], 'm_mlp_w_up': out['m_mlp_w_up'], 'm_mlp_w_down': out['m_mlp_w_down'], 'm_norm_final': out['m_norm_final'], 'v_gdn_w_in': out['v_gdn_w_in'], 'v_gdn_conv': out['v_gdn_conv'], 'v_gdn_a_log': out['v_gdn_a_log'], 'v_gdn_dt_bias': out['v_gdn_dt_bias'], 'v_gdn_onorm': out['v_gdn_onorm'], 'v_gdn_w_out': out['v_gdn_w_out'], 'v_hgrn_w_in': out['v_hgrn_w_in'], 'v_hgrn_lb_logits': out['v_hgrn_lb_logits'], 'v_hgrn_gnorm': out['v_hgrn_gnorm'], 'v_hgrn_w_out': out['v_hgrn_w_out'], 'v_norm_mix': out['v_norm_mix'], 'v_norm_mlp': out['v_norm_mlp'], 'v_mlp_w_up': out['v_mlp_w_up'], 'v_mlp_w_down': out['v_mlp_w_down'], 'v_norm_final': out['v_norm_final']}


def _loss(weights, diff, rest, loss_target):
    with _jax.named_scope("forward"):
        args = {**rest, TWIN_DIFF_INPUT: diff, **{k: w.astype(_WEIGHT_DTYPES[k]) for k, w in weights.items()}}
        y = _forward(args)
    with _jax.named_scope("loss_head"):
        err = _jnp.square(y.astype(_jnp.float32) - loss_target)
        return 0.5 * _jnp.sum(_jnp.mean(err, axis=-1)) if err.ndim else 0.5 * err


def _adamw(w, g, m, v):
    m = ADAM_B1 * m + (1.0 - ADAM_B1) * g
    v = ADAM_B2 * v + (1.0 - ADAM_B2) * _jnp.square(g)
    m_hat = m / (1.0 - ADAM_B1 ** ADAM_STEP)
    v_hat = v / (1.0 - ADAM_B2 ** ADAM_STEP)
    delta = -ADAM_LR * (m_hat / (_jnp.sqrt(v_hat) + ADAM_EPS) + ADAM_WD * w)
    return delta, m, v


def reference(x, gdn_w_in, gdn_conv, gdn_a_log, gdn_dt_bias, gdn_onorm, gdn_w_out, hgrn_w_in, hgrn_lb_logits, hgrn_gnorm, hgrn_w_out, norm_mix, norm_mlp, mlp_w_up, mlp_w_down, norm_final, loss_target, m_gdn_w_in, m_gdn_conv, m_gdn_a_log, m_gdn_dt_bias, m_gdn_onorm, m_gdn_w_out, m_hgrn_w_in, m_hgrn_lb_logits, m_hgrn_gnorm, m_hgrn_w_out, m_norm_mix, m_norm_mlp, m_mlp_w_up, m_mlp_w_down, m_norm_final, v_gdn_w_in, v_gdn_conv, v_gdn_a_log, v_gdn_dt_bias, v_gdn_onorm, v_gdn_w_out, v_hgrn_w_in, v_hgrn_lb_logits, v_hgrn_gnorm, v_hgrn_w_out, v_norm_mix, v_norm_mlp, v_mlp_w_up, v_mlp_w_down, v_norm_final):
    given = dict(x=x, gdn_w_in=gdn_w_in, gdn_conv=gdn_conv, gdn_a_log=gdn_a_log, gdn_dt_bias=gdn_dt_bias, gdn_onorm=gdn_onorm, gdn_w_out=gdn_w_out, hgrn_w_in=hgrn_w_in, hgrn_lb_logits=hgrn_lb_logits, hgrn_gnorm=hgrn_gnorm, hgrn_w_out=hgrn_w_out, norm_mix=norm_mix, norm_mlp=norm_mlp, mlp_w_up=mlp_w_up, mlp_w_down=mlp_w_down, norm_final=norm_final, loss_target=loss_target, m_gdn_w_in=m_gdn_w_in, m_gdn_conv=m_gdn_conv, m_gdn_a_log=m_gdn_a_log, m_gdn_dt_bias=m_gdn_dt_bias, m_gdn_onorm=m_gdn_onorm, m_gdn_w_out=m_gdn_w_out, m_hgrn_w_in=m_hgrn_w_in, m_hgrn_lb_logits=m_hgrn_lb_logits, m_hgrn_gnorm=m_hgrn_gnorm, m_hgrn_w_out=m_hgrn_w_out, m_norm_mix=m_norm_mix, m_norm_mlp=m_norm_mlp, m_mlp_w_up=m_mlp_w_up, m_mlp_w_down=m_mlp_w_down, m_norm_final=m_norm_final, v_gdn_w_in=v_gdn_w_in, v_gdn_conv=v_gdn_conv, v_gdn_a_log=v_gdn_a_log, v_gdn_dt_bias=v_gdn_dt_bias, v_gdn_onorm=v_gdn_onorm, v_gdn_w_out=v_gdn_w_out, v_hgrn_w_in=v_hgrn_w_in, v_hgrn_lb_logits=v_hgrn_lb_logits, v_hgrn_gnorm=v_hgrn_gnorm, v_hgrn_w_out=v_hgrn_w_out, v_norm_mix=v_norm_mix, v_norm_mlp=v_norm_mlp, v_mlp_w_up=v_mlp_w_up, v_mlp_w_down=v_mlp_w_down, v_norm_final=v_norm_final)
    weights = {n: given[n] for n in TWIN_WEIGHTS}
    shared = {n: given[n] for n in SHARED_INPUTS}
    per_example = {n: given[n] for n in ['x']}
    grad_fn = _jax.value_and_grad(_loss, argnums=(0, 1))

    def one_microbatch(ex, loss_target):
        ex = dict(ex)
        diff = ex.pop(TWIN_DIFF_INPUT)
        return grad_fn(weights, diff, {**shared, **ex}, loss_target)

    if N_MICROBATCH == 1:
        loss, (grad_w, grad_x) = one_microbatch(per_example, given["loss_target"])
    else:
        def body(carry, xs):
            loss_sum, grad_sum = carry
            l_k, (gw_k, gx_k) = one_microbatch(xs[0], xs[1])
            with _jax.named_scope("update"):
                return (loss_sum + l_k, _jax.tree.map(_jnp.add, grad_sum, gw_k)), gx_k

        init = (_jnp.zeros((), _jnp.float32), _jax.tree.map(_jnp.zeros_like, weights))
        (loss, grad_w), grad_x = _jax.lax.scan(body, init, (per_example, given["loss_target"]))
    with _jax.named_scope("update"):
        delta_w, new_m, new_v = {}, {}, {}
        for n in TWIN_WEIGHTS:
            delta_w[n], new_m[n], new_v[n] = _adamw(weights[n], grad_w[n], given["m_" + n], given["v_" + n])
    return (loss, grad_x, *[grad_w[n] for n in TWIN_WEIGHTS], *[delta_w[n] for n in TWIN_WEIGHTS],
            *[new_m[n] for n in TWIN_WEIGHTS], *[new_v[n] for n in TWIN_WEIGHTS])
```

```python
import functools

import jax
import jax.numpy as jnp
from jax import lax
from jax.experimental import pallas as pl
from jax.experimental.pallas import tpu as pltpu

F32 = jnp.float32
BF16 = jnp.bfloat16

D_MODEL = 1024
N_HEADS = 8
HEAD_DIM = 128
CHUNK = 64
SUB = 16
N_SUB = CHUNK // SUB
CONV_K = 4
HALO = 8
EPS = 1e-6
DEPTH = 4
N_DEV = 8
GDN_MAIN = 4 * D_MODEL
GDN_IN = GDN_MAIN + 2 * N_HEADS
AB_PAD = 128

ADAM_LR = 0.001
ADAM_B1 = 0.9
ADAM_B2 = 0.999
ADAM_EPS = 1e-08
ADAM_WD = 0.01
ADAM_STEP = 10

VMEM_LIMIT = 56 * 1024 * 1024

_DIMS = {
    "nn": (((1,), (0,)), ((), ())),
    "nt": (((1,), (1,)), ((), ())),
    "tn": (((0,), (0,)), ((), ())),
}


def _parts(x, n):
    out = []
    r = x.astype(F32)
    for i in range(n):
        p = r.astype(BF16)
        out.append(p)
        if i + 1 < n:
            r = r - p.astype(F32)
    return out


def _dot_raw(a, b, mode, na, nb):
    ap, bp = _parts(a, na), _parts(b, nb)
    nmax = max(na, nb)
    acc = None
    for i, xa in enumerate(ap):
        for j, xb in enumerate(bp):
            if i + j < nmax:
                t = lax.dot_general(xa, xb, _DIMS[mode], preferred_element_type=F32)
                acc = t if acc is None else acc + t
    return acc


@functools.partial(jax.custom_vjp, nondiff_argnums=(2, 3, 4))
def _dot(a, b, mode, na, nb):
    return _dot_raw(a, b, mode, na, nb)


def _dot_fwd(a, b, mode, na, nb):
    return _dot_raw(a, b, mode, na, nb), (a, b)


def _dot_bwd(mode, na, nb, res, ct):
    a, b = res
    nc = max(na, nb)
    if mode == "nn":
        da = _dot_raw(ct, b, "nt", nc, nb)
        db = _dot_raw(a, ct, "tn", na, nc)
    elif mode == "nt":
        da = _dot_raw(ct, b, "nn", nc, nb)
        db = _dot_raw(ct, a, "tn", nc, na)
    else:
        da = _dot_raw(b, ct, "nt", nb, nc)
        db = _dot_raw(a, ct, "nn", na, nc)
    return da, db


_dot.defvjp(_dot_fwd, _dot_bwd)


def _iota2(shape, dim):
    return lax.broadcasted_iota(jnp.int32, shape, dim)


def _tril_f32(n):
    return (_iota2((n, n), 0) >= _iota2((n, n), 1)).astype(F32)


def _cumsum_rows(g):
    return _dot(_tril_f32(g.shape[0]), g, "nn", 1, 3)


def _inv_unit_lower(L):
    n = L.shape[0]
    eye = (_iota2((n, n), 0) == _iota2((n, n), 1)).astype(F32)
    p = -L
    t = eye + p
    k = 2
    while k < n:
        p = _dot_raw(p, p, "nn", 2, 2)
        t = t + _dot_raw(t, p, "nn", 2, 2)
        k *= 2
    return t


@jax.custom_vjp
def _solve_unit_lower(L, rhs):
    return _dot_raw(_inv_unit_lower(L), rhs, "nn", 2, 2)


def _solve_fwd(L, rhs):
    t = _inv_unit_lower(L)
    sol = _dot_raw(t, rhs, "nn", 2, 2)
    return sol, (t, sol)


def _solve_bwd(res, ct):
    t, sol = res
    y = _dot_raw(t, ct, "tn", 2, 2)
    return -_dot_raw(y, sol, "nt", 2, 2), y


_solve_unit_lower.defvjp(_solve_fwd, _solve_bwd)


def _softplus(x):
    return jnp.maximum(x, 0.0) + jnp.log1p(jnp.exp(-jnp.abs(x)))


def _rms(x, w):
    return x * lax.rsqrt(jnp.mean(x * x, axis=-1, keepdims=True) + EPS) * w


def _hg_pre(p, lb):
    qraw = p[:, 0:D_MODEL]
    f = p[:, D_MODEL:2 * D_MODEL]
    v = p[:, 2 * D_MODEL:3 * D_MODEL]
    g = jnp.log(lb + (1.0 - lb) * jax.nn.sigmoid(f))
    k = (1.0 - lb) * jax.nn.sigmoid(-f)
    q = jax.nn.silu(qraw) * (HEAD_DIM ** -0.5)
    return q, k, v, g, _cumsum_rows(g)


def _hg_head(st, q, k, v, g, gc):
    i3 = lax.broadcasted_iota(jnp.int32, (SUB, SUB, HEAD_DIM), 0)
    j3 = lax.broadcasted_iota(jnp.int32, (SUB, SUB, HEAD_DIM), 1)
    rows = []
    for s in range(N_SUB):
        lo = s * SUB
        qs, ks, vs, gs = q[lo:lo + SUB], k[lo:lo + SUB], v[lo:lo + SUB], gc[lo:lo + SUB]
        dec = jnp.exp(jnp.where(i3 >= j3, gs[:, None, :] - gs[None, :, :], -jnp.inf))
        a_diag = jnp.sum(qs[:, None, :] * ks[None, :, :] * dec, axis=-1)
        o_s = _dot(a_diag, vs, "nn", 1, 1)
        if s > 0:
            gb = gc[lo:lo + 1] - g[lo:lo + 1]
            q_off = qs * jnp.exp(gs - gb)
            k_off = k[0:lo] * jnp.exp(gb - gc[0:lo])
            a_off = _dot(q_off, k_off, "nt", 1, 1)
            o_s = o_s + _dot(a_off, v[0:lo], "nn", 1, 1)
        rows.append(o_s)
    o = jnp.concatenate(rows, axis=0) + _dot(q * jnp.exp(gc), st, "nt", 1, 1)
    g_last = gc[CHUNK - 1:CHUNK]
    st_new = st * jnp.exp(g_last) + _dot(v, k * jnp.exp(g_last - gc), "tn", 1, 1)
    return o, st_new


def _hg_post(o, gate, gw):
    return _rms(o, gw) * jax.nn.silu(gate)


def _gd_pre(xp, a, b, cw, alog, dtb):
    off = HALO - (CONV_K - 1)
    y = cw[0:1] * xp[off:off + CHUNK]
    for kk in range(1, CONV_K):
        y = y + cw[kk:kk + 1] * xp[off + kk:off + kk + CHUNK]
    c = jax.nn.silu(y)
    beta = jax.nn.sigmoid(b)
    g = -jnp.exp(alog) * _softplus(a + dtb)
    expand = (_iota2((N_HEADS, D_MODEL), 1) // HEAD_DIM == _iota2((N_HEADS, D_MODEL), 0)).astype(F32)
    return c, _dot(beta, expand, "nn", 3, 1), _dot(g, expand, "nn", 3, 1)


def _gd_head(st, q, k, v, beta, g, gate, onw):
    q = q * lax.rsqrt(jnp.sum(q * q, axis=-1, keepdims=True) + EPS) * (HEAD_DIM ** -0.5)
    k = k * lax.rsqrt(jnp.sum(k * k, axis=-1, keepdims=True) + EPS)
    gc = _cumsum_rows(g)
    ri = _iota2((CHUNK, CHUNK), 0)
    ci = _iota2((CHUNK, CHUNK), 1)
    diff = gc[:, 0:CHUNK] - gc.T[0:CHUNK, :]
    decay = jnp.exp(jnp.where(ri >= ci, diff, -jnp.inf))
    kb = k * beta
    egc = jnp.exp(gc)
    L = jnp.where(ri > ci, _dot(kb, k, "nt", 1, 1) * decay, 0.0)
    sol = _solve_unit_lower(L, jnp.concatenate([v * beta, kb * egc], axis=1))
    u = sol[:, 0:HEAD_DIM]
    w = sol[:, HEAD_DIM:2 * HEAD_DIM]
    a_qk = jnp.where(ri >= ci, _dot(q, k, "nt", 1, 1) * decay, 0.0)
    g_last = gc[CHUNK - 1:CHUNK]
    v_new = u - _dot(w, st, "nt", 1, 1)
    o = _dot(q * egc, st, "nt", 1, 1) + _dot(a_qk, v_new, "nn", 1, 1)
    st_new = st * jnp.exp(g_last) + _dot(v_new, k * jnp.exp(g_last - gc), "tn", 1, 1)
    return _rms(o, onw) * jax.nn.silu(gate), st_new


def _params(*sem):
    return pltpu.CompilerParams(dimension_semantics=sem, vmem_limit_bytes=VMEM_LIMIT)


def _tile(n, pref):
    t = min(n, pref)
    assert n % t == 0, (n, pref)
    return t


def _mm(a, b, mode, out_dtypes, name, epilogue=None, extras=(), tm=512, tn=512, tk=1024):
    if mode == "nn":
        (m, k), (k2, n) = a.shape, b.shape
    elif mode == "nt":
        (m, k), (n, k2) = a.shape, b.shape
    else:
        (k, m), (k2, n) = a.shape, b.shape
    assert k == k2, (a.shape, b.shape, mode)
    tm, tn, tk = _tile(m, tm), _tile(n, tn), _tile(k, tk)
    nk = k // tk
    ne, no = len(extras), len(out_dtypes)
    if epilogue is None:
        epilogue = lambda acc: (acc,)

    def body(*refs):
        a_ref, b_ref = refs[0], refs[1]
        ex = refs[2:2 + ne]
        outs = refs[2 + ne:2 + ne + no]
        part = lax.dot_general(a_ref[...].astype(BF16), b_ref[...].astype(BF16), _DIMS[mode],
                               preferred_element_type=F32)

        def finish(acc):
            for o_ref, val in zip(outs, epilogue(acc, *[e[...] for e in ex])):
                o_ref[...] = val.astype(o_ref.dtype)

        if nk == 1:
            finish(part)
        else:
            acc_ref = refs[-1]
            kk = pl.program_id(2)

            @pl.when(kk == 0)
            def _():
                acc_ref[...] = part

            @pl.when(kk > 0)
            def _():
                acc_ref[...] += part

            @pl.when(kk == nk - 1)
            def _():
                finish(acc_ref[...])

    if mode == "tn":
        a_spec = pl.BlockSpec((tk, tm), lambda i, j, kk: (kk, i))
    else:
        a_spec = pl.BlockSpec((tm, tk), lambda i, j, kk: (i, kk))
    if mode == "nt":
        b_spec = pl.BlockSpec((tn, tk), lambda i, j, kk: (j, kk))
    else:
        b_spec = pl.BlockSpec((tk, tn), lambda i, j, kk: (kk, j))
    o_spec = pl.BlockSpec((tm, tn), lambda i, j, kk: (i, j))
    res = pl.pallas_call(
        body,
        name=name,
        grid=(m // tm, n // tn, nk),
        in_specs=[a_spec, b_spec] + [o_spec] * ne,
        out_specs=[o_spec] * no,
        out_shape=[jax.ShapeDtypeStruct((m, n), dt) for dt in out_dtypes],
        scratch_shapes=[pltpu.VMEM((tm, tn), F32)] if nk > 1 else [],
        compiler_params=_params("parallel", "parallel", "arbitrary"),
    )(a, b, *extras)
    return res[0] if no == 1 else res


def _rms_fwd(x, w, name, tm=512):
    n, d = x.shape
    tm = _tile(n, tm)

    def body(x_ref, w_ref, y_ref):
        y_ref[...] = _rms(x_ref[...], w_ref[...]).astype(y_ref.dtype)

    return pl.pallas_call(
        body, name=name, grid=(n // tm,),
        in_specs=[pl.BlockSpec((tm, d), lambda i: (i, 0)), pl.BlockSpec((1, d), lambda i: (0, 0))],
        out_specs=pl.BlockSpec((tm, d), lambda i: (i, 0)),
        out_shape=jax.ShapeDtypeStruct((n, d), BF16),
        compiler_params=_params("arbitrary"),
    )(x, w)


def _rms_bwd(x, w, dy, dres, name, tm=512):
    n, d = x.shape
    tm = _tile(n, tm)

    def body(x_ref, w_ref, dy_ref, dres_ref, dx_ref, dw_ref):
        _, vjp = jax.vjp(_rms, x_ref[...], w_ref[...])
        dx, dw = vjp(dy_ref[...].astype(F32))
        dx_ref[...] = dres_ref[...] + dx

        @pl.when(pl.program_id(0) == 0)
        def _():
            dw_ref[...] = dw

        @pl.when(pl.program_id(0) > 0)
        def _():
            dw_ref[...] += dw

    row = pl.BlockSpec((tm, d), lambda i: (i, 0))
    vec = pl.BlockSpec((1, d), lambda i: (0, 0))
    return pl.pallas_call(
        body, name=name, grid=(n // tm,),
        in_specs=[row, vec, row, row],
        out_specs=[row, vec],
        out_shape=[jax.ShapeDtypeStruct((n, d), F32), jax.ShapeDtypeStruct((1, d), F32)],
        compiler_params=_params("arbitrary"),
    )(x, w, dy, dres)


def _loss_head(h, w, target, name, tm=512):
    n, d = h.shape
    tm = _tile(n, tm)

    def body(h_ref, w_ref, t_ref, dh_ref, dw_ref, sq_ref):
        y, vjp = jax.vjp(_rms, h_ref[...], w_ref[...])
        err = y - t_ref[...]
        dh, dw = vjp(err * (1.0 / d))
        dh_ref[...] = dh
        sq = jnp.sum(err * err, axis=0, keepdims=True)

        @pl.when(pl.program_id(0) == 0)
        def _():
            dw_ref[...] = dw
            sq_ref[...] = sq

        @pl.when(pl.program_id(0) > 0)
        def _():
            dw_ref[...] += dw
            sq_ref[...] += sq

        @pl.when(pl.program_id(0) == n // tm - 1)
        def _():
            total = jnp.sum(sq_ref[...], axis=1, keepdims=True) * (0.5 / d)
            sq_ref[...] = jnp.broadcast_to(total, sq_ref.shape)

    row = pl.BlockSpec((tm, d), lambda i: (i, 0))
    vec = pl.BlockSpec((1, d), lambda i: (0, 0))
    return pl.pallas_call(
        body, name=name, grid=(n // tm,),
        in_specs=[row, vec, row],
        out_specs=[row, vec, vec],
        out_shape=[jax.ShapeDtypeStruct((n, d), F32), jax.ShapeDtypeStruct((1, d), F32),
                   jax.ShapeDtypeStruct((1, d), F32)],
        compiler_params=_params("arbitrary"),
    )(h, w, target)


def _lower_bounds(logits):
    sm = jax.nn.softmax(logits, axis=0)
    rows = [sm[0:1] * 0.0]
    for r in range(1, DEPTH):
        rows.append(rows[-1] + sm[r:r + 1])
    return jnp.concatenate(rows, axis=0)


def _lb_fwd(logits, name):
    def body(l_ref, o_ref):
        o_ref[...] = _lower_bounds(l_ref[...])

    return pl.pallas_call(body, name=name, out_shape=jax.ShapeDtypeStruct(logits.shape, F32))(logits)


def _lb_bwd(logits, dlb, name):
    def body(l_ref, d_ref, o_ref):
        _, vjp = jax.vjp(_lower_bounds, l_ref[...])
        (o_ref[...],) = vjp(d_ref[...])

    return pl.pallas_call(body, name=name, out_shape=jax.ShapeDtypeStruct(logits.shape, F32))(logits, dlb)


def _head_slice(h):
    return pl.ds(pl.multiple_of(h * HEAD_DIM, HEAD_DIM), HEAD_DIM)


def _hgrn_fwd(proj, lb, gw, seqs, name):
    n = proj.shape[0]
    nc = n // seqs // CHUNK
    d = D_MODEL

    def body(p_ref, lb_ref, gw_ref, o2_ref, o_ref, st_all_ref, st_sc, q_sc, k_sc, v_sc, g_sc, gc_sc):
        @pl.when(pl.program_id(1) == 0)
        def _():
            st_sc[...] = jnp.zeros_like(st_sc)

        q_sc[...], k_sc[...], v_sc[...], g_sc[...], gc_sc[...] = _hg_pre(p_ref[:, 0:3 * d], lb_ref[...])
        st_all_ref[0] = st_sc[...]

        def head(h, carry):
            sl = _head_slice(h)
            o_h, st_new = _hg_head(st_sc[h], q_sc[:, sl], k_sc[:, sl], v_sc[:, sl], g_sc[:, sl], gc_sc[:, sl])
            o_ref[:, sl] = o_h
            st_sc[h] = st_new
            return carry

        lax.fori_loop(0, N_HEADS, head, 0)
        o2_ref[...] = _hg_post(o_ref[...], p_ref[:, 3 * d:4 * d], gw_ref[...]).astype(o2_ref.dtype)

    idx = lambda b, c: (b * nc + c, 0)
    vec = pl.BlockSpec((1, d), lambda b, c: (0, 0))
    act = pl.BlockSpec((CHUNK, d), idx)
    return pl.pallas_call(
        body, name=name, grid=(seqs, nc),
        in_specs=[pl.BlockSpec((CHUNK, 4 * d), idx), vec, vec],
        out_specs=[act, act, pl.BlockSpec((1, N_HEADS, HEAD_DIM, HEAD_DIM), lambda b, c: (b * nc + c, 0, 0, 0))],
        out_shape=[jax.ShapeDtypeStruct((n, d), BF16), jax.ShapeDtypeStruct((n, d), F32),
                   jax.ShapeDtypeStruct((n // CHUNK, N_HEADS, HEAD_DIM, HEAD_DIM), F32)],
        scratch_shapes=[pltpu.VMEM((N_HEADS, HEAD_DIM, HEAD_DIM), F32)] + [pltpu.VMEM((CHUNK, d), F32)] * 5,
        compiler_params=_params("arbitrary", "arbitrary"),
    )(proj, lb, gw)


def _hgrn_bwd(proj, lb, gw, st_all, o, do2, seqs, name):
    n = proj.shape[0]
    nc = n // seqs // CHUNK
    d = D_MODEL

    def body(p_ref, lb_ref, gw_ref, st_all_ref, o_ref, do2_ref, dp_ref, dlb_ref, dgw_ref,
             dst_sc, q_sc, k_sc, v_sc, g_sc, gc_sc, do_sc, dq_sc, dk_sc, dv_sc, dg_sc, dgc_sc):
        first = (pl.program_id(0) == 0) & (pl.program_id(1) == 0)

        @pl.when(pl.program_id(1) == 0)
        def _():
            dst_sc[...] = jnp.zeros_like(dst_sc)

        pre_out, pre_vjp = jax.vjp(_hg_pre, p_ref[:, 0:3 * d], lb_ref[...])
        q_sc[...], k_sc[...], v_sc[...], g_sc[...], gc_sc[...] = pre_out
        _, post_vjp = jax.vjp(_hg_post, o_ref[...], p_ref[:, 3 * d:4 * d], gw_ref[...])
        do_sc[...], dgate, dgw = post_vjp(do2_ref[...].astype(F32))
        dp_ref[:, 3 * d:4 * d] = dgate.astype(dp_ref.dtype)

        def head(h, carry):
            sl = _head_slice(h)
            _, vjp = jax.vjp(_hg_head, st_all_ref[0, h], q_sc[:, sl], k_sc[:, sl], v_sc[:, sl],
                             g_sc[:, sl], gc_sc[:, sl])
            dst_sc[h], dq_sc[:, sl], dk_sc[:, sl], dv_sc[:, sl], dg_sc[:, sl], dgc_sc[:, sl] = vjp(
                (do_sc[:, sl], dst_sc[h]))
            return carry

        lax.fori_loop(0, N_HEADS, head, 0)
        dp, dlb = pre_vjp((dq_sc[...], dk_sc[...], dv_sc[...], dg_sc[...], dgc_sc[...]))
        dp_ref[:, 0:3 * d] = dp.astype(dp_ref.dtype)

        @pl.when(first)
        def _():
            dlb_ref[...] = dlb
            dgw_ref[...] = dgw

        @pl.when(jnp.logical_not(first))
        def _():
            dlb_ref[...] += dlb
            dgw_ref[...] += dgw

    idx = lambda b, c: (b * nc + nc - 1 - c, 0)
    vec = pl.BlockSpec((1, d), lambda b, c: (0, 0))
    act = pl.BlockSpec((CHUNK, d), idx)
    wide = pl.BlockSpec((CHUNK, 4 * d), idx)
    return pl.pallas_call(
        body, name=name, grid=(seqs, nc),
        in_specs=[wide, vec, vec,
                  pl.BlockSpec((1, N_HEADS, HEAD_DIM, HEAD_DIM), lambda b, c: (b * nc + nc - 1 - c, 0, 0, 0)),
                  act, act],
        out_specs=[wide, vec, vec],
        out_shape=[jax.ShapeDtypeStruct((n, 4 * d), BF16), jax.ShapeDtypeStruct((1, d), F32),
                   jax.ShapeDtypeStruct((1, d), F32)],
        scratch_shapes=[pltpu.VMEM((N_HEADS, HEAD_DIM, HEAD_DIM), F32)] + [pltpu.VMEM((CHUNK, d), F32)] * 11,
        compiler_params=_params("arbitrary", "arbitrary"),
    )(proj, lb, gw, st_all, o, do2)


def _gd_xp(halo_ref, p_ref, first_chunk):
    halo = jnp.where(first_chunk, 0.0, halo_ref[...])
    return jnp.concatenate([halo, p_ref[:, 0:3 * D_MODEL]], axis=0)


def _gdn_fwd(projm, projab, cw, alog, dtb, onw, seqs, name):
    n = projm.shape[0]
    nc = n // seqs // CHUNK
    d = D_MODEL
    per_halo = CHUNK // HALO

    def body(p_ref, halo_ref, ab_ref, cw_ref, alog_ref, dtb_ref, onw_ref, o2_ref, st_all_ref,
             st_sc, c_sc, beta_sc, g_sc):
        @pl.when(pl.program_id(1) == 0)
        def _():
            st_sc[...] = jnp.zeros_like(st_sc)

        xp = _gd_xp(halo_ref, p_ref, pl.program_id(1) == 0)
        c_sc[...], beta_sc[...], g_sc[...] = _gd_pre(
            xp, ab_ref[:, 0:N_HEADS], ab_ref[:, N_HEADS:2 * N_HEADS], cw_ref[...], alog_ref[...], dtb_ref[...])
        st_all_ref[0] = st_sc[...]

        def head(h, carry):
            sl = _head_slice(h)
            o2_h, st_new = _gd_head(
                st_sc[h], c_sc[:, sl], c_sc[:, _head_slice(h + N_HEADS)], c_sc[:, _head_slice(h + 2 * N_HEADS)],
                beta_sc[:, sl], g_sc[:, sl], p_ref[:, _head_slice(h + 3 * N_HEADS)], onw_ref[...])
            o2_ref[:, sl] = o2_h.astype(o2_ref.dtype)
            st_sc[h] = st_new
            return carry

        lax.fori_loop(0, N_HEADS, head, 0)

    idx = lambda b, c: (b * nc + c, 0)
    const = lambda b, c: (0, 0)
    return pl.pallas_call(
        body, name=name, grid=(seqs, nc),
        in_specs=[pl.BlockSpec((CHUNK, 4 * d), idx),
                  pl.BlockSpec((HALO, 3 * d), lambda b, c: (jnp.maximum((b * nc + c) * per_halo - 1, 0), 0)),
                  pl.BlockSpec((CHUNK, AB_PAD), idx),
                  pl.BlockSpec((CONV_K, 3 * d), const), pl.BlockSpec((1, N_HEADS), const),
                  pl.BlockSpec((1, N_HEADS), const), pl.BlockSpec((1, HEAD_DIM), const)],
        out_specs=[pl.BlockSpec((CHUNK, d), idx),
                   pl.BlockSpec((1, N_HEADS, HEAD_DIM, HEAD_DIM), lambda b, c: (b * nc + c, 0, 0, 0))],
        out_shape=[jax.ShapeDtypeStruct((n, d), BF16),
                   jax.ShapeDtypeStruct((n // CHUNK, N_HEADS, HEAD_DIM, HEAD_DIM), F32)],
        scratch_shapes=[pltpu.VMEM((N_HEADS, HEAD_DIM, HEAD_DIM), F32), pltpu.VMEM((CHUNK, 3 * d), F32),
                        pltpu.VMEM((CHUNK, d), F32), pltpu.VMEM((CHUNK, d), F32)],
        compiler_params=_params("arbitrary", "arbitrary"),
    )(projm, projm, projab, cw, alog, dtb, onw)


def _gdn_bwd(projm, projab, cw, alog, dtb, onw, st_all, do2, seqs, name):
    n = projm.shape[0]
    nc = n // seqs // CHUNK
    d = D_MODEL
    per_halo = CHUNK // HALO

    def body(p_ref, halo_ref, ab_ref, cw_ref, alog_ref, dtb_ref, onw_ref, st_all_ref, do2_ref,
             dp_ref, dab_ref, dcw_ref, dalog_ref, ddtb_ref, donw_ref,
             dst_sc, dhalo_sc, c_sc, beta_sc, g_sc, dc_sc, dbeta_sc, dg_sc, donw_sc):
        step = pl.program_id(1)
        first = (pl.program_id(0) == 0) & (step == 0)

        @pl.when(step == 0)
        def _():
            dst_sc[...] = jnp.zeros_like(dst_sc)
            dhalo_sc[...] = jnp.zeros_like(dhalo_sc)

        donw_sc[...] = jnp.zeros_like(donw_sc)
        xp = _gd_xp(halo_ref, p_ref, step == nc - 1)
        pre_out, pre_vjp = jax.vjp(_gd_pre, xp, ab_ref[:, 0:N_HEADS], ab_ref[:, N_HEADS:2 * N_HEADS],
                                   cw_ref[...], alog_ref[...], dtb_ref[...])
        c_sc[...], beta_sc[...], g_sc[...] = pre_out

        def head(h, carry):
            sl, slk, slv, slg = (_head_slice(h), _head_slice(h + N_HEADS), _head_slice(h + 2 * N_HEADS),
                                 _head_slice(h + 3 * N_HEADS))
            _, vjp = jax.vjp(_gd_head, st_all_ref[0, h], c_sc[:, sl], c_sc[:, slk], c_sc[:, slv],
                             beta_sc[:, sl], g_sc[:, sl], p_ref[:, slg], onw_ref[...])
            (dst_sc[h], dc_sc[:, sl], dc_sc[:, slk], dc_sc[:, slv], dbeta_sc[:, sl], dg_sc[:, sl],
             dgate, donw) = vjp((do2_ref[:, sl].astype(F32), dst_sc[h]))
            dp_ref[:, slg] = dgate.astype(dp_ref.dtype)
            donw_sc[...] += donw
            return carry

        lax.fori_loop(0, N_HEADS, head, 0)
        dxp, da, db, dcw, dalog, ddtb = pre_vjp((dc_sc[...], dbeta_sc[...], dg_sc[...]))
        dqkv = jnp.concatenate([dxp[HALO:CHUNK], dxp[CHUNK:HALO + CHUNK] + dhalo_sc[...]], axis=0)
        dp_ref[:, 0:3 * d] = dqkv.astype(dp_ref.dtype)
        dhalo_sc[...] = dxp[0:HALO]
        dab_ref[...] = jnp.concatenate(
            [da, db, jnp.zeros((CHUNK, AB_PAD - 2 * N_HEADS), F32)], axis=1).astype(dab_ref.dtype)

        @pl.when(first)
        def _():
            dcw_ref[...] = dcw
            dalog_ref[...] = dalog
            ddtb_ref[...] = ddtb
            donw_ref[...] = donw_sc[...]

        @pl.when(jnp.logical_not(first))
        def _():
            dcw_ref[...] += dcw
            dalog_ref[...] += dalog
            ddtb_ref[...] += ddtb
            donw_ref[...] += donw_sc[...]

    rev = lambda b, c: b * nc + nc - 1 - c
    idx = lambda b, c: (rev(b, c), 0)
    const = lambda b, c: (0, 0)
    small = [pl.BlockSpec((CONV_K, 3 * d), const), pl.BlockSpec((1, N_HEADS), const),
             pl.BlockSpec((1, N_HEADS), const), pl.BlockSpec((1, HEAD_DIM), const)]
    return pl.pallas_call(
        body, name=name, grid=(seqs, nc),
        in_specs=[pl.BlockSpec((CHUNK, 4 * d), idx),
                  pl.BlockSpec((HALO, 3 * d), lambda b, c: (jnp.maximum(rev(b, c) * per_halo - 1, 0), 0)),
                  pl.BlockSpec((CHUNK, AB_PAD), idx)] + small + [
                  pl.BlockSpec((1, N_HEADS, HEAD_DIM, HEAD_DIM), lambda b, c: (rev(b, c), 0, 0, 0)),
                  pl.BlockSpec((CHUNK, d), idx)],
        out_specs=[pl.BlockSpec((CHUNK, 4 * d), idx), pl.BlockSpec((CHUNK, AB_PAD), idx)] + small,
        out_shape=[jax.ShapeDtypeStruct((n, 4 * d), BF16), jax.ShapeDtypeStruct((n, AB_PAD), BF16),
                   jax.ShapeDtypeStruct((CONV_K, 3 * d), F32), jax.ShapeDtypeStruct((1, N_HEADS), F32),
                   jax.ShapeDtypeStruct((1, N_HEADS), F32), jax.ShapeDtypeStruct((1, HEAD_DIM), F32)],
        scratch_shapes=[pltpu.VMEM((N_HEADS, HEAD_DIM, HEAD_DIM), F32), pltpu.VMEM((HALO, 3 * d), F32),
                        pltpu.VMEM((CHUNK, 3 * d), F32), pltpu.VMEM((CHUNK, d), F32), pltpu.VMEM((CHUNK, d), F32),
                        pltpu.VMEM((CHUNK, 3 * d), F32), pltpu.VMEM((CHUNK, d), F32), pltpu.VMEM((CHUNK, d), F32),
                        pltpu.VMEM((1, HEAD_DIM), F32)],
        compiler_params=_params("arbitrary", "arbitrary"),
    )(projm, projm, projab, cw, alog, dtb, onw, st_all, do2)


def _adamw(w, g, m, v, name, slots=0, tr=256):
    r, c = w.shape
    tr = _tile(r, tr)
    b1c = 1.0 - ADAM_B1 ** ADAM_STEP
    b2c = 1.0 - ADAM_B2 ** ADAM_STEP

    def body(w_ref, g_ref, m_ref, v_ref, go_ref, d_ref, mo_ref, vo_ref):
        if slots:
            g = g_ref[0].astype(F32)
            for s in range(1, slots):
                g = g + g_ref[s].astype(F32)
        else:
            g = g_ref[...]
        m_new = ADAM_B1 * m_ref[...] + (1.0 - ADAM_B1) * g
        v_new = ADAM_B2 * v_ref[...] + (1.0 - ADAM_B2) * (g * g)
        go_ref[...] = g
        mo_ref[...] = m_new
        vo_ref[...] = v_new
        d_ref[...] = -ADAM_LR * ((m_new / b1c) / (jnp.sqrt(v_new / b2c) + ADAM_EPS) + ADAM_WD * w_ref[...])

    blk = pl.BlockSpec((tr, c), lambda i: (i, 0))
    g_spec = pl.BlockSpec((slots, tr, c), lambda i: (0, i, 0)) if slots else blk
    return pl.pallas_call(
        body, name=name, grid=(r // tr,),
        in_specs=[blk, g_spec, blk, blk],
        out_specs=[blk] * 4,
        out_shape=[jax.ShapeDtypeStruct((r, c), F32)] * 4,
        compiler_params=_params("arbitrary"),
    )(w, g, m, v)


def _mesh_pos():
    return lax.axis_index("x"), lax.axis_index("y"), lax.axis_index("c")


def _flip(pos, p):
    x, y, c = pos
    return ((1 - x) if p & 4 else x, (1 - y) if p & 2 else y, (1 - c) if p & 1 else c)


def _lin(pos):
    return 4 * pos[0] + 2 * pos[1] + pos[2]


def _exchange(srcs, dst_shapes, src_pick, dst_pick, name):
    na = len(srcs)

    def body(*refs):
        src = refs[0:na]
        dst = refs[na:2 * na]
        send_sems, recv_sems, loc_sems = refs[2 * na:]
        me = _mesh_pos()
        me_i = _lin(me)
        started = []
        for a in range(na):
            loc = pltpu.make_async_copy(src_pick[a](src[a], me_i), dst_pick[a](dst[a], me_i), loc_sems.at[a])
            loc.start()
            started.append(loc)
        for p in range(1, N_DEV):
            peer = _flip(me, p)
            for a in range(na):
                cp = pltpu.make_async_remote_copy(
                    src_ref=src_pick[a](src[a], _lin(peer)), dst_ref=dst_pick[a](dst[a], me_i),
                    send_sem=send_sems.at[a, p - 1], recv_sem=recv_sems.at[a, p - 1],
                    device_id=peer, device_id_type=pl.DeviceIdType.MESH)
                cp.start()
        for p in range(1, N_DEV):
            peer = _flip(me, p)
            for a in range(na):
                cp = pltpu.make_async_remote_copy(
                    src_ref=src_pick[a](src[a], _lin(peer)), dst_ref=dst_pick[a](dst[a], _lin(peer)),
                    send_sem=send_sems.at[a, p - 1], recv_sem=recv_sems.at[a, p - 1],
                    device_id=peer, device_id_type=pl.DeviceIdType.MESH)
                cp.wait_recv()
                cp.wait_send()
        for loc in started:
            loc.wait()

    anyspec = pl.BlockSpec(memory_space=pl.ANY)
    return pl.pallas_call(
        body, name=name,
        in_specs=[anyspec] * na, out_specs=[anyspec] * na,
        out_shape=[jax.ShapeDtypeStruct(s, src.dtype) for s, src in zip(dst_shapes, srcs)],
        scratch_shapes=[pltpu.SemaphoreType.DMA((na, N_DEV - 1)), pltpu.SemaphoreType.DMA((na, N_DEV - 1)),
                        pltpu.SemaphoreType.DMA((na,))],
        compiler_params=pltpu.CompilerParams(has_side_effects=True),
    )(*srcs)


def _whole(ref, i):
    return ref


def _slot(ref, i):
    return ref.at[i]


def _rows_of(r):
    return lambda ref, i: ref.at[:, pl.ds(pl.multiple_of(i * r, r), r), :]


def _cols_of(c):
    return lambda ref, i: ref.at[:, :, pl.ds(pl.multiple_of(i * c, c), c)]


def _all_reduce_small(buf, name):
    r, c = buf.shape

    def body(src_ref, out_ref, all_ref, send_sems, recv_sems):
        me = _mesh_pos()
        me_i = _lin(me)
        all_ref[me_i] = src_ref[...]
        for p in range(1, N_DEV):
            peer = _flip(me, p)
            pltpu.make_async_remote_copy(
                src_ref=src_ref, dst_ref=all_ref.at[me_i], send_sem=send_sems.at[p - 1], recv_sem=recv_sems.at[p - 1],
                device_id=peer, device_id_type=pl.DeviceIdType.MESH).start()
        for p in range(1, N_DEV):
            peer = _flip(me, p)
            cp = pltpu.make_async_remote_copy(
                src_ref=src_ref, dst_ref=all_ref.at[_lin(peer)], send_sem=send_sems.at[p - 1],
                recv_sem=recv_sems.at[p - 1], device_id=peer, device_id_type=pl.DeviceIdType.MESH)
            cp.wait_recv()
            cp.wait_send()
        acc = all_ref[0]
        for s in range(1, N_DEV):
            acc = acc + all_ref[s]
        out_ref[...] = acc

    vm = pl.BlockSpec(memory_space=pltpu.VMEM)
    return pl.pallas_call(
        body, name=name, in_specs=[vm], out_specs=vm,
        out_shape=jax.ShapeDtypeStruct((r, c), F32),
        scratch_shapes=[pltpu.VMEM((N_DEV, r, c), F32), pltpu.SemaphoreType.DMA((N_DEV - 1,)),
                        pltpu.SemaphoreType.DMA((N_DEV - 1,))],
        compiler_params=pltpu.CompilerParams(has_side_effects=True),
    )(buf)


def _unshard_cols(g):
    s, l, r, c = g.shape
    return jnp.transpose(g, (1, 2, 0, 3)).reshape(l, r, s * c)


def _shard_cols(w):
    l, r, c = w.shape
    return jnp.transpose(w.reshape(l, r, N_DEV, c // N_DEV), (2, 0, 1, 3))


def kernel(x, gdn_w_in, gdn_conv, gdn_a_log, gdn_dt_bias, gdn_onorm, gdn_w_out, hgrn_w_in, hgrn_lb_logits, hgrn_gnorm, hgrn_w_out, norm_mix, norm_mlp, mlp_w_up, mlp_w_down, norm_final, loss_target, m_gdn_w_in, m_gdn_conv, m_gdn_a_log, m_gdn_dt_bias, m_gdn_onorm, m_gdn_w_out, m_hgrn_w_in, m_hgrn_lb_logits, m_hgrn_gnorm, m_hgrn_w_out, m_norm_mix, m_norm_mlp, m_mlp_w_up, m_mlp_w_down, m_norm_final, v_gdn_w_in, v_gdn_conv, v_gdn_a_log, v_gdn_dt_bias, v_gdn_onorm, v_gdn_w_out, v_hgrn_w_in, v_hgrn_lb_logits, v_hgrn_gnorm, v_hgrn_w_out, v_norm_mix, v_norm_mlp, v_mlp_w_up, v_mlp_w_down, v_norm_final):
    seqs, seq_len, d = x.shape
    n = seqs * seq_len
    me_i = _lin(_mesh_pos())
    x2 = x.reshape(n, d)
    target = loss_target.reshape(n, d)
    n_gdn, n_hgrn = gdn_w_in.shape[0], hgrn_w_in.shape[0]

    srcs = [gdn_w_in.astype(BF16), gdn_w_out.astype(BF16), hgrn_w_in.astype(BF16), hgrn_w_out.astype(BF16),
            mlp_w_up.astype(BF16), mlp_w_down.astype(BF16), gdn_conv, hgrn_gnorm]
    r_out, r_down = gdn_w_out.shape[1], mlp_w_down.shape[1]
    c_hin, c_up = hgrn_w_in.shape[2], mlp_w_up.shape[2]
    g_win, w_gout, w_hin, w_hout, w_up, w_down, g_conv, g_gnorm = _exchange(
        srcs,
        [(N_DEV,) + gdn_w_in.shape, (n_gdn, N_DEV * r_out, d), (n_hgrn, d, N_DEV * c_hin),
         (n_hgrn, N_DEV * r_out, d), (DEPTH, d, N_DEV * c_up), (DEPTH, N_DEV * r_down, d),
         (N_DEV,) + gdn_conv.shape, (N_DEV,) + hgrn_gnorm.shape],
        [_whole] * 8,
        [_slot, _rows_of(r_out), _cols_of(c_hin), _rows_of(r_out), _cols_of(c_up), _rows_of(r_down), _slot, _slot],
        "gather_weights")
    w_gin = _unshard_cols(g_win)
    w_gm = w_gin[:, :, :GDN_MAIN]
    w_gab = jnp.pad(w_gin[:, :, GDN_MAIN:], ((0, 0), (0, 0), (0, AB_PAD - 2 * N_HEADS)))
    conv_full = _unshard_cols(g_conv)
    gnorm_full = jnp.transpose(g_gnorm, (1, 0, 2)).reshape(n_hgrn, d)
    lbs = _lb_fwd(hgrn_lb_logits, "lb_fwd")

    saved = []
    h = x2
    for i in range(DEPTH):
        j = i // 2
        y = _rms_fwd(h, norm_mix[i:i + 1], f"rms_mix_{i}")
        if i % 2 == 0:
            projm = _mm(y, w_gm[j], "nn", [F32], f"gdn_proj_{i}")
            projab = _mm(y, w_gab[j], "nn", [F32], f"gdn_proj_ab_{i}")
            o2, st_all = _gdn_fwd(projm, projab, conv_full[j], gdn_a_log[j:j + 1], gdn_dt_bias[j:j + 1],
                                  gdn_onorm[j:j + 1], seqs, f"gdn_fwd_{i}")
            mix = (projm, projab, st_all)
            w_o = w_gout[j]
        else:
            proj = _mm(y, w_hin[j], "nn", [F32], f"hgrn_proj_{i}")
            o2, o_raw, st_all = _hgrn_fwd(proj, lbs[i:i + 1], gnorm_full[j:j + 1], seqs, f"hgrn_fwd_{i}")
            mix = (proj, o_raw, st_all)
            w_o = w_hout[j]
        h1 = _mm(o2, w_o, "nn", [F32], f"mix_out_{i}", epilogue=lambda acc, res: (res + acc,), extras=(h,))
        y2 = _rms_fwd(h1, norm_mlp[i:i + 1], f"rms_mlp_{i}")
        u, act = _mm(y2, w_up[i], "nn", [BF16, BF16], f"mlp_up_{i}",
                     epilogue=lambda acc: (acc, jnp.square(jnp.maximum(acc, 0.0))))
        h2 = _mm(act, w_down[i], "nn", [F32], f"mlp_down_{i}", epilogue=lambda acc, res: (res + acc,), extras=(h1,))
        saved.append((h, y, mix, o2, h1, y2, u, act))
        h = h2

    dh, d_nf, sq = _loss_head(h, norm_final.reshape(1, d), target, "loss_head")

    g_up, g_down = [None] * DEPTH, [None] * DEPTH
    g_gin, g_gout, g_hin, g_hout = [None] * n_gdn, [None] * n_gdn, [None] * n_hgrn, [None] * n_hgrn
    d_nmix, d_nmlp = [None] * DEPTH, [None] * DEPTH
    d_conv, d_alog, d_dtb, d_onorm = [None] * n_gdn, [None] * n_gdn, [None] * n_gdn, [None] * n_gdn
    d_lb = [jnp.zeros((1, d), F32)] * DEPTH
    d_gnorm = [None] * n_hgrn
    for i in reversed(range(DEPTH)):
        j = i // 2
        h_in, y, mix, o2, h1, y2, u, act = saved[i]
        g_down[i] = _mm(act, dh, "tn", [BF16], f"g_down_{i}")
        du = _mm(dh, w_down[i], "nt", [BF16], f"d_u_{i}",
                 epilogue=lambda acc, uu: (acc * (2.0 * jnp.maximum(uu.astype(F32), 0.0)),), extras=(u,))
        g_up[i] = _mm(y2, du, "tn", [BF16], f"g_up_{i}")
        dy2 = _mm(du, w_up[i], "nt", [F32], f"d_y2_{i}")
        dh1, d_nmlp[i] = _rms_bwd(h1, norm_mlp[i:i + 1], dy2, dh, f"rms_mlp_bwd_{i}")
        if i % 2 == 0:
            projm, projab, st_all = mix
            g_gout[j] = _mm(o2, dh1, "tn", [BF16], f"g_out_{i}")
            do2 = _mm(dh1, w_gout[j], "nt", [BF16], f"d_o2_{i}")
            dpm, dpab, d_conv[j], d_alog[j], d_dtb[j], d_onorm[j] = _gdn_bwd(
                projm, projab, conv_full[j], gdn_a_log[j:j + 1], gdn_dt_bias[j:j + 1], gdn_onorm[j:j + 1],
                st_all, do2, seqs, f"gdn_bwd_{i}")
            g_main = _mm(y, dpm, "tn", [BF16], f"g_in_{i}")
            g_ab = _mm(y, dpab, "tn", [BF16], f"g_in_ab_{i}")
            g_gin[j] = jnp.concatenate([g_main, g_ab[:, :2 * N_HEADS]], axis=1)
            dy_ab = _mm(dpab, w_gab[j], "nt", [F32], f"d_y_ab_{i}")
            dy = _mm(dpm, w_gm[j], "nt", [F32], f"d_y_{i}", epilogue=lambda acc, e: (acc + e,), extras=(dy_ab,))
        else:
            proj, o_raw, st_all = mix
            g_hout[j] = _mm(o2, dh1, "tn", [BF16], f"g_out_{i}")
            do2 = _mm(dh1, w_hout[j], "nt", [BF16], f"d_o2_{i}")
            dp, d_lb[i], d_gnorm[j] = _hgrn_bwd(proj, lbs[i:i + 1], gnorm_full[j:j + 1], st_all, o_raw, do2,
                                               seqs, f"hgrn_bwd_{i}")
            g_hin[j] = _mm(y, dp, "tn", [BF16], f"g_in_{i}")
            dy = _mm(dp, w_hin[j], "nt", [F32], f"d_y_{i}")
        dh, d_nmix[i] = _rms_bwd(h_in, norm_mix[i:i + 1], dy, dh1, f"rms_mix_bwd_{i}")
    grad_x = dh.reshape(x.shape)

    full_grads = [_shard_cols(jnp.stack(g_gin)), jnp.stack(g_gout), jnp.stack(g_hin), jnp.stack(g_hout),
                  jnp.stack(g_up), jnp.stack(g_down)]
    shards = [gdn_w_in, gdn_w_out, hgrn_w_in, hgrn_w_out, mlp_w_up, mlp_w_down]
    parts = _exchange(
        full_grads, [(N_DEV,) + s.shape for s in shards],
        [_slot, _rows_of(r_out), _cols_of(c_hin), _rows_of(r_out), _cols_of(c_up), _rows_of(r_down)],
        [_slot] * 6, "scatter_grads")

    def update(name, w, g, m, v, slots=0):
        shape = w.shape
        c = shape[-1]
        w2, m2, v2 = w.reshape(-1, c), m.reshape(-1, c), v.reshape(-1, c)
        g2 = g.reshape((slots, -1, c)) if slots else g.reshape(-1, c)
        return [o.reshape(shape) for o in _adamw(w2, g2, m2, v2, "adamw_" + name, slots=slots)]

    upd = {}
    mats = dict(gdn_w_in=(gdn_w_in, m_gdn_w_in, v_gdn_w_in), gdn_w_out=(gdn_w_out, m_gdn_w_out, v_gdn_w_out),
                hgrn_w_in=(hgrn_w_in, m_hgrn_w_in, v_hgrn_w_in), hgrn_w_out=(hgrn_w_out, m_hgrn_w_out, v_hgrn_w_out),
                mlp_w_up=(mlp_w_up, m_mlp_w_up, v_mlp_w_up), mlp_w_down=(mlp_w_down, m_mlp_w_down, v_mlp_w_down))
    for part, (name, (w, m, v)) in zip(parts, mats.items()):
        upd[name] = update(name, w, part, m, v, slots=N_DEV)

    dlb_rows = jnp.concatenate(d_lb, axis=0)
    tail = jnp.concatenate(
        [jnp.concatenate(d_onorm, axis=1), jnp.concatenate(d_alog, axis=1), jnp.concatenate(d_dtb, axis=1)], axis=1)
    tail = jnp.pad(tail, ((0, 0), (0, d - tail.shape[1])))
    conv_rows = jnp.stack(d_conv).reshape(-1, d)
    packed = jnp.concatenate(
        [jnp.concatenate(d_nmix, axis=0), jnp.concatenate(d_nmlp, axis=0), d_nf, sq, dlb_rows,
         jnp.concatenate(d_gnorm, axis=0), tail, conv_rows], axis=0)
    pad_rows = (-packed.shape[0]) % 8
    packed = jnp.pad(packed, ((0, pad_rows), (0, 0)))
    tot = _all_reduce_small(packed, "reduce_small")
    r0 = 0
    g_nmix = tot[r0:r0 + DEPTH]; r0 += DEPTH
    g_nmlp = tot[r0:r0 + DEPTH]; r0 += DEPTH
    g_nf = tot[r0]; r0 += 1
    loss = tot[r0, 0]; r0 += 1
    g_lb = _lb_bwd(hgrn_lb_logits, tot[r0:r0 + DEPTH], "lb_bwd"); r0 += DEPTH
    g_gnorm_full = tot[r0:r0 + n_hgrn]; r0 += n_hgrn
    t_row = tot[r0]; r0 += 1
    g_conv_full = tot[r0:r0 + n_gdn * CONV_K * 3].reshape(n_gdn, CONV_K, 3 * d)
    g_onorm = t_row[0:n_gdn * HEAD_DIM].reshape(n_gdn, HEAD_DIM)
    o1 = n_gdn * HEAD_DIM
    g_alog = t_row[o1:o1 + n_gdn * N_HEADS].reshape(n_gdn, N_HEADS)
    g_dtb = t_row[o1 + n_gdn * N_HEADS:o1 + 2 * n_gdn * N_HEADS].reshape(n_gdn, N_HEADS)
    c_gn, c_cv = hgrn_gnorm.shape[1], gdn_conv.shape[2]
    g_gnorm = lax.dynamic_slice_in_dim(g_gnorm_full, me_i * c_gn, c_gn, axis=1)
    g_conv = lax.dynamic_slice_in_dim(g_conv_full, me_i * c_cv, c_cv, axis=2)

    upd["gdn_conv"] = update("gdn_conv", gdn_conv, g_conv, m_gdn_conv, v_gdn_conv)
    upd["gdn_a_log"] = update("gdn_a_log", gdn_a_log, g_alog, m_gdn_a_log, v_gdn_a_log)
    upd["gdn_dt_bias"] = update("gdn_dt_bias", gdn_dt_bias, g_dtb, m_gdn_dt_bias, v_gdn_dt_bias)
    upd["gdn_onorm"] = update("gdn_onorm", gdn_onorm, g_onorm, m_gdn_onorm, v_gdn_onorm)
    upd["hgrn_lb_logits"] = update("hgrn_lb_logits", hgrn_lb_logits, g_lb, m_hgrn_lb_logits, v_hgrn_lb_logits)
    upd["hgrn_gnorm"] = update("hgrn_gnorm", hgrn_gnorm, g_gnorm, m_hgrn_gnorm, v_hgrn_gnorm)
    upd["norm_mix"] = update("norm_mix", norm_mix, g_nmix, m_norm_mix, v_norm_mix)
    upd["norm_mlp"] = update("norm_mlp", norm_mlp, g_nmlp, m_norm_mlp, v_norm_mlp)
    upd["norm_final"] = update("norm_final", norm_final, g_nf, m_norm_final, v_norm_final)

    order = ["gdn_w_in", "gdn_conv", "gdn_a_log", "gdn_dt_bias", "gdn_onorm", "gdn_w_out", "hgrn_w_in",
             "hgrn_lb_logits", "hgrn_gnorm", "hgrn_w_out", "norm_mix", "norm_mlp", "mlp_w_up", "mlp_w_down",
             "norm_final"]
    outs = [loss, grad_x]
    for k in range(4):
        outs += [upd[name][k] for name in order]
    return tuple(outs)
```

```python
import functools

import jax
import jax.numpy as jnp
from jax import lax
from jax.experimental import pallas as pl
from jax.experimental.pallas import tpu as pltpu

F32 = jnp.float32
BF16 = jnp.bfloat16

D_MODEL = 1024
N_HEADS = 8
HEAD_DIM = 128
CHUNK = 64
SUB = 16
N_SUB = CHUNK // SUB
CONV_K = 4
HALO = 8
EPS = 1e-6
DEPTH = 4
N_DEV = 8
GDN_MAIN = 4 * D_MODEL
GDN_IN = GDN_MAIN + 2 * N_HEADS
AB_PAD = 128
HEAD_GROUP = 8

ADAM_LR = 0.001
ADAM_B1 = 0.9
ADAM_B2 = 0.999
ADAM_EPS = 1e-08
ADAM_WD = 0.01
ADAM_STEP = 10

VMEM_LIMIT = 56 * 1024 * 1024

_DIMS = {
    "nn": (((1,), (0,)), ((), ())),
    "nt": (((1,), (1,)), ((), ())),
    "tn": (((0,), (0,)), ((), ())),
}


def _parts(x, n):
    out = []
    r = x.astype(F32)
    for i in range(n):
        p = r.astype(BF16)
        out.append(p)
        if i + 1 < n:
            r = r - p.astype(F32)
    return out


def _dot_raw(a, b, mode, na, nb):
    ap, bp = _parts(a, na), _parts(b, nb)
    nmax = max(na, nb)
    acc = None
    for i, xa in enumerate(ap):
        for j, xb in enumerate(bp):
            if i + j < nmax:
                t = lax.dot_general(xa, xb, _DIMS[mode], preferred_element_type=F32)
                acc = t if acc is None else acc + t
    return acc


@functools.partial(jax.custom_vjp, nondiff_argnums=(2, 3, 4))
def _dot(a, b, mode, na, nb):
    return _dot_raw(a, b, mode, na, nb)


def _dot_fwd(a, b, mode, na, nb):
    return _dot_raw(a, b, mode, na, nb), (a, b)


def _dot_bwd(mode, na, nb, res, ct):
    a, b = res
    nc = max(na, nb)
    if mode == "nn":
        da = _dot_raw(ct, b, "nt", nc, nb)
        db = _dot_raw(a, ct, "tn", na, nc)
    elif mode == "nt":
        da = _dot_raw(ct, b, "nn", nc, nb)
        db = _dot_raw(ct, a, "tn", nc, na)
    else:
        da = _dot_raw(b, ct, "nt", nb, nc)
        db = _dot_raw(a, ct, "nn", na, nc)
    return da, db


_dot.defvjp(_dot_fwd, _dot_bwd)


def _iota2(shape, dim):
    return lax.broadcasted_iota(jnp.int32, shape, dim)


def _tril_f32(n):
    return (_iota2((n, n), 0) >= _iota2((n, n), 1)).astype(F32)


def _cumsum_rows(g):
    return _dot(_tril_f32(g.shape[0]), g, "nn", 1, 3)


def _inv_unit_lower(L):
    n = L.shape[0]
    eye = (_iota2((n, n), 0) == _iota2((n, n), 1)).astype(F32)
    p = -L
    t = eye + p
    k = 2
    while k < n:
        p = _dot_raw(p, p, "nn", 2, 2)
        t = t + _dot_raw(t, p, "nn", 2, 2)
        k *= 2
    return t


@jax.custom_vjp
def _solve_unit_lower(L, rhs):
    return _dot_raw(_inv_unit_lower(L), rhs, "nn", 2, 2)


def _solve_fwd(L, rhs):
    t = _inv_unit_lower(L)
    sol = _dot_raw(t, rhs, "nn", 2, 2)
    return sol, (t, sol)


def _solve_bwd(res, ct):
    t, sol = res
    y = _dot_raw(t, ct, "tn", 2, 2)
    return -_dot_raw(y, sol, "nt", 2, 2), y


_solve_unit_lower.defvjp(_solve_fwd, _solve_bwd)


def _softplus(x):
    return jnp.maximum(x, 0.0) + jnp.log1p(jnp.exp(-jnp.abs(x)))


def _rms(x, w):
    return x * lax.rsqrt(jnp.mean(x * x, axis=-1, keepdims=True) + EPS) * w


def _hg_pre(p, lb):
    qraw = p[:, 0:D_MODEL]
    f = p[:, D_MODEL:2 * D_MODEL]
    v = p[:, 2 * D_MODEL:3 * D_MODEL]
    g = jnp.log(lb + (1.0 - lb) * jax.nn.sigmoid(f))
    k = (1.0 - lb) * jax.nn.sigmoid(-f)
    q = jax.nn.silu(qraw) * (HEAD_DIM ** -0.5)
    return q, k, v, g, _cumsum_rows(g)


def _hg_head(st, q, k, v, g, gc):
    i3 = lax.broadcasted_iota(jnp.int32, (SUB, SUB, HEAD_DIM), 0)
    j3 = lax.broadcasted_iota(jnp.int32, (SUB, SUB, HEAD_DIM), 1)
    rows = []
    for s in range(N_SUB):
        lo = s * SUB
        qs, ks, vs, gs = q[lo:lo + SUB], k[lo:lo + SUB], v[lo:lo + SUB], gc[lo:lo + SUB]
        dec = jnp.exp(jnp.where(i3 >= j3, gs[:, None, :] - gs[None, :, :], -jnp.inf))
        a_diag = jnp.sum(qs[:, None, :] * ks[None, :, :] * dec, axis=-1)
        o_s = _dot(a_diag, vs, "nn", 1, 1)
        if s > 0:
            gb = gc[lo:lo + 1] - g[lo:lo + 1]
            q_off = qs * jnp.exp(gs - gb)
            k_off = k[0:lo] * jnp.exp(gb - gc[0:lo])
            a_off = _dot(q_off, k_off, "nt", 1, 1)
            o_s = o_s + _dot(a_off, v[0:lo], "nn", 1, 1)
        rows.append(o_s)
    o = jnp.concatenate(rows, axis=0) + _dot(q * jnp.exp(gc), st, "nt", 1, 1)
    g_last = gc[CHUNK - 1:CHUNK]
    st_new = st * jnp.exp(g_last) + _dot(v, k * jnp.exp(g_last - gc), "tn", 1, 1)
    return o, st_new


def _hg_post(o, gate, gw):
    return _rms(o, gw) * jax.nn.silu(gate)


def _gd_pre(xp, a, b, cw, alog, dtb):
    off = HALO - (CONV_K - 1)
    y = cw[0:1] * xp[off:off + CHUNK]
    for kk in range(1, CONV_K):
        y = y + cw[kk:kk + 1] * xp[off + kk:off + kk + CHUNK]
    c = jax.nn.silu(y)
    beta = jax.nn.sigmoid(b)
    g = -jnp.exp(alog) * _softplus(a + dtb)
    expand = (_iota2((N_HEADS, D_MODEL), 1) // HEAD_DIM == _iota2((N_HEADS, D_MODEL), 0)).astype(F32)
    return c, _dot(beta, expand, "nn", 3, 1), _dot(g, expand, "nn", 3, 1)


def _gd_head(st, q, k, v, beta, g, gate, onw):
    q = q * lax.rsqrt(jnp.sum(q * q, axis=-1, keepdims=True) + EPS) * (HEAD_DIM ** -0.5)
    k = k * lax.rsqrt(jnp.sum(k * k, axis=-1, keepdims=True) + EPS)
    gc = _cumsum_rows(g)
    ri = _iota2((CHUNK, CHUNK), 0)
    ci = _iota2((CHUNK, CHUNK), 1)
    diff = gc[:, 0:CHUNK] - gc.T[0:CHUNK, :]
    decay = jnp.exp(jnp.where(ri >= ci, diff, -jnp.inf))
    kb = k * beta
    egc = jnp.exp(gc)
    L = jnp.where(ri > ci, _dot(kb, k, "nt", 1, 1) * decay, 0.0)
    sol = _solve_unit_lower(L, jnp.concatenate([v * beta, kb * egc], axis=1))
    u = sol[:, 0:HEAD_DIM]
    w = sol[:, HEAD_DIM:2 * HEAD_DIM]
    a_qk = jnp.where(ri >= ci, _dot(q, k, "nt", 1, 1) * decay, 0.0)
    g_last = gc[CHUNK - 1:CHUNK]
    v_new = u - _dot(w, st, "nt", 1, 1)
    o = _dot(q * egc, st, "nt", 1, 1) + _dot(a_qk, v_new, "nn", 1, 1)
    st_new = st * jnp.exp(g_last) + _dot(v_new, k * jnp.exp(g_last - gc), "tn", 1, 1)
    return _rms(o, onw) * jax.nn.silu(gate), st_new


def _params(*sem):
    return pltpu.CompilerParams(dimension_semantics=sem, vmem_limit_bytes=VMEM_LIMIT)


def _tile(n, pref):
    t = min(n, pref)
    assert n % t == 0, (n, pref)
    return t


def _mm(a, b, mode, out_dtypes, name, epilogue=None, extras=(), tm=512, tn=512, tk=1024):
    if mode == "nn":
        (m, k), (k2, n) = a.shape, b.shape
    elif mode == "nt":
        (m, k), (n, k2) = a.shape, b.shape
    else:
        (k, m), (k2, n) = a.shape, b.shape
    assert k == k2, (a.shape, b.shape, mode)
    tm, tn, tk = _tile(m, tm), _tile(n, tn), _tile(k, tk)
    nk = k // tk
    ne, no = len(extras), len(out_dtypes)
    if epilogue is None:
        epilogue = lambda acc: (acc,)

    def body(*refs):
        a_ref, b_ref = refs[0], refs[1]
        ex = refs[2:2 + ne]
        outs = refs[2 + ne:2 + ne + no]
        part = lax.dot_general(a_ref[...].astype(BF16), b_ref[...].astype(BF16), _DIMS[mode],
                               preferred_element_type=F32)

        def finish(acc):
            for o_ref, val in zip(outs, epilogue(acc, *[e[...] for e in ex])):
                o_ref[...] = val.astype(o_ref.dtype)

        if nk == 1:
            finish(part)
        else:
            acc_ref = refs[-1]
            kk = pl.program_id(2)

            @pl.when(kk == 0)
            def _():
                acc_ref[...] = part

            @pl.when(kk > 0)
            def _():
                acc_ref[...] += part

            @pl.when(kk == nk - 1)
            def _():
                finish(acc_ref[...])

    if mode == "tn":
        a_spec = pl.BlockSpec((tk, tm), lambda i, j, kk: (kk, i))
    else:
        a_spec = pl.BlockSpec((tm, tk), lambda i, j, kk: (i, kk))
    if mode == "nt":
        b_spec = pl.BlockSpec((tn, tk), lambda i, j, kk: (j, kk))
    else:
        b_spec = pl.BlockSpec((tk, tn), lambda i, j, kk: (kk, j))
    o_spec = pl.BlockSpec((tm, tn), lambda i, j, kk: (i, j))
    res = pl.pallas_call(
        body,
        name=name,
        grid=(m // tm, n // tn, nk),
        in_specs=[a_spec, b_spec] + [o_spec] * ne,
        out_specs=[o_spec] * no,
        out_shape=[jax.ShapeDtypeStruct((m, n), dt) for dt in out_dtypes],
        scratch_shapes=[pltpu.VMEM((tm, tn), F32)] if nk > 1 else [],
        compiler_params=_params("parallel", "parallel", "arbitrary"),
    )(a, b, *extras)
    return res[0] if no == 1 else res


def _rms_fwd(x, w, name, tm=512):
    n, d = x.shape
    tm = _tile(n, tm)

    def body(x_ref, w_ref, y_ref):
        y_ref[...] = _rms(x_ref[...], w_ref[...]).astype(y_ref.dtype)

    return pl.pallas_call(
        body, name=name, grid=(n // tm,),
        in_specs=[pl.BlockSpec((tm, d), lambda i: (i, 0)), pl.BlockSpec((1, d), lambda i: (0, 0))],
        out_specs=pl.BlockSpec((tm, d), lambda i: (i, 0)),
        out_shape=jax.ShapeDtypeStruct((n, d), BF16),
        compiler_params=_params("arbitrary"),
    )(x, w)


def _rms_bwd(x, w, dy, dres, name, tm=512):
    n, d = x.shape
    tm = _tile(n, tm)

    def body(x_ref, w_ref, dy_ref, dres_ref, dx_ref, dw_ref):
        _, vjp = jax.vjp(_rms, x_ref[...], w_ref[...])
        dx, dw = vjp(dy_ref[...].astype(F32))
        dx_ref[...] = dres_ref[...] + dx

        @pl.when(pl.program_id(0) == 0)
        def _():
            dw_ref[...] = dw

        @pl.when(pl.program_id(0) > 0)
        def _():
            dw_ref[...] += dw

    row = pl.BlockSpec((tm, d), lambda i: (i, 0))
    vec = pl.BlockSpec((1, d), lambda i: (0, 0))
    return pl.pallas_call(
        body, name=name, grid=(n // tm,),
        in_specs=[row, vec, row, row],
        out_specs=[row, vec],
        out_shape=[jax.ShapeDtypeStruct((n, d), F32), jax.ShapeDtypeStruct((1, d), F32)],
        compiler_params=_params("arbitrary"),
    )(x, w, dy, dres)


def _loss_head(h, w, target, name, tm=512):
    n, d = h.shape
    tm = _tile(n, tm)

    def body(h_ref, w_ref, t_ref, dh_ref, dw_ref, sq_ref):
        y, vjp = jax.vjp(_rms, h_ref[...], w_ref[...])
        err = y - t_ref[...]
        dh, dw = vjp(err * (1.0 / d))
        dh_ref[...] = dh
        sq = jnp.sum(err * err, axis=0, keepdims=True)

        @pl.when(pl.program_id(0) == 0)
        def _():
            dw_ref[...] = dw
            sq_ref[...] = sq

        @pl.when(pl.program_id(0) > 0)
        def _():
            dw_ref[...] += dw
            sq_ref[...] += sq

        @pl.when(pl.program_id(0) == n // tm - 1)
        def _():
            total = jnp.sum(sq_ref[...], axis=1, keepdims=True) * (0.5 / d)
            sq_ref[...] = jnp.broadcast_to(total, sq_ref.shape)

    row = pl.BlockSpec((tm, d), lambda i: (i, 0))
    vec = pl.BlockSpec((1, d), lambda i: (0, 0))
    return pl.pallas_call(
        body, name=name, grid=(n // tm,),
        in_specs=[row, vec, row],
        out_specs=[row, vec, vec],
        out_shape=[jax.ShapeDtypeStruct((n, d), F32), jax.ShapeDtypeStruct((1, d), F32),
                   jax.ShapeDtypeStruct((1, d), F32)],
        compiler_params=_params("arbitrary"),
    )(h, w, target)


def _lower_bounds(logits):
    sm = jax.nn.softmax(logits, axis=0)
    rows = [sm[0:1] * 0.0]
    for r in range(1, DEPTH):
        rows.append(rows[-1] + sm[r:r + 1])
    return jnp.concatenate(rows, axis=0)


def _lb_fwd(logits, name):
    def body(l_ref, o_ref):
        o_ref[...] = _lower_bounds(l_ref[...])

    return pl.pallas_call(body, name=name, out_shape=jax.ShapeDtypeStruct(logits.shape, F32))(logits)


def _lb_bwd(logits, dlb, name):
    def body(l_ref, d_ref, o_ref):
        _, vjp = jax.vjp(_lower_bounds, l_ref[...])
        (o_ref[...],) = vjp(d_ref[...])

    return pl.pallas_call(body, name=name, out_shape=jax.ShapeDtypeStruct(logits.shape, F32))(logits, dlb)


def _head_slice(h):
    if isinstance(h, int):
        return pl.ds(h * HEAD_DIM, HEAD_DIM)
    return pl.ds(pl.multiple_of(h * HEAD_DIM, HEAD_DIM), HEAD_DIM)


def _head_groups(group_body):
    if HEAD_GROUP == N_HEADS:
        group_body(list(range(N_HEADS)))
        return

    def trip(i, carry):
        group_body([i * HEAD_GROUP + t for t in range(HEAD_GROUP)])
        return carry

    lax.fori_loop(0, N_HEADS // HEAD_GROUP, trip, 0)


def _stack_heads(ref, hs, first=0):
    return jnp.stack([ref[:, _head_slice(h + first)] for h in hs])


def _unstack_heads(ref, hs, val, first=0):
    for t, h in enumerate(hs):
        ref[:, _head_slice(h + first)] = val[t].astype(ref.dtype)


_GD_HEADS = jax.vmap(_gd_head, in_axes=(0, 0, 0, 0, 0, 0, 0, None))


def _hgrn_fwd(proj, lb, gw, seqs, name):
    n = proj.shape[0]
    nc = n // seqs // CHUNK
    d = D_MODEL

    def body(p_ref, lb_ref, gw_ref, o2_ref, o_ref, st_all_ref, st_sc, q_sc, k_sc, v_sc, g_sc, gc_sc):
        @pl.when(pl.program_id(1) == 0)
        def _():
            st_sc[...] = jnp.zeros_like(st_sc)

        q_sc[...], k_sc[...], v_sc[...], g_sc[...], gc_sc[...] = _hg_pre(p_ref[:, 0:3 * d], lb_ref[...])
        st_all_ref[0] = st_sc[...]

        def group(hs):
            sts = pl.ds(hs[0], len(hs))
            o, st_new = jax.vmap(_hg_head)(st_sc[sts], *[_stack_heads(r, hs) for r in (q_sc, k_sc, v_sc, g_sc, gc_sc)])
            _unstack_heads(o_ref, hs, o)
            st_sc[sts] = st_new

        _head_groups(group)
        o2_ref[...] = _hg_post(o_ref[...], p_ref[:, 3 * d:4 * d], gw_ref[...]).astype(o2_ref.dtype)

    idx = lambda b, c: (b * nc + c, 0)
    vec = pl.BlockSpec((1, d), lambda b, c: (0, 0))
    act = pl.BlockSpec((CHUNK, d), idx)
    return pl.pallas_call(
        body, name=name, grid=(seqs, nc),
        in_specs=[pl.BlockSpec((CHUNK, 4 * d), idx), vec, vec],
        out_specs=[act, act, pl.BlockSpec((1, N_HEADS, HEAD_DIM, HEAD_DIM), lambda b, c: (b * nc + c, 0, 0, 0))],
        out_shape=[jax.ShapeDtypeStruct((n, d), BF16), jax.ShapeDtypeStruct((n, d), F32),
                   jax.ShapeDtypeStruct((n // CHUNK, N_HEADS, HEAD_DIM, HEAD_DIM), F32)],
        scratch_shapes=[pltpu.VMEM((N_HEADS, HEAD_DIM, HEAD_DIM), F32)] + [pltpu.VMEM((CHUNK, d), F32)] * 5,
        compiler_params=_params("arbitrary", "arbitrary"),
    )(proj, lb, gw)


def _hgrn_bwd(proj, lb, gw, st_all, o, do2, seqs, name):
    n = proj.shape[0]
    nc = n // seqs // CHUNK
    d = D_MODEL

    def body(p_ref, lb_ref, gw_ref, st_all_ref, o_ref, do2_ref, dp_ref, dlb_ref, dgw_ref,
             dst_sc, q_sc, k_sc, v_sc, g_sc, gc_sc, do_sc, dq_sc, dk_sc, dv_sc, dg_sc, dgc_sc):
        first = (pl.program_id(0) == 0) & (pl.program_id(1) == 0)

        @pl.when(pl.program_id(1) == 0)
        def _():
            dst_sc[...] = jnp.zeros_like(dst_sc)

        pre_out, pre_vjp = jax.vjp(_hg_pre, p_ref[:, 0:3 * d], lb_ref[...])
        q_sc[...], k_sc[...], v_sc[...], g_sc[...], gc_sc[...] = pre_out
        _, post_vjp = jax.vjp(_hg_post, o_ref[...], p_ref[:, 3 * d:4 * d], gw_ref[...])
        do_sc[...], dgate, dgw = post_vjp(do2_ref[...].astype(F32))
        dp_ref[:, 3 * d:4 * d] = dgate.astype(dp_ref.dtype)

        def group(hs):
            sts = pl.ds(hs[0], len(hs))
            _, vjp = jax.vjp(jax.vmap(_hg_head), st_all_ref[0, sts],
                             *[_stack_heads(r, hs) for r in (q_sc, k_sc, v_sc, g_sc, gc_sc)])
            grads = vjp((_stack_heads(do_sc, hs), dst_sc[sts]))
            dst_sc[sts] = grads[0]
            for r, val in zip((dq_sc, dk_sc, dv_sc, dg_sc, dgc_sc), grads[1:]):
                _unstack_heads(r, hs, val)

        _head_groups(group)
        dp, dlb = pre_vjp((dq_sc[...], dk_sc[...], dv_sc[...], dg_sc[...], dgc_sc[...]))
        dp_ref[:, 0:3 * d] = dp.astype(dp_ref.dtype)

        @pl.when(first)
        def _():
            dlb_ref[...] = dlb
            dgw_ref[...] = dgw

        @pl.when(jnp.logical_not(first))
        def _():
            dlb_ref[...] += dlb
            dgw_ref[...] += dgw

    idx = lambda b, c: (b * nc + nc - 1 - c, 0)
    vec = pl.BlockSpec((1, d), lambda b, c: (0, 0))
    act = pl.BlockSpec((CHUNK, d), idx)
    wide = pl.BlockSpec((CHUNK, 4 * d), idx)
    return pl.pallas_call(
        body, name=name, grid=(seqs, nc),
        in_specs=[wide, vec, vec,
                  pl.BlockSpec((1, N_HEADS, HEAD_DIM, HEAD_DIM), lambda b, c: (b * nc + nc - 1 - c, 0, 0, 0)),
                  act, act],
        out_specs=[wide, vec, vec],
        out_shape=[jax.ShapeDtypeStruct((n, 4 * d), BF16), jax.ShapeDtypeStruct((1, d), F32),
                   jax.ShapeDtypeStruct((1, d), F32)],
        scratch_shapes=[pltpu.VMEM((N_HEADS, HEAD_DIM, HEAD_DIM), F32)] + [pltpu.VMEM((CHUNK, d), F32)] * 11,
        compiler_params=_params("arbitrary", "arbitrary"),
    )(proj, lb, gw, st_all, o, do2)


def _gd_xp(halo_ref, p_ref, first_chunk):
    halo = jnp.where(first_chunk, 0.0, halo_ref[...])
    return jnp.concatenate([halo, p_ref[:, 0:3 * D_MODEL]], axis=0)


def _gdn_fwd(projm, projab, cw, alog, dtb, onw, seqs, name):
    n = projm.shape[0]
    nc = n // seqs // CHUNK
    d = D_MODEL
    per_halo = CHUNK // HALO

    def body(p_ref, halo_ref, ab_ref, cw_ref, alog_ref, dtb_ref, onw_ref, o2_ref, st_all_ref,
             st_sc, c_sc, beta_sc, g_sc):
        @pl.when(pl.program_id(1) == 0)
        def _():
            st_sc[...] = jnp.zeros_like(st_sc)

        xp = _gd_xp(halo_ref, p_ref, pl.program_id(1) == 0)
        c_sc[...], beta_sc[...], g_sc[...] = _gd_pre(
            xp, ab_ref[:, 0:N_HEADS], ab_ref[:, N_HEADS:2 * N_HEADS], cw_ref[...], alog_ref[...], dtb_ref[...])
        st_all_ref[0] = st_sc[...]

        def group(hs):
            sts = pl.ds(hs[0], len(hs))
            o2, st_new = _GD_HEADS(
                st_sc[sts], _stack_heads(c_sc, hs), _stack_heads(c_sc, hs, N_HEADS), _stack_heads(c_sc, hs, 2 * N_HEADS),
                _stack_heads(beta_sc, hs), _stack_heads(g_sc, hs), _stack_heads(p_ref, hs, 3 * N_HEADS), onw_ref[...])
            _unstack_heads(o2_ref, hs, o2)
            st_sc[sts] = st_new

        _head_groups(group)

    idx = lambda b, c: (b * nc + c, 0)
    const = lambda b, c: (0, 0)
    return pl.pallas_call(
        body, name=name, grid=(seqs, nc),
        in_specs=[pl.BlockSpec((CHUNK, 4 * d), idx),
                  pl.BlockSpec((HALO, 3 * d), lambda b, c: (jnp.maximum((b * nc + c) * per_halo - 1, 0), 0)),
                  pl.BlockSpec((CHUNK, AB_PAD), idx),
                  pl.BlockSpec((CONV_K, 3 * d), const), pl.BlockSpec((1, N_HEADS), const),
                  pl.BlockSpec((1, N_HEADS), const), pl.BlockSpec((1, HEAD_DIM), const)],
        out_specs=[pl.BlockSpec((CHUNK, d), idx),
                   pl.BlockSpec((1, N_HEADS, HEAD_DIM, HEAD_DIM), lambda b, c: (b * nc + c, 0, 0, 0))],
        out_shape=[jax.ShapeDtypeStruct((n, d), BF16),
                   jax.ShapeDtypeStruct((n // CHUNK, N_HEADS, HEAD_DIM, HEAD_DIM), F32)],
        scratch_shapes=[pltpu.VMEM((N_HEADS, HEAD_DIM, HEAD_DIM), F32), pltpu.VMEM((CHUNK, 3 * d), F32),
                        pltpu.VMEM((CHUNK, d), F32), pltpu.VMEM((CHUNK, d), F32)],
        compiler_params=_params("arbitrary", "arbitrary"),
    )(projm, projm, projab, cw, alog, dtb, onw)


def _gdn_bwd(projm, projab, cw, alog, dtb, onw, st_all, do2, seqs, name):
    n = projm.shape[0]
    nc = n // seqs // CHUNK
    d = D_MODEL
    per_halo = CHUNK // HALO

    def body(p_ref, halo_ref, ab_ref, cw_ref, alog_ref, dtb_ref, onw_ref, st_all_ref, do2_ref,
             dp_ref, dab_ref, dcw_ref, dalog_ref, ddtb_ref, donw_ref,
             dst_sc, dhalo_sc, c_sc, beta_sc, g_sc, dc_sc, dbeta_sc, dg_sc, donw_sc):
        step = pl.program_id(1)
        first = (pl.program_id(0) == 0) & (step == 0)

        @pl.when(step == 0)
        def _():
            dst_sc[...] = jnp.zeros_like(dst_sc)
            dhalo_sc[...] = jnp.zeros_like(dhalo_sc)

        donw_sc[...] = jnp.zeros_like(donw_sc)
        xp = _gd_xp(halo_ref, p_ref, step == nc - 1)
        pre_out, pre_vjp = jax.vjp(_gd_pre, xp, ab_ref[:, 0:N_HEADS], ab_ref[:, N_HEADS:2 * N_HEADS],
                                   cw_ref[...], alog_ref[...], dtb_ref[...])
        c_sc[...], beta_sc[...], g_sc[...] = pre_out

        def group(hs):
            sts = pl.ds(hs[0], len(hs))
            _, vjp = jax.vjp(
                _GD_HEADS, st_all_ref[0, sts], _stack_heads(c_sc, hs), _stack_heads(c_sc, hs, N_HEADS),
                _stack_heads(c_sc, hs, 2 * N_HEADS), _stack_heads(beta_sc, hs), _stack_heads(g_sc, hs),
                _stack_heads(p_ref, hs, 3 * N_HEADS), onw_ref[...])
            dst, dq, dk, dv, dbeta, dg, dgate, donw = vjp((_stack_heads(do2_ref, hs).astype(F32), dst_sc[sts]))
            dst_sc[sts] = dst
            _unstack_heads(dc_sc, hs, dq)
            _unstack_heads(dc_sc, hs, dk, N_HEADS)
            _unstack_heads(dc_sc, hs, dv, 2 * N_HEADS)
            _unstack_heads(dbeta_sc, hs, dbeta)
            _unstack_heads(dg_sc, hs, dg)
            _unstack_heads(dp_ref, hs, dgate, 3 * N_HEADS)
            donw_sc[...] += donw

        _head_groups(group)
        dxp, da, db, dcw, dalog, ddtb = pre_vjp((dc_sc[...], dbeta_sc[...], dg_sc[...]))
        dqkv = jnp.concatenate([dxp[HALO:CHUNK], dxp[CHUNK:HALO + CHUNK] + dhalo_sc[...]], axis=0)
        dp_ref[:, 0:3 * d] = dqkv.astype(dp_ref.dtype)
        dhalo_sc[...] = dxp[0:HALO]
        dab_ref[...] = jnp.concatenate(
            [da, db, jnp.zeros((CHUNK, AB_PAD - 2 * N_HEADS), F32)], axis=1).astype(dab_ref.dtype)

        @pl.when(first)
        def _():
            dcw_ref[...] = dcw
            dalog_ref[...] = dalog
            ddtb_ref[...] = ddtb
            donw_ref[...] = donw_sc[...]

        @pl.when(jnp.logical_not(first))
        def _():
            dcw_ref[...] += dcw
            dalog_ref[...] += dalog
            ddtb_ref[...] += ddtb
            donw_ref[...] += donw_sc[...]

    rev = lambda b, c: b * nc + nc - 1 - c
    idx = lambda b, c: (rev(b, c), 0)
    const = lambda b, c: (0, 0)
    small = [pl.BlockSpec((CONV_K, 3 * d), const), pl.BlockSpec((1, N_HEADS), const),
             pl.BlockSpec((1, N_HEADS), const), pl.BlockSpec((1, HEAD_DIM), const)]
    return pl.pallas_call(
        body, name=name, grid=(seqs, nc),
        in_specs=[pl.BlockSpec((CHUNK, 4 * d), idx),
                  pl.BlockSpec((HALO, 3 * d), lambda b, c: (jnp.maximum(rev(b, c) * per_halo - 1, 0), 0)),
                  pl.BlockSpec((CHUNK, AB_PAD), idx)] + small + [
                  pl.BlockSpec((1, N_HEADS, HEAD_DIM, HEAD_DIM), lambda b, c: (rev(b, c), 0, 0, 0)),
                  pl.BlockSpec((CHUNK, d), idx)],
        out_specs=[pl.BlockSpec((CHUNK, 4 * d), idx), pl.BlockSpec((CHUNK, AB_PAD), idx)] + small,
        out_shape=[jax.ShapeDtypeStruct((n, 4 * d), BF16), jax.ShapeDtypeStruct((n, AB_PAD), BF16),
                   jax.ShapeDtypeStruct((CONV_K, 3 * d), F32), jax.ShapeDtypeStruct((1, N_HEADS), F32),
                   jax.ShapeDtypeStruct((1, N_HEADS), F32), jax.ShapeDtypeStruct((1, HEAD_DIM), F32)],
        scratch_shapes=[pltpu.VMEM((N_HEADS, HEAD_DIM, HEAD_DIM), F32), pltpu.VMEM((HALO, 3 * d), F32),
                        pltpu.VMEM((CHUNK, 3 * d), F32), pltpu.VMEM((CHUNK, d), F32), pltpu.VMEM((CHUNK, d), F32),
                        pltpu.VMEM((CHUNK, 3 * d), F32), pltpu.VMEM((CHUNK, d), F32), pltpu.VMEM((CHUNK, d), F32),
                        pltpu.VMEM((1, HEAD_DIM), F32)],
        compiler_params=_params("arbitrary", "arbitrary"),
    )(projm, projm, projab, cw, alog, dtb, onw, st_all, do2)


def _adamw(w, g, m, v, name, slots=0, tr=256):
    r, c = w.shape
    tr = _tile(r, tr)
    b1c = 1.0 - ADAM_B1 ** ADAM_STEP
    b2c = 1.0 - ADAM_B2 ** ADAM_STEP

    def body(w_ref, g_ref, m_ref, v_ref, go_ref, d_ref, mo_ref, vo_ref):
        if slots:
            g = g_ref[0].astype(F32)
            for s in range(1, slots):
                g = g + g_ref[s].astype(F32)
        else:
            g = g_ref[...]
        m_new = ADAM_B1 * m_ref[...] + (1.0 - ADAM_B1) * g
        v_new = ADAM_B2 * v_ref[...] + (1.0 - ADAM_B2) * (g * g)
        go_ref[...] = g
        mo_ref[...] = m_new
        vo_ref[...] = v_new
        d_ref[...] = -ADAM_LR * ((m_new / b1c) / (jnp.sqrt(v_new / b2c) + ADAM_EPS) + ADAM_WD * w_ref[...])

    blk = pl.BlockSpec((tr, c), lambda i: (i, 0))
    g_spec = pl.BlockSpec((slots, tr, c), lambda i: (0, i, 0)) if slots else blk
    return pl.pallas_call(
        body, name=name, grid=(r // tr,),
        in_specs=[blk, g_spec, blk, blk],
        out_specs=[blk] * 4,
        out_shape=[jax.ShapeDtypeStruct((r, c), F32)] * 4,
        compiler_params=_params("arbitrary"),
    )(w, g, m, v)


def _mesh_pos():
    return lax.axis_index("x"), lax.axis_index("y"), lax.axis_index("c")


def _flip(pos, p):
    x, y, c = pos
    return ((1 - x) if p & 4 else x, (1 - y) if p & 2 else y, (1 - c) if p & 1 else c)


def _lin(pos):
    return 4 * pos[0] + 2 * pos[1] + pos[2]


def _exchange(srcs, dst_shapes, src_pick, dst_pick, name):
    na = len(srcs)

    def body(*refs):
        src = refs[0:na]
        dst = refs[na:2 * na]
        send_sems, recv_sems, loc_sems = refs[2 * na:]
        me = _mesh_pos()
        me_i = _lin(me)
        started = []
        for a in range(na):
            loc = pltpu.make_async_copy(src_pick[a](src[a], me_i), dst_pick[a](dst[a], me_i), loc_sems.at[a])
            loc.start()
            started.append(loc)
        for p in range(1, N_DEV):
            peer = _flip(me, p)
            for a in range(na):
                cp = pltpu.make_async_remote_copy(
                    src_ref=src_pick[a](src[a], _lin(peer)), dst_ref=dst_pick[a](dst[a], me_i),
                    send_sem=send_sems.at[a, p - 1], recv_sem=recv_sems.at[a, p - 1],
                    device_id=peer, device_id_type=pl.DeviceIdType.MESH)
                cp.start()
        for p in range(1, N_DEV):
            peer = _flip(me, p)
            for a in range(na):
                cp = pltpu.make_async_remote_copy(
                    src_ref=src_pick[a](src[a], _lin(peer)), dst_ref=dst_pick[a](dst[a], _lin(peer)),
                    send_sem=send_sems.at[a, p - 1], recv_sem=recv_sems.at[a, p - 1],
                    device_id=peer, device_id_type=pl.DeviceIdType.MESH)
                cp.wait_recv()
                cp.wait_send()
        for loc in started:
            loc.wait()

    anyspec = pl.BlockSpec(memory_space=pl.ANY)
    return pl.pallas_call(
        body, name=name,
        in_specs=[anyspec] * na, out_specs=[anyspec] * na,
        out_shape=[jax.ShapeDtypeStruct(s, src.dtype) for s, src in zip(dst_shapes, srcs)],
        scratch_shapes=[pltpu.SemaphoreType.DMA((na, N_DEV - 1)), pltpu.SemaphoreType.DMA((na, N_DEV - 1)),
                        pltpu.SemaphoreType.DMA((na,))],
        compiler_params=pltpu.CompilerParams(has_side_effects=True),
    )(*srcs)


def _whole(ref, i):
    return ref


def _slot(ref, i):
    return ref.at[i]


def _rows_of(r):
    return lambda ref, i: ref.at[:, pl.ds(pl.multiple_of(i * r, r), r), :]


def _cols_of(c):
    return lambda ref, i: ref.at[:, :, pl.ds(pl.multiple_of(i * c, c), c)]


def _all_reduce_small(buf, name):
    r, c = buf.shape

    def body(src_ref, out_ref, all_ref, send_sems, recv_sems):
        me = _mesh_pos()
        me_i = _lin(me)
        all_ref[me_i] = src_ref[...]
        for p in range(1, N_DEV):
            peer = _flip(me, p)
            pltpu.make_async_remote_copy(
                src_ref=src_ref, dst_ref=all_ref.at[me_i], send_sem=send_sems.at[p - 1], recv_sem=recv_sems.at[p - 1],
                device_id=peer, device_id_type=pl.DeviceIdType.MESH).start()
        for p in range(1, N_DEV):
            peer = _flip(me, p)
            cp = pltpu.make_async_remote_copy(
                src_ref=src_ref, dst_ref=all_ref.at[_lin(peer)], send_sem=send_sems.at[p - 1],
                recv_sem=recv_sems.at[p - 1], device_id=peer, device_id_type=pl.DeviceIdType.MESH)
            cp.wait_recv()
            cp.wait_send()
        acc = all_ref[0]
        for s in range(1, N_DEV):
            acc = acc + all_ref[s]
        out_ref[...] = acc

    vm = pl.BlockSpec(memory_space=pltpu.VMEM)
    return pl.pallas_call(
        body, name=name, in_specs=[vm], out_specs=vm,
        out_shape=jax.ShapeDtypeStruct((r, c), F32),
        scratch_shapes=[pltpu.VMEM((N_DEV, r, c), F32), pltpu.SemaphoreType.DMA((N_DEV - 1,)),
                        pltpu.SemaphoreType.DMA((N_DEV - 1,))],
        compiler_params=pltpu.CompilerParams(has_side_effects=True),
    )(buf)


def _unshard_cols(g):
    s, l, r, c = g.shape
    return jnp.transpose(g, (1, 2, 0, 3)).reshape(l, r, s * c)


def _shard_cols(w):
    l, r, c = w.shape
    return jnp.transpose(w.reshape(l, r, N_DEV, c // N_DEV), (2, 0, 1, 3))


def kernel(x, gdn_w_in, gdn_conv, gdn_a_log, gdn_dt_bias, gdn_onorm, gdn_w_out, hgrn_w_in, hgrn_lb_logits, hgrn_gnorm, hgrn_w_out, norm_mix, norm_mlp, mlp_w_up, mlp_w_down, norm_final, loss_target, m_gdn_w_in, m_gdn_conv, m_gdn_a_log, m_gdn_dt_bias, m_gdn_onorm, m_gdn_w_out, m_hgrn_w_in, m_hgrn_lb_logits, m_hgrn_gnorm, m_hgrn_w_out, m_norm_mix, m_norm_mlp, m_mlp_w_up, m_mlp_w_down, m_norm_final, v_gdn_w_in, v_gdn_conv, v_gdn_a_log, v_gdn_dt_bias, v_gdn_onorm, v_gdn_w_out, v_hgrn_w_in, v_hgrn_lb_logits, v_hgrn_gnorm, v_hgrn_w_out, v_norm_mix, v_norm_mlp, v_mlp_w_up, v_mlp_w_down, v_norm_final):
    seqs, seq_len, d = x.shape
    n = seqs * seq_len
    me_i = _lin(_mesh_pos())
    x2 = x.reshape(n, d)
    target = loss_target.reshape(n, d)
    n_gdn, n_hgrn = gdn_w_in.shape[0], hgrn_w_in.shape[0]

    srcs = [gdn_w_in.astype(BF16), gdn_w_out.astype(BF16), hgrn_w_in.astype(BF16), hgrn_w_out.astype(BF16),
            mlp_w_up.astype(BF16), mlp_w_down.astype(BF16), gdn_conv, hgrn_gnorm]
    r_out, r_down = gdn_w_out.shape[1], mlp_w_down.shape[1]
    c_hin, c_up = hgrn_w_in.shape[2], mlp_w_up.shape[2]
    g_win, w_gout, w_hin, w_hout, w_up, w_down, g_conv, g_gnorm = _exchange(
        srcs,
        [(N_DEV,) + gdn_w_in.shape, (n_gdn, N_DEV * r_out, d), (n_hgrn, d, N_DEV * c_hin),
         (n_hgrn, N_DEV * r_out, d), (DEPTH, d, N_DEV * c_up), (DEPTH, N_DEV * r_down, d),
         (N_DEV,) + gdn_conv.shape, (N_DEV,) + hgrn_gnorm.shape],
        [_whole] * 8,
        [_slot, _rows_of(r_out), _cols_of(c_hin), _rows_of(r_out), _cols_of(c_up), _rows_of(r_down), _slot, _slot],
        "gather_weights")
    w_gin = _unshard_cols(g_win)
    w_gm = w_gin[:, :, :GDN_MAIN]
    w_gab = jnp.pad(w_gin[:, :, GDN_MAIN:], ((0, 0), (0, 0), (0, AB_PAD - 2 * N_HEADS)))
    conv_full = _unshard_cols(g_conv)
    gnorm_full = jnp.transpose(g_gnorm, (1, 0, 2)).reshape(n_hgrn, d)
    lbs = _lb_fwd(hgrn_lb_logits, "lb_fwd")

    saved = []
    h = x2
    for i in range(DEPTH):
        j = i // 2
        y = _rms_fwd(h, norm_mix[i:i + 1], f"rms_mix_{i}")
        if i % 2 == 0:
            projm = _mm(y, w_gm[j], "nn", [F32], f"gdn_proj_{i}")
            projab = _mm(y, w_gab[j], "nn", [F32], f"gdn_proj_ab_{i}")
            o2, st_all = _gdn_fwd(projm, projab, conv_full[j], gdn_a_log[j:j + 1], gdn_dt_bias[j:j + 1],
                                  gdn_onorm[j:j + 1], seqs, f"gdn_fwd_{i}")
            mix = (projm, projab, st_all)
            w_o = w_gout[j]
        else:
            proj = _mm(y, w_hin[j], "nn", [F32], f"hgrn_proj_{i}")
            o2, o_raw, st_all = _hgrn_fwd(proj, lbs[i:i + 1], gnorm_full[j:j + 1], seqs, f"hgrn_fwd_{i}")
            mix = (proj, o_raw, st_all)
            w_o = w_hout[j]
        h1 = _mm(o2, w_o, "nn", [F32], f"mix_out_{i}", epilogue=lambda acc, res: (res + acc,), extras=(h,))
        y2 = _rms_fwd(h1, norm_mlp[i:i + 1], f"rms_mlp_{i}")
        u, act = _mm(y2, w_up[i], "nn", [BF16, BF16], f"mlp_up_{i}",
                     epilogue=lambda acc: (acc, jnp.square(jnp.maximum(acc, 0.0))))
        h2 = _mm(act, w_down[i], "nn", [F32], f"mlp_down_{i}", epilogue=lambda acc, res: (res + acc,), extras=(h1,))
        saved.append((h, y, mix, o2, h1, y2, u, act))
        h = h2

    dh, d_nf, sq = _loss_head(h, norm_final.reshape(1, d), target, "loss_head")

    g_up, g_down = [None] * DEPTH, [None] * DEPTH
    g_gin, g_gout, g_hin, g_hout = [None] * n_gdn, [None] * n_gdn, [None] * n_hgrn, [None] * n_hgrn
    d_nmix, d_nmlp = [None] * DEPTH, [None] * DEPTH
    d_conv, d_alog, d_dtb, d_onorm = [None] * n_gdn, [None] * n_gdn, [None] * n_gdn, [None] * n_gdn
    d_lb = [jnp.zeros((1, d), F32)] * DEPTH
    d_gnorm = [None] * n_hgrn
    for i in reversed(range(DEPTH)):
        j = i // 2
        h_in, y, mix, o2, h1, y2, u, act = saved[i]
        g_down[i] = _mm(act, dh, "tn", [BF16], f"g_down_{i}")
        du = _mm(dh, w_down[i], "nt", [BF16], f"d_u_{i}",
                 epilogue=lambda acc, uu: (acc * (2.0 * jnp.maximum(uu.astype(F32), 0.0)),), extras=(u,))
        g_up[i] = _mm(y2, du, "tn", [BF16], f"g_up_{i}")
        dy2 = _mm(du, w_up[i], "nt", [F32], f"d_y2_{i}")
        dh1, d_nmlp[i] = _rms_bwd(h1, norm_mlp[i:i + 1], dy2, dh, f"rms_mlp_bwd_{i}")
        if i % 2 == 0:
            projm, projab, st_all = mix
            g_gout[j] = _mm(o2, dh1, "tn", [BF16], f"g_out_{i}")
            do2 = _mm(dh1, w_gout[j], "nt", [BF16], f"d_o2_{i}")
            dpm, dpab, d_conv[j], d_alog[j], d_dtb[j], d_onorm[j] = _gdn_bwd(
                projm, projab, conv_full[j], gdn_a_log[j:j + 1], gdn_dt_bias[j:j + 1], gdn_onorm[j:j + 1],
                st_all, do2, seqs, f"gdn_bwd_{i}")
            g_main = _mm(y, dpm, "tn", [BF16], f"g_in_{i}")
            g_ab = _mm(y, dpab, "tn", [BF16], f"g_in_ab_{i}")
            g_gin[j] = jnp.concatenate([g_main, g_ab[:, :2 * N_HEADS]], axis=1)
            dy_ab = _mm(dpab, w_gab[j], "nt", [F32], f"d_y_ab_{i}")
            dy = _mm(dpm, w_gm[j], "nt", [F32], f"d_y_{i}", epilogue=lambda acc, e: (acc + e,), extras=(dy_ab,))
        else:
            proj, o_raw, st_all = mix
            g_hout[j] = _mm(o2, dh1, "tn", [BF16], f"g_out_{i}")
            do2 = _mm(dh1, w_hout[j], "nt", [BF16], f"d_o2_{i}")
            dp, d_lb[i], d_gnorm[j] = _hgrn_bwd(proj, lbs[i:i + 1], gnorm_full[j:j + 1], st_all, o_raw, do2,
                                               seqs, f"hgrn_bwd_{i}")
            g_hin[j] = _mm(y, dp, "tn", [BF16], f"g_in_{i}")
            dy = _mm(dp, w_hin[j], "nt", [F32], f"d_y_{i}")
        dh, d_nmix[i] = _rms_bwd(h_in, norm_mix[i:i + 1], dy, dh1, f"rms_mix_bwd_{i}")
    grad_x = dh.reshape(x.shape)

    full_grads = [_shard_cols(jnp.stack(g_gin)), jnp.stack(g_gout), jnp.stack(g_hin), jnp.stack(g_hout),
                  jnp.stack(g_up), jnp.stack(g_down)]
    shards = [gdn_w_in, gdn_w_out, hgrn_w_in, hgrn_w_out, mlp_w_up, mlp_w_down]
    parts = _exchange(
        full_grads, [(N_DEV,) + s.shape for s in shards],
        [_slot, _rows_of(r_out), _cols_of(c_hin), _rows_of(r_out), _cols_of(c_up), _rows_of(r_down)],
        [_slot] * 6, "scatter_grads")

    def update(name, w, g, m, v, slots=0):
        shape = w.shape
        c = shape[-1]
        w2, m2, v2 = w.reshape(-1, c), m.reshape(-1, c), v.reshape(-1, c)
        g2 = g.reshape((slots, -1, c)) if slots else g.reshape(-1, c)
        return [o.reshape(shape) for o in _adamw(w2, g2, m2, v2, "adamw_" + name, slots=slots)]

    upd = {}
    mats = dict(gdn_w_in=(gdn_w_in, m_gdn_w_in, v_gdn_w_in), gdn_w_out=(gdn_w_out, m_gdn_w_out, v_gdn_w_out),
                hgrn_w_in=(hgrn_w_in, m_hgrn_w_in, v_hgrn_w_in), hgrn_w_out=(hgrn_w_out, m_hgrn_w_out, v_hgrn_w_out),
                mlp_w_up=(mlp_w_up, m_mlp_w_up, v_mlp_w_up), mlp_w_down=(mlp_w_down, m_mlp_w_down, v_mlp_w_down))
    for part, (name, (w, m, v)) in zip(parts, mats.items()):
        upd[name] = update(name, w, part, m, v, slots=N_DEV)

    dlb_rows = jnp.concatenate(d_lb, axis=0)
    tail = jnp.concatenate(
        [jnp.concatenate(d_onorm, axis=1), jnp.concatenate(d_alog, axis=1), jnp.concatenate(d_dtb, axis=1)], axis=1)
    tail = jnp.pad(tail, ((0, 0), (0, d - tail.shape[1])))
    conv_rows = jnp.stack(d_conv).reshape(-1, d)
    packed = jnp.concatenate(
        [jnp.concatenate(d_nmix, axis=0), jnp.concatenate(d_nmlp, axis=0), d_nf, sq, dlb_rows,
         jnp.concatenate(d_gnorm, axis=0), tail, conv_rows], axis=0)
    pad_rows = (-packed.shape[0]) % 8
    packed = jnp.pad(packed, ((0, pad_rows), (0, 0)))
    tot = _all_reduce_small(packed, "reduce_small")
    r0 = 0
    g_nmix = tot[r0:r0 + DEPTH]; r0 += DEPTH
    g_nmlp = tot[r0:r0 + DEPTH]; r0 += DEPTH
    g_nf = tot[r0]; r0 += 1
    loss = tot[r0, 0]; r0 += 1
    g_lb = _lb_bwd(hgrn_lb_logits, tot[r0:r0 + DEPTH], "lb_bwd"); r0 += DEPTH
    g_gnorm_full = tot[r0:r0 + n_hgrn]; r0 += n_hgrn
    t_row = tot[r0]; r0 += 1
    g_conv_full = tot[r0:r0 + n_gdn * CONV_K * 3].reshape(n_gdn, CONV_K, 3 * d)
    g_onorm = t_row[0:n_gdn * HEAD_DIM].reshape(n_gdn, HEAD_DIM)
    o1 = n_gdn * HEAD_DIM
    g_alog = t_row[o1:o1 + n_gdn * N_HEADS].reshape(n_gdn, N_HEADS)
    g_dtb = t_row[o1 + n_gdn * N_HEADS:o1 + 2 * n_gdn * N_HEADS].reshape(n_gdn, N_HEADS)
    c_gn, c_cv = hgrn_gnorm.shape[1], gdn_conv.shape[2]
    g_gnorm = lax.dynamic_slice_in_dim(g_gnorm_full, me_i * c_gn, c_gn, axis=1)
    g_conv = lax.dynamic_slice_in_dim(g_conv_full, me_i * c_cv, c_cv, axis=2)

    upd["gdn_conv"] = update("gdn_conv", gdn_conv, g_conv, m_gdn_conv, v_gdn_conv)
    upd["gdn_a_log"] = update("gdn_a_log", gdn_a_log, g_alog, m_gdn_a_log, v_gdn_a_log)
    upd["gdn_dt_bias"] = update("gdn_dt_bias", gdn_dt_bias, g_dtb, m_gdn_dt_bias, v_gdn_dt_bias)
    upd["gdn_onorm"] = update("gdn_onorm", gdn_onorm, g_onorm, m_gdn_onorm, v_gdn_onorm)
    upd["hgrn_lb_logits"] = update("hgrn_lb_logits", hgrn_lb_logits, g_lb, m_hgrn_lb_logits, v_hgrn_lb_logits)
    upd["hgrn_gnorm"] = update("hgrn_gnorm", hgrn_gnorm, g_gnorm, m_hgrn_gnorm, v_hgrn_gnorm)
    upd["norm_mix"] = update("norm_mix", norm_mix, g_nmix, m_norm_mix, v_norm_mix)
    upd["norm_mlp"] = update("norm_mlp", norm_mlp, g_nmlp, m_norm_mlp, v_norm_mlp)
    upd["norm_final"] = update("norm_final", norm_final, g_nf, m_norm_final, v_norm_final)

    order = ["gdn_w_in", "gdn_conv", "gdn_a_log", "gdn_dt_bias", "gdn_onorm", "gdn_w_out", "hgrn_w_in",
             "hgrn_lb_logits", "hgrn_gnorm", "hgrn_w_out", "norm_mix", "norm_mlp", "mlp_w_up", "mlp_w_down",
             "norm_final"]
    outs = [loss, grad_x]
    for k in range(4):
        outs += [upd[name][k] for name in order]
    return tuple(outs)
```

```python
import functools

import jax
import jax.numpy as jnp
from jax import lax
from jax.experimental import pallas as pl
from jax.experimental.pallas import tpu as pltpu

F32 = jnp.float32
BF16 = jnp.bfloat16

D_MODEL = 1024
N_HEADS = 8
HEAD_DIM = 128
CHUNK = 64
SUB = 16
N_SUB = CHUNK // SUB
CONV_K = 4
HALO = 8
EPS = 1e-6
DEPTH = 4
N_DEV = 8
GDN_MAIN = 4 * D_MODEL
GDN_IN = GDN_MAIN + 2 * N_HEADS
AB_PAD = 128
HEAD_GROUP = 8

ADAM_LR = 0.001
ADAM_B1 = 0.9
ADAM_B2 = 0.999
ADAM_EPS = 1e-08
ADAM_WD = 0.01
ADAM_STEP = 10

VMEM_LIMIT = 56 * 1024 * 1024

_DIMS = {
    "nn": (((1,), (0,)), ((), ())),
    "nt": (((1,), (1,)), ((), ())),
    "tn": (((0,), (0,)), ((), ())),
}


def _parts(x, n):
    out = []
    r = x.astype(F32)
    for i in range(n):
        p = r.astype(BF16)
        out.append(p)
        if i + 1 < n:
            r = r - p.astype(F32)
    return out


def _dot_raw(a, b, mode, na, nb):
    ap, bp = _parts(a, na), _parts(b, nb)
    nmax = max(na, nb)
    acc = None
    for i, xa in enumerate(ap):
        for j, xb in enumerate(bp):
            if i + j < nmax:
                t = lax.dot_general(xa, xb, _DIMS[mode], preferred_element_type=F32)
                acc = t if acc is None else acc + t
    return acc


@functools.partial(jax.custom_vjp, nondiff_argnums=(2, 3, 4))
def _dot(a, b, mode, na, nb):
    return _dot_raw(a, b, mode, na, nb)


def _dot_fwd(a, b, mode, na, nb):
    return _dot_raw(a, b, mode, na, nb), (a, b)


def _dot_bwd(mode, na, nb, res, ct):
    a, b = res
    nc = max(na, nb)
    if mode == "nn":
        da = _dot_raw(ct, b, "nt", nc, nb)
        db = _dot_raw(a, ct, "tn", na, nc)
    elif mode == "nt":
        da = _dot_raw(ct, b, "nn", nc, nb)
        db = _dot_raw(ct, a, "tn", nc, na)
    else:
        da = _dot_raw(b, ct, "nt", nb, nc)
        db = _dot_raw(a, ct, "nn", na, nc)
    return da, db


_dot.defvjp(_dot_fwd, _dot_bwd)


def _iota2(shape, dim):
    return lax.broadcasted_iota(jnp.int32, shape, dim)


def _tril_f32(n):
    return (_iota2((n, n), 0) >= _iota2((n, n), 1)).astype(F32)


def _cumsum_rows(g):
    return _dot(_tril_f32(g.shape[0]), g, "nn", 1, 3)


def _inv_unit_lower(L):
    n = L.shape[0]
    eye = (_iota2((n, n), 0) == _iota2((n, n), 1)).astype(F32)
    p = -L
    t = eye + p
    k = 2
    while k < n:
        p = _dot_raw(p, p, "nn", 2, 2)
        t = t + _dot_raw(t, p, "nn", 2, 2)
        k *= 2
    return t


@jax.custom_vjp
def _solve_unit_lower(L, rhs):
    return _dot_raw(_inv_unit_lower(L), rhs, "nn", 2, 2)


def _solve_fwd(L, rhs):
    t = _inv_unit_lower(L)
    sol = _dot_raw(t, rhs, "nn", 2, 2)
    return sol, (t, sol)


def _solve_bwd(res, ct):
    t, sol = res
    y = _dot_raw(t, ct, "tn", 2, 2)
    return -_dot_raw(y, sol, "nt", 2, 2), y


_solve_unit_lower.defvjp(_solve_fwd, _solve_bwd)


def _softplus(x):
    return jnp.maximum(x, 0.0) + jnp.log1p(jnp.exp(-jnp.abs(x)))


def _rms(x, w):
    return x * lax.rsqrt(jnp.mean(x * x, axis=-1, keepdims=True) + EPS) * w


def _hg_pre(p, lb):
    qraw = p[:, 0:D_MODEL]
    f = p[:, D_MODEL:2 * D_MODEL]
    v = p[:, 2 * D_MODEL:3 * D_MODEL]
    g = jnp.log(lb + (1.0 - lb) * jax.nn.sigmoid(f))
    k = (1.0 - lb) * jax.nn.sigmoid(-f)
    q = jax.nn.silu(qraw) * (HEAD_DIM ** -0.5)
    return q, k, v, g, _cumsum_rows(g)


def _hg_head(st, q, k, v, g, gc):
    i3 = lax.broadcasted_iota(jnp.int32, (SUB, SUB, HEAD_DIM), 0)
    j3 = lax.broadcasted_iota(jnp.int32, (SUB, SUB, HEAD_DIM), 1)
    rows = []
    for s in range(N_SUB):
        lo = s * SUB
        qs, ks, vs, gs = q[lo:lo + SUB], k[lo:lo + SUB], v[lo:lo + SUB], gc[lo:lo + SUB]
        dec = jnp.exp(jnp.where(i3 >= j3, gs[:, None, :] - gs[None, :, :], -jnp.inf))
        a_diag = jnp.sum(qs[:, None, :] * ks[None, :, :] * dec, axis=-1)
        o_s = _dot(a_diag, vs, "nn", 1, 1)
        if s > 0:
            gb = gc[lo:lo + 1] - g[lo:lo + 1]
            q_off = qs * jnp.exp(gs - gb)
            k_off = k[0:lo] * jnp.exp(gb - gc[0:lo])
            a_off = _dot(q_off, k_off, "nt", 1, 1)
            o_s = o_s + _dot(a_off, v[0:lo], "nn", 1, 1)
        rows.append(o_s)
    o = jnp.concatenate(rows, axis=0) + _dot(q * jnp.exp(gc), st, "nt", 1, 1)
    g_last = gc[CHUNK - 1:CHUNK]
    st_new = st * jnp.exp(g_last) + _dot(v, k * jnp.exp(g_last - gc), "tn", 1, 1)
    return o, st_new


def _hg_post(o, gate, gw):
    return _rms(o, gw) * jax.nn.silu(gate)


def _gd_pre(xp, a, b, cw, alog, dtb):
    off = HALO - (CONV_K - 1)
    y = cw[0:1] * xp[off:off + CHUNK]
    for kk in range(1, CONV_K):
        y = y + cw[kk:kk + 1] * xp[off + kk:off + kk + CHUNK]
    c = jax.nn.silu(y)
    beta = jax.nn.sigmoid(b)
    g = -jnp.exp(alog) * _softplus(a + dtb)
    expand = (_iota2((N_HEADS, D_MODEL), 1) // HEAD_DIM == _iota2((N_HEADS, D_MODEL), 0)).astype(F32)
    return c, _dot(beta, expand, "nn", 3, 1), _dot(g, expand, "nn", 3, 1)


def _gd_head(st, q, k, v, beta, g, gate, onw):
    q = q * lax.rsqrt(jnp.sum(q * q, axis=-1, keepdims=True) + EPS) * (HEAD_DIM ** -0.5)
    k = k * lax.rsqrt(jnp.sum(k * k, axis=-1, keepdims=True) + EPS)
    gc = _cumsum_rows(g)
    ri = _iota2((CHUNK, CHUNK), 0)
    ci = _iota2((CHUNK, CHUNK), 1)
    diff = gc[:, 0:CHUNK] - gc.T[0:CHUNK, :]
    decay = jnp.exp(jnp.where(ri >= ci, diff, -jnp.inf))
    kb = k * beta
    egc = jnp.exp(gc)
    L = jnp.where(ri > ci, _dot(kb, k, "nt", 1, 1) * decay, 0.0)
    sol = _solve_unit_lower(L, jnp.concatenate([v * beta, kb * egc], axis=1))
    u = sol[:, 0:HEAD_DIM]
    w = sol[:, HEAD_DIM:2 * HEAD_DIM]
    a_qk = jnp.where(ri >= ci, _dot(q, k, "nt", 1, 1) * decay, 0.0)
    g_last = gc[CHUNK - 1:CHUNK]
    v_new = u - _dot(w, st, "nt", 1, 1)
    o = _dot(q * egc, st, "nt", 1, 1) + _dot(a_qk, v_new, "nn", 1, 1)
    st_new = st * jnp.exp(g_last) + _dot(v_new, k * jnp.exp(g_last - gc), "tn", 1, 1)
    return _rms(o, onw) * jax.nn.silu(gate), st_new


def _params(*sem):
    return pltpu.CompilerParams(dimension_semantics=sem, vmem_limit_bytes=VMEM_LIMIT)


def _tile(n, pref):
    t = min(n, pref)
    assert n % t == 0, (n, pref)
    return t


def _mm(a, b, mode, out_dtypes, name, epilogue=None, extras=(), after=None, tm=512, tn=512, tk=1024):
    if mode == "nn":
        (m, k), (k2, n) = a.shape, b.shape
    elif mode == "nt":
        (m, k), (n, k2) = a.shape, b.shape
    else:
        (k, m), (k2, n) = a.shape, b.shape
    assert k == k2, (a.shape, b.shape, mode)
    tm, tn, tk = _tile(m, tm), _tile(n, tn), _tile(k, tk)
    nk = k // tk
    ne, no, nafter = len(extras), len(out_dtypes), int(after is not None)
    if epilogue is None:
        epilogue = lambda acc: (acc,)

    def body(*refs):
        a_ref, b_ref = refs[0], refs[1]
        ex = refs[2:2 + ne]
        outs = refs[2 + ne + nafter:2 + ne + nafter + no]
        part = lax.dot_general(a_ref[...].astype(BF16), b_ref[...].astype(BF16), _DIMS[mode],
                               preferred_element_type=F32)

        def finish(acc):
            for o_ref, val in zip(outs, epilogue(acc, *[e[...] for e in ex])):
                o_ref[...] = val.astype(o_ref.dtype)

        if nk == 1:
            finish(part)
        else:
            acc_ref = refs[-1]
            kk = pl.program_id(2)

            @pl.when(kk == 0)
            def _():
                acc_ref[...] = part

            @pl.when(kk > 0)
            def _():
                acc_ref[...] += part

            @pl.when(kk == nk - 1)
            def _():
                finish(acc_ref[...])

    if mode == "tn":
        a_spec = pl.BlockSpec((tk, tm), lambda i, j, kk: (kk, i))
    else:
        a_spec = pl.BlockSpec((tm, tk), lambda i, j, kk: (i, kk))
    if mode == "nt":
        b_spec = pl.BlockSpec((tn, tk), lambda i, j, kk: (j, kk))
    else:
        b_spec = pl.BlockSpec((tk, tn), lambda i, j, kk: (kk, j))
    o_spec = pl.BlockSpec((tm, tn), lambda i, j, kk: (i, j))
    res = pl.pallas_call(
        body,
        name=name,
        grid=(m // tm, n // tn, nk),
        in_specs=[a_spec, b_spec] + [o_spec] * ne + [pl.BlockSpec(memory_space=pl.ANY)] * nafter,
        out_specs=[o_spec] * no,
        out_shape=[jax.ShapeDtypeStruct((m, n), dt) for dt in out_dtypes],
        scratch_shapes=[pltpu.VMEM((tm, tn), F32)] if nk > 1 else [],
        compiler_params=_params("parallel", "parallel", "arbitrary"),
    )(a, b, *extras, *([after] if nafter else []))
    return res[0] if no == 1 else res


def _rms_fwd(x, w, name, tm=512):
    n, d = x.shape
    tm = _tile(n, tm)

    def body(x_ref, w_ref, y_ref):
        y_ref[...] = _rms(x_ref[...], w_ref[...]).astype(y_ref.dtype)

    return pl.pallas_call(
        body, name=name, grid=(n // tm,),
        in_specs=[pl.BlockSpec((tm, d), lambda i: (i, 0)), pl.BlockSpec((1, d), lambda i: (0, 0))],
        out_specs=pl.BlockSpec((tm, d), lambda i: (i, 0)),
        out_shape=jax.ShapeDtypeStruct((n, d), BF16),
        compiler_params=_params("arbitrary"),
    )(x, w)


def _rms_bwd(x, w, dy, dres, name, tm=512):
    n, d = x.shape
    tm = _tile(n, tm)

    def body(x_ref, w_ref, dy_ref, dres_ref, dx_ref, dw_ref):
        _, vjp = jax.vjp(_rms, x_ref[...], w_ref[...])
        dx, dw = vjp(dy_ref[...].astype(F32))
        dx_ref[...] = dres_ref[...] + dx

        @pl.when(pl.program_id(0) == 0)
        def _():
            dw_ref[...] = dw

        @pl.when(pl.program_id(0) > 0)
        def _():
            dw_ref[...] += dw

    row = pl.BlockSpec((tm, d), lambda i: (i, 0))
    vec = pl.BlockSpec((1, d), lambda i: (0, 0))
    return pl.pallas_call(
        body, name=name, grid=(n // tm,),
        in_specs=[row, vec, row, row],
        out_specs=[row, vec],
        out_shape=[jax.ShapeDtypeStruct((n, d), F32), jax.ShapeDtypeStruct((1, d), F32)],
        compiler_params=_params("arbitrary"),
    )(x, w, dy, dres)


def _loss_head(h, w, target, name, tm=512):
    n, d = h.shape
    tm = _tile(n, tm)

    def body(h_ref, w_ref, t_ref, dh_ref, dw_ref, sq_ref):
        y, vjp = jax.vjp(_rms, h_ref[...], w_ref[...])
        err = y - t_ref[...]
        dh, dw = vjp(err * (1.0 / d))
        dh_ref[...] = dh
        sq = jnp.sum(err * err, axis=0, keepdims=True)

        @pl.when(pl.program_id(0) == 0)
        def _():
            dw_ref[...] = dw
            sq_ref[...] = sq

        @pl.when(pl.program_id(0) > 0)
        def _():
            dw_ref[...] += dw
            sq_ref[...] += sq

        @pl.when(pl.program_id(0) == n // tm - 1)
        def _():
            total = jnp.sum(sq_ref[...], axis=1, keepdims=True) * (0.5 / d)
            sq_ref[...] = jnp.broadcast_to(total, sq_ref.shape)

    row = pl.BlockSpec((tm, d), lambda i: (i, 0))
    vec = pl.BlockSpec((1, d), lambda i: (0, 0))
    return pl.pallas_call(
        body, name=name, grid=(n // tm,),
        in_specs=[row, vec, row],
        out_specs=[row, vec, vec],
        out_shape=[jax.ShapeDtypeStruct((n, d), F32), jax.ShapeDtypeStruct((1, d), F32),
                   jax.ShapeDtypeStruct((1, d), F32)],
        compiler_params=_params("arbitrary"),
    )(h, w, target)


def _lower_bounds(logits):
    sm = jax.nn.softmax(logits, axis=0)
    rows = [sm[0:1] * 0.0]
    for r in range(1, DEPTH):
        rows.append(rows[-1] + sm[r:r + 1])
    return jnp.concatenate(rows, axis=0)


def _lb_fwd(logits, name):
    def body(l_ref, o_ref):
        o_ref[...] = _lower_bounds(l_ref[...])

    return pl.pallas_call(body, name=name, out_shape=jax.ShapeDtypeStruct(logits.shape, F32))(logits)


def _lb_bwd(logits, dlb, name):
    def body(l_ref, d_ref, o_ref):
        _, vjp = jax.vjp(_lower_bounds, l_ref[...])
        (o_ref[...],) = vjp(d_ref[...])

    return pl.pallas_call(body, name=name, out_shape=jax.ShapeDtypeStruct(logits.shape, F32))(logits, dlb)


def _head_slice(h):
    if isinstance(h, int):
        return pl.ds(h * HEAD_DIM, HEAD_DIM)
    return pl.ds(pl.multiple_of(h * HEAD_DIM, HEAD_DIM), HEAD_DIM)


def _head_groups(group_body):
    if HEAD_GROUP == N_HEADS:
        group_body(list(range(N_HEADS)))
        return

    def trip(i, carry):
        group_body([i * HEAD_GROUP + t for t in range(HEAD_GROUP)])
        return carry

    lax.fori_loop(0, N_HEADS // HEAD_GROUP, trip, 0)


def _stack_heads(ref, hs, first=0):
    return jnp.stack([ref[:, _head_slice(h + first)] for h in hs])


def _unstack_heads(ref, hs, val, first=0):
    for t, h in enumerate(hs):
        ref[:, _head_slice(h + first)] = val[t].astype(ref.dtype)


_GD_HEADS = jax.vmap(_gd_head, in_axes=(0, 0, 0, 0, 0, 0, 0, None))


def _hgrn_fwd(proj, lb, gw, seqs, name):
    n = proj.shape[0]
    nc = n // seqs // CHUNK
    d = D_MODEL

    def body(p_ref, lb_ref, gw_ref, o2_ref, o_ref, st_all_ref, st_sc, q_sc, k_sc, v_sc, g_sc, gc_sc):
        @pl.when(pl.program_id(1) == 0)
        def _():
            st_sc[...] = jnp.zeros_like(st_sc)

        q_sc[...], k_sc[...], v_sc[...], g_sc[...], gc_sc[...] = _hg_pre(p_ref[:, 0:3 * d], lb_ref[...])
        st_all_ref[0] = st_sc[...]

        def group(hs):
            sts = pl.ds(hs[0], len(hs))
            o, st_new = jax.vmap(_hg_head)(st_sc[sts], *[_stack_heads(r, hs) for r in (q_sc, k_sc, v_sc, g_sc, gc_sc)])
            _unstack_heads(o_ref, hs, o)
            st_sc[sts] = st_new

        _head_groups(group)
        o2_ref[...] = _hg_post(o_ref[...], p_ref[:, 3 * d:4 * d], gw_ref[...]).astype(o2_ref.dtype)

    idx = lambda b, c: (b * nc + c, 0)
    vec = pl.BlockSpec((1, d), lambda b, c: (0, 0))
    act = pl.BlockSpec((CHUNK, d), idx)
    return pl.pallas_call(
        body, name=name, grid=(seqs, nc),
        in_specs=[pl.BlockSpec((CHUNK, 4 * d), idx), vec, vec],
        out_specs=[act, act, pl.BlockSpec((1, N_HEADS, HEAD_DIM, HEAD_DIM), lambda b, c: (b * nc + c, 0, 0, 0))],
        out_shape=[jax.ShapeDtypeStruct((n, d), BF16), jax.ShapeDtypeStruct((n, d), F32),
                   jax.ShapeDtypeStruct((n // CHUNK, N_HEADS, HEAD_DIM, HEAD_DIM), F32)],
        scratch_shapes=[pltpu.VMEM((N_HEADS, HEAD_DIM, HEAD_DIM), F32)] + [pltpu.VMEM((CHUNK, d), F32)] * 5,
        compiler_params=_params("arbitrary", "arbitrary"),
    )(proj, lb, gw)


def _hgrn_bwd(proj, lb, gw, st_all, o, do2, seqs, name):
    n = proj.shape[0]
    nc = n // seqs // CHUNK
    d = D_MODEL

    def body(p_ref, lb_ref, gw_ref, st_all_ref, o_ref, do2_ref, dp_ref, dlb_ref, dgw_ref,
             dst_sc, q_sc, k_sc, v_sc, g_sc, gc_sc, do_sc, dq_sc, dk_sc, dv_sc, dg_sc, dgc_sc):
        first = (pl.program_id(0) == 0) & (pl.program_id(1) == 0)

        @pl.when(pl.program_id(1) == 0)
        def _():
            dst_sc[...] = jnp.zeros_like(dst_sc)

        pre_out, pre_vjp = jax.vjp(_hg_pre, p_ref[:, 0:3 * d], lb_ref[...])
        q_sc[...], k_sc[...], v_sc[...], g_sc[...], gc_sc[...] = pre_out
        _, post_vjp = jax.vjp(_hg_post, o_ref[...], p_ref[:, 3 * d:4 * d], gw_ref[...])
        do_sc[...], dgate, dgw = post_vjp(do2_ref[...].astype(F32))
        dp_ref[:, 3 * d:4 * d] = dgate.astype(dp_ref.dtype)

        def group(hs):
            sts = pl.ds(hs[0], len(hs))
            _, vjp = jax.vjp(jax.vmap(_hg_head), st_all_ref[0, sts],
                             *[_stack_heads(r, hs) for r in (q_sc, k_sc, v_sc, g_sc, gc_sc)])
            grads = vjp((_stack_heads(do_sc, hs), dst_sc[sts]))
            dst_sc[sts] = grads[0]
            for r, val in zip((dq_sc, dk_sc, dv_sc, dg_sc, dgc_sc), grads[1:]):
                _unstack_heads(r, hs, val)

        _head_groups(group)
        dp, dlb = pre_vjp((dq_sc[...], dk_sc[...], dv_sc[...], dg_sc[...], dgc_sc[...]))
        dp_ref[:, 0:3 * d] = dp.astype(dp_ref.dtype)

        @pl.when(first)
        def _():
            dlb_ref[...] = dlb
            dgw_ref[...] = dgw

        @pl.when(jnp.logical_not(first))
        def _():
            dlb_ref[...] += dlb
            dgw_ref[...] += dgw

    idx = lambda b, c: (b * nc + nc - 1 - c, 0)
    vec = pl.BlockSpec((1, d), lambda b, c: (0, 0))
    act = pl.BlockSpec((CHUNK, d), idx)
    wide = pl.BlockSpec((CHUNK, 4 * d), idx)
    return pl.pallas_call(
        body, name=name, grid=(seqs, nc),
        in_specs=[wide, vec, vec,
                  pl.BlockSpec((1, N_HEADS, HEAD_DIM, HEAD_DIM), lambda b, c: (b * nc + nc - 1 - c, 0, 0, 0)),
                  act, act],
        out_specs=[wide, vec, vec],
        out_shape=[jax.ShapeDtypeStruct((n, 4 * d), BF16), jax.ShapeDtypeStruct((1, d), F32),
                   jax.ShapeDtypeStruct((1, d), F32)],
        scratch_shapes=[pltpu.VMEM((N_HEADS, HEAD_DIM, HEAD_DIM), F32)] + [pltpu.VMEM((CHUNK, d), F32)] * 11,
        compiler_params=_params("arbitrary", "arbitrary"),
    )(proj, lb, gw, st_all, o, do2)


def _gd_xp(halo_ref, p_ref, first_chunk):
    halo = jnp.where(first_chunk, 0.0, halo_ref[...])
    return jnp.concatenate([halo, p_ref[:, 0:3 * D_MODEL]], axis=0)


def _gdn_fwd(projm, projab, cw, alog, dtb, onw, seqs, name):
    n = projm.shape[0]
    nc = n // seqs // CHUNK
    d = D_MODEL
    per_halo = CHUNK // HALO

    def body(p_ref, halo_ref, ab_ref, cw_ref, alog_ref, dtb_ref, onw_ref, o2_ref, st_all_ref,
             st_sc, c_sc, beta_sc, g_sc):
        @pl.when(pl.program_id(1) == 0)
        def _():
            st_sc[...] = jnp.zeros_like(st_sc)

        xp = _gd_xp(halo_ref, p_ref, pl.program_id(1) == 0)
        c_sc[...], beta_sc[...], g_sc[...] = _gd_pre(
            xp, ab_ref[:, 0:N_HEADS], ab_ref[:, N_HEADS:2 * N_HEADS], cw_ref[...], alog_ref[...], dtb_ref[...])
        st_all_ref[0] = st_sc[...]

        def group(hs):
            sts = pl.ds(hs[0], len(hs))
            o2, st_new = _GD_HEADS(
                st_sc[sts], _stack_heads(c_sc, hs), _stack_heads(c_sc, hs, N_HEADS), _stack_heads(c_sc, hs, 2 * N_HEADS),
                _stack_heads(beta_sc, hs), _stack_heads(g_sc, hs), _stack_heads(p_ref, hs, 3 * N_HEADS), onw_ref[...])
            _unstack_heads(o2_ref, hs, o2)
            st_sc[sts] = st_new

        _head_groups(group)

    idx = lambda b, c: (b * nc + c, 0)
    const = lambda b, c: (0, 0)
    return pl.pallas_call(
        body, name=name, grid=(seqs, nc),
        in_specs=[pl.BlockSpec((CHUNK, 4 * d), idx),
                  pl.BlockSpec((HALO, 3 * d), lambda b, c: (jnp.maximum((b * nc + c) * per_halo - 1, 0), 0)),
                  pl.BlockSpec((CHUNK, AB_PAD), idx),
                  pl.BlockSpec((CONV_K, 3 * d), const), pl.BlockSpec((1, N_HEADS), const),
                  pl.BlockSpec((1, N_HEADS), const), pl.BlockSpec((1, HEAD_DIM), const)],
        out_specs=[pl.BlockSpec((CHUNK, d), idx),
                   pl.BlockSpec((1, N_HEADS, HEAD_DIM, HEAD_DIM), lambda b, c: (b * nc + c, 0, 0, 0))],
        out_shape=[jax.ShapeDtypeStruct((n, d), BF16),
                   jax.ShapeDtypeStruct((n // CHUNK, N_HEADS, HEAD_DIM, HEAD_DIM), F32)],
        scratch_shapes=[pltpu.VMEM((N_HEADS, HEAD_DIM, HEAD_DIM), F32), pltpu.VMEM((CHUNK, 3 * d), F32),
                        pltpu.VMEM((CHUNK, d), F32), pltpu.VMEM((CHUNK, d), F32)],
        compiler_params=_params("arbitrary", "arbitrary"),
    )(projm, projm, projab, cw, alog, dtb, onw)


def _gdn_bwd(projm, projab, cw, alog, dtb, onw, st_all, do2, seqs, name):
    n = projm.shape[0]
    nc = n // seqs // CHUNK
    d = D_MODEL
    per_halo = CHUNK // HALO

    def body(p_ref, halo_ref, ab_ref, cw_ref, alog_ref, dtb_ref, onw_ref, st_all_ref, do2_ref,
             dp_ref, dab_ref, dcw_ref, dalog_ref, ddtb_ref, donw_ref,
             dst_sc, dhalo_sc, c_sc, beta_sc, g_sc, dc_sc, dbeta_sc, dg_sc, donw_sc):
        step = pl.program_id(1)
        first = (pl.program_id(0) == 0) & (step == 0)

        @pl.when(step == 0)
        def _():
            dst_sc[...] = jnp.zeros_like(dst_sc)
            dhalo_sc[...] = jnp.zeros_like(dhalo_sc)

        donw_sc[...] = jnp.zeros_like(donw_sc)
        xp = _gd_xp(halo_ref, p_ref, step == nc - 1)
        pre_out, pre_vjp = jax.vjp(_gd_pre, xp, ab_ref[:, 0:N_HEADS], ab_ref[:, N_HEADS:2 * N_HEADS],
                                   cw_ref[...], alog_ref[...], dtb_ref[...])
        c_sc[...], beta_sc[...], g_sc[...] = pre_out

        def group(hs):
            sts = pl.ds(hs[0], len(hs))
            _, vjp = jax.vjp(
                _GD_HEADS, st_all_ref[0, sts], _stack_heads(c_sc, hs), _stack_heads(c_sc, hs, N_HEADS),
                _stack_heads(c_sc, hs, 2 * N_HEADS), _stack_heads(beta_sc, hs), _stack_heads(g_sc, hs),
                _stack_heads(p_ref, hs, 3 * N_HEADS), onw_ref[...])
            dst, dq, dk, dv, dbeta, dg, dgate, donw = vjp((_stack_heads(do2_ref, hs).astype(F32), dst_sc[sts]))
            dst_sc[sts] = dst
            _unstack_heads(dc_sc, hs, dq)
            _unstack_heads(dc_sc, hs, dk, N_HEADS)
            _unstack_heads(dc_sc, hs, dv, 2 * N_HEADS)
            _unstack_heads(dbeta_sc, hs, dbeta)
            _unstack_heads(dg_sc, hs, dg)
            _unstack_heads(dp_ref, hs, dgate, 3 * N_HEADS)
            donw_sc[...] += donw

        _head_groups(group)
        dxp, da, db, dcw, dalog, ddtb = pre_vjp((dc_sc[...], dbeta_sc[...], dg_sc[...]))
        dqkv = jnp.concatenate([dxp[HALO:CHUNK], dxp[CHUNK:HALO + CHUNK] + dhalo_sc[...]], axis=0)
        dp_ref[:, 0:3 * d] = dqkv.astype(dp_ref.dtype)
        dhalo_sc[...] = dxp[0:HALO]
        dab_ref[...] = jnp.concatenate(
            [da, db, jnp.zeros((CHUNK, AB_PAD - 2 * N_HEADS), F32)], axis=1).astype(dab_ref.dtype)

        @pl.when(first)
        def _():
            dcw_ref[...] = dcw
            dalog_ref[...] = dalog
            ddtb_ref[...] = ddtb
            donw_ref[...] = donw_sc[...]

        @pl.when(jnp.logical_not(first))
        def _():
            dcw_ref[...] += dcw
            dalog_ref[...] += dalog
            ddtb_ref[...] += ddtb
            donw_ref[...] += donw_sc[...]

    rev = lambda b, c: b * nc + nc - 1 - c
    idx = lambda b, c: (rev(b, c), 0)
    const = lambda b, c: (0, 0)
    small = [pl.BlockSpec((CONV_K, 3 * d), const), pl.BlockSpec((1, N_HEADS), const),
             pl.BlockSpec((1, N_HEADS), const), pl.BlockSpec((1, HEAD_DIM), const)]
    return pl.pallas_call(
        body, name=name, grid=(seqs, nc),
        in_specs=[pl.BlockSpec((CHUNK, 4 * d), idx),
                  pl.BlockSpec((HALO, 3 * d), lambda b, c: (jnp.maximum(rev(b, c) * per_halo - 1, 0), 0)),
                  pl.BlockSpec((CHUNK, AB_PAD), idx)] + small + [
                  pl.BlockSpec((1, N_HEADS, HEAD_DIM, HEAD_DIM), lambda b, c: (rev(b, c), 0, 0, 0)),
                  pl.BlockSpec((CHUNK, d), idx)],
        out_specs=[pl.BlockSpec((CHUNK, 4 * d), idx), pl.BlockSpec((CHUNK, AB_PAD), idx)] + small,
        out_shape=[jax.ShapeDtypeStruct((n, 4 * d), BF16), jax.ShapeDtypeStruct((n, AB_PAD), BF16),
                   jax.ShapeDtypeStruct((CONV_K, 3 * d), F32), jax.ShapeDtypeStruct((1, N_HEADS), F32),
                   jax.ShapeDtypeStruct((1, N_HEADS), F32), jax.ShapeDtypeStruct((1, HEAD_DIM), F32)],
        scratch_shapes=[pltpu.VMEM((N_HEADS, HEAD_DIM, HEAD_DIM), F32), pltpu.VMEM((HALO, 3 * d), F32),
                        pltpu.VMEM((CHUNK, 3 * d), F32), pltpu.VMEM((CHUNK, d), F32), pltpu.VMEM((CHUNK, d), F32),
                        pltpu.VMEM((CHUNK, 3 * d), F32), pltpu.VMEM((CHUNK, d), F32), pltpu.VMEM((CHUNK, d), F32),
                        pltpu.VMEM((1, HEAD_DIM), F32)],
        compiler_params=_params("arbitrary", "arbitrary"),
    )(projm, projm, projab, cw, alog, dtb, onw, st_all, do2)


def _adam_update(w, g, m, v):
    b1c = 1.0 - ADAM_B1 ** ADAM_STEP
    b2c = 1.0 - ADAM_B2 ** ADAM_STEP
    m_new = ADAM_B1 * m + (1.0 - ADAM_B1) * g
    v_new = ADAM_B2 * v + (1.0 - ADAM_B2) * (g * g)
    delta = -ADAM_LR * ((m_new / b1c) / (jnp.sqrt(v_new / b2c) + ADAM_EPS) + ADAM_WD * w)
    return delta, m_new, v_new


def _adamw(w, g, m, v, name, tr=256):
    r, c = w.shape
    tr = _tile(r, tr)

    def body(w_ref, g_ref, m_ref, v_ref, d_ref, mo_ref, vo_ref):
        d_ref[...], mo_ref[...], vo_ref[...] = _adam_update(w_ref[...], g_ref[...], m_ref[...], v_ref[...])

    blk = pl.BlockSpec((tr, c), lambda i: (i, 0))
    return pl.pallas_call(
        body, name=name, grid=(r // tr,),
        in_specs=[blk] * 4, out_specs=[blk] * 3,
        out_shape=[jax.ShapeDtypeStruct((r, c), F32)] * 3,
        compiler_params=_params("arbitrary"),
    )(w, g, m, v)


def _adamw_slots(w, slot_bufs, m, v, name, tr=256):
    nl, r, c = w.shape
    tr = _tile(r, tr)

    def body(*refs):
        w_ref = refs[0]
        g_refs = refs[1:1 + nl]
        m_ref, v_ref, go_ref, d_ref, mo_ref, vo_ref = refs[1 + nl:]
        for k in range(nl):
            @pl.when(pl.program_id(0) == k)
            def _(k=k):
                g = g_refs[k][0].astype(F32)
                for s in range(1, N_DEV):
                    g = g + g_refs[k][s].astype(F32)
                go_ref[0] = g

        d_ref[0], mo_ref[0], vo_ref[0] = _adam_update(w_ref[0], go_ref[0], m_ref[0], v_ref[0])

    blk = pl.BlockSpec((1, tr, c), lambda l, i: (l, i, 0))
    g_specs = [pl.BlockSpec((N_DEV, tr, c), lambda l, i, k=k: (0, jnp.where(l == k, i, 0), 0)) for k in range(nl)]
    return pl.pallas_call(
        body, name=name, grid=(nl, r // tr),
        in_specs=[blk] + g_specs + [blk, blk], out_specs=[blk] * 4,
        out_shape=[jax.ShapeDtypeStruct((nl, r, c), F32)] * 4,
        compiler_params=_params("arbitrary", "arbitrary"),
    )(w, *slot_bufs, m, v)


def _mesh_pos():
    return lax.axis_index("x"), lax.axis_index("y"), lax.axis_index("c")


def _flip(pos, p):
    x, y, c = pos
    return ((1 - x) if p & 4 else x, (1 - y) if p & 2 else y, (1 - c) if p & 1 else c)


def _lin(pos):
    return 4 * pos[0] + 2 * pos[1] + pos[2]


_HBM = pl.BlockSpec(memory_space=pltpu.HBM)
_SEM = pl.BlockSpec(memory_space=pltpu.SEMAPHORE)
_DATAFLOW = pltpu.SideEffectType.DATAFLOW_SIDE_EFFECTING


class _Item:
    def __init__(self, src, land_shape, src_pick, dst_pick):
        self.src, self.land_shape, self.src_pick, self.dst_pick = src, land_shape, src_pick, dst_pick


def _remote_copies(items, src, land, send_sem, recv_sem, me, arriving):
    me_i = _lin(me)
    out = []
    for it, s_ref, l_ref in zip(items, src, land):
        for p in range(1, N_DEV):
            peer = _flip(me, p)
            out.append(pltpu.make_async_remote_copy(
                src_ref=it.src_pick(s_ref, _lin(peer)),
                dst_ref=it.dst_pick(l_ref, _lin(peer) if arriving else me_i),
                send_sem=send_sem, recv_sem=recv_sem, device_id=peer, device_id_type=pl.DeviceIdType.MESH))
    return out


def _exchange_start(groups, name):
    items = [it for g in groups for it in g]
    n, ng = len(items), len(groups)
    first = [sum(len(g) for g in groups[:gi]) for gi in range(ng)]

    def body(*refs):
        src, land = refs[0:n], refs[n:2 * n]
        send_sems, recv_sems = refs[2 * n:2 * n + ng], refs[2 * n + ng:2 * n + 2 * ng]
        token, loc_sems = refs[4 * n + 2 * ng], refs[4 * n + 2 * ng + 1]
        me = _mesh_pos()
        own = [pltpu.make_async_copy(it.src_pick(src[a], _lin(me)), it.dst_pick(land[a], _lin(me)), loc_sems.at[a])
               for a, it in enumerate(items)]
        for cp in own:
            cp.start()
        for gi, g in enumerate(groups):
            sl = slice(first[gi], first[gi] + len(g))
            for cp in _remote_copies(g, src[sl], land[sl], send_sems[gi], recv_sems[gi], me, arriving=False):
                cp.start()
        for cp in own:
            cp.wait()
        token[...] = jnp.zeros_like(token)

    srcs = [pltpu.with_memory_space_constraint(it.src, pltpu.HBM) for it in items]
    lands = [pltpu.with_memory_space_constraint(lax.empty(it.land_shape, it.src.dtype), pltpu.HBM) for it in items]
    res = pl.pallas_call(
        body, name=name,
        out_shape=([pltpu.SemaphoreType.DMA(())] * (2 * ng)
                   + [pltpu.HBM(it.src.shape, it.src.dtype) for it in items]
                   + [pltpu.HBM(it.land_shape, it.src.dtype) for it in items]
                   + [jax.ShapeDtypeStruct((8, 128), F32)]),
        in_specs=[_HBM] * (2 * n),
        out_specs=[_SEM] * (2 * ng) + [_HBM] * (2 * n) + [pl.BlockSpec(memory_space=pltpu.VMEM)],
        input_output_aliases={i: 2 * ng + i for i in range(2 * n)},
        scratch_shapes=[pltpu.SemaphoreType.DMA((n,))],
        compiler_params=pltpu.CompilerParams(has_side_effects=_DATAFLOW),
    )(*srcs, *lands)
    send_sems, recv_sems = res[0:ng], res[ng:2 * ng]
    src_thru, land_thru = res[2 * ng:2 * ng + n], res[2 * ng + n:2 * ng + 2 * n]
    handles = []
    for gi, g in enumerate(groups):
        sl = slice(first[gi], first[gi] + len(g))
        handles.append((g, src_thru[sl], land_thru[sl], send_sems[gi], recv_sems[gi]))
    return handles, res[-1]


def _exchange_wait(handle, after, name):
    items, src_thru, land_thru, send_sem, recv_sem = handle
    k = len(items)

    def body(*refs):
        src, land = refs[0:k], refs[k:2 * k]
        send_ref, recv_ref = refs[2 * k], refs[2 * k + 1]
        for cp in _remote_copies(items, src, land, send_ref, recv_ref, _mesh_pos(), arriving=True):
            cp.wait_send()
            cp.wait_recv()

    res = pl.pallas_call(
        body, name=name,
        out_shape=([pltpu.HBM(s.shape, s.dtype) for s in src_thru] + [pltpu.HBM(l.shape, l.dtype) for l in land_thru]),
        in_specs=[_HBM] * (2 * k) + [_SEM, _SEM, pl.BlockSpec(memory_space=pl.ANY)],
        out_specs=[_HBM] * (2 * k),
        input_output_aliases={i: i for i in range(2 * k)},
        compiler_params=pltpu.CompilerParams(has_side_effects=_DATAFLOW),
    )(*src_thru, *land_thru, send_sem, recv_sem, after)
    return res[k:2 * k]


def _whole(ref, i):
    return ref


def _slot(ref, i):
    return ref.at[i]


def _rows_of(r):
    return lambda ref, i: ref.at[pl.ds(pl.multiple_of(i * r, r), r), :]


def _cols_of(c):
    return lambda ref, i: ref.at[:, pl.ds(pl.multiple_of(i * c, c), c)]


def _all_reduce_small(buf, name):
    r, c = buf.shape

    def body(src_ref, out_ref, all_ref, send_sems, recv_sems):
        me = _mesh_pos()
        me_i = _lin(me)
        all_ref[me_i] = src_ref[...]
        for p in range(1, N_DEV):
            peer = _flip(me, p)
            pltpu.make_async_remote_copy(
                src_ref=src_ref, dst_ref=all_ref.at[me_i], send_sem=send_sems.at[p - 1], recv_sem=recv_sems.at[p - 1],
                device_id=peer, device_id_type=pl.DeviceIdType.MESH).start()
        for p in range(1, N_DEV):
            peer = _flip(me, p)
            cp = pltpu.make_async_remote_copy(
                src_ref=src_ref, dst_ref=all_ref.at[_lin(peer)], send_sem=send_sems.at[p - 1],
                recv_sem=recv_sems.at[p - 1], device_id=peer, device_id_type=pl.DeviceIdType.MESH)
            cp.wait_recv()
            cp.wait_send()
        acc = all_ref[0]
        for s in range(1, N_DEV):
            acc = acc + all_ref[s]
        out_ref[...] = acc

    vm = pl.BlockSpec(memory_space=pltpu.VMEM)
    return pl.pallas_call(
        body, name=name, in_specs=[vm], out_specs=vm,
        out_shape=jax.ShapeDtypeStruct((r, c), F32),
        scratch_shapes=[pltpu.VMEM((N_DEV, r, c), F32), pltpu.SemaphoreType.DMA((N_DEV - 1,)),
                        pltpu.SemaphoreType.DMA((N_DEV - 1,))],
        compiler_params=pltpu.CompilerParams(has_side_effects=True),
    )(buf)


def _unshard_cols(g):
    s, l, r, c = g.shape
    return jnp.transpose(g, (1, 2, 0, 3)).reshape(l, r, s * c)


def kernel(x, gdn_w_in, gdn_conv, gdn_a_log, gdn_dt_bias, gdn_onorm, gdn_w_out, hgrn_w_in, hgrn_lb_logits, hgrn_gnorm, hgrn_w_out, norm_mix, norm_mlp, mlp_w_up, mlp_w_down, norm_final, loss_target, m_gdn_w_in, m_gdn_conv, m_gdn_a_log, m_gdn_dt_bias, m_gdn_onorm, m_gdn_w_out, m_hgrn_w_in, m_hgrn_lb_logits, m_hgrn_gnorm, m_hgrn_w_out, m_norm_mix, m_norm_mlp, m_mlp_w_up, m_mlp_w_down, m_norm_final, v_gdn_w_in, v_gdn_conv, v_gdn_a_log, v_gdn_dt_bias, v_gdn_onorm, v_gdn_w_out, v_hgrn_w_in, v_hgrn_lb_logits, v_hgrn_gnorm, v_hgrn_w_out, v_norm_mix, v_norm_mlp, v_mlp_w_up, v_mlp_w_down, v_norm_final):
    seqs, seq_len, d = x.shape
    n = seqs * seq_len
    me_i = _lin(_mesh_pos())
    x2 = x.reshape(n, d)
    target = loss_target.reshape(n, d)
    n_gdn, n_hgrn = gdn_w_in.shape[0], hgrn_w_in.shape[0]

    r_out, r_down = gdn_w_out.shape[1], mlp_w_down.shape[1]
    c_gin, c_hin, c_up = gdn_w_in.shape[2], hgrn_w_in.shape[2], mlp_w_up.shape[2]

    def gathered(w, pick, land_shape):
        return _Item(w.astype(BF16), land_shape, _whole, pick)

    groups = []
    for i in range(DEPTH):
        j = i // 2
        if i % 2 == 0:
            mixer = [gathered(gdn_w_in[j], _slot, (N_DEV, d, c_gin)),
                     gathered(gdn_w_out[j], _rows_of(r_out), (N_DEV * r_out, d))]
        else:
            mixer = [gathered(hgrn_w_in[j], _cols_of(c_hin), (d, N_DEV * c_hin)),
                     gathered(hgrn_w_out[j], _rows_of(r_out), (N_DEV * r_out, d))]
        groups.append(mixer + [gathered(mlp_w_up[i], _cols_of(c_up), (d, N_DEV * c_up)),
                               gathered(mlp_w_down[i], _rows_of(r_down), (N_DEV * r_down, d))])
    groups[0] = [_Item(gdn_conv, (N_DEV,) + gdn_conv.shape, _whole, _slot),
                 _Item(hgrn_gnorm, (N_DEV,) + hgrn_gnorm.shape, _whole, _slot)] + groups[0]
    gather_handles, token = _exchange_start(groups, "gather_start")
    lbs = _lb_fwd(hgrn_lb_logits + token[0:1, 0:1], "lb_fwd")

    saved = []
    w_in, w_ab, w_out, w_up, w_down = ([None] * DEPTH for _ in range(5))
    h = x2
    for i in range(DEPTH):
        j = i // 2
        lands = _exchange_wait(gather_handles[i], h, f"gather_wait_{i}")
        if i == 0:
            conv_full = _unshard_cols(lands[0])
            gnorm_full = jnp.transpose(lands[1], (1, 0, 2)).reshape(n_hgrn, d)
            lands = lands[2:]
        w_in[i], w_out[i], w_up[i], w_down[i] = lands
        y = _rms_fwd(h, norm_mix[i:i + 1], f"rms_mix_{i}")
        if i % 2 == 0:
            w_gin = jnp.transpose(w_in[i], (1, 0, 2)).reshape(d, N_DEV * c_gin)
            w_in[i] = w_gin[:, :GDN_MAIN]
            w_ab[i] = jnp.pad(w_gin[:, GDN_MAIN:], ((0, 0), (0, AB_PAD - 2 * N_HEADS)))
            projm = _mm(y, w_in[i], "nn", [F32], f"gdn_proj_{i}")
            projab = _mm(y, w_ab[i], "nn", [F32], f"gdn_proj_ab_{i}")
            o2, st_all = _gdn_fwd(projm, projab, conv_full[j], gdn_a_log[j:j + 1], gdn_dt_bias[j:j + 1],
                                  gdn_onorm[j:j + 1], seqs, f"gdn_fwd_{i}")
            mix = (projm, projab, st_all)
        else:
            proj = _mm(y, w_in[i], "nn", [F32], f"hgrn_proj_{i}")
            o2, o_raw, st_all = _hgrn_fwd(proj, lbs[i:i + 1], gnorm_full[j:j + 1], seqs, f"hgrn_fwd_{i}")
            mix = (proj, o_raw, st_all)
        h1 = _mm(o2, w_out[i], "nn", [F32], f"mix_out_{i}", epilogue=lambda acc, res: (res + acc,), extras=(h,))
        y2 = _rms_fwd(h1, norm_mlp[i:i + 1], f"rms_mlp_{i}")
        u, act = _mm(y2, w_up[i], "nn", [BF16, BF16], f"mlp_up_{i}",
                     epilogue=lambda acc: (acc, jnp.square(jnp.maximum(acc, 0.0))))
        h2 = _mm(act, w_down[i], "nn", [F32], f"mlp_down_{i}", epilogue=lambda acc, res: (res + acc,), extras=(h1,))
        saved.append((h, y, mix, o2, h1, y2, u, act))
        h = h2

    dh, d_nf, sq = _loss_head(h, norm_final.reshape(1, d), target, "loss_head")

    d_nmix, d_nmlp = [None] * DEPTH, [None] * DEPTH
    d_conv, d_alog, d_dtb, d_onorm = [None] * n_gdn, [None] * n_gdn, [None] * n_gdn, [None] * n_gdn
    d_lb = [jnp.zeros((1, d), F32)] * DEPTH
    d_gnorm = [None] * n_hgrn
    scatter_handles = [None] * DEPTH
    token = None
    for i in reversed(range(DEPTH)):
        j = i // 2
        h_in, y, mix, o2, h1, y2, u, act = saved[i]
        g_down = _mm(act, dh, "tn", [BF16], f"g_down_{i}", after=token)
        du = _mm(dh, w_down[i], "nt", [BF16], f"d_u_{i}",
                 epilogue=lambda acc, uu: (acc * (2.0 * jnp.maximum(uu.astype(F32), 0.0)),), extras=(u,))
        g_up = _mm(y2, du, "tn", [BF16], f"g_up_{i}")
        dy2 = _mm(du, w_up[i], "nt", [F32], f"d_y2_{i}")
        dh1, d_nmlp[i] = _rms_bwd(h1, norm_mlp[i:i + 1], dy2, dh, f"rms_mlp_bwd_{i}")
        g_out = _mm(o2, dh1, "tn", [BF16], f"g_out_{i}")
        do2 = _mm(dh1, w_out[i], "nt", [BF16], f"d_o2_{i}")
        if i % 2 == 0:
            projm, projab, st_all = mix
            dpm, dpab, d_conv[j], d_alog[j], d_dtb[j], d_onorm[j] = _gdn_bwd(
                projm, projab, conv_full[j], gdn_a_log[j:j + 1], gdn_dt_bias[j:j + 1], gdn_onorm[j:j + 1],
                st_all, do2, seqs, f"gdn_bwd_{i}")
            g_main = _mm(y, dpm, "tn", [BF16], f"g_in_{i}")
            g_ab = _mm(y, dpab, "tn", [BF16], f"g_in_ab_{i}")
            g_in = jnp.concatenate([g_main, g_ab[:, :2 * N_HEADS]], axis=1)
            g_in = jnp.transpose(g_in.reshape(d, N_DEV, c_gin), (1, 0, 2))
            in_item = _Item(g_in, (N_DEV, d, c_gin), _slot, _slot)
            dy_ab = _mm(dpab, w_ab[i], "nt", [F32], f"d_y_ab_{i}")
            dy = _mm(dpm, w_in[i], "nt", [F32], f"d_y_{i}", epilogue=lambda acc, e: (acc + e,), extras=(dy_ab,))
        else:
            proj, o_raw, st_all = mix
            dp, d_lb[i], d_gnorm[j] = _hgrn_bwd(proj, lbs[i:i + 1], gnorm_full[j:j + 1], st_all, o_raw, do2,
                                               seqs, f"hgrn_bwd_{i}")
            g_in = _mm(y, dp, "tn", [BF16], f"g_in_{i}")
            in_item = _Item(g_in, (N_DEV, d, c_hin), _cols_of(c_hin), _slot)
            dy = _mm(dp, w_in[i], "nt", [F32], f"d_y_{i}")
        group = [_Item(g_down, (N_DEV, r_down, d), _rows_of(r_down), _slot),
                 _Item(g_up, (N_DEV, d, c_up), _cols_of(c_up), _slot),
                 _Item(g_out, (N_DEV, r_out, d), _rows_of(r_out), _slot), in_item]
        (scatter_handles[i],), token = _exchange_start([group], f"scatter_start_{i}")
        dh, d_nmix[i] = _rms_bwd(h_in, norm_mix[i:i + 1], dy, dh1, f"rms_mix_bwd_{i}")
    grad_x = dh.reshape(x.shape)

    slots = [_exchange_wait(scatter_handles[i], dh, f"scatter_wait_{i}") for i in range(DEPTH)]
    upd = {}
    upd["mlp_w_down"] = _adamw_slots(mlp_w_down, [slots[i][0] for i in range(DEPTH)], m_mlp_w_down, v_mlp_w_down,
                                     "adamw_mlp_w_down")
    upd["mlp_w_up"] = _adamw_slots(mlp_w_up, [slots[i][1] for i in range(DEPTH)], m_mlp_w_up, v_mlp_w_up,
                                   "adamw_mlp_w_up")
    upd["gdn_w_out"] = _adamw_slots(gdn_w_out, [slots[i][2] for i in range(0, DEPTH, 2)], m_gdn_w_out, v_gdn_w_out,
                                    "adamw_gdn_w_out")
    upd["hgrn_w_out"] = _adamw_slots(hgrn_w_out, [slots[i][2] for i in range(1, DEPTH, 2)], m_hgrn_w_out,
                                     v_hgrn_w_out, "adamw_hgrn_w_out")
    upd["gdn_w_in"] = _adamw_slots(gdn_w_in, [slots[i][3] for i in range(0, DEPTH, 2)], m_gdn_w_in, v_gdn_w_in,
                                   "adamw_gdn_w_in")
    upd["hgrn_w_in"] = _adamw_slots(hgrn_w_in, [slots[i][3] for i in range(1, DEPTH, 2)], m_hgrn_w_in, v_hgrn_w_in,
                                    "adamw_hgrn_w_in")

    def update(name, w, g, m, v):
        shape = w.shape
        c = shape[-1]
        res = _adamw(w.reshape(-1, c), g.reshape(-1, c), m.reshape(-1, c), v.reshape(-1, c), "adamw_" + name)
        return [g.reshape(shape)] + [o.reshape(shape) for o in res]

    dlb_rows = jnp.concatenate(d_lb, axis=0)
    tail = jnp.concatenate(
        [jnp.concatenate(d_onorm, axis=1), jnp.concatenate(d_alog, axis=1), jnp.concatenate(d_dtb, axis=1)], axis=1)
    tail = jnp.pad(tail, ((0, 0), (0, d - tail.shape[1])))
    conv_rows = jnp.stack(d_conv).reshape(-1, d)
    packed = jnp.concatenate(
        [jnp.concatenate(d_nmix, axis=0), jnp.concatenate(d_nmlp, axis=0), d_nf, sq, dlb_rows,
         jnp.concatenate(d_gnorm, axis=0), tail, conv_rows], axis=0)
    pad_rows = (-packed.shape[0]) % 8
    packed = jnp.pad(packed, ((0, pad_rows), (0, 0)))
    tot = _all_reduce_small(packed, "reduce_small")
    r0 = 0
    g_nmix = tot[r0:r0 + DEPTH]; r0 += DEPTH
    g_nmlp = tot[r0:r0 + DEPTH]; r0 += DEPTH
    g_nf = tot[r0]; r0 += 1
    loss = tot[r0, 0]; r0 += 1
    g_lb = _lb_bwd(hgrn_lb_logits, tot[r0:r0 + DEPTH], "lb_bwd"); r0 += DEPTH
    g_gnorm_full = tot[r0:r0 + n_hgrn]; r0 += n_hgrn
    t_row = tot[r0]; r0 += 1
    g_conv_full = tot[r0:r0 + n_gdn * CONV_K * 3].reshape(n_gdn, CONV_K, 3 * d)
    g_onorm = t_row[0:n_gdn * HEAD_DIM].reshape(n_gdn, HEAD_DIM)
    o1 = n_gdn * HEAD_DIM
    g_alog = t_row[o1:o1 + n_gdn * N_HEADS].reshape(n_gdn, N_HEADS)
    g_dtb = t_row[o1 + n_gdn * N_HEADS:o1 + 2 * n_gdn * N_HEADS].reshape(n_gdn, N_HEADS)
    c_gn, c_cv = hgrn_gnorm.shape[1], gdn_conv.shape[2]
    g_gnorm = lax.dynamic_slice_in_dim(g_gnorm_full, me_i * c_gn, c_gn, axis=1)
    g_conv = lax.dynamic_slice_in_dim(g_conv_full, me_i * c_cv, c_cv, axis=2)

    upd["gdn_conv"] = update("gdn_conv", gdn_conv, g_conv, m_gdn_conv, v_gdn_conv)
    upd["gdn_a_log"] = update("gdn_a_log", gdn_a_log, g_alog, m_gdn_a_log, v_gdn_a_log)
    upd["gdn_dt_bias"] = update("gdn_dt_bias", gdn_dt_bias, g_dtb, m_gdn_dt_bias, v_gdn_dt_bias)
    upd["gdn_onorm"] = update("gdn_onorm", gdn_onorm, g_onorm, m_gdn_onorm, v_gdn_onorm)
    upd["hgrn_lb_logits"] = update("hgrn_lb_logits", hgrn_lb_logits, g_lb, m_hgrn_lb_logits, v_hgrn_lb_logits)
    upd["hgrn_gnorm"] = update("hgrn_gnorm", hgrn_gnorm, g_gnorm, m_hgrn_gnorm, v_hgrn_gnorm)
    upd["norm_mix"] = update("norm_mix", norm_mix, g_nmix, m_norm_mix, v_norm_mix)
    upd["norm_mlp"] = update("norm_mlp", norm_mlp, g_nmlp, m_norm_mlp, v_norm_mlp)
    upd["norm_final"] = update("norm_final", norm_final, g_nf, m_norm_final, v_norm_final)

    order = ["gdn_w_in", "gdn_conv", "gdn_a_log", "gdn_dt_bias", "gdn_onorm", "gdn_w_out", "hgrn_w_in",
             "hgrn_lb_logits", "hgrn_gnorm", "hgrn_w_out", "norm_mix", "norm_mlp", "mlp_w_up", "mlp_w_down",
             "norm_final"]
    outs = [loss, grad_x]
    for k in range(4):
        outs += [upd[name][k] for name in order]
    return tuple(outs)
```

```python
import functools

import jax
import jax.numpy as jnp
from jax import lax
from jax.experimental import pallas as pl
from jax.experimental.pallas import tpu as pltpu

F32 = jnp.float32
BF16 = jnp.bfloat16

D_MODEL = 1024
N_HEADS = 8
HEAD_DIM = 128
CHUNK = 64
SUB = 16
N_SUB = CHUNK // SUB
CONV_K = 4
HALO = 8
EPS = 1e-6
DEPTH = 4
N_DEV = 8
GDN_MAIN = 4 * D_MODEL
GDN_IN = GDN_MAIN + 2 * N_HEADS
AB_PAD = 128
HEAD_GROUP = 8

ADAM_LR = 0.001
ADAM_B1 = 0.9
ADAM_B2 = 0.999
ADAM_EPS = 1e-08
ADAM_WD = 0.01
ADAM_STEP = 10

VMEM_LIMIT = 56 * 1024 * 1024

_DIMS = {
    "nn": (((1,), (0,)), ((), ())),
    "nt": (((1,), (1,)), ((), ())),
    "tn": (((0,), (0,)), ((), ())),
}


def _parts(x, n):
    out = []
    r = x.astype(F32)
    for i in range(n):
        p = r.astype(BF16)
        out.append(p)
        if i + 1 < n:
            r = r - p.astype(F32)
    return out


def _dot_raw(a, b, mode, na, nb):
    ap, bp = _parts(a, na), _parts(b, nb)
    nmax = max(na, nb)
    acc = None
    for i, xa in enumerate(ap):
        for j, xb in enumerate(bp):
            if i + j < nmax:
                t = lax.dot_general(xa, xb, _DIMS[mode], preferred_element_type=F32)
                acc = t if acc is None else acc + t
    return acc


@functools.partial(jax.custom_vjp, nondiff_argnums=(2, 3, 4))
def _dot(a, b, mode, na, nb):
    return _dot_raw(a, b, mode, na, nb)


def _dot_fwd(a, b, mode, na, nb):
    return _dot_raw(a, b, mode, na, nb), (a, b)


def _dot_bwd(mode, na, nb, res, ct):
    a, b = res
    nc = max(na, nb)
    if mode == "nn":
        da = _dot_raw(ct, b, "nt", nc, nb)
        db = _dot_raw(a, ct, "tn", na, nc)
    elif mode == "nt":
        da = _dot_raw(ct, b, "nn", nc, nb)
        db = _dot_raw(ct, a, "tn", nc, na)
    else:
        da = _dot_raw(b, ct, "nt", nb, nc)
        db = _dot_raw(a, ct, "nn", na, nc)
    return da, db


_dot.defvjp(_dot_fwd, _dot_bwd)


def _iota2(shape, dim):
    return lax.broadcasted_iota(jnp.int32, shape, dim)


def _tril_f32(n):
    return (_iota2((n, n), 0) >= _iota2((n, n), 1)).astype(F32)


def _cumsum_rows(g):
    return _dot(_tril_f32(g.shape[0]), g, "nn", 1, 3)


def _inv_unit_lower(L):
    n = L.shape[0]
    eye = (_iota2((n, n), 0) == _iota2((n, n), 1)).astype(F32)
    p = -L
    t = eye + p
    k = 2
    while k < n:
        p = _dot_raw(p, p, "nn", 2, 2)
        t = t + _dot_raw(t, p, "nn", 2, 2)
        k *= 2
    return t


@jax.custom_vjp
def _solve_unit_lower(L, rhs):
    return _dot_raw(_inv_unit_lower(L), rhs, "nn", 2, 2)


def _solve_fwd(L, rhs):
    t = _inv_unit_lower(L)
    sol = _dot_raw(t, rhs, "nn", 2, 2)
    return sol, (t, sol)


def _solve_bwd(res, ct):
    t, sol = res
    y = _dot_raw(t, ct, "tn", 2, 2)
    return -_dot_raw(y, sol, "nt", 2, 2), y


_solve_unit_lower.defvjp(_solve_fwd, _solve_bwd)


def _softplus(x):
    return jnp.maximum(x, 0.0) + jnp.log1p(jnp.exp(-jnp.abs(x)))


def _rms(x, w):
    return x * lax.rsqrt(jnp.mean(x * x, axis=-1, keepdims=True) + EPS) * w


def _hg_pre(p, lb):
    qraw = p[:, 0:D_MODEL]
    f = p[:, D_MODEL:2 * D_MODEL]
    v = p[:, 2 * D_MODEL:3 * D_MODEL]
    g = jnp.log(lb + (1.0 - lb) * jax.nn.sigmoid(f))
    k = (1.0 - lb) * jax.nn.sigmoid(-f)
    q = jax.nn.silu(qraw) * (HEAD_DIM ** -0.5)
    return q, k, v, g, _cumsum_rows(g)


def _hg_head(st, q, k, v, g, gc):
    i3 = lax.broadcasted_iota(jnp.int32, (SUB, SUB, HEAD_DIM), 0)
    j3 = lax.broadcasted_iota(jnp.int32, (SUB, SUB, HEAD_DIM), 1)
    rows = []
    for s in range(N_SUB):
        lo = s * SUB
        qs, ks, vs, gs = q[lo:lo + SUB], k[lo:lo + SUB], v[lo:lo + SUB], gc[lo:lo + SUB]
        dec = jnp.exp(jnp.where(i3 >= j3, gs[:, None, :] - gs[None, :, :], -jnp.inf))
        a_diag = jnp.sum(qs[:, None, :] * ks[None, :, :] * dec, axis=-1)
        o_s = _dot(a_diag, vs, "nn", 1, 1)
        if s > 0:
            gb = gc[lo:lo + 1] - g[lo:lo + 1]
            q_off = qs * jnp.exp(gs - gb)
            k_off = k[0:lo] * jnp.exp(gb - gc[0:lo])
            a_off = _dot(q_off, k_off, "nt", 1, 1)
            o_s = o_s + _dot(a_off, v[0:lo], "nn", 1, 1)
        rows.append(o_s)
    o = jnp.concatenate(rows, axis=0) + _dot(q * jnp.exp(gc), st, "nt", 1, 1)
    g_last = gc[CHUNK - 1:CHUNK]
    st_new = st * jnp.exp(g_last) + _dot(v, k * jnp.exp(g_last - gc), "tn", 1, 1)
    return o, st_new


def _hg_post(o, gate, gw):
    return _rms(o, gw) * jax.nn.silu(gate)


def _gd_pre(xp, a, b, cw, alog, dtb):
    off = HALO - (CONV_K - 1)
    y = cw[0:1] * xp[off:off + CHUNK]
    for kk in range(1, CONV_K):
        y = y + cw[kk:kk + 1] * xp[off + kk:off + kk + CHUNK]
    c = jax.nn.silu(y)
    beta = jax.nn.sigmoid(b)
    g = -jnp.exp(alog) * _softplus(a + dtb)
    expand = (_iota2((N_HEADS, D_MODEL), 1) // HEAD_DIM == _iota2((N_HEADS, D_MODEL), 0)).astype(F32)
    return c, _dot(beta, expand, "nn", 3, 1), _dot(g, expand, "nn", 3, 1)


def _gd_head(st, q, k, v, beta, g, gate, onw):
    q = q * lax.rsqrt(jnp.sum(q * q, axis=-1, keepdims=True) + EPS) * (HEAD_DIM ** -0.5)
    k = k * lax.rsqrt(jnp.sum(k * k, axis=-1, keepdims=True) + EPS)
    gc = _cumsum_rows(g)
    ri = _iota2((CHUNK, CHUNK), 0)
    ci = _iota2((CHUNK, CHUNK), 1)
    diff = gc[:, 0:CHUNK] - gc.T[0:CHUNK, :]
    decay = jnp.exp(jnp.where(ri >= ci, diff, -jnp.inf))
    kb = k * beta
    egc = jnp.exp(gc)
    L = jnp.where(ri > ci, _dot(kb, k, "nt", 1, 1) * decay, 0.0)
    sol = _solve_unit_lower(L, jnp.concatenate([v * beta, kb * egc], axis=1))
    u = sol[:, 0:HEAD_DIM]
    w = sol[:, HEAD_DIM:2 * HEAD_DIM]
    a_qk = jnp.where(ri >= ci, _dot(q, k, "nt", 1, 1) * decay, 0.0)
    g_last = gc[CHUNK - 1:CHUNK]
    v_new = u - _dot(w, st, "nt", 1, 1)
    o = _dot(q * egc, st, "nt", 1, 1) + _dot(a_qk, v_new, "nn", 1, 1)
    st_new = st * jnp.exp(g_last) + _dot(v_new, k * jnp.exp(g_last - gc), "tn", 1, 1)
    return _rms(o, onw) * jax.nn.silu(gate), st_new


def _params(*sem):
    return pltpu.CompilerParams(dimension_semantics=sem, vmem_limit_bytes=VMEM_LIMIT)


def _tile(n, pref):
    t = min(n, pref)
    assert n % t == 0, (n, pref)
    return t


def _mm(a, b, mode, out_dtypes, name, epilogue=None, extras=(), after=None, tm=512, tn=512, tk=1024):
    if mode == "nn":
        (m, k), (k2, n) = a.shape, b.shape
    elif mode == "nt":
        (m, k), (n, k2) = a.shape, b.shape
    else:
        (k, m), (k2, n) = a.shape, b.shape
    assert k == k2, (a.shape, b.shape, mode)
    tm, tn, tk = _tile(m, tm), _tile(n, tn), _tile(k, tk)
    nk = k // tk
    ne, no, nafter = len(extras), len(out_dtypes), int(after is not None)
    if epilogue is None:
        epilogue = lambda acc: (acc,)

    def body(*refs):
        a_ref, b_ref = refs[0], refs[1]
        ex = refs[2:2 + ne]
        outs = refs[2 + ne + nafter:2 + ne + nafter + no]
        part = lax.dot_general(a_ref[...].astype(BF16), b_ref[...].astype(BF16), _DIMS[mode],
                               preferred_element_type=F32)

        def finish(acc):
            for o_ref, val in zip(outs, epilogue(acc, *[e[...] for e in ex])):
                o_ref[...] = val.astype(o_ref.dtype)

        if nk == 1:
            finish(part)
        else:
            acc_ref = refs[-1]
            kk = pl.program_id(2)

            @pl.when(kk == 0)
            def _():
                acc_ref[...] = part

            @pl.when(kk > 0)
            def _():
                acc_ref[...] += part

            @pl.when(kk == nk - 1)
            def _():
                finish(acc_ref[...])

    if mode == "tn":
        a_spec = pl.BlockSpec((tk, tm), lambda i, j, kk: (kk, i))
    else:
        a_spec = pl.BlockSpec((tm, tk), lambda i, j, kk: (i, kk))
    if mode == "nt":
        b_spec = pl.BlockSpec((tn, tk), lambda i, j, kk: (j, kk))
    else:
        b_spec = pl.BlockSpec((tk, tn), lambda i, j, kk: (kk, j))
    o_spec = pl.BlockSpec((tm, tn), lambda i, j, kk: (i, j))
    res = pl.pallas_call(
        body,
        name=name,
        grid=(m // tm, n // tn, nk),
        in_specs=[a_spec, b_spec] + [o_spec] * ne + [pl.BlockSpec(memory_space=pl.ANY)] * nafter,
        out_specs=[o_spec] * no,
        out_shape=[jax.ShapeDtypeStruct((m, n), dt) for dt in out_dtypes],
        scratch_shapes=[pltpu.VMEM((tm, tn), F32)] if nk > 1 else [],
        compiler_params=_params("parallel", "parallel", "arbitrary"),
    )(a, b, *extras, *([after] if nafter else []))
    return res[0] if no == 1 else res


def _rms_fwd(x, w, name, tm=512):
    n, d = x.shape
    tm = _tile(n, tm)

    def body(x_ref, w_ref, y_ref):
        y_ref[...] = _rms(x_ref[...], w_ref[...]).astype(y_ref.dtype)

    return pl.pallas_call(
        body, name=name, grid=(n // tm,),
        in_specs=[pl.BlockSpec((tm, d), lambda i: (i, 0)), pl.BlockSpec((1, d), lambda i: (0, 0))],
        out_specs=pl.BlockSpec((tm, d), lambda i: (i, 0)),
        out_shape=jax.ShapeDtypeStruct((n, d), BF16),
        compiler_params=_params("arbitrary"),
    )(x, w)


def _rms_bwd(x, w, dy, dres, name, tm=512):
    n, d = x.shape
    tm = _tile(n, tm)

    def body(x_ref, w_ref, dy_ref, dres_ref, dx_ref, dw_ref):
        _, vjp = jax.vjp(_rms, x_ref[...], w_ref[...])
        dx, dw = vjp(dy_ref[...].astype(F32))
        dx_ref[...] = dres_ref[...] + dx

        @pl.when(pl.program_id(0) == 0)
        def _():
            dw_ref[...] = dw

        @pl.when(pl.program_id(0) > 0)
        def _():
            dw_ref[...] += dw

    row = pl.BlockSpec((tm, d), lambda i: (i, 0))
    vec = pl.BlockSpec((1, d), lambda i: (0, 0))
    return pl.pallas_call(
        body, name=name, grid=(n // tm,),
        in_specs=[row, vec, row, row],
        out_specs=[row, vec],
        out_shape=[jax.ShapeDtypeStruct((n, d), F32), jax.ShapeDtypeStruct((1, d), F32)],
        compiler_params=_params("arbitrary"),
    )(x, w, dy, dres)


def _loss_head(h, w, target, name, tm=512):
    n, d = h.shape
    tm = _tile(n, tm)

    def body(h_ref, w_ref, t_ref, dh_ref, dw_ref, sq_ref):
        y, vjp = jax.vjp(_rms, h_ref[...], w_ref[...])
        err = y - t_ref[...]
        dh, dw = vjp(err * (1.0 / d))
        dh_ref[...] = dh
        sq = jnp.sum(err * err, axis=0, keepdims=True)

        @pl.when(pl.program_id(0) == 0)
        def _():
            dw_ref[...] = dw
            sq_ref[...] = sq

        @pl.when(pl.program_id(0) > 0)
        def _():
            dw_ref[...] += dw
            sq_ref[...] += sq

        @pl.when(pl.program_id(0) == n // tm - 1)
        def _():
            total = jnp.sum(sq_ref[...], axis=1, keepdims=True) * (0.5 / d)
            sq_ref[...] = jnp.broadcast_to(total, sq_ref.shape)

    row = pl.BlockSpec((tm, d), lambda i: (i, 0))
    vec = pl.BlockSpec((1, d), lambda i: (0, 0))
    return pl.pallas_call(
        body, name=name, grid=(n // tm,),
        in_specs=[row, vec, row],
        out_specs=[row, vec, vec],
        out_shape=[jax.ShapeDtypeStruct((n, d), F32), jax.ShapeDtypeStruct((1, d), F32),
                   jax.ShapeDtypeStruct((1, d), F32)],
        compiler_params=_params("arbitrary"),
    )(h, w, target)


def _lower_bounds(logits):
    sm = jax.nn.softmax(logits, axis=0)
    rows = [sm[0:1] * 0.0]
    for r in range(1, DEPTH):
        rows.append(rows[-1] + sm[r:r + 1])
    return jnp.concatenate(rows, axis=0)


def _lb_fwd(logits, name):
    def body(l_ref, o_ref):
        o_ref[...] = _lower_bounds(l_ref[...])

    return pl.pallas_call(body, name=name, out_shape=jax.ShapeDtypeStruct(logits.shape, F32))(logits)


def _lb_bwd(logits, dlb, name):
    def body(l_ref, d_ref, o_ref):
        _, vjp = jax.vjp(_lower_bounds, l_ref[...])
        (o_ref[...],) = vjp(d_ref[...])

    return pl.pallas_call(body, name=name, out_shape=jax.ShapeDtypeStruct(logits.shape, F32))(logits, dlb)


def _head_slice(h):
    if isinstance(h, int):
        return pl.ds(h * HEAD_DIM, HEAD_DIM)
    return pl.ds(pl.multiple_of(h * HEAD_DIM, HEAD_DIM), HEAD_DIM)


def _head_groups(group_body):
    if HEAD_GROUP == N_HEADS:
        group_body(list(range(N_HEADS)))
        return

    def trip(i, carry):
        group_body([i * HEAD_GROUP + t for t in range(HEAD_GROUP)])
        return carry

    lax.fori_loop(0, N_HEADS // HEAD_GROUP, trip, 0)


def _stack_heads(ref, hs, first=0):
    return jnp.stack([ref[:, _head_slice(h + first)] for h in hs])


def _unstack_heads(ref, hs, val, first=0):
    for t, h in enumerate(hs):
        ref[:, _head_slice(h + first)] = val[t].astype(ref.dtype)


_GD_HEADS = jax.vmap(_gd_head, in_axes=(0, 0, 0, 0, 0, 0, 0, None))


def _hgrn_fwd(proj, lb, gw, seqs, name):
    n = proj.shape[0]
    nc = n // seqs // CHUNK
    d = D_MODEL

    def body(p_ref, lb_ref, gw_ref, o2_ref, o_ref, st_all_ref, st_sc, q_sc, k_sc, v_sc, g_sc, gc_sc):
        @pl.when(pl.program_id(1) == 0)
        def _():
            st_sc[...] = jnp.zeros_like(st_sc)

        q_sc[...], k_sc[...], v_sc[...], g_sc[...], gc_sc[...] = _hg_pre(p_ref[:, 0:3 * d], lb_ref[...])
        st_all_ref[0] = st_sc[...]

        def group(hs):
            sts = pl.ds(hs[0], len(hs))
            o, st_new = jax.vmap(_hg_head)(st_sc[sts], *[_stack_heads(r, hs) for r in (q_sc, k_sc, v_sc, g_sc, gc_sc)])
            _unstack_heads(o_ref, hs, o)
            st_sc[sts] = st_new

        _head_groups(group)
        o2_ref[...] = _hg_post(o_ref[...], p_ref[:, 3 * d:4 * d], gw_ref[...]).astype(o2_ref.dtype)

    idx = lambda b, c: (b * nc + c, 0)
    vec = pl.BlockSpec((1, d), lambda b, c: (0, 0))
    act = pl.BlockSpec((CHUNK, d), idx)
    return pl.pallas_call(
        body, name=name, grid=(seqs, nc),
        in_specs=[pl.BlockSpec((CHUNK, 4 * d), idx), vec, vec],
        out_specs=[act, act, pl.BlockSpec((1, N_HEADS, HEAD_DIM, HEAD_DIM), lambda b, c: (b * nc + c, 0, 0, 0))],
        out_shape=[jax.ShapeDtypeStruct((n, d), BF16), jax.ShapeDtypeStruct((n, d), F32),
                   jax.ShapeDtypeStruct((n // CHUNK, N_HEADS, HEAD_DIM, HEAD_DIM), F32)],
        scratch_shapes=[pltpu.VMEM((N_HEADS, HEAD_DIM, HEAD_DIM), F32)] + [pltpu.VMEM((CHUNK, d), F32)] * 5,
        compiler_params=_params("arbitrary", "arbitrary"),
    )(proj, lb, gw)


def _hgrn_bwd(proj, lb, gw, st_all, o, do2, seqs, name):
    n = proj.shape[0]
    nc = n // seqs // CHUNK
    d = D_MODEL

    def body(p_ref, lb_ref, gw_ref, st_all_ref, o_ref, do2_ref, dp_ref, dlb_ref, dgw_ref,
             dst_sc, q_sc, k_sc, v_sc, g_sc, gc_sc, do_sc, dq_sc, dk_sc, dv_sc, dg_sc, dgc_sc):
        first = (pl.program_id(0) == 0) & (pl.program_id(1) == 0)

        @pl.when(pl.program_id(1) == 0)
        def _():
            dst_sc[...] = jnp.zeros_like(dst_sc)

        pre_out, pre_vjp = jax.vjp(_hg_pre, p_ref[:, 0:3 * d], lb_ref[...])
        q_sc[...], k_sc[...], v_sc[...], g_sc[...], gc_sc[...] = pre_out
        _, post_vjp = jax.vjp(_hg_post, o_ref[...], p_ref[:, 3 * d:4 * d], gw_ref[...])
        do_sc[...], dgate, dgw = post_vjp(do2_ref[...].astype(F32))
        dp_ref[:, 3 * d:4 * d] = dgate.astype(dp_ref.dtype)

        def group(hs):
            sts = pl.ds(hs[0], len(hs))
            _, vjp = jax.vjp(jax.vmap(_hg_head), st_all_ref[0, sts],
                             *[_stack_heads(r, hs) for r in (q_sc, k_sc, v_sc, g_sc, gc_sc)])
            grads = vjp((_stack_heads(do_sc, hs), dst_sc[sts]))
            dst_sc[sts] = grads[0]
            for r, val in zip((dq_sc, dk_sc, dv_sc, dg_sc, dgc_sc), grads[1:]):
                _unstack_heads(r, hs, val)

        _head_groups(group)
        dp, dlb = pre_vjp((dq_sc[...], dk_sc[...], dv_sc[...], dg_sc[...], dgc_sc[...]))
        dp_ref[:, 0:3 * d] = dp.astype(dp_ref.dtype)

        @pl.when(first)
        def _():
            dlb_ref[...] = dlb
            dgw_ref[...] = dgw

        @pl.when(jnp.logical_not(first))
        def _():
            dlb_ref[...] += dlb
            dgw_ref[...] += dgw

    idx = lambda b, c: (b * nc + nc - 1 - c, 0)
    vec = pl.BlockSpec((1, d), lambda b, c: (0, 0))
    act = pl.BlockSpec((CHUNK, d), idx)
    wide = pl.BlockSpec((CHUNK, 4 * d), idx)
    return pl.pallas_call(
        body, name=name, grid=(seqs, nc),
        in_specs=[wide, vec, vec,
                  pl.BlockSpec((1, N_HEADS, HEAD_DIM, HEAD_DIM), lambda b, c: (b * nc + nc - 1 - c, 0, 0, 0)),
                  act, act],
        out_specs=[wide, vec, vec],
        out_shape=[jax.ShapeDtypeStruct((n, 4 * d), BF16), jax.ShapeDtypeStruct((1, d), F32),
                   jax.ShapeDtypeStruct((1, d), F32)],
        scratch_shapes=[pltpu.VMEM((N_HEADS, HEAD_DIM, HEAD_DIM), F32)] + [pltpu.VMEM((CHUNK, d), F32)] * 11,
        compiler_params=_params("arbitrary", "arbitrary"),
    )(proj, lb, gw, st_all, o, do2)


def _gd_xp(halo_ref, p_ref, first_chunk):
    halo = jnp.where(first_chunk, 0.0, halo_ref[...])
    return jnp.concatenate([halo, p_ref[:, 0:3 * D_MODEL]], axis=0)


def _gdn_fwd(projm, projab, cw, alog, dtb, onw, seqs, name):
    n = projm.shape[0]
    nc = n // seqs // CHUNK
    d = D_MODEL
    per_halo = CHUNK // HALO

    def body(p_ref, halo_ref, ab_ref, cw_ref, alog_ref, dtb_ref, onw_ref, o2_ref, st_all_ref,
             st_sc, c_sc, beta_sc, g_sc):
        @pl.when(pl.program_id(1) == 0)
        def _():
            st_sc[...] = jnp.zeros_like(st_sc)

        xp = _gd_xp(halo_ref, p_ref, pl.program_id(1) == 0)
        c_sc[...], beta_sc[...], g_sc[...] = _gd_pre(
            xp, ab_ref[:, 0:N_HEADS], ab_ref[:, N_HEADS:2 * N_HEADS], cw_ref[...], alog_ref[...], dtb_ref[...])
        st_all_ref[0] = st_sc[...]

        def group(hs):
            sts = pl.ds(hs[0], len(hs))
            o2, st_new = _GD_HEADS(
                st_sc[sts], _stack_heads(c_sc, hs), _stack_heads(c_sc, hs, N_HEADS), _stack_heads(c_sc, hs, 2 * N_HEADS),
                _stack_heads(beta_sc, hs), _stack_heads(g_sc, hs), _stack_heads(p_ref, hs, 3 * N_HEADS), onw_ref[...])
            _unstack_heads(o2_ref, hs, o2)
            st_sc[sts] = st_new

        _head_groups(group)

    idx = lambda b, c: (b * nc + c, 0)
    const = lambda b, c: (0, 0)
    return pl.pallas_call(
        body, name=name, grid=(seqs, nc),
        in_specs=[pl.BlockSpec((CHUNK, 4 * d), idx),
                  pl.BlockSpec((HALO, 3 * d), lambda b, c: (jnp.maximum((b * nc + c) * per_halo - 1, 0), 0)),
                  pl.BlockSpec((CHUNK, AB_PAD), idx),
                  pl.BlockSpec((CONV_K, 3 * d), const), pl.BlockSpec((1, N_HEADS), const),
                  pl.BlockSpec((1, N_HEADS), const), pl.BlockSpec((1, HEAD_DIM), const)],
        out_specs=[pl.BlockSpec((CHUNK, d), idx),
                   pl.BlockSpec((1, N_HEADS, HEAD_DIM, HEAD_DIM), lambda b, c: (b * nc + c, 0, 0, 0))],
        out_shape=[jax.ShapeDtypeStruct((n, d), BF16),
                   jax.ShapeDtypeStruct((n // CHUNK, N_HEADS, HEAD_DIM, HEAD_DIM), F32)],
        scratch_shapes=[pltpu.VMEM((N_HEADS, HEAD_DIM, HEAD_DIM), F32), pltpu.VMEM((CHUNK, 3 * d), F32),
                        pltpu.VMEM((CHUNK, d), F32), pltpu.VMEM((CHUNK, d), F32)],
        compiler_params=_params("arbitrary", "arbitrary"),
    )(projm, projm, projab, cw, alog, dtb, onw)


def _gdn_bwd(projm, projab, cw, alog, dtb, onw, st_all, do2, seqs, name):
    n = projm.shape[0]
    nc = n // seqs // CHUNK
    d = D_MODEL
    per_halo = CHUNK // HALO

    def body(p_ref, halo_ref, ab_ref, cw_ref, alog_ref, dtb_ref, onw_ref, st_all_ref, do2_ref,
             dp_ref, dab_ref, dcw_ref, dalog_ref, ddtb_ref, donw_ref,
             dst_sc, dhalo_sc, c_sc, beta_sc, g_sc, dc_sc, dbeta_sc, dg_sc, donw_sc):
        step = pl.program_id(1)
        first = (pl.program_id(0) == 0) & (step == 0)

        @pl.when(step == 0)
        def _():
            dst_sc[...] = jnp.zeros_like(dst_sc)
            dhalo_sc[...] = jnp.zeros_like(dhalo_sc)

        donw_sc[...] = jnp.zeros_like(donw_sc)
        xp = _gd_xp(halo_ref, p_ref, step == nc - 1)
        pre_out, pre_vjp = jax.vjp(_gd_pre, xp, ab_ref[:, 0:N_HEADS], ab_ref[:, N_HEADS:2 * N_HEADS],
                                   cw_ref[...], alog_ref[...], dtb_ref[...])
        c_sc[...], beta_sc[...], g_sc[...] = pre_out

        def group(hs):
            sts = pl.ds(hs[0], len(hs))
            _, vjp = jax.vjp(
                _GD_HEADS, st_all_ref[0, sts], _stack_heads(c_sc, hs), _stack_heads(c_sc, hs, N_HEADS),
                _stack_heads(c_sc, hs, 2 * N_HEADS), _stack_heads(beta_sc, hs), _stack_heads(g_sc, hs),
                _stack_heads(p_ref, hs, 3 * N_HEADS), onw_ref[...])
            dst, dq, dk, dv, dbeta, dg, dgate, donw = vjp((_stack_heads(do2_ref, hs).astype(F32), dst_sc[sts]))
            dst_sc[sts] = dst
            _unstack_heads(dc_sc, hs, dq)
            _unstack_heads(dc_sc, hs, dk, N_HEADS)
            _unstack_heads(dc_sc, hs, dv, 2 * N_HEADS)
            _unstack_heads(dbeta_sc, hs, dbeta)
            _unstack_heads(dg_sc, hs, dg)
            _unstack_heads(dp_ref, hs, dgate, 3 * N_HEADS)
            donw_sc[...] += donw

        _head_groups(group)
        dxp, da, db, dcw, dalog, ddtb = pre_vjp((dc_sc[...], dbeta_sc[...], dg_sc[...]))
        dqkv = jnp.concatenate([dxp[HALO:CHUNK], dxp[CHUNK:HALO + CHUNK] + dhalo_sc[...]], axis=0)
        dp_ref[:, 0:3 * d] = dqkv.astype(dp_ref.dtype)
        dhalo_sc[...] = dxp[0:HALO]
        dab_ref[...] = jnp.concatenate(
            [da, db, jnp.zeros((CHUNK, AB_PAD - 2 * N_HEADS), F32)], axis=1).astype(dab_ref.dtype)

        @pl.when(first)
        def _():
            dcw_ref[...] = dcw
            dalog_ref[...] = dalog
            ddtb_ref[...] = ddtb
            donw_ref[...] = donw_sc[...]

        @pl.when(jnp.logical_not(first))
        def _():
            dcw_ref[...] += dcw
            dalog_ref[...] += dalog
            ddtb_ref[...] += ddtb
            donw_ref[...] += donw_sc[...]

    rev = lambda b, c: b * nc + nc - 1 - c
    idx = lambda b, c: (rev(b, c), 0)
    const = lambda b, c: (0, 0)
    small = [pl.BlockSpec((CONV_K, 3 * d), const), pl.BlockSpec((1, N_HEADS), const),
             pl.BlockSpec((1, N_HEADS), const), pl.BlockSpec((1, HEAD_DIM), const)]
    return pl.pallas_call(
        body, name=name, grid=(seqs, nc),
        in_specs=[pl.BlockSpec((CHUNK, 4 * d), idx),
                  pl.BlockSpec((HALO, 3 * d), lambda b, c: (jnp.maximum(rev(b, c) * per_halo - 1, 0), 0)),
                  pl.BlockSpec((CHUNK, AB_PAD), idx)] + small + [
                  pl.BlockSpec((1, N_HEADS, HEAD_DIM, HEAD_DIM), lambda b, c: (rev(b, c), 0, 0, 0)),
                  pl.BlockSpec((CHUNK, d), idx)],
        out_specs=[pl.BlockSpec((CHUNK, 4 * d), idx), pl.BlockSpec((CHUNK, AB_PAD), idx)] + small,
        out_shape=[jax.ShapeDtypeStruct((n, 4 * d), BF16), jax.ShapeDtypeStruct((n, AB_PAD), BF16),
                   jax.ShapeDtypeStruct((CONV_K, 3 * d), F32), jax.ShapeDtypeStruct((1, N_HEADS), F32),
                   jax.ShapeDtypeStruct((1, N_HEADS), F32), jax.ShapeDtypeStruct((1, HEAD_DIM), F32)],
        scratch_shapes=[pltpu.VMEM((N_HEADS, HEAD_DIM, HEAD_DIM), F32), pltpu.VMEM((HALO, 3 * d), F32),
                        pltpu.VMEM((CHUNK, 3 * d), F32), pltpu.VMEM((CHUNK, d), F32), pltpu.VMEM((CHUNK, d), F32),
                        pltpu.VMEM((CHUNK, 3 * d), F32), pltpu.VMEM((CHUNK, d), F32), pltpu.VMEM((CHUNK, d), F32),
                        pltpu.VMEM((1, HEAD_DIM), F32)],
        compiler_params=_params("arbitrary", "arbitrary"),
    )(projm, projm, projab, cw, alog, dtb, onw, st_all, do2)


def _adam_update(w, g, m, v):
    b1c = 1.0 - ADAM_B1 ** ADAM_STEP
    b2c = 1.0 - ADAM_B2 ** ADAM_STEP
    m_new = ADAM_B1 * m + (1.0 - ADAM_B1) * g
    v_new = ADAM_B2 * v + (1.0 - ADAM_B2) * (g * g)
    delta = -ADAM_LR * ((m_new / b1c) / (jnp.sqrt(v_new / b2c) + ADAM_EPS) + ADAM_WD * w)
    return delta, m_new, v_new


def _adamw(w, g, m, v, name, tr=256):
    r, c = w.shape
    tr = _tile(r, tr)

    def body(w_ref, g_ref, m_ref, v_ref, d_ref, mo_ref, vo_ref):
        d_ref[...], mo_ref[...], vo_ref[...] = _adam_update(w_ref[...], g_ref[...], m_ref[...], v_ref[...])

    blk = pl.BlockSpec((tr, c), lambda i: (i, 0))
    return pl.pallas_call(
        body, name=name, grid=(r // tr,),
        in_specs=[blk] * 4, out_specs=[blk] * 3,
        out_shape=[jax.ShapeDtypeStruct((r, c), F32)] * 3,
        compiler_params=_params("arbitrary"),
    )(w, g, m, v)


def _adamw_slots(w, slot_bufs, m, v, name, tr=256):
    nl, r, c = w.shape
    tr = _tile(r, tr)

    def body(*refs):
        w_ref = refs[0]
        g_refs = refs[1:1 + nl]
        m_ref, v_ref, go_ref, d_ref, mo_ref, vo_ref = refs[1 + nl:]
        for k in range(nl):
            @pl.when(pl.program_id(0) == k)
            def _(k=k):
                g = g_refs[k][0].astype(F32)
                for s in range(1, N_DEV):
                    g = g + g_refs[k][s].astype(F32)
                go_ref[0] = g

        d_ref[0], mo_ref[0], vo_ref[0] = _adam_update(w_ref[0], go_ref[0], m_ref[0], v_ref[0])

    blk = pl.BlockSpec((1, tr, c), lambda l, i: (l, i, 0))
    g_specs = [pl.BlockSpec((N_DEV, tr, c), lambda l, i, k=k: (0, jnp.where(l == k, i, 0), 0)) for k in range(nl)]
    return pl.pallas_call(
        body, name=name, grid=(nl, r // tr),
        in_specs=[blk] + g_specs + [blk, blk], out_specs=[blk] * 4,
        out_shape=[jax.ShapeDtypeStruct((nl, r, c), F32)] * 4,
        compiler_params=_params("arbitrary", "arbitrary"),
    )(w, *slot_bufs, m, v)


def _mesh_pos():
    return lax.axis_index("x"), lax.axis_index("y"), lax.axis_index("c")


def _flip(pos, p):
    x, y, c = pos
    return ((1 - x) if p & 4 else x, (1 - y) if p & 2 else y, (1 - c) if p & 1 else c)


def _lin(pos):
    return 4 * pos[0] + 2 * pos[1] + pos[2]


_HBM = pl.BlockSpec(memory_space=pltpu.HBM)
_SEM = pl.BlockSpec(memory_space=pltpu.SEMAPHORE)
_DATAFLOW = pltpu.SideEffectType.DATAFLOW_SIDE_EFFECTING


class _Item:
    def __init__(self, src, land_shape, src_pick, dst_pick):
        self.src, self.land_shape, self.src_pick, self.dst_pick = src, land_shape, src_pick, dst_pick


def _remote_copies(items, src, land, send_sem, recv_sem, me, arriving):
    me_i = _lin(me)
    out = []
    for it, s_ref, l_ref in zip(items, src, land):
        for p in range(1, N_DEV):
            peer = _flip(me, p)
            out.append(pltpu.make_async_remote_copy(
                src_ref=it.src_pick(s_ref, _lin(peer)),
                dst_ref=it.dst_pick(l_ref, _lin(peer) if arriving else me_i),
                send_sem=send_sem, recv_sem=recv_sem, device_id=peer, device_id_type=pl.DeviceIdType.MESH))
    return out


def _own_copies(items, src, land, sem, me):
    me_i = _lin(me)
    return [pltpu.make_async_copy(it.src_pick(s_ref, me_i), it.dst_pick(l_ref, me_i), sem)
            for it, s_ref, l_ref in zip(items, src, land)]


def _exchange_start(groups, name):
    items = [it for g in groups for it in g]
    n, ng = len(items), len(groups)
    first = [sum(len(g) for g in groups[:gi]) for gi in range(ng)]

    def body(*refs):
        src, land = refs[0:n], refs[n:2 * n]
        send_sems, recv_sems = refs[2 * n:2 * n + ng], refs[2 * n + ng:2 * n + 2 * ng]
        token = refs[4 * n + 2 * ng]
        me = _mesh_pos()
        for gi, g in enumerate(groups):
            sl = slice(first[gi], first[gi] + len(g))
            for cp in _remote_copies(g, src[sl], land[sl], send_sems[gi], recv_sems[gi], me, arriving=False):
                cp.start()
            for cp in _own_copies(g, src[sl], land[sl], recv_sems[gi], me):
                cp.start()
        token[...] = jnp.zeros_like(token)

    srcs = [pltpu.with_memory_space_constraint(it.src, pltpu.HBM) for it in items]
    lands = [pltpu.with_memory_space_constraint(lax.empty(it.land_shape, it.src.dtype), pltpu.HBM) for it in items]
    res = pl.pallas_call(
        body, name=name,
        out_shape=([pltpu.SemaphoreType.DMA(())] * (2 * ng)
                   + [pltpu.HBM(it.src.shape, it.src.dtype) for it in items]
                   + [pltpu.HBM(it.land_shape, it.src.dtype) for it in items]
                   + [jax.ShapeDtypeStruct((8, 128), F32)]),
        in_specs=[_HBM] * (2 * n),
        out_specs=[_SEM] * (2 * ng) + [_HBM] * (2 * n) + [pl.BlockSpec(memory_space=pltpu.VMEM)],
        input_output_aliases={i: 2 * ng + i for i in range(2 * n)},
        compiler_params=pltpu.CompilerParams(has_side_effects=_DATAFLOW),
    )(*srcs, *lands)
    send_sems, recv_sems = res[0:ng], res[ng:2 * ng]
    src_thru, land_thru = res[2 * ng:2 * ng + n], res[2 * ng + n:2 * ng + 2 * n]
    handles = []
    for gi, g in enumerate(groups):
        sl = slice(first[gi], first[gi] + len(g))
        handles.append((g, src_thru[sl], land_thru[sl], send_sems[gi], recv_sems[gi]))
    return handles, res[-1]


def _exchange_wait(handle, after, name):
    items, src_thru, land_thru, send_sem, recv_sem = handle
    k = len(items)

    def body(*refs):
        src, land = refs[0:k], refs[k:2 * k]
        send_ref, recv_ref = refs[2 * k], refs[2 * k + 1]
        for cp in _remote_copies(items, src, land, send_ref, recv_ref, _mesh_pos(), arriving=True):
            cp.wait_send()
            cp.wait_recv()
        for cp in _own_copies(items, src, land, recv_ref, _mesh_pos()):
            cp.wait()

    res = pl.pallas_call(
        body, name=name,
        out_shape=([pltpu.HBM(s.shape, s.dtype) for s in src_thru] + [pltpu.HBM(l.shape, l.dtype) for l in land_thru]),
        in_specs=[_HBM] * (2 * k) + [_SEM, _SEM, pl.BlockSpec(memory_space=pl.ANY)],
        out_specs=[_HBM] * (2 * k),
        input_output_aliases={i: i for i in range(2 * k)},
        compiler_params=pltpu.CompilerParams(has_side_effects=_DATAFLOW),
    )(*src_thru, *land_thru, send_sem, recv_sem, after)
    return res[k:2 * k]


def _whole(ref, i):
    return ref


def _slot(ref, i):
    return ref.at[i]


def _rows_of(r):
    return lambda ref, i: ref.at[pl.ds(pl.multiple_of(i * r, r), r), :]


def _cols_of(c):
    return lambda ref, i: ref.at[:, pl.ds(pl.multiple_of(i * c, c), c)]


def _all_reduce_small(buf, name):
    r, c = buf.shape

    def body(src_ref, out_ref, all_ref, send_sems, recv_sems):
        me = _mesh_pos()
        me_i = _lin(me)
        all_ref[me_i] = src_ref[...]
        for p in range(1, N_DEV):
            peer = _flip(me, p)
            pltpu.make_async_remote_copy(
                src_ref=src_ref, dst_ref=all_ref.at[me_i], send_sem=send_sems.at[p - 1], recv_sem=recv_sems.at[p - 1],
                device_id=peer, device_id_type=pl.DeviceIdType.MESH).start()
        for p in range(1, N_DEV):
            peer = _flip(me, p)
            cp = pltpu.make_async_remote_copy(
                src_ref=src_ref, dst_ref=all_ref.at[_lin(peer)], send_sem=send_sems.at[p - 1],
                recv_sem=recv_sems.at[p - 1], device_id=peer, device_id_type=pl.DeviceIdType.MESH)
            cp.wait_recv()
            cp.wait_send()
        acc = all_ref[0]
        for s in range(1, N_DEV):
            acc = acc + all_ref[s]
        out_ref[...] = acc

    vm = pl.BlockSpec(memory_space=pltpu.VMEM)
    return pl.pallas_call(
        body, name=name, in_specs=[vm], out_specs=vm,
        out_shape=jax.ShapeDtypeStruct((r, c), F32),
        scratch_shapes=[pltpu.VMEM((N_DEV, r, c), F32), pltpu.SemaphoreType.DMA((N_DEV - 1,)),
                        pltpu.SemaphoreType.DMA((N_DEV - 1,))],
        compiler_params=pltpu.CompilerParams(has_side_effects=True),
    )(buf)


def _unshard_cols(g):
    s, l, r, c = g.shape
    return jnp.transpose(g, (1, 2, 0, 3)).reshape(l, r, s * c)


def kernel(x, gdn_w_in, gdn_conv, gdn_a_log, gdn_dt_bias, gdn_onorm, gdn_w_out, hgrn_w_in, hgrn_lb_logits, hgrn_gnorm, hgrn_w_out, norm_mix, norm_mlp, mlp_w_up, mlp_w_down, norm_final, loss_target, m_gdn_w_in, m_gdn_conv, m_gdn_a_log, m_gdn_dt_bias, m_gdn_onorm, m_gdn_w_out, m_hgrn_w_in, m_hgrn_lb_logits, m_hgrn_gnorm, m_hgrn_w_out, m_norm_mix, m_norm_mlp, m_mlp_w_up, m_mlp_w_down, m_norm_final, v_gdn_w_in, v_gdn_conv, v_gdn_a_log, v_gdn_dt_bias, v_gdn_onorm, v_gdn_w_out, v_hgrn_w_in, v_hgrn_lb_logits, v_hgrn_gnorm, v_hgrn_w_out, v_norm_mix, v_norm_mlp, v_mlp_w_up, v_mlp_w_down, v_norm_final):
    seqs, seq_len, d = x.shape
    n = seqs * seq_len
    me_i = _lin(_mesh_pos())
    x2 = x.reshape(n, d)
    target = loss_target.reshape(n, d)
    n_gdn, n_hgrn = gdn_w_in.shape[0], hgrn_w_in.shape[0]

    r_out, r_down = gdn_w_out.shape[1], mlp_w_down.shape[1]
    c_gin, c_hin, c_up = gdn_w_in.shape[2], hgrn_w_in.shape[2], mlp_w_up.shape[2]

    def gathered(w, pick, land_shape):
        return _Item(w.astype(BF16), land_shape, _whole, pick)

    groups = [[_Item(gdn_conv, (N_DEV,) + gdn_conv.shape, _whole, _slot),
               _Item(hgrn_gnorm, (N_DEV,) + hgrn_gnorm.shape, _whole, _slot)]]
    for i in range(DEPTH):
        j = i // 2
        if i % 2 == 0:
            groups += [[gathered(gdn_w_in[j], _slot, (N_DEV, d, c_gin))],
                       [gathered(gdn_w_out[j], _rows_of(r_out), (N_DEV * r_out, d))]]
        else:
            groups += [[gathered(hgrn_w_in[j], _cols_of(c_hin), (d, N_DEV * c_hin))],
                       [gathered(hgrn_w_out[j], _rows_of(r_out), (N_DEV * r_out, d))]]
        groups += [[gathered(mlp_w_up[i], _cols_of(c_up), (d, N_DEV * c_up))],
                   [gathered(mlp_w_down[i], _rows_of(r_down), (N_DEV * r_down, d))]]
    gather_handles, token = _exchange_start(groups, "gather_start")
    lbs = _lb_fwd(hgrn_lb_logits + token[0:1, 0:1], "lb_fwd")

    def arrived(k, after, name):
        return _exchange_wait(gather_handles[k], after, "gather_wait_" + name)

    saved = []
    w_in, w_ab, w_out, w_up, w_down = ([None] * DEPTH for _ in range(5))
    h = x2
    for i in range(DEPTH):
        j = i // 2
        if i == 0:
            g_conv, g_gnorm = arrived(0, h, "small")
            conv_full = _unshard_cols(g_conv)
            gnorm_full = jnp.transpose(g_gnorm, (1, 0, 2)).reshape(n_hgrn, d)
        y = _rms_fwd(h, norm_mix[i:i + 1], f"rms_mix_{i}")
        (w_in[i],) = arrived(1 + 4 * i, y, f"in_{i}")
        if i % 2 == 0:
            w_gin = jnp.transpose(w_in[i], (1, 0, 2)).reshape(d, N_DEV * c_gin)
            w_in[i] = w_gin[:, :GDN_MAIN]
            w_ab[i] = jnp.pad(w_gin[:, GDN_MAIN:], ((0, 0), (0, AB_PAD - 2 * N_HEADS)))
            projm = _mm(y, w_in[i], "nn", [F32], f"gdn_proj_{i}")
            projab = _mm(y, w_ab[i], "nn", [F32], f"gdn_proj_ab_{i}")
            o2, st_all = _gdn_fwd(projm, projab, conv_full[j], gdn_a_log[j:j + 1], gdn_dt_bias[j:j + 1],
                                  gdn_onorm[j:j + 1], seqs, f"gdn_fwd_{i}")
            mix = (projm, projab, st_all)
        else:
            proj = _mm(y, w_in[i], "nn", [F32], f"hgrn_proj_{i}")
            o2, o_raw, st_all = _hgrn_fwd(proj, lbs[i:i + 1], gnorm_full[j:j + 1], seqs, f"hgrn_fwd_{i}")
            mix = (proj, o_raw, st_all)
        (w_out[i],) = arrived(2 + 4 * i, o2, f"out_{i}")
        h1 = _mm(o2, w_out[i], "nn", [F32], f"mix_out_{i}", epilogue=lambda acc, res: (res + acc,), extras=(h,))
        y2 = _rms_fwd(h1, norm_mlp[i:i + 1], f"rms_mlp_{i}")
        (w_up[i],) = arrived(3 + 4 * i, y2, f"up_{i}")
        u, act = _mm(y2, w_up[i], "nn", [BF16, BF16], f"mlp_up_{i}",
                     epilogue=lambda acc: (acc, jnp.square(jnp.maximum(acc, 0.0))))
        (w_down[i],) = arrived(4 + 4 * i, act, f"down_{i}")
        h2 = _mm(act, w_down[i], "nn", [F32], f"mlp_down_{i}", epilogue=lambda acc, res: (res + acc,), extras=(h1,))
        saved.append((h, y, mix, o2, h1, y2, u, act))
        h = h2

    dh, d_nf, sq = _loss_head(h, norm_final.reshape(1, d), target, "loss_head")

    d_nmix, d_nmlp = [None] * DEPTH, [None] * DEPTH
    d_conv, d_alog, d_dtb, d_onorm = [None] * n_gdn, [None] * n_gdn, [None] * n_gdn, [None] * n_gdn
    d_lb = [jnp.zeros((1, d), F32)] * DEPTH
    d_gnorm = [None] * n_hgrn
    mlp_handles, mix_handles = [None] * DEPTH, [None] * DEPTH
    token = None
    for i in reversed(range(DEPTH)):
        j = i // 2
        h_in, y, mix, o2, h1, y2, u, act = saved[i]
        g_down = _mm(act, dh, "tn", [BF16], f"g_down_{i}", after=token)
        du = _mm(dh, w_down[i], "nt", [BF16], f"d_u_{i}",
                 epilogue=lambda acc, uu: (acc * (2.0 * jnp.maximum(uu.astype(F32), 0.0)),), extras=(u,))
        g_up = _mm(y2, du, "tn", [BF16], f"g_up_{i}")
        mlp_handles[i], token = _exchange_start(
            [[_Item(g_down, (N_DEV, r_down, d), _rows_of(r_down), _slot)],
             [_Item(g_up, (N_DEV, d, c_up), _cols_of(c_up), _slot)]], f"scatter_start_mlp_{i}")
        dy2 = _mm(du, w_up[i], "nt", [F32], f"d_y2_{i}", after=token)
        dh1, d_nmlp[i] = _rms_bwd(h1, norm_mlp[i:i + 1], dy2, dh, f"rms_mlp_bwd_{i}")
        g_out = _mm(o2, dh1, "tn", [BF16], f"g_out_{i}")
        do2 = _mm(dh1, w_out[i], "nt", [BF16], f"d_o2_{i}")
        if i % 2 == 0:
            projm, projab, st_all = mix
            dpm, dpab, d_conv[j], d_alog[j], d_dtb[j], d_onorm[j] = _gdn_bwd(
                projm, projab, conv_full[j], gdn_a_log[j:j + 1], gdn_dt_bias[j:j + 1], gdn_onorm[j:j + 1],
                st_all, do2, seqs, f"gdn_bwd_{i}")
            g_main = _mm(y, dpm, "tn", [BF16], f"g_in_{i}")
            g_ab = _mm(y, dpab, "tn", [BF16], f"g_in_ab_{i}")
            g_in = jnp.concatenate([g_main, g_ab[:, :2 * N_HEADS]], axis=1)
            g_in = jnp.transpose(g_in.reshape(d, N_DEV, c_gin), (1, 0, 2))
            in_item = _Item(g_in, (N_DEV, d, c_gin), _slot, _slot)
            dy_ab = _mm(dpab, w_ab[i], "nt", [F32], f"d_y_ab_{i}")
            dy = _mm(dpm, w_in[i], "nt", [F32], f"d_y_{i}", epilogue=lambda acc, e: (acc + e,), extras=(dy_ab,))
        else:
            proj, o_raw, st_all = mix
            dp, d_lb[i], d_gnorm[j] = _hgrn_bwd(proj, lbs[i:i + 1], gnorm_full[j:j + 1], st_all, o_raw, do2,
                                               seqs, f"hgrn_bwd_{i}")
            g_in = _mm(y, dp, "tn", [BF16], f"g_in_{i}")
            in_item = _Item(g_in, (N_DEV, d, c_hin), _cols_of(c_hin), _slot)
            dy = _mm(dp, w_in[i], "nt", [F32], f"d_y_{i}")
        mix_handles[i], token = _exchange_start(
            [[_Item(g_out, (N_DEV, r_out, d), _rows_of(r_out), _slot)], [in_item]], f"scatter_start_mix_{i}")
        dh, d_nmix[i] = _rms_bwd(h_in, norm_mix[i:i + 1], dy, dh1, f"rms_mix_bwd_{i}")
    grad_x = dh.reshape(x.shape)

    def landed(handles, k, layers, after, name):
        return [_exchange_wait(handles[i][k], after, f"scatter_wait_{name}_{i}")[0] for i in layers]

    every, even, odd = range(DEPTH), range(0, DEPTH, 2), range(1, DEPTH, 2)
    upd = {}
    upd["mlp_w_down"] = _adamw_slots(mlp_w_down, landed(mlp_handles, 0, every, dh, "down"), m_mlp_w_down,
                                     v_mlp_w_down, "adamw_mlp_w_down")
    upd["mlp_w_up"] = _adamw_slots(mlp_w_up, landed(mlp_handles, 1, every, upd["mlp_w_down"][1], "up"), m_mlp_w_up,
                                   v_mlp_w_up, "adamw_mlp_w_up")
    upd["hgrn_w_out"] = _adamw_slots(hgrn_w_out, landed(mix_handles, 0, odd, upd["mlp_w_up"][1], "out"),
                                     m_hgrn_w_out, v_hgrn_w_out, "adamw_hgrn_w_out")
    upd["hgrn_w_in"] = _adamw_slots(hgrn_w_in, landed(mix_handles, 1, odd, upd["hgrn_w_out"][1], "in"), m_hgrn_w_in,
                                    v_hgrn_w_in, "adamw_hgrn_w_in")
    upd["gdn_w_out"] = _adamw_slots(gdn_w_out, landed(mix_handles, 0, even, upd["hgrn_w_in"][1], "out"),
                                    m_gdn_w_out, v_gdn_w_out, "adamw_gdn_w_out")
    upd["gdn_w_in"] = _adamw_slots(gdn_w_in, landed(mix_handles, 1, even, upd["gdn_w_out"][1], "in"), m_gdn_w_in,
                                   v_gdn_w_in, "adamw_gdn_w_in")

    def update(name, w, g, m, v):
        shape = w.shape
        c = shape[-1]
        res = _adamw(w.reshape(-1, c), g.reshape(-1, c), m.reshape(-1, c), v.reshape(-1, c), "adamw_" + name)
        return [g.reshape(shape)] + [o.reshape(shape) for o in res]

    dlb_rows = jnp.concatenate(d_lb, axis=0)
    tail = jnp.concatenate(
        [jnp.concatenate(d_onorm, axis=1), jnp.concatenate(d_alog, axis=1), jnp.concatenate(d_dtb, axis=1)], axis=1)
    tail = jnp.pad(tail, ((0, 0), (0, d - tail.shape[1])))
    conv_rows = jnp.stack(d_conv).reshape(-1, d)
    packed = jnp.concatenate(
        [jnp.concatenate(d_nmix, axis=0), jnp.concatenate(d_nmlp, axis=0), d_nf, sq, dlb_rows,
         jnp.concatenate(d_gnorm, axis=0), tail, conv_rows], axis=0)
    pad_rows = (-packed.shape[0]) % 8
    packed = jnp.pad(packed, ((0, pad_rows), (0, 0)))
    tot = _all_reduce_small(packed, "reduce_small")
    r0 = 0
    g_nmix = tot[r0:r0 + DEPTH]; r0 += DEPTH
    g_nmlp = tot[r0:r0 + DEPTH]; r0 += DEPTH
    g_nf = tot[r0]; r0 += 1
    loss = tot[r0, 0]; r0 += 1
    g_lb = _lb_bwd(hgrn_lb_logits, tot[r0:r0 + DEPTH], "lb_bwd"); r0 += DEPTH
    g_gnorm_full = tot[r0:r0 + n_hgrn]; r0 += n_hgrn
    t_row = tot[r0]; r0 += 1
    g_conv_full = tot[r0:r0 + n_gdn * CONV_K * 3].reshape(n_gdn, CONV_K, 3 * d)
    g_onorm = t_row[0:n_gdn * HEAD_DIM].reshape(n_gdn, HEAD_DIM)
    o1 = n_gdn * HEAD_DIM
    g_alog = t_row[o1:o1 + n_gdn * N_HEADS].reshape(n_gdn, N_HEADS)
    g_dtb = t_row[o1 + n_gdn * N_HEADS:o1 + 2 * n_gdn * N_HEADS].reshape(n_gdn, N_HEADS)
    c_gn, c_cv = hgrn_gnorm.shape[1], gdn_conv.shape[2]
    g_gnorm = lax.dynamic_slice_in_dim(g_gnorm_full, me_i * c_gn, c_gn, axis=1)
    g_conv = lax.dynamic_slice_in_dim(g_conv_full, me_i * c_cv, c_cv, axis=2)

    upd["gdn_conv"] = update("gdn_conv", gdn_conv, g_conv, m_gdn_conv, v_gdn_conv)
    upd["gdn_a_log"] = update("gdn_a_log", gdn_a_log, g_alog, m_gdn_a_log, v_gdn_a_log)
    upd["gdn_dt_bias"] = update("gdn_dt_bias", gdn_dt_bias, g_dtb, m_gdn_dt_bias, v_gdn_dt_bias)
    upd["gdn_onorm"] = update("gdn_onorm", gdn_onorm, g_onorm, m_gdn_onorm, v_gdn_onorm)
    upd["hgrn_lb_logits"] = update("hgrn_lb_logits", hgrn_lb_logits, g_lb, m_hgrn_lb_logits, v_hgrn_lb_logits)
    upd["hgrn_gnorm"] = update("hgrn_gnorm", hgrn_gnorm, g_gnorm, m_hgrn_gnorm, v_hgrn_gnorm)
    upd["norm_mix"] = update("norm_mix", norm_mix, g_nmix, m_norm_mix, v_norm_mix)
    upd["norm_mlp"] = update("norm_mlp", norm_mlp, g_nmlp, m_norm_mlp, v_norm_mlp)
    upd["norm_final"] = update("norm_final", norm_final, g_nf, m_norm_final, v_norm_final)

    order = ["gdn_w_in", "gdn_conv", "gdn_a_log", "gdn_dt_bias", "gdn_onorm", "gdn_w_out", "hgrn_w_in",
             "hgrn_lb_logits", "hgrn_gnorm", "hgrn_w_out", "norm_mix", "norm_mlp", "mlp_w_up", "mlp_w_down",
             "norm_final"]
    outs = [loss, grad_x]
    for k in range(4):
        outs += [upd[name][k] for name in order]
    return tuple(outs)
```

```python
import functools

import jax
import jax.numpy as jnp
from jax import lax
from jax.experimental import pallas as pl
from jax.experimental.pallas import tpu as pltpu

F32 = jnp.float32
BF16 = jnp.bfloat16

D_MODEL = 1024
N_HEADS = 8
HEAD_DIM = 128
CHUNK = 64
SUB = 16
N_SUB = CHUNK // SUB
CONV_K = 4
HALO = 8
EPS = 1e-6
DEPTH = 4
N_DEV = 8
GDN_MAIN = 4 * D_MODEL
GDN_IN = GDN_MAIN + 2 * N_HEADS
AB_PAD = 128
HEAD_GROUP = 8

ADAM_LR = 0.001
ADAM_B1 = 0.9
ADAM_B2 = 0.999
ADAM_EPS = 1e-08
ADAM_WD = 0.01
ADAM_STEP = 10

VMEM_LIMIT = 56 * 1024 * 1024
MM_TILE = 1024
MM_VMEM_BUDGET = 40 * 1024 * 1024

_DIMS = {
    "nn": (((1,), (0,)), ((), ())),
    "nt": (((1,), (1,)), ((), ())),
    "tn": (((0,), (0,)), ((), ())),
}


def _parts(x, n):
    out = []
    r = x.astype(F32)
    for i in range(n):
        p = r.astype(BF16)
        out.append(p)
        if i + 1 < n:
            r = r - p.astype(F32)
    return out


def _dot_raw(a, b, mode, na, nb):
    ap, bp = _parts(a, na), _parts(b, nb)
    nmax = max(na, nb)
    acc = None
    for i, xa in enumerate(ap):
        for j, xb in enumerate(bp):
            if i + j < nmax:
                t = lax.dot_general(xa, xb, _DIMS[mode], preferred_element_type=F32)
                acc = t if acc is None else acc + t
    return acc


@functools.partial(jax.custom_vjp, nondiff_argnums=(2, 3, 4))
def _dot(a, b, mode, na, nb):
    return _dot_raw(a, b, mode, na, nb)


def _dot_fwd(a, b, mode, na, nb):
    return _dot_raw(a, b, mode, na, nb), (a, b)


def _dot_bwd(mode, na, nb, res, ct):
    a, b = res
    nc = max(na, nb)
    if mode == "nn":
        da = _dot_raw(ct, b, "nt", nc, nb)
        db = _dot_raw(a, ct, "tn", na, nc)
    elif mode == "nt":
        da = _dot_raw(ct, b, "nn", nc, nb)
        db = _dot_raw(ct, a, "tn", nc, na)
    else:
        da = _dot_raw(b, ct, "nt", nb, nc)
        db = _dot_raw(a, ct, "nn", na, nc)
    return da, db


_dot.defvjp(_dot_fwd, _dot_bwd)


def _iota2(shape, dim):
    return lax.broadcasted_iota(jnp.int32, shape, dim)


def _tril_f32(n):
    return (_iota2((n, n), 0) >= _iota2((n, n), 1)).astype(F32)


def _cumsum_rows(g):
    return _dot(_tril_f32(g.shape[0]), g, "nn", 1, 3)


def _inv_unit_lower(L):
    n = L.shape[0]
    eye = (_iota2((n, n), 0) == _iota2((n, n), 1)).astype(F32)
    p = -L
    t = eye + p
    k = 2
    while k < n:
        p = _dot_raw(p, p, "nn", 2, 2)
        t = t + _dot_raw(t, p, "nn", 2, 2)
        k *= 2
    return t


@jax.custom_vjp
def _solve_unit_lower(L, rhs):
    return _dot_raw(_inv_unit_lower(L), rhs, "nn", 2, 2)


def _solve_fwd(L, rhs):
    t = _inv_unit_lower(L)
    sol = _dot_raw(t, rhs, "nn", 2, 2)
    return sol, (t, sol)


def _solve_bwd(res, ct):
    t, sol = res
    y = _dot_raw(t, ct, "tn", 2, 2)
    return -_dot_raw(y, sol, "nt", 2, 2), y


_solve_unit_lower.defvjp(_solve_fwd, _solve_bwd)


def _softplus(x):
    return jnp.maximum(x, 0.0) + jnp.log1p(jnp.exp(-jnp.abs(x)))


def _rms(x, w):
    return x * lax.rsqrt(jnp.mean(x * x, axis=-1, keepdims=True) + EPS) * w


def _hg_pre(p, lb):
    qraw = p[:, 0:D_MODEL]
    f = p[:, D_MODEL:2 * D_MODEL]
    v = p[:, 2 * D_MODEL:3 * D_MODEL]
    g = jnp.log(lb + (1.0 - lb) * jax.nn.sigmoid(f))
    k = (1.0 - lb) * jax.nn.sigmoid(-f)
    q = jax.nn.silu(qraw) * (HEAD_DIM ** -0.5)
    return q, k, v, g, _cumsum_rows(g)


def _hg_head(st, q, k, v, g, gc):
    i3 = lax.broadcasted_iota(jnp.int32, (SUB, SUB, HEAD_DIM), 0)
    j3 = lax.broadcasted_iota(jnp.int32, (SUB, SUB, HEAD_DIM), 1)
    rows = []
    for s in range(N_SUB):
        lo = s * SUB
        qs, ks, vs, gs = q[lo:lo + SUB], k[lo:lo + SUB], v[lo:lo + SUB], gc[lo:lo + SUB]
        dec = jnp.exp(jnp.where(i3 >= j3, gs[:, None, :] - gs[None, :, :], -jnp.inf))
        a_diag = jnp.sum(qs[:, None, :] * ks[None, :, :] * dec, axis=-1)
        o_s = _dot(a_diag, vs, "nn", 1, 1)
        if s > 0:
            gb = gc[lo:lo + 1] - g[lo:lo + 1]
            q_off = qs * jnp.exp(gs - gb)
            k_off = k[0:lo] * jnp.exp(gb - gc[0:lo])
            a_off = _dot(q_off, k_off, "nt", 1, 1)
            o_s = o_s + _dot(a_off, v[0:lo], "nn", 1, 1)
        rows.append(o_s)
    o = jnp.concatenate(rows, axis=0) + _dot(q * jnp.exp(gc), st, "nt", 1, 1)
    g_last = gc[CHUNK - 1:CHUNK]
    st_new = st * jnp.exp(g_last) + _dot(v, k * jnp.exp(g_last - gc), "tn", 1, 1)
    return o, st_new


def _hg_post(o, gate, gw):
    return _rms(o, gw) * jax.nn.silu(gate)


def _gd_pre(xp, a, b, cw, alog, dtb):
    off = HALO - (CONV_K - 1)
    y = cw[0:1] * xp[off:off + CHUNK]
    for kk in range(1, CONV_K):
        y = y + cw[kk:kk + 1] * xp[off + kk:off + kk + CHUNK]
    c = jax.nn.silu(y)
    beta = jax.nn.sigmoid(b)
    g = -jnp.exp(alog) * _softplus(a + dtb)
    expand = (_iota2((N_HEADS, D_MODEL), 1) // HEAD_DIM == _iota2((N_HEADS, D_MODEL), 0)).astype(F32)
    return c, _dot(beta, expand, "nn", 3, 1), _dot(g, expand, "nn", 3, 1)


def _gd_head(st, q, k, v, beta, g, gate, onw):
    q = q * lax.rsqrt(jnp.sum(q * q, axis=-1, keepdims=True) + EPS) * (HEAD_DIM ** -0.5)
    k = k * lax.rsqrt(jnp.sum(k * k, axis=-1, keepdims=True) + EPS)
    gc = _cumsum_rows(g)
    ri = _iota2((CHUNK, CHUNK), 0)
    ci = _iota2((CHUNK, CHUNK), 1)
    diff = gc[:, 0:CHUNK] - gc.T[0:CHUNK, :]
    decay = jnp.exp(jnp.where(ri >= ci, diff, -jnp.inf))
    kb = k * beta
    egc = jnp.exp(gc)
    L = jnp.where(ri > ci, _dot(kb, k, "nt", 1, 1) * decay, 0.0)
    sol = _solve_unit_lower(L, jnp.concatenate([v * beta, kb * egc], axis=1))
    u = sol[:, 0:HEAD_DIM]
    w = sol[:, HEAD_DIM:2 * HEAD_DIM]
    a_qk = jnp.where(ri >= ci, _dot(q, k, "nt", 1, 1) * decay, 0.0)
    g_last = gc[CHUNK - 1:CHUNK]
    v_new = u - _dot(w, st, "nt", 1, 1)
    o = _dot(q * egc, st, "nt", 1, 1) + _dot(a_qk, v_new, "nn", 1, 1)
    st_new = st * jnp.exp(g_last) + _dot(v_new, k * jnp.exp(g_last - gc), "tn", 1, 1)
    return _rms(o, onw) * jax.nn.silu(gate), st_new


def _params(*sem):
    return pltpu.CompilerParams(dimension_semantics=sem, vmem_limit_bytes=VMEM_LIMIT)


def _tile(n, pref):
    t = min(n, pref)
    assert n % t == 0, (n, pref)
    return t


def _mm_tiles(m, n, k, a_size, b_size, tile_sizes):
    tm, tn, tk = _tile(m, MM_TILE), _tile(n, MM_TILE), k

    def need(tm, tn, tk):
        acc = 4 * tm * tn * (2 if tk < k else 1)
        return 2 * (tm * tk * a_size + tk * tn * b_size + tm * tn * sum(tile_sizes)) + acc

    while need(tm, tn, tk) > MM_VMEM_BUDGET:
        if tk > 2048 or (tk > 512 and tm <= 512):
            tk //= 2
        else:
            tm //= 2
    return tm, tn, tk


def _mm(a, b, mode, out_dtypes, name, epilogue=None, extras=(), after=None):
    if mode == "nn":
        (m, k), (k2, n) = a.shape, b.shape
    elif mode == "nt":
        (m, k), (n, k2) = a.shape, b.shape
    else:
        (k, m), (k2, n) = a.shape, b.shape
    assert k == k2, (a.shape, b.shape, mode)
    tm, tn, tk = _mm_tiles(m, n, k, a.dtype.itemsize, b.dtype.itemsize,
                           [e.dtype.itemsize for e in extras] + [jnp.dtype(dt).itemsize for dt in out_dtypes])
    nk = k // tk
    ne, no, nafter = len(extras), len(out_dtypes), int(after is not None)
    if epilogue is None:
        epilogue = lambda acc: (acc,)

    def body(*refs):
        a_ref, b_ref = refs[0], refs[1]
        ex = refs[2:2 + ne]
        outs = refs[2 + ne + nafter:2 + ne + nafter + no]
        part = lax.dot_general(a_ref[...].astype(BF16), b_ref[...].astype(BF16), _DIMS[mode],
                               preferred_element_type=F32)

        def finish(acc):
            for o_ref, val in zip(outs, epilogue(acc, *[e[...] for e in ex])):
                o_ref[...] = val.astype(o_ref.dtype)

        if nk == 1:
            finish(part)
        else:
            acc_ref = refs[-1]
            kk = pl.program_id(2)

            @pl.when(kk == 0)
            def _():
                acc_ref[...] = part

            @pl.when(kk > 0)
            def _():
                acc_ref[...] += part

            @pl.when(kk == nk - 1)
            def _():
                finish(acc_ref[...])

    if mode == "tn":
        a_spec = pl.BlockSpec((tk, tm), lambda i, j, kk: (kk, i))
    else:
        a_spec = pl.BlockSpec((tm, tk), lambda i, j, kk: (i, kk))
    if mode == "nt":
        b_spec = pl.BlockSpec((tn, tk), lambda i, j, kk: (j, kk))
    else:
        b_spec = pl.BlockSpec((tk, tn), lambda i, j, kk: (kk, j))
    o_spec = pl.BlockSpec((tm, tn), lambda i, j, kk: (i, j))
    res = pl.pallas_call(
        body,
        name=name,
        grid=(m // tm, n // tn, nk),
        in_specs=[a_spec, b_spec] + [o_spec] * ne + [pl.BlockSpec(memory_space=pl.ANY)] * nafter,
        out_specs=[o_spec] * no,
        out_shape=[jax.ShapeDtypeStruct((m, n), dt) for dt in out_dtypes],
        scratch_shapes=[pltpu.VMEM((tm, tn), F32)] if nk > 1 else [],
        compiler_params=_params("parallel", "parallel", "arbitrary"),
    )(a, b, *extras, *([after] if nafter else []))
    return res[0] if no == 1 else res


def _rms_fwd(x, w, name, tm=512):
    n, d = x.shape
    tm = _tile(n, tm)

    def body(x_ref, w_ref, y_ref):
        y_ref[...] = _rms(x_ref[...], w_ref[...]).astype(y_ref.dtype)

    return pl.pallas_call(
        body, name=name, grid=(n // tm,),
        in_specs=[pl.BlockSpec((tm, d), lambda i: (i, 0)), pl.BlockSpec((1, d), lambda i: (0, 0))],
        out_specs=pl.BlockSpec((tm, d), lambda i: (i, 0)),
        out_shape=jax.ShapeDtypeStruct((n, d), BF16),
        compiler_params=_params("arbitrary"),
    )(x, w)


def _rms_bwd(x, w, dy, dres, name, tm=512):
    n, d = x.shape
    tm = _tile(n, tm)

    def body(x_ref, w_ref, dy_ref, dres_ref, dx_ref, dxb_ref, dw_ref):
        _, vjp = jax.vjp(_rms, x_ref[...], w_ref[...])
        dx, dw = vjp(dy_ref[...].astype(F32))
        dx = dres_ref[...] + dx
        dx_ref[...] = dx
        dxb_ref[...] = dx.astype(dxb_ref.dtype)

        @pl.when(pl.program_id(0) == 0)
        def _():
            dw_ref[...] = dw

        @pl.when(pl.program_id(0) > 0)
        def _():
            dw_ref[...] += dw

    row = pl.BlockSpec((tm, d), lambda i: (i, 0))
    vec = pl.BlockSpec((1, d), lambda i: (0, 0))
    return pl.pallas_call(
        body, name=name, grid=(n // tm,),
        in_specs=[row, vec, row, row],
        out_specs=[row, row, vec],
        out_shape=[jax.ShapeDtypeStruct((n, d), F32), jax.ShapeDtypeStruct((n, d), BF16),
                   jax.ShapeDtypeStruct((1, d), F32)],
        compiler_params=_params("arbitrary"),
    )(x, w, dy, dres)


def _loss_head(h, w, target, name, tm=512):
    n, d = h.shape
    tm = _tile(n, tm)

    def body(h_ref, w_ref, t_ref, dh_ref, dhb_ref, dw_ref, sq_ref):
        y, vjp = jax.vjp(_rms, h_ref[...], w_ref[...])
        err = y - t_ref[...]
        dh, dw = vjp(err * (1.0 / d))
        dh_ref[...] = dh
        dhb_ref[...] = dh.astype(dhb_ref.dtype)
        sq = jnp.sum(err * err, axis=0, keepdims=True)

        @pl.when(pl.program_id(0) == 0)
        def _():
            dw_ref[...] = dw
            sq_ref[...] = sq

        @pl.when(pl.program_id(0) > 0)
        def _():
            dw_ref[...] += dw
            sq_ref[...] += sq

        @pl.when(pl.program_id(0) == n // tm - 1)
        def _():
            total = jnp.sum(sq_ref[...], axis=1, keepdims=True) * (0.5 / d)
            sq_ref[...] = jnp.broadcast_to(total, sq_ref.shape)

    row = pl.BlockSpec((tm, d), lambda i: (i, 0))
    vec = pl.BlockSpec((1, d), lambda i: (0, 0))
    return pl.pallas_call(
        body, name=name, grid=(n // tm,),
        in_specs=[row, vec, row],
        out_specs=[row, row, vec, vec],
        out_shape=[jax.ShapeDtypeStruct((n, d), F32), jax.ShapeDtypeStruct((n, d), BF16),
                   jax.ShapeDtypeStruct((1, d), F32), jax.ShapeDtypeStruct((1, d), F32)],
        compiler_params=_params("arbitrary"),
    )(h, w, target)


def _lower_bounds(logits):
    sm = jax.nn.softmax(logits, axis=0)
    rows = [sm[0:1] * 0.0]
    for r in range(1, DEPTH):
        rows.append(rows[-1] + sm[r:r + 1])
    return jnp.concatenate(rows, axis=0)


def _lb_fwd(logits, name):
    def body(l_ref, o_ref):
        o_ref[...] = _lower_bounds(l_ref[...])

    return pl.pallas_call(body, name=name, out_shape=jax.ShapeDtypeStruct(logits.shape, F32))(logits)


def _lb_bwd(logits, dlb, name):
    def body(l_ref, d_ref, o_ref):
        _, vjp = jax.vjp(_lower_bounds, l_ref[...])
        (o_ref[...],) = vjp(d_ref[...])

    return pl.pallas_call(body, name=name, out_shape=jax.ShapeDtypeStruct(logits.shape, F32))(logits, dlb)


def _head_slice(h):
    if isinstance(h, int):
        return pl.ds(h * HEAD_DIM, HEAD_DIM)
    return pl.ds(pl.multiple_of(h * HEAD_DIM, HEAD_DIM), HEAD_DIM)


def _head_groups(group_body):
    if HEAD_GROUP == N_HEADS:
        group_body(list(range(N_HEADS)))
        return

    def trip(i, carry):
        group_body([i * HEAD_GROUP + t for t in range(HEAD_GROUP)])
        return carry

    lax.fori_loop(0, N_HEADS // HEAD_GROUP, trip, 0)


def _stack_heads(ref, hs, first=0):
    return jnp.stack([ref[:, _head_slice(h + first)] for h in hs])


def _unstack_heads(ref, hs, val, first=0):
    for t, h in enumerate(hs):
        ref[:, _head_slice(h + first)] = val[t].astype(ref.dtype)


_GD_HEADS = jax.vmap(_gd_head, in_axes=(0, 0, 0, 0, 0, 0, 0, None))


def _hgrn_fwd(proj, lb, gw, seqs, name):
    n = proj.shape[0]
    nc = n // seqs // CHUNK
    d = D_MODEL

    def body(p_ref, lb_ref, gw_ref, o2_ref, o_ref, st_all_ref, st_sc, q_sc, k_sc, v_sc, g_sc, gc_sc):
        @pl.when(pl.program_id(1) == 0)
        def _():
            st_sc[...] = jnp.zeros_like(st_sc)

        q_sc[...], k_sc[...], v_sc[...], g_sc[...], gc_sc[...] = _hg_pre(p_ref[:, 0:3 * d], lb_ref[...])
        st_all_ref[0] = st_sc[...]

        def group(hs):
            sts = pl.ds(hs[0], len(hs))
            o, st_new = jax.vmap(_hg_head)(st_sc[sts], *[_stack_heads(r, hs) for r in (q_sc, k_sc, v_sc, g_sc, gc_sc)])
            _unstack_heads(o_ref, hs, o)
            st_sc[sts] = st_new

        _head_groups(group)
        o2_ref[...] = _hg_post(o_ref[...], p_ref[:, 3 * d:4 * d], gw_ref[...]).astype(o2_ref.dtype)

    idx = lambda b, c: (b * nc + c, 0)
    vec = pl.BlockSpec((1, d), lambda b, c: (0, 0))
    act = pl.BlockSpec((CHUNK, d), idx)
    return pl.pallas_call(
        body, name=name, grid=(seqs, nc),
        in_specs=[pl.BlockSpec((CHUNK, 4 * d), idx), vec, vec],
        out_specs=[act, act, pl.BlockSpec((1, N_HEADS, HEAD_DIM, HEAD_DIM), lambda b, c: (b * nc + c, 0, 0, 0))],
        out_shape=[jax.ShapeDtypeStruct((n, d), BF16), jax.ShapeDtypeStruct((n, d), F32),
                   jax.ShapeDtypeStruct((n // CHUNK, N_HEADS, HEAD_DIM, HEAD_DIM), F32)],
        scratch_shapes=[pltpu.VMEM((N_HEADS, HEAD_DIM, HEAD_DIM), F32)] + [pltpu.VMEM((CHUNK, d), F32)] * 5,
        compiler_params=_params("arbitrary", "arbitrary"),
    )(proj, lb, gw)


def _hgrn_bwd(proj, lb, gw, st_all, o, do2, seqs, name):
    n = proj.shape[0]
    nc = n // seqs // CHUNK
    d = D_MODEL

    def body(p_ref, lb_ref, gw_ref, st_all_ref, o_ref, do2_ref, dp_ref, dlb_ref, dgw_ref,
             dst_sc, q_sc, k_sc, v_sc, g_sc, gc_sc, do_sc, dq_sc, dk_sc, dv_sc, dg_sc, dgc_sc):
        first = (pl.program_id(0) == 0) & (pl.program_id(1) == 0)

        @pl.when(pl.program_id(1) == 0)
        def _():
            dst_sc[...] = jnp.zeros_like(dst_sc)

        pre_out, pre_vjp = jax.vjp(_hg_pre, p_ref[:, 0:3 * d], lb_ref[...])
        q_sc[...], k_sc[...], v_sc[...], g_sc[...], gc_sc[...] = pre_out
        _, post_vjp = jax.vjp(_hg_post, o_ref[...], p_ref[:, 3 * d:4 * d], gw_ref[...])
        do_sc[...], dgate, dgw = post_vjp(do2_ref[...].astype(F32))
        dp_ref[:, 3 * d:4 * d] = dgate.astype(dp_ref.dtype)

        def group(hs):
            sts = pl.ds(hs[0], len(hs))
            _, vjp = jax.vjp(jax.vmap(_hg_head), st_all_ref[0, sts],
                             *[_stack_heads(r, hs) for r in (q_sc, k_sc, v_sc, g_sc, gc_sc)])
            grads = vjp((_stack_heads(do_sc, hs), dst_sc[sts]))
            dst_sc[sts] = grads[0]
            for r, val in zip((dq_sc, dk_sc, dv_sc, dg_sc, dgc_sc), grads[1:]):
                _unstack_heads(r, hs, val)

        _head_groups(group)
        dp, dlb = pre_vjp((dq_sc[...], dk_sc[...], dv_sc[...], dg_sc[...], dgc_sc[...]))
        dp_ref[:, 0:3 * d] = dp.astype(dp_ref.dtype)

        @pl.when(first)
        def _():
            dlb_ref[...] = dlb
            dgw_ref[...] = dgw

        @pl.when(jnp.logical_not(first))
        def _():
            dlb_ref[...] += dlb
            dgw_ref[...] += dgw

    idx = lambda b, c: (b * nc + nc - 1 - c, 0)
    vec = pl.BlockSpec((1, d), lambda b, c: (0, 0))
    act = pl.BlockSpec((CHUNK, d), idx)
    wide = pl.BlockSpec((CHUNK, 4 * d), idx)
    return pl.pallas_call(
        body, name=name, grid=(seqs, nc),
        in_specs=[wide, vec, vec,
                  pl.BlockSpec((1, N_HEADS, HEAD_DIM, HEAD_DIM), lambda b, c: (b * nc + nc - 1 - c, 0, 0, 0)),
                  act, act],
        out_specs=[wide, vec, vec],
        out_shape=[jax.ShapeDtypeStruct((n, 4 * d), BF16), jax.ShapeDtypeStruct((1, d), F32),
                   jax.ShapeDtypeStruct((1, d), F32)],
        scratch_shapes=[pltpu.VMEM((N_HEADS, HEAD_DIM, HEAD_DIM), F32)] + [pltpu.VMEM((CHUNK, d), F32)] * 11,
        compiler_params=_params("arbitrary", "arbitrary"),
    )(proj, lb, gw, st_all, o, do2)


def _gd_xp(halo_ref, p_ref, first_chunk):
    halo = jnp.where(first_chunk, 0.0, halo_ref[...])
    return jnp.concatenate([halo, p_ref[:, 0:3 * D_MODEL]], axis=0)


def _gdn_fwd(projm, projab, cw, alog, dtb, onw, seqs, name):
    n = projm.shape[0]
    nc = n // seqs // CHUNK
    d = D_MODEL
    per_halo = CHUNK // HALO

    def body(p_ref, halo_ref, ab_ref, cw_ref, alog_ref, dtb_ref, onw_ref, o2_ref, st_all_ref,
             st_sc, c_sc, beta_sc, g_sc):
        @pl.when(pl.program_id(1) == 0)
        def _():
            st_sc[...] = jnp.zeros_like(st_sc)

        xp = _gd_xp(halo_ref, p_ref, pl.program_id(1) == 0)
        c_sc[...], beta_sc[...], g_sc[...] = _gd_pre(
            xp, ab_ref[:, 0:N_HEADS], ab_ref[:, N_HEADS:2 * N_HEADS], cw_ref[...], alog_ref[...], dtb_ref[...])
        st_all_ref[0] = st_sc[...]

        def group(hs):
            sts = pl.ds(hs[0], len(hs))
            o2, st_new = _GD_HEADS(
                st_sc[sts], _stack_heads(c_sc, hs), _stack_heads(c_sc, hs, N_HEADS), _stack_heads(c_sc, hs, 2 * N_HEADS),
                _stack_heads(beta_sc, hs), _stack_heads(g_sc, hs), _stack_heads(p_ref, hs, 3 * N_HEADS), onw_ref[...])
            _unstack_heads(o2_ref, hs, o2)
            st_sc[sts] = st_new

        _head_groups(group)

    idx = lambda b, c: (b * nc + c, 0)
    const = lambda b, c: (0, 0)
    return pl.pallas_call(
        body, name=name, grid=(seqs, nc),
        in_specs=[pl.BlockSpec((CHUNK, 4 * d), idx),
                  pl.BlockSpec((HALO, 3 * d), lambda b, c: (jnp.maximum((b * nc + c) * per_halo - 1, 0), 0)),
                  pl.BlockSpec((CHUNK, AB_PAD), idx),
                  pl.BlockSpec((CONV_K, 3 * d), const), pl.BlockSpec((1, N_HEADS), const),
                  pl.BlockSpec((1, N_HEADS), const), pl.BlockSpec((1, HEAD_DIM), const)],
        out_specs=[pl.BlockSpec((CHUNK, d), idx),
                   pl.BlockSpec((1, N_HEADS, HEAD_DIM, HEAD_DIM), lambda b, c: (b * nc + c, 0, 0, 0))],
        out_shape=[jax.ShapeDtypeStruct((n, d), BF16),
                   jax.ShapeDtypeStruct((n // CHUNK, N_HEADS, HEAD_DIM, HEAD_DIM), F32)],
        scratch_shapes=[pltpu.VMEM((N_HEADS, HEAD_DIM, HEAD_DIM), F32), pltpu.VMEM((CHUNK, 3 * d), F32),
                        pltpu.VMEM((CHUNK, d), F32), pltpu.VMEM((CHUNK, d), F32)],
        compiler_params=_params("arbitrary", "arbitrary"),
    )(projm, projm, projab, cw, alog, dtb, onw)


def _gdn_bwd(projm, projab, cw, alog, dtb, onw, st_all, do2, seqs, name):
    n = projm.shape[0]
    nc = n // seqs // CHUNK
    d = D_MODEL
    per_halo = CHUNK // HALO

    def body(p_ref, halo_ref, ab_ref, cw_ref, alog_ref, dtb_ref, onw_ref, st_all_ref, do2_ref,
             dp_ref, dab_ref, dcw_ref, dalog_ref, ddtb_ref, donw_ref,
             dst_sc, dhalo_sc, c_sc, beta_sc, g_sc, dc_sc, dbeta_sc, dg_sc, donw_sc):
        step = pl.program_id(1)
        first = (pl.program_id(0) == 0) & (step == 0)

        @pl.when(step == 0)
        def _():
            dst_sc[...] = jnp.zeros_like(dst_sc)
            dhalo_sc[...] = jnp.zeros_like(dhalo_sc)

        donw_sc[...] = jnp.zeros_like(donw_sc)
        xp = _gd_xp(halo_ref, p_ref, step == nc - 1)
        pre_out, pre_vjp = jax.vjp(_gd_pre, xp, ab_ref[:, 0:N_HEADS], ab_ref[:, N_HEADS:2 * N_HEADS],
                                   cw_ref[...], alog_ref[...], dtb_ref[...])
        c_sc[...], beta_sc[...], g_sc[...] = pre_out

        def group(hs):
            sts = pl.ds(hs[0], len(hs))
            _, vjp = jax.vjp(
                _GD_HEADS, st_all_ref[0, sts], _stack_heads(c_sc, hs), _stack_heads(c_sc, hs, N_HEADS),
                _stack_heads(c_sc, hs, 2 * N_HEADS), _stack_heads(beta_sc, hs), _stack_heads(g_sc, hs),
                _stack_heads(p_ref, hs, 3 * N_HEADS), onw_ref[...])
            dst, dq, dk, dv, dbeta, dg, dgate, donw = vjp((_stack_heads(do2_ref, hs).astype(F32), dst_sc[sts]))
            dst_sc[sts] = dst
            _unstack_heads(dc_sc, hs, dq)
            _unstack_heads(dc_sc, hs, dk, N_HEADS)
            _unstack_heads(dc_sc, hs, dv, 2 * N_HEADS)
            _unstack_heads(dbeta_sc, hs, dbeta)
            _unstack_heads(dg_sc, hs, dg)
            _unstack_heads(dp_ref, hs, dgate, 3 * N_HEADS)
            donw_sc[...] += donw

        _head_groups(group)
        dxp, da, db, dcw, dalog, ddtb = pre_vjp((dc_sc[...], dbeta_sc[...], dg_sc[...]))
        dqkv = jnp.concatenate([dxp[HALO:CHUNK], dxp[CHUNK:HALO + CHUNK] + dhalo_sc[...]], axis=0)
        dp_ref[:, 0:3 * d] = dqkv.astype(dp_ref.dtype)
        dhalo_sc[...] = dxp[0:HALO]
        dab_ref[...] = jnp.concatenate(
            [da, db, jnp.zeros((CHUNK, AB_PAD - 2 * N_HEADS), F32)], axis=1).astype(dab_ref.dtype)

        @pl.when(first)
        def _():
            dcw_ref[...] = dcw
            dalog_ref[...] = dalog
            ddtb_ref[...] = ddtb
            donw_ref[...] = donw_sc[...]

        @pl.when(jnp.logical_not(first))
        def _():
            dcw_ref[...] += dcw
            dalog_ref[...] += dalog
            ddtb_ref[...] += ddtb
            donw_ref[...] += donw_sc[...]

    rev = lambda b, c: b * nc + nc - 1 - c
    idx = lambda b, c: (rev(b, c), 0)
    const = lambda b, c: (0, 0)
    small = [pl.BlockSpec((CONV_K, 3 * d), const), pl.BlockSpec((1, N_HEADS), const),
             pl.BlockSpec((1, N_HEADS), const), pl.BlockSpec((1, HEAD_DIM), const)]
    return pl.pallas_call(
        body, name=name, grid=(seqs, nc),
        in_specs=[pl.BlockSpec((CHUNK, 4 * d), idx),
                  pl.BlockSpec((HALO, 3 * d), lambda b, c: (jnp.maximum(rev(b, c) * per_halo - 1, 0), 0)),
                  pl.BlockSpec((CHUNK, AB_PAD), idx)] + small + [
                  pl.BlockSpec((1, N_HEADS, HEAD_DIM, HEAD_DIM), lambda b, c: (rev(b, c), 0, 0, 0)),
                  pl.BlockSpec((CHUNK, d), idx)],
        out_specs=[pl.BlockSpec((CHUNK, 4 * d), idx), pl.BlockSpec((CHUNK, AB_PAD), idx)] + small,
        out_shape=[jax.ShapeDtypeStruct((n, 4 * d), BF16), jax.ShapeDtypeStruct((n, AB_PAD), BF16),
                   jax.ShapeDtypeStruct((CONV_K, 3 * d), F32), jax.ShapeDtypeStruct((1, N_HEADS), F32),
                   jax.ShapeDtypeStruct((1, N_HEADS), F32), jax.ShapeDtypeStruct((1, HEAD_DIM), F32)],
        scratch_shapes=[pltpu.VMEM((N_HEADS, HEAD_DIM, HEAD_DIM), F32), pltpu.VMEM((HALO, 3 * d), F32),
                        pltpu.VMEM((CHUNK, 3 * d), F32), pltpu.VMEM((CHUNK, d), F32), pltpu.VMEM((CHUNK, d), F32),
                        pltpu.VMEM((CHUNK, 3 * d), F32), pltpu.VMEM((CHUNK, d), F32), pltpu.VMEM((CHUNK, d), F32),
                        pltpu.VMEM((1, HEAD_DIM), F32)],
        compiler_params=_params("arbitrary", "arbitrary"),
    )(projm, projm, projab, cw, alog, dtb, onw, st_all, do2)


def _adam_update(w, g, m, v):
    b1c = 1.0 - ADAM_B1 ** ADAM_STEP
    b2c = 1.0 - ADAM_B2 ** ADAM_STEP
    m_new = ADAM_B1 * m + (1.0 - ADAM_B1) * g
    v_new = ADAM_B2 * v + (1.0 - ADAM_B2) * (g * g)
    delta = -ADAM_LR * ((m_new / b1c) / (jnp.sqrt(v_new / b2c) + ADAM_EPS) + ADAM_WD * w)
    return delta, m_new, v_new


def _adamw(w, g, m, v, name, tr=256):
    r, c = w.shape
    tr = _tile(r, tr)

    def body(w_ref, g_ref, m_ref, v_ref, d_ref, mo_ref, vo_ref):
        d_ref[...], mo_ref[...], vo_ref[...] = _adam_update(w_ref[...], g_ref[...], m_ref[...], v_ref[...])

    blk = pl.BlockSpec((tr, c), lambda i: (i, 0))
    return pl.pallas_call(
        body, name=name, grid=(r // tr,),
        in_specs=[blk] * 4, out_specs=[blk] * 3,
        out_shape=[jax.ShapeDtypeStruct((r, c), F32)] * 3,
        compiler_params=_params("arbitrary"),
    )(w, g, m, v)


def _adamw_slots(w, slot_bufs, m, v, name, tr=256):
    nl, r, c = w.shape
    tr = _tile(r, tr)

    def body(*refs):
        w_ref = refs[0]
        g_refs = refs[1:1 + nl]
        m_ref, v_ref, go_ref, d_ref, mo_ref, vo_ref = refs[1 + nl:]
        for k in range(nl):
            @pl.when(pl.program_id(0) == k)
            def _(k=k):
                g = g_refs[k][0].astype(F32)
                for s in range(1, N_DEV):
                    g = g + g_refs[k][s].astype(F32)
                go_ref[0] = g

        d_ref[0], mo_ref[0], vo_ref[0] = _adam_update(w_ref[0], go_ref[0], m_ref[0], v_ref[0])

    blk = pl.BlockSpec((1, tr, c), lambda l, i: (l, i, 0))
    g_specs = [pl.BlockSpec((N_DEV, tr, c), lambda l, i, k=k: (0, jnp.where(l == k, i, 0), 0)) for k in range(nl)]
    return pl.pallas_call(
        body, name=name, grid=(nl, r // tr),
        in_specs=[blk] + g_specs + [blk, blk], out_specs=[blk] * 4,
        out_shape=[jax.ShapeDtypeStruct((nl, r, c), F32)] * 4,
        compiler_params=_params("arbitrary", "arbitrary"),
    )(w, *slot_bufs, m, v)


def _mesh_pos():
    return lax.axis_index("x"), lax.axis_index("y"), lax.axis_index("c")


def _flip(pos, p):
    x, y, c = pos
    return ((1 - x) if p & 4 else x, (1 - y) if p & 2 else y, (1 - c) if p & 1 else c)


def _lin(pos):
    return 4 * pos[0] + 2 * pos[1] + pos[2]


_HBM = pl.BlockSpec(memory_space=pltpu.HBM)
_SEM = pl.BlockSpec(memory_space=pltpu.SEMAPHORE)
_DATAFLOW = pltpu.SideEffectType.DATAFLOW_SIDE_EFFECTING


class _Item:
    def __init__(self, src, land_shape, src_pick, dst_pick):
        self.src, self.land_shape, self.src_pick, self.dst_pick = src, land_shape, src_pick, dst_pick


def _remote_copies(items, src, land, send_sem, recv_sem, me, arriving):
    me_i = _lin(me)
    out = []
    for it, s_ref, l_ref in zip(items, src, land):
        for p in range(1, N_DEV):
            peer = _flip(me, p)
            out.append(pltpu.make_async_remote_copy(
                src_ref=it.src_pick(s_ref, _lin(peer)),
                dst_ref=it.dst_pick(l_ref, _lin(peer) if arriving else me_i),
                send_sem=send_sem, recv_sem=recv_sem, device_id=peer, device_id_type=pl.DeviceIdType.MESH))
    return out


def _own_copies(items, src, land, sem, me):
    me_i = _lin(me)
    return [pltpu.make_async_copy(it.src_pick(s_ref, me_i), it.dst_pick(l_ref, me_i), sem)
            for it, s_ref, l_ref in zip(items, src, land)]


def _exchange_start(groups, name):
    items = [it for g in groups for it in g]
    n, ng = len(items), len(groups)
    first = [sum(len(g) for g in groups[:gi]) for gi in range(ng)]

    def body(*refs):
        src, land = refs[0:n], refs[n:2 * n]
        send_sems, recv_sems = refs[2 * n:2 * n + ng], refs[2 * n + ng:2 * n + 2 * ng]
        token = refs[4 * n + 2 * ng]
        me = _mesh_pos()
        for gi, g in enumerate(groups):
            sl = slice(first[gi], first[gi] + len(g))
            for cp in _remote_copies(g, src[sl], land[sl], send_sems[gi], recv_sems[gi], me, arriving=False):
                cp.start()
            for cp in _own_copies(g, src[sl], land[sl], recv_sems[gi], me):
                cp.start()
        token[...] = jnp.zeros_like(token)

    srcs = [pltpu.with_memory_space_constraint(it.src, pltpu.HBM) for it in items]
    lands = [pltpu.with_memory_space_constraint(lax.empty(it.land_shape, it.src.dtype), pltpu.HBM) for it in items]
    res = pl.pallas_call(
        body, name=name,
        out_shape=([pltpu.SemaphoreType.DMA(())] * (2 * ng)
                   + [pltpu.HBM(it.src.shape, it.src.dtype) for it in items]
                   + [pltpu.HBM(it.land_shape, it.src.dtype) for it in items]
                   + [jax.ShapeDtypeStruct((8, 128), F32)]),
        in_specs=[_HBM] * (2 * n),
        out_specs=[_SEM] * (2 * ng) + [_HBM] * (2 * n) + [pl.BlockSpec(memory_space=pltpu.VMEM)],
        input_output_aliases={i: 2 * ng + i for i in range(2 * n)},
        compiler_params=pltpu.CompilerParams(has_side_effects=_DATAFLOW),
    )(*srcs, *lands)
    send_sems, recv_sems = res[0:ng], res[ng:2 * ng]
    src_thru, land_thru = res[2 * ng:2 * ng + n], res[2 * ng + n:2 * ng + 2 * n]
    handles = []
    for gi, g in enumerate(groups):
        sl = slice(first[gi], first[gi] + len(g))
        handles.append((g, src_thru[sl], land_thru[sl], send_sems[gi], recv_sems[gi]))
    return handles, res[-1]


def _exchange_wait(handle, after, name):
    items, src_thru, land_thru, send_sem, recv_sem = handle
    k = len(items)

    def body(*refs):
        src, land = refs[0:k], refs[k:2 * k]
        send_ref, recv_ref = refs[2 * k], refs[2 * k + 1]
        for cp in _remote_copies(items, src, land, send_ref, recv_ref, _mesh_pos(), arriving=True):
            cp.wait_send()
            cp.wait_recv()
        for cp in _own_copies(items, src, land, recv_ref, _mesh_pos()):
            cp.wait()

    res = pl.pallas_call(
        body, name=name,
        out_shape=([pltpu.HBM(s.shape, s.dtype) for s in src_thru] + [pltpu.HBM(l.shape, l.dtype) for l in land_thru]),
        in_specs=[_HBM] * (2 * k) + [_SEM, _SEM, pl.BlockSpec(memory_space=pl.ANY)],
        out_specs=[_HBM] * (2 * k),
        input_output_aliases={i: i for i in range(2 * k)},
        compiler_params=pltpu.CompilerParams(has_side_effects=_DATAFLOW),
    )(*src_thru, *land_thru, send_sem, recv_sem, after)
    return res[k:2 * k]


def _whole(ref, i):
    return ref


def _slot(ref, i):
    return ref.at[i]


def _rows_of(r):
    return lambda ref, i: ref.at[pl.ds(pl.multiple_of(i * r, r), r), :]


def _cols_of(c):
    return lambda ref, i: ref.at[:, pl.ds(pl.multiple_of(i * c, c), c)]


def _all_reduce_small(buf, name):
    r, c = buf.shape

    def body(src_ref, out_ref, all_ref, send_sems, recv_sems):
        me = _mesh_pos()
        me_i = _lin(me)
        all_ref[me_i] = src_ref[...]
        for p in range(1, N_DEV):
            peer = _flip(me, p)
            pltpu.make_async_remote_copy(
                src_ref=src_ref, dst_ref=all_ref.at[me_i], send_sem=send_sems.at[p - 1], recv_sem=recv_sems.at[p - 1],
                device_id=peer, device_id_type=pl.DeviceIdType.MESH).start()
        for p in range(1, N_DEV):
            peer = _flip(me, p)
            cp = pltpu.make_async_remote_copy(
                src_ref=src_ref, dst_ref=all_ref.at[_lin(peer)], send_sem=send_sems.at[p - 1],
                recv_sem=recv_sems.at[p - 1], device_id=peer, device_id_type=pl.DeviceIdType.MESH)
            cp.wait_recv()
            cp.wait_send()
        acc = all_ref[0]
        for s in range(1, N_DEV):
            acc = acc + all_ref[s]
        out_ref[...] = acc

    vm = pl.BlockSpec(memory_space=pltpu.VMEM)
    return pl.pallas_call(
        body, name=name, in_specs=[vm], out_specs=vm,
        out_shape=jax.ShapeDtypeStruct((r, c), F32),
        scratch_shapes=[pltpu.VMEM((N_DEV, r, c), F32), pltpu.SemaphoreType.DMA((N_DEV - 1,)),
                        pltpu.SemaphoreType.DMA((N_DEV - 1,))],
        compiler_params=pltpu.CompilerParams(has_side_effects=True),
    )(buf)


def _unshard_cols(g):
    s, l, r, c = g.shape
    return jnp.transpose(g, (1, 2, 0, 3)).reshape(l, r, s * c)


def kernel(x, gdn_w_in, gdn_conv, gdn_a_log, gdn_dt_bias, gdn_onorm, gdn_w_out, hgrn_w_in, hgrn_lb_logits, hgrn_gnorm, hgrn_w_out, norm_mix, norm_mlp, mlp_w_up, mlp_w_down, norm_final, loss_target, m_gdn_w_in, m_gdn_conv, m_gdn_a_log, m_gdn_dt_bias, m_gdn_onorm, m_gdn_w_out, m_hgrn_w_in, m_hgrn_lb_logits, m_hgrn_gnorm, m_hgrn_w_out, m_norm_mix, m_norm_mlp, m_mlp_w_up, m_mlp_w_down, m_norm_final, v_gdn_w_in, v_gdn_conv, v_gdn_a_log, v_gdn_dt_bias, v_gdn_onorm, v_gdn_w_out, v_hgrn_w_in, v_hgrn_lb_logits, v_hgrn_gnorm, v_hgrn_w_out, v_norm_mix, v_norm_mlp, v_mlp_w_up, v_mlp_w_down, v_norm_final):
    seqs, seq_len, d = x.shape
    n = seqs * seq_len
    me_i = _lin(_mesh_pos())
    x2 = x.reshape(n, d)
    target = loss_target.reshape(n, d)
    n_gdn, n_hgrn = gdn_w_in.shape[0], hgrn_w_in.shape[0]

    r_out, r_down = gdn_w_out.shape[1], mlp_w_down.shape[1]
    c_gin, c_hin, c_up = gdn_w_in.shape[2], hgrn_w_in.shape[2], mlp_w_up.shape[2]

    def gathered(w, pick, land_shape):
        return _Item(w.astype(BF16), land_shape, _whole, pick)

    groups = [[_Item(gdn_conv, (N_DEV,) + gdn_conv.shape, _whole, _slot),
               _Item(hgrn_gnorm, (N_DEV,) + hgrn_gnorm.shape, _whole, _slot)]]
    for i in range(DEPTH):
        j = i // 2
        if i % 2 == 0:
            groups += [[gathered(gdn_w_in[j], _slot, (N_DEV, d, c_gin))],
                       [gathered(gdn_w_out[j], _rows_of(r_out), (N_DEV * r_out, d))]]
        else:
            groups += [[gathered(hgrn_w_in[j], _cols_of(c_hin), (d, N_DEV * c_hin))],
                       [gathered(hgrn_w_out[j], _rows_of(r_out), (N_DEV * r_out, d))]]
        groups += [[gathered(mlp_w_up[i], _cols_of(c_up), (d, N_DEV * c_up))],
                   [gathered(mlp_w_down[i], _rows_of(r_down), (N_DEV * r_down, d))]]
    gather_handles, token = _exchange_start(groups, "gather_start")
    lbs = _lb_fwd(hgrn_lb_logits + token[0:1, 0:1], "lb_fwd")

    def arrived(k, after, name):
        return _exchange_wait(gather_handles[k], after, "gather_wait_" + name)

    saved = []
    w_in, w_ab, w_out, w_up, w_down = ([None] * DEPTH for _ in range(5))
    h = x2
    for i in range(DEPTH):
        j = i // 2
        if i == 0:
            g_conv, g_gnorm = arrived(0, h, "small")
            conv_full = _unshard_cols(g_conv)
            gnorm_full = jnp.transpose(g_gnorm, (1, 0, 2)).reshape(n_hgrn, d)
        y = _rms_fwd(h, norm_mix[i:i + 1], f"rms_mix_{i}")
        (w_in[i],) = arrived(1 + 4 * i, y, f"in_{i}")
        if i % 2 == 0:
            w_gin = jnp.transpose(w_in[i], (1, 0, 2)).reshape(d, N_DEV * c_gin)
            w_in[i] = w_gin[:, :GDN_MAIN]
            w_ab[i] = jnp.pad(w_gin[:, GDN_MAIN:], ((0, 0), (0, AB_PAD - 2 * N_HEADS)))
            projm = _mm(y, w_in[i], "nn", [F32], f"gdn_proj_{i}")
            projab = _mm(y, w_ab[i], "nn", [F32], f"gdn_proj_ab_{i}")
            o2, st_all = _gdn_fwd(projm, projab, conv_full[j], gdn_a_log[j:j + 1], gdn_dt_bias[j:j + 1],
                                  gdn_onorm[j:j + 1], seqs, f"gdn_fwd_{i}")
            mix = (projm, projab, st_all)
        else:
            proj = _mm(y, w_in[i], "nn", [F32], f"hgrn_proj_{i}")
            o2, o_raw, st_all = _hgrn_fwd(proj, lbs[i:i + 1], gnorm_full[j:j + 1], seqs, f"hgrn_fwd_{i}")
            mix = (proj, o_raw, st_all)
        (w_out[i],) = arrived(2 + 4 * i, o2, f"out_{i}")
        h1 = _mm(o2, w_out[i], "nn", [F32], f"mix_out_{i}", epilogue=lambda acc, res: (res + acc,), extras=(h,))
        y2 = _rms_fwd(h1, norm_mlp[i:i + 1], f"rms_mlp_{i}")
        (w_up[i],) = arrived(3 + 4 * i, y2, f"up_{i}")
        u, act = _mm(y2, w_up[i], "nn", [BF16, BF16], f"mlp_up_{i}",
                     epilogue=lambda acc: (acc, jnp.square(jnp.maximum(acc, 0.0))))
        (w_down[i],) = arrived(4 + 4 * i, act, f"down_{i}")
        h2 = _mm(act, w_down[i], "nn", [F32], f"mlp_down_{i}", epilogue=lambda acc, res: (res + acc,), extras=(h1,))
        saved.append((h, y, mix, o2, h1, y2, u, act))
        h = h2

    dh, dh_b, d_nf, sq = _loss_head(h, norm_final.reshape(1, d), target, "loss_head")

    d_nmix, d_nmlp = [None] * DEPTH, [None] * DEPTH
    d_conv, d_alog, d_dtb, d_onorm = [None] * n_gdn, [None] * n_gdn, [None] * n_gdn, [None] * n_gdn
    d_lb = [jnp.zeros((1, d), F32)] * DEPTH
    d_gnorm = [None] * n_hgrn
    mlp_handles, mix_handles = [None] * DEPTH, [None] * DEPTH
    token = None
    for i in reversed(range(DEPTH)):
        j = i // 2
        h_in, y, mix, o2, h1, y2, u, act = saved[i]
        g_down = _mm(act, dh_b, "tn", [BF16], f"g_down_{i}", after=token)
        du = _mm(dh_b, w_down[i], "nt", [BF16], f"d_u_{i}",
                 epilogue=lambda acc, uu: (acc * (2.0 * jnp.maximum(uu.astype(F32), 0.0)),), extras=(u,))
        g_up = _mm(y2, du, "tn", [BF16], f"g_up_{i}")
        mlp_handles[i], token = _exchange_start(
            [[_Item(g_down, (N_DEV, r_down, d), _rows_of(r_down), _slot)],
             [_Item(g_up, (N_DEV, d, c_up), _cols_of(c_up), _slot)]], f"scatter_start_mlp_{i}")
        dy2 = _mm(du, w_up[i], "nt", [BF16], f"d_y2_{i}", after=token)
        dh1, dh1_b, d_nmlp[i] = _rms_bwd(h1, norm_mlp[i:i + 1], dy2, dh, f"rms_mlp_bwd_{i}")
        g_out = _mm(o2, dh1_b, "tn", [BF16], f"g_out_{i}")
        do2 = _mm(dh1_b, w_out[i], "nt", [BF16], f"d_o2_{i}")
        if i % 2 == 0:
            projm, projab, st_all = mix
            dpm, dpab, d_conv[j], d_alog[j], d_dtb[j], d_onorm[j] = _gdn_bwd(
                projm, projab, conv_full[j], gdn_a_log[j:j + 1], gdn_dt_bias[j:j + 1], gdn_onorm[j:j + 1],
                st_all, do2, seqs, f"gdn_bwd_{i}")
            g_main = _mm(y, dpm, "tn", [BF16], f"g_in_{i}")
            g_ab = _mm(y, dpab, "tn", [BF16], f"g_in_ab_{i}")
            g_in = jnp.concatenate([g_main, g_ab[:, :2 * N_HEADS]], axis=1)
            g_in = jnp.transpose(g_in.reshape(d, N_DEV, c_gin), (1, 0, 2))
            in_item = _Item(g_in, (N_DEV, d, c_gin), _slot, _slot)
            dy_ab = _mm(dpab, w_ab[i], "nt", [F32], f"d_y_ab_{i}")
            dy = _mm(dpm, w_in[i], "nt", [BF16], f"d_y_{i}", epilogue=lambda acc, e: (acc + e,), extras=(dy_ab,))
        else:
            proj, o_raw, st_all = mix
            dp, d_lb[i], d_gnorm[j] = _hgrn_bwd(proj, lbs[i:i + 1], gnorm_full[j:j + 1], st_all, o_raw, do2,
                                               seqs, f"hgrn_bwd_{i}")
            g_in = _mm(y, dp, "tn", [BF16], f"g_in_{i}")
            in_item = _Item(g_in, (N_DEV, d, c_hin), _cols_of(c_hin), _slot)
            dy = _mm(dp, w_in[i], "nt", [BF16], f"d_y_{i}")
        mix_handles[i], token = _exchange_start(
            [[_Item(g_out, (N_DEV, r_out, d), _rows_of(r_out), _slot)], [in_item]], f"scatter_start_mix_{i}")
        dh, dh_b, d_nmix[i] = _rms_bwd(h_in, norm_mix[i:i + 1], dy, dh1, f"rms_mix_bwd_{i}")
    grad_x = dh.reshape(x.shape)

    def landed(handles, k, layers, after, name):
        return [_exchange_wait(handles[i][k], after, f"scatter_wait_{name}_{i}")[0] for i in layers]

    every, even, odd = range(DEPTH), range(0, DEPTH, 2), range(1, DEPTH, 2)
    upd = {}
    upd["mlp_w_down"] = _adamw_slots(mlp_w_down, landed(mlp_handles, 0, every, dh, "down"), m_mlp_w_down,
                                     v_mlp_w_down, "adamw_mlp_w_down")
    upd["mlp_w_up"] = _adamw_slots(mlp_w_up, landed(mlp_handles, 1, every, upd["mlp_w_down"][1], "up"), m_mlp_w_up,
                                   v_mlp_w_up, "adamw_mlp_w_up")
    upd["hgrn_w_out"] = _adamw_slots(hgrn_w_out, landed(mix_handles, 0, odd, upd["mlp_w_up"][1], "out"),
                                     m_hgrn_w_out, v_hgrn_w_out, "adamw_hgrn_w_out")
    upd["hgrn_w_in"] = _adamw_slots(hgrn_w_in, landed(mix_handles, 1, odd, upd["hgrn_w_out"][1], "in"), m_hgrn_w_in,
                                    v_hgrn_w_in, "adamw_hgrn_w_in")
    upd["gdn_w_out"] = _adamw_slots(gdn_w_out, landed(mix_handles, 0, even, upd["hgrn_w_in"][1], "out"),
                                    m_gdn_w_out, v_gdn_w_out, "adamw_gdn_w_out")
    upd["gdn_w_in"] = _adamw_slots(gdn_w_in, landed(mix_handles, 1, even, upd["gdn_w_out"][1], "in"), m_gdn_w_in,
                                   v_gdn_w_in, "adamw_gdn_w_in")

    def update(name, w, g, m, v):
        shape = w.shape
        c = shape[-1]
        res = _adamw(w.reshape(-1, c), g.reshape(-1, c), m.reshape(-1, c), v.reshape(-1, c), "adamw_" + name)
        return [g.reshape(shape)] + [o.reshape(shape) for o in res]

    dlb_rows = jnp.concatenate(d_lb, axis=0)
    tail = jnp.concatenate(
        [jnp.concatenate(d_onorm, axis=1), jnp.concatenate(d_alog, axis=1), jnp.concatenate(d_dtb, axis=1)], axis=1)
    tail = jnp.pad(tail, ((0, 0), (0, d - tail.shape[1])))
    conv_rows = jnp.stack(d_conv).reshape(-1, d)
    packed = jnp.concatenate(
        [jnp.concatenate(d_nmix, axis=0), jnp.concatenate(d_nmlp, axis=0), d_nf, sq, dlb_rows,
         jnp.concatenate(d_gnorm, axis=0), tail, conv_rows], axis=0)
    pad_rows = (-packed.shape[0]) % 8
    packed = jnp.pad(packed, ((0, pad_rows), (0, 0)))
    tot = _all_reduce_small(packed, "reduce_small")
    r0 = 0
    g_nmix = tot[r0:r0 + DEPTH]; r0 += DEPTH
    g_nmlp = tot[r0:r0 + DEPTH]; r0 += DEPTH
    g_nf = tot[r0]; r0 += 1
    loss = tot[r0, 0]; r0 += 1
    g_lb = _lb_bwd(hgrn_lb_logits, tot[r0:r0 + DEPTH], "lb_bwd"); r0 += DEPTH
    g_gnorm_full = tot[r0:r0 + n_hgrn]; r0 += n_hgrn
    t_row = tot[r0]; r0 += 1
    g_conv_full = tot[r0:r0 + n_gdn * CONV_K * 3].reshape(n_gdn, CONV_K, 3 * d)
    g_onorm = t_row[0:n_gdn * HEAD_DIM].reshape(n_gdn, HEAD_DIM)
    o1 = n_gdn * HEAD_DIM
    g_alog = t_row[o1:o1 + n_gdn * N_HEADS].reshape(n_gdn, N_HEADS)
    g_dtb = t_row[o1 + n_gdn * N_HEADS:o1 + 2 * n_gdn * N_HEADS].reshape(n_gdn, N_HEADS)
    c_gn, c_cv = hgrn_gnorm.shape[1], gdn_conv.shape[2]
    g_gnorm = lax.dynamic_slice_in_dim(g_gnorm_full, me_i * c_gn, c_gn, axis=1)
    g_conv = lax.dynamic_slice_in_dim(g_conv_full, me_i * c_cv, c_cv, axis=2)

    upd["gdn_conv"] = update("gdn_conv", gdn_conv, g_conv, m_gdn_conv, v_gdn_conv)
    upd["gdn_a_log"] = update("gdn_a_log", gdn_a_log, g_alog, m_gdn_a_log, v_gdn_a_log)
    upd["gdn_dt_bias"] = update("gdn_dt_bias", gdn_dt_bias, g_dtb, m_gdn_dt_bias, v_gdn_dt_bias)
    upd["gdn_onorm"] = update("gdn_onorm", gdn_onorm, g_onorm, m_gdn_onorm, v_gdn_onorm)
    upd["hgrn_lb_logits"] = update("hgrn_lb_logits", hgrn_lb_logits, g_lb, m_hgrn_lb_logits, v_hgrn_lb_logits)
    upd["hgrn_gnorm"] = update("hgrn_gnorm", hgrn_gnorm, g_gnorm, m_hgrn_gnorm, v_hgrn_gnorm)
    upd["norm_mix"] = update("norm_mix", norm_mix, g_nmix, m_norm_mix, v_norm_mix)
    upd["norm_mlp"] = update("norm_mlp", norm_mlp, g_nmlp, m_norm_mlp, v_norm_mlp)
    upd["norm_final"] = update("norm_final", norm_final, g_nf, m_norm_final, v_norm_final)

    order = ["gdn_w_in", "gdn_conv", "gdn_a_log", "gdn_dt_bias", "gdn_onorm", "gdn_w_out", "hgrn_w_in",
             "hgrn_lb_logits", "hgrn_gnorm", "hgrn_w_out", "norm_mix", "norm_mlp", "mlp_w_up", "mlp_w_down",
             "norm_final"]
    outs = [loss, grad_x]
    for k in range(4):
        outs += [upd[name][k] for name in order]
    return tuple(outs)
```

```python
import functools

import numpy as np
import jax
import jax.numpy as jnp
from jax import lax
from jax.experimental import pallas as pl
from jax.experimental.pallas import tpu as pltpu

F32 = jnp.float32
BF16 = jnp.bfloat16

D_MODEL = 1024
N_HEADS = 8
HEAD_DIM = 128
CHUNK = 64
SUB = 16
N_SUB = CHUNK // SUB
CONV_K = 4
HALO = 16
EPS = 1e-6
DEPTH = 4
N_DEV = 8
GDN_MAIN = 4 * D_MODEL
GDN_IN = GDN_MAIN + 2 * N_HEADS
AB_PAD = 128
HEAD_GROUP = 8

ADAM_LR = 0.001
ADAM_B1 = 0.9
ADAM_B2 = 0.999
ADAM_EPS = 1e-08
ADAM_WD = 0.01
ADAM_STEP = 10

VMEM_LIMIT = 56 * 1024 * 1024
MM_TILE = 1024
MM_VMEM_BUDGET = 40 * 1024 * 1024

_DIMS = {
    "nn": (((1,), (0,)), ((), ())),
    "nt": (((1,), (1,)), ((), ())),
    "tn": (((0,), (0,)), ((), ())),
}


def _parts(x, n):
    if n == 1 and x.dtype == BF16:
        return [x]
    out = []
    r = x.astype(F32)
    for i in range(n):
        p = r.astype(BF16)
        out.append(p)
        if i + 1 < n:
            r = r - p.astype(F32)
    return out


def _dot_raw(a, b, mode, na, nb):
    ap, bp = _parts(a, na), _parts(b, nb)
    nmax = max(na, nb)
    dot = lambda x, y: lax.dot_general(x, y, _DIMS[mode], preferred_element_type=F32)
    acc = None
    if na > 1 and mode != "tn":
        m = a.shape[0]
        for j, xb in enumerate(bp):
            mine = ap[:nmax - j]
            t = dot(mine[0] if len(mine) == 1 else jnp.concatenate(mine, axis=0), xb)
            for i in range(len(mine)):
                acc = t[i * m:(i + 1) * m] if acc is None else acc + t[i * m:(i + 1) * m]
        return acc
    axis = 0 if mode == "nt" else 1
    n = b.shape[axis]
    for i, xa in enumerate(ap):
        mine = bp[:nmax - i]
        t = dot(xa, mine[0] if len(mine) == 1 else jnp.concatenate(mine, axis=axis))
        for j in range(len(mine)):
            acc = t[:, j * n:(j + 1) * n] if acc is None else acc + t[:, j * n:(j + 1) * n]
    return acc


@functools.partial(jax.custom_vjp, nondiff_argnums=(2, 3, 4))
def _dot(a, b, mode, na, nb):
    return _dot_raw(a, b, mode, na, nb)


def _dot_fwd(a, b, mode, na, nb):
    return _dot_raw(a, b, mode, na, nb), (a, b)


def _dot_bwd(mode, na, nb, res, ct):
    a, b = res
    nc = max(na, nb)
    if mode == "nn":
        da = _dot_raw(ct, b, "nt", nc, nb)
        db = _dot_raw(a, ct, "tn", na, nc)
    elif mode == "nt":
        da = _dot_raw(ct, b, "nn", nc, nb)
        db = _dot_raw(ct, a, "tn", nc, na)
    else:
        da = _dot_raw(b, ct, "nt", nb, nc)
        db = _dot_raw(a, ct, "nn", na, nc)
    return da.astype(a.dtype), db.astype(b.dtype)


_dot.defvjp(_dot_fwd, _dot_bwd)


def _iota2(shape, dim):
    return lax.broadcasted_iota(jnp.int32, shape, dim)


def _tril_f32(n):
    return (_iota2((n, n), 0) >= _iota2((n, n), 1)).astype(F32)


def _cumsum_rows(g):
    return _dot(_tril_f32(g.shape[0]), g, "nn", 1, 3)


def _inv_unit_lower(L):
    n = L.shape[0]
    eye = (_iota2((n, n), 0) == _iota2((n, n), 1)).astype(F32)
    neg = -L
    s = eye + neg
    p = _dot_raw(neg, neg, "nn", 2, 2)
    m = 2
    while 2 * m < n:
        both = _dot_raw(jnp.concatenate([p, s], axis=0), p, "nn", 2, 2)
        p, s = both[0:n], s + both[n:2 * n]
        m *= 2
    return s + _dot_raw(s, p, "nn", 2, 2)


@jax.custom_vjp
def _solve_unit_lower(L, rhs):
    return _dot_raw(_inv_unit_lower(L), rhs, "nn", 2, 2)


def _solve_fwd(L, rhs):
    t = _inv_unit_lower(L)
    sol = _dot_raw(t, rhs, "nn", 2, 2)
    return sol, (t, sol)


def _solve_bwd(res, ct):
    t, sol = res
    y = _dot_raw(t, ct, "tn", 2, 2)
    return -_dot_raw(y, sol, "nt", 2, 2), y


_solve_unit_lower.defvjp(_solve_fwd, _solve_bwd)


def _softplus(x):
    return jnp.maximum(x, 0.0) + jnp.log1p(jnp.exp(-jnp.abs(x)))


def _rms(x, w):
    return x * lax.rsqrt(jnp.mean(x * x, axis=-1, keepdims=True) + EPS) * w


HG_LEVELS = (32, 16, 8, 4, 2, 1)


def _hg_level_sums():
    i = np.arange(CHUNK)[:, None]
    m = np.arange(CHUNK)[None, :]
    to_row = [(m <= i) & (m // b == i // b) for b in HG_LEVELS]
    to_col = [(m > i) & (m // b == i // b) for b in HG_LEVELS]
    return jnp.asarray(np.concatenate(to_row + to_col + [m <= i]), BF16)


def _hg_level_masks():
    i = np.arange(CHUNK)[:, None]
    j = np.arange(CHUNK)[None, :]
    return jnp.asarray(np.stack([(i // b == j // b + 1) & ((i // b) % 2 == 1) for b in HG_LEVELS]), F32)


def _hg_pre(p, lb, sums):
    qraw = p[:, 0:D_MODEL]
    f = p[:, D_MODEL:2 * D_MODEL]
    v = p[:, 2 * D_MODEL:3 * D_MODEL]
    g = jnp.log(lb + (1.0 - lb) * jax.nn.sigmoid(f))
    k = (1.0 - lb) * jax.nn.sigmoid(-f)
    q = jax.nn.silu(qraw) * (HEAD_DIM ** -0.5)
    return q, k, v, _dot(sums, g, "nn", 1, 3)


def _hg_head(st, q, k, v, e, masks):
    nl = len(HG_LEVELS)
    eye = (_iota2((CHUNK, CHUNK), 0) == _iota2((CHUNK, CHUNK), 1)).astype(F32)
    a = eye * jnp.sum(q * k, axis=-1, keepdims=True)
    for l in range(nl):
        rows = q * jnp.exp(e[l * CHUNK:(l + 1) * CHUNK])
        cols = k * jnp.exp(e[(nl + l) * CHUNK:(nl + l + 1) * CHUNK])
        a = a + masks[l] * _dot(rows, cols, "nt", 1, 1)
    gc = e[2 * nl * CHUNK:(2 * nl + 1) * CHUNK]
    o = _dot(a, v, "nn", 1, 1) + _dot(q * jnp.exp(gc), st, "nt", 1, 1)
    g_last = gc[CHUNK - 1:CHUNK]
    st_new = st * jnp.exp(g_last) + _dot(v, k * jnp.exp(g_last - gc), "tn", 1, 1)
    return o, st_new


_HG_HEADS = jax.vmap(_hg_head, in_axes=(0, 0, 0, 0, 0, None))


def _hg_post(o, gate, gw):
    return _rms(o, gw) * jax.nn.silu(gate)


def _gd_pre(xp, a, b, cw, alog, dtb):
    off = HALO - (CONV_K - 1)
    y = cw[0:1] * xp[off:off + CHUNK]
    for kk in range(1, CONV_K):
        y = y + cw[kk:kk + 1] * xp[off + kk:off + kk + CHUNK]
    c = jax.nn.silu(y)
    beta = jax.nn.sigmoid(b)
    g = -jnp.exp(alog) * _softplus(a + dtb)
    expand = (_iota2((N_HEADS, D_MODEL), 1) // HEAD_DIM == _iota2((N_HEADS, D_MODEL), 0)).astype(F32)
    return c, _dot(beta, expand, "nn", 3, 1), _dot(g, expand, "nn", 3, 1)


def _gd_head(st, q, k, v, beta, g, gate, onw):
    q = q * lax.rsqrt(jnp.sum(q * q, axis=-1, keepdims=True) + EPS) * (HEAD_DIM ** -0.5)
    k = k * lax.rsqrt(jnp.sum(k * k, axis=-1, keepdims=True) + EPS)
    ri = _iota2((CHUNK, CHUNK), 0)
    ci = _iota2((CHUNK, CHUNK), 1)
    after = (_iota2((CHUNK, HEAD_DIM), 0) > _iota2((CHUNK, HEAD_DIM), 1)).astype(F32)
    sums = _dot(_tril_f32(CHUNK), jnp.concatenate([g, g * after], axis=1), "nn", 1, 3)
    gc = sums[:, 0:HEAD_DIM]
    decay = jnp.exp(jnp.where(ri >= ci, sums[:, HEAD_DIM:HEAD_DIM + CHUNK], -jnp.inf))
    kb = k * beta
    egc = jnp.exp(gc)
    L = jnp.where(ri > ci, _dot(kb, k, "nt", 1, 1) * decay, 0.0)
    sol = _solve_unit_lower(L, jnp.concatenate([v * beta, kb * egc], axis=1))
    u = sol[:, 0:HEAD_DIM]
    w = sol[:, HEAD_DIM:2 * HEAD_DIM]
    a_qk = jnp.where(ri >= ci, _dot(q, k, "nt", 1, 1) * decay, 0.0)
    g_last = gc[CHUNK - 1:CHUNK]
    v_new = u - _dot(w, st, "nt", 1, 1)
    o = _dot(q * egc, st, "nt", 1, 1) + _dot(a_qk, v_new, "nn", 1, 1)
    st_new = st * jnp.exp(g_last) + _dot(v_new, k * jnp.exp(g_last - gc), "tn", 1, 1)
    return _rms(o, onw) * jax.nn.silu(gate), st_new


def _params(*sem):
    return pltpu.CompilerParams(dimension_semantics=sem, vmem_limit_bytes=VMEM_LIMIT)


def _tile(n, pref):
    t = min(n, pref)
    assert n % t == 0, (n, pref)
    return t


def _mm_tiles(m, n, k, a_size, b_size, tile_sizes):
    tm, tn, tk = _tile(m, MM_TILE), _tile(n, MM_TILE), k

    def need(tm, tn, tk):
        acc = 4 * tm * tn * (2 if tk < k else 1)
        return 2 * (tm * tk * a_size + tk * tn * b_size + tm * tn * sum(tile_sizes)) + acc

    while need(tm, tn, tk) > MM_VMEM_BUDGET:
        if tk > 2048 or (tk > 512 and tm <= 512):
            tk //= 2
        else:
            tm //= 2
    return tm, tn, tk


def _mm(a, b, mode, out_dtypes, name, epilogue=None, extras=(), after=None):
    if mode == "nn":
        (m, k), (k2, n) = a.shape, b.shape
    elif mode == "nt":
        (m, k), (n, k2) = a.shape, b.shape
    else:
        (k, m), (k2, n) = a.shape, b.shape
    assert k == k2, (a.shape, b.shape, mode)
    tm, tn, tk = _mm_tiles(m, n, k, a.dtype.itemsize, b.dtype.itemsize,
                           [e.dtype.itemsize for e in extras] + [jnp.dtype(dt).itemsize for dt in out_dtypes])
    nk = k // tk
    ne, no, nafter = len(extras), len(out_dtypes), int(after is not None)
    if epilogue is None:
        epilogue = lambda acc: (acc,)

    def body(*refs):
        a_ref, b_ref = refs[0], refs[1]
        ex = refs[2:2 + ne]
        outs = refs[2 + ne + nafter:2 + ne + nafter + no]
        part = lax.dot_general(a_ref[...].astype(BF16), b_ref[...].astype(BF16), _DIMS[mode],
                               preferred_element_type=F32)

        def finish(acc):
            for o_ref, val in zip(outs, epilogue(acc, *[e[...] for e in ex])):
                o_ref[...] = val.astype(o_ref.dtype)

        if nk == 1:
            finish(part)
        else:
            acc_ref = refs[-1]
            kk = pl.program_id(2)

            @pl.when(kk == 0)
            def _():
                acc_ref[...] = part

            @pl.when(kk > 0)
            def _():
                acc_ref[...] += part

            @pl.when(kk == nk - 1)
            def _():
                finish(acc_ref[...])

    if mode == "tn":
        a_spec = pl.BlockSpec((tk, tm), lambda i, j, kk: (kk, i))
    else:
        a_spec = pl.BlockSpec((tm, tk), lambda i, j, kk: (i, kk))
    if mode == "nt":
        b_spec = pl.BlockSpec((tn, tk), lambda i, j, kk: (j, kk))
    else:
        b_spec = pl.BlockSpec((tk, tn), lambda i, j, kk: (kk, j))
    o_spec = pl.BlockSpec((tm, tn), lambda i, j, kk: (i, j))
    res = pl.pallas_call(
        body,
        name=name,
        grid=(m // tm, n // tn, nk),
        in_specs=[a_spec, b_spec] + [o_spec] * ne + [pl.BlockSpec(memory_space=pl.ANY)] * nafter,
        out_specs=[o_spec] * no,
        out_shape=[jax.ShapeDtypeStruct((m, n), dt) for dt in out_dtypes],
        scratch_shapes=[pltpu.VMEM((tm, tn), F32)] if nk > 1 else [],
        compiler_params=_params("parallel", "parallel", "arbitrary"),
    )(a, b, *extras, *([after] if nafter else []))
    return res[0] if no == 1 else res


def _rms_fwd(x, w, name, tm=512):
    n, d = x.shape
    tm = _tile(n, tm)

    def body(x_ref, w_ref, y_ref):
        y_ref[...] = _rms(x_ref[...], w_ref[...]).astype(y_ref.dtype)

    return pl.pallas_call(
        body, name=name, grid=(n // tm,),
        in_specs=[pl.BlockSpec((tm, d), lambda i: (i, 0)), pl.BlockSpec((1, d), lambda i: (0, 0))],
        out_specs=pl.BlockSpec((tm, d), lambda i: (i, 0)),
        out_shape=jax.ShapeDtypeStruct((n, d), BF16),
        compiler_params=_params("arbitrary"),
    )(x, w)


def _rms_bwd(x, w, dy, dres, name, tm=512):
    n, d = x.shape
    tm = _tile(n, tm)

    def body(x_ref, w_ref, dy_ref, dres_ref, dx_ref, dxb_ref, dw_ref):
        _, vjp = jax.vjp(_rms, x_ref[...], w_ref[...])
        dx, dw = vjp(dy_ref[...].astype(F32))
        dx = dres_ref[...] + dx
        dx_ref[...] = dx
        dxb_ref[...] = dx.astype(dxb_ref.dtype)

        @pl.when(pl.program_id(0) == 0)
        def _():
            dw_ref[...] = dw

        @pl.when(pl.program_id(0) > 0)
        def _():
            dw_ref[...] += dw

    row = pl.BlockSpec((tm, d), lambda i: (i, 0))
    vec = pl.BlockSpec((1, d), lambda i: (0, 0))
    return pl.pallas_call(
        body, name=name, grid=(n // tm,),
        in_specs=[row, vec, row, row],
        out_specs=[row, row, vec],
        out_shape=[jax.ShapeDtypeStruct((n, d), F32), jax.ShapeDtypeStruct((n, d), BF16),
                   jax.ShapeDtypeStruct((1, d), F32)],
        compiler_params=_params("arbitrary"),
    )(x, w, dy, dres)


def _loss_head(h, w, target, name, tm=512):
    n, d = h.shape
    tm = _tile(n, tm)

    def body(h_ref, w_ref, t_ref, dh_ref, dhb_ref, dw_ref, sq_ref):
        y, vjp = jax.vjp(_rms, h_ref[...], w_ref[...])
        err = y - t_ref[...]
        dh, dw = vjp(err * (1.0 / d))
        dh_ref[...] = dh
        dhb_ref[...] = dh.astype(dhb_ref.dtype)
        sq = jnp.sum(err * err, axis=0, keepdims=True)

        @pl.when(pl.program_id(0) == 0)
        def _():
            dw_ref[...] = dw
            sq_ref[...] = sq

        @pl.when(pl.program_id(0) > 0)
        def _():
            dw_ref[...] += dw
            sq_ref[...] += sq

        @pl.when(pl.program_id(0) == n // tm - 1)
        def _():
            total = jnp.sum(sq_ref[...], axis=1, keepdims=True) * (0.5 / d)
            sq_ref[...] = jnp.broadcast_to(total, sq_ref.shape)

    row = pl.BlockSpec((tm, d), lambda i: (i, 0))
    vec = pl.BlockSpec((1, d), lambda i: (0, 0))
    return pl.pallas_call(
        body, name=name, grid=(n // tm,),
        in_specs=[row, vec, row],
        out_specs=[row, row, vec, vec],
        out_shape=[jax.ShapeDtypeStruct((n, d), F32), jax.ShapeDtypeStruct((n, d), BF16),
                   jax.ShapeDtypeStruct((1, d), F32), jax.ShapeDtypeStruct((1, d), F32)],
        compiler_params=_params("arbitrary"),
    )(h, w, target)


def _lower_bounds(logits):
    sm = jax.nn.softmax(logits, axis=0)
    rows = [sm[0:1] * 0.0]
    for r in range(1, DEPTH):
        rows.append(rows[-1] + sm[r:r + 1])
    return jnp.concatenate(rows, axis=0)


def _lb_fwd(logits, name):
    def body(l_ref, o_ref):
        o_ref[...] = _lower_bounds(l_ref[...])

    return pl.pallas_call(body, name=name, out_shape=jax.ShapeDtypeStruct(logits.shape, F32))(logits)


def _lb_bwd(logits, dlb, name):
    def body(l_ref, d_ref, o_ref):
        _, vjp = jax.vjp(_lower_bounds, l_ref[...])
        (o_ref[...],) = vjp(d_ref[...])

    return pl.pallas_call(body, name=name, out_shape=jax.ShapeDtypeStruct(logits.shape, F32))(logits, dlb)


def _head_slice(h):
    if isinstance(h, int):
        return pl.ds(h * HEAD_DIM, HEAD_DIM)
    return pl.ds(pl.multiple_of(h * HEAD_DIM, HEAD_DIM), HEAD_DIM)


def _head_groups(group_body):
    if HEAD_GROUP == N_HEADS:
        group_body(list(range(N_HEADS)))
        return

    def trip(i, carry):
        group_body([i * HEAD_GROUP + t for t in range(HEAD_GROUP)])
        return carry

    lax.fori_loop(0, N_HEADS // HEAD_GROUP, trip, 0)


def _stack_heads(ref, hs, first=0):
    return jnp.stack([ref[:, _head_slice(h + first)] for h in hs])


def _unstack_heads(ref, hs, val, first=0):
    for t, h in enumerate(hs):
        ref[:, _head_slice(h + first)] = val[t].astype(ref.dtype)


_GD_HEADS = jax.vmap(_gd_head, in_axes=(0, 0, 0, 0, 0, 0, 0, None))


def _hgrn_fwd(proj, lb, gw, seqs, name):
    n = proj.shape[0]
    nc = n // seqs // CHUNK
    d = D_MODEL

    sums, masks = _hg_level_sums(), _hg_level_masks()

    def body(p_ref, lb_ref, gw_ref, sums_ref, masks_ref, o2_ref, o_ref, st_all_ref, st_sc, q_sc, k_sc, v_sc, e_sc):
        @pl.when(pl.program_id(1) == 0)
        def _():
            st_sc[...] = jnp.zeros_like(st_sc)

        q_sc[...], k_sc[...], v_sc[...], e_sc[...] = _hg_pre(p_ref[:, 0:3 * d].astype(F32), lb_ref[...], sums_ref[...])
        st_all_ref[0] = st_sc[...]

        def group(hs):
            sts = pl.ds(hs[0], len(hs))
            o, st_new = _HG_HEADS(st_sc[sts], *[_stack_heads(r, hs) for r in (q_sc, k_sc, v_sc, e_sc)], masks_ref[...])
            _unstack_heads(o_ref, hs, o)
            st_sc[sts] = st_new

        _head_groups(group)
        o2_ref[...] = _hg_post(o_ref[...], p_ref[:, 3 * d:4 * d].astype(F32), gw_ref[...]).astype(o2_ref.dtype)

    idx = lambda b, c: (b * nc + c, 0)
    vec = pl.BlockSpec((1, d), lambda b, c: (0, 0))
    act = pl.BlockSpec((CHUNK, d), idx)
    return pl.pallas_call(
        body, name=name, grid=(seqs, nc),
        in_specs=[pl.BlockSpec((CHUNK, 4 * d), idx), vec, vec, pl.BlockSpec(sums.shape, lambda b, c: (0, 0)),
                  pl.BlockSpec(masks.shape, lambda b, c: (0, 0, 0))],
        out_specs=[act, act, pl.BlockSpec((1, N_HEADS, HEAD_DIM, HEAD_DIM), lambda b, c: (b * nc + c, 0, 0, 0))],
        out_shape=[jax.ShapeDtypeStruct((n, d), BF16), jax.ShapeDtypeStruct((n, d), F32),
                   jax.ShapeDtypeStruct((n // CHUNK, N_HEADS, HEAD_DIM, HEAD_DIM), F32)],
        scratch_shapes=[pltpu.VMEM((N_HEADS, HEAD_DIM, HEAD_DIM), F32)] + [pltpu.VMEM((CHUNK, d), F32)] * 3
        + [pltpu.VMEM((sums.shape[0], d), F32)],
        compiler_params=_params("arbitrary", "arbitrary"),
    )(proj, lb, gw, sums, masks)


def _hgrn_bwd(proj, lb, gw, st_all, o, do2, seqs, name):
    n = proj.shape[0]
    nc = n // seqs // CHUNK
    d = D_MODEL

    sums, masks = _hg_level_sums(), _hg_level_masks()

    def body(p_ref, lb_ref, gw_ref, sums_ref, masks_ref, st_all_ref, o_ref, do2_ref, dp_ref, dlb_ref, dgw_ref,
             dst_sc, q_sc, k_sc, v_sc, e_sc, do_sc, dq_sc, dk_sc, dv_sc, de_sc):
        first = (pl.program_id(0) == 0) & (pl.program_id(1) == 0)

        @pl.when(pl.program_id(1) == 0)
        def _():
            dst_sc[...] = jnp.zeros_like(dst_sc)

        level_sums = sums_ref[...]
        pre_out, pre_vjp = jax.vjp(lambda p, lb: _hg_pre(p, lb, level_sums), p_ref[:, 0:3 * d].astype(F32),
                                   lb_ref[...])
        q_sc[...], k_sc[...], v_sc[...], e_sc[...] = pre_out
        _, post_vjp = jax.vjp(_hg_post, o_ref[...], p_ref[:, 3 * d:4 * d].astype(F32), gw_ref[...])
        do_sc[...], dgate, dgw = post_vjp(do2_ref[...].astype(F32))
        dp_ref[:, 3 * d:4 * d] = dgate.astype(dp_ref.dtype)

        def group(hs):
            sts = pl.ds(hs[0], len(hs))
            level_masks = masks_ref[...]
            _, vjp = jax.vjp(lambda *a: _HG_HEADS(*a, level_masks), st_all_ref[0, sts],
                             *[_stack_heads(r, hs) for r in (q_sc, k_sc, v_sc, e_sc)])
            grads = vjp((_stack_heads(do_sc, hs), dst_sc[sts]))
            dst_sc[sts] = grads[0]
            for r, val in zip((dq_sc, dk_sc, dv_sc, de_sc), grads[1:]):
                _unstack_heads(r, hs, val)

        _head_groups(group)
        dp, dlb = pre_vjp((dq_sc[...], dk_sc[...], dv_sc[...], de_sc[...]))
        dp_ref[:, 0:3 * d] = dp.astype(dp_ref.dtype)

        @pl.when(first)
        def _():
            dlb_ref[...] = dlb
            dgw_ref[...] = dgw

        @pl.when(jnp.logical_not(first))
        def _():
            dlb_ref[...] += dlb
            dgw_ref[...] += dgw

    idx = lambda b, c: (b * nc + nc - 1 - c, 0)
    vec = pl.BlockSpec((1, d), lambda b, c: (0, 0))
    act = pl.BlockSpec((CHUNK, d), idx)
    wide = pl.BlockSpec((CHUNK, 4 * d), idx)
    return pl.pallas_call(
        body, name=name, grid=(seqs, nc),
        in_specs=[wide, vec, vec, pl.BlockSpec(sums.shape, lambda b, c: (0, 0)),
                  pl.BlockSpec(masks.shape, lambda b, c: (0, 0, 0)),
                  pl.BlockSpec((1, N_HEADS, HEAD_DIM, HEAD_DIM), lambda b, c: (b * nc + nc - 1 - c, 0, 0, 0)),
                  act, act],
        out_specs=[wide, vec, vec],
        out_shape=[jax.ShapeDtypeStruct((n, 4 * d), BF16), jax.ShapeDtypeStruct((1, d), F32),
                   jax.ShapeDtypeStruct((1, d), F32)],
        scratch_shapes=[pltpu.VMEM((N_HEADS, HEAD_DIM, HEAD_DIM), F32)]
        + [pltpu.VMEM((CHUNK, d), F32)] * 3 + [pltpu.VMEM((sums.shape[0], d), F32)]
        + [pltpu.VMEM((CHUNK, d), F32)] * 4 + [pltpu.VMEM((sums.shape[0], d), F32)],
        compiler_params=_params("arbitrary", "arbitrary"),
    )(proj, lb, gw, sums, masks, st_all, o, do2)


def _gd_xp(halo_ref, p_ref, first_chunk):
    halo = jnp.where(first_chunk, 0.0, halo_ref[...].astype(F32))
    return jnp.concatenate([halo, p_ref[:, 0:3 * D_MODEL].astype(F32)], axis=0)


def _gdn_fwd(projm, projab, cw, alog, dtb, onw, seqs, name):
    n = projm.shape[0]
    nc = n // seqs // CHUNK
    d = D_MODEL
    per_halo = CHUNK // HALO

    def body(p_ref, halo_ref, ab_ref, cw_ref, alog_ref, dtb_ref, onw_ref, o2_ref, st_all_ref,
             st_sc, c_sc, beta_sc, g_sc):
        @pl.when(pl.program_id(1) == 0)
        def _():
            st_sc[...] = jnp.zeros_like(st_sc)

        xp = _gd_xp(halo_ref, p_ref, pl.program_id(1) == 0)
        c_sc[...], beta_sc[...], g_sc[...] = _gd_pre(
            xp, ab_ref[:, 0:N_HEADS], ab_ref[:, N_HEADS:2 * N_HEADS], cw_ref[...], alog_ref[...], dtb_ref[...])
        st_all_ref[0] = st_sc[...]

        def group(hs):
            sts = pl.ds(hs[0], len(hs))
            o2, st_new = _GD_HEADS(
                st_sc[sts], _stack_heads(c_sc, hs), _stack_heads(c_sc, hs, N_HEADS), _stack_heads(c_sc, hs, 2 * N_HEADS),
                _stack_heads(beta_sc, hs), _stack_heads(g_sc, hs), _stack_heads(p_ref, hs, 3 * N_HEADS).astype(F32), onw_ref[...])
            _unstack_heads(o2_ref, hs, o2)
            st_sc[sts] = st_new

        _head_groups(group)

    idx = lambda b, c: (b * nc + c, 0)
    const = lambda b, c: (0, 0)
    return pl.pallas_call(
        body, name=name, grid=(seqs, nc),
        in_specs=[pl.BlockSpec((CHUNK, 4 * d), idx),
                  pl.BlockSpec((HALO, 3 * d), lambda b, c: (jnp.maximum((b * nc + c) * per_halo - 1, 0), 0)),
                  pl.BlockSpec((CHUNK, AB_PAD), idx),
                  pl.BlockSpec((CONV_K, 3 * d), const), pl.BlockSpec((1, N_HEADS), const),
                  pl.BlockSpec((1, N_HEADS), const), pl.BlockSpec((1, HEAD_DIM), const)],
        out_specs=[pl.BlockSpec((CHUNK, d), idx),
                   pl.BlockSpec((1, N_HEADS, HEAD_DIM, HEAD_DIM), lambda b, c: (b * nc + c, 0, 0, 0))],
        out_shape=[jax.ShapeDtypeStruct((n, d), BF16),
                   jax.ShapeDtypeStruct((n // CHUNK, N_HEADS, HEAD_DIM, HEAD_DIM), F32)],
        scratch_shapes=[pltpu.VMEM((N_HEADS, HEAD_DIM, HEAD_DIM), F32), pltpu.VMEM((CHUNK, 3 * d), F32),
                        pltpu.VMEM((CHUNK, d), F32), pltpu.VMEM((CHUNK, d), F32)],
        compiler_params=_params("arbitrary", "arbitrary"),
    )(projm, projm, projab, cw, alog, dtb, onw)


def _gdn_bwd(projm, projab, cw, alog, dtb, onw, st_all, do2, seqs, name):
    n = projm.shape[0]
    nc = n // seqs // CHUNK
    d = D_MODEL
    per_halo = CHUNK // HALO

    def body(p_ref, halo_ref, ab_ref, cw_ref, alog_ref, dtb_ref, onw_ref, st_all_ref, do2_ref,
             dp_ref, dab_ref, dcw_ref, dalog_ref, ddtb_ref, donw_ref,
             dst_sc, dhalo_sc, c_sc, beta_sc, g_sc, dc_sc, dbeta_sc, dg_sc, donw_sc):
        step = pl.program_id(1)
        first = (pl.program_id(0) == 0) & (step == 0)

        @pl.when(step == 0)
        def _():
            dst_sc[...] = jnp.zeros_like(dst_sc)
            dhalo_sc[...] = jnp.zeros_like(dhalo_sc)

        donw_sc[...] = jnp.zeros_like(donw_sc)
        xp = _gd_xp(halo_ref, p_ref, step == nc - 1)
        pre_out, pre_vjp = jax.vjp(_gd_pre, xp, ab_ref[:, 0:N_HEADS], ab_ref[:, N_HEADS:2 * N_HEADS],
                                   cw_ref[...], alog_ref[...], dtb_ref[...])
        c_sc[...], beta_sc[...], g_sc[...] = pre_out

        def group(hs):
            sts = pl.ds(hs[0], len(hs))
            _, vjp = jax.vjp(
                _GD_HEADS, st_all_ref[0, sts], _stack_heads(c_sc, hs), _stack_heads(c_sc, hs, N_HEADS),
                _stack_heads(c_sc, hs, 2 * N_HEADS), _stack_heads(beta_sc, hs), _stack_heads(g_sc, hs),
                _stack_heads(p_ref, hs, 3 * N_HEADS).astype(F32), onw_ref[...])
            dst, dq, dk, dv, dbeta, dg, dgate, donw = vjp((_stack_heads(do2_ref, hs).astype(F32), dst_sc[sts]))
            dst_sc[sts] = dst
            _unstack_heads(dc_sc, hs, dq)
            _unstack_heads(dc_sc, hs, dk, N_HEADS)
            _unstack_heads(dc_sc, hs, dv, 2 * N_HEADS)
            _unstack_heads(dbeta_sc, hs, dbeta)
            _unstack_heads(dg_sc, hs, dg)
            _unstack_heads(dp_ref, hs, dgate, 3 * N_HEADS)
            donw_sc[...] += donw

        _head_groups(group)
        dxp, da, db, dcw, dalog, ddtb = pre_vjp((dc_sc[...], dbeta_sc[...], dg_sc[...]))
        dqkv = jnp.concatenate([dxp[HALO:CHUNK], dxp[CHUNK:HALO + CHUNK] + dhalo_sc[...]], axis=0)
        dp_ref[:, 0:3 * d] = dqkv.astype(dp_ref.dtype)
        dhalo_sc[...] = dxp[0:HALO]
        dab_ref[...] = jnp.concatenate(
            [da, db, jnp.zeros((CHUNK, AB_PAD - 2 * N_HEADS), F32)], axis=1).astype(dab_ref.dtype)

        @pl.when(first)
        def _():
            dcw_ref[...] = dcw
            dalog_ref[...] = dalog
            ddtb_ref[...] = ddtb
            donw_ref[...] = donw_sc[...]

        @pl.when(jnp.logical_not(first))
        def _():
            dcw_ref[...] += dcw
            dalog_ref[...] += dalog
            ddtb_ref[...] += ddtb
            donw_ref[...] += donw_sc[...]

    rev = lambda b, c: b * nc + nc - 1 - c
    idx = lambda b, c: (rev(b, c), 0)
    const = lambda b, c: (0, 0)
    small = [pl.BlockSpec((CONV_K, 3 * d), const), pl.BlockSpec((1, N_HEADS), const),
             pl.BlockSpec((1, N_HEADS), const), pl.BlockSpec((1, HEAD_DIM), const)]
    return pl.pallas_call(
        body, name=name, grid=(seqs, nc),
        in_specs=[pl.BlockSpec((CHUNK, 4 * d), idx),
                  pl.BlockSpec((HALO, 3 * d), lambda b, c: (jnp.maximum(rev(b, c) * per_halo - 1, 0), 0)),
                  pl.BlockSpec((CHUNK, AB_PAD), idx)] + small + [
                  pl.BlockSpec((1, N_HEADS, HEAD_DIM, HEAD_DIM), lambda b, c: (rev(b, c), 0, 0, 0)),
                  pl.BlockSpec((CHUNK, d), idx)],
        out_specs=[pl.BlockSpec((CHUNK, 4 * d), idx), pl.BlockSpec((CHUNK, AB_PAD), idx)] + small,
        out_shape=[jax.ShapeDtypeStruct((n, 4 * d), BF16), jax.ShapeDtypeStruct((n, AB_PAD), BF16),
                   jax.ShapeDtypeStruct((CONV_K, 3 * d), F32), jax.ShapeDtypeStruct((1, N_HEADS), F32),
                   jax.ShapeDtypeStruct((1, N_HEADS), F32), jax.ShapeDtypeStruct((1, HEAD_DIM), F32)],
        scratch_shapes=[pltpu.VMEM((N_HEADS, HEAD_DIM, HEAD_DIM), F32), pltpu.VMEM((HALO, 3 * d), F32),
                        pltpu.VMEM((CHUNK, 3 * d), F32), pltpu.VMEM((CHUNK, d), F32), pltpu.VMEM((CHUNK, d), F32),
                        pltpu.VMEM((CHUNK, 3 * d), F32), pltpu.VMEM((CHUNK, d), F32), pltpu.VMEM((CHUNK, d), F32),
                        pltpu.VMEM((1, HEAD_DIM), F32)],
        compiler_params=_params("arbitrary", "arbitrary"),
    )(projm, projm, projab, cw, alog, dtb, onw, st_all, do2)


def _adam_update(w, g, m, v):
    b1c = 1.0 - ADAM_B1 ** ADAM_STEP
    b2c = 1.0 - ADAM_B2 ** ADAM_STEP
    m_new = ADAM_B1 * m + (1.0 - ADAM_B1) * g
    v_new = ADAM_B2 * v + (1.0 - ADAM_B2) * (g * g)
    delta = -ADAM_LR * ((m_new / b1c) / (jnp.sqrt(v_new / b2c) + ADAM_EPS) + ADAM_WD * w)
    return delta, m_new, v_new


def _adamw(w, g, m, v, name, tr=256):
    r, c = w.shape
    tr = _tile(r, tr)

    def body(w_ref, g_ref, m_ref, v_ref, d_ref, mo_ref, vo_ref):
        d_ref[...], mo_ref[...], vo_ref[...] = _adam_update(w_ref[...], g_ref[...], m_ref[...], v_ref[...])

    blk = pl.BlockSpec((tr, c), lambda i: (i, 0))
    return pl.pallas_call(
        body, name=name, grid=(r // tr,),
        in_specs=[blk] * 4, out_specs=[blk] * 3,
        out_shape=[jax.ShapeDtypeStruct((r, c), F32)] * 3,
        compiler_params=_params("arbitrary"),
    )(w, g, m, v)


def _adamw_slots(w, slot_bufs, m, v, name, tr=256):
    nl, r, c = w.shape
    tr = _tile(r, tr)

    def body(*refs):
        w_ref = refs[0]
        g_refs = refs[1:1 + nl]
        m_ref, v_ref, go_ref, d_ref, mo_ref, vo_ref = refs[1 + nl:]
        for k in range(nl):
            @pl.when(pl.program_id(0) == k)
            def _(k=k):
                g = g_refs[k][0].astype(F32)
                for s in range(1, N_DEV):
                    g = g + g_refs[k][s].astype(F32)
                go_ref[0] = g

        d_ref[0], mo_ref[0], vo_ref[0] = _adam_update(w_ref[0], go_ref[0], m_ref[0], v_ref[0])

    blk = pl.BlockSpec((1, tr, c), lambda l, i: (l, i, 0))
    g_specs = [pl.BlockSpec((N_DEV, tr, c), lambda l, i, k=k: (0, jnp.where(l == k, i, 0), 0)) for k in range(nl)]
    return pl.pallas_call(
        body, name=name, grid=(nl, r // tr),
        in_specs=[blk] + g_specs + [blk, blk], out_specs=[blk] * 4,
        out_shape=[jax.ShapeDtypeStruct((nl, r, c), F32)] * 4,
        compiler_params=_params("arbitrary", "arbitrary"),
    )(w, *slot_bufs, m, v)


def _mesh_pos():
    return lax.axis_index("x"), lax.axis_index("y"), lax.axis_index("c")


def _flip(pos, p):
    x, y, c = pos
    return ((1 - x) if p & 4 else x, (1 - y) if p & 2 else y, (1 - c) if p & 1 else c)


def _lin(pos):
    return 4 * pos[0] + 2 * pos[1] + pos[2]


_HBM = pl.BlockSpec(memory_space=pltpu.HBM)
_SEM = pl.BlockSpec(memory_space=pltpu.SEMAPHORE)
_DATAFLOW = pltpu.SideEffectType.DATAFLOW_SIDE_EFFECTING


class _Item:
    def __init__(self, src, land_shape, src_pick, dst_pick):
        self.src, self.land_shape, self.src_pick, self.dst_pick = src, land_shape, src_pick, dst_pick


def _remote_copies(items, src, land, send_sem, recv_sem, me, arriving):
    me_i = _lin(me)
    out = []
    for it, s_ref, l_ref in zip(items, src, land):
        for p in range(1, N_DEV):
            peer = _flip(me, p)
            out.append(pltpu.make_async_remote_copy(
                src_ref=it.src_pick(s_ref, _lin(peer)),
                dst_ref=it.dst_pick(l_ref, _lin(peer) if arriving else me_i),
                send_sem=send_sem, recv_sem=recv_sem, device_id=peer, device_id_type=pl.DeviceIdType.MESH))
    return out


def _own_copies(items, src, land, sem, me):
    me_i = _lin(me)
    return [pltpu.make_async_copy(it.src_pick(s_ref, me_i), it.dst_pick(l_ref, me_i), sem)
            for it, s_ref, l_ref in zip(items, src, land)]


def _exchange_start(groups, name):
    items = [it for g in groups for it in g]
    n, ng = len(items), len(groups)
    first = [sum(len(g) for g in groups[:gi]) for gi in range(ng)]

    def body(*refs):
        src, land = refs[0:n], refs[n:2 * n]
        send_sems, recv_sems = refs[2 * n:2 * n + ng], refs[2 * n + ng:2 * n + 2 * ng]
        token = refs[4 * n + 2 * ng]
        me = _mesh_pos()
        for gi, g in enumerate(groups):
            sl = slice(first[gi], first[gi] + len(g))
            for cp in _remote_copies(g, src[sl], land[sl], send_sems[gi], recv_sems[gi], me, arriving=False):
                cp.start()
            for cp in _own_copies(g, src[sl], land[sl], recv_sems[gi], me):
                cp.start()
        token[...] = jnp.zeros_like(token)

    srcs = [pltpu.with_memory_space_constraint(it.src, pltpu.HBM) for it in items]
    lands = [pltpu.with_memory_space_constraint(lax.empty(it.land_shape, it.src.dtype), pltpu.HBM) for it in items]
    res = pl.pallas_call(
        body, name=name,
        out_shape=([pltpu.SemaphoreType.DMA(())] * (2 * ng)
                   + [pltpu.HBM(it.src.shape, it.src.dtype) for it in items]
                   + [pltpu.HBM(it.land_shape, it.src.dtype) for it in items]
                   + [jax.ShapeDtypeStruct((8, 128), F32)]),
        in_specs=[_HBM] * (2 * n),
        out_specs=[_SEM] * (2 * ng) + [_HBM] * (2 * n) + [pl.BlockSpec(memory_space=pltpu.VMEM)],
        input_output_aliases={i: 2 * ng + i for i in range(2 * n)},
        compiler_params=pltpu.CompilerParams(has_side_effects=_DATAFLOW),
    )(*srcs, *lands)
    send_sems, recv_sems = res[0:ng], res[ng:2 * ng]
    src_thru, land_thru = res[2 * ng:2 * ng + n], res[2 * ng + n:2 * ng + 2 * n]
    handles = []
    for gi, g in enumerate(groups):
        sl = slice(first[gi], first[gi] + len(g))
        handles.append((g, src_thru[sl], land_thru[sl], send_sems[gi], recv_sems[gi]))
    return handles, res[-1]


def _exchange_wait(handle, after, name):
    items, src_thru, land_thru, send_sem, recv_sem = handle
    k = len(items)

    def body(*refs):
        src, land = refs[0:k], refs[k:2 * k]
        send_ref, recv_ref = refs[2 * k], refs[2 * k + 1]
        for cp in _remote_copies(items, src, land, send_ref, recv_ref, _mesh_pos(), arriving=True):
            cp.wait_send()
            cp.wait_recv()
        for cp in _own_copies(items, src, land, recv_ref, _mesh_pos()):
            cp.wait()

    res = pl.pallas_call(
        body, name=name,
        out_shape=([pltpu.HBM(s.shape, s.dtype) for s in src_thru] + [pltpu.HBM(l.shape, l.dtype) for l in land_thru]),
        in_specs=[_HBM] * (2 * k) + [_SEM, _SEM, pl.BlockSpec(memory_space=pl.ANY)],
        out_specs=[_HBM] * (2 * k),
        input_output_aliases={i: i for i in range(2 * k)},
        compiler_params=pltpu.CompilerParams(has_side_effects=_DATAFLOW),
    )(*src_thru, *land_thru, send_sem, recv_sem, after)
    return res[k:2 * k]


def _whole(ref, i):
    return ref


def _slot(ref, i):
    return ref.at[i]


def _rows_of(r):
    return lambda ref, i: ref.at[pl.ds(pl.multiple_of(i * r, r), r), :]


def _cols_of(c):
    return lambda ref, i: ref.at[:, pl.ds(pl.multiple_of(i * c, c), c)]


def _all_reduce_small(buf, name):
    r, c = buf.shape

    def body(src_ref, out_ref, all_ref, send_sems, recv_sems):
        me = _mesh_pos()
        me_i = _lin(me)
        all_ref[me_i] = src_ref[...]
        for p in range(1, N_DEV):
            peer = _flip(me, p)
            pltpu.make_async_remote_copy(
                src_ref=src_ref, dst_ref=all_ref.at[me_i], send_sem=send_sems.at[p - 1], recv_sem=recv_sems.at[p - 1],
                device_id=peer, device_id_type=pl.DeviceIdType.MESH).start()
        for p in range(1, N_DEV):
            peer = _flip(me, p)
            cp = pltpu.make_async_remote_copy(
                src_ref=src_ref, dst_ref=all_ref.at[_lin(peer)], send_sem=send_sems.at[p - 1],
                recv_sem=recv_sems.at[p - 1], device_id=peer, device_id_type=pl.DeviceIdType.MESH)
            cp.wait_recv()
            cp.wait_send()
        acc = all_ref[0]
        for s in range(1, N_DEV):
            acc = acc + all_ref[s]
        out_ref[...] = acc

    vm = pl.BlockSpec(memory_space=pltpu.VMEM)
    return pl.pallas_call(
        body, name=name, in_specs=[vm], out_specs=vm,
        out_shape=jax.ShapeDtypeStruct((r, c), F32),
        scratch_shapes=[pltpu.VMEM((N_DEV, r, c), F32), pltpu.SemaphoreType.DMA((N_DEV - 1,)),
                        pltpu.SemaphoreType.DMA((N_DEV - 1,))],
        compiler_params=pltpu.CompilerParams(has_side_effects=True),
    )(buf)


def _unshard_cols(g):
    s, l, r, c = g.shape
    return jnp.transpose(g, (1, 2, 0, 3)).reshape(l, r, s * c)


def kernel(x, gdn_w_in, gdn_conv, gdn_a_log, gdn_dt_bias, gdn_onorm, gdn_w_out, hgrn_w_in, hgrn_lb_logits, hgrn_gnorm, hgrn_w_out, norm_mix, norm_mlp, mlp_w_up, mlp_w_down, norm_final, loss_target, m_gdn_w_in, m_gdn_conv, m_gdn_a_log, m_gdn_dt_bias, m_gdn_onorm, m_gdn_w_out, m_hgrn_w_in, m_hgrn_lb_logits, m_hgrn_gnorm, m_hgrn_w_out, m_norm_mix, m_norm_mlp, m_mlp_w_up, m_mlp_w_down, m_norm_final, v_gdn_w_in, v_gdn_conv, v_gdn_a_log, v_gdn_dt_bias, v_gdn_onorm, v_gdn_w_out, v_hgrn_w_in, v_hgrn_lb_logits, v_hgrn_gnorm, v_hgrn_w_out, v_norm_mix, v_norm_mlp, v_mlp_w_up, v_mlp_w_down, v_norm_final):
    seqs, seq_len, d = x.shape
    n = seqs * seq_len
    me_i = _lin(_mesh_pos())
    x2 = x.reshape(n, d)
    target = loss_target.reshape(n, d)
    n_gdn, n_hgrn = gdn_w_in.shape[0], hgrn_w_in.shape[0]

    r_out, r_down = gdn_w_out.shape[1], mlp_w_down.shape[1]
    c_gin, c_hin, c_up = gdn_w_in.shape[2], hgrn_w_in.shape[2], mlp_w_up.shape[2]

    def gathered(w, pick, land_shape):
        return _Item(w.astype(BF16), land_shape, _whole, pick)

    groups = [[_Item(gdn_conv, (N_DEV,) + gdn_conv.shape, _whole, _slot),
               _Item(hgrn_gnorm, (N_DEV,) + hgrn_gnorm.shape, _whole, _slot)]]
    for i in range(DEPTH):
        j = i // 2
        if i % 2 == 0:
            groups += [[gathered(gdn_w_in[j], _slot, (N_DEV, d, c_gin))],
                       [gathered(gdn_w_out[j], _rows_of(r_out), (N_DEV * r_out, d))]]
        else:
            groups += [[gathered(hgrn_w_in[j], _cols_of(c_hin), (d, N_DEV * c_hin))],
                       [gathered(hgrn_w_out[j], _rows_of(r_out), (N_DEV * r_out, d))]]
        groups += [[gathered(mlp_w_up[i], _cols_of(c_up), (d, N_DEV * c_up))],
                   [gathered(mlp_w_down[i], _rows_of(r_down), (N_DEV * r_down, d))]]
    gather_handles, token = _exchange_start(groups, "gather_start")
    lbs = _lb_fwd(hgrn_lb_logits + token[0:1, 0:1], "lb_fwd")

    def arrived(k, after, name):
        return _exchange_wait(gather_handles[k], after, "gather_wait_" + name)

    saved = []
    w_in, w_ab, w_out, w_up, w_down = ([None] * DEPTH for _ in range(5))
    h = x2
    for i in range(DEPTH):
        j = i // 2
        if i == 0:
            g_conv, g_gnorm = arrived(0, h, "small")
            conv_full = _unshard_cols(g_conv)
            gnorm_full = jnp.transpose(g_gnorm, (1, 0, 2)).reshape(n_hgrn, d)
        y = _rms_fwd(h, norm_mix[i:i + 1], f"rms_mix_{i}")
        (w_in[i],) = arrived(1 + 4 * i, y, f"in_{i}")
        if i % 2 == 0:
            w_gin = jnp.transpose(w_in[i], (1, 0, 2)).reshape(d, N_DEV * c_gin)
            w_in[i] = w_gin[:, :GDN_MAIN]
            w_ab[i] = jnp.pad(w_gin[:, GDN_MAIN:], ((0, 0), (0, AB_PAD - 2 * N_HEADS)))
            projm = _mm(y, w_in[i], "nn", [BF16], f"gdn_proj_{i}")
            projab = _mm(y, w_ab[i], "nn", [F32], f"gdn_proj_ab_{i}")
            o2, st_all = _gdn_fwd(projm, projab, conv_full[j], gdn_a_log[j:j + 1], gdn_dt_bias[j:j + 1],
                                  gdn_onorm[j:j + 1], seqs, f"gdn_fwd_{i}")
            mix = (projm, projab, st_all)
        else:
            proj = _mm(y, w_in[i], "nn", [BF16], f"hgrn_proj_{i}")
            o2, o_raw, st_all = _hgrn_fwd(proj, lbs[i:i + 1], gnorm_full[j:j + 1], seqs, f"hgrn_fwd_{i}")
            mix = (proj, o_raw, st_all)
        (w_out[i],) = arrived(2 + 4 * i, o2, f"out_{i}")
        h1 = _mm(o2, w_out[i], "nn", [F32], f"mix_out_{i}", epilogue=lambda acc, res: (res + acc,), extras=(h,))
        y2 = _rms_fwd(h1, norm_mlp[i:i + 1], f"rms_mlp_{i}")
        (w_up[i],) = arrived(3 + 4 * i, y2, f"up_{i}")
        u, act = _mm(y2, w_up[i], "nn", [BF16, BF16], f"mlp_up_{i}",
                     epilogue=lambda acc: (acc, jnp.square(jnp.maximum(acc, 0.0))))
        (w_down[i],) = arrived(4 + 4 * i, act, f"down_{i}")
        h2 = _mm(act, w_down[i], "nn", [F32], f"mlp_down_{i}", epilogue=lambda acc, res: (res + acc,), extras=(h1,))
        saved.append((h, y, mix, o2, h1, y2, u, act))
        h = h2

    dh, dh_b, d_nf, sq = _loss_head(h, norm_final.reshape(1, d), target, "loss_head")

    d_nmix, d_nmlp = [None] * DEPTH, [None] * DEPTH
    d_conv, d_alog, d_dtb, d_onorm = [None] * n_gdn, [None] * n_gdn, [None] * n_gdn, [None] * n_gdn
    d_lb = [jnp.zeros((1, d), F32)] * DEPTH
    d_gnorm = [None] * n_hgrn
    mlp_handles, mix_handles = [None] * DEPTH, [None] * DEPTH
    token = None
    for i in reversed(range(DEPTH)):
        j = i // 2
        h_in, y, mix, o2, h1, y2, u, act = saved[i]
        g_down = _mm(act, dh_b, "tn", [BF16], f"g_down_{i}", after=token)
        du = _mm(dh_b, w_down[i], "nt", [BF16], f"d_u_{i}",
                 epilogue=lambda acc, uu: (acc * (2.0 * jnp.maximum(uu.astype(F32), 0.0)),), extras=(u,))
        g_up = _mm(y2, du, "tn", [BF16], f"g_up_{i}")
        mlp_handles[i], token = _exchange_start(
            [[_Item(g_down, (N_DEV, r_down, d), _rows_of(r_down), _slot)],
             [_Item(g_up, (N_DEV, d, c_up), _cols_of(c_up), _slot)]], f"scatter_start_mlp_{i}")
        dy2 = _mm(du, w_up[i], "nt", [BF16], f"d_y2_{i}", after=token)
        dh1, dh1_b, d_nmlp[i] = _rms_bwd(h1, norm_mlp[i:i + 1], dy2, dh, f"rms_mlp_bwd_{i}")
        g_out = _mm(o2, dh1_b, "tn", [BF16], f"g_out_{i}")
        do2 = _mm(dh1_b, w_out[i], "nt", [BF16], f"d_o2_{i}")
        if i % 2 == 0:
            projm, projab, st_all = mix
            dpm, dpab, d_conv[j], d_alog[j], d_dtb[j], d_onorm[j] = _gdn_bwd(
                projm, projab, conv_full[j], gdn_a_log[j:j + 1], gdn_dt_bias[j:j + 1], gdn_onorm[j:j + 1],
                st_all, do2, seqs, f"gdn_bwd_{i}")
            g_main = _mm(y, dpm, "tn", [BF16], f"g_in_{i}")
            g_ab = _mm(y, dpab, "tn", [BF16], f"g_in_ab_{i}")
            g_in = jnp.concatenate([g_main, g_ab[:, :2 * N_HEADS]], axis=1)
            g_in = jnp.transpose(g_in.reshape(d, N_DEV, c_gin), (1, 0, 2))
            in_item = _Item(g_in, (N_DEV, d, c_gin), _slot, _slot)
            dy_ab = _mm(dpab, w_ab[i], "nt", [F32], f"d_y_ab_{i}")
            dy = _mm(dpm, w_in[i], "nt", [BF16], f"d_y_{i}", epilogue=lambda acc, e: (acc + e,), extras=(dy_ab,))
        else:
            proj, o_raw, st_all = mix
            dp, d_lb[i], d_gnorm[j] = _hgrn_bwd(proj, lbs[i:i + 1], gnorm_full[j:j + 1], st_all, o_raw, do2,
                                               seqs, f"hgrn_bwd_{i}")
            g_in = _mm(y, dp, "tn", [BF16], f"g_in_{i}")
            in_item = _Item(g_in, (N_DEV, d, c_hin), _cols_of(c_hin), _slot)
            dy = _mm(dp, w_in[i], "nt", [BF16], f"d_y_{i}")
        mix_handles[i], token = _exchange_start(
            [[_Item(g_out, (N_DEV, r_out, d), _rows_of(r_out), _slot)], [in_item]], f"scatter_start_mix_{i}")
        dh, dh_b, d_nmix[i] = _rms_bwd(h_in, norm_mix[i:i + 1], dy, dh1, f"rms_mix_bwd_{i}")
    grad_x = dh.reshape(x.shape)

    def landed(handles, k, layers, after, name):
        return [_exchange_wait(handles[i][k], after, f"scatter_wait_{name}_{i}")[0] for i in layers]

    every, even, odd = range(DEPTH), range(0, DEPTH, 2), range(1, DEPTH, 2)
    upd = {}
    upd["mlp_w_down"] = _adamw_slots(mlp_w_down, landed(mlp_handles, 0, every, dh, "down"), m_mlp_w_down,
                                     v_mlp_w_down, "adamw_mlp_w_down")
    upd["mlp_w_up"] = _adamw_slots(mlp_w_up, landed(mlp_handles, 1, every, upd["mlp_w_down"][1], "up"), m_mlp_w_up,
                                   v_mlp_w_up, "adamw_mlp_w_up")
    upd["hgrn_w_out"] = _adamw_slots(hgrn_w_out, landed(mix_handles, 0, odd, upd["mlp_w_up"][1], "out"),
                                     m_hgrn_w_out, v_hgrn_w_out, "adamw_hgrn_w_out")
    upd["hgrn_w_in"] = _adamw_slots(hgrn_w_in, landed(mix_handles, 1, odd, upd["hgrn_w_out"][1], "in"), m_hgrn_w_in,
                                    v_hgrn_w_in, "adamw_hgrn_w_in")
    upd["gdn_w_out"] = _adamw_slots(gdn_w_out, landed(mix_handles, 0, even, upd["hgrn_w_in"][1], "out"),
                                    m_gdn_w_out, v_gdn_w_out, "adamw_gdn_w_out")
    upd["gdn_w_in"] = _adamw_slots(gdn_w_in, landed(mix_handles, 1, even, upd["gdn_w_out"][1], "in"), m_gdn_w_in,
                                   v_gdn_w_in, "adamw_gdn_w_in")

    def update(name, w, g, m, v):
        shape = w.shape
        c = shape[-1]
        res = _adamw(w.reshape(-1, c), g.reshape(-1, c), m.reshape(-1, c), v.reshape(-1, c), "adamw_" + name)
        return [g.reshape(shape)] + [o.reshape(shape) for o in res]

    dlb_rows = jnp.concatenate(d_lb, axis=0)
    tail = jnp.concatenate(
        [jnp.concatenate(d_onorm, axis=1), jnp.concatenate(d_alog, axis=1), jnp.concatenate(d_dtb, axis=1)], axis=1)
    tail = jnp.pad(tail, ((0, 0), (0, d - tail.shape[1])))
    conv_rows = jnp.stack(d_conv).reshape(-1, d)
    packed = jnp.concatenate(
        [jnp.concatenate(d_nmix, axis=0), jnp.concatenate(d_nmlp, axis=0), d_nf, sq, dlb_rows,
         jnp.concatenate(d_gnorm, axis=0), tail, conv_rows], axis=0)
    pad_rows = (-packed.shape[0]) % 8
    packed = jnp.pad(packed, ((0, pad_rows), (0, 0)))
    tot = _all_reduce_small(packed, "reduce_small")
    r0 = 0
    g_nmix = tot[r0:r0 + DEPTH]; r0 += DEPTH
    g_nmlp = tot[r0:r0 + DEPTH]; r0 += DEPTH
    g_nf = tot[r0]; r0 += 1
    loss = tot[r0, 0]; r0 += 1
    g_lb = _lb_bwd(hgrn_lb_logits, tot[r0:r0 + DEPTH], "lb_bwd"); r0 += DEPTH
    g_gnorm_full = tot[r0:r0 + n_hgrn]; r0 += n_hgrn
    t_row = tot[r0]; r0 += 1
    g_conv_full = tot[r0:r0 + n_gdn * CONV_K * 3].reshape(n_gdn, CONV_K, 3 * d)
    g_onorm = t_row[0:n_gdn * HEAD_DIM].reshape(n_gdn, HEAD_DIM)
    o1 = n_gdn * HEAD_DIM
    g_alog = t_row[o1:o1 + n_gdn * N_HEADS].reshape(n_gdn, N_HEADS)
    g_dtb = t_row[o1 + n_gdn * N_HEADS:o1 + 2 * n_gdn * N_HEADS].reshape(n_gdn, N_HEADS)
    c_gn, c_cv = hgrn_gnorm.shape[1], gdn_conv.shape[2]
    g_gnorm = lax.dynamic_slice_in_dim(g_gnorm_full, me_i * c_gn, c_gn, axis=1)
    g_conv = lax.dynamic_slice_in_dim(g_conv_full, me_i * c_cv, c_cv, axis=2)

    upd["gdn_conv"] = update("gdn_conv", gdn_conv, g_conv, m_gdn_conv, v_gdn_conv)
    upd["gdn_a_log"] = update("gdn_a_log", gdn_a_log, g_alog, m_gdn_a_log, v_gdn_a_log)
    upd["gdn_dt_bias"] = update("gdn_dt_bias", gdn_dt_bias, g_dtb, m_gdn_dt_bias, v_gdn_dt_bias)
    upd["gdn_onorm"] = update("gdn_onorm", gdn_onorm, g_onorm, m_gdn_onorm, v_gdn_onorm)
    upd["hgrn_lb_logits"] = update("hgrn_lb_logits", hgrn_lb_logits, g_lb, m_hgrn_lb_logits, v_hgrn_lb_logits)
    upd["hgrn_gnorm"] = update("hgrn_gnorm", hgrn_gnorm, g_gnorm, m_hgrn_gnorm, v_hgrn_gnorm)
    upd["norm_mix"] = update("norm_mix", norm_mix, g_nmix, m_norm_mix, v_norm_mix)
    upd["norm_mlp"] = update("norm_mlp", norm_mlp, g_nmlp, m_norm_mlp, v_norm_mlp)
    upd["norm_final"] = update("norm_final", norm_final, g_nf, m_norm_final, v_norm_final)

    order = ["gdn_w_in", "gdn_conv", "gdn_a_log", "gdn_dt_bias", "gdn_onorm", "gdn_w_out", "hgrn_w_in",
             "hgrn_lb_logits", "hgrn_gnorm", "hgrn_w_out", "norm_mix", "norm_mlp", "mlp_w_up", "mlp_w_down",
             "norm_final"]
    outs = [loss, grad_x]
    for k in range(4):
        outs += [upd[name][k] for name in order]
    return tuple(outs)
```

```python
import functools

import numpy as np
import jax
import jax.numpy as jnp
from jax import lax
from jax.experimental import pallas as pl
from jax.experimental.pallas import tpu as pltpu

F32 = jnp.float32
BF16 = jnp.bfloat16

D_MODEL = 1024
N_HEADS = 8
HEAD_DIM = 128
CHUNK = 64
SUB = 16
N_SUB = CHUNK // SUB
CONV_K = 4
HALO = 16
EPS = 1e-6
DEPTH = 4
N_DEV = 8
GDN_MAIN = 4 * D_MODEL
GDN_IN = GDN_MAIN + 2 * N_HEADS
AB_PAD = 128
HEAD_GROUP = 8
LANE_BLOCK = 256
ROW_BLOCK = 16

ADAM_LR = 0.001
ADAM_B1 = 0.9
ADAM_B2 = 0.999
ADAM_EPS = 1e-08
ADAM_WD = 0.01
ADAM_STEP = 10

VMEM_LIMIT = 56 * 1024 * 1024
MM_TILE = 1024
MM_VMEM_BUDGET = 40 * 1024 * 1024

_DIMS = {
    "nn": (((1,), (0,)), ((), ())),
    "nt": (((1,), (1,)), ((), ())),
    "tn": (((0,), (0,)), ((), ())),
}


def _parts(x, n):
    if n == 1 and x.dtype == BF16:
        return [x]
    out = []
    r = x.astype(F32)
    for i in range(n):
        p = r.astype(BF16)
        out.append(p)
        if i + 1 < n:
            r = r - p.astype(F32)
    return out


def _dot_raw(a, b, mode, na, nb):
    ap, bp = _parts(a, na), _parts(b, nb)
    nmax = max(na, nb)
    dot = lambda x, y: lax.dot_general(x, y, _DIMS[mode], preferred_element_type=F32)
    acc = None
    if na > 1 and mode != "tn":
        m = a.shape[0]
        for j, xb in enumerate(bp):
            mine = ap[:nmax - j]
            t = dot(mine[0] if len(mine) == 1 else jnp.concatenate(mine, axis=0), xb)
            for i in range(len(mine)):
                acc = t[i * m:(i + 1) * m] if acc is None else acc + t[i * m:(i + 1) * m]
        return acc
    axis = 0 if mode == "nt" else 1
    n = b.shape[axis]
    for i, xa in enumerate(ap):
        mine = bp[:nmax - i]
        t = dot(xa, mine[0] if len(mine) == 1 else jnp.concatenate(mine, axis=axis))
        for j in range(len(mine)):
            acc = t[:, j * n:(j + 1) * n] if acc is None else acc + t[:, j * n:(j + 1) * n]
    return acc


@functools.partial(jax.custom_vjp, nondiff_argnums=(2, 3, 4))
def _dot(a, b, mode, na, nb):
    return _dot_raw(a, b, mode, na, nb)


def _dot_fwd(a, b, mode, na, nb):
    return _dot_raw(a, b, mode, na, nb), (a, b)


def _dot_bwd(mode, na, nb, res, ct):
    a, b = res
    nc = max(na, nb)
    if mode == "nn":
        da = _dot_raw(ct, b, "nt", nc, nb)
        db = _dot_raw(a, ct, "tn", na, nc)
    elif mode == "nt":
        da = _dot_raw(ct, b, "nn", nc, nb)
        db = _dot_raw(ct, a, "tn", nc, na)
    else:
        da = _dot_raw(b, ct, "nt", nb, nc)
        db = _dot_raw(a, ct, "nn", na, nc)
    return da.astype(a.dtype), db.astype(b.dtype)


_dot.defvjp(_dot_fwd, _dot_bwd)


N_EXACT = 3


@jax.custom_vjp
def _dot01(x, m_wide, m_tall):
    return lax.dot_general(m_wide, jnp.concatenate(_parts(x, N_EXACT), axis=0), _DIMS["nn"], preferred_element_type=F32)


def _dot01_fwd(x, m_wide, m_tall):
    return _dot01(x, m_wide, m_tall), (m_wide, m_tall)


def _dot01_bwd(res, ct):
    m_wide, m_tall = res
    dx = lax.dot_general(m_tall, jnp.concatenate(_parts(ct, N_EXACT), axis=0), _DIMS["tn"], preferred_element_type=F32)
    return dx, jnp.zeros_like(m_wide), jnp.zeros_like(m_tall)


_dot01.defvjp(_dot01_fwd, _dot01_bwd)


def _thrice(m):
    return jnp.concatenate([m] * N_EXACT, axis=1).astype(BF16), jnp.concatenate([m] * N_EXACT, axis=0).astype(BF16)


def _iota2(shape, dim):
    return lax.broadcasted_iota(jnp.int32, shape, dim)


def _tril_f32(n):
    return (_iota2((n, n), 0) >= _iota2((n, n), 1)).astype(F32)


def _cumsum_rows(g):
    return _dot(_tril_f32(g.shape[0]), g, "nn", 1, 3)


def _inv_unit_lower(L):
    n = L.shape[0]
    eye = (_iota2((n, n), 0) == _iota2((n, n), 1)).astype(F32)
    neg = -L
    s = eye + neg
    p = _dot_raw(neg, neg, "nn", 2, 2)
    m = 2
    while 2 * m < n:
        both = _dot_raw(jnp.concatenate([p, s], axis=0), p, "nn", 2, 2)
        p, s = both[0:n], s + both[n:2 * n]
        m *= 2
    return s + _dot_raw(s, p, "nn", 2, 2)


@jax.custom_vjp
def _solve_unit_lower(L, rhs):
    return _dot_raw(_inv_unit_lower(L), rhs, "nn", 2, 2)


def _solve_fwd(L, rhs):
    t = _inv_unit_lower(L)
    sol = _dot_raw(t, rhs, "nn", 2, 2)
    return sol, (t, sol)


def _solve_bwd(res, ct):
    t, sol = res
    y = _dot_raw(t, ct, "tn", 2, 2)
    return -_dot_raw(y, sol, "nt", 2, 2), y


_solve_unit_lower.defvjp(_solve_fwd, _solve_bwd)


def _softplus(x):
    return jnp.maximum(x, 0.0) + jnp.log1p(jnp.exp(-jnp.abs(x)))


def _rms(x, w):
    return x * lax.rsqrt(jnp.mean(x * x, axis=-1, keepdims=True) + EPS) * w


HG_LEVELS = (32, 16, 8, 4, 2, 1)


def _hg_level_sums():
    i = np.arange(CHUNK)[:, None]
    m = np.arange(CHUNK)[None, :]
    to_row = [(m <= i) & (m // b == i // b) for b in HG_LEVELS]
    to_col = [(m > i) & (m // b == i // b) for b in HG_LEVELS]
    return _thrice(jnp.asarray(np.concatenate(to_row + to_col + [m <= i]), F32))


def _hg_level_masks():
    i = np.arange(CHUNK)[:, None]
    j = np.arange(CHUNK)[None, :]
    return jnp.asarray(np.stack([(i // b == j // b + 1) & ((i // b) % 2 == 1) for b in HG_LEVELS]), F32)


def _hg_pre(qraw, f, lb, sums):
    g = jnp.log(lb + (1.0 - lb) * jax.nn.sigmoid(f))
    k = (1.0 - lb) * jax.nn.sigmoid(-f)
    q = jax.nn.silu(qraw) * (HEAD_DIM ** -0.5)
    return q, k, _dot01(g, *sums)


def _hg_head(st, q, k, v, e, masks):
    nl = len(HG_LEVELS)
    eye = (_iota2((CHUNK, CHUNK), 0) == _iota2((CHUNK, CHUNK), 1)).astype(F32)
    a = eye * jnp.sum(q * k, axis=-1, keepdims=True)
    for l in range(nl):
        rows = q * jnp.exp(e[l * CHUNK:(l + 1) * CHUNK])
        cols = k * jnp.exp(e[(nl + l) * CHUNK:(nl + l + 1) * CHUNK])
        a = a + masks[l] * _dot(rows, cols, "nt", 1, 1)
    gc = e[2 * nl * CHUNK:(2 * nl + 1) * CHUNK]
    o = _dot(a, v, "nn", 1, 1) + _dot(q * jnp.exp(gc), st, "nt", 1, 1)
    g_last = gc[CHUNK - 1:CHUNK]
    st_new = st * jnp.exp(g_last) + _dot(v, k * jnp.exp(g_last - gc), "tn", 1, 1)
    return o, st_new


_HG_HEADS = jax.vmap(_hg_head, in_axes=(0, 0, 0, 0, 0, None))


def _hg_post(o, gate, gw):
    return _rms(o, gw) * jax.nn.silu(gate)


def _gd_conv(xp, cw):
    off = HALO - (CONV_K - 1)
    y = cw[0:1] * xp[off:off + CHUNK]
    for kk in range(1, CONV_K):
        y = y + cw[kk:kk + 1] * xp[off + kk:off + kk + CHUNK]
    return jax.nn.silu(y)


def _gd_gates(a, b, alog, dtb):
    beta = jax.nn.sigmoid(b)
    g = -jnp.exp(alog) * _softplus(a + dtb)
    expand = (_iota2((N_HEADS, D_MODEL), 1) // HEAD_DIM == _iota2((N_HEADS, D_MODEL), 0)).astype(F32)
    g_x = _dot(g, expand, "nn", 3, 1)
    after = (_iota2((CHUNK, D_MODEL), 0) > _iota2((CHUNK, D_MODEL), 1) % HEAD_DIM).astype(F32)
    sums = _dot01(jnp.concatenate([g_x, g_x * after], axis=1), *_thrice(_tril_f32(CHUNK)))
    return _dot(beta, expand, "nn", 3, 1), sums


def _gd_head(st, q, k, v, beta, gc, diff, gate, onw):
    q = q * lax.rsqrt(jnp.sum(q * q, axis=-1, keepdims=True) + EPS) * (HEAD_DIM ** -0.5)
    k = k * lax.rsqrt(jnp.sum(k * k, axis=-1, keepdims=True) + EPS)
    ri = _iota2((CHUNK, CHUNK), 0)
    ci = _iota2((CHUNK, CHUNK), 1)
    decay = jnp.exp(jnp.where(ri >= ci, diff[:, 0:CHUNK], -jnp.inf))
    kb = k * beta
    egc = jnp.exp(gc)
    L = jnp.where(ri > ci, _dot(kb, k, "nt", 1, 1) * decay, 0.0)
    sol = _solve_unit_lower(L, jnp.concatenate([v * beta, kb * egc], axis=1))
    u = sol[:, 0:HEAD_DIM]
    w = sol[:, HEAD_DIM:2 * HEAD_DIM]
    a_qk = jnp.where(ri >= ci, _dot(q, k, "nt", 1, 1) * decay, 0.0)
    g_last = gc[CHUNK - 1:CHUNK]
    v_new = u - _dot(w, st, "nt", 1, 1)
    o = _dot(q * egc, st, "nt", 1, 1) + _dot(a_qk, v_new, "nn", 1, 1)
    st_new = st * jnp.exp(g_last) + _dot(v_new, k * jnp.exp(g_last - gc), "tn", 1, 1)
    return _rms(o, onw) * jax.nn.silu(gate), st_new


def _params(*sem):
    return pltpu.CompilerParams(dimension_semantics=sem, vmem_limit_bytes=VMEM_LIMIT)


def _tile(n, pref):
    t = min(n, pref)
    assert n % t == 0, (n, pref)
    return t


def _mm_tiles(m, n, k, a_size, b_size, tile_sizes):
    tm, tn, tk = _tile(m, MM_TILE), _tile(n, MM_TILE), k

    def need(tm, tn, tk):
        acc = 4 * tm * tn * (2 if tk < k else 1)
        return 2 * (tm * tk * a_size + tk * tn * b_size + tm * tn * sum(tile_sizes)) + acc

    while need(tm, tn, tk) > MM_VMEM_BUDGET:
        if tk > 2048 or (tk > 512 and tm <= 512):
            tk //= 2
        else:
            tm //= 2
    return tm, tn, tk


def _mm(a, b, mode, out_dtypes, name, epilogue=None, extras=(), after=None):
    if mode == "nn":
        (m, k), (k2, n) = a.shape, b.shape
    elif mode == "nt":
        (m, k), (n, k2) = a.shape, b.shape
    else:
        (k, m), (k2, n) = a.shape, b.shape
    assert k == k2, (a.shape, b.shape, mode)
    tm, tn, tk = _mm_tiles(m, n, k, a.dtype.itemsize, b.dtype.itemsize,
                           [e.dtype.itemsize for e in extras] + [jnp.dtype(dt).itemsize for dt in out_dtypes])
    nk = k // tk
    ne, no, nafter = len(extras), len(out_dtypes), int(after is not None)
    if epilogue is None:
        epilogue = lambda acc: (acc,)

    def body(*refs):
        a_ref, b_ref = refs[0], refs[1]
        ex = refs[2:2 + ne]
        outs = refs[2 + ne + nafter:2 + ne + nafter + no]
        part = lax.dot_general(a_ref[...].astype(BF16), b_ref[...].astype(BF16), _DIMS[mode],
                               preferred_element_type=F32)

        def finish(acc):
            for o_ref, val in zip(outs, epilogue(acc, *[e[...] for e in ex])):
                o_ref[...] = val.astype(o_ref.dtype)

        if nk == 1:
            finish(part)
        else:
            acc_ref = refs[-1]
            kk = pl.program_id(2)

            @pl.when(kk == 0)
            def _():
                acc_ref[...] = part

            @pl.when(kk > 0)
            def _():
                acc_ref[...] += part

            @pl.when(kk == nk - 1)
            def _():
                finish(acc_ref[...])

    if mode == "tn":
        a_spec = pl.BlockSpec((tk, tm), lambda i, j, kk: (kk, i))
    else:
        a_spec = pl.BlockSpec((tm, tk), lambda i, j, kk: (i, kk))
    if mode == "nt":
        b_spec = pl.BlockSpec((tn, tk), lambda i, j, kk: (j, kk))
    else:
        b_spec = pl.BlockSpec((tk, tn), lambda i, j, kk: (kk, j))
    o_spec = pl.BlockSpec((tm, tn), lambda i, j, kk: (i, j))
    res = pl.pallas_call(
        body,
        name=name,
        grid=(m // tm, n // tn, nk),
        in_specs=[a_spec, b_spec] + [o_spec] * ne + [pl.BlockSpec(memory_space=pl.ANY)] * nafter,
        out_specs=[o_spec] * no,
        out_shape=[jax.ShapeDtypeStruct((m, n), dt) for dt in out_dtypes],
        scratch_shapes=[pltpu.VMEM((tm, tn), F32)] if nk > 1 else [],
        compiler_params=_params("parallel", "parallel", "arbitrary"),
    )(a, b, *extras, *([after] if nafter else []))
    return res[0] if no == 1 else res


def _rms_fwd(x, w, name, tm=512):
    n, d = x.shape
    tm = _tile(n, tm)

    def body(x_ref, w_ref, y_ref):
        y_ref[...] = _rms(x_ref[...], w_ref[...]).astype(y_ref.dtype)

    return pl.pallas_call(
        body, name=name, grid=(n // tm,),
        in_specs=[pl.BlockSpec((tm, d), lambda i: (i, 0)), pl.BlockSpec((1, d), lambda i: (0, 0))],
        out_specs=pl.BlockSpec((tm, d), lambda i: (i, 0)),
        out_shape=jax.ShapeDtypeStruct((n, d), BF16),
        compiler_params=_params("arbitrary"),
    )(x, w)


def _rms_bwd(x, w, dy, dres, name, tm=512):
    n, d = x.shape
    tm = _tile(n, tm)

    def body(x_ref, w_ref, dy_ref, dres_ref, dx_ref, dxb_ref, dw_ref):
        _, vjp = jax.vjp(_rms, x_ref[...], w_ref[...])
        dx, dw = vjp(dy_ref[...].astype(F32))
        dx = dres_ref[...] + dx
        dx_ref[...] = dx
        dxb_ref[...] = dx.astype(dxb_ref.dtype)

        @pl.when(pl.program_id(0) == 0)
        def _():
            dw_ref[...] = dw

        @pl.when(pl.program_id(0) > 0)
        def _():
            dw_ref[...] += dw

    row = pl.BlockSpec((tm, d), lambda i: (i, 0))
    vec = pl.BlockSpec((1, d), lambda i: (0, 0))
    return pl.pallas_call(
        body, name=name, grid=(n // tm,),
        in_specs=[row, vec, row, row],
        out_specs=[row, row, vec],
        out_shape=[jax.ShapeDtypeStruct((n, d), F32), jax.ShapeDtypeStruct((n, d), BF16),
                   jax.ShapeDtypeStruct((1, d), F32)],
        compiler_params=_params("arbitrary"),
    )(x, w, dy, dres)


def _loss_head(h, w, target, name, tm=512):
    n, d = h.shape
    tm = _tile(n, tm)

    def body(h_ref, w_ref, t_ref, dh_ref, dhb_ref, dw_ref, sq_ref):
        y, vjp = jax.vjp(_rms, h_ref[...], w_ref[...])
        err = y - t_ref[...]
        dh, dw = vjp(err * (1.0 / d))
        dh_ref[...] = dh
        dhb_ref[...] = dh.astype(dhb_ref.dtype)
        sq = jnp.sum(err * err, axis=0, keepdims=True)

        @pl.when(pl.program_id(0) == 0)
        def _():
            dw_ref[...] = dw
            sq_ref[...] = sq

        @pl.when(pl.program_id(0) > 0)
        def _():
            dw_ref[...] += dw
            sq_ref[...] += sq

        @pl.when(pl.program_id(0) == n // tm - 1)
        def _():
            total = jnp.sum(sq_ref[...], axis=1, keepdims=True) * (0.5 / d)
            sq_ref[...] = jnp.broadcast_to(total, sq_ref.shape)

    row = pl.BlockSpec((tm, d), lambda i: (i, 0))
    vec = pl.BlockSpec((1, d), lambda i: (0, 0))
    return pl.pallas_call(
        body, name=name, grid=(n // tm,),
        in_specs=[row, vec, row],
        out_specs=[row, row, vec, vec],
        out_shape=[jax.ShapeDtypeStruct((n, d), F32), jax.ShapeDtypeStruct((n, d), BF16),
                   jax.ShapeDtypeStruct((1, d), F32), jax.ShapeDtypeStruct((1, d), F32)],
        compiler_params=_params("arbitrary"),
    )(h, w, target)


def _lower_bounds(logits):
    sm = jax.nn.softmax(logits, axis=0)
    rows = [sm[0:1] * 0.0]
    for r in range(1, DEPTH):
        rows.append(rows[-1] + sm[r:r + 1])
    return jnp.concatenate(rows, axis=0)


def _lb_fwd(logits, name):
    def body(l_ref, o_ref):
        o_ref[...] = _lower_bounds(l_ref[...])

    return pl.pallas_call(body, name=name, out_shape=jax.ShapeDtypeStruct(logits.shape, F32))(logits)


def _lb_bwd(logits, dlb, name):
    def body(l_ref, d_ref, o_ref):
        _, vjp = jax.vjp(_lower_bounds, l_ref[...])
        (o_ref[...],) = vjp(d_ref[...])

    return pl.pallas_call(body, name=name, out_shape=jax.ShapeDtypeStruct(logits.shape, F32))(logits, dlb)


def _head_slice(h):
    if isinstance(h, int):
        return pl.ds(h * HEAD_DIM, HEAD_DIM)
    return pl.ds(pl.multiple_of(h * HEAD_DIM, HEAD_DIM), HEAD_DIM)


def _head_groups(group_body):
    if HEAD_GROUP == N_HEADS:
        group_body(list(range(N_HEADS)))
        return

    def trip(i, carry):
        group_body([i * HEAD_GROUP + t for t in range(HEAD_GROUP)])
        return carry

    lax.fori_loop(0, N_HEADS // HEAD_GROUP, trip, 0)


def _stack_heads(ref, hs, first=0):
    return jnp.stack([ref[:, _head_slice(h + first)] for h in hs])


def _unstack_heads(ref, hs, val, first=0):
    for t, h in enumerate(hs):
        ref[:, _head_slice(h + first)] = val[t].astype(ref.dtype)


_GD_HEADS = jax.vmap(_gd_head, in_axes=(0, 0, 0, 0, 0, 0, 0, 0, None))


def _hgrn_fwd(proj, lb, gw, seqs, name):
    n = proj.shape[0]
    nc = n // seqs // CHUNK
    d = D_MODEL

    sums, masks = _hg_level_sums(), _hg_level_masks()

    def body(p_ref, lb_ref, gw_ref, sums_wide_ref, sums_tall_ref, masks_ref, o2_ref, o_ref, st_all_ref,
             st_sc, q_sc, k_sc, v_sc, e_sc):
        @pl.when(pl.program_id(1) == 0)
        def _():
            st_sc[...] = jnp.zeros_like(st_sc)

        sums_refs = (sums_wide_ref, sums_tall_ref)
        _lane_blocks(d, functools.partial(_hg_pre_block, p_ref, lb_ref, sums_refs, q_sc, k_sc, v_sc, e_sc))
        st_all_ref[0] = st_sc[...]

        def group(hs):
            sts = pl.ds(hs[0], len(hs))
            o, st_new = _HG_HEADS(st_sc[sts], *[_stack_heads(r, hs) for r in (q_sc, k_sc, v_sc, e_sc)], masks_ref[...])
            _unstack_heads(o_ref, hs, o)
            st_sc[sts] = st_new

        _head_groups(group)

        def post(rows):
            gate = p_ref[rows, 3 * d:4 * d].astype(F32)
            o2_ref[rows, :] = _hg_post(o_ref[rows, :], gate, gw_ref[...]).astype(o2_ref.dtype)

        _row_blocks(CHUNK, post)

    idx = lambda b, c: (b * nc + c, 0)
    vec = pl.BlockSpec((1, d), lambda b, c: (0, 0))
    act = pl.BlockSpec((CHUNK, d), idx)
    return pl.pallas_call(
        body, name=name, grid=(seqs, nc),
        in_specs=[pl.BlockSpec((CHUNK, 4 * d), idx), vec, vec] + [pl.BlockSpec(s.shape, lambda b, c: (0, 0)) for s in sums]
        + [pl.BlockSpec(masks.shape, lambda b, c: (0, 0, 0))],
        out_specs=[act, act, pl.BlockSpec((1, N_HEADS, HEAD_DIM, HEAD_DIM), lambda b, c: (b * nc + c, 0, 0, 0))],
        out_shape=[jax.ShapeDtypeStruct((n, d), BF16), jax.ShapeDtypeStruct((n, d), F32),
                   jax.ShapeDtypeStruct((n // CHUNK, N_HEADS, HEAD_DIM, HEAD_DIM), F32)],
        scratch_shapes=[pltpu.VMEM((N_HEADS, HEAD_DIM, HEAD_DIM), F32)] + [pltpu.VMEM((CHUNK, d), F32)] * 3
        + [pltpu.VMEM((sums[0].shape[0], d), F32)],
        compiler_params=_params("arbitrary", "arbitrary"),
    )(proj, lb, gw, *sums, masks)


def _hgrn_bwd(proj, lb, gw, st_all, o, do2, seqs, name):
    n = proj.shape[0]
    nc = n // seqs // CHUNK
    d = D_MODEL

    sums, masks = _hg_level_sums(), _hg_level_masks()

    def body(p_ref, lb_ref, gw_ref, sums_wide_ref, sums_tall_ref, masks_ref, st_all_ref, o_ref, do2_ref,
             dp_ref, dlb_ref, dgw_ref,
             dst_sc, q_sc, k_sc, v_sc, e_sc, do_sc, dq_sc, dk_sc, dv_sc, de_sc, dgw_sc):
        first = (pl.program_id(0) == 0) & (pl.program_id(1) == 0)

        @pl.when(pl.program_id(1) == 0)
        def _():
            dst_sc[...] = jnp.zeros_like(dst_sc)

        sums_refs = (sums_wide_ref, sums_tall_ref)
        _lane_blocks(d, functools.partial(_hg_pre_block, p_ref, lb_ref, sums_refs, q_sc, k_sc, v_sc, e_sc))
        dgw_sc[...] = jnp.zeros_like(dgw_sc)

        def post_bwd(rows):
            _, vjp = jax.vjp(_hg_post, o_ref[rows, :], p_ref[rows, 3 * d:4 * d].astype(F32), gw_ref[...])
            do_sc[rows, :], dgate, dgw = vjp(do2_ref[rows, :].astype(F32))
            dp_ref[rows, 3 * d:4 * d] = dgate.astype(dp_ref.dtype)
            dgw_sc[...] += dgw

        _row_blocks(CHUNK, post_bwd)

        def group(hs):
            sts = pl.ds(hs[0], len(hs))
            level_masks = masks_ref[...]
            _, vjp = jax.vjp(lambda *a: _HG_HEADS(*a, level_masks), st_all_ref[0, sts],
                             *[_stack_heads(r, hs) for r in (q_sc, k_sc, v_sc, e_sc)])
            grads = vjp((_stack_heads(do_sc, hs), dst_sc[sts]))
            dst_sc[sts] = grads[0]
            for r, val in zip((dq_sc, dk_sc, dv_sc, de_sc), grads[1:]):
                _unstack_heads(r, hs, val)

        _head_groups(group)

        def pre_bwd(at):
            sl = at()
            level_sums = (sums_wide_ref[...], sums_tall_ref[...])
            _, vjp = jax.vjp(lambda qraw, f, lb: _hg_pre(qraw, f, lb, level_sums), p_ref[:, sl].astype(F32),
                             p_ref[:, at(d)].astype(F32), lb_ref[:, sl])
            dqraw, df, dlb = vjp((dq_sc[:, sl], dk_sc[:, sl], de_sc[:, sl]))
            dp_ref[:, sl] = dqraw.astype(dp_ref.dtype)
            dp_ref[:, at(d)] = df.astype(dp_ref.dtype)
            dp_ref[:, at(2 * d)] = dv_sc[:, sl].astype(dp_ref.dtype)

            @pl.when(first)
            def _():
                dlb_ref[:, sl] = dlb

            @pl.when(jnp.logical_not(first))
            def _():
                dlb_ref[:, sl] += dlb

        _lane_blocks(d, pre_bwd)

        @pl.when(first)
        def _():
            dgw_ref[...] = dgw_sc[...]

        @pl.when(jnp.logical_not(first))
        def _():
            dgw_ref[...] += dgw_sc[...]

    idx = lambda b, c: (b * nc + nc - 1 - c, 0)
    vec = pl.BlockSpec((1, d), lambda b, c: (0, 0))
    act = pl.BlockSpec((CHUNK, d), idx)
    wide = pl.BlockSpec((CHUNK, 4 * d), idx)
    return pl.pallas_call(
        body, name=name, grid=(seqs, nc),
        in_specs=[wide, vec, vec] + [pl.BlockSpec(s.shape, lambda b, c: (0, 0)) for s in sums] + [
                  pl.BlockSpec(masks.shape, lambda b, c: (0, 0, 0)),
                  pl.BlockSpec((1, N_HEADS, HEAD_DIM, HEAD_DIM), lambda b, c: (b * nc + nc - 1 - c, 0, 0, 0)),
                  act, act],
        out_specs=[wide, vec, vec],
        out_shape=[jax.ShapeDtypeStruct((n, 4 * d), BF16), jax.ShapeDtypeStruct((1, d), F32),
                   jax.ShapeDtypeStruct((1, d), F32)],
        scratch_shapes=[pltpu.VMEM((N_HEADS, HEAD_DIM, HEAD_DIM), F32)]
        + [pltpu.VMEM((CHUNK, d), F32)] * 3 + [pltpu.VMEM((sums[0].shape[0], d), F32)]
        + [pltpu.VMEM((CHUNK, d), F32)] * 4 + [pltpu.VMEM((sums[0].shape[0], d), F32), pltpu.VMEM((1, d), F32)],
        compiler_params=_params("arbitrary", "arbitrary"),
    )(proj, lb, gw, *sums, masks, st_all, o, do2)


def _lane_blocks(width, block_body):
    def trip(j, carry):
        block_body(lambda base=0: pl.ds(pl.multiple_of(j * LANE_BLOCK + base, LANE_BLOCK), LANE_BLOCK))
        return carry

    lax.fori_loop(0, width // LANE_BLOCK, trip, 0)


def _row_blocks(rows, block_body):
    def trip(j, carry):
        block_body(pl.ds(pl.multiple_of(j * ROW_BLOCK, ROW_BLOCK), ROW_BLOCK))
        return carry

    lax.fori_loop(0, rows // ROW_BLOCK, trip, 0)


def _hg_pre_block(p_ref, lb_ref, sums_refs, q_sc, k_sc, v_sc, e_sc, at):
    sl = at()
    q_sc[:, sl], k_sc[:, sl], e_sc[:, sl] = _hg_pre(
        p_ref[:, sl].astype(F32), p_ref[:, at(D_MODEL)].astype(F32), lb_ref[:, sl], [r[...] for r in sums_refs])
    v_sc[:, sl] = p_ref[:, at(2 * D_MODEL)].astype(F32)


def _gd_xp(halo_ref, p_ref, sl, first_chunk):
    halo = jnp.where(first_chunk, 0.0, halo_ref[:, sl].astype(F32))
    return jnp.concatenate([halo, p_ref[:, sl].astype(F32)], axis=0)


def _gdn_fwd(projm, projab, cw, alog, dtb, onw, seqs, name):
    n = projm.shape[0]
    nc = n // seqs // CHUNK
    d = D_MODEL
    per_halo = CHUNK // HALO

    def body(p_ref, halo_ref, ab_ref, cw_ref, alog_ref, dtb_ref, onw_ref, o2_ref, st_all_ref,
             st_sc, c_sc, beta_sc, g_sc):
        @pl.when(pl.program_id(1) == 0)
        def _():
            st_sc[...] = jnp.zeros_like(st_sc)

        def conv(at):
            sl = at()
            c_sc[:, sl] = _gd_conv(_gd_xp(halo_ref, p_ref, sl, pl.program_id(1) == 0), cw_ref[:, sl])

        _lane_blocks(3 * d, conv)
        beta_sc[...], g_sc[...] = _gd_gates(ab_ref[:, 0:N_HEADS], ab_ref[:, N_HEADS:2 * N_HEADS], alog_ref[...],
                                            dtb_ref[...])
        st_all_ref[0] = st_sc[...]

        def group(hs):
            sts = pl.ds(hs[0], len(hs))
            o2, st_new = _GD_HEADS(
                st_sc[sts], _stack_heads(c_sc, hs), _stack_heads(c_sc, hs, N_HEADS), _stack_heads(c_sc, hs, 2 * N_HEADS),
                _stack_heads(beta_sc, hs), _stack_heads(g_sc, hs), _stack_heads(g_sc, hs, N_HEADS),
                _stack_heads(p_ref, hs, 3 * N_HEADS).astype(F32), onw_ref[...])
            _unstack_heads(o2_ref, hs, o2)
            st_sc[sts] = st_new

        _head_groups(group)

    idx = lambda b, c: (b * nc + c, 0)
    const = lambda b, c: (0, 0)
    return pl.pallas_call(
        body, name=name, grid=(seqs, nc),
        in_specs=[pl.BlockSpec((CHUNK, 4 * d), idx),
                  pl.BlockSpec((HALO, 3 * d), lambda b, c: (jnp.maximum((b * nc + c) * per_halo - 1, 0), 0)),
                  pl.BlockSpec((CHUNK, AB_PAD), idx),
                  pl.BlockSpec((CONV_K, 3 * d), const), pl.BlockSpec((1, N_HEADS), const),
                  pl.BlockSpec((1, N_HEADS), const), pl.BlockSpec((1, HEAD_DIM), const)],
        out_specs=[pl.BlockSpec((CHUNK, d), idx),
                   pl.BlockSpec((1, N_HEADS, HEAD_DIM, HEAD_DIM), lambda b, c: (b * nc + c, 0, 0, 0))],
        out_shape=[jax.ShapeDtypeStruct((n, d), BF16),
                   jax.ShapeDtypeStruct((n // CHUNK, N_HEADS, HEAD_DIM, HEAD_DIM), F32)],
        scratch_shapes=[pltpu.VMEM((N_HEADS, HEAD_DIM, HEAD_DIM), F32), pltpu.VMEM((CHUNK, 3 * d), F32),
                        pltpu.VMEM((CHUNK, d), F32), pltpu.VMEM((CHUNK, 2 * d), F32)],
        compiler_params=_params("arbitrary", "arbitrary"),
    )(projm, projm, projab, cw, alog, dtb, onw)


def _gdn_bwd(projm, projab, cw, alog, dtb, onw, st_all, do2, seqs, name):
    n = projm.shape[0]
    nc = n // seqs // CHUNK
    d = D_MODEL
    per_halo = CHUNK // HALO

    def body(p_ref, halo_ref, ab_ref, cw_ref, alog_ref, dtb_ref, onw_ref, st_all_ref, do2_ref,
             dp_ref, dab_ref, dcw_ref, dalog_ref, ddtb_ref, donw_ref,
             dst_sc, dhalo_sc, c_sc, beta_sc, g_sc, dc_sc, dbeta_sc, dg_sc, donw_sc):
        step = pl.program_id(1)
        first = (pl.program_id(0) == 0) & (step == 0)

        @pl.when(step == 0)
        def _():
            dst_sc[...] = jnp.zeros_like(dst_sc)
            dhalo_sc[...] = jnp.zeros_like(dhalo_sc)

        donw_sc[...] = jnp.zeros_like(donw_sc)

        def conv(at):
            sl = at()
            c_sc[:, sl] = _gd_conv(_gd_xp(halo_ref, p_ref, sl, step == nc - 1), cw_ref[:, sl])

        _lane_blocks(3 * d, conv)
        (beta_sc[...], g_sc[...]), gates_vjp = jax.vjp(
            _gd_gates, ab_ref[:, 0:N_HEADS], ab_ref[:, N_HEADS:2 * N_HEADS], alog_ref[...], dtb_ref[...])

        def group(hs):
            sts = pl.ds(hs[0], len(hs))
            _, vjp = jax.vjp(
                _GD_HEADS, st_all_ref[0, sts], _stack_heads(c_sc, hs), _stack_heads(c_sc, hs, N_HEADS),
                _stack_heads(c_sc, hs, 2 * N_HEADS), _stack_heads(beta_sc, hs), _stack_heads(g_sc, hs),
                _stack_heads(g_sc, hs, N_HEADS), _stack_heads(p_ref, hs, 3 * N_HEADS).astype(F32), onw_ref[...])
            dst, dq, dk, dv, dbeta, dg, ddiff, dgate, donw = vjp(
                (_stack_heads(do2_ref, hs).astype(F32), dst_sc[sts]))
            _unstack_heads(dg_sc, hs, ddiff, N_HEADS)
            dst_sc[sts] = dst
            _unstack_heads(dc_sc, hs, dq)
            _unstack_heads(dc_sc, hs, dk, N_HEADS)
            _unstack_heads(dc_sc, hs, dv, 2 * N_HEADS)
            _unstack_heads(dbeta_sc, hs, dbeta)
            _unstack_heads(dg_sc, hs, dg)
            _unstack_heads(dp_ref, hs, dgate, 3 * N_HEADS)
            donw_sc[...] += donw

        _head_groups(group)
        def conv_bwd(at):
            sl = at()
            _, vjp = jax.vjp(_gd_conv, _gd_xp(halo_ref, p_ref, sl, step == nc - 1), cw_ref[:, sl])
            dxp, dcw = vjp(dc_sc[:, sl])
            dqkv = jnp.concatenate([dxp[HALO:CHUNK], dxp[CHUNK:HALO + CHUNK] + dhalo_sc[:, sl]], axis=0)
            dp_ref[:, sl] = dqkv.astype(dp_ref.dtype)
            dhalo_sc[:, sl] = dxp[0:HALO]

            @pl.when(first)
            def _():
                dcw_ref[:, sl] = dcw

            @pl.when(jnp.logical_not(first))
            def _():
                dcw_ref[:, sl] += dcw

        _lane_blocks(3 * d, conv_bwd)
        da, db, dalog, ddtb = gates_vjp((dbeta_sc[...], dg_sc[...]))
        dab_ref[...] = jnp.concatenate(
            [da, db, jnp.zeros((CHUNK, AB_PAD - 2 * N_HEADS), F32)], axis=1).astype(dab_ref.dtype)

        @pl.when(first)
        def _():
            dalog_ref[...] = dalog
            ddtb_ref[...] = ddtb
            donw_ref[...] = donw_sc[...]

        @pl.when(jnp.logical_not(first))
        def _():
            dalog_ref[...] += dalog
            ddtb_ref[...] += ddtb
            donw_ref[...] += donw_sc[...]

    rev = lambda b, c: b * nc + nc - 1 - c
    idx = lambda b, c: (rev(b, c), 0)
    const = lambda b, c: (0, 0)
    small = [pl.BlockSpec((CONV_K, 3 * d), const), pl.BlockSpec((1, N_HEADS), const),
             pl.BlockSpec((1, N_HEADS), const), pl.BlockSpec((1, HEAD_DIM), const)]
    return pl.pallas_call(
        body, name=name, grid=(seqs, nc),
        in_specs=[pl.BlockSpec((CHUNK, 4 * d), idx),
                  pl.BlockSpec((HALO, 3 * d), lambda b, c: (jnp.maximum(rev(b, c) * per_halo - 1, 0), 0)),
                  pl.BlockSpec((CHUNK, AB_PAD), idx)] + small + [
                  pl.BlockSpec((1, N_HEADS, HEAD_DIM, HEAD_DIM), lambda b, c: (rev(b, c), 0, 0, 0)),
                  pl.BlockSpec((CHUNK, d), idx)],
        out_specs=[pl.BlockSpec((CHUNK, 4 * d), idx), pl.BlockSpec((CHUNK, AB_PAD), idx)] + small,
        out_shape=[jax.ShapeDtypeStruct((n, 4 * d), BF16), jax.ShapeDtypeStruct((n, AB_PAD), BF16),
                   jax.ShapeDtypeStruct((CONV_K, 3 * d), F32), jax.ShapeDtypeStruct((1, N_HEADS), F32),
                   jax.ShapeDtypeStruct((1, N_HEADS), F32), jax.ShapeDtypeStruct((1, HEAD_DIM), F32)],
        scratch_shapes=[pltpu.VMEM((N_HEADS, HEAD_DIM, HEAD_DIM), F32), pltpu.VMEM((HALO, 3 * d), F32),
                        pltpu.VMEM((CHUNK, 3 * d), F32), pltpu.VMEM((CHUNK, d), F32), pltpu.VMEM((CHUNK, 2 * d), F32),
                        pltpu.VMEM((CHUNK, 3 * d), F32), pltpu.VMEM((CHUNK, d), F32), pltpu.VMEM((CHUNK, 2 * d), F32),
                        pltpu.VMEM((1, HEAD_DIM), F32)],
        compiler_params=_params("arbitrary", "arbitrary"),
    )(projm, projm, projab, cw, alog, dtb, onw, st_all, do2)


def _adam_update(w, g, m, v):
    b1c = 1.0 - ADAM_B1 ** ADAM_STEP
    b2c = 1.0 - ADAM_B2 ** ADAM_STEP
    m_new = ADAM_B1 * m + (1.0 - ADAM_B1) * g
    v_new = ADAM_B2 * v + (1.0 - ADAM_B2) * (g * g)
    delta = -ADAM_LR * ((m_new / b1c) / (jnp.sqrt(v_new / b2c) + ADAM_EPS) + ADAM_WD * w)
    return delta, m_new, v_new


def _adamw(w, g, m, v, name, tr=256):
    r, c = w.shape
    tr = _tile(r, tr)

    def body(w_ref, g_ref, m_ref, v_ref, d_ref, mo_ref, vo_ref):
        d_ref[...], mo_ref[...], vo_ref[...] = _adam_update(w_ref[...], g_ref[...], m_ref[...], v_ref[...])

    blk = pl.BlockSpec((tr, c), lambda i: (i, 0))
    return pl.pallas_call(
        body, name=name, grid=(r // tr,),
        in_specs=[blk] * 4, out_specs=[blk] * 3,
        out_shape=[jax.ShapeDtypeStruct((r, c), F32)] * 3,
        compiler_params=_params("arbitrary"),
    )(w, g, m, v)


def _adamw_slots(w, slot_bufs, m, v, name, tr=256):
    nl, r, c = w.shape
    tr = _tile(r, tr)

    def body(*refs):
        w_ref = refs[0]
        g_refs = refs[1:1 + nl]
        m_ref, v_ref, go_ref, d_ref, mo_ref, vo_ref = refs[1 + nl:]
        for k in range(nl):
            @pl.when(pl.program_id(0) == k)
            def _(k=k):
                g = g_refs[k][0].astype(F32)
                for s in range(1, N_DEV):
                    g = g + g_refs[k][s].astype(F32)
                go_ref[0] = g

        d_ref[0], mo_ref[0], vo_ref[0] = _adam_update(w_ref[0], go_ref[0], m_ref[0], v_ref[0])

    blk = pl.BlockSpec((1, tr, c), lambda l, i: (l, i, 0))
    g_specs = [pl.BlockSpec((N_DEV, tr, c), lambda l, i, k=k: (0, jnp.where(l == k, i, 0), 0)) for k in range(nl)]
    return pl.pallas_call(
        body, name=name, grid=(nl, r // tr),
        in_specs=[blk] + g_specs + [blk, blk], out_specs=[blk] * 4,
        out_shape=[jax.ShapeDtypeStruct((nl, r, c), F32)] * 4,
        compiler_params=_params("arbitrary", "arbitrary"),
    )(w, *slot_bufs, m, v)


def _mesh_pos():
    return lax.axis_index("x"), lax.axis_index("y"), lax.axis_index("c")


def _flip(pos, p):
    x, y, c = pos
    return ((1 - x) if p & 4 else x, (1 - y) if p & 2 else y, (1 - c) if p & 1 else c)


def _lin(pos):
    return 4 * pos[0] + 2 * pos[1] + pos[2]


_HBM = pl.BlockSpec(memory_space=pltpu.HBM)
_SEM = pl.BlockSpec(memory_space=pltpu.SEMAPHORE)
_DATAFLOW = pltpu.SideEffectType.DATAFLOW_SIDE_EFFECTING


class _Item:
    def __init__(self, src, land_shape, src_pick, dst_pick):
        self.src, self.land_shape, self.src_pick, self.dst_pick = src, land_shape, src_pick, dst_pick


def _remote_copies(items, src, land, send_sem, recv_sem, me, arriving):
    me_i = _lin(me)
    out = []
    for it, s_ref, l_ref in zip(items, src, land):
        for p in range(1, N_DEV):
            peer = _flip(me, p)
            out.append(pltpu.make_async_remote_copy(
                src_ref=it.src_pick(s_ref, _lin(peer)),
                dst_ref=it.dst_pick(l_ref, _lin(peer) if arriving else me_i),
                send_sem=send_sem, recv_sem=recv_sem, device_id=peer, device_id_type=pl.DeviceIdType.MESH))
    return out


def _own_copies(items, src, land, sem, me):
    me_i = _lin(me)
    return [pltpu.make_async_copy(it.src_pick(s_ref, me_i), it.dst_pick(l_ref, me_i), sem)
            for it, s_ref, l_ref in zip(items, src, land)]


def _exchange_start(groups, name):
    items = [it for g in groups for it in g]
    n, ng = len(items), len(groups)
    first = [sum(len(g) for g in groups[:gi]) for gi in range(ng)]

    def body(*refs):
        src, land = refs[0:n], refs[n:2 * n]
        send_sems, recv_sems = refs[2 * n:2 * n + ng], refs[2 * n + ng:2 * n + 2 * ng]
        token = refs[4 * n + 2 * ng]
        me = _mesh_pos()
        for gi, g in enumerate(groups):
            sl = slice(first[gi], first[gi] + len(g))
            for cp in _remote_copies(g, src[sl], land[sl], send_sems[gi], recv_sems[gi], me, arriving=False):
                cp.start()
            for cp in _own_copies(g, src[sl], land[sl], recv_sems[gi], me):
                cp.start()
        token[...] = jnp.zeros_like(token)

    srcs = [pltpu.with_memory_space_constraint(it.src, pltpu.HBM) for it in items]
    lands = [pltpu.with_memory_space_constraint(lax.empty(it.land_shape, it.src.dtype), pltpu.HBM) for it in items]
    res = pl.pallas_call(
        body, name=name,
        out_shape=([pltpu.SemaphoreType.DMA(())] * (2 * ng)
                   + [pltpu.HBM(it.src.shape, it.src.dtype) for it in items]
                   + [pltpu.HBM(it.land_shape, it.src.dtype) for it in items]
                   + [jax.ShapeDtypeStruct((8, 128), F32)]),
        in_specs=[_HBM] * (2 * n),
        out_specs=[_SEM] * (2 * ng) + [_HBM] * (2 * n) + [pl.BlockSpec(memory_space=pltpu.VMEM)],
        input_output_aliases={i: 2 * ng + i for i in range(2 * n)},
        compiler_params=pltpu.CompilerParams(has_side_effects=_DATAFLOW),
    )(*srcs, *lands)
    send_sems, recv_sems = res[0:ng], res[ng:2 * ng]
    src_thru, land_thru = res[2 * ng:2 * ng + n], res[2 * ng + n:2 * ng + 2 * n]
    handles = []
    for gi, g in enumerate(groups):
        sl = slice(first[gi], first[gi] + len(g))
        handles.append((g, src_thru[sl], land_thru[sl], send_sems[gi], recv_sems[gi]))
    return handles, res[-1]


def _exchange_wait(handle, after, name):
    items, src_thru, land_thru, send_sem, recv_sem = handle
    k = len(items)

    def body(*refs):
        src, land = refs[0:k], refs[k:2 * k]
        send_ref, recv_ref = refs[2 * k], refs[2 * k + 1]
        for cp in _remote_copies(items, src, land, send_ref, recv_ref, _mesh_pos(), arriving=True):
            cp.wait_send()
            cp.wait_recv()
        for cp in _own_copies(items, src, land, recv_ref, _mesh_pos()):
            cp.wait()

    res = pl.pallas_call(
        body, name=name,
        out_shape=([pltpu.HBM(s.shape, s.dtype) for s in src_thru] + [pltpu.HBM(l.shape, l.dtype) for l in land_thru]),
        in_specs=[_HBM] * (2 * k) + [_SEM, _SEM, pl.BlockSpec(memory_space=pl.ANY)],
        out_specs=[_HBM] * (2 * k),
        input_output_aliases={i: i for i in range(2 * k)},
        compiler_params=pltpu.CompilerParams(has_side_effects=_DATAFLOW),
    )(*src_thru, *land_thru, send_sem, recv_sem, after)
    return res[k:2 * k]


def _whole(ref, i):
    return ref


def _slot(ref, i):
    return ref.at[i]


def _rows_of(r):
    return lambda ref, i: ref.at[pl.ds(pl.multiple_of(i * r, r), r), :]


def _cols_of(c):
    return lambda ref, i: ref.at[:, pl.ds(pl.multiple_of(i * c, c), c)]


def _all_reduce_small(buf, name):
    r, c = buf.shape

    def body(src_ref, out_ref, all_ref, send_sems, recv_sems):
        me = _mesh_pos()
        me_i = _lin(me)
        all_ref[me_i] = src_ref[...]
        for p in range(1, N_DEV):
            peer = _flip(me, p)
            pltpu.make_async_remote_copy(
                src_ref=src_ref, dst_ref=all_ref.at[me_i], send_sem=send_sems.at[p - 1], recv_sem=recv_sems.at[p - 1],
                device_id=peer, device_id_type=pl.DeviceIdType.MESH).start()
        for p in range(1, N_DEV):
            peer = _flip(me, p)
            cp = pltpu.make_async_remote_copy(
                src_ref=src_ref, dst_ref=all_ref.at[_lin(peer)], send_sem=send_sems.at[p - 1],
                recv_sem=recv_sems.at[p - 1], device_id=peer, device_id_type=pl.DeviceIdType.MESH)
            cp.wait_recv()
            cp.wait_send()
        acc = all_ref[0]
        for s in range(1, N_DEV):
            acc = acc + all_ref[s]
        out_ref[...] = acc

    vm = pl.BlockSpec(memory_space=pltpu.VMEM)
    return pl.pallas_call(
        body, name=name, in_specs=[vm], out_specs=vm,
        out_shape=jax.ShapeDtypeStruct((r, c), F32),
        scratch_shapes=[pltpu.VMEM((N_DEV, r, c), F32), pltpu.SemaphoreType.DMA((N_DEV - 1,)),
                        pltpu.SemaphoreType.DMA((N_DEV - 1,))],
        compiler_params=pltpu.CompilerParams(has_side_effects=True),
    )(buf)


def _unshard_cols(g):
    s, l, r, c = g.shape
    return jnp.transpose(g, (1, 2, 0, 3)).reshape(l, r, s * c)


def kernel(x, gdn_w_in, gdn_conv, gdn_a_log, gdn_dt_bias, gdn_onorm, gdn_w_out, hgrn_w_in, hgrn_lb_logits, hgrn_gnorm, hgrn_w_out, norm_mix, norm_mlp, mlp_w_up, mlp_w_down, norm_final, loss_target, m_gdn_w_in, m_gdn_conv, m_gdn_a_log, m_gdn_dt_bias, m_gdn_onorm, m_gdn_w_out, m_hgrn_w_in, m_hgrn_lb_logits, m_hgrn_gnorm, m_hgrn_w_out, m_norm_mix, m_norm_mlp, m_mlp_w_up, m_mlp_w_down, m_norm_final, v_gdn_w_in, v_gdn_conv, v_gdn_a_log, v_gdn_dt_bias, v_gdn_onorm, v_gdn_w_out, v_hgrn_w_in, v_hgrn_lb_logits, v_hgrn_gnorm, v_hgrn_w_out, v_norm_mix, v_norm_mlp, v_mlp_w_up, v_mlp_w_down, v_norm_final):
    seqs, seq_len, d = x.shape
    n = seqs * seq_len
    me_i = _lin(_mesh_pos())
    x2 = x.reshape(n, d)
    target = loss_target.reshape(n, d)
    n_gdn, n_hgrn = gdn_w_in.shape[0], hgrn_w_in.shape[0]

    r_out, r_down = gdn_w_out.shape[1], mlp_w_down.shape[1]
    c_gin, c_hin, c_up = gdn_w_in.shape[2], hgrn_w_in.shape[2], mlp_w_up.shape[2]

    def gathered(w, pick, land_shape):
        return _Item(w.astype(BF16), land_shape, _whole, pick)

    groups = [[_Item(gdn_conv, (N_DEV,) + gdn_conv.shape, _whole, _slot),
               _Item(hgrn_gnorm, (N_DEV,) + hgrn_gnorm.shape, _whole, _slot)]]
    for i in range(DEPTH):
        j = i // 2
        if i % 2 == 0:
            groups += [[gathered(gdn_w_in[j], _slot, (N_DEV, d, c_gin))],
                       [gathered(gdn_w_out[j], _rows_of(r_out), (N_DEV * r_out, d))]]
        else:
            groups += [[gathered(hgrn_w_in[j], _cols_of(c_hin), (d, N_DEV * c_hin))],
                       [gathered(hgrn_w_out[j], _rows_of(r_out), (N_DEV * r_out, d))]]
        groups += [[gathered(mlp_w_up[i], _cols_of(c_up), (d, N_DEV * c_up))],
                   [gathered(mlp_w_down[i], _rows_of(r_down), (N_DEV * r_down, d))]]
    gather_handles, token = _exchange_start(groups, "gather_start")
    lbs = _lb_fwd(hgrn_lb_logits + token[0:1, 0:1], "lb_fwd")

    def arrived(k, after, name):
        return _exchange_wait(gather_handles[k], after, "gather_wait_" + name)

    saved = []
    w_in, w_ab, w_out, w_up, w_down = ([None] * DEPTH for _ in range(5))
    h = x2
    for i in range(DEPTH):
        j = i // 2
        if i == 0:
            g_conv, g_gnorm = arrived(0, h, "small")
            conv_full = _unshard_cols(g_conv)
            gnorm_full = jnp.transpose(g_gnorm, (1, 0, 2)).reshape(n_hgrn, d)
        y = _rms_fwd(h, norm_mix[i:i + 1], f"rms_mix_{i}")
        (w_in[i],) = arrived(1 + 4 * i, y, f"in_{i}")
        if i % 2 == 0:
            w_gin = jnp.transpose(w_in[i], (1, 0, 2)).reshape(d, N_DEV * c_gin)
            w_in[i] = w_gin[:, :GDN_MAIN]
            w_ab[i] = jnp.pad(w_gin[:, GDN_MAIN:], ((0, 0), (0, AB_PAD - 2 * N_HEADS)))
            projm = _mm(y, w_in[i], "nn", [BF16], f"gdn_proj_{i}")
            projab = _mm(y, w_ab[i], "nn", [F32], f"gdn_proj_ab_{i}")
            o2, st_all = _gdn_fwd(projm, projab, conv_full[j], gdn_a_log[j:j + 1], gdn_dt_bias[j:j + 1],
                                  gdn_onorm[j:j + 1], seqs, f"gdn_fwd_{i}")
            mix = (projm, projab, st_all)
        else:
            proj = _mm(y, w_in[i], "nn", [BF16], f"hgrn_proj_{i}")
            o2, o_raw, st_all = _hgrn_fwd(proj, lbs[i:i + 1], gnorm_full[j:j + 1], seqs, f"hgrn_fwd_{i}")
            mix = (proj, o_raw, st_all)
        (w_out[i],) = arrived(2 + 4 * i, o2, f"out_{i}")
        h1 = _mm(o2, w_out[i], "nn", [F32], f"mix_out_{i}", epilogue=lambda acc, res: (res + acc,), extras=(h,))
        y2 = _rms_fwd(h1, norm_mlp[i:i + 1], f"rms_mlp_{i}")
        (w_up[i],) = arrived(3 + 4 * i, y2, f"up_{i}")
        u, act = _mm(y2, w_up[i], "nn", [BF16, BF16], f"mlp_up_{i}",
                     epilogue=lambda acc: (acc, jnp.square(jnp.maximum(acc, 0.0))))
        (w_down[i],) = arrived(4 + 4 * i, act, f"down_{i}")
        h2 = _mm(act, w_down[i], "nn", [F32], f"mlp_down_{i}", epilogue=lambda acc, res: (res + acc,), extras=(h1,))
        saved.append((h, y, mix, o2, h1, y2, u, act))
        h = h2

    dh, dh_b, d_nf, sq = _loss_head(h, norm_final.reshape(1, d), target, "loss_head")

    d_nmix, d_nmlp = [None] * DEPTH, [None] * DEPTH
    d_conv, d_alog, d_dtb, d_onorm = [None] * n_gdn, [None] * n_gdn, [None] * n_gdn, [None] * n_gdn
    d_lb = [jnp.zeros((1, d), F32)] * DEPTH
    d_gnorm = [None] * n_hgrn
    mlp_handles, mix_handles = [None] * DEPTH, [None] * DEPTH
    token = None
    for i in reversed(range(DEPTH)):
        j = i // 2
        h_in, y, mix, o2, h1, y2, u, act = saved[i]
        g_down = _mm(act, dh_b, "tn", [BF16], f"g_down_{i}", after=token)
        du = _mm(dh_b, w_down[i], "nt", [BF16], f"d_u_{i}",
                 epilogue=lambda acc, uu: (acc * (2.0 * jnp.maximum(uu.astype(F32), 0.0)),), extras=(u,))
        g_up = _mm(y2, du, "tn", [BF16], f"g_up_{i}")
        mlp_handles[i], token = _exchange_start(
            [[_Item(g_down, (N_DEV, r_down, d), _rows_of(r_down), _slot)],
             [_Item(g_up, (N_DEV, d, c_up), _cols_of(c_up), _slot)]], f"scatter_start_mlp_{i}")
        dy2 = _mm(du, w_up[i], "nt", [BF16], f"d_y2_{i}", after=token)
        dh1, dh1_b, d_nmlp[i] = _rms_bwd(h1, norm_mlp[i:i + 1], dy2, dh, f"rms_mlp_bwd_{i}")
        g_out = _mm(o2, dh1_b, "tn", [BF16], f"g_out_{i}")
        do2 = _mm(dh1_b, w_out[i], "nt", [BF16], f"d_o2_{i}")
        if i % 2 == 0:
            projm, projab, st_all = mix
            dpm, dpab, d_conv[j], d_alog[j], d_dtb[j], d_onorm[j] = _gdn_bwd(
                projm, projab, conv_full[j], gdn_a_log[j:j + 1], gdn_dt_bias[j:j + 1], gdn_onorm[j:j + 1],
                st_all, do2, seqs, f"gdn_bwd_{i}")
            g_main = _mm(y, dpm, "tn", [BF16], f"g_in_{i}")
            g_ab = _mm(y, dpab, "tn", [BF16], f"g_in_ab_{i}")
            g_in = jnp.concatenate([g_main, g_ab[:, :2 * N_HEADS]], axis=1)
            g_in = jnp.transpose(g_in.reshape(d, N_DEV, c_gin), (1, 0, 2))
            in_item = _Item(g_in, (N_DEV, d, c_gin), _slot, _slot)
            dy_ab = _mm(dpab, w_ab[i], "nt", [F32], f"d_y_ab_{i}")
            dy = _mm(dpm, w_in[i], "nt", [BF16], f"d_y_{i}", epilogue=lambda acc, e: (acc + e,), extras=(dy_ab,))
        else:
            proj, o_raw, st_all = mix
            dp, d_lb[i], d_gnorm[j] = _hgrn_bwd(proj, lbs[i:i + 1], gnorm_full[j:j + 1], st_all, o_raw, do2,
                                               seqs, f"hgrn_bwd_{i}")
            g_in = _mm(y, dp, "tn", [BF16], f"g_in_{i}")
            in_item = _Item(g_in, (N_DEV, d, c_hin), _cols_of(c_hin), _slot)
            dy = _mm(dp, w_in[i], "nt", [BF16], f"d_y_{i}")
        mix_handles[i], token = _exchange_start(
            [[_Item(g_out, (N_DEV, r_out, d), _rows_of(r_out), _slot)], [in_item]], f"scatter_start_mix_{i}")
        dh, dh_b, d_nmix[i] = _rms_bwd(h_in, norm_mix[i:i + 1], dy, dh1, f"rms_mix_bwd_{i}")
    grad_x = dh.reshape(x.shape)

    def landed(handles, k, layers, after, name):
        return [_exchange_wait(handles[i][k], after, f"scatter_wait_{name}_{i}")[0] for i in layers]

    every, even, odd = range(DEPTH), range(0, DEPTH, 2), range(1, DEPTH, 2)
    upd = {}
    upd["mlp_w_down"] = _adamw_slots(mlp_w_down, landed(mlp_handles, 0, every, dh, "down"), m_mlp_w_down,
                                     v_mlp_w_down, "adamw_mlp_w_down")
    upd["mlp_w_up"] = _adamw_slots(mlp_w_up, landed(mlp_handles, 1, every, upd["mlp_w_down"][1], "up"), m_mlp_w_up,
                                   v_mlp_w_up, "adamw_mlp_w_up")
    upd["hgrn_w_out"] = _adamw_slots(hgrn_w_out, landed(mix_handles, 0, odd, upd["mlp_w_up"][1], "out"),
                                     m_hgrn_w_out, v_hgrn_w_out, "adamw_hgrn_w_out")
    upd["hgrn_w_in"] = _adamw_slots(hgrn_w_in, landed(mix_handles, 1, odd, upd["hgrn_w_out"][1], "in"), m_hgrn_w_in,
                                    v_hgrn_w_in, "adamw_hgrn_w_in")
    upd["gdn_w_out"] = _adamw_slots(gdn_w_out, landed(mix_handles, 0, even, upd["hgrn_w_in"][1], "out"),
                                    m_gdn_w_out, v_gdn_w_out, "adamw_gdn_w_out")
    upd["gdn_w_in"] = _adamw_slots(gdn_w_in, landed(mix_handles, 1, even, upd["gdn_w_out"][1], "in"), m_gdn_w_in,
                                   v_gdn_w_in, "adamw_gdn_w_in")

    def update(name, w, g, m, v):
        shape = w.shape
        c = shape[-1]
        res = _adamw(w.reshape(-1, c), g.reshape(-1, c), m.reshape(-1, c), v.reshape(-1, c), "adamw_" + name)
        return [g.reshape(shape)] + [o.reshape(shape) for o in res]

    dlb_rows = jnp.concatenate(d_lb, axis=0)
    tail = jnp.concatenate(
        [jnp.concatenate(d_onorm, axis=1), jnp.concatenate(d_alog, axis=1), jnp.concatenate(d_dtb, axis=1)], axis=1)
    tail = jnp.pad(tail, ((0, 0), (0, d - tail.shape[1])))
    conv_rows = jnp.stack(d_conv).reshape(-1, d)
    packed = jnp.concatenate(
        [jnp.concatenate(d_nmix, axis=0), jnp.concatenate(d_nmlp, axis=0), d_nf, sq, dlb_rows,
         jnp.concatenate(d_gnorm, axis=0), tail, conv_rows], axis=0)
    pad_rows = (-packed.shape[0]) % 8
    packed = jnp.pad(packed, ((0, pad_rows), (0, 0)))
    tot = _all_reduce_small(packed, "reduce_small")
    r0 = 0
    g_nmix = tot[r0:r0 + DEPTH]; r0 += DEPTH
    g_nmlp = tot[r0:r0 + DEPTH]; r0 += DEPTH
    g_nf = tot[r0]; r0 += 1
    loss = tot[r0, 0]; r0 += 1
    g_lb = _lb_bwd(hgrn_lb_logits, tot[r0:r0 + DEPTH], "lb_bwd"); r0 += DEPTH
    g_gnorm_full = tot[r0:r0 + n_hgrn]; r0 += n_hgrn
    t_row = tot[r0]; r0 += 1
    g_conv_full = tot[r0:r0 + n_gdn * CONV_K * 3].reshape(n_gdn, CONV_K, 3 * d)
    g_onorm = t_row[0:n_gdn * HEAD_DIM].reshape(n_gdn, HEAD_DIM)
    o1 = n_gdn * HEAD_DIM
    g_alog = t_row[o1:o1 + n_gdn * N_HEADS].reshape(n_gdn, N_HEADS)
    g_dtb = t_row[o1 + n_gdn * N_HEADS:o1 + 2 * n_gdn * N_HEADS].reshape(n_gdn, N_HEADS)
    c_gn, c_cv = hgrn_gnorm.shape[1], gdn_conv.shape[2]
    g_gnorm = lax.dynamic_slice_in_dim(g_gnorm_full, me_i * c_gn, c_gn, axis=1)
    g_conv = lax.dynamic_slice_in_dim(g_conv_full, me_i * c_cv, c_cv, axis=2)

    upd["gdn_conv"] = update("gdn_conv", gdn_conv, g_conv, m_gdn_conv, v_gdn_conv)
    upd["gdn_a_log"] = update("gdn_a_log", gdn_a_log, g_alog, m_gdn_a_log, v_gdn_a_log)
    upd["gdn_dt_bias"] = update("gdn_dt_bias", gdn_dt_bias, g_dtb, m_gdn_dt_bias, v_gdn_dt_bias)
    upd["gdn_onorm"] = update("gdn_onorm", gdn_onorm, g_onorm, m_gdn_onorm, v_gdn_onorm)
    upd["hgrn_lb_logits"] = update("hgrn_lb_logits", hgrn_lb_logits, g_lb, m_hgrn_lb_logits, v_hgrn_lb_logits)
    upd["hgrn_gnorm"] = update("hgrn_gnorm", hgrn_gnorm, g_gnorm, m_hgrn_gnorm, v_hgrn_gnorm)
    upd["norm_mix"] = update("norm_mix", norm_mix, g_nmix, m_norm_mix, v_norm_mix)
    upd["norm_mlp"] = update("norm_mlp", norm_mlp, g_nmlp, m_norm_mlp, v_norm_mlp)
    upd["norm_final"] = update("norm_final", norm_final, g_nf, m_norm_final, v_norm_final)

    order = ["gdn_w_in", "gdn_conv", "gdn_a_log", "gdn_dt_bias", "gdn_onorm", "gdn_w_out", "hgrn_w_in",
             "hgrn_lb_logits", "hgrn_gnorm", "hgrn_w_out", "norm_mix", "norm_mlp", "mlp_w_up", "mlp_w_down",
             "norm_final"]
    outs = [loss, grad_x]
    for k in range(4):
        outs += [upd[name][k] for name in order]
    return tuple(outs)
```

```python
import functools

import numpy as np
import jax
import jax.numpy as jnp
from jax import lax
from jax.experimental import pallas as pl
from jax.experimental.pallas import tpu as pltpu

F32 = jnp.float32
BF16 = jnp.bfloat16

D_MODEL = 1024
N_HEADS = 8
HEAD_DIM = 128
CHUNK = 64
SUB = 16
N_SUB = CHUNK // SUB
CONV_K = 4
HALO = 16
EPS = 1e-6
DEPTH = 4
N_DEV = 8
GDN_MAIN = 4 * D_MODEL
GDN_IN = GDN_MAIN + 2 * N_HEADS
AB_PAD = 128
HEAD_GROUP = 8
LANE_BLOCK = 256
ROW_BLOCK = 16

ADAM_LR = 0.001
ADAM_B1 = 0.9
ADAM_B2 = 0.999
ADAM_EPS = 1e-08
ADAM_WD = 0.01
ADAM_STEP = 10

VMEM_LIMIT = 56 * 1024 * 1024
MM_TILE = 1024
MM_VMEM_BUDGET = 40 * 1024 * 1024

_DIMS = {
    "nn": (((1,), (0,)), ((), ())),
    "nt": (((1,), (1,)), ((), ())),
    "tn": (((0,), (0,)), ((), ())),
}


def _parts(x, n):
    if n == 1 and x.dtype == BF16:
        return [x]
    out = []
    r = x.astype(F32)
    for i in range(n):
        p = r.astype(BF16)
        out.append(p)
        if i + 1 < n:
            r = r - p.astype(F32)
    return out


def _dot_raw(a, b, mode, na, nb):
    ap, bp = _parts(a, na), _parts(b, nb)
    nmax = max(na, nb)
    dot = lambda x, y: lax.dot_general(x, y, _DIMS[mode], preferred_element_type=F32)
    acc = None
    if na > 1 and mode != "tn":
        m = a.shape[0]
        for j, xb in enumerate(bp):
            mine = ap[:nmax - j]
            t = dot(mine[0] if len(mine) == 1 else jnp.concatenate(mine, axis=0), xb)
            for i in range(len(mine)):
                acc = t[i * m:(i + 1) * m] if acc is None else acc + t[i * m:(i + 1) * m]
        return acc
    axis = 0 if mode == "nt" else 1
    n = b.shape[axis]
    for i, xa in enumerate(ap):
        mine = bp[:nmax - i]
        t = dot(xa, mine[0] if len(mine) == 1 else jnp.concatenate(mine, axis=axis))
        for j in range(len(mine)):
            acc = t[:, j * n:(j + 1) * n] if acc is None else acc + t[:, j * n:(j + 1) * n]
    return acc


@functools.partial(jax.custom_vjp, nondiff_argnums=(2, 3, 4))
def _dot(a, b, mode, na, nb):
    return _dot_raw(a, b, mode, na, nb)


def _dot_fwd(a, b, mode, na, nb):
    return _dot_raw(a, b, mode, na, nb), (a, b)


def _dot_bwd(mode, na, nb, res, ct):
    a, b = res
    if mode == "nn":
        da = _dot_raw(ct, b, "nt", 1, 1)
        db = _dot_raw(a, ct, "tn", 1, 1)
    elif mode == "nt":
        da = _dot_raw(ct, b, "nn", 1, 1)
        db = _dot_raw(ct, a, "tn", 1, 1)
    else:
        da = _dot_raw(b, ct, "nt", 1, 1)
        db = _dot_raw(a, ct, "nn", 1, 1)
    return da.astype(a.dtype), db.astype(b.dtype)


_dot.defvjp(_dot_fwd, _dot_bwd)


N_EXACT = 3


@jax.custom_vjp
def _dot01(x, m_wide, m):
    return lax.dot_general(m_wide, jnp.concatenate(_parts(x, N_EXACT), axis=0), _DIMS["nn"], preferred_element_type=F32)


def _dot01_fwd(x, m_wide, m):
    return _dot01(x, m_wide, m), (m_wide, m)


def _dot01_bwd(res, ct):
    m_wide, m = res
    dx = lax.dot_general(m, ct.astype(BF16), _DIMS["tn"], preferred_element_type=F32)
    return dx, jnp.zeros_like(m_wide), jnp.zeros_like(m)


_dot01.defvjp(_dot01_fwd, _dot01_bwd)


def _thrice(m):
    return jnp.concatenate([m] * N_EXACT, axis=1).astype(BF16), m.astype(BF16)


def _iota2(shape, dim):
    return lax.broadcasted_iota(jnp.int32, shape, dim)


def _tril_f32(n):
    return (_iota2((n, n), 0) >= _iota2((n, n), 1)).astype(F32)


def _cumsum_rows(g):
    return _dot(_tril_f32(g.shape[0]), g, "nn", 1, 3)


def _inv_unit_lower(L):
    n = L.shape[0]
    eye = (_iota2((n, n), 0) == _iota2((n, n), 1)).astype(F32)
    neg = -L
    s = eye + neg
    p = _dot_raw(neg, neg, "nn", 2, 2)
    m = 2
    while 2 * m < n:
        both = _dot_raw(jnp.concatenate([p, s], axis=0), p, "nn", 2, 2)
        p, s = both[0:n], s + both[n:2 * n]
        m *= 2
    return s + _dot_raw(s, p, "nn", 2, 2)


@jax.custom_vjp
def _solve_unit_lower(L, rhs):
    return _dot_raw(_inv_unit_lower(L), rhs, "nn", 2, 2)


def _solve_fwd(L, rhs):
    t = _inv_unit_lower(L)
    sol = _dot_raw(t, rhs, "nn", 2, 2)
    return sol, (t, sol)


def _solve_bwd(res, ct):
    t, sol = res
    y = _dot_raw(t, ct, "tn", 2, 2)
    return -_dot_raw(y, sol, "nt", 2, 2), y


_solve_unit_lower.defvjp(_solve_fwd, _solve_bwd)


def _softplus(x):
    return jnp.maximum(x, 0.0) + jnp.log1p(jnp.exp(-jnp.abs(x)))


def _rms(x, w):
    return x * lax.rsqrt(jnp.mean(x * x, axis=-1, keepdims=True) + EPS) * w


HG_LEVELS = (32, 16, 8, 4, 2, 1)


def _hg_level_sums():
    i = np.arange(CHUNK)[:, None]
    m = np.arange(CHUNK)[None, :]
    to_row = [(m <= i) & (m // b == i // b) for b in HG_LEVELS]
    to_col = [(m > i) & (m // b == i // b) for b in HG_LEVELS]
    return _thrice(jnp.asarray(np.concatenate(to_row + to_col + [m <= i]), F32))


def _hg_level_masks():
    i = np.arange(CHUNK)[:, None]
    j = np.arange(CHUNK)[None, :]
    return jnp.asarray(np.stack([(i // b == j // b + 1) & ((i // b) % 2 == 1) for b in HG_LEVELS]), F32)


def _hg_pre(qraw, f, lb, sums):
    g = jnp.log(lb + (1.0 - lb) * jax.nn.sigmoid(f))
    k = (1.0 - lb) * jax.nn.sigmoid(-f)
    q = jax.nn.silu(qraw) * (HEAD_DIM ** -0.5)
    return q, k, _dot01(g, *sums)


def _hg_head(st, q, k, v, e, masks):
    nl = len(HG_LEVELS)
    eye = (_iota2((CHUNK, CHUNK), 0) == _iota2((CHUNK, CHUNK), 1)).astype(F32)
    a = eye * jnp.sum(q * k, axis=-1, keepdims=True)
    for l in range(nl):
        rows = q * jnp.exp(e[l * CHUNK:(l + 1) * CHUNK])
        cols = k * jnp.exp(e[(nl + l) * CHUNK:(nl + l + 1) * CHUNK])
        a = a + masks[l] * _dot(rows, cols, "nt", 1, 1)
    gc = e[2 * nl * CHUNK:(2 * nl + 1) * CHUNK]
    o = _dot(a, v, "nn", 1, 1) + _dot(q * jnp.exp(gc), st, "nt", 1, 1)
    g_last = gc[CHUNK - 1:CHUNK]
    st_new = st * jnp.exp(g_last) + _dot(v, k * jnp.exp(g_last - gc), "tn", 1, 1)
    return o, st_new


_HG_HEADS = jax.vmap(_hg_head, in_axes=(0, 0, 0, 0, 0, None))


def _hg_post(o, gate, gw):
    return _rms(o, gw) * jax.nn.silu(gate)


def _gd_conv(xp, cw):
    off = HALO - (CONV_K - 1)
    y = cw[0:1] * xp[off:off + CHUNK]
    for kk in range(1, CONV_K):
        y = y + cw[kk:kk + 1] * xp[off + kk:off + kk + CHUNK]
    return jax.nn.silu(y)


def _gd_gates(a, b, alog, dtb):
    beta = jax.nn.sigmoid(b)
    g = -jnp.exp(alog) * _softplus(a + dtb)
    expand = (_iota2((N_HEADS, D_MODEL), 1) // HEAD_DIM == _iota2((N_HEADS, D_MODEL), 0)).astype(F32)
    g_x = _dot(g, expand, "nn", 3, 1)
    after = (_iota2((CHUNK, D_MODEL), 0) > _iota2((CHUNK, D_MODEL), 1) % HEAD_DIM).astype(F32)
    sums = _dot01(jnp.concatenate([g_x, g_x * after], axis=1), *_thrice(_tril_f32(CHUNK)))
    return _dot(beta, expand, "nn", 3, 1), sums


def _gd_head(st, q, k, v, beta, gc, diff, gate, onw):
    q = q * lax.rsqrt(jnp.sum(q * q, axis=-1, keepdims=True) + EPS) * (HEAD_DIM ** -0.5)
    k = k * lax.rsqrt(jnp.sum(k * k, axis=-1, keepdims=True) + EPS)
    ri = _iota2((CHUNK, CHUNK), 0)
    ci = _iota2((CHUNK, CHUNK), 1)
    decay = jnp.exp(jnp.where(ri >= ci, diff[:, 0:CHUNK], -jnp.inf))
    kb = k * beta
    egc = jnp.exp(gc)
    L = jnp.where(ri > ci, _dot(kb, k, "nt", 1, 1) * decay, 0.0)
    sol = _solve_unit_lower(L, jnp.concatenate([v * beta, kb * egc], axis=1))
    u = sol[:, 0:HEAD_DIM]
    w = sol[:, HEAD_DIM:2 * HEAD_DIM]
    a_qk = jnp.where(ri >= ci, _dot(q, k, "nt", 1, 1) * decay, 0.0)
    g_last = gc[CHUNK - 1:CHUNK]
    v_new = u - _dot(w, st, "nt", 1, 1)
    o = _dot(q * egc, st, "nt", 1, 1) + _dot(a_qk, v_new, "nn", 1, 1)
    st_new = st * jnp.exp(g_last) + _dot(v_new, k * jnp.exp(g_last - gc), "tn", 1, 1)
    return _rms(o, onw) * jax.nn.silu(gate), st_new


def _params(*sem):
    return pltpu.CompilerParams(dimension_semantics=sem, vmem_limit_bytes=VMEM_LIMIT)


def _tile(n, pref):
    t = min(n, pref)
    assert n % t == 0, (n, pref)
    return t


def _mm_tiles(m, n, k, a_size, b_size, tile_sizes):
    tm, tn, tk = _tile(m, MM_TILE), _tile(n, MM_TILE), k

    def need(tm, tn, tk):
        acc = 4 * tm * tn * (2 if tk < k else 1)
        return 2 * (tm * tk * a_size + tk * tn * b_size + tm * tn * sum(tile_sizes)) + acc

    while need(tm, tn, tk) > MM_VMEM_BUDGET:
        if tk > 2048 or (tk > 512 and tm <= 512):
            tk //= 2
        else:
            tm //= 2
    return tm, tn, tk


def _mm(a, b, mode, out_dtypes, name, epilogue=None, extras=(), vectors=(), n_sums=0, after=None):
    if mode == "nn":
        (m, k), (k2, n) = a.shape, b.shape
    elif mode == "nt":
        (m, k), (n, k2) = a.shape, b.shape
    else:
        (k, m), (k2, n) = a.shape, b.shape
    assert k == k2, (a.shape, b.shape, mode)
    tm, tn, tk = _mm_tiles(m, n, k, a.dtype.itemsize, b.dtype.itemsize,
                           [e.dtype.itemsize for e in extras] + [jnp.dtype(dt).itemsize for dt in out_dtypes])
    nk = k // tk
    assert not (vectors or n_sums) or tn == n, "whole-row epilogues need the result tile to span the rows"
    ne, no, nafter = len(extras) + len(vectors), len(out_dtypes), int(after is not None)
    if epilogue is None:
        epilogue = lambda acc: (acc,)

    def body(*refs):
        a_ref, b_ref = refs[0], refs[1]
        ex = refs[2:2 + ne]
        outs = refs[2 + ne + nafter:2 + ne + nafter + no]
        sums = refs[2 + ne + nafter + no:2 + ne + nafter + no + n_sums]
        part = lax.dot_general(a_ref[...].astype(BF16), b_ref[...].astype(BF16), _DIMS[mode],
                               preferred_element_type=F32)

        def finish(acc):
            vals = epilogue(acc, *[e[...] for e in ex])
            for o_ref, val in zip(outs, vals[:no]):
                o_ref[...] = val.astype(o_ref.dtype)
            for s_ref, val in zip(sums, vals[no:]):
                @pl.when(pl.program_id(0) == 0)
                def _(s_ref=s_ref, val=val):
                    s_ref[...] = val

                @pl.when(pl.program_id(0) > 0)
                def _(s_ref=s_ref, val=val):
                    s_ref[...] += val

        if nk == 1:
            finish(part)
        else:
            acc_ref = refs[-1]
            kk = pl.program_id(2)

            @pl.when(kk == 0)
            def _():
                acc_ref[...] = part

            @pl.when(kk > 0)
            def _():
                acc_ref[...] += part

            @pl.when(kk == nk - 1)
            def _():
                finish(acc_ref[...])

    if mode == "tn":
        a_spec = pl.BlockSpec((tk, tm), lambda i, j, kk: (kk, i))
    else:
        a_spec = pl.BlockSpec((tm, tk), lambda i, j, kk: (i, kk))
    if mode == "nt":
        b_spec = pl.BlockSpec((tn, tk), lambda i, j, kk: (j, kk))
    else:
        b_spec = pl.BlockSpec((tk, tn), lambda i, j, kk: (kk, j))
    o_spec = pl.BlockSpec((tm, tn), lambda i, j, kk: (i, j))
    v_spec = pl.BlockSpec((1, tn), lambda i, j, kk: (0, j))
    res = pl.pallas_call(
        body,
        name=name,
        grid=(m // tm, n // tn, nk),
        in_specs=([a_spec, b_spec] + [o_spec] * len(extras) + [v_spec] * len(vectors)
                  + [pl.BlockSpec(memory_space=pl.ANY)] * nafter),
        out_specs=[o_spec] * no + [v_spec] * n_sums,
        out_shape=[jax.ShapeDtypeStruct((m, n), dt) for dt in out_dtypes] + [jax.ShapeDtypeStruct((1, n), F32)] * n_sums,
        scratch_shapes=[pltpu.VMEM((tm, tn), F32)] if nk > 1 else [],
        compiler_params=_params(*(("arbitrary",) * 3 if n_sums else ("parallel", "parallel", "arbitrary"))),
    )(a, b, *extras, *vectors, *([after] if nafter else []))
    return res[0] if no + n_sums == 1 else res


def _ep_residual_norm(acc, res, w):
    h = res + acc
    return h, _rms(h, w)


def _ep_norm_bwd(acc, x, dres, w):
    _, vjp = jax.vjp(_rms, x, w)
    dx, dw = vjp(acc)
    dx = dres + dx
    return dx, dx, dw


def _rms_fwd(x, w, name, tm=512):
    n, d = x.shape
    tm = _tile(n, tm)

    def body(x_ref, w_ref, y_ref):
        y_ref[...] = _rms(x_ref[...], w_ref[...]).astype(y_ref.dtype)

    return pl.pallas_call(
        body, name=name, grid=(n // tm,),
        in_specs=[pl.BlockSpec((tm, d), lambda i: (i, 0)), pl.BlockSpec((1, d), lambda i: (0, 0))],
        out_specs=pl.BlockSpec((tm, d), lambda i: (i, 0)),
        out_shape=jax.ShapeDtypeStruct((n, d), BF16),
        compiler_params=_params("arbitrary"),
    )(x, w)


def _loss_head(h, w, target, name, tm=512):
    n, d = h.shape
    tm = _tile(n, tm)

    def body(h_ref, w_ref, t_ref, dh_ref, dhb_ref, dw_ref, sq_ref):
        y, vjp = jax.vjp(_rms, h_ref[...], w_ref[...])
        err = y - t_ref[...]
        dh, dw = vjp(err * (1.0 / d))
        dh_ref[...] = dh
        dhb_ref[...] = dh.astype(dhb_ref.dtype)
        sq = jnp.sum(err * err, axis=0, keepdims=True)

        @pl.when(pl.program_id(0) == 0)
        def _():
            dw_ref[...] = dw
            sq_ref[...] = sq

        @pl.when(pl.program_id(0) > 0)
        def _():
            dw_ref[...] += dw
            sq_ref[...] += sq

        @pl.when(pl.program_id(0) == n // tm - 1)
        def _():
            total = jnp.sum(sq_ref[...], axis=1, keepdims=True) * (0.5 / d)
            sq_ref[...] = jnp.broadcast_to(total, sq_ref.shape)

    row = pl.BlockSpec((tm, d), lambda i: (i, 0))
    vec = pl.BlockSpec((1, d), lambda i: (0, 0))
    return pl.pallas_call(
        body, name=name, grid=(n // tm,),
        in_specs=[row, vec, row],
        out_specs=[row, row, vec, vec],
        out_shape=[jax.ShapeDtypeStruct((n, d), F32), jax.ShapeDtypeStruct((n, d), BF16),
                   jax.ShapeDtypeStruct((1, d), F32), jax.ShapeDtypeStruct((1, d), F32)],
        compiler_params=_params("arbitrary"),
    )(h, w, target)


def _lower_bounds(logits):
    sm = jax.nn.softmax(logits, axis=0)
    rows = [sm[0:1] * 0.0]
    for r in range(1, DEPTH):
        rows.append(rows[-1] + sm[r:r + 1])
    return jnp.concatenate(rows, axis=0)


def _lb_fwd(logits, name):
    def body(l_ref, o_ref):
        o_ref[...] = _lower_bounds(l_ref[...])

    return pl.pallas_call(body, name=name, out_shape=jax.ShapeDtypeStruct(logits.shape, F32))(logits)


def _lb_bwd(logits, dlb, name):
    def body(l_ref, d_ref, o_ref):
        _, vjp = jax.vjp(_lower_bounds, l_ref[...])
        (o_ref[...],) = vjp(d_ref[...])

    return pl.pallas_call(body, name=name, out_shape=jax.ShapeDtypeStruct(logits.shape, F32))(logits, dlb)


def _head_slice(h):
    if isinstance(h, int):
        return pl.ds(h * HEAD_DIM, HEAD_DIM)
    return pl.ds(pl.multiple_of(h * HEAD_DIM, HEAD_DIM), HEAD_DIM)


def _head_groups(group_body):
    if HEAD_GROUP == N_HEADS:
        group_body(list(range(N_HEADS)))
        return

    def trip(i, carry):
        group_body([i * HEAD_GROUP + t for t in range(HEAD_GROUP)])
        return carry

    lax.fori_loop(0, N_HEADS // HEAD_GROUP, trip, 0)


def _stack_heads(ref, hs, first=0):
    return jnp.stack([ref[:, _head_slice(h + first)] for h in hs])


def _unstack_heads(ref, hs, val, first=0):
    for t, h in enumerate(hs):
        ref[:, _head_slice(h + first)] = val[t].astype(ref.dtype)


_GD_HEADS = jax.vmap(_gd_head, in_axes=(0, 0, 0, 0, 0, 0, 0, 0, None))


def _hgrn_fwd(proj, lb, gw, seqs, name):
    n = proj.shape[0]
    nc = n // seqs // CHUNK
    d = D_MODEL

    sums, masks = _hg_level_sums(), _hg_level_masks()

    def body(p_ref, lb_ref, gw_ref, sums_wide_ref, sums_once_ref, masks_ref, o2_ref, o_ref, st_all_ref,
             st_sc, q_sc, k_sc, v_sc, e_sc):
        @pl.when(pl.program_id(1) == 0)
        def _():
            st_sc[...] = jnp.zeros_like(st_sc)

        sums_refs = (sums_wide_ref, sums_once_ref)
        _lane_blocks(d, functools.partial(_hg_pre_block, p_ref, lb_ref, sums_refs, q_sc, k_sc, v_sc, e_sc))
        st_all_ref[0] = st_sc[...]

        def group(hs):
            sts = pl.ds(hs[0], len(hs))
            o, st_new = _HG_HEADS(st_sc[sts], *[_stack_heads(r, hs) for r in (q_sc, k_sc, v_sc, e_sc)], masks_ref[...])
            _unstack_heads(o_ref, hs, o)
            st_sc[sts] = st_new

        _head_groups(group)

        def post(rows):
            gate = p_ref[rows, 3 * d:4 * d].astype(F32)
            o2_ref[rows, :] = _hg_post(o_ref[rows, :], gate, gw_ref[...]).astype(o2_ref.dtype)

        _row_blocks(CHUNK, post)

    idx = lambda b, c: (b * nc + c, 0)
    vec = pl.BlockSpec((1, d), lambda b, c: (0, 0))
    act = pl.BlockSpec((CHUNK, d), idx)
    return pl.pallas_call(
        body, name=name, grid=(seqs, nc),
        in_specs=[pl.BlockSpec((CHUNK, 4 * d), idx), vec, vec] + [pl.BlockSpec(s.shape, lambda b, c: (0, 0)) for s in sums]
        + [pl.BlockSpec(masks.shape, lambda b, c: (0, 0, 0))],
        out_specs=[act, act, pl.BlockSpec((1, N_HEADS, HEAD_DIM, HEAD_DIM), lambda b, c: (b * nc + c, 0, 0, 0))],
        out_shape=[jax.ShapeDtypeStruct((n, d), BF16), jax.ShapeDtypeStruct((n, d), F32),
                   jax.ShapeDtypeStruct((n // CHUNK, N_HEADS, HEAD_DIM, HEAD_DIM), F32)],
        scratch_shapes=[pltpu.VMEM((N_HEADS, HEAD_DIM, HEAD_DIM), F32)] + [pltpu.VMEM((CHUNK, d), F32)] * 3
        + [pltpu.VMEM((sums[0].shape[0], d), F32)],
        compiler_params=_params("arbitrary", "arbitrary"),
    )(proj, lb, gw, *sums, masks)


def _hgrn_bwd(proj, lb, gw, st_all, o, do2, seqs, name):
    n = proj.shape[0]
    nc = n // seqs // CHUNK
    d = D_MODEL

    sums, masks = _hg_level_sums(), _hg_level_masks()

    def body(p_ref, lb_ref, gw_ref, sums_wide_ref, sums_once_ref, masks_ref, st_all_ref, o_ref, do2_ref,
             dp_ref, dlb_ref, dgw_ref,
             dst_sc, q_sc, k_sc, v_sc, e_sc, do_sc, dq_sc, dk_sc, dv_sc, de_sc, dgw_sc):
        first = (pl.program_id(0) == 0) & (pl.program_id(1) == 0)

        @pl.when(pl.program_id(1) == 0)
        def _():
            dst_sc[...] = jnp.zeros_like(dst_sc)

        sums_refs = (sums_wide_ref, sums_once_ref)
        _lane_blocks(d, functools.partial(_hg_pre_block, p_ref, lb_ref, sums_refs, q_sc, k_sc, v_sc, e_sc))
        dgw_sc[...] = jnp.zeros_like(dgw_sc)

        def post_bwd(rows):
            _, vjp = jax.vjp(_hg_post, o_ref[rows, :], p_ref[rows, 3 * d:4 * d].astype(F32), gw_ref[...])
            do_sc[rows, :], dgate, dgw = vjp(do2_ref[rows, :].astype(F32))
            dp_ref[rows, 3 * d:4 * d] = dgate.astype(dp_ref.dtype)
            dgw_sc[...] += dgw

        _row_blocks(CHUNK, post_bwd)

        def group(hs):
            sts = pl.ds(hs[0], len(hs))
            level_masks = masks_ref[...]
            _, vjp = jax.vjp(lambda *a: _HG_HEADS(*a, level_masks), st_all_ref[0, sts],
                             *[_stack_heads(r, hs) for r in (q_sc, k_sc, v_sc, e_sc)])
            grads = vjp((_stack_heads(do_sc, hs), dst_sc[sts]))
            dst_sc[sts] = grads[0]
            for r, val in zip((dq_sc, dk_sc, dv_sc, de_sc), grads[1:]):
                _unstack_heads(r, hs, val)

        _head_groups(group)

        def pre_bwd(at):
            sl = at()
            level_sums = (sums_wide_ref[...], sums_once_ref[...])
            _, vjp = jax.vjp(lambda qraw, f, lb: _hg_pre(qraw, f, lb, level_sums), p_ref[:, sl].astype(F32),
                             p_ref[:, at(d)].astype(F32), lb_ref[:, sl])
            dqraw, df, dlb = vjp((dq_sc[:, sl], dk_sc[:, sl], de_sc[:, sl]))
            dp_ref[:, sl] = dqraw.astype(dp_ref.dtype)
            dp_ref[:, at(d)] = df.astype(dp_ref.dtype)
            dp_ref[:, at(2 * d)] = dv_sc[:, sl].astype(dp_ref.dtype)

            @pl.when(first)
            def _():
                dlb_ref[:, sl] = dlb

            @pl.when(jnp.logical_not(first))
            def _():
                dlb_ref[:, sl] += dlb

        _lane_blocks(d, pre_bwd)

        @pl.when(first)
        def _():
            dgw_ref[...] = dgw_sc[...]

        @pl.when(jnp.logical_not(first))
        def _():
            dgw_ref[...] += dgw_sc[...]

    idx = lambda b, c: (b * nc + nc - 1 - c, 0)
    vec = pl.BlockSpec((1, d), lambda b, c: (0, 0))
    act = pl.BlockSpec((CHUNK, d), idx)
    wide = pl.BlockSpec((CHUNK, 4 * d), idx)
    return pl.pallas_call(
        body, name=name, grid=(seqs, nc),
        in_specs=[wide, vec, vec] + [pl.BlockSpec(s.shape, lambda b, c: (0, 0)) for s in sums] + [
                  pl.BlockSpec(masks.shape, lambda b, c: (0, 0, 0)),
                  pl.BlockSpec((1, N_HEADS, HEAD_DIM, HEAD_DIM), lambda b, c: (b * nc + nc - 1 - c, 0, 0, 0)),
                  act, act],
        out_specs=[wide, vec, vec],
        out_shape=[jax.ShapeDtypeStruct((n, 4 * d), BF16), jax.ShapeDtypeStruct((1, d), F32),
                   jax.ShapeDtypeStruct((1, d), F32)],
        scratch_shapes=[pltpu.VMEM((N_HEADS, HEAD_DIM, HEAD_DIM), F32)]
        + [pltpu.VMEM((CHUNK, d), F32)] * 3 + [pltpu.VMEM((sums[0].shape[0], d), F32)]
        + [pltpu.VMEM((CHUNK, d), F32)] * 4 + [pltpu.VMEM((sums[0].shape[0], d), F32), pltpu.VMEM((1, d), F32)],
        compiler_params=_params("arbitrary", "arbitrary"),
    )(proj, lb, gw, *sums, masks, st_all, o, do2)


def _lane_blocks(width, block_body):
    def trip(j, carry):
        block_body(lambda base=0: pl.ds(pl.multiple_of(j * LANE_BLOCK + base, LANE_BLOCK), LANE_BLOCK))
        return carry

    lax.fori_loop(0, width // LANE_BLOCK, trip, 0)


def _row_blocks(rows, block_body):
    def trip(j, carry):
        block_body(pl.ds(pl.multiple_of(j * ROW_BLOCK, ROW_BLOCK), ROW_BLOCK))
        return carry

    lax.fori_loop(0, rows // ROW_BLOCK, trip, 0)


def _hg_pre_block(p_ref, lb_ref, sums_refs, q_sc, k_sc, v_sc, e_sc, at):
    sl = at()
    q_sc[:, sl], k_sc[:, sl], e_sc[:, sl] = _hg_pre(
        p_ref[:, sl].astype(F32), p_ref[:, at(D_MODEL)].astype(F32), lb_ref[:, sl], [r[...] for r in sums_refs])
    v_sc[:, sl] = p_ref[:, at(2 * D_MODEL)].astype(F32)


def _gd_xp(halo_ref, p_ref, sl, first_chunk):
    halo = jnp.where(first_chunk, 0.0, halo_ref[:, sl].astype(F32))
    return jnp.concatenate([halo, p_ref[:, sl].astype(F32)], axis=0)


def _gdn_fwd(projm, projab, cw, alog, dtb, onw, seqs, name):
    n = projm.shape[0]
    nc = n // seqs // CHUNK
    d = D_MODEL
    per_halo = CHUNK // HALO

    def body(p_ref, halo_ref, ab_ref, cw_ref, alog_ref, dtb_ref, onw_ref, o2_ref, st_all_ref,
             st_sc, c_sc, beta_sc, g_sc):
        @pl.when(pl.program_id(1) == 0)
        def _():
            st_sc[...] = jnp.zeros_like(st_sc)

        def conv(at):
            sl = at()
            c_sc[:, sl] = _gd_conv(_gd_xp(halo_ref, p_ref, sl, pl.program_id(1) == 0), cw_ref[:, sl])

        _lane_blocks(3 * d, conv)
        beta_sc[...], g_sc[...] = _gd_gates(ab_ref[:, 0:N_HEADS], ab_ref[:, N_HEADS:2 * N_HEADS], alog_ref[...],
                                            dtb_ref[...])
        st_all_ref[0] = st_sc[...]

        def group(hs):
            sts = pl.ds(hs[0], len(hs))
            o2, st_new = _GD_HEADS(
                st_sc[sts], _stack_heads(c_sc, hs), _stack_heads(c_sc, hs, N_HEADS), _stack_heads(c_sc, hs, 2 * N_HEADS),
                _stack_heads(beta_sc, hs), _stack_heads(g_sc, hs), _stack_heads(g_sc, hs, N_HEADS),
                _stack_heads(p_ref, hs, 3 * N_HEADS).astype(F32), onw_ref[...])
            _unstack_heads(o2_ref, hs, o2)
            st_sc[sts] = st_new

        _head_groups(group)

    idx = lambda b, c: (b * nc + c, 0)
    const = lambda b, c: (0, 0)
    return pl.pallas_call(
        body, name=name, grid=(seqs, nc),
        in_specs=[pl.BlockSpec((CHUNK, 4 * d), idx),
                  pl.BlockSpec((HALO, 3 * d), lambda b, c: (jnp.maximum((b * nc + c) * per_halo - 1, 0), 0)),
                  pl.BlockSpec((CHUNK, AB_PAD), idx),
                  pl.BlockSpec((CONV_K, 3 * d), const), pl.BlockSpec((1, N_HEADS), const),
                  pl.BlockSpec((1, N_HEADS), const), pl.BlockSpec((1, HEAD_DIM), const)],
        out_specs=[pl.BlockSpec((CHUNK, d), idx),
                   pl.BlockSpec((1, N_HEADS, HEAD_DIM, HEAD_DIM), lambda b, c: (b * nc + c, 0, 0, 0))],
        out_shape=[jax.ShapeDtypeStruct((n, d), BF16),
                   jax.ShapeDtypeStruct((n // CHUNK, N_HEADS, HEAD_DIM, HEAD_DIM), F32)],
        scratch_shapes=[pltpu.VMEM((N_HEADS, HEAD_DIM, HEAD_DIM), F32), pltpu.VMEM((CHUNK, 3 * d), F32),
                        pltpu.VMEM((CHUNK, d), F32), pltpu.VMEM((CHUNK, 2 * d), F32)],
        compiler_params=_params("arbitrary", "arbitrary"),
    )(projm, projm, projab, cw, alog, dtb, onw)


def _gdn_bwd(projm, projab, cw, alog, dtb, onw, st_all, do2, seqs, name):
    n = projm.shape[0]
    nc = n // seqs // CHUNK
    d = D_MODEL
    per_halo = CHUNK // HALO

    def body(p_ref, halo_ref, ab_ref, cw_ref, alog_ref, dtb_ref, onw_ref, st_all_ref, do2_ref,
             dp_ref, dab_ref, dcw_ref, dalog_ref, ddtb_ref, donw_ref,
             dst_sc, dhalo_sc, c_sc, beta_sc, g_sc, dc_sc, dbeta_sc, dg_sc, donw_sc):
        step = pl.program_id(1)
        first = (pl.program_id(0) == 0) & (step == 0)

        @pl.when(step == 0)
        def _():
            dst_sc[...] = jnp.zeros_like(dst_sc)
            dhalo_sc[...] = jnp.zeros_like(dhalo_sc)

        donw_sc[...] = jnp.zeros_like(donw_sc)

        def conv(at):
            sl = at()
            c_sc[:, sl] = _gd_conv(_gd_xp(halo_ref, p_ref, sl, step == nc - 1), cw_ref[:, sl])

        _lane_blocks(3 * d, conv)
        (beta_sc[...], g_sc[...]), gates_vjp = jax.vjp(
            _gd_gates, ab_ref[:, 0:N_HEADS], ab_ref[:, N_HEADS:2 * N_HEADS], alog_ref[...], dtb_ref[...])

        def group(hs):
            sts = pl.ds(hs[0], len(hs))
            _, vjp = jax.vjp(
                _GD_HEADS, st_all_ref[0, sts], _stack_heads(c_sc, hs), _stack_heads(c_sc, hs, N_HEADS),
                _stack_heads(c_sc, hs, 2 * N_HEADS), _stack_heads(beta_sc, hs), _stack_heads(g_sc, hs),
                _stack_heads(g_sc, hs, N_HEADS), _stack_heads(p_ref, hs, 3 * N_HEADS).astype(F32), onw_ref[...])
            dst, dq, dk, dv, dbeta, dg, ddiff, dgate, donw = vjp(
                (_stack_heads(do2_ref, hs).astype(F32), dst_sc[sts]))
            _unstack_heads(dg_sc, hs, ddiff, N_HEADS)
            dst_sc[sts] = dst
            _unstack_heads(dc_sc, hs, dq)
            _unstack_heads(dc_sc, hs, dk, N_HEADS)
            _unstack_heads(dc_sc, hs, dv, 2 * N_HEADS)
            _unstack_heads(dbeta_sc, hs, dbeta)
            _unstack_heads(dg_sc, hs, dg)
            _unstack_heads(dp_ref, hs, dgate, 3 * N_HEADS)
            donw_sc[...] += donw

        _head_groups(group)
        def conv_bwd(at):
            sl = at()
            _, vjp = jax.vjp(_gd_conv, _gd_xp(halo_ref, p_ref, sl, step == nc - 1), cw_ref[:, sl])
            dxp, dcw = vjp(dc_sc[:, sl])
            dqkv = jnp.concatenate([dxp[HALO:CHUNK], dxp[CHUNK:HALO + CHUNK] + dhalo_sc[:, sl]], axis=0)
            dp_ref[:, sl] = dqkv.astype(dp_ref.dtype)
            dhalo_sc[:, sl] = dxp[0:HALO]

            @pl.when(first)
            def _():
                dcw_ref[:, sl] = dcw

            @pl.when(jnp.logical_not(first))
            def _():
                dcw_ref[:, sl] += dcw

        _lane_blocks(3 * d, conv_bwd)
        da, db, dalog, ddtb = gates_vjp((dbeta_sc[...], dg_sc[...]))
        dab_ref[...] = jnp.concatenate(
            [da, db, jnp.zeros((CHUNK, AB_PAD - 2 * N_HEADS), F32)], axis=1).astype(dab_ref.dtype)

        @pl.when(first)
        def _():
            dalog_ref[...] = dalog
            ddtb_ref[...] = ddtb
            donw_ref[...] = donw_sc[...]

        @pl.when(jnp.logical_not(first))
        def _():
            dalog_ref[...] += dalog
            ddtb_ref[...] += ddtb
            donw_ref[...] += donw_sc[...]

    rev = lambda b, c: b * nc + nc - 1 - c
    idx = lambda b, c: (rev(b, c), 0)
    const = lambda b, c: (0, 0)
    small = [pl.BlockSpec((CONV_K, 3 * d), const), pl.BlockSpec((1, N_HEADS), const),
             pl.BlockSpec((1, N_HEADS), const), pl.BlockSpec((1, HEAD_DIM), const)]
    return pl.pallas_call(
        body, name=name, grid=(seqs, nc),
        in_specs=[pl.BlockSpec((CHUNK, 4 * d), idx),
                  pl.BlockSpec((HALO, 3 * d), lambda b, c: (jnp.maximum(rev(b, c) * per_halo - 1, 0), 0)),
                  pl.BlockSpec((CHUNK, AB_PAD), idx)] + small + [
                  pl.BlockSpec((1, N_HEADS, HEAD_DIM, HEAD_DIM), lambda b, c: (rev(b, c), 0, 0, 0)),
                  pl.BlockSpec((CHUNK, d), idx)],
        out_specs=[pl.BlockSpec((CHUNK, 4 * d), idx), pl.BlockSpec((CHUNK, AB_PAD), idx)] + small,
        out_shape=[jax.ShapeDtypeStruct((n, 4 * d), BF16), jax.ShapeDtypeStruct((n, AB_PAD), BF16),
                   jax.ShapeDtypeStruct((CONV_K, 3 * d), F32), jax.ShapeDtypeStruct((1, N_HEADS), F32),
                   jax.ShapeDtypeStruct((1, N_HEADS), F32), jax.ShapeDtypeStruct((1, HEAD_DIM), F32)],
        scratch_shapes=[pltpu.VMEM((N_HEADS, HEAD_DIM, HEAD_DIM), F32), pltpu.VMEM((HALO, 3 * d), F32),
                        pltpu.VMEM((CHUNK, 3 * d), F32), pltpu.VMEM((CHUNK, d), F32), pltpu.VMEM((CHUNK, 2 * d), F32),
                        pltpu.VMEM((CHUNK, 3 * d), F32), pltpu.VMEM((CHUNK, d), F32), pltpu.VMEM((CHUNK, 2 * d), F32),
                        pltpu.VMEM((1, HEAD_DIM), F32)],
        compiler_params=_params("arbitrary", "arbitrary"),
    )(projm, projm, projab, cw, alog, dtb, onw, st_all, do2)


def _adam_update(w, g, m, v):
    b1c = 1.0 - ADAM_B1 ** ADAM_STEP
    b2c = 1.0 - ADAM_B2 ** ADAM_STEP
    m_new = ADAM_B1 * m + (1.0 - ADAM_B1) * g
    v_new = ADAM_B2 * v + (1.0 - ADAM_B2) * (g * g)
    delta = -ADAM_LR * ((m_new / b1c) / (jnp.sqrt(v_new / b2c) + ADAM_EPS) + ADAM_WD * w)
    return delta, m_new, v_new


def _adamw(w, g, m, v, name, tr=256):
    r, c = w.shape
    tr = _tile(r, tr)

    def body(w_ref, g_ref, m_ref, v_ref, d_ref, mo_ref, vo_ref):
        d_ref[...], mo_ref[...], vo_ref[...] = _adam_update(w_ref[...], g_ref[...], m_ref[...], v_ref[...])

    blk = pl.BlockSpec((tr, c), lambda i: (i, 0))
    return pl.pallas_call(
        body, name=name, grid=(r // tr,),
        in_specs=[blk] * 4, out_specs=[blk] * 3,
        out_shape=[jax.ShapeDtypeStruct((r, c), F32)] * 3,
        compiler_params=_params("arbitrary"),
    )(w, g, m, v)


def _adamw_slots(w, slot_bufs, m, v, name, tr=256):
    nl, r, c = w.shape
    tr = _tile(r, tr)

    def body(*refs):
        w_ref = refs[0]
        g_refs = refs[1:1 + nl]
        m_ref, v_ref, go_ref, d_ref, mo_ref, vo_ref = refs[1 + nl:]
        for k in range(nl):
            @pl.when(pl.program_id(0) == k)
            def _(k=k):
                g = g_refs[k][0].astype(F32)
                for s in range(1, N_DEV):
                    g = g + g_refs[k][s].astype(F32)
                go_ref[0] = g

        d_ref[0], mo_ref[0], vo_ref[0] = _adam_update(w_ref[0], go_ref[0], m_ref[0], v_ref[0])

    blk = pl.BlockSpec((1, tr, c), lambda l, i: (l, i, 0))
    g_specs = [pl.BlockSpec((N_DEV, tr, c), lambda l, i, k=k: (0, jnp.where(l == k, i, 0), 0)) for k in range(nl)]
    return pl.pallas_call(
        body, name=name, grid=(nl, r // tr),
        in_specs=[blk] + g_specs + [blk, blk], out_specs=[blk] * 4,
        out_shape=[jax.ShapeDtypeStruct((nl, r, c), F32)] * 4,
        compiler_params=_params("arbitrary", "arbitrary"),
    )(w, *slot_bufs, m, v)


def _mesh_pos():
    return lax.axis_index("x"), lax.axis_index("y"), lax.axis_index("c")


def _flip(pos, p):
    x, y, c = pos
    return ((1 - x) if p & 4 else x, (1 - y) if p & 2 else y, (1 - c) if p & 1 else c)


def _lin(pos):
    return 4 * pos[0] + 2 * pos[1] + pos[2]


_HBM = pl.BlockSpec(memory_space=pltpu.HBM)
_SEM = pl.BlockSpec(memory_space=pltpu.SEMAPHORE)
_DATAFLOW = pltpu.SideEffectType.DATAFLOW_SIDE_EFFECTING


class _Item:
    def __init__(self, src, land_shape, src_pick, dst_pick):
        self.src, self.land_shape, self.src_pick, self.dst_pick = src, land_shape, src_pick, dst_pick


def _remote_copies(items, src, land, send_sem, recv_sem, me, arriving):
    me_i = _lin(me)
    out = []
    for it, s_ref, l_ref in zip(items, src, land):
        for p in range(1, N_DEV):
            peer = _flip(me, p)
            out.append(pltpu.make_async_remote_copy(
                src_ref=it.src_pick(s_ref, _lin(peer)),
                dst_ref=it.dst_pick(l_ref, _lin(peer) if arriving else me_i),
                send_sem=send_sem, recv_sem=recv_sem, device_id=peer, device_id_type=pl.DeviceIdType.MESH))
    return out


def _own_copies(items, src, land, sem, me):
    me_i = _lin(me)
    return [pltpu.make_async_copy(it.src_pick(s_ref, me_i), it.dst_pick(l_ref, me_i), sem)
            for it, s_ref, l_ref in zip(items, src, land)]


def _exchange_start(groups, name):
    items = [it for g in groups for it in g]
    n, ng = len(items), len(groups)
    first = [sum(len(g) for g in groups[:gi]) for gi in range(ng)]

    def body(*refs):
        src, land = refs[0:n], refs[n:2 * n]
        send_sems, recv_sems = refs[2 * n:2 * n + ng], refs[2 * n + ng:2 * n + 2 * ng]
        token = refs[4 * n + 2 * ng]
        me = _mesh_pos()
        for gi, g in enumerate(groups):
            sl = slice(first[gi], first[gi] + len(g))
            for cp in _remote_copies(g, src[sl], land[sl], send_sems[gi], recv_sems[gi], me, arriving=False):
                cp.start()
            for cp in _own_copies(g, src[sl], land[sl], recv_sems[gi], me):
                cp.start()
        token[...] = jnp.zeros_like(token)

    srcs = [pltpu.with_memory_space_constraint(it.src, pltpu.HBM) for it in items]
    lands = [pltpu.with_memory_space_constraint(lax.empty(it.land_shape, it.src.dtype), pltpu.HBM) for it in items]
    res = pl.pallas_call(
        body, name=name,
        out_shape=([pltpu.SemaphoreType.DMA(())] * (2 * ng)
                   + [pltpu.HBM(it.src.shape, it.src.dtype) for it in items]
                   + [pltpu.HBM(it.land_shape, it.src.dtype) for it in items]
                   + [jax.ShapeDtypeStruct((8, 128), F32)]),
        in_specs=[_HBM] * (2 * n),
        out_specs=[_SEM] * (2 * ng) + [_HBM] * (2 * n) + [pl.BlockSpec(memory_space=pltpu.VMEM)],
        input_output_aliases={i: 2 * ng + i for i in range(2 * n)},
        compiler_params=pltpu.CompilerParams(has_side_effects=_DATAFLOW),
    )(*srcs, *lands)
    send_sems, recv_sems = res[0:ng], res[ng:2 * ng]
    src_thru, land_thru = res[2 * ng:2 * ng + n], res[2 * ng + n:2 * ng + 2 * n]
    handles = []
    for gi, g in enumerate(groups):
        sl = slice(first[gi], first[gi] + len(g))
        handles.append((g, src_thru[sl], land_thru[sl], send_sems[gi], recv_sems[gi]))
    return handles, res[-1]


def _exchange_wait(handle, after, name):
    items, src_thru, land_thru, send_sem, recv_sem = handle
    k = len(items)

    def body(*refs):
        src, land = refs[0:k], refs[k:2 * k]
        send_ref, recv_ref = refs[2 * k], refs[2 * k + 1]
        for cp in _remote_copies(items, src, land, send_ref, recv_ref, _mesh_pos(), arriving=True):
            cp.wait_send()
            cp.wait_recv()
        for cp in _own_copies(items, src, land, recv_ref, _mesh_pos()):
            cp.wait()

    res = pl.pallas_call(
        body, name=name,
        out_shape=([pltpu.HBM(s.shape, s.dtype) for s in src_thru] + [pltpu.HBM(l.shape, l.dtype) for l in land_thru]),
        in_specs=[_HBM] * (2 * k) + [_SEM, _SEM, pl.BlockSpec(memory_space=pl.ANY)],
        out_specs=[_HBM] * (2 * k),
        input_output_aliases={i: i for i in range(2 * k)},
        compiler_params=pltpu.CompilerParams(has_side_effects=_DATAFLOW),
    )(*src_thru, *land_thru, send_sem, recv_sem, after)
    return res[k:2 * k]


def _whole(ref, i):
    return ref


def _slot(ref, i):
    return ref.at[i]


def _rows_of(r):
    return lambda ref, i: ref.at[pl.ds(pl.multiple_of(i * r, r), r), :]


def _cols_of(c):
    return lambda ref, i: ref.at[:, pl.ds(pl.multiple_of(i * c, c), c)]


def _all_reduce_small(buf, name):
    r, c = buf.shape

    def body(src_ref, out_ref, all_ref, send_sems, recv_sems):
        me = _mesh_pos()
        me_i = _lin(me)
        all_ref[me_i] = src_ref[...]
        for p in range(1, N_DEV):
            peer = _flip(me, p)
            pltpu.make_async_remote_copy(
                src_ref=src_ref, dst_ref=all_ref.at[me_i], send_sem=send_sems.at[p - 1], recv_sem=recv_sems.at[p - 1],
                device_id=peer, device_id_type=pl.DeviceIdType.MESH).start()
        for p in range(1, N_DEV):
            peer = _flip(me, p)
            cp = pltpu.make_async_remote_copy(
                src_ref=src_ref, dst_ref=all_ref.at[_lin(peer)], send_sem=send_sems.at[p - 1],
                recv_sem=recv_sems.at[p - 1], device_id=peer, device_id_type=pl.DeviceIdType.MESH)
            cp.wait_recv()
            cp.wait_send()
        acc = all_ref[0]
        for s in range(1, N_DEV):
            acc = acc + all_ref[s]
        out_ref[...] = acc

    vm = pl.BlockSpec(memory_space=pltpu.VMEM)
    return pl.pallas_call(
        body, name=name, in_specs=[vm], out_specs=vm,
        out_shape=jax.ShapeDtypeStruct((r, c), F32),
        scratch_shapes=[pltpu.VMEM((N_DEV, r, c), F32), pltpu.SemaphoreType.DMA((N_DEV - 1,)),
                        pltpu.SemaphoreType.DMA((N_DEV - 1,))],
        compiler_params=pltpu.CompilerParams(has_side_effects=True),
    )(buf)


def _unshard_cols(g):
    s, l, r, c = g.shape
    return jnp.transpose(g, (1, 2, 0, 3)).reshape(l, r, s * c)


def kernel(x, gdn_w_in, gdn_conv, gdn_a_log, gdn_dt_bias, gdn_onorm, gdn_w_out, hgrn_w_in, hgrn_lb_logits, hgrn_gnorm, hgrn_w_out, norm_mix, norm_mlp, mlp_w_up, mlp_w_down, norm_final, loss_target, m_gdn_w_in, m_gdn_conv, m_gdn_a_log, m_gdn_dt_bias, m_gdn_onorm, m_gdn_w_out, m_hgrn_w_in, m_hgrn_lb_logits, m_hgrn_gnorm, m_hgrn_w_out, m_norm_mix, m_norm_mlp, m_mlp_w_up, m_mlp_w_down, m_norm_final, v_gdn_w_in, v_gdn_conv, v_gdn_a_log, v_gdn_dt_bias, v_gdn_onorm, v_gdn_w_out, v_hgrn_w_in, v_hgrn_lb_logits, v_hgrn_gnorm, v_hgrn_w_out, v_norm_mix, v_norm_mlp, v_mlp_w_up, v_mlp_w_down, v_norm_final):
    seqs, seq_len, d = x.shape
    n = seqs * seq_len
    me_i = _lin(_mesh_pos())
    x2 = x.reshape(n, d)
    target = loss_target.reshape(n, d)
    n_gdn, n_hgrn = gdn_w_in.shape[0], hgrn_w_in.shape[0]

    r_out, r_down = gdn_w_out.shape[1], mlp_w_down.shape[1]
    c_gin, c_hin, c_up = gdn_w_in.shape[2], hgrn_w_in.shape[2], mlp_w_up.shape[2]

    def gathered(w, pick, land_shape):
        return _Item(w.astype(BF16), land_shape, _whole, pick)

    groups = [[_Item(gdn_conv, (N_DEV,) + gdn_conv.shape, _whole, _slot),
               _Item(hgrn_gnorm, (N_DEV,) + hgrn_gnorm.shape, _whole, _slot)]]
    for i in range(DEPTH):
        j = i // 2
        if i % 2 == 0:
            groups += [[gathered(gdn_w_in[j], _slot, (N_DEV, d, c_gin))],
                       [gathered(gdn_w_out[j], _rows_of(r_out), (N_DEV * r_out, d))]]
        else:
            groups += [[gathered(hgrn_w_in[j], _cols_of(c_hin), (d, N_DEV * c_hin))],
                       [gathered(hgrn_w_out[j], _rows_of(r_out), (N_DEV * r_out, d))]]
        groups += [[gathered(mlp_w_up[i], _cols_of(c_up), (d, N_DEV * c_up))],
                   [gathered(mlp_w_down[i], _rows_of(r_down), (N_DEV * r_down, d))]]
    gather_handles, token = _exchange_start(groups, "gather_start")
    lbs = _lb_fwd(hgrn_lb_logits + token[0:1, 0:1], "lb_fwd")

    def arrived(k, after, name):
        return _exchange_wait(gather_handles[k], after, "gather_wait_" + name)

    saved = []
    w_in, w_ab, w_out, w_up, w_down = ([None] * DEPTH for _ in range(5))
    h = x2
    for i in range(DEPTH):
        j = i // 2
        if i == 0:
            g_conv, g_gnorm = arrived(0, h, "small")
            conv_full = _unshard_cols(g_conv)
            gnorm_full = jnp.transpose(g_gnorm, (1, 0, 2)).reshape(n_hgrn, d)
        if i == 0:
            y = _rms_fwd(h, norm_mix[0:1], "rms_mix_0")
        (w_in[i],) = arrived(1 + 4 * i, y, f"in_{i}")
        if i % 2 == 0:
            w_gin = jnp.transpose(w_in[i], (1, 0, 2)).reshape(d, N_DEV * c_gin)
            w_in[i] = w_gin[:, :GDN_MAIN]
            w_ab[i] = jnp.pad(w_gin[:, GDN_MAIN:], ((0, 0), (0, AB_PAD - 2 * N_HEADS)))
            projm = _mm(y, w_in[i], "nn", [BF16], f"gdn_proj_{i}")
            projab = _mm(y, w_ab[i], "nn", [F32], f"gdn_proj_ab_{i}")
            o2, st_all = _gdn_fwd(projm, projab, conv_full[j], gdn_a_log[j:j + 1], gdn_dt_bias[j:j + 1],
                                  gdn_onorm[j:j + 1], seqs, f"gdn_fwd_{i}")
            mix = (projm, projab, st_all)
        else:
            proj = _mm(y, w_in[i], "nn", [BF16], f"hgrn_proj_{i}")
            o2, o_raw, st_all = _hgrn_fwd(proj, lbs[i:i + 1], gnorm_full[j:j + 1], seqs, f"hgrn_fwd_{i}")
            mix = (proj, o_raw, st_all)
        (w_out[i],) = arrived(2 + 4 * i, o2, f"out_{i}")
        h1, y2 = _mm(o2, w_out[i], "nn", [F32, BF16], f"mix_out_{i}", epilogue=_ep_residual_norm, extras=(h,),
                     vectors=(norm_mlp[i:i + 1],))
        (w_up[i],) = arrived(3 + 4 * i, y2, f"up_{i}")
        u, act = _mm(y2, w_up[i], "nn", [BF16, BF16], f"mlp_up_{i}",
                     epilogue=lambda acc: (acc, jnp.square(jnp.maximum(acc, 0.0))))
        (w_down[i],) = arrived(4 + 4 * i, act, f"down_{i}")
        saved.append((h, y, mix, o2, h1, y2, u, act))
        if i + 1 < DEPTH:
            h, y = _mm(act, w_down[i], "nn", [F32, BF16], f"mlp_down_{i}", epilogue=_ep_residual_norm, extras=(h1,),
                       vectors=(norm_mix[i + 1:i + 2],))
        else:
            h = _mm(act, w_down[i], "nn", [F32], f"mlp_down_{i}", epilogue=lambda acc, res: (res + acc,),
                    extras=(h1,))

    dh, dh_b, d_nf, sq = _loss_head(h, norm_final.reshape(1, d), target, "loss_head")

    d_nmix, d_nmlp = [None] * DEPTH, [None] * DEPTH
    d_conv, d_alog, d_dtb, d_onorm = [None] * n_gdn, [None] * n_gdn, [None] * n_gdn, [None] * n_gdn
    d_lb = [jnp.zeros((1, d), F32)] * DEPTH
    d_gnorm = [None] * n_hgrn
    mlp_handles, mix_handles = [None] * DEPTH, [None] * DEPTH
    token = None
    for i in reversed(range(DEPTH)):
        j = i // 2
        h_in, y, mix, o2, h1, y2, u, act = saved[i]
        g_down = _mm(act, dh_b, "tn", [BF16], f"g_down_{i}", after=token)
        du = _mm(dh_b, w_down[i], "nt", [BF16], f"d_u_{i}",
                 epilogue=lambda acc, uu: (acc * (2.0 * jnp.maximum(uu.astype(F32), 0.0)),), extras=(u,))
        g_up = _mm(y2, du, "tn", [BF16], f"g_up_{i}")
        mlp_handles[i], token = _exchange_start(
            [[_Item(g_down, (N_DEV, r_down, d), _rows_of(r_down), _slot)],
             [_Item(g_up, (N_DEV, d, c_up), _cols_of(c_up), _slot)]], f"scatter_start_mlp_{i}")
        dh1, dh1_b, d_nmlp[i] = _mm(du, w_up[i], "nt", [F32, BF16], f"d_y2_{i}", epilogue=_ep_norm_bwd,
                                     extras=(h1, dh), vectors=(norm_mlp[i:i + 1],), n_sums=1, after=token)
        g_out = _mm(o2, dh1_b, "tn", [BF16], f"g_out_{i}")
        do2 = _mm(dh1_b, w_out[i], "nt", [BF16], f"d_o2_{i}")
        if i % 2 == 0:
            projm, projab, st_all = mix
            dpm, dpab, d_conv[j], d_alog[j], d_dtb[j], d_onorm[j] = _gdn_bwd(
                projm, projab, conv_full[j], gdn_a_log[j:j + 1], gdn_dt_bias[j:j + 1], gdn_onorm[j:j + 1],
                st_all, do2, seqs, f"gdn_bwd_{i}")
            g_main = _mm(y, dpm, "tn", [BF16], f"g_in_{i}")
            g_ab = _mm(y, dpab, "tn", [BF16], f"g_in_ab_{i}")
            g_in = jnp.concatenate([g_main, g_ab[:, :2 * N_HEADS]], axis=1)
            g_in = jnp.transpose(g_in.reshape(d, N_DEV, c_gin), (1, 0, 2))
            in_item = _Item(g_in, (N_DEV, d, c_gin), _slot, _slot)
            dy_ab = _mm(dpab, w_ab[i], "nt", [F32], f"d_y_ab_{i}")
            dp, dy_extras = dpm, (dy_ab, h_in, dh1)
            dy_epilogue = lambda acc, e, xx, dres, w: _ep_norm_bwd(acc + e, xx, dres, w)
        else:
            proj, o_raw, st_all = mix
            dp, d_lb[i], d_gnorm[j] = _hgrn_bwd(proj, lbs[i:i + 1], gnorm_full[j:j + 1], st_all, o_raw, do2,
                                               seqs, f"hgrn_bwd_{i}")
            g_in = _mm(y, dp, "tn", [BF16], f"g_in_{i}")
            in_item = _Item(g_in, (N_DEV, d, c_hin), _cols_of(c_hin), _slot)
            dy_extras, dy_epilogue = (h_in, dh1), _ep_norm_bwd
        mix_handles[i], token = _exchange_start(
            [[_Item(g_out, (N_DEV, r_out, d), _rows_of(r_out), _slot)], [in_item]], f"scatter_start_mix_{i}")
        dh, dh_b, d_nmix[i] = _mm(dp, w_in[i], "nt", [F32, BF16], f"d_y_{i}", epilogue=dy_epilogue, extras=dy_extras,
                                  vectors=(norm_mix[i:i + 1],), n_sums=1, after=token)
        token = None
    grad_x = dh.reshape(x.shape)

    def landed(handles, k, layers, after, name):
        return [_exchange_wait(handles[i][k], after, f"scatter_wait_{name}_{i}")[0] for i in layers]

    every, even, odd = range(DEPTH), range(0, DEPTH, 2), range(1, DEPTH, 2)
    upd = {}
    upd["mlp_w_down"] = _adamw_slots(mlp_w_down, landed(mlp_handles, 0, every, dh, "down"), m_mlp_w_down,
                                     v_mlp_w_down, "adamw_mlp_w_down")
    upd["mlp_w_up"] = _adamw_slots(mlp_w_up, landed(mlp_handles, 1, every, upd["mlp_w_down"][1], "up"), m_mlp_w_up,
                                   v_mlp_w_up, "adamw_mlp_w_up")
    upd["hgrn_w_out"] = _adamw_slots(hgrn_w_out, landed(mix_handles, 0, odd, upd["mlp_w_up"][1], "out"),
                                     m_hgrn_w_out, v_hgrn_w_out, "adamw_hgrn_w_out")
    upd["hgrn_w_in"] = _adamw_slots(hgrn_w_in, landed(mix_handles, 1, odd, upd["hgrn_w_out"][1], "in"), m_hgrn_w_in,
                                    v_hgrn_w_in, "adamw_hgrn_w_in")
    upd["gdn_w_out"] = _adamw_slots(gdn_w_out, landed(mix_handles, 0, even, upd["hgrn_w_in"][1], "out"),
                                    m_gdn_w_out, v_gdn_w_out, "adamw_gdn_w_out")
    upd["gdn_w_in"] = _adamw_slots(gdn_w_in, landed(mix_handles, 1, even, upd["gdn_w_out"][1], "in"), m_gdn_w_in,
                                   v_gdn_w_in, "adamw_gdn_w_in")

    def update(name, w, g, m, v):
        shape = w.shape
        c = shape[-1]
        res = _adamw(w.reshape(-1, c), g.reshape(-1, c), m.reshape(-1, c), v.reshape(-1, c), "adamw_" + name)
        return [g.reshape(shape)] + [o.reshape(shape) for o in res]

    dlb_rows = jnp.concatenate(d_lb, axis=0)
    tail = jnp.concatenate(
        [jnp.concatenate(d_onorm, axis=1), jnp.concatenate(d_alog, axis=1), jnp.concatenate(d_dtb, axis=1)], axis=1)
    tail = jnp.pad(tail, ((0, 0), (0, d - tail.shape[1])))
    conv_rows = jnp.stack(d_conv).reshape(-1, d)
    packed = jnp.concatenate(
        [jnp.concatenate(d_nmix, axis=0), jnp.concatenate(d_nmlp, axis=0), d_nf, sq, dlb_rows,
         jnp.concatenate(d_gnorm, axis=0), tail, conv_rows], axis=0)
    pad_rows = (-packed.shape[0]) % 8
    packed = jnp.pad(packed, ((0, pad_rows), (0, 0)))
    tot = _all_reduce_small(packed, "reduce_small")
    r0 = 0
    g_nmix = tot[r0:r0 + DEPTH]; r0 += DEPTH
    g_nmlp = tot[r0:r0 + DEPTH]; r0 += DEPTH
    g_nf = tot[r0]; r0 += 1
    loss = tot[r0, 0]; r0 += 1
    g_lb = _lb_bwd(hgrn_lb_logits, tot[r0:r0 + DEPTH], "lb_bwd"); r0 += DEPTH
    g_gnorm_full = tot[r0:r0 + n_hgrn]; r0 += n_hgrn
    t_row = tot[r0]; r0 += 1
    g_conv_full = tot[r0:r0 + n_gdn * CONV_K * 3].reshape(n_gdn, CONV_K, 3 * d)
    g_onorm = t_row[0:n_gdn * HEAD_DIM].reshape(n_gdn, HEAD_DIM)
    o1 = n_gdn * HEAD_DIM
    g_alog = t_row[o1:o1 + n_gdn * N_HEADS].reshape(n_gdn, N_HEADS)
    g_dtb = t_row[o1 + n_gdn * N_HEADS:o1 + 2 * n_gdn * N_HEADS].reshape(n_gdn, N_HEADS)
    c_gn, c_cv = hgrn_gnorm.shape[1], gdn_conv.shape[2]
    g_gnorm = lax.dynamic_slice_in_dim(g_gnorm_full, me_i * c_gn, c_gn, axis=1)
    g_conv = lax.dynamic_slice_in_dim(g_conv_full, me_i * c_cv, c_cv, axis=2)

    upd["gdn_conv"] = update("gdn_conv", gdn_conv, g_conv, m_gdn_conv, v_gdn_conv)
    upd["gdn_a_log"] = update("gdn_a_log", gdn_a_log, g_alog, m_gdn_a_log, v_gdn_a_log)
    upd["gdn_dt_bias"] = update("gdn_dt_bias", gdn_dt_bias, g_dtb, m_gdn_dt_bias, v_gdn_dt_bias)
    upd["gdn_onorm"] = update("gdn_onorm", gdn_onorm, g_onorm, m_gdn_onorm, v_gdn_onorm)
    upd["hgrn_lb_logits"] = update("hgrn_lb_logits", hgrn_lb_logits, g_lb, m_hgrn_lb_logits, v_hgrn_lb_logits)
    upd["hgrn_gnorm"] = update("hgrn_gnorm", hgrn_gnorm, g_gnorm, m_hgrn_gnorm, v_hgrn_gnorm)
    upd["norm_mix"] = update("norm_mix", norm_mix, g_nmix, m_norm_mix, v_norm_mix)
    upd["norm_mlp"] = update("norm_mlp", norm_mlp, g_nmlp, m_norm_mlp, v_norm_mlp)
    upd["norm_final"] = update("norm_final", norm_final, g_nf, m_norm_final, v_norm_final)

    order = ["gdn_w_in", "gdn_conv", "gdn_a_log", "gdn_dt_bias", "gdn_onorm", "gdn_w_out", "hgrn_w_in",
             "hgrn_lb_logits", "hgrn_gnorm", "hgrn_w_out", "norm_mix", "norm_mlp", "mlp_w_up", "mlp_w_down",
             "norm_final"]
    outs = [loss, grad_x]
    for k in range(4):
        outs += [upd[name][k] for name in order]
    return tuple(outs)
```

```python
import functools

import numpy as np
import jax
import jax.numpy as jnp
from jax import lax
from jax.experimental import pallas as pl
from jax.experimental.pallas import tpu as pltpu

F32 = jnp.float32
BF16 = jnp.bfloat16

D_MODEL = 1024
N_HEADS = 8
HEAD_DIM = 128
CHUNK = 64
SUB = 16
N_SUB = CHUNK // SUB
CONV_K = 4
HALO = 16
EPS = 1e-6
DEPTH = 4
N_DEV = 8
GDN_MAIN = 4 * D_MODEL
GDN_IN = GDN_MAIN + 2 * N_HEADS
AB_PAD = 128
HEAD_GROUP = 8
LANE_BLOCK = 256
ROW_BLOCK = 16

ADAM_LR = 0.001
ADAM_B1 = 0.9
ADAM_B2 = 0.999
ADAM_EPS = 1e-08
ADAM_WD = 0.01
ADAM_STEP = 10

VMEM_LIMIT = 56 * 1024 * 1024
MM_TILE = 1024
MM_VMEM_BUDGET = 40 * 1024 * 1024

_DIMS = {
    "nn": (((1,), (0,)), ((), ())),
    "nt": (((1,), (1,)), ((), ())),
    "tn": (((0,), (0,)), ((), ())),
}


def _parts(x, n):
    if n == 1 and x.dtype == BF16:
        return [x]
    out = []
    r = x.astype(F32)
    for i in range(n):
        p = r.astype(BF16)
        out.append(p)
        if i + 1 < n:
            r = r - p.astype(F32)
    return out


def _dot_raw(a, b, mode, na, nb):
    ap, bp = _parts(a, na), _parts(b, nb)
    nmax = max(na, nb)
    pairs = [(i, j) for i in range(na) for j in range(nb) if i + j < nmax]
    ka = 0 if mode == "tn" else 1
    kb = 1 if mode == "nt" else 0
    xa = ap[0] if len(pairs) == 1 else jnp.concatenate([ap[i] for i, _ in pairs], axis=ka)
    xb = bp[0] if len(pairs) == 1 else jnp.concatenate([bp[j] for _, j in pairs], axis=kb)
    return lax.dot_general(xa, xb, _DIMS[mode], preferred_element_type=F32)


@functools.partial(jax.custom_vjp, nondiff_argnums=(2, 3, 4))
def _dot(a, b, mode, na, nb):
    return _dot_raw(a, b, mode, na, nb)


def _dot_fwd(a, b, mode, na, nb):
    return _dot_raw(a, b, mode, na, nb), (a, b)


def _dot_bwd(mode, na, nb, res, ct):
    a, b = res
    if mode == "nn":
        da = _dot_raw(ct, b, "nt", 1, 1)
        db = _dot_raw(a, ct, "tn", 1, 1)
    elif mode == "nt":
        da = _dot_raw(ct, b, "nn", 1, 1)
        db = _dot_raw(ct, a, "tn", 1, 1)
    else:
        da = _dot_raw(b, ct, "nt", 1, 1)
        db = _dot_raw(a, ct, "nn", 1, 1)
    return da.astype(a.dtype), db.astype(b.dtype)


_dot.defvjp(_dot_fwd, _dot_bwd)


N_EXACT = 3


@jax.custom_vjp
def _dot01(x, m_wide, m):
    return lax.dot_general(m_wide, jnp.concatenate(_parts(x, N_EXACT), axis=0), _DIMS["nn"], preferred_element_type=F32)


def _dot01_fwd(x, m_wide, m):
    return _dot01(x, m_wide, m), (m_wide, m)


def _dot01_bwd(res, ct):
    m_wide, m = res
    dx = lax.dot_general(m, ct.astype(BF16), _DIMS["tn"], preferred_element_type=F32)
    return dx, jnp.zeros_like(m_wide), jnp.zeros_like(m)


_dot01.defvjp(_dot01_fwd, _dot01_bwd)


def _thrice(m):
    return jnp.concatenate([m] * N_EXACT, axis=1).astype(BF16), m.astype(BF16)


def _iota2(shape, dim):
    return lax.broadcasted_iota(jnp.int32, shape, dim)


def _tril_f32(n):
    return (_iota2((n, n), 0) >= _iota2((n, n), 1)).astype(F32)


def _cumsum_rows(g):
    return _dot(_tril_f32(g.shape[0]), g, "nn", 1, 3)


def _below_block(n, b):
    ri, ci = _iota2((n, n), 0) // b, _iota2((n, n), 1) // b
    return (ri == ci + 1) & (ri % 2 == 1)


def _half_inverses(L):
    n = L.shape[0]
    eye = (_iota2((n, n), 0) == _iota2((n, n), 1)).astype(F32)
    d = eye - jnp.where(_below_block(n, 1), L, 0.0)
    b = 2
    while 2 * b < n:
        e = jnp.where(_below_block(n, b), L, 0.0)
        d = d - _dot_raw(d, _dot_raw(e, d, "nn", 2, 2), "nn", 2, 2)
        b *= 2
    return d, jnp.where(_below_block(n, b), L, 0.0)


def _solve_with(d, e, rhs):
    y = _dot_raw(d, rhs, "nn", 2, 2)
    return y - _dot_raw(d, _dot_raw(e, y, "nn", 2, 2), "nn", 2, 2)


@jax.custom_vjp
def _solve_unit_lower(L, rhs):
    return _solve_with(*_half_inverses(L), rhs)


def _solve_fwd(L, rhs):
    d, e = _half_inverses(L)
    sol = _solve_with(d, e, rhs)
    return sol, (d, e, sol)


def _solve_bwd(res, ct):
    d, e, sol = res
    y = _dot_raw(d, ct - _dot_raw(e, _dot_raw(d, ct, "tn", 2, 2), "tn", 2, 2), "tn", 2, 2)
    return -_dot_raw(y, sol, "nt", 2, 2), y


_solve_unit_lower.defvjp(_solve_fwd, _solve_bwd)


def _softplus(x):
    return jnp.maximum(x, 0.0) + jnp.log1p(jnp.exp(-jnp.abs(x)))


def _rms(x, w):
    return x * lax.rsqrt(jnp.mean(x * x, axis=-1, keepdims=True) + EPS) * w


HG_LEVELS = (32, 16, 8, 4, 2, 1)


def _hg_level_sums():
    i = np.arange(CHUNK)[:, None]
    m = np.arange(CHUNK)[None, :]
    to_row = [(m <= i) & (m // b == i // b) for b in HG_LEVELS]
    to_col = [(m > i) & (m // b == i // b) for b in HG_LEVELS]
    return _thrice(jnp.asarray(np.concatenate(to_row + to_col + [m <= i]), F32))


def _hg_level_masks():
    i = np.arange(CHUNK)[:, None]
    j = np.arange(CHUNK)[None, :]
    return jnp.asarray(np.stack([(i // b == j // b + 1) & ((i // b) % 2 == 1) for b in HG_LEVELS]), F32)


def _hg_pre(qraw, f, lb, sums):
    g = jnp.log(lb + (1.0 - lb) * jax.nn.sigmoid(f))
    k = (1.0 - lb) * jax.nn.sigmoid(-f)
    q = jax.nn.silu(qraw) * (HEAD_DIM ** -0.5)
    return q, k, _dot01(g, *sums)


def _hg_head(st, q, k, v, e, masks):
    nl = len(HG_LEVELS)
    eye = (_iota2((CHUNK, CHUNK), 0) == _iota2((CHUNK, CHUNK), 1)).astype(F32)
    a = eye * jnp.sum(q * k, axis=-1, keepdims=True)
    for l in range(nl):
        rows = q * jnp.exp(e[l * CHUNK:(l + 1) * CHUNK])
        cols = k * jnp.exp(e[(nl + l) * CHUNK:(nl + l + 1) * CHUNK])
        a = a + masks[l] * _dot(rows, cols, "nt", 1, 1)
    gc = e[2 * nl * CHUNK:(2 * nl + 1) * CHUNK]
    o = _dot(a, v, "nn", 1, 1) + _dot(q * jnp.exp(gc), st, "nt", 1, 1)
    g_last = gc[CHUNK - 1:CHUNK]
    st_new = st * jnp.exp(g_last) + _dot(v, k * jnp.exp(g_last - gc), "tn", 1, 1)
    return o, st_new


_HG_HEADS = jax.vmap(_hg_head, in_axes=(0, 0, 0, 0, 0, None))


def _hg_post(o, gate, gw):
    return _rms(o, gw) * jax.nn.silu(gate)


def _gd_conv(xp, cw):
    off = HALO - (CONV_K - 1)
    y = cw[0:1] * xp[off:off + CHUNK]
    for kk in range(1, CONV_K):
        y = y + cw[kk:kk + 1] * xp[off + kk:off + kk + CHUNK]
    return jax.nn.silu(y)


def _gd_gates(a, b, alog, dtb):
    beta = jax.nn.sigmoid(b)
    g = -jnp.exp(alog) * _softplus(a + dtb)
    expand = (_iota2((N_HEADS, D_MODEL), 1) // HEAD_DIM == _iota2((N_HEADS, D_MODEL), 0)).astype(F32)
    g_x = _dot(g, expand, "nn", 3, 1)
    after = (_iota2((CHUNK, D_MODEL), 0) > _iota2((CHUNK, D_MODEL), 1) % HEAD_DIM).astype(F32)
    sums = _dot01(jnp.concatenate([g_x, g_x * after], axis=1), *_thrice(_tril_f32(CHUNK)))
    return _dot(beta, expand, "nn", 3, 1), sums


def _gd_head(st, q, k, v, beta, gc, diff, gate, onw):
    q = q * lax.rsqrt(jnp.sum(q * q, axis=-1, keepdims=True) + EPS) * (HEAD_DIM ** -0.5)
    k = k * lax.rsqrt(jnp.sum(k * k, axis=-1, keepdims=True) + EPS)
    ri = _iota2((CHUNK, CHUNK), 0)
    ci = _iota2((CHUNK, CHUNK), 1)
    decay = jnp.exp(jnp.where(ri >= ci, diff[:, 0:CHUNK], -jnp.inf))
    kb = k * beta
    egc = jnp.exp(gc)
    L = jnp.where(ri > ci, _dot(kb, k, "nt", 1, 1) * decay, 0.0)
    sol = _solve_unit_lower(L, jnp.concatenate([v * beta, kb * egc], axis=1))
    u = sol[:, 0:HEAD_DIM]
    w = sol[:, HEAD_DIM:2 * HEAD_DIM]
    a_qk = jnp.where(ri >= ci, _dot(q, k, "nt", 1, 1) * decay, 0.0)
    g_last = gc[CHUNK - 1:CHUNK]
    v_new = u - _dot(w, st, "nt", 1, 1)
    o = _dot(q * egc, st, "nt", 1, 1) + _dot(a_qk, v_new, "nn", 1, 1)
    st_new = st * jnp.exp(g_last) + _dot(v_new, k * jnp.exp(g_last - gc), "tn", 1, 1)
    return _rms(o, onw) * jax.nn.silu(gate), st_new


def _params(*sem):
    return pltpu.CompilerParams(dimension_semantics=sem, vmem_limit_bytes=VMEM_LIMIT)


def _tile(n, pref):
    t = min(n, pref)
    assert n % t == 0, (n, pref)
    return t


def _mm_tiles(m, n, k, a_size, b_size, tile_sizes):
    tm, tn, tk = _tile(m, MM_TILE), _tile(n, MM_TILE), k

    def need(tm, tn, tk):
        acc = 4 * tm * tn * (2 if tk < k else 1)
        return 2 * (tm * tk * a_size + tk * tn * b_size + tm * tn * sum(tile_sizes)) + acc

    while need(tm, tn, tk) > MM_VMEM_BUDGET:
        if tk > 2048 or (tk > 512 and tm <= 512):
            tk //= 2
        else:
            tm //= 2
    return tm, tn, tk


def _mm(a, b, mode, out_dtypes, name, epilogue=None, extras=(), vectors=(), n_sums=0, after=None):
    if mode == "nn":
        (m, k), (k2, n) = a.shape, b.shape
    elif mode == "nt":
        (m, k), (n, k2) = a.shape, b.shape
    else:
        (k, m), (k2, n) = a.shape, b.shape
    assert k == k2, (a.shape, b.shape, mode)
    tm, tn, tk = _mm_tiles(m, n, k, a.dtype.itemsize, b.dtype.itemsize,
                           [e.dtype.itemsize for e in extras] + [jnp.dtype(dt).itemsize for dt in out_dtypes])
    nk = k // tk
    assert not (vectors or n_sums) or tn == n, "whole-row epilogues need the result tile to span the rows"
    ne, no, nafter = len(extras) + len(vectors), len(out_dtypes), int(after is not None)
    if epilogue is None:
        epilogue = lambda acc: (acc,)

    def body(*refs):
        a_ref, b_ref = refs[0], refs[1]
        ex = refs[2:2 + ne]
        outs = refs[2 + ne + nafter:2 + ne + nafter + no]
        sums = refs[2 + ne + nafter + no:2 + ne + nafter + no + n_sums]
        part = lax.dot_general(a_ref[...].astype(BF16), b_ref[...].astype(BF16), _DIMS[mode],
                               preferred_element_type=F32)

        def finish(acc):
            vals = epilogue(acc, *[e[...] for e in ex])
            for o_ref, val in zip(outs, vals[:no]):
                o_ref[...] = val.astype(o_ref.dtype)
            for s_ref, val in zip(sums, vals[no:]):
                @pl.when(pl.program_id(0) == 0)
                def _(s_ref=s_ref, val=val):
                    s_ref[...] = val

                @pl.when(pl.program_id(0) > 0)
                def _(s_ref=s_ref, val=val):
                    s_ref[...] += val

        if nk == 1:
            finish(part)
        else:
            acc_ref = refs[-1]
            kk = pl.program_id(2)

            @pl.when(kk == 0)
            def _():
                acc_ref[...] = part

            @pl.when(kk > 0)
            def _():
                acc_ref[...] += part

            @pl.when(kk == nk - 1)
            def _():
                finish(acc_ref[...])

    if mode == "tn":
        a_spec = pl.BlockSpec((tk, tm), lambda i, j, kk: (kk, i))
    else:
        a_spec = pl.BlockSpec((tm, tk), lambda i, j, kk: (i, kk))
    if mode == "nt":
        b_spec = pl.BlockSpec((tn, tk), lambda i, j, kk: (j, kk))
    else:
        b_spec = pl.BlockSpec((tk, tn), lambda i, j, kk: (kk, j))
    o_spec = pl.BlockSpec((tm, tn), lambda i, j, kk: (i, j))
    v_spec = pl.BlockSpec((1, tn), lambda i, j, kk: (0, j))
    res = pl.pallas_call(
        body,
        name=name,
        grid=(m // tm, n // tn, nk),
        in_specs=([a_spec, b_spec] + [o_spec] * len(extras) + [v_spec] * len(vectors)
                  + [pl.BlockSpec(memory_space=pl.ANY)] * nafter),
        out_specs=[o_spec] * no + [v_spec] * n_sums,
        out_shape=[jax.ShapeDtypeStruct((m, n), dt) for dt in out_dtypes] + [jax.ShapeDtypeStruct((1, n), F32)] * n_sums,
        scratch_shapes=[pltpu.VMEM((tm, tn), F32)] if nk > 1 else [],
        compiler_params=_params(*(("arbitrary",) * 3 if n_sums else ("parallel", "parallel", "arbitrary"))),
    )(a, b, *extras, *vectors, *([after] if nafter else []))
    return res[0] if no + n_sums == 1 else res


def _ep_residual_norm(acc, res, w):
    h = res + acc
    return h, _rms(h, w)


def _ep_norm_bwd(acc, x, dres, w):
    _, vjp = jax.vjp(_rms, x, w)
    dx, dw = vjp(acc)
    dx = dres + dx
    return dx, dx, dw


def _rms_fwd(x, w, name, tm=512):
    n, d = x.shape
    tm = _tile(n, tm)

    def body(x_ref, w_ref, y_ref):
        y_ref[...] = _rms(x_ref[...], w_ref[...]).astype(y_ref.dtype)

    return pl.pallas_call(
        body, name=name, grid=(n // tm,),
        in_specs=[pl.BlockSpec((tm, d), lambda i: (i, 0)), pl.BlockSpec((1, d), lambda i: (0, 0))],
        out_specs=pl.BlockSpec((tm, d), lambda i: (i, 0)),
        out_shape=jax.ShapeDtypeStruct((n, d), BF16),
        compiler_params=_params("arbitrary"),
    )(x, w)


def _loss_head(h, w, target, name, tm=512):
    n, d = h.shape
    tm = _tile(n, tm)

    def body(h_ref, w_ref, t_ref, dh_ref, dhb_ref, dw_ref, sq_ref):
        y, vjp = jax.vjp(_rms, h_ref[...], w_ref[...])
        err = y - t_ref[...]
        dh, dw = vjp(err * (1.0 / d))
        dh_ref[...] = dh
        dhb_ref[...] = dh.astype(dhb_ref.dtype)
        sq = jnp.sum(err * err, axis=0, keepdims=True)

        @pl.when(pl.program_id(0) == 0)
        def _():
            dw_ref[...] = dw
            sq_ref[...] = sq

        @pl.when(pl.program_id(0) > 0)
        def _():
            dw_ref[...] += dw
            sq_ref[...] += sq

        @pl.when(pl.program_id(0) == n // tm - 1)
        def _():
            total = jnp.sum(sq_ref[...], axis=1, keepdims=True) * (0.5 / d)
            sq_ref[...] = jnp.broadcast_to(total, sq_ref.shape)

    row = pl.BlockSpec((tm, d), lambda i: (i, 0))
    vec = pl.BlockSpec((1, d), lambda i: (0, 0))
    return pl.pallas_call(
        body, name=name, grid=(n // tm,),
        in_specs=[row, vec, row],
        out_specs=[row, row, vec, vec],
        out_shape=[jax.ShapeDtypeStruct((n, d), F32), jax.ShapeDtypeStruct((n, d), BF16),
                   jax.ShapeDtypeStruct((1, d), F32), jax.ShapeDtypeStruct((1, d), F32)],
        compiler_params=_params("arbitrary"),
    )(h, w, target)


def _lower_bounds(logits):
    sm = jax.nn.softmax(logits, axis=0)
    rows = [sm[0:1] * 0.0]
    for r in range(1, DEPTH):
        rows.append(rows[-1] + sm[r:r + 1])
    return jnp.concatenate(rows, axis=0)


def _lb_fwd(logits, name):
    def body(l_ref, o_ref):
        o_ref[...] = _lower_bounds(l_ref[...])

    return pl.pallas_call(body, name=name, out_shape=jax.ShapeDtypeStruct(logits.shape, F32))(logits)


def _lb_bwd(logits, dlb, name):
    def body(l_ref, d_ref, o_ref):
        _, vjp = jax.vjp(_lower_bounds, l_ref[...])
        (o_ref[...],) = vjp(d_ref[...])

    return pl.pallas_call(body, name=name, out_shape=jax.ShapeDtypeStruct(logits.shape, F32))(logits, dlb)


def _head_slice(h):
    if isinstance(h, int):
        return pl.ds(h * HEAD_DIM, HEAD_DIM)
    return pl.ds(pl.multiple_of(h * HEAD_DIM, HEAD_DIM), HEAD_DIM)


def _head_groups(group_body):
    if HEAD_GROUP == N_HEADS:
        group_body(list(range(N_HEADS)))
        return

    def trip(i, carry):
        group_body([i * HEAD_GROUP + t for t in range(HEAD_GROUP)])
        return carry

    lax.fori_loop(0, N_HEADS // HEAD_GROUP, trip, 0)


def _stack_heads(ref, hs, first=0):
    return jnp.stack([ref[:, _head_slice(h + first)] for h in hs])


def _unstack_heads(ref, hs, val, first=0):
    for t, h in enumerate(hs):
        ref[:, _head_slice(h + first)] = val[t].astype(ref.dtype)


_GD_HEADS = jax.vmap(_gd_head, in_axes=(0, 0, 0, 0, 0, 0, 0, 0, None))


def _hgrn_fwd(proj, lb, gw, seqs, name):
    n = proj.shape[0]
    nc = n // seqs // CHUNK
    d = D_MODEL

    sums, masks = _hg_level_sums(), _hg_level_masks()

    def body(p_ref, lb_ref, gw_ref, sums_wide_ref, sums_once_ref, masks_ref, o2_ref, o_ref, st_all_ref,
             st_sc, q_sc, k_sc, v_sc, e_sc):
        @pl.when(pl.program_id(1) == 0)
        def _():
            st_sc[...] = jnp.zeros_like(st_sc)

        sums_refs = (sums_wide_ref, sums_once_ref)
        _lane_blocks(d, functools.partial(_hg_pre_block, p_ref, lb_ref, sums_refs, q_sc, k_sc, v_sc, e_sc))
        st_all_ref[0] = st_sc[...]

        def group(hs):
            sts = pl.ds(hs[0], len(hs))
            o, st_new = _HG_HEADS(st_sc[sts], *[_stack_heads(r, hs) for r in (q_sc, k_sc, v_sc, e_sc)], masks_ref[...])
            _unstack_heads(o_ref, hs, o)
            st_sc[sts] = st_new

        _head_groups(group)

        def post(rows):
            gate = p_ref[rows, 3 * d:4 * d].astype(F32)
            o2_ref[rows, :] = _hg_post(o_ref[rows, :], gate, gw_ref[...]).astype(o2_ref.dtype)

        _row_blocks(CHUNK, post)

    idx = lambda b, c: (b * nc + c, 0)
    vec = pl.BlockSpec((1, d), lambda b, c: (0, 0))
    act = pl.BlockSpec((CHUNK, d), idx)
    return pl.pallas_call(
        body, name=name, grid=(seqs, nc),
        in_specs=[pl.BlockSpec((CHUNK, 4 * d), idx), vec, vec] + [pl.BlockSpec(s.shape, lambda b, c: (0, 0)) for s in sums]
        + [pl.BlockSpec(masks.shape, lambda b, c: (0, 0, 0))],
        out_specs=[act, act, pl.BlockSpec((1, N_HEADS, HEAD_DIM, HEAD_DIM), lambda b, c: (b * nc + c, 0, 0, 0))],
        out_shape=[jax.ShapeDtypeStruct((n, d), BF16), jax.ShapeDtypeStruct((n, d), F32),
                   jax.ShapeDtypeStruct((n // CHUNK, N_HEADS, HEAD_DIM, HEAD_DIM), F32)],
        scratch_shapes=[pltpu.VMEM((N_HEADS, HEAD_DIM, HEAD_DIM), F32)] + [pltpu.VMEM((CHUNK, d), F32)] * 3
        + [pltpu.VMEM((sums[0].shape[0], d), F32)],
        compiler_params=_params("arbitrary", "arbitrary"),
    )(proj, lb, gw, *sums, masks)


def _hgrn_bwd(proj, lb, gw, st_all, o, do2, seqs, name):
    n = proj.shape[0]
    nc = n // seqs // CHUNK
    d = D_MODEL

    sums, masks = _hg_level_sums(), _hg_level_masks()

    def body(p_ref, lb_ref, gw_ref, sums_wide_ref, sums_once_ref, masks_ref, st_all_ref, o_ref, do2_ref,
             dp_ref, dlb_ref, dgw_ref,
             dst_sc, q_sc, k_sc, v_sc, e_sc, do_sc, dq_sc, dk_sc, dv_sc, de_sc, dgw_sc):
        first = (pl.program_id(0) == 0) & (pl.program_id(1) == 0)

        @pl.when(pl.program_id(1) == 0)
        def _():
            dst_sc[...] = jnp.zeros_like(dst_sc)

        sums_refs = (sums_wide_ref, sums_once_ref)
        _lane_blocks(d, functools.partial(_hg_pre_block, p_ref, lb_ref, sums_refs, q_sc, k_sc, v_sc, e_sc))
        dgw_sc[...] = jnp.zeros_like(dgw_sc)

        def post_bwd(rows):
            _, vjp = jax.vjp(_hg_post, o_ref[rows, :], p_ref[rows, 3 * d:4 * d].astype(F32), gw_ref[...])
            do_sc[rows, :], dgate, dgw = vjp(do2_ref[rows, :].astype(F32))
            dp_ref[rows, 3 * d:4 * d] = dgate.astype(dp_ref.dtype)
            dgw_sc[...] += dgw

        _row_blocks(CHUNK, post_bwd)

        def group(hs):
            sts = pl.ds(hs[0], len(hs))
            level_masks = masks_ref[...]
            _, vjp = jax.vjp(lambda *a: _HG_HEADS(*a, level_masks), st_all_ref[0, sts],
                             *[_stack_heads(r, hs) for r in (q_sc, k_sc, v_sc, e_sc)])
            grads = vjp((_stack_heads(do_sc, hs), dst_sc[sts]))
            dst_sc[sts] = grads[0]
            for r, val in zip((dq_sc, dk_sc, dv_sc, de_sc), grads[1:]):
                _unstack_heads(r, hs, val)

        _head_groups(group)

        def pre_bwd(at):
            sl = at()
            level_sums = (sums_wide_ref[...], sums_once_ref[...])
            _, vjp = jax.vjp(lambda qraw, f, lb: _hg_pre(qraw, f, lb, level_sums), p_ref[:, sl].astype(F32),
                             p_ref[:, at(d)].astype(F32), lb_ref[:, sl])
            dqraw, df, dlb = vjp((dq_sc[:, sl], dk_sc[:, sl], de_sc[:, sl]))
            dp_ref[:, sl] = dqraw.astype(dp_ref.dtype)
            dp_ref[:, at(d)] = df.astype(dp_ref.dtype)
            dp_ref[:, at(2 * d)] = dv_sc[:, sl].astype(dp_ref.dtype)

            @pl.when(first)
            def _():
                dlb_ref[:, sl] = dlb

            @pl.when(jnp.logical_not(first))
            def _():
                dlb_ref[:, sl] += dlb

        _lane_blocks(d, pre_bwd)

        @pl.when(first)
        def _():
            dgw_ref[...] = dgw_sc[...]

        @pl.when(jnp.logical_not(first))
        def _():
            dgw_ref[...] += dgw_sc[...]

    idx = lambda b, c: (b * nc + nc - 1 - c, 0)
    vec = pl.BlockSpec((1, d), lambda b, c: (0, 0))
    act = pl.BlockSpec((CHUNK, d), idx)
    wide = pl.BlockSpec((CHUNK, 4 * d), idx)
    return pl.pallas_call(
        body, name=name, grid=(seqs, nc),
        in_specs=[wide, vec, vec] + [pl.BlockSpec(s.shape, lambda b, c: (0, 0)) for s in sums] + [
                  pl.BlockSpec(masks.shape, lambda b, c: (0, 0, 0)),
                  pl.BlockSpec((1, N_HEADS, HEAD_DIM, HEAD_DIM), lambda b, c: (b * nc + nc - 1 - c, 0, 0, 0)),
                  act, act],
        out_specs=[wide, vec, vec],
        out_shape=[jax.ShapeDtypeStruct((n, 4 * d), BF16), jax.ShapeDtypeStruct((1, d), F32),
                   jax.ShapeDtypeStruct((1, d), F32)],
        scratch_shapes=[pltpu.VMEM((N_HEADS, HEAD_DIM, HEAD_DIM), F32)]
        + [pltpu.VMEM((CHUNK, d), F32)] * 3 + [pltpu.VMEM((sums[0].shape[0], d), F32)]
        + [pltpu.VMEM((CHUNK, d), F32)] * 4 + [pltpu.VMEM((sums[0].shape[0], d), F32), pltpu.VMEM((1, d), F32)],
        compiler_params=_params("arbitrary", "arbitrary"),
    )(proj, lb, gw, *sums, masks, st_all, o, do2)


def _lane_blocks(width, block_body):
    def trip(j, carry):
        block_body(lambda base=0: pl.ds(pl.multiple_of(j * LANE_BLOCK + base, LANE_BLOCK), LANE_BLOCK))
        return carry

    lax.fori_loop(0, width // LANE_BLOCK, trip, 0)


def _row_blocks(rows, block_body):
    def trip(j, carry):
        block_body(pl.ds(pl.multiple_of(j * ROW_BLOCK, ROW_BLOCK), ROW_BLOCK))
        return carry

    lax.fori_loop(0, rows // ROW_BLOCK, trip, 0)


def _hg_pre_block(p_ref, lb_ref, sums_refs, q_sc, k_sc, v_sc, e_sc, at):
    sl = at()
    q_sc[:, sl], k_sc[:, sl], e_sc[:, sl] = _hg_pre(
        p_ref[:, sl].astype(F32), p_ref[:, at(D_MODEL)].astype(F32), lb_ref[:, sl], [r[...] for r in sums_refs])
    v_sc[:, sl] = p_ref[:, at(2 * D_MODEL)].astype(F32)


def _gd_xp(halo_ref, p_ref, sl, first_chunk):
    halo = jnp.where(first_chunk, 0.0, halo_ref[:, sl].astype(F32))
    return jnp.concatenate([halo, p_ref[:, sl].astype(F32)], axis=0)


def _gdn_fwd(projm, projab, cw, alog, dtb, onw, seqs, name):
    n = projm.shape[0]
    nc = n // seqs // CHUNK
    d = D_MODEL
    per_halo = CHUNK // HALO

    def body(p_ref, halo_ref, ab_ref, cw_ref, alog_ref, dtb_ref, onw_ref, o2_ref, st_all_ref,
             st_sc, c_sc, beta_sc, g_sc):
        @pl.when(pl.program_id(1) == 0)
        def _():
            st_sc[...] = jnp.zeros_like(st_sc)

        def conv(at):
            sl = at()
            c_sc[:, sl] = _gd_conv(_gd_xp(halo_ref, p_ref, sl, pl.program_id(1) == 0), cw_ref[:, sl])

        _lane_blocks(3 * d, conv)
        beta_sc[...], g_sc[...] = _gd_gates(ab_ref[:, 0:N_HEADS], ab_ref[:, N_HEADS:2 * N_HEADS], alog_ref[...],
                                            dtb_ref[...])
        st_all_ref[0] = st_sc[...]

        def group(hs):
            sts = pl.ds(hs[0], len(hs))
            o2, st_new = _GD_HEADS(
                st_sc[sts], _stack_heads(c_sc, hs), _stack_heads(c_sc, hs, N_HEADS), _stack_heads(c_sc, hs, 2 * N_HEADS),
                _stack_heads(beta_sc, hs), _stack_heads(g_sc, hs), _stack_heads(g_sc, hs, N_HEADS),
                _stack_heads(p_ref, hs, 3 * N_HEADS).astype(F32), onw_ref[...])
            _unstack_heads(o2_ref, hs, o2)
            st_sc[sts] = st_new

        _head_groups(group)

    idx = lambda b, c: (b * nc + c, 0)
    const = lambda b, c: (0, 0)
    return pl.pallas_call(
        body, name=name, grid=(seqs, nc),
        in_specs=[pl.BlockSpec((CHUNK, 4 * d), idx),
                  pl.BlockSpec((HALO, 3 * d), lambda b, c: (jnp.maximum((b * nc + c) * per_halo - 1, 0), 0)),
                  pl.BlockSpec((CHUNK, AB_PAD), idx),
                  pl.BlockSpec((CONV_K, 3 * d), const), pl.BlockSpec((1, N_HEADS), const),
                  pl.BlockSpec((1, N_HEADS), const), pl.BlockSpec((1, HEAD_DIM), const)],
        out_specs=[pl.BlockSpec((CHUNK, d), idx),
                   pl.BlockSpec((1, N_HEADS, HEAD_DIM, HEAD_DIM), lambda b, c: (b * nc + c, 0, 0, 0))],
        out_shape=[jax.ShapeDtypeStruct((n, d), BF16),
                   jax.ShapeDtypeStruct((n // CHUNK, N_HEADS, HEAD_DIM, HEAD_DIM), F32)],
        scratch_shapes=[pltpu.VMEM((N_HEADS, HEAD_DIM, HEAD_DIM), F32), pltpu.VMEM((CHUNK, 3 * d), F32),
                        pltpu.VMEM((CHUNK, d), F32), pltpu.VMEM((CHUNK, 2 * d), F32)],
        compiler_params=_params("arbitrary", "arbitrary"),
    )(projm, projm, projab, cw, alog, dtb, onw)


def _gdn_bwd(projm, projab, cw, alog, dtb, onw, st_all, do2, seqs, name):
    n = projm.shape[0]
    nc = n // seqs // CHUNK
    d = D_MODEL
    per_halo = CHUNK // HALO

    def body(p_ref, halo_ref, ab_ref, cw_ref, alog_ref, dtb_ref, onw_ref, st_all_ref, do2_ref,
             dp_ref, dab_ref, dcw_ref, dalog_ref, ddtb_ref, donw_ref,
             dst_sc, dhalo_sc, c_sc, beta_sc, g_sc, dc_sc, dbeta_sc, dg_sc, donw_sc):
        step = pl.program_id(1)
        first = (pl.program_id(0) == 0) & (step == 0)

        @pl.when(step == 0)
        def _():
            dst_sc[...] = jnp.zeros_like(dst_sc)
            dhalo_sc[...] = jnp.zeros_like(dhalo_sc)

        donw_sc[...] = jnp.zeros_like(donw_sc)

        def conv(at):
            sl = at()
            c_sc[:, sl] = _gd_conv(_gd_xp(halo_ref, p_ref, sl, step == nc - 1), cw_ref[:, sl])

        _lane_blocks(3 * d, conv)
        (beta_sc[...], g_sc[...]), gates_vjp = jax.vjp(
            _gd_gates, ab_ref[:, 0:N_HEADS], ab_ref[:, N_HEADS:2 * N_HEADS], alog_ref[...], dtb_ref[...])

        def group(hs):
            sts = pl.ds(hs[0], len(hs))
            _, vjp = jax.vjp(
                _GD_HEADS, st_all_ref[0, sts], _stack_heads(c_sc, hs), _stack_heads(c_sc, hs, N_HEADS),
                _stack_heads(c_sc, hs, 2 * N_HEADS), _stack_heads(beta_sc, hs), _stack_heads(g_sc, hs),
                _stack_heads(g_sc, hs, N_HEADS), _stack_heads(p_ref, hs, 3 * N_HEADS).astype(F32), onw_ref[...])
            dst, dq, dk, dv, dbeta, dg, ddiff, dgate, donw = vjp(
                (_stack_heads(do2_ref, hs).astype(F32), dst_sc[sts]))
            _unstack_heads(dg_sc, hs, ddiff, N_HEADS)
            dst_sc[sts] = dst
            _unstack_heads(dc_sc, hs, dq)
            _unstack_heads(dc_sc, hs, dk, N_HEADS)
            _unstack_heads(dc_sc, hs, dv, 2 * N_HEADS)
            _unstack_heads(dbeta_sc, hs, dbeta)
            _unstack_heads(dg_sc, hs, dg)
            _unstack_heads(dp_ref, hs, dgate, 3 * N_HEADS)
            donw_sc[...] += donw

        _head_groups(group)
        def conv_bwd(at):
            sl = at()
            _, vjp = jax.vjp(_gd_conv, _gd_xp(halo_ref, p_ref, sl, step == nc - 1), cw_ref[:, sl])
            dxp, dcw = vjp(dc_sc[:, sl])
            dqkv = jnp.concatenate([dxp[HALO:CHUNK], dxp[CHUNK:HALO + CHUNK] + dhalo_sc[:, sl]], axis=0)
            dp_ref[:, sl] = dqkv.astype(dp_ref.dtype)
            dhalo_sc[:, sl] = dxp[0:HALO]

            @pl.when(first)
            def _():
                dcw_ref[:, sl] = dcw

            @pl.when(jnp.logical_not(first))
            def _():
                dcw_ref[:, sl] += dcw

        _lane_blocks(3 * d, conv_bwd)
        da, db, dalog, ddtb = gates_vjp((dbeta_sc[...], dg_sc[...]))
        dab_ref[...] = jnp.concatenate(
            [da, db, jnp.zeros((CHUNK, AB_PAD - 2 * N_HEADS), F32)], axis=1).astype(dab_ref.dtype)

        @pl.when(first)
        def _():
            dalog_ref[...] = dalog
            ddtb_ref[...] = ddtb
            donw_ref[...] = donw_sc[...]

        @pl.when(jnp.logical_not(first))
        def _():
            dalog_ref[...] += dalog
            ddtb_ref[...] += ddtb
            donw_ref[...] += donw_sc[...]

    rev = lambda b, c: b * nc + nc - 1 - c
    idx = lambda b, c: (rev(b, c), 0)
    const = lambda b, c: (0, 0)
    small = [pl.BlockSpec((CONV_K, 3 * d), const), pl.BlockSpec((1, N_HEADS), const),
             pl.BlockSpec((1, N_HEADS), const), pl.BlockSpec((1, HEAD_DIM), const)]
    return pl.pallas_call(
        body, name=name, grid=(seqs, nc),
        in_specs=[pl.BlockSpec((CHUNK, 4 * d), idx),
                  pl.BlockSpec((HALO, 3 * d), lambda b, c: (jnp.maximum(rev(b, c) * per_halo - 1, 0), 0)),
                  pl.BlockSpec((CHUNK, AB_PAD), idx)] + small + [
                  pl.BlockSpec((1, N_HEADS, HEAD_DIM, HEAD_DIM), lambda b, c: (rev(b, c), 0, 0, 0)),
                  pl.BlockSpec((CHUNK, d), idx)],
        out_specs=[pl.BlockSpec((CHUNK, 4 * d), idx), pl.BlockSpec((CHUNK, AB_PAD), idx)] + small,
        out_shape=[jax.ShapeDtypeStruct((n, 4 * d), BF16), jax.ShapeDtypeStruct((n, AB_PAD), BF16),
                   jax.ShapeDtypeStruct((CONV_K, 3 * d), F32), jax.ShapeDtypeStruct((1, N_HEADS), F32),
                   jax.ShapeDtypeStruct((1, N_HEADS), F32), jax.ShapeDtypeStruct((1, HEAD_DIM), F32)],
        scratch_shapes=[pltpu.VMEM((N_HEADS, HEAD_DIM, HEAD_DIM), F32), pltpu.VMEM((HALO, 3 * d), F32),
                        pltpu.VMEM((CHUNK, 3 * d), F32), pltpu.VMEM((CHUNK, d), F32), pltpu.VMEM((CHUNK, 2 * d), F32),
                        pltpu.VMEM((CHUNK, 3 * d), F32), pltpu.VMEM((CHUNK, d), F32), pltpu.VMEM((CHUNK, 2 * d), F32),
                        pltpu.VMEM((1, HEAD_DIM), F32)],
        compiler_params=_params("arbitrary", "arbitrary"),
    )(projm, projm, projab, cw, alog, dtb, onw, st_all, do2)


def _adam_update(w, g, m, v):
    b1c = 1.0 - ADAM_B1 ** ADAM_STEP
    b2c = 1.0 - ADAM_B2 ** ADAM_STEP
    m_new = ADAM_B1 * m + (1.0 - ADAM_B1) * g
    v_new = ADAM_B2 * v + (1.0 - ADAM_B2) * (g * g)
    delta = -ADAM_LR * ((m_new / b1c) / (jnp.sqrt(v_new / b2c) + ADAM_EPS) + ADAM_WD * w)
    return delta, m_new, v_new


def _adamw(w, g, m, v, name, tr=256):
    r, c = w.shape
    tr = _tile(r, tr)

    def body(w_ref, g_ref, m_ref, v_ref, d_ref, mo_ref, vo_ref):
        d_ref[...], mo_ref[...], vo_ref[...] = _adam_update(w_ref[...], g_ref[...], m_ref[...], v_ref[...])

    blk = pl.BlockSpec((tr, c), lambda i: (i, 0))
    return pl.pallas_call(
        body, name=name, grid=(r // tr,),
        in_specs=[blk] * 4, out_specs=[blk] * 3,
        out_shape=[jax.ShapeDtypeStruct((r, c), F32)] * 3,
        compiler_params=_params("arbitrary"),
    )(w, g, m, v)


def _adamw_slots(w, slot_bufs, m, v, name, tr=256):
    nl, r, c = w.shape
    tr = _tile(r, tr)

    def body(*refs):
        w_ref = refs[0]
        g_refs = refs[1:1 + nl]
        m_ref, v_ref, go_ref, d_ref, mo_ref, vo_ref = refs[1 + nl:]
        for k in range(nl):
            @pl.when(pl.program_id(0) == k)
            def _(k=k):
                g = g_refs[k][0].astype(F32)
                for s in range(1, N_DEV):
                    g = g + g_refs[k][s].astype(F32)
                go_ref[0] = g

        d_ref[0], mo_ref[0], vo_ref[0] = _adam_update(w_ref[0], go_ref[0], m_ref[0], v_ref[0])

    blk = pl.BlockSpec((1, tr, c), lambda l, i: (l, i, 0))
    g_specs = [pl.BlockSpec((N_DEV, tr, c), lambda l, i, k=k: (0, jnp.where(l == k, i, 0), 0)) for k in range(nl)]
    return pl.pallas_call(
        body, name=name, grid=(nl, r // tr),
        in_specs=[blk] + g_specs + [blk, blk], out_specs=[blk] * 4,
        out_shape=[jax.ShapeDtypeStruct((nl, r, c), F32)] * 4,
        compiler_params=_params("arbitrary", "arbitrary"),
    )(w, *slot_bufs, m, v)


def _mesh_pos():
    return lax.axis_index("x"), lax.axis_index("y"), lax.axis_index("c")


def _flip(pos, p):
    x, y, c = pos
    return ((1 - x) if p & 4 else x, (1 - y) if p & 2 else y, (1 - c) if p & 1 else c)


def _lin(pos):
    return 4 * pos[0] + 2 * pos[1] + pos[2]


_HBM = pl.BlockSpec(memory_space=pltpu.HBM)
_SEM = pl.BlockSpec(memory_space=pltpu.SEMAPHORE)
_DATAFLOW = pltpu.SideEffectType.DATAFLOW_SIDE_EFFECTING


class _Item:
    def __init__(self, src, land_shape, src_pick, dst_pick):
        self.src, self.land_shape, self.src_pick, self.dst_pick = src, land_shape, src_pick, dst_pick


def _remote_copies(items, src, land, send_sem, recv_sem, me, arriving):
    me_i = _lin(me)
    out = []
    for it, s_ref, l_ref in zip(items, src, land):
        for p in range(1, N_DEV):
            peer = _flip(me, p)
            out.append(pltpu.make_async_remote_copy(
                src_ref=it.src_pick(s_ref, _lin(peer)),
                dst_ref=it.dst_pick(l_ref, _lin(peer) if arriving else me_i),
                send_sem=send_sem, recv_sem=recv_sem, device_id=peer, device_id_type=pl.DeviceIdType.MESH))
    return out


def _own_copies(items, src, land, sem, me):
    me_i = _lin(me)
    return [pltpu.make_async_copy(it.src_pick(s_ref, me_i), it.dst_pick(l_ref, me_i), sem)
            for it, s_ref, l_ref in zip(items, src, land)]


def _exchange_start(groups, name):
    items = [it for g in groups for it in g]
    n, ng = len(items), len(groups)
    first = [sum(len(g) for g in groups[:gi]) for gi in range(ng)]

    def body(*refs):
        src, land = refs[0:n], refs[n:2 * n]
        send_sems, recv_sems = refs[2 * n:2 * n + ng], refs[2 * n + ng:2 * n + 2 * ng]
        token = refs[4 * n + 2 * ng]
        me = _mesh_pos()
        for gi, g in enumerate(groups):
            sl = slice(first[gi], first[gi] + len(g))
            for cp in _remote_copies(g, src[sl], land[sl], send_sems[gi], recv_sems[gi], me, arriving=False):
                cp.start()
            for cp in _own_copies(g, src[sl], land[sl], recv_sems[gi], me):
                cp.start()
        token[...] = jnp.zeros_like(token)

    srcs = [pltpu.with_memory_space_constraint(it.src, pltpu.HBM) for it in items]
    lands = [pltpu.with_memory_space_constraint(lax.empty(it.land_shape, it.src.dtype), pltpu.HBM) for it in items]
    res = pl.pallas_call(
        body, name=name,
        out_shape=([pltpu.SemaphoreType.DMA(())] * (2 * ng)
                   + [pltpu.HBM(it.src.shape, it.src.dtype) for it in items]
                   + [pltpu.HBM(it.land_shape, it.src.dtype) for it in items]
                   + [jax.ShapeDtypeStruct((8, 128), F32)]),
        in_specs=[_HBM] * (2 * n),
        out_specs=[_SEM] * (2 * ng) + [_HBM] * (2 * n) + [pl.BlockSpec(memory_space=pltpu.VMEM)],
        input_output_aliases={i: 2 * ng + i for i in range(2 * n)},
        compiler_params=pltpu.CompilerParams(has_side_effects=_DATAFLOW),
    )(*srcs, *lands)
    send_sems, recv_sems = res[0:ng], res[ng:2 * ng]
    src_thru, land_thru = res[2 * ng:2 * ng + n], res[2 * ng + n:2 * ng + 2 * n]
    handles = []
    for gi, g in enumerate(groups):
        sl = slice(first[gi], first[gi] + len(g))
        handles.append((g, src_thru[sl], land_thru[sl], send_sems[gi], recv_sems[gi]))
    return handles, res[-1]


def _exchange_wait(handle, after, name):
    items, src_thru, land_thru, send_sem, recv_sem = handle
    k = len(items)

    def body(*refs):
        src, land = refs[0:k], refs[k:2 * k]
        send_ref, recv_ref = refs[2 * k], refs[2 * k + 1]
        for cp in _remote_copies(items, src, land, send_ref, recv_ref, _mesh_pos(), arriving=True):
            cp.wait_send()
            cp.wait_recv()
        for cp in _own_copies(items, src, land, recv_ref, _mesh_pos()):
            cp.wait()

    res = pl.pallas_call(
        body, name=name,
        out_shape=([pltpu.HBM(s.shape, s.dtype) for s in src_thru] + [pltpu.HBM(l.shape, l.dtype) for l in land_thru]),
        in_specs=[_HBM] * (2 * k) + [_SEM, _SEM, pl.BlockSpec(memory_space=pl.ANY)],
        out_specs=[_HBM] * (2 * k),
        input_output_aliases={i: i for i in range(2 * k)},
        compiler_params=pltpu.CompilerParams(has_side_effects=_DATAFLOW),
    )(*src_thru, *land_thru, send_sem, recv_sem, after)
    return res[k:2 * k]


def _whole(ref, i):
    return ref


def _slot(ref, i):
    return ref.at[i]


def _rows_of(r):
    return lambda ref, i: ref.at[pl.ds(pl.multiple_of(i * r, r), r), :]


def _cols_of(c):
    return lambda ref, i: ref.at[:, pl.ds(pl.multiple_of(i * c, c), c)]


def _all_reduce_small(buf, name):
    r, c = buf.shape

    def body(src_ref, out_ref, all_ref, send_sems, recv_sems):
        me = _mesh_pos()
        me_i = _lin(me)
        all_ref[me_i] = src_ref[...]
        for p in range(1, N_DEV):
            peer = _flip(me, p)
            pltpu.make_async_remote_copy(
                src_ref=src_ref, dst_ref=all_ref.at[me_i], send_sem=send_sems.at[p - 1], recv_sem=recv_sems.at[p - 1],
                device_id=peer, device_id_type=pl.DeviceIdType.MESH).start()
        for p in range(1, N_DEV):
            peer = _flip(me, p)
            cp = pltpu.make_async_remote_copy(
                src_ref=src_ref, dst_ref=all_ref.at[_lin(peer)], send_sem=send_sems.at[p - 1],
                recv_sem=recv_sems.at[p - 1], device_id=peer, device_id_type=pl.DeviceIdType.MESH)
            cp.wait_recv()
            cp.wait_send()
        acc = all_ref[0]
        for s in range(1, N_DEV):
            acc = acc + all_ref[s]
        out_ref[...] = acc

    vm = pl.BlockSpec(memory_space=pltpu.VMEM)
    return pl.pallas_call(
        body, name=name, in_specs=[vm], out_specs=vm,
        out_shape=jax.ShapeDtypeStruct((r, c), F32),
        scratch_shapes=[pltpu.VMEM((N_DEV, r, c), F32), pltpu.SemaphoreType.DMA((N_DEV - 1,)),
                        pltpu.SemaphoreType.DMA((N_DEV - 1,))],
        compiler_params=pltpu.CompilerParams(has_side_effects=True),
    )(buf)


def _unshard_cols(g):
    s, l, r, c = g.shape
    return jnp.transpose(g, (1, 2, 0, 3)).reshape(l, r, s * c)


def kernel(x, gdn_w_in, gdn_conv, gdn_a_log, gdn_dt_bias, gdn_onorm, gdn_w_out, hgrn_w_in, hgrn_lb_logits, hgrn_gnorm, hgrn_w_out, norm_mix, norm_mlp, mlp_w_up, mlp_w_down, norm_final, loss_target, m_gdn_w_in, m_gdn_conv, m_gdn_a_log, m_gdn_dt_bias, m_gdn_onorm, m_gdn_w_out, m_hgrn_w_in, m_hgrn_lb_logits, m_hgrn_gnorm, m_hgrn_w_out, m_norm_mix, m_norm_mlp, m_mlp_w_up, m_mlp_w_down, m_norm_final, v_gdn_w_in, v_gdn_conv, v_gdn_a_log, v_gdn_dt_bias, v_gdn_onorm, v_gdn_w_out, v_hgrn_w_in, v_hgrn_lb_logits, v_hgrn_gnorm, v_hgrn_w_out, v_norm_mix, v_norm_mlp, v_mlp_w_up, v_mlp_w_down, v_norm_final):
    seqs, seq_len, d = x.shape
    n = seqs * seq_len
    me_i = _lin(_mesh_pos())
    x2 = x.reshape(n, d)
    target = loss_target.reshape(n, d)
    n_gdn, n_hgrn = gdn_w_in.shape[0], hgrn_w_in.shape[0]

    r_out, r_down = gdn_w_out.shape[1], mlp_w_down.shape[1]
    c_gin, c_hin, c_up = gdn_w_in.shape[2], hgrn_w_in.shape[2], mlp_w_up.shape[2]

    def gathered(w, pick, land_shape):
        return _Item(w.astype(BF16), land_shape, _whole, pick)

    groups = [[_Item(gdn_conv, (N_DEV,) + gdn_conv.shape, _whole, _slot),
               _Item(hgrn_gnorm, (N_DEV,) + hgrn_gnorm.shape, _whole, _slot)]]
    for i in range(DEPTH):
        j = i // 2
        if i % 2 == 0:
            groups += [[gathered(gdn_w_in[j], _slot, (N_DEV, d, c_gin))],
                       [gathered(gdn_w_out[j], _rows_of(r_out), (N_DEV * r_out, d))]]
        else:
            groups += [[gathered(hgrn_w_in[j], _cols_of(c_hin), (d, N_DEV * c_hin))],
                       [gathered(hgrn_w_out[j], _rows_of(r_out), (N_DEV * r_out, d))]]
        groups += [[gathered(mlp_w_up[i], _cols_of(c_up), (d, N_DEV * c_up))],
                   [gathered(mlp_w_down[i], _rows_of(r_down), (N_DEV * r_down, d))]]
    gather_handles, token = _exchange_start(groups, "gather_start")
    lbs = _lb_fwd(hgrn_lb_logits + token[0:1, 0:1], "lb_fwd")

    def arrived(k, after, name):
        return _exchange_wait(gather_handles[k], after, "gather_wait_" + name)

    saved = []
    w_in, w_ab, w_out, w_up, w_down = ([None] * DEPTH for _ in range(5))
    h = x2
    for i in range(DEPTH):
        j = i // 2
        if i == 0:
            g_conv, g_gnorm = arrived(0, h, "small")
            conv_full = _unshard_cols(g_conv)
            gnorm_full = jnp.transpose(g_gnorm, (1, 0, 2)).reshape(n_hgrn, d)
        if i == 0:
            y = _rms_fwd(h, norm_mix[0:1], "rms_mix_0")
        (w_in[i],) = arrived(1 + 4 * i, y, f"in_{i}")
        if i % 2 == 0:
            w_gin = jnp.transpose(w_in[i], (1, 0, 2)).reshape(d, N_DEV * c_gin)
            w_in[i] = w_gin[:, :GDN_MAIN]
            w_ab[i] = jnp.pad(w_gin[:, GDN_MAIN:], ((0, 0), (0, AB_PAD - 2 * N_HEADS)))
            projm = _mm(y, w_in[i], "nn", [BF16], f"gdn_proj_{i}")
            projab = _mm(y, w_ab[i], "nn", [F32], f"gdn_proj_ab_{i}")
            o2, st_all = _gdn_fwd(projm, projab, conv_full[j], gdn_a_log[j:j + 1], gdn_dt_bias[j:j + 1],
                                  gdn_onorm[j:j + 1], seqs, f"gdn_fwd_{i}")
            mix = (projm, projab, st_all)
        else:
            proj = _mm(y, w_in[i], "nn", [BF16], f"hgrn_proj_{i}")
            o2, o_raw, st_all = _hgrn_fwd(proj, lbs[i:i + 1], gnorm_full[j:j + 1], seqs, f"hgrn_fwd_{i}")
            mix = (proj, o_raw, st_all)
        (w_out[i],) = arrived(2 + 4 * i, o2, f"out_{i}")
        h1, y2 = _mm(o2, w_out[i], "nn", [F32, BF16], f"mix_out_{i}", epilogue=_ep_residual_norm, extras=(h,),
                     vectors=(norm_mlp[i:i + 1],))
        (w_up[i],) = arrived(3 + 4 * i, y2, f"up_{i}")
        u, act = _mm(y2, w_up[i], "nn", [BF16, BF16], f"mlp_up_{i}",
                     epilogue=lambda acc: (acc, jnp.square(jnp.maximum(acc, 0.0))))
        (w_down[i],) = arrived(4 + 4 * i, act, f"down_{i}")
        saved.append((h, y, mix, o2, h1, y2, u, act))
        if i + 1 < DEPTH:
            h, y = _mm(act, w_down[i], "nn", [F32, BF16], f"mlp_down_{i}", epilogue=_ep_residual_norm, extras=(h1,),
                       vectors=(norm_mix[i + 1:i + 2],))
        else:
            h = _mm(act, w_down[i], "nn", [F32], f"mlp_down_{i}", epilogue=lambda acc, res: (res + acc,),
                    extras=(h1,))

    dh, dh_b, d_nf, sq = _loss_head(h, norm_final.reshape(1, d), target, "loss_head")

    d_nmix, d_nmlp = [None] * DEPTH, [None] * DEPTH
    d_conv, d_alog, d_dtb, d_onorm = [None] * n_gdn, [None] * n_gdn, [None] * n_gdn, [None] * n_gdn
    d_lb = [jnp.zeros((1, d), F32)] * DEPTH
    d_gnorm = [None] * n_hgrn
    mlp_handles, mix_handles = [None] * DEPTH, [None] * DEPTH
    token = None
    for i in reversed(range(DEPTH)):
        j = i // 2
        h_in, y, mix, o2, h1, y2, u, act = saved[i]
        g_down = _mm(act, dh_b, "tn", [BF16], f"g_down_{i}", after=token)
        du = _mm(dh_b, w_down[i], "nt", [BF16], f"d_u_{i}",
                 epilogue=lambda acc, uu: (acc * (2.0 * jnp.maximum(uu.astype(F32), 0.0)),), extras=(u,))
        g_up = _mm(y2, du, "tn", [BF16], f"g_up_{i}")
        mlp_handles[i], token = _exchange_start(
            [[_Item(g_down, (N_DEV, r_down, d), _rows_of(r_down), _slot)],
             [_Item(g_up, (N_DEV, d, c_up), _cols_of(c_up), _slot)]], f"scatter_start_mlp_{i}")
        dh1, dh1_b, d_nmlp[i] = _mm(du, w_up[i], "nt", [F32, BF16], f"d_y2_{i}", epilogue=_ep_norm_bwd,
                                     extras=(h1, dh), vectors=(norm_mlp[i:i + 1],), n_sums=1, after=token)
        g_out = _mm(o2, dh1_b, "tn", [BF16], f"g_out_{i}")
        do2 = _mm(dh1_b, w_out[i], "nt", [BF16], f"d_o2_{i}")
        if i % 2 == 0:
            projm, projab, st_all = mix
            dpm, dpab, d_conv[j], d_alog[j], d_dtb[j], d_onorm[j] = _gdn_bwd(
                projm, projab, conv_full[j], gdn_a_log[j:j + 1], gdn_dt_bias[j:j + 1], gdn_onorm[j:j + 1],
                st_all, do2, seqs, f"gdn_bwd_{i}")
            g_main = _mm(y, dpm, "tn", [BF16], f"g_in_{i}")
            g_ab = _mm(y, dpab, "tn", [BF16], f"g_in_ab_{i}")
            g_in = jnp.concatenate([g_main, g_ab[:, :2 * N_HEADS]], axis=1)
            g_in = jnp.transpose(g_in.reshape(d, N_DEV, c_gin), (1, 0, 2))
            in_item = _Item(g_in, (N_DEV, d, c_gin), _slot, _slot)
            dy_ab = _mm(dpab, w_ab[i], "nt", [F32], f"d_y_ab_{i}")
            dp, dy_extras = dpm, (dy_ab, h_in, dh1)
            dy_epilogue = lambda acc, e, xx, dres, w: _ep_norm_bwd(acc + e, xx, dres, w)
        else:
            proj, o_raw, st_all = mix
            dp, d_lb[i], d_gnorm[j] = _hgrn_bwd(proj, lbs[i:i + 1], gnorm_full[j:j + 1], st_all, o_raw, do2,
                                               seqs, f"hgrn_bwd_{i}")
            g_in = _mm(y, dp, "tn", [BF16], f"g_in_{i}")
            in_item = _Item(g_in, (N_DEV, d, c_hin), _cols_of(c_hin), _slot)
            dy_extras, dy_epilogue = (h_in, dh1), _ep_norm_bwd
        mix_handles[i], token = _exchange_start(
            [[_Item(g_out, (N_DEV, r_out, d), _rows_of(r_out), _slot)], [in_item]], f"scatter_start_mix_{i}")
        dh, dh_b, d_nmix[i] = _mm(dp, w_in[i], "nt", [F32, BF16], f"d_y_{i}", epilogue=dy_epilogue, extras=dy_extras,
                                  vectors=(norm_mix[i:i + 1],), n_sums=1, after=token)
        token = None
    grad_x = dh.reshape(x.shape)

    def landed(handles, k, layers, after, name):
        return [_exchange_wait(handles[i][k], after, f"scatter_wait_{name}_{i}")[0] for i in layers]

    every, even, odd = range(DEPTH), range(0, DEPTH, 2), range(1, DEPTH, 2)
    upd = {}
    upd["mlp_w_down"] = _adamw_slots(mlp_w_down, landed(mlp_handles, 0, every, dh, "down"), m_mlp_w_down,
                                     v_mlp_w_down, "adamw_mlp_w_down")
    upd["mlp_w_up"] = _adamw_slots(mlp_w_up, landed(mlp_handles, 1, every, upd["mlp_w_down"][1], "up"), m_mlp_w_up,
                                   v_mlp_w_up, "adamw_mlp_w_up")
    upd["hgrn_w_out"] = _adamw_slots(hgrn_w_out, landed(mix_handles, 0, odd, upd["mlp_w_up"][1], "out"),
                                     m_hgrn_w_out, v_hgrn_w_out, "adamw_hgrn_w_out")
    upd["hgrn_w_in"] = _adamw_slots(hgrn_w_in, landed(mix_handles, 1, odd, upd["hgrn_w_out"][1], "in"), m_hgrn_w_in,
                                    v_hgrn_w_in, "adamw_hgrn_w_in")
    upd["gdn_w_out"] = _adamw_slots(gdn_w_out, landed(mix_handles, 0, even, upd["hgrn_w_in"][1], "out"),
                                    m_gdn_w_out, v_gdn_w_out, "adamw_gdn_w_out")
    upd["gdn_w_in"] = _adamw_slots(gdn_w_in, landed(mix_handles, 1, even, upd["gdn_w_out"][1], "in"), m_gdn_w_in,
                                   v_gdn_w_in, "adamw_gdn_w_in")

    def update(name, w, g, m, v):
        shape = w.shape
        c = shape[-1]
        res = _adamw(w.reshape(-1, c), g.reshape(-1, c), m.reshape(-1, c), v.reshape(-1, c), "adamw_" + name)
        return [g.reshape(shape)] + [o.reshape(shape) for o in res]

    dlb_rows = jnp.concatenate(d_lb, axis=0)
    tail = jnp.concatenate(
        [jnp.concatenate(d_onorm, axis=1), jnp.concatenate(d_alog, axis=1), jnp.concatenate(d_dtb, axis=1)], axis=1)
    tail = jnp.pad(tail, ((0, 0), (0, d - tail.shape[1])))
    conv_rows = jnp.stack(d_conv).reshape(-1, d)
    packed = jnp.concatenate(
        [jnp.concatenate(d_nmix, axis=0), jnp.concatenate(d_nmlp, axis=0), d_nf, sq, dlb_rows,
         jnp.concatenate(d_gnorm, axis=0), tail, conv_rows], axis=0)
    pad_rows = (-packed.shape[0]) % 8
    packed = jnp.pad(packed, ((0, pad_rows), (0, 0)))
    tot = _all_reduce_small(packed, "reduce_small")
    r0 = 0
    g_nmix = tot[r0:r0 + DEPTH]; r0 += DEPTH
    g_nmlp = tot[r0:r0 + DEPTH]; r0 += DEPTH
    g_nf = tot[r0]; r0 += 1
    loss = tot[r0, 0]; r0 += 1
    g_lb = _lb_bwd(hgrn_lb_logits, tot[r0:r0 + DEPTH], "lb_bwd"); r0 += DEPTH
    g_gnorm_full = tot[r0:r0 + n_hgrn]; r0 += n_hgrn
    t_row = tot[r0]; r0 += 1
    g_conv_full = tot[r0:r0 + n_gdn * CONV_K * 3].reshape(n_gdn, CONV_K, 3 * d)
    g_onorm = t_row[0:n_gdn * HEAD_DIM].reshape(n_gdn, HEAD_DIM)
    o1 = n_gdn * HEAD_DIM
    g_alog = t_row[o1:o1 + n_gdn * N_HEADS].reshape(n_gdn, N_HEADS)
    g_dtb = t_row[o1 + n_gdn * N_HEADS:o1 + 2 * n_gdn * N_HEADS].reshape(n_gdn, N_HEADS)
    c_gn, c_cv = hgrn_gnorm.shape[1], gdn_conv.shape[2]
    g_gnorm = lax.dynamic_slice_in_dim(g_gnorm_full, me_i * c_gn, c_gn, axis=1)
    g_conv = lax.dynamic_slice_in_dim(g_conv_full, me_i * c_cv, c_cv, axis=2)

    upd["gdn_conv"] = update("gdn_conv", gdn_conv, g_conv, m_gdn_conv, v_gdn_conv)
    upd["gdn_a_log"] = update("gdn_a_log", gdn_a_log, g_alog, m_gdn_a_log, v_gdn_a_log)
    upd["gdn_dt_bias"] = update("gdn_dt_bias", gdn_dt_bias, g_dtb, m_gdn_dt_bias, v_gdn_dt_bias)
    upd["gdn_onorm"] = update("gdn_onorm", gdn_onorm, g_onorm, m_gdn_onorm, v_gdn_onorm)
    upd["hgrn_lb_logits"] = update("hgrn_lb_logits", hgrn_lb_logits, g_lb, m_hgrn_lb_logits, v_hgrn_lb_logits)
    upd["hgrn_gnorm"] = update("hgrn_gnorm", hgrn_gnorm, g_gnorm, m_hgrn_gnorm, v_hgrn_gnorm)
    upd["norm_mix"] = update("norm_mix", norm_mix, g_nmix, m_norm_mix, v_norm_mix)
    upd["norm_mlp"] = update("norm_mlp", norm_mlp, g_nmlp, m_norm_mlp, v_norm_mlp)
    upd["norm_final"] = update("norm_final", norm_final, g_nf, m_norm_final, v_norm_final)

    order = ["gdn_w_in", "gdn_conv", "gdn_a_log", "gdn_dt_bias", "gdn_onorm", "gdn_w_out", "hgrn_w_in",
             "hgrn_lb_logits", "hgrn_gnorm", "hgrn_w_out", "norm_mix", "norm_mlp", "mlp_w_up", "mlp_w_down",
             "norm_final"]
    outs = [loss, grad_x]
    for k in range(4):
        outs += [upd[name][k] for name in order]
    return tuple(outs)
```

```python
import functools

import numpy as np
import jax
import jax.numpy as jnp
from jax import lax
from jax.experimental import pallas as pl
from jax.experimental.pallas import tpu as pltpu

F32 = jnp.float32
BF16 = jnp.bfloat16

D_MODEL = 1024
N_HEADS = 8
HEAD_DIM = 128
CHUNK = 64
SUB = 16
N_SUB = CHUNK // SUB
CONV_K = 4
HALO = 16
EPS = 1e-6
DEPTH = 4
N_DEV = 8
GDN_MAIN = 4 * D_MODEL
GDN_IN = GDN_MAIN + 2 * N_HEADS
AB_PAD = 128
HEAD_GROUP = 8
LANE_BLOCK = 256
ROW_BLOCK = 16

ADAM_LR = 0.001
ADAM_B1 = 0.9
ADAM_B2 = 0.999
ADAM_EPS = 1e-08
ADAM_WD = 0.01
ADAM_STEP = 10

VMEM_LIMIT = 56 * 1024 * 1024
MM_TILE = 1024
MM_VMEM_BUDGET = 40 * 1024 * 1024

_DIMS = {
    "nn": (((1,), (0,)), ((), ())),
    "nt": (((1,), (1,)), ((), ())),
    "tn": (((0,), (0,)), ((), ())),
}


def _parts(x, n):
    if n == 1 and x.dtype == BF16:
        return [x]
    out = []
    r = x.astype(F32)
    for i in range(n):
        p = r.astype(BF16)
        out.append(p)
        if i + 1 < n:
            r = r - p.astype(F32)
    return out


def _dot_raw(a, b, mode, na, nb):
    ap, bp = _parts(a, na), _parts(b, nb)
    nmax = max(na, nb)
    pairs = [(i, j) for i in range(na) for j in range(nb) if i + j < nmax]
    ka = 0 if mode == "tn" else 1
    kb = 1 if mode == "nt" else 0
    xa = ap[0] if len(pairs) == 1 else jnp.concatenate([ap[i] for i, _ in pairs], axis=ka)
    xb = bp[0] if len(pairs) == 1 else jnp.concatenate([bp[j] for _, j in pairs], axis=kb)
    return lax.dot_general(xa, xb, _DIMS[mode], preferred_element_type=F32)


@functools.partial(jax.custom_vjp, nondiff_argnums=(2, 3, 4))
def _dot(a, b, mode, na, nb):
    return _dot_raw(a, b, mode, na, nb)


def _dot_fwd(a, b, mode, na, nb):
    return _dot_raw(a, b, mode, na, nb), (a, b)


def _dot_bwd(mode, na, nb, res, ct):
    a, b = res
    if mode == "nn":
        da = _dot_raw(ct, b, "nt", 1, 1)
        db = _dot_raw(a, ct, "tn", 1, 1)
    elif mode == "nt":
        da = _dot_raw(ct, b, "nn", 1, 1)
        db = _dot_raw(ct, a, "tn", 1, 1)
    else:
        da = _dot_raw(b, ct, "nt", 1, 1)
        db = _dot_raw(a, ct, "nn", 1, 1)
    return da.astype(a.dtype), db.astype(b.dtype)


_dot.defvjp(_dot_fwd, _dot_bwd)


N_EXACT = 3


@jax.custom_vjp
def _dot01(x, m_wide, m):
    return lax.dot_general(m_wide, jnp.concatenate(_parts(x, N_EXACT), axis=0), _DIMS["nn"], preferred_element_type=F32)


def _dot01_fwd(x, m_wide, m):
    return _dot01(x, m_wide, m), (m_wide, m)


def _dot01_bwd(res, ct):
    m_wide, m = res
    dx = lax.dot_general(m, ct.astype(BF16), _DIMS["tn"], preferred_element_type=F32)
    return dx, jnp.zeros_like(m_wide), jnp.zeros_like(m)


_dot01.defvjp(_dot01_fwd, _dot01_bwd)


def _thrice(m):
    return jnp.concatenate([m] * N_EXACT, axis=1).astype(BF16), m.astype(BF16)


def _iota2(shape, dim):
    return lax.broadcasted_iota(jnp.int32, shape, dim)


def _tril_f32(n):
    return (_iota2((n, n), 0) >= _iota2((n, n), 1)).astype(F32)


def _cumsum_rows(g):
    return _dot(_tril_f32(g.shape[0]), g, "nn", 1, 3)


def _below_block(n, b):
    ri, ci = _iota2((n, n), 0) // b, _iota2((n, n), 1) // b
    return (ri == ci + 1) & (ri % 2 == 1)


def _half_inverses(L):
    n = L.shape[0]
    eye = (_iota2((n, n), 0) == _iota2((n, n), 1)).astype(F32)
    d = eye - jnp.where(_below_block(n, 1), L, 0.0)
    b = 2
    while 2 * b < n:
        e = jnp.where(_below_block(n, b), L, 0.0)
        d = d - _dot_raw(d, _dot_raw(e, d, "nn", 2, 2), "nn", 2, 2)
        b *= 2
    return d, jnp.where(_below_block(n, b), L, 0.0)


def _solve_with(d, e, rhs):
    y = _dot_raw(d, rhs, "nn", 2, 2)
    return y - _dot_raw(d, _dot_raw(e, y, "nn", 2, 2), "nn", 2, 2)


@jax.custom_vjp
def _solve_unit_lower(L, rhs):
    return _solve_with(*_half_inverses(L), rhs)


def _solve_fwd(L, rhs):
    d, e = _half_inverses(L)
    sol = _solve_with(d, e, rhs)
    return sol, (d, e, sol)


def _solve_bwd(res, ct):
    d, e, sol = res
    y = _dot_raw(d, ct - _dot_raw(e, _dot_raw(d, ct, "tn", 2, 2), "tn", 2, 2), "tn", 2, 2)
    return -_dot_raw(y, sol, "nt", 2, 2), y


_solve_unit_lower.defvjp(_solve_fwd, _solve_bwd)


def _softplus(x):
    return jnp.maximum(x, 0.0) + jnp.log1p(jnp.exp(-jnp.abs(x)))


def _rms(x, w):
    return x * lax.rsqrt(jnp.mean(x * x, axis=-1, keepdims=True) + EPS) * w


HG_LEVELS = (32, 16, 8, 4, 2, 1)


def _hg_level_sums():
    i = np.arange(CHUNK)[:, None]
    m = np.arange(CHUNK)[None, :]
    to_row = [(m <= i) & (m // b == i // b) for b in HG_LEVELS]
    to_col = [(m > i) & (m // b == i // b) for b in HG_LEVELS if b > 1]
    return _thrice(jnp.asarray(np.concatenate(to_row + to_col + [m <= i]), F32))


def _hg_level_masks():
    i = np.arange(CHUNK)[:, None]
    j = np.arange(CHUNK)[None, :]
    return jnp.asarray(np.stack([(i // b == j // b + 1) & ((i // b) % 2 == 1) for b in HG_LEVELS]), F32)


def _hg_pre(qraw, f, lb, sums):
    g = jnp.log(lb + (1.0 - lb) * jax.nn.sigmoid(f))
    k = (1.0 - lb) * jax.nn.sigmoid(-f)
    q = jax.nn.silu(qraw) * (HEAD_DIM ** -0.5)
    return q, k, _dot01(g, *sums)


def _hg_head(st, q, k, v, e, masks):
    nl = len(HG_LEVELS)
    eye = (_iota2((CHUNK, CHUNK), 0) == _iota2((CHUNK, CHUNK), 1)).astype(F32)
    a = eye * jnp.sum(q * k, axis=-1, keepdims=True)
    for l, b in enumerate(HG_LEVELS):
        rows = q * jnp.exp(e[l * CHUNK:(l + 1) * CHUNK])
        cols = k * jnp.exp(e[(nl + l) * CHUNK:(nl + l + 1) * CHUNK]) if b > 1 else k
        a = a + masks[l] * _dot(rows, cols, "nt", 1, 1)
    gc = e[(2 * nl - 1) * CHUNK:2 * nl * CHUNK]
    o = _dot(a, v, "nn", 1, 1) + _dot(q * jnp.exp(gc), st, "nt", 1, 1)
    g_last = gc[CHUNK - 1:CHUNK]
    st_new = st * jnp.exp(g_last) + _dot(v, k * jnp.exp(g_last - gc), "tn", 1, 1)
    return o, st_new


_HG_HEADS = jax.vmap(_hg_head, in_axes=(0, 0, 0, 0, 0, None))


def _hg_post(o, gate, gw):
    return _rms(o, gw) * jax.nn.silu(gate)


def _gd_conv(xp, cw):
    off = HALO - (CONV_K - 1)
    y = cw[0:1] * xp[off:off + CHUNK]
    for kk in range(1, CONV_K):
        y = y + cw[kk:kk + 1] * xp[off + kk:off + kk + CHUNK]
    return y


def _gd_conv_bwd(xp, cw, y, dc):
    off = HALO - (CONV_K - 1)
    sig = jax.nn.sigmoid(y)
    dy = dc * (sig * (1.0 + y * (1.0 - sig)))
    dxp, dcw = None, []
    for kk in range(CONV_K):
        moved = jnp.pad(dy, ((off + kk, HALO - off - kk), (0, 0)))
        term = cw[kk:kk + 1] * moved
        dxp = term if dxp is None else dxp + term
        dcw.append(jnp.sum(xp * moved, axis=0, keepdims=True))
    return dxp, jnp.concatenate(dcw, axis=0)


def _gd_gates(a, b, alog, dtb):
    beta = jax.nn.sigmoid(b)
    g = -jnp.exp(alog) * _softplus(a + dtb)
    expand = (_iota2((N_HEADS, D_MODEL), 1) // HEAD_DIM == _iota2((N_HEADS, D_MODEL), 0)).astype(F32)
    g_x = _dot(g, expand, "nn", 3, 1)
    after = (_iota2((CHUNK, D_MODEL), 0) > _iota2((CHUNK, D_MODEL), 1) % HEAD_DIM).astype(F32)
    sums = _dot01(jnp.concatenate([g_x, g_x * after], axis=1), *_thrice(_tril_f32(CHUNK)))
    return _dot(beta, expand, "nn", 3, 1), sums


def _gd_head(st, q, k, v, beta, gc, diff, gate, onw):
    q = q * lax.rsqrt(jnp.sum(q * q, axis=-1, keepdims=True) + EPS) * (HEAD_DIM ** -0.5)
    k = k * lax.rsqrt(jnp.sum(k * k, axis=-1, keepdims=True) + EPS)
    ri = _iota2((CHUNK, CHUNK), 0)
    ci = _iota2((CHUNK, CHUNK), 1)
    decay = jnp.exp(jnp.where(ri >= ci, diff[:, 0:CHUNK], -jnp.inf))
    kb = k * beta
    egc = jnp.exp(gc)
    L = jnp.where(ri > ci, _dot(kb, k, "nt", 1, 1) * decay, 0.0)
    sol = _solve_unit_lower(L, jnp.concatenate([v * beta, kb * egc], axis=1))
    u = sol[:, 0:HEAD_DIM]
    w = sol[:, HEAD_DIM:2 * HEAD_DIM]
    a_qk = jnp.where(ri >= ci, _dot(q, k, "nt", 1, 1) * decay, 0.0)
    g_last = gc[CHUNK - 1:CHUNK]
    v_new = u - _dot(w, st, "nt", 1, 1)
    o = _dot(q * egc, st, "nt", 1, 1) + _dot(a_qk, v_new, "nn", 1, 1)
    st_new = st * jnp.exp(g_last) + _dot(v_new, k * jnp.exp(g_last - gc), "tn", 1, 1)
    return _rms(o, onw) * jax.nn.silu(gate), st_new


def _params(*sem):
    return pltpu.CompilerParams(dimension_semantics=sem, vmem_limit_bytes=VMEM_LIMIT)


def _tile(n, pref):
    t = min(n, pref)
    assert n % t == 0, (n, pref)
    return t


def _mm_tiles(m, n, k, a_size, b_size, tile_sizes):
    tm, tn, tk = _tile(m, MM_TILE), _tile(n, MM_TILE), k

    def need(tm, tn, tk):
        acc = 4 * tm * tn * (2 if tk < k else 1)
        return 2 * (tm * tk * a_size + tk * tn * b_size + tm * tn * sum(tile_sizes)) + acc

    while need(tm, tn, tk) > MM_VMEM_BUDGET:
        if tk > 2048 or (tk > 512 and tm <= 512):
            tk //= 2
        else:
            tm //= 2
    return tm, tn, tk


def _mm(a, b, mode, out_dtypes, name, epilogue=None, extras=(), vectors=(), n_sums=0, after=None):
    if mode == "nn":
        (m, k), (k2, n) = a.shape, b.shape
    elif mode == "nt":
        (m, k), (n, k2) = a.shape, b.shape
    else:
        (k, m), (k2, n) = a.shape, b.shape
    assert k == k2, (a.shape, b.shape, mode)
    tm, tn, tk = _mm_tiles(m, n, k, a.dtype.itemsize, b.dtype.itemsize,
                           [e.dtype.itemsize for e in extras] + [jnp.dtype(dt).itemsize for dt in out_dtypes])
    nk = k // tk
    assert not (vectors or n_sums) or tn == n, "whole-row epilogues need the result tile to span the rows"
    ne, no, nafter = len(extras) + len(vectors), len(out_dtypes), int(after is not None)
    if epilogue is None:
        epilogue = lambda acc: (acc,)

    def body(*refs):
        a_ref, b_ref = refs[0], refs[1]
        ex = refs[2:2 + ne]
        outs = refs[2 + ne + nafter:2 + ne + nafter + no]
        sums = refs[2 + ne + nafter + no:2 + ne + nafter + no + n_sums]
        part = lax.dot_general(a_ref[...].astype(BF16), b_ref[...].astype(BF16), _DIMS[mode],
                               preferred_element_type=F32)

        def finish(acc):
            vals = epilogue(acc, *[e[...] for e in ex])
            for o_ref, val in zip(outs, vals[:no]):
                o_ref[...] = val.astype(o_ref.dtype)
            for s_ref, val in zip(sums, vals[no:]):
                @pl.when(pl.program_id(0) == 0)
                def _(s_ref=s_ref, val=val):
                    s_ref[...] = val

                @pl.when(pl.program_id(0) > 0)
                def _(s_ref=s_ref, val=val):
                    s_ref[...] += val

        if nk == 1:
            finish(part)
        else:
            acc_ref = refs[-1]
            kk = pl.program_id(2)

            @pl.when(kk == 0)
            def _():
                acc_ref[...] = part

            @pl.when(kk > 0)
            def _():
                acc_ref[...] += part

            @pl.when(kk == nk - 1)
            def _():
                finish(acc_ref[...])

    if mode == "tn":
        a_spec = pl.BlockSpec((tk, tm), lambda i, j, kk: (kk, i))
    else:
        a_spec = pl.BlockSpec((tm, tk), lambda i, j, kk: (i, kk))
    if mode == "nt":
        b_spec = pl.BlockSpec((tn, tk), lambda i, j, kk: (j, kk))
    else:
        b_spec = pl.BlockSpec((tk, tn), lambda i, j, kk: (kk, j))
    o_spec = pl.BlockSpec((tm, tn), lambda i, j, kk: (i, j))
    v_spec = pl.BlockSpec((1, tn), lambda i, j, kk: (0, j))
    res = pl.pallas_call(
        body,
        name=name,
        grid=(m // tm, n // tn, nk),
        in_specs=([a_spec, b_spec] + [o_spec] * len(extras) + [v_spec] * len(vectors)
                  + [pl.BlockSpec(memory_space=pl.ANY)] * nafter),
        out_specs=[o_spec] * no + [v_spec] * n_sums,
        out_shape=[jax.ShapeDtypeStruct((m, n), dt) for dt in out_dtypes] + [jax.ShapeDtypeStruct((1, n), F32)] * n_sums,
        scratch_shapes=[pltpu.VMEM((tm, tn), F32)] if nk > 1 else [],
        compiler_params=_params(*(("arbitrary",) * 3 if n_sums else ("parallel", "parallel", "arbitrary"))),
    )(a, b, *extras, *vectors, *([after] if nafter else []))
    return res[0] if no + n_sums == 1 else res


def _ep_residual_norm(acc, res, w):
    h = res + acc
    return h, _rms(h, w)


def _ep_norm_bwd(acc, x, dres, w):
    _, vjp = jax.vjp(_rms, x, w)
    dx, dw = vjp(acc)
    dx = dres + dx
    return dx, dx, dw


def _rms_fwd(x, w, name, tm=512):
    n, d = x.shape
    tm = _tile(n, tm)

    def body(x_ref, w_ref, y_ref):
        y_ref[...] = _rms(x_ref[...], w_ref[...]).astype(y_ref.dtype)

    return pl.pallas_call(
        body, name=name, grid=(n // tm,),
        in_specs=[pl.BlockSpec((tm, d), lambda i: (i, 0)), pl.BlockSpec((1, d), lambda i: (0, 0))],
        out_specs=pl.BlockSpec((tm, d), lambda i: (i, 0)),
        out_shape=jax.ShapeDtypeStruct((n, d), BF16),
        compiler_params=_params("arbitrary"),
    )(x, w)


def _loss_head(h, w, target, name, tm=512):
    n, d = h.shape
    tm = _tile(n, tm)

    def body(h_ref, w_ref, t_ref, dh_ref, dhb_ref, dw_ref, sq_ref):
        y, vjp = jax.vjp(_rms, h_ref[...], w_ref[...])
        err = y - t_ref[...]
        dh, dw = vjp(err * (1.0 / d))
        dh_ref[...] = dh
        dhb_ref[...] = dh.astype(dhb_ref.dtype)
        sq = jnp.sum(err * err, axis=0, keepdims=True)

        @pl.when(pl.program_id(0) == 0)
        def _():
            dw_ref[...] = dw
            sq_ref[...] = sq

        @pl.when(pl.program_id(0) > 0)
        def _():
            dw_ref[...] += dw
            sq_ref[...] += sq

        @pl.when(pl.program_id(0) == n // tm - 1)
        def _():
            total = jnp.sum(sq_ref[...], axis=1, keepdims=True) * (0.5 / d)
            sq_ref[...] = jnp.broadcast_to(total, sq_ref.shape)

    row = pl.BlockSpec((tm, d), lambda i: (i, 0))
    vec = pl.BlockSpec((1, d), lambda i: (0, 0))
    return pl.pallas_call(
        body, name=name, grid=(n // tm,),
        in_specs=[row, vec, row],
        out_specs=[row, row, vec, vec],
        out_shape=[jax.ShapeDtypeStruct((n, d), F32), jax.ShapeDtypeStruct((n, d), BF16),
                   jax.ShapeDtypeStruct((1, d), F32), jax.ShapeDtypeStruct((1, d), F32)],
        compiler_params=_params("arbitrary"),
    )(h, w, target)


def _lower_bounds(logits):
    sm = jax.nn.softmax(logits, axis=0)
    rows = [sm[0:1] * 0.0]
    for r in range(1, DEPTH):
        rows.append(rows[-1] + sm[r:r + 1])
    return jnp.concatenate(rows, axis=0)


def _lb_fwd(logits, name):
    def body(l_ref, o_ref):
        o_ref[...] = _lower_bounds(l_ref[...])

    return pl.pallas_call(body, name=name, out_shape=jax.ShapeDtypeStruct(logits.shape, F32))(logits)


def _lb_bwd(logits, dlb, name):
    def body(l_ref, d_ref, o_ref):
        _, vjp = jax.vjp(_lower_bounds, l_ref[...])
        (o_ref[...],) = vjp(d_ref[...])

    return pl.pallas_call(body, name=name, out_shape=jax.ShapeDtypeStruct(logits.shape, F32))(logits, dlb)


def _head_slice(h):
    if isinstance(h, int):
        return pl.ds(h * HEAD_DIM, HEAD_DIM)
    return pl.ds(pl.multiple_of(h * HEAD_DIM, HEAD_DIM), HEAD_DIM)


def _head_groups(group_body):
    if HEAD_GROUP == N_HEADS:
        group_body(list(range(N_HEADS)))
        return

    def trip(i, carry):
        group_body([i * HEAD_GROUP + t for t in range(HEAD_GROUP)])
        return carry

    lax.fori_loop(0, N_HEADS // HEAD_GROUP, trip, 0)


def _stack_heads(ref, hs, first=0):
    return jnp.stack([ref[:, _head_slice(h + first)] for h in hs])


def _unstack_heads(ref, hs, val, first=0):
    for t, h in enumerate(hs):
        ref[:, _head_slice(h + first)] = val[t].astype(ref.dtype)


_GD_HEADS = jax.vmap(_gd_head, in_axes=(0, 0, 0, 0, 0, 0, 0, 0, None))


def _hgrn_fwd(proj, lb, gw, seqs, name):
    n = proj.shape[0]
    nc = n // seqs // CHUNK
    d = D_MODEL

    sums, masks = _hg_level_sums(), _hg_level_masks()

    def body(p_ref, lb_ref, gw_ref, sums_wide_ref, sums_once_ref, masks_ref, o2_ref, o_ref, st_all_ref,
             st_sc, q_sc, k_sc, v_sc, e_sc):
        @pl.when(pl.program_id(1) == 0)
        def _():
            st_sc[...] = jnp.zeros_like(st_sc)

        sums_refs = (sums_wide_ref, sums_once_ref)
        _lane_blocks(d, functools.partial(_hg_pre_block, p_ref, lb_ref, sums_refs, q_sc, k_sc, v_sc, e_sc))
        st_all_ref[0] = st_sc[...]

        def group(hs):
            sts = pl.ds(hs[0], len(hs))
            o, st_new = _HG_HEADS(st_sc[sts], *[_stack_heads(r, hs) for r in (q_sc, k_sc, v_sc, e_sc)], masks_ref[...])
            _unstack_heads(o_ref, hs, o)
            st_sc[sts] = st_new

        _head_groups(group)

        def post(rows):
            gate = p_ref[rows, 3 * d:4 * d].astype(F32)
            o2_ref[rows, :] = _hg_post(o_ref[rows, :], gate, gw_ref[...]).astype(o2_ref.dtype)

        _row_blocks(CHUNK, post)

    idx = lambda b, c: (b * nc + c, 0)
    vec = pl.BlockSpec((1, d), lambda b, c: (0, 0))
    act = pl.BlockSpec((CHUNK, d), idx)
    return pl.pallas_call(
        body, name=name, grid=(seqs, nc),
        in_specs=[pl.BlockSpec((CHUNK, 4 * d), idx), vec, vec] + [pl.BlockSpec(s.shape, lambda b, c: (0, 0)) for s in sums]
        + [pl.BlockSpec(masks.shape, lambda b, c: (0, 0, 0))],
        out_specs=[act, act, pl.BlockSpec((1, N_HEADS, HEAD_DIM, HEAD_DIM), lambda b, c: (b * nc + c, 0, 0, 0))],
        out_shape=[jax.ShapeDtypeStruct((n, d), BF16), jax.ShapeDtypeStruct((n, d), F32),
                   jax.ShapeDtypeStruct((n // CHUNK, N_HEADS, HEAD_DIM, HEAD_DIM), F32)],
        scratch_shapes=[pltpu.VMEM((N_HEADS, HEAD_DIM, HEAD_DIM), F32)] + [pltpu.VMEM((CHUNK, d), F32)] * 3
        + [pltpu.VMEM((sums[0].shape[0], d), F32)],
        compiler_params=_params("arbitrary", "arbitrary"),
    )(proj, lb, gw, *sums, masks)


def _hgrn_bwd(proj, lb, gw, st_all, o, do2, seqs, name):
    n = proj.shape[0]
    nc = n // seqs // CHUNK
    d = D_MODEL

    sums, masks = _hg_level_sums(), _hg_level_masks()

    def body(p_ref, lb_ref, gw_ref, sums_wide_ref, sums_once_ref, masks_ref, st_all_ref, o_ref, do2_ref,
             dp_ref, dlb_ref, dgw_ref,
             dst_sc, q_sc, k_sc, v_sc, e_sc, do_sc, dq_sc, dk_sc, dv_sc, de_sc, dgw_sc):
        first = (pl.program_id(0) == 0) & (pl.program_id(1) == 0)

        @pl.when(pl.program_id(1) == 0)
        def _():
            dst_sc[...] = jnp.zeros_like(dst_sc)

        sums_refs = (sums_wide_ref, sums_once_ref)
        _lane_blocks(d, functools.partial(_hg_pre_block, p_ref, lb_ref, sums_refs, q_sc, k_sc, v_sc, e_sc))
        dgw_sc[...] = jnp.zeros_like(dgw_sc)

        def post_bwd(rows):
            _, vjp = jax.vjp(_hg_post, o_ref[rows, :], p_ref[rows, 3 * d:4 * d].astype(F32), gw_ref[...])
            do_sc[rows, :], dgate, dgw = vjp(do2_ref[rows, :].astype(F32))
            dp_ref[rows, 3 * d:4 * d] = dgate.astype(dp_ref.dtype)
            dgw_sc[...] += dgw

        _row_blocks(CHUNK, post_bwd)

        def group(hs):
            sts = pl.ds(hs[0], len(hs))
            level_masks = masks_ref[...]
            _, vjp = jax.vjp(lambda *a: _HG_HEADS(*a, level_masks), st_all_ref[0, sts],
                             *[_stack_heads(r, hs) for r in (q_sc, k_sc, v_sc, e_sc)])
            grads = vjp((_stack_heads(do_sc, hs), dst_sc[sts]))
            dst_sc[sts] = grads[0]
            for r, val in zip((dq_sc, dk_sc, dv_sc, de_sc), grads[1:]):
                _unstack_heads(r, hs, val)

        _head_groups(group)

        def pre_bwd(at):
            sl = at()
            level_sums = (sums_wide_ref[...], sums_once_ref[...])
            _, vjp = jax.vjp(lambda qraw, f, lb: _hg_pre(qraw, f, lb, level_sums), p_ref[:, sl].astype(F32),
                             p_ref[:, at(d)].astype(F32), lb_ref[:, sl])
            dqraw, df, dlb = vjp((dq_sc[:, sl], dk_sc[:, sl], de_sc[:, sl]))
            dp_ref[:, sl] = dqraw.astype(dp_ref.dtype)
            dp_ref[:, at(d)] = df.astype(dp_ref.dtype)
            dp_ref[:, at(2 * d)] = dv_sc[:, sl].astype(dp_ref.dtype)

            @pl.when(first)
            def _():
                dlb_ref[:, sl] = dlb

            @pl.when(jnp.logical_not(first))
            def _():
                dlb_ref[:, sl] += dlb

        _lane_blocks(d, pre_bwd)

        @pl.when(first)
        def _():
            dgw_ref[...] = dgw_sc[...]

        @pl.when(jnp.logical_not(first))
        def _():
            dgw_ref[...] += dgw_sc[...]

    idx = lambda b, c: (b * nc + nc - 1 - c, 0)
    vec = pl.BlockSpec((1, d), lambda b, c: (0, 0))
    act = pl.BlockSpec((CHUNK, d), idx)
    wide = pl.BlockSpec((CHUNK, 4 * d), idx)
    return pl.pallas_call(
        body, name=name, grid=(seqs, nc),
        in_specs=[wide, vec, vec] + [pl.BlockSpec(s.shape, lambda b, c: (0, 0)) for s in sums] + [
                  pl.BlockSpec(masks.shape, lambda b, c: (0, 0, 0)),
                  pl.BlockSpec((1, N_HEADS, HEAD_DIM, HEAD_DIM), lambda b, c: (b * nc + nc - 1 - c, 0, 0, 0)),
                  act, act],
        out_specs=[wide, vec, vec],
        out_shape=[jax.ShapeDtypeStruct((n, 4 * d), BF16), jax.ShapeDtypeStruct((1, d), F32),
                   jax.ShapeDtypeStruct((1, d), F32)],
        scratch_shapes=[pltpu.VMEM((N_HEADS, HEAD_DIM, HEAD_DIM), F32)]
        + [pltpu.VMEM((CHUNK, d), F32)] * 3 + [pltpu.VMEM((sums[0].shape[0], d), F32)]
        + [pltpu.VMEM((CHUNK, d), F32)] * 4 + [pltpu.VMEM((sums[0].shape[0], d), F32), pltpu.VMEM((1, d), F32)],
        compiler_params=_params("arbitrary", "arbitrary"),
    )(proj, lb, gw, *sums, masks, st_all, o, do2)


def _lane_blocks(width, block_body):
    def trip(j, carry):
        block_body(lambda base=0: pl.ds(pl.multiple_of(j * LANE_BLOCK + base, LANE_BLOCK), LANE_BLOCK))
        return carry

    lax.fori_loop(0, width // LANE_BLOCK, trip, 0)


def _row_blocks(rows, block_body):
    def trip(j, carry):
        block_body(pl.ds(pl.multiple_of(j * ROW_BLOCK, ROW_BLOCK), ROW_BLOCK))
        return carry

    lax.fori_loop(0, rows // ROW_BLOCK, trip, 0)


def _hg_pre_block(p_ref, lb_ref, sums_refs, q_sc, k_sc, v_sc, e_sc, at):
    sl = at()
    q_sc[:, sl], k_sc[:, sl], e_sc[:, sl] = _hg_pre(
        p_ref[:, sl].astype(F32), p_ref[:, at(D_MODEL)].astype(F32), lb_ref[:, sl], [r[...] for r in sums_refs])
    v_sc[:, sl] = p_ref[:, at(2 * D_MODEL)].astype(F32)


def _gd_xp(halo_ref, p_ref, sl, first_chunk):
    halo = jnp.where(first_chunk, 0.0, halo_ref[:, sl].astype(F32))
    return jnp.concatenate([halo, p_ref[:, sl].astype(F32)], axis=0)


def _gdn_fwd(projm, projab, cw, alog, dtb, onw, seqs, name):
    n = projm.shape[0]
    nc = n // seqs // CHUNK
    d = D_MODEL
    per_halo = CHUNK // HALO

    def body(p_ref, halo_ref, ab_ref, cw_ref, alog_ref, dtb_ref, onw_ref, o2_ref, st_all_ref, y_ref,
             st_sc, c_sc, beta_sc, g_sc):
        @pl.when(pl.program_id(1) == 0)
        def _():
            st_sc[...] = jnp.zeros_like(st_sc)

        def conv(at):
            sl = at()
            y = _gd_conv(_gd_xp(halo_ref, p_ref, sl, pl.program_id(1) == 0), cw_ref[:, sl])
            y_ref[:, sl] = y
            c_sc[:, sl] = jax.nn.silu(y)

        _lane_blocks(3 * d, conv)
        beta_sc[...], g_sc[...] = _gd_gates(ab_ref[:, 0:N_HEADS], ab_ref[:, N_HEADS:2 * N_HEADS], alog_ref[...],
                                            dtb_ref[...])
        st_all_ref[0] = st_sc[...]

        def group(hs):
            sts = pl.ds(hs[0], len(hs))
            o2, st_new = _GD_HEADS(
                st_sc[sts], _stack_heads(c_sc, hs), _stack_heads(c_sc, hs, N_HEADS), _stack_heads(c_sc, hs, 2 * N_HEADS),
                _stack_heads(beta_sc, hs), _stack_heads(g_sc, hs), _stack_heads(g_sc, hs, N_HEADS),
                _stack_heads(p_ref, hs, 3 * N_HEADS).astype(F32), onw_ref[...])
            _unstack_heads(o2_ref, hs, o2)
            st_sc[sts] = st_new

        _head_groups(group)

    idx = lambda b, c: (b * nc + c, 0)
    const = lambda b, c: (0, 0)
    return pl.pallas_call(
        body, name=name, grid=(seqs, nc),
        in_specs=[pl.BlockSpec((CHUNK, 4 * d), idx),
                  pl.BlockSpec((HALO, 3 * d), lambda b, c: (jnp.maximum((b * nc + c) * per_halo - 1, 0), 0)),
                  pl.BlockSpec((CHUNK, AB_PAD), idx),
                  pl.BlockSpec((CONV_K, 3 * d), const), pl.BlockSpec((1, N_HEADS), const),
                  pl.BlockSpec((1, N_HEADS), const), pl.BlockSpec((1, HEAD_DIM), const)],
        out_specs=[pl.BlockSpec((CHUNK, d), idx),
                   pl.BlockSpec((1, N_HEADS, HEAD_DIM, HEAD_DIM), lambda b, c: (b * nc + c, 0, 0, 0)),
                   pl.BlockSpec((CHUNK, 3 * d), idx)],
        out_shape=[jax.ShapeDtypeStruct((n, d), BF16),
                   jax.ShapeDtypeStruct((n // CHUNK, N_HEADS, HEAD_DIM, HEAD_DIM), F32),
                   jax.ShapeDtypeStruct((n, 3 * d), F32)],
        scratch_shapes=[pltpu.VMEM((N_HEADS, HEAD_DIM, HEAD_DIM), F32), pltpu.VMEM((CHUNK, 3 * d), F32),
                        pltpu.VMEM((CHUNK, d), F32), pltpu.VMEM((CHUNK, 2 * d), F32)],
        compiler_params=_params("arbitrary", "arbitrary"),
    )(projm, projm, projab, cw, alog, dtb, onw)


def _gdn_bwd(projm, projab, conv_y, cw, alog, dtb, onw, st_all, do2, seqs, name):
    n = projm.shape[0]
    nc = n // seqs // CHUNK
    d = D_MODEL
    per_halo = CHUNK // HALO

    def body(p_ref, halo_ref, ab_ref, y_ref, cw_ref, alog_ref, dtb_ref, onw_ref, st_all_ref, do2_ref,
             dp_ref, dab_ref, dcw_ref, dalog_ref, ddtb_ref, donw_ref,
             dst_sc, dhalo_sc, c_sc, beta_sc, g_sc, dc_sc, dbeta_sc, dg_sc, donw_sc):
        step = pl.program_id(1)
        first = (pl.program_id(0) == 0) & (step == 0)

        @pl.when(step == 0)
        def _():
            dst_sc[...] = jnp.zeros_like(dst_sc)
            dhalo_sc[...] = jnp.zeros_like(dhalo_sc)

        donw_sc[...] = jnp.zeros_like(donw_sc)

        def act(at):
            c_sc[:, at()] = jax.nn.silu(y_ref[:, at()])

        _lane_blocks(3 * d, act)
        (beta_sc[...], g_sc[...]), gates_vjp = jax.vjp(
            _gd_gates, ab_ref[:, 0:N_HEADS], ab_ref[:, N_HEADS:2 * N_HEADS], alog_ref[...], dtb_ref[...])

        def group(hs):
            sts = pl.ds(hs[0], len(hs))
            _, vjp = jax.vjp(
                _GD_HEADS, st_all_ref[0, sts], _stack_heads(c_sc, hs), _stack_heads(c_sc, hs, N_HEADS),
                _stack_heads(c_sc, hs, 2 * N_HEADS), _stack_heads(beta_sc, hs), _stack_heads(g_sc, hs),
                _stack_heads(g_sc, hs, N_HEADS), _stack_heads(p_ref, hs, 3 * N_HEADS).astype(F32), onw_ref[...])
            dst, dq, dk, dv, dbeta, dg, ddiff, dgate, donw = vjp(
                (_stack_heads(do2_ref, hs).astype(F32), dst_sc[sts]))
            _unstack_heads(dg_sc, hs, ddiff, N_HEADS)
            dst_sc[sts] = dst
            _unstack_heads(dc_sc, hs, dq)
            _unstack_heads(dc_sc, hs, dk, N_HEADS)
            _unstack_heads(dc_sc, hs, dv, 2 * N_HEADS)
            _unstack_heads(dbeta_sc, hs, dbeta)
            _unstack_heads(dg_sc, hs, dg)
            _unstack_heads(dp_ref, hs, dgate, 3 * N_HEADS)
            donw_sc[...] += donw

        _head_groups(group)
        def conv_bwd(at):
            sl = at()
            dxp, dcw = _gd_conv_bwd(_gd_xp(halo_ref, p_ref, sl, step == nc - 1), cw_ref[:, sl], y_ref[:, sl],
                                    dc_sc[:, sl])
            dqkv = jnp.concatenate([dxp[HALO:CHUNK], dxp[CHUNK:HALO + CHUNK] + dhalo_sc[:, sl]], axis=0)
            dp_ref[:, sl] = dqkv.astype(dp_ref.dtype)
            dhalo_sc[:, sl] = dxp[0:HALO]

            @pl.when(first)
            def _():
                dcw_ref[:, sl] = dcw

            @pl.when(jnp.logical_not(first))
            def _():
                dcw_ref[:, sl] += dcw

        _lane_blocks(3 * d, conv_bwd)
        da, db, dalog, ddtb = gates_vjp((dbeta_sc[...], dg_sc[...]))
        dab_ref[...] = jnp.concatenate(
            [da, db, jnp.zeros((CHUNK, AB_PAD - 2 * N_HEADS), F32)], axis=1).astype(dab_ref.dtype)

        @pl.when(first)
        def _():
            dalog_ref[...] = dalog
            ddtb_ref[...] = ddtb
            donw_ref[...] = donw_sc[...]

        @pl.when(jnp.logical_not(first))
        def _():
            dalog_ref[...] += dalog
            ddtb_ref[...] += ddtb
            donw_ref[...] += donw_sc[...]

    rev = lambda b, c: b * nc + nc - 1 - c
    idx = lambda b, c: (rev(b, c), 0)
    const = lambda b, c: (0, 0)
    small = [pl.BlockSpec((CONV_K, 3 * d), const), pl.BlockSpec((1, N_HEADS), const),
             pl.BlockSpec((1, N_HEADS), const), pl.BlockSpec((1, HEAD_DIM), const)]
    return pl.pallas_call(
        body, name=name, grid=(seqs, nc),
        in_specs=[pl.BlockSpec((CHUNK, 4 * d), idx),
                  pl.BlockSpec((HALO, 3 * d), lambda b, c: (jnp.maximum(rev(b, c) * per_halo - 1, 0), 0)),
                  pl.BlockSpec((CHUNK, AB_PAD), idx), pl.BlockSpec((CHUNK, 3 * d), idx)] + small + [
                  pl.BlockSpec((1, N_HEADS, HEAD_DIM, HEAD_DIM), lambda b, c: (rev(b, c), 0, 0, 0)),
                  pl.BlockSpec((CHUNK, d), idx)],
        out_specs=[pl.BlockSpec((CHUNK, 4 * d), idx), pl.BlockSpec((CHUNK, AB_PAD), idx)] + small,
        out_shape=[jax.ShapeDtypeStruct((n, 4 * d), BF16), jax.ShapeDtypeStruct((n, AB_PAD), BF16),
                   jax.ShapeDtypeStruct((CONV_K, 3 * d), F32), jax.ShapeDtypeStruct((1, N_HEADS), F32),
                   jax.ShapeDtypeStruct((1, N_HEADS), F32), jax.ShapeDtypeStruct((1, HEAD_DIM), F32)],
        scratch_shapes=[pltpu.VMEM((N_HEADS, HEAD_DIM, HEAD_DIM), F32), pltpu.VMEM((HALO, 3 * d), F32),
                        pltpu.VMEM((CHUNK, 3 * d), F32), pltpu.VMEM((CHUNK, d), F32), pltpu.VMEM((CHUNK, 2 * d), F32),
                        pltpu.VMEM((CHUNK, 3 * d), F32), pltpu.VMEM((CHUNK, d), F32), pltpu.VMEM((CHUNK, 2 * d), F32),
                        pltpu.VMEM((1, HEAD_DIM), F32)],
        compiler_params=_params("arbitrary", "arbitrary"),
    )(projm, projm, projab, conv_y, cw, alog, dtb, onw, st_all, do2)


def _adam_update(w, g, m, v):
    b1c = 1.0 - ADAM_B1 ** ADAM_STEP
    b2c = 1.0 - ADAM_B2 ** ADAM_STEP
    m_new = ADAM_B1 * m + (1.0 - ADAM_B1) * g
    v_new = ADAM_B2 * v + (1.0 - ADAM_B2) * (g * g)
    delta = -ADAM_LR * ((m_new / b1c) / (jnp.sqrt(v_new / b2c) + ADAM_EPS) + ADAM_WD * w)
    return delta, m_new, v_new


def _adamw(w, g, m, v, name, tr=256):
    r, c = w.shape
    tr = _tile(r, tr)

    def body(w_ref, g_ref, m_ref, v_ref, d_ref, mo_ref, vo_ref):
        d_ref[...], mo_ref[...], vo_ref[...] = _adam_update(w_ref[...], g_ref[...], m_ref[...], v_ref[...])

    blk = pl.BlockSpec((tr, c), lambda i: (i, 0))
    return pl.pallas_call(
        body, name=name, grid=(r // tr,),
        in_specs=[blk] * 4, out_specs=[blk] * 3,
        out_shape=[jax.ShapeDtypeStruct((r, c), F32)] * 3,
        compiler_params=_params("arbitrary"),
    )(w, g, m, v)


def _adamw_slots(w, slot_bufs, m, v, name, tr=256):
    nl, r, c = w.shape
    tr = _tile(r, tr)

    def body(*refs):
        w_ref = refs[0]
        g_refs = refs[1:1 + nl]
        m_ref, v_ref, go_ref, d_ref, mo_ref, vo_ref = refs[1 + nl:]
        for k in range(nl):
            @pl.when(pl.program_id(0) == k)
            def _(k=k):
                g = g_refs[k][0].astype(F32)
                for s in range(1, N_DEV):
                    g = g + g_refs[k][s].astype(F32)
                go_ref[0] = g

        d_ref[0], mo_ref[0], vo_ref[0] = _adam_update(w_ref[0], go_ref[0], m_ref[0], v_ref[0])

    blk = pl.BlockSpec((1, tr, c), lambda l, i: (l, i, 0))
    g_specs = [pl.BlockSpec((N_DEV, tr, c), lambda l, i, k=k: (0, jnp.where(l == k, i, 0), 0)) for k in range(nl)]
    return pl.pallas_call(
        body, name=name, grid=(nl, r // tr),
        in_specs=[blk] + g_specs + [blk, blk], out_specs=[blk] * 4,
        out_shape=[jax.ShapeDtypeStruct((nl, r, c), F32)] * 4,
        compiler_params=_params("arbitrary", "arbitrary"),
    )(w, *slot_bufs, m, v)


def _mesh_pos():
    return lax.axis_index("x"), lax.axis_index("y"), lax.axis_index("c")


def _flip(pos, p):
    x, y, c = pos
    return ((1 - x) if p & 4 else x, (1 - y) if p & 2 else y, (1 - c) if p & 1 else c)


def _lin(pos):
    return 4 * pos[0] + 2 * pos[1] + pos[2]


_HBM = pl.BlockSpec(memory_space=pltpu.HBM)
_SEM = pl.BlockSpec(memory_space=pltpu.SEMAPHORE)
_DATAFLOW = pltpu.SideEffectType.DATAFLOW_SIDE_EFFECTING


class _Item:
    def __init__(self, src, land_shape, src_pick, dst_pick):
        self.src, self.land_shape, self.src_pick, self.dst_pick = src, land_shape, src_pick, dst_pick


def _remote_copies(items, src, land, send_sem, recv_sem, me, arriving):
    me_i = _lin(me)
    out = []
    for it, s_ref, l_ref in zip(items, src, land):
        for p in range(1, N_DEV):
            peer = _flip(me, p)
            out.append(pltpu.make_async_remote_copy(
                src_ref=it.src_pick(s_ref, _lin(peer)),
                dst_ref=it.dst_pick(l_ref, _lin(peer) if arriving else me_i),
                send_sem=send_sem, recv_sem=recv_sem, device_id=peer, device_id_type=pl.DeviceIdType.MESH))
    return out


def _own_copies(items, src, land, sem, me):
    me_i = _lin(me)
    return [pltpu.make_async_copy(it.src_pick(s_ref, me_i), it.dst_pick(l_ref, me_i), sem)
            for it, s_ref, l_ref in zip(items, src, land)]


def _exchange_start(groups, name):
    items = [it for g in groups for it in g]
    n, ng = len(items), len(groups)
    first = [sum(len(g) for g in groups[:gi]) for gi in range(ng)]

    def body(*refs):
        src, land = refs[0:n], refs[n:2 * n]
        send_sems, recv_sems = refs[2 * n:2 * n + ng], refs[2 * n + ng:2 * n + 2 * ng]
        token = refs[4 * n + 2 * ng]
        me = _mesh_pos()
        for gi, g in enumerate(groups):
            sl = slice(first[gi], first[gi] + len(g))
            for cp in _remote_copies(g, src[sl], land[sl], send_sems[gi], recv_sems[gi], me, arriving=False):
                cp.start()
            for cp in _own_copies(g, src[sl], land[sl], recv_sems[gi], me):
                cp.start()
        token[...] = jnp.zeros_like(token)

    srcs = [pltpu.with_memory_space_constraint(it.src, pltpu.HBM) for it in items]
    lands = [pltpu.with_memory_space_constraint(lax.empty(it.land_shape, it.src.dtype), pltpu.HBM) for it in items]
    res = pl.pallas_call(
        body, name=name,
        out_shape=([pltpu.SemaphoreType.DMA(())] * (2 * ng)
                   + [pltpu.HBM(it.src.shape, it.src.dtype) for it in items]
                   + [pltpu.HBM(it.land_shape, it.src.dtype) for it in items]
                   + [jax.ShapeDtypeStruct((8, 128), F32)]),
        in_specs=[_HBM] * (2 * n),
        out_specs=[_SEM] * (2 * ng) + [_HBM] * (2 * n) + [pl.BlockSpec(memory_space=pltpu.VMEM)],
        input_output_aliases={i: 2 * ng + i for i in range(2 * n)},
        compiler_params=pltpu.CompilerParams(has_side_effects=_DATAFLOW),
    )(*srcs, *lands)
    send_sems, recv_sems = res[0:ng], res[ng:2 * ng]
    src_thru, land_thru = res[2 * ng:2 * ng + n], res[2 * ng + n:2 * ng + 2 * n]
    handles = []
    for gi, g in enumerate(groups):
        sl = slice(first[gi], first[gi] + len(g))
        handles.append((g, src_thru[sl], land_thru[sl], send_sems[gi], recv_sems[gi]))
    return handles, res[-1]


def _exchange_wait(handle, after, name):
    items, src_thru, land_thru, send_sem, recv_sem = handle
    k = len(items)

    def body(*refs):
        src, land = refs[0:k], refs[k:2 * k]
        send_ref, recv_ref = refs[2 * k], refs[2 * k + 1]
        for cp in _remote_copies(items, src, land, send_ref, recv_ref, _mesh_pos(), arriving=True):
            cp.wait_send()
            cp.wait_recv()
        for cp in _own_copies(items, src, land, recv_ref, _mesh_pos()):
            cp.wait()

    res = pl.pallas_call(
        body, name=name,
        out_shape=([pltpu.HBM(s.shape, s.dtype) for s in src_thru] + [pltpu.HBM(l.shape, l.dtype) for l in land_thru]),
        in_specs=[_HBM] * (2 * k) + [_SEM, _SEM, pl.BlockSpec(memory_space=pl.ANY)],
        out_specs=[_HBM] * (2 * k),
        input_output_aliases={i: i for i in range(2 * k)},
        compiler_params=pltpu.CompilerParams(has_side_effects=_DATAFLOW),
    )(*src_thru, *land_thru, send_sem, recv_sem, after)
    return res[k:2 * k]


def _whole(ref, i):
    return ref


def _slot(ref, i):
    return ref.at[i]


def _rows_of(r):
    return lambda ref, i: ref.at[pl.ds(pl.multiple_of(i * r, r), r), :]


def _cols_of(c):
    return lambda ref, i: ref.at[:, pl.ds(pl.multiple_of(i * c, c), c)]


def _all_reduce_small(buf, name):
    r, c = buf.shape

    def body(src_ref, out_ref, all_ref, send_sems, recv_sems):
        me = _mesh_pos()
        me_i = _lin(me)
        all_ref[me_i] = src_ref[...]
        for p in range(1, N_DEV):
            peer = _flip(me, p)
            pltpu.make_async_remote_copy(
                src_ref=src_ref, dst_ref=all_ref.at[me_i], send_sem=send_sems.at[p - 1], recv_sem=recv_sems.at[p - 1],
                device_id=peer, device_id_type=pl.DeviceIdType.MESH).start()
        for p in range(1, N_DEV):
            peer = _flip(me, p)
            cp = pltpu.make_async_remote_copy(
                src_ref=src_ref, dst_ref=all_ref.at[_lin(peer)], send_sem=send_sems.at[p - 1],
                recv_sem=recv_sems.at[p - 1], device_id=peer, device_id_type=pl.DeviceIdType.MESH)
            cp.wait_recv()
            cp.wait_send()
        acc = all_ref[0]
        for s in range(1, N_DEV):
            acc = acc + all_ref[s]
        out_ref[...] = acc

    vm = pl.BlockSpec(memory_space=pltpu.VMEM)
    return pl.pallas_call(
        body, name=name, in_specs=[vm], out_specs=vm,
        out_shape=jax.ShapeDtypeStruct((r, c), F32),
        scratch_shapes=[pltpu.VMEM((N_DEV, r, c), F32), pltpu.SemaphoreType.DMA((N_DEV - 1,)),
                        pltpu.SemaphoreType.DMA((N_DEV - 1,))],
        compiler_params=pltpu.CompilerParams(has_side_effects=True),
    )(buf)


def _unshard_cols(g):
    s, l, r, c = g.shape
    return jnp.transpose(g, (1, 2, 0, 3)).reshape(l, r, s * c)


def kernel(x, gdn_w_in, gdn_conv, gdn_a_log, gdn_dt_bias, gdn_onorm, gdn_w_out, hgrn_w_in, hgrn_lb_logits, hgrn_gnorm, hgrn_w_out, norm_mix, norm_mlp, mlp_w_up, mlp_w_down, norm_final, loss_target, m_gdn_w_in, m_gdn_conv, m_gdn_a_log, m_gdn_dt_bias, m_gdn_onorm, m_gdn_w_out, m_hgrn_w_in, m_hgrn_lb_logits, m_hgrn_gnorm, m_hgrn_w_out, m_norm_mix, m_norm_mlp, m_mlp_w_up, m_mlp_w_down, m_norm_final, v_gdn_w_in, v_gdn_conv, v_gdn_a_log, v_gdn_dt_bias, v_gdn_onorm, v_gdn_w_out, v_hgrn_w_in, v_hgrn_lb_logits, v_hgrn_gnorm, v_hgrn_w_out, v_norm_mix, v_norm_mlp, v_mlp_w_up, v_mlp_w_down, v_norm_final):
    seqs, seq_len, d = x.shape
    n = seqs * seq_len
    me_i = _lin(_mesh_pos())
    x2 = x.reshape(n, d)
    target = loss_target.reshape(n, d)
    n_gdn, n_hgrn = gdn_w_in.shape[0], hgrn_w_in.shape[0]

    r_out, r_down = gdn_w_out.shape[1], mlp_w_down.shape[1]
    c_gin, c_hin, c_up = gdn_w_in.shape[2], hgrn_w_in.shape[2], mlp_w_up.shape[2]

    def gathered(w, pick, land_shape):
        return _Item(w.astype(BF16), land_shape, _whole, pick)

    groups = [[_Item(gdn_conv, (N_DEV,) + gdn_conv.shape, _whole, _slot),
               _Item(hgrn_gnorm, (N_DEV,) + hgrn_gnorm.shape, _whole, _slot)]]
    for i in range(DEPTH):
        j = i // 2
        if i % 2 == 0:
            groups += [[gathered(gdn_w_in[j], _slot, (N_DEV, d, c_gin))],
                       [gathered(gdn_w_out[j], _rows_of(r_out), (N_DEV * r_out, d))]]
        else:
            groups += [[gathered(hgrn_w_in[j], _cols_of(c_hin), (d, N_DEV * c_hin))],
                       [gathered(hgrn_w_out[j], _rows_of(r_out), (N_DEV * r_out, d))]]
        groups += [[gathered(mlp_w_up[i], _cols_of(c_up), (d, N_DEV * c_up))],
                   [gathered(mlp_w_down[i], _rows_of(r_down), (N_DEV * r_down, d))]]
    gather_handles, token = _exchange_start(groups, "gather_start")
    lbs = _lb_fwd(hgrn_lb_logits + token[0:1, 0:1], "lb_fwd")

    def arrived(k, after, name):
        return _exchange_wait(gather_handles[k], after, "gather_wait_" + name)

    saved = []
    w_in, w_ab, w_out, w_up, w_down = ([None] * DEPTH for _ in range(5))
    h = x2
    for i in range(DEPTH):
        j = i // 2
        if i == 0:
            g_conv, g_gnorm = arrived(0, h, "small")
            conv_full = _unshard_cols(g_conv)
            gnorm_full = jnp.transpose(g_gnorm, (1, 0, 2)).reshape(n_hgrn, d)
        if i == 0:
            y = _rms_fwd(h, norm_mix[0:1], "rms_mix_0")
        (w_in[i],) = arrived(1 + 4 * i, y, f"in_{i}")
        if i % 2 == 0:
            w_gin = jnp.transpose(w_in[i], (1, 0, 2)).reshape(d, N_DEV * c_gin)
            w_in[i] = w_gin[:, :GDN_MAIN]
            w_ab[i] = jnp.pad(w_gin[:, GDN_MAIN:], ((0, 0), (0, AB_PAD - 2 * N_HEADS)))
            projm = _mm(y, w_in[i], "nn", [BF16], f"gdn_proj_{i}")
            projab = _mm(y, w_ab[i], "nn", [F32], f"gdn_proj_ab_{i}")
            o2, st_all, conv_y = _gdn_fwd(projm, projab, conv_full[j], gdn_a_log[j:j + 1], gdn_dt_bias[j:j + 1],
                                          gdn_onorm[j:j + 1], seqs, f"gdn_fwd_{i}")
            mix = (projm, projab, conv_y, st_all)
        else:
            proj = _mm(y, w_in[i], "nn", [BF16], f"hgrn_proj_{i}")
            o2, o_raw, st_all = _hgrn_fwd(proj, lbs[i:i + 1], gnorm_full[j:j + 1], seqs, f"hgrn_fwd_{i}")
            mix = (proj, o_raw, st_all)
        (w_out[i],) = arrived(2 + 4 * i, o2, f"out_{i}")
        h1, y2 = _mm(o2, w_out[i], "nn", [F32, BF16], f"mix_out_{i}", epilogue=_ep_residual_norm, extras=(h,),
                     vectors=(norm_mlp[i:i + 1],))
        (w_up[i],) = arrived(3 + 4 * i, y2, f"up_{i}")
        u, act = _mm(y2, w_up[i], "nn", [BF16, BF16], f"mlp_up_{i}",
                     epilogue=lambda acc: (acc, jnp.square(jnp.maximum(acc, 0.0))))
        (w_down[i],) = arrived(4 + 4 * i, act, f"down_{i}")
        saved.append((h, y, mix, o2, h1, y2, u, act))
        if i + 1 < DEPTH:
            h, y = _mm(act, w_down[i], "nn", [F32, BF16], f"mlp_down_{i}", epilogue=_ep_residual_norm, extras=(h1,),
                       vectors=(norm_mix[i + 1:i + 2],))
        else:
            h = _mm(act, w_down[i], "nn", [F32], f"mlp_down_{i}", epilogue=lambda acc, res: (res + acc,),
                    extras=(h1,))

    dh, dh_b, d_nf, sq = _loss_head(h, norm_final.reshape(1, d), target, "loss_head")

    d_nmix, d_nmlp = [None] * DEPTH, [None] * DEPTH
    d_conv, d_alog, d_dtb, d_onorm = [None] * n_gdn, [None] * n_gdn, [None] * n_gdn, [None] * n_gdn
    d_lb = [jnp.zeros((1, d), F32)] * DEPTH
    d_gnorm = [None] * n_hgrn
    mlp_handles, mix_handles = [None] * DEPTH, [None] * DEPTH
    token = None
    for i in reversed(range(DEPTH)):
        j = i // 2
        h_in, y, mix, o2, h1, y2, u, act = saved[i]
        g_down = _mm(act, dh_b, "tn", [BF16], f"g_down_{i}", after=token)
        du = _mm(dh_b, w_down[i], "nt", [BF16], f"d_u_{i}",
                 epilogue=lambda acc, uu: (acc * (2.0 * jnp.maximum(uu.astype(F32), 0.0)),), extras=(u,))
        g_up = _mm(y2, du, "tn", [BF16], f"g_up_{i}")
        mlp_handles[i], token = _exchange_start(
            [[_Item(g_down, (N_DEV, r_down, d), _rows_of(r_down), _slot)],
             [_Item(g_up, (N_DEV, d, c_up), _cols_of(c_up), _slot)]], f"scatter_start_mlp_{i}")
        dh1, dh1_b, d_nmlp[i] = _mm(du, w_up[i], "nt", [F32, BF16], f"d_y2_{i}", epilogue=_ep_norm_bwd,
                                     extras=(h1, dh), vectors=(norm_mlp[i:i + 1],), n_sums=1, after=token)
        g_out = _mm(o2, dh1_b, "tn", [BF16], f"g_out_{i}")
        do2 = _mm(dh1_b, w_out[i], "nt", [BF16], f"d_o2_{i}")
        if i % 2 == 0:
            projm, projab, conv_y, st_all = mix
            dpm, dpab, d_conv[j], d_alog[j], d_dtb[j], d_onorm[j] = _gdn_bwd(
                projm, projab, conv_y, conv_full[j], gdn_a_log[j:j + 1], gdn_dt_bias[j:j + 1], gdn_onorm[j:j + 1],
                st_all, do2, seqs, f"gdn_bwd_{i}")
            g_main = _mm(y, dpm, "tn", [BF16], f"g_in_{i}")
            g_ab = _mm(y, dpab, "tn", [BF16], f"g_in_ab_{i}")
            g_in = jnp.concatenate([g_main, g_ab[:, :2 * N_HEADS]], axis=1)
            g_in = jnp.transpose(g_in.reshape(d, N_DEV, c_gin), (1, 0, 2))
            in_item = _Item(g_in, (N_DEV, d, c_gin), _slot, _slot)
            dy_ab = _mm(dpab, w_ab[i], "nt", [F32], f"d_y_ab_{i}")
            dp, dy_extras = dpm, (dy_ab, h_in, dh1)
            dy_epilogue = lambda acc, e, xx, dres, w: _ep_norm_bwd(acc + e, xx, dres, w)
        else:
            proj, o_raw, st_all = mix
            dp, d_lb[i], d_gnorm[j] = _hgrn_bwd(proj, lbs[i:i + 1], gnorm_full[j:j + 1], st_all, o_raw, do2,
                                               seqs, f"hgrn_bwd_{i}")
            g_in = _mm(y, dp, "tn", [BF16], f"g_in_{i}")
            in_item = _Item(g_in, (N_DEV, d, c_hin), _cols_of(c_hin), _slot)
            dy_extras, dy_epilogue = (h_in, dh1), _ep_norm_bwd
        mix_handles[i], token = _exchange_start(
            [[_Item(g_out, (N_DEV, r_out, d), _rows_of(r_out), _slot)], [in_item]], f"scatter_start_mix_{i}")
        dh, dh_b, d_nmix[i] = _mm(dp, w_in[i], "nt", [F32, BF16], f"d_y_{i}", epilogue=dy_epilogue, extras=dy_extras,
                                  vectors=(norm_mix[i:i + 1],), n_sums=1, after=token)
        token = None
    grad_x = dh.reshape(x.shape)

    def landed(handles, k, layers, after, name):
        return [_exchange_wait(handles[i][k], after, f"scatter_wait_{name}_{i}")[0] for i in layers]

    every, even, odd = range(DEPTH), range(0, DEPTH, 2), range(1, DEPTH, 2)
    upd = {}
    upd["mlp_w_down"] = _adamw_slots(mlp_w_down, landed(mlp_handles, 0, every, dh, "down"), m_mlp_w_down,
                                     v_mlp_w_down, "adamw_mlp_w_down")
    upd["mlp_w_up"] = _adamw_slots(mlp_w_up, landed(mlp_handles, 1, every, upd["mlp_w_down"][1], "up"), m_mlp_w_up,
                                   v_mlp_w_up, "adamw_mlp_w_up")
    upd["hgrn_w_out"] = _adamw_slots(hgrn_w_out, landed(mix_handles, 0, odd, upd["mlp_w_up"][1], "out"),
                                     m_hgrn_w_out, v_hgrn_w_out, "adamw_hgrn_w_out")
    upd["hgrn_w_in"] = _adamw_slots(hgrn_w_in, landed(mix_handles, 1, odd, upd["hgrn_w_out"][1], "in"), m_hgrn_w_in,
                                    v_hgrn_w_in, "adamw_hgrn_w_in")
    upd["gdn_w_out"] = _adamw_slots(gdn_w_out, landed(mix_handles, 0, even, upd["hgrn_w_in"][1], "out"),
                                    m_gdn_w_out, v_gdn_w_out, "adamw_gdn_w_out")
    upd["gdn_w_in"] = _adamw_slots(gdn_w_in, landed(mix_handles, 1, even, upd["gdn_w_out"][1], "in"), m_gdn_w_in,
                                   v_gdn_w_in, "adamw_gdn_w_in")

    def update(name, w, g, m, v):
        shape = w.shape
        c = shape[-1]
        res = _adamw(w.reshape(-1, c), g.reshape(-1, c), m.reshape(-1, c), v.reshape(-1, c), "adamw_" + name)
        return [g.reshape(shape)] + [o.reshape(shape) for o in res]

    dlb_rows = jnp.concatenate(d_lb, axis=0)
    tail = jnp.concatenate(
        [jnp.concatenate(d_onorm, axis=1), jnp.concatenate(d_alog, axis=1), jnp.concatenate(d_dtb, axis=1)], axis=1)
    tail = jnp.pad(tail, ((0, 0), (0, d - tail.shape[1])))
    conv_rows = jnp.stack(d_conv).reshape(-1, d)
    packed = jnp.concatenate(
        [jnp.concatenate(d_nmix, axis=0), jnp.concatenate(d_nmlp, axis=0), d_nf, sq, dlb_rows,
         jnp.concatenate(d_gnorm, axis=0), tail, conv_rows], axis=0)
    pad_rows = (-packed.shape[0]) % 8
    packed = jnp.pad(packed, ((0, pad_rows), (0, 0)))
    tot = _all_reduce_small(packed, "reduce_small")
    r0 = 0
    g_nmix = tot[r0:r0 + DEPTH]; r0 += DEPTH
    g_nmlp = tot[r0:r0 + DEPTH]; r0 += DEPTH
    g_nf = tot[r0]; r0 += 1
    loss = tot[r0, 0]; r0 += 1
    g_lb = _lb_bwd(hgrn_lb_logits, tot[r0:r0 + DEPTH], "lb_bwd"); r0 += DEPTH
    g_gnorm_full = tot[r0:r0 + n_hgrn]; r0 += n_hgrn
    t_row = tot[r0]; r0 += 1
    g_conv_full = tot[r0:r0 + n_gdn * CONV_K * 3].reshape(n_gdn, CONV_K, 3 * d)
    g_onorm = t_row[0:n_gdn * HEAD_DIM].reshape(n_gdn, HEAD_DIM)
    o1 = n_gdn * HEAD_DIM
    g_alog = t_row[o1:o1 + n_gdn * N_HEADS].reshape(n_gdn, N_HEADS)
    g_dtb = t_row[o1 + n_gdn * N_HEADS:o1 + 2 * n_gdn * N_HEADS].reshape(n_gdn, N_HEADS)
    c_gn, c_cv = hgrn_gnorm.shape[1], gdn_conv.shape[2]
    g_gnorm = lax.dynamic_slice_in_dim(g_gnorm_full, me_i * c_gn, c_gn, axis=1)
    g_conv = lax.dynamic_slice_in_dim(g_conv_full, me_i * c_cv, c_cv, axis=2)

    upd["gdn_conv"] = update("gdn_conv", gdn_conv, g_conv, m_gdn_conv, v_gdn_conv)
    upd["gdn_a_log"] = update("gdn_a_log", gdn_a_log, g_alog, m_gdn_a_log, v_gdn_a_log)
    upd["gdn_dt_bias"] = update("gdn_dt_bias", gdn_dt_bias, g_dtb, m_gdn_dt_bias, v_gdn_dt_bias)
    upd["gdn_onorm"] = update("gdn_onorm", gdn_onorm, g_onorm, m_gdn_onorm, v_gdn_onorm)
    upd["hgrn_lb_logits"] = update("hgrn_lb_logits", hgrn_lb_logits, g_lb, m_hgrn_lb_logits, v_hgrn_lb_logits)
    upd["hgrn_gnorm"] = update("hgrn_gnorm", hgrn_gnorm, g_gnorm, m_hgrn_gnorm, v_hgrn_gnorm)
    upd["norm_mix"] = update("norm_mix", norm_mix, g_nmix, m_norm_mix, v_norm_mix)
    upd["norm_mlp"] = update("norm_mlp", norm_mlp, g_nmlp, m_norm_mlp, v_norm_mlp)
    upd["norm_final"] = update("norm_final", norm_final, g_nf, m_norm_final, v_norm_final)

    order = ["gdn_w_in", "gdn_conv", "gdn_a_log", "gdn_dt_bias", "gdn_onorm", "gdn_w_out", "hgrn_w_in",
             "hgrn_lb_logits", "hgrn_gnorm", "hgrn_w_out", "norm_mix", "norm_mlp", "mlp_w_up", "mlp_w_down",
             "norm_final"]
    outs = [loss, grad_x]
    for k in range(4):
        outs += [upd[name][k] for name in order]
    return tuple(outs)
```

```python
import functools

import numpy as np
import jax
import jax.numpy as jnp
from jax import lax
from jax.experimental import pallas as pl
from jax.experimental.pallas import tpu as pltpu

F32 = jnp.float32
BF16 = jnp.bfloat16

D_MODEL = 1024
N_HEADS = 8
HEAD_DIM = 128
CHUNK = 64
SUB = 16
N_SUB = CHUNK // SUB
CONV_K = 4
HALO = 16
EPS = 1e-6
DEPTH = 4
N_DEV = 8
GDN_MAIN = 4 * D_MODEL
GDN_IN = GDN_MAIN + 2 * N_HEADS
AB_PAD = 128
HEAD_GROUP = 8
LANE_BLOCK = 256
ROW_BLOCK = 16

ADAM_LR = 0.001
ADAM_B1 = 0.9
ADAM_B2 = 0.999
ADAM_EPS = 1e-08
ADAM_WD = 0.01
ADAM_STEP = 10

VMEM_LIMIT = 56 * 1024 * 1024
MM_TILE = 1024
MM_VMEM_BUDGET = 40 * 1024 * 1024

_DIMS = {
    "nn": (((1,), (0,)), ((), ())),
    "nt": (((1,), (1,)), ((), ())),
    "tn": (((0,), (0,)), ((), ())),
}


def _parts(x, n):
    if n == 1 and x.dtype == BF16:
        return [x]
    out = []
    r = x.astype(F32)
    for i in range(n):
        p = r.astype(BF16)
        out.append(p)
        if i + 1 < n:
            r = r - p.astype(F32)
    return out


def _dot_raw(a, b, mode, na, nb):
    ap, bp = _parts(a, na), _parts(b, nb)
    nmax = max(na, nb)
    pairs = [(i, j) for i in range(na) for j in range(nb) if i + j < nmax]
    ka = 0 if mode == "tn" else 1
    kb = 1 if mode == "nt" else 0
    xa = ap[0] if len(pairs) == 1 else jnp.concatenate([ap[i] for i, _ in pairs], axis=ka)
    xb = bp[0] if len(pairs) == 1 else jnp.concatenate([bp[j] for _, j in pairs], axis=kb)
    return lax.dot_general(xa, xb, _DIMS[mode], preferred_element_type=F32)


@functools.partial(jax.custom_vjp, nondiff_argnums=(2, 3, 4))
def _dot(a, b, mode, na, nb):
    return _dot_raw(a, b, mode, na, nb)


def _dot_fwd(a, b, mode, na, nb):
    return _dot_raw(a, b, mode, na, nb), (a, b)


def _dot_bwd(mode, na, nb, res, ct):
    a, b = res
    if mode == "nn":
        da = _dot_raw(ct, b, "nt", 1, 1)
        db = _dot_raw(a, ct, "tn", 1, 1)
    elif mode == "nt":
        da = _dot_raw(ct, b, "nn", 1, 1)
        db = _dot_raw(ct, a, "tn", 1, 1)
    else:
        da = _dot_raw(b, ct, "nt", 1, 1)
        db = _dot_raw(a, ct, "nn", 1, 1)
    return da.astype(a.dtype), db.astype(b.dtype)


_dot.defvjp(_dot_fwd, _dot_bwd)


N_EXACT = 3


@jax.custom_vjp
def _dot01(x, m_wide, m):
    return lax.dot_general(m_wide, jnp.concatenate(_parts(x, N_EXACT), axis=0), _DIMS["nn"], preferred_element_type=F32)


def _dot01_fwd(x, m_wide, m):
    return _dot01(x, m_wide, m), (m_wide, m)


def _dot01_bwd(res, ct):
    m_wide, m = res
    dx = lax.dot_general(m, ct.astype(BF16), _DIMS["tn"], preferred_element_type=F32)
    return dx, jnp.zeros_like(m_wide), jnp.zeros_like(m)


_dot01.defvjp(_dot01_fwd, _dot01_bwd)


def _thrice(m):
    return jnp.concatenate([m] * N_EXACT, axis=1).astype(BF16), m.astype(BF16)


def _iota2(shape, dim):
    return lax.broadcasted_iota(jnp.int32, shape, dim)


def _tril_f32(n):
    return (_iota2((n, n), 0) >= _iota2((n, n), 1)).astype(F32)


def _cumsum_rows(g):
    return _dot(_tril_f32(g.shape[0]), g, "nn", 1, 3)


def _below_block(n, b):
    ri, ci = _iota2((n, n), 0) // b, _iota2((n, n), 1) // b
    return (ri == ci + 1) & (ri % 2 == 1)


def _half_inverses(L):
    n = L.shape[0]
    eye = (_iota2((n, n), 0) == _iota2((n, n), 1)).astype(F32)
    d = eye - jnp.where(_below_block(n, 1), L, 0.0)
    b = 2
    while 2 * b < n:
        e = jnp.where(_below_block(n, b), L, 0.0)
        d = d - _dot_raw(d, _dot_raw(e, d, "nn", 2, 2), "nn", 2, 2)
        b *= 2
    return d, jnp.where(_below_block(n, b), L, 0.0)


def _solve_with(d, e, rhs):
    y = _dot_raw(d, rhs, "nn", 2, 2)
    return y - _dot_raw(d, _dot_raw(e, y, "nn", 2, 2), "nn", 2, 2)


@jax.custom_vjp
def _solve_unit_lower(L, rhs):
    return _solve_with(*_half_inverses(L), rhs)


def _solve_fwd(L, rhs):
    d, e = _half_inverses(L)
    sol = _solve_with(d, e, rhs)
    return sol, (d, e, sol)


def _solve_bwd(res, ct):
    d, e, sol = res
    y = _dot_raw(d, ct - _dot_raw(e, _dot_raw(d, ct, "tn", 2, 2), "tn", 2, 2), "tn", 2, 2)
    return -_dot_raw(y, sol, "nt", 2, 2), y


_solve_unit_lower.defvjp(_solve_fwd, _solve_bwd)


def _softplus(x):
    return jnp.maximum(x, 0.0) + jnp.log1p(jnp.exp(-jnp.abs(x)))


def _rms(x, w):
    return x * lax.rsqrt(jnp.mean(x * x, axis=-1, keepdims=True) + EPS) * w


HG_LEVELS = (32, 16, 8, 4, 2, 1)


def _hg_level_sums():
    i = np.arange(CHUNK)[:, None]
    m = np.arange(CHUNK)[None, :]
    to_row = [(m <= i) & (m // b == i // b) for b in HG_LEVELS]
    to_col = [(m > i) & (m // b == i // b) for b in HG_LEVELS if b > 1]
    return _thrice(jnp.asarray(np.concatenate(to_row + to_col + [m <= i]), F32))


def _hg_level_masks():
    i = np.arange(CHUNK)[:, None]
    j = np.arange(CHUNK)[None, :]
    return jnp.asarray(np.stack([(i // b == j // b + 1) & ((i // b) % 2 == 1) for b in HG_LEVELS]), F32)


def _hg_pre(qraw, f, lb, sums):
    g = jnp.log(lb + (1.0 - lb) * jax.nn.sigmoid(f))
    k = (1.0 - lb) * jax.nn.sigmoid(-f)
    q = jax.nn.silu(qraw) * (HEAD_DIM ** -0.5)
    return q, k, _dot01(g, *sums)


def _hg_head(st, q, k, v, e, masks):
    nl = len(HG_LEVELS)
    eye = (_iota2((CHUNK, CHUNK), 0) == _iota2((CHUNK, CHUNK), 1)).astype(F32)
    a = eye * jnp.sum(q * k, axis=-1, keepdims=True)
    for l, b in enumerate(HG_LEVELS):
        rows = q * jnp.exp(e[l * CHUNK:(l + 1) * CHUNK])
        cols = k * jnp.exp(e[(nl + l) * CHUNK:(nl + l + 1) * CHUNK]) if b > 1 else k
        a = a + masks[l] * _dot(rows, cols, "nt", 1, 1)
    gc = e[(2 * nl - 1) * CHUNK:2 * nl * CHUNK]
    o = _dot(a, v, "nn", 1, 1) + _dot(q * jnp.exp(gc), st, "nt", 1, 1)
    g_last = gc[CHUNK - 1:CHUNK]
    st_new = st * jnp.exp(g_last) + _dot(v, k * jnp.exp(g_last - gc), "tn", 1, 1)
    return o, st_new


_HG_HEADS = jax.vmap(_hg_head, in_axes=(0, 0, 0, 0, 0, None))


def _hg_post(o, gate, gw):
    return _rms(o, gw) * jax.nn.silu(gate)


def _gd_conv(xp, cw):
    off = HALO - (CONV_K - 1)
    y = cw[0:1] * xp[off:off + CHUNK]
    for kk in range(1, CONV_K):
        y = y + cw[kk:kk + 1] * xp[off + kk:off + kk + CHUNK]
    return y


def _gd_conv_bwd(xp, cw, y, dc):
    off = HALO - (CONV_K - 1)
    sig = jax.nn.sigmoid(y)
    dy = dc * (sig * (1.0 + y * (1.0 - sig)))
    dxp, dcw = None, []
    for kk in range(CONV_K):
        moved = jnp.pad(dy, ((off + kk, HALO - off - kk), (0, 0)))
        term = cw[kk:kk + 1] * moved
        dxp = term if dxp is None else dxp + term
        dcw.append(jnp.sum(xp * moved, axis=0, keepdims=True))
    return dxp, jnp.concatenate(dcw, axis=0)


def _gd_gates(a, b, alog, dtb):
    beta = jax.nn.sigmoid(b)
    g = -jnp.exp(alog) * _softplus(a + dtb)
    expand = (_iota2((N_HEADS, D_MODEL), 1) // HEAD_DIM == _iota2((N_HEADS, D_MODEL), 0)).astype(F32)
    g_x = _dot(g, expand, "nn", 3, 1)
    after = (_iota2((CHUNK, D_MODEL), 0) > _iota2((CHUNK, D_MODEL), 1) % HEAD_DIM).astype(F32)
    sums = _dot01(jnp.concatenate([g_x, g_x * after], axis=1), *_thrice(_tril_f32(CHUNK)))
    return _dot(beta, expand, "nn", 3, 1), sums


def _gd_head(st, q, k, v, beta, gc, diff, gate, onw):
    q = q * lax.rsqrt(jnp.sum(q * q, axis=-1, keepdims=True) + EPS) * (HEAD_DIM ** -0.5)
    k = k * lax.rsqrt(jnp.sum(k * k, axis=-1, keepdims=True) + EPS)
    ri = _iota2((CHUNK, CHUNK), 0)
    ci = _iota2((CHUNK, CHUNK), 1)
    decay = jnp.exp(jnp.where(ri >= ci, diff[:, 0:CHUNK], -jnp.inf))
    kb = k * beta
    egc = jnp.exp(gc)
    L = jnp.where(ri > ci, _dot(kb, k, "nt", 1, 1) * decay, 0.0)
    sol = _solve_unit_lower(L, jnp.concatenate([v * beta, kb * egc], axis=1))
    u = sol[:, 0:HEAD_DIM]
    w = sol[:, HEAD_DIM:2 * HEAD_DIM]
    a_qk = jnp.where(ri >= ci, _dot(q, k, "nt", 1, 1) * decay, 0.0)
    g_last = gc[CHUNK - 1:CHUNK]
    v_new = u - _dot(w, st, "nt", 1, 1)
    o = _dot(q * egc, st, "nt", 1, 1) + _dot(a_qk, v_new, "nn", 1, 1)
    st_new = st * jnp.exp(g_last) + _dot(v_new, k * jnp.exp(g_last - gc), "tn", 1, 1)
    return _rms(o, onw) * jax.nn.silu(gate), st_new


def _params(*sem):
    return pltpu.CompilerParams(dimension_semantics=sem, vmem_limit_bytes=VMEM_LIMIT)


def _tile(n, pref):
    t = min(n, pref)
    assert n % t == 0, (n, pref)
    return t


def _mm_tiles(m, n, k, a_size, b_size, tile_sizes):
    tm, tn, tk = _tile(m, MM_TILE), _tile(n, MM_TILE), k

    def need(tm, tn, tk):
        acc = 4 * tm * tn * (2 if tk < k else 1)
        return 2 * (tm * tk * a_size + tk * tn * b_size + tm * tn * sum(tile_sizes)) + acc

    while need(tm, tn, tk) > MM_VMEM_BUDGET:
        if tk > 2048 or (tk > 512 and tm <= 512):
            tk //= 2
        else:
            tm //= 2
    return tm, tn, tk


def _mm(a, b, mode, out_dtypes, name, epilogue=None, extras=(), vectors=(), n_sums=0, after=None):
    if mode == "nn":
        (m, k), (k2, n) = a.shape, b.shape
    elif mode == "nt":
        (m, k), (n, k2) = a.shape, b.shape
    else:
        (k, m), (k2, n) = a.shape, b.shape
    assert k == k2, (a.shape, b.shape, mode)
    tm, tn, tk = _mm_tiles(m, n, k, a.dtype.itemsize, b.dtype.itemsize,
                           [e.dtype.itemsize for e in extras] + [jnp.dtype(dt).itemsize for dt in out_dtypes])
    nk = k // tk
    assert not (vectors or n_sums) or tn == n, "whole-row epilogues need the result tile to span the rows"
    ne, no, nafter = len(extras) + len(vectors), len(out_dtypes), int(after is not None)
    if epilogue is None:
        epilogue = lambda acc: (acc,)

    def body(*refs):
        a_ref, b_ref = refs[0], refs[1]
        ex = refs[2:2 + ne]
        outs = refs[2 + ne + nafter:2 + ne + nafter + no]
        sums = refs[2 + ne + nafter + no:2 + ne + nafter + no + n_sums]
        part = lax.dot_general(a_ref[...].astype(BF16), b_ref[...].astype(BF16), _DIMS[mode],
                               preferred_element_type=F32)

        def finish(acc):
            vals = epilogue(acc, *[e[...] for e in ex])
            for o_ref, val in zip(outs, vals[:no]):
                o_ref[...] = val.astype(o_ref.dtype)
            for s_ref, val in zip(sums, vals[no:]):
                @pl.when(pl.program_id(0) == 0)
                def _(s_ref=s_ref, val=val):
                    s_ref[...] = val

                @pl.when(pl.program_id(0) > 0)
                def _(s_ref=s_ref, val=val):
                    s_ref[...] += val

        if nk == 1:
            finish(part)
        else:
            acc_ref = refs[-1]
            kk = pl.program_id(2)

            @pl.when(kk == 0)
            def _():
                acc_ref[...] = part

            @pl.when(kk > 0)
            def _():
                acc_ref[...] += part

            @pl.when(kk == nk - 1)
            def _():
                finish(acc_ref[...])

    if mode == "tn":
        a_spec = pl.BlockSpec((tk, tm), lambda i, j, kk: (kk, i))
    else:
        a_spec = pl.BlockSpec((tm, tk), lambda i, j, kk: (i, kk))
    if mode == "nt":
        b_spec = pl.BlockSpec((tn, tk), lambda i, j, kk: (j, kk))
    else:
        b_spec = pl.BlockSpec((tk, tn), lambda i, j, kk: (kk, j))
    o_spec = pl.BlockSpec((tm, tn), lambda i, j, kk: (i, j))
    v_spec = pl.BlockSpec((1, tn), lambda i, j, kk: (0, j))
    res = pl.pallas_call(
        body,
        name=name,
        grid=(m // tm, n // tn, nk),
        in_specs=([a_spec, b_spec] + [o_spec] * len(extras) + [v_spec] * len(vectors)
                  + [pl.BlockSpec(memory_space=pl.ANY)] * nafter),
        out_specs=[o_spec] * no + [v_spec] * n_sums,
        out_shape=[jax.ShapeDtypeStruct((m, n), dt) for dt in out_dtypes] + [jax.ShapeDtypeStruct((1, n), F32)] * n_sums,
        scratch_shapes=[pltpu.VMEM((tm, tn), F32)] if nk > 1 else [],
        compiler_params=_params(*(("arbitrary",) * 3 if n_sums else ("parallel", "parallel", "arbitrary"))),
    )(a, b, *extras, *vectors, *([after] if nafter else []))
    return res[0] if no + n_sums == 1 else res


def _ep_residual_norm(acc, res, w):
    h = res + acc
    return h, _rms(h, w)


def _ep_norm_bwd(acc, x, dres, w):
    _, vjp = jax.vjp(_rms, x, w)
    dx, dw = vjp(acc)
    dx = dres + dx
    return dx, dx, dw


def _rms_fwd(x, w, name, tm=512):
    n, d = x.shape
    tm = _tile(n, tm)

    def body(x_ref, w_ref, y_ref):
        y_ref[...] = _rms(x_ref[...], w_ref[...]).astype(y_ref.dtype)

    return pl.pallas_call(
        body, name=name, grid=(n // tm,),
        in_specs=[pl.BlockSpec((tm, d), lambda i: (i, 0)), pl.BlockSpec((1, d), lambda i: (0, 0))],
        out_specs=pl.BlockSpec((tm, d), lambda i: (i, 0)),
        out_shape=jax.ShapeDtypeStruct((n, d), BF16),
        compiler_params=_params("arbitrary"),
    )(x, w)


def _loss_head(h, w, target, name, tm=512):
    n, d = h.shape
    tm = _tile(n, tm)

    def body(h_ref, w_ref, t_ref, dh_ref, dhb_ref, dw_ref, sq_ref):
        y, vjp = jax.vjp(_rms, h_ref[...], w_ref[...])
        err = y - t_ref[...]
        dh, dw = vjp(err * (1.0 / d))
        dh_ref[...] = dh
        dhb_ref[...] = dh.astype(dhb_ref.dtype)
        sq = jnp.sum(err * err, axis=0, keepdims=True)

        @pl.when(pl.program_id(0) == 0)
        def _():
            dw_ref[...] = dw
            sq_ref[...] = sq

        @pl.when(pl.program_id(0) > 0)
        def _():
            dw_ref[...] += dw
            sq_ref[...] += sq

        @pl.when(pl.program_id(0) == n // tm - 1)
        def _():
            total = jnp.sum(sq_ref[...], axis=1, keepdims=True) * (0.5 / d)
            sq_ref[...] = jnp.broadcast_to(total, sq_ref.shape)

    row = pl.BlockSpec((tm, d), lambda i: (i, 0))
    vec = pl.BlockSpec((1, d), lambda i: (0, 0))
    return pl.pallas_call(
        body, name=name, grid=(n // tm,),
        in_specs=[row, vec, row],
        out_specs=[row, row, vec, vec],
        out_shape=[jax.ShapeDtypeStruct((n, d), F32), jax.ShapeDtypeStruct((n, d), BF16),
                   jax.ShapeDtypeStruct((1, d), F32), jax.ShapeDtypeStruct((1, d), F32)],
        compiler_params=_params("arbitrary"),
    )(h, w, target)


def _lower_bounds(logits):
    sm = jax.nn.softmax(logits, axis=0)
    rows = [sm[0:1] * 0.0]
    for r in range(1, DEPTH):
        rows.append(rows[-1] + sm[r:r + 1])
    return jnp.concatenate(rows, axis=0)


def _lb_fwd(logits, name):
    def body(l_ref, o_ref):
        o_ref[...] = _lower_bounds(l_ref[...])

    return pl.pallas_call(body, name=name, out_shape=jax.ShapeDtypeStruct(logits.shape, F32))(logits)


def _lb_bwd(logits, dlb, name):
    def body(l_ref, d_ref, o_ref):
        _, vjp = jax.vjp(_lower_bounds, l_ref[...])
        (o_ref[...],) = vjp(d_ref[...])

    return pl.pallas_call(body, name=name, out_shape=jax.ShapeDtypeStruct(logits.shape, F32))(logits, dlb)


def _head_slice(h):
    if isinstance(h, int):
        return pl.ds(h * HEAD_DIM, HEAD_DIM)
    return pl.ds(pl.multiple_of(h * HEAD_DIM, HEAD_DIM), HEAD_DIM)


def _head_groups(group_body):
    if HEAD_GROUP == N_HEADS:
        group_body(list(range(N_HEADS)))
        return

    def trip(i, carry):
        group_body([i * HEAD_GROUP + t for t in range(HEAD_GROUP)])
        return carry

    lax.fori_loop(0, N_HEADS // HEAD_GROUP, trip, 0)


def _stack_heads(ref, hs, first=0):
    return jnp.stack([ref[:, _head_slice(h + first)] for h in hs])


def _unstack_heads(ref, hs, val, first=0):
    for t, h in enumerate(hs):
        ref[:, _head_slice(h + first)] = val[t].astype(ref.dtype)


_GD_HEADS = jax.vmap(_gd_head, in_axes=(0, 0, 0, 0, 0, 0, 0, 0, None))


def _hgrn_fwd(proj, lb, gw, seqs, name):
    n = proj.shape[0]
    nc = n // seqs // CHUNK
    d = D_MODEL

    sums, masks = _hg_level_sums(), _hg_level_masks()

    def body(p_ref, lb_ref, gw_ref, sums_wide_ref, sums_once_ref, masks_ref, o2_ref, o_ref, st_all_ref,
             st_sc, q_sc, k_sc, v_sc, e_sc):
        @pl.when(pl.program_id(1) == 0)
        def _():
            st_sc[...] = jnp.zeros_like(st_sc)

        sums_refs = (sums_wide_ref, sums_once_ref)
        _lane_blocks(d, functools.partial(_hg_pre_block, p_ref, lb_ref, sums_refs, q_sc, k_sc, v_sc, e_sc))
        st_all_ref[0] = st_sc[...]

        def group(hs):
            sts = pl.ds(hs[0], len(hs))
            o, st_new = _HG_HEADS(st_sc[sts], *[_stack_heads(r, hs) for r in (q_sc, k_sc, v_sc, e_sc)], masks_ref[...])
            _unstack_heads(o_ref, hs, o)
            st_sc[sts] = st_new

        _head_groups(group)

        def post(rows):
            gate = p_ref[rows, 3 * d:4 * d].astype(F32)
            o2_ref[rows, :] = _hg_post(o_ref[rows, :], gate, gw_ref[...]).astype(o2_ref.dtype)

        _row_blocks(CHUNK, post)

    idx = lambda b, c: (b * nc + c, 0)
    vec = pl.BlockSpec((1, d), lambda b, c: (0, 0))
    act = pl.BlockSpec((CHUNK, d), idx)
    return pl.pallas_call(
        body, name=name, grid=(seqs, nc),
        in_specs=[pl.BlockSpec((CHUNK, 4 * d), idx), vec, vec] + [pl.BlockSpec(s.shape, lambda b, c: (0, 0)) for s in sums]
        + [pl.BlockSpec(masks.shape, lambda b, c: (0, 0, 0))],
        out_specs=[act, act, pl.BlockSpec((1, N_HEADS, HEAD_DIM, HEAD_DIM), lambda b, c: (b * nc + c, 0, 0, 0))],
        out_shape=[jax.ShapeDtypeStruct((n, d), BF16), jax.ShapeDtypeStruct((n, d), F32),
                   jax.ShapeDtypeStruct((n // CHUNK, N_HEADS, HEAD_DIM, HEAD_DIM), F32)],
        scratch_shapes=[pltpu.VMEM((N_HEADS, HEAD_DIM, HEAD_DIM), F32)] + [pltpu.VMEM((CHUNK, d), F32)] * 3
        + [pltpu.VMEM((sums[0].shape[0], d), F32)],
        compiler_params=_params("arbitrary", "arbitrary"),
    )(proj, lb, gw, *sums, masks)


def _hgrn_bwd(proj, lb, gw, st_all, o, do2, seqs, name):
    n = proj.shape[0]
    nc = n // seqs // CHUNK
    d = D_MODEL

    sums, masks = _hg_level_sums(), _hg_level_masks()

    def body(p_ref, lb_ref, gw_ref, sums_wide_ref, sums_once_ref, masks_ref, st_all_ref, o_ref, do2_ref,
             dp_ref, dlb_ref, dgw_ref,
             dst_sc, q_sc, k_sc, v_sc, e_sc, do_sc, dq_sc, dk_sc, dv_sc, de_sc, dgw_sc):
        first = (pl.program_id(0) == 0) & (pl.program_id(1) == 0)

        @pl.when(pl.program_id(1) == 0)
        def _():
            dst_sc[...] = jnp.zeros_like(dst_sc)

        sums_refs = (sums_wide_ref, sums_once_ref)
        _lane_blocks(d, functools.partial(_hg_pre_block, p_ref, lb_ref, sums_refs, q_sc, k_sc, v_sc, e_sc))
        dgw_sc[...] = jnp.zeros_like(dgw_sc)

        def post_bwd(rows):
            _, vjp = jax.vjp(_hg_post, o_ref[rows, :], p_ref[rows, 3 * d:4 * d].astype(F32), gw_ref[...])
            do_sc[rows, :], dgate, dgw = vjp(do2_ref[rows, :].astype(F32))
            dp_ref[rows, 3 * d:4 * d] = dgate.astype(dp_ref.dtype)
            dgw_sc[...] += dgw

        _row_blocks(CHUNK, post_bwd)

        def group(hs):
            sts = pl.ds(hs[0], len(hs))
            level_masks = masks_ref[...]
            _, vjp = jax.vjp(lambda *a: _HG_HEADS(*a, level_masks), st_all_ref[0, sts],
                             *[_stack_heads(r, hs) for r in (q_sc, k_sc, v_sc, e_sc)])
            grads = vjp((_stack_heads(do_sc, hs), dst_sc[sts]))
            dst_sc[sts] = grads[0]
            for r, val in zip((dq_sc, dk_sc, dv_sc, de_sc), grads[1:]):
                _unstack_heads(r, hs, val)

        _head_groups(group)

        def pre_bwd(at):
            sl = at()
            level_sums = (sums_wide_ref[...], sums_once_ref[...])
            _, vjp = jax.vjp(lambda qraw, f, lb: _hg_pre(qraw, f, lb, level_sums), p_ref[:, sl].astype(F32),
                             p_ref[:, at(d)].astype(F32), lb_ref[:, sl])
            dqraw, df, dlb = vjp((dq_sc[:, sl], dk_sc[:, sl], de_sc[:, sl]))
            dp_ref[:, sl] = dqraw.astype(dp_ref.dtype)
            dp_ref[:, at(d)] = df.astype(dp_ref.dtype)
            dp_ref[:, at(2 * d)] = dv_sc[:, sl].astype(dp_ref.dtype)

            @pl.when(first)
            def _():
                dlb_ref[:, sl] = dlb

            @pl.when(jnp.logical_not(first))
            def _():
                dlb_ref[:, sl] += dlb

        _lane_blocks(d, pre_bwd)

        @pl.when(first)
        def _():
            dgw_ref[...] = dgw_sc[...]

        @pl.when(jnp.logical_not(first))
        def _():
            dgw_ref[...] += dgw_sc[...]

    idx = lambda b, c: (b * nc + nc - 1 - c, 0)
    vec = pl.BlockSpec((1, d), lambda b, c: (0, 0))
    act = pl.BlockSpec((CHUNK, d), idx)
    wide = pl.BlockSpec((CHUNK, 4 * d), idx)
    return pl.pallas_call(
        body, name=name, grid=(seqs, nc),
        in_specs=[wide, vec, vec] + [pl.BlockSpec(s.shape, lambda b, c: (0, 0)) for s in sums] + [
                  pl.BlockSpec(masks.shape, lambda b, c: (0, 0, 0)),
                  pl.BlockSpec((1, N_HEADS, HEAD_DIM, HEAD_DIM), lambda b, c: (b * nc + nc - 1 - c, 0, 0, 0)),
                  act, act],
        out_specs=[wide, vec, vec],
        out_shape=[jax.ShapeDtypeStruct((n, 4 * d), BF16), jax.ShapeDtypeStruct((1, d), F32),
                   jax.ShapeDtypeStruct((1, d), F32)],
        scratch_shapes=[pltpu.VMEM((N_HEADS, HEAD_DIM, HEAD_DIM), F32)]
        + [pltpu.VMEM((CHUNK, d), F32)] * 3 + [pltpu.VMEM((sums[0].shape[0], d), F32)]
        + [pltpu.VMEM((CHUNK, d), F32)] * 4 + [pltpu.VMEM((sums[0].shape[0], d), F32), pltpu.VMEM((1, d), F32)],
        compiler_params=_params("arbitrary", "arbitrary"),
    )(proj, lb, gw, *sums, masks, st_all, o, do2)


def _lane_blocks(width, block_body):
    def trip(j, carry):
        block_body(lambda base=0: pl.ds(pl.multiple_of(j * LANE_BLOCK + base, LANE_BLOCK), LANE_BLOCK))
        return carry

    lax.fori_loop(0, width // LANE_BLOCK, trip, 0)


def _row_blocks(rows, block_body):
    def trip(j, carry):
        block_body(pl.ds(pl.multiple_of(j * ROW_BLOCK, ROW_BLOCK), ROW_BLOCK))
        return carry

    lax.fori_loop(0, rows // ROW_BLOCK, trip, 0)


def _hg_pre_block(p_ref, lb_ref, sums_refs, q_sc, k_sc, v_sc, e_sc, at):
    sl = at()
    q_sc[:, sl], k_sc[:, sl], e_sc[:, sl] = _hg_pre(
        p_ref[:, sl].astype(F32), p_ref[:, at(D_MODEL)].astype(F32), lb_ref[:, sl], [r[...] for r in sums_refs])
    v_sc[:, sl] = p_ref[:, at(2 * D_MODEL)].astype(F32)


def _gd_xp(halo_ref, p_ref, sl, first_chunk):
    halo = jnp.where(first_chunk, 0.0, halo_ref[:, sl].astype(F32))
    return jnp.concatenate([halo, p_ref[:, sl].astype(F32)], axis=0)


def _gdn_fwd(projm, projab, cw, alog, dtb, onw, seqs, name):
    n = projm.shape[0]
    nc = n // seqs // CHUNK
    d = D_MODEL
    per_halo = CHUNK // HALO

    def body(p_ref, halo_ref, ab_ref, cw_ref, alog_ref, dtb_ref, onw_ref, o2_ref, st_all_ref, y_ref,
             st_sc, c_sc, beta_sc, g_sc):
        @pl.when(pl.program_id(1) == 0)
        def _():
            st_sc[...] = jnp.zeros_like(st_sc)

        def conv(at):
            sl = at()
            y = _gd_conv(_gd_xp(halo_ref, p_ref, sl, pl.program_id(1) == 0), cw_ref[:, sl])
            y_ref[:, sl] = y
            c_sc[:, sl] = jax.nn.silu(y)

        _lane_blocks(3 * d, conv)
        beta_sc[...], g_sc[...] = _gd_gates(ab_ref[:, 0:N_HEADS], ab_ref[:, N_HEADS:2 * N_HEADS], alog_ref[...],
                                            dtb_ref[...])
        st_all_ref[0] = st_sc[...]

        def group(hs):
            sts = pl.ds(hs[0], len(hs))
            o2, st_new = _GD_HEADS(
                st_sc[sts], _stack_heads(c_sc, hs), _stack_heads(c_sc, hs, N_HEADS), _stack_heads(c_sc, hs, 2 * N_HEADS),
                _stack_heads(beta_sc, hs), _stack_heads(g_sc, hs), _stack_heads(g_sc, hs, N_HEADS),
                _stack_heads(p_ref, hs, 3 * N_HEADS).astype(F32), onw_ref[...])
            _unstack_heads(o2_ref, hs, o2)
            st_sc[sts] = st_new

        _head_groups(group)

    idx = lambda b, c: (b * nc + c, 0)
    const = lambda b, c: (0, 0)
    return pl.pallas_call(
        body, name=name, grid=(seqs, nc),
        in_specs=[pl.BlockSpec((CHUNK, 4 * d), idx),
                  pl.BlockSpec((HALO, 3 * d), lambda b, c: (jnp.maximum((b * nc + c) * per_halo - 1, 0), 0)),
                  pl.BlockSpec((CHUNK, AB_PAD), idx),
                  pl.BlockSpec((CONV_K, 3 * d), const), pl.BlockSpec((1, N_HEADS), const),
                  pl.BlockSpec((1, N_HEADS), const), pl.BlockSpec((1, HEAD_DIM), const)],
        out_specs=[pl.BlockSpec((CHUNK, d), idx),
                   pl.BlockSpec((1, N_HEADS, HEAD_DIM, HEAD_DIM), lambda b, c: (b * nc + c, 0, 0, 0)),
                   pl.BlockSpec((CHUNK, 3 * d), idx)],
        out_shape=[jax.ShapeDtypeStruct((n, d), BF16),
                   jax.ShapeDtypeStruct((n // CHUNK, N_HEADS, HEAD_DIM, HEAD_DIM), F32),
                   jax.ShapeDtypeStruct((n, 3 * d), F32)],
        scratch_shapes=[pltpu.VMEM((N_HEADS, HEAD_DIM, HEAD_DIM), F32), pltpu.VMEM((CHUNK, 3 * d), F32),
                        pltpu.VMEM((CHUNK, d), F32), pltpu.VMEM((CHUNK, 2 * d), F32)],
        compiler_params=_params("arbitrary", "arbitrary"),
    )(projm, projm, projab, cw, alog, dtb, onw)


def _gdn_bwd(projm, projab, conv_y, cw, alog, dtb, onw, st_all, do2, seqs, name):
    n = projm.shape[0]
    nc = n // seqs // CHUNK
    d = D_MODEL
    per_halo = CHUNK // HALO

    def body(p_ref, halo_ref, ab_ref, y_ref, cw_ref, alog_ref, dtb_ref, onw_ref, st_all_ref, do2_ref,
             dp_ref, dab_ref, dcw_ref, dalog_ref, ddtb_ref, donw_ref,
             dst_sc, dhalo_sc, c_sc, beta_sc, g_sc, dc_sc, dbeta_sc, dg_sc, donw_sc):
        step = pl.program_id(1)
        first = (pl.program_id(0) == 0) & (step == 0)

        @pl.when(step == 0)
        def _():
            dst_sc[...] = jnp.zeros_like(dst_sc)
            dhalo_sc[...] = jnp.zeros_like(dhalo_sc)

        donw_sc[...] = jnp.zeros_like(donw_sc)

        def act(at):
            c_sc[:, at()] = jax.nn.silu(y_ref[:, at()])

        _lane_blocks(3 * d, act)
        (beta_sc[...], g_sc[...]), gates_vjp = jax.vjp(
            _gd_gates, ab_ref[:, 0:N_HEADS], ab_ref[:, N_HEADS:2 * N_HEADS], alog_ref[...], dtb_ref[...])

        def group(hs):
            sts = pl.ds(hs[0], len(hs))
            _, vjp = jax.vjp(
                _GD_HEADS, st_all_ref[0, sts], _stack_heads(c_sc, hs), _stack_heads(c_sc, hs, N_HEADS),
                _stack_heads(c_sc, hs, 2 * N_HEADS), _stack_heads(beta_sc, hs), _stack_heads(g_sc, hs),
                _stack_heads(g_sc, hs, N_HEADS), _stack_heads(p_ref, hs, 3 * N_HEADS).astype(F32), onw_ref[...])
            dst, dq, dk, dv, dbeta, dg, ddiff, dgate, donw = vjp(
                (_stack_heads(do2_ref, hs).astype(F32), dst_sc[sts]))
            _unstack_heads(dg_sc, hs, ddiff, N_HEADS)
            dst_sc[sts] = dst
            _unstack_heads(dc_sc, hs, dq)
            _unstack_heads(dc_sc, hs, dk, N_HEADS)
            _unstack_heads(dc_sc, hs, dv, 2 * N_HEADS)
            _unstack_heads(dbeta_sc, hs, dbeta)
            _unstack_heads(dg_sc, hs, dg)
            _unstack_heads(dp_ref, hs, dgate, 3 * N_HEADS)
            donw_sc[...] += donw

        _head_groups(group)
        def conv_bwd(at):
            sl = at()
            dxp, dcw = _gd_conv_bwd(_gd_xp(halo_ref, p_ref, sl, step == nc - 1), cw_ref[:, sl], y_ref[:, sl],
                                    dc_sc[:, sl])
            dqkv = jnp.concatenate([dxp[HALO:CHUNK], dxp[CHUNK:HALO + CHUNK] + dhalo_sc[:, sl]], axis=0)
            dp_ref[:, sl] = dqkv.astype(dp_ref.dtype)
            dhalo_sc[:, sl] = dxp[0:HALO]

            @pl.when(first)
            def _():
                dcw_ref[:, sl] = dcw

            @pl.when(jnp.logical_not(first))
            def _():
                dcw_ref[:, sl] += dcw

        _lane_blocks(3 * d, conv_bwd)
        da, db, dalog, ddtb = gates_vjp((dbeta_sc[...], dg_sc[...]))
        dab_ref[...] = jnp.concatenate(
            [da, db, jnp.zeros((CHUNK, AB_PAD - 2 * N_HEADS), F32)], axis=1).astype(dab_ref.dtype)

        @pl.when(first)
        def _():
            dalog_ref[...] = dalog
            ddtb_ref[...] = ddtb
            donw_ref[...] = donw_sc[...]

        @pl.when(jnp.logical_not(first))
        def _():
            dalog_ref[...] += dalog
            ddtb_ref[...] += ddtb
            donw_ref[...] += donw_sc[...]

    rev = lambda b, c: b * nc + nc - 1 - c
    idx = lambda b, c: (rev(b, c), 0)
    const = lambda b, c: (0, 0)
    small = [pl.BlockSpec((CONV_K, 3 * d), const), pl.BlockSpec((1, N_HEADS), const),
             pl.BlockSpec((1, N_HEADS), const), pl.BlockSpec((1, HEAD_DIM), const)]
    return pl.pallas_call(
        body, name=name, grid=(seqs, nc),
        in_specs=[pl.BlockSpec((CHUNK, 4 * d), idx),
                  pl.BlockSpec((HALO, 3 * d), lambda b, c: (jnp.maximum(rev(b, c) * per_halo - 1, 0), 0)),
                  pl.BlockSpec((CHUNK, AB_PAD), idx), pl.BlockSpec((CHUNK, 3 * d), idx)] + small + [
                  pl.BlockSpec((1, N_HEADS, HEAD_DIM, HEAD_DIM), lambda b, c: (rev(b, c), 0, 0, 0)),
                  pl.BlockSpec((CHUNK, d), idx)],
        out_specs=[pl.BlockSpec((CHUNK, 4 * d), idx), pl.BlockSpec((CHUNK, AB_PAD), idx)] + small,
        out_shape=[jax.ShapeDtypeStruct((n, 4 * d), BF16), jax.ShapeDtypeStruct((n, AB_PAD), BF16),
                   jax.ShapeDtypeStruct((CONV_K, 3 * d), F32), jax.ShapeDtypeStruct((1, N_HEADS), F32),
                   jax.ShapeDtypeStruct((1, N_HEADS), F32), jax.ShapeDtypeStruct((1, HEAD_DIM), F32)],
        scratch_shapes=[pltpu.VMEM((N_HEADS, HEAD_DIM, HEAD_DIM), F32), pltpu.VMEM((HALO, 3 * d), F32),
                        pltpu.VMEM((CHUNK, 3 * d), F32), pltpu.VMEM((CHUNK, d), F32), pltpu.VMEM((CHUNK, 2 * d), F32),
                        pltpu.VMEM((CHUNK, 3 * d), F32), pltpu.VMEM((CHUNK, d), F32), pltpu.VMEM((CHUNK, 2 * d), F32),
                        pltpu.VMEM((1, HEAD_DIM), F32)],
        compiler_params=_params("arbitrary", "arbitrary"),
    )(projm, projm, projab, conv_y, cw, alog, dtb, onw, st_all, do2)


def _adam_update(w, g, m, v):
    b1c = 1.0 - ADAM_B1 ** ADAM_STEP
    b2c = 1.0 - ADAM_B2 ** ADAM_STEP
    m_new = ADAM_B1 * m + (1.0 - ADAM_B1) * g
    v_new = ADAM_B2 * v + (1.0 - ADAM_B2) * (g * g)
    delta = -ADAM_LR * ((m_new / b1c) / (jnp.sqrt(v_new / b2c) + ADAM_EPS) + ADAM_WD * w)
    return delta, m_new, v_new


def _adamw(w, g, m, v, name, tr=256):
    r, c = w.shape
    tr = _tile(r, tr)

    def body(w_ref, g_ref, m_ref, v_ref, d_ref, mo_ref, vo_ref):
        d_ref[...], mo_ref[...], vo_ref[...] = _adam_update(w_ref[...], g_ref[...], m_ref[...], v_ref[...])

    blk = pl.BlockSpec((tr, c), lambda i: (i, 0))
    return pl.pallas_call(
        body, name=name, grid=(r // tr,),
        in_specs=[blk] * 4, out_specs=[blk] * 3,
        out_shape=[jax.ShapeDtypeStruct((r, c), F32)] * 3,
        compiler_params=_params("arbitrary"),
    )(w, g, m, v)


def _adamw_slots(w, slot_bufs, m, v, name, tr=256):
    nl, r, c = w.shape
    tr = _tile(r, tr)

    def body(*refs):
        w_ref = refs[0]
        g_refs = refs[1:1 + nl]
        m_ref, v_ref, go_ref, d_ref, mo_ref, vo_ref = refs[1 + nl:]
        for k in range(nl):
            @pl.when(pl.program_id(0) == k)
            def _(k=k):
                g = g_refs[k][0].astype(F32)
                for s in range(1, N_DEV):
                    g = g + g_refs[k][s].astype(F32)
                go_ref[0] = g

        d_ref[0], mo_ref[0], vo_ref[0] = _adam_update(w_ref[0], go_ref[0], m_ref[0], v_ref[0])

    blk = pl.BlockSpec((1, tr, c), lambda l, i: (l, i, 0))
    g_specs = [pl.BlockSpec((N_DEV, tr, c), lambda l, i, k=k: (0, jnp.where(l == k, i, 0), 0)) for k in range(nl)]
    return pl.pallas_call(
        body, name=name, grid=(nl, r // tr),
        in_specs=[blk] + g_specs + [blk, blk], out_specs=[blk] * 4,
        out_shape=[jax.ShapeDtypeStruct((nl, r, c), F32)] * 4,
        compiler_params=_params("arbitrary", "arbitrary"),
    )(w, *slot_bufs, m, v)


def _mesh_pos():
    return lax.axis_index("x"), lax.axis_index("y"), lax.axis_index("c")


def _flip(pos, p):
    x, y, c = pos
    return ((1 - x) if p & 4 else x, (1 - y) if p & 2 else y, (1 - c) if p & 1 else c)


def _lin(pos):
    return 4 * pos[0] + 2 * pos[1] + pos[2]


_HBM = pl.BlockSpec(memory_space=pltpu.HBM)
_SEM = pl.BlockSpec(memory_space=pltpu.SEMAPHORE)
_DATAFLOW = pltpu.SideEffectType.DATAFLOW_SIDE_EFFECTING


class _Item:
    def __init__(self, src, land_shape, src_pick, dst_pick):
        self.src, self.land_shape, self.src_pick, self.dst_pick = src, land_shape, src_pick, dst_pick


def _remote_copies(items, src, land, send_sem, recv_sem, me, arriving):
    me_i = _lin(me)
    out = []
    for it, s_ref, l_ref in zip(items, src, land):
        for p in range(1, N_DEV):
            peer = _flip(me, p)
            out.append(pltpu.make_async_remote_copy(
                src_ref=it.src_pick(s_ref, _lin(peer)),
                dst_ref=it.dst_pick(l_ref, _lin(peer) if arriving else me_i),
                send_sem=send_sem, recv_sem=recv_sem, device_id=peer, device_id_type=pl.DeviceIdType.MESH))
    return out


def _own_copies(items, src, land, sem, me):
    me_i = _lin(me)
    return [pltpu.make_async_copy(it.src_pick(s_ref, me_i), it.dst_pick(l_ref, me_i), sem)
            for it, s_ref, l_ref in zip(items, src, land)]


def _exchange_start(groups, name):
    items = [it for g in groups for it in g]
    n, ng = len(items), len(groups)
    first = [sum(len(g) for g in groups[:gi]) for gi in range(ng)]

    def body(*refs):
        src, land = refs[0:n], refs[n:2 * n]
        send_sems, recv_sems = refs[2 * n:2 * n + ng], refs[2 * n + ng:2 * n + 2 * ng]
        token = refs[4 * n + 2 * ng]
        me = _mesh_pos()
        for gi, g in enumerate(groups):
            sl = slice(first[gi], first[gi] + len(g))
            for cp in _remote_copies(g, src[sl], land[sl], send_sems[gi], recv_sems[gi], me, arriving=False):
                cp.start()
            for cp in _own_copies(g, src[sl], land[sl], recv_sems[gi], me):
                cp.start()
        token[...] = jnp.zeros_like(token)

    srcs = [pltpu.with_memory_space_constraint(it.src, pltpu.HBM) for it in items]
    lands = [pltpu.with_memory_space_constraint(lax.empty(it.land_shape, it.src.dtype), pltpu.HBM) for it in items]
    res = pl.pallas_call(
        body, name=name,
        out_shape=([pltpu.SemaphoreType.DMA(())] * (2 * ng)
                   + [pltpu.HBM(it.src.shape, it.src.dtype) for it in items]
                   + [pltpu.HBM(it.land_shape, it.src.dtype) for it in items]
                   + [jax.ShapeDtypeStruct((8, 128), F32)]),
        in_specs=[_HBM] * (2 * n),
        out_specs=[_SEM] * (2 * ng) + [_HBM] * (2 * n) + [pl.BlockSpec(memory_space=pltpu.VMEM)],
        input_output_aliases={i: 2 * ng + i for i in range(2 * n)},
        compiler_params=pltpu.CompilerParams(has_side_effects=_DATAFLOW),
    )(*srcs, *lands)
    send_sems, recv_sems = res[0:ng], res[ng:2 * ng]
    src_thru, land_thru = res[2 * ng:2 * ng + n], res[2 * ng + n:2 * ng + 2 * n]
    handles = []
    for gi, g in enumerate(groups):
        sl = slice(first[gi], first[gi] + len(g))
        handles.append((g, src_thru[sl], land_thru[sl], send_sems[gi], recv_sems[gi]))
    return handles, res[-1]


def _exchange_wait(handle, after, name):
    items, src_thru, land_thru, send_sem, recv_sem = handle
    k = len(items)

    def body(*refs):
        src, land = refs[0:k], refs[k:2 * k]
        send_ref, recv_ref = refs[2 * k], refs[2 * k + 1]
        for cp in _remote_copies(items, src, land, send_ref, recv_ref, _mesh_pos(), arriving=True):
            cp.wait_send()
            cp.wait_recv()
        for cp in _own_copies(items, src, land, recv_ref, _mesh_pos()):
            cp.wait()

    res = pl.pallas_call(
        body, name=name,
        out_shape=([pltpu.HBM(s.shape, s.dtype) for s in src_thru] + [pltpu.HBM(l.shape, l.dtype) for l in land_thru]),
        in_specs=[_HBM] * (2 * k) + [_SEM, _SEM, pl.BlockSpec(memory_space=pl.ANY)],
        out_specs=[_HBM] * (2 * k),
        input_output_aliases={i: i for i in range(2 * k)},
        compiler_params=pltpu.CompilerParams(has_side_effects=_DATAFLOW),
    )(*src_thru, *land_thru, send_sem, recv_sem, after)
    return res[k:2 * k]


def _whole(ref, i):
    return ref


def _slot(ref, i):
    return ref.at[i]


def _rows_of(r):
    return lambda ref, i: ref.at[pl.ds(pl.multiple_of(i * r, r), r), :]


def _cols_of(c):
    return lambda ref, i: ref.at[:, pl.ds(pl.multiple_of(i * c, c), c)]


def _all_reduce_small(buf, after, name):
    r, c = buf.shape

    def body(src_ref, after_ref, out_ref, all_ref, send_sems, recv_sems):
        me = _mesh_pos()
        me_i = _lin(me)
        all_ref[me_i] = src_ref[...]
        for p in range(1, N_DEV):
            peer = _flip(me, p)
            pltpu.make_async_remote_copy(
                src_ref=src_ref, dst_ref=all_ref.at[me_i], send_sem=send_sems.at[p - 1], recv_sem=recv_sems.at[p - 1],
                device_id=peer, device_id_type=pl.DeviceIdType.MESH).start()
        for p in range(1, N_DEV):
            peer = _flip(me, p)
            cp = pltpu.make_async_remote_copy(
                src_ref=src_ref, dst_ref=all_ref.at[_lin(peer)], send_sem=send_sems.at[p - 1],
                recv_sem=recv_sems.at[p - 1], device_id=peer, device_id_type=pl.DeviceIdType.MESH)
            cp.wait_recv()
            cp.wait_send()
        acc = all_ref[0]
        for s in range(1, N_DEV):
            acc = acc + all_ref[s]
        out_ref[...] = acc

    vm = pl.BlockSpec(memory_space=pltpu.VMEM)
    return pl.pallas_call(
        body, name=name, in_specs=[vm, pl.BlockSpec(memory_space=pl.ANY)], out_specs=vm,
        out_shape=jax.ShapeDtypeStruct((r, c), F32),
        scratch_shapes=[pltpu.VMEM((N_DEV, r, c), F32), pltpu.SemaphoreType.DMA((N_DEV - 1,)),
                        pltpu.SemaphoreType.DMA((N_DEV - 1,))],
        compiler_params=pltpu.CompilerParams(has_side_effects=True),
    )(buf, after)


def _unshard_cols(g):
    s, l, r, c = g.shape
    return jnp.transpose(g, (1, 2, 0, 3)).reshape(l, r, s * c)


def kernel(x, gdn_w_in, gdn_conv, gdn_a_log, gdn_dt_bias, gdn_onorm, gdn_w_out, hgrn_w_in, hgrn_lb_logits, hgrn_gnorm, hgrn_w_out, norm_mix, norm_mlp, mlp_w_up, mlp_w_down, norm_final, loss_target, m_gdn_w_in, m_gdn_conv, m_gdn_a_log, m_gdn_dt_bias, m_gdn_onorm, m_gdn_w_out, m_hgrn_w_in, m_hgrn_lb_logits, m_hgrn_gnorm, m_hgrn_w_out, m_norm_mix, m_norm_mlp, m_mlp_w_up, m_mlp_w_down, m_norm_final, v_gdn_w_in, v_gdn_conv, v_gdn_a_log, v_gdn_dt_bias, v_gdn_onorm, v_gdn_w_out, v_hgrn_w_in, v_hgrn_lb_logits, v_hgrn_gnorm, v_hgrn_w_out, v_norm_mix, v_norm_mlp, v_mlp_w_up, v_mlp_w_down, v_norm_final):
    seqs, seq_len, d = x.shape
    n = seqs * seq_len
    me_i = _lin(_mesh_pos())
    x2 = x.reshape(n, d)
    target = loss_target.reshape(n, d)
    n_gdn, n_hgrn = gdn_w_in.shape[0], hgrn_w_in.shape[0]

    r_out, r_down = gdn_w_out.shape[1], mlp_w_down.shape[1]
    c_gin, c_hin, c_up = gdn_w_in.shape[2], hgrn_w_in.shape[2], mlp_w_up.shape[2]

    def gathered(w, pick, land_shape):
        return _Item(w.astype(BF16), land_shape, _whole, pick)

    groups = [[_Item(gdn_conv, (N_DEV,) + gdn_conv.shape, _whole, _slot),
               _Item(hgrn_gnorm, (N_DEV,) + hgrn_gnorm.shape, _whole, _slot)]]
    for i in range(DEPTH):
        j = i // 2
        if i % 2 == 0:
            groups += [[gathered(gdn_w_in[j], _slot, (N_DEV, d, c_gin))],
                       [gathered(gdn_w_out[j], _rows_of(r_out), (N_DEV * r_out, d))]]
        else:
            groups += [[gathered(hgrn_w_in[j], _cols_of(c_hin), (d, N_DEV * c_hin))],
                       [gathered(hgrn_w_out[j], _rows_of(r_out), (N_DEV * r_out, d))]]
        groups += [[gathered(mlp_w_up[i], _cols_of(c_up), (d, N_DEV * c_up))],
                   [gathered(mlp_w_down[i], _rows_of(r_down), (N_DEV * r_down, d))]]
    gather_handles, token = _exchange_start(groups, "gather_start")
    lbs = _lb_fwd(hgrn_lb_logits + token[0:1, 0:1], "lb_fwd")

    def arrived(k, after, name):
        return _exchange_wait(gather_handles[k], after, "gather_wait_" + name)

    saved = []
    w_in, w_ab, w_out, w_up, w_down = ([None] * DEPTH for _ in range(5))
    h = x2
    for i in range(DEPTH):
        j = i // 2
        if i == 0:
            g_conv, g_gnorm = arrived(0, h, "small")
            conv_full = _unshard_cols(g_conv)
            gnorm_full = jnp.transpose(g_gnorm, (1, 0, 2)).reshape(n_hgrn, d)
        if i == 0:
            y = _rms_fwd(h, norm_mix[0:1], "rms_mix_0")
        (w_in[i],) = arrived(1 + 4 * i, y, f"in_{i}")
        if i % 2 == 0:
            w_gin = jnp.transpose(w_in[i], (1, 0, 2)).reshape(d, N_DEV * c_gin)
            w_in[i] = w_gin[:, :GDN_MAIN]
            w_ab[i] = jnp.pad(w_gin[:, GDN_MAIN:], ((0, 0), (0, AB_PAD - 2 * N_HEADS)))
            projm = _mm(y, w_in[i], "nn", [BF16], f"gdn_proj_{i}")
            projab = _mm(y, w_ab[i], "nn", [F32], f"gdn_proj_ab_{i}")
            o2, st_all, conv_y = _gdn_fwd(projm, projab, conv_full[j], gdn_a_log[j:j + 1], gdn_dt_bias[j:j + 1],
                                          gdn_onorm[j:j + 1], seqs, f"gdn_fwd_{i}")
            mix = (projm, projab, conv_y, st_all)
        else:
            proj = _mm(y, w_in[i], "nn", [BF16], f"hgrn_proj_{i}")
            o2, o_raw, st_all = _hgrn_fwd(proj, lbs[i:i + 1], gnorm_full[j:j + 1], seqs, f"hgrn_fwd_{i}")
            mix = (proj, o_raw, st_all)
        (w_out[i],) = arrived(2 + 4 * i, o2, f"out_{i}")
        h1, y2 = _mm(o2, w_out[i], "nn", [F32, BF16], f"mix_out_{i}", epilogue=_ep_residual_norm, extras=(h,),
                     vectors=(norm_mlp[i:i + 1],))
        (w_up[i],) = arrived(3 + 4 * i, y2, f"up_{i}")
        u, act = _mm(y2, w_up[i], "nn", [BF16, BF16], f"mlp_up_{i}",
                     epilogue=lambda acc: (acc, jnp.square(jnp.maximum(acc, 0.0))))
        (w_down[i],) = arrived(4 + 4 * i, act, f"down_{i}")
        saved.append((h, y, mix, o2, h1, y2, u, act))
        if i + 1 < DEPTH:
            h, y = _mm(act, w_down[i], "nn", [F32, BF16], f"mlp_down_{i}", epilogue=_ep_residual_norm, extras=(h1,),
                       vectors=(norm_mix[i + 1:i + 2],))
        else:
            h = _mm(act, w_down[i], "nn", [F32], f"mlp_down_{i}", epilogue=lambda acc, res: (res + acc,),
                    extras=(h1,))

    dh, dh_b, d_nf, sq = _loss_head(h, norm_final.reshape(1, d), target, "loss_head")

    d_nmix, d_nmlp = [None] * DEPTH, [None] * DEPTH
    d_conv, d_alog, d_dtb, d_onorm = [None] * n_gdn, [None] * n_gdn, [None] * n_gdn, [None] * n_gdn
    d_lb = [jnp.zeros((1, d), F32)] * DEPTH
    d_gnorm = [None] * n_hgrn
    mlp_handles, mix_handles = [None] * DEPTH, [None] * DEPTH
    token = None
    for i in reversed(range(DEPTH)):
        j = i // 2
        h_in, y, mix, o2, h1, y2, u, act = saved[i]
        g_down = _mm(act, dh_b, "tn", [BF16], f"g_down_{i}", after=token)
        du = _mm(dh_b, w_down[i], "nt", [BF16], f"d_u_{i}",
                 epilogue=lambda acc, uu: (acc * (2.0 * jnp.maximum(uu.astype(F32), 0.0)),), extras=(u,))
        g_up = _mm(y2, du, "tn", [BF16], f"g_up_{i}")
        mlp_handles[i], token = _exchange_start(
            [[_Item(g_down, (N_DEV, r_down, d), _rows_of(r_down), _slot)],
             [_Item(g_up, (N_DEV, d, c_up), _cols_of(c_up), _slot)]], f"scatter_start_mlp_{i}")
        dh1, dh1_b, d_nmlp[i] = _mm(du, w_up[i], "nt", [F32, BF16], f"d_y2_{i}", epilogue=_ep_norm_bwd,
                                     extras=(h1, dh), vectors=(norm_mlp[i:i + 1],), n_sums=1, after=token)
        g_out = _mm(o2, dh1_b, "tn", [BF16], f"g_out_{i}")
        do2 = _mm(dh1_b, w_out[i], "nt", [BF16], f"d_o2_{i}")
        if i % 2 == 0:
            projm, projab, conv_y, st_all = mix
            dpm, dpab, d_conv[j], d_alog[j], d_dtb[j], d_onorm[j] = _gdn_bwd(
                projm, projab, conv_y, conv_full[j], gdn_a_log[j:j + 1], gdn_dt_bias[j:j + 1], gdn_onorm[j:j + 1],
                st_all, do2, seqs, f"gdn_bwd_{i}")
            g_main = _mm(y, dpm, "tn", [BF16], f"g_in_{i}")
            g_ab = _mm(y, dpab, "tn", [BF16], f"g_in_ab_{i}")
            g_in = jnp.concatenate([g_main, g_ab[:, :2 * N_HEADS]], axis=1)
            g_in = jnp.transpose(g_in.reshape(d, N_DEV, c_gin), (1, 0, 2))
            in_item = _Item(g_in, (N_DEV, d, c_gin), _slot, _slot)
            dy_ab = _mm(dpab, w_ab[i], "nt", [F32], f"d_y_ab_{i}")
            dp, dy_extras = dpm, (dy_ab, h_in, dh1)
            dy_epilogue = lambda acc, e, xx, dres, w: _ep_norm_bwd(acc + e, xx, dres, w)
        else:
            proj, o_raw, st_all = mix
            dp, d_lb[i], d_gnorm[j] = _hgrn_bwd(proj, lbs[i:i + 1], gnorm_full[j:j + 1], st_all, o_raw, do2,
                                               seqs, f"hgrn_bwd_{i}")
            g_in = _mm(y, dp, "tn", [BF16], f"g_in_{i}")
            in_item = _Item(g_in, (N_DEV, d, c_hin), _cols_of(c_hin), _slot)
            dy_extras, dy_epilogue = (h_in, dh1), _ep_norm_bwd
        mix_handles[i], token = _exchange_start(
            [[_Item(g_out, (N_DEV, r_out, d), _rows_of(r_out), _slot)], [in_item]], f"scatter_start_mix_{i}")
        dh, dh_b, d_nmix[i] = _mm(dp, w_in[i], "nt", [F32, BF16], f"d_y_{i}", epilogue=dy_epilogue, extras=dy_extras,
                                  vectors=(norm_mix[i:i + 1],), n_sums=1, after=token)
        token = None
    grad_x = dh.reshape(x.shape)

    def landed(handles, k, layers, after, name):
        return [_exchange_wait(handles[i][k], after, f"scatter_wait_{name}_{i}")[0] for i in layers]

    every, even, odd = range(DEPTH), range(0, DEPTH, 2), range(1, DEPTH, 2)
    upd = {}
    upd["mlp_w_down"] = _adamw_slots(mlp_w_down, landed(mlp_handles, 0, every, dh, "down"), m_mlp_w_down,
                                     v_mlp_w_down, "adamw_mlp_w_down")
    upd["mlp_w_up"] = _adamw_slots(mlp_w_up, landed(mlp_handles, 1, every, upd["mlp_w_down"][1], "up"), m_mlp_w_up,
                                   v_mlp_w_up, "adamw_mlp_w_up")
    upd["hgrn_w_out"] = _adamw_slots(hgrn_w_out, landed(mix_handles, 0, odd, upd["mlp_w_up"][1], "out"),
                                     m_hgrn_w_out, v_hgrn_w_out, "adamw_hgrn_w_out")
    upd["hgrn_w_in"] = _adamw_slots(hgrn_w_in, landed(mix_handles, 1, odd, upd["hgrn_w_out"][1], "in"), m_hgrn_w_in,
                                    v_hgrn_w_in, "adamw_hgrn_w_in")

    dlb_rows = jnp.concatenate(d_lb, axis=0)
    tail = jnp.concatenate(
        [jnp.concatenate(d_onorm, axis=1), jnp.concatenate(d_alog, axis=1), jnp.concatenate(d_dtb, axis=1)], axis=1)
    tail = jnp.pad(tail, ((0, 0), (0, d - tail.shape[1])))
    conv_rows = jnp.stack(d_conv).reshape(-1, d)
    packed = jnp.concatenate(
        [jnp.concatenate(d_nmix, axis=0), jnp.concatenate(d_nmlp, axis=0), d_nf, sq, dlb_rows,
         jnp.concatenate(d_gnorm, axis=0), tail, conv_rows], axis=0)
    pad_rows = (-packed.shape[0]) % 8
    packed = jnp.pad(packed, ((0, pad_rows), (0, 0)))
    tot = _all_reduce_small(packed, upd["hgrn_w_in"][1], "reduce_small")

    upd["gdn_w_out"] = _adamw_slots(gdn_w_out, landed(mix_handles, 0, even, tot, "out"),
                                    m_gdn_w_out, v_gdn_w_out, "adamw_gdn_w_out")
    upd["gdn_w_in"] = _adamw_slots(gdn_w_in, landed(mix_handles, 1, even, upd["gdn_w_out"][1], "in"), m_gdn_w_in,
                                   v_gdn_w_in, "adamw_gdn_w_in")

    def update(name, w, g, m, v):
        shape = w.shape
        c = shape[-1]
        res = _adamw(w.reshape(-1, c), g.reshape(-1, c), m.reshape(-1, c), v.reshape(-1, c), "adamw_" + name)
        return [g.reshape(shape)] + [o.reshape(shape) for o in res]

    r0 = 0
    g_nmix = tot[r0:r0 + DEPTH]; r0 += DEPTH
    g_nmlp = tot[r0:r0 + DEPTH]; r0 += DEPTH
    g_nf = tot[r0]; r0 += 1
    loss = tot[r0, 0]; r0 += 1
    g_lb = _lb_bwd(hgrn_lb_logits, tot[r0:r0 + DEPTH], "lb_bwd"); r0 += DEPTH
    g_gnorm_full = tot[r0:r0 + n_hgrn]; r0 += n_hgrn
    t_row = tot[r0]; r0 += 1
    g_conv_full = tot[r0:r0 + n_gdn * CONV_K * 3].reshape(n_gdn, CONV_K, 3 * d)
    g_onorm = t_row[0:n_gdn * HEAD_DIM].reshape(n_gdn, HEAD_DIM)
    o1 = n_gdn * HEAD_DIM
    g_alog = t_row[o1:o1 + n_gdn * N_HEADS].reshape(n_gdn, N_HEADS)
    g_dtb = t_row[o1 + n_gdn * N_HEADS:o1 + 2 * n_gdn * N_HEADS].reshape(n_gdn, N_HEADS)
    c_gn, c_cv = hgrn_gnorm.shape[1], gdn_conv.shape[2]
    g_gnorm = lax.dynamic_slice_in_dim(g_gnorm_full, me_i * c_gn, c_gn, axis=1)
    g_conv = lax.dynamic_slice_in_dim(g_conv_full, me_i * c_cv, c_cv, axis=2)

    upd["gdn_conv"] = update("gdn_conv", gdn_conv, g_conv, m_gdn_conv, v_gdn_conv)
    upd["gdn_a_log"] = update("gdn_a_log", gdn_a_log, g_alog, m_gdn_a_log, v_gdn_a_log)
    upd["gdn_dt_bias"] = update("gdn_dt_bias", gdn_dt_bias, g_dtb, m_gdn_dt_bias, v_gdn_dt_bias)
    upd["gdn_onorm"] = update("gdn_onorm", gdn_onorm, g_onorm, m_gdn_onorm, v_gdn_onorm)
    upd["hgrn_lb_logits"] = update("hgrn_lb_logits", hgrn_lb_logits, g_lb, m_hgrn_lb_logits, v_hgrn_lb_logits)
    upd["hgrn_gnorm"] = update("hgrn_gnorm", hgrn_gnorm, g_gnorm, m_hgrn_gnorm, v_hgrn_gnorm)
    upd["norm_mix"] = update("norm_mix", norm_mix, g_nmix, m_norm_mix, v_norm_mix)
    upd["norm_mlp"] = update("norm_mlp", norm_mlp, g_nmlp, m_norm_mlp, v_norm_mlp)
    upd["norm_final"] = update("norm_final", norm_final, g_nf, m_norm_final, v_norm_final)

    order = ["gdn_w_in", "gdn_conv", "gdn_a_log", "gdn_dt_bias", "gdn_onorm", "gdn_w_out", "hgrn_w_in",
             "hgrn_lb_logits", "hgrn_gnorm", "hgrn_w_out", "norm_mix", "norm_mlp", "mlp_w_up", "mlp_w_down",
             "norm_final"]
    outs = [loss, grad_x]
    for k in range(4):
        outs += [upd[name][k] for name in order]
    return tuple(outs)
```

```python
import functools

import numpy as np
import jax
import jax.numpy as jnp
from jax import lax
from jax.experimental import pallas as pl
from jax.experimental.pallas import tpu as pltpu

F32 = jnp.float32
BF16 = jnp.bfloat16

D_MODEL = 1024
N_HEADS = 8
HEAD_DIM = 128
CHUNK = 64
SUB = 16
N_SUB = CHUNK // SUB
CONV_K = 4
HALO = 16
EPS = 1e-6
DEPTH = 4
N_DEV = 8
GDN_MAIN = 4 * D_MODEL
GDN_IN = GDN_MAIN + 2 * N_HEADS
AB_PAD = 128
HEAD_GROUP = 8
LANE_BLOCK = 256
ROW_BLOCK = 16

ADAM_LR = 0.001
ADAM_B1 = 0.9
ADAM_B2 = 0.999
ADAM_EPS = 1e-08
ADAM_WD = 0.01
ADAM_STEP = 10

VMEM_LIMIT = 56 * 1024 * 1024
MM_TILE = 1024
MM_VMEM_BUDGET = 40 * 1024 * 1024

_DIMS = {
    "nn": (((1,), (0,)), ((), ())),
    "nt": (((1,), (1,)), ((), ())),
    "tn": (((0,), (0,)), ((), ())),
}


def _parts(x, n):
    if n == 1 and x.dtype == BF16:
        return [x]
    out = []
    r = x.astype(F32)
    for i in range(n):
        p = r.astype(BF16)
        out.append(p)
        if i + 1 < n:
            r = r - p.astype(F32)
    return out


def _dot_raw(a, b, mode, na, nb):
    ap, bp = _parts(a, na), _parts(b, nb)
    nmax = max(na, nb)
    pairs = [(i, j) for i in range(na) for j in range(nb) if i + j < nmax]
    ka = 0 if mode == "tn" else 1
    kb = 1 if mode == "nt" else 0
    xa = ap[0] if len(pairs) == 1 else jnp.concatenate([ap[i] for i, _ in pairs], axis=ka)
    xb = bp[0] if len(pairs) == 1 else jnp.concatenate([bp[j] for _, j in pairs], axis=kb)
    return lax.dot_general(xa, xb, _DIMS[mode], preferred_element_type=F32)


@functools.partial(jax.custom_vjp, nondiff_argnums=(2, 3, 4))
def _dot(a, b, mode, na, nb):
    return _dot_raw(a, b, mode, na, nb)


def _dot_fwd(a, b, mode, na, nb):
    return _dot_raw(a, b, mode, na, nb), (a, b)


def _dot_bwd(mode, na, nb, res, ct):
    a, b = res
    if mode == "nn":
        da = _dot_raw(ct, b, "nt", 1, 1)
        db = _dot_raw(a, ct, "tn", 1, 1)
    elif mode == "nt":
        da = _dot_raw(ct, b, "nn", 1, 1)
        db = _dot_raw(ct, a, "tn", 1, 1)
    else:
        da = _dot_raw(b, ct, "nt", 1, 1)
        db = _dot_raw(a, ct, "nn", 1, 1)
    return da.astype(a.dtype), db.astype(b.dtype)


_dot.defvjp(_dot_fwd, _dot_bwd)


N_EXACT = 3


@jax.custom_vjp
def _dot01(x, m_wide, m):
    return lax.dot_general(m_wide, jnp.concatenate(_parts(x, N_EXACT), axis=0), _DIMS["nn"], preferred_element_type=F32)


def _dot01_fwd(x, m_wide, m):
    return _dot01(x, m_wide, m), (m_wide, m)


def _dot01_bwd(res, ct):
    m_wide, m = res
    dx = lax.dot_general(m, ct.astype(BF16), _DIMS["tn"], preferred_element_type=F32)
    return dx, jnp.zeros_like(m_wide), jnp.zeros_like(m)


_dot01.defvjp(_dot01_fwd, _dot01_bwd)


def _thrice(m):
    return jnp.concatenate([m] * N_EXACT, axis=1).astype(BF16), m.astype(BF16)


def _iota2(shape, dim):
    return lax.broadcasted_iota(jnp.int32, shape, dim)


def _tril_f32(n):
    return (_iota2((n, n), 0) >= _iota2((n, n), 1)).astype(F32)


def _cumsum_rows(g):
    return _dot(_tril_f32(g.shape[0]), g, "nn", 1, 3)


def _below_block(n, b):
    ri, ci = _iota2((n, n), 0) // b, _iota2((n, n), 1) // b
    return (ri == ci + 1) & (ri % 2 == 1)


def _half_inverses(L):
    n = L.shape[0]
    eye = (_iota2((n, n), 0) == _iota2((n, n), 1)).astype(F32)
    d = eye - jnp.where(_below_block(n, 1), L, 0.0)
    b = 2
    while 2 * b < n:
        e = jnp.where(_below_block(n, b), L, 0.0)
        d = d - _dot_raw(d, _dot_raw(e, d, "nn", 2, 2), "nn", 2, 2)
        b *= 2
    return d, jnp.where(_below_block(n, b), L, 0.0)


def _solve_with(d, e, rhs):
    y = _dot_raw(d, rhs, "nn", 2, 2)
    return y - _dot_raw(d, _dot_raw(e, y, "nn", 2, 2), "nn", 2, 2)


@jax.custom_vjp
def _solve_unit_lower(L, rhs, d):
    n = L.shape[0]
    return _solve_with(d, jnp.where(_below_block(n, n // 2), L, 0.0), rhs)


def _solve_fwd(L, rhs, d):
    n = L.shape[0]
    e = jnp.where(_below_block(n, n // 2), L, 0.0)
    sol = _solve_with(d, e, rhs)
    return sol, (d, e, sol)


def _solve_bwd(res, ct):
    d, e, sol = res
    y = _dot_raw(d, ct - _dot_raw(e, _dot_raw(d, ct, "tn", 2, 2), "tn", 2, 2), "tn", 2, 2)
    return -_dot_raw(y, sol, "nt", 2, 2), y, jnp.zeros_like(d)


_solve_unit_lower.defvjp(_solve_fwd, _solve_bwd)


def _softplus(x):
    return jnp.maximum(x, 0.0) + jnp.log1p(jnp.exp(-jnp.abs(x)))


def _rms(x, w):
    return x * lax.rsqrt(jnp.mean(x * x, axis=-1, keepdims=True) + EPS) * w


HG_LEVELS = (32, 16, 8, 4, 2, 1)


def _hg_level_sums():
    i = np.arange(CHUNK)[:, None]
    m = np.arange(CHUNK)[None, :]
    to_row = [(m <= i) & (m // b == i // b) for b in HG_LEVELS]
    to_col = [(m > i) & (m // b == i // b) for b in HG_LEVELS if b > 1]
    return _thrice(jnp.asarray(np.concatenate(to_row + to_col + [m <= i]), F32))


def _hg_level_masks():
    i = np.arange(CHUNK)[:, None]
    j = np.arange(CHUNK)[None, :]
    return jnp.asarray(np.stack([(i // b == j // b + 1) & ((i // b) % 2 == 1) for b in HG_LEVELS]), F32)


def _hg_pre(qraw, f, lb, sums):
    g = jnp.log(lb + (1.0 - lb) * jax.nn.sigmoid(f))
    k = (1.0 - lb) * jax.nn.sigmoid(-f)
    q = jax.nn.silu(qraw) * (HEAD_DIM ** -0.5)
    return q, k, _dot01(g, *sums)


def _hg_head(st, q, k, v, e, masks):
    nl = len(HG_LEVELS)
    eye = (_iota2((CHUNK, CHUNK), 0) == _iota2((CHUNK, CHUNK), 1)).astype(F32)
    a = eye * jnp.sum(q * k, axis=-1, keepdims=True)
    for l, b in enumerate(HG_LEVELS):
        rows = q * jnp.exp(e[l * CHUNK:(l + 1) * CHUNK])
        cols = k * jnp.exp(e[(nl + l) * CHUNK:(nl + l + 1) * CHUNK]) if b > 1 else k
        a = a + masks[l] * _dot(rows, cols, "nt", 1, 1)
    gc = e[(2 * nl - 1) * CHUNK:2 * nl * CHUNK]
    o = _dot(a, v, "nn", 1, 1) + _dot(q * jnp.exp(gc), st, "nt", 1, 1)
    g_last = gc[CHUNK - 1:CHUNK]
    st_new = st * jnp.exp(g_last) + _dot(v, k * jnp.exp(g_last - gc), "tn", 1, 1)
    return o, st_new


_HG_HEADS = jax.vmap(_hg_head, in_axes=(0, 0, 0, 0, 0, None))


def _hg_post(o, gate, gw):
    return _rms(o, gw) * jax.nn.silu(gate)


def _gd_conv(xp, cw):
    off = HALO - (CONV_K - 1)
    y = cw[0:1] * xp[off:off + CHUNK]
    for kk in range(1, CONV_K):
        y = y + cw[kk:kk + 1] * xp[off + kk:off + kk + CHUNK]
    return y


def _gd_conv_bwd(xp, cw, y, dc):
    off = HALO - (CONV_K - 1)
    sig = jax.nn.sigmoid(y)
    dy = dc * (sig * (1.0 + y * (1.0 - sig)))
    dxp, dcw = None, []
    for kk in range(CONV_K):
        moved = jnp.pad(dy, ((off + kk, HALO - off - kk), (0, 0)))
        term = cw[kk:kk + 1] * moved
        dxp = term if dxp is None else dxp + term
        dcw.append(jnp.sum(xp * moved, axis=0, keepdims=True))
    return dxp, jnp.concatenate(dcw, axis=0)


def _gd_gates(a, b, alog, dtb):
    beta = jax.nn.sigmoid(b)
    g = -jnp.exp(alog) * _softplus(a + dtb)
    expand = (_iota2((N_HEADS, D_MODEL), 1) // HEAD_DIM == _iota2((N_HEADS, D_MODEL), 0)).astype(F32)
    g_x = _dot(g, expand, "nn", 3, 1)
    after = (_iota2((CHUNK, D_MODEL), 0) > _iota2((CHUNK, D_MODEL), 1) % HEAD_DIM).astype(F32)
    sums = _dot01(jnp.concatenate([g_x, g_x * after], axis=1), *_thrice(_tril_f32(CHUNK)))
    return _dot(beta, expand, "nn", 3, 1), sums


def _gd_head(st, q, k, v, beta, gc, diff, gate, onw, dinv=None):
    q = q * lax.rsqrt(jnp.sum(q * q, axis=-1, keepdims=True) + EPS) * (HEAD_DIM ** -0.5)
    k = k * lax.rsqrt(jnp.sum(k * k, axis=-1, keepdims=True) + EPS)
    ri = _iota2((CHUNK, CHUNK), 0)
    ci = _iota2((CHUNK, CHUNK), 1)
    decay = jnp.exp(jnp.where(ri >= ci, diff[:, 0:CHUNK], -jnp.inf))
    kb = k * beta
    egc = jnp.exp(gc)
    L = jnp.where(ri > ci, _dot(kb, k, "nt", 1, 1) * decay, 0.0)
    made = dinv is None
    if made:
        dinv = _half_inverses(L)[0]
    sol = _solve_unit_lower(L, jnp.concatenate([v * beta, kb * egc], axis=1), dinv)
    u = sol[:, 0:HEAD_DIM]
    w = sol[:, HEAD_DIM:2 * HEAD_DIM]
    a_qk = jnp.where(ri >= ci, _dot(q, k, "nt", 1, 1) * decay, 0.0)
    g_last = gc[CHUNK - 1:CHUNK]
    v_new = u - _dot(w, st, "nt", 1, 1)
    o = _dot(q * egc, st, "nt", 1, 1) + _dot(a_qk, v_new, "nn", 1, 1)
    st_new = st * jnp.exp(g_last) + _dot(v_new, k * jnp.exp(g_last - gc), "tn", 1, 1)
    out = (_rms(o, onw) * jax.nn.silu(gate), st_new)
    return out + (dinv,) if made else out


def _params(*sem):
    return pltpu.CompilerParams(dimension_semantics=sem, vmem_limit_bytes=VMEM_LIMIT)


def _tile(n, pref):
    t = min(n, pref)
    assert n % t == 0, (n, pref)
    return t


def _mm_tiles(m, n, k, a_size, b_size, tile_sizes):
    tm, tn, tk = _tile(m, MM_TILE), _tile(n, MM_TILE), k

    def need(tm, tn, tk):
        acc = 4 * tm * tn * (2 if tk < k else 1)
        return 2 * (tm * tk * a_size + tk * tn * b_size + tm * tn * sum(tile_sizes)) + acc

    while need(tm, tn, tk) > MM_VMEM_BUDGET:
        if tk > 2048 or (tk > 512 and tm <= 512):
            tk //= 2
        else:
            tm //= 2
    return tm, tn, tk


def _mm(a, b, mode, out_dtypes, name, epilogue=None, extras=(), vectors=(), n_sums=0, after=None):
    if mode == "nn":
        (m, k), (k2, n) = a.shape, b.shape
    elif mode == "nt":
        (m, k), (n, k2) = a.shape, b.shape
    else:
        (k, m), (k2, n) = a.shape, b.shape
    assert k == k2, (a.shape, b.shape, mode)
    tm, tn, tk = _mm_tiles(m, n, k, a.dtype.itemsize, b.dtype.itemsize,
                           [e.dtype.itemsize for e in extras] + [jnp.dtype(dt).itemsize for dt in out_dtypes])
    nk = k // tk
    assert not (vectors or n_sums) or tn == n, "whole-row epilogues need the result tile to span the rows"
    ne, no, nafter = len(extras) + len(vectors), len(out_dtypes), int(after is not None)
    if epilogue is None:
        epilogue = lambda acc: (acc,)

    def body(*refs):
        a_ref, b_ref = refs[0], refs[1]
        ex = refs[2:2 + ne]
        outs = refs[2 + ne + nafter:2 + ne + nafter + no]
        sums = refs[2 + ne + nafter + no:2 + ne + nafter + no + n_sums]
        part = lax.dot_general(a_ref[...].astype(BF16), b_ref[...].astype(BF16), _DIMS[mode],
                               preferred_element_type=F32)

        def finish(acc):
            vals = epilogue(acc, *[e[...] for e in ex])
            for o_ref, val in zip(outs, vals[:no]):
                o_ref[...] = val.astype(o_ref.dtype)
            for s_ref, val in zip(sums, vals[no:]):
                @pl.when(pl.program_id(0) == 0)
                def _(s_ref=s_ref, val=val):
                    s_ref[...] = val

                @pl.when(pl.program_id(0) > 0)
                def _(s_ref=s_ref, val=val):
                    s_ref[...] += val

        if nk == 1:
            finish(part)
        else:
            acc_ref = refs[-1]
            kk = pl.program_id(2)

            @pl.when(kk == 0)
            def _():
                acc_ref[...] = part

            @pl.when(kk > 0)
            def _():
                acc_ref[...] += part

            @pl.when(kk == nk - 1)
            def _():
                finish(acc_ref[...])

    if mode == "tn":
        a_spec = pl.BlockSpec((tk, tm), lambda i, j, kk: (kk, i))
    else:
        a_spec = pl.BlockSpec((tm, tk), lambda i, j, kk: (i, kk))
    if mode == "nt":
        b_spec = pl.BlockSpec((tn, tk), lambda i, j, kk: (j, kk))
    else:
        b_spec = pl.BlockSpec((tk, tn), lambda i, j, kk: (kk, j))
    o_spec = pl.BlockSpec((tm, tn), lambda i, j, kk: (i, j))
    v_spec = pl.BlockSpec((1, tn), lambda i, j, kk: (0, j))
    res = pl.pallas_call(
        body,
        name=name,
        grid=(m // tm, n // tn, nk),
        in_specs=([a_spec, b_spec] + [o_spec] * len(extras) + [v_spec] * len(vectors)
                  + [pl.BlockSpec(memory_space=pl.ANY)] * nafter),
        out_specs=[o_spec] * no + [v_spec] * n_sums,
        out_shape=[jax.ShapeDtypeStruct((m, n), dt) for dt in out_dtypes] + [jax.ShapeDtypeStruct((1, n), F32)] * n_sums,
        scratch_shapes=[pltpu.VMEM((tm, tn), F32)] if nk > 1 else [],
        compiler_params=_params(*(("arbitrary",) * 3 if n_sums else ("parallel", "parallel", "arbitrary"))),
    )(a, b, *extras, *vectors, *([after] if nafter else []))
    return res[0] if no + n_sums == 1 else res


def _ep_residual_norm(acc, res, w):
    h = res + acc
    return h, _rms(h, w)


def _ep_norm_bwd(acc, x, dres, w):
    _, vjp = jax.vjp(_rms, x, w)
    dx, dw = vjp(acc)
    dx = dres + dx
    return dx, dx, dw


def _rms_fwd(x, w, name, tm=512):
    n, d = x.shape
    tm = _tile(n, tm)

    def body(x_ref, w_ref, y_ref):
        y_ref[...] = _rms(x_ref[...], w_ref[...]).astype(y_ref.dtype)

    return pl.pallas_call(
        body, name=name, grid=(n // tm,),
        in_specs=[pl.BlockSpec((tm, d), lambda i: (i, 0)), pl.BlockSpec((1, d), lambda i: (0, 0))],
        out_specs=pl.BlockSpec((tm, d), lambda i: (i, 0)),
        out_shape=jax.ShapeDtypeStruct((n, d), BF16),
        compiler_params=_params("arbitrary"),
    )(x, w)


def _loss_head(h, w, target, name, tm=512):
    n, d = h.shape
    tm = _tile(n, tm)

    def body(h_ref, w_ref, t_ref, dh_ref, dhb_ref, dw_ref, sq_ref):
        y, vjp = jax.vjp(_rms, h_ref[...], w_ref[...])
        err = y - t_ref[...]
        dh, dw = vjp(err * (1.0 / d))
        dh_ref[...] = dh
        dhb_ref[...] = dh.astype(dhb_ref.dtype)
        sq = jnp.sum(err * err, axis=0, keepdims=True)

        @pl.when(pl.program_id(0) == 0)
        def _():
            dw_ref[...] = dw
            sq_ref[...] = sq

        @pl.when(pl.program_id(0) > 0)
        def _():
            dw_ref[...] += dw
            sq_ref[...] += sq

        @pl.when(pl.program_id(0) == n // tm - 1)
        def _():
            total = jnp.sum(sq_ref[...], axis=1, keepdims=True) * (0.5 / d)
            sq_ref[...] = jnp.broadcast_to(total, sq_ref.shape)

    row = pl.BlockSpec((tm, d), lambda i: (i, 0))
    vec = pl.BlockSpec((1, d), lambda i: (0, 0))
    return pl.pallas_call(
        body, name=name, grid=(n // tm,),
        in_specs=[row, vec, row],
        out_specs=[row, row, vec, vec],
        out_shape=[jax.ShapeDtypeStruct((n, d), F32), jax.ShapeDtypeStruct((n, d), BF16),
                   jax.ShapeDtypeStruct((1, d), F32), jax.ShapeDtypeStruct((1, d), F32)],
        compiler_params=_params("arbitrary"),
    )(h, w, target)


def _lower_bounds(logits):
    sm = jax.nn.softmax(logits, axis=0)
    rows = [sm[0:1] * 0.0]
    for r in range(1, DEPTH):
        rows.append(rows[-1] + sm[r:r + 1])
    return jnp.concatenate(rows, axis=0)


def _lb_fwd(logits, name):
    def body(l_ref, o_ref):
        o_ref[...] = _lower_bounds(l_ref[...])

    return pl.pallas_call(body, name=name, out_shape=jax.ShapeDtypeStruct(logits.shape, F32))(logits)


def _lb_bwd(logits, dlb, name):
    def body(l_ref, d_ref, o_ref):
        _, vjp = jax.vjp(_lower_bounds, l_ref[...])
        (o_ref[...],) = vjp(d_ref[...])

    return pl.pallas_call(body, name=name, out_shape=jax.ShapeDtypeStruct(logits.shape, F32))(logits, dlb)


def _head_slice(h):
    if isinstance(h, int):
        return pl.ds(h * HEAD_DIM, HEAD_DIM)
    return pl.ds(pl.multiple_of(h * HEAD_DIM, HEAD_DIM), HEAD_DIM)


def _head_groups(group_body):
    if HEAD_GROUP == N_HEADS:
        group_body(list(range(N_HEADS)))
        return

    def trip(i, carry):
        group_body([i * HEAD_GROUP + t for t in range(HEAD_GROUP)])
        return carry

    lax.fori_loop(0, N_HEADS // HEAD_GROUP, trip, 0)


def _stack_heads(ref, hs, first=0):
    return jnp.stack([ref[:, _head_slice(h + first)] for h in hs])


def _unstack_heads(ref, hs, val, first=0):
    for t, h in enumerate(hs):
        ref[:, _head_slice(h + first)] = val[t].astype(ref.dtype)


_GD_HEADS = jax.vmap(_gd_head, in_axes=(0, 0, 0, 0, 0, 0, 0, 0, None))
_GD_HEADS_AGAIN = jax.vmap(_gd_head, in_axes=(0, 0, 0, 0, 0, 0, 0, 0, None, 0))


def _hgrn_fwd(proj, lb, gw, seqs, name):
    n = proj.shape[0]
    nc = n // seqs // CHUNK
    d = D_MODEL

    sums, masks = _hg_level_sums(), _hg_level_masks()

    def body(p_ref, lb_ref, gw_ref, sums_wide_ref, sums_once_ref, masks_ref, o2_ref, o_ref, st_all_ref,
             st_sc, q_sc, k_sc, v_sc, e_sc):
        @pl.when(pl.program_id(1) == 0)
        def _():
            st_sc[...] = jnp.zeros_like(st_sc)

        sums_refs = (sums_wide_ref, sums_once_ref)
        _lane_blocks(d, functools.partial(_hg_pre_block, p_ref, lb_ref, sums_refs, q_sc, k_sc, v_sc, e_sc))
        st_all_ref[0] = st_sc[...]

        def group(hs):
            sts = pl.ds(hs[0], len(hs))
            o, st_new = _HG_HEADS(st_sc[sts], *[_stack_heads(r, hs) for r in (q_sc, k_sc, v_sc, e_sc)], masks_ref[...])
            _unstack_heads(o_ref, hs, o)
            st_sc[sts] = st_new

        _head_groups(group)

        def post(rows):
            gate = p_ref[rows, 3 * d:4 * d].astype(F32)
            o2_ref[rows, :] = _hg_post(o_ref[rows, :], gate, gw_ref[...]).astype(o2_ref.dtype)

        _row_blocks(CHUNK, post)

    idx = lambda b, c: (b * nc + c, 0)
    vec = pl.BlockSpec((1, d), lambda b, c: (0, 0))
    act = pl.BlockSpec((CHUNK, d), idx)
    return pl.pallas_call(
        body, name=name, grid=(seqs, nc),
        in_specs=[pl.BlockSpec((CHUNK, 4 * d), idx), vec, vec] + [pl.BlockSpec(s.shape, lambda b, c: (0, 0)) for s in sums]
        + [pl.BlockSpec(masks.shape, lambda b, c: (0, 0, 0))],
        out_specs=[act, act, pl.BlockSpec((1, N_HEADS, HEAD_DIM, HEAD_DIM), lambda b, c: (b * nc + c, 0, 0, 0))],
        out_shape=[jax.ShapeDtypeStruct((n, d), BF16), jax.ShapeDtypeStruct((n, d), F32),
                   jax.ShapeDtypeStruct((n // CHUNK, N_HEADS, HEAD_DIM, HEAD_DIM), F32)],
        scratch_shapes=[pltpu.VMEM((N_HEADS, HEAD_DIM, HEAD_DIM), F32)] + [pltpu.VMEM((CHUNK, d), F32)] * 3
        + [pltpu.VMEM((sums[0].shape[0], d), F32)],
        compiler_params=_params("arbitrary", "arbitrary"),
    )(proj, lb, gw, *sums, masks)


def _hgrn_bwd(proj, lb, gw, st_all, o, do2, seqs, name):
    n = proj.shape[0]
    nc = n // seqs // CHUNK
    d = D_MODEL

    sums, masks = _hg_level_sums(), _hg_level_masks()

    def body(p_ref, lb_ref, gw_ref, sums_wide_ref, sums_once_ref, masks_ref, st_all_ref, o_ref, do2_ref,
             dp_ref, dlb_ref, dgw_ref,
             dst_sc, q_sc, k_sc, v_sc, e_sc, do_sc, dq_sc, dk_sc, dv_sc, de_sc, dgw_sc):
        first = (pl.program_id(0) == 0) & (pl.program_id(1) == 0)

        @pl.when(pl.program_id(1) == 0)
        def _():
            dst_sc[...] = jnp.zeros_like(dst_sc)

        sums_refs = (sums_wide_ref, sums_once_ref)
        _lane_blocks(d, functools.partial(_hg_pre_block, p_ref, lb_ref, sums_refs, q_sc, k_sc, v_sc, e_sc))
        dgw_sc[...] = jnp.zeros_like(dgw_sc)

        def post_bwd(rows):
            _, vjp = jax.vjp(_hg_post, o_ref[rows, :], p_ref[rows, 3 * d:4 * d].astype(F32), gw_ref[...])
            do_sc[rows, :], dgate, dgw = vjp(do2_ref[rows, :].astype(F32))
            dp_ref[rows, 3 * d:4 * d] = dgate.astype(dp_ref.dtype)
            dgw_sc[...] += dgw

        _row_blocks(CHUNK, post_bwd)

        def group(hs):
            sts = pl.ds(hs[0], len(hs))
            level_masks = masks_ref[...]
            _, vjp = jax.vjp(lambda *a: _HG_HEADS(*a, level_masks), st_all_ref[0, sts],
                             *[_stack_heads(r, hs) for r in (q_sc, k_sc, v_sc, e_sc)])
            grads = vjp((_stack_heads(do_sc, hs), dst_sc[sts]))
            dst_sc[sts] = grads[0]
            for r, val in zip((dq_sc, dk_sc, dv_sc, de_sc), grads[1:]):
                _unstack_heads(r, hs, val)

        _head_groups(group)

        def pre_bwd(at):
            sl = at()
            level_sums = (sums_wide_ref[...], sums_once_ref[...])
            _, vjp = jax.vjp(lambda qraw, f, lb: _hg_pre(qraw, f, lb, level_sums), p_ref[:, sl].astype(F32),
                             p_ref[:, at(d)].astype(F32), lb_ref[:, sl])
            dqraw, df, dlb = vjp((dq_sc[:, sl], dk_sc[:, sl], de_sc[:, sl]))
            dp_ref[:, sl] = dqraw.astype(dp_ref.dtype)
            dp_ref[:, at(d)] = df.astype(dp_ref.dtype)
            dp_ref[:, at(2 * d)] = dv_sc[:, sl].astype(dp_ref.dtype)

            @pl.when(first)
            def _():
                dlb_ref[:, sl] = dlb

            @pl.when(jnp.logical_not(first))
            def _():
                dlb_ref[:, sl] += dlb

        _lane_blocks(d, pre_bwd)

        @pl.when(first)
        def _():
            dgw_ref[...] = dgw_sc[...]

        @pl.when(jnp.logical_not(first))
        def _():
            dgw_ref[...] += dgw_sc[...]

    idx = lambda b, c: (b * nc + nc - 1 - c, 0)
    vec = pl.BlockSpec((1, d), lambda b, c: (0, 0))
    act = pl.BlockSpec((CHUNK, d), idx)
    wide = pl.BlockSpec((CHUNK, 4 * d), idx)
    return pl.pallas_call(
        body, name=name, grid=(seqs, nc),
        in_specs=[wide, vec, vec] + [pl.BlockSpec(s.shape, lambda b, c: (0, 0)) for s in sums] + [
                  pl.BlockSpec(masks.shape, lambda b, c: (0, 0, 0)),
                  pl.BlockSpec((1, N_HEADS, HEAD_DIM, HEAD_DIM), lambda b, c: (b * nc + nc - 1 - c, 0, 0, 0)),
                  act, act],
        out_specs=[wide, vec, vec],
        out_shape=[jax.ShapeDtypeStruct((n, 4 * d), BF16), jax.ShapeDtypeStruct((1, d), F32),
                   jax.ShapeDtypeStruct((1, d), F32)],
        scratch_shapes=[pltpu.VMEM((N_HEADS, HEAD_DIM, HEAD_DIM), F32)]
        + [pltpu.VMEM((CHUNK, d), F32)] * 3 + [pltpu.VMEM((sums[0].shape[0], d), F32)]
        + [pltpu.VMEM((CHUNK, d), F32)] * 4 + [pltpu.VMEM((sums[0].shape[0], d), F32), pltpu.VMEM((1, d), F32)],
        compiler_params=_params("arbitrary", "arbitrary"),
    )(proj, lb, gw, *sums, masks, st_all, o, do2)


def _lane_blocks(width, block_body):
    def trip(j, carry):
        block_body(lambda base=0: pl.ds(pl.multiple_of(j * LANE_BLOCK + base, LANE_BLOCK), LANE_BLOCK))
        return carry

    lax.fori_loop(0, width // LANE_BLOCK, trip, 0)


def _row_blocks(rows, block_body):
    def trip(j, carry):
        block_body(pl.ds(pl.multiple_of(j * ROW_BLOCK, ROW_BLOCK), ROW_BLOCK))
        return carry

    lax.fori_loop(0, rows // ROW_BLOCK, trip, 0)


def _hg_pre_block(p_ref, lb_ref, sums_refs, q_sc, k_sc, v_sc, e_sc, at):
    sl = at()
    q_sc[:, sl], k_sc[:, sl], e_sc[:, sl] = _hg_pre(
        p_ref[:, sl].astype(F32), p_ref[:, at(D_MODEL)].astype(F32), lb_ref[:, sl], [r[...] for r in sums_refs])
    v_sc[:, sl] = p_ref[:, at(2 * D_MODEL)].astype(F32)


def _gd_xp(halo_ref, p_ref, sl, first_chunk):
    halo = jnp.where(first_chunk, 0.0, halo_ref[:, sl].astype(F32))
    return jnp.concatenate([halo, p_ref[:, sl].astype(F32)], axis=0)


def _gdn_fwd(projm, projab, cw, alog, dtb, onw, seqs, name):
    n = projm.shape[0]
    nc = n // seqs // CHUNK
    d = D_MODEL
    per_halo = CHUNK // HALO

    def body(p_ref, halo_ref, ab_ref, cw_ref, alog_ref, dtb_ref, onw_ref, o2_ref, st_all_ref, y_ref, dinv_ref,
             st_sc, c_sc, beta_sc, g_sc):
        @pl.when(pl.program_id(1) == 0)
        def _():
            st_sc[...] = jnp.zeros_like(st_sc)

        def conv(at):
            sl = at()
            y = _gd_conv(_gd_xp(halo_ref, p_ref, sl, pl.program_id(1) == 0), cw_ref[:, sl])
            y_ref[:, sl] = y
            c_sc[:, sl] = jax.nn.silu(y)

        _lane_blocks(3 * d, conv)
        beta_sc[...], g_sc[...] = _gd_gates(ab_ref[:, 0:N_HEADS], ab_ref[:, N_HEADS:2 * N_HEADS], alog_ref[...],
                                            dtb_ref[...])
        st_all_ref[0] = st_sc[...]

        def group(hs):
            sts = pl.ds(hs[0], len(hs))
            o2, st_new, dinv_ref[0, sts] = _GD_HEADS(
                st_sc[sts], _stack_heads(c_sc, hs), _stack_heads(c_sc, hs, N_HEADS), _stack_heads(c_sc, hs, 2 * N_HEADS),
                _stack_heads(beta_sc, hs), _stack_heads(g_sc, hs), _stack_heads(g_sc, hs, N_HEADS),
                _stack_heads(p_ref, hs, 3 * N_HEADS).astype(F32), onw_ref[...])
            _unstack_heads(o2_ref, hs, o2)
            st_sc[sts] = st_new

        _head_groups(group)

    idx = lambda b, c: (b * nc + c, 0)
    const = lambda b, c: (0, 0)
    return pl.pallas_call(
        body, name=name, grid=(seqs, nc),
        in_specs=[pl.BlockSpec((CHUNK, 4 * d), idx),
                  pl.BlockSpec((HALO, 3 * d), lambda b, c: (jnp.maximum((b * nc + c) * per_halo - 1, 0), 0)),
                  pl.BlockSpec((CHUNK, AB_PAD), idx),
                  pl.BlockSpec((CONV_K, 3 * d), const), pl.BlockSpec((1, N_HEADS), const),
                  pl.BlockSpec((1, N_HEADS), const), pl.BlockSpec((1, HEAD_DIM), const)],
        out_specs=[pl.BlockSpec((CHUNK, d), idx),
                   pl.BlockSpec((1, N_HEADS, HEAD_DIM, HEAD_DIM), lambda b, c: (b * nc + c, 0, 0, 0)),
                   pl.BlockSpec((CHUNK, 3 * d), idx),
                   pl.BlockSpec((1, N_HEADS, CHUNK, CHUNK), lambda b, c: (b * nc + c, 0, 0, 0))],
        out_shape=[jax.ShapeDtypeStruct((n, d), BF16),
                   jax.ShapeDtypeStruct((n // CHUNK, N_HEADS, HEAD_DIM, HEAD_DIM), F32),
                   jax.ShapeDtypeStruct((n, 3 * d), F32),
                   jax.ShapeDtypeStruct((n // CHUNK, N_HEADS, CHUNK, CHUNK), F32)],
        scratch_shapes=[pltpu.VMEM((N_HEADS, HEAD_DIM, HEAD_DIM), F32), pltpu.VMEM((CHUNK, 3 * d), F32),
                        pltpu.VMEM((CHUNK, d), F32), pltpu.VMEM((CHUNK, 2 * d), F32)],
        compiler_params=_params("arbitrary", "arbitrary"),
    )(projm, projm, projab, cw, alog, dtb, onw)


def _gdn_bwd(projm, projab, conv_y, cw, alog, dtb, onw, st_all, dinv_all, do2, seqs, name):
    n = projm.shape[0]
    nc = n // seqs // CHUNK
    d = D_MODEL
    per_halo = CHUNK // HALO

    def body(p_ref, halo_ref, ab_ref, y_ref, cw_ref, alog_ref, dtb_ref, onw_ref, st_all_ref, dinv_ref, do2_ref,
             dp_ref, dab_ref, dcw_ref, dalog_ref, ddtb_ref, donw_ref,
             dst_sc, dhalo_sc, c_sc, beta_sc, g_sc, dc_sc, dbeta_sc, dg_sc, donw_sc):
        step = pl.program_id(1)
        first = (pl.program_id(0) == 0) & (step == 0)

        @pl.when(step == 0)
        def _():
            dst_sc[...] = jnp.zeros_like(dst_sc)
            dhalo_sc[...] = jnp.zeros_like(dhalo_sc)

        donw_sc[...] = jnp.zeros_like(donw_sc)

        def act(at):
            c_sc[:, at()] = jax.nn.silu(y_ref[:, at()])

        _lane_blocks(3 * d, act)
        (beta_sc[...], g_sc[...]), gates_vjp = jax.vjp(
            _gd_gates, ab_ref[:, 0:N_HEADS], ab_ref[:, N_HEADS:2 * N_HEADS], alog_ref[...], dtb_ref[...])

        def group(hs):
            sts = pl.ds(hs[0], len(hs))
            dinv = dinv_ref[0, sts]
            _, vjp = jax.vjp(
                lambda *a: _GD_HEADS_AGAIN(*a, dinv), st_all_ref[0, sts], _stack_heads(c_sc, hs),
                _stack_heads(c_sc, hs, N_HEADS), _stack_heads(c_sc, hs, 2 * N_HEADS), _stack_heads(beta_sc, hs),
                _stack_heads(g_sc, hs), _stack_heads(g_sc, hs, N_HEADS),
                _stack_heads(p_ref, hs, 3 * N_HEADS).astype(F32), onw_ref[...])
            dst, dq, dk, dv, dbeta, dg, ddiff, dgate, donw = vjp(
                (_stack_heads(do2_ref, hs).astype(F32), dst_sc[sts]))
            _unstack_heads(dg_sc, hs, ddiff, N_HEADS)
            dst_sc[sts] = dst
            _unstack_heads(dc_sc, hs, dq)
            _unstack_heads(dc_sc, hs, dk, N_HEADS)
            _unstack_heads(dc_sc, hs, dv, 2 * N_HEADS)
            _unstack_heads(dbeta_sc, hs, dbeta)
            _unstack_heads(dg_sc, hs, dg)
            _unstack_heads(dp_ref, hs, dgate, 3 * N_HEADS)
            donw_sc[...] += donw

        _head_groups(group)
        def conv_bwd(at):
            sl = at()
            dxp, dcw = _gd_conv_bwd(_gd_xp(halo_ref, p_ref, sl, step == nc - 1), cw_ref[:, sl], y_ref[:, sl],
                                    dc_sc[:, sl])
            dqkv = jnp.concatenate([dxp[HALO:CHUNK], dxp[CHUNK:HALO + CHUNK] + dhalo_sc[:, sl]], axis=0)
            dp_ref[:, sl] = dqkv.astype(dp_ref.dtype)
            dhalo_sc[:, sl] = dxp[0:HALO]

            @pl.when(first)
            def _():
                dcw_ref[:, sl] = dcw

            @pl.when(jnp.logical_not(first))
            def _():
                dcw_ref[:, sl] += dcw

        _lane_blocks(3 * d, conv_bwd)
        da, db, dalog, ddtb = gates_vjp((dbeta_sc[...], dg_sc[...]))
        dab_ref[...] = jnp.concatenate(
            [da, db, jnp.zeros((CHUNK, AB_PAD - 2 * N_HEADS), F32)], axis=1).astype(dab_ref.dtype)

        @pl.when(first)
        def _():
            dalog_ref[...] = dalog
            ddtb_ref[...] = ddtb
            donw_ref[...] = donw_sc[...]

        @pl.when(jnp.logical_not(first))
        def _():
            dalog_ref[...] += dalog
            ddtb_ref[...] += ddtb
            donw_ref[...] += donw_sc[...]

    rev = lambda b, c: b * nc + nc - 1 - c
    idx = lambda b, c: (rev(b, c), 0)
    const = lambda b, c: (0, 0)
    small = [pl.BlockSpec((CONV_K, 3 * d), const), pl.BlockSpec((1, N_HEADS), const),
             pl.BlockSpec((1, N_HEADS), const), pl.BlockSpec((1, HEAD_DIM), const)]
    return pl.pallas_call(
        body, name=name, grid=(seqs, nc),
        in_specs=[pl.BlockSpec((CHUNK, 4 * d), idx),
                  pl.BlockSpec((HALO, 3 * d), lambda b, c: (jnp.maximum(rev(b, c) * per_halo - 1, 0), 0)),
                  pl.BlockSpec((CHUNK, AB_PAD), idx), pl.BlockSpec((CHUNK, 3 * d), idx)] + small + [
                  pl.BlockSpec((1, N_HEADS, HEAD_DIM, HEAD_DIM), lambda b, c: (rev(b, c), 0, 0, 0)),
                  pl.BlockSpec((1, N_HEADS, CHUNK, CHUNK), lambda b, c: (rev(b, c), 0, 0, 0)),
                  pl.BlockSpec((CHUNK, d), idx)],
        out_specs=[pl.BlockSpec((CHUNK, 4 * d), idx), pl.BlockSpec((CHUNK, AB_PAD), idx)] + small,
        out_shape=[jax.ShapeDtypeStruct((n, 4 * d), BF16), jax.ShapeDtypeStruct((n, AB_PAD), BF16),
                   jax.ShapeDtypeStruct((CONV_K, 3 * d), F32), jax.ShapeDtypeStruct((1, N_HEADS), F32),
                   jax.ShapeDtypeStruct((1, N_HEADS), F32), jax.ShapeDtypeStruct((1, HEAD_DIM), F32)],
        scratch_shapes=[pltpu.VMEM((N_HEADS, HEAD_DIM, HEAD_DIM), F32), pltpu.VMEM((HALO, 3 * d), F32),
                        pltpu.VMEM((CHUNK, 3 * d), F32), pltpu.VMEM((CHUNK, d), F32), pltpu.VMEM((CHUNK, 2 * d), F32),
                        pltpu.VMEM((CHUNK, 3 * d), F32), pltpu.VMEM((CHUNK, d), F32), pltpu.VMEM((CHUNK, 2 * d), F32),
                        pltpu.VMEM((1, HEAD_DIM), F32)],
        compiler_params=_params("arbitrary", "arbitrary"),
    )(projm, projm, projab, conv_y, cw, alog, dtb, onw, st_all, dinv_all, do2)


def _adam_update(w, g, m, v):
    b1c = 1.0 - ADAM_B1 ** ADAM_STEP
    b2c = 1.0 - ADAM_B2 ** ADAM_STEP
    m_new = ADAM_B1 * m + (1.0 - ADAM_B1) * g
    v_new = ADAM_B2 * v + (1.0 - ADAM_B2) * (g * g)
    delta = -ADAM_LR * ((m_new / b1c) / (jnp.sqrt(v_new / b2c) + ADAM_EPS) + ADAM_WD * w)
    return delta, m_new, v_new


def _adamw(w, g, m, v, name, tr=256):
    r, c = w.shape
    tr = _tile(r, tr)

    def body(w_ref, g_ref, m_ref, v_ref, d_ref, mo_ref, vo_ref):
        d_ref[...], mo_ref[...], vo_ref[...] = _adam_update(w_ref[...], g_ref[...], m_ref[...], v_ref[...])

    blk = pl.BlockSpec((tr, c), lambda i: (i, 0))
    return pl.pallas_call(
        body, name=name, grid=(r // tr,),
        in_specs=[blk] * 4, out_specs=[blk] * 3,
        out_shape=[jax.ShapeDtypeStruct((r, c), F32)] * 3,
        compiler_params=_params("arbitrary"),
    )(w, g, m, v)


def _adamw_slots(w, slot_bufs, m, v, name, tr=256):
    nl, r, c = w.shape
    tr = _tile(r, tr)

    def body(*refs):
        w_ref = refs[0]
        g_refs = refs[1:1 + nl]
        m_ref, v_ref, go_ref, d_ref, mo_ref, vo_ref = refs[1 + nl:]
        for k in range(nl):
            @pl.when(pl.program_id(0) == k)
            def _(k=k):
                g = g_refs[k][0].astype(F32)
                for s in range(1, N_DEV):
                    g = g + g_refs[k][s].astype(F32)
                go_ref[0] = g

        d_ref[0], mo_ref[0], vo_ref[0] = _adam_update(w_ref[0], go_ref[0], m_ref[0], v_ref[0])

    blk = pl.BlockSpec((1, tr, c), lambda l, i: (l, i, 0))
    g_specs = [pl.BlockSpec((N_DEV, tr, c), lambda l, i, k=k: (0, jnp.where(l == k, i, 0), 0)) for k in range(nl)]
    return pl.pallas_call(
        body, name=name, grid=(nl, r // tr),
        in_specs=[blk] + g_specs + [blk, blk], out_specs=[blk] * 4,
        out_shape=[jax.ShapeDtypeStruct((nl, r, c), F32)] * 4,
        compiler_params=_params("arbitrary", "arbitrary"),
    )(w, *slot_bufs, m, v)


def _mesh_pos():
    return lax.axis_index("x"), lax.axis_index("y"), lax.axis_index("c")


def _flip(pos, p):
    x, y, c = pos
    return ((1 - x) if p & 4 else x, (1 - y) if p & 2 else y, (1 - c) if p & 1 else c)


def _lin(pos):
    return 4 * pos[0] + 2 * pos[1] + pos[2]


_HBM = pl.BlockSpec(memory_space=pltpu.HBM)
_SEM = pl.BlockSpec(memory_space=pltpu.SEMAPHORE)
_DATAFLOW = pltpu.SideEffectType.DATAFLOW_SIDE_EFFECTING


class _Item:
    def __init__(self, src, land_shape, src_pick, dst_pick):
        self.src, self.land_shape, self.src_pick, self.dst_pick = src, land_shape, src_pick, dst_pick


def _remote_copies(items, src, land, send_sem, recv_sem, me, arriving):
    me_i = _lin(me)
    out = []
    for it, s_ref, l_ref in zip(items, src, land):
        for p in range(1, N_DEV):
            peer = _flip(me, p)
            out.append(pltpu.make_async_remote_copy(
                src_ref=it.src_pick(s_ref, _lin(peer)),
                dst_ref=it.dst_pick(l_ref, _lin(peer) if arriving else me_i),
                send_sem=send_sem, recv_sem=recv_sem, device_id=peer, device_id_type=pl.DeviceIdType.MESH))
    return out


def _own_copies(items, src, land, sem, me):
    me_i = _lin(me)
    return [pltpu.make_async_copy(it.src_pick(s_ref, me_i), it.dst_pick(l_ref, me_i), sem)
            for it, s_ref, l_ref in zip(items, src, land)]


def _exchange_start(groups, name):
    items = [it for g in groups for it in g]
    n, ng = len(items), len(groups)
    first = [sum(len(g) for g in groups[:gi]) for gi in range(ng)]

    def body(*refs):
        src, land = refs[0:n], refs[n:2 * n]
        send_sems, recv_sems = refs[2 * n:2 * n + ng], refs[2 * n + ng:2 * n + 2 * ng]
        token = refs[4 * n + 2 * ng]
        me = _mesh_pos()
        for gi, g in enumerate(groups):
            sl = slice(first[gi], first[gi] + len(g))
            for cp in _remote_copies(g, src[sl], land[sl], send_sems[gi], recv_sems[gi], me, arriving=False):
                cp.start()
            for cp in _own_copies(g, src[sl], land[sl], recv_sems[gi], me):
                cp.start()
        token[...] = jnp.zeros_like(token)

    srcs = [pltpu.with_memory_space_constraint(it.src, pltpu.HBM) for it in items]
    lands = [pltpu.with_memory_space_constraint(lax.empty(it.land_shape, it.src.dtype), pltpu.HBM) for it in items]
    res = pl.pallas_call(
        body, name=name,
        out_shape=([pltpu.SemaphoreType.DMA(())] * (2 * ng)
                   + [pltpu.HBM(it.src.shape, it.src.dtype) for it in items]
                   + [pltpu.HBM(it.land_shape, it.src.dtype) for it in items]
                   + [jax.ShapeDtypeStruct((8, 128), F32)]),
        in_specs=[_HBM] * (2 * n),
        out_specs=[_SEM] * (2 * ng) + [_HBM] * (2 * n) + [pl.BlockSpec(memory_space=pltpu.VMEM)],
        input_output_aliases={i: 2 * ng + i for i in range(2 * n)},
        compiler_params=pltpu.CompilerParams(has_side_effects=_DATAFLOW),
    )(*srcs, *lands)
    send_sems, recv_sems = res[0:ng], res[ng:2 * ng]
    src_thru, land_thru = res[2 * ng:2 * ng + n], res[2 * ng + n:2 * ng + 2 * n]
    handles = []
    for gi, g in enumerate(groups):
        sl = slice(first[gi], first[gi] + len(g))
        handles.append((g, src_thru[sl], land_thru[sl], send_sems[gi], recv_sems[gi]))
    return handles, res[-1]


def _exchange_wait(handle, after, name):
    items, src_thru, land_thru, send_sem, recv_sem = handle
    k = len(items)

    def body(*refs):
        src, land = refs[0:k], refs[k:2 * k]
        send_ref, recv_ref = refs[2 * k], refs[2 * k + 1]
        for cp in _remote_copies(items, src, land, send_ref, recv_ref, _mesh_pos(), arriving=True):
            cp.wait_send()
            cp.wait_recv()
        for cp in _own_copies(items, src, land, recv_ref, _mesh_pos()):
            cp.wait()

    res = pl.pallas_call(
        body, name=name,
        out_shape=([pltpu.HBM(s.shape, s.dtype) for s in src_thru] + [pltpu.HBM(l.shape, l.dtype) for l in land_thru]),
        in_specs=[_HBM] * (2 * k) + [_SEM, _SEM, pl.BlockSpec(memory_space=pl.ANY)],
        out_specs=[_HBM] * (2 * k),
        input_output_aliases={i: i for i in range(2 * k)},
        compiler_params=pltpu.CompilerParams(has_side_effects=_DATAFLOW),
    )(*src_thru, *land_thru, send_sem, recv_sem, after)
    return res[k:2 * k]


def _whole(ref, i):
    return ref


def _slot(ref, i):
    return ref.at[i]


def _rows_of(r):
    return lambda ref, i: ref.at[pl.ds(pl.multiple_of(i * r, r), r), :]


def _cols_of(c):
    return lambda ref, i: ref.at[:, pl.ds(pl.multiple_of(i * c, c), c)]


def _all_reduce_small(buf, after, name):
    r, c = buf.shape

    def body(src_ref, after_ref, out_ref, all_ref, send_sems, recv_sems):
        me = _mesh_pos()
        me_i = _lin(me)
        all_ref[me_i] = src_ref[...]
        for p in range(1, N_DEV):
            peer = _flip(me, p)
            pltpu.make_async_remote_copy(
                src_ref=src_ref, dst_ref=all_ref.at[me_i], send_sem=send_sems.at[p - 1], recv_sem=recv_sems.at[p - 1],
                device_id=peer, device_id_type=pl.DeviceIdType.MESH).start()
        for p in range(1, N_DEV):
            peer = _flip(me, p)
            cp = pltpu.make_async_remote_copy(
                src_ref=src_ref, dst_ref=all_ref.at[_lin(peer)], send_sem=send_sems.at[p - 1],
                recv_sem=recv_sems.at[p - 1], device_id=peer, device_id_type=pl.DeviceIdType.MESH)
            cp.wait_recv()
            cp.wait_send()
        acc = all_ref[0]
        for s in range(1, N_DEV):
            acc = acc + all_ref[s]
        out_ref[...] = acc

    vm = pl.BlockSpec(memory_space=pltpu.VMEM)
    return pl.pallas_call(
        body, name=name, in_specs=[vm, pl.BlockSpec(memory_space=pl.ANY)], out_specs=vm,
        out_shape=jax.ShapeDtypeStruct((r, c), F32),
        scratch_shapes=[pltpu.VMEM((N_DEV, r, c), F32), pltpu.SemaphoreType.DMA((N_DEV - 1,)),
                        pltpu.SemaphoreType.DMA((N_DEV - 1,))],
        compiler_params=pltpu.CompilerParams(has_side_effects=True),
    )(buf, after)


def _unshard_cols(g):
    s, l, r, c = g.shape
    return jnp.transpose(g, (1, 2, 0, 3)).reshape(l, r, s * c)


def kernel(x, gdn_w_in, gdn_conv, gdn_a_log, gdn_dt_bias, gdn_onorm, gdn_w_out, hgrn_w_in, hgrn_lb_logits, hgrn_gnorm, hgrn_w_out, norm_mix, norm_mlp, mlp_w_up, mlp_w_down, norm_final, loss_target, m_gdn_w_in, m_gdn_conv, m_gdn_a_log, m_gdn_dt_bias, m_gdn_onorm, m_gdn_w_out, m_hgrn_w_in, m_hgrn_lb_logits, m_hgrn_gnorm, m_hgrn_w_out, m_norm_mix, m_norm_mlp, m_mlp_w_up, m_mlp_w_down, m_norm_final, v_gdn_w_in, v_gdn_conv, v_gdn_a_log, v_gdn_dt_bias, v_gdn_onorm, v_gdn_w_out, v_hgrn_w_in, v_hgrn_lb_logits, v_hgrn_gnorm, v_hgrn_w_out, v_norm_mix, v_norm_mlp, v_mlp_w_up, v_mlp_w_down, v_norm_final):
    seqs, seq_len, d = x.shape
    n = seqs * seq_len
    me_i = _lin(_mesh_pos())
    x2 = x.reshape(n, d)
    target = loss_target.reshape(n, d)
    n_gdn, n_hgrn = gdn_w_in.shape[0], hgrn_w_in.shape[0]

    r_out, r_down = gdn_w_out.shape[1], mlp_w_down.shape[1]
    c_gin, c_hin, c_up = gdn_w_in.shape[2], hgrn_w_in.shape[2], mlp_w_up.shape[2]

    def gathered(w, pick, land_shape):
        return _Item(w.astype(BF16), land_shape, _whole, pick)

    groups = [[_Item(gdn_conv, (N_DEV,) + gdn_conv.shape, _whole, _slot),
               _Item(hgrn_gnorm, (N_DEV,) + hgrn_gnorm.shape, _whole, _slot)]]
    for i in range(DEPTH):
        j = i // 2
        if i % 2 == 0:
            groups += [[gathered(gdn_w_in[j], _slot, (N_DEV, d, c_gin))],
                       [gathered(gdn_w_out[j], _rows_of(r_out), (N_DEV * r_out, d))]]
        else:
            groups += [[gathered(hgrn_w_in[j], _cols_of(c_hin), (d, N_DEV * c_hin))],
                       [gathered(hgrn_w_out[j], _rows_of(r_out), (N_DEV * r_out, d))]]
        groups += [[gathered(mlp_w_up[i], _cols_of(c_up), (d, N_DEV * c_up))],
                   [gathered(mlp_w_down[i], _rows_of(r_down), (N_DEV * r_down, d))]]
    gather_handles, token = _exchange_start(groups, "gather_start")
    lbs = _lb_fwd(hgrn_lb_logits + token[0:1, 0:1], "lb_fwd")

    def arrived(k, after, name):
        return _exchange_wait(gather_handles[k], after, "gather_wait_" + name)

    saved = []
    w_in, w_ab, w_out, w_up, w_down = ([None] * DEPTH for _ in range(5))
    h = x2
    for i in range(DEPTH):
        j = i // 2
        if i == 0:
            g_conv, g_gnorm = arrived(0, h, "small")
            conv_full = _unshard_cols(g_conv)
            gnorm_full = jnp.transpose(g_gnorm, (1, 0, 2)).reshape(n_hgrn, d)
        if i == 0:
            y = _rms_fwd(h, norm_mix[0:1], "rms_mix_0")
        (w_in[i],) = arrived(1 + 4 * i, y, f"in_{i}")
        if i % 2 == 0:
            w_gin = jnp.transpose(w_in[i], (1, 0, 2)).reshape(d, N_DEV * c_gin)
            w_in[i] = w_gin[:, :GDN_MAIN]
            w_ab[i] = jnp.pad(w_gin[:, GDN_MAIN:], ((0, 0), (0, AB_PAD - 2 * N_HEADS)))
            projm = _mm(y, w_in[i], "nn", [BF16], f"gdn_proj_{i}")
            projab = _mm(y, w_ab[i], "nn", [F32], f"gdn_proj_ab_{i}")
            o2, st_all, conv_y, dinv_all = _gdn_fwd(projm, projab, conv_full[j], gdn_a_log[j:j + 1],
                                                    gdn_dt_bias[j:j + 1], gdn_onorm[j:j + 1], seqs, f"gdn_fwd_{i}")
            mix = (projm, projab, conv_y, st_all, dinv_all)
        else:
            proj = _mm(y, w_in[i], "nn", [BF16], f"hgrn_proj_{i}")
            o2, o_raw, st_all = _hgrn_fwd(proj, lbs[i:i + 1], gnorm_full[j:j + 1], seqs, f"hgrn_fwd_{i}")
            mix = (proj, o_raw, st_all)
        (w_out[i],) = arrived(2 + 4 * i, o2, f"out_{i}")
        h1, y2 = _mm(o2, w_out[i], "nn", [F32, BF16], f"mix_out_{i}", epilogue=_ep_residual_norm, extras=(h,),
                     vectors=(norm_mlp[i:i + 1],))
        (w_up[i],) = arrived(3 + 4 * i, y2, f"up_{i}")
        u, act = _mm(y2, w_up[i], "nn", [BF16, BF16], f"mlp_up_{i}",
                     epilogue=lambda acc: (acc, jnp.square(jnp.maximum(acc, 0.0))))
        (w_down[i],) = arrived(4 + 4 * i, act, f"down_{i}")
        saved.append((h, y, mix, o2, h1, y2, u, act))
        if i + 1 < DEPTH:
            h, y = _mm(act, w_down[i], "nn", [F32, BF16], f"mlp_down_{i}", epilogue=_ep_residual_norm, extras=(h1,),
                       vectors=(norm_mix[i + 1:i + 2],))
        else:
            h = _mm(act, w_down[i], "nn", [F32], f"mlp_down_{i}", epilogue=lambda acc, res: (res + acc,),
                    extras=(h1,))

    dh, dh_b, d_nf, sq = _loss_head(h, norm_final.reshape(1, d), target, "loss_head")

    d_nmix, d_nmlp = [None] * DEPTH, [None] * DEPTH
    d_conv, d_alog, d_dtb, d_onorm = [None] * n_gdn, [None] * n_gdn, [None] * n_gdn, [None] * n_gdn
    d_lb = [jnp.zeros((1, d), F32)] * DEPTH
    d_gnorm = [None] * n_hgrn
    mlp_handles, mix_handles = [None] * DEPTH, [None] * DEPTH
    token = None
    for i in reversed(range(DEPTH)):
        j = i // 2
        h_in, y, mix, o2, h1, y2, u, act = saved[i]
        g_down = _mm(act, dh_b, "tn", [BF16], f"g_down_{i}", after=token)
        du = _mm(dh_b, w_down[i], "nt", [BF16], f"d_u_{i}",
                 epilogue=lambda acc, uu: (acc * (2.0 * jnp.maximum(uu.astype(F32), 0.0)),), extras=(u,))
        g_up = _mm(y2, du, "tn", [BF16], f"g_up_{i}")
        mlp_handles[i], token = _exchange_start(
            [[_Item(g_down, (N_DEV, r_down, d), _rows_of(r_down), _slot)],
             [_Item(g_up, (N_DEV, d, c_up), _cols_of(c_up), _slot)]], f"scatter_start_mlp_{i}")
        dh1, dh1_b, d_nmlp[i] = _mm(du, w_up[i], "nt", [F32, BF16], f"d_y2_{i}", epilogue=_ep_norm_bwd,
                                     extras=(h1, dh), vectors=(norm_mlp[i:i + 1],), n_sums=1, after=token)
        g_out = _mm(o2, dh1_b, "tn", [BF16], f"g_out_{i}")
        do2 = _mm(dh1_b, w_out[i], "nt", [BF16], f"d_o2_{i}")
        if i % 2 == 0:
            projm, projab, conv_y, st_all, dinv_all = mix
            dpm, dpab, d_conv[j], d_alog[j], d_dtb[j], d_onorm[j] = _gdn_bwd(
                projm, projab, conv_y, conv_full[j], gdn_a_log[j:j + 1], gdn_dt_bias[j:j + 1], gdn_onorm[j:j + 1],
                st_all, dinv_all, do2, seqs, f"gdn_bwd_{i}")
            g_main = _mm(y, dpm, "tn", [BF16], f"g_in_{i}")
            g_ab = _mm(y, dpab, "tn", [BF16], f"g_in_ab_{i}")
            g_in = jnp.concatenate([g_main, g_ab[:, :2 * N_HEADS]], axis=1)
            g_in = jnp.transpose(g_in.reshape(d, N_DEV, c_gin), (1, 0, 2))
            in_item = _Item(g_in, (N_DEV, d, c_gin), _slot, _slot)
            dy_ab = _mm(dpab, w_ab[i], "nt", [F32], f"d_y_ab_{i}")
            dp, dy_extras = dpm, (dy_ab, h_in, dh1)
            dy_epilogue = lambda acc, e, xx, dres, w: _ep_norm_bwd(acc + e, xx, dres, w)
        else:
            proj, o_raw, st_all = mix
            dp, d_lb[i], d_gnorm[j] = _hgrn_bwd(proj, lbs[i:i + 1], gnorm_full[j:j + 1], st_all, o_raw, do2,
                                               seqs, f"hgrn_bwd_{i}")
            g_in = _mm(y, dp, "tn", [BF16], f"g_in_{i}")
            in_item = _Item(g_in, (N_DEV, d, c_hin), _cols_of(c_hin), _slot)
            dy_extras, dy_epilogue = (h_in, dh1), _ep_norm_bwd
        mix_handles[i], token = _exchange_start(
            [[_Item(g_out, (N_DEV, r_out, d), _rows_of(r_out), _slot)], [in_item]], f"scatter_start_mix_{i}")
        dh, dh_b, d_nmix[i] = _mm(dp, w_in[i], "nt", [F32, BF16], f"d_y_{i}", epilogue=dy_epilogue, extras=dy_extras,
                                  vectors=(norm_mix[i:i + 1],), n_sums=1, after=token)
        token = None
    grad_x = dh.reshape(x.shape)

    def landed(handles, k, layers, after, name):
        return [_exchange_wait(handles[i][k], after, f"scatter_wait_{name}_{i}")[0] for i in layers]

    every, even, odd = range(DEPTH), range(0, DEPTH, 2), range(1, DEPTH, 2)
    upd = {}
    upd["mlp_w_down"] = _adamw_slots(mlp_w_down, landed(mlp_handles, 0, every, dh, "down"), m_mlp_w_down,
                                     v_mlp_w_down, "adamw_mlp_w_down")
    upd["mlp_w_up"] = _adamw_slots(mlp_w_up, landed(mlp_handles, 1, every, upd["mlp_w_down"][1], "up"), m_mlp_w_up,
                                   v_mlp_w_up, "adamw_mlp_w_up")
    upd["hgrn_w_out"] = _adamw_slots(hgrn_w_out, landed(mix_handles, 0, odd, upd["mlp_w_up"][1], "out"),
                                     m_hgrn_w_out, v_hgrn_w_out, "adamw_hgrn_w_out")
    upd["hgrn_w_in"] = _adamw_slots(hgrn_w_in, landed(mix_handles, 1, odd, upd["hgrn_w_out"][1], "in"), m_hgrn_w_in,
                                    v_hgrn_w_in, "adamw_hgrn_w_in")

    dlb_rows = jnp.concatenate(d_lb, axis=0)
    tail = jnp.concatenate(
        [jnp.concatenate(d_onorm, axis=1), jnp.concatenate(d_alog, axis=1), jnp.concatenate(d_dtb, axis=1)], axis=1)
    tail = jnp.pad(tail, ((0, 0), (0, d - tail.shape[1])))
    conv_rows = jnp.stack(d_conv).reshape(-1, d)
    packed = jnp.concatenate(
        [jnp.concatenate(d_nmix, axis=0), jnp.concatenate(d_nmlp, axis=0), d_nf, sq, dlb_rows,
         jnp.concatenate(d_gnorm, axis=0), tail, conv_rows], axis=0)
    pad_rows = (-packed.shape[0]) % 8
    packed = jnp.pad(packed, ((0, pad_rows), (0, 0)))
    tot = _all_reduce_small(packed, upd["hgrn_w_in"][1], "reduce_small")

    upd["gdn_w_out"] = _adamw_slots(gdn_w_out, landed(mix_handles, 0, even, tot, "out"),
                                    m_gdn_w_out, v_gdn_w_out, "adamw_gdn_w_out")
    upd["gdn_w_in"] = _adamw_slots(gdn_w_in, landed(mix_handles, 1, even, upd["gdn_w_out"][1], "in"), m_gdn_w_in,
                                   v_gdn_w_in, "adamw_gdn_w_in")

    def update(name, w, g, m, v):
        shape = w.shape
        c = shape[-1]
        res = _adamw(w.reshape(-1, c), g.reshape(-1, c), m.reshape(-1, c), v.reshape(-1, c), "adamw_" + name)
        return [g.reshape(shape)] + [o.reshape(shape) for o in res]

    r0 = 0
    g_nmix = tot[r0:r0 + DEPTH]; r0 += DEPTH
    g_nmlp = tot[r0:r0 + DEPTH]; r0 += DEPTH
    g_nf = tot[r0]; r0 += 1
    loss = tot[r0, 0]; r0 += 1
    g_lb = _lb_bwd(hgrn_lb_logits, tot[r0:r0 + DEPTH], "lb_bwd"); r0 += DEPTH
    g_gnorm_full = tot[r0:r0 + n_hgrn]; r0 += n_hgrn
    t_row = tot[r0]; r0 += 1
    g_conv_full = tot[r0:r0 + n_gdn * CONV_K * 3].reshape(n_gdn, CONV_K, 3 * d)
    g_onorm = t_row[0:n_gdn * HEAD_DIM].reshape(n_gdn, HEAD_DIM)
    o1 = n_gdn * HEAD_DIM
    g_alog = t_row[o1:o1 + n_gdn * N_HEADS].reshape(n_gdn, N_HEADS)
    g_dtb = t_row[o1 + n_gdn * N_HEADS:o1 + 2 * n_gdn * N_HEADS].reshape(n_gdn, N_HEADS)
    c_gn, c_cv = hgrn_gnorm.shape[1], gdn_conv.shape[2]
    g_gnorm = lax.dynamic_slice_in_dim(g_gnorm_full, me_i * c_gn, c_gn, axis=1)
    g_conv = lax.dynamic_slice_in_dim(g_conv_full, me_i * c_cv, c_cv, axis=2)

    upd["gdn_conv"] = update("gdn_conv", gdn_conv, g_conv, m_gdn_conv, v_gdn_conv)
    upd["gdn_a_log"] = update("gdn_a_log", gdn_a_log, g_alog, m_gdn_a_log, v_gdn_a_log)
    upd["gdn_dt_bias"] = update("gdn_dt_bias", gdn_dt_bias, g_dtb, m_gdn_dt_bias, v_gdn_dt_bias)
    upd["gdn_onorm"] = update("gdn_onorm", gdn_onorm, g_onorm, m_gdn_onorm, v_gdn_onorm)
    upd["hgrn_lb_logits"] = update("hgrn_lb_logits", hgrn_lb_logits, g_lb, m_hgrn_lb_logits, v_hgrn_lb_logits)
    upd["hgrn_gnorm"] = update("hgrn_gnorm", hgrn_gnorm, g_gnorm, m_hgrn_gnorm, v_hgrn_gnorm)
    upd["norm_mix"] = update("norm_mix", norm_mix, g_nmix, m_norm_mix, v_norm_mix)
    upd["norm_mlp"] = update("norm_mlp", norm_mlp, g_nmlp, m_norm_mlp, v_norm_mlp)
    upd["norm_final"] = update("norm_final", norm_final, g_nf, m_norm_final, v_norm_final)

    order = ["gdn_w_in", "gdn_conv", "gdn_a_log", "gdn_dt_bias", "gdn_onorm", "gdn_w_out", "hgrn_w_in",
             "hgrn_lb_logits", "hgrn_gnorm", "hgrn_w_out", "norm_mix", "norm_mlp", "mlp_w_up", "mlp_w_down",
             "norm_final"]
    outs = [loss, grad_x]
    for k in range(4):
        outs += [upd[name][k] for name in order]
    return tuple(outs)
```

```python
import functools

import numpy as np
import jax
import jax.numpy as jnp
from jax import lax
from jax.experimental import pallas as pl
from jax.experimental.pallas import tpu as pltpu

F32 = jnp.float32
BF16 = jnp.bfloat16

D_MODEL = 1024
N_HEADS = 8
HEAD_DIM = 128
CHUNK = 64
SUB = 16
N_SUB = CHUNK // SUB
CONV_K = 4
HALO = 16
EPS = 1e-6
DEPTH = 4
N_DEV = 8
GDN_MAIN = 4 * D_MODEL
GDN_IN = GDN_MAIN + 2 * N_HEADS
AB_PAD = 128
HEAD_GROUP = 8
LANE_BLOCK = 256
ROW_BLOCK = 16
BLOCK_UNROLL = 4

ADAM_LR = 0.001
ADAM_B1 = 0.9
ADAM_B2 = 0.999
ADAM_EPS = 1e-08
ADAM_WD = 0.01
ADAM_STEP = 10

VMEM_LIMIT = 56 * 1024 * 1024
MM_TILE = 1024
MM_VMEM_BUDGET = 40 * 1024 * 1024

_DIMS = {
    "nn": (((1,), (0,)), ((), ())),
    "nt": (((1,), (1,)), ((), ())),
    "tn": (((0,), (0,)), ((), ())),
}


def _parts(x, n):
    if n == 1 and x.dtype == BF16:
        return [x]
    out = []
    r = x.astype(F32)
    for i in range(n):
        p = r.astype(BF16)
        out.append(p)
        if i + 1 < n:
            r = r - p.astype(F32)
    return out


def _dot_raw(a, b, mode, na, nb):
    ap, bp = _parts(a, na), _parts(b, nb)
    nmax = max(na, nb)
    pairs = [(i, j) for i in range(na) for j in range(nb) if i + j < nmax]
    ka = 0 if mode == "tn" else 1
    kb = 1 if mode == "nt" else 0
    xa = ap[0] if len(pairs) == 1 else jnp.concatenate([ap[i] for i, _ in pairs], axis=ka)
    xb = bp[0] if len(pairs) == 1 else jnp.concatenate([bp[j] for _, j in pairs], axis=kb)
    return lax.dot_general(xa, xb, _DIMS[mode], preferred_element_type=F32)


@functools.partial(jax.custom_vjp, nondiff_argnums=(2, 3, 4))
def _dot(a, b, mode, na, nb):
    return _dot_raw(a, b, mode, na, nb)


def _dot_fwd(a, b, mode, na, nb):
    return _dot_raw(a, b, mode, na, nb), (a, b)


def _dot_bwd(mode, na, nb, res, ct):
    a, b = res
    if mode == "nn":
        da = _dot_raw(ct, b, "nt", 1, 1)
        db = _dot_raw(a, ct, "tn", 1, 1)
    elif mode == "nt":
        da = _dot_raw(ct, b, "nn", 1, 1)
        db = _dot_raw(ct, a, "tn", 1, 1)
    else:
        da = _dot_raw(b, ct, "nt", 1, 1)
        db = _dot_raw(a, ct, "nn", 1, 1)
    return da.astype(a.dtype), db.astype(b.dtype)


_dot.defvjp(_dot_fwd, _dot_bwd)


N_EXACT = 3


@jax.custom_vjp
def _dot01(x, m_wide, m):
    return lax.dot_general(m_wide, jnp.concatenate(_parts(x, N_EXACT), axis=0), _DIMS["nn"], preferred_element_type=F32)


def _dot01_fwd(x, m_wide, m):
    return _dot01(x, m_wide, m), (m_wide, m)


def _dot01_bwd(res, ct):
    m_wide, m = res
    dx = lax.dot_general(m, ct.astype(BF16), _DIMS["tn"], preferred_element_type=F32)
    return dx, jnp.zeros_like(m_wide), jnp.zeros_like(m)


_dot01.defvjp(_dot01_fwd, _dot01_bwd)


def _thrice(m):
    return jnp.concatenate([m] * N_EXACT, axis=1).astype(BF16), m.astype(BF16)


def _iota2(shape, dim):
    return lax.broadcasted_iota(jnp.int32, shape, dim)


def _tril_f32(n):
    return (_iota2((n, n), 0) >= _iota2((n, n), 1)).astype(F32)


def _cumsum_rows(g):
    return _dot(_tril_f32(g.shape[0]), g, "nn", 1, 3)


def _below_block(n, b):
    ri, ci = _iota2((n, n), 0) // b, _iota2((n, n), 1) // b
    return (ri == ci + 1) & (ri % 2 == 1)


def _half_inverses(L):
    n = L.shape[0]
    eye = (_iota2((n, n), 0) == _iota2((n, n), 1)).astype(F32)
    d = eye - jnp.where(_below_block(n, 1), L, 0.0)
    b = 2
    while 2 * b < n:
        e = jnp.where(_below_block(n, b), L, 0.0)
        d = d - _dot_raw(d, _dot_raw(e, d, "nn", 2, 2), "nn", 2, 2)
        b *= 2
    return d, jnp.where(_below_block(n, b), L, 0.0)


def _solve_with(d, e, rhs):
    y = _dot_raw(d, rhs, "nn", 2, 2)
    return y - _dot_raw(d, _dot_raw(e, y, "nn", 2, 2), "nn", 2, 2)


@jax.custom_vjp
def _solve_unit_lower(L, rhs, d):
    n = L.shape[0]
    return _solve_with(d, jnp.where(_below_block(n, n // 2), L, 0.0), rhs)


def _solve_fwd(L, rhs, d):
    n = L.shape[0]
    e = jnp.where(_below_block(n, n // 2), L, 0.0)
    sol = _solve_with(d, e, rhs)
    return sol, (d, e, sol)


def _solve_bwd(res, ct):
    d, e, sol = res
    y = _dot_raw(d, ct - _dot_raw(e, _dot_raw(d, ct, "tn", 2, 2), "tn", 2, 2), "tn", 2, 2)
    return -_dot_raw(y, sol, "nt", 2, 2), y, jnp.zeros_like(d)


_solve_unit_lower.defvjp(_solve_fwd, _solve_bwd)


def _softplus(x):
    return jnp.maximum(x, 0.0) + jnp.log1p(jnp.exp(-jnp.abs(x)))


def _rms(x, w):
    return x * lax.rsqrt(jnp.mean(x * x, axis=-1, keepdims=True) + EPS) * w


HG_LEVELS = (32, 16, 8, 4, 2, 1)


def _hg_level_sums():
    i = np.arange(CHUNK)[:, None]
    m = np.arange(CHUNK)[None, :]
    to_row = [(m <= i) & (m // b == i // b) for b in HG_LEVELS]
    to_col = [(m > i) & (m // b == i // b) for b in HG_LEVELS if b > 1]
    return _thrice(jnp.asarray(np.concatenate(to_row + to_col + [m <= i]), F32))


def _hg_level_masks():
    i = np.arange(CHUNK)[:, None]
    j = np.arange(CHUNK)[None, :]
    return jnp.asarray(np.stack([(i // b == j // b + 1) & ((i // b) % 2 == 1) for b in HG_LEVELS]), F32)


def _hg_pre(qraw, f, lb, sums):
    g = jnp.log(lb + (1.0 - lb) * jax.nn.sigmoid(f))
    k = (1.0 - lb) * jax.nn.sigmoid(-f)
    q = jax.nn.silu(qraw) * (HEAD_DIM ** -0.5)
    return q, k, _dot01(g, *sums)


def _hg_head(st, q, k, v, e, masks):
    nl = len(HG_LEVELS)
    eye = (_iota2((CHUNK, CHUNK), 0) == _iota2((CHUNK, CHUNK), 1)).astype(F32)
    a = eye * jnp.sum(q * k, axis=-1, keepdims=True)
    for l, b in enumerate(HG_LEVELS):
        rows = q * jnp.exp(e[l * CHUNK:(l + 1) * CHUNK])
        cols = k * jnp.exp(e[(nl + l) * CHUNK:(nl + l + 1) * CHUNK]) if b > 1 else k
        a = a + masks[l] * _dot(rows, cols, "nt", 1, 1)
    gc = e[(2 * nl - 1) * CHUNK:2 * nl * CHUNK]
    o = _dot(a, v, "nn", 1, 1) + _dot(q * jnp.exp(gc), st, "nt", 1, 1)
    g_last = gc[CHUNK - 1:CHUNK]
    st_new = st * jnp.exp(g_last) + _dot(v, k * jnp.exp(g_last - gc), "tn", 1, 1)
    return o, st_new


_HG_HEADS = jax.vmap(_hg_head, in_axes=(0, 0, 0, 0, 0, None))


def _hg_post(o, gate, gw):
    return _rms(o, gw) * jax.nn.silu(gate)


def _gd_conv(xp, cw):
    off = HALO - (CONV_K - 1)
    y = cw[0:1] * xp[off:off + CHUNK]
    for kk in range(1, CONV_K):
        y = y + cw[kk:kk + 1] * xp[off + kk:off + kk + CHUNK]
    return y


def _gd_conv_bwd(xp, cw, y, dc):
    off = HALO - (CONV_K - 1)
    sig = jax.nn.sigmoid(y)
    dy = dc * (sig * (1.0 + y * (1.0 - sig)))
    dxp, dcw = None, []
    for kk in range(CONV_K):
        moved = jnp.pad(dy, ((off + kk, HALO - off - kk), (0, 0)))
        term = cw[kk:kk + 1] * moved
        dxp = term if dxp is None else dxp + term
        dcw.append(jnp.sum(xp * moved, axis=0, keepdims=True))
    return dxp, jnp.concatenate(dcw, axis=0)


def _gd_gates(a, b, alog, dtb):
    beta = jax.nn.sigmoid(b)
    g = -jnp.exp(alog) * _softplus(a + dtb)
    expand = (_iota2((N_HEADS, D_MODEL), 1) // HEAD_DIM == _iota2((N_HEADS, D_MODEL), 0)).astype(F32)
    g_x = _dot(g, expand, "nn", 3, 1)
    after = (_iota2((CHUNK, D_MODEL), 0) > _iota2((CHUNK, D_MODEL), 1) % HEAD_DIM).astype(F32)
    sums = _dot01(jnp.concatenate([g_x, g_x * after], axis=1), *_thrice(_tril_f32(CHUNK)))
    return _dot(beta, expand, "nn", 3, 1), sums


def _gd_head(st, q, k, v, beta, gc, diff, gate, onw, dinv=None):
    q = q * lax.rsqrt(jnp.sum(q * q, axis=-1, keepdims=True) + EPS) * (HEAD_DIM ** -0.5)
    k = k * lax.rsqrt(jnp.sum(k * k, axis=-1, keepdims=True) + EPS)
    ri = _iota2((CHUNK, CHUNK), 0)
    ci = _iota2((CHUNK, CHUNK), 1)
    decay = jnp.exp(jnp.where(ri >= ci, diff[:, 0:CHUNK], -jnp.inf))
    kb = k * beta
    egc = jnp.exp(gc)
    L = jnp.where(ri > ci, _dot(kb, k, "nt", 1, 1) * decay, 0.0)
    made = dinv is None
    if made:
        dinv = _half_inverses(L)[0]
    sol = _solve_unit_lower(L, jnp.concatenate([v * beta, kb * egc], axis=1), dinv)
    u = sol[:, 0:HEAD_DIM]
    w = sol[:, HEAD_DIM:2 * HEAD_DIM]
    a_qk = jnp.where(ri >= ci, _dot(q, k, "nt", 1, 1) * decay, 0.0)
    g_last = gc[CHUNK - 1:CHUNK]
    v_new = u - _dot(w, st, "nt", 1, 1)
    o = _dot(q * egc, st, "nt", 1, 1) + _dot(a_qk, v_new, "nn", 1, 1)
    st_new = st * jnp.exp(g_last) + _dot(v_new, k * jnp.exp(g_last - gc), "tn", 1, 1)
    out = (_rms(o, onw) * jax.nn.silu(gate), st_new)
    return out + (dinv,) if made else out


def _params(*sem):
    return pltpu.CompilerParams(dimension_semantics=sem, vmem_limit_bytes=VMEM_LIMIT)


def _tile(n, pref):
    t = min(n, pref)
    assert n % t == 0, (n, pref)
    return t


def _mm_tiles(m, n, k, a_size, b_size, tile_sizes):
    tm, tn, tk = _tile(m, MM_TILE), _tile(n, MM_TILE), k

    def need(tm, tn, tk):
        acc = 4 * tm * tn * (2 if tk < k else 1)
        return 2 * (tm * tk * a_size + tk * tn * b_size + tm * tn * sum(tile_sizes)) + acc

    while need(tm, tn, tk) > MM_VMEM_BUDGET:
        if tk > 2048 or (tk > 512 and tm <= 512):
            tk //= 2
        else:
            tm //= 2
    return tm, tn, tk


def _mm(a, b, mode, out_dtypes, name, epilogue=None, extras=(), vectors=(), n_sums=0, after=None):
    if mode == "nn":
        (m, k), (k2, n) = a.shape, b.shape
    elif mode == "nt":
        (m, k), (n, k2) = a.shape, b.shape
    else:
        (k, m), (k2, n) = a.shape, b.shape
    assert k == k2, (a.shape, b.shape, mode)
    tm, tn, tk = _mm_tiles(m, n, k, a.dtype.itemsize, b.dtype.itemsize,
                           [e.dtype.itemsize for e in extras] + [jnp.dtype(dt).itemsize for dt in out_dtypes])
    nk = k // tk
    assert not (vectors or n_sums) or tn == n, "whole-row epilogues need the result tile to span the rows"
    ne, no, nafter = len(extras) + len(vectors), len(out_dtypes), int(after is not None)
    if epilogue is None:
        epilogue = lambda acc: (acc,)

    def body(*refs):
        a_ref, b_ref = refs[0], refs[1]
        ex = refs[2:2 + ne]
        outs = refs[2 + ne + nafter:2 + ne + nafter + no]
        sums = refs[2 + ne + nafter + no:2 + ne + nafter + no + n_sums]
        part = lax.dot_general(a_ref[...].astype(BF16), b_ref[...].astype(BF16), _DIMS[mode],
                               preferred_element_type=F32)

        def finish(acc):
            vals = epilogue(acc, *[e[...] for e in ex])
            for o_ref, val in zip(outs, vals[:no]):
                o_ref[...] = val.astype(o_ref.dtype)
            for s_ref, val in zip(sums, vals[no:]):
                @pl.when(pl.program_id(0) == 0)
                def _(s_ref=s_ref, val=val):
                    s_ref[...] = val

                @pl.when(pl.program_id(0) > 0)
                def _(s_ref=s_ref, val=val):
                    s_ref[...] += val

        if nk == 1:
            finish(part)
        else:
            acc_ref = refs[-1]
            kk = pl.program_id(2)

            @pl.when(kk == 0)
            def _():
                acc_ref[...] = part

            @pl.when(kk > 0)
            def _():
                acc_ref[...] += part

            @pl.when(kk == nk - 1)
            def _():
                finish(acc_ref[...])

    if mode == "tn":
        a_spec = pl.BlockSpec((tk, tm), lambda i, j, kk: (kk, i))
    else:
        a_spec = pl.BlockSpec((tm, tk), lambda i, j, kk: (i, kk))
    if mode == "nt":
        b_spec = pl.BlockSpec((tn, tk), lambda i, j, kk: (j, kk))
    else:
        b_spec = pl.BlockSpec((tk, tn), lambda i, j, kk: (kk, j))
    o_spec = pl.BlockSpec((tm, tn), lambda i, j, kk: (i, j))
    v_spec = pl.BlockSpec((1, tn), lambda i, j, kk: (0, j))
    res = pl.pallas_call(
        body,
        name=name,
        grid=(m // tm, n // tn, nk),
        in_specs=([a_spec, b_spec] + [o_spec] * len(extras) + [v_spec] * len(vectors)
                  + [pl.BlockSpec(memory_space=pl.ANY)] * nafter),
        out_specs=[o_spec] * no + [v_spec] * n_sums,
        out_shape=[jax.ShapeDtypeStruct((m, n), dt) for dt in out_dtypes] + [jax.ShapeDtypeStruct((1, n), F32)] * n_sums,
        scratch_shapes=[pltpu.VMEM((tm, tn), F32)] if nk > 1 else [],
        compiler_params=_params(*(("arbitrary",) * 3 if n_sums else ("parallel", "parallel", "arbitrary"))),
    )(a, b, *extras, *vectors, *([after] if nafter else []))
    return res[0] if no + n_sums == 1 else res


def _ep_residual_norm(acc, res, w):
    h = res + acc
    return h, _rms(h, w)


def _ep_norm_bwd(acc, x, dres, w):
    _, vjp = jax.vjp(_rms, x, w)
    dx, dw = vjp(acc)
    dx = dres + dx
    return dx, dx, dw


def _rms_fwd(x, w, name, tm=512):
    n, d = x.shape
    tm = _tile(n, tm)

    def body(x_ref, w_ref, y_ref):
        y_ref[...] = _rms(x_ref[...], w_ref[...]).astype(y_ref.dtype)

    return pl.pallas_call(
        body, name=name, grid=(n // tm,),
        in_specs=[pl.BlockSpec((tm, d), lambda i: (i, 0)), pl.BlockSpec((1, d), lambda i: (0, 0))],
        out_specs=pl.BlockSpec((tm, d), lambda i: (i, 0)),
        out_shape=jax.ShapeDtypeStruct((n, d), BF16),
        compiler_params=_params("arbitrary"),
    )(x, w)


def _loss_head(h, w, target, name, tm=512):
    n, d = h.shape
    tm = _tile(n, tm)

    def body(h_ref, w_ref, t_ref, dh_ref, dhb_ref, dw_ref, sq_ref):
        y, vjp = jax.vjp(_rms, h_ref[...], w_ref[...])
        err = y - t_ref[...]
        dh, dw = vjp(err * (1.0 / d))
        dh_ref[...] = dh
        dhb_ref[...] = dh.astype(dhb_ref.dtype)
        sq = jnp.sum(err * err, axis=0, keepdims=True)

        @pl.when(pl.program_id(0) == 0)
        def _():
            dw_ref[...] = dw
            sq_ref[...] = sq

        @pl.when(pl.program_id(0) > 0)
        def _():
            dw_ref[...] += dw
            sq_ref[...] += sq

        @pl.when(pl.program_id(0) == n // tm - 1)
        def _():
            total = jnp.sum(sq_ref[...], axis=1, keepdims=True) * (0.5 / d)
            sq_ref[...] = jnp.broadcast_to(total, sq_ref.shape)

    row = pl.BlockSpec((tm, d), lambda i: (i, 0))
    vec = pl.BlockSpec((1, d), lambda i: (0, 0))
    return pl.pallas_call(
        body, name=name, grid=(n // tm,),
        in_specs=[row, vec, row],
        out_specs=[row, row, vec, vec],
        out_shape=[jax.ShapeDtypeStruct((n, d), F32), jax.ShapeDtypeStruct((n, d), BF16),
                   jax.ShapeDtypeStruct((1, d), F32), jax.ShapeDtypeStruct((1, d), F32)],
        compiler_params=_params("arbitrary"),
    )(h, w, target)


def _lower_bounds(logits):
    sm = jax.nn.softmax(logits, axis=0)
    rows = [sm[0:1] * 0.0]
    for r in range(1, DEPTH):
        rows.append(rows[-1] + sm[r:r + 1])
    return jnp.concatenate(rows, axis=0)


def _lb_fwd(logits, name):
    def body(l_ref, o_ref):
        o_ref[...] = _lower_bounds(l_ref[...])

    return pl.pallas_call(body, name=name, out_shape=jax.ShapeDtypeStruct(logits.shape, F32))(logits)


def _lb_bwd(logits, dlb, name):
    def body(l_ref, d_ref, o_ref):
        _, vjp = jax.vjp(_lower_bounds, l_ref[...])
        (o_ref[...],) = vjp(d_ref[...])

    return pl.pallas_call(body, name=name, out_shape=jax.ShapeDtypeStruct(logits.shape, F32))(logits, dlb)


def _head_slice(h):
    if isinstance(h, int):
        return pl.ds(h * HEAD_DIM, HEAD_DIM)
    return pl.ds(pl.multiple_of(h * HEAD_DIM, HEAD_DIM), HEAD_DIM)


def _head_groups(group_body):
    if HEAD_GROUP == N_HEADS:
        group_body(list(range(N_HEADS)))
        return

    def trip(i, carry):
        group_body([i * HEAD_GROUP + t for t in range(HEAD_GROUP)])
        return carry

    lax.fori_loop(0, N_HEADS // HEAD_GROUP, trip, 0)


def _stack_heads(ref, hs, first=0):
    return jnp.stack([ref[:, _head_slice(h + first)] for h in hs])


def _unstack_heads(ref, hs, val, first=0):
    for t, h in enumerate(hs):
        ref[:, _head_slice(h + first)] = val[t].astype(ref.dtype)


_GD_HEADS = jax.vmap(_gd_head, in_axes=(0, 0, 0, 0, 0, 0, 0, 0, None))
_GD_HEADS_AGAIN = jax.vmap(_gd_head, in_axes=(0, 0, 0, 0, 0, 0, 0, 0, None, 0))


def _hgrn_fwd(proj, lb, gw, seqs, name):
    n = proj.shape[0]
    nc = n // seqs // CHUNK
    d = D_MODEL

    sums, masks = _hg_level_sums(), _hg_level_masks()

    def body(p_ref, lb_ref, gw_ref, sums_wide_ref, sums_once_ref, masks_ref, o2_ref, o_ref, st_all_ref,
             st_sc, q_sc, k_sc, v_sc, e_sc):
        @pl.when(pl.program_id(1) == 0)
        def _():
            st_sc[...] = jnp.zeros_like(st_sc)

        sums_refs = (sums_wide_ref, sums_once_ref)
        _lane_blocks(d, functools.partial(_hg_pre_block, p_ref, lb_ref, sums_refs, q_sc, k_sc, v_sc, e_sc))
        st_all_ref[0] = st_sc[...]

        def group(hs):
            sts = pl.ds(hs[0], len(hs))
            o, st_new = _HG_HEADS(st_sc[sts], *[_stack_heads(r, hs) for r in (q_sc, k_sc, v_sc, e_sc)], masks_ref[...])
            _unstack_heads(o_ref, hs, o)
            st_sc[sts] = st_new

        _head_groups(group)

        def post(rows):
            gate = p_ref[rows, 3 * d:4 * d].astype(F32)
            o2_ref[rows, :] = _hg_post(o_ref[rows, :], gate, gw_ref[...]).astype(o2_ref.dtype)

        _row_blocks(CHUNK, post)

    idx = lambda b, c: (b * nc + c, 0)
    vec = pl.BlockSpec((1, d), lambda b, c: (0, 0))
    act = pl.BlockSpec((CHUNK, d), idx)
    return pl.pallas_call(
        body, name=name, grid=(seqs, nc),
        in_specs=[pl.BlockSpec((CHUNK, 4 * d), idx), vec, vec] + [pl.BlockSpec(s.shape, lambda b, c: (0, 0)) for s in sums]
        + [pl.BlockSpec(masks.shape, lambda b, c: (0, 0, 0))],
        out_specs=[act, act, pl.BlockSpec((1, N_HEADS, HEAD_DIM, HEAD_DIM), lambda b, c: (b * nc + c, 0, 0, 0))],
        out_shape=[jax.ShapeDtypeStruct((n, d), BF16), jax.ShapeDtypeStruct((n, d), F32),
                   jax.ShapeDtypeStruct((n // CHUNK, N_HEADS, HEAD_DIM, HEAD_DIM), F32)],
        scratch_shapes=[pltpu.VMEM((N_HEADS, HEAD_DIM, HEAD_DIM), F32)] + [pltpu.VMEM((CHUNK, d), F32)] * 3
        + [pltpu.VMEM((sums[0].shape[0], d), F32)],
        compiler_params=_params("arbitrary", "arbitrary"),
    )(proj, lb, gw, *sums, masks)


def _hgrn_bwd(proj, lb, gw, st_all, o, do2, seqs, name):
    n = proj.shape[0]
    nc = n // seqs // CHUNK
    d = D_MODEL

    sums, masks = _hg_level_sums(), _hg_level_masks()

    def body(p_ref, lb_ref, gw_ref, sums_wide_ref, sums_once_ref, masks_ref, st_all_ref, o_ref, do2_ref,
             dp_ref, dlb_ref, dgw_ref,
             dst_sc, q_sc, k_sc, v_sc, e_sc, do_sc, dq_sc, dk_sc, dv_sc, de_sc, dgw_sc):
        first = (pl.program_id(0) == 0) & (pl.program_id(1) == 0)

        @pl.when(pl.program_id(1) == 0)
        def _():
            dst_sc[...] = jnp.zeros_like(dst_sc)

        sums_refs = (sums_wide_ref, sums_once_ref)
        _lane_blocks(d, functools.partial(_hg_pre_block, p_ref, lb_ref, sums_refs, q_sc, k_sc, v_sc, e_sc))
        dgw_sc[...] = jnp.zeros_like(dgw_sc)

        def post_bwd(rows):
            _, vjp = jax.vjp(_hg_post, o_ref[rows, :], p_ref[rows, 3 * d:4 * d].astype(F32), gw_ref[...])
            do_sc[rows, :], dgate, dgw = vjp(do2_ref[rows, :].astype(F32))
            dp_ref[rows, 3 * d:4 * d] = dgate.astype(dp_ref.dtype)
            dgw_sc[...] += dgw

        _row_blocks(CHUNK, post_bwd)

        def group(hs):
            sts = pl.ds(hs[0], len(hs))
            level_masks = masks_ref[...]
            _, vjp = jax.vjp(lambda *a: _HG_HEADS(*a, level_masks), st_all_ref[0, sts],
                             *[_stack_heads(r, hs) for r in (q_sc, k_sc, v_sc, e_sc)])
            grads = vjp((_stack_heads(do_sc, hs), dst_sc[sts]))
            dst_sc[sts] = grads[0]
            for r, val in zip((dq_sc, dk_sc, dv_sc, de_sc), grads[1:]):
                _unstack_heads(r, hs, val)

        _head_groups(group)

        def pre_bwd(at):
            sl = at()
            level_sums = (sums_wide_ref[...], sums_once_ref[...])
            _, vjp = jax.vjp(lambda qraw, f, lb: _hg_pre(qraw, f, lb, level_sums), p_ref[:, sl].astype(F32),
                             p_ref[:, at(d)].astype(F32), lb_ref[:, sl])
            dqraw, df, dlb = vjp((dq_sc[:, sl], dk_sc[:, sl], de_sc[:, sl]))
            dp_ref[:, sl] = dqraw.astype(dp_ref.dtype)
            dp_ref[:, at(d)] = df.astype(dp_ref.dtype)
            dp_ref[:, at(2 * d)] = dv_sc[:, sl].astype(dp_ref.dtype)

            @pl.when(first)
            def _():
                dlb_ref[:, sl] = dlb

            @pl.when(jnp.logical_not(first))
            def _():
                dlb_ref[:, sl] += dlb

        _lane_blocks(d, pre_bwd)

        @pl.when(first)
        def _():
            dgw_ref[...] = dgw_sc[...]

        @pl.when(jnp.logical_not(first))
        def _():
            dgw_ref[...] += dgw_sc[...]

    idx = lambda b, c: (b * nc + nc - 1 - c, 0)
    vec = pl.BlockSpec((1, d), lambda b, c: (0, 0))
    act = pl.BlockSpec((CHUNK, d), idx)
    wide = pl.BlockSpec((CHUNK, 4 * d), idx)
    return pl.pallas_call(
        body, name=name, grid=(seqs, nc),
        in_specs=[wide, vec, vec] + [pl.BlockSpec(s.shape, lambda b, c: (0, 0)) for s in sums] + [
                  pl.BlockSpec(masks.shape, lambda b, c: (0, 0, 0)),
                  pl.BlockSpec((1, N_HEADS, HEAD_DIM, HEAD_DIM), lambda b, c: (b * nc + nc - 1 - c, 0, 0, 0)),
                  act, act],
        out_specs=[wide, vec, vec],
        out_shape=[jax.ShapeDtypeStruct((n, 4 * d), BF16), jax.ShapeDtypeStruct((1, d), F32),
                   jax.ShapeDtypeStruct((1, d), F32)],
        scratch_shapes=[pltpu.VMEM((N_HEADS, HEAD_DIM, HEAD_DIM), F32)]
        + [pltpu.VMEM((CHUNK, d), F32)] * 3 + [pltpu.VMEM((sums[0].shape[0], d), F32)]
        + [pltpu.VMEM((CHUNK, d), F32)] * 4 + [pltpu.VMEM((sums[0].shape[0], d), F32), pltpu.VMEM((1, d), F32)],
        compiler_params=_params("arbitrary", "arbitrary"),
    )(proj, lb, gw, *sums, masks, st_all, o, do2)


def _lane_blocks(width, block_body):
    def trip(j, carry):
        block_body(lambda base=0: pl.ds(pl.multiple_of(j * LANE_BLOCK + base, LANE_BLOCK), LANE_BLOCK))
        return carry

    lax.fori_loop(0, width // LANE_BLOCK, trip, 0, unroll=BLOCK_UNROLL)


def _row_blocks(rows, block_body):
    def trip(j, carry):
        block_body(pl.ds(pl.multiple_of(j * ROW_BLOCK, ROW_BLOCK), ROW_BLOCK))
        return carry

    lax.fori_loop(0, rows // ROW_BLOCK, trip, 0, unroll=BLOCK_UNROLL)


def _hg_pre_block(p_ref, lb_ref, sums_refs, q_sc, k_sc, v_sc, e_sc, at):
    sl = at()
    q_sc[:, sl], k_sc[:, sl], e_sc[:, sl] = _hg_pre(
        p_ref[:, sl].astype(F32), p_ref[:, at(D_MODEL)].astype(F32), lb_ref[:, sl], [r[...] for r in sums_refs])
    v_sc[:, sl] = p_ref[:, at(2 * D_MODEL)].astype(F32)


def _gd_xp(halo_ref, p_ref, sl, first_chunk):
    halo = jnp.where(first_chunk, 0.0, halo_ref[:, sl].astype(F32))
    return jnp.concatenate([halo, p_ref[:, sl].astype(F32)], axis=0)


def _gdn_fwd(projm, projab, cw, alog, dtb, onw, seqs, name):
    n = projm.shape[0]
    nc = n // seqs // CHUNK
    d = D_MODEL
    per_halo = CHUNK // HALO

    def body(p_ref, halo_ref, ab_ref, cw_ref, alog_ref, dtb_ref, onw_ref, o2_ref, st_all_ref, y_ref, dinv_ref,
             st_sc, c_sc, beta_sc, g_sc):
        @pl.when(pl.program_id(1) == 0)
        def _():
            st_sc[...] = jnp.zeros_like(st_sc)

        def conv(at):
            sl = at()
            y = _gd_conv(_gd_xp(halo_ref, p_ref, sl, pl.program_id(1) == 0), cw_ref[:, sl])
            y_ref[:, sl] = y
            c_sc[:, sl] = jax.nn.silu(y)

        _lane_blocks(3 * d, conv)
        beta_sc[...], g_sc[...] = _gd_gates(ab_ref[:, 0:N_HEADS], ab_ref[:, N_HEADS:2 * N_HEADS], alog_ref[...],
                                            dtb_ref[...])
        st_all_ref[0] = st_sc[...]

        def group(hs):
            sts = pl.ds(hs[0], len(hs))
            o2, st_new, dinv_ref[0, sts] = _GD_HEADS(
                st_sc[sts], _stack_heads(c_sc, hs), _stack_heads(c_sc, hs, N_HEADS), _stack_heads(c_sc, hs, 2 * N_HEADS),
                _stack_heads(beta_sc, hs), _stack_heads(g_sc, hs), _stack_heads(g_sc, hs, N_HEADS),
                _stack_heads(p_ref, hs, 3 * N_HEADS).astype(F32), onw_ref[...])
            _unstack_heads(o2_ref, hs, o2)
            st_sc[sts] = st_new

        _head_groups(group)

    idx = lambda b, c: (b * nc + c, 0)
    const = lambda b, c: (0, 0)
    return pl.pallas_call(
        body, name=name, grid=(seqs, nc),
        in_specs=[pl.BlockSpec((CHUNK, 4 * d), idx),
                  pl.BlockSpec((HALO, 3 * d), lambda b, c: (jnp.maximum((b * nc + c) * per_halo - 1, 0), 0)),
                  pl.BlockSpec((CHUNK, AB_PAD), idx),
                  pl.BlockSpec((CONV_K, 3 * d), const), pl.BlockSpec((1, N_HEADS), const),
                  pl.BlockSpec((1, N_HEADS), const), pl.BlockSpec((1, HEAD_DIM), const)],
        out_specs=[pl.BlockSpec((CHUNK, d), idx),
                   pl.BlockSpec((1, N_HEADS, HEAD_DIM, HEAD_DIM), lambda b, c: (b * nc + c, 0, 0, 0)),
                   pl.BlockSpec((CHUNK, 3 * d), idx),
                   pl.BlockSpec((1, N_HEADS, CHUNK, CHUNK), lambda b, c: (b * nc + c, 0, 0, 0))],
        out_shape=[jax.ShapeDtypeStruct((n, d), BF16),
                   jax.ShapeDtypeStruct((n // CHUNK, N_HEADS, HEAD_DIM, HEAD_DIM), F32),
                   jax.ShapeDtypeStruct((n, 3 * d), F32),
                   jax.ShapeDtypeStruct((n // CHUNK, N_HEADS, CHUNK, CHUNK), F32)],
        scratch_shapes=[pltpu.VMEM((N_HEADS, HEAD_DIM, HEAD_DIM), F32), pltpu.VMEM((CHUNK, 3 * d), F32),
                        pltpu.VMEM((CHUNK, d), F32), pltpu.VMEM((CHUNK, 2 * d), F32)],
        compiler_params=_params("arbitrary", "arbitrary"),
    )(projm, projm, projab, cw, alog, dtb, onw)


def _gdn_bwd(projm, projab, conv_y, cw, alog, dtb, onw, st_all, dinv_all, do2, seqs, name):
    n = projm.shape[0]
    nc = n // seqs // CHUNK
    d = D_MODEL
    per_halo = CHUNK // HALO

    def body(p_ref, halo_ref, ab_ref, y_ref, cw_ref, alog_ref, dtb_ref, onw_ref, st_all_ref, dinv_ref, do2_ref,
             dp_ref, dab_ref, dcw_ref, dalog_ref, ddtb_ref, donw_ref,
             dst_sc, dhalo_sc, c_sc, beta_sc, g_sc, dc_sc, dbeta_sc, dg_sc, donw_sc):
        step = pl.program_id(1)
        first = (pl.program_id(0) == 0) & (step == 0)

        @pl.when(step == 0)
        def _():
            dst_sc[...] = jnp.zeros_like(dst_sc)
            dhalo_sc[...] = jnp.zeros_like(dhalo_sc)

        donw_sc[...] = jnp.zeros_like(donw_sc)

        def act(at):
            c_sc[:, at()] = jax.nn.silu(y_ref[:, at()])

        _lane_blocks(3 * d, act)
        (beta_sc[...], g_sc[...]), gates_vjp = jax.vjp(
            _gd_gates, ab_ref[:, 0:N_HEADS], ab_ref[:, N_HEADS:2 * N_HEADS], alog_ref[...], dtb_ref[...])

        def group(hs):
            sts = pl.ds(hs[0], len(hs))
            dinv = dinv_ref[0, sts]
            _, vjp = jax.vjp(
                lambda *a: _GD_HEADS_AGAIN(*a, dinv), st_all_ref[0, sts], _stack_heads(c_sc, hs),
                _stack_heads(c_sc, hs, N_HEADS), _stack_heads(c_sc, hs, 2 * N_HEADS), _stack_heads(beta_sc, hs),
                _stack_heads(g_sc, hs), _stack_heads(g_sc, hs, N_HEADS),
                _stack_heads(p_ref, hs, 3 * N_HEADS).astype(F32), onw_ref[...])
            dst, dq, dk, dv, dbeta, dg, ddiff, dgate, donw = vjp(
                (_stack_heads(do2_ref, hs).astype(F32), dst_sc[sts]))
            _unstack_heads(dg_sc, hs, ddiff, N_HEADS)
            dst_sc[sts] = dst
            _unstack_heads(dc_sc, hs, dq)
            _unstack_heads(dc_sc, hs, dk, N_HEADS)
            _unstack_heads(dc_sc, hs, dv, 2 * N_HEADS)
            _unstack_heads(dbeta_sc, hs, dbeta)
            _unstack_heads(dg_sc, hs, dg)
            _unstack_heads(dp_ref, hs, dgate, 3 * N_HEADS)
            donw_sc[...] += donw

        _head_groups(group)
        def conv_bwd(at):
            sl = at()
            dxp, dcw = _gd_conv_bwd(_gd_xp(halo_ref, p_ref, sl, step == nc - 1), cw_ref[:, sl], y_ref[:, sl],
                                    dc_sc[:, sl])
            dqkv = jnp.concatenate([dxp[HALO:CHUNK], dxp[CHUNK:HALO + CHUNK] + dhalo_sc[:, sl]], axis=0)
            dp_ref[:, sl] = dqkv.astype(dp_ref.dtype)
            dhalo_sc[:, sl] = dxp[0:HALO]

            @pl.when(first)
            def _():
                dcw_ref[:, sl] = dcw

            @pl.when(jnp.logical_not(first))
            def _():
                dcw_ref[:, sl] += dcw

        _lane_blocks(3 * d, conv_bwd)
        da, db, dalog, ddtb = gates_vjp((dbeta_sc[...], dg_sc[...]))
        dab_ref[...] = jnp.concatenate(
            [da, db, jnp.zeros((CHUNK, AB_PAD - 2 * N_HEADS), F32)], axis=1).astype(dab_ref.dtype)

        @pl.when(first)
        def _():
            dalog_ref[...] = dalog
            ddtb_ref[...] = ddtb
            donw_ref[...] = donw_sc[...]

        @pl.when(jnp.logical_not(first))
        def _():
            dalog_ref[...] += dalog
            ddtb_ref[...] += ddtb
            donw_ref[...] += donw_sc[...]

    rev = lambda b, c: b * nc + nc - 1 - c
    idx = lambda b, c: (rev(b, c), 0)
    const = lambda b, c: (0, 0)
    small = [pl.BlockSpec((CONV_K, 3 * d), const), pl.BlockSpec((1, N_HEADS), const),
             pl.BlockSpec((1, N_HEADS), const), pl.BlockSpec((1, HEAD_DIM), const)]
    return pl.pallas_call(
        body, name=name, grid=(seqs, nc),
        in_specs=[pl.BlockSpec((CHUNK, 4 * d), idx),
                  pl.BlockSpec((HALO, 3 * d), lambda b, c: (jnp.maximum(rev(b, c) * per_halo - 1, 0), 0)),
                  pl.BlockSpec((CHUNK, AB_PAD), idx), pl.BlockSpec((CHUNK, 3 * d), idx)] + small + [
                  pl.BlockSpec((1, N_HEADS, HEAD_DIM, HEAD_DIM), lambda b, c: (rev(b, c), 0, 0, 0)),
                  pl.BlockSpec((1, N_HEADS, CHUNK, CHUNK), lambda b, c: (rev(b, c), 0, 0, 0)),
                  pl.BlockSpec((CHUNK, d), idx)],
        out_specs=[pl.BlockSpec((CHUNK, 4 * d), idx), pl.BlockSpec((CHUNK, AB_PAD), idx)] + small,
        out_shape=[jax.ShapeDtypeStruct((n, 4 * d), BF16), jax.ShapeDtypeStruct((n, AB_PAD), BF16),
                   jax.ShapeDtypeStruct((CONV_K, 3 * d), F32), jax.ShapeDtypeStruct((1, N_HEADS), F32),
                   jax.ShapeDtypeStruct((1, N_HEADS), F32), jax.ShapeDtypeStruct((1, HEAD_DIM), F32)],
        scratch_shapes=[pltpu.VMEM((N_HEADS, HEAD_DIM, HEAD_DIM), F32), pltpu.VMEM((HALO, 3 * d), F32),
                        pltpu.VMEM((CHUNK, 3 * d), F32), pltpu.VMEM((CHUNK, d), F32), pltpu.VMEM((CHUNK, 2 * d), F32),
                        pltpu.VMEM((CHUNK, 3 * d), F32), pltpu.VMEM((CHUNK, d), F32), pltpu.VMEM((CHUNK, 2 * d), F32),
                        pltpu.VMEM((1, HEAD_DIM), F32)],
        compiler_params=_params("arbitrary", "arbitrary"),
    )(projm, projm, projab, conv_y, cw, alog, dtb, onw, st_all, dinv_all, do2)


def _adam_update(w, g, m, v):
    b1c = 1.0 - ADAM_B1 ** ADAM_STEP
    b2c = 1.0 - ADAM_B2 ** ADAM_STEP
    m_new = ADAM_B1 * m + (1.0 - ADAM_B1) * g
    v_new = ADAM_B2 * v + (1.0 - ADAM_B2) * (g * g)
    delta = -ADAM_LR * ((m_new / b1c) / (jnp.sqrt(v_new / b2c) + ADAM_EPS) + ADAM_WD * w)
    return delta, m_new, v_new


def _adamw(w, g, m, v, name, tr=256):
    r, c = w.shape
    tr = _tile(r, tr)

    def body(w_ref, g_ref, m_ref, v_ref, d_ref, mo_ref, vo_ref):
        d_ref[...], mo_ref[...], vo_ref[...] = _adam_update(w_ref[...], g_ref[...], m_ref[...], v_ref[...])

    blk = pl.BlockSpec((tr, c), lambda i: (i, 0))
    return pl.pallas_call(
        body, name=name, grid=(r // tr,),
        in_specs=[blk] * 4, out_specs=[blk] * 3,
        out_shape=[jax.ShapeDtypeStruct((r, c), F32)] * 3,
        compiler_params=_params("arbitrary"),
    )(w, g, m, v)


def _adamw_slots(w, slot_bufs, m, v, name, tr=256):
    nl, r, c = w.shape
    tr = _tile(r, tr)

    def body(*refs):
        w_ref = refs[0]
        g_refs = refs[1:1 + nl]
        m_ref, v_ref, go_ref, d_ref, mo_ref, vo_ref = refs[1 + nl:]
        for k in range(nl):
            @pl.when(pl.program_id(0) == k)
            def _(k=k):
                g = g_refs[k][0].astype(F32)
                for s in range(1, N_DEV):
                    g = g + g_refs[k][s].astype(F32)
                go_ref[0] = g

        d_ref[0], mo_ref[0], vo_ref[0] = _adam_update(w_ref[0], go_ref[0], m_ref[0], v_ref[0])

    blk = pl.BlockSpec((1, tr, c), lambda l, i: (l, i, 0))
    g_specs = [pl.BlockSpec((N_DEV, tr, c), lambda l, i, k=k: (0, jnp.where(l == k, i, 0), 0)) for k in range(nl)]
    return pl.pallas_call(
        body, name=name, grid=(nl, r // tr),
        in_specs=[blk] + g_specs + [blk, blk], out_specs=[blk] * 4,
        out_shape=[jax.ShapeDtypeStruct((nl, r, c), F32)] * 4,
        compiler_params=_params("arbitrary", "arbitrary"),
    )(w, *slot_bufs, m, v)


def _mesh_pos():
    return lax.axis_index("x"), lax.axis_index("y"), lax.axis_index("c")


def _flip(pos, p):
    x, y, c = pos
    return ((1 - x) if p & 4 else x, (1 - y) if p & 2 else y, (1 - c) if p & 1 else c)


def _lin(pos):
    return 4 * pos[0] + 2 * pos[1] + pos[2]


_HBM = pl.BlockSpec(memory_space=pltpu.HBM)
_SEM = pl.BlockSpec(memory_space=pltpu.SEMAPHORE)
_DATAFLOW = pltpu.SideEffectType.DATAFLOW_SIDE_EFFECTING


class _Item:
    def __init__(self, src, land_shape, src_pick, dst_pick):
        self.src, self.land_shape, self.src_pick, self.dst_pick = src, land_shape, src_pick, dst_pick


def _remote_copies(items, src, land, send_sem, recv_sem, me, arriving):
    me_i = _lin(me)
    out = []
    for it, s_ref, l_ref in zip(items, src, land):
        for p in range(1, N_DEV):
            peer = _flip(me, p)
            out.append(pltpu.make_async_remote_copy(
                src_ref=it.src_pick(s_ref, _lin(peer)),
                dst_ref=it.dst_pick(l_ref, _lin(peer) if arriving else me_i),
                send_sem=send_sem, recv_sem=recv_sem, device_id=peer, device_id_type=pl.DeviceIdType.MESH))
    return out


def _own_copies(items, src, land, sem, me):
    me_i = _lin(me)
    return [pltpu.make_async_copy(it.src_pick(s_ref, me_i), it.dst_pick(l_ref, me_i), sem)
            for it, s_ref, l_ref in zip(items, src, land)]


def _exchange_start(groups, name):
    items = [it for g in groups for it in g]
    n, ng = len(items), len(groups)
    first = [sum(len(g) for g in groups[:gi]) for gi in range(ng)]

    def body(*refs):
        src, land = refs[0:n], refs[n:2 * n]
        send_sems, recv_sems = refs[2 * n:2 * n + ng], refs[2 * n + ng:2 * n + 2 * ng]
        token = refs[4 * n + 2 * ng]
        me = _mesh_pos()
        for gi, g in enumerate(groups):
            sl = slice(first[gi], first[gi] + len(g))
            for cp in _remote_copies(g, src[sl], land[sl], send_sems[gi], recv_sems[gi], me, arriving=False):
                cp.start()
            for cp in _own_copies(g, src[sl], land[sl], recv_sems[gi], me):
                cp.start()
        token[...] = jnp.zeros_like(token)

    srcs = [pltpu.with_memory_space_constraint(it.src, pltpu.HBM) for it in items]
    lands = [pltpu.with_memory_space_constraint(lax.empty(it.land_shape, it.src.dtype), pltpu.HBM) for it in items]
    res = pl.pallas_call(
        body, name=name,
        out_shape=([pltpu.SemaphoreType.DMA(())] * (2 * ng)
                   + [pltpu.HBM(it.src.shape, it.src.dtype) for it in items]
                   + [pltpu.HBM(it.land_shape, it.src.dtype) for it in items]
                   + [jax.ShapeDtypeStruct((8, 128), F32)]),
        in_specs=[_HBM] * (2 * n),
        out_specs=[_SEM] * (2 * ng) + [_HBM] * (2 * n) + [pl.BlockSpec(memory_space=pltpu.VMEM)],
        input_output_aliases={i: 2 * ng + i for i in range(2 * n)},
        compiler_params=pltpu.CompilerParams(has_side_effects=_DATAFLOW),
    )(*srcs, *lands)
    send_sems, recv_sems = res[0:ng], res[ng:2 * ng]
    src_thru, land_thru = res[2 * ng:2 * ng + n], res[2 * ng + n:2 * ng + 2 * n]
    handles = []
    for gi, g in enumerate(groups):
        sl = slice(first[gi], first[gi] + len(g))
        handles.append((g, src_thru[sl], land_thru[sl], send_sems[gi], recv_sems[gi]))
    return handles, res[-1]


def _exchange_wait(handle, after, name):
    items, src_thru, land_thru, send_sem, recv_sem = handle
    k = len(items)

    def body(*refs):
        src, land = refs[0:k], refs[k:2 * k]
        send_ref, recv_ref = refs[2 * k], refs[2 * k + 1]
        for cp in _remote_copies(items, src, land, send_ref, recv_ref, _mesh_pos(), arriving=True):
            cp.wait_send()
            cp.wait_recv()
        for cp in _own_copies(items, src, land, recv_ref, _mesh_pos()):
            cp.wait()

    res = pl.pallas_call(
        body, name=name,
        out_shape=([pltpu.HBM(s.shape, s.dtype) for s in src_thru] + [pltpu.HBM(l.shape, l.dtype) for l in land_thru]),
        in_specs=[_HBM] * (2 * k) + [_SEM, _SEM, pl.BlockSpec(memory_space=pl.ANY)],
        out_specs=[_HBM] * (2 * k),
        input_output_aliases={i: i for i in range(2 * k)},
        compiler_params=pltpu.CompilerParams(has_side_effects=_DATAFLOW),
    )(*src_thru, *land_thru, send_sem, recv_sem, after)
    return res[k:2 * k]


def _whole(ref, i):
    return ref


def _slot(ref, i):
    return ref.at[i]


def _rows_of(r):
    return lambda ref, i: ref.at[pl.ds(pl.multiple_of(i * r, r), r), :]


def _cols_of(c):
    return lambda ref, i: ref.at[:, pl.ds(pl.multiple_of(i * c, c), c)]


def _all_reduce_small(buf, after, name):
    r, c = buf.shape

    def body(src_ref, after_ref, out_ref, all_ref, send_sems, recv_sems):
        me = _mesh_pos()
        me_i = _lin(me)
        all_ref[me_i] = src_ref[...]
        for p in range(1, N_DEV):
            peer = _flip(me, p)
            pltpu.make_async_remote_copy(
                src_ref=src_ref, dst_ref=all_ref.at[me_i], send_sem=send_sems.at[p - 1], recv_sem=recv_sems.at[p - 1],
                device_id=peer, device_id_type=pl.DeviceIdType.MESH).start()
        for p in range(1, N_DEV):
            peer = _flip(me, p)
            cp = pltpu.make_async_remote_copy(
                src_ref=src_ref, dst_ref=all_ref.at[_lin(peer)], send_sem=send_sems.at[p - 1],
                recv_sem=recv_sems.at[p - 1], device_id=peer, device_id_type=pl.DeviceIdType.MESH)
            cp.wait_recv()
            cp.wait_send()
        acc = all_ref[0]
        for s in range(1, N_DEV):
            acc = acc + all_ref[s]
        out_ref[...] = acc

    vm = pl.BlockSpec(memory_space=pltpu.VMEM)
    return pl.pallas_call(
        body, name=name, in_specs=[vm, pl.BlockSpec(memory_space=pl.ANY)], out_specs=vm,
        out_shape=jax.ShapeDtypeStruct((r, c), F32),
        scratch_shapes=[pltpu.VMEM((N_DEV, r, c), F32), pltpu.SemaphoreType.DMA((N_DEV - 1,)),
                        pltpu.SemaphoreType.DMA((N_DEV - 1,))],
        compiler_params=pltpu.CompilerParams(has_side_effects=True),
    )(buf, after)


def _unshard_cols(g):
    s, l, r, c = g.shape
    return jnp.transpose(g, (1, 2, 0, 3)).reshape(l, r, s * c)


def kernel(x, gdn_w_in, gdn_conv, gdn_a_log, gdn_dt_bias, gdn_onorm, gdn_w_out, hgrn_w_in, hgrn_lb_logits, hgrn_gnorm, hgrn_w_out, norm_mix, norm_mlp, mlp_w_up, mlp_w_down, norm_final, loss_target, m_gdn_w_in, m_gdn_conv, m_gdn_a_log, m_gdn_dt_bias, m_gdn_onorm, m_gdn_w_out, m_hgrn_w_in, m_hgrn_lb_logits, m_hgrn_gnorm, m_hgrn_w_out, m_norm_mix, m_norm_mlp, m_mlp_w_up, m_mlp_w_down, m_norm_final, v_gdn_w_in, v_gdn_conv, v_gdn_a_log, v_gdn_dt_bias, v_gdn_onorm, v_gdn_w_out, v_hgrn_w_in, v_hgrn_lb_logits, v_hgrn_gnorm, v_hgrn_w_out, v_norm_mix, v_norm_mlp, v_mlp_w_up, v_mlp_w_down, v_norm_final):
    seqs, seq_len, d = x.shape
    n = seqs * seq_len
    me_i = _lin(_mesh_pos())
    x2 = x.reshape(n, d)
    target = loss_target.reshape(n, d)
    n_gdn, n_hgrn = gdn_w_in.shape[0], hgrn_w_in.shape[0]

    r_out, r_down = gdn_w_out.shape[1], mlp_w_down.shape[1]
    c_gin, c_hin, c_up = gdn_w_in.shape[2], hgrn_w_in.shape[2], mlp_w_up.shape[2]

    def gathered(w, pick, land_shape):
        return _Item(w.astype(BF16), land_shape, _whole, pick)

    groups = [[_Item(gdn_conv, (N_DEV,) + gdn_conv.shape, _whole, _slot),
               _Item(hgrn_gnorm, (N_DEV,) + hgrn_gnorm.shape, _whole, _slot)]]
    for i in range(DEPTH):
        j = i // 2
        if i % 2 == 0:
            groups += [[gathered(gdn_w_in[j], _slot, (N_DEV, d, c_gin))],
                       [gathered(gdn_w_out[j], _rows_of(r_out), (N_DEV * r_out, d))]]
        else:
            groups += [[gathered(hgrn_w_in[j], _cols_of(c_hin), (d, N_DEV * c_hin))],
                       [gathered(hgrn_w_out[j], _rows_of(r_out), (N_DEV * r_out, d))]]
        groups += [[gathered(mlp_w_up[i], _cols_of(c_up), (d, N_DEV * c_up))],
                   [gathered(mlp_w_down[i], _rows_of(r_down), (N_DEV * r_down, d))]]
    gather_handles, token = _exchange_start(groups, "gather_start")
    lbs = _lb_fwd(hgrn_lb_logits + token[0:1, 0:1], "lb_fwd")

    def arrived(k, after, name):
        return _exchange_wait(gather_handles[k], after, "gather_wait_" + name)

    saved = []
    w_in, w_ab, w_out, w_up, w_down = ([None] * DEPTH for _ in range(5))
    h = x2
    for i in range(DEPTH):
        j = i // 2
        if i == 0:
            g_conv, g_gnorm = arrived(0, h, "small")
            conv_full = _unshard_cols(g_conv)
            gnorm_full = jnp.transpose(g_gnorm, (1, 0, 2)).reshape(n_hgrn, d)
        if i == 0:
            y = _rms_fwd(h, norm_mix[0:1], "rms_mix_0")
        (w_in[i],) = arrived(1 + 4 * i, y, f"in_{i}")
        if i % 2 == 0:
            w_gin = jnp.transpose(w_in[i], (1, 0, 2)).reshape(d, N_DEV * c_gin)
            w_in[i] = w_gin[:, :GDN_MAIN]
            w_ab[i] = jnp.pad(w_gin[:, GDN_MAIN:], ((0, 0), (0, AB_PAD - 2 * N_HEADS)))
            projm = _mm(y, w_in[i], "nn", [BF16], f"gdn_proj_{i}")
            projab = _mm(y, w_ab[i], "nn", [F32], f"gdn_proj_ab_{i}")
            o2, st_all, conv_y, dinv_all = _gdn_fwd(projm, projab, conv_full[j], gdn_a_log[j:j + 1],
                                                    gdn_dt_bias[j:j + 1], gdn_onorm[j:j + 1], seqs, f"gdn_fwd_{i}")
            mix = (projm, projab, conv_y, st_all, dinv_all)
        else:
            proj = _mm(y, w_in[i], "nn", [BF16], f"hgrn_proj_{i}")
            o2, o_raw, st_all = _hgrn_fwd(proj, lbs[i:i + 1], gnorm_full[j:j + 1], seqs, f"hgrn_fwd_{i}")
            mix = (proj, o_raw, st_all)
        (w_out[i],) = arrived(2 + 4 * i, o2, f"out_{i}")
        h1, y2 = _mm(o2, w_out[i], "nn", [F32, BF16], f"mix_out_{i}", epilogue=_ep_residual_norm, extras=(h,),
                     vectors=(norm_mlp[i:i + 1],))
        (w_up[i],) = arrived(3 + 4 * i, y2, f"up_{i}")
        u, act = _mm(y2, w_up[i], "nn", [BF16, BF16], f"mlp_up_{i}",
                     epilogue=lambda acc: (acc, jnp.square(jnp.maximum(acc, 0.0))))
        (w_down[i],) = arrived(4 + 4 * i, act, f"down_{i}")
        saved.append((h, y, mix, o2, h1, y2, u, act))
        if i + 1 < DEPTH:
            h, y = _mm(act, w_down[i], "nn", [F32, BF16], f"mlp_down_{i}", epilogue=_ep_residual_norm, extras=(h1,),
                       vectors=(norm_mix[i + 1:i + 2],))
        else:
            h = _mm(act, w_down[i], "nn", [F32], f"mlp_down_{i}", epilogue=lambda acc, res: (res + acc,),
                    extras=(h1,))

    dh, dh_b, d_nf, sq = _loss_head(h, norm_final.reshape(1, d), target, "loss_head")

    d_nmix, d_nmlp = [None] * DEPTH, [None] * DEPTH
    d_conv, d_alog, d_dtb, d_onorm = [None] * n_gdn, [None] * n_gdn, [None] * n_gdn, [None] * n_gdn
    d_lb = [jnp.zeros((1, d), F32)] * DEPTH
    d_gnorm = [None] * n_hgrn
    mlp_handles, mix_handles = [None] * DEPTH, [None] * DEPTH
    token = None
    for i in reversed(range(DEPTH)):
        j = i // 2
        h_in, y, mix, o2, h1, y2, u, act = saved[i]
        g_down = _mm(act, dh_b, "tn", [BF16], f"g_down_{i}", after=token)
        du = _mm(dh_b, w_down[i], "nt", [BF16], f"d_u_{i}",
                 epilogue=lambda acc, uu: (acc * (2.0 * jnp.maximum(uu.astype(F32), 0.0)),), extras=(u,))
        g_up = _mm(y2, du, "tn", [BF16], f"g_up_{i}")
        mlp_handles[i], token = _exchange_start(
            [[_Item(g_down, (N_DEV, r_down, d), _rows_of(r_down), _slot)],
             [_Item(g_up, (N_DEV, d, c_up), _cols_of(c_up), _slot)]], f"scatter_start_mlp_{i}")
        dh1, dh1_b, d_nmlp[i] = _mm(du, w_up[i], "nt", [F32, BF16], f"d_y2_{i}", epilogue=_ep_norm_bwd,
                                     extras=(h1, dh), vectors=(norm_mlp[i:i + 1],), n_sums=1, after=token)
        g_out = _mm(o2, dh1_b, "tn", [BF16], f"g_out_{i}")
        do2 = _mm(dh1_b, w_out[i], "nt", [BF16], f"d_o2_{i}")
        if i % 2 == 0:
            projm, projab, conv_y, st_all, dinv_all = mix
            dpm, dpab, d_conv[j], d_alog[j], d_dtb[j], d_onorm[j] = _gdn_bwd(
                projm, projab, conv_y, conv_full[j], gdn_a_log[j:j + 1], gdn_dt_bias[j:j + 1], gdn_onorm[j:j + 1],
                st_all, dinv_all, do2, seqs, f"gdn_bwd_{i}")
            g_main = _mm(y, dpm, "tn", [BF16], f"g_in_{i}")
            g_ab = _mm(y, dpab, "tn", [BF16], f"g_in_ab_{i}")
            g_in = jnp.concatenate([g_main, g_ab[:, :2 * N_HEADS]], axis=1)
            g_in = jnp.transpose(g_in.reshape(d, N_DEV, c_gin), (1, 0, 2))
            in_item = _Item(g_in, (N_DEV, d, c_gin), _slot, _slot)
            dy_ab = _mm(dpab, w_ab[i], "nt", [F32], f"d_y_ab_{i}")
            dp, dy_extras = dpm, (dy_ab, h_in, dh1)
            dy_epilogue = lambda acc, e, xx, dres, w: _ep_norm_bwd(acc + e, xx, dres, w)
        else:
            proj, o_raw, st_all = mix
            dp, d_lb[i], d_gnorm[j] = _hgrn_bwd(proj, lbs[i:i + 1], gnorm_full[j:j + 1], st_all, o_raw, do2,
                                               seqs, f"hgrn_bwd_{i}")
            g_in = _mm(y, dp, "tn", [BF16], f"g_in_{i}")
            in_item = _Item(g_in, (N_DEV, d, c_hin), _cols_of(c_hin), _slot)
            dy_extras, dy_epilogue = (h_in, dh1), _ep_norm_bwd
        mix_handles[i], token = _exchange_start(
            [[_Item(g_out, (N_DEV, r_out, d), _rows_of(r_out), _slot)], [in_item]], f"scatter_start_mix_{i}")
        dh, dh_b, d_nmix[i] = _mm(dp, w_in[i], "nt", [F32, BF16], f"d_y_{i}", epilogue=dy_epilogue, extras=dy_extras,
                                  vectors=(norm_mix[i:i + 1],), n_sums=1, after=token)
        token = None
    grad_x = dh.reshape(x.shape)

    def landed(handles, k, layers, after, name):
        return [_exchange_wait(handles[i][k], after, f"scatter_wait_{name}_{i}")[0] for i in layers]

    every, even, odd = range(DEPTH), range(0, DEPTH, 2), range(1, DEPTH, 2)
    upd = {}
    upd["mlp_w_down"] = _adamw_slots(mlp_w_down, landed(mlp_handles, 0, every, dh, "down"), m_mlp_w_down,
                                     v_mlp_w_down, "adamw_mlp_w_down")
    upd["mlp_w_up"] = _adamw_slots(mlp_w_up, landed(mlp_handles, 1, every, upd["mlp_w_down"][1], "up"), m_mlp_w_up,
                                   v_mlp_w_up, "adamw_mlp_w_up")
    upd["hgrn_w_out"] = _adamw_slots(hgrn_w_out, landed(mix_handles, 0, odd, upd["mlp_w_up"][1], "out"),
                                     m_hgrn_w_out, v_hgrn_w_out, "adamw_hgrn_w_out")
    upd["hgrn_w_in"] = _adamw_slots(hgrn_w_in, landed(mix_handles, 1, odd, upd["hgrn_w_out"][1], "in"), m_hgrn_w_in,
                                    v_hgrn_w_in, "adamw_hgrn_w_in")

    dlb_rows = jnp.concatenate(d_lb, axis=0)
    tail = jnp.concatenate(
        [jnp.concatenate(d_onorm, axis=1), jnp.concatenate(d_alog, axis=1), jnp.concatenate(d_dtb, axis=1)], axis=1)
    tail = jnp.pad(tail, ((0, 0), (0, d - tail.shape[1])))
    conv_rows = jnp.stack(d_conv).reshape(-1, d)
    packed = jnp.concatenate(
        [jnp.concatenate(d_nmix, axis=0), jnp.concatenate(d_nmlp, axis=0), d_nf, sq, dlb_rows,
         jnp.concatenate(d_gnorm, axis=0), tail, conv_rows], axis=0)
    pad_rows = (-packed.shape[0]) % 8
    packed = jnp.pad(packed, ((0, pad_rows), (0, 0)))
    tot = _all_reduce_small(packed, upd["hgrn_w_in"][1], "reduce_small")

    upd["gdn_w_out"] = _adamw_slots(gdn_w_out, landed(mix_handles, 0, even, tot, "out"),
                                    m_gdn_w_out, v_gdn_w_out, "adamw_gdn_w_out")
    upd["gdn_w_in"] = _adamw_slots(gdn_w_in, landed(mix_handles, 1, even, upd["gdn_w_out"][1], "in"), m_gdn_w_in,
                                   v_gdn_w_in, "adamw_gdn_w_in")

    def update(name, w, g, m, v):
        shape = w.shape
        c = shape[-1]
        res = _adamw(w.reshape(-1, c), g.reshape(-1, c), m.reshape(-1, c), v.reshape(-1, c), "adamw_" + name)
        return [g.reshape(shape)] + [o.reshape(shape) for o in res]

    r0 = 0
    g_nmix = tot[r0:r0 + DEPTH]; r0 += DEPTH
    g_nmlp = tot[r0:r0 + DEPTH]; r0 += DEPTH
    g_nf = tot[r0]; r0 += 1
    loss = tot[r0, 0]; r0 += 1
    g_lb = _lb_bwd(hgrn_lb_logits, tot[r0:r0 + DEPTH], "lb_bwd"); r0 += DEPTH
    g_gnorm_full = tot[r0:r0 + n_hgrn]; r0 += n_hgrn
    t_row = tot[r0]; r0 += 1
    g_conv_full = tot[r0:r0 + n_gdn * CONV_K * 3].reshape(n_gdn, CONV_K, 3 * d)
    g_onorm = t_row[0:n_gdn * HEAD_DIM].reshape(n_gdn, HEAD_DIM)
    o1 = n_gdn * HEAD_DIM
    g_alog = t_row[o1:o1 + n_gdn * N_HEADS].reshape(n_gdn, N_HEADS)
    g_dtb = t_row[o1 + n_gdn * N_HEADS:o1 + 2 * n_gdn * N_HEADS].reshape(n_gdn, N_HEADS)
    c_gn, c_cv = hgrn_gnorm.shape[1], gdn_conv.shape[2]
    g_gnorm = lax.dynamic_slice_in_dim(g_gnorm_full, me_i * c_gn, c_gn, axis=1)
    g_conv = lax.dynamic_slice_in_dim(g_conv_full, me_i * c_cv, c_cv, axis=2)

    upd["gdn_conv"] = update("gdn_conv", gdn_conv, g_conv, m_gdn_conv, v_gdn_conv)
    upd["gdn_a_log"] = update("gdn_a_log", gdn_a_log, g_alog, m_gdn_a_log, v_gdn_a_log)
    upd["gdn_dt_bias"] = update("gdn_dt_bias", gdn_dt_bias, g_dtb, m_gdn_dt_bias, v_gdn_dt_bias)
    upd["gdn_onorm"] = update("gdn_onorm", gdn_onorm, g_onorm, m_gdn_onorm, v_gdn_onorm)
    upd["hgrn_lb_logits"] = update("hgrn_lb_logits", hgrn_lb_logits, g_lb, m_hgrn_lb_logits, v_hgrn_lb_logits)
    upd["hgrn_gnorm"] = update("hgrn_gnorm", hgrn_gnorm, g_gnorm, m_hgrn_gnorm, v_hgrn_gnorm)
    upd["norm_mix"] = update("norm_mix", norm_mix, g_nmix, m_norm_mix, v_norm_mix)
    upd["norm_mlp"] = update("norm_mlp", norm_mlp, g_nmlp, m_norm_mlp, v_norm_mlp)
    upd["norm_final"] = update("norm_final", norm_final, g_nf, m_norm_final, v_norm_final)

    order = ["gdn_w_in", "gdn_conv", "gdn_a_log", "gdn_dt_bias", "gdn_onorm", "gdn_w_out", "hgrn_w_in",
             "hgrn_lb_logits", "hgrn_gnorm", "hgrn_w_out", "norm_mix", "norm_mlp", "mlp_w_up", "mlp_w_down",
             "norm_final"]
    outs = [loss, grad_x]
    for k in range(4):
        outs += [upd[name][k] for name in order]
    return tuple(outs)
```

```python
import functools

import numpy as np
import jax
import jax.numpy as jnp
from jax import lax
from jax.experimental import pallas as pl
from jax.experimental.pallas import tpu as pltpu

F32 = jnp.float32
BF16 = jnp.bfloat16

D_MODEL = 1024
N_HEADS = 8
HEAD_DIM = 128
CHUNK = 64
SUB = 16
N_SUB = CHUNK // SUB
CONV_K = 4
HALO = 16
EPS = 1e-6
DEPTH = 4
N_DEV = 8
GDN_MAIN = 4 * D_MODEL
GDN_IN = GDN_MAIN + 2 * N_HEADS
AB_PAD = 128
LANE_BLOCK = 256
ROW_BLOCK = 16
BLOCK_UNROLL = 4

ADAM_LR = 0.001
ADAM_B1 = 0.9
ADAM_B2 = 0.999
ADAM_EPS = 1e-08
ADAM_WD = 0.01
ADAM_STEP = 10

VMEM_LIMIT = 56 * 1024 * 1024
MM_TILE = 1024
MM_VMEM_BUDGET = 40 * 1024 * 1024

_DIMS = {
    "nn": (((1,), (0,)), ((), ())),
    "nt": (((1,), (1,)), ((), ())),
    "tn": (((0,), (0,)), ((), ())),
}


def _parts(x, n):
    if n == 1 and x.dtype == BF16:
        return [x]
    out = []
    r = x.astype(F32)
    for i in range(n):
        p = r.astype(BF16)
        out.append(p)
        if i + 1 < n:
            r = r - p.astype(F32)
    return out


def _dot_raw(a, b, mode, na, nb):
    ap, bp = _parts(a, na), _parts(b, nb)
    nmax = max(na, nb)
    pairs = [(i, j) for i in range(na) for j in range(nb) if i + j < nmax]
    ka = 0 if mode == "tn" else 1
    kb = 1 if mode == "nt" else 0
    xa = ap[0] if len(pairs) == 1 else jnp.concatenate([ap[i] for i, _ in pairs], axis=ka)
    xb = bp[0] if len(pairs) == 1 else jnp.concatenate([bp[j] for _, j in pairs], axis=kb)
    return lax.dot_general(xa, xb, _DIMS[mode], preferred_element_type=F32)


@functools.partial(jax.custom_vjp, nondiff_argnums=(2, 3, 4))
def _dot(a, b, mode, na, nb):
    return _dot_raw(a, b, mode, na, nb)


def _dot_fwd(a, b, mode, na, nb):
    return _dot_raw(a, b, mode, na, nb), (a, b)


def _dot_bwd(mode, na, nb, res, ct):
    a, b = res
    if mode == "nn":
        da = _dot_raw(ct, b, "nt", 1, 1)
        db = _dot_raw(a, ct, "tn", 1, 1)
    elif mode == "nt":
        da = _dot_raw(ct, b, "nn", 1, 1)
        db = _dot_raw(ct, a, "tn", 1, 1)
    else:
        da = _dot_raw(b, ct, "nt", 1, 1)
        db = _dot_raw(a, ct, "nn", 1, 1)
    return da.astype(a.dtype), db.astype(b.dtype)


_dot.defvjp(_dot_fwd, _dot_bwd)


N_EXACT = 3


@jax.custom_vjp
def _dot01(x, m_wide, m):
    return lax.dot_general(m_wide, jnp.concatenate(_parts(x, N_EXACT), axis=0), _DIMS["nn"], preferred_element_type=F32)


def _dot01_fwd(x, m_wide, m):
    return _dot01(x, m_wide, m), (m_wide, m)


def _dot01_bwd(res, ct):
    m_wide, m = res
    dx = lax.dot_general(m, ct.astype(BF16), _DIMS["tn"], preferred_element_type=F32)
    return dx, jnp.zeros_like(m_wide), jnp.zeros_like(m)


_dot01.defvjp(_dot01_fwd, _dot01_bwd)


def _thrice(m):
    return jnp.concatenate([m] * N_EXACT, axis=1).astype(BF16), m.astype(BF16)


def _iota2(shape, dim):
    return lax.broadcasted_iota(jnp.int32, shape, dim)


def _tril_f32(n):
    return (_iota2((n, n), 0) >= _iota2((n, n), 1)).astype(F32)


def _cumsum_rows(g):
    return _dot(_tril_f32(g.shape[0]), g, "nn", 1, 3)


def _below_block(n, b):
    ri, ci = _iota2((n, n), 0) // b, _iota2((n, n), 1) // b
    return (ri == ci + 1) & (ri % 2 == 1)


def _half_inverses(L):
    n = L.shape[0]
    eye = (_iota2((n, n), 0) == _iota2((n, n), 1)).astype(F32)
    d = eye - jnp.where(_below_block(n, 1), L, 0.0)
    b = 2
    while 2 * b < n:
        e = jnp.where(_below_block(n, b), L, 0.0)
        d = d - _dot_raw(d, _dot_raw(e, d, "nn", 2, 2), "nn", 2, 2)
        b *= 2
    return d, jnp.where(_below_block(n, b), L, 0.0)


def _solve_with(d, e, rhs):
    y = _dot_raw(d, rhs, "nn", 2, 2)
    return y - _dot_raw(d, _dot_raw(e, y, "nn", 2, 2), "nn", 2, 2)


@jax.custom_vjp
def _solve_unit_lower(L, rhs, d):
    n = L.shape[0]
    return _solve_with(d, jnp.where(_below_block(n, n // 2), L, 0.0), rhs)


def _solve_fwd(L, rhs, d):
    n = L.shape[0]
    e = jnp.where(_below_block(n, n // 2), L, 0.0)
    sol = _solve_with(d, e, rhs)
    return sol, (d, e, sol)


def _solve_bwd(res, ct):
    d, e, sol = res
    y = _dot_raw(d, ct - _dot_raw(e, _dot_raw(d, ct, "tn", 2, 2), "tn", 2, 2), "tn", 2, 2)
    return -_dot_raw(y, sol, "nt", 2, 2), y, jnp.zeros_like(d)


_solve_unit_lower.defvjp(_solve_fwd, _solve_bwd)


def _softplus(x):
    return jnp.maximum(x, 0.0) + jnp.log1p(jnp.exp(-jnp.abs(x)))


def _rms(x, w):
    return x * lax.rsqrt(jnp.mean(x * x, axis=-1, keepdims=True) + EPS) * w


HG_LEVELS = (32, 16, 8, 4, 2, 1)


def _hg_level_sums():
    i = np.arange(CHUNK)[:, None]
    m = np.arange(CHUNK)[None, :]
    to_row = [(m <= i) & (m // b == i // b) for b in HG_LEVELS]
    to_col = [(m > i) & (m // b == i // b) for b in HG_LEVELS if b > 1]
    return _thrice(jnp.asarray(np.concatenate(to_row + to_col + [m <= i]), F32))


def _hg_level_masks():
    i = np.arange(CHUNK)[:, None]
    j = np.arange(CHUNK)[None, :]
    return jnp.asarray(np.stack([(i // b == j // b + 1) & ((i // b) % 2 == 1) for b in HG_LEVELS]), F32)


def _hg_pre(qraw, f, lb, sums):
    g = jnp.log(lb + (1.0 - lb) * jax.nn.sigmoid(f))
    k = (1.0 - lb) * jax.nn.sigmoid(-f)
    q = jax.nn.silu(qraw) * (HEAD_DIM ** -0.5)
    return q, k, _dot01(g, *sums)


def _hg_head(st, q, k, v, e, masks):
    nl = len(HG_LEVELS)
    eye = (_iota2((CHUNK, CHUNK), 0) == _iota2((CHUNK, CHUNK), 1)).astype(F32)
    a = eye * jnp.sum(q * k, axis=-1, keepdims=True)
    for l, b in enumerate(HG_LEVELS):
        rows = q * jnp.exp(e[l * CHUNK:(l + 1) * CHUNK])
        cols = k * jnp.exp(e[(nl + l) * CHUNK:(nl + l + 1) * CHUNK]) if b > 1 else k
        a = a + masks[l] * _dot(rows, cols, "nt", 1, 1)
    gc = e[(2 * nl - 1) * CHUNK:2 * nl * CHUNK]
    o = _dot(a, v, "nn", 1, 1) + _dot(q * jnp.exp(gc), st, "nt", 1, 1)
    g_last = gc[CHUNK - 1:CHUNK]
    st_new = st * jnp.exp(g_last) + _dot(v, k * jnp.exp(g_last - gc), "tn", 1, 1)
    return o, st_new


_HG_HEADS = jax.vmap(_hg_head, in_axes=(0, 0, 0, 0, 0, None))


def _hg_post(o, gate, gw):
    return _rms(o, gw) * jax.nn.silu(gate)


def _gd_conv(xp, cw):
    off = HALO - (CONV_K - 1)
    y = cw[0:1] * xp[off:off + CHUNK]
    for kk in range(1, CONV_K):
        y = y + cw[kk:kk + 1] * xp[off + kk:off + kk + CHUNK]
    return y


def _gd_conv_bwd(xp, cw, y, dc):
    off = HALO - (CONV_K - 1)
    sig = jax.nn.sigmoid(y)
    dy = dc * (sig * (1.0 + y * (1.0 - sig)))
    dxp, dcw = None, []
    for kk in range(CONV_K):
        moved = jnp.pad(dy, ((off + kk, HALO - off - kk), (0, 0)))
        term = cw[kk:kk + 1] * moved
        dxp = term if dxp is None else dxp + term
        dcw.append(jnp.sum(xp * moved, axis=0, keepdims=True))
    return dxp, jnp.concatenate(dcw, axis=0)


def _gd_gates(a, b, alog, dtb):
    beta = jax.nn.sigmoid(b)
    g = -jnp.exp(alog) * _softplus(a + dtb)
    expand = (_iota2((N_HEADS, D_MODEL), 1) // HEAD_DIM == _iota2((N_HEADS, D_MODEL), 0)).astype(F32)
    g_x = _dot(g, expand, "nn", 3, 1)
    after = (_iota2((CHUNK, D_MODEL), 0) > _iota2((CHUNK, D_MODEL), 1) % HEAD_DIM).astype(F32)
    sums = _dot01(jnp.concatenate([g_x, g_x * after], axis=1), *_thrice(_tril_f32(CHUNK)))
    return _dot(beta, expand, "nn", 3, 1), sums


def _gd_head(st, q, k, v, beta, gc, diff, gate, onw, dinv=None):
    q = q * lax.rsqrt(jnp.sum(q * q, axis=-1, keepdims=True) + EPS) * (HEAD_DIM ** -0.5)
    k = k * lax.rsqrt(jnp.sum(k * k, axis=-1, keepdims=True) + EPS)
    ri = _iota2((CHUNK, CHUNK), 0)
    ci = _iota2((CHUNK, CHUNK), 1)
    decay = jnp.exp(jnp.where(ri >= ci, diff[:, 0:CHUNK], -jnp.inf))
    kb = k * beta
    egc = jnp.exp(gc)
    L = jnp.where(ri > ci, _dot(kb, k, "nt", 1, 1) * decay, 0.0)
    made = dinv is None
    if made:
        dinv = _half_inverses(L)[0]
    sol = _solve_unit_lower(L, jnp.concatenate([v * beta, kb * egc], axis=1), dinv)
    u = sol[:, 0:HEAD_DIM]
    w = sol[:, HEAD_DIM:2 * HEAD_DIM]
    a_qk = jnp.where(ri >= ci, _dot(q, k, "nt", 1, 1) * decay, 0.0)
    g_last = gc[CHUNK - 1:CHUNK]
    v_new = u - _dot(w, st, "nt", 1, 1)
    o = _dot(q * egc, st, "nt", 1, 1) + _dot(a_qk, v_new, "nn", 1, 1)
    st_new = st * jnp.exp(g_last) + _dot(v_new, k * jnp.exp(g_last - gc), "tn", 1, 1)
    out = (_rms(o, onw) * jax.nn.silu(gate), st_new)
    return out + (dinv,) if made else out


def _params(*sem):
    return pltpu.CompilerParams(dimension_semantics=sem, vmem_limit_bytes=VMEM_LIMIT)


def _tile(n, pref):
    t = min(n, pref)
    assert n % t == 0, (n, pref)
    return t


def _mm_tiles(m, n, k, a_size, b_size, tile_sizes):
    tm, tn, tk = _tile(m, MM_TILE), _tile(n, MM_TILE), k

    def need(tm, tn, tk):
        acc = 4 * tm * tn * (2 if tk < k else 1)
        return 2 * (tm * tk * a_size + tk * tn * b_size + tm * tn * sum(tile_sizes)) + acc

    while need(tm, tn, tk) > MM_VMEM_BUDGET:
        if tk > 2048 or (tk > 512 and tm <= 512):
            tk //= 2
        else:
            tm //= 2
    return tm, tn, tk


def _mm(a, b, mode, out_dtypes, name, epilogue=None, extras=(), vectors=(), n_sums=0, after=None):
    if mode == "nn":
        (m, k), (k2, n) = a.shape, b.shape
    elif mode == "nt":
        (m, k), (n, k2) = a.shape, b.shape
    else:
        (k, m), (k2, n) = a.shape, b.shape
    assert k == k2, (a.shape, b.shape, mode)
    tm, tn, tk = _mm_tiles(m, n, k, a.dtype.itemsize, b.dtype.itemsize,
                           [e.dtype.itemsize for e in extras] + [jnp.dtype(dt).itemsize for dt in out_dtypes])
    nk = k // tk
    assert not (vectors or n_sums) or tn == n, "whole-row epilogues need the result tile to span the rows"
    ne, no, nafter = len(extras) + len(vectors), len(out_dtypes), int(after is not None)
    if epilogue is None:
        epilogue = lambda acc: (acc,)

    def body(*refs):
        a_ref, b_ref = refs[0], refs[1]
        ex = refs[2:2 + ne]
        outs = refs[2 + ne + nafter:2 + ne + nafter + no]
        sums = refs[2 + ne + nafter + no:2 + ne + nafter + no + n_sums]
        part = lax.dot_general(a_ref[...].astype(BF16), b_ref[...].astype(BF16), _DIMS[mode],
                               preferred_element_type=F32)

        def finish(acc):
            vals = epilogue(acc, *[e[...] for e in ex])
            for o_ref, val in zip(outs, vals[:no]):
                o_ref[...] = val.astype(o_ref.dtype)
            for s_ref, val in zip(sums, vals[no:]):
                @pl.when(pl.program_id(0) == 0)
                def _(s_ref=s_ref, val=val):
                    s_ref[...] = val

                @pl.when(pl.program_id(0) > 0)
                def _(s_ref=s_ref, val=val):
                    s_ref[...] += val

        if nk == 1:
            finish(part)
        else:
            acc_ref = refs[-1]
            kk = pl.program_id(2)

            @pl.when(kk == 0)
            def _():
                acc_ref[...] = part

            @pl.when(kk > 0)
            def _():
                acc_ref[...] += part

            @pl.when(kk == nk - 1)
            def _():
                finish(acc_ref[...])

    if mode == "tn":
        a_spec = pl.BlockSpec((tk, tm), lambda i, j, kk: (kk, i))
    else:
        a_spec = pl.BlockSpec((tm, tk), lambda i, j, kk: (i, kk))
    if mode == "nt":
        b_spec = pl.BlockSpec((tn, tk), lambda i, j, kk: (j, kk))
    else:
        b_spec = pl.BlockSpec((tk, tn), lambda i, j, kk: (kk, j))
    o_spec = pl.BlockSpec((tm, tn), lambda i, j, kk: (i, j))
    v_spec = pl.BlockSpec((1, tn), lambda i, j, kk: (0, j))
    res = pl.pallas_call(
        body,
        name=name,
        grid=(m // tm, n // tn, nk),
        in_specs=([a_spec, b_spec] + [o_spec] * len(extras) + [v_spec] * len(vectors)
                  + [pl.BlockSpec(memory_space=pl.ANY)] * nafter),
        out_specs=[o_spec] * no + [v_spec] * n_sums,
        out_shape=[jax.ShapeDtypeStruct((m, n), dt) for dt in out_dtypes] + [jax.ShapeDtypeStruct((1, n), F32)] * n_sums,
        scratch_shapes=[pltpu.VMEM((tm, tn), F32)] if nk > 1 else [],
        compiler_params=_params(*(("arbitrary",) * 3 if n_sums else ("parallel", "parallel", "arbitrary"))),
    )(a, b, *extras, *vectors, *([after] if nafter else []))
    return res[0] if no + n_sums == 1 else res


def _ep_residual_norm(acc, res, w):
    h = res + acc
    return h, _rms(h, w)


def _ep_norm_bwd(acc, x, dres, w):
    _, vjp = jax.vjp(_rms, x, w)
    dx, dw = vjp(acc)
    dx = dres + dx
    return dx, dx, dw


def _rms_fwd(x, w, name, tm=512):
    n, d = x.shape
    tm = _tile(n, tm)

    def body(x_ref, w_ref, y_ref):
        y_ref[...] = _rms(x_ref[...], w_ref[...]).astype(y_ref.dtype)

    return pl.pallas_call(
        body, name=name, grid=(n // tm,),
        in_specs=[pl.BlockSpec((tm, d), lambda i: (i, 0)), pl.BlockSpec((1, d), lambda i: (0, 0))],
        out_specs=pl.BlockSpec((tm, d), lambda i: (i, 0)),
        out_shape=jax.ShapeDtypeStruct((n, d), BF16),
        compiler_params=_params("arbitrary"),
    )(x, w)


def _loss_head(h, w, target, name, tm=512):
    n, d = h.shape
    tm = _tile(n, tm)

    def body(h_ref, w_ref, t_ref, dh_ref, dhb_ref, dw_ref, sq_ref):
        y, vjp = jax.vjp(_rms, h_ref[...], w_ref[...])
        err = y - t_ref[...]
        dh, dw = vjp(err * (1.0 / d))
        dh_ref[...] = dh
        dhb_ref[...] = dh.astype(dhb_ref.dtype)
        sq = jnp.sum(err * err, axis=0, keepdims=True)

        @pl.when(pl.program_id(0) == 0)
        def _():
            dw_ref[...] = dw
            sq_ref[...] = sq

        @pl.when(pl.program_id(0) > 0)
        def _():
            dw_ref[...] += dw
            sq_ref[...] += sq

        @pl.when(pl.program_id(0) == n // tm - 1)
        def _():
            total = jnp.sum(sq_ref[...], axis=1, keepdims=True) * (0.5 / d)
            sq_ref[...] = jnp.broadcast_to(total, sq_ref.shape)

    row = pl.BlockSpec((tm, d), lambda i: (i, 0))
    vec = pl.BlockSpec((1, d), lambda i: (0, 0))
    return pl.pallas_call(
        body, name=name, grid=(n // tm,),
        in_specs=[row, vec, row],
        out_specs=[row, row, vec, vec],
        out_shape=[jax.ShapeDtypeStruct((n, d), F32), jax.ShapeDtypeStruct((n, d), BF16),
                   jax.ShapeDtypeStruct((1, d), F32), jax.ShapeDtypeStruct((1, d), F32)],
        compiler_params=_params("arbitrary"),
    )(h, w, target)


def _lower_bounds(logits):
    sm = jax.nn.softmax(logits, axis=0)
    rows = [sm[0:1] * 0.0]
    for r in range(1, DEPTH):
        rows.append(rows[-1] + sm[r:r + 1])
    return jnp.concatenate(rows, axis=0)


def _lb_fwd(logits, name):
    def body(l_ref, o_ref):
        o_ref[...] = _lower_bounds(l_ref[...])

    return pl.pallas_call(body, name=name, out_shape=jax.ShapeDtypeStruct(logits.shape, F32))(logits)


def _lb_bwd(logits, dlb, name):
    def body(l_ref, d_ref, o_ref):
        _, vjp = jax.vjp(_lower_bounds, l_ref[...])
        (o_ref[...],) = vjp(d_ref[...])

    return pl.pallas_call(body, name=name, out_shape=jax.ShapeDtypeStruct(logits.shape, F32))(logits, dlb)


def _head_slice(h):
    return pl.ds(h * HEAD_DIM, HEAD_DIM)


_GD_HEADS = jax.vmap(_gd_head, in_axes=(0, 0, 0, 0, 0, 0, 0, 0, None))
_GD_HEADS_AGAIN = jax.vmap(_gd_head, in_axes=(0, 0, 0, 0, 0, 0, 0, 0, None, 0))


def _lane_blocks(width, block_body):
    def trip(j, carry):
        block_body(lambda base=0: pl.ds(pl.multiple_of(j * LANE_BLOCK + base, LANE_BLOCK), LANE_BLOCK))
        return carry

    lax.fori_loop(0, width // LANE_BLOCK, trip, 0, unroll=BLOCK_UNROLL)


def _row_blocks(rows, block_body):
    def trip(j, carry):
        block_body(pl.ds(pl.multiple_of(j * ROW_BLOCK, ROW_BLOCK), ROW_BLOCK))
        return carry

    lax.fori_loop(0, rows // ROW_BLOCK, trip, 0, unroll=BLOCK_UNROLL)


def _hg_pre_block(p_ref, lb_ref, sums_refs, q_sc, k_sc, v_sc, e_sc, at):
    sl = at()
    q_sc[:, sl], k_sc[:, sl], e_sc[:, sl] = _hg_pre(
        p_ref[:, sl].astype(F32), p_ref[:, at(D_MODEL)].astype(F32), lb_ref[:, sl], [r[...] for r in sums_refs])
    v_sc[:, sl] = p_ref[:, at(2 * D_MODEL)].astype(F32)


def _gd_xp(halo_ref, p_ref, sl, first_chunk):
    halo = jnp.where(first_chunk, 0.0, halo_ref[:, sl].astype(F32))
    return jnp.concatenate([halo, p_ref[:, sl].astype(F32)], axis=0)


def _stack_all(ref, first=0):
    return jnp.stack([ref[s, :, _head_slice(h + first)] for s in range(ref.shape[0]) for h in range(N_HEADS)])


def _unstack_all(ref, val, first=0):
    for s in range(ref.shape[0]):
        for h in range(N_HEADS):
            ref[s, :, _head_slice(h + first)] = val[s * N_HEADS + h].astype(ref.dtype)


def _gdn_fwd_all(projm, projab, cw, alog, dtb, onw, seqs, name):
    n = projm.shape[0]
    t = n // seqs
    nc = t // CHUNK
    d = D_MODEL
    per_halo = CHUNK // HALO
    nh = seqs * N_HEADS

    def body(p_ref, halo_ref, ab_ref, cw_ref, alog_ref, dtb_ref, onw_ref, o2_ref, st_all_ref, y_ref, dinv_ref,
             st_sc, c_sc, beta_sc, g_sc):
        first_chunk = pl.program_id(0) == 0

        @pl.when(first_chunk)
        def _():
            st_sc[...] = jnp.zeros_like(st_sc)

        for s in range(seqs):
            def conv(at, s=s):
                sl = at()
                y = _gd_conv(_gd_xp(halo_ref.at[s], p_ref.at[s], sl, first_chunk), cw_ref[:, sl])
                y_ref[s, :, sl] = y
                c_sc[s, :, sl] = jax.nn.silu(y)

            _lane_blocks(3 * d, conv)
            beta_sc[s], g_sc[s] = _gd_gates(ab_ref[s, :, 0:N_HEADS], ab_ref[s, :, N_HEADS:2 * N_HEADS],
                                            alog_ref[...], dtb_ref[...])
        st_all_ref[0] = st_sc[...]
        o2, st_sc[...], dinv_ref[0] = _GD_HEADS(
            st_sc[...], _stack_all(c_sc), _stack_all(c_sc, N_HEADS), _stack_all(c_sc, 2 * N_HEADS), _stack_all(beta_sc),
            _stack_all(g_sc), _stack_all(g_sc, N_HEADS), _stack_all(p_ref, 3 * N_HEADS).astype(F32), onw_ref[...])
        _unstack_all(o2_ref, o2)

    rows = lambda c: (0, c, 0)
    const = lambda c: (0, 0)
    per_chunk = lambda c: (c, 0, 0, 0)
    p3 = projm.reshape(seqs, t, 4 * d)
    o2, st_all, conv_y, dinv_all = pl.pallas_call(
        body, name=name, grid=(nc,),
        in_specs=[pl.BlockSpec((seqs, CHUNK, 4 * d), rows),
                  pl.BlockSpec((seqs, HALO, 3 * d), lambda c: (0, jnp.maximum(c * per_halo - 1, 0), 0)),
                  pl.BlockSpec((seqs, CHUNK, AB_PAD), rows),
                  pl.BlockSpec((CONV_K, 3 * d), const), pl.BlockSpec((1, N_HEADS), const),
                  pl.BlockSpec((1, N_HEADS), const), pl.BlockSpec((1, HEAD_DIM), const)],
        out_specs=[pl.BlockSpec((seqs, CHUNK, d), rows), pl.BlockSpec((1, nh, HEAD_DIM, HEAD_DIM), per_chunk),
                   pl.BlockSpec((seqs, CHUNK, 3 * d), rows), pl.BlockSpec((1, nh, CHUNK, CHUNK), per_chunk)],
        out_shape=[jax.ShapeDtypeStruct((seqs, t, d), BF16), jax.ShapeDtypeStruct((nc, nh, HEAD_DIM, HEAD_DIM), F32),
                   jax.ShapeDtypeStruct((seqs, t, 3 * d), F32), jax.ShapeDtypeStruct((nc, nh, CHUNK, CHUNK), F32)],
        scratch_shapes=[pltpu.VMEM((nh, HEAD_DIM, HEAD_DIM), F32), pltpu.VMEM((seqs, CHUNK, 3 * d), F32),
                        pltpu.VMEM((seqs, CHUNK, d), F32), pltpu.VMEM((seqs, CHUNK, 2 * d), F32)],
        compiler_params=_params("arbitrary"),
    )(p3, p3, projab.reshape(seqs, t, AB_PAD), cw, alog, dtb, onw)
    return o2.reshape(n, d), st_all, conv_y, dinv_all


def _gdn_bwd_all(projm, projab, conv_y, cw, alog, dtb, onw, st_all, dinv_all, do2, seqs, name):
    n = projm.shape[0]
    t = n // seqs
    nc = t // CHUNK
    d = D_MODEL
    per_halo = CHUNK // HALO
    nh = seqs * N_HEADS

    def body(p_ref, halo_ref, ab_ref, y_ref, cw_ref, alog_ref, dtb_ref, onw_ref, st_all_ref, dinv_ref, do2_ref,
             dp_ref, dab_ref, dcw_ref, dalog_ref, ddtb_ref, donw_ref,
             dst_sc, dhalo_sc, c_sc, beta_sc, g_sc, dc_sc, dbeta_sc, dg_sc):
        first = pl.program_id(0) == 0
        first_chunk = pl.program_id(0) == nc - 1

        @pl.when(first)
        def _():
            dst_sc[...] = jnp.zeros_like(dst_sc)
            dhalo_sc[...] = jnp.zeros_like(dhalo_sc)

        gates_vjps = []
        for s in range(seqs):
            def act(at, s=s):
                c_sc[s, :, at()] = jax.nn.silu(y_ref[s, :, at()])

            _lane_blocks(3 * d, act)
            (beta_sc[s], g_sc[s]), gates_vjp = jax.vjp(
                _gd_gates, ab_ref[s, :, 0:N_HEADS], ab_ref[s, :, N_HEADS:2 * N_HEADS], alog_ref[...], dtb_ref[...])
            gates_vjps.append(gates_vjp)

        dinv = dinv_ref[0]
        _, vjp = jax.vjp(
            lambda *a: _GD_HEADS_AGAIN(*a, dinv), st_all_ref[0], _stack_all(c_sc), _stack_all(c_sc, N_HEADS),
            _stack_all(c_sc, 2 * N_HEADS), _stack_all(beta_sc), _stack_all(g_sc), _stack_all(g_sc, N_HEADS),
            _stack_all(p_ref, 3 * N_HEADS).astype(F32), onw_ref[...])
        dst_sc[...], dq, dk, dv, dbeta, dg, ddiff, dgate, donw = vjp((_stack_all(do2_ref).astype(F32), dst_sc[...]))
        _unstack_all(dc_sc, dq)
        _unstack_all(dc_sc, dk, N_HEADS)
        _unstack_all(dc_sc, dv, 2 * N_HEADS)
        _unstack_all(dbeta_sc, dbeta)
        _unstack_all(dg_sc, dg)
        _unstack_all(dg_sc, ddiff, N_HEADS)
        _unstack_all(dp_ref, dgate, 3 * N_HEADS)

        dalog, ddtb = None, None
        for s in range(seqs):
            def conv_bwd(at, s=s):
                sl = at()
                dxp, dcw = _gd_conv_bwd(_gd_xp(halo_ref.at[s], p_ref.at[s], sl, first_chunk), cw_ref[:, sl],
                                        y_ref[s, :, sl], dc_sc[s, :, sl])
                dqkv = jnp.concatenate([dxp[HALO:CHUNK], dxp[CHUNK:HALO + CHUNK] + dhalo_sc[s, :, sl]], axis=0)
                dp_ref[s, :, sl] = dqkv.astype(dp_ref.dtype)
                dhalo_sc[s, :, sl] = dxp[0:HALO]

                if s > 0:
                    dcw_ref[:, sl] += dcw
                    return

                @pl.when(first)
                def _():
                    dcw_ref[:, sl] = dcw

                @pl.when(jnp.logical_not(first))
                def _():
                    dcw_ref[:, sl] += dcw

            _lane_blocks(3 * d, conv_bwd)
            da, db, dalog_s, ddtb_s = gates_vjps[s]((dbeta_sc[s], dg_sc[s]))
            dab_ref[s] = jnp.concatenate(
                [da, db, jnp.zeros((CHUNK, AB_PAD - 2 * N_HEADS), F32)], axis=1).astype(dab_ref.dtype)
            dalog = dalog_s if dalog is None else dalog + dalog_s
            ddtb = ddtb_s if ddtb is None else ddtb + ddtb_s

        @pl.when(first)
        def _():
            dalog_ref[...] = dalog
            ddtb_ref[...] = ddtb
            donw_ref[...] = donw

        @pl.when(jnp.logical_not(first))
        def _():
            dalog_ref[...] += dalog
            ddtb_ref[...] += ddtb
            donw_ref[...] += donw

    back = lambda c: nc - 1 - c
    rows = lambda c: (0, back(c), 0)
    const = lambda c: (0, 0)
    per_chunk = lambda c: (back(c), 0, 0, 0)
    small = [pl.BlockSpec((CONV_K, 3 * d), const), pl.BlockSpec((1, N_HEADS), const),
             pl.BlockSpec((1, N_HEADS), const), pl.BlockSpec((1, HEAD_DIM), const)]
    p3 = projm.reshape(seqs, t, 4 * d)
    dp, dab, dcw, dalog, ddtb, donw = pl.pallas_call(
        body, name=name, grid=(nc,),
        in_specs=[pl.BlockSpec((seqs, CHUNK, 4 * d), rows),
                  pl.BlockSpec((seqs, HALO, 3 * d), lambda c: (0, jnp.maximum(back(c) * per_halo - 1, 0), 0)),
                  pl.BlockSpec((seqs, CHUNK, AB_PAD), rows), pl.BlockSpec((seqs, CHUNK, 3 * d), rows)] + small + [
                  pl.BlockSpec((1, nh, HEAD_DIM, HEAD_DIM), per_chunk), pl.BlockSpec((1, nh, CHUNK, CHUNK), per_chunk),
                  pl.BlockSpec((seqs, CHUNK, d), rows)],
        out_specs=[pl.BlockSpec((seqs, CHUNK, 4 * d), rows), pl.BlockSpec((seqs, CHUNK, AB_PAD), rows)] + small,
        out_shape=[jax.ShapeDtypeStruct((seqs, t, 4 * d), BF16), jax.ShapeDtypeStruct((seqs, t, AB_PAD), BF16),
                   jax.ShapeDtypeStruct((CONV_K, 3 * d), F32), jax.ShapeDtypeStruct((1, N_HEADS), F32),
                   jax.ShapeDtypeStruct((1, N_HEADS), F32), jax.ShapeDtypeStruct((1, HEAD_DIM), F32)],
        scratch_shapes=[pltpu.VMEM((nh, HEAD_DIM, HEAD_DIM), F32), pltpu.VMEM((seqs, HALO, 3 * d), F32),
                        pltpu.VMEM((seqs, CHUNK, 3 * d), F32), pltpu.VMEM((seqs, CHUNK, d), F32),
                        pltpu.VMEM((seqs, CHUNK, 2 * d), F32), pltpu.VMEM((seqs, CHUNK, 3 * d), F32),
                        pltpu.VMEM((seqs, CHUNK, d), F32), pltpu.VMEM((seqs, CHUNK, 2 * d), F32)],
        compiler_params=_params("arbitrary"),
    )(p3, p3, projab.reshape(seqs, t, AB_PAD), conv_y, cw, alog, dtb, onw, st_all, dinv_all,
      do2.reshape(seqs, t, d))
    return dp.reshape(n, 4 * d), dab.reshape(n, AB_PAD), dcw, dalog, ddtb, donw


def _hgrn_fwd_all(proj, lb, gw, seqs, name):
    n = proj.shape[0]
    t = n // seqs
    nc = t // CHUNK
    d = D_MODEL
    nh = seqs * N_HEADS
    sums, masks = _hg_level_sums(), _hg_level_masks()

    def body(p_ref, lb_ref, gw_ref, sums_wide_ref, sums_once_ref, masks_ref, o2_ref, o_ref, st_all_ref,
             st_sc, q_sc, k_sc, v_sc, e_sc):
        @pl.when(pl.program_id(0) == 0)
        def _():
            st_sc[...] = jnp.zeros_like(st_sc)

        sums_refs = (sums_wide_ref, sums_once_ref)
        for s in range(seqs):
            _lane_blocks(d, functools.partial(_hg_pre_block, p_ref.at[s], lb_ref, sums_refs, q_sc.at[s], k_sc.at[s],
                                              v_sc.at[s], e_sc.at[s]))
        st_all_ref[0] = st_sc[...]
        o, st_sc[...] = _HG_HEADS(st_sc[...], *[_stack_all(r) for r in (q_sc, k_sc, v_sc, e_sc)], masks_ref[...])
        _unstack_all(o_ref, o)
        for s in range(seqs):
            def post(rows, s=s):
                gate = p_ref[s, rows, 3 * d:4 * d].astype(F32)
                o2_ref[s, rows, :] = _hg_post(o_ref[s, rows, :], gate, gw_ref[...]).astype(o2_ref.dtype)

            _row_blocks(CHUNK, post)

    rows = lambda c: (0, c, 0)
    vec = pl.BlockSpec((1, d), lambda c: (0, 0))
    act = pl.BlockSpec((seqs, CHUNK, d), rows)
    o2, o, st_all = pl.pallas_call(
        body, name=name, grid=(nc,),
        in_specs=[pl.BlockSpec((seqs, CHUNK, 4 * d), rows), vec, vec]
        + [pl.BlockSpec(m.shape, lambda c: (0, 0)) for m in sums] + [pl.BlockSpec(masks.shape, lambda c: (0, 0, 0))],
        out_specs=[act, act, pl.BlockSpec((1, nh, HEAD_DIM, HEAD_DIM), lambda c: (c, 0, 0, 0))],
        out_shape=[jax.ShapeDtypeStruct((seqs, t, d), BF16), jax.ShapeDtypeStruct((seqs, t, d), F32),
                   jax.ShapeDtypeStruct((nc, nh, HEAD_DIM, HEAD_DIM), F32)],
        scratch_shapes=[pltpu.VMEM((nh, HEAD_DIM, HEAD_DIM), F32)] + [pltpu.VMEM((seqs, CHUNK, d), F32)] * 3
        + [pltpu.VMEM((seqs, sums[0].shape[0], d), F32)],
        compiler_params=_params("arbitrary"),
    )(proj.reshape(seqs, t, 4 * d), lb, gw, *sums, masks)
    return o2.reshape(n, d), o, st_all


def _hgrn_bwd_all(proj, lb, gw, st_all, o, do2, seqs, name):
    n = proj.shape[0]
    t = n // seqs
    nc = t // CHUNK
    d = D_MODEL
    nh = seqs * N_HEADS
    sums, masks = _hg_level_sums(), _hg_level_masks()

    def body(p_ref, lb_ref, gw_ref, sums_wide_ref, sums_once_ref, masks_ref, st_all_ref, o_ref, do2_ref,
             dp_ref, dlb_ref, dgw_ref,
             dst_sc, q_sc, k_sc, v_sc, e_sc, do_sc, dq_sc, dk_sc, dv_sc, de_sc, dgw_sc):
        first = pl.program_id(0) == 0

        @pl.when(first)
        def _():
            dst_sc[...] = jnp.zeros_like(dst_sc)

        sums_refs = (sums_wide_ref, sums_once_ref)
        dgw_sc[...] = jnp.zeros_like(dgw_sc)
        for s in range(seqs):
            _lane_blocks(d, functools.partial(_hg_pre_block, p_ref.at[s], lb_ref, sums_refs, q_sc.at[s], k_sc.at[s],
                                              v_sc.at[s], e_sc.at[s]))

            def post_bwd(rows, s=s):
                _, vjp = jax.vjp(_hg_post, o_ref[s, rows, :], p_ref[s, rows, 3 * d:4 * d].astype(F32), gw_ref[...])
                do_sc[s, rows, :], dgate, dgw = vjp(do2_ref[s, rows, :].astype(F32))
                dp_ref[s, rows, 3 * d:4 * d] = dgate.astype(dp_ref.dtype)
                dgw_sc[...] += dgw

            _row_blocks(CHUNK, post_bwd)

        level_masks = masks_ref[...]
        _, vjp = jax.vjp(lambda *a: _HG_HEADS(*a, level_masks), st_all_ref[0],
                         *[_stack_all(r) for r in (q_sc, k_sc, v_sc, e_sc)])
        grads = vjp((_stack_all(do_sc), dst_sc[...]))
        dst_sc[...] = grads[0]
        for r, val in zip((dq_sc, dk_sc, dv_sc, de_sc), grads[1:]):
            _unstack_all(r, val)

        for s in range(seqs):
            def pre_bwd(at, s=s):
                sl = at()
                level_sums = (sums_wide_ref[...], sums_once_ref[...])
                _, vjp = jax.vjp(lambda qraw, f, lb: _hg_pre(qraw, f, lb, level_sums), p_ref[s, :, sl].astype(F32),
                                 p_ref[s, :, at(d)].astype(F32), lb_ref[:, sl])
                dqraw, df, dlb = vjp((dq_sc[s, :, sl], dk_sc[s, :, sl], de_sc[s, :, sl]))
                dp_ref[s, :, sl] = dqraw.astype(dp_ref.dtype)
                dp_ref[s, :, at(d)] = df.astype(dp_ref.dtype)
                dp_ref[s, :, at(2 * d)] = dv_sc[s, :, sl].astype(dp_ref.dtype)
                if s > 0:
                    dlb_ref[:, sl] += dlb
                    return

                @pl.when(first)
                def _():
                    dlb_ref[:, sl] = dlb

                @pl.when(jnp.logical_not(first))
                def _():
                    dlb_ref[:, sl] += dlb

            _lane_blocks(d, pre_bwd)

        @pl.when(first)
        def _():
            dgw_ref[...] = dgw_sc[...]

        @pl.when(jnp.logical_not(first))
        def _():
            dgw_ref[...] += dgw_sc[...]

    rows = lambda c: (0, nc - 1 - c, 0)
    vec = pl.BlockSpec((1, d), lambda c: (0, 0))
    act = pl.BlockSpec((seqs, CHUNK, d), rows)
    wide = pl.BlockSpec((seqs, CHUNK, 4 * d), rows)
    e_rows = sums[0].shape[0]
    dp, dlb, dgw = pl.pallas_call(
        body, name=name, grid=(nc,),
        in_specs=[wide, vec, vec] + [pl.BlockSpec(m.shape, lambda c: (0, 0)) for m in sums] + [
                  pl.BlockSpec(masks.shape, lambda c: (0, 0, 0)),
                  pl.BlockSpec((1, nh, HEAD_DIM, HEAD_DIM), lambda c: (nc - 1 - c, 0, 0, 0)), act, act],
        out_specs=[wide, vec, vec],
        out_shape=[jax.ShapeDtypeStruct((seqs, t, 4 * d), BF16), jax.ShapeDtypeStruct((1, d), F32),
                   jax.ShapeDtypeStruct((1, d), F32)],
        scratch_shapes=[pltpu.VMEM((nh, HEAD_DIM, HEAD_DIM), F32)]
        + [pltpu.VMEM((seqs, CHUNK, d), F32)] * 3 + [pltpu.VMEM((seqs, e_rows, d), F32)]
        + [pltpu.VMEM((seqs, CHUNK, d), F32)] * 4 + [pltpu.VMEM((seqs, e_rows, d), F32), pltpu.VMEM((1, d), F32)],
        compiler_params=_params("arbitrary"),
    )(proj.reshape(seqs, t, 4 * d), lb, gw, *sums, masks, st_all, o, do2.reshape(seqs, t, d))
    return dp.reshape(n, 4 * d), dlb, dgw


def _adam_update(w, g, m, v):
    b1c = 1.0 - ADAM_B1 ** ADAM_STEP
    b2c = 1.0 - ADAM_B2 ** ADAM_STEP
    m_new = ADAM_B1 * m + (1.0 - ADAM_B1) * g
    v_new = ADAM_B2 * v + (1.0 - ADAM_B2) * (g * g)
    delta = -ADAM_LR * ((m_new / b1c) / (jnp.sqrt(v_new / b2c) + ADAM_EPS) + ADAM_WD * w)
    return delta, m_new, v_new


def _adamw(w, g, m, v, name, tr=256):
    r, c = w.shape
    tr = _tile(r, tr)

    def body(w_ref, g_ref, m_ref, v_ref, d_ref, mo_ref, vo_ref):
        d_ref[...], mo_ref[...], vo_ref[...] = _adam_update(w_ref[...], g_ref[...], m_ref[...], v_ref[...])

    blk = pl.BlockSpec((tr, c), lambda i: (i, 0))
    return pl.pallas_call(
        body, name=name, grid=(r // tr,),
        in_specs=[blk] * 4, out_specs=[blk] * 3,
        out_shape=[jax.ShapeDtypeStruct((r, c), F32)] * 3,
        compiler_params=_params("arbitrary"),
    )(w, g, m, v)


def _adamw_slots(w, slot_bufs, m, v, name, tr=256):
    nl, r, c = w.shape
    tr = _tile(r, tr)

    def body(*refs):
        w_ref = refs[0]
        g_refs = refs[1:1 + nl]
        m_ref, v_ref, go_ref, d_ref, mo_ref, vo_ref = refs[1 + nl:]
        for k in range(nl):
            @pl.when(pl.program_id(0) == k)
            def _(k=k):
                g = g_refs[k][0].astype(F32)
                for s in range(1, N_DEV):
                    g = g + g_refs[k][s].astype(F32)
                go_ref[0] = g

        d_ref[0], mo_ref[0], vo_ref[0] = _adam_update(w_ref[0], go_ref[0], m_ref[0], v_ref[0])

    blk = pl.BlockSpec((1, tr, c), lambda l, i: (l, i, 0))
    g_specs = [pl.BlockSpec((N_DEV, tr, c), lambda l, i, k=k: (0, jnp.where(l == k, i, 0), 0)) for k in range(nl)]
    return pl.pallas_call(
        body, name=name, grid=(nl, r // tr),
        in_specs=[blk] + g_specs + [blk, blk], out_specs=[blk] * 4,
        out_shape=[jax.ShapeDtypeStruct((nl, r, c), F32)] * 4,
        compiler_params=_params("arbitrary", "arbitrary"),
    )(w, *slot_bufs, m, v)


def _mesh_pos():
    return lax.axis_index("x"), lax.axis_index("y"), lax.axis_index("c")


def _flip(pos, p):
    x, y, c = pos
    return ((1 - x) if p & 4 else x, (1 - y) if p & 2 else y, (1 - c) if p & 1 else c)


def _lin(pos):
    return 4 * pos[0] + 2 * pos[1] + pos[2]


_HBM = pl.BlockSpec(memory_space=pltpu.HBM)
_SEM = pl.BlockSpec(memory_space=pltpu.SEMAPHORE)
_DATAFLOW = pltpu.SideEffectType.DATAFLOW_SIDE_EFFECTING


class _Item:
    def __init__(self, src, land_shape, src_pick, dst_pick):
        self.src, self.land_shape, self.src_pick, self.dst_pick = src, land_shape, src_pick, dst_pick


def _remote_copies(items, src, land, send_sem, recv_sem, me, arriving):
    me_i = _lin(me)
    out = []
    for it, s_ref, l_ref in zip(items, src, land):
        for p in range(1, N_DEV):
            peer = _flip(me, p)
            out.append(pltpu.make_async_remote_copy(
                src_ref=it.src_pick(s_ref, _lin(peer)),
                dst_ref=it.dst_pick(l_ref, _lin(peer) if arriving else me_i),
                send_sem=send_sem, recv_sem=recv_sem, device_id=peer, device_id_type=pl.DeviceIdType.MESH))
    return out


def _own_copies(items, src, land, sem, me):
    me_i = _lin(me)
    return [pltpu.make_async_copy(it.src_pick(s_ref, me_i), it.dst_pick(l_ref, me_i), sem)
            for it, s_ref, l_ref in zip(items, src, land)]


def _exchange_start(groups, name):
    items = [it for g in groups for it in g]
    n, ng = len(items), len(groups)
    first = [sum(len(g) for g in groups[:gi]) for gi in range(ng)]

    def body(*refs):
        src, land = refs[0:n], refs[n:2 * n]
        send_sems, recv_sems = refs[2 * n:2 * n + ng], refs[2 * n + ng:2 * n + 2 * ng]
        token = refs[4 * n + 2 * ng]
        me = _mesh_pos()
        for gi, g in enumerate(groups):
            sl = slice(first[gi], first[gi] + len(g))
            for cp in _remote_copies(g, src[sl], land[sl], send_sems[gi], recv_sems[gi], me, arriving=False):
                cp.start()
            for cp in _own_copies(g, src[sl], land[sl], recv_sems[gi], me):
                cp.start()
        token[...] = jnp.zeros_like(token)

    srcs = [pltpu.with_memory_space_constraint(it.src, pltpu.HBM) for it in items]
    lands = [pltpu.with_memory_space_constraint(lax.empty(it.land_shape, it.src.dtype), pltpu.HBM) for it in items]
    res = pl.pallas_call(
        body, name=name,
        out_shape=([pltpu.SemaphoreType.DMA(())] * (2 * ng)
                   + [pltpu.HBM(it.src.shape, it.src.dtype) for it in items]
                   + [pltpu.HBM(it.land_shape, it.src.dtype) for it in items]
                   + [jax.ShapeDtypeStruct((8, 128), F32)]),
        in_specs=[_HBM] * (2 * n),
        out_specs=[_SEM] * (2 * ng) + [_HBM] * (2 * n) + [pl.BlockSpec(memory_space=pltpu.VMEM)],
        input_output_aliases={i: 2 * ng + i for i in range(2 * n)},
        compiler_params=pltpu.CompilerParams(has_side_effects=_DATAFLOW),
    )(*srcs, *lands)
    send_sems, recv_sems = res[0:ng], res[ng:2 * ng]
    src_thru, land_thru = res[2 * ng:2 * ng + n], res[2 * ng + n:2 * ng + 2 * n]
    handles = []
    for gi, g in enumerate(groups):
        sl = slice(first[gi], first[gi] + len(g))
        handles.append((g, src_thru[sl], land_thru[sl], send_sems[gi], recv_sems[gi]))
    return handles, res[-1]


def _exchange_wait(handle, after, name):
    items, src_thru, land_thru, send_sem, recv_sem = handle
    k = len(items)

    def body(*refs):
        src, land = refs[0:k], refs[k:2 * k]
        send_ref, recv_ref = refs[2 * k], refs[2 * k + 1]
        for cp in _remote_copies(items, src, land, send_ref, recv_ref, _mesh_pos(), arriving=True):
            cp.wait_send()
            cp.wait_recv()
        for cp in _own_copies(items, src, land, recv_ref, _mesh_pos()):
            cp.wait()

    res = pl.pallas_call(
        body, name=name,
        out_shape=([pltpu.HBM(s.shape, s.dtype) for s in src_thru] + [pltpu.HBM(l.shape, l.dtype) for l in land_thru]),
        in_specs=[_HBM] * (2 * k) + [_SEM, _SEM, pl.BlockSpec(memory_space=pl.ANY)],
        out_specs=[_HBM] * (2 * k),
        input_output_aliases={i: i for i in range(2 * k)},
        compiler_params=pltpu.CompilerParams(has_side_effects=_DATAFLOW),
    )(*src_thru, *land_thru, send_sem, recv_sem, after)
    return res[k:2 * k]


def _whole(ref, i):
    return ref


def _slot(ref, i):
    return ref.at[i]


def _rows_of(r):
    return lambda ref, i: ref.at[pl.ds(pl.multiple_of(i * r, r), r), :]


def _cols_of(c):
    return lambda ref, i: ref.at[:, pl.ds(pl.multiple_of(i * c, c), c)]


def _all_reduce_small(buf, after, name):
    r, c = buf.shape

    def body(src_ref, after_ref, out_ref, all_ref, send_sems, recv_sems):
        me = _mesh_pos()
        me_i = _lin(me)
        all_ref[me_i] = src_ref[...]
        for p in range(1, N_DEV):
            peer = _flip(me, p)
            pltpu.make_async_remote_copy(
                src_ref=src_ref, dst_ref=all_ref.at[me_i], send_sem=send_sems.at[p - 1], recv_sem=recv_sems.at[p - 1],
                device_id=peer, device_id_type=pl.DeviceIdType.MESH).start()
        for p in range(1, N_DEV):
            peer = _flip(me, p)
            cp = pltpu.make_async_remote_copy(
                src_ref=src_ref, dst_ref=all_ref.at[_lin(peer)], send_sem=send_sems.at[p - 1],
                recv_sem=recv_sems.at[p - 1], device_id=peer, device_id_type=pl.DeviceIdType.MESH)
            cp.wait_recv()
            cp.wait_send()
        acc = all_ref[0]
        for s in range(1, N_DEV):
            acc = acc + all_ref[s]
        out_ref[...] = acc

    vm = pl.BlockSpec(memory_space=pltpu.VMEM)
    return pl.pallas_call(
        body, name=name, in_specs=[vm, pl.BlockSpec(memory_space=pl.ANY)], out_specs=vm,
        out_shape=jax.ShapeDtypeStruct((r, c), F32),
        scratch_shapes=[pltpu.VMEM((N_DEV, r, c), F32), pltpu.SemaphoreType.DMA((N_DEV - 1,)),
                        pltpu.SemaphoreType.DMA((N_DEV - 1,))],
        compiler_params=pltpu.CompilerParams(has_side_effects=True),
    )(buf, after)


def _unshard_cols(g):
    s, l, r, c = g.shape
    return jnp.transpose(g, (1, 2, 0, 3)).reshape(l, r, s * c)


def kernel(x, gdn_w_in, gdn_conv, gdn_a_log, gdn_dt_bias, gdn_onorm, gdn_w_out, hgrn_w_in, hgrn_lb_logits, hgrn_gnorm, hgrn_w_out, norm_mix, norm_mlp, mlp_w_up, mlp_w_down, norm_final, loss_target, m_gdn_w_in, m_gdn_conv, m_gdn_a_log, m_gdn_dt_bias, m_gdn_onorm, m_gdn_w_out, m_hgrn_w_in, m_hgrn_lb_logits, m_hgrn_gnorm, m_hgrn_w_out, m_norm_mix, m_norm_mlp, m_mlp_w_up, m_mlp_w_down, m_norm_final, v_gdn_w_in, v_gdn_conv, v_gdn_a_log, v_gdn_dt_bias, v_gdn_onorm, v_gdn_w_out, v_hgrn_w_in, v_hgrn_lb_logits, v_hgrn_gnorm, v_hgrn_w_out, v_norm_mix, v_norm_mlp, v_mlp_w_up, v_mlp_w_down, v_norm_final):
    seqs, seq_len, d = x.shape
    n = seqs * seq_len
    me_i = _lin(_mesh_pos())
    x2 = x.reshape(n, d)
    target = loss_target.reshape(n, d)
    n_gdn, n_hgrn = gdn_w_in.shape[0], hgrn_w_in.shape[0]

    r_out, r_down = gdn_w_out.shape[1], mlp_w_down.shape[1]
    c_gin, c_hin, c_up = gdn_w_in.shape[2], hgrn_w_in.shape[2], mlp_w_up.shape[2]

    def gathered(w, pick, land_shape):
        return _Item(w.astype(BF16), land_shape, _whole, pick)

    groups = [[_Item(gdn_conv, (N_DEV,) + gdn_conv.shape, _whole, _slot),
               _Item(hgrn_gnorm, (N_DEV,) + hgrn_gnorm.shape, _whole, _slot)]]
    for i in range(DEPTH):
        j = i // 2
        if i % 2 == 0:
            groups += [[gathered(gdn_w_in[j], _slot, (N_DEV, d, c_gin))],
                       [gathered(gdn_w_out[j], _rows_of(r_out), (N_DEV * r_out, d))]]
        else:
            groups += [[gathered(hgrn_w_in[j], _cols_of(c_hin), (d, N_DEV * c_hin))],
                       [gathered(hgrn_w_out[j], _rows_of(r_out), (N_DEV * r_out, d))]]
        groups += [[gathered(mlp_w_up[i], _cols_of(c_up), (d, N_DEV * c_up))],
                   [gathered(mlp_w_down[i], _rows_of(r_down), (N_DEV * r_down, d))]]
    gather_handles, token = _exchange_start(groups, "gather_start")
    lbs = _lb_fwd(hgrn_lb_logits + token[0:1, 0:1], "lb_fwd")

    def arrived(k, after, name):
        return _exchange_wait(gather_handles[k], after, "gather_wait_" + name)

    saved = []
    w_in, w_ab, w_out, w_up, w_down = ([None] * DEPTH for _ in range(5))
    h = x2
    for i in range(DEPTH):
        j = i // 2
        if i == 0:
            g_conv, g_gnorm = arrived(0, h, "small")
            conv_full = _unshard_cols(g_conv)
            gnorm_full = jnp.transpose(g_gnorm, (1, 0, 2)).reshape(n_hgrn, d)
        if i == 0:
            y = _rms_fwd(h, norm_mix[0:1], "rms_mix_0")
        (w_in[i],) = arrived(1 + 4 * i, y, f"in_{i}")
        if i % 2 == 0:
            w_gin = jnp.transpose(w_in[i], (1, 0, 2)).reshape(d, N_DEV * c_gin)
            w_in[i] = w_gin[:, :GDN_MAIN]
            w_ab[i] = jnp.pad(w_gin[:, GDN_MAIN:], ((0, 0), (0, AB_PAD - 2 * N_HEADS)))
            projm = _mm(y, w_in[i], "nn", [BF16], f"gdn_proj_{i}")
            projab = _mm(y, w_ab[i], "nn", [F32], f"gdn_proj_ab_{i}")
            o2, st_all, conv_y, dinv_all = _gdn_fwd_all(projm, projab, conv_full[j], gdn_a_log[j:j + 1],
                                                    gdn_dt_bias[j:j + 1], gdn_onorm[j:j + 1], seqs, f"gdn_fwd_{i}")
            mix = (projm, projab, conv_y, st_all, dinv_all)
        else:
            proj = _mm(y, w_in[i], "nn", [BF16], f"hgrn_proj_{i}")
            o2, o_raw, st_all = _hgrn_fwd_all(proj, lbs[i:i + 1], gnorm_full[j:j + 1], seqs, f"hgrn_fwd_{i}")
            mix = (proj, o_raw, st_all)
        (w_out[i],) = arrived(2 + 4 * i, o2, f"out_{i}")
        h1, y2 = _mm(o2, w_out[i], "nn", [F32, BF16], f"mix_out_{i}", epilogue=_ep_residual_norm, extras=(h,),
                     vectors=(norm_mlp[i:i + 1],))
        (w_up[i],) = arrived(3 + 4 * i, y2, f"up_{i}")
        u, act = _mm(y2, w_up[i], "nn", [BF16, BF16], f"mlp_up_{i}",
                     epilogue=lambda acc: (acc, jnp.square(jnp.maximum(acc, 0.0))))
        (w_down[i],) = arrived(4 + 4 * i, act, f"down_{i}")
        saved.append((h, y, mix, o2, h1, y2, u, act))
        if i + 1 < DEPTH:
            h, y = _mm(act, w_down[i], "nn", [F32, BF16], f"mlp_down_{i}", epilogue=_ep_residual_norm, extras=(h1,),
                       vectors=(norm_mix[i + 1:i + 2],))
        else:
            h = _mm(act, w_down[i], "nn", [F32], f"mlp_down_{i}", epilogue=lambda acc, res: (res + acc,),
                    extras=(h1,))

    dh, dh_b, d_nf, sq = _loss_head(h, norm_final.reshape(1, d), target, "loss_head")

    d_nmix, d_nmlp = [None] * DEPTH, [None] * DEPTH
    d_conv, d_alog, d_dtb, d_onorm = [None] * n_gdn, [None] * n_gdn, [None] * n_gdn, [None] * n_gdn
    d_lb = [jnp.zeros((1, d), F32)] * DEPTH
    d_gnorm = [None] * n_hgrn
    mlp_handles, mix_handles = [None] * DEPTH, [None] * DEPTH
    token = None
    for i in reversed(range(DEPTH)):
        j = i // 2
        h_in, y, mix, o2, h1, y2, u, act = saved[i]
        g_down = _mm(act, dh_b, "tn", [BF16], f"g_down_{i}", after=token)
        du = _mm(dh_b, w_down[i], "nt", [BF16], f"d_u_{i}",
                 epilogue=lambda acc, uu: (acc * (2.0 * jnp.maximum(uu.astype(F32), 0.0)),), extras=(u,))
        g_up = _mm(y2, du, "tn", [BF16], f"g_up_{i}")
        mlp_handles[i], token = _exchange_start(
            [[_Item(g_down, (N_DEV, r_down, d), _rows_of(r_down), _slot)],
             [_Item(g_up, (N_DEV, d, c_up), _cols_of(c_up), _slot)]], f"scatter_start_mlp_{i}")
        dh1, dh1_b, d_nmlp[i] = _mm(du, w_up[i], "nt", [F32, BF16], f"d_y2_{i}", epilogue=_ep_norm_bwd,
                                     extras=(h1, dh), vectors=(norm_mlp[i:i + 1],), n_sums=1, after=token)
        g_out = _mm(o2, dh1_b, "tn", [BF16], f"g_out_{i}")
        do2 = _mm(dh1_b, w_out[i], "nt", [BF16], f"d_o2_{i}")
        if i % 2 == 0:
            projm, projab, conv_y, st_all, dinv_all = mix
            dpm, dpab, d_conv[j], d_alog[j], d_dtb[j], d_onorm[j] = _gdn_bwd_all(
                projm, projab, conv_y, conv_full[j], gdn_a_log[j:j + 1], gdn_dt_bias[j:j + 1], gdn_onorm[j:j + 1],
                st_all, dinv_all, do2, seqs, f"gdn_bwd_{i}")
            g_main = _mm(y, dpm, "tn", [BF16], f"g_in_{i}")
            g_ab = _mm(y, dpab, "tn", [BF16], f"g_in_ab_{i}")
            g_in = jnp.concatenate([g_main, g_ab[:, :2 * N_HEADS]], axis=1)
            g_in = jnp.transpose(g_in.reshape(d, N_DEV, c_gin), (1, 0, 2))
            in_item = _Item(g_in, (N_DEV, d, c_gin), _slot, _slot)
            dy_ab = _mm(dpab, w_ab[i], "nt", [F32], f"d_y_ab_{i}")
            dp, dy_extras = dpm, (dy_ab, h_in, dh1)
            dy_epilogue = lambda acc, e, xx, dres, w: _ep_norm_bwd(acc + e, xx, dres, w)
        else:
            proj, o_raw, st_all = mix
            dp, d_lb[i], d_gnorm[j] = _hgrn_bwd_all(proj, lbs[i:i + 1], gnorm_full[j:j + 1], st_all, o_raw, do2,
                                               seqs, f"hgrn_bwd_{i}")
            g_in = _mm(y, dp, "tn", [BF16], f"g_in_{i}")
            in_item = _Item(g_in, (N_DEV, d, c_hin), _cols_of(c_hin), _slot)
            dy_extras, dy_epilogue = (h_in, dh1), _ep_norm_bwd
        mix_handles[i], token = _exchange_start(
            [[_Item(g_out, (N_DEV, r_out, d), _rows_of(r_out), _slot)], [in_item]], f"scatter_start_mix_{i}")
        dh, dh_b, d_nmix[i] = _mm(dp, w_in[i], "nt", [F32, BF16], f"d_y_{i}", epilogue=dy_epilogue, extras=dy_extras,
                                  vectors=(norm_mix[i:i + 1],), n_sums=1, after=token)
        token = None
    grad_x = dh.reshape(x.shape)

    def landed(handles, k, layers, after, name):
        return [_exchange_wait(handles[i][k], after, f"scatter_wait_{name}_{i}")[0] for i in layers]

    every, even, odd = range(DEPTH), range(0, DEPTH, 2), range(1, DEPTH, 2)
    upd = {}
    upd["mlp_w_down"] = _adamw_slots(mlp_w_down, landed(mlp_handles, 0, every, dh, "down"), m_mlp_w_down,
                                     v_mlp_w_down, "adamw_mlp_w_down")
    upd["mlp_w_up"] = _adamw_slots(mlp_w_up, landed(mlp_handles, 1, every, upd["mlp_w_down"][1], "up"), m_mlp_w_up,
                                   v_mlp_w_up, "adamw_mlp_w_up")
    upd["hgrn_w_out"] = _adamw_slots(hgrn_w_out, landed(mix_handles, 0, odd, upd["mlp_w_up"][1], "out"),
                                     m_hgrn_w_out, v_hgrn_w_out, "adamw_hgrn_w_out")
    upd["hgrn_w_in"] = _adamw_slots(hgrn_w_in, landed(mix_handles, 1, odd, upd["hgrn_w_out"][1], "in"), m_hgrn_w_in,
                                    v_hgrn_w_in, "adamw_hgrn_w_in")

    dlb_rows = jnp.concatenate(d_lb, axis=0)
    tail = jnp.concatenate(
        [jnp.concatenate(d_onorm, axis=1), jnp.concatenate(d_alog, axis=1), jnp.concatenate(d_dtb, axis=1)], axis=1)
    tail = jnp.pad(tail, ((0, 0), (0, d - tail.shape[1])))
    conv_rows = jnp.stack(d_conv).reshape(-1, d)
    packed = jnp.concatenate(
        [jnp.concatenate(d_nmix, axis=0), jnp.concatenate(d_nmlp, axis=0), d_nf, sq, dlb_rows,
         jnp.concatenate(d_gnorm, axis=0), tail, conv_rows], axis=0)
    pad_rows = (-packed.shape[0]) % 8
    packed = jnp.pad(packed, ((0, pad_rows), (0, 0)))
    tot = _all_reduce_small(packed, upd["hgrn_w_in"][1], "reduce_small")

    upd["gdn_w_out"] = _adamw_slots(gdn_w_out, landed(mix_handles, 0, even, tot, "out"),
                                    m_gdn_w_out, v_gdn_w_out, "adamw_gdn_w_out")
    upd["gdn_w_in"] = _adamw_slots(gdn_w_in, landed(mix_handles, 1, even, upd["gdn_w_out"][1], "in"), m_gdn_w_in,
                                   v_gdn_w_in, "adamw_gdn_w_in")

    def update(name, w, g, m, v):
        shape = w.shape
        c = shape[-1]
        res = _adamw(w.reshape(-1, c), g.reshape(-1, c), m.reshape(-1, c), v.reshape(-1, c), "adamw_" + name)
        return [g.reshape(shape)] + [o.reshape(shape) for o in res]

    r0 = 0
    g_nmix = tot[r0:r0 + DEPTH]; r0 += DEPTH
    g_nmlp = tot[r0:r0 + DEPTH]; r0 += DEPTH
    g_nf = tot[r0]; r0 += 1
    loss = tot[r0, 0]; r0 += 1
    g_lb = _lb_bwd(hgrn_lb_logits, tot[r0:r0 + DEPTH], "lb_bwd"); r0 += DEPTH
    g_gnorm_full = tot[r0:r0 + n_hgrn]; r0 += n_hgrn
    t_row = tot[r0]; r0 += 1
    g_conv_full = tot[r0:r0 + n_gdn * CONV_K * 3].reshape(n_gdn, CONV_K, 3 * d)
    g_onorm = t_row[0:n_gdn * HEAD_DIM].reshape(n_gdn, HEAD_DIM)
    o1 = n_gdn * HEAD_DIM
    g_alog = t_row[o1:o1 + n_gdn * N_HEADS].reshape(n_gdn, N_HEADS)
    g_dtb = t_row[o1 + n_gdn * N_HEADS:o1 + 2 * n_gdn * N_HEADS].reshape(n_gdn, N_HEADS)
    c_gn, c_cv = hgrn_gnorm.shape[1], gdn_conv.shape[2]
    g_gnorm = lax.dynamic_slice_in_dim(g_gnorm_full, me_i * c_gn, c_gn, axis=1)
    g_conv = lax.dynamic_slice_in_dim(g_conv_full, me_i * c_cv, c_cv, axis=2)

    upd["gdn_conv"] = update("gdn_conv", gdn_conv, g_conv, m_gdn_conv, v_gdn_conv)
    upd["gdn_a_log"] = update("gdn_a_log", gdn_a_log, g_alog, m_gdn_a_log, v_gdn_a_log)
    upd["gdn_dt_bias"] = update("gdn_dt_bias", gdn_dt_bias, g_dtb, m_gdn_dt_bias, v_gdn_dt_bias)
    upd["gdn_onorm"] = update("gdn_onorm", gdn_onorm, g_onorm, m_gdn_onorm, v_gdn_onorm)
    upd["hgrn_lb_logits"] = update("hgrn_lb_logits", hgrn_lb_logits, g_lb, m_hgrn_lb_logits, v_hgrn_lb_logits)
    upd["hgrn_gnorm"] = update("hgrn_gnorm", hgrn_gnorm, g_gnorm, m_hgrn_gnorm, v_hgrn_gnorm)
    upd["norm_mix"] = update("norm_mix", norm_mix, g_nmix, m_norm_mix, v_norm_mix)
    upd["norm_mlp"] = update("norm_mlp", norm_mlp, g_nmlp, m_norm_mlp, v_norm_mlp)
    upd["norm_final"] = update("norm_final", norm_final, g_nf, m_norm_final, v_norm_final)

    order = ["gdn_w_in", "gdn_conv", "gdn_a_log", "gdn_dt_bias", "gdn_onorm", "gdn_w_out", "hgrn_w_in",
             "hgrn_lb_logits", "hgrn_gnorm", "hgrn_w_out", "norm_mix", "norm_mlp", "mlp_w_up", "mlp_w_down",
             "norm_final"]
    outs = [loss, grad_x]
    for k in range(4):
        outs += [upd[name][k] for name in order]
    return tuple(outs)
```

```python
import functools

import numpy as np
import jax
import jax.numpy as jnp
from jax import lax
from jax.experimental import pallas as pl
from jax.experimental.pallas import tpu as pltpu

F32 = jnp.float32
BF16 = jnp.bfloat16

D_MODEL = 1024
N_HEADS = 8
HEAD_DIM = 128
CHUNK = 64
SUB = 16
N_SUB = CHUNK // SUB
CONV_K = 4
HALO = 16
EPS = 1e-6
DEPTH = 4
N_DEV = 8
GDN_MAIN = 4 * D_MODEL
GDN_IN = GDN_MAIN + 2 * N_HEADS
AB_PAD = 128
LANE_BLOCK = 256
ROW_BLOCK = 16
BLOCK_UNROLL = 4

ADAM_LR = 0.001
ADAM_B1 = 0.9
ADAM_B2 = 0.999
ADAM_EPS = 1e-08
ADAM_WD = 0.01
ADAM_STEP = 10

VMEM_LIMIT = 56 * 1024 * 1024
MM_TILE = 1024
MM_VMEM_BUDGET = 40 * 1024 * 1024
MM_ROWS_TILE = 512
_DIMS = {
    "nn": (((1,), (0,)), ((), ())),
    "nt": (((1,), (1,)), ((), ())),
    "tn": (((0,), (0,)), ((), ())),
}


def _parts(x, n):
    if n == 1 and x.dtype == BF16:
        return [x]
    out = []
    r = x.astype(F32)
    for i in range(n):
        p = r.astype(BF16)
        out.append(p)
        if i + 1 < n:
            r = r - p.astype(F32)
    return out


def _dot_raw(a, b, mode, na, nb):
    ap, bp = _parts(a, na), _parts(b, nb)
    nmax = max(na, nb)
    pairs = [(i, j) for i in range(na) for j in range(nb) if i + j < nmax]
    ka = 0 if mode == "tn" else 1
    kb = 1 if mode == "nt" else 0
    xa = ap[0] if len(pairs) == 1 else jnp.concatenate([ap[i] for i, _ in pairs], axis=ka)
    xb = bp[0] if len(pairs) == 1 else jnp.concatenate([bp[j] for _, j in pairs], axis=kb)
    return lax.dot_general(xa, xb, _DIMS[mode], preferred_element_type=F32)


@functools.partial(jax.custom_vjp, nondiff_argnums=(2, 3, 4))
def _dot(a, b, mode, na, nb):
    return _dot_raw(a, b, mode, na, nb)


def _dot_fwd(a, b, mode, na, nb):
    return _dot_raw(a, b, mode, na, nb), (a, b)


def _dot_bwd(mode, na, nb, res, ct):
    a, b = res
    if mode == "nn":
        da = _dot_raw(ct, b, "nt", 1, 1)
        db = _dot_raw(a, ct, "tn", 1, 1)
    elif mode == "nt":
        da = _dot_raw(ct, b, "nn", 1, 1)
        db = _dot_raw(ct, a, "tn", 1, 1)
    else:
        da = _dot_raw(b, ct, "nt", 1, 1)
        db = _dot_raw(a, ct, "nn", 1, 1)
    return da.astype(a.dtype), db.astype(b.dtype)


_dot.defvjp(_dot_fwd, _dot_bwd)


N_EXACT = 3


@jax.custom_vjp
def _dot01(x, m_wide, m):
    return lax.dot_general(m_wide, jnp.concatenate(_parts(x, N_EXACT), axis=0), _DIMS["nn"], preferred_element_type=F32)


def _dot01_fwd(x, m_wide, m):
    return _dot01(x, m_wide, m), (m_wide, m)


def _dot01_bwd(res, ct):
    m_wide, m = res
    dx = lax.dot_general(m, ct.astype(BF16), _DIMS["tn"], preferred_element_type=F32)
    return dx, jnp.zeros_like(m_wide), jnp.zeros_like(m)


_dot01.defvjp(_dot01_fwd, _dot01_bwd)


def _thrice(m):
    return jnp.concatenate([m] * N_EXACT, axis=1).astype(BF16), m.astype(BF16)


def _iota2(shape, dim):
    return lax.broadcasted_iota(jnp.int32, shape, dim)


def _tril_f32(n):
    return (_iota2((n, n), 0) >= _iota2((n, n), 1)).astype(F32)


def _cumsum_rows(g):
    return _dot(_tril_f32(g.shape[0]), g, "nn", 1, 3)


def _below_block(n, b):
    ri, ci = _iota2((n, n), 0) // b, _iota2((n, n), 1) // b
    return (ri == ci + 1) & (ri % 2 == 1)


def _half_inverses(L):
    n = L.shape[0]
    eye = (_iota2((n, n), 0) == _iota2((n, n), 1)).astype(F32)
    d = eye - jnp.where(_below_block(n, 1), L, 0.0)
    b = 2
    while 2 * b < n:
        e = jnp.where(_below_block(n, b), L, 0.0)
        d = d - _dot_raw(d, _dot_raw(e, d, "nn", 2, 2), "nn", 2, 2)
        b *= 2
    return d, jnp.where(_below_block(n, b), L, 0.0)


def _solve_with(d, e, rhs):
    y = _dot_raw(d, rhs, "nn", 2, 2)
    return y - _dot_raw(d, _dot_raw(e, y, "nn", 2, 2), "nn", 2, 2)


@jax.custom_vjp
def _solve_unit_lower(L, rhs, d):
    n = L.shape[0]
    return _solve_with(d, jnp.where(_below_block(n, n // 2), L, 0.0), rhs)


def _solve_fwd(L, rhs, d):
    n = L.shape[0]
    e = jnp.where(_below_block(n, n // 2), L, 0.0)
    sol = _solve_with(d, e, rhs)
    return sol, (d, e, sol)


def _solve_bwd(res, ct):
    d, e, sol = res
    y = _dot_raw(d, ct - _dot_raw(e, _dot_raw(d, ct, "tn", 2, 2), "tn", 2, 2), "tn", 2, 2)
    return -_dot_raw(y, sol, "nt", 2, 2), y, jnp.zeros_like(d)


_solve_unit_lower.defvjp(_solve_fwd, _solve_bwd)


def _softplus(x):
    return jnp.maximum(x, 0.0) + jnp.log1p(jnp.exp(-jnp.abs(x)))


def _rms(x, w):
    return x * lax.rsqrt(jnp.mean(x * x, axis=-1, keepdims=True) + EPS) * w


HG_LEVELS = (32, 16, 8, 4, 2, 1)


def _hg_level_sums():
    i = np.arange(CHUNK)[:, None]
    m = np.arange(CHUNK)[None, :]
    to_row = [(m <= i) & (m // b == i // b) for b in HG_LEVELS]
    to_col = [(m > i) & (m // b == i // b) for b in HG_LEVELS if b > 1]
    return _thrice(jnp.asarray(np.concatenate(to_row + to_col + [m <= i]), F32))


def _hg_level_masks():
    i = np.arange(CHUNK)[:, None]
    j = np.arange(CHUNK)[None, :]
    return jnp.asarray(np.stack([(i // b == j // b + 1) & ((i // b) % 2 == 1) for b in HG_LEVELS]), F32)


def _hg_pre(qraw, f, lb, sums):
    g = jnp.log(lb + (1.0 - lb) * jax.nn.sigmoid(f))
    k = (1.0 - lb) * jax.nn.sigmoid(-f)
    q = jax.nn.silu(qraw) * (HEAD_DIM ** -0.5)
    return q, k, _dot01(g, *sums)


def _hg_head(st, q, k, v, e, masks):
    nl = len(HG_LEVELS)
    eye = (_iota2((CHUNK, CHUNK), 0) == _iota2((CHUNK, CHUNK), 1)).astype(F32)
    a = eye * jnp.sum(q * k, axis=-1, keepdims=True)
    for l, b in enumerate(HG_LEVELS):
        rows = q * jnp.exp(e[l * CHUNK:(l + 1) * CHUNK])
        cols = k * jnp.exp(e[(nl + l) * CHUNK:(nl + l + 1) * CHUNK]) if b > 1 else k
        a = a + masks[l] * _dot(rows, cols, "nt", 1, 1)
    gc = e[(2 * nl - 1) * CHUNK:2 * nl * CHUNK]
    o = _dot(a, v, "nn", 1, 1) + _dot(q * jnp.exp(gc), st, "nt", 1, 1)
    g_last = gc[CHUNK - 1:CHUNK]
    st_new = st * jnp.exp(g_last) + _dot(v, k * jnp.exp(g_last - gc), "tn", 1, 1)
    return o, st_new


_HG_HEADS = jax.vmap(_hg_head, in_axes=(0, 0, 0, 0, 0, None))


def _hg_post(o, gate, gw):
    return _rms(o, gw) * jax.nn.silu(gate)


def _gd_conv(xp, cw):
    off = HALO - (CONV_K - 1)
    y = cw[0:1] * xp[off:off + CHUNK]
    for kk in range(1, CONV_K):
        y = y + cw[kk:kk + 1] * xp[off + kk:off + kk + CHUNK]
    return y


def _gd_conv_bwd(xp, cw, y, dc):
    off = HALO - (CONV_K - 1)
    sig = jax.nn.sigmoid(y)
    dy = dc * (sig * (1.0 + y * (1.0 - sig)))
    dxp, dcw = None, []
    for kk in range(CONV_K):
        moved = jnp.pad(dy, ((off + kk, HALO - off - kk), (0, 0)))
        term = cw[kk:kk + 1] * moved
        dxp = term if dxp is None else dxp + term
        dcw.append(jnp.sum(xp * moved, axis=0, keepdims=True))
    return dxp, jnp.concatenate(dcw, axis=0)


def _gd_gates(a, b, alog, dtb):
    beta = jax.nn.sigmoid(b)
    g = -jnp.exp(alog) * _softplus(a + dtb)
    expand = (_iota2((N_HEADS, D_MODEL), 1) // HEAD_DIM == _iota2((N_HEADS, D_MODEL), 0)).astype(F32)
    g_x = _dot(g, expand, "nn", 3, 1)
    after = (_iota2((CHUNK, D_MODEL), 0) > _iota2((CHUNK, D_MODEL), 1) % HEAD_DIM).astype(F32)
    sums = _dot01(jnp.concatenate([g_x, g_x * after], axis=1), *_thrice(_tril_f32(CHUNK)))
    return _dot(beta, expand, "nn", 3, 1), sums


def _gd_head(st, q, k, v, beta, gc, diff, gate, onw, dinv=None):
    q = q * lax.rsqrt(jnp.sum(q * q, axis=-1, keepdims=True) + EPS) * (HEAD_DIM ** -0.5)
    k = k * lax.rsqrt(jnp.sum(k * k, axis=-1, keepdims=True) + EPS)
    ri = _iota2((CHUNK, CHUNK), 0)
    ci = _iota2((CHUNK, CHUNK), 1)
    decay = jnp.exp(jnp.where(ri >= ci, diff[:, 0:CHUNK], -jnp.inf))
    kb = k * beta
    egc = jnp.exp(gc)
    L = jnp.where(ri > ci, _dot(kb, k, "nt", 1, 1) * decay, 0.0)
    made = dinv is None
    if made:
        dinv = _half_inverses(L)[0]
    sol = _solve_unit_lower(L, jnp.concatenate([v * beta, kb * egc], axis=1), dinv)
    u = sol[:, 0:HEAD_DIM]
    w = sol[:, HEAD_DIM:2 * HEAD_DIM]
    a_qk = jnp.where(ri >= ci, _dot(q, k, "nt", 1, 1) * decay, 0.0)
    g_last = gc[CHUNK - 1:CHUNK]
    v_new = u - _dot(w, st, "nt", 1, 1)
    o = _dot(q * egc, st, "nt", 1, 1) + _dot(a_qk, v_new, "nn", 1, 1)
    st_new = st * jnp.exp(g_last) + _dot(v_new, k * jnp.exp(g_last - gc), "tn", 1, 1)
    out = (_rms(o, onw) * jax.nn.silu(gate), st_new)
    return out + (dinv,) if made else out


def _params(*sem):
    return pltpu.CompilerParams(dimension_semantics=sem, vmem_limit_bytes=VMEM_LIMIT)


def _tile(n, pref):
    t = min(n, pref)
    assert n % t == 0, (n, pref)
    return t


def _mm_tiles(m, n, k, a_size, b_size, tile_sizes):
    tm, tn, tk = _tile(m, MM_TILE), _tile(n, MM_TILE), k

    def need(tm, tn, tk):
        acc = 4 * tm * tn * (2 if tk < k else 1)
        return 2 * (tm * tk * a_size + tk * tn * b_size + tm * tn * sum(tile_sizes)) + acc

    while need(tm, tn, tk) > MM_VMEM_BUDGET:
        if tk > 2048 or (tk > 512 and tm <= 512):
            tk //= 2
        else:
            tm //= 2
    return tm, tn, tk


def _mm(a, b, mode, out_dtypes, name, epilogue=None, extras=(), after=None):
    if mode == "nn":
        (m, k), (k2, n) = a.shape, b.shape
    elif mode == "nt":
        (m, k), (n, k2) = a.shape, b.shape
    else:
        (k, m), (k2, n) = a.shape, b.shape
    assert k == k2, (a.shape, b.shape, mode)
    tm, tn, tk = _mm_tiles(m, n, k, a.dtype.itemsize, b.dtype.itemsize,
                           [e.dtype.itemsize for e in extras] + [jnp.dtype(dt).itemsize for dt in out_dtypes])
    nk = k // tk
    ne, no, nafter = len(extras), len(out_dtypes), int(after is not None)
    if epilogue is None:
        epilogue = lambda acc: (acc,)

    def body(*refs):
        a_ref, b_ref = refs[0], refs[1]
        ex = refs[2:2 + ne]
        outs = refs[2 + ne + nafter:2 + ne + nafter + no]
        part = lax.dot_general(a_ref[...].astype(BF16), b_ref[...].astype(BF16), _DIMS[mode],
                               preferred_element_type=F32)

        def finish(acc):
            for o_ref, val in zip(outs, epilogue(acc, *[e[...] for e in ex])):
                o_ref[...] = val.astype(o_ref.dtype)

        if nk == 1:
            finish(part)
        else:
            acc_ref = refs[-1]
            kk = pl.program_id(2)

            @pl.when(kk == 0)
            def _():
                acc_ref[...] = part

            @pl.when(kk > 0)
            def _():
                acc_ref[...] += part

            @pl.when(kk == nk - 1)
            def _():
                finish(acc_ref[...])

    if mode == "tn":
        a_spec = pl.BlockSpec((tk, tm), lambda i, j, kk: (kk, i))
    else:
        a_spec = pl.BlockSpec((tm, tk), lambda i, j, kk: (i, kk))
    if mode == "nt":
        b_spec = pl.BlockSpec((tn, tk), lambda i, j, kk: (j, kk))
    else:
        b_spec = pl.BlockSpec((tk, tn), lambda i, j, kk: (kk, j))
    o_spec = pl.BlockSpec((tm, tn), lambda i, j, kk: (i, j))
    res = pl.pallas_call(
        body,
        name=name,
        grid=(m // tm, n // tn, nk),
        in_specs=[a_spec, b_spec] + [o_spec] * ne + [pl.BlockSpec(memory_space=pl.ANY)] * nafter,
        out_specs=[o_spec] * no,
        out_shape=[jax.ShapeDtypeStruct((m, n), dt) for dt in out_dtypes],
        scratch_shapes=[pltpu.VMEM((tm, tn), F32)] if nk > 1 else [],
        compiler_params=_params("parallel", "parallel", "arbitrary"),
    )(a, b, *extras, *([after] if nafter else []))
    return res[0] if no == 1 else res


def _mm_rows(a, b, mode, out_dtypes, name, epilogue, extras=(), vectors=(), n_sums=0, after=None):
    assert mode in ("nn", "nt")
    (m, k), n = a.shape, (b.shape[1] if mode == "nn" else b.shape[0])
    tm = _tile(m, MM_ROWS_TILE)
    mt = m // tm
    ne, no, nafter = len(extras) + len(vectors), len(out_dtypes), int(after is not None)

    def body(*refs):
        a_ref, b_ref = refs[0], refs[1]
        ex = refs[2:2 + ne]
        outs = refs[2 + ne + nafter:2 + ne + nafter + no]
        sums = refs[2 + ne + nafter + no:2 + ne + nafter + no + n_sums]
        acc_ref = refs[-1]
        i = pl.program_id(0)

        @pl.when(i == 0)
        def _():
            acc_ref[1] = jnp.zeros((tm, n), F32)

        vals = epilogue(acc_ref[1 - i % 2], *[e[...] for e in ex])
        acc_ref[i % 2] = lax.dot_general(a_ref[...].astype(BF16), b_ref[...].astype(BF16), _DIMS[mode],
                                         preferred_element_type=F32)
        for o_ref, val in zip(outs, vals[:no]):
            o_ref[...] = val.astype(o_ref.dtype)
        for s_ref, val in zip(sums, vals[no:]):
            @pl.when(i <= 1)
            def _(s_ref=s_ref, val=val):
                s_ref[...] = val

            @pl.when(i > 1)
            def _(s_ref=s_ref, val=val):
                s_ref[...] += val

    ahead = lambda i: (jnp.minimum(i, mt - 1), 0)
    behind = lambda i: (jnp.maximum(i - 1, 0), 0)
    fixed = lambda i: (0, 0)
    row = pl.BlockSpec((tm, n), behind)
    vec = pl.BlockSpec((1, n), fixed)
    res = pl.pallas_call(
        body, name=name, grid=(mt + 1,),
        in_specs=([pl.BlockSpec((tm, k), ahead), pl.BlockSpec(b.shape, fixed)] + [row] * len(extras)
                  + [vec] * len(vectors) + [pl.BlockSpec(memory_space=pl.ANY)] * nafter),
        out_specs=[row] * no + [vec] * n_sums,
        out_shape=[jax.ShapeDtypeStruct((m, n), dt) for dt in out_dtypes] + [jax.ShapeDtypeStruct((1, n), F32)] * n_sums,
        scratch_shapes=[pltpu.VMEM((2, tm, n), F32)],
        compiler_params=_params("arbitrary"),
    )(a, b, *extras, *vectors, *([after] if nafter else []))
    return res[0] if no + n_sums == 1 else res


def _ep_residual_norm(acc, res, w):
    h = res + acc
    return h, _rms(h, w)


def _ep_norm_bwd(acc, x, dres, w):
    r = lax.rsqrt(jnp.mean(x * x, axis=-1, keepdims=True) + EPS)
    g = acc * w
    dx = dres + (r * g - x * (r * r * r * jnp.mean(g * x, axis=-1, keepdims=True)))
    return dx, dx, jnp.sum(acc * (x * r), axis=0, keepdims=True)


def _rms_fwd(x, w, name, tm=512):
    n, d = x.shape
    tm = _tile(n, tm)

    def body(x_ref, w_ref, y_ref):
        y_ref[...] = _rms(x_ref[...], w_ref[...]).astype(y_ref.dtype)

    return pl.pallas_call(
        body, name=name, grid=(n // tm,),
        in_specs=[pl.BlockSpec((tm, d), lambda i: (i, 0)), pl.BlockSpec((1, d), lambda i: (0, 0))],
        out_specs=pl.BlockSpec((tm, d), lambda i: (i, 0)),
        out_shape=jax.ShapeDtypeStruct((n, d), BF16),
        compiler_params=_params("arbitrary"),
    )(x, w)


def _loss_head(h, w, target, name, tm=512):
    n, d = h.shape
    tm = _tile(n, tm)

    def body(h_ref, w_ref, t_ref, dh_ref, dhb_ref, dw_ref, sq_ref):
        y, vjp = jax.vjp(_rms, h_ref[...], w_ref[...])
        err = y - t_ref[...]
        dh, dw = vjp(err * (1.0 / d))
        dh_ref[...] = dh
        dhb_ref[...] = dh.astype(dhb_ref.dtype)
        sq = jnp.sum(err * err, axis=0, keepdims=True)

        @pl.when(pl.program_id(0) == 0)
        def _():
            dw_ref[...] = dw
            sq_ref[...] = sq

        @pl.when(pl.program_id(0) > 0)
        def _():
            dw_ref[...] += dw
            sq_ref[...] += sq

        @pl.when(pl.program_id(0) == n // tm - 1)
        def _():
            total = jnp.sum(sq_ref[...], axis=1, keepdims=True) * (0.5 / d)
            sq_ref[...] = jnp.broadcast_to(total, sq_ref.shape)

    row = pl.BlockSpec((tm, d), lambda i: (i, 0))
    vec = pl.BlockSpec((1, d), lambda i: (0, 0))
    return pl.pallas_call(
        body, name=name, grid=(n // tm,),
        in_specs=[row, vec, row],
        out_specs=[row, row, vec, vec],
        out_shape=[jax.ShapeDtypeStruct((n, d), F32), jax.ShapeDtypeStruct((n, d), BF16),
                   jax.ShapeDtypeStruct((1, d), F32), jax.ShapeDtypeStruct((1, d), F32)],
        compiler_params=_params("arbitrary"),
    )(h, w, target)


def _lower_bounds(logits):
    sm = jax.nn.softmax(logits, axis=0)
    rows = [sm[0:1] * 0.0]
    for r in range(1, DEPTH):
        rows.append(rows[-1] + sm[r:r + 1])
    return jnp.concatenate(rows, axis=0)


def _lb_fwd(logits, name):
    def body(l_ref, o_ref):
        o_ref[...] = _lower_bounds(l_ref[...])

    return pl.pallas_call(body, name=name, out_shape=jax.ShapeDtypeStruct(logits.shape, F32))(logits)


def _lb_bwd(logits, dlb, name):
    def body(l_ref, d_ref, o_ref):
        _, vjp = jax.vjp(_lower_bounds, l_ref[...])
        (o_ref[...],) = vjp(d_ref[...])

    return pl.pallas_call(body, name=name, out_shape=jax.ShapeDtypeStruct(logits.shape, F32))(logits, dlb)


def _head_slice(h):
    return pl.ds(h * HEAD_DIM, HEAD_DIM)


_GD_HEADS = jax.vmap(_gd_head, in_axes=(0, 0, 0, 0, 0, 0, 0, 0, None))
_GD_HEADS_AGAIN = jax.vmap(_gd_head, in_axes=(0, 0, 0, 0, 0, 0, 0, 0, None, 0))


def _lane_blocks(width, block_body):
    def trip(j, carry):
        block_body(lambda base=0: pl.ds(pl.multiple_of(j * LANE_BLOCK + base, LANE_BLOCK), LANE_BLOCK))
        return carry

    lax.fori_loop(0, width // LANE_BLOCK, trip, 0, unroll=BLOCK_UNROLL)


def _row_blocks(rows, block_body):
    def trip(j, carry):
        block_body(pl.ds(pl.multiple_of(j * ROW_BLOCK, ROW_BLOCK), ROW_BLOCK))
        return carry

    lax.fori_loop(0, rows // ROW_BLOCK, trip, 0, unroll=BLOCK_UNROLL)


def _hg_pre_block(p_ref, lb_ref, sums_refs, q_sc, k_sc, v_sc, e_sc, at):
    sl = at()
    q_sc[:, sl], k_sc[:, sl], e_sc[:, sl] = _hg_pre(
        p_ref[:, sl].astype(F32), p_ref[:, at(D_MODEL)].astype(F32), lb_ref[:, sl], [r[...] for r in sums_refs])
    v_sc[:, sl] = p_ref[:, at(2 * D_MODEL)].astype(F32)


def _gd_xp(halo_ref, p_ref, sl, first_chunk):
    halo = jnp.where(first_chunk, 0.0, halo_ref[:, sl].astype(F32))
    return jnp.concatenate([halo, p_ref[:, sl].astype(F32)], axis=0)


def _stack_all(ref, first=0):
    return jnp.stack([ref[s, :, _head_slice(h + first)] for s in range(ref.shape[0]) for h in range(N_HEADS)])


def _unstack_all(ref, val, first=0):
    for s in range(ref.shape[0]):
        for h in range(N_HEADS):
            ref[s, :, _head_slice(h + first)] = val[s * N_HEADS + h].astype(ref.dtype)


def _gdn_fwd_all(projm, projab, cw, alog, dtb, onw, seqs, name):
    n = projm.shape[0]
    t = n // seqs
    nc = t // CHUNK
    d = D_MODEL
    per_halo = CHUNK // HALO
    nh = seqs * N_HEADS

    def body(p_ref, halo_ref, ab_ref, cw_ref, alog_ref, dtb_ref, onw_ref, o2_ref, st_all_ref, y_ref, dinv_ref,
             st_sc, c_sc, beta_sc, g_sc):
        first_chunk = pl.program_id(0) == 0

        @pl.when(first_chunk)
        def _():
            st_sc[...] = jnp.zeros_like(st_sc)

        for s in range(seqs):
            def conv(at, s=s):
                sl = at()
                y = _gd_conv(_gd_xp(halo_ref.at[s], p_ref.at[s], sl, first_chunk), cw_ref[:, sl])
                y_ref[s, :, sl] = y
                c_sc[s, :, sl] = jax.nn.silu(y)

            _lane_blocks(3 * d, conv)
            beta_sc[s], g_sc[s] = _gd_gates(ab_ref[s, :, 0:N_HEADS], ab_ref[s, :, N_HEADS:2 * N_HEADS],
                                            alog_ref[...], dtb_ref[...])
        st_all_ref[0] = st_sc[...]
        o2, st_sc[...], dinv_ref[0] = _GD_HEADS(
            st_sc[...], _stack_all(c_sc), _stack_all(c_sc, N_HEADS), _stack_all(c_sc, 2 * N_HEADS), _stack_all(beta_sc),
            _stack_all(g_sc), _stack_all(g_sc, N_HEADS), _stack_all(p_ref, 3 * N_HEADS).astype(F32), onw_ref[...])
        _unstack_all(o2_ref, o2)

    rows = lambda c: (0, c, 0)
    const = lambda c: (0, 0)
    per_chunk = lambda c: (c, 0, 0, 0)
    p3 = projm.reshape(seqs, t, 4 * d)
    o2, st_all, conv_y, dinv_all = pl.pallas_call(
        body, name=name, grid=(nc,),
        in_specs=[pl.BlockSpec((seqs, CHUNK, 4 * d), rows),
                  pl.BlockSpec((seqs, HALO, 3 * d), lambda c: (0, jnp.maximum(c * per_halo - 1, 0), 0)),
                  pl.BlockSpec((seqs, CHUNK, AB_PAD), rows),
                  pl.BlockSpec((CONV_K, 3 * d), const), pl.BlockSpec((1, N_HEADS), const),
                  pl.BlockSpec((1, N_HEADS), const), pl.BlockSpec((1, HEAD_DIM), const)],
        out_specs=[pl.BlockSpec((seqs, CHUNK, d), rows), pl.BlockSpec((1, nh, HEAD_DIM, HEAD_DIM), per_chunk),
                   pl.BlockSpec((seqs, CHUNK, 3 * d), rows), pl.BlockSpec((1, nh, CHUNK, CHUNK), per_chunk)],
        out_shape=[jax.ShapeDtypeStruct((seqs, t, d), BF16), jax.ShapeDtypeStruct((nc, nh, HEAD_DIM, HEAD_DIM), F32),
                   jax.ShapeDtypeStruct((seqs, t, 3 * d), F32), jax.ShapeDtypeStruct((nc, nh, CHUNK, CHUNK), F32)],
        scratch_shapes=[pltpu.VMEM((nh, HEAD_DIM, HEAD_DIM), F32), pltpu.VMEM((seqs, CHUNK, 3 * d), F32),
                        pltpu.VMEM((seqs, CHUNK, d), F32), pltpu.VMEM((seqs, CHUNK, 2 * d), F32)],
        compiler_params=_params("arbitrary"),
    )(p3, p3, projab.reshape(seqs, t, AB_PAD), cw, alog, dtb, onw)
    return o2.reshape(n, d), st_all, conv_y, dinv_all


def _gdn_bwd_all(projm, projab, conv_y, cw, alog, dtb, onw, st_all, dinv_all, do2, seqs, name):
    n = projm.shape[0]
    t = n // seqs
    nc = t // CHUNK
    d = D_MODEL
    per_halo = CHUNK // HALO
    nh = seqs * N_HEADS

    def body(p_ref, halo_ref, ab_ref, y_ref, cw_ref, alog_ref, dtb_ref, onw_ref, st_all_ref, dinv_ref, do2_ref,
             dp_ref, dab_ref, dcw_ref, dalog_ref, ddtb_ref, donw_ref,
             dst_sc, dhalo_sc, c_sc, beta_sc, g_sc, dc_sc, dbeta_sc, dg_sc):
        first = pl.program_id(0) == 0
        first_chunk = pl.program_id(0) == nc - 1

        @pl.when(first)
        def _():
            dst_sc[...] = jnp.zeros_like(dst_sc)
            dhalo_sc[...] = jnp.zeros_like(dhalo_sc)

        gates_vjps = []
        for s in range(seqs):
            def act(at, s=s):
                c_sc[s, :, at()] = jax.nn.silu(y_ref[s, :, at()])

            _lane_blocks(3 * d, act)
            (beta_sc[s], g_sc[s]), gates_vjp = jax.vjp(
                _gd_gates, ab_ref[s, :, 0:N_HEADS], ab_ref[s, :, N_HEADS:2 * N_HEADS], alog_ref[...], dtb_ref[...])
            gates_vjps.append(gates_vjp)

        dinv = dinv_ref[0]
        _, vjp = jax.vjp(
            lambda *a: _GD_HEADS_AGAIN(*a, dinv), st_all_ref[0], _stack_all(c_sc), _stack_all(c_sc, N_HEADS),
            _stack_all(c_sc, 2 * N_HEADS), _stack_all(beta_sc), _stack_all(g_sc), _stack_all(g_sc, N_HEADS),
            _stack_all(p_ref, 3 * N_HEADS).astype(F32), onw_ref[...])
        dst_sc[...], dq, dk, dv, dbeta, dg, ddiff, dgate, donw = vjp((_stack_all(do2_ref).astype(F32), dst_sc[...]))
        _unstack_all(dc_sc, dq)
        _unstack_all(dc_sc, dk, N_HEADS)
        _unstack_all(dc_sc, dv, 2 * N_HEADS)
        _unstack_all(dbeta_sc, dbeta)
        _unstack_all(dg_sc, dg)
        _unstack_all(dg_sc, ddiff, N_HEADS)
        _unstack_all(dp_ref, dgate, 3 * N_HEADS)

        dalog, ddtb = None, None
        for s in range(seqs):
            def conv_bwd(at, s=s):
                sl = at()
                dxp, dcw = _gd_conv_bwd(_gd_xp(halo_ref.at[s], p_ref.at[s], sl, first_chunk), cw_ref[:, sl],
                                        y_ref[s, :, sl], dc_sc[s, :, sl])
                dqkv = jnp.concatenate([dxp[HALO:CHUNK], dxp[CHUNK:HALO + CHUNK] + dhalo_sc[s, :, sl]], axis=0)
                dp_ref[s, :, sl] = dqkv.astype(dp_ref.dtype)
                dhalo_sc[s, :, sl] = dxp[0:HALO]

                if s > 0:
                    dcw_ref[:, sl] += dcw
                    return

                @pl.when(first)
                def _():
                    dcw_ref[:, sl] = dcw

                @pl.when(jnp.logical_not(first))
                def _():
                    dcw_ref[:, sl] += dcw

            _lane_blocks(3 * d, conv_bwd)
            da, db, dalog_s, ddtb_s = gates_vjps[s]((dbeta_sc[s], dg_sc[s]))
            dab_ref[s] = jnp.concatenate(
                [da, db, jnp.zeros((CHUNK, AB_PAD - 2 * N_HEADS), F32)], axis=1).astype(dab_ref.dtype)
            dalog = dalog_s if dalog is None else dalog + dalog_s
            ddtb = ddtb_s if ddtb is None else ddtb + ddtb_s

        @pl.when(first)
        def _():
            dalog_ref[...] = dalog
            ddtb_ref[...] = ddtb
            donw_ref[...] = donw

        @pl.when(jnp.logical_not(first))
        def _():
            dalog_ref[...] += dalog
            ddtb_ref[...] += ddtb
            donw_ref[...] += donw

    back = lambda c: nc - 1 - c
    rows = lambda c: (0, back(c), 0)
    const = lambda c: (0, 0)
    per_chunk = lambda c: (back(c), 0, 0, 0)
    small = [pl.BlockSpec((CONV_K, 3 * d), const), pl.BlockSpec((1, N_HEADS), const),
             pl.BlockSpec((1, N_HEADS), const), pl.BlockSpec((1, HEAD_DIM), const)]
    p3 = projm.reshape(seqs, t, 4 * d)
    dp, dab, dcw, dalog, ddtb, donw = pl.pallas_call(
        body, name=name, grid=(nc,),
        in_specs=[pl.BlockSpec((seqs, CHUNK, 4 * d), rows),
                  pl.BlockSpec((seqs, HALO, 3 * d), lambda c: (0, jnp.maximum(back(c) * per_halo - 1, 0), 0)),
                  pl.BlockSpec((seqs, CHUNK, AB_PAD), rows), pl.BlockSpec((seqs, CHUNK, 3 * d), rows)] + small + [
                  pl.BlockSpec((1, nh, HEAD_DIM, HEAD_DIM), per_chunk), pl.BlockSpec((1, nh, CHUNK, CHUNK), per_chunk),
                  pl.BlockSpec((seqs, CHUNK, d), rows)],
        out_specs=[pl.BlockSpec((seqs, CHUNK, 4 * d), rows), pl.BlockSpec((seqs, CHUNK, AB_PAD), rows)] + small,
        out_shape=[jax.ShapeDtypeStruct((seqs, t, 4 * d), BF16), jax.ShapeDtypeStruct((seqs, t, AB_PAD), BF16),
                   jax.ShapeDtypeStruct((CONV_K, 3 * d), F32), jax.ShapeDtypeStruct((1, N_HEADS), F32),
                   jax.ShapeDtypeStruct((1, N_HEADS), F32), jax.ShapeDtypeStruct((1, HEAD_DIM), F32)],
        scratch_shapes=[pltpu.VMEM((nh, HEAD_DIM, HEAD_DIM), F32), pltpu.VMEM((seqs, HALO, 3 * d), F32),
                        pltpu.VMEM((seqs, CHUNK, 3 * d), F32), pltpu.VMEM((seqs, CHUNK, d), F32),
                        pltpu.VMEM((seqs, CHUNK, 2 * d), F32), pltpu.VMEM((seqs, CHUNK, 3 * d), F32),
                        pltpu.VMEM((seqs, CHUNK, d), F32), pltpu.VMEM((seqs, CHUNK, 2 * d), F32)],
        compiler_params=_params("arbitrary"),
    )(p3, p3, projab.reshape(seqs, t, AB_PAD), conv_y, cw, alog, dtb, onw, st_all, dinv_all,
      do2.reshape(seqs, t, d))
    return dp.reshape(n, 4 * d), dab.reshape(n, AB_PAD), dcw, dalog, ddtb, donw


def _hgrn_fwd_all(proj, lb, gw, seqs, name):
    n = proj.shape[0]
    t = n // seqs
    nc = t // CHUNK
    d = D_MODEL
    nh = seqs * N_HEADS
    sums, masks = _hg_level_sums(), _hg_level_masks()

    def body(p_ref, lb_ref, gw_ref, sums_wide_ref, sums_once_ref, masks_ref, o2_ref, o_ref, st_all_ref,
             st_sc, q_sc, k_sc, v_sc, e_sc):
        @pl.when(pl.program_id(0) == 0)
        def _():
            st_sc[...] = jnp.zeros_like(st_sc)

        sums_refs = (sums_wide_ref, sums_once_ref)
        for s in range(seqs):
            _lane_blocks(d, functools.partial(_hg_pre_block, p_ref.at[s], lb_ref, sums_refs, q_sc.at[s], k_sc.at[s],
                                              v_sc.at[s], e_sc.at[s]))
        st_all_ref[0] = st_sc[...]
        for s in range(seqs):
            one, mine = pl.ds(s, 1), pl.ds(s * N_HEADS, N_HEADS)
            o, st_sc[mine] = _HG_HEADS(st_sc[mine], *[_stack_all(r.at[one]) for r in (q_sc, k_sc, v_sc, e_sc)],
                                       masks_ref[...])
            _unstack_all(o_ref.at[one], o)

            def post(rows, s=s):
                gate = p_ref[s, rows, 3 * d:4 * d].astype(F32)
                o2_ref[s, rows, :] = _hg_post(o_ref[s, rows, :], gate, gw_ref[...]).astype(o2_ref.dtype)

            _row_blocks(CHUNK, post)

    rows = lambda c: (0, c, 0)
    vec = pl.BlockSpec((1, d), lambda c: (0, 0))
    act = pl.BlockSpec((seqs, CHUNK, d), rows)
    o2, o, st_all = pl.pallas_call(
        body, name=name, grid=(nc,),
        in_specs=[pl.BlockSpec((seqs, CHUNK, 4 * d), rows), vec, vec]
        + [pl.BlockSpec(m.shape, lambda c: (0, 0)) for m in sums] + [pl.BlockSpec(masks.shape, lambda c: (0, 0, 0))],
        out_specs=[act, act, pl.BlockSpec((1, nh, HEAD_DIM, HEAD_DIM), lambda c: (c, 0, 0, 0))],
        out_shape=[jax.ShapeDtypeStruct((seqs, t, d), BF16), jax.ShapeDtypeStruct((seqs, t, d), F32),
                   jax.ShapeDtypeStruct((nc, nh, HEAD_DIM, HEAD_DIM), F32)],
        scratch_shapes=[pltpu.VMEM((nh, HEAD_DIM, HEAD_DIM), F32)] + [pltpu.VMEM((seqs, CHUNK, d), F32)] * 3
        + [pltpu.VMEM((seqs, sums[0].shape[0], d), F32)],
        compiler_params=_params("arbitrary"),
    )(proj.reshape(seqs, t, 4 * d), lb, gw, *sums, masks)
    return o2.reshape(n, d), o, st_all


def _hgrn_bwd_all(proj, lb, gw, st_all, o, do2, seqs, name):
    n = proj.shape[0]
    t = n // seqs
    nc = t // CHUNK
    d = D_MODEL
    nh = seqs * N_HEADS
    sums, masks = _hg_level_sums(), _hg_level_masks()

    def body(p_ref, lb_ref, gw_ref, sums_wide_ref, sums_once_ref, masks_ref, st_all_ref, o_ref, do2_ref,
             dp_ref, dlb_ref, dgw_ref,
             dst_sc, q_sc, k_sc, v_sc, e_sc, do_sc, dq_sc, dk_sc, dv_sc, de_sc, dgw_sc):
        first = pl.program_id(0) == 0

        @pl.when(first)
        def _():
            dst_sc[...] = jnp.zeros_like(dst_sc)

        sums_refs = (sums_wide_ref, sums_once_ref)
        dgw_sc[...] = jnp.zeros_like(dgw_sc)
        for s in range(seqs):
            _lane_blocks(d, functools.partial(_hg_pre_block, p_ref.at[s], lb_ref, sums_refs, q_sc.at[s], k_sc.at[s],
                                              v_sc.at[s], e_sc.at[s]))

            def post_bwd(rows, s=s):
                _, vjp = jax.vjp(_hg_post, o_ref[s, rows, :], p_ref[s, rows, 3 * d:4 * d].astype(F32), gw_ref[...])
                do_sc[s, rows, :], dgate, dgw = vjp(do2_ref[s, rows, :].astype(F32))
                dp_ref[s, rows, 3 * d:4 * d] = dgate.astype(dp_ref.dtype)
                dgw_sc[...] += dgw

            _row_blocks(CHUNK, post_bwd)

        level_masks = masks_ref[...]
        _, vjp = jax.vjp(lambda *a: _HG_HEADS(*a, level_masks), st_all_ref[0],
                         *[_stack_all(r) for r in (q_sc, k_sc, v_sc, e_sc)])
        grads = vjp((_stack_all(do_sc), dst_sc[...]))
        dst_sc[...] = grads[0]
        for r, val in zip((dq_sc, dk_sc, dv_sc, de_sc), grads[1:]):
            _unstack_all(r, val)

        for s in range(seqs):
            def pre_bwd(at, s=s):
                sl = at()
                level_sums = (sums_wide_ref[...], sums_once_ref[...])
                _, vjp = jax.vjp(lambda qraw, f, lb: _hg_pre(qraw, f, lb, level_sums), p_ref[s, :, sl].astype(F32),
                                 p_ref[s, :, at(d)].astype(F32), lb_ref[:, sl])
                dqraw, df, dlb = vjp((dq_sc[s, :, sl], dk_sc[s, :, sl], de_sc[s, :, sl]))
                dp_ref[s, :, sl] = dqraw.astype(dp_ref.dtype)
                dp_ref[s, :, at(d)] = df.astype(dp_ref.dtype)
                dp_ref[s, :, at(2 * d)] = dv_sc[s, :, sl].astype(dp_ref.dtype)
                if s > 0:
                    dlb_ref[:, sl] += dlb
                    return

                @pl.when(first)
                def _():
                    dlb_ref[:, sl] = dlb

                @pl.when(jnp.logical_not(first))
                def _():
                    dlb_ref[:, sl] += dlb

            _lane_blocks(d, pre_bwd)

        @pl.when(first)
        def _():
            dgw_ref[...] = dgw_sc[...]

        @pl.when(jnp.logical_not(first))
        def _():
            dgw_ref[...] += dgw_sc[...]

    rows = lambda c: (0, nc - 1 - c, 0)
    vec = pl.BlockSpec((1, d), lambda c: (0, 0))
    act = pl.BlockSpec((seqs, CHUNK, d), rows)
    wide = pl.BlockSpec((seqs, CHUNK, 4 * d), rows)
    e_rows = sums[0].shape[0]
    dp, dlb, dgw = pl.pallas_call(
        body, name=name, grid=(nc,),
        in_specs=[wide, vec, vec] + [pl.BlockSpec(m.shape, lambda c: (0, 0)) for m in sums] + [
                  pl.BlockSpec(masks.shape, lambda c: (0, 0, 0)),
                  pl.BlockSpec((1, nh, HEAD_DIM, HEAD_DIM), lambda c: (nc - 1 - c, 0, 0, 0)), act, act],
        out_specs=[wide, vec, vec],
        out_shape=[jax.ShapeDtypeStruct((seqs, t, 4 * d), BF16), jax.ShapeDtypeStruct((1, d), F32),
                   jax.ShapeDtypeStruct((1, d), F32)],
        scratch_shapes=[pltpu.VMEM((nh, HEAD_DIM, HEAD_DIM), F32)]
        + [pltpu.VMEM((seqs, CHUNK, d), F32)] * 3 + [pltpu.VMEM((seqs, e_rows, d), F32)]
        + [pltpu.VMEM((seqs, CHUNK, d), F32)] * 4 + [pltpu.VMEM((seqs, e_rows, d), F32), pltpu.VMEM((1, d), F32)],
        compiler_params=_params("arbitrary"),
    )(proj.reshape(seqs, t, 4 * d), lb, gw, *sums, masks, st_all, o, do2.reshape(seqs, t, d))
    return dp.reshape(n, 4 * d), dlb, dgw


def _adam_update(w, g, m, v):
    b1c = 1.0 - ADAM_B1 ** ADAM_STEP
    b2c = 1.0 - ADAM_B2 ** ADAM_STEP
    m_new = ADAM_B1 * m + (1.0 - ADAM_B1) * g
    v_new = ADAM_B2 * v + (1.0 - ADAM_B2) * (g * g)
    delta = -ADAM_LR * ((m_new / b1c) / (jnp.sqrt(v_new / b2c) + ADAM_EPS) + ADAM_WD * w)
    return delta, m_new, v_new


def _adamw(w, g, m, v, name, tr=256):
    r, c = w.shape
    tr = _tile(r, tr)

    def body(w_ref, g_ref, m_ref, v_ref, d_ref, mo_ref, vo_ref):
        d_ref[...], mo_ref[...], vo_ref[...] = _adam_update(w_ref[...], g_ref[...], m_ref[...], v_ref[...])

    blk = pl.BlockSpec((tr, c), lambda i: (i, 0))
    return pl.pallas_call(
        body, name=name, grid=(r // tr,),
        in_specs=[blk] * 4, out_specs=[blk] * 3,
        out_shape=[jax.ShapeDtypeStruct((r, c), F32)] * 3,
        compiler_params=_params("arbitrary"),
    )(w, g, m, v)


def _adamw_slots(w, slot_bufs, m, v, name, tr=256):
    nl, r, c = w.shape
    tr = _tile(r, tr)

    def body(*refs):
        w_ref = refs[0]
        g_refs = refs[1:1 + nl]
        m_ref, v_ref, go_ref, d_ref, mo_ref, vo_ref = refs[1 + nl:]
        for k in range(nl):
            @pl.when(pl.program_id(0) == k)
            def _(k=k):
                g = g_refs[k][0].astype(F32)
                for s in range(1, N_DEV):
                    g = g + g_refs[k][s].astype(F32)
                go_ref[0] = g

        d_ref[0], mo_ref[0], vo_ref[0] = _adam_update(w_ref[0], go_ref[0], m_ref[0], v_ref[0])

    blk = pl.BlockSpec((1, tr, c), lambda l, i: (l, i, 0))
    g_specs = [pl.BlockSpec((N_DEV, tr, c), lambda l, i, k=k: (0, jnp.where(l == k, i, 0), 0)) for k in range(nl)]
    return pl.pallas_call(
        body, name=name, grid=(nl, r // tr),
        in_specs=[blk] + g_specs + [blk, blk], out_specs=[blk] * 4,
        out_shape=[jax.ShapeDtypeStruct((nl, r, c), F32)] * 4,
        compiler_params=_params("arbitrary", "arbitrary"),
    )(w, *slot_bufs, m, v)


def _mesh_pos():
    return lax.axis_index("x"), lax.axis_index("y"), lax.axis_index("c")


def _flip(pos, p):
    x, y, c = pos
    return ((1 - x) if p & 4 else x, (1 - y) if p & 2 else y, (1 - c) if p & 1 else c)


def _lin(pos):
    return 4 * pos[0] + 2 * pos[1] + pos[2]


_HBM = pl.BlockSpec(memory_space=pltpu.HBM)
_SEM = pl.BlockSpec(memory_space=pltpu.SEMAPHORE)
_DATAFLOW = pltpu.SideEffectType.DATAFLOW_SIDE_EFFECTING


class _Item:
    def __init__(self, src, land_shape, src_pick, dst_pick):
        self.src, self.land_shape, self.src_pick, self.dst_pick = src, land_shape, src_pick, dst_pick


def _remote_copies(items, src, land, send_sem, recv_sem, me, arriving):
    me_i = _lin(me)
    out = []
    for it, s_ref, l_ref in zip(items, src, land):
        for p in range(1, N_DEV):
            peer = _flip(me, p)
            out.append(pltpu.make_async_remote_copy(
                src_ref=it.src_pick(s_ref, _lin(peer)),
                dst_ref=it.dst_pick(l_ref, _lin(peer) if arriving else me_i),
                send_sem=send_sem, recv_sem=recv_sem, device_id=peer, device_id_type=pl.DeviceIdType.MESH))
    return out


def _own_copies(items, src, land, sem, me):
    me_i = _lin(me)
    return [pltpu.make_async_copy(it.src_pick(s_ref, me_i), it.dst_pick(l_ref, me_i), sem)
            for it, s_ref, l_ref in zip(items, src, land)]


def _exchange_start(groups, name):
    items = [it for g in groups for it in g]
    n, ng = len(items), len(groups)
    first = [sum(len(g) for g in groups[:gi]) for gi in range(ng)]

    def body(*refs):
        src, land = refs[0:n], refs[n:2 * n]
        send_sems, recv_sems = refs[2 * n:2 * n + ng], refs[2 * n + ng:2 * n + 2 * ng]
        token = refs[4 * n + 2 * ng]
        me = _mesh_pos()
        for gi, g in enumerate(groups):
            sl = slice(first[gi], first[gi] + len(g))
            for cp in _remote_copies(g, src[sl], land[sl], send_sems[gi], recv_sems[gi], me, arriving=False):
                cp.start()
            for cp in _own_copies(g, src[sl], land[sl], recv_sems[gi], me):
                cp.start()
        token[...] = jnp.zeros_like(token)

    srcs = [pltpu.with_memory_space_constraint(it.src, pltpu.HBM) for it in items]
    lands = [pltpu.with_memory_space_constraint(lax.empty(it.land_shape, it.src.dtype), pltpu.HBM) for it in items]
    res = pl.pallas_call(
        body, name=name,
        out_shape=([pltpu.SemaphoreType.DMA(())] * (2 * ng)
                   + [pltpu.HBM(it.src.shape, it.src.dtype) for it in items]
                   + [pltpu.HBM(it.land_shape, it.src.dtype) for it in items]
                   + [jax.ShapeDtypeStruct((8, 128), F32)]),
        in_specs=[_HBM] * (2 * n),
        out_specs=[_SEM] * (2 * ng) + [_HBM] * (2 * n) + [pl.BlockSpec(memory_space=pltpu.VMEM)],
        input_output_aliases={i: 2 * ng + i for i in range(2 * n)},
        compiler_params=pltpu.CompilerParams(has_side_effects=_DATAFLOW),
    )(*srcs, *lands)
    send_sems, recv_sems = res[0:ng], res[ng:2 * ng]
    src_thru, land_thru = res[2 * ng:2 * ng + n], res[2 * ng + n:2 * ng + 2 * n]
    handles = []
    for gi, g in enumerate(groups):
        sl = slice(first[gi], first[gi] + len(g))
        handles.append((g, src_thru[sl], land_thru[sl], send_sems[gi], recv_sems[gi]))
    return handles, res[-1]


def _exchange_wait(handle, after, name):
    items, src_thru, land_thru, send_sem, recv_sem = handle
    k = len(items)

    def body(*refs):
        src, land = refs[0:k], refs[k:2 * k]
        send_ref, recv_ref = refs[2 * k], refs[2 * k + 1]
        for cp in _remote_copies(items, src, land, send_ref, recv_ref, _mesh_pos(), arriving=True):
            cp.wait_send()
            cp.wait_recv()
        for cp in _own_copies(items, src, land, recv_ref, _mesh_pos()):
            cp.wait()

    res = pl.pallas_call(
        body, name=name,
        out_shape=([pltpu.HBM(s.shape, s.dtype) for s in src_thru] + [pltpu.HBM(l.shape, l.dtype) for l in land_thru]),
        in_specs=[_HBM] * (2 * k) + [_SEM, _SEM, pl.BlockSpec(memory_space=pl.ANY)],
        out_specs=[_HBM] * (2 * k),
        input_output_aliases={i: i for i in range(2 * k)},
        compiler_params=pltpu.CompilerParams(has_side_effects=_DATAFLOW),
    )(*src_thru, *land_thru, send_sem, recv_sem, after)
    return res[k:2 * k]


def _whole(ref, i):
    return ref


def _slot(ref, i):
    return ref.at[i]


def _rows_of(r):
    return lambda ref, i: ref.at[pl.ds(pl.multiple_of(i * r, r), r), :]


def _cols_of(c):
    return lambda ref, i: ref.at[:, pl.ds(pl.multiple_of(i * c, c), c)]


def _all_reduce_small(buf, after, name):
    r, c = buf.shape

    def body(src_ref, after_ref, out_ref, all_ref, send_sems, recv_sems):
        me = _mesh_pos()
        me_i = _lin(me)
        all_ref[me_i] = src_ref[...]
        for p in range(1, N_DEV):
            peer = _flip(me, p)
            pltpu.make_async_remote_copy(
                src_ref=src_ref, dst_ref=all_ref.at[me_i], send_sem=send_sems.at[p - 1], recv_sem=recv_sems.at[p - 1],
                device_id=peer, device_id_type=pl.DeviceIdType.MESH).start()
        for p in range(1, N_DEV):
            peer = _flip(me, p)
            cp = pltpu.make_async_remote_copy(
                src_ref=src_ref, dst_ref=all_ref.at[_lin(peer)], send_sem=send_sems.at[p - 1],
                recv_sem=recv_sems.at[p - 1], device_id=peer, device_id_type=pl.DeviceIdType.MESH)
            cp.wait_recv()
            cp.wait_send()
        acc = all_ref[0]
        for s in range(1, N_DEV):
            acc = acc + all_ref[s]
        out_ref[...] = acc

    vm = pl.BlockSpec(memory_space=pltpu.VMEM)
    return pl.pallas_call(
        body, name=name, in_specs=[vm, pl.BlockSpec(memory_space=pl.ANY)], out_specs=vm,
        out_shape=jax.ShapeDtypeStruct((r, c), F32),
        scratch_shapes=[pltpu.VMEM((N_DEV, r, c), F32), pltpu.SemaphoreType.DMA((N_DEV - 1,)),
                        pltpu.SemaphoreType.DMA((N_DEV - 1,))],
        compiler_params=pltpu.CompilerParams(has_side_effects=True),
    )(buf, after)


def _unshard_cols(g):
    s, l, r, c = g.shape
    return jnp.transpose(g, (1, 2, 0, 3)).reshape(l, r, s * c)


def kernel(x, gdn_w_in, gdn_conv, gdn_a_log, gdn_dt_bias, gdn_onorm, gdn_w_out, hgrn_w_in, hgrn_lb_logits, hgrn_gnorm, hgrn_w_out, norm_mix, norm_mlp, mlp_w_up, mlp_w_down, norm_final, loss_target, m_gdn_w_in, m_gdn_conv, m_gdn_a_log, m_gdn_dt_bias, m_gdn_onorm, m_gdn_w_out, m_hgrn_w_in, m_hgrn_lb_logits, m_hgrn_gnorm, m_hgrn_w_out, m_norm_mix, m_norm_mlp, m_mlp_w_up, m_mlp_w_down, m_norm_final, v_gdn_w_in, v_gdn_conv, v_gdn_a_log, v_gdn_dt_bias, v_gdn_onorm, v_gdn_w_out, v_hgrn_w_in, v_hgrn_lb_logits, v_hgrn_gnorm, v_hgrn_w_out, v_norm_mix, v_norm_mlp, v_mlp_w_up, v_mlp_w_down, v_norm_final):
    seqs, seq_len, d = x.shape
    n = seqs * seq_len
    me_i = _lin(_mesh_pos())
    x2 = x.reshape(n, d)
    target = loss_target.reshape(n, d)
    n_gdn, n_hgrn = gdn_w_in.shape[0], hgrn_w_in.shape[0]

    r_out, r_down = gdn_w_out.shape[1], mlp_w_down.shape[1]
    c_gin, c_hin, c_up = gdn_w_in.shape[2], hgrn_w_in.shape[2], mlp_w_up.shape[2]

    def gathered(w, pick, land_shape):
        return _Item(w.astype(BF16), land_shape, _whole, pick)

    groups = [[_Item(gdn_conv, (N_DEV,) + gdn_conv.shape, _whole, _slot),
               _Item(hgrn_gnorm, (N_DEV,) + hgrn_gnorm.shape, _whole, _slot)]]
    for i in range(DEPTH):
        j = i // 2
        if i % 2 == 0:
            groups += [[gathered(gdn_w_in[j], _slot, (N_DEV, d, c_gin))],
                       [gathered(gdn_w_out[j], _rows_of(r_out), (N_DEV * r_out, d))]]
        else:
            groups += [[gathered(hgrn_w_in[j], _cols_of(c_hin), (d, N_DEV * c_hin))],
                       [gathered(hgrn_w_out[j], _rows_of(r_out), (N_DEV * r_out, d))]]
        groups += [[gathered(mlp_w_up[i], _cols_of(c_up), (d, N_DEV * c_up))],
                   [gathered(mlp_w_down[i], _rows_of(r_down), (N_DEV * r_down, d))]]
    gather_handles, token = _exchange_start(groups, "gather_start")
    lbs = _lb_fwd(hgrn_lb_logits + token[0:1, 0:1], "lb_fwd")

    def arrived(k, after, name):
        return _exchange_wait(gather_handles[k], after, "gather_wait_" + name)

    saved = []
    w_in, w_ab, w_out, w_up, w_down = ([None] * DEPTH for _ in range(5))
    h = x2
    for i in range(DEPTH):
        j = i // 2
        if i == 0:
            g_conv, g_gnorm = arrived(0, h, "small")
            conv_full = _unshard_cols(g_conv)
            gnorm_full = jnp.transpose(g_gnorm, (1, 0, 2)).reshape(n_hgrn, d)
        if i == 0:
            y = _rms_fwd(h, norm_mix[0:1], "rms_mix_0")
        (w_in[i],) = arrived(1 + 4 * i, y, f"in_{i}")
        if i % 2 == 0:
            w_gin = jnp.transpose(w_in[i], (1, 0, 2)).reshape(d, N_DEV * c_gin)
            w_in[i] = w_gin[:, :GDN_MAIN]
            w_ab[i] = jnp.pad(w_gin[:, GDN_MAIN:], ((0, 0), (0, AB_PAD - 2 * N_HEADS)))
            projm = _mm(y, w_in[i], "nn", [BF16], f"gdn_proj_{i}")
            projab = _mm(y, w_ab[i], "nn", [F32], f"gdn_proj_ab_{i}")
            o2, st_all, conv_y, dinv_all = _gdn_fwd_all(projm, projab, conv_full[j], gdn_a_log[j:j + 1],
                                                    gdn_dt_bias[j:j + 1], gdn_onorm[j:j + 1], seqs, f"gdn_fwd_{i}")
            mix = (projm, projab, conv_y, st_all, dinv_all)
        else:
            proj = _mm(y, w_in[i], "nn", [BF16], f"hgrn_proj_{i}")
            o2, o_raw, st_all = _hgrn_fwd_all(proj, lbs[i:i + 1], gnorm_full[j:j + 1], seqs, f"hgrn_fwd_{i}")
            mix = (proj, o_raw, st_all)
        (w_out[i],) = arrived(2 + 4 * i, o2, f"out_{i}")
        h1, y2 = _mm_rows(o2, w_out[i], "nn", [F32, BF16], f"mix_out_{i}", epilogue=_ep_residual_norm, extras=(h,),
                     vectors=(norm_mlp[i:i + 1],))
        (w_up[i],) = arrived(3 + 4 * i, y2, f"up_{i}")
        u, act = _mm(y2, w_up[i], "nn", [BF16, BF16], f"mlp_up_{i}",
                     epilogue=lambda acc: (acc, jnp.square(jnp.maximum(acc, 0.0))))
        (w_down[i],) = arrived(4 + 4 * i, act, f"down_{i}")
        saved.append((h, y, mix, o2, h1, y2, u, act))
        if i + 1 < DEPTH:
            h, y = _mm_rows(act, w_down[i], "nn", [F32, BF16], f"mlp_down_{i}", epilogue=_ep_residual_norm, extras=(h1,),
                       vectors=(norm_mix[i + 1:i + 2],))
        else:
            h = _mm(act, w_down[i], "nn", [F32], f"mlp_down_{i}", epilogue=lambda acc, res: (res + acc,),
                    extras=(h1,))

    dh, dh_b, d_nf, sq = _loss_head(h, norm_final.reshape(1, d), target, "loss_head")

    d_nmix, d_nmlp = [None] * DEPTH, [None] * DEPTH
    d_conv, d_alog, d_dtb, d_onorm = [None] * n_gdn, [None] * n_gdn, [None] * n_gdn, [None] * n_gdn
    d_lb = [jnp.zeros((1, d), F32)] * DEPTH
    d_gnorm = [None] * n_hgrn
    mlp_handles, mix_handles = [None] * DEPTH, [None] * DEPTH
    token = None
    for i in reversed(range(DEPTH)):
        j = i // 2
        h_in, y, mix, o2, h1, y2, u, act = saved[i]
        g_down = _mm(act, dh_b, "tn", [BF16], f"g_down_{i}", after=token)
        du = _mm(dh_b, w_down[i], "nt", [BF16], f"d_u_{i}",
                 epilogue=lambda acc, uu: (acc * (2.0 * jnp.maximum(uu.astype(F32), 0.0)),), extras=(u,))
        g_up = _mm(y2, du, "tn", [BF16], f"g_up_{i}")
        mlp_handles[i], token = _exchange_start(
            [[_Item(g_down, (N_DEV, r_down, d), _rows_of(r_down), _slot)],
             [_Item(g_up, (N_DEV, d, c_up), _cols_of(c_up), _slot)]], f"scatter_start_mlp_{i}")
        dh1, dh1_b, d_nmlp[i] = _mm_rows(du, w_up[i], "nt", [F32, BF16], f"d_y2_{i}", epilogue=_ep_norm_bwd,
                                     extras=(h1, dh), vectors=(norm_mlp[i:i + 1],), n_sums=1, after=token)
        g_out = _mm(o2, dh1_b, "tn", [BF16], f"g_out_{i}")
        do2 = _mm(dh1_b, w_out[i], "nt", [BF16], f"d_o2_{i}")
        if i % 2 == 0:
            projm, projab, conv_y, st_all, dinv_all = mix
            dpm, dpab, d_conv[j], d_alog[j], d_dtb[j], d_onorm[j] = _gdn_bwd_all(
                projm, projab, conv_y, conv_full[j], gdn_a_log[j:j + 1], gdn_dt_bias[j:j + 1], gdn_onorm[j:j + 1],
                st_all, dinv_all, do2, seqs, f"gdn_bwd_{i}")
            g_main = _mm(y, dpm, "tn", [BF16], f"g_in_{i}")
            g_ab = _mm(y, dpab, "tn", [BF16], f"g_in_ab_{i}")
            g_in = jnp.concatenate([g_main, g_ab[:, :2 * N_HEADS]], axis=1)
            g_in = jnp.transpose(g_in.reshape(d, N_DEV, c_gin), (1, 0, 2))
            in_item = _Item(g_in, (N_DEV, d, c_gin), _slot, _slot)
            dy_ab = _mm(dpab, w_ab[i], "nt", [F32], f"d_y_ab_{i}")
            dp, dy_extras = dpm, (dy_ab, h_in, dh1)
            dy_epilogue = lambda acc, e, xx, dres, w: _ep_norm_bwd(acc + e, xx, dres, w)
        else:
            proj, o_raw, st_all = mix
            dp, d_lb[i], d_gnorm[j] = _hgrn_bwd_all(proj, lbs[i:i + 1], gnorm_full[j:j + 1], st_all, o_raw, do2,
                                               seqs, f"hgrn_bwd_{i}")
            g_in = _mm(y, dp, "tn", [BF16], f"g_in_{i}")
            in_item = _Item(g_in, (N_DEV, d, c_hin), _cols_of(c_hin), _slot)
            dy_extras, dy_epilogue = (h_in, dh1), _ep_norm_bwd
        mix_handles[i], token = _exchange_start(
            [[_Item(g_out, (N_DEV, r_out, d), _rows_of(r_out), _slot)], [in_item]], f"scatter_start_mix_{i}")
        dh, dh_b, d_nmix[i] = _mm_rows(dp, w_in[i], "nt", [F32, BF16], f"d_y_{i}", epilogue=dy_epilogue, extras=dy_extras,
                                  vectors=(norm_mix[i:i + 1],), n_sums=1, after=token)
        token = None
    grad_x = dh.reshape(x.shape)

    def landed(handles, k, layers, after, name):
        return [_exchange_wait(handles[i][k], after, f"scatter_wait_{name}_{i}")[0] for i in layers]

    every, even, odd = range(DEPTH), range(0, DEPTH, 2), range(1, DEPTH, 2)
    upd = {}
    upd["mlp_w_down"] = _adamw_slots(mlp_w_down, landed(mlp_handles, 0, every, dh, "down"), m_mlp_w_down,
                                     v_mlp_w_down, "adamw_mlp_w_down")
    upd["mlp_w_up"] = _adamw_slots(mlp_w_up, landed(mlp_handles, 1, every, upd["mlp_w_down"][1], "up"), m_mlp_w_up,
                                   v_mlp_w_up, "adamw_mlp_w_up")
    upd["hgrn_w_out"] = _adamw_slots(hgrn_w_out, landed(mix_handles, 0, odd, upd["mlp_w_up"][1], "out"),
                                     m_hgrn_w_out, v_hgrn_w_out, "adamw_hgrn_w_out")
    upd["hgrn_w_in"] = _adamw_slots(hgrn_w_in, landed(mix_handles, 1, odd, upd["hgrn_w_out"][1], "in"), m_hgrn_w_in,
                                    v_hgrn_w_in, "adamw_hgrn_w_in")

    dlb_rows = jnp.concatenate(d_lb, axis=0)
    tail = jnp.concatenate(
        [jnp.concatenate(d_onorm, axis=1), jnp.concatenate(d_alog, axis=1), jnp.concatenate(d_dtb, axis=1)], axis=1)
    tail = jnp.pad(tail, ((0, 0), (0, d - tail.shape[1])))
    conv_rows = jnp.stack(d_conv).reshape(-1, d)
    packed = jnp.concatenate(
        [jnp.concatenate(d_nmix, axis=0), jnp.concatenate(d_nmlp, axis=0), d_nf, sq, dlb_rows,
         jnp.concatenate(d_gnorm, axis=0), tail, conv_rows], axis=0)
    pad_rows = (-packed.shape[0]) % 8
    packed = jnp.pad(packed, ((0, pad_rows), (0, 0)))
    tot = _all_reduce_small(packed, upd["hgrn_w_in"][1], "reduce_small")

    upd["gdn_w_out"] = _adamw_slots(gdn_w_out, landed(mix_handles, 0, even, tot, "out"),
                                    m_gdn_w_out, v_gdn_w_out, "adamw_gdn_w_out")
    upd["gdn_w_in"] = _adamw_slots(gdn_w_in, landed(mix_handles, 1, even, upd["gdn_w_out"][1], "in"), m_gdn_w_in,
                                   v_gdn_w_in, "adamw_gdn_w_in")

    def update(name, w, g, m, v):
        shape = w.shape
        c = shape[-1]
        res = _adamw(w.reshape(-1, c), g.reshape(-1, c), m.reshape(-1, c), v.reshape(-1, c), "adamw_" + name)
        return [g.reshape(shape)] + [o.reshape(shape) for o in res]

    r0 = 0
    g_nmix = tot[r0:r0 + DEPTH]; r0 += DEPTH
    g_nmlp = tot[r0:r0 + DEPTH]; r0 += DEPTH
    g_nf = tot[r0]; r0 += 1
    loss = tot[r0, 0]; r0 += 1
    g_lb = _lb_bwd(hgrn_lb_logits, tot[r0:r0 + DEPTH], "lb_bwd"); r0 += DEPTH
    g_gnorm_full = tot[r0:r0 + n_hgrn]; r0 += n_hgrn
    t_row = tot[r0]; r0 += 1
    g_conv_full = tot[r0:r0 + n_gdn * CONV_K * 3].reshape(n_gdn, CONV_K, 3 * d)
    g_onorm = t_row[0:n_gdn * HEAD_DIM].reshape(n_gdn, HEAD_DIM)
    o1 = n_gdn * HEAD_DIM
    g_alog = t_row[o1:o1 + n_gdn * N_HEADS].reshape(n_gdn, N_HEADS)
    g_dtb = t_row[o1 + n_gdn * N_HEADS:o1 + 2 * n_gdn * N_HEADS].reshape(n_gdn, N_HEADS)
    c_gn, c_cv = hgrn_gnorm.shape[1], gdn_conv.shape[2]
    g_gnorm = lax.dynamic_slice_in_dim(g_gnorm_full, me_i * c_gn, c_gn, axis=1)
    g_conv = lax.dynamic_slice_in_dim(g_conv_full, me_i * c_cv, c_cv, axis=2)

    upd["gdn_conv"] = update("gdn_conv", gdn_conv, g_conv, m_gdn_conv, v_gdn_conv)
    upd["gdn_a_log"] = update("gdn_a_log", gdn_a_log, g_alog, m_gdn_a_log, v_gdn_a_log)
    upd["gdn_dt_bias"] = update("gdn_dt_bias", gdn_dt_bias, g_dtb, m_gdn_dt_bias, v_gdn_dt_bias)
    upd["gdn_onorm"] = update("gdn_onorm", gdn_onorm, g_onorm, m_gdn_onorm, v_gdn_onorm)
    upd["hgrn_lb_logits"] = update("hgrn_lb_logits", hgrn_lb_logits, g_lb, m_hgrn_lb_logits, v_hgrn_lb_logits)
    upd["hgrn_gnorm"] = update("hgrn_gnorm", hgrn_gnorm, g_gnorm, m_hgrn_gnorm, v_hgrn_gnorm)
    upd["norm_mix"] = update("norm_mix", norm_mix, g_nmix, m_norm_mix, v_norm_mix)
    upd["norm_mlp"] = update("norm_mlp", norm_mlp, g_nmlp, m_norm_mlp, v_norm_mlp)
    upd["norm_final"] = update("norm_final", norm_final, g_nf, m_norm_final, v_norm_final)

    order = ["gdn_w_in", "gdn_conv", "gdn_a_log", "gdn_dt_bias", "gdn_onorm", "gdn_w_out", "hgrn_w_in",
             "hgrn_lb_logits", "hgrn_gnorm", "hgrn_w_out", "norm_mix", "norm_mlp", "mlp_w_up", "mlp_w_down",
             "norm_final"]
    outs = [loss, grad_x]
    for k in range(4):
        outs += [upd[name][k] for name in order]
    return tuple(outs)
```

```python
import functools

import numpy as np
import jax
import jax.numpy as jnp
from jax import lax
from jax.experimental import pallas as pl
from jax.experimental.pallas import tpu as pltpu

F32 = jnp.float32
BF16 = jnp.bfloat16

D_MODEL = 1024
N_HEADS = 8
HEAD_DIM = 128
CHUNK = 64
SUB = 16
N_SUB = CHUNK // SUB
CONV_K = 4
HALO = 16
EPS = 1e-6
DEPTH = 4
N_DEV = 8
GDN_MAIN = 4 * D_MODEL
GDN_IN = GDN_MAIN + 2 * N_HEADS
AB_PAD = 128
LANE_BLOCK = 256
ROW_BLOCK = 16
BLOCK_UNROLL = 4

ADAM_LR = 0.001
ADAM_B1 = 0.9
ADAM_B2 = 0.999
ADAM_EPS = 1e-08
ADAM_WD = 0.01
ADAM_STEP = 10

VMEM_LIMIT = 56 * 1024 * 1024
MM_TILE = 1024
MM_VMEM_BUDGET = 40 * 1024 * 1024
MM_ROWS_TILE = 512
_DIMS = {
    "nn": (((1,), (0,)), ((), ())),
    "nt": (((1,), (1,)), ((), ())),
    "tn": (((0,), (0,)), ((), ())),
}


def _parts(x, n):
    if n == 1 and x.dtype == BF16:
        return [x]
    out = []
    r = x.astype(F32)
    for i in range(n):
        p = r.astype(BF16)
        out.append(p)
        if i + 1 < n:
            r = r - p.astype(F32)
    return out


def _dot_raw(a, b, mode, na, nb):
    ap, bp = _parts(a, na), _parts(b, nb)
    nmax = max(na, nb)
    pairs = [(i, j) for i in range(na) for j in range(nb) if i + j < nmax]
    ka = 0 if mode == "tn" else 1
    kb = 1 if mode == "nt" else 0
    xa = ap[0] if len(pairs) == 1 else jnp.concatenate([ap[i] for i, _ in pairs], axis=ka)
    xb = bp[0] if len(pairs) == 1 else jnp.concatenate([bp[j] for _, j in pairs], axis=kb)
    return lax.dot_general(xa, xb, _DIMS[mode], preferred_element_type=F32)


@functools.partial(jax.custom_vjp, nondiff_argnums=(2, 3, 4))
def _dot(a, b, mode, na, nb):
    return _dot_raw(a, b, mode, na, nb)


def _dot_fwd(a, b, mode, na, nb):
    return _dot_raw(a, b, mode, na, nb), (a, b)


def _dot_bwd(mode, na, nb, res, ct):
    a, b = res
    if mode == "nn":
        da = _dot_raw(ct, b, "nt", 1, 1)
        db = _dot_raw(a, ct, "tn", 1, 1)
    elif mode == "nt":
        da = _dot_raw(ct, b, "nn", 1, 1)
        db = _dot_raw(ct, a, "tn", 1, 1)
    else:
        da = _dot_raw(b, ct, "nt", 1, 1)
        db = _dot_raw(a, ct, "nn", 1, 1)
    return da.astype(a.dtype), db.astype(b.dtype)


_dot.defvjp(_dot_fwd, _dot_bwd)


N_EXACT = 3


@jax.custom_vjp
def _dot01(x, m_wide, m):
    return lax.dot_general(m_wide, jnp.concatenate(_parts(x, N_EXACT), axis=0), _DIMS["nn"], preferred_element_type=F32)


def _dot01_fwd(x, m_wide, m):
    return _dot01(x, m_wide, m), (m_wide, m)


def _dot01_bwd(res, ct):
    m_wide, m = res
    dx = lax.dot_general(m, ct.astype(BF16), _DIMS["tn"], preferred_element_type=F32)
    return dx, jnp.zeros_like(m_wide), jnp.zeros_like(m)


_dot01.defvjp(_dot01_fwd, _dot01_bwd)


def _thrice(m):
    return jnp.concatenate([m] * N_EXACT, axis=1).astype(BF16), m.astype(BF16)


def _iota2(shape, dim):
    return lax.broadcasted_iota(jnp.int32, shape, dim)


def _tril_f32(n):
    return (_iota2((n, n), 0) >= _iota2((n, n), 1)).astype(F32)


def _cumsum_rows(g):
    return _dot(_tril_f32(g.shape[0]), g, "nn", 1, 3)


def _below_block(n, b):
    ri, ci = _iota2((n, n), 0) // b, _iota2((n, n), 1) // b
    return (ri == ci + 1) & (ri % 2 == 1)


def _half_inverses(L):
    n = L.shape[0]
    eye = (_iota2((n, n), 0) == _iota2((n, n), 1)).astype(F32)
    d = eye - jnp.where(_below_block(n, 1), L, 0.0)
    b = 2
    while 2 * b < n:
        e = jnp.where(_below_block(n, b), L, 0.0)
        d = d - _dot_raw(d, _dot_raw(e, d, "nn", 2, 2), "nn", 2, 2)
        b *= 2
    return d, jnp.where(_below_block(n, b), L, 0.0)


def _solve_with(d, e, rhs):
    y = _dot_raw(d, rhs, "nn", 2, 2)
    return y - _dot_raw(d, _dot_raw(e, y, "nn", 2, 2), "nn", 2, 2)


@jax.custom_vjp
def _solve_unit_lower(L, rhs, d):
    n = L.shape[0]
    return _solve_with(d, jnp.where(_below_block(n, n // 2), L, 0.0), rhs)


def _solve_fwd(L, rhs, d):
    n = L.shape[0]
    e = jnp.where(_below_block(n, n // 2), L, 0.0)
    sol = _solve_with(d, e, rhs)
    return sol, (d, e, sol)


def _solve_bwd(res, ct):
    d, e, sol = res
    y = _dot_raw(d, ct - _dot_raw(e, _dot_raw(d, ct, "tn", 2, 2), "tn", 2, 2), "tn", 2, 2)
    return -_dot_raw(y, sol, "nt", 2, 2), y, jnp.zeros_like(d)


_solve_unit_lower.defvjp(_solve_fwd, _solve_bwd)


def _softplus(x):
    return jnp.maximum(x, 0.0) + jnp.log1p(jnp.exp(-jnp.abs(x)))


def _rms(x, w):
    return x * lax.rsqrt(jnp.mean(x * x, axis=-1, keepdims=True) + EPS) * w


HG_LEVELS = (32, 16, 8, 4, 2, 1)


def _hg_level_sums():
    i = np.arange(CHUNK)[:, None]
    m = np.arange(CHUNK)[None, :]
    to_row = [(m <= i) & (m // b == i // b) for b in HG_LEVELS]
    to_col = [(m > i) & (m // b == i // b) for b in HG_LEVELS if b > 1]
    return _thrice(jnp.asarray(np.concatenate(to_row + to_col + [m <= i]), F32))


def _hg_level_masks():
    i = np.arange(CHUNK)[:, None]
    j = np.arange(CHUNK)[None, :]
    return jnp.asarray(np.stack([(i // b == j // b + 1) & ((i // b) % 2 == 1) for b in HG_LEVELS]), F32)


def _hg_pre(qraw, f, lb, sums):
    g = jnp.log(lb + (1.0 - lb) * jax.nn.sigmoid(f))
    k = (1.0 - lb) * jax.nn.sigmoid(-f)
    q = jax.nn.silu(qraw) * (HEAD_DIM ** -0.5)
    return q, k, _dot01(g, *sums)


def _hg_head(st, q, k, v, e, masks):
    nl = len(HG_LEVELS)
    eye = (_iota2((CHUNK, CHUNK), 0) == _iota2((CHUNK, CHUNK), 1)).astype(F32)
    a = eye * jnp.sum(q * k, axis=-1, keepdims=True)
    for l, b in enumerate(HG_LEVELS):
        rows = q * jnp.exp(e[l * CHUNK:(l + 1) * CHUNK])
        cols = k * jnp.exp(e[(nl + l) * CHUNK:(nl + l + 1) * CHUNK]) if b > 1 else k
        a = a + masks[l] * _dot(rows, cols, "nt", 1, 1)
    gc = e[(2 * nl - 1) * CHUNK:2 * nl * CHUNK]
    o = _dot(a, v, "nn", 1, 1) + _dot(q * jnp.exp(gc), st, "nt", 1, 1)
    g_last = gc[CHUNK - 1:CHUNK]
    st_new = st * jnp.exp(g_last) + _dot(v, k * jnp.exp(g_last - gc), "tn", 1, 1)
    return o, st_new


_HG_HEADS = jax.vmap(_hg_head, in_axes=(0, 0, 0, 0, 0, None))


def _hg_post(o, gate, gw):
    return _rms(o, gw) * jax.nn.silu(gate)


def _gd_conv(xp, cw):
    off = HALO - (CONV_K - 1)
    y = cw[0:1] * xp[off:off + CHUNK]
    for kk in range(1, CONV_K):
        y = y + cw[kk:kk + 1] * xp[off + kk:off + kk + CHUNK]
    return y


def _gd_conv_bwd(xp, cw, y, dc):
    off = HALO - (CONV_K - 1)
    sig = jax.nn.sigmoid(y)
    dy = dc * (sig * (1.0 + y * (1.0 - sig)))
    dxp, dcw = None, []
    for kk in range(CONV_K):
        moved = jnp.pad(dy, ((off + kk, HALO - off - kk), (0, 0)))
        term = cw[kk:kk + 1] * moved
        dxp = term if dxp is None else dxp + term
        dcw.append(jnp.sum(xp * moved, axis=0, keepdims=True))
    return dxp, jnp.concatenate(dcw, axis=0)


def _gd_gates(a, b, alog, dtb):
    beta = jax.nn.sigmoid(b)
    g = -jnp.exp(alog) * _softplus(a + dtb)
    expand = (_iota2((N_HEADS, D_MODEL), 1) // HEAD_DIM == _iota2((N_HEADS, D_MODEL), 0)).astype(F32)
    g_x = _dot(g, expand, "nn", 3, 1)
    after = (_iota2((CHUNK, D_MODEL), 0) > _iota2((CHUNK, D_MODEL), 1) % HEAD_DIM).astype(F32)
    sums = _dot01(jnp.concatenate([g_x, g_x * after], axis=1), *_thrice(_tril_f32(CHUNK)))
    return _dot(beta, expand, "nn", 3, 1), sums


def _gd_head(st, q, k, v, beta, gc, diff, gate, onw, dinv=None):
    q = q * lax.rsqrt(jnp.sum(q * q, axis=-1, keepdims=True) + EPS) * (HEAD_DIM ** -0.5)
    k = k * lax.rsqrt(jnp.sum(k * k, axis=-1, keepdims=True) + EPS)
    ri = _iota2((CHUNK, CHUNK), 0)
    ci = _iota2((CHUNK, CHUNK), 1)
    decay = jnp.exp(jnp.where(ri >= ci, diff[:, 0:CHUNK], -jnp.inf))
    kb = k * beta
    egc = jnp.exp(gc)
    L = jnp.where(ri > ci, _dot(kb, k, "nt", 1, 1) * decay, 0.0)
    made = dinv is None
    if made:
        dinv = _half_inverses(L)[0]
    sol = _solve_unit_lower(L, jnp.concatenate([v * beta, kb * egc], axis=1), dinv)
    u = sol[:, 0:HEAD_DIM]
    w = sol[:, HEAD_DIM:2 * HEAD_DIM]
    a_qk = jnp.where(ri >= ci, _dot(q, k, "nt", 1, 1) * decay, 0.0)
    g_last = gc[CHUNK - 1:CHUNK]
    v_new = u - _dot(w, st, "nt", 1, 1)
    o = _dot(q * egc, st, "nt", 1, 1) + _dot(a_qk, v_new, "nn", 1, 1)
    st_new = st * jnp.exp(g_last) + _dot(v_new, k * jnp.exp(g_last - gc), "tn", 1, 1)
    out = (_rms(o, onw) * jax.nn.silu(gate), st_new)
    return out + (dinv,) if made else out


def _params(*sem):
    return pltpu.CompilerParams(dimension_semantics=sem, vmem_limit_bytes=VMEM_LIMIT)


def _tile(n, pref):
    t = min(n, pref)
    assert n % t == 0, (n, pref)
    return t


def _mm_tiles(m, n, k, a_size, b_size, tile_sizes):
    tm, tn, tk = _tile(m, MM_TILE), _tile(n, MM_TILE), k

    def need(tm, tn, tk):
        acc = 4 * tm * tn * (2 if tk < k else 1)
        return 2 * (tm * tk * a_size + tk * tn * b_size + tm * tn * sum(tile_sizes)) + acc

    while need(tm, tn, tk) > MM_VMEM_BUDGET:
        if tk > 2048 or (tk > 512 and tm <= 512):
            tk //= 2
        else:
            tm //= 2
    return tm, tn, tk


def _mm(a, b, mode, out_dtypes, name, epilogue=None, extras=(), after=None):
    if mode == "nn":
        (m, k), (k2, n) = a.shape, b.shape
    elif mode == "nt":
        (m, k), (n, k2) = a.shape, b.shape
    else:
        (k, m), (k2, n) = a.shape, b.shape
    assert k == k2, (a.shape, b.shape, mode)
    tm, tn, tk = _mm_tiles(m, n, k, a.dtype.itemsize, b.dtype.itemsize,
                           [e.dtype.itemsize for e in extras] + [jnp.dtype(dt).itemsize for dt in out_dtypes])
    nk = k // tk
    ne, no, nafter = len(extras), len(out_dtypes), int(after is not None)
    if epilogue is None:
        epilogue = lambda acc: (acc,)

    def body(*refs):
        a_ref, b_ref = refs[0], refs[1]
        ex = refs[2:2 + ne]
        outs = refs[2 + ne + nafter:2 + ne + nafter + no]
        part = lax.dot_general(a_ref[...].astype(BF16), b_ref[...].astype(BF16), _DIMS[mode],
                               preferred_element_type=F32)

        def finish(acc):
            for o_ref, val in zip(outs, epilogue(acc, *[e[...] for e in ex])):
                o_ref[...] = val.astype(o_ref.dtype)

        if nk == 1:
            finish(part)
        else:
            acc_ref = refs[-1]
            kk = pl.program_id(2)

            @pl.when(kk == 0)
            def _():
                acc_ref[...] = part

            @pl.when(kk > 0)
            def _():
                acc_ref[...] += part

            @pl.when(kk == nk - 1)
            def _():
                finish(acc_ref[...])

    if mode == "tn":
        a_spec = pl.BlockSpec((tk, tm), lambda i, j, kk: (kk, i))
    else:
        a_spec = pl.BlockSpec((tm, tk), lambda i, j, kk: (i, kk))
    if mode == "nt":
        b_spec = pl.BlockSpec((tn, tk), lambda i, j, kk: (j, kk))
    else:
        b_spec = pl.BlockSpec((tk, tn), lambda i, j, kk: (kk, j))
    o_spec = pl.BlockSpec((tm, tn), lambda i, j, kk: (i, j))
    res = pl.pallas_call(
        body,
        name=name,
        grid=(m // tm, n // tn, nk),
        in_specs=[a_spec, b_spec] + [o_spec] * ne + [pl.BlockSpec(memory_space=pl.ANY)] * nafter,
        out_specs=[o_spec] * no,
        out_shape=[jax.ShapeDtypeStruct((m, n), dt) for dt in out_dtypes],
        scratch_shapes=[pltpu.VMEM((tm, tn), F32)] if nk > 1 else [],
        compiler_params=_params("parallel", "parallel", "arbitrary"),
    )(a, b, *extras, *([after] if nafter else []))
    return res[0] if no == 1 else res


def _mm_rows(a, b, mode, out_dtypes, name, epilogue, extras=(), vectors=(), n_sums=0, after=None):
    assert mode in ("nn", "nt")
    (m, k), n = a.shape, (b.shape[1] if mode == "nn" else b.shape[0])
    tm = _tile(m, MM_ROWS_TILE)
    mt = m // tm
    ne, no, nafter = len(extras) + len(vectors), len(out_dtypes), int(after is not None)

    def body(*refs):
        a_ref, b_ref = refs[0], refs[1]
        ex = refs[2:2 + ne]
        outs = refs[2 + ne + nafter:2 + ne + nafter + no]
        sums = refs[2 + ne + nafter + no:2 + ne + nafter + no + n_sums]
        acc_ref = refs[-1]
        i = pl.program_id(0)

        @pl.when(i == 0)
        def _():
            acc_ref[1] = jnp.zeros((tm, n), F32)

        vals = epilogue(acc_ref[1 - i % 2], *[e[...] for e in ex])
        acc_ref[i % 2] = lax.dot_general(a_ref[...].astype(BF16), b_ref[...].astype(BF16), _DIMS[mode],
                                         preferred_element_type=F32)
        for o_ref, val in zip(outs, vals[:no]):
            o_ref[...] = val.astype(o_ref.dtype)
        for s_ref, val in zip(sums, vals[no:]):
            @pl.when(i <= 1)
            def _(s_ref=s_ref, val=val):
                s_ref[...] = val

            @pl.when(i > 1)
            def _(s_ref=s_ref, val=val):
                s_ref[...] += val

    ahead = lambda i: (jnp.minimum(i, mt - 1), 0)
    behind = lambda i: (jnp.maximum(i - 1, 0), 0)
    fixed = lambda i: (0, 0)
    row = pl.BlockSpec((tm, n), behind)
    vec = pl.BlockSpec((1, n), fixed)
    res = pl.pallas_call(
        body, name=name, grid=(mt + 1,),
        in_specs=([pl.BlockSpec((tm, k), ahead), pl.BlockSpec(b.shape, fixed)] + [row] * len(extras)
                  + [vec] * len(vectors) + [pl.BlockSpec(memory_space=pl.ANY)] * nafter),
        out_specs=[row] * no + [vec] * n_sums,
        out_shape=[jax.ShapeDtypeStruct((m, n), dt) for dt in out_dtypes] + [jax.ShapeDtypeStruct((1, n), F32)] * n_sums,
        scratch_shapes=[pltpu.VMEM((2, tm, n), F32)],
        compiler_params=_params("arbitrary"),
    )(a, b, *extras, *vectors, *([after] if nafter else []))
    return res[0] if no + n_sums == 1 else res


def _ep_residual_norm(acc, res, w):
    h = res + acc
    return h, _rms(h, w)


def _ep_norm_bwd(acc, x, dres, w):
    r = lax.rsqrt(jnp.mean(x * x, axis=-1, keepdims=True) + EPS)
    g = acc * w
    dx = dres + (r * g - x * (r * r * r * jnp.mean(g * x, axis=-1, keepdims=True)))
    return dx, dx, jnp.sum(acc * (x * r), axis=0, keepdims=True)


def _rms_fwd(x, w, name, tm=512):
    n, d = x.shape
    tm = _tile(n, tm)

    def body(x_ref, w_ref, y_ref):
        y_ref[...] = _rms(x_ref[...], w_ref[...]).astype(y_ref.dtype)

    return pl.pallas_call(
        body, name=name, grid=(n // tm,),
        in_specs=[pl.BlockSpec((tm, d), lambda i: (i, 0)), pl.BlockSpec((1, d), lambda i: (0, 0))],
        out_specs=pl.BlockSpec((tm, d), lambda i: (i, 0)),
        out_shape=jax.ShapeDtypeStruct((n, d), BF16),
        compiler_params=_params("arbitrary"),
    )(x, w)


def _loss_head(h, w, target, name, tm=512):
    n, d = h.shape
    tm = _tile(n, tm)

    def body(h_ref, w_ref, t_ref, dh_ref, dhb_ref, dw_ref, sq_ref):
        y, vjp = jax.vjp(_rms, h_ref[...], w_ref[...])
        err = y - t_ref[...]
        dh, dw = vjp(err * (1.0 / d))
        dh_ref[...] = dh
        dhb_ref[...] = dh.astype(dhb_ref.dtype)
        sq = jnp.sum(err * err, axis=0, keepdims=True)

        @pl.when(pl.program_id(0) == 0)
        def _():
            dw_ref[...] = dw
            sq_ref[...] = sq

        @pl.when(pl.program_id(0) > 0)
        def _():
            dw_ref[...] += dw
            sq_ref[...] += sq

        @pl.when(pl.program_id(0) == n // tm - 1)
        def _():
            total = jnp.sum(sq_ref[...], axis=1, keepdims=True) * (0.5 / d)
            sq_ref[...] = jnp.broadcast_to(total, sq_ref.shape)

    row = pl.BlockSpec((tm, d), lambda i: (i, 0))
    vec = pl.BlockSpec((1, d), lambda i: (0, 0))
    return pl.pallas_call(
        body, name=name, grid=(n // tm,),
        in_specs=[row, vec, row],
        out_specs=[row, row, vec, vec],
        out_shape=[jax.ShapeDtypeStruct((n, d), F32), jax.ShapeDtypeStruct((n, d), BF16),
                   jax.ShapeDtypeStruct((1, d), F32), jax.ShapeDtypeStruct((1, d), F32)],
        compiler_params=_params("arbitrary"),
    )(h, w, target)


def _lower_bounds(logits):
    sm = jax.nn.softmax(logits, axis=0)
    rows = [sm[0:1] * 0.0]
    for r in range(1, DEPTH):
        rows.append(rows[-1] + sm[r:r + 1])
    return jnp.concatenate(rows, axis=0)


def _lb_fwd(logits, name):
    def body(l_ref, o_ref):
        o_ref[...] = _lower_bounds(l_ref[...])

    return pl.pallas_call(body, name=name, out_shape=jax.ShapeDtypeStruct(logits.shape, F32))(logits)


def _lb_bwd(logits, dlb, name):
    def body(l_ref, d_ref, o_ref):
        _, vjp = jax.vjp(_lower_bounds, l_ref[...])
        (o_ref[...],) = vjp(d_ref[...])

    return pl.pallas_call(body, name=name, out_shape=jax.ShapeDtypeStruct(logits.shape, F32))(logits, dlb)


def _head_slice(h):
    return pl.ds(h * HEAD_DIM, HEAD_DIM)


_GD_HEADS = jax.vmap(_gd_head, in_axes=(0, 0, 0, 0, 0, 0, 0, 0, None))
_GD_HEADS_AGAIN = jax.vmap(_gd_head, in_axes=(0, 0, 0, 0, 0, 0, 0, 0, None, 0))


def _lane_blocks(width, block_body):
    def trip(j, carry):
        block_body(lambda base=0: pl.ds(pl.multiple_of(j * LANE_BLOCK + base, LANE_BLOCK), LANE_BLOCK))
        return carry

    lax.fori_loop(0, width // LANE_BLOCK, trip, 0, unroll=BLOCK_UNROLL)


def _row_blocks(rows, block_body):
    def trip(j, carry):
        block_body(pl.ds(pl.multiple_of(j * ROW_BLOCK, ROW_BLOCK), ROW_BLOCK))
        return carry

    lax.fori_loop(0, rows // ROW_BLOCK, trip, 0, unroll=BLOCK_UNROLL)


def _hg_pre_block(p_ref, lb_ref, sums_refs, q_sc, k_sc, v_sc, e_sc, at):
    sl = at()
    q_sc[:, sl], k_sc[:, sl], e_sc[:, sl] = _hg_pre(
        p_ref[:, sl].astype(F32), p_ref[:, at(D_MODEL)].astype(F32), lb_ref[:, sl], [r[...] for r in sums_refs])
    v_sc[:, sl] = p_ref[:, at(2 * D_MODEL)].astype(F32)


def _gd_xp(halo_ref, p_ref, sl, first_chunk):
    halo = jnp.where(first_chunk, 0.0, halo_ref[:, sl].astype(F32))
    return jnp.concatenate([halo, p_ref[:, sl].astype(F32)], axis=0)


def _stack_all(ref, first=0):
    return jnp.stack([ref[s, :, _head_slice(h + first)] for s in range(ref.shape[0]) for h in range(N_HEADS)])


def _unstack_all(ref, val, first=0):
    for s in range(ref.shape[0]):
        for h in range(N_HEADS):
            ref[s, :, _head_slice(h + first)] = val[s * N_HEADS + h].astype(ref.dtype)


def _gdn_fwd_all(projm, projab, cw, alog, dtb, onw, seqs, name):
    n = projm.shape[0]
    t = n // seqs
    nc = t // CHUNK
    d = D_MODEL
    per_halo = CHUNK // HALO
    nh = seqs * N_HEADS

    def body(p_ref, halo_ref, ab_ref, cw_ref, alog_ref, dtb_ref, onw_ref, o2_ref, st_all_ref, y_ref, dinv_ref,
             st_sc, c_sc, beta_sc, g_sc):
        first_chunk = pl.program_id(0) == 0

        @pl.when(first_chunk)
        def _():
            st_sc[...] = jnp.zeros_like(st_sc)

        for s in range(seqs):
            def conv(at, s=s):
                sl = at()
                y = _gd_conv(_gd_xp(halo_ref.at[s], p_ref.at[s], sl, first_chunk), cw_ref[:, sl])
                y_ref[s, :, sl] = y
                c_sc[s, :, sl] = jax.nn.silu(y)

            _lane_blocks(3 * d, conv)
            beta_sc[s], g_sc[s] = _gd_gates(ab_ref[s, :, 0:N_HEADS], ab_ref[s, :, N_HEADS:2 * N_HEADS],
                                            alog_ref[...], dtb_ref[...])
        st_all_ref[0] = st_sc[...]
        o2, st_sc[...], dinv_ref[0] = _GD_HEADS(
            st_sc[...], _stack_all(c_sc), _stack_all(c_sc, N_HEADS), _stack_all(c_sc, 2 * N_HEADS), _stack_all(beta_sc),
            _stack_all(g_sc), _stack_all(g_sc, N_HEADS), _stack_all(p_ref, 3 * N_HEADS).astype(F32), onw_ref[...])
        _unstack_all(o2_ref, o2)

    rows = lambda c: (0, c, 0)
    const = lambda c: (0, 0)
    per_chunk = lambda c: (c, 0, 0, 0)
    p3 = projm.reshape(seqs, t, 4 * d)
    o2, st_all, conv_y, dinv_all = pl.pallas_call(
        body, name=name, grid=(nc,),
        in_specs=[pl.BlockSpec((seqs, CHUNK, 4 * d), rows),
                  pl.BlockSpec((seqs, HALO, 3 * d), lambda c: (0, jnp.maximum(c * per_halo - 1, 0), 0)),
                  pl.BlockSpec((seqs, CHUNK, AB_PAD), rows),
                  pl.BlockSpec((CONV_K, 3 * d), const), pl.BlockSpec((1, N_HEADS), const),
                  pl.BlockSpec((1, N_HEADS), const), pl.BlockSpec((1, HEAD_DIM), const)],
        out_specs=[pl.BlockSpec((seqs, CHUNK, d), rows), pl.BlockSpec((1, nh, HEAD_DIM, HEAD_DIM), per_chunk),
                   pl.BlockSpec((seqs, CHUNK, 3 * d), rows), pl.BlockSpec((1, nh, CHUNK, CHUNK), per_chunk)],
        out_shape=[jax.ShapeDtypeStruct((seqs, t, d), BF16), jax.ShapeDtypeStruct((nc, nh, HEAD_DIM, HEAD_DIM), F32),
                   jax.ShapeDtypeStruct((seqs, t, 3 * d), F32), jax.ShapeDtypeStruct((nc, nh, CHUNK, CHUNK), F32)],
        scratch_shapes=[pltpu.VMEM((nh, HEAD_DIM, HEAD_DIM), F32), pltpu.VMEM((seqs, CHUNK, 3 * d), F32),
                        pltpu.VMEM((seqs, CHUNK, d), F32), pltpu.VMEM((seqs, CHUNK, 2 * d), F32)],
        compiler_params=_params("arbitrary"),
    )(p3, p3, projab.reshape(seqs, t, AB_PAD), cw, alog, dtb, onw)
    return o2.reshape(n, d), st_all, conv_y, dinv_all


def _gdn_bwd_all(projm, projab, conv_y, cw, alog, dtb, onw, st_all, dinv_all, do2, seqs, name):
    n = projm.shape[0]
    t = n // seqs
    nc = t // CHUNK
    d = D_MODEL
    per_halo = CHUNK // HALO
    nh = seqs * N_HEADS

    def body(p_ref, halo_ref, ab_ref, y_ref, cw_ref, alog_ref, dtb_ref, onw_ref, st_all_ref, dinv_ref, do2_ref,
             dp_ref, dab_ref, dcw_ref, dalog_ref, ddtb_ref, donw_ref,
             dst_sc, dhalo_sc, c_sc, beta_sc, g_sc, dc_sc, dbeta_sc, dg_sc):
        first = pl.program_id(0) == 0
        first_chunk = pl.program_id(0) == nc - 1

        @pl.when(first)
        def _():
            dst_sc[...] = jnp.zeros_like(dst_sc)
            dhalo_sc[...] = jnp.zeros_like(dhalo_sc)

        gates_vjps = []
        for s in range(seqs):
            def act(at, s=s):
                c_sc[s, :, at()] = jax.nn.silu(y_ref[s, :, at()])

            _lane_blocks(3 * d, act)
            (beta_sc[s], g_sc[s]), gates_vjp = jax.vjp(
                _gd_gates, ab_ref[s, :, 0:N_HEADS], ab_ref[s, :, N_HEADS:2 * N_HEADS], alog_ref[...], dtb_ref[...])
            gates_vjps.append(gates_vjp)

        dinv = dinv_ref[0]
        _, vjp = jax.vjp(
            lambda *a: _GD_HEADS_AGAIN(*a, dinv), st_all_ref[0], _stack_all(c_sc), _stack_all(c_sc, N_HEADS),
            _stack_all(c_sc, 2 * N_HEADS), _stack_all(beta_sc), _stack_all(g_sc), _stack_all(g_sc, N_HEADS),
            _stack_all(p_ref, 3 * N_HEADS).astype(F32), onw_ref[...])
        dst_sc[...], dq, dk, dv, dbeta, dg, ddiff, dgate, donw = vjp((_stack_all(do2_ref).astype(F32), dst_sc[...]))
        _unstack_all(dc_sc, dq)
        _unstack_all(dc_sc, dk, N_HEADS)
        _unstack_all(dc_sc, dv, 2 * N_HEADS)
        _unstack_all(dbeta_sc, dbeta)
        _unstack_all(dg_sc, dg)
        _unstack_all(dg_sc, ddiff, N_HEADS)
        _unstack_all(dp_ref, dgate, 3 * N_HEADS)

        dalog, ddtb = None, None
        for s in range(seqs):
            def conv_bwd(at, s=s):
                sl = at()
                dxp, dcw = _gd_conv_bwd(_gd_xp(halo_ref.at[s], p_ref.at[s], sl, first_chunk), cw_ref[:, sl],
                                        y_ref[s, :, sl], dc_sc[s, :, sl])
                dqkv = jnp.concatenate([dxp[HALO:CHUNK], dxp[CHUNK:HALO + CHUNK] + dhalo_sc[s, :, sl]], axis=0)
                dp_ref[s, :, sl] = dqkv.astype(dp_ref.dtype)
                dhalo_sc[s, :, sl] = dxp[0:HALO]

                if s > 0:
                    dcw_ref[:, sl] += dcw
                    return

                @pl.when(first)
                def _():
                    dcw_ref[:, sl] = dcw

                @pl.when(jnp.logical_not(first))
                def _():
                    dcw_ref[:, sl] += dcw

            _lane_blocks(3 * d, conv_bwd)
            da, db, dalog_s, ddtb_s = gates_vjps[s]((dbeta_sc[s], dg_sc[s]))
            dab_ref[s] = jnp.concatenate(
                [da, db, jnp.zeros((CHUNK, AB_PAD - 2 * N_HEADS), F32)], axis=1).astype(dab_ref.dtype)
            dalog = dalog_s if dalog is None else dalog + dalog_s
            ddtb = ddtb_s if ddtb is None else ddtb + ddtb_s

        @pl.when(first)
        def _():
            dalog_ref[...] = dalog
            ddtb_ref[...] = ddtb
            donw_ref[...] = donw

        @pl.when(jnp.logical_not(first))
        def _():
            dalog_ref[...] += dalog
            ddtb_ref[...] += ddtb
            donw_ref[...] += donw

    back = lambda c: nc - 1 - c
    rows = lambda c: (0, back(c), 0)
    const = lambda c: (0, 0)
    per_chunk = lambda c: (back(c), 0, 0, 0)
    small = [pl.BlockSpec((CONV_K, 3 * d), const), pl.BlockSpec((1, N_HEADS), const),
             pl.BlockSpec((1, N_HEADS), const), pl.BlockSpec((1, HEAD_DIM), const)]
    p3 = projm.reshape(seqs, t, 4 * d)
    dp, dab, dcw, dalog, ddtb, donw = pl.pallas_call(
        body, name=name, grid=(nc,),
        in_specs=[pl.BlockSpec((seqs, CHUNK, 4 * d), rows),
                  pl.BlockSpec((seqs, HALO, 3 * d), lambda c: (0, jnp.maximum(back(c) * per_halo - 1, 0), 0)),
                  pl.BlockSpec((seqs, CHUNK, AB_PAD), rows), pl.BlockSpec((seqs, CHUNK, 3 * d), rows)] + small + [
                  pl.BlockSpec((1, nh, HEAD_DIM, HEAD_DIM), per_chunk), pl.BlockSpec((1, nh, CHUNK, CHUNK), per_chunk),
                  pl.BlockSpec((seqs, CHUNK, d), rows)],
        out_specs=[pl.BlockSpec((seqs, CHUNK, 4 * d), rows), pl.BlockSpec((seqs, CHUNK, AB_PAD), rows)] + small,
        out_shape=[jax.ShapeDtypeStruct((seqs, t, 4 * d), BF16), jax.ShapeDtypeStruct((seqs, t, AB_PAD), BF16),
                   jax.ShapeDtypeStruct((CONV_K, 3 * d), F32), jax.ShapeDtypeStruct((1, N_HEADS), F32),
                   jax.ShapeDtypeStruct((1, N_HEADS), F32), jax.ShapeDtypeStruct((1, HEAD_DIM), F32)],
        scratch_shapes=[pltpu.VMEM((nh, HEAD_DIM, HEAD_DIM), F32), pltpu.VMEM((seqs, HALO, 3 * d), F32),
                        pltpu.VMEM((seqs, CHUNK, 3 * d), F32), pltpu.VMEM((seqs, CHUNK, d), F32),
                        pltpu.VMEM((seqs, CHUNK, 2 * d), F32), pltpu.VMEM((seqs, CHUNK, 3 * d), F32),
                        pltpu.VMEM((seqs, CHUNK, d), F32), pltpu.VMEM((seqs, CHUNK, 2 * d), F32)],
        compiler_params=_params("arbitrary"),
    )(p3, p3, projab.reshape(seqs, t, AB_PAD), conv_y, cw, alog, dtb, onw, st_all, dinv_all,
      do2.reshape(seqs, t, d))
    return dp.reshape(n, 4 * d), dab.reshape(n, AB_PAD), dcw, dalog, ddtb, donw


def _hgrn_fwd_all(proj, lb, gw, seqs, name):
    n = proj.shape[0]
    t = n // seqs
    nc = t // CHUNK
    d = D_MODEL
    nh = seqs * N_HEADS
    sums, masks = _hg_level_sums(), _hg_level_masks()

    def body(p_ref, lb_ref, gw_ref, sums_wide_ref, sums_once_ref, masks_ref, o2_ref, o_ref, st_all_ref,
             st_sc, q_sc, k_sc, v_sc, e_sc):
        @pl.when(pl.program_id(0) == 0)
        def _():
            st_sc[...] = jnp.zeros_like(st_sc)

        sums_refs = (sums_wide_ref, sums_once_ref)
        for s in range(seqs):
            _lane_blocks(d, functools.partial(_hg_pre_block, p_ref.at[s], lb_ref, sums_refs, q_sc.at[s], k_sc.at[s],
                                              v_sc.at[s], e_sc.at[s]))
        st_all_ref[0] = st_sc[...]
        for s in range(seqs):
            one, mine = pl.ds(s, 1), pl.ds(s * N_HEADS, N_HEADS)
            o, st_sc[mine] = _HG_HEADS(st_sc[mine], *[_stack_all(r.at[one]) for r in (q_sc, k_sc, v_sc, e_sc)],
                                       masks_ref[...])
            _unstack_all(o_ref.at[one], o)

            def post(rows, s=s):
                gate = p_ref[s, rows, 3 * d:4 * d].astype(F32)
                o2_ref[s, rows, :] = _hg_post(o_ref[s, rows, :], gate, gw_ref[...]).astype(o2_ref.dtype)

            _row_blocks(CHUNK, post)

    rows = lambda c: (0, c, 0)
    vec = pl.BlockSpec((1, d), lambda c: (0, 0))
    act = pl.BlockSpec((seqs, CHUNK, d), rows)
    o2, o, st_all = pl.pallas_call(
        body, name=name, grid=(nc,),
        in_specs=[pl.BlockSpec((seqs, CHUNK, 4 * d), rows), vec, vec]
        + [pl.BlockSpec(m.shape, lambda c: (0, 0)) for m in sums] + [pl.BlockSpec(masks.shape, lambda c: (0, 0, 0))],
        out_specs=[act, act, pl.BlockSpec((1, nh, HEAD_DIM, HEAD_DIM), lambda c: (c, 0, 0, 0))],
        out_shape=[jax.ShapeDtypeStruct((seqs, t, d), BF16), jax.ShapeDtypeStruct((seqs, t, d), F32),
                   jax.ShapeDtypeStruct((nc, nh, HEAD_DIM, HEAD_DIM), F32)],
        scratch_shapes=[pltpu.VMEM((nh, HEAD_DIM, HEAD_DIM), F32)] + [pltpu.VMEM((seqs, CHUNK, d), F32)] * 3
        + [pltpu.VMEM((seqs, sums[0].shape[0], d), F32)],
        compiler_params=_params("arbitrary"),
    )(proj.reshape(seqs, t, 4 * d), lb, gw, *sums, masks)
    return o2.reshape(n, d), o, st_all


def _hgrn_bwd_all(proj, lb, gw, st_all, o, do2, seqs, name):
    n = proj.shape[0]
    t = n // seqs
    nc = t // CHUNK
    d = D_MODEL
    nh = seqs * N_HEADS
    sums, masks = _hg_level_sums(), _hg_level_masks()

    def body(p_ref, lb_ref, gw_ref, sums_wide_ref, sums_once_ref, masks_ref, st_all_ref, o_ref, do2_ref,
             dp_ref, dlb_ref, dgw_ref,
             dst_sc, q_sc, k_sc, v_sc, e_sc, do_sc, dq_sc, dk_sc, dv_sc, de_sc, dgw_sc):
        first = pl.program_id(0) == 0

        @pl.when(first)
        def _():
            dst_sc[...] = jnp.zeros_like(dst_sc)

        sums_refs = (sums_wide_ref, sums_once_ref)
        dgw_sc[...] = jnp.zeros_like(dgw_sc)
        for s in range(seqs):
            _lane_blocks(d, functools.partial(_hg_pre_block, p_ref.at[s], lb_ref, sums_refs, q_sc.at[s], k_sc.at[s],
                                              v_sc.at[s], e_sc.at[s]))

            def post_bwd(rows, s=s):
                _, vjp = jax.vjp(_hg_post, o_ref[s, rows, :], p_ref[s, rows, 3 * d:4 * d].astype(F32), gw_ref[...])
                do_sc[s, rows, :], dgate, dgw = vjp(do2_ref[s, rows, :].astype(F32))
                dp_ref[s, rows, 3 * d:4 * d] = dgate.astype(dp_ref.dtype)
                dgw_sc[...] += dgw

            _row_blocks(CHUNK, post_bwd)

        level_masks = masks_ref[...]
        _, vjp = jax.vjp(lambda *a: _HG_HEADS(*a, level_masks), st_all_ref[0],
                         *[_stack_all(r) for r in (q_sc, k_sc, v_sc, e_sc)])
        grads = vjp((_stack_all(do_sc), dst_sc[...]))
        dst_sc[...] = grads[0]
        for r, val in zip((dq_sc, dk_sc, dv_sc, de_sc), grads[1:]):
            _unstack_all(r, val)

        for s in range(seqs):
            def pre_bwd(at, s=s):
                sl = at()
                level_sums = (sums_wide_ref[...], sums_once_ref[...])
                _, vjp = jax.vjp(lambda qraw, f, lb: _hg_pre(qraw, f, lb, level_sums), p_ref[s, :, sl].astype(F32),
                                 p_ref[s, :, at(d)].astype(F32), lb_ref[:, sl])
                dqraw, df, dlb = vjp((dq_sc[s, :, sl], dk_sc[s, :, sl], de_sc[s, :, sl]))
                dp_ref[s, :, sl] = dqraw.astype(dp_ref.dtype)
                dp_ref[s, :, at(d)] = df.astype(dp_ref.dtype)
                dp_ref[s, :, at(2 * d)] = dv_sc[s, :, sl].astype(dp_ref.dtype)
                if s > 0:
                    dlb_ref[:, sl] += dlb
                    return

                @pl.when(first)
                def _():
                    dlb_ref[:, sl] = dlb

                @pl.when(jnp.logical_not(first))
                def _():
                    dlb_ref[:, sl] += dlb

            _lane_blocks(d, pre_bwd)

        @pl.when(first)
        def _():
            dgw_ref[...] = dgw_sc[...]

        @pl.when(jnp.logical_not(first))
        def _():
            dgw_ref[...] += dgw_sc[...]

    rows = lambda c: (0, nc - 1 - c, 0)
    vec = pl.BlockSpec((1, d), lambda c: (0, 0))
    act = pl.BlockSpec((seqs, CHUNK, d), rows)
    wide = pl.BlockSpec((seqs, CHUNK, 4 * d), rows)
    e_rows = sums[0].shape[0]
    dp, dlb, dgw = pl.pallas_call(
        body, name=name, grid=(nc,),
        in_specs=[wide, vec, vec] + [pl.BlockSpec(m.shape, lambda c: (0, 0)) for m in sums] + [
                  pl.BlockSpec(masks.shape, lambda c: (0, 0, 0)),
                  pl.BlockSpec((1, nh, HEAD_DIM, HEAD_DIM), lambda c: (nc - 1 - c, 0, 0, 0)), act, act],
        out_specs=[wide, vec, vec],
        out_shape=[jax.ShapeDtypeStruct((seqs, t, 4 * d), BF16), jax.ShapeDtypeStruct((1, d), F32),
                   jax.ShapeDtypeStruct((1, d), F32)],
        scratch_shapes=[pltpu.VMEM((nh, HEAD_DIM, HEAD_DIM), F32)]
        + [pltpu.VMEM((seqs, CHUNK, d), F32)] * 3 + [pltpu.VMEM((seqs, e_rows, d), F32)]
        + [pltpu.VMEM((seqs, CHUNK, d), F32)] * 4 + [pltpu.VMEM((seqs, e_rows, d), F32), pltpu.VMEM((1, d), F32)],
        compiler_params=_params("arbitrary"),
    )(proj.reshape(seqs, t, 4 * d), lb, gw, *sums, masks, st_all, o, do2.reshape(seqs, t, d))
    return dp.reshape(n, 4 * d), dlb, dgw


def _adam_update(w, g, m, v):
    b1c = 1.0 - ADAM_B1 ** ADAM_STEP
    b2c = 1.0 - ADAM_B2 ** ADAM_STEP
    m_new = ADAM_B1 * m + (1.0 - ADAM_B1) * g
    v_new = ADAM_B2 * v + (1.0 - ADAM_B2) * (g * g)
    delta = -ADAM_LR * ((m_new / b1c) / (jnp.sqrt(v_new / b2c) + ADAM_EPS) + ADAM_WD * w)
    return delta, m_new, v_new


def _adamw(w, g, m, v, name, tr=256):
    r, c = w.shape
    tr = _tile(r, tr)

    def body(w_ref, g_ref, m_ref, v_ref, d_ref, mo_ref, vo_ref):
        d_ref[...], mo_ref[...], vo_ref[...] = _adam_update(w_ref[...], g_ref[...], m_ref[...], v_ref[...])

    blk = pl.BlockSpec((tr, c), lambda i: (i, 0))
    return pl.pallas_call(
        body, name=name, grid=(r // tr,),
        in_specs=[blk] * 4, out_specs=[blk] * 3,
        out_shape=[jax.ShapeDtypeStruct((r, c), F32)] * 3,
        compiler_params=_params("arbitrary"),
    )(w, g, m, v)


def _adamw_slots(w, slot_bufs, m, v, name, tr=256):
    nl, r, c = w.shape
    tr = _tile(r, tr)

    def body(*refs):
        w_ref = refs[0]
        g_refs = refs[1:1 + nl]
        m_ref, v_ref, go_ref, d_ref, mo_ref, vo_ref = refs[1 + nl:]
        for k in range(nl):
            @pl.when(pl.program_id(0) == k)
            def _(k=k):
                g = g_refs[k][0].astype(F32)
                for s in range(1, N_DEV):
                    g = g + g_refs[k][s].astype(F32)
                go_ref[0] = g

        d_ref[0], mo_ref[0], vo_ref[0] = _adam_update(w_ref[0], go_ref[0], m_ref[0], v_ref[0])

    blk = pl.BlockSpec((1, tr, c), lambda l, i: (l, i, 0))
    g_specs = [pl.BlockSpec((N_DEV, tr, c), lambda l, i, k=k: (0, jnp.where(l == k, i, 0), 0)) for k in range(nl)]
    return pl.pallas_call(
        body, name=name, grid=(nl, r // tr),
        in_specs=[blk] + g_specs + [blk, blk], out_specs=[blk] * 4,
        out_shape=[jax.ShapeDtypeStruct((nl, r, c), F32)] * 4,
        compiler_params=_params("arbitrary", "arbitrary"),
    )(w, *slot_bufs, m, v)


def _mesh_pos():
    return lax.axis_index("x"), lax.axis_index("y"), lax.axis_index("c")


def _flip(pos, p):
    x, y, c = pos
    return ((1 - x) if p & 4 else x, (1 - y) if p & 2 else y, (1 - c) if p & 1 else c)


def _lin(pos):
    return 4 * pos[0] + 2 * pos[1] + pos[2]


_HBM = pl.BlockSpec(memory_space=pltpu.HBM)
_SEM = pl.BlockSpec(memory_space=pltpu.SEMAPHORE)
_DATAFLOW = pltpu.SideEffectType.DATAFLOW_SIDE_EFFECTING


class _Item:
    def __init__(self, src, land_shape, src_pick, dst_pick, peers=tuple(range(1, N_DEV))):
        self.src, self.land_shape, self.src_pick, self.dst_pick = src, land_shape, src_pick, dst_pick
        self.peers = peers


def _remote_copies(items, src, land, send_sem, recv_sem, me, arriving):
    me_i = _lin(me)
    out = []
    for it, s_ref, l_ref in zip(items, src, land):
        for p in it.peers:
            peer = _flip(me, p)
            out.append(pltpu.make_async_remote_copy(
                src_ref=it.src_pick(s_ref, _lin(peer)),
                dst_ref=it.dst_pick(l_ref, _lin(peer) if arriving else me_i),
                send_sem=send_sem, recv_sem=recv_sem, device_id=peer, device_id_type=pl.DeviceIdType.MESH))
    return out


def _own_copies(items, src, land, sem, me):
    me_i = _lin(me)
    return [pltpu.make_async_copy(it.src_pick(s_ref, me_i), it.dst_pick(l_ref, me_i), sem)
            for it, s_ref, l_ref in zip(items, src, land)]


def _exchange_start(groups, name):
    items = [it for g in groups for it in g]
    n, ng = len(items), len(groups)
    first = [sum(len(g) for g in groups[:gi]) for gi in range(ng)]

    def body(*refs):
        src, land = refs[0:n], refs[n:2 * n]
        send_sems, recv_sems = refs[2 * n:2 * n + ng], refs[2 * n + ng:2 * n + 2 * ng]
        token = refs[4 * n + 2 * ng]
        me = _mesh_pos()
        for gi, g in enumerate(groups):
            sl = slice(first[gi], first[gi] + len(g))
            for cp in _remote_copies(g, src[sl], land[sl], send_sems[gi], recv_sems[gi], me, arriving=False):
                cp.start()
            for cp in _own_copies(g, src[sl], land[sl], recv_sems[gi], me):
                cp.start()
        token[...] = jnp.zeros_like(token)

    srcs = [pltpu.with_memory_space_constraint(it.src, pltpu.HBM) for it in items]
    lands = [pltpu.with_memory_space_constraint(lax.empty(it.land_shape, it.src.dtype), pltpu.HBM) for it in items]
    res = pl.pallas_call(
        body, name=name,
        out_shape=([pltpu.SemaphoreType.DMA(())] * (2 * ng)
                   + [pltpu.HBM(it.src.shape, it.src.dtype) for it in items]
                   + [pltpu.HBM(it.land_shape, it.src.dtype) for it in items]
                   + [jax.ShapeDtypeStruct((8, 128), F32)]),
        in_specs=[_HBM] * (2 * n),
        out_specs=[_SEM] * (2 * ng) + [_HBM] * (2 * n) + [pl.BlockSpec(memory_space=pltpu.VMEM)],
        input_output_aliases={i: 2 * ng + i for i in range(2 * n)},
        compiler_params=pltpu.CompilerParams(has_side_effects=_DATAFLOW),
    )(*srcs, *lands)
    send_sems, recv_sems = res[0:ng], res[ng:2 * ng]
    src_thru, land_thru = res[2 * ng:2 * ng + n], res[2 * ng + n:2 * ng + 2 * n]
    handles = []
    for gi, g in enumerate(groups):
        sl = slice(first[gi], first[gi] + len(g))
        handles.append((g, src_thru[sl], land_thru[sl], send_sems[gi], recv_sems[gi]))
    return handles, res[-1]


def _exchange_wait(handle, after, name):
    items, src_thru, land_thru, send_sem, recv_sem = handle
    k = len(items)

    def body(*refs):
        src, land = refs[0:k], refs[k:2 * k]
        send_ref, recv_ref = refs[2 * k], refs[2 * k + 1]
        for cp in _remote_copies(items, src, land, send_ref, recv_ref, _mesh_pos(), arriving=True):
            cp.wait_send()
            cp.wait_recv()
        for cp in _own_copies(items, src, land, recv_ref, _mesh_pos()):
            cp.wait()

    res = pl.pallas_call(
        body, name=name,
        out_shape=([pltpu.HBM(s.shape, s.dtype) for s in src_thru] + [pltpu.HBM(l.shape, l.dtype) for l in land_thru]),
        in_specs=[_HBM] * (2 * k) + [_SEM, _SEM, pl.BlockSpec(memory_space=pl.ANY)],
        out_specs=[_HBM] * (2 * k),
        input_output_aliases={i: i for i in range(2 * k)},
        compiler_params=pltpu.CompilerParams(has_side_effects=_DATAFLOW),
    )(*src_thru, *land_thru, send_sem, recv_sem, after)
    return res[k:2 * k]


SAME_CORE = (2, 4, 6)
SIBLING = 1


def _pass_on_start(buf, name):
    def body(buf_ref, send_sem, recv_sem, thru_ref):
        me = _mesh_pos()
        for p in SAME_CORE:
            slot = buf_ref.at[_lin(_flip(me, p))]
            pltpu.make_async_remote_copy(src_ref=slot, dst_ref=slot, send_sem=send_sem, recv_sem=recv_sem,
                                         device_id=_flip(me, SIBLING), device_id_type=pl.DeviceIdType.MESH).start()

    return pl.pallas_call(
        body, name=name,
        out_shape=[pltpu.SemaphoreType.DMA(()), pltpu.SemaphoreType.DMA(()), pltpu.HBM(buf.shape, buf.dtype)],
        in_specs=[_HBM], out_specs=[_SEM, _SEM, _HBM], input_output_aliases={0: 2},
        compiler_params=pltpu.CompilerParams(has_side_effects=_DATAFLOW),
    )(pltpu.with_memory_space_constraint(buf, pltpu.HBM))


def _pass_on_wait(handle, name):
    send_sem, recv_sem, thru = handle

    def body(buf_ref, send_ref, recv_ref, out_ref):
        me = _mesh_pos()
        sibling = _flip(me, SIBLING)
        for p in SAME_CORE:
            mine, theirs = buf_ref.at[_lin(_flip(me, p))], buf_ref.at[_lin(_flip(sibling, p))]
            cp = pltpu.make_async_remote_copy(src_ref=mine, dst_ref=theirs, send_sem=send_ref, recv_sem=recv_ref,
                                              device_id=sibling, device_id_type=pl.DeviceIdType.MESH)
            cp.wait_send()
            cp.wait_recv()

    return pl.pallas_call(
        body, name=name, out_shape=pltpu.HBM(thru.shape, thru.dtype),
        in_specs=[_HBM, _SEM, _SEM], out_specs=_HBM, input_output_aliases={0: 0},
        compiler_params=pltpu.CompilerParams(has_side_effects=_DATAFLOW),
    )(thru, send_sem, recv_sem)


def _whole(ref, i):
    return ref


def _slot(ref, i):
    return ref.at[i]


def _rows_of(r):
    return lambda ref, i: ref.at[pl.ds(pl.multiple_of(i * r, r), r), :]


def _cols_of(c):
    return lambda ref, i: ref.at[:, pl.ds(pl.multiple_of(i * c, c), c)]


def _all_reduce_small(buf, after, name):
    r, c = buf.shape

    def body(src_ref, after_ref, out_ref, all_ref, send_sems, recv_sems):
        me = _mesh_pos()
        me_i = _lin(me)
        all_ref[me_i] = src_ref[...]
        for p in range(1, N_DEV):
            peer = _flip(me, p)
            pltpu.make_async_remote_copy(
                src_ref=src_ref, dst_ref=all_ref.at[me_i], send_sem=send_sems.at[p - 1], recv_sem=recv_sems.at[p - 1],
                device_id=peer, device_id_type=pl.DeviceIdType.MESH).start()
        for p in range(1, N_DEV):
            peer = _flip(me, p)
            cp = pltpu.make_async_remote_copy(
                src_ref=src_ref, dst_ref=all_ref.at[_lin(peer)], send_sem=send_sems.at[p - 1],
                recv_sem=recv_sems.at[p - 1], device_id=peer, device_id_type=pl.DeviceIdType.MESH)
            cp.wait_recv()
            cp.wait_send()
        acc = all_ref[0]
        for s in range(1, N_DEV):
            acc = acc + all_ref[s]
        out_ref[...] = acc

    vm = pl.BlockSpec(memory_space=pltpu.VMEM)
    return pl.pallas_call(
        body, name=name, in_specs=[vm, pl.BlockSpec(memory_space=pl.ANY)], out_specs=vm,
        out_shape=jax.ShapeDtypeStruct((r, c), F32),
        scratch_shapes=[pltpu.VMEM((N_DEV, r, c), F32), pltpu.SemaphoreType.DMA((N_DEV - 1,)),
                        pltpu.SemaphoreType.DMA((N_DEV - 1,))],
        compiler_params=pltpu.CompilerParams(has_side_effects=True),
    )(buf, after)


def _unshard_cols(g):
    s, l, r, c = g.shape
    return jnp.transpose(g, (1, 2, 0, 3)).reshape(l, r, s * c)


def kernel(x, gdn_w_in, gdn_conv, gdn_a_log, gdn_dt_bias, gdn_onorm, gdn_w_out, hgrn_w_in, hgrn_lb_logits, hgrn_gnorm, hgrn_w_out, norm_mix, norm_mlp, mlp_w_up, mlp_w_down, norm_final, loss_target, m_gdn_w_in, m_gdn_conv, m_gdn_a_log, m_gdn_dt_bias, m_gdn_onorm, m_gdn_w_out, m_hgrn_w_in, m_hgrn_lb_logits, m_hgrn_gnorm, m_hgrn_w_out, m_norm_mix, m_norm_mlp, m_mlp_w_up, m_mlp_w_down, m_norm_final, v_gdn_w_in, v_gdn_conv, v_gdn_a_log, v_gdn_dt_bias, v_gdn_onorm, v_gdn_w_out, v_hgrn_w_in, v_hgrn_lb_logits, v_hgrn_gnorm, v_hgrn_w_out, v_norm_mix, v_norm_mlp, v_mlp_w_up, v_mlp_w_down, v_norm_final):
    seqs, seq_len, d = x.shape
    n = seqs * seq_len
    me_i = _lin(_mesh_pos())
    x2 = x.reshape(n, d)
    target = loss_target.reshape(n, d)
    n_gdn, n_hgrn = gdn_w_in.shape[0], hgrn_w_in.shape[0]

    r_out, r_down = gdn_w_out.shape[1], mlp_w_down.shape[1]
    c_gin, c_hin, c_up = gdn_w_in.shape[2], hgrn_w_in.shape[2], mlp_w_up.shape[2]

    def gathered(w, pick, land_shape, **kw):
        return _Item(w.astype(BF16), land_shape, _whole, pick, **kw)

    groups = [[_Item(gdn_conv, (N_DEV,) + gdn_conv.shape, _whole, _slot),
               _Item(hgrn_gnorm, (N_DEV,) + hgrn_gnorm.shape, _whole, _slot)]]
    for i in range(DEPTH):
        j = i // 2
        if i % 2 == 0:
            direct = (SIBLING,) + SAME_CORE if i == 0 else tuple(range(1, N_DEV))
            groups += [[gathered(gdn_w_in[j], _slot, (N_DEV, d, c_gin), peers=direct)],
                       [gathered(gdn_w_out[j], _rows_of(r_out), (N_DEV * r_out, d))]]
        else:
            groups += [[gathered(hgrn_w_in[j], _cols_of(c_hin), (d, N_DEV * c_hin))],
                       [gathered(hgrn_w_out[j], _rows_of(r_out), (N_DEV * r_out, d))]]
        groups += [[gathered(mlp_w_up[i], _cols_of(c_up), (d, N_DEV * c_up))],
                   [gathered(mlp_w_down[i], _rows_of(r_down), (N_DEV * r_down, d))]]
    gather_handles, token = _exchange_start(groups, "gather_start")
    lbs = _lb_fwd(hgrn_lb_logits + token[0:1, 0:1], "lb_fwd")

    def arrived(k, after, name):
        return _exchange_wait(gather_handles[k], after, "gather_wait_" + name)

    saved = []
    w_in, w_ab, w_out, w_up, w_down = ([None] * DEPTH for _ in range(5))
    h = x2
    for i in range(DEPTH):
        j = i // 2
        if i == 0:
            g_conv, g_gnorm = arrived(0, h, "small")
            conv_full = _unshard_cols(g_conv)
            gnorm_full = jnp.transpose(g_gnorm, (1, 0, 2)).reshape(n_hgrn, d)
        if i == 0:
            y = _rms_fwd(h, norm_mix[0:1], "rms_mix_0")
        (w_in[i],) = arrived(1 + 4 * i, y, f"in_{i}")
        if i == 0:
            w_in[i] = _pass_on_wait(_pass_on_start(w_in[i], "pass_on_start_in_0"), "pass_on_wait_in_0")
        if i % 2 == 0:
            w_gin = jnp.transpose(w_in[i], (1, 0, 2)).reshape(d, N_DEV * c_gin)
            w_in[i] = w_gin[:, :GDN_MAIN]
            w_ab[i] = jnp.pad(w_gin[:, GDN_MAIN:], ((0, 0), (0, AB_PAD - 2 * N_HEADS)))
            projm = _mm(y, w_in[i], "nn", [BF16], f"gdn_proj_{i}")
            projab = _mm(y, w_ab[i], "nn", [F32], f"gdn_proj_ab_{i}")
            o2, st_all, conv_y, dinv_all = _gdn_fwd_all(projm, projab, conv_full[j], gdn_a_log[j:j + 1],
                                                    gdn_dt_bias[j:j + 1], gdn_onorm[j:j + 1], seqs, f"gdn_fwd_{i}")
            mix = (projm, projab, conv_y, st_all, dinv_all)
        else:
            proj = _mm(y, w_in[i], "nn", [BF16], f"hgrn_proj_{i}")
            o2, o_raw, st_all = _hgrn_fwd_all(proj, lbs[i:i + 1], gnorm_full[j:j + 1], seqs, f"hgrn_fwd_{i}")
            mix = (proj, o_raw, st_all)
        (w_out[i],) = arrived(2 + 4 * i, o2, f"out_{i}")
        h1, y2 = _mm_rows(o2, w_out[i], "nn", [F32, BF16], f"mix_out_{i}", epilogue=_ep_residual_norm, extras=(h,),
                     vectors=(norm_mlp[i:i + 1],))
        (w_up[i],) = arrived(3 + 4 * i, y2, f"up_{i}")
        u, act = _mm(y2, w_up[i], "nn", [BF16, BF16], f"mlp_up_{i}",
                     epilogue=lambda acc: (acc, jnp.square(jnp.maximum(acc, 0.0))))
        (w_down[i],) = arrived(4 + 4 * i, act, f"down_{i}")
        saved.append((h, y, mix, o2, h1, y2, u, act))
        if i + 1 < DEPTH:
            h, y = _mm_rows(act, w_down[i], "nn", [F32, BF16], f"mlp_down_{i}", epilogue=_ep_residual_norm, extras=(h1,),
                       vectors=(norm_mix[i + 1:i + 2],))
        else:
            h = _mm(act, w_down[i], "nn", [F32], f"mlp_down_{i}", epilogue=lambda acc, res: (res + acc,),
                    extras=(h1,))

    dh, dh_b, d_nf, sq = _loss_head(h, norm_final.reshape(1, d), target, "loss_head")

    d_nmix, d_nmlp = [None] * DEPTH, [None] * DEPTH
    d_conv, d_alog, d_dtb, d_onorm = [None] * n_gdn, [None] * n_gdn, [None] * n_gdn, [None] * n_gdn
    d_lb = [jnp.zeros((1, d), F32)] * DEPTH
    d_gnorm = [None] * n_hgrn
    mlp_handles, mix_handles = [None] * DEPTH, [None] * DEPTH
    token = None
    for i in reversed(range(DEPTH)):
        j = i // 2
        h_in, y, mix, o2, h1, y2, u, act = saved[i]
        g_down = _mm(act, dh_b, "tn", [BF16], f"g_down_{i}", after=token)
        du = _mm(dh_b, w_down[i], "nt", [BF16], f"d_u_{i}",
                 epilogue=lambda acc, uu: (acc * (2.0 * jnp.maximum(uu.astype(F32), 0.0)),), extras=(u,))
        g_up = _mm(y2, du, "tn", [BF16], f"g_up_{i}")
        mlp_handles[i], token = _exchange_start(
            [[_Item(g_down, (N_DEV, r_down, d), _rows_of(r_down), _slot)],
             [_Item(g_up, (N_DEV, d, c_up), _cols_of(c_up), _slot)]], f"scatter_start_mlp_{i}")
        dh1, dh1_b, d_nmlp[i] = _mm_rows(du, w_up[i], "nt", [F32, BF16], f"d_y2_{i}", epilogue=_ep_norm_bwd,
                                     extras=(h1, dh), vectors=(norm_mlp[i:i + 1],), n_sums=1, after=token)
        g_out = _mm(o2, dh1_b, "tn", [BF16], f"g_out_{i}")
        do2 = _mm(dh1_b, w_out[i], "nt", [BF16], f"d_o2_{i}")
        if i % 2 == 0:
            projm, projab, conv_y, st_all, dinv_all = mix
            dpm, dpab, d_conv[j], d_alog[j], d_dtb[j], d_onorm[j] = _gdn_bwd_all(
                projm, projab, conv_y, conv_full[j], gdn_a_log[j:j + 1], gdn_dt_bias[j:j + 1], gdn_onorm[j:j + 1],
                st_all, dinv_all, do2, seqs, f"gdn_bwd_{i}")
            g_main = _mm(y, dpm, "tn", [BF16], f"g_in_{i}")
            g_ab = _mm(y, dpab, "tn", [BF16], f"g_in_ab_{i}")
            g_in = jnp.concatenate([g_main, g_ab[:, :2 * N_HEADS]], axis=1)
            g_in = jnp.transpose(g_in.reshape(d, N_DEV, c_gin), (1, 0, 2))
            in_item = _Item(g_in, (N_DEV, d, c_gin), _slot, _slot)
            dy_ab = _mm(dpab, w_ab[i], "nt", [F32], f"d_y_ab_{i}")
            dp, dy_extras = dpm, (dy_ab, h_in, dh1)
            dy_epilogue = lambda acc, e, xx, dres, w: _ep_norm_bwd(acc + e, xx, dres, w)
        else:
            proj, o_raw, st_all = mix
            dp, d_lb[i], d_gnorm[j] = _hgrn_bwd_all(proj, lbs[i:i + 1], gnorm_full[j:j + 1], st_all, o_raw, do2,
                                               seqs, f"hgrn_bwd_{i}")
            g_in = _mm(y, dp, "tn", [BF16], f"g_in_{i}")
            in_item = _Item(g_in, (N_DEV, d, c_hin), _cols_of(c_hin), _slot)
            dy_extras, dy_epilogue = (h_in, dh1), _ep_norm_bwd
        mix_handles[i], token = _exchange_start(
            [[_Item(g_out, (N_DEV, r_out, d), _rows_of(r_out), _slot)], [in_item]], f"scatter_start_mix_{i}")
        dh, dh_b, d_nmix[i] = _mm_rows(dp, w_in[i], "nt", [F32, BF16], f"d_y_{i}", epilogue=dy_epilogue, extras=dy_extras,
                                  vectors=(norm_mix[i:i + 1],), n_sums=1, after=token)
        token = None
    grad_x = dh.reshape(x.shape)

    def landed(handles, k, layers, after, name):
        return [_exchange_wait(handles[i][k], after, f"scatter_wait_{name}_{i}")[0] for i in layers]

    every, even, odd = range(DEPTH), range(0, DEPTH, 2), range(1, DEPTH, 2)
    upd = {}
    upd["mlp_w_down"] = _adamw_slots(mlp_w_down, landed(mlp_handles, 0, every, dh, "down"), m_mlp_w_down,
                                     v_mlp_w_down, "adamw_mlp_w_down")
    upd["mlp_w_up"] = _adamw_slots(mlp_w_up, landed(mlp_handles, 1, every, upd["mlp_w_down"][1], "up"), m_mlp_w_up,
                                   v_mlp_w_up, "adamw_mlp_w_up")
    upd["hgrn_w_out"] = _adamw_slots(hgrn_w_out, landed(mix_handles, 0, odd, upd["mlp_w_up"][1], "out"),
                                     m_hgrn_w_out, v_hgrn_w_out, "adamw_hgrn_w_out")
    upd["hgrn_w_in"] = _adamw_slots(hgrn_w_in, landed(mix_handles, 1, odd, upd["hgrn_w_out"][1], "in"), m_hgrn_w_in,
                                    v_hgrn_w_in, "adamw_hgrn_w_in")

    dlb_rows = jnp.concatenate(d_lb, axis=0)
    tail = jnp.concatenate(
        [jnp.concatenate(d_onorm, axis=1), jnp.concatenate(d_alog, axis=1), jnp.concatenate(d_dtb, axis=1)], axis=1)
    tail = jnp.pad(tail, ((0, 0), (0, d - tail.shape[1])))
    conv_rows = jnp.stack(d_conv).reshape(-1, d)
    packed = jnp.concatenate(
        [jnp.concatenate(d_nmix, axis=0), jnp.concatenate(d_nmlp, axis=0), d_nf, sq, dlb_rows,
         jnp.concatenate(d_gnorm, axis=0), tail, conv_rows], axis=0)
    pad_rows = (-packed.shape[0]) % 8
    packed = jnp.pad(packed, ((0, pad_rows), (0, 0)))
    tot = _all_reduce_small(packed, upd["hgrn_w_in"][1], "reduce_small")

    upd["gdn_w_out"] = _adamw_slots(gdn_w_out, landed(mix_handles, 0, even, tot, "out"),
                                    m_gdn_w_out, v_gdn_w_out, "adamw_gdn_w_out")
    upd["gdn_w_in"] = _adamw_slots(gdn_w_in, landed(mix_handles, 1, even, upd["gdn_w_out"][1], "in"), m_gdn_w_in,
                                   v_gdn_w_in, "adamw_gdn_w_in")

    def update(name, w, g, m, v):
        shape = w.shape
        c = shape[-1]
        res = _adamw(w.reshape(-1, c), g.reshape(-1, c), m.reshape(-1, c), v.reshape(-1, c), "adamw_" + name)
        return [g.reshape(shape)] + [o.reshape(shape) for o in res]

    r0 = 0
    g_nmix = tot[r0:r0 + DEPTH]; r0 += DEPTH
    g_nmlp = tot[r0:r0 + DEPTH]; r0 += DEPTH
    g_nf = tot[r0]; r0 += 1
    loss = tot[r0, 0]; r0 += 1
    g_lb = _lb_bwd(hgrn_lb_logits, tot[r0:r0 + DEPTH], "lb_bwd"); r0 += DEPTH
    g_gnorm_full = tot[r0:r0 + n_hgrn]; r0 += n_hgrn
    t_row = tot[r0]; r0 += 1
    g_conv_full = tot[r0:r0 + n_gdn * CONV_K * 3].reshape(n_gdn, CONV_K, 3 * d)
    g_onorm = t_row[0:n_gdn * HEAD_DIM].reshape(n_gdn, HEAD_DIM)
    o1 = n_gdn * HEAD_DIM
    g_alog = t_row[o1:o1 + n_gdn * N_HEADS].reshape(n_gdn, N_HEADS)
    g_dtb = t_row[o1 + n_gdn * N_HEADS:o1 + 2 * n_gdn * N_HEADS].reshape(n_gdn, N_HEADS)
    c_gn, c_cv = hgrn_gnorm.shape[1], gdn_conv.shape[2]
    g_gnorm = lax.dynamic_slice_in_dim(g_gnorm_full, me_i * c_gn, c_gn, axis=1)
    g_conv = lax.dynamic_slice_in_dim(g_conv_full, me_i * c_cv, c_cv, axis=2)

    upd["gdn_conv"] = update("gdn_conv", gdn_conv, g_conv, m_gdn_conv, v_gdn_conv)
    upd["gdn_a_log"] = update("gdn_a_log", gdn_a_log, g_alog, m_gdn_a_log, v_gdn_a_log)
    upd["gdn_dt_bias"] = update("gdn_dt_bias", gdn_dt_bias, g_dtb, m_gdn_dt_bias, v_gdn_dt_bias)
    upd["gdn_onorm"] = update("gdn_onorm", gdn_onorm, g_onorm, m_gdn_onorm, v_gdn_onorm)
    upd["hgrn_lb_logits"] = update("hgrn_lb_logits", hgrn_lb_logits, g_lb, m_hgrn_lb_logits, v_hgrn_lb_logits)
    upd["hgrn_gnorm"] = update("hgrn_gnorm", hgrn_gnorm, g_gnorm, m_hgrn_gnorm, v_hgrn_gnorm)
    upd["norm_mix"] = update("norm_mix", norm_mix, g_nmix, m_norm_mix, v_norm_mix)
    upd["norm_mlp"] = update("norm_mlp", norm_mlp, g_nmlp, m_norm_mlp, v_norm_mlp)
    upd["norm_final"] = update("norm_final", norm_final, g_nf, m_norm_final, v_norm_final)

    order = ["gdn_w_in", "gdn_conv", "gdn_a_log", "gdn_dt_bias", "gdn_onorm", "gdn_w_out", "hgrn_w_in",
             "hgrn_lb_logits", "hgrn_gnorm", "hgrn_w_out", "norm_mix", "norm_mlp", "mlp_w_up", "mlp_w_down",
             "norm_final"]
    outs = [loss, grad_x]
    for k in range(4):
        outs += [upd[name][k] for name in order]
    return tuple(outs)
```

```python
import functools

import numpy as np
import jax
import jax.numpy as jnp
from jax import lax
from jax.experimental import pallas as pl
from jax.experimental.pallas import tpu as pltpu

F32 = jnp.float32
BF16 = jnp.bfloat16

D_MODEL = 1024
N_HEADS = 8
HEAD_DIM = 128
CHUNK = 64
CONV_K = 4
HALO = 16
EPS = 1e-6
DEPTH = 4
N_DEV = 8
GDN_MAIN = 4 * D_MODEL
AB_PAD = 128
LANE_BLOCK = 256
ROW_BLOCK = 16
BLOCK_UNROLL = 4

ADAM_LR = 0.001
ADAM_B1 = 0.9
ADAM_B2 = 0.999
ADAM_EPS = 1e-08
ADAM_WD = 0.01
ADAM_STEP = 10

VMEM_LIMIT = 56 * 1024 * 1024
MM_TILE = 1024
MM_VMEM_BUDGET = 40 * 1024 * 1024
MM_ROWS_TILE = 512
_DIMS = {
    "nn": (((1,), (0,)), ((), ())),
    "nt": (((1,), (1,)), ((), ())),
    "tn": (((0,), (0,)), ((), ())),
}


def _parts(x, n):
    if n == 1 and x.dtype == BF16:
        return [x]
    out = []
    r = x.astype(F32)
    for i in range(n):
        p = r.astype(BF16)
        out.append(p)
        if i + 1 < n:
            r = r - p.astype(F32)
    return out


def _dot_raw(a, b, mode, na, nb):
    ap, bp = _parts(a, na), _parts(b, nb)
    nmax = max(na, nb)
    pairs = [(i, j) for i in range(na) for j in range(nb) if i + j < nmax]
    ka = 0 if mode == "tn" else 1
    kb = 1 if mode == "nt" else 0
    xa = ap[0] if len(pairs) == 1 else jnp.concatenate([ap[i] for i, _ in pairs], axis=ka)
    xb = bp[0] if len(pairs) == 1 else jnp.concatenate([bp[j] for _, j in pairs], axis=kb)
    return lax.dot_general(xa, xb, _DIMS[mode], preferred_element_type=F32)


@functools.partial(jax.custom_vjp, nondiff_argnums=(2, 3, 4))
def _dot(a, b, mode, na, nb):
    return _dot_raw(a, b, mode, na, nb)


def _dot_fwd(a, b, mode, na, nb):
    return _dot_raw(a, b, mode, na, nb), (a, b)


def _dot_bwd(mode, na, nb, res, ct):
    a, b = res
    if mode == "nn":
        da = _dot_raw(ct, b, "nt", 1, 1)
        db = _dot_raw(a, ct, "tn", 1, 1)
    elif mode == "nt":
        da = _dot_raw(ct, b, "nn", 1, 1)
        db = _dot_raw(ct, a, "tn", 1, 1)
    else:
        da = _dot_raw(b, ct, "nt", 1, 1)
        db = _dot_raw(a, ct, "nn", 1, 1)
    return da.astype(a.dtype), db.astype(b.dtype)


_dot.defvjp(_dot_fwd, _dot_bwd)


N_EXACT = 3


@jax.custom_vjp
def _dot01(x, m_wide, m):
    return lax.dot_general(m_wide, jnp.concatenate(_parts(x, N_EXACT), axis=0), _DIMS["nn"], preferred_element_type=F32)


def _dot01_fwd(x, m_wide, m):
    return _dot01(x, m_wide, m), (m_wide, m)


def _dot01_bwd(res, ct):
    m_wide, m = res
    dx = lax.dot_general(m, ct.astype(BF16), _DIMS["tn"], preferred_element_type=F32)
    return dx, jnp.zeros_like(m_wide), jnp.zeros_like(m)


_dot01.defvjp(_dot01_fwd, _dot01_bwd)


def _thrice(m):
    return jnp.concatenate([m] * N_EXACT, axis=1).astype(BF16), m.astype(BF16)


def _iota2(shape, dim):
    return lax.broadcasted_iota(jnp.int32, shape, dim)


def _tril_f32(n):
    return (_iota2((n, n), 0) >= _iota2((n, n), 1)).astype(F32)


def _below_block(n, b):
    ri, ci = _iota2((n, n), 0) // b, _iota2((n, n), 1) // b
    return (ri == ci + 1) & (ri % 2 == 1)


def _half_inverses(L):
    n = L.shape[0]
    eye = (_iota2((n, n), 0) == _iota2((n, n), 1)).astype(F32)
    d = eye - jnp.where(_below_block(n, 1), L, 0.0)
    b = 2
    while 2 * b < n:
        e = jnp.where(_below_block(n, b), L, 0.0)
        d = d - _dot_raw(d, _dot_raw(e, d, "nn", 2, 2), "nn", 2, 2)
        b *= 2
    return d, jnp.where(_below_block(n, b), L, 0.0)


def _solve_with(d, e, rhs):
    y = _dot_raw(d, rhs, "nn", 2, 2)
    return y - _dot_raw(d, _dot_raw(e, y, "nn", 2, 2), "nn", 2, 2)


@jax.custom_vjp
def _solve_unit_lower(L, rhs, d):
    n = L.shape[0]
    return _solve_with(d, jnp.where(_below_block(n, n // 2), L, 0.0), rhs)


def _solve_fwd(L, rhs, d):
    n = L.shape[0]
    e = jnp.where(_below_block(n, n // 2), L, 0.0)
    sol = _solve_with(d, e, rhs)
    return sol, (d, e, sol)


def _solve_bwd(res, ct):
    d, e, sol = res
    y = _dot_raw(d, ct - _dot_raw(e, _dot_raw(d, ct, "tn", 2, 2), "tn", 2, 2), "tn", 2, 2)
    return -_dot_raw(y, sol, "nt", 2, 2), y, jnp.zeros_like(d)


_solve_unit_lower.defvjp(_solve_fwd, _solve_bwd)


def _softplus(x):
    return jnp.maximum(x, 0.0) + jnp.log1p(jnp.exp(-jnp.abs(x)))


def _rms(x, w):
    return x * lax.rsqrt(jnp.mean(x * x, axis=-1, keepdims=True) + EPS) * w


HG_LEVELS = (32, 16, 8, 4, 2, 1)


def _hg_level_sums():
    i = np.arange(CHUNK)[:, None]
    m = np.arange(CHUNK)[None, :]
    to_row = [(m <= i) & (m // b == i // b) for b in HG_LEVELS]
    to_col = [(m > i) & (m // b == i // b) for b in HG_LEVELS if b > 1]
    return _thrice(jnp.asarray(np.concatenate(to_row + to_col + [m <= i]), F32))


def _hg_level_masks():
    i = np.arange(CHUNK)[:, None]
    j = np.arange(CHUNK)[None, :]
    return jnp.asarray(np.stack([(i // b == j // b + 1) & ((i // b) % 2 == 1) for b in HG_LEVELS]), F32)


def _hg_pre(qraw, f, lb, sums):
    g = jnp.log(lb + (1.0 - lb) * jax.nn.sigmoid(f))
    k = (1.0 - lb) * jax.nn.sigmoid(-f)
    q = jax.nn.silu(qraw) * (HEAD_DIM ** -0.5)
    return q, k, _dot01(g, *sums)


def _hg_head(st, q, k, v, e, masks):
    nl = len(HG_LEVELS)
    eye = (_iota2((CHUNK, CHUNK), 0) == _iota2((CHUNK, CHUNK), 1)).astype(F32)
    a = eye * jnp.sum(q * k, axis=-1, keepdims=True)
    for l, b in enumerate(HG_LEVELS):
        rows = q * jnp.exp(e[l * CHUNK:(l + 1) * CHUNK])
        cols = k * jnp.exp(e[(nl + l) * CHUNK:(nl + l + 1) * CHUNK]) if b > 1 else k
        a = a + masks[l] * _dot(rows, cols, "nt", 1, 1)
    gc = e[(2 * nl - 1) * CHUNK:2 * nl * CHUNK]
    o = _dot(a, v, "nn", 1, 1) + _dot(q * jnp.exp(gc), st, "nt", 1, 1)
    g_last = gc[CHUNK - 1:CHUNK]
    st_new = st * jnp.exp(g_last) + _dot(v, k * jnp.exp(g_last - gc), "tn", 1, 1)
    return o, st_new


_HG_HEADS = jax.vmap(_hg_head, in_axes=(0, 0, 0, 0, 0, None))


def _hg_post(o, gate, gw):
    return _rms(o, gw) * jax.nn.silu(gate)


def _gd_conv(xp, cw):
    off = HALO - (CONV_K - 1)
    y = cw[0:1] * xp[off:off + CHUNK]
    for kk in range(1, CONV_K):
        y = y + cw[kk:kk + 1] * xp[off + kk:off + kk + CHUNK]
    return y


def _gd_conv_bwd(xp, cw, y, dc):
    off = HALO - (CONV_K - 1)
    sig = jax.nn.sigmoid(y)
    dy = dc * (sig * (1.0 + y * (1.0 - sig)))
    dxp, dcw = None, []
    for kk in range(CONV_K):
        moved = jnp.pad(dy, ((off + kk, HALO - off - kk), (0, 0)))
        term = cw[kk:kk + 1] * moved
        dxp = term if dxp is None else dxp + term
        dcw.append(jnp.sum(xp * moved, axis=0, keepdims=True))
    return dxp, jnp.concatenate(dcw, axis=0)


def _gd_gates(a, b, alog, dtb):
    beta = jax.nn.sigmoid(b)
    g = -jnp.exp(alog) * _softplus(a + dtb)
    expand = (_iota2((N_HEADS, D_MODEL), 1) // HEAD_DIM == _iota2((N_HEADS, D_MODEL), 0)).astype(F32)
    g_x = _dot(g, expand, "nn", 3, 1)
    after = (_iota2((CHUNK, D_MODEL), 0) > _iota2((CHUNK, D_MODEL), 1) % HEAD_DIM).astype(F32)
    sums = _dot01(jnp.concatenate([g_x, g_x * after], axis=1), *_thrice(_tril_f32(CHUNK)))
    return _dot(beta, expand, "nn", 3, 1), sums


def _gd_head(st, q, k, v, beta, gc, diff, gate, onw, dinv=None):
    q = q * lax.rsqrt(jnp.sum(q * q, axis=-1, keepdims=True) + EPS) * (HEAD_DIM ** -0.5)
    k = k * lax.rsqrt(jnp.sum(k * k, axis=-1, keepdims=True) + EPS)
    ri = _iota2((CHUNK, CHUNK), 0)
    ci = _iota2((CHUNK, CHUNK), 1)
    decay = jnp.exp(jnp.where(ri >= ci, diff[:, 0:CHUNK], -jnp.inf))
    kb = k * beta
    egc = jnp.exp(gc)
    L = jnp.where(ri > ci, _dot(kb, k, "nt", 1, 1) * decay, 0.0)
    made = dinv is None
    if made:
        dinv = _half_inverses(L)[0]
    sol = _solve_unit_lower(L, jnp.concatenate([v * beta, kb * egc], axis=1), dinv)
    u = sol[:, 0:HEAD_DIM]
    w = sol[:, HEAD_DIM:2 * HEAD_DIM]
    a_qk = jnp.where(ri >= ci, _dot(q, k, "nt", 1, 1) * decay, 0.0)
    g_last = gc[CHUNK - 1:CHUNK]
    v_new = u - _dot(w, st, "nt", 1, 1)
    o = _dot(q * egc, st, "nt", 1, 1) + _dot(a_qk, v_new, "nn", 1, 1)
    st_new = st * jnp.exp(g_last) + _dot(v_new, k * jnp.exp(g_last - gc), "tn", 1, 1)
    out = (_rms(o, onw) * jax.nn.silu(gate), st_new)
    return out + (dinv,) if made else out


def _params(*sem):
    return pltpu.CompilerParams(dimension_semantics=sem, vmem_limit_bytes=VMEM_LIMIT)


def _tile(n, pref):
    t = min(n, pref)
    assert n % t == 0, (n, pref)
    return t


def _mm_tiles(m, n, k, a_size, b_size, tile_sizes):
    tm, tn, tk = _tile(m, MM_TILE), _tile(n, MM_TILE), k

    def need(tm, tn, tk):
        acc = 4 * tm * tn * (2 if tk < k else 1)
        return 2 * (tm * tk * a_size + tk * tn * b_size + tm * tn * sum(tile_sizes)) + acc

    while need(tm, tn, tk) > MM_VMEM_BUDGET:
        if tk > 2048 or (tk > 512 and tm <= 512):
            tk //= 2
        else:
            tm //= 2
    return tm, tn, tk


def _mm(a, b, mode, out_dtypes, name, epilogue=None, extras=(), after=None):
    if mode == "nn":
        (m, k), (k2, n) = a.shape, b.shape
    elif mode == "nt":
        (m, k), (n, k2) = a.shape, b.shape
    else:
        (k, m), (k2, n) = a.shape, b.shape
    assert k == k2, (a.shape, b.shape, mode)
    tm, tn, tk = _mm_tiles(m, n, k, a.dtype.itemsize, b.dtype.itemsize,
                           [e.dtype.itemsize for e in extras] + [jnp.dtype(dt).itemsize for dt in out_dtypes])
    nk = k // tk
    ne, no, nafter = len(extras), len(out_dtypes), int(after is not None)
    if epilogue is None:
        epilogue = lambda acc: (acc,)

    def body(*refs):
        a_ref, b_ref = refs[0], refs[1]
        ex = refs[2:2 + ne]
        outs = refs[2 + ne + nafter:2 + ne + nafter + no]
        part = lax.dot_general(a_ref[...].astype(BF16), b_ref[...].astype(BF16), _DIMS[mode],
                               preferred_element_type=F32)

        def finish(acc):
            for o_ref, val in zip(outs, epilogue(acc, *[e[...] for e in ex])):
                o_ref[...] = val.astype(o_ref.dtype)

        if nk == 1:
            finish(part)
        else:
            acc_ref = refs[-1]
            kk = pl.program_id(2)

            @pl.when(kk == 0)
            def _():
                acc_ref[...] = part

            @pl.when(kk > 0)
            def _():
                acc_ref[...] += part

            @pl.when(kk == nk - 1)
            def _():
                finish(acc_ref[...])

    if mode == "tn":
        a_spec = pl.BlockSpec((tk, tm), lambda i, j, kk: (kk, i))
    else:
        a_spec = pl.BlockSpec((tm, tk), lambda i, j, kk: (i, kk))
    if mode == "nt":
        b_spec = pl.BlockSpec((tn, tk), lambda i, j, kk: (j, kk))
    else:
        b_spec = pl.BlockSpec((tk, tn), lambda i, j, kk: (kk, j))
    o_spec = pl.BlockSpec((tm, tn), lambda i, j, kk: (i, j))
    res = pl.pallas_call(
        body,
        name=name,
        grid=(m // tm, n // tn, nk),
        in_specs=[a_spec, b_spec] + [o_spec] * ne + [pl.BlockSpec(memory_space=pl.ANY)] * nafter,
        out_specs=[o_spec] * no,
        out_shape=[jax.ShapeDtypeStruct((m, n), dt) for dt in out_dtypes],
        scratch_shapes=[pltpu.VMEM((tm, tn), F32)] if nk > 1 else [],
        compiler_params=_params("parallel", "parallel", "arbitrary"),
    )(a, b, *extras, *([after] if nafter else []))
    return res[0] if no == 1 else res


def _mm_rows(a, b, mode, out_dtypes, name, epilogue, extras=(), vectors=(), n_sums=0, after=None):
    assert mode in ("nn", "nt")
    (m, k), n = a.shape, (b.shape[1] if mode == "nn" else b.shape[0])
    tm = _tile(m, MM_ROWS_TILE)
    mt = m // tm
    ne, no, nafter = len(extras) + len(vectors), len(out_dtypes), int(after is not None)

    def body(*refs):
        a_ref, b_ref = refs[0], refs[1]
        ex = refs[2:2 + ne]
        outs = refs[2 + ne + nafter:2 + ne + nafter + no]
        sums = refs[2 + ne + nafter + no:2 + ne + nafter + no + n_sums]
        acc_ref = refs[-1]
        i = pl.program_id(0)

        @pl.when(i == 0)
        def _():
            acc_ref[1] = jnp.zeros((tm, n), F32)

        vals = epilogue(acc_ref[1 - i % 2], *[e[...] for e in ex])
        acc_ref[i % 2] = lax.dot_general(a_ref[...].astype(BF16), b_ref[...].astype(BF16), _DIMS[mode],
                                         preferred_element_type=F32)
        for o_ref, val in zip(outs, vals[:no]):
            o_ref[...] = val.astype(o_ref.dtype)
        for s_ref, val in zip(sums, vals[no:]):
            @pl.when(i <= 1)
            def _(s_ref=s_ref, val=val):
                s_ref[...] = val

            @pl.when(i > 1)
            def _(s_ref=s_ref, val=val):
                s_ref[...] += val

    ahead = lambda i: (jnp.minimum(i, mt - 1), 0)
    behind = lambda i: (jnp.maximum(i - 1, 0), 0)
    fixed = lambda i: (0, 0)
    row = pl.BlockSpec((tm, n), behind)
    vec = pl.BlockSpec((1, n), fixed)
    res = pl.pallas_call(
        body, name=name, grid=(mt + 1,),
        in_specs=([pl.BlockSpec((tm, k), ahead), pl.BlockSpec(b.shape, fixed)] + [row] * len(extras)
                  + [vec] * len(vectors) + [pl.BlockSpec(memory_space=pl.ANY)] * nafter),
        out_specs=[row] * no + [vec] * n_sums,
        out_shape=[jax.ShapeDtypeStruct((m, n), dt) for dt in out_dtypes] + [jax.ShapeDtypeStruct((1, n), F32)] * n_sums,
        scratch_shapes=[pltpu.VMEM((2, tm, n), F32)],
        compiler_params=_params("arbitrary"),
    )(a, b, *extras, *vectors, *([after] if nafter else []))
    return res[0] if no + n_sums == 1 else res


def _ep_residual_norm(acc, res, w):
    h = res + acc
    return h, _rms(h, w)


def _ep_norm_bwd(acc, x, dres, w):
    r = lax.rsqrt(jnp.mean(x * x, axis=-1, keepdims=True) + EPS)
    g = acc * w
    dx = dres + (r * g - x * (r * r * r * jnp.mean(g * x, axis=-1, keepdims=True)))
    return dx, dx, jnp.sum(acc * (x * r), axis=0, keepdims=True)


def _rms_fwd(x, w, name, tm=512):
    n, d = x.shape
    tm = _tile(n, tm)

    def body(x_ref, w_ref, y_ref):
        y_ref[...] = _rms(x_ref[...], w_ref[...]).astype(y_ref.dtype)

    return pl.pallas_call(
        body, name=name, grid=(n // tm,),
        in_specs=[pl.BlockSpec((tm, d), lambda i: (i, 0)), pl.BlockSpec((1, d), lambda i: (0, 0))],
        out_specs=pl.BlockSpec((tm, d), lambda i: (i, 0)),
        out_shape=jax.ShapeDtypeStruct((n, d), BF16),
        compiler_params=_params("arbitrary"),
    )(x, w)


def _loss_head(h, w, target, name, tm=512):
    n, d = h.shape
    tm = _tile(n, tm)

    def body(h_ref, w_ref, t_ref, dh_ref, dhb_ref, dw_ref, sq_ref):
        y, vjp = jax.vjp(_rms, h_ref[...], w_ref[...])
        err = y - t_ref[...]
        dh, dw = vjp(err * (1.0 / d))
        dh_ref[...] = dh
        dhb_ref[...] = dh.astype(dhb_ref.dtype)
        sq = jnp.sum(err * err, axis=0, keepdims=True)

        @pl.when(pl.program_id(0) == 0)
        def _():
            dw_ref[...] = dw
            sq_ref[...] = sq

        @pl.when(pl.program_id(0) > 0)
        def _():
            dw_ref[...] += dw
            sq_ref[...] += sq

        @pl.when(pl.program_id(0) == n // tm - 1)
        def _():
            total = jnp.sum(sq_ref[...], axis=1, keepdims=True) * (0.5 / d)
            sq_ref[...] = jnp.broadcast_to(total, sq_ref.shape)

    row = pl.BlockSpec((tm, d), lambda i: (i, 0))
    vec = pl.BlockSpec((1, d), lambda i: (0, 0))
    return pl.pallas_call(
        body, name=name, grid=(n // tm,),
        in_specs=[row, vec, row],
        out_specs=[row, row, vec, vec],
        out_shape=[jax.ShapeDtypeStruct((n, d), F32), jax.ShapeDtypeStruct((n, d), BF16),
                   jax.ShapeDtypeStruct((1, d), F32), jax.ShapeDtypeStruct((1, d), F32)],
        compiler_params=_params("arbitrary"),
    )(h, w, target)


def _lower_bounds(logits):
    sm = jax.nn.softmax(logits, axis=0)
    rows = [sm[0:1] * 0.0]
    for r in range(1, DEPTH):
        rows.append(rows[-1] + sm[r:r + 1])
    return jnp.concatenate(rows, axis=0)


def _lb_fwd(logits, name):
    def body(l_ref, o_ref):
        o_ref[...] = _lower_bounds(l_ref[...])

    return pl.pallas_call(body, name=name, out_shape=jax.ShapeDtypeStruct(logits.shape, F32))(logits)


def _lb_bwd(logits, dlb, name):
    def body(l_ref, d_ref, o_ref):
        _, vjp = jax.vjp(_lower_bounds, l_ref[...])
        (o_ref[...],) = vjp(d_ref[...])

    return pl.pallas_call(body, name=name, out_shape=jax.ShapeDtypeStruct(logits.shape, F32))(logits, dlb)


def _head_slice(h):
    return pl.ds(h * HEAD_DIM, HEAD_DIM)


_GD_HEADS = jax.vmap(_gd_head, in_axes=(0, 0, 0, 0, 0, 0, 0, 0, None))
_GD_HEADS_AGAIN = jax.vmap(_gd_head, in_axes=(0, 0, 0, 0, 0, 0, 0, 0, None, 0))


def _lane_blocks(width, block_body):
    def trip(j, carry):
        block_body(lambda base=0: pl.ds(pl.multiple_of(j * LANE_BLOCK + base, LANE_BLOCK), LANE_BLOCK))
        return carry

    lax.fori_loop(0, width // LANE_BLOCK, trip, 0, unroll=BLOCK_UNROLL)


def _row_blocks(rows, block_body):
    def trip(j, carry):
        block_body(pl.ds(pl.multiple_of(j * ROW_BLOCK, ROW_BLOCK), ROW_BLOCK))
        return carry

    lax.fori_loop(0, rows // ROW_BLOCK, trip, 0, unroll=BLOCK_UNROLL)


def _hg_pre_block(p_ref, lb_ref, sums_refs, q_sc, k_sc, v_sc, e_sc, at):
    sl = at()
    q_sc[:, sl], k_sc[:, sl], e_sc[:, sl] = _hg_pre(
        p_ref[:, sl].astype(F32), p_ref[:, at(D_MODEL)].astype(F32), lb_ref[:, sl], [r[...] for r in sums_refs])
    v_sc[:, sl] = p_ref[:, at(2 * D_MODEL)].astype(F32)


def _gd_xp(halo_ref, p_ref, sl, first_chunk):
    halo = jnp.where(first_chunk, 0.0, halo_ref[:, sl].astype(F32))
    return jnp.concatenate([halo, p_ref[:, sl].astype(F32)], axis=0)


def _stack_all(ref, first=0):
    return jnp.stack([ref[s, :, _head_slice(h + first)] for s in range(ref.shape[0]) for h in range(N_HEADS)])


def _unstack_all(ref, val, first=0):
    for s in range(ref.shape[0]):
        for h in range(N_HEADS):
            ref[s, :, _head_slice(h + first)] = val[s * N_HEADS + h].astype(ref.dtype)


def _gdn_fwd_all(projm, projab, cw, alog, dtb, onw, seqs, name):
    n = projm.shape[0]
    t = n // seqs
    nc = t // CHUNK
    d = D_MODEL
    per_halo = CHUNK // HALO
    nh = seqs * N_HEADS

    def body(p_ref, halo_ref, ab_ref, cw_ref, alog_ref, dtb_ref, onw_ref, o2_ref, st_all_ref, y_ref, dinv_ref,
             st_sc, c_sc, beta_sc, g_sc):
        first_chunk = pl.program_id(0) == 0

        @pl.when(first_chunk)
        def _():
            st_sc[...] = jnp.zeros_like(st_sc)

        for s in range(seqs):
            def conv(at, s=s):
                sl = at()
                y = _gd_conv(_gd_xp(halo_ref.at[s], p_ref.at[s], sl, first_chunk), cw_ref[:, sl])
                y_ref[s, :, sl] = y
                c_sc[s, :, sl] = jax.nn.silu(y)

            _lane_blocks(3 * d, conv)
            beta_sc[s], g_sc[s] = _gd_gates(ab_ref[s, :, 0:N_HEADS], ab_ref[s, :, N_HEADS:2 * N_HEADS],
                                            alog_ref[...], dtb_ref[...])
        st_all_ref[0] = st_sc[...]
        o2, st_sc[...], dinv_ref[0] = _GD_HEADS(
            st_sc[...], _stack_all(c_sc), _stack_all(c_sc, N_HEADS), _stack_all(c_sc, 2 * N_HEADS), _stack_all(beta_sc),
            _stack_all(g_sc), _stack_all(g_sc, N_HEADS), _stack_all(p_ref, 3 * N_HEADS).astype(F32), onw_ref[...])
        _unstack_all(o2_ref, o2)

    rows = lambda c: (0, c, 0)
    const = lambda c: (0, 0)
    per_chunk = lambda c: (c, 0, 0, 0)
    p3 = projm.reshape(seqs, t, 4 * d)
    o2, st_all, conv_y, dinv_all = pl.pallas_call(
        body, name=name, grid=(nc,),
        in_specs=[pl.BlockSpec((seqs, CHUNK, 4 * d), rows),
                  pl.BlockSpec((seqs, HALO, 3 * d), lambda c: (0, jnp.maximum(c * per_halo - 1, 0), 0)),
                  pl.BlockSpec((seqs, CHUNK, AB_PAD), rows),
                  pl.BlockSpec((CONV_K, 3 * d), const), pl.BlockSpec((1, N_HEADS), const),
                  pl.BlockSpec((1, N_HEADS), const), pl.BlockSpec((1, HEAD_DIM), const)],
        out_specs=[pl.BlockSpec((seqs, CHUNK, d), rows), pl.BlockSpec((1, nh, HEAD_DIM, HEAD_DIM), per_chunk),
                   pl.BlockSpec((seqs, CHUNK, 3 * d), rows), pl.BlockSpec((1, nh, CHUNK, CHUNK), per_chunk)],
        out_shape=[jax.ShapeDtypeStruct((seqs, t, d), BF16), jax.ShapeDtypeStruct((nc, nh, HEAD_DIM, HEAD_DIM), F32),
                   jax.ShapeDtypeStruct((seqs, t, 3 * d), F32), jax.ShapeDtypeStruct((nc, nh, CHUNK, CHUNK), F32)],
        scratch_shapes=[pltpu.VMEM((nh, HEAD_DIM, HEAD_DIM), F32), pltpu.VMEM((seqs, CHUNK, 3 * d), F32),
                        pltpu.VMEM((seqs, CHUNK, d), F32), pltpu.VMEM((seqs, CHUNK, 2 * d), F32)],
        compiler_params=_params("arbitrary"),
    )(p3, p3, projab.reshape(seqs, t, AB_PAD), cw, alog, dtb, onw)
    return o2.reshape(n, d), st_all, conv_y, dinv_all


def _gdn_bwd_all(projm, projab, conv_y, cw, alog, dtb, onw, st_all, dinv_all, do2, seqs, name):
    n = projm.shape[0]
    t = n // seqs
    nc = t // CHUNK
    d = D_MODEL
    per_halo = CHUNK // HALO
    nh = seqs * N_HEADS

    def body(p_ref, halo_ref, ab_ref, y_ref, cw_ref, alog_ref, dtb_ref, onw_ref, st_all_ref, dinv_ref, do2_ref,
             dp_ref, dab_ref, dcw_ref, dalog_ref, ddtb_ref, donw_ref,
             dst_sc, dhalo_sc, c_sc, beta_sc, g_sc, dc_sc, dbeta_sc, dg_sc):
        first = pl.program_id(0) == 0
        first_chunk = pl.program_id(0) == nc - 1

        @pl.when(first)
        def _():
            dst_sc[...] = jnp.zeros_like(dst_sc)
            dhalo_sc[...] = jnp.zeros_like(dhalo_sc)

        gates_vjps = []
        for s in range(seqs):
            def act(at, s=s):
                c_sc[s, :, at()] = jax.nn.silu(y_ref[s, :, at()])

            _lane_blocks(3 * d, act)
            (beta_sc[s], g_sc[s]), gates_vjp = jax.vjp(
                _gd_gates, ab_ref[s, :, 0:N_HEADS], ab_ref[s, :, N_HEADS:2 * N_HEADS], alog_ref[...], dtb_ref[...])
            gates_vjps.append(gates_vjp)

        dinv = dinv_ref[0]
        _, vjp = jax.vjp(
            lambda *a: _GD_HEADS_AGAIN(*a, dinv), st_all_ref[0], _stack_all(c_sc), _stack_all(c_sc, N_HEADS),
            _stack_all(c_sc, 2 * N_HEADS), _stack_all(beta_sc), _stack_all(g_sc), _stack_all(g_sc, N_HEADS),
            _stack_all(p_ref, 3 * N_HEADS).astype(F32), onw_ref[...])
        dst_sc[...], dq, dk, dv, dbeta, dg, ddiff, dgate, donw = vjp((_stack_all(do2_ref).astype(F32), dst_sc[...]))
        _unstack_all(dc_sc, dq)
        _unstack_all(dc_sc, dk, N_HEADS)
        _unstack_all(dc_sc, dv, 2 * N_HEADS)
        _unstack_all(dbeta_sc, dbeta)
        _unstack_all(dg_sc, dg)
        _unstack_all(dg_sc, ddiff, N_HEADS)
        _unstack_all(dp_ref, dgate, 3 * N_HEADS)

        dalog, ddtb = None, None
        for s in range(seqs):
            def conv_bwd(at, s=s):
                sl = at()
                dxp, dcw = _gd_conv_bwd(_gd_xp(halo_ref.at[s], p_ref.at[s], sl, first_chunk), cw_ref[:, sl],
                                        y_ref[s, :, sl], dc_sc[s, :, sl])
                dqkv = jnp.concatenate([dxp[HALO:CHUNK], dxp[CHUNK:HALO + CHUNK] + dhalo_sc[s, :, sl]], axis=0)
                dp_ref[s, :, sl] = dqkv.astype(dp_ref.dtype)
                dhalo_sc[s, :, sl] = dxp[0:HALO]

                if s > 0:
                    dcw_ref[:, sl] += dcw
                    return

                @pl.when(first)
                def _():
                    dcw_ref[:, sl] = dcw

                @pl.when(jnp.logical_not(first))
                def _():
                    dcw_ref[:, sl] += dcw

            _lane_blocks(3 * d, conv_bwd)
            da, db, dalog_s, ddtb_s = gates_vjps[s]((dbeta_sc[s], dg_sc[s]))
            dab_ref[s] = jnp.concatenate(
                [da, db, jnp.zeros((CHUNK, AB_PAD - 2 * N_HEADS), F32)], axis=1).astype(dab_ref.dtype)
            dalog = dalog_s if dalog is None else dalog + dalog_s
            ddtb = ddtb_s if ddtb is None else ddtb + ddtb_s

        @pl.when(first)
        def _():
            dalog_ref[...] = dalog
            ddtb_ref[...] = ddtb
            donw_ref[...] = donw

        @pl.when(jnp.logical_not(first))
        def _():
            dalog_ref[...] += dalog
            ddtb_ref[...] += ddtb
            donw_ref[...] += donw

    back = lambda c: nc - 1 - c
    rows = lambda c: (0, back(c), 0)
    const = lambda c: (0, 0)
    per_chunk = lambda c: (back(c), 0, 0, 0)
    small = [pl.BlockSpec((CONV_K, 3 * d), const), pl.BlockSpec((1, N_HEADS), const),
             pl.BlockSpec((1, N_HEADS), const), pl.BlockSpec((1, HEAD_DIM), const)]
    p3 = projm.reshape(seqs, t, 4 * d)
    dp, dab, dcw, dalog, ddtb, donw = pl.pallas_call(
        body, name=name, grid=(nc,),
        in_specs=[pl.BlockSpec((seqs, CHUNK, 4 * d), rows),
                  pl.BlockSpec((seqs, HALO, 3 * d), lambda c: (0, jnp.maximum(back(c) * per_halo - 1, 0), 0)),
                  pl.BlockSpec((seqs, CHUNK, AB_PAD), rows), pl.BlockSpec((seqs, CHUNK, 3 * d), rows)] + small + [
                  pl.BlockSpec((1, nh, HEAD_DIM, HEAD_DIM), per_chunk), pl.BlockSpec((1, nh, CHUNK, CHUNK), per_chunk),
                  pl.BlockSpec((seqs, CHUNK, d), rows)],
        out_specs=[pl.BlockSpec((seqs, CHUNK, 4 * d), rows), pl.BlockSpec((seqs, CHUNK, AB_PAD), rows)] + small,
        out_shape=[jax.ShapeDtypeStruct((seqs, t, 4 * d), BF16), jax.ShapeDtypeStruct((seqs, t, AB_PAD), BF16),
                   jax.ShapeDtypeStruct((CONV_K, 3 * d), F32), jax.ShapeDtypeStruct((1, N_HEADS), F32),
                   jax.ShapeDtypeStruct((1, N_HEADS), F32), jax.ShapeDtypeStruct((1, HEAD_DIM), F32)],
        scratch_shapes=[pltpu.VMEM((nh, HEAD_DIM, HEAD_DIM), F32), pltpu.VMEM((seqs, HALO, 3 * d), F32),
                        pltpu.VMEM((seqs, CHUNK, 3 * d), F32), pltpu.VMEM((seqs, CHUNK, d), F32),
                        pltpu.VMEM((seqs, CHUNK, 2 * d), F32), pltpu.VMEM((seqs, CHUNK, 3 * d), F32),
                        pltpu.VMEM((seqs, CHUNK, d), F32), pltpu.VMEM((seqs, CHUNK, 2 * d), F32)],
        compiler_params=_params("arbitrary"),
    )(p3, p3, projab.reshape(seqs, t, AB_PAD), conv_y, cw, alog, dtb, onw, st_all, dinv_all,
      do2.reshape(seqs, t, d))
    return dp.reshape(n, 4 * d), dab.reshape(n, AB_PAD), dcw, dalog, ddtb, donw


def _hgrn_fwd_all(proj, lb, gw, seqs, name):
    n = proj.shape[0]
    t = n // seqs
    nc = t // CHUNK
    d = D_MODEL
    nh = seqs * N_HEADS
    sums, masks = _hg_level_sums(), _hg_level_masks()

    def body(p_ref, lb_ref, gw_ref, sums_wide_ref, sums_once_ref, masks_ref, o2_ref, o_ref, st_all_ref,
             st_sc, q_sc, k_sc, v_sc, e_sc):
        @pl.when(pl.program_id(0) == 0)
        def _():
            st_sc[...] = jnp.zeros_like(st_sc)

        sums_refs = (sums_wide_ref, sums_once_ref)
        for s in range(seqs):
            _lane_blocks(d, functools.partial(_hg_pre_block, p_ref.at[s], lb_ref, sums_refs, q_sc.at[s], k_sc.at[s],
                                              v_sc.at[s], e_sc.at[s]))
        st_all_ref[0] = st_sc[...]
        for s in range(seqs):
            one, mine = pl.ds(s, 1), pl.ds(s * N_HEADS, N_HEADS)
            o, st_sc[mine] = _HG_HEADS(st_sc[mine], *[_stack_all(r.at[one]) for r in (q_sc, k_sc, v_sc, e_sc)],
                                       masks_ref[...])
            _unstack_all(o_ref.at[one], o)

            def post(rows, s=s):
                gate = p_ref[s, rows, 3 * d:4 * d].astype(F32)
                o2_ref[s, rows, :] = _hg_post(o_ref[s, rows, :], gate, gw_ref[...]).astype(o2_ref.dtype)

            _row_blocks(CHUNK, post)

    rows = lambda c: (0, c, 0)
    vec = pl.BlockSpec((1, d), lambda c: (0, 0))
    act = pl.BlockSpec((seqs, CHUNK, d), rows)
    o2, o, st_all = pl.pallas_call(
        body, name=name, grid=(nc,),
        in_specs=[pl.BlockSpec((seqs, CHUNK, 4 * d), rows), vec, vec]
        + [pl.BlockSpec(m.shape, lambda c: (0, 0)) for m in sums] + [pl.BlockSpec(masks.shape, lambda c: (0, 0, 0))],
        out_specs=[act, act, pl.BlockSpec((1, nh, HEAD_DIM, HEAD_DIM), lambda c: (c, 0, 0, 0))],
        out_shape=[jax.ShapeDtypeStruct((seqs, t, d), BF16), jax.ShapeDtypeStruct((seqs, t, d), F32),
                   jax.ShapeDtypeStruct((nc, nh, HEAD_DIM, HEAD_DIM), F32)],
        scratch_shapes=[pltpu.VMEM((nh, HEAD_DIM, HEAD_DIM), F32)] + [pltpu.VMEM((seqs, CHUNK, d), F32)] * 3
        + [pltpu.VMEM((seqs, sums[0].shape[0], d), F32)],
        compiler_params=_params("arbitrary"),
    )(proj.reshape(seqs, t, 4 * d), lb, gw, *sums, masks)
    return o2.reshape(n, d), o, st_all


def _hgrn_bwd_all(proj, lb, gw, st_all, o, do2, seqs, name):
    n = proj.shape[0]
    t = n // seqs
    nc = t // CHUNK
    d = D_MODEL
    nh = seqs * N_HEADS
    sums, masks = _hg_level_sums(), _hg_level_masks()

    def body(p_ref, lb_ref, gw_ref, sums_wide_ref, sums_once_ref, masks_ref, st_all_ref, o_ref, do2_ref,
             dp_ref, dlb_ref, dgw_ref,
             dst_sc, q_sc, k_sc, v_sc, e_sc, do_sc, dq_sc, dk_sc, dv_sc, de_sc, dgw_sc):
        first = pl.program_id(0) == 0

        @pl.when(first)
        def _():
            dst_sc[...] = jnp.zeros_like(dst_sc)

        sums_refs = (sums_wide_ref, sums_once_ref)
        dgw_sc[...] = jnp.zeros_like(dgw_sc)
        for s in range(seqs):
            _lane_blocks(d, functools.partial(_hg_pre_block, p_ref.at[s], lb_ref, sums_refs, q_sc.at[s], k_sc.at[s],
                                              v_sc.at[s], e_sc.at[s]))

            def post_bwd(rows, s=s):
                _, vjp = jax.vjp(_hg_post, o_ref[s, rows, :], p_ref[s, rows, 3 * d:4 * d].astype(F32), gw_ref[...])
                do_sc[s, rows, :], dgate, dgw = vjp(do2_ref[s, rows, :].astype(F32))
                dp_ref[s, rows, 3 * d:4 * d] = dgate.astype(dp_ref.dtype)
                dgw_sc[...] += dgw

            _row_blocks(CHUNK, post_bwd)

        level_masks = masks_ref[...]
        _, vjp = jax.vjp(lambda *a: _HG_HEADS(*a, level_masks), st_all_ref[0],
                         *[_stack_all(r) for r in (q_sc, k_sc, v_sc, e_sc)])
        grads = vjp((_stack_all(do_sc), dst_sc[...]))
        dst_sc[...] = grads[0]
        for r, val in zip((dq_sc, dk_sc, dv_sc, de_sc), grads[1:]):
            _unstack_all(r, val)

        for s in range(seqs):
            def pre_bwd(at, s=s):
                sl = at()
                level_sums = (sums_wide_ref[...], sums_once_ref[...])
                _, vjp = jax.vjp(lambda qraw, f, lb: _hg_pre(qraw, f, lb, level_sums), p_ref[s, :, sl].astype(F32),
                                 p_ref[s, :, at(d)].astype(F32), lb_ref[:, sl])
                dqraw, df, dlb = vjp((dq_sc[s, :, sl], dk_sc[s, :, sl], de_sc[s, :, sl]))
                dp_ref[s, :, sl] = dqraw.astype(dp_ref.dtype)
                dp_ref[s, :, at(d)] = df.astype(dp_ref.dtype)
                dp_ref[s, :, at(2 * d)] = dv_sc[s, :, sl].astype(dp_ref.dtype)
                if s > 0:
                    dlb_ref[:, sl] += dlb
                    return

                @pl.when(first)
                def _():
                    dlb_ref[:, sl] = dlb

                @pl.when(jnp.logical_not(first))
                def _():
                    dlb_ref[:, sl] += dlb

            _lane_blocks(d, pre_bwd)

        @pl.when(first)
        def _():
            dgw_ref[...] = dgw_sc[...]

        @pl.when(jnp.logical_not(first))
        def _():
            dgw_ref[...] += dgw_sc[...]

    rows = lambda c: (0, nc - 1 - c, 0)
    vec = pl.BlockSpec((1, d), lambda c: (0, 0))
    act = pl.BlockSpec((seqs, CHUNK, d), rows)
    wide = pl.BlockSpec((seqs, CHUNK, 4 * d), rows)
    e_rows = sums[0].shape[0]
    dp, dlb, dgw = pl.pallas_call(
        body, name=name, grid=(nc,),
        in_specs=[wide, vec, vec] + [pl.BlockSpec(m.shape, lambda c: (0, 0)) for m in sums] + [
                  pl.BlockSpec(masks.shape, lambda c: (0, 0, 0)),
                  pl.BlockSpec((1, nh, HEAD_DIM, HEAD_DIM), lambda c: (nc - 1 - c, 0, 0, 0)), act, act],
        out_specs=[wide, vec, vec],
        out_shape=[jax.ShapeDtypeStruct((seqs, t, 4 * d), BF16), jax.ShapeDtypeStruct((1, d), F32),
                   jax.ShapeDtypeStruct((1, d), F32)],
        scratch_shapes=[pltpu.VMEM((nh, HEAD_DIM, HEAD_DIM), F32)]
        + [pltpu.VMEM((seqs, CHUNK, d), F32)] * 3 + [pltpu.VMEM((seqs, e_rows, d), F32)]
        + [pltpu.VMEM((seqs, CHUNK, d), F32)] * 4 + [pltpu.VMEM((seqs, e_rows, d), F32), pltpu.VMEM((1, d), F32)],
        compiler_params=_params("arbitrary"),
    )(proj.reshape(seqs, t, 4 * d), lb, gw, *sums, masks, st_all, o, do2.reshape(seqs, t, d))
    return dp.reshape(n, 4 * d), dlb, dgw


def _adam_update(w, g, m, v):
    b1c = 1.0 - ADAM_B1 ** ADAM_STEP
    b2c = 1.0 - ADAM_B2 ** ADAM_STEP
    m_new = ADAM_B1 * m + (1.0 - ADAM_B1) * g
    v_new = ADAM_B2 * v + (1.0 - ADAM_B2) * (g * g)
    delta = -ADAM_LR * ((m_new / b1c) / (jnp.sqrt(v_new / b2c) + ADAM_EPS) + ADAM_WD * w)
    return delta, m_new, v_new


def _adamw(w, g, m, v, name, tr=256):
    r, c = w.shape
    tr = _tile(r, tr)

    def body(w_ref, g_ref, m_ref, v_ref, d_ref, mo_ref, vo_ref):
        d_ref[...], mo_ref[...], vo_ref[...] = _adam_update(w_ref[...], g_ref[...], m_ref[...], v_ref[...])

    blk = pl.BlockSpec((tr, c), lambda i: (i, 0))
    return pl.pallas_call(
        body, name=name, grid=(r // tr,),
        in_specs=[blk] * 4, out_specs=[blk] * 3,
        out_shape=[jax.ShapeDtypeStruct((r, c), F32)] * 3,
        compiler_params=_params("arbitrary"),
    )(w, g, m, v)


def _adamw_slots(w, slot_bufs, m, v, name, tr=256):
    nl, r, c = w.shape
    tr = _tile(r, tr)

    def body(*refs):
        w_ref = refs[0]
        g_refs = refs[1:1 + nl]
        m_ref, v_ref, go_ref, d_ref, mo_ref, vo_ref = refs[1 + nl:]
        for k in range(nl):
            @pl.when(pl.program_id(0) == k)
            def _(k=k):
                g = g_refs[k][0].astype(F32)
                for s in range(1, N_DEV):
                    g = g + g_refs[k][s].astype(F32)
                go_ref[0] = g

        d_ref[0], mo_ref[0], vo_ref[0] = _adam_update(w_ref[0], go_ref[0], m_ref[0], v_ref[0])

    blk = pl.BlockSpec((1, tr, c), lambda l, i: (l, i, 0))
    g_specs = [pl.BlockSpec((N_DEV, tr, c), lambda l, i, k=k: (0, jnp.where(l == k, i, 0), 0)) for k in range(nl)]
    return pl.pallas_call(
        body, name=name, grid=(nl, r // tr),
        in_specs=[blk] + g_specs + [blk, blk], out_specs=[blk] * 4,
        out_shape=[jax.ShapeDtypeStruct((nl, r, c), F32)] * 4,
        compiler_params=_params("arbitrary", "arbitrary"),
    )(w, *slot_bufs, m, v)


def _mesh_pos():
    return lax.axis_index("x"), lax.axis_index("y"), lax.axis_index("c")


def _flip(pos, p):
    x, y, c = pos
    return ((1 - x) if p & 4 else x, (1 - y) if p & 2 else y, (1 - c) if p & 1 else c)


def _lin(pos):
    return 4 * pos[0] + 2 * pos[1] + pos[2]


_HBM = pl.BlockSpec(memory_space=pltpu.HBM)
_SEM = pl.BlockSpec(memory_space=pltpu.SEMAPHORE)
_DATAFLOW = pltpu.SideEffectType.DATAFLOW_SIDE_EFFECTING


class _Item:
    def __init__(self, src, land_shape, src_pick, dst_pick, peers=tuple(range(1, N_DEV))):
        self.src, self.land_shape, self.src_pick, self.dst_pick = src, land_shape, src_pick, dst_pick
        self.peers = peers


def _remote_copies(items, src, land, send_sem, recv_sem, me, arriving):
    me_i = _lin(me)
    out = []
    for it, s_ref, l_ref in zip(items, src, land):
        for p in it.peers:
            peer = _flip(me, p)
            out.append(pltpu.make_async_remote_copy(
                src_ref=it.src_pick(s_ref, _lin(peer)),
                dst_ref=it.dst_pick(l_ref, _lin(peer) if arriving else me_i),
                send_sem=send_sem, recv_sem=recv_sem, device_id=peer, device_id_type=pl.DeviceIdType.MESH))
    return out


def _own_copies(items, src, land, sem, me):
    me_i = _lin(me)
    return [pltpu.make_async_copy(it.src_pick(s_ref, me_i), it.dst_pick(l_ref, me_i), sem)
            for it, s_ref, l_ref in zip(items, src, land)]


def _exchange_start(groups, name):
    items = [it for g in groups for it in g]
    n, ng = len(items), len(groups)
    first = [sum(len(g) for g in groups[:gi]) for gi in range(ng)]

    def body(*refs):
        src, land = refs[0:n], refs[n:2 * n]
        send_sems, recv_sems = refs[2 * n:2 * n + ng], refs[2 * n + ng:2 * n + 2 * ng]
        token = refs[4 * n + 2 * ng]
        me = _mesh_pos()
        for gi, g in enumerate(groups):
            sl = slice(first[gi], first[gi] + len(g))
            for cp in _remote_copies(g, src[sl], land[sl], send_sems[gi], recv_sems[gi], me, arriving=False):
                cp.start()
            for cp in _own_copies(g, src[sl], land[sl], recv_sems[gi], me):
                cp.start()
        token[...] = jnp.zeros_like(token)

    srcs = [pltpu.with_memory_space_constraint(it.src, pltpu.HBM) for it in items]
    lands = [pltpu.with_memory_space_constraint(lax.empty(it.land_shape, it.src.dtype), pltpu.HBM) for it in items]
    res = pl.pallas_call(
        body, name=name,
        out_shape=([pltpu.SemaphoreType.DMA(())] * (2 * ng)
                   + [pltpu.HBM(it.src.shape, it.src.dtype) for it in items]
                   + [pltpu.HBM(it.land_shape, it.src.dtype) for it in items]
                   + [jax.ShapeDtypeStruct((8, 128), F32)]),
        in_specs=[_HBM] * (2 * n),
        out_specs=[_SEM] * (2 * ng) + [_HBM] * (2 * n) + [pl.BlockSpec(memory_space=pltpu.VMEM)],
        input_output_aliases={i: 2 * ng + i for i in range(2 * n)},
        compiler_params=pltpu.CompilerParams(has_side_effects=_DATAFLOW),
    )(*srcs, *lands)
    send_sems, recv_sems = res[0:ng], res[ng:2 * ng]
    src_thru, land_thru = res[2 * ng:2 * ng + n], res[2 * ng + n:2 * ng + 2 * n]
    handles = []
    for gi, g in enumerate(groups):
        sl = slice(first[gi], first[gi] + len(g))
        handles.append((g, src_thru[sl], land_thru[sl], send_sems[gi], recv_sems[gi]))
    return handles, res[-1]


def _exchange_wait(handle, after, name):
    items, src_thru, land_thru, send_sem, recv_sem = handle
    k = len(items)
    afters = list(after) if isinstance(after, (list, tuple)) else [after]

    def body(*refs):
        src, land = refs[0:k], refs[k:2 * k]
        send_ref, recv_ref = refs[2 * k], refs[2 * k + 1]
        for cp in _remote_copies(items, src, land, send_ref, recv_ref, _mesh_pos(), arriving=True):
            cp.wait_send()
            cp.wait_recv()
        for cp in _own_copies(items, src, land, recv_ref, _mesh_pos()):
            cp.wait()

    res = pl.pallas_call(
        body, name=name,
        out_shape=([pltpu.HBM(s.shape, s.dtype) for s in src_thru] + [pltpu.HBM(l.shape, l.dtype) for l in land_thru]),
        in_specs=[_HBM] * (2 * k) + [_SEM, _SEM] + [pl.BlockSpec(memory_space=pl.ANY)] * len(afters),
        out_specs=[_HBM] * (2 * k),
        input_output_aliases={i: i for i in range(2 * k)},
        compiler_params=pltpu.CompilerParams(has_side_effects=_DATAFLOW),
    )(*src_thru, *land_thru, send_sem, recv_sem, *afters)
    return res[k:2 * k]


SAME_CORE = (2, 4, 6)
SIBLING = 1


def _pass_on_start(buf, name):
    def body(buf_ref, send_sem, recv_sem, thru_ref):
        me = _mesh_pos()
        for p in SAME_CORE:
            slot = buf_ref.at[_lin(_flip(me, p))]
            pltpu.make_async_remote_copy(src_ref=slot, dst_ref=slot, send_sem=send_sem, recv_sem=recv_sem,
                                         device_id=_flip(me, SIBLING), device_id_type=pl.DeviceIdType.MESH).start()

    return pl.pallas_call(
        body, name=name,
        out_shape=[pltpu.SemaphoreType.DMA(()), pltpu.SemaphoreType.DMA(()), pltpu.HBM(buf.shape, buf.dtype)],
        in_specs=[_HBM], out_specs=[_SEM, _SEM, _HBM], input_output_aliases={0: 2},
        compiler_params=pltpu.CompilerParams(has_side_effects=_DATAFLOW),
    )(pltpu.with_memory_space_constraint(buf, pltpu.HBM))


def _pass_on_wait(handle, name):
    send_sem, recv_sem, thru = handle

    def body(buf_ref, send_ref, recv_ref, out_ref):
        me = _mesh_pos()
        sibling = _flip(me, SIBLING)
        for p in SAME_CORE:
            mine, theirs = buf_ref.at[_lin(_flip(me, p))], buf_ref.at[_lin(_flip(sibling, p))]
            cp = pltpu.make_async_remote_copy(src_ref=mine, dst_ref=theirs, send_sem=send_ref, recv_sem=recv_ref,
                                              device_id=sibling, device_id_type=pl.DeviceIdType.MESH)
            cp.wait_send()
            cp.wait_recv()

    return pl.pallas_call(
        body, name=name, out_shape=pltpu.HBM(thru.shape, thru.dtype),
        in_specs=[_HBM, _SEM, _SEM], out_specs=_HBM, input_output_aliases={0: 0},
        compiler_params=pltpu.CompilerParams(has_side_effects=_DATAFLOW),
    )(thru, send_sem, recv_sem)


def _whole(ref, i):
    return ref


def _slot(ref, i):
    return ref.at[i]


def _rows_of(r):
    return lambda ref, i: ref.at[pl.ds(pl.multiple_of(i * r, r), r), :]


def _cols_of(c):
    return lambda ref, i: ref.at[:, pl.ds(pl.multiple_of(i * c, c), c)]


def _all_reduce_small(buf, after, name):
    r, c = buf.shape

    def body(src_ref, after_ref, out_ref, all_ref, send_sems, recv_sems):
        me = _mesh_pos()
        me_i = _lin(me)
        all_ref[me_i] = src_ref[...]
        for p in range(1, N_DEV):
            peer = _flip(me, p)
            pltpu.make_async_remote_copy(
                src_ref=src_ref, dst_ref=all_ref.at[me_i], send_sem=send_sems.at[p - 1], recv_sem=recv_sems.at[p - 1],
                device_id=peer, device_id_type=pl.DeviceIdType.MESH).start()
        for p in range(1, N_DEV):
            peer = _flip(me, p)
            cp = pltpu.make_async_remote_copy(
                src_ref=src_ref, dst_ref=all_ref.at[_lin(peer)], send_sem=send_sems.at[p - 1],
                recv_sem=recv_sems.at[p - 1], device_id=peer, device_id_type=pl.DeviceIdType.MESH)
            cp.wait_recv()
            cp.wait_send()
        acc = all_ref[0]
        for s in range(1, N_DEV):
            acc = acc + all_ref[s]
        out_ref[...] = acc

    vm = pl.BlockSpec(memory_space=pltpu.VMEM)
    return pl.pallas_call(
        body, name=name, in_specs=[vm, pl.BlockSpec(memory_space=pl.ANY)], out_specs=vm,
        out_shape=jax.ShapeDtypeStruct((r, c), F32),
        scratch_shapes=[pltpu.VMEM((N_DEV, r, c), F32), pltpu.SemaphoreType.DMA((N_DEV - 1,)),
                        pltpu.SemaphoreType.DMA((N_DEV - 1,))],
        compiler_params=pltpu.CompilerParams(has_side_effects=True),
    )(buf, after)


def _unshard_cols(g):
    s, l, r, c = g.shape
    return jnp.transpose(g, (1, 2, 0, 3)).reshape(l, r, s * c)


def kernel(x, gdn_w_in, gdn_conv, gdn_a_log, gdn_dt_bias, gdn_onorm, gdn_w_out, hgrn_w_in, hgrn_lb_logits, hgrn_gnorm, hgrn_w_out, norm_mix, norm_mlp, mlp_w_up, mlp_w_down, norm_final, loss_target, m_gdn_w_in, m_gdn_conv, m_gdn_a_log, m_gdn_dt_bias, m_gdn_onorm, m_gdn_w_out, m_hgrn_w_in, m_hgrn_lb_logits, m_hgrn_gnorm, m_hgrn_w_out, m_norm_mix, m_norm_mlp, m_mlp_w_up, m_mlp_w_down, m_norm_final, v_gdn_w_in, v_gdn_conv, v_gdn_a_log, v_gdn_dt_bias, v_gdn_onorm, v_gdn_w_out, v_hgrn_w_in, v_hgrn_lb_logits, v_hgrn_gnorm, v_hgrn_w_out, v_norm_mix, v_norm_mlp, v_mlp_w_up, v_mlp_w_down, v_norm_final):
    seqs, seq_len, d = x.shape
    n = seqs * seq_len
    me_i = _lin(_mesh_pos())
    x2 = x.reshape(n, d)
    target = loss_target.reshape(n, d)
    n_gdn, n_hgrn = gdn_w_in.shape[0], hgrn_w_in.shape[0]

    r_out, r_down = gdn_w_out.shape[1], mlp_w_down.shape[1]
    c_gin, c_hin, c_up = gdn_w_in.shape[2], hgrn_w_in.shape[2], mlp_w_up.shape[2]

    def gathered(w, pick, land_shape, **kw):
        return _Item(w.astype(BF16), land_shape, _whole, pick, **kw)

    groups = [[_Item(gdn_conv, (N_DEV,) + gdn_conv.shape, _whole, _slot),
               _Item(hgrn_gnorm, (N_DEV,) + hgrn_gnorm.shape, _whole, _slot)]]
    for i in range(DEPTH):
        j = i // 2
        if i % 2 == 0:
            direct = (SIBLING,) + SAME_CORE if i == 0 else tuple(range(1, N_DEV))
            groups += [[gathered(gdn_w_in[j], _slot, (N_DEV, d, c_gin), peers=direct)],
                       [gathered(gdn_w_out[j], _rows_of(r_out), (N_DEV * r_out, d))]]
        else:
            groups += [[gathered(hgrn_w_in[j], _cols_of(c_hin), (d, N_DEV * c_hin))],
                       [gathered(hgrn_w_out[j], _rows_of(r_out), (N_DEV * r_out, d))]]
        groups += [[gathered(mlp_w_up[i], _cols_of(c_up), (d, N_DEV * c_up))],
                   [gathered(mlp_w_down[i], _rows_of(r_down), (N_DEV * r_down, d))]]
    gather_handles, token = _exchange_start(groups, "gather_start")
    lbs = _lb_fwd(hgrn_lb_logits + token[0:1, 0:1], "lb_fwd")

    def arrived(k, after, name):
        return _exchange_wait(gather_handles[k], after, "gather_wait_" + name)

    saved = []
    w_in, w_ab, w_out, w_up, w_down = ([None] * DEPTH for _ in range(5))
    h = x2
    for i in range(DEPTH):
        j = i // 2
        if i == 0:
            g_conv, g_gnorm = arrived(0, h, "small")
            conv_full = _unshard_cols(g_conv)
            gnorm_full = jnp.transpose(g_gnorm, (1, 0, 2)).reshape(n_hgrn, d)
        if i == 0:
            y = _rms_fwd(h, norm_mix[0:1] + token[0:1, 0:1], "rms_mix_0")
        (w_in[i],) = arrived(1 + 4 * i, [y, lbs, conv_full, gnorm_full] if i == 0 else y, f"in_{i}")
        if i == 0:
            w_in[i] = _pass_on_wait(_pass_on_start(w_in[i], "pass_on_start_in_0"), "pass_on_wait_in_0")
        if i % 2 == 0:
            w_gin = jnp.transpose(w_in[i], (1, 0, 2)).reshape(d, N_DEV * c_gin)
            w_in[i] = w_gin[:, :GDN_MAIN]
            w_ab[i] = jnp.pad(w_gin[:, GDN_MAIN:], ((0, 0), (0, AB_PAD - 2 * N_HEADS)))
            projm = _mm(y, w_in[i], "nn", [BF16], f"gdn_proj_{i}")
            projab = _mm(y, w_ab[i], "nn", [F32], f"gdn_proj_ab_{i}")
            o2, st_all, conv_y, dinv_all = _gdn_fwd_all(projm, projab, conv_full[j], gdn_a_log[j:j + 1],
                                                    gdn_dt_bias[j:j + 1], gdn_onorm[j:j + 1], seqs, f"gdn_fwd_{i}")
            mix = (projm, projab, conv_y, st_all, dinv_all)
        else:
            proj = _mm(y, w_in[i], "nn", [BF16], f"hgrn_proj_{i}")
            o2, o_raw, st_all = _hgrn_fwd_all(proj, lbs[i:i + 1], gnorm_full[j:j + 1], seqs, f"hgrn_fwd_{i}")
            mix = (proj, o_raw, st_all)
        (w_out[i],) = arrived(2 + 4 * i, o2, f"out_{i}")
        h1, y2 = _mm_rows(o2, w_out[i], "nn", [F32, BF16], f"mix_out_{i}", epilogue=_ep_residual_norm, extras=(h,),
                     vectors=(norm_mlp[i:i + 1],))
        (w_up[i],) = arrived(3 + 4 * i, y2, f"up_{i}")
        u, act = _mm(y2, w_up[i], "nn", [BF16, BF16], f"mlp_up_{i}",
                     epilogue=lambda acc: (acc, jnp.square(jnp.maximum(acc, 0.0))))
        (w_down[i],) = arrived(4 + 4 * i, act, f"down_{i}")
        saved.append((h, y, mix, o2, h1, y2, u, act))
        if i + 1 < DEPTH:
            h, y = _mm_rows(act, w_down[i], "nn", [F32, BF16], f"mlp_down_{i}", epilogue=_ep_residual_norm, extras=(h1,),
                       vectors=(norm_mix[i + 1:i + 2],))
        else:
            h = _mm(act, w_down[i], "nn", [F32], f"mlp_down_{i}", epilogue=lambda acc, res: (res + acc,),
                    extras=(h1,))

    dh, dh_b, d_nf, sq = _loss_head(h, norm_final.reshape(1, d), target, "loss_head")

    d_nmix, d_nmlp = [None] * DEPTH, [None] * DEPTH
    d_conv, d_alog, d_dtb, d_onorm = [None] * n_gdn, [None] * n_gdn, [None] * n_gdn, [None] * n_gdn
    d_lb = [jnp.zeros((1, d), F32)] * DEPTH
    d_gnorm = [None] * n_hgrn
    mlp_handles, mix_handles = [None] * DEPTH, [None] * DEPTH
    token = None
    for i in reversed(range(DEPTH)):
        j = i // 2
        h_in, y, mix, o2, h1, y2, u, act = saved[i]
        g_down = _mm(act, dh_b, "tn", [BF16], f"g_down_{i}", after=token)
        du = _mm(dh_b, w_down[i], "nt", [BF16], f"d_u_{i}",
                 epilogue=lambda acc, uu: (acc * (2.0 * jnp.maximum(uu.astype(F32), 0.0)),), extras=(u,))
        g_up = _mm(y2, du, "tn", [BF16], f"g_up_{i}")
        mlp_handles[i], token = _exchange_start(
            [[_Item(g_down, (N_DEV, r_down, d), _rows_of(r_down), _slot)],
             [_Item(g_up, (N_DEV, d, c_up), _cols_of(c_up), _slot)]], f"scatter_start_mlp_{i}")
        dh1, dh1_b, d_nmlp[i] = _mm_rows(du, w_up[i], "nt", [F32, BF16], f"d_y2_{i}", epilogue=_ep_norm_bwd,
                                     extras=(h1, dh), vectors=(norm_mlp[i:i + 1],), n_sums=1, after=token)
        g_out = _mm(o2, dh1_b, "tn", [BF16], f"g_out_{i}")
        do2 = _mm(dh1_b, w_out[i], "nt", [BF16], f"d_o2_{i}")
        if i % 2 == 0:
            projm, projab, conv_y, st_all, dinv_all = mix
            dpm, dpab, d_conv[j], d_alog[j], d_dtb[j], d_onorm[j] = _gdn_bwd_all(
                projm, projab, conv_y, conv_full[j], gdn_a_log[j:j + 1], gdn_dt_bias[j:j + 1], gdn_onorm[j:j + 1],
                st_all, dinv_all, do2, seqs, f"gdn_bwd_{i}")
            g_main = _mm(y, dpm, "tn", [BF16], f"g_in_{i}")
            g_ab = _mm(y, dpab, "tn", [BF16], f"g_in_ab_{i}")
            g_in = jnp.concatenate([g_main, g_ab[:, :2 * N_HEADS]], axis=1)
            g_in = jnp.transpose(g_in.reshape(d, N_DEV, c_gin), (1, 0, 2))
            in_item = _Item(g_in, (N_DEV, d, c_gin), _slot, _slot)
            dy_ab = _mm(dpab, w_ab[i], "nt", [F32], f"d_y_ab_{i}")
            dp, dy_extras = dpm, (dy_ab, h_in, dh1)
            dy_epilogue = lambda acc, e, xx, dres, w: _ep_norm_bwd(acc + e, xx, dres, w)
        else:
            proj, o_raw, st_all = mix
            dp, d_lb[i], d_gnorm[j] = _hgrn_bwd_all(proj, lbs[i:i + 1], gnorm_full[j:j + 1], st_all, o_raw, do2,
                                               seqs, f"hgrn_bwd_{i}")
            g_in = _mm(y, dp, "tn", [BF16], f"g_in_{i}")
            in_item = _Item(g_in, (N_DEV, d, c_hin), _cols_of(c_hin), _slot)
            dy_extras, dy_epilogue = (h_in, dh1), _ep_norm_bwd
        mix_handles[i], token = _exchange_start(
            [[_Item(g_out, (N_DEV, r_out, d), _rows_of(r_out), _slot)], [in_item]], f"scatter_start_mix_{i}")
        dh, dh_b, d_nmix[i] = _mm_rows(dp, w_in[i], "nt", [F32, BF16], f"d_y_{i}", epilogue=dy_epilogue, extras=dy_extras,
                                  vectors=(norm_mix[i:i + 1],), n_sums=1, after=token)
        token = None
    grad_x = dh.reshape(x.shape)

    def landed(handles, k, layers, after, name):
        return [_exchange_wait(handles[i][k], after, f"scatter_wait_{name}_{i}")[0] for i in layers]

    every, even, odd = range(DEPTH), range(0, DEPTH, 2), range(1, DEPTH, 2)
    upd = {}
    upd["mlp_w_down"] = _adamw_slots(mlp_w_down, landed(mlp_handles, 0, every, dh, "down"), m_mlp_w_down,
                                     v_mlp_w_down, "adamw_mlp_w_down")
    upd["mlp_w_up"] = _adamw_slots(mlp_w_up, landed(mlp_handles, 1, every, upd["mlp_w_down"][1], "up"), m_mlp_w_up,
                                   v_mlp_w_up, "adamw_mlp_w_up")
    upd["hgrn_w_out"] = _adamw_slots(hgrn_w_out, landed(mix_handles, 0, odd, upd["mlp_w_up"][1], "out"),
                                     m_hgrn_w_out, v_hgrn_w_out, "adamw_hgrn_w_out")
    upd["hgrn_w_in"] = _adamw_slots(hgrn_w_in, landed(mix_handles, 1, odd, upd["hgrn_w_out"][1], "in"), m_hgrn_w_in,
                                    v_hgrn_w_in, "adamw_hgrn_w_in")

    dlb_rows = jnp.concatenate(d_lb, axis=0)
    tail = jnp.concatenate(
        [jnp.concatenate(d_onorm, axis=1), jnp.concatenate(d_alog, axis=1), jnp.concatenate(d_dtb, axis=1)], axis=1)
    tail = jnp.pad(tail, ((0, 0), (0, d - tail.shape[1])))
    conv_rows = jnp.stack(d_conv).reshape(-1, d)
    packed = jnp.concatenate(
        [jnp.concatenate(d_nmix, axis=0), jnp.concatenate(d_nmlp, axis=0), d_nf, sq, dlb_rows,
         jnp.concatenate(d_gnorm, axis=0), tail, conv_rows], axis=0)
    pad_rows = (-packed.shape[0]) % 8
    packed = jnp.pad(packed, ((0, pad_rows), (0, 0)))
    tot = _all_reduce_small(packed, upd["hgrn_w_in"][1], "reduce_small")

    upd["gdn_w_out"] = _adamw_slots(gdn_w_out, landed(mix_handles, 0, even, tot, "out"),
                                    m_gdn_w_out, v_gdn_w_out, "adamw_gdn_w_out")
    upd["gdn_w_in"] = _adamw_slots(gdn_w_in, landed(mix_handles, 1, even, upd["gdn_w_out"][1], "in"), m_gdn_w_in,
                                   v_gdn_w_in, "adamw_gdn_w_in")

    def update(name, w, g, m, v):
        shape = w.shape
        c = shape[-1]
        res = _adamw(w.reshape(-1, c), g.reshape(-1, c), m.reshape(-1, c), v.reshape(-1, c), "adamw_" + name)
        return [g.reshape(shape)] + [o.reshape(shape) for o in res]

    r0 = 0
    g_nmix = tot[r0:r0 + DEPTH]; r0 += DEPTH
    g_nmlp = tot[r0:r0 + DEPTH]; r0 += DEPTH
    g_nf = tot[r0]; r0 += 1
    loss = tot[r0, 0]; r0 += 1
    g_lb = _lb_bwd(hgrn_lb_logits, tot[r0:r0 + DEPTH], "lb_bwd"); r0 += DEPTH
    g_gnorm_full = tot[r0:r0 + n_hgrn]; r0 += n_hgrn
    t_row = tot[r0]; r0 += 1
    g_conv_full = tot[r0:r0 + n_gdn * CONV_K * 3].reshape(n_gdn, CONV_K, 3 * d)
    g_onorm = t_row[0:n_gdn * HEAD_DIM].reshape(n_gdn, HEAD_DIM)
    o1 = n_gdn * HEAD_DIM
    g_alog = t_row[o1:o1 + n_gdn * N_HEADS].reshape(n_gdn, N_HEADS)
    g_dtb = t_row[o1 + n_gdn * N_HEADS:o1 + 2 * n_gdn * N_HEADS].reshape(n_gdn, N_HEADS)
    c_gn, c_cv = hgrn_gnorm.shape[1], gdn_conv.shape[2]
    g_gnorm = lax.dynamic_slice_in_dim(g_gnorm_full, me_i * c_gn, c_gn, axis=1)
    g_conv = lax.dynamic_slice_in_dim(g_conv_full, me_i * c_cv, c_cv, axis=2)

    upd["gdn_conv"] = update("gdn_conv", gdn_conv, g_conv, m_gdn_conv, v_gdn_conv)
    upd["gdn_a_log"] = update("gdn_a_log", gdn_a_log, g_alog, m_gdn_a_log, v_gdn_a_log)
    upd["gdn_dt_bias"] = update("gdn_dt_bias", gdn_dt_bias, g_dtb, m_gdn_dt_bias, v_gdn_dt_bias)
    upd["gdn_onorm"] = update("gdn_onorm", gdn_onorm, g_onorm, m_gdn_onorm, v_gdn_onorm)
    upd["hgrn_lb_logits"] = update("hgrn_lb_logits", hgrn_lb_logits, g_lb, m_hgrn_lb_logits, v_hgrn_lb_logits)
    upd["hgrn_gnorm"] = update("hgrn_gnorm", hgrn_gnorm, g_gnorm, m_hgrn_gnorm, v_hgrn_gnorm)
    upd["norm_mix"] = update("norm_mix", norm_mix, g_nmix, m_norm_mix, v_norm_mix)
    upd["norm_mlp"] = update("norm_mlp", norm_mlp, g_nmlp, m_norm_mlp, v_norm_mlp)
    upd["norm_final"] = update("norm_final", norm_final, g_nf, m_norm_final, v_norm_final)

    order = ["gdn_w_in", "gdn_conv", "gdn_a_log", "gdn_dt_bias", "gdn_onorm", "gdn_w_out", "hgrn_w_in",
             "hgrn_lb_logits", "hgrn_gnorm", "hgrn_w_out", "norm_mix", "norm_mlp", "mlp_w_up", "mlp_w_down",
             "norm_final"]
    outs = [loss, grad_x]
    for k in range(4):
        outs += [upd[name][k] for name in order]
    return tuple(outs)
```

```python
import functools

import numpy as np
import jax
import jax.numpy as jnp
from jax import lax
from jax.experimental import pallas as pl
from jax.experimental.pallas import tpu as pltpu

F32 = jnp.float32
BF16 = jnp.bfloat16

D_MODEL = 1024
N_HEADS = 8
HEAD_DIM = 128
CHUNK = 64
CONV_K = 4
HALO = 16
EPS = 1e-6
DEPTH = 4
N_DEV = 8
GDN_MAIN = 4 * D_MODEL
AB_PAD = 128
LANE_BLOCK = 256
ROW_BLOCK = 16
BLOCK_UNROLL = 4

ADAM_LR = 0.001
ADAM_B1 = 0.9
ADAM_B2 = 0.999
ADAM_EPS = 1e-08
ADAM_WD = 0.01
ADAM_STEP = 10

VMEM_LIMIT = 56 * 1024 * 1024
MM_TILE = 1024
MM_ROWS_MAX = 2048
MM_VMEM_BUDGET = 40 * 1024 * 1024
MM_ROWS_TILE = 512
_DIMS = {
    "nn": (((1,), (0,)), ((), ())),
    "nt": (((1,), (1,)), ((), ())),
    "tn": (((0,), (0,)), ((), ())),
}


def _parts(x, n):
    if n == 1 and x.dtype == BF16:
        return [x]
    out = []
    r = x.astype(F32)
    for i in range(n):
        p = r.astype(BF16)
        out.append(p)
        if i + 1 < n:
            r = r - p.astype(F32)
    return out


def _dot_raw(a, b, mode, na, nb):
    ap, bp = _parts(a, na), _parts(b, nb)
    nmax = max(na, nb)
    pairs = [(i, j) for i in range(na) for j in range(nb) if i + j < nmax]
    ka = 0 if mode == "tn" else 1
    kb = 1 if mode == "nt" else 0
    xa = ap[0] if len(pairs) == 1 else jnp.concatenate([ap[i] for i, _ in pairs], axis=ka)
    xb = bp[0] if len(pairs) == 1 else jnp.concatenate([bp[j] for _, j in pairs], axis=kb)
    return lax.dot_general(xa, xb, _DIMS[mode], preferred_element_type=F32)


@functools.partial(jax.custom_vjp, nondiff_argnums=(2, 3, 4))
def _dot(a, b, mode, na, nb):
    return _dot_raw(a, b, mode, na, nb)


def _dot_fwd(a, b, mode, na, nb):
    return _dot_raw(a, b, mode, na, nb), (a, b)


def _dot_bwd(mode, na, nb, res, ct):
    a, b = res
    if mode == "nn":
        da = _dot_raw(ct, b, "nt", 1, 1)
        db = _dot_raw(a, ct, "tn", 1, 1)
    elif mode == "nt":
        da = _dot_raw(ct, b, "nn", 1, 1)
        db = _dot_raw(ct, a, "tn", 1, 1)
    else:
        da = _dot_raw(b, ct, "nt", 1, 1)
        db = _dot_raw(a, ct, "nn", 1, 1)
    return da.astype(a.dtype), db.astype(b.dtype)


_dot.defvjp(_dot_fwd, _dot_bwd)


N_EXACT = 3


@jax.custom_vjp
def _dot01(x, m_wide, m):
    return lax.dot_general(m_wide, jnp.concatenate(_parts(x, N_EXACT), axis=0), _DIMS["nn"], preferred_element_type=F32)


def _dot01_fwd(x, m_wide, m):
    return _dot01(x, m_wide, m), (m_wide, m)


def _dot01_bwd(res, ct):
    m_wide, m = res
    dx = lax.dot_general(m, ct.astype(BF16), _DIMS["tn"], preferred_element_type=F32)
    return dx, jnp.zeros_like(m_wide), jnp.zeros_like(m)


_dot01.defvjp(_dot01_fwd, _dot01_bwd)


def _thrice(m):
    return jnp.concatenate([m] * N_EXACT, axis=1).astype(BF16), m.astype(BF16)


def _iota2(shape, dim):
    return lax.broadcasted_iota(jnp.int32, shape, dim)


def _tril_f32(n):
    return (_iota2((n, n), 0) >= _iota2((n, n), 1)).astype(F32)


def _below_block(n, b):
    ri, ci = _iota2((n, n), 0) // b, _iota2((n, n), 1) // b
    return (ri == ci + 1) & (ri % 2 == 1)


def _half_inverses(L):
    n = L.shape[0]
    eye = (_iota2((n, n), 0) == _iota2((n, n), 1)).astype(F32)
    d = eye - jnp.where(_below_block(n, 1), L, 0.0)
    b = 2
    while 2 * b < n:
        e = jnp.where(_below_block(n, b), L, 0.0)
        d = d - _dot_raw(d, _dot_raw(e, d, "nn", 2, 2), "nn", 2, 2)
        b *= 2
    return d, jnp.where(_below_block(n, b), L, 0.0)


def _solve_with(d, e, rhs):
    y = _dot_raw(d, rhs, "nn", 2, 2)
    return y - _dot_raw(d, _dot_raw(e, y, "nn", 2, 2), "nn", 2, 2)


@jax.custom_vjp
def _solve_unit_lower(L, rhs, d):
    n = L.shape[0]
    return _solve_with(d, jnp.where(_below_block(n, n // 2), L, 0.0), rhs)


def _solve_fwd(L, rhs, d):
    n = L.shape[0]
    e = jnp.where(_below_block(n, n // 2), L, 0.0)
    sol = _solve_with(d, e, rhs)
    return sol, (d, e, sol)


def _solve_bwd(res, ct):
    d, e, sol = res
    y = _dot_raw(d, ct - _dot_raw(e, _dot_raw(d, ct, "tn", 2, 2), "tn", 2, 2), "tn", 2, 2)
    return -_dot_raw(y, sol, "nt", 2, 2), y, jnp.zeros_like(d)


_solve_unit_lower.defvjp(_solve_fwd, _solve_bwd)


def _softplus(x):
    return jnp.maximum(x, 0.0) + jnp.log1p(jnp.exp(-jnp.abs(x)))


def _rms(x, w):
    return x * lax.rsqrt(jnp.mean(x * x, axis=-1, keepdims=True) + EPS) * w


HG_LEVELS = (32, 16, 8, 4, 2, 1)


def _hg_level_sums():
    i = np.arange(CHUNK)[:, None]
    m = np.arange(CHUNK)[None, :]
    to_row = [(m <= i) & (m // b == i // b) for b in HG_LEVELS]
    to_col = [(m > i) & (m // b == i // b) for b in HG_LEVELS if b > 1]
    return _thrice(jnp.asarray(np.concatenate(to_row + to_col + [m <= i]), F32))


def _hg_level_masks():
    i = np.arange(CHUNK)[:, None]
    j = np.arange(CHUNK)[None, :]
    return jnp.asarray(np.stack([(i // b == j // b + 1) & ((i // b) % 2 == 1) for b in HG_LEVELS]), F32)


def _hg_pre(qraw, f, lb, sums):
    g = jnp.log(lb + (1.0 - lb) * jax.nn.sigmoid(f))
    k = (1.0 - lb) * jax.nn.sigmoid(-f)
    q = jax.nn.silu(qraw) * (HEAD_DIM ** -0.5)
    return q, k, _dot01(g, *sums)


def _hg_head(st, q, k, v, e, masks):
    nl = len(HG_LEVELS)
    eye = (_iota2((CHUNK, CHUNK), 0) == _iota2((CHUNK, CHUNK), 1)).astype(F32)
    a = eye * jnp.sum(q * k, axis=-1, keepdims=True)
    for l, b in enumerate(HG_LEVELS):
        rows = q * jnp.exp(e[l * CHUNK:(l + 1) * CHUNK])
        cols = k * jnp.exp(e[(nl + l) * CHUNK:(nl + l + 1) * CHUNK]) if b > 1 else k
        a = a + masks[l] * _dot(rows, cols, "nt", 1, 1)
    gc = e[(2 * nl - 1) * CHUNK:2 * nl * CHUNK]
    o = _dot(a, v, "nn", 1, 1) + _dot(q * jnp.exp(gc), st, "nt", 1, 1)
    g_last = gc[CHUNK - 1:CHUNK]
    st_new = st * jnp.exp(g_last) + _dot(v, k * jnp.exp(g_last - gc), "tn", 1, 1)
    return o, st_new


_HG_HEADS = jax.vmap(_hg_head, in_axes=(0, 0, 0, 0, 0, None))


def _hg_post(o, gate, gw):
    return _rms(o, gw) * jax.nn.silu(gate)


def _gd_conv(xp, cw):
    off = HALO - (CONV_K - 1)
    y = cw[0:1] * xp[off:off + CHUNK]
    for kk in range(1, CONV_K):
        y = y + cw[kk:kk + 1] * xp[off + kk:off + kk + CHUNK]
    return y


def _gd_conv_bwd(xp, cw, y, dc):
    off = HALO - (CONV_K - 1)
    sig = jax.nn.sigmoid(y)
    dy = dc * (sig * (1.0 + y * (1.0 - sig)))
    dxp, dcw = None, []
    for kk in range(CONV_K):
        moved = jnp.pad(dy, ((off + kk, HALO - off - kk), (0, 0)))
        term = cw[kk:kk + 1] * moved
        dxp = term if dxp is None else dxp + term
        dcw.append(jnp.sum(xp * moved, axis=0, keepdims=True))
    return dxp, jnp.concatenate(dcw, axis=0)


def _gd_gates(a, b, alog, dtb):
    beta = jax.nn.sigmoid(b)
    g = -jnp.exp(alog) * _softplus(a + dtb)
    expand = (_iota2((N_HEADS, D_MODEL), 1) // HEAD_DIM == _iota2((N_HEADS, D_MODEL), 0)).astype(F32)
    g_x = _dot(g, expand, "nn", 3, 1)
    after = (_iota2((CHUNK, D_MODEL), 0) > _iota2((CHUNK, D_MODEL), 1) % HEAD_DIM).astype(F32)
    sums = _dot01(jnp.concatenate([g_x, g_x * after], axis=1), *_thrice(_tril_f32(CHUNK)))
    return _dot(beta, expand, "nn", 3, 1), sums


def _gd_head(st, q, k, v, beta, gc, diff, gate, onw, dinv=None):
    q = q * lax.rsqrt(jnp.sum(q * q, axis=-1, keepdims=True) + EPS) * (HEAD_DIM ** -0.5)
    k = k * lax.rsqrt(jnp.sum(k * k, axis=-1, keepdims=True) + EPS)
    ri = _iota2((CHUNK, CHUNK), 0)
    ci = _iota2((CHUNK, CHUNK), 1)
    decay = jnp.exp(jnp.where(ri >= ci, diff[:, 0:CHUNK], -jnp.inf))
    kb = k * beta
    egc = jnp.exp(gc)
    L = jnp.where(ri > ci, _dot(kb, k, "nt", 1, 1) * decay, 0.0)
    made = dinv is None
    if made:
        dinv = _half_inverses(L)[0]
    sol = _solve_unit_lower(L, jnp.concatenate([v * beta, kb * egc], axis=1), dinv)
    u = sol[:, 0:HEAD_DIM]
    w = sol[:, HEAD_DIM:2 * HEAD_DIM]
    a_qk = jnp.where(ri >= ci, _dot(q, k, "nt", 1, 1) * decay, 0.0)
    g_last = gc[CHUNK - 1:CHUNK]
    v_new = u - _dot(w, st, "nt", 1, 1)
    o = _dot(q * egc, st, "nt", 1, 1) + _dot(a_qk, v_new, "nn", 1, 1)
    st_new = st * jnp.exp(g_last) + _dot(v_new, k * jnp.exp(g_last - gc), "tn", 1, 1)
    out = (_rms(o, onw) * jax.nn.silu(gate), st_new)
    return out + (dinv,) if made else out


def _params(*sem):
    return pltpu.CompilerParams(dimension_semantics=sem, vmem_limit_bytes=VMEM_LIMIT)


def _tile(n, pref):
    t = min(n, pref)
    assert n % t == 0, (n, pref)
    return t


def _mm_tiles(m, n, k, a_size, b_size, tile_sizes):
    tn = _tile(n, MM_TILE)

    def need(tm, tk):
        acc = 4 * tm * tn * (2 if tk < k else 1)
        return 2 * (tm * tk * a_size + tk * tn * b_size + tm * tn * sum(tile_sizes)) + acc

    tk = k
    while True:
        tm = _tile(m, MM_ROWS_MAX)
        while tm > 256 and need(tm, tk) > MM_VMEM_BUDGET:
            tm //= 2
        if need(tm, tk) <= MM_VMEM_BUDGET or tk <= 512:
            return tm, tn, tk
        tk //= 2


def _mm(a, b, mode, out_dtypes, name, epilogue=None, extras=(), after=None):
    if mode == "nn":
        (m, k), (k2, n) = a.shape, b.shape
    elif mode == "nt":
        (m, k), (n, k2) = a.shape, b.shape
    else:
        (k, m), (k2, n) = a.shape, b.shape
    assert k == k2, (a.shape, b.shape, mode)
    tm, tn, tk = _mm_tiles(m, n, k, a.dtype.itemsize, b.dtype.itemsize,
                           [e.dtype.itemsize for e in extras] + [jnp.dtype(dt).itemsize for dt in out_dtypes])
    nk = k // tk
    ne, no, nafter = len(extras), len(out_dtypes), int(after is not None)
    if epilogue is None:
        epilogue = lambda acc: (acc,)

    def body(*refs):
        a_ref, b_ref = refs[0], refs[1]
        ex = refs[2:2 + ne]
        outs = refs[2 + ne + nafter:2 + ne + nafter + no]
        part = lax.dot_general(a_ref[...].astype(BF16), b_ref[...].astype(BF16), _DIMS[mode],
                               preferred_element_type=F32)

        def finish(acc):
            for o_ref, val in zip(outs, epilogue(acc, *[e[...] for e in ex])):
                o_ref[...] = val.astype(o_ref.dtype)

        if nk == 1:
            finish(part)
        else:
            acc_ref = refs[-1]
            kk = pl.program_id(2)

            @pl.when(kk == 0)
            def _():
                acc_ref[...] = part

            @pl.when(kk > 0)
            def _():
                acc_ref[...] += part

            @pl.when(kk == nk - 1)
            def _():
                finish(acc_ref[...])

    if mode == "tn":
        a_spec = pl.BlockSpec((tk, tm), lambda i, j, kk: (kk, i))
    else:
        a_spec = pl.BlockSpec((tm, tk), lambda i, j, kk: (i, kk))
    if mode == "nt":
        b_spec = pl.BlockSpec((tn, tk), lambda i, j, kk: (j, kk))
    else:
        b_spec = pl.BlockSpec((tk, tn), lambda i, j, kk: (kk, j))
    o_spec = pl.BlockSpec((tm, tn), lambda i, j, kk: (i, j))
    res = pl.pallas_call(
        body,
        name=name,
        grid=(m // tm, n // tn, nk),
        in_specs=[a_spec, b_spec] + [o_spec] * ne + [pl.BlockSpec(memory_space=pl.ANY)] * nafter,
        out_specs=[o_spec] * no,
        out_shape=[jax.ShapeDtypeStruct((m, n), dt) for dt in out_dtypes],
        scratch_shapes=[pltpu.VMEM((tm, tn), F32)] if nk > 1 else [],
        compiler_params=_params("parallel", "parallel", "arbitrary"),
    )(a, b, *extras, *([after] if nafter else []))
    return res[0] if no == 1 else res


def _mm_rows(a, b, mode, out_dtypes, name, epilogue, extras=(), vectors=(), n_sums=0, after=None):
    assert mode in ("nn", "nt")
    (m, k), n = a.shape, (b.shape[1] if mode == "nn" else b.shape[0])
    tm = _tile(m, MM_ROWS_TILE)
    mt = m // tm
    ne, no, nafter = len(extras) + len(vectors), len(out_dtypes), int(after is not None)

    def body(*refs):
        a_ref, b_ref = refs[0], refs[1]
        ex = refs[2:2 + ne]
        outs = refs[2 + ne + nafter:2 + ne + nafter + no]
        sums = refs[2 + ne + nafter + no:2 + ne + nafter + no + n_sums]
        acc_ref = refs[-1]
        i = pl.program_id(0)

        @pl.when(i == 0)
        def _():
            acc_ref[1] = jnp.zeros((tm, n), F32)

        vals = epilogue(acc_ref[1 - i % 2], *[e[...] for e in ex])
        acc_ref[i % 2] = lax.dot_general(a_ref[...].astype(BF16), b_ref[...].astype(BF16), _DIMS[mode],
                                         preferred_element_type=F32)
        for o_ref, val in zip(outs, vals[:no]):
            o_ref[...] = val.astype(o_ref.dtype)
        for s_ref, val in zip(sums, vals[no:]):
            @pl.when(i <= 1)
            def _(s_ref=s_ref, val=val):
                s_ref[...] = val

            @pl.when(i > 1)
            def _(s_ref=s_ref, val=val):
                s_ref[...] += val

    ahead = lambda i: (jnp.minimum(i, mt - 1), 0)
    behind = lambda i: (jnp.maximum(i - 1, 0), 0)
    fixed = lambda i: (0, 0)
    row = pl.BlockSpec((tm, n), behind)
    vec = pl.BlockSpec((1, n), fixed)
    res = pl.pallas_call(
        body, name=name, grid=(mt + 1,),
        in_specs=([pl.BlockSpec((tm, k), ahead), pl.BlockSpec(b.shape, fixed)] + [row] * len(extras)
                  + [vec] * len(vectors) + [pl.BlockSpec(memory_space=pl.ANY)] * nafter),
        out_specs=[row] * no + [vec] * n_sums,
        out_shape=[jax.ShapeDtypeStruct((m, n), dt) for dt in out_dtypes] + [jax.ShapeDtypeStruct((1, n), F32)] * n_sums,
        scratch_shapes=[pltpu.VMEM((2, tm, n), F32)],
        compiler_params=_params("arbitrary"),
    )(a, b, *extras, *vectors, *([after] if nafter else []))
    return res[0] if no + n_sums == 1 else res


def _ep_residual_norm(acc, res, w):
    h = res + acc
    return h, _rms(h, w)


def _ep_norm_bwd(acc, x, dres, w):
    r = lax.rsqrt(jnp.mean(x * x, axis=-1, keepdims=True) + EPS)
    g = acc * w
    dx = dres + (r * g - x * (r * r * r * jnp.mean(g * x, axis=-1, keepdims=True)))
    return dx, dx, jnp.sum(acc * (x * r), axis=0, keepdims=True)


def _rms_fwd(x, w, name, tm=512):
    n, d = x.shape
    tm = _tile(n, tm)

    def body(x_ref, w_ref, y_ref):
        y_ref[...] = _rms(x_ref[...], w_ref[...]).astype(y_ref.dtype)

    return pl.pallas_call(
        body, name=name, grid=(n // tm,),
        in_specs=[pl.BlockSpec((tm, d), lambda i: (i, 0)), pl.BlockSpec((1, d), lambda i: (0, 0))],
        out_specs=pl.BlockSpec((tm, d), lambda i: (i, 0)),
        out_shape=jax.ShapeDtypeStruct((n, d), BF16),
        compiler_params=_params("arbitrary"),
    )(x, w)


def _loss_head(h, w, target, name, tm=512):
    n, d = h.shape
    tm = _tile(n, tm)

    def body(h_ref, w_ref, t_ref, dh_ref, dhb_ref, dw_ref, sq_ref):
        y, vjp = jax.vjp(_rms, h_ref[...], w_ref[...])
        err = y - t_ref[...]
        dh, dw = vjp(err * (1.0 / d))
        dh_ref[...] = dh
        dhb_ref[...] = dh.astype(dhb_ref.dtype)
        sq = jnp.sum(err * err, axis=0, keepdims=True)

        @pl.when(pl.program_id(0) == 0)
        def _():
            dw_ref[...] = dw
            sq_ref[...] = sq

        @pl.when(pl.program_id(0) > 0)
        def _():
            dw_ref[...] += dw
            sq_ref[...] += sq

        @pl.when(pl.program_id(0) == n // tm - 1)
        def _():
            total = jnp.sum(sq_ref[...], axis=1, keepdims=True) * (0.5 / d)
            sq_ref[...] = jnp.broadcast_to(total, sq_ref.shape)

    row = pl.BlockSpec((tm, d), lambda i: (i, 0))
    vec = pl.BlockSpec((1, d), lambda i: (0, 0))
    return pl.pallas_call(
        body, name=name, grid=(n // tm,),
        in_specs=[row, vec, row],
        out_specs=[row, row, vec, vec],
        out_shape=[jax.ShapeDtypeStruct((n, d), F32), jax.ShapeDtypeStruct((n, d), BF16),
                   jax.ShapeDtypeStruct((1, d), F32), jax.ShapeDtypeStruct((1, d), F32)],
        compiler_params=_params("arbitrary"),
    )(h, w, target)


def _lower_bounds(logits):
    sm = jax.nn.softmax(logits, axis=0)
    rows = [sm[0:1] * 0.0]
    for r in range(1, DEPTH):
        rows.append(rows[-1] + sm[r:r + 1])
    return jnp.concatenate(rows, axis=0)


def _lb_fwd(logits, name):
    def body(l_ref, o_ref):
        o_ref[...] = _lower_bounds(l_ref[...])

    return pl.pallas_call(body, name=name, out_shape=jax.ShapeDtypeStruct(logits.shape, F32))(logits)


def _lb_bwd(logits, dlb, name):
    def body(l_ref, d_ref, o_ref):
        _, vjp = jax.vjp(_lower_bounds, l_ref[...])
        (o_ref[...],) = vjp(d_ref[...])

    return pl.pallas_call(body, name=name, out_shape=jax.ShapeDtypeStruct(logits.shape, F32))(logits, dlb)


def _head_slice(h):
    return pl.ds(h * HEAD_DIM, HEAD_DIM)


_GD_HEADS = jax.vmap(_gd_head, in_axes=(0, 0, 0, 0, 0, 0, 0, 0, None))
_GD_HEADS_AGAIN = jax.vmap(_gd_head, in_axes=(0, 0, 0, 0, 0, 0, 0, 0, None, 0))


def _lane_blocks(width, block_body):
    def trip(j, carry):
        block_body(lambda base=0: pl.ds(pl.multiple_of(j * LANE_BLOCK + base, LANE_BLOCK), LANE_BLOCK))
        return carry

    lax.fori_loop(0, width // LANE_BLOCK, trip, 0, unroll=BLOCK_UNROLL)


def _row_blocks(rows, block_body):
    def trip(j, carry):
        block_body(pl.ds(pl.multiple_of(j * ROW_BLOCK, ROW_BLOCK), ROW_BLOCK))
        return carry

    lax.fori_loop(0, rows // ROW_BLOCK, trip, 0, unroll=BLOCK_UNROLL)


def _hg_pre_block(p_ref, lb_ref, sums_refs, q_sc, k_sc, v_sc, e_sc, at):
    sl = at()
    q_sc[:, sl], k_sc[:, sl], e_sc[:, sl] = _hg_pre(
        p_ref[:, sl].astype(F32), p_ref[:, at(D_MODEL)].astype(F32), lb_ref[:, sl], [r[...] for r in sums_refs])
    v_sc[:, sl] = p_ref[:, at(2 * D_MODEL)].astype(F32)


def _gd_xp(halo_ref, p_ref, sl, first_chunk):
    halo = jnp.where(first_chunk, 0.0, halo_ref[:, sl].astype(F32))
    return jnp.concatenate([halo, p_ref[:, sl].astype(F32)], axis=0)


def _stack_all(ref, first=0):
    return jnp.stack([ref[s, :, _head_slice(h + first)] for s in range(ref.shape[0]) for h in range(N_HEADS)])


def _unstack_all(ref, val, first=0):
    for s in range(ref.shape[0]):
        for h in range(N_HEADS):
            ref[s, :, _head_slice(h + first)] = val[s * N_HEADS + h].astype(ref.dtype)


def _gdn_fwd_all(projm, projab, cw, alog, dtb, onw, seqs, name):
    n = projm.shape[0]
    t = n // seqs
    nc = t // CHUNK
    d = D_MODEL
    per_halo = CHUNK // HALO
    nh = seqs * N_HEADS

    def body(p_ref, halo_ref, ab_ref, cw_ref, alog_ref, dtb_ref, onw_ref, o2_ref, st_all_ref, y_ref, dinv_ref,
             st_sc, c_sc, beta_sc, g_sc):
        first_chunk = pl.program_id(0) == 0

        @pl.when(first_chunk)
        def _():
            st_sc[...] = jnp.zeros_like(st_sc)

        for s in range(seqs):
            def conv(at, s=s):
                sl = at()
                y = _gd_conv(_gd_xp(halo_ref.at[s], p_ref.at[s], sl, first_chunk), cw_ref[:, sl])
                y_ref[s, :, sl] = y
                c_sc[s, :, sl] = jax.nn.silu(y)

            _lane_blocks(3 * d, conv)
            beta_sc[s], g_sc[s] = _gd_gates(ab_ref[s, :, 0:N_HEADS], ab_ref[s, :, N_HEADS:2 * N_HEADS],
                                            alog_ref[...], dtb_ref[...])
        st_all_ref[0] = st_sc[...]
        o2, st_sc[...], dinv_ref[0] = _GD_HEADS(
            st_sc[...], _stack_all(c_sc), _stack_all(c_sc, N_HEADS), _stack_all(c_sc, 2 * N_HEADS), _stack_all(beta_sc),
            _stack_all(g_sc), _stack_all(g_sc, N_HEADS), _stack_all(p_ref, 3 * N_HEADS).astype(F32), onw_ref[...])
        _unstack_all(o2_ref, o2)

    rows = lambda c: (0, c, 0)
    const = lambda c: (0, 0)
    per_chunk = lambda c: (c, 0, 0, 0)
    p3 = projm.reshape(seqs, t, 4 * d)
    o2, st_all, conv_y, dinv_all = pl.pallas_call(
        body, name=name, grid=(nc,),
        in_specs=[pl.BlockSpec((seqs, CHUNK, 4 * d), rows),
                  pl.BlockSpec((seqs, HALO, 3 * d), lambda c: (0, jnp.maximum(c * per_halo - 1, 0), 0)),
                  pl.BlockSpec((seqs, CHUNK, AB_PAD), rows),
                  pl.BlockSpec((CONV_K, 3 * d), const), pl.BlockSpec((1, N_HEADS), const),
                  pl.BlockSpec((1, N_HEADS), const), pl.BlockSpec((1, HEAD_DIM), const)],
        out_specs=[pl.BlockSpec((seqs, CHUNK, d), rows), pl.BlockSpec((1, nh, HEAD_DIM, HEAD_DIM), per_chunk),
                   pl.BlockSpec((seqs, CHUNK, 3 * d), rows), pl.BlockSpec((1, nh, CHUNK, CHUNK), per_chunk)],
        out_shape=[jax.ShapeDtypeStruct((seqs, t, d), BF16), jax.ShapeDtypeStruct((nc, nh, HEAD_DIM, HEAD_DIM), F32),
                   jax.ShapeDtypeStruct((seqs, t, 3 * d), F32), jax.ShapeDtypeStruct((nc, nh, CHUNK, CHUNK), F32)],
        scratch_shapes=[pltpu.VMEM((nh, HEAD_DIM, HEAD_DIM), F32), pltpu.VMEM((seqs, CHUNK, 3 * d), F32),
                        pltpu.VMEM((seqs, CHUNK, d), F32), pltpu.VMEM((seqs, CHUNK, 2 * d), F32)],
        compiler_params=_params("arbitrary"),
    )(p3, p3, projab.reshape(seqs, t, AB_PAD), cw, alog, dtb, onw)
    return o2.reshape(n, d), st_all, conv_y, dinv_all


def _gdn_bwd_all(projm, projab, conv_y, cw, alog, dtb, onw, st_all, dinv_all, do2, seqs, name):
    n = projm.shape[0]
    t = n // seqs
    nc = t // CHUNK
    d = D_MODEL
    per_halo = CHUNK // HALO
    nh = seqs * N_HEADS

    def body(p_ref, halo_ref, ab_ref, y_ref, cw_ref, alog_ref, dtb_ref, onw_ref, st_all_ref, dinv_ref, do2_ref,
             dp_ref, dab_ref, dcw_ref, dalog_ref, ddtb_ref, donw_ref,
             dst_sc, dhalo_sc, c_sc, beta_sc, g_sc, dc_sc, dbeta_sc, dg_sc):
        first = pl.program_id(0) == 0
        first_chunk = pl.program_id(0) == nc - 1

        @pl.when(first)
        def _():
            dst_sc[...] = jnp.zeros_like(dst_sc)
            dhalo_sc[...] = jnp.zeros_like(dhalo_sc)

        gates_vjps = []
        for s in range(seqs):
            def act(at, s=s):
                c_sc[s, :, at()] = jax.nn.silu(y_ref[s, :, at()])

            _lane_blocks(3 * d, act)
            (beta_sc[s], g_sc[s]), gates_vjp = jax.vjp(
                _gd_gates, ab_ref[s, :, 0:N_HEADS], ab_ref[s, :, N_HEADS:2 * N_HEADS], alog_ref[...], dtb_ref[...])
            gates_vjps.append(gates_vjp)

        dinv = dinv_ref[0]
        _, vjp = jax.vjp(
            lambda *a: _GD_HEADS_AGAIN(*a, dinv), st_all_ref[0], _stack_all(c_sc), _stack_all(c_sc, N_HEADS),
            _stack_all(c_sc, 2 * N_HEADS), _stack_all(beta_sc), _stack_all(g_sc), _stack_all(g_sc, N_HEADS),
            _stack_all(p_ref, 3 * N_HEADS).astype(F32), onw_ref[...])
        dst_sc[...], dq, dk, dv, dbeta, dg, ddiff, dgate, donw = vjp((_stack_all(do2_ref).astype(F32), dst_sc[...]))
        _unstack_all(dc_sc, dq)
        _unstack_all(dc_sc, dk, N_HEADS)
        _unstack_all(dc_sc, dv, 2 * N_HEADS)
        _unstack_all(dbeta_sc, dbeta)
        _unstack_all(dg_sc, dg)
        _unstack_all(dg_sc, ddiff, N_HEADS)
        _unstack_all(dp_ref, dgate, 3 * N_HEADS)

        dalog, ddtb = None, None
        for s in range(seqs):
            def conv_bwd(at, s=s):
                sl = at()
                dxp, dcw = _gd_conv_bwd(_gd_xp(halo_ref.at[s], p_ref.at[s], sl, first_chunk), cw_ref[:, sl],
                                        y_ref[s, :, sl], dc_sc[s, :, sl])
                dqkv = jnp.concatenate([dxp[HALO:CHUNK], dxp[CHUNK:HALO + CHUNK] + dhalo_sc[s, :, sl]], axis=0)
                dp_ref[s, :, sl] = dqkv.astype(dp_ref.dtype)
                dhalo_sc[s, :, sl] = dxp[0:HALO]

                if s > 0:
                    dcw_ref[:, sl] += dcw
                    return

                @pl.when(first)
                def _():
                    dcw_ref[:, sl] = dcw

                @pl.when(jnp.logical_not(first))
                def _():
                    dcw_ref[:, sl] += dcw

            _lane_blocks(3 * d, conv_bwd)
            da, db, dalog_s, ddtb_s = gates_vjps[s]((dbeta_sc[s], dg_sc[s]))
            dab_ref[s] = jnp.concatenate(
                [da, db, jnp.zeros((CHUNK, AB_PAD - 2 * N_HEADS), F32)], axis=1).astype(dab_ref.dtype)
            dalog = dalog_s if dalog is None else dalog + dalog_s
            ddtb = ddtb_s if ddtb is None else ddtb + ddtb_s

        @pl.when(first)
        def _():
            dalog_ref[...] = dalog
            ddtb_ref[...] = ddtb
            donw_ref[...] = donw

        @pl.when(jnp.logical_not(first))
        def _():
            dalog_ref[...] += dalog
            ddtb_ref[...] += ddtb
            donw_ref[...] += donw

    back = lambda c: nc - 1 - c
    rows = lambda c: (0, back(c), 0)
    const = lambda c: (0, 0)
    per_chunk = lambda c: (back(c), 0, 0, 0)
    small = [pl.BlockSpec((CONV_K, 3 * d), const), pl.BlockSpec((1, N_HEADS), const),
             pl.BlockSpec((1, N_HEADS), const), pl.BlockSpec((1, HEAD_DIM), const)]
    p3 = projm.reshape(seqs, t, 4 * d)
    dp, dab, dcw, dalog, ddtb, donw = pl.pallas_call(
        body, name=name, grid=(nc,),
        in_specs=[pl.BlockSpec((seqs, CHUNK, 4 * d), rows),
                  pl.BlockSpec((seqs, HALO, 3 * d), lambda c: (0, jnp.maximum(back(c) * per_halo - 1, 0), 0)),
                  pl.BlockSpec((seqs, CHUNK, AB_PAD), rows), pl.BlockSpec((seqs, CHUNK, 3 * d), rows)] + small + [
                  pl.BlockSpec((1, nh, HEAD_DIM, HEAD_DIM), per_chunk), pl.BlockSpec((1, nh, CHUNK, CHUNK), per_chunk),
                  pl.BlockSpec((seqs, CHUNK, d), rows)],
        out_specs=[pl.BlockSpec((seqs, CHUNK, 4 * d), rows), pl.BlockSpec((seqs, CHUNK, AB_PAD), rows)] + small,
        out_shape=[jax.ShapeDtypeStruct((seqs, t, 4 * d), BF16), jax.ShapeDtypeStruct((seqs, t, AB_PAD), BF16),
                   jax.ShapeDtypeStruct((CONV_K, 3 * d), F32), jax.ShapeDtypeStruct((1, N_HEADS), F32),
                   jax.ShapeDtypeStruct((1, N_HEADS), F32), jax.ShapeDtypeStruct((1, HEAD_DIM), F32)],
        scratch_shapes=[pltpu.VMEM((nh, HEAD_DIM, HEAD_DIM), F32), pltpu.VMEM((seqs, HALO, 3 * d), F32),
                        pltpu.VMEM((seqs, CHUNK, 3 * d), F32), pltpu.VMEM((seqs, CHUNK, d), F32),
                        pltpu.VMEM((seqs, CHUNK, 2 * d), F32), pltpu.VMEM((seqs, CHUNK, 3 * d), F32),
                        pltpu.VMEM((seqs, CHUNK, d), F32), pltpu.VMEM((seqs, CHUNK, 2 * d), F32)],
        compiler_params=_params("arbitrary"),
    )(p3, p3, projab.reshape(seqs, t, AB_PAD), conv_y, cw, alog, dtb, onw, st_all, dinv_all,
      do2.reshape(seqs, t, d))
    return dp.reshape(n, 4 * d), dab.reshape(n, AB_PAD), dcw, dalog, ddtb, donw


def _hgrn_fwd_all(proj, lb, gw, seqs, name):
    n = proj.shape[0]
    t = n // seqs
    nc = t // CHUNK
    d = D_MODEL
    nh = seqs * N_HEADS
    sums, masks = _hg_level_sums(), _hg_level_masks()

    def body(p_ref, lb_ref, gw_ref, sums_wide_ref, sums_once_ref, masks_ref, o2_ref, o_ref, st_all_ref,
             st_sc, q_sc, k_sc, v_sc, e_sc):
        @pl.when(pl.program_id(0) == 0)
        def _():
            st_sc[...] = jnp.zeros_like(st_sc)

        sums_refs = (sums_wide_ref, sums_once_ref)
        for s in range(seqs):
            _lane_blocks(d, functools.partial(_hg_pre_block, p_ref.at[s], lb_ref, sums_refs, q_sc.at[s], k_sc.at[s],
                                              v_sc.at[s], e_sc.at[s]))
        st_all_ref[0] = st_sc[...]
        for s in range(seqs):
            one, mine = pl.ds(s, 1), pl.ds(s * N_HEADS, N_HEADS)
            o, st_sc[mine] = _HG_HEADS(st_sc[mine], *[_stack_all(r.at[one]) for r in (q_sc, k_sc, v_sc, e_sc)],
                                       masks_ref[...])
            _unstack_all(o_ref.at[one], o)

            def post(rows, s=s):
                gate = p_ref[s, rows, 3 * d:4 * d].astype(F32)
                o2_ref[s, rows, :] = _hg_post(o_ref[s, rows, :], gate, gw_ref[...]).astype(o2_ref.dtype)

            _row_blocks(CHUNK, post)

    rows = lambda c: (0, c, 0)
    vec = pl.BlockSpec((1, d), lambda c: (0, 0))
    act = pl.BlockSpec((seqs, CHUNK, d), rows)
    o2, o, st_all = pl.pallas_call(
        body, name=name, grid=(nc,),
        in_specs=[pl.BlockSpec((seqs, CHUNK, 4 * d), rows), vec, vec]
        + [pl.BlockSpec(m.shape, lambda c: (0, 0)) for m in sums] + [pl.BlockSpec(masks.shape, lambda c: (0, 0, 0))],
        out_specs=[act, act, pl.BlockSpec((1, nh, HEAD_DIM, HEAD_DIM), lambda c: (c, 0, 0, 0))],
        out_shape=[jax.ShapeDtypeStruct((seqs, t, d), BF16), jax.ShapeDtypeStruct((seqs, t, d), F32),
                   jax.ShapeDtypeStruct((nc, nh, HEAD_DIM, HEAD_DIM), F32)],
        scratch_shapes=[pltpu.VMEM((nh, HEAD_DIM, HEAD_DIM), F32)] + [pltpu.VMEM((seqs, CHUNK, d), F32)] * 3
        + [pltpu.VMEM((seqs, sums[0].shape[0], d), F32)],
        compiler_params=_params("arbitrary"),
    )(proj.reshape(seqs, t, 4 * d), lb, gw, *sums, masks)
    return o2.reshape(n, d), o, st_all


def _hgrn_bwd_all(proj, lb, gw, st_all, o, do2, seqs, name):
    n = proj.shape[0]
    t = n // seqs
    nc = t // CHUNK
    d = D_MODEL
    nh = seqs * N_HEADS
    sums, masks = _hg_level_sums(), _hg_level_masks()

    def body(p_ref, lb_ref, gw_ref, sums_wide_ref, sums_once_ref, masks_ref, st_all_ref, o_ref, do2_ref,
             dp_ref, dlb_ref, dgw_ref,
             dst_sc, q_sc, k_sc, v_sc, e_sc, do_sc, dq_sc, dk_sc, dv_sc, de_sc, dgw_sc):
        first = pl.program_id(0) == 0

        @pl.when(first)
        def _():
            dst_sc[...] = jnp.zeros_like(dst_sc)

        sums_refs = (sums_wide_ref, sums_once_ref)
        dgw_sc[...] = jnp.zeros_like(dgw_sc)
        for s in range(seqs):
            _lane_blocks(d, functools.partial(_hg_pre_block, p_ref.at[s], lb_ref, sums_refs, q_sc.at[s], k_sc.at[s],
                                              v_sc.at[s], e_sc.at[s]))

            def post_bwd(rows, s=s):
                _, vjp = jax.vjp(_hg_post, o_ref[s, rows, :], p_ref[s, rows, 3 * d:4 * d].astype(F32), gw_ref[...])
                do_sc[s, rows, :], dgate, dgw = vjp(do2_ref[s, rows, :].astype(F32))
                dp_ref[s, rows, 3 * d:4 * d] = dgate.astype(dp_ref.dtype)
                dgw_sc[...] += dgw

            _row_blocks(CHUNK, post_bwd)

        level_masks = masks_ref[...]
        _, vjp = jax.vjp(lambda *a: _HG_HEADS(*a, level_masks), st_all_ref[0],
                         *[_stack_all(r) for r in (q_sc, k_sc, v_sc, e_sc)])
        grads = vjp((_stack_all(do_sc), dst_sc[...]))
        dst_sc[...] = grads[0]
        for r, val in zip((dq_sc, dk_sc, dv_sc, de_sc), grads[1:]):
            _unstack_all(r, val)

        for s in range(seqs):
            def pre_bwd(at, s=s):
                sl = at()
                level_sums = (sums_wide_ref[...], sums_once_ref[...])
                _, vjp = jax.vjp(lambda qraw, f, lb: _hg_pre(qraw, f, lb, level_sums), p_ref[s, :, sl].astype(F32),
                                 p_ref[s, :, at(d)].astype(F32), lb_ref[:, sl])
                dqraw, df, dlb = vjp((dq_sc[s, :, sl], dk_sc[s, :, sl], de_sc[s, :, sl]))
                dp_ref[s, :, sl] = dqraw.astype(dp_ref.dtype)
                dp_ref[s, :, at(d)] = df.astype(dp_ref.dtype)
                dp_ref[s, :, at(2 * d)] = dv_sc[s, :, sl].astype(dp_ref.dtype)
                if s > 0:
                    dlb_ref[:, sl] += dlb
                    return

                @pl.when(first)
                def _():
                    dlb_ref[:, sl] = dlb

                @pl.when(jnp.logical_not(first))
                def _():
                    dlb_ref[:, sl] += dlb

            _lane_blocks(d, pre_bwd)

        @pl.when(first)
        def _():
            dgw_ref[...] = dgw_sc[...]

        @pl.when(jnp.logical_not(first))
        def _():
            dgw_ref[...] += dgw_sc[...]

    rows = lambda c: (0, nc - 1 - c, 0)
    vec = pl.BlockSpec((1, d), lambda c: (0, 0))
    act = pl.BlockSpec((seqs, CHUNK, d), rows)
    wide = pl.BlockSpec((seqs, CHUNK, 4 * d), rows)
    e_rows = sums[0].shape[0]
    dp, dlb, dgw = pl.pallas_call(
        body, name=name, grid=(nc,),
        in_specs=[wide, vec, vec] + [pl.BlockSpec(m.shape, lambda c: (0, 0)) for m in sums] + [
                  pl.BlockSpec(masks.shape, lambda c: (0, 0, 0)),
                  pl.BlockSpec((1, nh, HEAD_DIM, HEAD_DIM), lambda c: (nc - 1 - c, 0, 0, 0)), act, act],
        out_specs=[wide, vec, vec],
        out_shape=[jax.ShapeDtypeStruct((seqs, t, 4 * d), BF16), jax.ShapeDtypeStruct((1, d), F32),
                   jax.ShapeDtypeStruct((1, d), F32)],
        scratch_shapes=[pltpu.VMEM((nh, HEAD_DIM, HEAD_DIM), F32)]
        + [pltpu.VMEM((seqs, CHUNK, d), F32)] * 3 + [pltpu.VMEM((seqs, e_rows, d), F32)]
        + [pltpu.VMEM((seqs, CHUNK, d), F32)] * 4 + [pltpu.VMEM((seqs, e_rows, d), F32), pltpu.VMEM((1, d), F32)],
        compiler_params=_params("arbitrary"),
    )(proj.reshape(seqs, t, 4 * d), lb, gw, *sums, masks, st_all, o, do2.reshape(seqs, t, d))
    return dp.reshape(n, 4 * d), dlb, dgw


def _adam_update(w, g, m, v):
    b1c = 1.0 - ADAM_B1 ** ADAM_STEP
    b2c = 1.0 - ADAM_B2 ** ADAM_STEP
    m_new = ADAM_B1 * m + (1.0 - ADAM_B1) * g
    v_new = ADAM_B2 * v + (1.0 - ADAM_B2) * (g * g)
    delta = -ADAM_LR * ((m_new / b1c) / (jnp.sqrt(v_new / b2c) + ADAM_EPS) + ADAM_WD * w)
    return delta, m_new, v_new


def _adamw(w, g, m, v, name, tr=256):
    r, c = w.shape
    tr = _tile(r, tr)

    def body(w_ref, g_ref, m_ref, v_ref, d_ref, mo_ref, vo_ref):
        d_ref[...], mo_ref[...], vo_ref[...] = _adam_update(w_ref[...], g_ref[...], m_ref[...], v_ref[...])

    blk = pl.BlockSpec((tr, c), lambda i: (i, 0))
    return pl.pallas_call(
        body, name=name, grid=(r // tr,),
        in_specs=[blk] * 4, out_specs=[blk] * 3,
        out_shape=[jax.ShapeDtypeStruct((r, c), F32)] * 3,
        compiler_params=_params("arbitrary"),
    )(w, g, m, v)


def _adamw_slots(w, slot_bufs, m, v, name, tr=256):
    nl, r, c = w.shape
    tr = _tile(r, tr)

    def body(*refs):
        w_ref = refs[0]
        g_refs = refs[1:1 + nl]
        m_ref, v_ref, go_ref, d_ref, mo_ref, vo_ref = refs[1 + nl:]
        for k in range(nl):
            @pl.when(pl.program_id(0) == k)
            def _(k=k):
                g = g_refs[k][0].astype(F32)
                for s in range(1, N_DEV):
                    g = g + g_refs[k][s].astype(F32)
                go_ref[0] = g

        d_ref[0], mo_ref[0], vo_ref[0] = _adam_update(w_ref[0], go_ref[0], m_ref[0], v_ref[0])

    blk = pl.BlockSpec((1, tr, c), lambda l, i: (l, i, 0))
    g_specs = [pl.BlockSpec((N_DEV, tr, c), lambda l, i, k=k: (0, jnp.where(l == k, i, 0), 0)) for k in range(nl)]
    return pl.pallas_call(
        body, name=name, grid=(nl, r // tr),
        in_specs=[blk] + g_specs + [blk, blk], out_specs=[blk] * 4,
        out_shape=[jax.ShapeDtypeStruct((nl, r, c), F32)] * 4,
        compiler_params=_params("arbitrary", "arbitrary"),
    )(w, *slot_bufs, m, v)


def _mesh_pos():
    return lax.axis_index("x"), lax.axis_index("y"), lax.axis_index("c")


def _flip(pos, p):
    x, y, c = pos
    return ((1 - x) if p & 4 else x, (1 - y) if p & 2 else y, (1 - c) if p & 1 else c)


def _lin(pos):
    return 4 * pos[0] + 2 * pos[1] + pos[2]


_HBM = pl.BlockSpec(memory_space=pltpu.HBM)
_SEM = pl.BlockSpec(memory_space=pltpu.SEMAPHORE)
_DATAFLOW = pltpu.SideEffectType.DATAFLOW_SIDE_EFFECTING


class _Item:
    def __init__(self, src, land_shape, src_pick, dst_pick, peers=tuple(range(1, N_DEV))):
        self.src, self.land_shape, self.src_pick, self.dst_pick = src, land_shape, src_pick, dst_pick
        self.peers = peers


def _remote_copies(items, src, land, send_sem, recv_sem, me, arriving):
    me_i = _lin(me)
    out = []
    for it, s_ref, l_ref in zip(items, src, land):
        for p in it.peers:
            peer = _flip(me, p)
            out.append(pltpu.make_async_remote_copy(
                src_ref=it.src_pick(s_ref, _lin(peer)),
                dst_ref=it.dst_pick(l_ref, _lin(peer) if arriving else me_i),
                send_sem=send_sem, recv_sem=recv_sem, device_id=peer, device_id_type=pl.DeviceIdType.MESH))
    return out


def _own_copies(items, src, land, sem, me):
    me_i = _lin(me)
    return [pltpu.make_async_copy(it.src_pick(s_ref, me_i), it.dst_pick(l_ref, me_i), sem)
            for it, s_ref, l_ref in zip(items, src, land)]


def _exchange_start(groups, name):
    items = [it for g in groups for it in g]
    n, ng = len(items), len(groups)
    first = [sum(len(g) for g in groups[:gi]) for gi in range(ng)]

    def body(*refs):
        src, land = refs[0:n], refs[n:2 * n]
        send_sems, recv_sems = refs[2 * n:2 * n + ng], refs[2 * n + ng:2 * n + 2 * ng]
        token = refs[4 * n + 2 * ng]
        me = _mesh_pos()
        for gi, g in enumerate(groups):
            sl = slice(first[gi], first[gi] + len(g))
            for cp in _remote_copies(g, src[sl], land[sl], send_sems[gi], recv_sems[gi], me, arriving=False):
                cp.start()
            for cp in _own_copies(g, src[sl], land[sl], recv_sems[gi], me):
                cp.start()
        token[...] = jnp.zeros_like(token)

    srcs = [pltpu.with_memory_space_constraint(it.src, pltpu.HBM) for it in items]
    lands = [pltpu.with_memory_space_constraint(lax.empty(it.land_shape, it.src.dtype), pltpu.HBM) for it in items]
    res = pl.pallas_call(
        body, name=name,
        out_shape=([pltpu.SemaphoreType.DMA(())] * (2 * ng)
                   + [pltpu.HBM(it.src.shape, it.src.dtype) for it in items]
                   + [pltpu.HBM(it.land_shape, it.src.dtype) for it in items]
                   + [jax.ShapeDtypeStruct((8, 128), F32)]),
        in_specs=[_HBM] * (2 * n),
        out_specs=[_SEM] * (2 * ng) + [_HBM] * (2 * n) + [pl.BlockSpec(memory_space=pltpu.VMEM)],
        input_output_aliases={i: 2 * ng + i for i in range(2 * n)},
        compiler_params=pltpu.CompilerParams(has_side_effects=_DATAFLOW),
    )(*srcs, *lands)
    send_sems, recv_sems = res[0:ng], res[ng:2 * ng]
    src_thru, land_thru = res[2 * ng:2 * ng + n], res[2 * ng + n:2 * ng + 2 * n]
    handles = []
    for gi, g in enumerate(groups):
        sl = slice(first[gi], first[gi] + len(g))
        handles.append((g, src_thru[sl], land_thru[sl], send_sems[gi], recv_sems[gi]))
    return handles, res[-1]


def _exchange_wait(handle, after, name):
    items, src_thru, land_thru, send_sem, recv_sem = handle
    k = len(items)
    afters = list(after) if isinstance(after, (list, tuple)) else [after]

    def body(*refs):
        src, land = refs[0:k], refs[k:2 * k]
        send_ref, recv_ref = refs[2 * k], refs[2 * k + 1]
        for cp in _remote_copies(items, src, land, send_ref, recv_ref, _mesh_pos(), arriving=True):
            cp.wait_send()
            cp.wait_recv()
        for cp in _own_copies(items, src, land, recv_ref, _mesh_pos()):
            cp.wait()

    res = pl.pallas_call(
        body, name=name,
        out_shape=([pltpu.HBM(s.shape, s.dtype) for s in src_thru] + [pltpu.HBM(l.shape, l.dtype) for l in land_thru]),
        in_specs=[_HBM] * (2 * k) + [_SEM, _SEM] + [pl.BlockSpec(memory_space=pl.ANY)] * len(afters),
        out_specs=[_HBM] * (2 * k),
        input_output_aliases={i: i for i in range(2 * k)},
        compiler_params=pltpu.CompilerParams(has_side_effects=_DATAFLOW),
    )(*src_thru, *land_thru, send_sem, recv_sem, *afters)
    return res[k:2 * k]


SAME_CORE = (2, 4, 6)
SIBLING = 1


def _pass_on_start(buf, name):
    def body(buf_ref, send_sem, recv_sem, thru_ref):
        me = _mesh_pos()
        for p in SAME_CORE:
            slot = buf_ref.at[_lin(_flip(me, p))]
            pltpu.make_async_remote_copy(src_ref=slot, dst_ref=slot, send_sem=send_sem, recv_sem=recv_sem,
                                         device_id=_flip(me, SIBLING), device_id_type=pl.DeviceIdType.MESH).start()

    return pl.pallas_call(
        body, name=name,
        out_shape=[pltpu.SemaphoreType.DMA(()), pltpu.SemaphoreType.DMA(()), pltpu.HBM(buf.shape, buf.dtype)],
        in_specs=[_HBM], out_specs=[_SEM, _SEM, _HBM], input_output_aliases={0: 2},
        compiler_params=pltpu.CompilerParams(has_side_effects=_DATAFLOW),
    )(pltpu.with_memory_space_constraint(buf, pltpu.HBM))


def _pass_on_wait(handle, name):
    send_sem, recv_sem, thru = handle

    def body(buf_ref, send_ref, recv_ref, out_ref):
        me = _mesh_pos()
        sibling = _flip(me, SIBLING)
        for p in SAME_CORE:
            mine, theirs = buf_ref.at[_lin(_flip(me, p))], buf_ref.at[_lin(_flip(sibling, p))]
            cp = pltpu.make_async_remote_copy(src_ref=mine, dst_ref=theirs, send_sem=send_ref, recv_sem=recv_ref,
                                              device_id=sibling, device_id_type=pl.DeviceIdType.MESH)
            cp.wait_send()
            cp.wait_recv()

    return pl.pallas_call(
        body, name=name, out_shape=pltpu.HBM(thru.shape, thru.dtype),
        in_specs=[_HBM, _SEM, _SEM], out_specs=_HBM, input_output_aliases={0: 0},
        compiler_params=pltpu.CompilerParams(has_side_effects=_DATAFLOW),
    )(thru, send_sem, recv_sem)


def _whole(ref, i):
    return ref


def _slot(ref, i):
    return ref.at[i]


def _rows_of(r):
    return lambda ref, i: ref.at[pl.ds(pl.multiple_of(i * r, r), r), :]


def _cols_of(c):
    return lambda ref, i: ref.at[:, pl.ds(pl.multiple_of(i * c, c), c)]


def _all_reduce_small(buf, after, name):
    r, c = buf.shape

    def body(src_ref, after_ref, out_ref, all_ref, send_sems, recv_sems):
        me = _mesh_pos()
        me_i = _lin(me)
        all_ref[me_i] = src_ref[...]
        for p in range(1, N_DEV):
            peer = _flip(me, p)
            pltpu.make_async_remote_copy(
                src_ref=src_ref, dst_ref=all_ref.at[me_i], send_sem=send_sems.at[p - 1], recv_sem=recv_sems.at[p - 1],
                device_id=peer, device_id_type=pl.DeviceIdType.MESH).start()
        for p in range(1, N_DEV):
            peer = _flip(me, p)
            cp = pltpu.make_async_remote_copy(
                src_ref=src_ref, dst_ref=all_ref.at[_lin(peer)], send_sem=send_sems.at[p - 1],
                recv_sem=recv_sems.at[p - 1], device_id=peer, device_id_type=pl.DeviceIdType.MESH)
            cp.wait_recv()
            cp.wait_send()
        acc = all_ref[0]
        for s in range(1, N_DEV):
            acc = acc + all_ref[s]
        out_ref[...] = acc

    vm = pl.BlockSpec(memory_space=pltpu.VMEM)
    return pl.pallas_call(
        body, name=name, in_specs=[vm, pl.BlockSpec(memory_space=pl.ANY)], out_specs=vm,
        out_shape=jax.ShapeDtypeStruct((r, c), F32),
        scratch_shapes=[pltpu.VMEM((N_DEV, r, c), F32), pltpu.SemaphoreType.DMA((N_DEV - 1,)),
                        pltpu.SemaphoreType.DMA((N_DEV - 1,))],
        compiler_params=pltpu.CompilerParams(has_side_effects=True),
    )(buf, after)


def _unshard_cols(g):
    s, l, r, c = g.shape
    return jnp.transpose(g, (1, 2, 0, 3)).reshape(l, r, s * c)


def kernel(x, gdn_w_in, gdn_conv, gdn_a_log, gdn_dt_bias, gdn_onorm, gdn_w_out, hgrn_w_in, hgrn_lb_logits, hgrn_gnorm, hgrn_w_out, norm_mix, norm_mlp, mlp_w_up, mlp_w_down, norm_final, loss_target, m_gdn_w_in, m_gdn_conv, m_gdn_a_log, m_gdn_dt_bias, m_gdn_onorm, m_gdn_w_out, m_hgrn_w_in, m_hgrn_lb_logits, m_hgrn_gnorm, m_hgrn_w_out, m_norm_mix, m_norm_mlp, m_mlp_w_up, m_mlp_w_down, m_norm_final, v_gdn_w_in, v_gdn_conv, v_gdn_a_log, v_gdn_dt_bias, v_gdn_onorm, v_gdn_w_out, v_hgrn_w_in, v_hgrn_lb_logits, v_hgrn_gnorm, v_hgrn_w_out, v_norm_mix, v_norm_mlp, v_mlp_w_up, v_mlp_w_down, v_norm_final):
    seqs, seq_len, d = x.shape
    n = seqs * seq_len
    me_i = _lin(_mesh_pos())
    x2 = x.reshape(n, d)
    target = loss_target.reshape(n, d)
    n_gdn, n_hgrn = gdn_w_in.shape[0], hgrn_w_in.shape[0]

    r_out, r_down = gdn_w_out.shape[1], mlp_w_down.shape[1]
    c_gin, c_hin, c_up = gdn_w_in.shape[2], hgrn_w_in.shape[2], mlp_w_up.shape[2]

    def gathered(w, pick, land_shape, **kw):
        return _Item(w.astype(BF16), land_shape, _whole, pick, **kw)

    groups = [[_Item(gdn_conv, (N_DEV,) + gdn_conv.shape, _whole, _slot),
               _Item(hgrn_gnorm, (N_DEV,) + hgrn_gnorm.shape, _whole, _slot)]]
    for i in range(DEPTH):
        j = i // 2
        if i % 2 == 0:
            direct = (SIBLING,) + SAME_CORE if i == 0 else tuple(range(1, N_DEV))
            groups += [[gathered(gdn_w_in[j], _slot, (N_DEV, d, c_gin), peers=direct)],
                       [gathered(gdn_w_out[j], _rows_of(r_out), (N_DEV * r_out, d))]]
        else:
            groups += [[gathered(hgrn_w_in[j], _cols_of(c_hin), (d, N_DEV * c_hin))],
                       [gathered(hgrn_w_out[j], _rows_of(r_out), (N_DEV * r_out, d))]]
        groups += [[gathered(mlp_w_up[i], _cols_of(c_up), (d, N_DEV * c_up))],
                   [gathered(mlp_w_down[i], _rows_of(r_down), (N_DEV * r_down, d))]]
    gather_handles, token = _exchange_start(groups, "gather_start")
    lbs = _lb_fwd(hgrn_lb_logits + token[0:1, 0:1], "lb_fwd")

    def arrived(k, after, name):
        return _exchange_wait(gather_handles[k], after, "gather_wait_" + name)

    saved = []
    w_in, w_ab, w_out, w_up, w_down = ([None] * DEPTH for _ in range(5))
    h = x2
    for i in range(DEPTH):
        j = i // 2
        if i == 0:
            g_conv, g_gnorm = arrived(0, h, "small")
            conv_full = _unshard_cols(g_conv)
            gnorm_full = jnp.transpose(g_gnorm, (1, 0, 2)).reshape(n_hgrn, d)
        if i == 0:
            y = _rms_fwd(h, norm_mix[0:1] + token[0:1, 0:1], "rms_mix_0")
        (w_in[i],) = arrived(1 + 4 * i, [y, lbs, conv_full, gnorm_full] if i == 0 else y, f"in_{i}")
        if i == 0:
            w_in[i] = _pass_on_wait(_pass_on_start(w_in[i], "pass_on_start_in_0"), "pass_on_wait_in_0")
        if i % 2 == 0:
            w_gin = jnp.transpose(w_in[i], (1, 0, 2)).reshape(d, N_DEV * c_gin)
            w_in[i] = w_gin[:, :GDN_MAIN]
            w_ab[i] = jnp.pad(w_gin[:, GDN_MAIN:], ((0, 0), (0, AB_PAD - 2 * N_HEADS)))
            projm = _mm(y, w_in[i], "nn", [BF16], f"gdn_proj_{i}")
            projab = _mm(y, w_ab[i], "nn", [F32], f"gdn_proj_ab_{i}")
            o2, st_all, conv_y, dinv_all = _gdn_fwd_all(projm, projab, conv_full[j], gdn_a_log[j:j + 1],
                                                    gdn_dt_bias[j:j + 1], gdn_onorm[j:j + 1], seqs, f"gdn_fwd_{i}")
            mix = (projm, projab, conv_y, st_all, dinv_all)
        else:
            proj = _mm(y, w_in[i], "nn", [BF16], f"hgrn_proj_{i}")
            o2, o_raw, st_all = _hgrn_fwd_all(proj, lbs[i:i + 1], gnorm_full[j:j + 1], seqs, f"hgrn_fwd_{i}")
            mix = (proj, o_raw, st_all)
        (w_out[i],) = arrived(2 + 4 * i, o2, f"out_{i}")
        h1, y2 = _mm_rows(o2, w_out[i], "nn", [F32, BF16], f"mix_out_{i}", epilogue=_ep_residual_norm, extras=(h,),
                     vectors=(norm_mlp[i:i + 1],))
        (w_up[i],) = arrived(3 + 4 * i, y2, f"up_{i}")
        u, act = _mm(y2, w_up[i], "nn", [BF16, BF16], f"mlp_up_{i}",
                     epilogue=lambda acc: (acc, jnp.square(jnp.maximum(acc, 0.0))))
        (w_down[i],) = arrived(4 + 4 * i, act, f"down_{i}")
        saved.append((h, y, mix, o2, h1, y2, u, act))
        if i + 1 < DEPTH:
            h, y = _mm_rows(act, w_down[i], "nn", [F32, BF16], f"mlp_down_{i}", epilogue=_ep_residual_norm, extras=(h1,),
                       vectors=(norm_mix[i + 1:i + 2],))
        else:
            h = _mm(act, w_down[i], "nn", [F32], f"mlp_down_{i}", epilogue=lambda acc, res: (res + acc,),
                    extras=(h1,))

    dh, dh_b, d_nf, sq = _loss_head(h, norm_final.reshape(1, d), target, "loss_head")

    d_nmix, d_nmlp = [None] * DEPTH, [None] * DEPTH
    d_conv, d_alog, d_dtb, d_onorm = [None] * n_gdn, [None] * n_gdn, [None] * n_gdn, [None] * n_gdn
    d_lb = [jnp.zeros((1, d), F32)] * DEPTH
    d_gnorm = [None] * n_hgrn
    mlp_handles, mix_handles = [None] * DEPTH, [None] * DEPTH
    token = None
    for i in reversed(range(DEPTH)):
        j = i // 2
        h_in, y, mix, o2, h1, y2, u, act = saved[i]
        g_down = _mm(act, dh_b, "tn", [BF16], f"g_down_{i}", after=token)
        du = _mm(dh_b, w_down[i], "nt", [BF16], f"d_u_{i}",
                 epilogue=lambda acc, uu: (acc * (2.0 * jnp.maximum(uu.astype(F32), 0.0)),), extras=(u,))
        g_up = _mm(y2, du, "tn", [BF16], f"g_up_{i}")
        mlp_handles[i], token = _exchange_start(
            [[_Item(g_down, (N_DEV, r_down, d), _rows_of(r_down), _slot)],
             [_Item(g_up, (N_DEV, d, c_up), _cols_of(c_up), _slot)]], f"scatter_start_mlp_{i}")
        dh1, dh1_b, d_nmlp[i] = _mm_rows(du, w_up[i], "nt", [F32, BF16], f"d_y2_{i}", epilogue=_ep_norm_bwd,
                                     extras=(h1, dh), vectors=(norm_mlp[i:i + 1],), n_sums=1, after=token)
        g_out = _mm(o2, dh1_b, "tn", [BF16], f"g_out_{i}")
        do2 = _mm(dh1_b, w_out[i], "nt", [BF16], f"d_o2_{i}")
        if i % 2 == 0:
            projm, projab, conv_y, st_all, dinv_all = mix
            dpm, dpab, d_conv[j], d_alog[j], d_dtb[j], d_onorm[j] = _gdn_bwd_all(
                projm, projab, conv_y, conv_full[j], gdn_a_log[j:j + 1], gdn_dt_bias[j:j + 1], gdn_onorm[j:j + 1],
                st_all, dinv_all, do2, seqs, f"gdn_bwd_{i}")
            g_main = _mm(y, dpm, "tn", [BF16], f"g_in_{i}")
            g_ab = _mm(y, dpab, "tn", [BF16], f"g_in_ab_{i}")
            g_in = jnp.concatenate([g_main, g_ab[:, :2 * N_HEADS]], axis=1)
            g_in = jnp.transpose(g_in.reshape(d, N_DEV, c_gin), (1, 0, 2))
            in_item = _Item(g_in, (N_DEV, d, c_gin), _slot, _slot)
            dy_ab = _mm(dpab, w_ab[i], "nt", [F32], f"d_y_ab_{i}")
            dp, dy_extras = dpm, (dy_ab, h_in, dh1)
            dy_epilogue = lambda acc, e, xx, dres, w: _ep_norm_bwd(acc + e, xx, dres, w)
        else:
            proj, o_raw, st_all = mix
            dp, d_lb[i], d_gnorm[j] = _hgrn_bwd_all(proj, lbs[i:i + 1], gnorm_full[j:j + 1], st_all, o_raw, do2,
                                               seqs, f"hgrn_bwd_{i}")
            g_in = _mm(y, dp, "tn", [BF16], f"g_in_{i}")
            in_item = _Item(g_in, (N_DEV, d, c_hin), _cols_of(c_hin), _slot)
            dy_extras, dy_epilogue = (h_in, dh1), _ep_norm_bwd
        mix_handles[i], token = _exchange_start(
            [[_Item(g_out, (N_DEV, r_out, d), _rows_of(r_out), _slot)], [in_item]], f"scatter_start_mix_{i}")
        dh, dh_b, d_nmix[i] = _mm_rows(dp, w_in[i], "nt", [F32, BF16], f"d_y_{i}", epilogue=dy_epilogue, extras=dy_extras,
                                  vectors=(norm_mix[i:i + 1],), n_sums=1, after=token)
        token = None
    grad_x = dh.reshape(x.shape)

    def landed(handles, k, layers, after, name):
        return [_exchange_wait(handles[i][k], after, f"scatter_wait_{name}_{i}")[0] for i in layers]

    every, even, odd = range(DEPTH), range(0, DEPTH, 2), range(1, DEPTH, 2)
    upd = {}
    upd["mlp_w_down"] = _adamw_slots(mlp_w_down, landed(mlp_handles, 0, every, dh, "down"), m_mlp_w_down,
                                     v_mlp_w_down, "adamw_mlp_w_down")
    upd["mlp_w_up"] = _adamw_slots(mlp_w_up, landed(mlp_handles, 1, every, upd["mlp_w_down"][1], "up"), m_mlp_w_up,
                                   v_mlp_w_up, "adamw_mlp_w_up")
    upd["hgrn_w_out"] = _adamw_slots(hgrn_w_out, landed(mix_handles, 0, odd, upd["mlp_w_up"][1], "out"),
                                     m_hgrn_w_out, v_hgrn_w_out, "adamw_hgrn_w_out")
    upd["hgrn_w_in"] = _adamw_slots(hgrn_w_in, landed(mix_handles, 1, odd, upd["hgrn_w_out"][1], "in"), m_hgrn_w_in,
                                    v_hgrn_w_in, "adamw_hgrn_w_in")

    dlb_rows = jnp.concatenate(d_lb, axis=0)
    tail = jnp.concatenate(
        [jnp.concatenate(d_onorm, axis=1), jnp.concatenate(d_alog, axis=1), jnp.concatenate(d_dtb, axis=1)], axis=1)
    tail = jnp.pad(tail, ((0, 0), (0, d - tail.shape[1])))
    conv_rows = jnp.stack(d_conv).reshape(-1, d)
    packed = jnp.concatenate(
        [jnp.concatenate(d_nmix, axis=0), jnp.concatenate(d_nmlp, axis=0), d_nf, sq, dlb_rows,
         jnp.concatenate(d_gnorm, axis=0), tail, conv_rows], axis=0)
    pad_rows = (-packed.shape[0]) % 8
    packed = jnp.pad(packed, ((0, pad_rows), (0, 0)))
    tot = _all_reduce_small(packed, upd["hgrn_w_in"][1], "reduce_small")

    upd["gdn_w_out"] = _adamw_slots(gdn_w_out, landed(mix_handles, 0, even, tot, "out"),
                                    m_gdn_w_out, v_gdn_w_out, "adamw_gdn_w_out")
    upd["gdn_w_in"] = _adamw_slots(gdn_w_in, landed(mix_handles, 1, even, upd["gdn_w_out"][1], "in"), m_gdn_w_in,
                                   v_gdn_w_in, "adamw_gdn_w_in")

    def update(name, w, g, m, v):
        shape = w.shape
        c = shape[-1]
        res = _adamw(w.reshape(-1, c), g.reshape(-1, c), m.reshape(-1, c), v.reshape(-1, c), "adamw_" + name)
        return [g.reshape(shape)] + [o.reshape(shape) for o in res]

    r0 = 0
    g_nmix = tot[r0:r0 + DEPTH]; r0 += DEPTH
    g_nmlp = tot[r0:r0 + DEPTH]; r0 += DEPTH
    g_nf = tot[r0]; r0 += 1
    loss = tot[r0, 0]; r0 += 1
    g_lb = _lb_bwd(hgrn_lb_logits, tot[r0:r0 + DEPTH], "lb_bwd"); r0 += DEPTH
    g_gnorm_full = tot[r0:r0 + n_hgrn]; r0 += n_hgrn
    t_row = tot[r0]; r0 += 1
    g_conv_full = tot[r0:r0 + n_gdn * CONV_K * 3].reshape(n_gdn, CONV_K, 3 * d)
    g_onorm = t_row[0:n_gdn * HEAD_DIM].reshape(n_gdn, HEAD_DIM)
    o1 = n_gdn * HEAD_DIM
    g_alog = t_row[o1:o1 + n_gdn * N_HEADS].reshape(n_gdn, N_HEADS)
    g_dtb = t_row[o1 + n_gdn * N_HEADS:o1 + 2 * n_gdn * N_HEADS].reshape(n_gdn, N_HEADS)
    c_gn, c_cv = hgrn_gnorm.shape[1], gdn_conv.shape[2]
    g_gnorm = lax.dynamic_slice_in_dim(g_gnorm_full, me_i * c_gn, c_gn, axis=1)
    g_conv = lax.dynamic_slice_in_dim(g_conv_full, me_i * c_cv, c_cv, axis=2)

    upd["gdn_conv"] = update("gdn_conv", gdn_conv, g_conv, m_gdn_conv, v_gdn_conv)
    upd["gdn_a_log"] = update("gdn_a_log", gdn_a_log, g_alog, m_gdn_a_log, v_gdn_a_log)
    upd["gdn_dt_bias"] = update("gdn_dt_bias", gdn_dt_bias, g_dtb, m_gdn_dt_bias, v_gdn_dt_bias)
    upd["gdn_onorm"] = update("gdn_onorm", gdn_onorm, g_onorm, m_gdn_onorm, v_gdn_onorm)
    upd["hgrn_lb_logits"] = update("hgrn_lb_logits", hgrn_lb_logits, g_lb, m_hgrn_lb_logits, v_hgrn_lb_logits)
    upd["hgrn_gnorm"] = update("hgrn_gnorm", hgrn_gnorm, g_gnorm, m_hgrn_gnorm, v_hgrn_gnorm)
    upd["norm_mix"] = update("norm_mix", norm_mix, g_nmix, m_norm_mix, v_norm_mix)
    upd["norm_mlp"] = update("norm_mlp", norm_mlp, g_nmlp, m_norm_mlp, v_norm_mlp)
    upd["norm_final"] = update("norm_final", norm_final, g_nf, m_norm_final, v_norm_final)

    order = ["gdn_w_in", "gdn_conv", "gdn_a_log", "gdn_dt_bias", "gdn_onorm", "gdn_w_out", "hgrn_w_in",
             "hgrn_lb_logits", "hgrn_gnorm", "hgrn_w_out", "norm_mix", "norm_mlp", "mlp_w_up", "mlp_w_down",
             "norm_final"]
    outs = [loss, grad_x]
    for k in range(4):
        outs += [upd[name][k] for name in order]
    return tuple(outs)
```

```python
import functools

import numpy as np
import jax
import jax.numpy as jnp
from jax import lax
from jax.experimental import pallas as pl
from jax.experimental.pallas import tpu as pltpu

F32 = jnp.float32
BF16 = jnp.bfloat16

D_MODEL = 1024
N_HEADS = 8
HEAD_DIM = 128
CHUNK = 64
CONV_K = 4
HALO = 16
EPS = 1e-6
DEPTH = 4
N_DEV = 8
GDN_MAIN = 4 * D_MODEL
AB_PAD = 128
LANE_BLOCK = 256
ROW_BLOCK = 16
BLOCK_UNROLL = 4

ADAM_LR = 0.001
ADAM_B1 = 0.9
ADAM_B2 = 0.999
ADAM_EPS = 1e-08
ADAM_WD = 0.01
ADAM_STEP = 10

VMEM_LIMIT = 56 * 1024 * 1024
MM_TILE = 1024
MM_ROWS_MAX = 2048
MM_VMEM_BUDGET = 40 * 1024 * 1024
MM_ROWS_TILE = 512
_DIMS = {
    "nn": (((1,), (0,)), ((), ())),
    "nt": (((1,), (1,)), ((), ())),
    "tn": (((0,), (0,)), ((), ())),
}


def _parts(x, n):
    if n == 1 and x.dtype == BF16:
        return [x]
    out = []
    r = x.astype(F32)
    for i in range(n):
        p = r.astype(BF16)
        out.append(p)
        if i + 1 < n:
            r = r - p.astype(F32)
    return out


def _dot_raw(a, b, mode, na, nb):
    ap, bp = _parts(a, na), _parts(b, nb)
    nmax = max(na, nb)
    pairs = [(i, j) for i in range(na) for j in range(nb) if i + j < nmax]
    ka = 0 if mode == "tn" else 1
    kb = 1 if mode == "nt" else 0
    xa = ap[0] if len(pairs) == 1 else jnp.concatenate([ap[i] for i, _ in pairs], axis=ka)
    xb = bp[0] if len(pairs) == 1 else jnp.concatenate([bp[j] for _, j in pairs], axis=kb)
    return lax.dot_general(xa, xb, _DIMS[mode], preferred_element_type=F32)


@functools.partial(jax.custom_vjp, nondiff_argnums=(2, 3, 4))
def _dot(a, b, mode, na, nb):
    return _dot_raw(a, b, mode, na, nb)


def _dot_fwd(a, b, mode, na, nb):
    return _dot_raw(a, b, mode, na, nb), (a, b)


def _dot_bwd(mode, na, nb, res, ct):
    a, b = res
    if mode == "nn":
        da = _dot_raw(ct, b, "nt", 1, 1)
        db = _dot_raw(a, ct, "tn", 1, 1)
    elif mode == "nt":
        da = _dot_raw(ct, b, "nn", 1, 1)
        db = _dot_raw(ct, a, "tn", 1, 1)
    else:
        da = _dot_raw(b, ct, "nt", 1, 1)
        db = _dot_raw(a, ct, "nn", 1, 1)
    return da.astype(a.dtype), db.astype(b.dtype)


_dot.defvjp(_dot_fwd, _dot_bwd)


N_EXACT = 3


@jax.custom_vjp
def _dot01(x, m_wide, m):
    return lax.dot_general(m_wide, jnp.concatenate(_parts(x, N_EXACT), axis=0), _DIMS["nn"], preferred_element_type=F32)


def _dot01_fwd(x, m_wide, m):
    return _dot01(x, m_wide, m), (m_wide, m)


def _dot01_bwd(res, ct):
    m_wide, m = res
    dx = lax.dot_general(m, ct.astype(BF16), _DIMS["tn"], preferred_element_type=F32)
    return dx, jnp.zeros_like(m_wide), jnp.zeros_like(m)


_dot01.defvjp(_dot01_fwd, _dot01_bwd)


def _thrice(m):
    return jnp.concatenate([m] * N_EXACT, axis=1).astype(BF16), m.astype(BF16)


def _iota2(shape, dim):
    return lax.broadcasted_iota(jnp.int32, shape, dim)


def _tril_f32(n):
    return (_iota2((n, n), 0) >= _iota2((n, n), 1)).astype(F32)


def _below_block(n, b):
    ri, ci = _iota2((n, n), 0) // b, _iota2((n, n), 1) // b
    return (ri == ci + 1) & (ri % 2 == 1)


def _half_inverses(L):
    n = L.shape[0]
    eye = (_iota2((n, n), 0) == _iota2((n, n), 1)).astype(F32)
    d = eye - jnp.where(_below_block(n, 1), L, 0.0)
    b = 2
    while 2 * b < n:
        e = jnp.where(_below_block(n, b), L, 0.0)
        d = d - _dot_raw(d, _dot_raw(e, d, "nn", 2, 2), "nn", 2, 2)
        b *= 2
    return d, jnp.where(_below_block(n, b), L, 0.0)


def _solve_with(d, e, rhs):
    y = _dot_raw(d, rhs, "nn", 2, 2)
    return y - _dot_raw(d, _dot_raw(e, y, "nn", 2, 2), "nn", 2, 2)


@jax.custom_vjp
def _solve_unit_lower(L, rhs, d):
    n = L.shape[0]
    return _solve_with(d, jnp.where(_below_block(n, n // 2), L, 0.0), rhs)


def _solve_fwd(L, rhs, d):
    n = L.shape[0]
    e = jnp.where(_below_block(n, n // 2), L, 0.0)
    sol = _solve_with(d, e, rhs)
    return sol, (d, e, sol)


def _solve_bwd(res, ct):
    d, e, sol = res
    y = _dot_raw(d, ct - _dot_raw(e, _dot_raw(d, ct, "tn", 2, 2), "tn", 2, 2), "tn", 2, 2)
    return -_dot_raw(y, sol, "nt", 2, 2), y, jnp.zeros_like(d)


_solve_unit_lower.defvjp(_solve_fwd, _solve_bwd)


def _softplus(x):
    return jnp.maximum(x, 0.0) + jnp.log1p(jnp.exp(-jnp.abs(x)))


def _rms(x, w):
    return x * lax.rsqrt(jnp.mean(x * x, axis=-1, keepdims=True) + EPS) * w


HG_LEVELS = (32, 16, 8, 4, 2, 1)


def _hg_level_sums():
    i = np.arange(CHUNK)[:, None]
    m = np.arange(CHUNK)[None, :]
    to_row = [(m <= i) & (m // b == i // b) for b in HG_LEVELS]
    to_col = [(m > i) & (m // b == i // b) for b in HG_LEVELS if b > 1]
    return _thrice(jnp.asarray(np.concatenate(to_row + to_col + [m <= i]), F32))


def _hg_level_masks():
    i = np.arange(CHUNK)[:, None]
    j = np.arange(CHUNK)[None, :]
    return jnp.asarray(np.stack([(i // b == j // b + 1) & ((i // b) % 2 == 1) for b in HG_LEVELS]), F32)


def _hg_pre(qraw, f, lb, sums):
    g = jnp.log(lb + (1.0 - lb) * jax.nn.sigmoid(f))
    k = (1.0 - lb) * jax.nn.sigmoid(-f)
    q = jax.nn.silu(qraw) * (HEAD_DIM ** -0.5)
    return q, k, _dot01(g, *sums)


def _hg_head(st, q, k, v, e, masks):
    nl = len(HG_LEVELS)
    eye = (_iota2((CHUNK, CHUNK), 0) == _iota2((CHUNK, CHUNK), 1)).astype(F32)
    a = eye * jnp.sum(q * k, axis=-1, keepdims=True)
    for l, b in enumerate(HG_LEVELS):
        rows = q * jnp.exp(e[l * CHUNK:(l + 1) * CHUNK])
        cols = k * jnp.exp(e[(nl + l) * CHUNK:(nl + l + 1) * CHUNK]) if b > 1 else k
        a = a + masks[l] * _dot(rows, cols, "nt", 1, 1)
    gc = e[(2 * nl - 1) * CHUNK:2 * nl * CHUNK]
    o = _dot(a, v, "nn", 1, 1) + _dot(q * jnp.exp(gc), st, "nt", 1, 1)
    g_last = gc[CHUNK - 1:CHUNK]
    st_new = st * jnp.exp(g_last) + _dot(v, k * jnp.exp(g_last - gc), "tn", 1, 1)
    return o, st_new


_HG_HEADS = jax.vmap(_hg_head, in_axes=(0, 0, 0, 0, 0, None))


def _hg_post(o, gate, gw):
    return _rms(o, gw) * jax.nn.silu(gate)


def _gd_conv(xp, cw):
    off = HALO - (CONV_K - 1)
    y = cw[0:1] * xp[off:off + CHUNK]
    for kk in range(1, CONV_K):
        y = y + cw[kk:kk + 1] * xp[off + kk:off + kk + CHUNK]
    return y


def _gd_conv_bwd(xp, cw, y, dc):
    off = HALO - (CONV_K - 1)
    sig = jax.nn.sigmoid(y)
    dy = dc * (sig * (1.0 + y * (1.0 - sig)))
    dxp, dcw = None, []
    for kk in range(CONV_K):
        moved = jnp.pad(dy, ((off + kk, HALO - off - kk), (0, 0)))
        term = cw[kk:kk + 1] * moved
        dxp = term if dxp is None else dxp + term
        dcw.append(jnp.sum(xp * moved, axis=0, keepdims=True))
    return dxp, jnp.concatenate(dcw, axis=0)


def _gd_gates(a, b, alog, dtb):
    beta = jax.nn.sigmoid(b)
    g = -jnp.exp(alog) * _softplus(a + dtb)
    expand = (_iota2((N_HEADS, D_MODEL), 1) // HEAD_DIM == _iota2((N_HEADS, D_MODEL), 0)).astype(F32)
    g_x = _dot(g, expand, "nn", 3, 1)
    after = (_iota2((CHUNK, D_MODEL), 0) > _iota2((CHUNK, D_MODEL), 1) % HEAD_DIM).astype(F32)
    sums = _dot01(jnp.concatenate([g_x, g_x * after], axis=1), *_thrice(_tril_f32(CHUNK)))
    return _dot(beta, expand, "nn", 3, 1), sums


def _gd_head(st, q, k, v, beta, gc, diff, gate, onw, dinv=None):
    q = q * lax.rsqrt(jnp.sum(q * q, axis=-1, keepdims=True) + EPS) * (HEAD_DIM ** -0.5)
    k = k * lax.rsqrt(jnp.sum(k * k, axis=-1, keepdims=True) + EPS)
    ri = _iota2((CHUNK, CHUNK), 0)
    ci = _iota2((CHUNK, CHUNK), 1)
    decay = jnp.exp(jnp.where(ri >= ci, diff[:, 0:CHUNK], -jnp.inf))
    kb = k * beta
    egc = jnp.exp(gc)
    L = jnp.where(ri > ci, _dot(kb, k, "nt", 1, 1) * decay, 0.0)
    made = dinv is None
    if made:
        dinv = _half_inverses(L)[0]
    sol = _solve_unit_lower(L, jnp.concatenate([v * beta, kb * egc], axis=1), dinv)
    u = sol[:, 0:HEAD_DIM]
    w = sol[:, HEAD_DIM:2 * HEAD_DIM]
    a_qk = jnp.where(ri >= ci, _dot(q, k, "nt", 1, 1) * decay, 0.0)
    g_last = gc[CHUNK - 1:CHUNK]
    v_new = u - _dot(w, st, "nt", 1, 1)
    o = _dot(q * egc, st, "nt", 1, 1) + _dot(a_qk, v_new, "nn", 1, 1)
    st_new = st * jnp.exp(g_last) + _dot(v_new, k * jnp.exp(g_last - gc), "tn", 1, 1)
    out = (_rms(o, onw) * jax.nn.silu(gate), st_new)
    return out + (dinv,) if made else out


def _params(*sem):
    return pltpu.CompilerParams(dimension_semantics=sem, vmem_limit_bytes=VMEM_LIMIT)


def _tile(n, pref):
    t = min(n, pref)
    assert n % t == 0, (n, pref)
    return t


def _mm_tiles(m, n, k, a_size, b_size, tile_sizes):
    tn = _tile(n, MM_TILE)

    def need(tm, tk):
        acc = 4 * tm * tn * (2 if tk < k else 1)
        return 2 * (tm * tk * a_size + tk * tn * b_size + tm * tn * sum(tile_sizes)) + acc

    tk = k
    while True:
        tm = _tile(m, MM_ROWS_MAX)
        while tm > 256 and need(tm, tk) > MM_VMEM_BUDGET:
            tm //= 2
        if need(tm, tk) <= MM_VMEM_BUDGET or tk <= 512:
            return tm, tn, tk
        tk //= 2


def _mm(a, b, mode, out_dtypes, name, epilogue=None, extras=(), after=None):
    if mode == "nn":
        (m, k), (k2, n) = a.shape, b.shape
    elif mode == "nt":
        (m, k), (n, k2) = a.shape, b.shape
    else:
        (k, m), (k2, n) = a.shape, b.shape
    assert k == k2, (a.shape, b.shape, mode)
    tm, tn, tk = _mm_tiles(m, n, k, a.dtype.itemsize, b.dtype.itemsize,
                           [e.dtype.itemsize for e in extras] + [jnp.dtype(dt).itemsize for dt in out_dtypes])
    nk = k // tk
    ne, no, nafter = len(extras), len(out_dtypes), int(after is not None)
    if epilogue is None:
        epilogue = lambda acc: (acc,)

    def body(*refs):
        a_ref, b_ref = refs[0], refs[1]
        ex = refs[2:2 + ne]
        outs = refs[2 + ne + nafter:2 + ne + nafter + no]
        part = lax.dot_general(a_ref[...].astype(BF16), b_ref[...].astype(BF16), _DIMS[mode],
                               preferred_element_type=F32)

        def finish(acc):
            for o_ref, val in zip(outs, epilogue(acc, *[e[...] for e in ex])):
                o_ref[...] = val.astype(o_ref.dtype)

        if nk == 1:
            finish(part)
        else:
            acc_ref = refs[-1]
            kk = pl.program_id(2)

            @pl.when(kk == 0)
            def _():
                acc_ref[...] = part

            @pl.when(kk > 0)
            def _():
                acc_ref[...] += part

            @pl.when(kk == nk - 1)
            def _():
                finish(acc_ref[...])

    if mode == "tn":
        a_spec = pl.BlockSpec((tk, tm), lambda i, j, kk: (kk, i))
    else:
        a_spec = pl.BlockSpec((tm, tk), lambda i, j, kk: (i, kk))
    if mode == "nt":
        b_spec = pl.BlockSpec((tn, tk), lambda i, j, kk: (j, kk))
    else:
        b_spec = pl.BlockSpec((tk, tn), lambda i, j, kk: (kk, j))
    o_spec = pl.BlockSpec((tm, tn), lambda i, j, kk: (i, j))
    res = pl.pallas_call(
        body,
        name=name,
        grid=(m // tm, n // tn, nk),
        in_specs=[a_spec, b_spec] + [o_spec] * ne + [pl.BlockSpec(memory_space=pl.ANY)] * nafter,
        out_specs=[o_spec] * no,
        out_shape=[jax.ShapeDtypeStruct((m, n), dt) for dt in out_dtypes],
        scratch_shapes=[pltpu.VMEM((tm, tn), F32)] if nk > 1 else [],
        compiler_params=_params("parallel", "parallel", "arbitrary"),
    )(a, b, *extras, *([after] if nafter else []))
    return res[0] if no == 1 else res


def _mm_rows(a, b, mode, out_dtypes, name, epilogue, extras=(), vectors=(), n_sums=0, after=None):
    assert mode in ("nn", "nt")
    (m, k), n = a.shape, (b.shape[1] if mode == "nn" else b.shape[0])
    tm = _tile(m, MM_ROWS_TILE)
    mt = m // tm
    ne, no, nafter = len(extras) + len(vectors), len(out_dtypes), int(after is not None)

    def body(*refs):
        a_ref, b_ref = refs[0], refs[1]
        ex = refs[2:2 + ne]
        outs = refs[2 + ne + nafter:2 + ne + nafter + no]
        sums = refs[2 + ne + nafter + no:2 + ne + nafter + no + n_sums]
        acc_ref = refs[-1]
        i = pl.program_id(0)

        @pl.when(i == 0)
        def _():
            acc_ref[1] = jnp.zeros((tm, n), F32)

        vals = epilogue(acc_ref[1 - i % 2], *[e[...] for e in ex])
        acc_ref[i % 2] = lax.dot_general(a_ref[...].astype(BF16), b_ref[...].astype(BF16), _DIMS[mode],
                                         preferred_element_type=F32)
        for o_ref, val in zip(outs, vals[:no]):
            o_ref[...] = val.astype(o_ref.dtype)
        for s_ref, val in zip(sums, vals[no:]):
            @pl.when(i <= 1)
            def _(s_ref=s_ref, val=val):
                s_ref[...] = val

            @pl.when(i > 1)
            def _(s_ref=s_ref, val=val):
                s_ref[...] += val

    ahead = lambda i: (jnp.minimum(i, mt - 1), 0)
    behind = lambda i: (jnp.maximum(i - 1, 0), 0)
    fixed = lambda i: (0, 0)
    row = pl.BlockSpec((tm, n), behind)
    vec = pl.BlockSpec((1, n), fixed)
    res = pl.pallas_call(
        body, name=name, grid=(mt + 1,),
        in_specs=([pl.BlockSpec((tm, k), ahead), pl.BlockSpec(b.shape, fixed)] + [row] * len(extras)
                  + [vec] * len(vectors) + [pl.BlockSpec(memory_space=pl.ANY)] * nafter),
        out_specs=[row] * no + [vec] * n_sums,
        out_shape=[jax.ShapeDtypeStruct((m, n), dt) for dt in out_dtypes] + [jax.ShapeDtypeStruct((1, n), F32)] * n_sums,
        scratch_shapes=[pltpu.VMEM((2, tm, n), F32)],
        compiler_params=_params("arbitrary"),
    )(a, b, *extras, *vectors, *([after] if nafter else []))
    return res[0] if no + n_sums == 1 else res


def _ep_residual_norm(acc, res, w):
    h = res + acc
    return h, _rms(h, w)


def _ep_norm_bwd(acc, x, dres, w):
    r = lax.rsqrt(jnp.mean(x * x, axis=-1, keepdims=True) + EPS)
    g = acc * w
    dx = dres + (r * g - x * (r * r * r * jnp.mean(g * x, axis=-1, keepdims=True)))
    return dx, dx, jnp.sum(acc * (x * r), axis=0, keepdims=True)


def _rms_fwd(x, w, name, tm=512):
    n, d = x.shape
    tm = _tile(n, tm)

    def body(x_ref, w_ref, y_ref):
        y_ref[...] = _rms(x_ref[...], w_ref[...]).astype(y_ref.dtype)

    return pl.pallas_call(
        body, name=name, grid=(n // tm,),
        in_specs=[pl.BlockSpec((tm, d), lambda i: (i, 0)), pl.BlockSpec((1, d), lambda i: (0, 0))],
        out_specs=pl.BlockSpec((tm, d), lambda i: (i, 0)),
        out_shape=jax.ShapeDtypeStruct((n, d), BF16),
        compiler_params=_params("arbitrary"),
    )(x, w)


def _loss_head(h, w, target, name, tm=512):
    n, d = h.shape
    tm = _tile(n, tm)

    def body(h_ref, w_ref, t_ref, dh_ref, dhb_ref, dw_ref, sq_ref):
        y, vjp = jax.vjp(_rms, h_ref[...], w_ref[...])
        err = y - t_ref[...]
        dh, dw = vjp(err * (1.0 / d))
        dh_ref[...] = dh
        dhb_ref[...] = dh.astype(dhb_ref.dtype)
        sq = jnp.sum(err * err, axis=0, keepdims=True)

        @pl.when(pl.program_id(0) == 0)
        def _():
            dw_ref[...] = dw
            sq_ref[...] = sq

        @pl.when(pl.program_id(0) > 0)
        def _():
            dw_ref[...] += dw
            sq_ref[...] += sq

        @pl.when(pl.program_id(0) == n // tm - 1)
        def _():
            total = jnp.sum(sq_ref[...], axis=1, keepdims=True) * (0.5 / d)
            sq_ref[...] = jnp.broadcast_to(total, sq_ref.shape)

    row = pl.BlockSpec((tm, d), lambda i: (i, 0))
    vec = pl.BlockSpec((1, d), lambda i: (0, 0))
    return pl.pallas_call(
        body, name=name, grid=(n // tm,),
        in_specs=[row, vec, row],
        out_specs=[row, row, vec, vec],
        out_shape=[jax.ShapeDtypeStruct((n, d), F32), jax.ShapeDtypeStruct((n, d), BF16),
                   jax.ShapeDtypeStruct((1, d), F32), jax.ShapeDtypeStruct((1, d), F32)],
        compiler_params=_params("arbitrary"),
    )(h, w, target)


def _lower_bounds(logits):
    sm = jax.nn.softmax(logits, axis=0)
    rows = [sm[0:1] * 0.0]
    for r in range(1, DEPTH):
        rows.append(rows[-1] + sm[r:r + 1])
    return jnp.concatenate(rows, axis=0)


def _lb_fwd(logits, name):
    def body(l_ref, o_ref):
        o_ref[...] = _lower_bounds(l_ref[...])

    return pl.pallas_call(body, name=name, out_shape=jax.ShapeDtypeStruct(logits.shape, F32))(logits)


def _lb_bwd(logits, dlb, name):
    def body(l_ref, d_ref, o_ref):
        _, vjp = jax.vjp(_lower_bounds, l_ref[...])
        (o_ref[...],) = vjp(d_ref[...])

    return pl.pallas_call(body, name=name, out_shape=jax.ShapeDtypeStruct(logits.shape, F32))(logits, dlb)


def _head_slice(h):
    return pl.ds(h * HEAD_DIM, HEAD_DIM)


_GD_HEADS = jax.vmap(_gd_head, in_axes=(0, 0, 0, 0, 0, 0, 0, 0, None))
_GD_HEADS_AGAIN = jax.vmap(_gd_head, in_axes=(0, 0, 0, 0, 0, 0, 0, 0, None, 0))


def _lane_blocks(width, block_body):
    def trip(j, carry):
        block_body(lambda base=0: pl.ds(pl.multiple_of(j * LANE_BLOCK + base, LANE_BLOCK), LANE_BLOCK))
        return carry

    lax.fori_loop(0, width // LANE_BLOCK, trip, 0, unroll=BLOCK_UNROLL)


def _row_blocks(rows, block_body):
    def trip(j, carry):
        block_body(pl.ds(pl.multiple_of(j * ROW_BLOCK, ROW_BLOCK), ROW_BLOCK))
        return carry

    lax.fori_loop(0, rows // ROW_BLOCK, trip, 0, unroll=BLOCK_UNROLL)


def _hg_pre_block(p_ref, lb_ref, sums_refs, q_sc, k_sc, v_sc, e_sc, at):
    sl = at()
    q_sc[:, sl], k_sc[:, sl], e_sc[:, sl] = _hg_pre(
        p_ref[:, sl].astype(F32), p_ref[:, at(D_MODEL)].astype(F32), lb_ref[:, sl], [r[...] for r in sums_refs])
    v_sc[:, sl] = p_ref[:, at(2 * D_MODEL)].astype(F32)


def _gd_xp(halo_ref, p_ref, sl, first_chunk):
    halo = jnp.where(first_chunk, 0.0, halo_ref[:, sl].astype(F32))
    return jnp.concatenate([halo, p_ref[:, sl].astype(F32)], axis=0)


def _stack_all(ref, first=0):
    return jnp.stack([ref[s, :, _head_slice(h + first)] for s in range(ref.shape[0]) for h in range(N_HEADS)])


def _unstack_all(ref, val, first=0):
    for s in range(ref.shape[0]):
        for h in range(N_HEADS):
            ref[s, :, _head_slice(h + first)] = val[s * N_HEADS + h].astype(ref.dtype)


def _gdn_fwd_all(projm, projab, cw, alog, dtb, onw, seqs, name):
    n = projm.shape[0]
    t = n // seqs
    nc = t // CHUNK
    d = D_MODEL
    per_halo = CHUNK // HALO
    nh = seqs * N_HEADS

    def body(p_ref, halo_ref, ab_ref, cw_ref, alog_ref, dtb_ref, onw_ref, o2_ref, st_all_ref, y_ref, dinv_ref,
             st_sc, c_sc, beta_sc, g_sc):
        first_chunk = pl.program_id(0) == 0

        @pl.when(first_chunk)
        def _():
            st_sc[...] = jnp.zeros_like(st_sc)

        for s in range(seqs):
            def conv(at, s=s):
                sl = at()
                y = _gd_conv(_gd_xp(halo_ref.at[s], p_ref.at[s], sl, first_chunk), cw_ref[:, sl])
                y_ref[s, :, sl] = y
                c_sc[s, :, sl] = jax.nn.silu(y)

            _lane_blocks(3 * d, conv)
            beta_sc[s], g_sc[s] = _gd_gates(ab_ref[s, :, 0:N_HEADS], ab_ref[s, :, N_HEADS:2 * N_HEADS],
                                            alog_ref[...], dtb_ref[...])
        st_all_ref[0] = st_sc[...]
        o2, st_sc[...], dinv_ref[0] = _GD_HEADS(
            st_sc[...], _stack_all(c_sc), _stack_all(c_sc, N_HEADS), _stack_all(c_sc, 2 * N_HEADS), _stack_all(beta_sc),
            _stack_all(g_sc), _stack_all(g_sc, N_HEADS), _stack_all(p_ref, 3 * N_HEADS).astype(F32), onw_ref[...])
        _unstack_all(o2_ref, o2)

    rows = lambda c: (0, c, 0)
    const = lambda c: (0, 0)
    per_chunk = lambda c: (c, 0, 0, 0)
    p3 = projm.reshape(seqs, t, 4 * d)
    o2, st_all, conv_y, dinv_all = pl.pallas_call(
        body, name=name, grid=(nc,),
        in_specs=[pl.BlockSpec((seqs, CHUNK, 4 * d), rows),
                  pl.BlockSpec((seqs, HALO, 3 * d), lambda c: (0, jnp.maximum(c * per_halo - 1, 0), 0)),
                  pl.BlockSpec((seqs, CHUNK, AB_PAD), rows),
                  pl.BlockSpec((CONV_K, 3 * d), const), pl.BlockSpec((1, N_HEADS), const),
                  pl.BlockSpec((1, N_HEADS), const), pl.BlockSpec((1, HEAD_DIM), const)],
        out_specs=[pl.BlockSpec((seqs, CHUNK, d), rows), pl.BlockSpec((1, nh, HEAD_DIM, HEAD_DIM), per_chunk),
                   pl.BlockSpec((seqs, CHUNK, 3 * d), rows), pl.BlockSpec((1, nh, CHUNK, CHUNK), per_chunk)],
        out_shape=[jax.ShapeDtypeStruct((seqs, t, d), BF16), jax.ShapeDtypeStruct((nc, nh, HEAD_DIM, HEAD_DIM), F32),
                   jax.ShapeDtypeStruct((seqs, t, 3 * d), F32), jax.ShapeDtypeStruct((nc, nh, CHUNK, CHUNK), F32)],
        scratch_shapes=[pltpu.VMEM((nh, HEAD_DIM, HEAD_DIM), F32), pltpu.VMEM((seqs, CHUNK, 3 * d), F32),
                        pltpu.VMEM((seqs, CHUNK, d), F32), pltpu.VMEM((seqs, CHUNK, 2 * d), F32)],
        compiler_params=_params("arbitrary"),
    )(p3, p3, projab.reshape(seqs, t, AB_PAD), cw, alog, dtb, onw)
    return o2.reshape(n, d), st_all, conv_y, dinv_all


def _gdn_bwd_all(projm, projab, conv_y, cw, alog, dtb, onw, st_all, dinv_all, do2, seqs, name):
    n = projm.shape[0]
    t = n // seqs
    nc = t // CHUNK
    d = D_MODEL
    per_halo = CHUNK // HALO
    nh = seqs * N_HEADS

    def body(p_ref, halo_ref, ab_ref, y_ref, cw_ref, alog_ref, dtb_ref, onw_ref, st_all_ref, dinv_ref, do2_ref,
             dp_ref, dab_ref, dcw_ref, dalog_ref, ddtb_ref, donw_ref,
             dst_sc, dhalo_sc, c_sc, beta_sc, g_sc, dc_sc, dbeta_sc, dg_sc):
        first = pl.program_id(0) == 0
        first_chunk = pl.program_id(0) == nc - 1

        @pl.when(first)
        def _():
            dst_sc[...] = jnp.zeros_like(dst_sc)
            dhalo_sc[...] = jnp.zeros_like(dhalo_sc)

        gates_vjps = []
        for s in range(seqs):
            def act(at, s=s):
                c_sc[s, :, at()] = jax.nn.silu(y_ref[s, :, at()])

            _lane_blocks(3 * d, act)
            (beta_sc[s], g_sc[s]), gates_vjp = jax.vjp(
                _gd_gates, ab_ref[s, :, 0:N_HEADS], ab_ref[s, :, N_HEADS:2 * N_HEADS], alog_ref[...], dtb_ref[...])
            gates_vjps.append(gates_vjp)

        dinv = dinv_ref[0]
        _, vjp = jax.vjp(
            lambda *a: _GD_HEADS_AGAIN(*a, dinv), st_all_ref[0], _stack_all(c_sc), _stack_all(c_sc, N_HEADS),
            _stack_all(c_sc, 2 * N_HEADS), _stack_all(beta_sc), _stack_all(g_sc), _stack_all(g_sc, N_HEADS),
            _stack_all(p_ref, 3 * N_HEADS).astype(F32), onw_ref[...])
        dst_sc[...], dq, dk, dv, dbeta, dg, ddiff, dgate, donw = vjp((_stack_all(do2_ref).astype(F32), dst_sc[...]))
        _unstack_all(dc_sc, dq)
        _unstack_all(dc_sc, dk, N_HEADS)
        _unstack_all(dc_sc, dv, 2 * N_HEADS)
        _unstack_all(dbeta_sc, dbeta)
        _unstack_all(dg_sc, dg)
        _unstack_all(dg_sc, ddiff, N_HEADS)
        _unstack_all(dp_ref, dgate, 3 * N_HEADS)

        dalog, ddtb = None, None
        for s in range(seqs):
            def conv_bwd(at, s=s):
                sl = at()
                dxp, dcw = _gd_conv_bwd(_gd_xp(halo_ref.at[s], p_ref.at[s], sl, first_chunk), cw_ref[:, sl],
                                        y_ref[s, :, sl], dc_sc[s, :, sl])
                dqkv = jnp.concatenate([dxp[HALO:CHUNK], dxp[CHUNK:HALO + CHUNK] + dhalo_sc[s, :, sl]], axis=0)
                dp_ref[s, :, sl] = dqkv.astype(dp_ref.dtype)
                dhalo_sc[s, :, sl] = dxp[0:HALO]

                if s > 0:
                    dcw_ref[:, sl] += dcw
                    return

                @pl.when(first)
                def _():
                    dcw_ref[:, sl] = dcw

                @pl.when(jnp.logical_not(first))
                def _():
                    dcw_ref[:, sl] += dcw

            _lane_blocks(3 * d, conv_bwd)
            da, db, dalog_s, ddtb_s = gates_vjps[s]((dbeta_sc[s], dg_sc[s]))
            dab_ref[s] = jnp.concatenate(
                [da, db, jnp.zeros((CHUNK, AB_PAD - 2 * N_HEADS), F32)], axis=1).astype(dab_ref.dtype)
            dalog = dalog_s if dalog is None else dalog + dalog_s
            ddtb = ddtb_s if ddtb is None else ddtb + ddtb_s

        @pl.when(first)
        def _():
            dalog_ref[...] = dalog
            ddtb_ref[...] = ddtb
            donw_ref[...] = donw

        @pl.when(jnp.logical_not(first))
        def _():
            dalog_ref[...] += dalog
            ddtb_ref[...] += ddtb
            donw_ref[...] += donw

    back = lambda c: nc - 1 - c
    rows = lambda c: (0, back(c), 0)
    const = lambda c: (0, 0)
    per_chunk = lambda c: (back(c), 0, 0, 0)
    small = [pl.BlockSpec((CONV_K, 3 * d), const), pl.BlockSpec((1, N_HEADS), const),
             pl.BlockSpec((1, N_HEADS), const), pl.BlockSpec((1, HEAD_DIM), const)]
    p3 = projm.reshape(seqs, t, 4 * d)
    dp, dab, dcw, dalog, ddtb, donw = pl.pallas_call(
        body, name=name, grid=(nc,),
        in_specs=[pl.BlockSpec((seqs, CHUNK, 4 * d), rows),
                  pl.BlockSpec((seqs, HALO, 3 * d), lambda c: (0, jnp.maximum(back(c) * per_halo - 1, 0), 0)),
                  pl.BlockSpec((seqs, CHUNK, AB_PAD), rows), pl.BlockSpec((seqs, CHUNK, 3 * d), rows)] + small + [
                  pl.BlockSpec((1, nh, HEAD_DIM, HEAD_DIM), per_chunk), pl.BlockSpec((1, nh, CHUNK, CHUNK), per_chunk),
                  pl.BlockSpec((seqs, CHUNK, d), rows)],
        out_specs=[pl.BlockSpec((seqs, CHUNK, 4 * d), rows), pl.BlockSpec((seqs, CHUNK, AB_PAD), rows)] + small,
        out_shape=[jax.ShapeDtypeStruct((seqs, t, 4 * d), BF16), jax.ShapeDtypeStruct((seqs, t, AB_PAD), BF16),
                   jax.ShapeDtypeStruct((CONV_K, 3 * d), F32), jax.ShapeDtypeStruct((1, N_HEADS), F32),
                   jax.ShapeDtypeStruct((1, N_HEADS), F32), jax.ShapeDtypeStruct((1, HEAD_DIM), F32)],
        scratch_shapes=[pltpu.VMEM((nh, HEAD_DIM, HEAD_DIM), F32), pltpu.VMEM((seqs, HALO, 3 * d), F32),
                        pltpu.VMEM((seqs, CHUNK, 3 * d), F32), pltpu.VMEM((seqs, CHUNK, d), F32),
                        pltpu.VMEM((seqs, CHUNK, 2 * d), F32), pltpu.VMEM((seqs, CHUNK, 3 * d), F32),
                        pltpu.VMEM((seqs, CHUNK, d), F32), pltpu.VMEM((seqs, CHUNK, 2 * d), F32)],
        compiler_params=_params("arbitrary"),
    )(p3, p3, projab.reshape(seqs, t, AB_PAD), conv_y, cw, alog, dtb, onw, st_all, dinv_all,
      do2.reshape(seqs, t, d))
    return dp.reshape(n, 4 * d), dab.reshape(n, AB_PAD), dcw, dalog, ddtb, donw


def _hgrn_fwd_all(proj, lb, gw, seqs, name):
    n = proj.shape[0]
    t = n // seqs
    nc = t // CHUNK
    d = D_MODEL
    nh = seqs * N_HEADS
    sums, masks = _hg_level_sums(), _hg_level_masks()

    def body(p_ref, lb_ref, gw_ref, sums_wide_ref, sums_once_ref, masks_ref, o2_ref, o_ref, st_all_ref,
             st_sc, q_sc, k_sc, v_sc, e_sc):
        @pl.when(pl.program_id(0) == 0)
        def _():
            st_sc[...] = jnp.zeros_like(st_sc)

        sums_refs = (sums_wide_ref, sums_once_ref)
        for s in range(seqs):
            _lane_blocks(d, functools.partial(_hg_pre_block, p_ref.at[s], lb_ref, sums_refs, q_sc.at[s], k_sc.at[s],
                                              v_sc.at[s], e_sc.at[s]))
        st_all_ref[0] = st_sc[...]
        for s in range(seqs):
            one, mine = pl.ds(s, 1), pl.ds(s * N_HEADS, N_HEADS)
            o, st_sc[mine] = _HG_HEADS(st_sc[mine], *[_stack_all(r.at[one]) for r in (q_sc, k_sc, v_sc, e_sc)],
                                       masks_ref[...])
            _unstack_all(o_ref.at[one], o)

            def post(rows, s=s):
                gate = p_ref[s, rows, 3 * d:4 * d].astype(F32)
                o2_ref[s, rows, :] = _hg_post(o_ref[s, rows, :], gate, gw_ref[...]).astype(o2_ref.dtype)

            _row_blocks(CHUNK, post)

    rows = lambda c: (0, c, 0)
    vec = pl.BlockSpec((1, d), lambda c: (0, 0))
    act = pl.BlockSpec((seqs, CHUNK, d), rows)
    o2, o, st_all = pl.pallas_call(
        body, name=name, grid=(nc,),
        in_specs=[pl.BlockSpec((seqs, CHUNK, 4 * d), rows), vec, vec]
        + [pl.BlockSpec(m.shape, lambda c: (0, 0)) for m in sums] + [pl.BlockSpec(masks.shape, lambda c: (0, 0, 0))],
        out_specs=[act, act, pl.BlockSpec((1, nh, HEAD_DIM, HEAD_DIM), lambda c: (c, 0, 0, 0))],
        out_shape=[jax.ShapeDtypeStruct((seqs, t, d), BF16), jax.ShapeDtypeStruct((seqs, t, d), F32),
                   jax.ShapeDtypeStruct((nc, nh, HEAD_DIM, HEAD_DIM), F32)],
        scratch_shapes=[pltpu.VMEM((nh, HEAD_DIM, HEAD_DIM), F32)] + [pltpu.VMEM((seqs, CHUNK, d), F32)] * 3
        + [pltpu.VMEM((seqs, sums[0].shape[0], d), F32)],
        compiler_params=_params("arbitrary"),
    )(proj.reshape(seqs, t, 4 * d), lb, gw, *sums, masks)
    return o2.reshape(n, d), o, st_all


def _hgrn_bwd_all(proj, lb, gw, st_all, o, do2, seqs, name):
    n = proj.shape[0]
    t = n // seqs
    nc = t // CHUNK
    d = D_MODEL
    nh = seqs * N_HEADS
    sums, masks = _hg_level_sums(), _hg_level_masks()

    def body(p_ref, lb_ref, gw_ref, sums_wide_ref, sums_once_ref, masks_ref, st_all_ref, o_ref, do2_ref,
             dp_ref, dlb_ref, dgw_ref,
             dst_sc, q_sc, k_sc, v_sc, e_sc, do_sc, dq_sc, dk_sc, dv_sc, de_sc, dgw_sc):
        first = pl.program_id(0) == 0

        @pl.when(first)
        def _():
            dst_sc[...] = jnp.zeros_like(dst_sc)

        sums_refs = (sums_wide_ref, sums_once_ref)
        dgw_sc[...] = jnp.zeros_like(dgw_sc)
        for s in range(seqs):
            _lane_blocks(d, functools.partial(_hg_pre_block, p_ref.at[s], lb_ref, sums_refs, q_sc.at[s], k_sc.at[s],
                                              v_sc.at[s], e_sc.at[s]))

            def post_bwd(rows, s=s):
                _, vjp = jax.vjp(_hg_post, o_ref[s, rows, :], p_ref[s, rows, 3 * d:4 * d].astype(F32), gw_ref[...])
                do_sc[s, rows, :], dgate, dgw = vjp(do2_ref[s, rows, :].astype(F32))
                dp_ref[s, rows, 3 * d:4 * d] = dgate.astype(dp_ref.dtype)
                dgw_sc[...] += dgw

            _row_blocks(CHUNK, post_bwd)

        level_masks = masks_ref[...]
        _, vjp = jax.vjp(lambda *a: _HG_HEADS(*a, level_masks), st_all_ref[0],
                         *[_stack_all(r) for r in (q_sc, k_sc, v_sc, e_sc)])
        grads = vjp((_stack_all(do_sc), dst_sc[...]))
        dst_sc[...] = grads[0]
        for r, val in zip((dq_sc, dk_sc, dv_sc, de_sc), grads[1:]):
            _unstack_all(r, val)

        for s in range(seqs):
            def pre_bwd(at, s=s):
                sl = at()
                level_sums = (sums_wide_ref[...], sums_once_ref[...])
                _, vjp = jax.vjp(lambda qraw, f, lb: _hg_pre(qraw, f, lb, level_sums), p_ref[s, :, sl].astype(F32),
                                 p_ref[s, :, at(d)].astype(F32), lb_ref[:, sl])
                dqraw, df, dlb = vjp((dq_sc[s, :, sl], dk_sc[s, :, sl], de_sc[s, :, sl]))
                dp_ref[s, :, sl] = dqraw.astype(dp_ref.dtype)
                dp_ref[s, :, at(d)] = df.astype(dp_ref.dtype)
                dp_ref[s, :, at(2 * d)] = dv_sc[s, :, sl].astype(dp_ref.dtype)
                if s > 0:
                    dlb_ref[:, sl] += dlb
                    return

                @pl.when(first)
                def _():
                    dlb_ref[:, sl] = dlb

                @pl.when(jnp.logical_not(first))
                def _():
                    dlb_ref[:, sl] += dlb

            _lane_blocks(d, pre_bwd)

        @pl.when(first)
        def _():
            dgw_ref[...] = dgw_sc[...]

        @pl.when(jnp.logical_not(first))
        def _():
            dgw_ref[...] += dgw_sc[...]

    rows = lambda c: (0, nc - 1 - c, 0)
    vec = pl.BlockSpec((1, d), lambda c: (0, 0))
    act = pl.BlockSpec((seqs, CHUNK, d), rows)
    wide = pl.BlockSpec((seqs, CHUNK, 4 * d), rows)
    e_rows = sums[0].shape[0]
    dp, dlb, dgw = pl.pallas_call(
        body, name=name, grid=(nc,),
        in_specs=[wide, vec, vec] + [pl.BlockSpec(m.shape, lambda c: (0, 0)) for m in sums] + [
                  pl.BlockSpec(masks.shape, lambda c: (0, 0, 0)),
                  pl.BlockSpec((1, nh, HEAD_DIM, HEAD_DIM), lambda c: (nc - 1 - c, 0, 0, 0)), act, act],
        out_specs=[wide, vec, vec],
        out_shape=[jax.ShapeDtypeStruct((seqs, t, 4 * d), BF16), jax.ShapeDtypeStruct((1, d), F32),
                   jax.ShapeDtypeStruct((1, d), F32)],
        scratch_shapes=[pltpu.VMEM((nh, HEAD_DIM, HEAD_DIM), F32)]
        + [pltpu.VMEM((seqs, CHUNK, d), F32)] * 3 + [pltpu.VMEM((seqs, e_rows, d), F32)]
        + [pltpu.VMEM((seqs, CHUNK, d), F32)] * 4 + [pltpu.VMEM((seqs, e_rows, d), F32), pltpu.VMEM((1, d), F32)],
        compiler_params=_params("arbitrary"),
    )(proj.reshape(seqs, t, 4 * d), lb, gw, *sums, masks, st_all, o, do2.reshape(seqs, t, d))
    return dp.reshape(n, 4 * d), dlb, dgw


def _adam_update(w, g, m, v):
    b1c = 1.0 - ADAM_B1 ** ADAM_STEP
    b2c = 1.0 - ADAM_B2 ** ADAM_STEP
    m_new = ADAM_B1 * m + (1.0 - ADAM_B1) * g
    v_new = ADAM_B2 * v + (1.0 - ADAM_B2) * (g * g)
    delta = -ADAM_LR * ((m_new / b1c) / (jnp.sqrt(v_new / b2c) + ADAM_EPS) + ADAM_WD * w)
    return delta, m_new, v_new


def _adamw(w, g, m, v, name, tr=256):
    r, c = w.shape
    tr = _tile(r, tr)

    def body(w_ref, g_ref, m_ref, v_ref, d_ref, mo_ref, vo_ref):
        d_ref[...], mo_ref[...], vo_ref[...] = _adam_update(w_ref[...], g_ref[...], m_ref[...], v_ref[...])

    blk = pl.BlockSpec((tr, c), lambda i: (i, 0))
    return pl.pallas_call(
        body, name=name, grid=(r // tr,),
        in_specs=[blk] * 4, out_specs=[blk] * 3,
        out_shape=[jax.ShapeDtypeStruct((r, c), F32)] * 3,
        compiler_params=_params("arbitrary"),
    )(w, g, m, v)


def _adamw_slots(w, slot_bufs, m, v, name, tr=256):
    nl, r, c = w.shape
    tr = _tile(r, tr)

    def body(*refs):
        w_ref = refs[0]
        g_refs = refs[1:1 + nl]
        m_ref, v_ref, go_ref, d_ref, mo_ref, vo_ref = refs[1 + nl:]
        for k in range(nl):
            @pl.when(pl.program_id(0) == k)
            def _(k=k):
                g = g_refs[k][0].astype(F32)
                for s in range(1, N_DEV):
                    g = g + g_refs[k][s].astype(F32)
                go_ref[0] = g

        d_ref[0], mo_ref[0], vo_ref[0] = _adam_update(w_ref[0], go_ref[0], m_ref[0], v_ref[0])

    blk = pl.BlockSpec((1, tr, c), lambda l, i: (l, i, 0))
    g_specs = [pl.BlockSpec((N_DEV, tr, c), lambda l, i, k=k: (0, jnp.where(l == k, i, 0), 0)) for k in range(nl)]
    return pl.pallas_call(
        body, name=name, grid=(nl, r // tr),
        in_specs=[blk] + g_specs + [blk, blk], out_specs=[blk] * 4,
        out_shape=[jax.ShapeDtypeStruct((nl, r, c), F32)] * 4,
        compiler_params=_params("arbitrary", "arbitrary"),
    )(w, *slot_bufs, m, v)


def _adamw_windows(w, lo_bufs, hi_bufs, me, m, v, name, tr=256):
    nl, r, c = w.shape
    wl, wh = lo_bufs[0].shape[2], hi_bufs[0].shape[2]
    width, step = wl + wh, c - wl
    assert step >= 0 and (N_DEV - 1) * step + c <= width and N_DEV == 8
    tr = _tile(r, tr)

    def body(*refs):
        me_ref, w_ref = refs[0], refs[1]
        lo_refs, hi_refs = refs[2:2 + nl], refs[2 + nl:2 + 2 * nl]
        m_ref, v_ref, go_ref, d_ref, mo_ref, vo_ref = refs[2 + 2 * nl:]
        for k in range(nl):
            @pl.when(pl.program_id(0) == k)
            def _(k=k):
                lo, hi = lo_refs[k][0].astype(F32), hi_refs[k][0].astype(F32)
                for s in range(1, N_DEV):
                    lo, hi = lo + lo_refs[k][s].astype(F32), hi + hi_refs[k][s].astype(F32)
                g = jnp.concatenate([lo, hi], axis=1)
                for bit in range(3):
                    moved = pltpu.roll(g, width - (step << bit), axis=1)
                    g = jnp.where((me_ref[0] >> bit) & 1 == 1, moved, g)
                go_ref[0] = g[:, :c]

        d_ref[0], mo_ref[0], vo_ref[0] = _adam_update(w_ref[0], go_ref[0], m_ref[0], v_ref[0])

    blk = pl.BlockSpec((1, tr, c), lambda l, i: (l, i, 0))
    g_specs = [pl.BlockSpec((N_DEV, tr, cols), lambda l, i, k=k: (0, jnp.where(l == k, i, 0), 0))
               for cols in (wl, wh) for k in range(nl)]
    return pl.pallas_call(
        body, name=name, grid=(nl, r // tr),
        in_specs=[pl.BlockSpec(memory_space=pltpu.SMEM), blk] + g_specs + [blk, blk], out_specs=[blk] * 4,
        out_shape=[jax.ShapeDtypeStruct((nl, r, c), F32)] * 4,
        compiler_params=_params("arbitrary", "arbitrary"),
    )(me, w, *lo_bufs, *hi_bufs, m, v)


def _mesh_pos():
    return lax.axis_index("x"), lax.axis_index("y"), lax.axis_index("c")


def _flip(pos, p):
    x, y, c = pos
    return ((1 - x) if p & 4 else x, (1 - y) if p & 2 else y, (1 - c) if p & 1 else c)


def _lin(pos):
    return 4 * pos[0] + 2 * pos[1] + pos[2]


_HBM = pl.BlockSpec(memory_space=pltpu.HBM)
_SEM = pl.BlockSpec(memory_space=pltpu.SEMAPHORE)
_DATAFLOW = pltpu.SideEffectType.DATAFLOW_SIDE_EFFECTING


class _Item:
    def __init__(self, src, land_shape, src_pick, dst_pick, peers=tuple(range(1, N_DEV))):
        self.src, self.land_shape, self.src_pick, self.dst_pick = src, land_shape, src_pick, dst_pick
        self.peers = peers


def _remote_copies(items, src, land, send_sem, recv_sem, me, arriving):
    me_i = _lin(me)
    out = []
    for it, s_ref, l_ref in zip(items, src, land):
        for p in it.peers:
            peer = _flip(me, p)
            out.append(pltpu.make_async_remote_copy(
                src_ref=it.src_pick(s_ref, _lin(peer)),
                dst_ref=it.dst_pick(l_ref, _lin(peer) if arriving else me_i),
                send_sem=send_sem, recv_sem=recv_sem, device_id=peer, device_id_type=pl.DeviceIdType.MESH))
    return out


def _own_copies(items, src, land, sem, me):
    me_i = _lin(me)
    return [pltpu.make_async_copy(it.src_pick(s_ref, me_i), it.dst_pick(l_ref, me_i), sem)
            for it, s_ref, l_ref in zip(items, src, land)]


def _exchange_start(groups, name):
    items = [it for g in groups for it in g]
    n, ng = len(items), len(groups)
    first = [sum(len(g) for g in groups[:gi]) for gi in range(ng)]

    def body(*refs):
        src, land = refs[0:n], refs[n:2 * n]
        send_sems, recv_sems = refs[2 * n:2 * n + ng], refs[2 * n + ng:2 * n + 2 * ng]
        token = refs[4 * n + 2 * ng]
        me = _mesh_pos()
        for gi, g in enumerate(groups):
            sl = slice(first[gi], first[gi] + len(g))
            for cp in _remote_copies(g, src[sl], land[sl], send_sems[gi], recv_sems[gi], me, arriving=False):
                cp.start()
            for cp in _own_copies(g, src[sl], land[sl], recv_sems[gi], me):
                cp.start()
        token[...] = jnp.zeros_like(token)

    srcs = [pltpu.with_memory_space_constraint(it.src, pltpu.HBM) for it in items]
    lands = [pltpu.with_memory_space_constraint(lax.empty(it.land_shape, it.src.dtype), pltpu.HBM) for it in items]
    res = pl.pallas_call(
        body, name=name,
        out_shape=([pltpu.SemaphoreType.DMA(())] * (2 * ng)
                   + [pltpu.HBM(it.src.shape, it.src.dtype) for it in items]
                   + [pltpu.HBM(it.land_shape, it.src.dtype) for it in items]
                   + [jax.ShapeDtypeStruct((8, 128), F32)]),
        in_specs=[_HBM] * (2 * n),
        out_specs=[_SEM] * (2 * ng) + [_HBM] * (2 * n) + [pl.BlockSpec(memory_space=pltpu.VMEM)],
        input_output_aliases={i: 2 * ng + i for i in range(2 * n)},
        compiler_params=pltpu.CompilerParams(has_side_effects=_DATAFLOW),
    )(*srcs, *lands)
    send_sems, recv_sems = res[0:ng], res[ng:2 * ng]
    src_thru, land_thru = res[2 * ng:2 * ng + n], res[2 * ng + n:2 * ng + 2 * n]
    handles = []
    for gi, g in enumerate(groups):
        sl = slice(first[gi], first[gi] + len(g))
        handles.append((g, src_thru[sl], land_thru[sl], send_sems[gi], recv_sems[gi]))
    return handles, res[-1]


def _exchange_wait(handle, after, name):
    items, src_thru, land_thru, send_sem, recv_sem = handle
    k = len(items)
    afters = list(after) if isinstance(after, (list, tuple)) else [after]

    def body(*refs):
        src, land = refs[0:k], refs[k:2 * k]
        send_ref, recv_ref = refs[2 * k], refs[2 * k + 1]
        for cp in _remote_copies(items, src, land, send_ref, recv_ref, _mesh_pos(), arriving=True):
            cp.wait_send()
            cp.wait_recv()
        for cp in _own_copies(items, src, land, recv_ref, _mesh_pos()):
            cp.wait()

    res = pl.pallas_call(
        body, name=name,
        out_shape=([pltpu.HBM(s.shape, s.dtype) for s in src_thru] + [pltpu.HBM(l.shape, l.dtype) for l in land_thru]),
        in_specs=[_HBM] * (2 * k) + [_SEM, _SEM] + [pl.BlockSpec(memory_space=pl.ANY)] * len(afters),
        out_specs=[_HBM] * (2 * k),
        input_output_aliases={i: i for i in range(2 * k)},
        compiler_params=pltpu.CompilerParams(has_side_effects=_DATAFLOW),
    )(*src_thru, *land_thru, send_sem, recv_sem, *afters)
    return res[k:2 * k]


SAME_CORE = (2, 4, 6)
SIBLING = 1


def _pass_on_start(buf, name):
    def body(buf_ref, send_sem, recv_sem, thru_ref):
        me = _mesh_pos()
        for p in SAME_CORE:
            slot = buf_ref.at[_lin(_flip(me, p))]
            pltpu.make_async_remote_copy(src_ref=slot, dst_ref=slot, send_sem=send_sem, recv_sem=recv_sem,
                                         device_id=_flip(me, SIBLING), device_id_type=pl.DeviceIdType.MESH).start()

    return pl.pallas_call(
        body, name=name,
        out_shape=[pltpu.SemaphoreType.DMA(()), pltpu.SemaphoreType.DMA(()), pltpu.HBM(buf.shape, buf.dtype)],
        in_specs=[_HBM], out_specs=[_SEM, _SEM, _HBM], input_output_aliases={0: 2},
        compiler_params=pltpu.CompilerParams(has_side_effects=_DATAFLOW),
    )(pltpu.with_memory_space_constraint(buf, pltpu.HBM))


def _pass_on_wait(handle, name):
    send_sem, recv_sem, thru = handle

    def body(buf_ref, send_ref, recv_ref, out_ref):
        me = _mesh_pos()
        sibling = _flip(me, SIBLING)
        for p in SAME_CORE:
            mine, theirs = buf_ref.at[_lin(_flip(me, p))], buf_ref.at[_lin(_flip(sibling, p))]
            cp = pltpu.make_async_remote_copy(src_ref=mine, dst_ref=theirs, send_sem=send_ref, recv_sem=recv_ref,
                                              device_id=sibling, device_id_type=pl.DeviceIdType.MESH)
            cp.wait_send()
            cp.wait_recv()

    return pl.pallas_call(
        body, name=name, out_shape=pltpu.HBM(thru.shape, thru.dtype),
        in_specs=[_HBM, _SEM, _SEM], out_specs=_HBM, input_output_aliases={0: 0},
        compiler_params=pltpu.CompilerParams(has_side_effects=_DATAFLOW),
    )(thru, send_sem, recv_sem)


def _whole(ref, i):
    return ref


def _slot(ref, i):
    return ref.at[i]


def _rows_of(r):
    return lambda ref, i: ref.at[pl.ds(pl.multiple_of(i * r, r), r), :]


def _cols_of(c):
    return lambda ref, i: ref.at[:, pl.ds(pl.multiple_of(i * c, c), c)]


def _all_reduce_small(buf, after, name):
    r, c = buf.shape

    def body(src_ref, after_ref, out_ref, all_ref, send_sems, recv_sems):
        me = _mesh_pos()
        me_i = _lin(me)
        all_ref[me_i] = src_ref[...]
        for p in range(1, N_DEV):
            peer = _flip(me, p)
            pltpu.make_async_remote_copy(
                src_ref=src_ref, dst_ref=all_ref.at[me_i], send_sem=send_sems.at[p - 1], recv_sem=recv_sems.at[p - 1],
                device_id=peer, device_id_type=pl.DeviceIdType.MESH).start()
        for p in range(1, N_DEV):
            peer = _flip(me, p)
            cp = pltpu.make_async_remote_copy(
                src_ref=src_ref, dst_ref=all_ref.at[_lin(peer)], send_sem=send_sems.at[p - 1],
                recv_sem=recv_sems.at[p - 1], device_id=peer, device_id_type=pl.DeviceIdType.MESH)
            cp.wait_recv()
            cp.wait_send()
        acc = all_ref[0]
        for s in range(1, N_DEV):
            acc = acc + all_ref[s]
        out_ref[...] = acc

    vm = pl.BlockSpec(memory_space=pltpu.VMEM)
    return pl.pallas_call(
        body, name=name, in_specs=[vm, pl.BlockSpec(memory_space=pl.ANY)], out_specs=vm,
        out_shape=jax.ShapeDtypeStruct((r, c), F32),
        scratch_shapes=[pltpu.VMEM((N_DEV, r, c), F32), pltpu.SemaphoreType.DMA((N_DEV - 1,)),
                        pltpu.SemaphoreType.DMA((N_DEV - 1,))],
        compiler_params=pltpu.CompilerParams(has_side_effects=True),
    )(buf, after)


def _unshard_cols(g):
    s, l, r, c = g.shape
    return jnp.transpose(g, (1, 2, 0, 3)).reshape(l, r, s * c)


def kernel(x, gdn_w_in, gdn_conv, gdn_a_log, gdn_dt_bias, gdn_onorm, gdn_w_out, hgrn_w_in, hgrn_lb_logits, hgrn_gnorm, hgrn_w_out, norm_mix, norm_mlp, mlp_w_up, mlp_w_down, norm_final, loss_target, m_gdn_w_in, m_gdn_conv, m_gdn_a_log, m_gdn_dt_bias, m_gdn_onorm, m_gdn_w_out, m_hgrn_w_in, m_hgrn_lb_logits, m_hgrn_gnorm, m_hgrn_w_out, m_norm_mix, m_norm_mlp, m_mlp_w_up, m_mlp_w_down, m_norm_final, v_gdn_w_in, v_gdn_conv, v_gdn_a_log, v_gdn_dt_bias, v_gdn_onorm, v_gdn_w_out, v_hgrn_w_in, v_hgrn_lb_logits, v_hgrn_gnorm, v_hgrn_w_out, v_norm_mix, v_norm_mlp, v_mlp_w_up, v_mlp_w_down, v_norm_final):
    seqs, seq_len, d = x.shape
    n = seqs * seq_len
    me_i = _lin(_mesh_pos())
    x2 = x.reshape(n, d)
    target = loss_target.reshape(n, d)
    n_gdn, n_hgrn = gdn_w_in.shape[0], hgrn_w_in.shape[0]

    r_out, r_down = gdn_w_out.shape[1], mlp_w_down.shape[1]
    c_gin, c_hin, c_up = gdn_w_in.shape[2], hgrn_w_in.shape[2], mlp_w_up.shape[2]

    def gathered(w, pick, land_shape, **kw):
        return _Item(w.astype(BF16), land_shape, _whole, pick, **kw)

    groups = [[_Item(gdn_conv, (N_DEV,) + gdn_conv.shape, _whole, _slot),
               _Item(hgrn_gnorm, (N_DEV,) + hgrn_gnorm.shape, _whole, _slot)]]
    for i in range(DEPTH):
        j = i // 2
        if i % 2 == 0:
            direct = (SIBLING,) + SAME_CORE if i == 0 else tuple(range(1, N_DEV))
            groups += [[gathered(gdn_w_in[j], _slot, (N_DEV, d, c_gin), peers=direct)],
                       [gathered(gdn_w_out[j], _rows_of(r_out), (N_DEV * r_out, d))]]
        else:
            groups += [[gathered(hgrn_w_in[j], _cols_of(c_hin), (d, N_DEV * c_hin))],
                       [gathered(hgrn_w_out[j], _rows_of(r_out), (N_DEV * r_out, d))]]
        groups += [[gathered(mlp_w_up[i], _cols_of(c_up), (d, N_DEV * c_up))],
                   [gathered(mlp_w_down[i], _rows_of(r_down), (N_DEV * r_down, d))]]
    gather_handles, token = _exchange_start(groups, "gather_start")
    lbs = _lb_fwd(hgrn_lb_logits + token[0:1, 0:1], "lb_fwd")

    def arrived(k, after, name):
        return _exchange_wait(gather_handles[k], after, "gather_wait_" + name)

    saved = []
    w_in, w_ab, w_out, w_up, w_down = ([None] * DEPTH for _ in range(5))
    h = x2
    for i in range(DEPTH):
        j = i // 2
        if i == 0:
            g_conv, g_gnorm = arrived(0, h, "small")
            conv_full = _unshard_cols(g_conv)
            gnorm_full = jnp.transpose(g_gnorm, (1, 0, 2)).reshape(n_hgrn, d)
        if i == 0:
            y = _rms_fwd(h, norm_mix[0:1] + token[0:1, 0:1], "rms_mix_0")
        (w_in[i],) = arrived(1 + 4 * i, [y, lbs, conv_full, gnorm_full] if i == 0 else y, f"in_{i}")
        if i == 0:
            w_in[i] = _pass_on_wait(_pass_on_start(w_in[i], "pass_on_start_in_0"), "pass_on_wait_in_0")
        if i % 2 == 0:
            w_gin = jnp.transpose(w_in[i], (1, 0, 2)).reshape(d, N_DEV * c_gin)
            w_in[i] = w_gin[:, :GDN_MAIN]
            w_ab[i] = jnp.pad(w_gin[:, GDN_MAIN:], ((0, 0), (0, AB_PAD - 2 * N_HEADS)))
            projm = _mm(y, w_in[i], "nn", [BF16], f"gdn_proj_{i}")
            projab = _mm(y, w_ab[i], "nn", [F32], f"gdn_proj_ab_{i}")
            o2, st_all, conv_y, dinv_all = _gdn_fwd_all(projm, projab, conv_full[j], gdn_a_log[j:j + 1],
                                                    gdn_dt_bias[j:j + 1], gdn_onorm[j:j + 1], seqs, f"gdn_fwd_{i}")
            mix = (projm, projab, conv_y, st_all, dinv_all)
        else:
            proj = _mm(y, w_in[i], "nn", [BF16], f"hgrn_proj_{i}")
            o2, o_raw, st_all = _hgrn_fwd_all(proj, lbs[i:i + 1], gnorm_full[j:j + 1], seqs, f"hgrn_fwd_{i}")
            mix = (proj, o_raw, st_all)
        (w_out[i],) = arrived(2 + 4 * i, o2, f"out_{i}")
        h1, y2 = _mm_rows(o2, w_out[i], "nn", [F32, BF16], f"mix_out_{i}", epilogue=_ep_residual_norm, extras=(h,),
                     vectors=(norm_mlp[i:i + 1],))
        (w_up[i],) = arrived(3 + 4 * i, y2, f"up_{i}")
        u, act = _mm(y2, w_up[i], "nn", [BF16, BF16], f"mlp_up_{i}",
                     epilogue=lambda acc: (acc, jnp.square(jnp.maximum(acc, 0.0))))
        (w_down[i],) = arrived(4 + 4 * i, act, f"down_{i}")
        saved.append((h, y, mix, o2, h1, y2, u, act))
        if i + 1 < DEPTH:
            h, y = _mm_rows(act, w_down[i], "nn", [F32, BF16], f"mlp_down_{i}", epilogue=_ep_residual_norm, extras=(h1,),
                       vectors=(norm_mix[i + 1:i + 2],))
        else:
            h = _mm(act, w_down[i], "nn", [F32], f"mlp_down_{i}", epilogue=lambda acc, res: (res + acc,),
                    extras=(h1,))

    dh, dh_b, d_nf, sq = _loss_head(h, norm_final.reshape(1, d), target, "loss_head")

    d_nmix, d_nmlp = [None] * DEPTH, [None] * DEPTH
    d_conv, d_alog, d_dtb, d_onorm = [None] * n_gdn, [None] * n_gdn, [None] * n_gdn, [None] * n_gdn
    d_lb = [jnp.zeros((1, d), F32)] * DEPTH
    d_gnorm = [None] * n_hgrn
    mlp_handles, mix_handles = [None] * DEPTH, [None] * DEPTH
    token = None
    for i in reversed(range(DEPTH)):
        j = i // 2
        h_in, y, mix, o2, h1, y2, u, act = saved[i]
        g_down = _mm(act, dh_b, "tn", [BF16], f"g_down_{i}", after=token)
        du = _mm(dh_b, w_down[i], "nt", [BF16], f"d_u_{i}",
                 epilogue=lambda acc, uu: (acc * (2.0 * jnp.maximum(uu.astype(F32), 0.0)),), extras=(u,))
        g_up = _mm(y2, du, "tn", [BF16], f"g_up_{i}")
        mlp_handles[i], token = _exchange_start(
            [[_Item(g_down, (N_DEV, r_down, d), _rows_of(r_down), _slot)],
             [_Item(g_up, (N_DEV, d, c_up), _cols_of(c_up), _slot)]], f"scatter_start_mlp_{i}")
        dh1, dh1_b, d_nmlp[i] = _mm_rows(du, w_up[i], "nt", [F32, BF16], f"d_y2_{i}", epilogue=_ep_norm_bwd,
                                     extras=(h1, dh), vectors=(norm_mlp[i:i + 1],), n_sums=1, after=token)
        g_out = _mm(o2, dh1_b, "tn", [BF16], f"g_out_{i}")
        do2 = _mm(dh1_b, w_out[i], "nt", [BF16], f"d_o2_{i}")
        if i % 2 == 0:
            projm, projab, conv_y, st_all, dinv_all = mix
            dpm, dpab, d_conv[j], d_alog[j], d_dtb[j], d_onorm[j] = _gdn_bwd_all(
                projm, projab, conv_y, conv_full[j], gdn_a_log[j:j + 1], gdn_dt_bias[j:j + 1], gdn_onorm[j:j + 1],
                st_all, dinv_all, do2, seqs, f"gdn_bwd_{i}")
            g_main = _mm(y, dpm, "tn", [BF16], f"g_in_{i}")
            g_ab = _mm(y, dpab, "tn", [BF16], f"g_in_ab_{i}")
            wl = GDN_MAIN // N_DEV
            g_next = jnp.concatenate([g_main.reshape(d, N_DEV, wl)[:, 1:, :AB_PAD].reshape(d, -1), g_ab], axis=1)
            in_items = [_Item(g_main, (N_DEV, d, wl), _cols_of(wl), _slot),
                        _Item(g_next, (N_DEV, d, AB_PAD), _cols_of(AB_PAD), _slot)]
            dy_ab = _mm(dpab, w_ab[i], "nt", [F32], f"d_y_ab_{i}")
            dp, dy_extras = dpm, (dy_ab, h_in, dh1)
            dy_epilogue = lambda acc, e, xx, dres, w: _ep_norm_bwd(acc + e, xx, dres, w)
        else:
            proj, o_raw, st_all = mix
            dp, d_lb[i], d_gnorm[j] = _hgrn_bwd_all(proj, lbs[i:i + 1], gnorm_full[j:j + 1], st_all, o_raw, do2,
                                               seqs, f"hgrn_bwd_{i}")
            g_in = _mm(y, dp, "tn", [BF16], f"g_in_{i}")
            in_items = [_Item(g_in, (N_DEV, d, c_hin), _cols_of(c_hin), _slot)]
            dy_extras, dy_epilogue = (h_in, dh1), _ep_norm_bwd
        mix_handles[i], token = _exchange_start(
            [[_Item(g_out, (N_DEV, r_out, d), _rows_of(r_out), _slot)], in_items], f"scatter_start_mix_{i}")
        dh, dh_b, d_nmix[i] = _mm_rows(dp, w_in[i], "nt", [F32, BF16], f"d_y_{i}", epilogue=dy_epilogue, extras=dy_extras,
                                  vectors=(norm_mix[i:i + 1],), n_sums=1, after=token)
        token = None
    grad_x = dh.reshape(x.shape)

    def landed(handles, k, layers, after, name):
        return [_exchange_wait(handles[i][k], after, f"scatter_wait_{name}_{i}")[0] for i in layers]

    every, even, odd = range(DEPTH), range(0, DEPTH, 2), range(1, DEPTH, 2)
    upd = {}
    upd["mlp_w_down"] = _adamw_slots(mlp_w_down, landed(mlp_handles, 0, every, dh, "down"), m_mlp_w_down,
                                     v_mlp_w_down, "adamw_mlp_w_down")
    upd["mlp_w_up"] = _adamw_slots(mlp_w_up, landed(mlp_handles, 1, every, upd["mlp_w_down"][1], "up"), m_mlp_w_up,
                                   v_mlp_w_up, "adamw_mlp_w_up")
    upd["hgrn_w_out"] = _adamw_slots(hgrn_w_out, landed(mix_handles, 0, odd, upd["mlp_w_up"][1], "out"),
                                     m_hgrn_w_out, v_hgrn_w_out, "adamw_hgrn_w_out")
    upd["hgrn_w_in"] = _adamw_slots(hgrn_w_in, landed(mix_handles, 1, odd, upd["hgrn_w_out"][1], "in"), m_hgrn_w_in,
                                    v_hgrn_w_in, "adamw_hgrn_w_in")

    dlb_rows = jnp.concatenate(d_lb, axis=0)
    tail = jnp.concatenate(
        [jnp.concatenate(d_onorm, axis=1), jnp.concatenate(d_alog, axis=1), jnp.concatenate(d_dtb, axis=1)], axis=1)
    tail = jnp.pad(tail, ((0, 0), (0, d - tail.shape[1])))
    conv_rows = jnp.stack(d_conv).reshape(-1, d)
    packed = jnp.concatenate(
        [jnp.concatenate(d_nmix, axis=0), jnp.concatenate(d_nmlp, axis=0), d_nf, sq, dlb_rows,
         jnp.concatenate(d_gnorm, axis=0), tail, conv_rows], axis=0)
    pad_rows = (-packed.shape[0]) % 8
    packed = jnp.pad(packed, ((0, pad_rows), (0, 0)))
    tot = _all_reduce_small(packed, upd["hgrn_w_in"][1], "reduce_small")

    upd["gdn_w_out"] = _adamw_slots(gdn_w_out, landed(mix_handles, 0, even, tot, "out"),
                                    m_gdn_w_out, v_gdn_w_out, "adamw_gdn_w_out")
    windows = [_exchange_wait(mix_handles[i][1], upd["gdn_w_out"][1], f"scatter_wait_in_{i}") for i in even]
    upd["gdn_w_in"] = _adamw_windows(gdn_w_in, [lo for lo, hi in windows], [hi for lo, hi in windows],
                                     me_i.astype(jnp.int32).reshape(1), m_gdn_w_in, v_gdn_w_in, "adamw_gdn_w_in")

    def update(name, w, g, m, v):
        shape = w.shape
        c = shape[-1]
        res = _adamw(w.reshape(-1, c), g.reshape(-1, c), m.reshape(-1, c), v.reshape(-1, c), "adamw_" + name)
        return [g.reshape(shape)] + [o.reshape(shape) for o in res]

    r0 = 0
    g_nmix = tot[r0:r0 + DEPTH]; r0 += DEPTH
    g_nmlp = tot[r0:r0 + DEPTH]; r0 += DEPTH
    g_nf = tot[r0]; r0 += 1
    loss = tot[r0, 0]; r0 += 1
    g_lb = _lb_bwd(hgrn_lb_logits, tot[r0:r0 + DEPTH], "lb_bwd"); r0 += DEPTH
    g_gnorm_full = tot[r0:r0 + n_hgrn]; r0 += n_hgrn
    t_row = tot[r0]; r0 += 1
    g_conv_full = tot[r0:r0 + n_gdn * CONV_K * 3].reshape(n_gdn, CONV_K, 3 * d)
    g_onorm = t_row[0:n_gdn * HEAD_DIM].reshape(n_gdn, HEAD_DIM)
    o1 = n_gdn * HEAD_DIM
    g_alog = t_row[o1:o1 + n_gdn * N_HEADS].reshape(n_gdn, N_HEADS)
    g_dtb = t_row[o1 + n_gdn * N_HEADS:o1 + 2 * n_gdn * N_HEADS].reshape(n_gdn, N_HEADS)
    c_gn, c_cv = hgrn_gnorm.shape[1], gdn_conv.shape[2]
    g_gnorm = lax.dynamic_slice_in_dim(g_gnorm_full, me_i * c_gn, c_gn, axis=1)
    g_conv = lax.dynamic_slice_in_dim(g_conv_full, me_i * c_cv, c_cv, axis=2)

    upd["gdn_conv"] = update("gdn_conv", gdn_conv, g_conv, m_gdn_conv, v_gdn_conv)
    upd["gdn_a_log"] = update("gdn_a_log", gdn_a_log, g_alog, m_gdn_a_log, v_gdn_a_log)
    upd["gdn_dt_bias"] = update("gdn_dt_bias", gdn_dt_bias, g_dtb, m_gdn_dt_bias, v_gdn_dt_bias)
    upd["gdn_onorm"] = update("gdn_onorm", gdn_onorm, g_onorm, m_gdn_onorm, v_gdn_onorm)
    upd["hgrn_lb_logits"] = update("hgrn_lb_logits", hgrn_lb_logits, g_lb, m_hgrn_lb_logits, v_hgrn_lb_logits)
    upd["hgrn_gnorm"] = update("hgrn_gnorm", hgrn_gnorm, g_gnorm, m_hgrn_gnorm, v_hgrn_gnorm)
    upd["norm_mix"] = update("norm_mix", norm_mix, g_nmix, m_norm_mix, v_norm_mix)
    upd["norm_mlp"] = update("norm_mlp", norm_mlp, g_nmlp, m_norm_mlp, v_norm_mlp)
    upd["norm_final"] = update("norm_final", norm_final, g_nf, m_norm_final, v_norm_final)

    order = ["gdn_w_in", "gdn_conv", "gdn_a_log", "gdn_dt_bias", "gdn_onorm", "gdn_w_out", "hgrn_w_in",
             "hgrn_lb_logits", "hgrn_gnorm", "hgrn_w_out", "norm_mix", "norm_mlp", "mlp_w_up", "mlp_w_down",
             "norm_final"]
    outs = [loss, grad_x]
    for k in range(4):
        outs += [upd[name][k] for name in order]
    return tuple(outs)
```

```python
import functools

import numpy as np
import jax
import jax.numpy as jnp
from jax import lax
from jax.experimental import pallas as pl
from jax.experimental.pallas import tpu as pltpu

F32 = jnp.float32
BF16 = jnp.bfloat16

D_MODEL = 1024
N_HEADS = 8
HEAD_DIM = 128
CHUNK = 64
CONV_K = 4
HALO = 16
EPS = 1e-6
DEPTH = 4
N_DEV = 8
GDN_MAIN = 4 * D_MODEL
AB_PAD = 128
LANE_BLOCK = 256
ROW_BLOCK = 16
BLOCK_UNROLL = 4

ADAM_LR = 0.001
ADAM_B1 = 0.9
ADAM_B2 = 0.999
ADAM_EPS = 1e-08
ADAM_WD = 0.01
ADAM_STEP = 10

VMEM_LIMIT = 56 * 1024 * 1024
MM_TILE = 1024
MM_ROWS_MAX = 2048
MM_VMEM_BUDGET = 40 * 1024 * 1024
MM_ROWS_TILE = 512
_DIMS = {
    "nn": (((1,), (0,)), ((), ())),
    "nt": (((1,), (1,)), ((), ())),
    "tn": (((0,), (0,)), ((), ())),
}


def _parts(x, n):
    if n == 1 and x.dtype == BF16:
        return [x]
    out = []
    r = x.astype(F32)
    for i in range(n):
        p = r.astype(BF16)
        out.append(p)
        if i + 1 < n:
            r = r - p.astype(F32)
    return out


def _dot_raw(a, b, mode, na, nb):
    ap, bp = _parts(a, na), _parts(b, nb)
    nmax = max(na, nb)
    pairs = [(i, j) for i in range(na) for j in range(nb) if i + j < nmax]
    ka = 0 if mode == "tn" else 1
    kb = 1 if mode == "nt" else 0
    xa = ap[0] if len(pairs) == 1 else jnp.concatenate([ap[i] for i, _ in pairs], axis=ka)
    xb = bp[0] if len(pairs) == 1 else jnp.concatenate([bp[j] for _, j in pairs], axis=kb)
    return lax.dot_general(xa, xb, _DIMS[mode], preferred_element_type=F32)


@functools.partial(jax.custom_vjp, nondiff_argnums=(2, 3, 4))
def _dot(a, b, mode, na, nb):
    return _dot_raw(a, b, mode, na, nb)


def _dot_fwd(a, b, mode, na, nb):
    return _dot_raw(a, b, mode, na, nb), (a, b)


def _dot_bwd(mode, na, nb, res, ct):
    a, b = res
    if mode == "nn":
        da = _dot_raw(ct, b, "nt", 1, 1)
        db = _dot_raw(a, ct, "tn", 1, 1)
    elif mode == "nt":
        da = _dot_raw(ct, b, "nn", 1, 1)
        db = _dot_raw(ct, a, "tn", 1, 1)
    else:
        da = _dot_raw(b, ct, "nt", 1, 1)
        db = _dot_raw(a, ct, "nn", 1, 1)
    return da.astype(a.dtype), db.astype(b.dtype)


_dot.defvjp(_dot_fwd, _dot_bwd)


N_EXACT = 3


@jax.custom_vjp
def _dot01(x, m_wide, m):
    return lax.dot_general(m_wide, jnp.concatenate(_parts(x, N_EXACT), axis=0), _DIMS["nn"], preferred_element_type=F32)


def _dot01_fwd(x, m_wide, m):
    return _dot01(x, m_wide, m), (m_wide, m)


def _dot01_bwd(res, ct):
    m_wide, m = res
    dx = lax.dot_general(m, ct.astype(BF16), _DIMS["tn"], preferred_element_type=F32)
    return dx, jnp.zeros_like(m_wide), jnp.zeros_like(m)


_dot01.defvjp(_dot01_fwd, _dot01_bwd)


def _thrice(m):
    return jnp.concatenate([m] * N_EXACT, axis=1).astype(BF16), m.astype(BF16)


def _iota2(shape, dim):
    return lax.broadcasted_iota(jnp.int32, shape, dim)


def _tril_f32(n):
    return (_iota2((n, n), 0) >= _iota2((n, n), 1)).astype(F32)


def _below_block(n, b):
    ri, ci = _iota2((n, n), 0) // b, _iota2((n, n), 1) // b
    return (ri == ci + 1) & (ri % 2 == 1)


def _half_inverses(L):
    n = L.shape[0]
    eye = (_iota2((n, n), 0) == _iota2((n, n), 1)).astype(F32)
    d = eye - jnp.where(_below_block(n, 1), L, 0.0)
    b = 2
    while 2 * b < n:
        e = jnp.where(_below_block(n, b), L, 0.0)
        d = d - _dot_raw(d, _dot_raw(e, d, "nn", 2, 2), "nn", 2, 2)
        b *= 2
    return d, jnp.where(_below_block(n, b), L, 0.0)


def _solve_with(d, e, rhs):
    y = _dot_raw(d, rhs, "nn", 2, 2)
    return y - _dot_raw(d, _dot_raw(e, y, "nn", 2, 2), "nn", 2, 2)


@jax.custom_vjp
def _solve_unit_lower(L, rhs, d):
    n = L.shape[0]
    return _solve_with(d, jnp.where(_below_block(n, n // 2), L, 0.0), rhs)


def _solve_fwd(L, rhs, d):
    n = L.shape[0]
    e = jnp.where(_below_block(n, n // 2), L, 0.0)
    sol = _solve_with(d, e, rhs)
    return sol, (d, e, sol)


def _solve_bwd(res, ct):
    d, e, sol = res
    y = _dot_raw(d, ct - _dot_raw(e, _dot_raw(d, ct, "tn", 2, 2), "tn", 2, 2), "tn", 2, 2)
    return -_dot_raw(y, sol, "nt", 2, 2), y, jnp.zeros_like(d)


_solve_unit_lower.defvjp(_solve_fwd, _solve_bwd)


def _softplus(x):
    return jnp.maximum(x, 0.0) + jnp.log1p(jnp.exp(-jnp.abs(x)))


def _rms(x, w):
    return x * lax.rsqrt(jnp.mean(x * x, axis=-1, keepdims=True) + EPS) * w


HG_LEVELS = (32, 16, 8, 4, 2, 1)


def _hg_level_sums():
    i = np.arange(CHUNK)[:, None]
    m = np.arange(CHUNK)[None, :]
    to_row = [(m <= i) & (m // b == i // b) for b in HG_LEVELS]
    to_col = [(m > i) & (m // b == i // b) for b in HG_LEVELS if b > 1]
    return _thrice(jnp.asarray(np.concatenate(to_row + to_col + [m <= i]), F32))


def _hg_level_masks():
    i = np.arange(CHUNK)[:, None]
    j = np.arange(CHUNK)[None, :]
    return jnp.asarray(np.stack([(i // b == j // b + 1) & ((i // b) % 2 == 1) for b in HG_LEVELS]), F32)


def _hg_pre(qraw, f, lb, sums):
    g = jnp.log(lb + (1.0 - lb) * jax.nn.sigmoid(f))
    k = (1.0 - lb) * jax.nn.sigmoid(-f)
    q = jax.nn.silu(qraw) * (HEAD_DIM ** -0.5)
    return q, k, _dot01(g, *sums)


def _hg_head(st, q, k, v, e, masks):
    nl = len(HG_LEVELS)
    eye = (_iota2((CHUNK, CHUNK), 0) == _iota2((CHUNK, CHUNK), 1)).astype(F32)
    a = eye * jnp.sum(q * k, axis=-1, keepdims=True)
    for l, b in enumerate(HG_LEVELS):
        rows = q * jnp.exp(e[l * CHUNK:(l + 1) * CHUNK])
        cols = k * jnp.exp(e[(nl + l) * CHUNK:(nl + l + 1) * CHUNK]) if b > 1 else k
        a = a + masks[l] * _dot(rows, cols, "nt", 1, 1)
    gc = e[(2 * nl - 1) * CHUNK:2 * nl * CHUNK]
    o = _dot(a, v, "nn", 1, 1) + _dot(q * jnp.exp(gc), st, "nt", 1, 1)
    g_last = gc[CHUNK - 1:CHUNK]
    st_new = st * jnp.exp(g_last) + _dot(v, k * jnp.exp(g_last - gc), "tn", 1, 1)
    return o, st_new


_HG_HEADS = jax.vmap(_hg_head, in_axes=(0, 0, 0, 0, 0, None))


def _hg_post(o, gate, gw):
    return _rms(o, gw) * jax.nn.silu(gate)


def _gd_conv(xp, cw):
    off = HALO - (CONV_K - 1)
    y = cw[0:1] * xp[off:off + CHUNK]
    for kk in range(1, CONV_K):
        y = y + cw[kk:kk + 1] * xp[off + kk:off + kk + CHUNK]
    return y


def _gd_conv_bwd(xp, cw, y, dc):
    off = HALO - (CONV_K - 1)
    sig = jax.nn.sigmoid(y)
    dy = dc * (sig * (1.0 + y * (1.0 - sig)))
    dxp, dcw = None, []
    for kk in range(CONV_K):
        moved = jnp.pad(dy, ((off + kk, HALO - off - kk), (0, 0)))
        term = cw[kk:kk + 1] * moved
        dxp = term if dxp is None else dxp + term
        dcw.append(jnp.sum(xp * moved, axis=0, keepdims=True))
    return dxp, jnp.concatenate(dcw, axis=0)


def _gd_gates(a, b, alog, dtb):
    beta = jax.nn.sigmoid(b)
    g = -jnp.exp(alog) * _softplus(a + dtb)
    expand = (_iota2((N_HEADS, D_MODEL), 1) // HEAD_DIM == _iota2((N_HEADS, D_MODEL), 0)).astype(F32)
    g_x = _dot(g, expand, "nn", 3, 1)
    after = (_iota2((CHUNK, D_MODEL), 0) > _iota2((CHUNK, D_MODEL), 1) % HEAD_DIM).astype(F32)
    sums = _dot01(jnp.concatenate([g_x, g_x * after], axis=1), *_thrice(_tril_f32(CHUNK)))
    return _dot(beta, expand, "nn", 3, 1), sums


def _gd_head(st, q, k, v, beta, gc, diff, gate, onw, dinv=None):
    q = q * lax.rsqrt(jnp.sum(q * q, axis=-1, keepdims=True) + EPS) * (HEAD_DIM ** -0.5)
    k = k * lax.rsqrt(jnp.sum(k * k, axis=-1, keepdims=True) + EPS)
    ri = _iota2((CHUNK, CHUNK), 0)
    ci = _iota2((CHUNK, CHUNK), 1)
    decay = jnp.exp(jnp.where(ri >= ci, diff[:, 0:CHUNK], -jnp.inf))
    kb = k * beta
    egc = jnp.exp(gc)
    L = jnp.where(ri > ci, _dot(kb, k, "nt", 1, 1) * decay, 0.0)
    made = dinv is None
    if made:
        dinv = _half_inverses(L)[0]
    sol = _solve_unit_lower(L, jnp.concatenate([v * beta, kb * egc], axis=1), dinv)
    u = sol[:, 0:HEAD_DIM]
    w = sol[:, HEAD_DIM:2 * HEAD_DIM]
    a_qk = jnp.where(ri >= ci, _dot(q, k, "nt", 1, 1) * decay, 0.0)
    g_last = gc[CHUNK - 1:CHUNK]
    v_new = u - _dot(w, st, "nt", 1, 1)
    o = _dot(q * egc, st, "nt", 1, 1) + _dot(a_qk, v_new, "nn", 1, 1)
    st_new = st * jnp.exp(g_last) + _dot(v_new, k * jnp.exp(g_last - gc), "tn", 1, 1)
    out = (_rms(o, onw) * jax.nn.silu(gate), st_new)
    return out + (dinv,) if made else out


def _params(*sem):
    return pltpu.CompilerParams(dimension_semantics=sem, vmem_limit_bytes=VMEM_LIMIT)


def _tile(n, pref):
    t = min(n, pref)
    assert n % t == 0, (n, pref)
    return t


def _mm_tiles(m, n, k, a_size, b_size, tile_sizes):
    tn = _tile(n, MM_TILE)

    def need(tm, tk):
        acc = 4 * tm * tn * (2 if tk < k else 1)
        return 2 * (tm * tk * a_size + tk * tn * b_size + tm * tn * sum(tile_sizes)) + acc

    tk = k
    while True:
        tm = _tile(m, MM_ROWS_MAX)
        while tm > 256 and need(tm, tk) > MM_VMEM_BUDGET:
            tm //= 2
        if need(tm, tk) <= MM_VMEM_BUDGET or tk <= 512:
            return tm, tn, tk
        tk //= 2


def _mm(a, b, mode, out_dtypes, name, epilogue=None, extras=(), after=None):
    if mode == "nn":
        (m, k), (k2, n) = a.shape, b.shape
    elif mode == "nt":
        (m, k), (n, k2) = a.shape, b.shape
    else:
        (k, m), (k2, n) = a.shape, b.shape
    assert k == k2, (a.shape, b.shape, mode)
    tm, tn, tk = _mm_tiles(m, n, k, a.dtype.itemsize, b.dtype.itemsize,
                           [e.dtype.itemsize for e in extras] + [jnp.dtype(dt).itemsize for dt in out_dtypes])
    nk = k // tk
    ne, no, nafter = len(extras), len(out_dtypes), int(after is not None)
    if epilogue is None:
        epilogue = lambda acc: (acc,)

    def body(*refs):
        a_ref, b_ref = refs[0], refs[1]
        ex = refs[2:2 + ne]
        outs = refs[2 + ne + nafter:2 + ne + nafter + no]
        part = lax.dot_general(a_ref[...].astype(BF16), b_ref[...].astype(BF16), _DIMS[mode],
                               preferred_element_type=F32)

        def finish(acc):
            for o_ref, val in zip(outs, epilogue(acc, *[e[...] for e in ex])):
                o_ref[...] = val.astype(o_ref.dtype)

        if nk == 1:
            finish(part)
        else:
            acc_ref = refs[-1]
            kk = pl.program_id(2)

            @pl.when(kk == 0)
            def _():
                acc_ref[...] = part

            @pl.when(kk > 0)
            def _():
                acc_ref[...] += part

            @pl.when(kk == nk - 1)
            def _():
                finish(acc_ref[...])

    if mode == "tn":
        a_spec = pl.BlockSpec((tk, tm), lambda i, j, kk: (kk, i))
    else:
        a_spec = pl.BlockSpec((tm, tk), lambda i, j, kk: (i, kk))
    if mode == "nt":
        b_spec = pl.BlockSpec((tn, tk), lambda i, j, kk: (j, kk))
    else:
        b_spec = pl.BlockSpec((tk, tn), lambda i, j, kk: (kk, j))
    o_spec = pl.BlockSpec((tm, tn), lambda i, j, kk: (i, j))
    res = pl.pallas_call(
        body,
        name=name,
        grid=(m // tm, n // tn, nk),
        in_specs=[a_spec, b_spec] + [o_spec] * ne + [pl.BlockSpec(memory_space=pl.ANY)] * nafter,
        out_specs=[o_spec] * no,
        out_shape=[jax.ShapeDtypeStruct((m, n), dt) for dt in out_dtypes],
        scratch_shapes=[pltpu.VMEM((tm, tn), F32)] if nk > 1 else [],
        compiler_params=_params("parallel", "parallel", "arbitrary"),
    )(a, b, *extras, *([after] if nafter else []))
    return res[0] if no == 1 else res


def _mm_rows(a, b, mode, out_dtypes, name, epilogue, extras=(), vectors=(), n_sums=0, after=None):
    assert mode in ("nn", "nt")
    (m, k), n = a.shape, (b.shape[1] if mode == "nn" else b.shape[0])
    tm = _tile(m, MM_ROWS_TILE)
    mt = m // tm
    ne, no, nafter = len(extras) + len(vectors), len(out_dtypes), int(after is not None)

    def body(*refs):
        a_ref, b_ref = refs[0], refs[1]
        ex = refs[2:2 + ne]
        outs = refs[2 + ne + nafter:2 + ne + nafter + no]
        sums = refs[2 + ne + nafter + no:2 + ne + nafter + no + n_sums]
        acc_ref = refs[-1]
        i = pl.program_id(0)

        @pl.when(i == 0)
        def _():
            acc_ref[1] = jnp.zeros((tm, n), F32)

        vals = epilogue(acc_ref[1 - i % 2], *[e[...] for e in ex])
        acc_ref[i % 2] = lax.dot_general(a_ref[...].astype(BF16), b_ref[...].astype(BF16), _DIMS[mode],
                                         preferred_element_type=F32)
        for o_ref, val in zip(outs, vals[:no]):
            o_ref[...] = val.astype(o_ref.dtype)
        for s_ref, val in zip(sums, vals[no:]):
            @pl.when(i <= 1)
            def _(s_ref=s_ref, val=val):
                s_ref[...] = val

            @pl.when(i > 1)
            def _(s_ref=s_ref, val=val):
                s_ref[...] += val

    ahead = lambda i: (jnp.minimum(i, mt - 1), 0)
    behind = lambda i: (jnp.maximum(i - 1, 0), 0)
    fixed = lambda i: (0, 0)
    row = pl.BlockSpec((tm, n), behind)
    vec = pl.BlockSpec((1, n), fixed)
    res = pl.pallas_call(
        body, name=name, grid=(mt + 1,),
        in_specs=([pl.BlockSpec((tm, k), ahead), pl.BlockSpec(b.shape, fixed)] + [row] * len(extras)
                  + [vec] * len(vectors) + [pl.BlockSpec(memory_space=pl.ANY)] * nafter),
        out_specs=[row] * no + [vec] * n_sums,
        out_shape=[jax.ShapeDtypeStruct((m, n), dt) for dt in out_dtypes] + [jax.ShapeDtypeStruct((1, n), F32)] * n_sums,
        scratch_shapes=[pltpu.VMEM((2, tm, n), F32)],
        compiler_params=_params("arbitrary"),
    )(a, b, *extras, *vectors, *([after] if nafter else []))
    return res[0] if no + n_sums == 1 else res


def _ep_residual_norm(acc, res, w):
    h = res + acc
    return h, _rms(h, w)


def _ep_norm_bwd(acc, x, dres, w):
    r = lax.rsqrt(jnp.mean(x * x, axis=-1, keepdims=True) + EPS)
    g = acc * w
    dx = dres + (r * g - x * (r * r * r * jnp.mean(g * x, axis=-1, keepdims=True)))
    return dx, dx, jnp.sum(acc * (x * r), axis=0, keepdims=True)


def _rms_fwd(x, w, name, tm=512):
    n, d = x.shape
    tm = _tile(n, tm)

    def body(x_ref, w_ref, y_ref):
        y_ref[...] = _rms(x_ref[...], w_ref[...]).astype(y_ref.dtype)

    return pl.pallas_call(
        body, name=name, grid=(n // tm,),
        in_specs=[pl.BlockSpec((tm, d), lambda i: (i, 0)), pl.BlockSpec((1, d), lambda i: (0, 0))],
        out_specs=pl.BlockSpec((tm, d), lambda i: (i, 0)),
        out_shape=jax.ShapeDtypeStruct((n, d), BF16),
        compiler_params=_params("arbitrary"),
    )(x, w)


def _loss_head(h, w, target, name, tm=512):
    n, d = h.shape
    tm = _tile(n, tm)

    def body(h_ref, w_ref, t_ref, dh_ref, dhb_ref, dw_ref, sq_ref):
        y, vjp = jax.vjp(_rms, h_ref[...], w_ref[...])
        err = y - t_ref[...]
        dh, dw = vjp(err * (1.0 / d))
        dh_ref[...] = dh
        dhb_ref[...] = dh.astype(dhb_ref.dtype)
        sq = jnp.sum(err * err, axis=0, keepdims=True)

        @pl.when(pl.program_id(0) == 0)
        def _():
            dw_ref[...] = dw
            sq_ref[...] = sq

        @pl.when(pl.program_id(0) > 0)
        def _():
            dw_ref[...] += dw
            sq_ref[...] += sq

        @pl.when(pl.program_id(0) == n // tm - 1)
        def _():
            total = jnp.sum(sq_ref[...], axis=1, keepdims=True) * (0.5 / d)
            sq_ref[...] = jnp.broadcast_to(total, sq_ref.shape)

    row = pl.BlockSpec((tm, d), lambda i: (i, 0))
    vec = pl.BlockSpec((1, d), lambda i: (0, 0))
    return pl.pallas_call(
        body, name=name, grid=(n // tm,),
        in_specs=[row, vec, row],
        out_specs=[row, row, vec, vec],
        out_shape=[jax.ShapeDtypeStruct((n, d), F32), jax.ShapeDtypeStruct((n, d), BF16),
                   jax.ShapeDtypeStruct((1, d), F32), jax.ShapeDtypeStruct((1, d), F32)],
        compiler_params=_params("arbitrary"),
    )(h, w, target)


def _lower_bounds(logits):
    sm = jax.nn.softmax(logits, axis=0)
    rows = [sm[0:1] * 0.0]
    for r in range(1, DEPTH):
        rows.append(rows[-1] + sm[r:r + 1])
    return jnp.concatenate(rows, axis=0)


def _lb_fwd(logits, name):
    def body(l_ref, o_ref):
        o_ref[...] = _lower_bounds(l_ref[...])

    return pl.pallas_call(body, name=name, out_shape=jax.ShapeDtypeStruct(logits.shape, F32))(logits)


def _lb_bwd(logits, dlb, name):
    def body(l_ref, d_ref, o_ref):
        _, vjp = jax.vjp(_lower_bounds, l_ref[...])
        (o_ref[...],) = vjp(d_ref[...])

    return pl.pallas_call(body, name=name, out_shape=jax.ShapeDtypeStruct(logits.shape, F32))(logits, dlb)


def _head_slice(h):
    return pl.ds(h * HEAD_DIM, HEAD_DIM)


_GD_HEADS = jax.vmap(_gd_head, in_axes=(0, 0, 0, 0, 0, 0, 0, 0, None))
_GD_HEADS_AGAIN = jax.vmap(_gd_head, in_axes=(0, 0, 0, 0, 0, 0, 0, 0, None, 0))


def _lane_blocks(width, block_body):
    def trip(j, carry):
        block_body(lambda base=0: pl.ds(pl.multiple_of(j * LANE_BLOCK + base, LANE_BLOCK), LANE_BLOCK))
        return carry

    lax.fori_loop(0, width // LANE_BLOCK, trip, 0, unroll=BLOCK_UNROLL)


def _row_blocks(rows, block_body):
    def trip(j, carry):
        block_body(pl.ds(pl.multiple_of(j * ROW_BLOCK, ROW_BLOCK), ROW_BLOCK))
        return carry

    lax.fori_loop(0, rows // ROW_BLOCK, trip, 0, unroll=BLOCK_UNROLL)


def _hg_pre_block(p_ref, lb_ref, sums_refs, q_sc, k_sc, v_sc, e_sc, at):
    sl = at()
    q_sc[:, sl], k_sc[:, sl], e_sc[:, sl] = _hg_pre(
        p_ref[:, sl].astype(F32), p_ref[:, at(D_MODEL)].astype(F32), lb_ref[:, sl], [r[...] for r in sums_refs])
    v_sc[:, sl] = p_ref[:, at(2 * D_MODEL)].astype(F32)


def _gd_xp(halo_ref, p_ref, sl, first_chunk):
    halo = jnp.where(first_chunk, 0.0, halo_ref[:, sl].astype(F32))
    return jnp.concatenate([halo, p_ref[:, sl].astype(F32)], axis=0)


def _stack_all(ref, first=0):
    return jnp.stack([ref[s, :, _head_slice(h + first)] for s in range(ref.shape[0]) for h in range(N_HEADS)])


def _unstack_all(ref, val, first=0):
    for s in range(ref.shape[0]):
        for h in range(N_HEADS):
            ref[s, :, _head_slice(h + first)] = val[s * N_HEADS + h].astype(ref.dtype)


def _gdn_fwd_all(projm, projab, cw, alog, dtb, onw, seqs, name):
    n = projm.shape[0]
    t = n // seqs
    nc = t // CHUNK
    d = D_MODEL
    per_halo = CHUNK // HALO
    nh = seqs * N_HEADS

    def body(p_ref, halo_ref, ab_ref, cw_ref, alog_ref, dtb_ref, onw_ref, o2_ref, st_all_ref, y_ref, dinv_ref,
             st_sc, c_sc, beta_sc, g_sc):
        first_chunk = pl.program_id(0) == 0

        @pl.when(first_chunk)
        def _():
            st_sc[...] = jnp.zeros_like(st_sc)

        for s in range(seqs):
            def conv(at, s=s):
                sl = at()
                y = _gd_conv(_gd_xp(halo_ref.at[s], p_ref.at[s], sl, first_chunk), cw_ref[:, sl])
                y_ref[s, :, sl] = y
                c_sc[s, :, sl] = jax.nn.silu(y)

            _lane_blocks(3 * d, conv)
            beta_sc[s], g_sc[s] = _gd_gates(ab_ref[s, :, 0:N_HEADS], ab_ref[s, :, N_HEADS:2 * N_HEADS],
                                            alog_ref[...], dtb_ref[...])
        st_all_ref[0] = st_sc[...]
        o2, st_sc[...], dinv_ref[0] = _GD_HEADS(
            st_sc[...], _stack_all(c_sc), _stack_all(c_sc, N_HEADS), _stack_all(c_sc, 2 * N_HEADS), _stack_all(beta_sc),
            _stack_all(g_sc), _stack_all(g_sc, N_HEADS), _stack_all(p_ref, 3 * N_HEADS).astype(F32), onw_ref[...])
        _unstack_all(o2_ref, o2)

    rows = lambda c: (0, c, 0)
    const = lambda c: (0, 0)
    per_chunk = lambda c: (c, 0, 0, 0)
    p3 = projm.reshape(seqs, t, 4 * d)
    o2, st_all, conv_y, dinv_all = pl.pallas_call(
        body, name=name, grid=(nc,),
        in_specs=[pl.BlockSpec((seqs, CHUNK, 4 * d), rows),
                  pl.BlockSpec((seqs, HALO, 3 * d), lambda c: (0, jnp.maximum(c * per_halo - 1, 0), 0)),
                  pl.BlockSpec((seqs, CHUNK, AB_PAD), rows),
                  pl.BlockSpec((CONV_K, 3 * d), const), pl.BlockSpec((1, N_HEADS), const),
                  pl.BlockSpec((1, N_HEADS), const), pl.BlockSpec((1, HEAD_DIM), const)],
        out_specs=[pl.BlockSpec((seqs, CHUNK, d), rows), pl.BlockSpec((1, nh, HEAD_DIM, HEAD_DIM), per_chunk),
                   pl.BlockSpec((seqs, CHUNK, 3 * d), rows), pl.BlockSpec((1, nh, CHUNK, CHUNK), per_chunk)],
        out_shape=[jax.ShapeDtypeStruct((seqs, t, d), BF16), jax.ShapeDtypeStruct((nc, nh, HEAD_DIM, HEAD_DIM), F32),
                   jax.ShapeDtypeStruct((seqs, t, 3 * d), F32), jax.ShapeDtypeStruct((nc, nh, CHUNK, CHUNK), F32)],
        scratch_shapes=[pltpu.VMEM((nh, HEAD_DIM, HEAD_DIM), F32), pltpu.VMEM((seqs, CHUNK, 3 * d), F32),
                        pltpu.VMEM((seqs, CHUNK, d), F32), pltpu.VMEM((seqs, CHUNK, 2 * d), F32)],
        compiler_params=_params("arbitrary"),
    )(p3, p3, projab.reshape(seqs, t, AB_PAD), cw, alog, dtb, onw)
    return o2.reshape(n, d), st_all, conv_y, dinv_all


def _gdn_bwd_all(projm, projab, conv_y, cw, alog, dtb, onw, st_all, dinv_all, do2, seqs, name):
    n = projm.shape[0]
    t = n // seqs
    nc = t // CHUNK
    d = D_MODEL
    per_halo = CHUNK // HALO
    nh = seqs * N_HEADS

    def body(p_ref, halo_ref, ab_ref, y_ref, cw_ref, alog_ref, dtb_ref, onw_ref, st_all_ref, dinv_ref, do2_ref,
             dp_ref, dab_ref, dcw_ref, dalog_ref, ddtb_ref, donw_ref,
             dst_sc, dhalo_sc, c_sc, beta_sc, g_sc, dc_sc, dbeta_sc, dg_sc):
        first = pl.program_id(0) == 0
        first_chunk = pl.program_id(0) == nc - 1

        @pl.when(first)
        def _():
            dst_sc[...] = jnp.zeros_like(dst_sc)
            dhalo_sc[...] = jnp.zeros_like(dhalo_sc)

        gates_vjps = []
        for s in range(seqs):
            def act(at, s=s):
                c_sc[s, :, at()] = jax.nn.silu(y_ref[s, :, at()])

            _lane_blocks(3 * d, act)
            (beta_sc[s], g_sc[s]), gates_vjp = jax.vjp(
                _gd_gates, ab_ref[s, :, 0:N_HEADS], ab_ref[s, :, N_HEADS:2 * N_HEADS], alog_ref[...], dtb_ref[...])
            gates_vjps.append(gates_vjp)

        dinv = dinv_ref[0]
        _, vjp = jax.vjp(
            lambda *a: _GD_HEADS_AGAIN(*a, dinv), st_all_ref[0], _stack_all(c_sc), _stack_all(c_sc, N_HEADS),
            _stack_all(c_sc, 2 * N_HEADS), _stack_all(beta_sc), _stack_all(g_sc), _stack_all(g_sc, N_HEADS),
            _stack_all(p_ref, 3 * N_HEADS).astype(F32), onw_ref[...])
        dst_sc[...], dq, dk, dv, dbeta, dg, ddiff, dgate, donw = vjp((_stack_all(do2_ref).astype(F32), dst_sc[...]))
        _unstack_all(dc_sc, dq)
        _unstack_all(dc_sc, dk, N_HEADS)
        _unstack_all(dc_sc, dv, 2 * N_HEADS)
        _unstack_all(dbeta_sc, dbeta)
        _unstack_all(dg_sc, dg)
        _unstack_all(dg_sc, ddiff, N_HEADS)
        _unstack_all(dp_ref, dgate, 3 * N_HEADS)

        dalog, ddtb = None, None
        for s in range(seqs):
            def conv_bwd(at, s=s):
                sl = at()
                dxp, dcw = _gd_conv_bwd(_gd_xp(halo_ref.at[s], p_ref.at[s], sl, first_chunk), cw_ref[:, sl],
                                        y_ref[s, :, sl], dc_sc[s, :, sl])
                dqkv = jnp.concatenate([dxp[HALO:CHUNK], dxp[CHUNK:HALO + CHUNK] + dhalo_sc[s, :, sl]], axis=0)
                dp_ref[s, :, sl] = dqkv.astype(dp_ref.dtype)
                dhalo_sc[s, :, sl] = dxp[0:HALO]

                if s > 0:
                    dcw_ref[:, sl] += dcw
                    return

                @pl.when(first)
                def _():
                    dcw_ref[:, sl] = dcw

                @pl.when(jnp.logical_not(first))
                def _():
                    dcw_ref[:, sl] += dcw

            _lane_blocks(3 * d, conv_bwd)
            da, db, dalog_s, ddtb_s = gates_vjps[s]((dbeta_sc[s], dg_sc[s]))
            dab_ref[s] = jnp.concatenate(
                [da, db, jnp.zeros((CHUNK, AB_PAD - 2 * N_HEADS), F32)], axis=1).astype(dab_ref.dtype)
            dalog = dalog_s if dalog is None else dalog + dalog_s
            ddtb = ddtb_s if ddtb is None else ddtb + ddtb_s

        @pl.when(first)
        def _():
            dalog_ref[...] = dalog
            ddtb_ref[...] = ddtb
            donw_ref[...] = donw

        @pl.when(jnp.logical_not(first))
        def _():
            dalog_ref[...] += dalog
            ddtb_ref[...] += ddtb
            donw_ref[...] += donw

    back = lambda c: nc - 1 - c
    rows = lambda c: (0, back(c), 0)
    const = lambda c: (0, 0)
    per_chunk = lambda c: (back(c), 0, 0, 0)
    small = [pl.BlockSpec((CONV_K, 3 * d), const), pl.BlockSpec((1, N_HEADS), const),
             pl.BlockSpec((1, N_HEADS), const), pl.BlockSpec((1, HEAD_DIM), const)]
    p3 = projm.reshape(seqs, t, 4 * d)
    dp, dab, dcw, dalog, ddtb, donw = pl.pallas_call(
        body, name=name, grid=(nc,),
        in_specs=[pl.BlockSpec((seqs, CHUNK, 4 * d), rows),
                  pl.BlockSpec((seqs, HALO, 3 * d), lambda c: (0, jnp.maximum(back(c) * per_halo - 1, 0), 0)),
                  pl.BlockSpec((seqs, CHUNK, AB_PAD), rows), pl.BlockSpec((seqs, CHUNK, 3 * d), rows)] + small + [
                  pl.BlockSpec((1, nh, HEAD_DIM, HEAD_DIM), per_chunk), pl.BlockSpec((1, nh, CHUNK, CHUNK), per_chunk),
                  pl.BlockSpec((seqs, CHUNK, d), rows)],
        out_specs=[pl.BlockSpec((seqs, CHUNK, 4 * d), rows), pl.BlockSpec((seqs, CHUNK, AB_PAD), rows)] + small,
        out_shape=[jax.ShapeDtypeStruct((seqs, t, 4 * d), BF16), jax.ShapeDtypeStruct((seqs, t, AB_PAD), BF16),
                   jax.ShapeDtypeStruct((CONV_K, 3 * d), F32), jax.ShapeDtypeStruct((1, N_HEADS), F32),
                   jax.ShapeDtypeStruct((1, N_HEADS), F32), jax.ShapeDtypeStruct((1, HEAD_DIM), F32)],
        scratch_shapes=[pltpu.VMEM((nh, HEAD_DIM, HEAD_DIM), F32), pltpu.VMEM((seqs, HALO, 3 * d), F32),
                        pltpu.VMEM((seqs, CHUNK, 3 * d), F32), pltpu.VMEM((seqs, CHUNK, d), F32),
                        pltpu.VMEM((seqs, CHUNK, 2 * d), F32), pltpu.VMEM((seqs, CHUNK, 3 * d), F32),
                        pltpu.VMEM((seqs, CHUNK, d), F32), pltpu.VMEM((seqs, CHUNK, 2 * d), F32)],
        compiler_params=_params("arbitrary"),
    )(p3, p3, projab.reshape(seqs, t, AB_PAD), conv_y, cw, alog, dtb, onw, st_all, dinv_all,
      do2.reshape(seqs, t, d))
    return dp.reshape(n, 4 * d), dab.reshape(n, AB_PAD), dcw, dalog, ddtb, donw


def _hgrn_fwd_all(proj, lb, gw, seqs, name):
    n = proj.shape[0]
    t = n // seqs
    nc = t // CHUNK
    d = D_MODEL
    nh = seqs * N_HEADS
    sums, masks = _hg_level_sums(), _hg_level_masks()

    def body(p_ref, lb_ref, gw_ref, sums_wide_ref, sums_once_ref, masks_ref, o2_ref, o_ref, st_all_ref,
             st_sc, q_sc, k_sc, v_sc, e_sc):
        @pl.when(pl.program_id(0) == 0)
        def _():
            st_sc[...] = jnp.zeros_like(st_sc)

        sums_refs = (sums_wide_ref, sums_once_ref)
        for s in range(seqs):
            _lane_blocks(d, functools.partial(_hg_pre_block, p_ref.at[s], lb_ref, sums_refs, q_sc.at[s], k_sc.at[s],
                                              v_sc.at[s], e_sc.at[s]))
        st_all_ref[0] = st_sc[...]
        for s in range(seqs):
            one, mine = pl.ds(s, 1), pl.ds(s * N_HEADS, N_HEADS)
            o, st_sc[mine] = _HG_HEADS(st_sc[mine], *[_stack_all(r.at[one]) for r in (q_sc, k_sc, v_sc, e_sc)],
                                       masks_ref[...])
            _unstack_all(o_ref.at[one], o)

            def post(rows, s=s):
                gate = p_ref[s, rows, 3 * d:4 * d].astype(F32)
                o2_ref[s, rows, :] = _hg_post(o_ref[s, rows, :], gate, gw_ref[...]).astype(o2_ref.dtype)

            _row_blocks(CHUNK, post)

    rows = lambda c: (0, c, 0)
    vec = pl.BlockSpec((1, d), lambda c: (0, 0))
    act = pl.BlockSpec((seqs, CHUNK, d), rows)
    o2, o, st_all = pl.pallas_call(
        body, name=name, grid=(nc,),
        in_specs=[pl.BlockSpec((seqs, CHUNK, 4 * d), rows), vec, vec]
        + [pl.BlockSpec(m.shape, lambda c: (0, 0)) for m in sums] + [pl.BlockSpec(masks.shape, lambda c: (0, 0, 0))],
        out_specs=[act, act, pl.BlockSpec((1, nh, HEAD_DIM, HEAD_DIM), lambda c: (c, 0, 0, 0))],
        out_shape=[jax.ShapeDtypeStruct((seqs, t, d), BF16), jax.ShapeDtypeStruct((seqs, t, d), F32),
                   jax.ShapeDtypeStruct((nc, nh, HEAD_DIM, HEAD_DIM), F32)],
        scratch_shapes=[pltpu.VMEM((nh, HEAD_DIM, HEAD_DIM), F32)] + [pltpu.VMEM((seqs, CHUNK, d), F32)] * 3
        + [pltpu.VMEM((seqs, sums[0].shape[0], d), F32)],
        compiler_params=_params("arbitrary"),
    )(proj.reshape(seqs, t, 4 * d), lb, gw, *sums, masks)
    return o2.reshape(n, d), o, st_all


def _hgrn_bwd_all(proj, lb, gw, st_all, o, do2, seqs, name):
    n = proj.shape[0]
    t = n // seqs
    nc = t // CHUNK
    d = D_MODEL
    nh = seqs * N_HEADS
    sums, masks = _hg_level_sums(), _hg_level_masks()

    def body(p_ref, lb_ref, gw_ref, sums_wide_ref, sums_once_ref, masks_ref, st_all_ref, o_ref, do2_ref,
             dp_ref, dlb_ref, dgw_ref,
             dst_sc, q_sc, k_sc, v_sc, e_sc, do_sc, dq_sc, dk_sc, dv_sc, de_sc, dgw_sc):
        first = pl.program_id(0) == 0

        @pl.when(first)
        def _():
            dst_sc[...] = jnp.zeros_like(dst_sc)

        sums_refs = (sums_wide_ref, sums_once_ref)
        dgw_sc[...] = jnp.zeros_like(dgw_sc)
        for s in range(seqs):
            _lane_blocks(d, functools.partial(_hg_pre_block, p_ref.at[s], lb_ref, sums_refs, q_sc.at[s], k_sc.at[s],
                                              v_sc.at[s], e_sc.at[s]))

            def post_bwd(rows, s=s):
                _, vjp = jax.vjp(_hg_post, o_ref[s, rows, :], p_ref[s, rows, 3 * d:4 * d].astype(F32), gw_ref[...])
                do_sc[s, rows, :], dgate, dgw = vjp(do2_ref[s, rows, :].astype(F32))
                dp_ref[s, rows, 3 * d:4 * d] = dgate.astype(dp_ref.dtype)
                dgw_sc[...] += dgw

            _row_blocks(CHUNK, post_bwd)

        level_masks = masks_ref[...]
        _, vjp = jax.vjp(lambda *a: _HG_HEADS(*a, level_masks), st_all_ref[0],
                         *[_stack_all(r) for r in (q_sc, k_sc, v_sc, e_sc)])
        grads = vjp((_stack_all(do_sc), dst_sc[...]))
        dst_sc[...] = grads[0]
        for r, val in zip((dq_sc, dk_sc, dv_sc, de_sc), grads[1:]):
            _unstack_all(r, val)

        for s in range(seqs):
            def pre_bwd(at, s=s):
                sl = at()
                level_sums = (sums_wide_ref[...], sums_once_ref[...])
                _, vjp = jax.vjp(lambda qraw, f, lb: _hg_pre(qraw, f, lb, level_sums), p_ref[s, :, sl].astype(F32),
                                 p_ref[s, :, at(d)].astype(F32), lb_ref[:, sl])
                dqraw, df, dlb = vjp((dq_sc[s, :, sl], dk_sc[s, :, sl], de_sc[s, :, sl]))
                dp_ref[s, :, sl] = dqraw.astype(dp_ref.dtype)
                dp_ref[s, :, at(d)] = df.astype(dp_ref.dtype)
                dp_ref[s, :, at(2 * d)] = dv_sc[s, :, sl].astype(dp_ref.dtype)
                if s > 0:
                    dlb_ref[:, sl] += dlb
                    return

                @pl.when(first)
                def _():
                    dlb_ref[:, sl] = dlb

                @pl.when(jnp.logical_not(first))
                def _():
                    dlb_ref[:, sl] += dlb

            _lane_blocks(d, pre_bwd)

        @pl.when(first)
        def _():
            dgw_ref[...] = dgw_sc[...]

        @pl.when(jnp.logical_not(first))
        def _():
            dgw_ref[...] += dgw_sc[...]

    rows = lambda c: (0, nc - 1 - c, 0)
    vec = pl.BlockSpec((1, d), lambda c: (0, 0))
    act = pl.BlockSpec((seqs, CHUNK, d), rows)
    wide = pl.BlockSpec((seqs, CHUNK, 4 * d), rows)
    e_rows = sums[0].shape[0]
    dp, dlb, dgw = pl.pallas_call(
        body, name=name, grid=(nc,),
        in_specs=[wide, vec, vec] + [pl.BlockSpec(m.shape, lambda c: (0, 0)) for m in sums] + [
                  pl.BlockSpec(masks.shape, lambda c: (0, 0, 0)),
                  pl.BlockSpec((1, nh, HEAD_DIM, HEAD_DIM), lambda c: (nc - 1 - c, 0, 0, 0)), act, act],
        out_specs=[wide, vec, vec],
        out_shape=[jax.ShapeDtypeStruct((seqs, t, 4 * d), BF16), jax.ShapeDtypeStruct((1, d), F32),
                   jax.ShapeDtypeStruct((1, d), F32)],
        scratch_shapes=[pltpu.VMEM((nh, HEAD_DIM, HEAD_DIM), F32)]
        + [pltpu.VMEM((seqs, CHUNK, d), F32)] * 3 + [pltpu.VMEM((seqs, e_rows, d), F32)]
        + [pltpu.VMEM((seqs, CHUNK, d), F32)] * 4 + [pltpu.VMEM((seqs, e_rows, d), F32), pltpu.VMEM((1, d), F32)],
        compiler_params=_params("arbitrary"),
    )(proj.reshape(seqs, t, 4 * d), lb, gw, *sums, masks, st_all, o, do2.reshape(seqs, t, d))
    return dp.reshape(n, 4 * d), dlb, dgw


def _adam_update(w, g, m, v):
    b1c = 1.0 - ADAM_B1 ** ADAM_STEP
    b2c = 1.0 - ADAM_B2 ** ADAM_STEP
    m_new = ADAM_B1 * m + (1.0 - ADAM_B1) * g
    v_new = ADAM_B2 * v + (1.0 - ADAM_B2) * (g * g)
    delta = -ADAM_LR * ((m_new / b1c) / (jnp.sqrt(v_new / b2c) + ADAM_EPS) + ADAM_WD * w)
    return delta, m_new, v_new


def _adamw(w, g, m, v, name, tr=256):
    r, c = w.shape
    tr = _tile(r, tr)

    def body(w_ref, g_ref, m_ref, v_ref, d_ref, mo_ref, vo_ref):
        d_ref[...], mo_ref[...], vo_ref[...] = _adam_update(w_ref[...], g_ref[...], m_ref[...], v_ref[...])

    blk = pl.BlockSpec((tr, c), lambda i: (i, 0))
    return pl.pallas_call(
        body, name=name, grid=(r // tr,),
        in_specs=[blk] * 4, out_specs=[blk] * 3,
        out_shape=[jax.ShapeDtypeStruct((r, c), F32)] * 3,
        compiler_params=_params("arbitrary"),
    )(w, g, m, v)


def _adamw_slots(w, slot_bufs, m, v, name, tr=256):
    nl, r, c = w.shape
    tr = _tile(r, tr)

    def body(*refs):
        w_ref = refs[0]
        g_refs = refs[1:1 + nl]
        m_ref, v_ref, go_ref, d_ref, mo_ref, vo_ref = refs[1 + nl:]
        for k in range(nl):
            @pl.when(pl.program_id(0) == k)
            def _(k=k):
                g = g_refs[k][0].astype(F32)
                for s in range(1, N_DEV):
                    g = g + g_refs[k][s].astype(F32)
                go_ref[0] = g

        d_ref[0], mo_ref[0], vo_ref[0] = _adam_update(w_ref[0], go_ref[0], m_ref[0], v_ref[0])

    blk = pl.BlockSpec((1, tr, c), lambda l, i: (l, i, 0))
    g_specs = [pl.BlockSpec((N_DEV, tr, c), lambda l, i, k=k: (0, jnp.where(l == k, i, 0), 0)) for k in range(nl)]
    return pl.pallas_call(
        body, name=name, grid=(nl, r // tr),
        in_specs=[blk] + g_specs + [blk, blk], out_specs=[blk] * 4,
        out_shape=[jax.ShapeDtypeStruct((nl, r, c), F32)] * 4,
        compiler_params=_params("arbitrary", "arbitrary"),
    )(w, *slot_bufs, m, v)


def _adamw_windows(w, lo_bufs, hi_bufs, end_bufs, me, m, v, name, tr=256):
    nl, r, c = w.shape
    wl, wh = lo_bufs[0].shape[2], hi_bufs[0].shape[2]
    width, step = wl + wh, c - wl
    assert step >= 0 and (N_DEV - 1) * step + c <= width and N_DEV == 8
    tr = _tile(r, tr)

    def body(*refs):
        me_ref, w_ref = refs[0], refs[1]
        lo_refs, hi_refs, end_refs = refs[2:2 + nl], refs[2 + nl:2 + 2 * nl], refs[2 + 2 * nl:2 + 3 * nl]
        m_ref, v_ref, go_ref, d_ref, mo_ref, vo_ref = refs[2 + 3 * nl:]
        for k in range(nl):
            @pl.when(pl.program_id(0) == k)
            def _(k=k):
                last = me_ref[0] == N_DEV - 1
                hi_of = lambda s: jnp.where(last, end_refs[k][s].astype(F32), hi_refs[k][s].astype(F32))
                lo, hi = lo_refs[k][0].astype(F32), hi_of(0)
                for s in range(1, N_DEV):
                    lo, hi = lo + lo_refs[k][s].astype(F32), hi + hi_of(s)
                g = jnp.concatenate([lo, hi], axis=1)
                for bit in range(3):
                    moved = pltpu.roll(g, width - (step << bit), axis=1)
                    g = jnp.where((me_ref[0] >> bit) & 1 == 1, moved, g)
                go_ref[0] = g[:, :c]

        d_ref[0], mo_ref[0], vo_ref[0] = _adam_update(w_ref[0], go_ref[0], m_ref[0], v_ref[0])

    blk = pl.BlockSpec((1, tr, c), lambda l, i: (l, i, 0))
    g_specs = [pl.BlockSpec((N_DEV, tr, cols), lambda l, i, k=k: (0, jnp.where(l == k, i, 0), 0))
               for cols in (wl, wh, wh) for k in range(nl)]
    return pl.pallas_call(
        body, name=name, grid=(nl, r // tr),
        in_specs=[pl.BlockSpec(memory_space=pltpu.SMEM), blk] + g_specs + [blk, blk], out_specs=[blk] * 4,
        out_shape=[jax.ShapeDtypeStruct((nl, r, c), F32)] * 4,
        compiler_params=_params("arbitrary", "arbitrary"),
    )(me, w, *lo_bufs, *hi_bufs, *end_bufs, m, v)


def _unshard_windows(win, c, name, tr=256):
    nd, r, w = win.shape
    wl = w - AB_PAD
    step = c - wl
    assert 0 <= step and nd * step <= AB_PAD
    tr = _tile(r, tr)

    def body(win_ref, main_ref, tail_ref):
        lane = lax.broadcasted_iota(jnp.int32, (tr, AB_PAD), 1)

        def past(s):
            return win_ref[s, :, wl:w].astype(F32)

        for s in range(nd):
            first = win_ref[s, :, 0:AB_PAD].astype(F32)
            if s > 0:
                first = jnp.where(lane < s * step, past(s - 1), first)
            main_ref[:, s * wl:s * wl + AB_PAD] = first.astype(main_ref.dtype)
            main_ref[:, s * wl + AB_PAD:(s + 1) * wl] = win_ref[s, :, AB_PAD:wl]
        tail_ref[...] = jnp.where(lane < nd * step, past(nd - 1), 0.0).astype(tail_ref.dtype)

    return pl.pallas_call(
        body, name=name, grid=(r // tr,),
        in_specs=[pl.BlockSpec((nd, tr, w), lambda i: (0, i, 0))],
        out_specs=[pl.BlockSpec((tr, nd * wl), lambda i: (i, 0)), pl.BlockSpec((tr, AB_PAD), lambda i: (i, 0))],
        out_shape=[jax.ShapeDtypeStruct((r, nd * wl), win.dtype), jax.ShapeDtypeStruct((r, AB_PAD), win.dtype)],
        compiler_params=_params("arbitrary"),
    )(win)


def _mesh_pos():
    return lax.axis_index("x"), lax.axis_index("y"), lax.axis_index("c")


def _flip(pos, p):
    x, y, c = pos
    return ((1 - x) if p & 4 else x, (1 - y) if p & 2 else y, (1 - c) if p & 1 else c)


def _lin(pos):
    return 4 * pos[0] + 2 * pos[1] + pos[2]


_HBM = pl.BlockSpec(memory_space=pltpu.HBM)
_SEM = pl.BlockSpec(memory_space=pltpu.SEMAPHORE)
_DATAFLOW = pltpu.SideEffectType.DATAFLOW_SIDE_EFFECTING


class _Item:
    def __init__(self, src, land_shape, src_pick, dst_pick, peers=tuple(range(1, N_DEV))):
        self.src, self.land_shape, self.src_pick, self.dst_pick = src, land_shape, src_pick, dst_pick
        self.peers = peers


def _distinct(arrays):
    found, where = [], []
    for a in arrays:
        hits = [k for k, f in enumerate(found) if f is a]
        where.append(hits[0] if hits else len(found))
        if not hits:
            found.append(a)
    return found, where


def _remote_copies(items, src, land, send_sem, recv_sem, me, arriving):
    me_i = _lin(me)
    out = []
    for it, s_ref, l_ref in zip(items, src, land):
        for p in it.peers:
            peer = _flip(me, p)
            out.append(pltpu.make_async_remote_copy(
                src_ref=it.src_pick(s_ref, _lin(peer)),
                dst_ref=it.dst_pick(l_ref, _lin(peer) if arriving else me_i),
                send_sem=send_sem, recv_sem=recv_sem, device_id=peer, device_id_type=pl.DeviceIdType.MESH))
    return out


def _own_copies(items, src, land, sem, me):
    me_i = _lin(me)
    return [pltpu.make_async_copy(it.src_pick(s_ref, me_i), it.dst_pick(l_ref, me_i), sem)
            for it, s_ref, l_ref in zip(items, src, land)]


def _exchange_start(groups, name):
    items = [it for g in groups for it in g]
    n, ng = len(items), len(groups)
    first = [sum(len(g) for g in groups[:gi]) for gi in range(ng)]
    arrays, where = _distinct([it.src for it in items])
    nu = len(arrays)

    def body(*refs):
        src, land = [refs[k] for k in where], refs[nu:nu + n]
        send_sems, recv_sems = refs[nu + n:nu + n + ng], refs[nu + n + ng:nu + n + 2 * ng]
        token = refs[2 * (nu + n) + 2 * ng]
        me = _mesh_pos()
        for gi, g in enumerate(groups):
            sl = slice(first[gi], first[gi] + len(g))
            for cp in _remote_copies(g, src[sl], land[sl], send_sems[gi], recv_sems[gi], me, arriving=False):
                cp.start()
            for cp in _own_copies(g, src[sl], land[sl], recv_sems[gi], me):
                cp.start()
        token[...] = jnp.zeros_like(token)

    srcs = [pltpu.with_memory_space_constraint(a, pltpu.HBM) for a in arrays]
    lands = [pltpu.with_memory_space_constraint(lax.empty(it.land_shape, it.src.dtype), pltpu.HBM) for it in items]
    res = pl.pallas_call(
        body, name=name,
        out_shape=([pltpu.SemaphoreType.DMA(())] * (2 * ng)
                   + [pltpu.HBM(a.shape, a.dtype) for a in arrays]
                   + [pltpu.HBM(it.land_shape, it.src.dtype) for it in items]
                   + [jax.ShapeDtypeStruct((8, 128), F32)]),
        in_specs=[_HBM] * (nu + n),
        out_specs=[_SEM] * (2 * ng) + [_HBM] * (nu + n) + [pl.BlockSpec(memory_space=pltpu.VMEM)],
        input_output_aliases={i: 2 * ng + i for i in range(nu + n)},
        compiler_params=pltpu.CompilerParams(has_side_effects=_DATAFLOW),
    )(*srcs, *lands)
    send_sems, recv_sems = res[0:ng], res[ng:2 * ng]
    src_thru, land_thru = [res[2 * ng + k] for k in where], res[2 * ng + nu:2 * ng + nu + n]
    handles = []
    for gi, g in enumerate(groups):
        sl = slice(first[gi], first[gi] + len(g))
        handles.append((g, src_thru[sl], land_thru[sl], send_sems[gi], recv_sems[gi]))
    return handles, res[-1]


def _exchange_wait(handle, after, name):
    items, src_thru, land_thru, send_sem, recv_sem = handle
    k = len(items)
    arrays, where = _distinct(src_thru)
    nu = len(arrays)
    afters = list(after) if isinstance(after, (list, tuple)) else [after]

    def body(*refs):
        src, land = [refs[u] for u in where], refs[nu:nu + k]
        send_ref, recv_ref = refs[nu + k], refs[nu + k + 1]
        for cp in _remote_copies(items, src, land, send_ref, recv_ref, _mesh_pos(), arriving=True):
            cp.wait_send()
            cp.wait_recv()
        for cp in _own_copies(items, src, land, recv_ref, _mesh_pos()):
            cp.wait()

    res = pl.pallas_call(
        body, name=name,
        out_shape=([pltpu.HBM(s.shape, s.dtype) for s in arrays] + [pltpu.HBM(l.shape, l.dtype) for l in land_thru]),
        in_specs=[_HBM] * (nu + k) + [_SEM, _SEM] + [pl.BlockSpec(memory_space=pl.ANY)] * len(afters),
        out_specs=[_HBM] * (nu + k),
        input_output_aliases={i: i for i in range(nu + k)},
        compiler_params=pltpu.CompilerParams(has_side_effects=_DATAFLOW),
    )(*arrays, *land_thru, send_sem, recv_sem, *afters)
    return res[nu:nu + k]


SAME_CORE = (2, 4, 6)
SIBLING = 1


def _pass_on_start(buf, name):
    def body(buf_ref, send_sem, recv_sem, thru_ref):
        me = _mesh_pos()
        for p in SAME_CORE:
            slot = buf_ref.at[_lin(_flip(me, p))]
            pltpu.make_async_remote_copy(src_ref=slot, dst_ref=slot, send_sem=send_sem, recv_sem=recv_sem,
                                         device_id=_flip(me, SIBLING), device_id_type=pl.DeviceIdType.MESH).start()

    return pl.pallas_call(
        body, name=name,
        out_shape=[pltpu.SemaphoreType.DMA(()), pltpu.SemaphoreType.DMA(()), pltpu.HBM(buf.shape, buf.dtype)],
        in_specs=[_HBM], out_specs=[_SEM, _SEM, _HBM], input_output_aliases={0: 2},
        compiler_params=pltpu.CompilerParams(has_side_effects=_DATAFLOW),
    )(pltpu.with_memory_space_constraint(buf, pltpu.HBM))


def _pass_on_wait(handle, name):
    send_sem, recv_sem, thru = handle

    def body(buf_ref, send_ref, recv_ref, out_ref):
        me = _mesh_pos()
        sibling = _flip(me, SIBLING)
        for p in SAME_CORE:
            mine, theirs = buf_ref.at[_lin(_flip(me, p))], buf_ref.at[_lin(_flip(sibling, p))]
            cp = pltpu.make_async_remote_copy(src_ref=mine, dst_ref=theirs, send_sem=send_ref, recv_sem=recv_ref,
                                              device_id=sibling, device_id_type=pl.DeviceIdType.MESH)
            cp.wait_send()
            cp.wait_recv()

    return pl.pallas_call(
        body, name=name, out_shape=pltpu.HBM(thru.shape, thru.dtype),
        in_specs=[_HBM, _SEM, _SEM], out_specs=_HBM, input_output_aliases={0: 0},
        compiler_params=pltpu.CompilerParams(has_side_effects=_DATAFLOW),
    )(thru, send_sem, recv_sem)


def _whole(ref, i):
    return ref


def _slot(ref, i):
    return ref.at[i]


def _rows_of(r):
    return lambda ref, i: ref.at[pl.ds(pl.multiple_of(i * r, r), r), :]


def _cols_of(c):
    return lambda ref, i: ref.at[:, pl.ds(pl.multiple_of(i * c, c), c)]


def _all_reduce_small(buf, after, name):
    r, c = buf.shape

    def body(src_ref, after_ref, out_ref, all_ref, send_sems, recv_sems):
        me = _mesh_pos()
        me_i = _lin(me)
        all_ref[me_i] = src_ref[...]
        for p in range(1, N_DEV):
            peer = _flip(me, p)
            pltpu.make_async_remote_copy(
                src_ref=src_ref, dst_ref=all_ref.at[me_i], send_sem=send_sems.at[p - 1], recv_sem=recv_sems.at[p - 1],
                device_id=peer, device_id_type=pl.DeviceIdType.MESH).start()
        for p in range(1, N_DEV):
            peer = _flip(me, p)
            cp = pltpu.make_async_remote_copy(
                src_ref=src_ref, dst_ref=all_ref.at[_lin(peer)], send_sem=send_sems.at[p - 1],
                recv_sem=recv_sems.at[p - 1], device_id=peer, device_id_type=pl.DeviceIdType.MESH)
            cp.wait_recv()
            cp.wait_send()
        acc = all_ref[0]
        for s in range(1, N_DEV):
            acc = acc + all_ref[s]
        out_ref[...] = acc

    vm = pl.BlockSpec(memory_space=pltpu.VMEM)
    return pl.pallas_call(
        body, name=name, in_specs=[vm, pl.BlockSpec(memory_space=pl.ANY)], out_specs=vm,
        out_shape=jax.ShapeDtypeStruct((r, c), F32),
        scratch_shapes=[pltpu.VMEM((N_DEV, r, c), F32), pltpu.SemaphoreType.DMA((N_DEV - 1,)),
                        pltpu.SemaphoreType.DMA((N_DEV - 1,))],
        compiler_params=pltpu.CompilerParams(has_side_effects=True),
    )(buf, after)


def _unshard_cols(g):
    s, l, r, c = g.shape
    return jnp.transpose(g, (1, 2, 0, 3)).reshape(l, r, s * c)


def kernel(x, gdn_w_in, gdn_conv, gdn_a_log, gdn_dt_bias, gdn_onorm, gdn_w_out, hgrn_w_in, hgrn_lb_logits, hgrn_gnorm, hgrn_w_out, norm_mix, norm_mlp, mlp_w_up, mlp_w_down, norm_final, loss_target, m_gdn_w_in, m_gdn_conv, m_gdn_a_log, m_gdn_dt_bias, m_gdn_onorm, m_gdn_w_out, m_hgrn_w_in, m_hgrn_lb_logits, m_hgrn_gnorm, m_hgrn_w_out, m_norm_mix, m_norm_mlp, m_mlp_w_up, m_mlp_w_down, m_norm_final, v_gdn_w_in, v_gdn_conv, v_gdn_a_log, v_gdn_dt_bias, v_gdn_onorm, v_gdn_w_out, v_hgrn_w_in, v_hgrn_lb_logits, v_hgrn_gnorm, v_hgrn_w_out, v_norm_mix, v_norm_mlp, v_mlp_w_up, v_mlp_w_down, v_norm_final):
    seqs, seq_len, d = x.shape
    n = seqs * seq_len
    me_i = _lin(_mesh_pos())
    x2 = x.reshape(n, d)
    target = loss_target.reshape(n, d)
    n_gdn, n_hgrn = gdn_w_in.shape[0], hgrn_w_in.shape[0]

    r_out, r_down = gdn_w_out.shape[1], mlp_w_down.shape[1]
    c_gin, c_hin, c_up = gdn_w_in.shape[2], hgrn_w_in.shape[2], mlp_w_up.shape[2]

    def gathered(w, pick, land_shape, **kw):
        return _Item(w.astype(BF16), land_shape, _whole, pick, **kw)

    wl = GDN_MAIN // N_DEV

    def window(w):
        return lax.dynamic_update_slice(jnp.zeros((d, wl + AB_PAD), BF16), w.astype(BF16), (0, me_i * (c_gin - wl)))

    def next_tile(ref, i):
        return ref.at[:, pl.ds(pl.multiple_of(jnp.minimum(i + 1, N_DEV - 1) * wl, AB_PAD), AB_PAD)]

    groups = [[_Item(gdn_conv, (N_DEV,) + gdn_conv.shape, _whole, _slot),
               _Item(hgrn_gnorm, (N_DEV,) + hgrn_gnorm.shape, _whole, _slot)]]
    for i in range(DEPTH):
        j = i // 2
        if i % 2 == 0:
            direct = (SIBLING,) + SAME_CORE if i == 0 else tuple(range(1, N_DEV))
            groups += [[_Item(window(gdn_w_in[j]), (N_DEV, d, wl + AB_PAD), _whole, _slot, peers=direct)],
                       [gathered(gdn_w_out[j], _rows_of(r_out), (N_DEV * r_out, d))]]
        else:
            groups += [[gathered(hgrn_w_in[j], _cols_of(c_hin), (d, N_DEV * c_hin))],
                       [gathered(hgrn_w_out[j], _rows_of(r_out), (N_DEV * r_out, d))]]
        groups += [[gathered(mlp_w_up[i], _cols_of(c_up), (d, N_DEV * c_up))],
                   [gathered(mlp_w_down[i], _rows_of(r_down), (N_DEV * r_down, d))]]
    gather_handles, token = _exchange_start(groups, "gather_start")
    lbs = _lb_fwd(hgrn_lb_logits + token[0:1, 0:1], "lb_fwd")

    def arrived(k, after, name):
        return _exchange_wait(gather_handles[k], after, "gather_wait_" + name)

    saved = []
    w_in, w_ab, w_out, w_up, w_down = ([None] * DEPTH for _ in range(5))
    h = x2
    for i in range(DEPTH):
        j = i // 2
        if i == 0:
            g_conv, g_gnorm = arrived(0, h, "small")
            conv_full = _unshard_cols(g_conv)
            gnorm_full = jnp.transpose(g_gnorm, (1, 0, 2)).reshape(n_hgrn, d)
        if i == 0:
            y = _rms_fwd(h, norm_mix[0:1] + token[0:1, 0:1], "rms_mix_0")
        (w_in[i],) = arrived(1 + 4 * i, [y, lbs, conv_full, gnorm_full] if i == 0 else y, f"in_{i}")
        if i == 0:
            w_in[i] = _pass_on_wait(_pass_on_start(w_in[i], "pass_on_start_in_0"), "pass_on_wait_in_0")
        if i % 2 == 0:
            w_in[i], w_ab[i] = _unshard_windows(w_in[i], c_gin, f"gdn_w_in_{i}")
            projm = _mm(y, w_in[i], "nn", [BF16], f"gdn_proj_{i}")
            projab = _mm(y, w_ab[i], "nn", [F32], f"gdn_proj_ab_{i}")
            o2, st_all, conv_y, dinv_all = _gdn_fwd_all(projm, projab, conv_full[j], gdn_a_log[j:j + 1],
                                                    gdn_dt_bias[j:j + 1], gdn_onorm[j:j + 1], seqs, f"gdn_fwd_{i}")
            mix = (projm, projab, conv_y, st_all, dinv_all)
        else:
            proj = _mm(y, w_in[i], "nn", [BF16], f"hgrn_proj_{i}")
            o2, o_raw, st_all = _hgrn_fwd_all(proj, lbs[i:i + 1], gnorm_full[j:j + 1], seqs, f"hgrn_fwd_{i}")
            mix = (proj, o_raw, st_all)
        (w_out[i],) = arrived(2 + 4 * i, o2, f"out_{i}")
        h1, y2 = _mm_rows(o2, w_out[i], "nn", [F32, BF16], f"mix_out_{i}", epilogue=_ep_residual_norm, extras=(h,),
                     vectors=(norm_mlp[i:i + 1],))
        (w_up[i],) = arrived(3 + 4 * i, y2, f"up_{i}")
        u, act = _mm(y2, w_up[i], "nn", [BF16, BF16], f"mlp_up_{i}",
                     epilogue=lambda acc: (acc, jnp.square(jnp.maximum(acc, 0.0))))
        (w_down[i],) = arrived(4 + 4 * i, act, f"down_{i}")
        saved.append((h, y, mix, o2, h1, y2, u, act))
        if i + 1 < DEPTH:
            h, y = _mm_rows(act, w_down[i], "nn", [F32, BF16], f"mlp_down_{i}", epilogue=_ep_residual_norm, extras=(h1,),
                       vectors=(norm_mix[i + 1:i + 2],))
        else:
            h = _mm(act, w_down[i], "nn", [F32], f"mlp_down_{i}", epilogue=lambda acc, res: (res + acc,),
                    extras=(h1,))

    dh, dh_b, d_nf, sq = _loss_head(h, norm_final.reshape(1, d), target, "loss_head")

    d_nmix, d_nmlp = [None] * DEPTH, [None] * DEPTH
    d_conv, d_alog, d_dtb, d_onorm = [None] * n_gdn, [None] * n_gdn, [None] * n_gdn, [None] * n_gdn
    d_lb = [jnp.zeros((1, d), F32)] * DEPTH
    d_gnorm = [None] * n_hgrn
    mlp_handles, mix_handles = [None] * DEPTH, [None] * DEPTH
    token = None
    for i in reversed(range(DEPTH)):
        j = i // 2
        h_in, y, mix, o2, h1, y2, u, act = saved[i]
        g_down = _mm(act, dh_b, "tn", [BF16], f"g_down_{i}", after=token)
        du = _mm(dh_b, w_down[i], "nt", [BF16], f"d_u_{i}",
                 epilogue=lambda acc, uu: (acc * (2.0 * jnp.maximum(uu.astype(F32), 0.0)),), extras=(u,))
        g_up = _mm(y2, du, "tn", [BF16], f"g_up_{i}")
        mlp_handles[i], token = _exchange_start(
            [[_Item(g_down, (N_DEV, r_down, d), _rows_of(r_down), _slot)],
             [_Item(g_up, (N_DEV, d, c_up), _cols_of(c_up), _slot)]], f"scatter_start_mlp_{i}")
        dh1, dh1_b, d_nmlp[i] = _mm_rows(du, w_up[i], "nt", [F32, BF16], f"d_y2_{i}", epilogue=_ep_norm_bwd,
                                     extras=(h1, dh), vectors=(norm_mlp[i:i + 1],), n_sums=1, after=token)
        g_out = _mm(o2, dh1_b, "tn", [BF16], f"g_out_{i}")
        do2 = _mm(dh1_b, w_out[i], "nt", [BF16], f"d_o2_{i}")
        if i % 2 == 0:
            projm, projab, conv_y, st_all, dinv_all = mix
            dpm, dpab, d_conv[j], d_alog[j], d_dtb[j], d_onorm[j] = _gdn_bwd_all(
                projm, projab, conv_y, conv_full[j], gdn_a_log[j:j + 1], gdn_dt_bias[j:j + 1], gdn_onorm[j:j + 1],
                st_all, dinv_all, do2, seqs, f"gdn_bwd_{i}")
            g_main = _mm(y, dpm, "tn", [BF16], f"g_in_{i}")
            g_ab = _mm(y, dpab, "tn", [BF16], f"g_in_ab_{i}")
            in_items = [_Item(g_main, (N_DEV, d, wl), _cols_of(wl), _slot),
                        _Item(g_main, (N_DEV, d, AB_PAD), next_tile, _slot),
                        _Item(g_ab, (N_DEV, d, AB_PAD), _whole, _slot)]
            dy_ab = _mm(dpab, w_ab[i], "nt", [F32], f"d_y_ab_{i}")
            dp, dy_extras = dpm, (dy_ab, h_in, dh1)
            dy_epilogue = lambda acc, e, xx, dres, w: _ep_norm_bwd(acc + e, xx, dres, w)
        else:
            proj, o_raw, st_all = mix
            dp, d_lb[i], d_gnorm[j] = _hgrn_bwd_all(proj, lbs[i:i + 1], gnorm_full[j:j + 1], st_all, o_raw, do2,
                                               seqs, f"hgrn_bwd_{i}")
            g_in = _mm(y, dp, "tn", [BF16], f"g_in_{i}")
            in_items = [_Item(g_in, (N_DEV, d, c_hin), _cols_of(c_hin), _slot)]
            dy_extras, dy_epilogue = (h_in, dh1), _ep_norm_bwd
        mix_handles[i], token = _exchange_start(
            [[_Item(g_out, (N_DEV, r_out, d), _rows_of(r_out), _slot)], in_items], f"scatter_start_mix_{i}")
        dh, dh_b, d_nmix[i] = _mm_rows(dp, w_in[i], "nt", [F32, BF16], f"d_y_{i}", epilogue=dy_epilogue, extras=dy_extras,
                                  vectors=(norm_mix[i:i + 1],), n_sums=1, after=token)
        token = None
    grad_x = dh.reshape(x.shape)

    def landed(handles, k, layers, after, name):
        return [_exchange_wait(handles[i][k], after, f"scatter_wait_{name}_{i}")[0] for i in layers]

    every, even, odd = range(DEPTH), range(0, DEPTH, 2), range(1, DEPTH, 2)
    upd = {}
    upd["mlp_w_down"] = _adamw_slots(mlp_w_down, landed(mlp_handles, 0, every, dh, "down"), m_mlp_w_down,
                                     v_mlp_w_down, "adamw_mlp_w_down")
    upd["mlp_w_up"] = _adamw_slots(mlp_w_up, landed(mlp_handles, 1, every, upd["mlp_w_down"][1], "up"), m_mlp_w_up,
                                   v_mlp_w_up, "adamw_mlp_w_up")
    upd["hgrn_w_out"] = _adamw_slots(hgrn_w_out, landed(mix_handles, 0, odd, upd["mlp_w_up"][1], "out"),
                                     m_hgrn_w_out, v_hgrn_w_out, "adamw_hgrn_w_out")
    upd["hgrn_w_in"] = _adamw_slots(hgrn_w_in, landed(mix_handles, 1, odd, upd["hgrn_w_out"][1], "in"), m_hgrn_w_in,
                                    v_hgrn_w_in, "adamw_hgrn_w_in")

    dlb_rows = jnp.concatenate(d_lb, axis=0)
    tail = jnp.concatenate(
        [jnp.concatenate(d_onorm, axis=1), jnp.concatenate(d_alog, axis=1), jnp.concatenate(d_dtb, axis=1)], axis=1)
    tail = jnp.pad(tail, ((0, 0), (0, d - tail.shape[1])))
    conv_rows = jnp.stack(d_conv).reshape(-1, d)
    packed = jnp.concatenate(
        [jnp.concatenate(d_nmix, axis=0), jnp.concatenate(d_nmlp, axis=0), d_nf, sq, dlb_rows,
         jnp.concatenate(d_gnorm, axis=0), tail, conv_rows], axis=0)
    pad_rows = (-packed.shape[0]) % 8
    packed = jnp.pad(packed, ((0, pad_rows), (0, 0)))
    tot = _all_reduce_small(packed, upd["hgrn_w_in"][1], "reduce_small")

    upd["gdn_w_out"] = _adamw_slots(gdn_w_out, landed(mix_handles, 0, even, tot, "out"),
                                    m_gdn_w_out, v_gdn_w_out, "adamw_gdn_w_out")
    windows = [_exchange_wait(mix_handles[i][1], upd["gdn_w_out"][1], f"scatter_wait_in_{i}") for i in even]
    upd["gdn_w_in"] = _adamw_windows(gdn_w_in, *([win[k] for win in windows] for k in range(3)),
                                     me_i.astype(jnp.int32).reshape(1), m_gdn_w_in, v_gdn_w_in, "adamw_gdn_w_in")

    def update(name, w, g, m, v):
        shape = w.shape
        c = shape[-1]
        res = _adamw(w.reshape(-1, c), g.reshape(-1, c), m.reshape(-1, c), v.reshape(-1, c), "adamw_" + name)
        return [g.reshape(shape)] + [o.reshape(shape) for o in res]

    r0 = 0
    g_nmix = tot[r0:r0 + DEPTH]; r0 += DEPTH
    g_nmlp = tot[r0:r0 + DEPTH]; r0 += DEPTH
    g_nf = tot[r0]; r0 += 1
    loss = tot[r0, 0]; r0 += 1
    g_lb = _lb_bwd(hgrn_lb_logits, tot[r0:r0 + DEPTH], "lb_bwd"); r0 += DEPTH
    g_gnorm_full = tot[r0:r0 + n_hgrn]; r0 += n_hgrn
    t_row = tot[r0]; r0 += 1
    g_conv_full = tot[r0:r0 + n_gdn * CONV_K * 3].reshape(n_gdn, CONV_K, 3 * d)
    g_onorm = t_row[0:n_gdn * HEAD_DIM].reshape(n_gdn, HEAD_DIM)
    o1 = n_gdn * HEAD_DIM
    g_alog = t_row[o1:o1 + n_gdn * N_HEADS].reshape(n_gdn, N_HEADS)
    g_dtb = t_row[o1 + n_gdn * N_HEADS:o1 + 2 * n_gdn * N_HEADS].reshape(n_gdn, N_HEADS)
    c_gn, c_cv = hgrn_gnorm.shape[1], gdn_conv.shape[2]
    g_gnorm = lax.dynamic_slice_in_dim(g_gnorm_full, me_i * c_gn, c_gn, axis=1)
    g_conv = lax.dynamic_slice_in_dim(g_conv_full, me_i * c_cv, c_cv, axis=2)

    upd["gdn_conv"] = update("gdn_conv", gdn_conv, g_conv, m_gdn_conv, v_gdn_conv)
    upd["gdn_a_log"] = update("gdn_a_log", gdn_a_log, g_alog, m_gdn_a_log, v_gdn_a_log)
    upd["gdn_dt_bias"] = update("gdn_dt_bias", gdn_dt_bias, g_dtb, m_gdn_dt_bias, v_gdn_dt_bias)
    upd["gdn_onorm"] = update("gdn_onorm", gdn_onorm, g_onorm, m_gdn_onorm, v_gdn_onorm)
    upd["hgrn_lb_logits"] = update("hgrn_lb_logits", hgrn_lb_logits, g_lb, m_hgrn_lb_logits, v_hgrn_lb_logits)
    upd["hgrn_gnorm"] = update("hgrn_gnorm", hgrn_gnorm, g_gnorm, m_hgrn_gnorm, v_hgrn_gnorm)
    upd["norm_mix"] = update("norm_mix", norm_mix, g_nmix, m_norm_mix, v_norm_mix)
    upd["norm_mlp"] = update("norm_mlp", norm_mlp, g_nmlp, m_norm_mlp, v_norm_mlp)
    upd["norm_final"] = update("norm_final", norm_final, g_nf, m_norm_final, v_norm_final)

    order = ["gdn_w_in", "gdn_conv", "gdn_a_log", "gdn_dt_bias", "gdn_onorm", "gdn_w_out", "hgrn_w_in",
             "hgrn_lb_logits", "hgrn_gnorm", "hgrn_w_out", "norm_mix", "norm_mlp", "mlp_w_up", "mlp_w_down",
             "norm_final"]
    outs = [loss, grad_x]
    for k in range(4):
        outs += [upd[name][k] for name in order]
    return tuple(outs)
```

```python
import functools

import numpy as np
import jax
import jax.numpy as jnp
from jax import lax
from jax.experimental import pallas as pl
from jax.experimental.pallas import tpu as pltpu

F32 = jnp.float32
BF16 = jnp.bfloat16

D_MODEL = 1024
N_HEADS = 8
HEAD_DIM = 128
CHUNK = 64
CONV_K = 4
HALO = 16
EPS = 1e-6
DEPTH = 4
N_DEV = 8
GDN_MAIN = 4 * D_MODEL
AB_PAD = 128
LANE_BLOCK = 256
ROW_BLOCK = 16
BLOCK_UNROLL = 4

ADAM_LR = 0.001
ADAM_B1 = 0.9
ADAM_B2 = 0.999
ADAM_EPS = 1e-08
ADAM_WD = 0.01
ADAM_STEP = 10

VMEM_LIMIT = 56 * 1024 * 1024
MM_TILE = 1024
MM_ROWS_MAX = 2048
MM_VMEM_BUDGET = 40 * 1024 * 1024
MM_ROWS_TILE = 512
_DIMS = {
    "nn": (((1,), (0,)), ((), ())),
    "nt": (((1,), (1,)), ((), ())),
    "tn": (((0,), (0,)), ((), ())),
}


def _parts(x, n):
    if n == 1 and x.dtype == BF16:
        return [x]
    out = []
    r = x.astype(F32)
    for i in range(n):
        p = r.astype(BF16)
        out.append(p)
        if i + 1 < n:
            r = r - p.astype(F32)
    return out


def _dot_raw(a, b, mode, na, nb):
    ap, bp = _parts(a, na), _parts(b, nb)
    nmax = max(na, nb)
    pairs = [(i, j) for i in range(na) for j in range(nb) if i + j < nmax]
    ka = 0 if mode == "tn" else 1
    kb = 1 if mode == "nt" else 0
    xa = ap[0] if len(pairs) == 1 else jnp.concatenate([ap[i] for i, _ in pairs], axis=ka)
    xb = bp[0] if len(pairs) == 1 else jnp.concatenate([bp[j] for _, j in pairs], axis=kb)
    return lax.dot_general(xa, xb, _DIMS[mode], preferred_element_type=F32)


@functools.partial(jax.custom_vjp, nondiff_argnums=(2, 3, 4))
def _dot(a, b, mode, na, nb):
    return _dot_raw(a, b, mode, na, nb)


def _dot_fwd(a, b, mode, na, nb):
    return _dot_raw(a, b, mode, na, nb), (a, b)


def _dot_bwd(mode, na, nb, res, ct):
    a, b = res
    if mode == "nn":
        da = _dot_raw(ct, b, "nt", 1, 1)
        db = _dot_raw(a, ct, "tn", 1, 1)
    elif mode == "nt":
        da = _dot_raw(ct, b, "nn", 1, 1)
        db = _dot_raw(ct, a, "tn", 1, 1)
    else:
        da = _dot_raw(b, ct, "nt", 1, 1)
        db = _dot_raw(a, ct, "nn", 1, 1)
    return da.astype(a.dtype), db.astype(b.dtype)


_dot.defvjp(_dot_fwd, _dot_bwd)


N_EXACT = 3


@jax.custom_vjp
def _dot01(x, m_wide, m):
    return lax.dot_general(m_wide, jnp.concatenate(_parts(x, N_EXACT), axis=0), _DIMS["nn"], preferred_element_type=F32)


def _dot01_fwd(x, m_wide, m):
    return _dot01(x, m_wide, m), (m_wide, m)


def _dot01_bwd(res, ct):
    m_wide, m = res
    dx = lax.dot_general(m, ct.astype(BF16), _DIMS["tn"], preferred_element_type=F32)
    return dx, jnp.zeros_like(m_wide), jnp.zeros_like(m)


_dot01.defvjp(_dot01_fwd, _dot01_bwd)


def _thrice(m):
    return jnp.concatenate([m] * N_EXACT, axis=1).astype(BF16), m.astype(BF16)


def _iota2(shape, dim):
    return lax.broadcasted_iota(jnp.int32, shape, dim)


def _tril_f32(n):
    return (_iota2((n, n), 0) >= _iota2((n, n), 1)).astype(F32)


def _below_block(n, b):
    ri, ci = _iota2((n, n), 0) // b, _iota2((n, n), 1) // b
    return (ri == ci + 1) & (ri % 2 == 1)


def _half_inverses(L):
    n = L.shape[0]
    eye = (_iota2((n, n), 0) == _iota2((n, n), 1)).astype(F32)
    d = eye - jnp.where(_below_block(n, 1), L, 0.0)
    b = 2
    while 2 * b < n:
        e = jnp.where(_below_block(n, b), L, 0.0)
        d = d - _dot_raw(d, _dot_raw(e, d, "nn", 2, 2), "nn", 2, 2)
        b *= 2
    return d, jnp.where(_below_block(n, b), L, 0.0)


def _solve_with(d, e, rhs):
    y = _dot_raw(d, rhs, "nn", 2, 2)
    return y - _dot_raw(d, _dot_raw(e, y, "nn", 2, 2), "nn", 2, 2)


@jax.custom_vjp
def _solve_unit_lower(L, rhs, d):
    n = L.shape[0]
    return _solve_with(d, jnp.where(_below_block(n, n // 2), L, 0.0), rhs)


def _solve_fwd(L, rhs, d):
    n = L.shape[0]
    e = jnp.where(_below_block(n, n // 2), L, 0.0)
    sol = _solve_with(d, e, rhs)
    return sol, (d, e, sol)


def _solve_bwd(res, ct):
    d, e, sol = res
    y = _dot_raw(d, ct - _dot_raw(e, _dot_raw(d, ct, "tn", 2, 2), "tn", 2, 2), "tn", 2, 2)
    return -_dot_raw(y, sol, "nt", 2, 2), y, jnp.zeros_like(d)


_solve_unit_lower.defvjp(_solve_fwd, _solve_bwd)


def _softplus(x):
    return jnp.maximum(x, 0.0) + jnp.log1p(jnp.exp(-jnp.abs(x)))


def _rms(x, w):
    return x * lax.rsqrt(jnp.mean(x * x, axis=-1, keepdims=True) + EPS) * w


HG_LEVELS = (32, 16, 8, 4, 2, 1)


def _hg_level_sums():
    i = np.arange(CHUNK)[:, None]
    m = np.arange(CHUNK)[None, :]
    to_row = [(m <= i) & (m // b == i // b) for b in HG_LEVELS]
    to_col = [(m > i) & (m // b == i // b) for b in HG_LEVELS if b > 1]
    return _thrice(jnp.asarray(np.concatenate(to_row + to_col + [m <= i]), F32))


def _hg_level_masks():
    i = np.arange(CHUNK)[:, None]
    j = np.arange(CHUNK)[None, :]
    return jnp.asarray(np.stack([(i // b == j // b + 1) & ((i // b) % 2 == 1) for b in HG_LEVELS]), F32)


def _hg_pre(qraw, f, lb, sums):
    g = jnp.log(lb + (1.0 - lb) * jax.nn.sigmoid(f))
    k = (1.0 - lb) * jax.nn.sigmoid(-f)
    q = jax.nn.silu(qraw) * (HEAD_DIM ** -0.5)
    return q, k, _dot01(g, *sums)


def _hg_head(st, q, k, v, e, masks):
    nl = len(HG_LEVELS)
    eye = (_iota2((CHUNK, CHUNK), 0) == _iota2((CHUNK, CHUNK), 1)).astype(F32)
    a = eye * jnp.sum(q * k, axis=-1, keepdims=True)
    for l, b in enumerate(HG_LEVELS):
        rows = q * jnp.exp(e[l * CHUNK:(l + 1) * CHUNK])
        cols = k * jnp.exp(e[(nl + l) * CHUNK:(nl + l + 1) * CHUNK]) if b > 1 else k
        a = a + masks[l] * _dot(rows, cols, "nt", 1, 1)
    gc = e[(2 * nl - 1) * CHUNK:2 * nl * CHUNK]
    o = _dot(a, v, "nn", 1, 1) + _dot(q * jnp.exp(gc), st, "nt", 1, 1)
    g_last = gc[CHUNK - 1:CHUNK]
    st_new = st * jnp.exp(g_last) + _dot(v, k * jnp.exp(g_last - gc), "tn", 1, 1)
    return o, st_new


_HG_HEADS = jax.vmap(_hg_head, in_axes=(0, 0, 0, 0, 0, None))


def _hg_post(o, gate, gw):
    return _rms(o, gw) * jax.nn.silu(gate)


def _gd_conv(xp, cw):
    off = HALO - (CONV_K - 1)
    y = cw[0:1] * xp[off:off + CHUNK]
    for kk in range(1, CONV_K):
        y = y + cw[kk:kk + 1] * xp[off + kk:off + kk + CHUNK]
    return y


def _gd_conv_bwd(xp, cw, y, dc):
    off = HALO - (CONV_K - 1)
    sig = jax.nn.sigmoid(y)
    dy = dc * (sig * (1.0 + y * (1.0 - sig)))
    dxp, dcw = None, []
    for kk in range(CONV_K):
        moved = jnp.pad(dy, ((off + kk, HALO - off - kk), (0, 0)))
        term = cw[kk:kk + 1] * moved
        dxp = term if dxp is None else dxp + term
        dcw.append(jnp.sum(xp * moved, axis=0, keepdims=True))
    return dxp, jnp.concatenate(dcw, axis=0)


def _gd_gates(a, b, alog, dtb):
    beta = jax.nn.sigmoid(b)
    g = -jnp.exp(alog) * _softplus(a + dtb)
    expand = (_iota2((N_HEADS, D_MODEL), 1) // HEAD_DIM == _iota2((N_HEADS, D_MODEL), 0)).astype(F32)
    g_x = _dot(g, expand, "nn", 3, 1)
    after = (_iota2((CHUNK, D_MODEL), 0) > _iota2((CHUNK, D_MODEL), 1) % HEAD_DIM).astype(F32)
    sums = _dot01(jnp.concatenate([g_x, g_x * after], axis=1), *_thrice(_tril_f32(CHUNK)))
    return _dot(beta, expand, "nn", 3, 1), sums


def _gd_head(st, q, k, v, beta, gc, diff, gate, onw, dinv=None):
    q = q * lax.rsqrt(jnp.sum(q * q, axis=-1, keepdims=True) + EPS) * (HEAD_DIM ** -0.5)
    k = k * lax.rsqrt(jnp.sum(k * k, axis=-1, keepdims=True) + EPS)
    ri = _iota2((CHUNK, CHUNK), 0)
    ci = _iota2((CHUNK, CHUNK), 1)
    decay = jnp.exp(jnp.where(ri >= ci, diff[:, 0:CHUNK], -jnp.inf))
    kb = k * beta
    egc = jnp.exp(gc)
    L = jnp.where(ri > ci, _dot(kb, k, "nt", 1, 1) * decay, 0.0)
    made = dinv is None
    if made:
        dinv = _half_inverses(L)[0]
    sol = _solve_unit_lower(L, jnp.concatenate([v * beta, kb * egc], axis=1), dinv)
    u = sol[:, 0:HEAD_DIM]
    w = sol[:, HEAD_DIM:2 * HEAD_DIM]
    a_qk = jnp.where(ri >= ci, _dot(q, k, "nt", 1, 1) * decay, 0.0)
    g_last = gc[CHUNK - 1:CHUNK]
    v_new = u - _dot(w, st, "nt", 1, 1)
    o = _dot(q * egc, st, "nt", 1, 1) + _dot(a_qk, v_new, "nn", 1, 1)
    st_new = st * jnp.exp(g_last) + _dot(v_new, k * jnp.exp(g_last - gc), "tn", 1, 1)
    out = (_rms(o, onw) * jax.nn.silu(gate), st_new)
    return out + (dinv,) if made else out


def _params(*sem):
    return pltpu.CompilerParams(dimension_semantics=sem, vmem_limit_bytes=VMEM_LIMIT)


def _tile(n, pref):
    t = min(n, pref)
    assert n % t == 0, (n, pref)
    return t


def _mm_tiles(m, n, k, a_size, b_size, tile_sizes):
    tn = _tile(n, MM_TILE)

    def need(tm, tk):
        acc = 4 * tm * tn * (2 if tk < k else 1)
        return 2 * (tm * tk * a_size + tk * tn * b_size + tm * tn * sum(tile_sizes)) + acc

    tk = k
    while True:
        tm = _tile(m, MM_ROWS_MAX)
        while tm > 256 and need(tm, tk) > MM_VMEM_BUDGET:
            tm //= 2
        if need(tm, tk) <= MM_VMEM_BUDGET or tk <= 512:
            return tm, tn, tk
        tk //= 2


def _mm(a, b, mode, out_dtypes, name, epilogue=None, extras=(), after=None):
    if mode == "nn":
        (m, k), (k2, n) = a.shape, b.shape
    elif mode == "nt":
        (m, k), (n, k2) = a.shape, b.shape
    else:
        (k, m), (k2, n) = a.shape, b.shape
    assert k == k2, (a.shape, b.shape, mode)
    tm, tn, tk = _mm_tiles(m, n, k, a.dtype.itemsize, b.dtype.itemsize,
                           [e.dtype.itemsize for e in extras] + [jnp.dtype(dt).itemsize for dt in out_dtypes])
    nk = k // tk
    ne, no, nafter = len(extras), len(out_dtypes), int(after is not None)
    if epilogue is None:
        epilogue = lambda acc: (acc,)

    def body(*refs):
        a_ref, b_ref = refs[0], refs[1]
        ex = refs[2:2 + ne]
        outs = refs[2 + ne + nafter:2 + ne + nafter + no]
        part = lax.dot_general(a_ref[...].astype(BF16), b_ref[...].astype(BF16), _DIMS[mode],
                               preferred_element_type=F32)

        def finish(acc):
            for o_ref, val in zip(outs, epilogue(acc, *[e[...] for e in ex])):
                o_ref[...] = val.astype(o_ref.dtype)

        if nk == 1:
            finish(part)
        else:
            acc_ref = refs[-1]
            kk = pl.program_id(2)

            @pl.when(kk == 0)
            def _():
                acc_ref[...] = part

            @pl.when(kk > 0)
            def _():
                acc_ref[...] += part

            @pl.when(kk == nk - 1)
            def _():
                finish(acc_ref[...])

    if mode == "tn":
        a_spec = pl.BlockSpec((tk, tm), lambda i, j, kk: (kk, i))
    else:
        a_spec = pl.BlockSpec((tm, tk), lambda i, j, kk: (i, kk))
    if mode == "nt":
        b_spec = pl.BlockSpec((tn, tk), lambda i, j, kk: (j, kk))
    else:
        b_spec = pl.BlockSpec((tk, tn), lambda i, j, kk: (kk, j))
    o_spec = pl.BlockSpec((tm, tn), lambda i, j, kk: (i, j))
    res = pl.pallas_call(
        body,
        name=name,
        grid=(m // tm, n // tn, nk),
        in_specs=[a_spec, b_spec] + [o_spec] * ne + [pl.BlockSpec(memory_space=pl.ANY)] * nafter,
        out_specs=[o_spec] * no,
        out_shape=[jax.ShapeDtypeStruct((m, n), dt) for dt in out_dtypes],
        scratch_shapes=[pltpu.VMEM((tm, tn), F32)] if nk > 1 else [],
        compiler_params=_params("parallel", "parallel", "arbitrary"),
    )(a, b, *extras, *([after] if nafter else []))
    return res[0] if no == 1 else res


def _mm_rows(a, b, mode, out_dtypes, name, epilogue, extras=(), vectors=(), n_sums=0, after=None):
    assert mode in ("nn", "nt")
    (m, k), n = a.shape, (b.shape[1] if mode == "nn" else b.shape[0])
    tm = _tile(m, MM_ROWS_TILE)
    mt = m // tm
    ne, no, nafter = len(extras) + len(vectors), len(out_dtypes), int(after is not None)

    def body(*refs):
        a_ref, b_ref = refs[0], refs[1]
        ex = refs[2:2 + ne]
        outs = refs[2 + ne + nafter:2 + ne + nafter + no]
        sums = refs[2 + ne + nafter + no:2 + ne + nafter + no + n_sums]
        acc_ref = refs[-1]
        i = pl.program_id(0)

        @pl.when(i == 0)
        def _():
            acc_ref[1] = jnp.zeros((tm, n), F32)

        vals = epilogue(acc_ref[1 - i % 2], *[e[...] for e in ex])
        acc_ref[i % 2] = lax.dot_general(a_ref[...].astype(BF16), b_ref[...].astype(BF16), _DIMS[mode],
                                         preferred_element_type=F32)
        for o_ref, val in zip(outs, vals[:no]):
            o_ref[...] = val.astype(o_ref.dtype)
        for s_ref, val in zip(sums, vals[no:]):
            @pl.when(i <= 1)
            def _(s_ref=s_ref, val=val):
                s_ref[...] = val

            @pl.when(i > 1)
            def _(s_ref=s_ref, val=val):
                s_ref[...] += val

    ahead = lambda i: (jnp.minimum(i, mt - 1), 0)
    behind = lambda i: (jnp.maximum(i - 1, 0), 0)
    fixed = lambda i: (0, 0)
    row = pl.BlockSpec((tm, n), behind)
    vec = pl.BlockSpec((1, n), fixed)
    res = pl.pallas_call(
        body, name=name, grid=(mt + 1,),
        in_specs=([pl.BlockSpec((tm, k), ahead), pl.BlockSpec(b.shape, fixed)] + [row] * len(extras)
                  + [vec] * len(vectors) + [pl.BlockSpec(memory_space=pl.ANY)] * nafter),
        out_specs=[row] * no + [vec] * n_sums,
        out_shape=[jax.ShapeDtypeStruct((m, n), dt) for dt in out_dtypes] + [jax.ShapeDtypeStruct((1, n), F32)] * n_sums,
        scratch_shapes=[pltpu.VMEM((2, tm, n), F32)],
        compiler_params=_params("arbitrary"),
    )(a, b, *extras, *vectors, *([after] if nafter else []))
    return res[0] if no + n_sums == 1 else res


def _ep_residual_norm(acc, res, w):
    h = res + acc
    return h, _rms(h, w)


def _ep_norm_bwd(acc, x, dres, w):
    r = lax.rsqrt(jnp.mean(x * x, axis=-1, keepdims=True) + EPS)
    g = acc * w
    dx = dres + (r * g - x * (r * r * r * jnp.mean(g * x, axis=-1, keepdims=True)))
    return dx, dx, jnp.sum(acc * (x * r), axis=0, keepdims=True)


def _rms_fwd(x, w, name, tm=512):
    n, d = x.shape
    tm = _tile(n, tm)

    def body(x_ref, w_ref, y_ref):
        y_ref[...] = _rms(x_ref[...], w_ref[...]).astype(y_ref.dtype)

    return pl.pallas_call(
        body, name=name, grid=(n // tm,),
        in_specs=[pl.BlockSpec((tm, d), lambda i: (i, 0)), pl.BlockSpec((1, d), lambda i: (0, 0))],
        out_specs=pl.BlockSpec((tm, d), lambda i: (i, 0)),
        out_shape=jax.ShapeDtypeStruct((n, d), BF16),
        compiler_params=_params("arbitrary"),
    )(x, w)


def _loss_head(h, w, target, name, tm=512):
    n, d = h.shape
    tm = _tile(n, tm)

    def body(h_ref, w_ref, t_ref, dh_ref, dhb_ref, dw_ref, sq_ref):
        y, vjp = jax.vjp(_rms, h_ref[...], w_ref[...])
        err = y - t_ref[...]
        dh, dw = vjp(err * (1.0 / d))
        dh_ref[...] = dh
        dhb_ref[...] = dh.astype(dhb_ref.dtype)
        sq = jnp.sum(err * err, axis=0, keepdims=True)

        @pl.when(pl.program_id(0) == 0)
        def _():
            dw_ref[...] = dw
            sq_ref[...] = sq

        @pl.when(pl.program_id(0) > 0)
        def _():
            dw_ref[...] += dw
            sq_ref[...] += sq

        @pl.when(pl.program_id(0) == n // tm - 1)
        def _():
            total = jnp.sum(sq_ref[...], axis=1, keepdims=True) * (0.5 / d)
            sq_ref[...] = jnp.broadcast_to(total, sq_ref.shape)

    row = pl.BlockSpec((tm, d), lambda i: (i, 0))
    vec = pl.BlockSpec((1, d), lambda i: (0, 0))
    return pl.pallas_call(
        body, name=name, grid=(n // tm,),
        in_specs=[row, vec, row],
        out_specs=[row, row, vec, vec],
        out_shape=[jax.ShapeDtypeStruct((n, d), F32), jax.ShapeDtypeStruct((n, d), BF16),
                   jax.ShapeDtypeStruct((1, d), F32), jax.ShapeDtypeStruct((1, d), F32)],
        compiler_params=_params("arbitrary"),
    )(h, w, target)


def _lower_bounds(logits):
    sm = jax.nn.softmax(logits, axis=0)
    rows = [sm[0:1] * 0.0]
    for r in range(1, DEPTH):
        rows.append(rows[-1] + sm[r:r + 1])
    return jnp.concatenate(rows, axis=0)


def _lb_fwd(logits, name):
    def body(l_ref, o_ref):
        o_ref[...] = _lower_bounds(l_ref[...])

    return pl.pallas_call(body, name=name, out_shape=jax.ShapeDtypeStruct(logits.shape, F32))(logits)


def _lb_bwd(logits, dlb, name):
    def body(l_ref, d_ref, o_ref):
        _, vjp = jax.vjp(_lower_bounds, l_ref[...])
        (o_ref[...],) = vjp(d_ref[...])

    return pl.pallas_call(body, name=name, out_shape=jax.ShapeDtypeStruct(logits.shape, F32))(logits, dlb)


def _head_slice(h):
    return pl.ds(h * HEAD_DIM, HEAD_DIM)


_GD_HEADS = jax.vmap(_gd_head, in_axes=(0, 0, 0, 0, 0, 0, 0, 0, None))
_GD_HEADS_AGAIN = jax.vmap(_gd_head, in_axes=(0, 0, 0, 0, 0, 0, 0, 0, None, 0))


def _lane_blocks(width, block_body):
    def trip(j, carry):
        block_body(lambda base=0: pl.ds(pl.multiple_of(j * LANE_BLOCK + base, LANE_BLOCK), LANE_BLOCK))
        return carry

    lax.fori_loop(0, width // LANE_BLOCK, trip, 0, unroll=BLOCK_UNROLL)


def _row_blocks(rows, block_body):
    def trip(j, carry):
        block_body(pl.ds(pl.multiple_of(j * ROW_BLOCK, ROW_BLOCK), ROW_BLOCK))
        return carry

    lax.fori_loop(0, rows // ROW_BLOCK, trip, 0, unroll=BLOCK_UNROLL)


def _hg_pre_block(p_ref, lb_ref, sums_refs, q_sc, k_sc, v_sc, e_sc, at):
    sl = at()
    q_sc[:, sl], k_sc[:, sl], e_sc[:, sl] = _hg_pre(
        p_ref[:, sl].astype(F32), p_ref[:, at(D_MODEL)].astype(F32), lb_ref[:, sl], [r[...] for r in sums_refs])
    v_sc[:, sl] = p_ref[:, at(2 * D_MODEL)].astype(F32)


def _gd_xp(halo_ref, p_ref, sl, first_chunk):
    halo = jnp.where(first_chunk, 0.0, halo_ref[:, sl].astype(F32))
    return jnp.concatenate([halo, p_ref[:, sl].astype(F32)], axis=0)


def _stack_all(ref, first=0):
    return jnp.stack([ref[s, :, _head_slice(h + first)] for s in range(ref.shape[0]) for h in range(N_HEADS)])


def _unstack_all(ref, val, first=0):
    for s in range(ref.shape[0]):
        for h in range(N_HEADS):
            ref[s, :, _head_slice(h + first)] = val[s * N_HEADS + h].astype(ref.dtype)


def _gdn_fwd_all(projm, projab, cw, alog, dtb, onw, seqs, name):
    n = projm.shape[0]
    t = n // seqs
    nc = t // CHUNK
    d = D_MODEL
    per_halo = CHUNK // HALO
    nh = seqs * N_HEADS

    def body(p_ref, halo_ref, ab_ref, cw_ref, alog_ref, dtb_ref, onw_ref, o2_ref, st_all_ref, y_ref, dinv_ref,
             st_sc, c_sc, beta_sc, g_sc):
        first_chunk = pl.program_id(0) == 0

        @pl.when(first_chunk)
        def _():
            st_sc[...] = jnp.zeros_like(st_sc)

        for s in range(seqs):
            def conv(at, s=s):
                sl = at()
                y = _gd_conv(_gd_xp(halo_ref.at[s], p_ref.at[s], sl, first_chunk), cw_ref[:, sl])
                y_ref[s, :, sl] = y
                c_sc[s, :, sl] = jax.nn.silu(y)

            _lane_blocks(3 * d, conv)
            beta_sc[s], g_sc[s] = _gd_gates(ab_ref[s, :, 0:N_HEADS], ab_ref[s, :, N_HEADS:2 * N_HEADS],
                                            alog_ref[...], dtb_ref[...])
        st_all_ref[0] = st_sc[...]
        o2, st_sc[...], dinv_ref[0] = _GD_HEADS(
            st_sc[...], _stack_all(c_sc), _stack_all(c_sc, N_HEADS), _stack_all(c_sc, 2 * N_HEADS), _stack_all(beta_sc),
            _stack_all(g_sc), _stack_all(g_sc, N_HEADS), _stack_all(p_ref, 3 * N_HEADS).astype(F32), onw_ref[...])
        _unstack_all(o2_ref, o2)

    rows = lambda c: (0, c, 0)
    const = lambda c: (0, 0)
    per_chunk = lambda c: (c, 0, 0, 0)
    p3 = projm.reshape(seqs, t, 4 * d)
    o2, st_all, conv_y, dinv_all = pl.pallas_call(
        body, name=name, grid=(nc,),
        in_specs=[pl.BlockSpec((seqs, CHUNK, 4 * d), rows),
                  pl.BlockSpec((seqs, HALO, 3 * d), lambda c: (0, jnp.maximum(c * per_halo - 1, 0), 0)),
                  pl.BlockSpec((seqs, CHUNK, AB_PAD), rows),
                  pl.BlockSpec((CONV_K, 3 * d), const), pl.BlockSpec((1, N_HEADS), const),
                  pl.BlockSpec((1, N_HEADS), const), pl.BlockSpec((1, HEAD_DIM), const)],
        out_specs=[pl.BlockSpec((seqs, CHUNK, d), rows), pl.BlockSpec((1, nh, HEAD_DIM, HEAD_DIM), per_chunk),
                   pl.BlockSpec((seqs, CHUNK, 3 * d), rows), pl.BlockSpec((1, nh, CHUNK, CHUNK), per_chunk)],
        out_shape=[jax.ShapeDtypeStruct((seqs, t, d), BF16), jax.ShapeDtypeStruct((nc, nh, HEAD_DIM, HEAD_DIM), F32),
                   jax.ShapeDtypeStruct((seqs, t, 3 * d), F32), jax.ShapeDtypeStruct((nc, nh, CHUNK, CHUNK), F32)],
        scratch_shapes=[pltpu.VMEM((nh, HEAD_DIM, HEAD_DIM), F32), pltpu.VMEM((seqs, CHUNK, 3 * d), F32),
                        pltpu.VMEM((seqs, CHUNK, d), F32), pltpu.VMEM((seqs, CHUNK, 2 * d), F32)],
        compiler_params=_params("arbitrary"),
    )(p3, p3, projab.reshape(seqs, t, AB_PAD), cw, alog, dtb, onw)
    return o2.reshape(n, d), st_all, conv_y, dinv_all


def _gdn_bwd_all(projm, projab, conv_y, cw, alog, dtb, onw, st_all, dinv_all, do2, seqs, name):
    n = projm.shape[0]
    t = n // seqs
    nc = t // CHUNK
    d = D_MODEL
    per_halo = CHUNK // HALO
    nh = seqs * N_HEADS

    def body(p_ref, halo_ref, ab_ref, y_ref, cw_ref, alog_ref, dtb_ref, onw_ref, st_all_ref, dinv_ref, do2_ref,
             dp_ref, dab_ref, dcw_ref, dalog_ref, ddtb_ref, donw_ref,
             dst_sc, dhalo_sc, c_sc, beta_sc, g_sc, dc_sc, dbeta_sc, dg_sc):
        first = pl.program_id(0) == 0
        first_chunk = pl.program_id(0) == nc - 1

        @pl.when(first)
        def _():
            dst_sc[...] = jnp.zeros_like(dst_sc)
            dhalo_sc[...] = jnp.zeros_like(dhalo_sc)

        gates_vjps = []
        for s in range(seqs):
            def act(at, s=s):
                c_sc[s, :, at()] = jax.nn.silu(y_ref[s, :, at()])

            _lane_blocks(3 * d, act)
            (beta_sc[s], g_sc[s]), gates_vjp = jax.vjp(
                _gd_gates, ab_ref[s, :, 0:N_HEADS], ab_ref[s, :, N_HEADS:2 * N_HEADS], alog_ref[...], dtb_ref[...])
            gates_vjps.append(gates_vjp)

        dinv = dinv_ref[0]
        _, vjp = jax.vjp(
            lambda *a: _GD_HEADS_AGAIN(*a, dinv), st_all_ref[0], _stack_all(c_sc), _stack_all(c_sc, N_HEADS),
            _stack_all(c_sc, 2 * N_HEADS), _stack_all(beta_sc), _stack_all(g_sc), _stack_all(g_sc, N_HEADS),
            _stack_all(p_ref, 3 * N_HEADS).astype(F32), onw_ref[...])
        dst_sc[...], dq, dk, dv, dbeta, dg, ddiff, dgate, donw = vjp((_stack_all(do2_ref).astype(F32), dst_sc[...]))
        _unstack_all(dc_sc, dq)
        _unstack_all(dc_sc, dk, N_HEADS)
        _unstack_all(dc_sc, dv, 2 * N_HEADS)
        _unstack_all(dbeta_sc, dbeta)
        _unstack_all(dg_sc, dg)
        _unstack_all(dg_sc, ddiff, N_HEADS)
        _unstack_all(dp_ref, dgate, 3 * N_HEADS)

        dalog, ddtb = None, None
        for s in range(seqs):
            def conv_bwd(at, s=s):
                sl = at()
                dxp, dcw = _gd_conv_bwd(_gd_xp(halo_ref.at[s], p_ref.at[s], sl, first_chunk), cw_ref[:, sl],
                                        y_ref[s, :, sl], dc_sc[s, :, sl])
                dqkv = jnp.concatenate([dxp[HALO:CHUNK], dxp[CHUNK:HALO + CHUNK] + dhalo_sc[s, :, sl]], axis=0)
                dp_ref[s, :, sl] = dqkv.astype(dp_ref.dtype)
                dhalo_sc[s, :, sl] = dxp[0:HALO]

                if s > 0:
                    dcw_ref[:, sl] += dcw
                    return

                @pl.when(first)
                def _():
                    dcw_ref[:, sl] = dcw

                @pl.when(jnp.logical_not(first))
                def _():
                    dcw_ref[:, sl] += dcw

            _lane_blocks(3 * d, conv_bwd)
            da, db, dalog_s, ddtb_s = gates_vjps[s]((dbeta_sc[s], dg_sc[s]))
            dab_ref[s] = jnp.concatenate(
                [da, db, jnp.zeros((CHUNK, AB_PAD - 2 * N_HEADS), F32)], axis=1).astype(dab_ref.dtype)
            dalog = dalog_s if dalog is None else dalog + dalog_s
            ddtb = ddtb_s if ddtb is None else ddtb + ddtb_s

        @pl.when(first)
        def _():
            dalog_ref[...] = dalog
            ddtb_ref[...] = ddtb
            donw_ref[...] = donw

        @pl.when(jnp.logical_not(first))
        def _():
            dalog_ref[...] += dalog
            ddtb_ref[...] += ddtb
            donw_ref[...] += donw

    back = lambda c: nc - 1 - c
    rows = lambda c: (0, back(c), 0)
    const = lambda c: (0, 0)
    per_chunk = lambda c: (back(c), 0, 0, 0)
    small = [pl.BlockSpec((CONV_K, 3 * d), const), pl.BlockSpec((1, N_HEADS), const),
             pl.BlockSpec((1, N_HEADS), const), pl.BlockSpec((1, HEAD_DIM), const)]
    p3 = projm.reshape(seqs, t, 4 * d)
    dp, dab, dcw, dalog, ddtb, donw = pl.pallas_call(
        body, name=name, grid=(nc,),
        in_specs=[pl.BlockSpec((seqs, CHUNK, 4 * d), rows),
                  pl.BlockSpec((seqs, HALO, 3 * d), lambda c: (0, jnp.maximum(back(c) * per_halo - 1, 0), 0)),
                  pl.BlockSpec((seqs, CHUNK, AB_PAD), rows), pl.BlockSpec((seqs, CHUNK, 3 * d), rows)] + small + [
                  pl.BlockSpec((1, nh, HEAD_DIM, HEAD_DIM), per_chunk), pl.BlockSpec((1, nh, CHUNK, CHUNK), per_chunk),
                  pl.BlockSpec((seqs, CHUNK, d), rows)],
        out_specs=[pl.BlockSpec((seqs, CHUNK, 4 * d), rows), pl.BlockSpec((seqs, CHUNK, AB_PAD), rows)] + small,
        out_shape=[jax.ShapeDtypeStruct((seqs, t, 4 * d), BF16), jax.ShapeDtypeStruct((seqs, t, AB_PAD), BF16),
                   jax.ShapeDtypeStruct((CONV_K, 3 * d), F32), jax.ShapeDtypeStruct((1, N_HEADS), F32),
                   jax.ShapeDtypeStruct((1, N_HEADS), F32), jax.ShapeDtypeStruct((1, HEAD_DIM), F32)],
        scratch_shapes=[pltpu.VMEM((nh, HEAD_DIM, HEAD_DIM), F32), pltpu.VMEM((seqs, HALO, 3 * d), F32),
                        pltpu.VMEM((seqs, CHUNK, 3 * d), F32), pltpu.VMEM((seqs, CHUNK, d), F32),
                        pltpu.VMEM((seqs, CHUNK, 2 * d), F32), pltpu.VMEM((seqs, CHUNK, 3 * d), F32),
                        pltpu.VMEM((seqs, CHUNK, d), F32), pltpu.VMEM((seqs, CHUNK, 2 * d), F32)],
        compiler_params=_params("arbitrary"),
    )(p3, p3, projab.reshape(seqs, t, AB_PAD), conv_y, cw, alog, dtb, onw, st_all, dinv_all,
      do2.reshape(seqs, t, d))
    return dp.reshape(n, 4 * d), dab.reshape(n, AB_PAD), dcw, dalog, ddtb, donw


def _hgrn_fwd_all(proj, lb, gw, seqs, name):
    n = proj.shape[0]
    t = n // seqs
    nc = t // CHUNK
    d = D_MODEL
    nh = seqs * N_HEADS
    sums, masks = _hg_level_sums(), _hg_level_masks()

    def body(p_ref, lb_ref, gw_ref, sums_wide_ref, sums_once_ref, masks_ref, o2_ref, o_ref, st_all_ref,
             st_sc, q_sc, k_sc, v_sc, e_sc):
        @pl.when(pl.program_id(0) == 0)
        def _():
            st_sc[...] = jnp.zeros_like(st_sc)

        sums_refs = (sums_wide_ref, sums_once_ref)
        for s in range(seqs):
            _lane_blocks(d, functools.partial(_hg_pre_block, p_ref.at[s], lb_ref, sums_refs, q_sc.at[s], k_sc.at[s],
                                              v_sc.at[s], e_sc.at[s]))
        st_all_ref[0] = st_sc[...]
        for s in range(seqs):
            one, mine = pl.ds(s, 1), pl.ds(s * N_HEADS, N_HEADS)
            o, st_sc[mine] = _HG_HEADS(st_sc[mine], *[_stack_all(r.at[one]) for r in (q_sc, k_sc, v_sc, e_sc)],
                                       masks_ref[...])
            _unstack_all(o_ref.at[one], o)

            def post(rows, s=s):
                gate = p_ref[s, rows, 3 * d:4 * d].astype(F32)
                o2_ref[s, rows, :] = _hg_post(o_ref[s, rows, :], gate, gw_ref[...]).astype(o2_ref.dtype)

            _row_blocks(CHUNK, post)

    rows = lambda c: (0, c, 0)
    vec = pl.BlockSpec((1, d), lambda c: (0, 0))
    act = pl.BlockSpec((seqs, CHUNK, d), rows)
    o2, o, st_all = pl.pallas_call(
        body, name=name, grid=(nc,),
        in_specs=[pl.BlockSpec((seqs, CHUNK, 4 * d), rows), vec, vec]
        + [pl.BlockSpec(m.shape, lambda c: (0, 0)) for m in sums] + [pl.BlockSpec(masks.shape, lambda c: (0, 0, 0))],
        out_specs=[act, act, pl.BlockSpec((1, nh, HEAD_DIM, HEAD_DIM), lambda c: (c, 0, 0, 0))],
        out_shape=[jax.ShapeDtypeStruct((seqs, t, d), BF16), jax.ShapeDtypeStruct((seqs, t, d), F32),
                   jax.ShapeDtypeStruct((nc, nh, HEAD_DIM, HEAD_DIM), F32)],
        scratch_shapes=[pltpu.VMEM((nh, HEAD_DIM, HEAD_DIM), F32)] + [pltpu.VMEM((seqs, CHUNK, d), F32)] * 3
        + [pltpu.VMEM((seqs, sums[0].shape[0], d), F32)],
        compiler_params=_params("arbitrary"),
    )(proj.reshape(seqs, t, 4 * d), lb, gw, *sums, masks)
    return o2.reshape(n, d), o, st_all


def _hgrn_bwd_all(proj, lb, gw, st_all, o, do2, seqs, name):
    n = proj.shape[0]
    t = n // seqs
    nc = t // CHUNK
    d = D_MODEL
    nh = seqs * N_HEADS
    sums, masks = _hg_level_sums(), _hg_level_masks()

    def body(p_ref, lb_ref, gw_ref, sums_wide_ref, sums_once_ref, masks_ref, st_all_ref, o_ref, do2_ref,
             dp_ref, dlb_ref, dgw_ref,
             dst_sc, q_sc, k_sc, v_sc, e_sc, do_sc, dq_sc, dk_sc, dv_sc, de_sc, dgw_sc):
        first = pl.program_id(0) == 0

        @pl.when(first)
        def _():
            dst_sc[...] = jnp.zeros_like(dst_sc)

        sums_refs = (sums_wide_ref, sums_once_ref)
        dgw_sc[...] = jnp.zeros_like(dgw_sc)
        for s in range(seqs):
            _lane_blocks(d, functools.partial(_hg_pre_block, p_ref.at[s], lb_ref, sums_refs, q_sc.at[s], k_sc.at[s],
                                              v_sc.at[s], e_sc.at[s]))

            def post_bwd(rows, s=s):
                _, vjp = jax.vjp(_hg_post, o_ref[s, rows, :], p_ref[s, rows, 3 * d:4 * d].astype(F32), gw_ref[...])
                do_sc[s, rows, :], dgate, dgw = vjp(do2_ref[s, rows, :].astype(F32))
                dp_ref[s, rows, 3 * d:4 * d] = dgate.astype(dp_ref.dtype)
                dgw_sc[...] += dgw

            _row_blocks(CHUNK, post_bwd)

        level_masks = masks_ref[...]
        _, vjp = jax.vjp(lambda *a: _HG_HEADS(*a, level_masks), st_all_ref[0],
                         *[_stack_all(r) for r in (q_sc, k_sc, v_sc, e_sc)])
        grads = vjp((_stack_all(do_sc), dst_sc[...]))
        dst_sc[...] = grads[0]
        for r, val in zip((dq_sc, dk_sc, dv_sc, de_sc), grads[1:]):
            _unstack_all(r, val)

        for s in range(seqs):
            def pre_bwd(at, s=s):
                sl = at()
                level_sums = (sums_wide_ref[...], sums_once_ref[...])
                _, vjp = jax.vjp(lambda qraw, f, lb: _hg_pre(qraw, f, lb, level_sums), p_ref[s, :, sl].astype(F32),
                                 p_ref[s, :, at(d)].astype(F32), lb_ref[:, sl])
                dqraw, df, dlb = vjp((dq_sc[s, :, sl], dk_sc[s, :, sl], de_sc[s, :, sl]))
                dp_ref[s, :, sl] = dqraw.astype(dp_ref.dtype)
                dp_ref[s, :, at(d)] = df.astype(dp_ref.dtype)
                dp_ref[s, :, at(2 * d)] = dv_sc[s, :, sl].astype(dp_ref.dtype)
                if s > 0:
                    dlb_ref[:, sl] += dlb
                    return

                @pl.when(first)
                def _():
                    dlb_ref[:, sl] = dlb

                @pl.when(jnp.logical_not(first))
                def _():
                    dlb_ref[:, sl] += dlb

            _lane_blocks(d, pre_bwd)

        @pl.when(first)
        def _():
            dgw_ref[...] = dgw_sc[...]

        @pl.when(jnp.logical_not(first))
        def _():
            dgw_ref[...] += dgw_sc[...]

    rows = lambda c: (0, nc - 1 - c, 0)
    vec = pl.BlockSpec((1, d), lambda c: (0, 0))
    act = pl.BlockSpec((seqs, CHUNK, d), rows)
    wide = pl.BlockSpec((seqs, CHUNK, 4 * d), rows)
    e_rows = sums[0].shape[0]
    dp, dlb, dgw = pl.pallas_call(
        body, name=name, grid=(nc,),
        in_specs=[wide, vec, vec] + [pl.BlockSpec(m.shape, lambda c: (0, 0)) for m in sums] + [
                  pl.BlockSpec(masks.shape, lambda c: (0, 0, 0)),
                  pl.BlockSpec((1, nh, HEAD_DIM, HEAD_DIM), lambda c: (nc - 1 - c, 0, 0, 0)), act, act],
        out_specs=[wide, vec, vec],
        out_shape=[jax.ShapeDtypeStruct((seqs, t, 4 * d), BF16), jax.ShapeDtypeStruct((1, d), F32),
                   jax.ShapeDtypeStruct((1, d), F32)],
        scratch_shapes=[pltpu.VMEM((nh, HEAD_DIM, HEAD_DIM), F32)]
        + [pltpu.VMEM((seqs, CHUNK, d), F32)] * 3 + [pltpu.VMEM((seqs, e_rows, d), F32)]
        + [pltpu.VMEM((seqs, CHUNK, d), F32)] * 4 + [pltpu.VMEM((seqs, e_rows, d), F32), pltpu.VMEM((1, d), F32)],
        compiler_params=_params("arbitrary"),
    )(proj.reshape(seqs, t, 4 * d), lb, gw, *sums, masks, st_all, o, do2.reshape(seqs, t, d))
    return dp.reshape(n, 4 * d), dlb, dgw


def _adam_update(w, g, m, v):
    b1c = 1.0 - ADAM_B1 ** ADAM_STEP
    b2c = 1.0 - ADAM_B2 ** ADAM_STEP
    m_new = ADAM_B1 * m + (1.0 - ADAM_B1) * g
    v_new = ADAM_B2 * v + (1.0 - ADAM_B2) * (g * g)
    delta = -ADAM_LR * ((m_new / b1c) / (jnp.sqrt(v_new / b2c) + ADAM_EPS) + ADAM_WD * w)
    return delta, m_new, v_new


def _adamw(w, g, m, v, name, tr=256):
    r, c = w.shape
    tr = _tile(r, tr)

    def body(w_ref, g_ref, m_ref, v_ref, d_ref, mo_ref, vo_ref):
        d_ref[...], mo_ref[...], vo_ref[...] = _adam_update(w_ref[...], g_ref[...], m_ref[...], v_ref[...])

    blk = pl.BlockSpec((tr, c), lambda i: (i, 0))
    return pl.pallas_call(
        body, name=name, grid=(r // tr,),
        in_specs=[blk] * 4, out_specs=[blk] * 3,
        out_shape=[jax.ShapeDtypeStruct((r, c), F32)] * 3,
        compiler_params=_params("arbitrary"),
    )(w, g, m, v)


def _adamw_slots(w, slot_bufs, m, v, name, tr=256):
    nl, r, c = w.shape
    tr = _tile(r, tr)

    def body(*refs):
        w_ref = refs[0]
        g_refs = refs[1:1 + nl]
        m_ref, v_ref, go_ref, d_ref, mo_ref, vo_ref = refs[1 + nl:]
        for k in range(nl):
            @pl.when(pl.program_id(0) == k)
            def _(k=k):
                g = g_refs[k][0].astype(F32)
                for s in range(1, N_DEV):
                    g = g + g_refs[k][s].astype(F32)
                go_ref[0] = g

        d_ref[0], mo_ref[0], vo_ref[0] = _adam_update(w_ref[0], go_ref[0], m_ref[0], v_ref[0])

    blk = pl.BlockSpec((1, tr, c), lambda l, i: (l, i, 0))
    g_specs = [pl.BlockSpec((N_DEV, tr, c), lambda l, i, k=k: (0, jnp.where(l == k, i, 0), 0)) for k in range(nl)]
    return pl.pallas_call(
        body, name=name, grid=(nl, r // tr),
        in_specs=[blk] + g_specs + [blk, blk], out_specs=[blk] * 4,
        out_shape=[jax.ShapeDtypeStruct((nl, r, c), F32)] * 4,
        compiler_params=_params("arbitrary", "arbitrary"),
    )(w, *slot_bufs, m, v)


def _adamw_windows(w, lo_bufs, hi_bufs, end_bufs, me, m, v, name, tr=256):
    nl, r, c = w.shape
    wl, wh = lo_bufs[0].shape[2], hi_bufs[0].shape[2]
    width, step = wl + wh, c - wl
    assert step >= 0 and (N_DEV - 1) * step + c <= width and N_DEV == 8
    tr = _tile(r, tr)

    def body(*refs):
        me_ref, w_ref = refs[0], refs[1]
        lo_refs, hi_refs, end_refs = refs[2:2 + nl], refs[2 + nl:2 + 2 * nl], refs[2 + 2 * nl:2 + 3 * nl]
        m_ref, v_ref, go_ref, d_ref, mo_ref, vo_ref = refs[2 + 3 * nl:]
        for k in range(nl):
            @pl.when(pl.program_id(0) == k)
            def _(k=k):
                last = me_ref[0] == N_DEV - 1
                hi_of = lambda s: jnp.where(last, end_refs[k][s].astype(F32), hi_refs[k][s].astype(F32))
                lo, hi = lo_refs[k][0].astype(F32), hi_of(0)
                for s in range(1, N_DEV):
                    lo, hi = lo + lo_refs[k][s].astype(F32), hi + hi_of(s)
                g = jnp.concatenate([lo, hi], axis=1)
                for bit in range(3):
                    moved = pltpu.roll(g, width - (step << bit), axis=1)
                    g = jnp.where((me_ref[0] >> bit) & 1 == 1, moved, g)
                go_ref[0] = g[:, :c]

        d_ref[0], mo_ref[0], vo_ref[0] = _adam_update(w_ref[0], go_ref[0], m_ref[0], v_ref[0])

    blk = pl.BlockSpec((1, tr, c), lambda l, i: (l, i, 0))
    g_specs = [pl.BlockSpec((N_DEV, tr, cols), lambda l, i, k=k: (0, jnp.where(l == k, i, 0), 0))
               for cols in (wl, wh, wh) for k in range(nl)]
    return pl.pallas_call(
        body, name=name, grid=(nl, r // tr),
        in_specs=[pl.BlockSpec(memory_space=pltpu.SMEM), blk] + g_specs + [blk, blk], out_specs=[blk] * 4,
        out_shape=[jax.ShapeDtypeStruct((nl, r, c), F32)] * 4,
        compiler_params=_params("arbitrary", "arbitrary"),
    )(me, w, *lo_bufs, *hi_bufs, *end_bufs, m, v)


def _window(w, me, width, step, name, tr=256):
    r, c = w.shape
    assert (N_DEV - 1) * step + c <= width and N_DEV == 8
    tr = _tile(r, tr)

    def body(me_ref, w_ref, out_ref, wide):
        wide[...] = jnp.zeros_like(wide)
        wide[:, 0:c] = w_ref[...]
        g = wide[...]
        for bit in range(3):
            moved = pltpu.roll(g, step << bit, axis=1)
            g = jnp.where((me_ref[0] >> bit) & 1 == 1, moved, g)
        out_ref[...] = g.astype(out_ref.dtype)

    return pl.pallas_call(
        body, name=name, grid=(r // tr,),
        in_specs=[pl.BlockSpec(memory_space=pltpu.SMEM), pl.BlockSpec((tr, c), lambda i: (i, 0))],
        out_specs=pl.BlockSpec((tr, width), lambda i: (i, 0)),
        out_shape=jax.ShapeDtypeStruct((r, width), BF16),
        scratch_shapes=[pltpu.VMEM((tr, width), F32)],
        compiler_params=_params("arbitrary"),
    )(me, w)


def _unshard_windows(win, c, name, tr=256):
    nd, r, w = win.shape
    wl = w - AB_PAD
    step = c - wl
    assert 0 <= step and nd * step <= AB_PAD
    tr = _tile(r, tr)

    def body(win_ref, main_ref, tail_ref):
        lane = lax.broadcasted_iota(jnp.int32, (tr, AB_PAD), 1)

        def past(s):
            return win_ref[s, :, wl:w].astype(F32)

        for s in range(nd):
            first = win_ref[s, :, 0:AB_PAD].astype(F32)
            if s > 0:
                first = jnp.where(lane < s * step, past(s - 1), first)
            main_ref[:, s * wl:s * wl + AB_PAD] = first.astype(main_ref.dtype)
            main_ref[:, s * wl + AB_PAD:(s + 1) * wl] = win_ref[s, :, AB_PAD:wl]
        tail_ref[...] = jnp.where(lane < nd * step, past(nd - 1), 0.0).astype(tail_ref.dtype)

    return pl.pallas_call(
        body, name=name, grid=(r // tr,),
        in_specs=[pl.BlockSpec((nd, tr, w), lambda i: (0, i, 0))],
        out_specs=[pl.BlockSpec((tr, nd * wl), lambda i: (i, 0)), pl.BlockSpec((tr, AB_PAD), lambda i: (i, 0))],
        out_shape=[jax.ShapeDtypeStruct((r, nd * wl), win.dtype), jax.ShapeDtypeStruct((r, AB_PAD), win.dtype)],
        compiler_params=_params("arbitrary"),
    )(win)


def _mesh_pos():
    return lax.axis_index("x"), lax.axis_index("y"), lax.axis_index("c")


def _flip(pos, p):
    x, y, c = pos
    return ((1 - x) if p & 4 else x, (1 - y) if p & 2 else y, (1 - c) if p & 1 else c)


def _lin(pos):
    return 4 * pos[0] + 2 * pos[1] + pos[2]


_HBM = pl.BlockSpec(memory_space=pltpu.HBM)
_SEM = pl.BlockSpec(memory_space=pltpu.SEMAPHORE)
_DATAFLOW = pltpu.SideEffectType.DATAFLOW_SIDE_EFFECTING


class _Item:
    def __init__(self, src, land_shape, src_pick, dst_pick, peers=tuple(range(1, N_DEV))):
        self.src, self.land_shape, self.src_pick, self.dst_pick = src, land_shape, src_pick, dst_pick
        self.peers = peers


def _distinct(arrays):
    found, where = [], []
    for a in arrays:
        hits = [k for k, f in enumerate(found) if f is a]
        where.append(hits[0] if hits else len(found))
        if not hits:
            found.append(a)
    return found, where


def _remote_copies(items, src, land, send_sem, recv_sem, me, arriving):
    me_i = _lin(me)
    out = []
    for it, s_ref, l_ref in zip(items, src, land):
        for p in it.peers:
            peer = _flip(me, p)
            out.append(pltpu.make_async_remote_copy(
                src_ref=it.src_pick(s_ref, _lin(peer)),
                dst_ref=it.dst_pick(l_ref, _lin(peer) if arriving else me_i),
                send_sem=send_sem, recv_sem=recv_sem, device_id=peer, device_id_type=pl.DeviceIdType.MESH))
    return out


def _own_copies(items, src, land, sem, me):
    me_i = _lin(me)
    return [pltpu.make_async_copy(it.src_pick(s_ref, me_i), it.dst_pick(l_ref, me_i), sem)
            for it, s_ref, l_ref in zip(items, src, land)]


def _exchange_start(groups, name):
    items = [it for g in groups for it in g]
    n, ng = len(items), len(groups)
    first = [sum(len(g) for g in groups[:gi]) for gi in range(ng)]
    arrays, where = _distinct([it.src for it in items])
    nu = len(arrays)

    def body(*refs):
        src, land = [refs[k] for k in where], refs[nu:nu + n]
        send_sems, recv_sems = refs[nu + n:nu + n + ng], refs[nu + n + ng:nu + n + 2 * ng]
        token = refs[2 * (nu + n) + 2 * ng]
        me = _mesh_pos()
        for gi, g in enumerate(groups):
            sl = slice(first[gi], first[gi] + len(g))
            for cp in _remote_copies(g, src[sl], land[sl], send_sems[gi], recv_sems[gi], me, arriving=False):
                cp.start()
            for cp in _own_copies(g, src[sl], land[sl], recv_sems[gi], me):
                cp.start()
        token[...] = jnp.zeros_like(token)

    srcs = [pltpu.with_memory_space_constraint(a, pltpu.HBM) for a in arrays]
    lands = [pltpu.with_memory_space_constraint(lax.empty(it.land_shape, it.src.dtype), pltpu.HBM) for it in items]
    res = pl.pallas_call(
        body, name=name,
        out_shape=([pltpu.SemaphoreType.DMA(())] * (2 * ng)
                   + [pltpu.HBM(a.shape, a.dtype) for a in arrays]
                   + [pltpu.HBM(it.land_shape, it.src.dtype) for it in items]
                   + [jax.ShapeDtypeStruct((8, 128), F32)]),
        in_specs=[_HBM] * (nu + n),
        out_specs=[_SEM] * (2 * ng) + [_HBM] * (nu + n) + [pl.BlockSpec(memory_space=pltpu.VMEM)],
        input_output_aliases={i: 2 * ng + i for i in range(nu + n)},
        compiler_params=pltpu.CompilerParams(has_side_effects=_DATAFLOW),
    )(*srcs, *lands)
    send_sems, recv_sems = res[0:ng], res[ng:2 * ng]
    src_thru, land_thru = [res[2 * ng + k] for k in where], res[2 * ng + nu:2 * ng + nu + n]
    handles = []
    for gi, g in enumerate(groups):
        sl = slice(first[gi], first[gi] + len(g))
        handles.append((g, src_thru[sl], land_thru[sl], send_sems[gi], recv_sems[gi]))
    return handles, res[-1]


def _exchange_wait(handle, after, name):
    items, src_thru, land_thru, send_sem, recv_sem = handle
    k = len(items)
    arrays, where = _distinct(src_thru)
    nu = len(arrays)
    afters = list(after) if isinstance(after, (list, tuple)) else [after]

    def body(*refs):
        src, land = [refs[u] for u in where], refs[nu:nu + k]
        send_ref, recv_ref = refs[nu + k], refs[nu + k + 1]
        for cp in _remote_copies(items, src, land, send_ref, recv_ref, _mesh_pos(), arriving=True):
            cp.wait_send()
            cp.wait_recv()
        for cp in _own_copies(items, src, land, recv_ref, _mesh_pos()):
            cp.wait()

    res = pl.pallas_call(
        body, name=name,
        out_shape=([pltpu.HBM(s.shape, s.dtype) for s in arrays] + [pltpu.HBM(l.shape, l.dtype) for l in land_thru]),
        in_specs=[_HBM] * (nu + k) + [_SEM, _SEM] + [pl.BlockSpec(memory_space=pl.ANY)] * len(afters),
        out_specs=[_HBM] * (nu + k),
        input_output_aliases={i: i for i in range(nu + k)},
        compiler_params=pltpu.CompilerParams(has_side_effects=_DATAFLOW),
    )(*arrays, *land_thru, send_sem, recv_sem, *afters)
    return res[nu:nu + k]


SAME_CORE = (2, 4, 6)
SIBLING = 1


def _pass_on_start(buf, name):
    def body(buf_ref, send_sem, recv_sem, thru_ref):
        me = _mesh_pos()
        for p in SAME_CORE:
            slot = buf_ref.at[_lin(_flip(me, p))]
            pltpu.make_async_remote_copy(src_ref=slot, dst_ref=slot, send_sem=send_sem, recv_sem=recv_sem,
                                         device_id=_flip(me, SIBLING), device_id_type=pl.DeviceIdType.MESH).start()

    return pl.pallas_call(
        body, name=name,
        out_shape=[pltpu.SemaphoreType.DMA(()), pltpu.SemaphoreType.DMA(()), pltpu.HBM(buf.shape, buf.dtype)],
        in_specs=[_HBM], out_specs=[_SEM, _SEM, _HBM], input_output_aliases={0: 2},
        compiler_params=pltpu.CompilerParams(has_side_effects=_DATAFLOW),
    )(pltpu.with_memory_space_constraint(buf, pltpu.HBM))


def _pass_on_wait(handle, name):
    send_sem, recv_sem, thru = handle

    def body(buf_ref, send_ref, recv_ref, out_ref):
        me = _mesh_pos()
        sibling = _flip(me, SIBLING)
        for p in SAME_CORE:
            mine, theirs = buf_ref.at[_lin(_flip(me, p))], buf_ref.at[_lin(_flip(sibling, p))]
            cp = pltpu.make_async_remote_copy(src_ref=mine, dst_ref=theirs, send_sem=send_ref, recv_sem=recv_ref,
                                              device_id=sibling, device_id_type=pl.DeviceIdType.MESH)
            cp.wait_send()
            cp.wait_recv()

    return pl.pallas_call(
        body, name=name, out_shape=pltpu.HBM(thru.shape, thru.dtype),
        in_specs=[_HBM, _SEM, _SEM], out_specs=_HBM, input_output_aliases={0: 0},
        compiler_params=pltpu.CompilerParams(has_side_effects=_DATAFLOW),
    )(thru, send_sem, recv_sem)


def _whole(ref, i):
    return ref


def _slot(ref, i):
    return ref.at[i]


def _rows_of(r):
    return lambda ref, i: ref.at[pl.ds(pl.multiple_of(i * r, r), r), :]


def _cols_of(c):
    return lambda ref, i: ref.at[:, pl.ds(pl.multiple_of(i * c, c), c)]


def _all_reduce_small(buf, after, name):
    r, c = buf.shape

    def body(src_ref, after_ref, out_ref, all_ref, send_sems, recv_sems):
        me = _mesh_pos()
        me_i = _lin(me)
        all_ref[me_i] = src_ref[...]
        for p in range(1, N_DEV):
            peer = _flip(me, p)
            pltpu.make_async_remote_copy(
                src_ref=src_ref, dst_ref=all_ref.at[me_i], send_sem=send_sems.at[p - 1], recv_sem=recv_sems.at[p - 1],
                device_id=peer, device_id_type=pl.DeviceIdType.MESH).start()
        for p in range(1, N_DEV):
            peer = _flip(me, p)
            cp = pltpu.make_async_remote_copy(
                src_ref=src_ref, dst_ref=all_ref.at[_lin(peer)], send_sem=send_sems.at[p - 1],
                recv_sem=recv_sems.at[p - 1], device_id=peer, device_id_type=pl.DeviceIdType.MESH)
            cp.wait_recv()
            cp.wait_send()
        acc = all_ref[0]
        for s in range(1, N_DEV):
            acc = acc + all_ref[s]
        out_ref[...] = acc

    vm = pl.BlockSpec(memory_space=pltpu.VMEM)
    return pl.pallas_call(
        body, name=name, in_specs=[vm, pl.BlockSpec(memory_space=pl.ANY)], out_specs=vm,
        out_shape=jax.ShapeDtypeStruct((r, c), F32),
        scratch_shapes=[pltpu.VMEM((N_DEV, r, c), F32), pltpu.SemaphoreType.DMA((N_DEV - 1,)),
                        pltpu.SemaphoreType.DMA((N_DEV - 1,))],
        compiler_params=pltpu.CompilerParams(has_side_effects=True),
    )(buf, after)


def _unshard_cols(g):
    s, l, r, c = g.shape
    return jnp.transpose(g, (1, 2, 0, 3)).reshape(l, r, s * c)


def kernel(x, gdn_w_in, gdn_conv, gdn_a_log, gdn_dt_bias, gdn_onorm, gdn_w_out, hgrn_w_in, hgrn_lb_logits, hgrn_gnorm, hgrn_w_out, norm_mix, norm_mlp, mlp_w_up, mlp_w_down, norm_final, loss_target, m_gdn_w_in, m_gdn_conv, m_gdn_a_log, m_gdn_dt_bias, m_gdn_onorm, m_gdn_w_out, m_hgrn_w_in, m_hgrn_lb_logits, m_hgrn_gnorm, m_hgrn_w_out, m_norm_mix, m_norm_mlp, m_mlp_w_up, m_mlp_w_down, m_norm_final, v_gdn_w_in, v_gdn_conv, v_gdn_a_log, v_gdn_dt_bias, v_gdn_onorm, v_gdn_w_out, v_hgrn_w_in, v_hgrn_lb_logits, v_hgrn_gnorm, v_hgrn_w_out, v_norm_mix, v_norm_mlp, v_mlp_w_up, v_mlp_w_down, v_norm_final):
    seqs, seq_len, d = x.shape
    n = seqs * seq_len
    me_i = _lin(_mesh_pos())
    x2 = x.reshape(n, d)
    target = loss_target.reshape(n, d)
    n_gdn, n_hgrn = gdn_w_in.shape[0], hgrn_w_in.shape[0]

    r_out, r_down = gdn_w_out.shape[1], mlp_w_down.shape[1]
    c_gin, c_hin, c_up = gdn_w_in.shape[2], hgrn_w_in.shape[2], mlp_w_up.shape[2]

    def gathered(w, pick, land_shape, **kw):
        return _Item(w.astype(BF16), land_shape, _whole, pick, **kw)

    wl = GDN_MAIN // N_DEV

    me_1 = me_i.astype(jnp.int32).reshape(1)

    def next_tile(ref, i):
        return ref.at[:, pl.ds(pl.multiple_of(jnp.minimum(i + 1, N_DEV - 1) * wl, AB_PAD), AB_PAD)]

    groups = [[_Item(gdn_conv, (N_DEV,) + gdn_conv.shape, _whole, _slot),
               _Item(hgrn_gnorm, (N_DEV,) + hgrn_gnorm.shape, _whole, _slot)]]
    for i in range(DEPTH):
        j = i // 2
        if i % 2 == 0:
            direct = (SIBLING,) + SAME_CORE if i == 0 else tuple(range(1, N_DEV))
            win = _window(gdn_w_in[j], me_1, wl + AB_PAD, c_gin - wl, f"window_in_{i}")
            groups += [[_Item(win, (N_DEV, d, wl + AB_PAD), _whole, _slot, peers=direct)],
                       [gathered(gdn_w_out[j], _rows_of(r_out), (N_DEV * r_out, d))]]
        else:
            groups += [[gathered(hgrn_w_in[j], _cols_of(c_hin), (d, N_DEV * c_hin))],
                       [gathered(hgrn_w_out[j], _rows_of(r_out), (N_DEV * r_out, d))]]
        groups += [[gathered(mlp_w_up[i], _cols_of(c_up), (d, N_DEV * c_up))],
                   [gathered(mlp_w_down[i], _rows_of(r_down), (N_DEV * r_down, d))]]
    gather_handles, token = _exchange_start(groups, "gather_start")
    lbs = _lb_fwd(hgrn_lb_logits + token[0:1, 0:1], "lb_fwd")

    def arrived(k, after, name):
        return _exchange_wait(gather_handles[k], after, "gather_wait_" + name)

    saved = []
    w_in, w_ab, w_out, w_up, w_down = ([None] * DEPTH for _ in range(5))
    h = x2
    for i in range(DEPTH):
        j = i // 2
        if i == 0:
            g_conv, g_gnorm = arrived(0, h, "small")
            conv_full = _unshard_cols(g_conv)
            gnorm_full = jnp.transpose(g_gnorm, (1, 0, 2)).reshape(n_hgrn, d)
        if i == 0:
            y = _rms_fwd(h, norm_mix[0:1] + token[0:1, 0:1], "rms_mix_0")
        (w_in[i],) = arrived(1 + 4 * i, [y, lbs, conv_full, gnorm_full] if i == 0 else y, f"in_{i}")
        if i == 0:
            w_in[i] = _pass_on_wait(_pass_on_start(w_in[i], "pass_on_start_in_0"), "pass_on_wait_in_0")
        if i % 2 == 0:
            w_in[i], w_ab[i] = _unshard_windows(w_in[i], c_gin, f"gdn_w_in_{i}")
            projm = _mm(y, w_in[i], "nn", [BF16], f"gdn_proj_{i}")
            projab = _mm(y, w_ab[i], "nn", [F32], f"gdn_proj_ab_{i}")
            o2, st_all, conv_y, dinv_all = _gdn_fwd_all(projm, projab, conv_full[j], gdn_a_log[j:j + 1],
                                                    gdn_dt_bias[j:j + 1], gdn_onorm[j:j + 1], seqs, f"gdn_fwd_{i}")
            mix = (projm, projab, conv_y, st_all, dinv_all)
        else:
            proj = _mm(y, w_in[i], "nn", [BF16], f"hgrn_proj_{i}")
            o2, o_raw, st_all = _hgrn_fwd_all(proj, lbs[i:i + 1], gnorm_full[j:j + 1], seqs, f"hgrn_fwd_{i}")
            mix = (proj, o_raw, st_all)
        (w_out[i],) = arrived(2 + 4 * i, o2, f"out_{i}")
        h1, y2 = _mm_rows(o2, w_out[i], "nn", [F32, BF16], f"mix_out_{i}", epilogue=_ep_residual_norm, extras=(h,),
                     vectors=(norm_mlp[i:i + 1],))
        (w_up[i],) = arrived(3 + 4 * i, y2, f"up_{i}")
        u, act = _mm(y2, w_up[i], "nn", [BF16, BF16], f"mlp_up_{i}",
                     epilogue=lambda acc: (acc, jnp.square(jnp.maximum(acc, 0.0))))
        (w_down[i],) = arrived(4 + 4 * i, act, f"down_{i}")
        saved.append((h, y, mix, o2, h1, y2, u, act))
        if i + 1 < DEPTH:
            h, y = _mm_rows(act, w_down[i], "nn", [F32, BF16], f"mlp_down_{i}", epilogue=_ep_residual_norm, extras=(h1,),
                       vectors=(norm_mix[i + 1:i + 2],))
        else:
            h = _mm(act, w_down[i], "nn", [F32], f"mlp_down_{i}", epilogue=lambda acc, res: (res + acc,),
                    extras=(h1,))

    dh, dh_b, d_nf, sq = _loss_head(h, norm_final.reshape(1, d), target, "loss_head")

    d_nmix, d_nmlp = [None] * DEPTH, [None] * DEPTH
    d_conv, d_alog, d_dtb, d_onorm = [None] * n_gdn, [None] * n_gdn, [None] * n_gdn, [None] * n_gdn
    d_lb = [jnp.zeros((1, d), F32)] * DEPTH
    d_gnorm = [None] * n_hgrn
    mlp_handles, mix_handles = [None] * DEPTH, [None] * DEPTH
    token = None
    for i in reversed(range(DEPTH)):
        j = i // 2
        h_in, y, mix, o2, h1, y2, u, act = saved[i]
        g_down = _mm(act, dh_b, "tn", [BF16], f"g_down_{i}", after=token)
        du = _mm(dh_b, w_down[i], "nt", [BF16], f"d_u_{i}",
                 epilogue=lambda acc, uu: (acc * (2.0 * jnp.maximum(uu.astype(F32), 0.0)),), extras=(u,))
        g_up = _mm(y2, du, "tn", [BF16], f"g_up_{i}")
        mlp_handles[i], token = _exchange_start(
            [[_Item(g_down, (N_DEV, r_down, d), _rows_of(r_down), _slot)],
             [_Item(g_up, (N_DEV, d, c_up), _cols_of(c_up), _slot)]], f"scatter_start_mlp_{i}")
        dh1, dh1_b, d_nmlp[i] = _mm_rows(du, w_up[i], "nt", [F32, BF16], f"d_y2_{i}", epilogue=_ep_norm_bwd,
                                     extras=(h1, dh), vectors=(norm_mlp[i:i + 1],), n_sums=1, after=token)
        g_out = _mm(o2, dh1_b, "tn", [BF16], f"g_out_{i}")
        do2 = _mm(dh1_b, w_out[i], "nt", [BF16], f"d_o2_{i}")
        if i % 2 == 0:
            projm, projab, conv_y, st_all, dinv_all = mix
            dpm, dpab, d_conv[j], d_alog[j], d_dtb[j], d_onorm[j] = _gdn_bwd_all(
                projm, projab, conv_y, conv_full[j], gdn_a_log[j:j + 1], gdn_dt_bias[j:j + 1], gdn_onorm[j:j + 1],
                st_all, dinv_all, do2, seqs, f"gdn_bwd_{i}")
            g_main = _mm(y, dpm, "tn", [BF16], f"g_in_{i}")
            g_ab = _mm(y, dpab, "tn", [BF16], f"g_in_ab_{i}")
            in_items = [_Item(g_main, (N_DEV, d, wl), _cols_of(wl), _slot),
                        _Item(g_main, (N_DEV, d, AB_PAD), next_tile, _slot),
                        _Item(g_ab, (N_DEV, d, AB_PAD), _whole, _slot)]
            dy_ab = _mm(dpab, w_ab[i], "nt", [F32], f"d_y_ab_{i}")
            dp, dy_extras = dpm, (dy_ab, h_in, dh1)
            dy_epilogue = lambda acc, e, xx, dres, w: _ep_norm_bwd(acc + e, xx, dres, w)
        else:
            proj, o_raw, st_all = mix
            dp, d_lb[i], d_gnorm[j] = _hgrn_bwd_all(proj, lbs[i:i + 1], gnorm_full[j:j + 1], st_all, o_raw, do2,
                                               seqs, f"hgrn_bwd_{i}")
            g_in = _mm(y, dp, "tn", [BF16], f"g_in_{i}")
            in_items = [_Item(g_in, (N_DEV, d, c_hin), _cols_of(c_hin), _slot)]
            dy_extras, dy_epilogue = (h_in, dh1), _ep_norm_bwd
        mix_handles[i], token = _exchange_start(
            [[_Item(g_out, (N_DEV, r_out, d), _rows_of(r_out), _slot)], in_items], f"scatter_start_mix_{i}")
        dh, dh_b, d_nmix[i] = _mm_rows(dp, w_in[i], "nt", [F32, BF16], f"d_y_{i}", epilogue=dy_epilogue, extras=dy_extras,
                                  vectors=(norm_mix[i:i + 1],), n_sums=1, after=token)
        token = None
    grad_x = dh.reshape(x.shape)

    def landed(handles, k, layers, after, name):
        return [_exchange_wait(handles[i][k], after, f"scatter_wait_{name}_{i}")[0] for i in layers]

    every, even, odd = range(DEPTH), range(0, DEPTH, 2), range(1, DEPTH, 2)
    upd = {}
    upd["mlp_w_down"] = _adamw_slots(mlp_w_down, landed(mlp_handles, 0, every, dh, "down"), m_mlp_w_down,
                                     v_mlp_w_down, "adamw_mlp_w_down")
    upd["mlp_w_up"] = _adamw_slots(mlp_w_up, landed(mlp_handles, 1, every, upd["mlp_w_down"][1], "up"), m_mlp_w_up,
                                   v_mlp_w_up, "adamw_mlp_w_up")
    upd["hgrn_w_out"] = _adamw_slots(hgrn_w_out, landed(mix_handles, 0, odd, upd["mlp_w_up"][1], "out"),
                                     m_hgrn_w_out, v_hgrn_w_out, "adamw_hgrn_w_out")
    upd["hgrn_w_in"] = _adamw_slots(hgrn_w_in, landed(mix_handles, 1, odd, upd["hgrn_w_out"][1], "in"), m_hgrn_w_in,
                                    v_hgrn_w_in, "adamw_hgrn_w_in")

    dlb_rows = jnp.concatenate(d_lb, axis=0)
    tail = jnp.concatenate(
        [jnp.concatenate(d_onorm, axis=1), jnp.concatenate(d_alog, axis=1), jnp.concatenate(d_dtb, axis=1)], axis=1)
    tail = jnp.pad(tail, ((0, 0), (0, d - tail.shape[1])))
    conv_rows = jnp.stack(d_conv).reshape(-1, d)
    packed = jnp.concatenate(
        [jnp.concatenate(d_nmix, axis=0), jnp.concatenate(d_nmlp, axis=0), d_nf, sq, dlb_rows,
         jnp.concatenate(d_gnorm, axis=0), tail, conv_rows], axis=0)
    pad_rows = (-packed.shape[0]) % 8
    packed = jnp.pad(packed, ((0, pad_rows), (0, 0)))
    tot = _all_reduce_small(packed, upd["hgrn_w_in"][1], "reduce_small")

    upd["gdn_w_out"] = _adamw_slots(gdn_w_out, landed(mix_handles, 0, even, tot, "out"),
                                    m_gdn_w_out, v_gdn_w_out, "adamw_gdn_w_out")
    windows = [_exchange_wait(mix_handles[i][1], upd["gdn_w_out"][1], f"scatter_wait_in_{i}") for i in even]
    upd["gdn_w_in"] = _adamw_windows(gdn_w_in, *([win[k] for win in windows] for k in range(3)),
                                     me_1, m_gdn_w_in, v_gdn_w_in, "adamw_gdn_w_in")

    def update(name, w, g, m, v):
        shape = w.shape
        c = shape[-1]
        res = _adamw(w.reshape(-1, c), g.reshape(-1, c), m.reshape(-1, c), v.reshape(-1, c), "adamw_" + name)
        return [g.reshape(shape)] + [o.reshape(shape) for o in res]

    r0 = 0
    g_nmix = tot[r0:r0 + DEPTH]; r0 += DEPTH
    g_nmlp = tot[r0:r0 + DEPTH]; r0 += DEPTH
    g_nf = tot[r0]; r0 += 1
    loss = tot[r0, 0]; r0 += 1
    g_lb = _lb_bwd(hgrn_lb_logits, tot[r0:r0 + DEPTH], "lb_bwd"); r0 += DEPTH
    g_gnorm_full = tot[r0:r0 + n_hgrn]; r0 += n_hgrn
    t_row = tot[r0]; r0 += 1
    g_conv_full = tot[r0:r0 + n_gdn * CONV_K * 3].reshape(n_gdn, CONV_K, 3 * d)
    g_onorm = t_row[0:n_gdn * HEAD_DIM].reshape(n_gdn, HEAD_DIM)
    o1 = n_gdn * HEAD_DIM
    g_alog = t_row[o1:o1 + n_gdn * N_HEADS].reshape(n_gdn, N_HEADS)
    g_dtb = t_row[o1 + n_gdn * N_HEADS:o1 + 2 * n_gdn * N_HEADS].reshape(n_gdn, N_HEADS)
    c_gn, c_cv = hgrn_gnorm.shape[1], gdn_conv.shape[2]
    g_gnorm = lax.dynamic_slice_in_dim(g_gnorm_full, me_i * c_gn, c_gn, axis=1)
    g_conv = lax.dynamic_slice_in_dim(g_conv_full, me_i * c_cv, c_cv, axis=2)

    upd["gdn_conv"] = update("gdn_conv", gdn_conv, g_conv, m_gdn_conv, v_gdn_conv)
    upd["gdn_a_log"] = update("gdn_a_log", gdn_a_log, g_alog, m_gdn_a_log, v_gdn_a_log)
    upd["gdn_dt_bias"] = update("gdn_dt_bias", gdn_dt_bias, g_dtb, m_gdn_dt_bias, v_gdn_dt_bias)
    upd["gdn_onorm"] = update("gdn_onorm", gdn_onorm, g_onorm, m_gdn_onorm, v_gdn_onorm)
    upd["hgrn_lb_logits"] = update("hgrn_lb_logits", hgrn_lb_logits, g_lb, m_hgrn_lb_logits, v_hgrn_lb_logits)
    upd["hgrn_gnorm"] = update("hgrn_gnorm", hgrn_gnorm, g_gnorm, m_hgrn_gnorm, v_hgrn_gnorm)
    upd["norm_mix"] = update("norm_mix", norm_mix, g_nmix, m_norm_mix, v_norm_mix)
    upd["norm_mlp"] = update("norm_mlp", norm_mlp, g_nmlp, m_norm_mlp, v_norm_mlp)
    upd["norm_final"] = update("norm_final", norm_final, g_nf, m_norm_final, v_norm_final)

    order = ["gdn_w_in", "gdn_conv", "gdn_a_log", "gdn_dt_bias", "gdn_onorm", "gdn_w_out", "hgrn_w_in",
             "hgrn_lb_logits", "hgrn_gnorm", "hgrn_w_out", "norm_mix", "norm_mlp", "mlp_w_up", "mlp_w_down",
             "norm_final"]
    outs = [loss, grad_x]
    for k in range(4):
        outs += [upd[name][k] for name in order]
    return tuple(outs)
```

```python
import functools

import numpy as np
import jax
import jax.numpy as jnp
from jax import lax
from jax.experimental import pallas as pl
from jax.experimental.pallas import tpu as pltpu

F32 = jnp.float32
BF16 = jnp.bfloat16

D_MODEL = 1024
N_HEADS = 8
HEAD_DIM = 128
CHUNK = 64
CONV_K = 4
HALO = 16
EPS = 1e-6
DEPTH = 4
N_DEV = 8
GDN_MAIN = 4 * D_MODEL
AB_PAD = 128
LANE_BLOCK = 256
ROW_BLOCK = 16
BLOCK_UNROLL = 4

ADAM_LR = 0.001
ADAM_B1 = 0.9
ADAM_B2 = 0.999
ADAM_EPS = 1e-08
ADAM_WD = 0.01
ADAM_STEP = 10

VMEM_LIMIT = 56 * 1024 * 1024
MM_TILE = 1024
MM_ROWS_MAX = 2048
MM_VMEM_BUDGET = 40 * 1024 * 1024
MM_ROWS_TILE = 512
_DIMS = {
    "nn": (((1,), (0,)), ((), ())),
    "nt": (((1,), (1,)), ((), ())),
    "tn": (((0,), (0,)), ((), ())),
}


def _parts(x, n):
    if n == 1 and x.dtype == BF16:
        return [x]
    out = []
    r = x.astype(F32)
    for i in range(n):
        p = r.astype(BF16)
        out.append(p)
        if i + 1 < n:
            r = r - p.astype(F32)
    return out


def _dot_raw(a, b, mode, na, nb):
    ap, bp = _parts(a, na), _parts(b, nb)
    nmax = max(na, nb)
    pairs = [(i, j) for i in range(na) for j in range(nb) if i + j < nmax]
    ka = 0 if mode == "tn" else 1
    kb = 1 if mode == "nt" else 0
    xa = ap[0] if len(pairs) == 1 else jnp.concatenate([ap[i] for i, _ in pairs], axis=ka)
    xb = bp[0] if len(pairs) == 1 else jnp.concatenate([bp[j] for _, j in pairs], axis=kb)
    return lax.dot_general(xa, xb, _DIMS[mode], preferred_element_type=F32)


@functools.partial(jax.custom_vjp, nondiff_argnums=(2, 3, 4))
def _dot(a, b, mode, na, nb):
    return _dot_raw(a, b, mode, na, nb)


def _dot_fwd(a, b, mode, na, nb):
    return _dot_raw(a, b, mode, na, nb), (a, b)


def _dot_bwd(mode, na, nb, res, ct):
    a, b = res
    if mode == "nn":
        da = _dot_raw(ct, b, "nt", 1, 1)
        db = _dot_raw(a, ct, "tn", 1, 1)
    elif mode == "nt":
        da = _dot_raw(ct, b, "nn", 1, 1)
        db = _dot_raw(ct, a, "tn", 1, 1)
    else:
        da = _dot_raw(b, ct, "nt", 1, 1)
        db = _dot_raw(a, ct, "nn", 1, 1)
    return da.astype(a.dtype), db.astype(b.dtype)


_dot.defvjp(_dot_fwd, _dot_bwd)


N_EXACT = 3


@jax.custom_vjp
def _dot01(x, m_wide, m):
    return lax.dot_general(m_wide, jnp.concatenate(_parts(x, N_EXACT), axis=0), _DIMS["nn"], preferred_element_type=F32)


def _dot01_fwd(x, m_wide, m):
    return _dot01(x, m_wide, m), (m_wide, m)


def _dot01_bwd(res, ct):
    m_wide, m = res
    dx = lax.dot_general(m, ct.astype(BF16), _DIMS["tn"], preferred_element_type=F32)
    return dx, jnp.zeros_like(m_wide), jnp.zeros_like(m)


_dot01.defvjp(_dot01_fwd, _dot01_bwd)


def _thrice(m):
    return jnp.concatenate([m] * N_EXACT, axis=1).astype(BF16), m.astype(BF16)


def _iota2(shape, dim):
    return lax.broadcasted_iota(jnp.int32, shape, dim)


def _tril_f32(n):
    return (_iota2((n, n), 0) >= _iota2((n, n), 1)).astype(F32)


def _below_block(n, b):
    ri, ci = _iota2((n, n), 0) // b, _iota2((n, n), 1) // b
    return (ri == ci + 1) & (ri % 2 == 1)


def _half_inverses(L):
    n = L.shape[0]
    eye = (_iota2((n, n), 0) == _iota2((n, n), 1)).astype(F32)
    d = eye - jnp.where(_below_block(n, 1), L, 0.0)
    b = 2
    while 2 * b < n:
        e = jnp.where(_below_block(n, b), L, 0.0)
        d = d - _dot_raw(d, _dot_raw(e, d, "nn", 2, 2), "nn", 2, 2)
        b *= 2
    return d, jnp.where(_below_block(n, b), L, 0.0)


def _solve_with(d, e, rhs):
    y = _dot_raw(d, rhs, "nn", 2, 2)
    return y - _dot_raw(d, _dot_raw(e, y, "nn", 2, 2), "nn", 2, 2)


@jax.custom_vjp
def _solve_unit_lower(L, rhs, d):
    n = L.shape[0]
    return _solve_with(d, jnp.where(_below_block(n, n // 2), L, 0.0), rhs)


def _solve_fwd(L, rhs, d):
    n = L.shape[0]
    e = jnp.where(_below_block(n, n // 2), L, 0.0)
    sol = _solve_with(d, e, rhs)
    return sol, (d, e, sol)


def _solve_bwd(res, ct):
    d, e, sol = res
    y = _dot_raw(d, ct - _dot_raw(e, _dot_raw(d, ct, "tn", 2, 2), "tn", 2, 2), "tn", 2, 2)
    return -_dot_raw(y, sol, "nt", 2, 2), y, jnp.zeros_like(d)


_solve_unit_lower.defvjp(_solve_fwd, _solve_bwd)


def _softplus(x):
    return jnp.maximum(x, 0.0) + jnp.log1p(jnp.exp(-jnp.abs(x)))


def _rms(x, w):
    return x * lax.rsqrt(jnp.mean(x * x, axis=-1, keepdims=True) + EPS) * w


HG_LEVELS = (32, 16, 8, 4, 2, 1)


def _hg_level_sums():
    i = np.arange(CHUNK)[:, None]
    m = np.arange(CHUNK)[None, :]
    to_row = [(m <= i) & (m // b == i // b) for b in HG_LEVELS]
    to_col = [(m > i) & (m // b == i // b) for b in HG_LEVELS if b > 1]
    return _thrice(jnp.asarray(np.concatenate(to_row + to_col + [m <= i]), F32))


def _hg_level_masks():
    i = np.arange(CHUNK)[:, None]
    j = np.arange(CHUNK)[None, :]
    return jnp.asarray(np.stack([(i // b == j // b + 1) & ((i // b) % 2 == 1) for b in HG_LEVELS]), F32)


def _hg_pre(qraw, f, lb, sums):
    g = jnp.log(lb + (1.0 - lb) * jax.nn.sigmoid(f))
    k = (1.0 - lb) * jax.nn.sigmoid(-f)
    q = jax.nn.silu(qraw) * (HEAD_DIM ** -0.5)
    return q, k, _dot01(g, *sums)


def _hg_head(st, q, k, v, e, masks):
    nl = len(HG_LEVELS)
    eye = (_iota2((CHUNK, CHUNK), 0) == _iota2((CHUNK, CHUNK), 1)).astype(F32)
    a = eye * jnp.sum(q * k, axis=-1, keepdims=True)
    for l, b in enumerate(HG_LEVELS):
        rows = q * jnp.exp(e[l * CHUNK:(l + 1) * CHUNK])
        cols = k * jnp.exp(e[(nl + l) * CHUNK:(nl + l + 1) * CHUNK]) if b > 1 else k
        a = a + masks[l] * _dot(rows, cols, "nt", 1, 1)
    gc = e[(2 * nl - 1) * CHUNK:2 * nl * CHUNK]
    o = _dot(a, v, "nn", 1, 1) + _dot(q * jnp.exp(gc), st, "nt", 1, 1)
    g_last = gc[CHUNK - 1:CHUNK]
    st_new = st * jnp.exp(g_last) + _dot(v, k * jnp.exp(g_last - gc), "tn", 1, 1)
    return o, st_new


_HG_HEADS = jax.vmap(_hg_head, in_axes=(0, 0, 0, 0, 0, None))


def _hg_post(o, gate, gw):
    return _rms(o, gw) * jax.nn.silu(gate)


def _gd_conv(xp, cw):
    off = HALO - (CONV_K - 1)
    y = cw[0:1] * xp[off:off + CHUNK]
    for kk in range(1, CONV_K):
        y = y + cw[kk:kk + 1] * xp[off + kk:off + kk + CHUNK]
    return y


def _gd_conv_bwd(xp, cw, y, dc):
    off = HALO - (CONV_K - 1)
    sig = jax.nn.sigmoid(y)
    dy = dc * (sig * (1.0 + y * (1.0 - sig)))
    dxp, dcw = None, []
    for kk in range(CONV_K):
        moved = jnp.pad(dy, ((off + kk, HALO - off - kk), (0, 0)))
        term = cw[kk:kk + 1] * moved
        dxp = term if dxp is None else dxp + term
        dcw.append(jnp.sum(xp * moved, axis=0, keepdims=True))
    return dxp, jnp.concatenate(dcw, axis=0)


def _gd_gates(a, b, alog, dtb):
    beta = jax.nn.sigmoid(b)
    g = -jnp.exp(alog) * _softplus(a + dtb)
    expand = (_iota2((N_HEADS, D_MODEL), 1) // HEAD_DIM == _iota2((N_HEADS, D_MODEL), 0)).astype(F32)
    g_x = _dot(g, expand, "nn", 3, 1)
    after = (_iota2((CHUNK, D_MODEL), 0) > _iota2((CHUNK, D_MODEL), 1) % HEAD_DIM).astype(F32)
    sums = _dot01(jnp.concatenate([g_x, g_x * after], axis=1), *_thrice(_tril_f32(CHUNK)))
    return _dot(beta, expand, "nn", 3, 1), sums


def _gd_head(st, q, k, v, beta, gc, diff, gate, onw, dinv=None):
    q = q * lax.rsqrt(jnp.sum(q * q, axis=-1, keepdims=True) + EPS) * (HEAD_DIM ** -0.5)
    k = k * lax.rsqrt(jnp.sum(k * k, axis=-1, keepdims=True) + EPS)
    ri = _iota2((CHUNK, CHUNK), 0)
    ci = _iota2((CHUNK, CHUNK), 1)
    decay = jnp.exp(jnp.where(ri >= ci, diff[:, 0:CHUNK], -jnp.inf))
    kb = k * beta
    egc = jnp.exp(gc)
    L = jnp.where(ri > ci, _dot(kb, k, "nt", 1, 1) * decay, 0.0)
    made = dinv is None
    if made:
        dinv = _half_inverses(L)[0]
    sol = _solve_unit_lower(L, jnp.concatenate([v * beta, kb * egc], axis=1), dinv)
    u = sol[:, 0:HEAD_DIM]
    w = sol[:, HEAD_DIM:2 * HEAD_DIM]
    a_qk = jnp.where(ri >= ci, _dot(q, k, "nt", 1, 1) * decay, 0.0)
    g_last = gc[CHUNK - 1:CHUNK]
    v_new = u - _dot(w, st, "nt", 1, 1)
    o = _dot(q * egc, st, "nt", 1, 1) + _dot(a_qk, v_new, "nn", 1, 1)
    st_new = st * jnp.exp(g_last) + _dot(v_new, k * jnp.exp(g_last - gc), "tn", 1, 1)
    out = (_rms(o, onw) * jax.nn.silu(gate), st_new)
    return out + (dinv,) if made else out


def _params(*sem):
    return pltpu.CompilerParams(dimension_semantics=sem, vmem_limit_bytes=VMEM_LIMIT)


def _tile(n, pref):
    t = min(n, pref)
    assert n % t == 0, (n, pref)
    return t


def _mm_tiles(m, n, k, a_size, b_size, tile_sizes):
    tn = _tile(n, MM_TILE)

    def need(tm, tk):
        acc = 4 * tm * tn * (2 if tk < k else 1)
        return 2 * (tm * tk * a_size + tk * tn * b_size + tm * tn * sum(tile_sizes)) + acc

    tk = k
    while True:
        tm = _tile(m, MM_ROWS_MAX)
        while tm > 256 and need(tm, tk) > MM_VMEM_BUDGET:
            tm //= 2
        if need(tm, tk) <= MM_VMEM_BUDGET or tk <= 512:
            return tm, tn, tk
        tk //= 2


def _mm(a, b, mode, out_dtypes, name, epilogue=None, extras=(), after=None):
    if mode == "nn":
        (m, k), (k2, n) = a.shape, b.shape
    elif mode == "nt":
        (m, k), (n, k2) = a.shape, b.shape
    else:
        (k, m), (k2, n) = a.shape, b.shape
    assert k == k2, (a.shape, b.shape, mode)
    tm, tn, tk = _mm_tiles(m, n, k, a.dtype.itemsize, b.dtype.itemsize,
                           [e.dtype.itemsize for e in extras] + [jnp.dtype(dt).itemsize for dt in out_dtypes])
    nk = k // tk
    ne, no, nafter = len(extras), len(out_dtypes), int(after is not None)
    if epilogue is None:
        epilogue = lambda acc: (acc,)

    def body(*refs):
        a_ref, b_ref = refs[0], refs[1]
        ex = refs[2:2 + ne]
        outs = refs[2 + ne + nafter:2 + ne + nafter + no]
        part = lax.dot_general(a_ref[...].astype(BF16), b_ref[...].astype(BF16), _DIMS[mode],
                               preferred_element_type=F32)

        def finish(acc):
            for o_ref, val in zip(outs, epilogue(acc, *[e[...] for e in ex])):
                o_ref[...] = val.astype(o_ref.dtype)

        if nk == 1:
            finish(part)
        else:
            acc_ref = refs[-1]
            kk = pl.program_id(2)

            @pl.when(kk == 0)
            def _():
                acc_ref[...] = part

            @pl.when(kk > 0)
            def _():
                acc_ref[...] += part

            @pl.when(kk == nk - 1)
            def _():
                finish(acc_ref[...])

    if mode == "tn":
        a_spec = pl.BlockSpec((tk, tm), lambda i, j, kk: (kk, i))
    else:
        a_spec = pl.BlockSpec((tm, tk), lambda i, j, kk: (i, kk))
    if mode == "nt":
        b_spec = pl.BlockSpec((tn, tk), lambda i, j, kk: (j, kk))
    else:
        b_spec = pl.BlockSpec((tk, tn), lambda i, j, kk: (kk, j))
    o_spec = pl.BlockSpec((tm, tn), lambda i, j, kk: (i, j))
    res = pl.pallas_call(
        body,
        name=name,
        grid=(m // tm, n // tn, nk),
        in_specs=[a_spec, b_spec] + [o_spec] * ne + [pl.BlockSpec(memory_space=pl.ANY)] * nafter,
        out_specs=[o_spec] * no,
        out_shape=[jax.ShapeDtypeStruct((m, n), dt) for dt in out_dtypes],
        scratch_shapes=[pltpu.VMEM((tm, tn), F32)] if nk > 1 else [],
        compiler_params=_params("parallel", "parallel", "arbitrary"),
    )(a, b, *extras, *([after] if nafter else []))
    return res[0] if no == 1 else res


def _mm_rows(a, b, mode, out_dtypes, name, epilogue, extras=(), vectors=(), n_sums=0, after=None, more=()):
    assert mode in ("nn", "nt")
    (m, k), n = a.shape, (b.shape[1] if mode == "nn" else b.shape[0])
    tm = _tile(m, MM_ROWS_TILE)
    mt = m // tm
    pairs = [(a, b)] + list(more)
    np_ = 2 * len(pairs)
    ne, no, nafter = len(extras) + len(vectors), len(out_dtypes), int(after is not None)

    def body(*refs):
        ex = refs[np_:np_ + ne]
        outs = refs[np_ + ne + nafter:np_ + ne + nafter + no]
        sums = refs[np_ + ne + nafter + no:np_ + ne + nafter + no + n_sums]
        acc_ref = refs[-1]
        i = pl.program_id(0)

        @pl.when(i == 0)
        def _():
            acc_ref[1] = jnp.zeros((tm, n), F32)

        vals = epilogue(acc_ref[1 - i % 2], *[e[...] for e in ex])
        prods = [lax.dot_general(refs[p][...].astype(BF16), refs[p + 1][...].astype(BF16), _DIMS[mode],
                                 preferred_element_type=F32) for p in range(0, np_, 2)]
        acc = prods[0]
        for prod in prods[1:]:
            acc = acc + prod
        acc_ref[i % 2] = acc
        for o_ref, val in zip(outs, vals[:no]):
            o_ref[...] = val.astype(o_ref.dtype)
        for s_ref, val in zip(sums, vals[no:]):
            @pl.when(i <= 1)
            def _(s_ref=s_ref, val=val):
                s_ref[...] = val

            @pl.when(i > 1)
            def _(s_ref=s_ref, val=val):
                s_ref[...] += val

    ahead = lambda i: (jnp.minimum(i, mt - 1), 0)
    behind = lambda i: (jnp.maximum(i - 1, 0), 0)
    fixed = lambda i: (0, 0)
    row = pl.BlockSpec((tm, n), behind)
    vec = pl.BlockSpec((1, n), fixed)
    res = pl.pallas_call(
        body, name=name, grid=(mt + 1,),
        in_specs=([spec for pa, pb in pairs for spec in (pl.BlockSpec((tm, pa.shape[1]), ahead),
                                                         pl.BlockSpec(pb.shape, fixed))] + [row] * len(extras)
                  + [vec] * len(vectors) + [pl.BlockSpec(memory_space=pl.ANY)] * nafter),
        out_specs=[row] * no + [vec] * n_sums,
        out_shape=[jax.ShapeDtypeStruct((m, n), dt) for dt in out_dtypes] + [jax.ShapeDtypeStruct((1, n), F32)] * n_sums,
        scratch_shapes=[pltpu.VMEM((2, tm, n), F32)],
        compiler_params=_params("arbitrary"),
    )(*[x for pair in pairs for x in pair], *extras, *vectors, *([after] if nafter else []))
    return res[0] if no + n_sums == 1 else res


def _ep_residual_norm(acc, res, w):
    h = res + acc
    return h, _rms(h, w)


def _ep_norm_bwd(acc, x, dres, w):
    r = lax.rsqrt(jnp.mean(x * x, axis=-1, keepdims=True) + EPS)
    g = acc * w
    dx = dres + (r * g - x * (r * r * r * jnp.mean(g * x, axis=-1, keepdims=True)))
    return dx, dx, jnp.sum(acc * (x * r), axis=0, keepdims=True)


def _rms_fwd(x, w, name, tm=512):
    n, d = x.shape
    tm = _tile(n, tm)

    def body(x_ref, w_ref, y_ref):
        y_ref[...] = _rms(x_ref[...], w_ref[...]).astype(y_ref.dtype)

    return pl.pallas_call(
        body, name=name, grid=(n // tm,),
        in_specs=[pl.BlockSpec((tm, d), lambda i: (i, 0)), pl.BlockSpec((1, d), lambda i: (0, 0))],
        out_specs=pl.BlockSpec((tm, d), lambda i: (i, 0)),
        out_shape=jax.ShapeDtypeStruct((n, d), BF16),
        compiler_params=_params("arbitrary"),
    )(x, w)


def _loss_head(h, w, target, name, tm=512):
    n, d = h.shape
    tm = _tile(n, tm)

    def body(h_ref, w_ref, t_ref, dh_ref, dhb_ref, dw_ref, sq_ref):
        y, vjp = jax.vjp(_rms, h_ref[...], w_ref[...])
        err = y - t_ref[...]
        dh, dw = vjp(err * (1.0 / d))
        dh_ref[...] = dh
        dhb_ref[...] = dh.astype(dhb_ref.dtype)
        sq = jnp.sum(err * err, axis=0, keepdims=True)

        @pl.when(pl.program_id(0) == 0)
        def _():
            dw_ref[...] = dw
            sq_ref[...] = sq

        @pl.when(pl.program_id(0) > 0)
        def _():
            dw_ref[...] += dw
            sq_ref[...] += sq

        @pl.when(pl.program_id(0) == n // tm - 1)
        def _():
            total = jnp.sum(sq_ref[...], axis=1, keepdims=True) * (0.5 / d)
            sq_ref[...] = jnp.broadcast_to(total, sq_ref.shape)

    row = pl.BlockSpec((tm, d), lambda i: (i, 0))
    vec = pl.BlockSpec((1, d), lambda i: (0, 0))
    return pl.pallas_call(
        body, name=name, grid=(n // tm,),
        in_specs=[row, vec, row],
        out_specs=[row, row, vec, vec],
        out_shape=[jax.ShapeDtypeStruct((n, d), F32), jax.ShapeDtypeStruct((n, d), BF16),
                   jax.ShapeDtypeStruct((1, d), F32), jax.ShapeDtypeStruct((1, d), F32)],
        compiler_params=_params("arbitrary"),
    )(h, w, target)


def _lower_bounds(logits):
    sm = jax.nn.softmax(logits, axis=0)
    rows = [sm[0:1] * 0.0]
    for r in range(1, DEPTH):
        rows.append(rows[-1] + sm[r:r + 1])
    return jnp.concatenate(rows, axis=0)


def _lb_fwd(logits, name):
    def body(l_ref, o_ref):
        o_ref[...] = _lower_bounds(l_ref[...])

    return pl.pallas_call(body, name=name, out_shape=jax.ShapeDtypeStruct(logits.shape, F32))(logits)


def _lb_bwd(logits, dlb, name):
    def body(l_ref, d_ref, o_ref):
        _, vjp = jax.vjp(_lower_bounds, l_ref[...])
        (o_ref[...],) = vjp(d_ref[...])

    return pl.pallas_call(body, name=name, out_shape=jax.ShapeDtypeStruct(logits.shape, F32))(logits, dlb)


def _head_slice(h):
    return pl.ds(h * HEAD_DIM, HEAD_DIM)


_GD_HEADS = jax.vmap(_gd_head, in_axes=(0, 0, 0, 0, 0, 0, 0, 0, None))
_GD_HEADS_AGAIN = jax.vmap(_gd_head, in_axes=(0, 0, 0, 0, 0, 0, 0, 0, None, 0))


def _lane_blocks(width, block_body):
    def trip(j, carry):
        block_body(lambda base=0: pl.ds(pl.multiple_of(j * LANE_BLOCK + base, LANE_BLOCK), LANE_BLOCK))
        return carry

    lax.fori_loop(0, width // LANE_BLOCK, trip, 0, unroll=BLOCK_UNROLL)


def _row_blocks(rows, block_body):
    def trip(j, carry):
        block_body(pl.ds(pl.multiple_of(j * ROW_BLOCK, ROW_BLOCK), ROW_BLOCK))
        return carry

    lax.fori_loop(0, rows // ROW_BLOCK, trip, 0, unroll=BLOCK_UNROLL)


def _hg_pre_block(p_ref, lb_ref, sums_refs, q_sc, k_sc, v_sc, e_sc, at):
    sl = at()
    q_sc[:, sl], k_sc[:, sl], e_sc[:, sl] = _hg_pre(
        p_ref[:, sl].astype(F32), p_ref[:, at(D_MODEL)].astype(F32), lb_ref[:, sl], [r[...] for r in sums_refs])
    v_sc[:, sl] = p_ref[:, at(2 * D_MODEL)].astype(F32)


def _gd_xp(halo_ref, p_ref, sl, first_chunk):
    halo = jnp.where(first_chunk, 0.0, halo_ref[:, sl].astype(F32))
    return jnp.concatenate([halo, p_ref[:, sl].astype(F32)], axis=0)


def _stack_all(ref, first=0):
    return jnp.stack([ref[s, :, _head_slice(h + first)] for s in range(ref.shape[0]) for h in range(N_HEADS)])


def _unstack_all(ref, val, first=0):
    for s in range(ref.shape[0]):
        for h in range(N_HEADS):
            ref[s, :, _head_slice(h + first)] = val[s * N_HEADS + h].astype(ref.dtype)


def _gdn_fwd_all(projm, projab, cw, alog, dtb, onw, seqs, name):
    n = projm.shape[0]
    t = n // seqs
    nc = t // CHUNK
    d = D_MODEL
    per_halo = CHUNK // HALO
    nh = seqs * N_HEADS

    def body(p_ref, halo_ref, ab_ref, cw_ref, alog_ref, dtb_ref, onw_ref, o2_ref, st_all_ref, y_ref, dinv_ref,
             st_sc, c_sc, beta_sc, g_sc):
        first_chunk = pl.program_id(0) == 0

        @pl.when(first_chunk)
        def _():
            st_sc[...] = jnp.zeros_like(st_sc)

        for s in range(seqs):
            def conv(at, s=s):
                sl = at()
                y = _gd_conv(_gd_xp(halo_ref.at[s], p_ref.at[s], sl, first_chunk), cw_ref[:, sl])
                y_ref[s, :, sl] = y
                c_sc[s, :, sl] = jax.nn.silu(y)

            _lane_blocks(3 * d, conv)
            beta_sc[s], g_sc[s] = _gd_gates(ab_ref[s, :, 0:N_HEADS], ab_ref[s, :, N_HEADS:2 * N_HEADS],
                                            alog_ref[...], dtb_ref[...])
        st_all_ref[0] = st_sc[...]
        o2, st_sc[...], dinv_ref[0] = _GD_HEADS(
            st_sc[...], _stack_all(c_sc), _stack_all(c_sc, N_HEADS), _stack_all(c_sc, 2 * N_HEADS), _stack_all(beta_sc),
            _stack_all(g_sc), _stack_all(g_sc, N_HEADS), _stack_all(p_ref, 3 * N_HEADS).astype(F32), onw_ref[...])
        _unstack_all(o2_ref, o2)

    rows = lambda c: (0, c, 0)
    const = lambda c: (0, 0)
    per_chunk = lambda c: (c, 0, 0, 0)
    p3 = projm.reshape(seqs, t, 4 * d)
    o2, st_all, conv_y, dinv_all = pl.pallas_call(
        body, name=name, grid=(nc,),
        in_specs=[pl.BlockSpec((seqs, CHUNK, 4 * d), rows),
                  pl.BlockSpec((seqs, HALO, 3 * d), lambda c: (0, jnp.maximum(c * per_halo - 1, 0), 0)),
                  pl.BlockSpec((seqs, CHUNK, AB_PAD), rows),
                  pl.BlockSpec((CONV_K, 3 * d), const), pl.BlockSpec((1, N_HEADS), const),
                  pl.BlockSpec((1, N_HEADS), const), pl.BlockSpec((1, HEAD_DIM), const)],
        out_specs=[pl.BlockSpec((seqs, CHUNK, d), rows), pl.BlockSpec((1, nh, HEAD_DIM, HEAD_DIM), per_chunk),
                   pl.BlockSpec((seqs, CHUNK, 3 * d), rows), pl.BlockSpec((1, nh, CHUNK, CHUNK), per_chunk)],
        out_shape=[jax.ShapeDtypeStruct((seqs, t, d), BF16), jax.ShapeDtypeStruct((nc, nh, HEAD_DIM, HEAD_DIM), F32),
                   jax.ShapeDtypeStruct((seqs, t, 3 * d), F32), jax.ShapeDtypeStruct((nc, nh, CHUNK, CHUNK), F32)],
        scratch_shapes=[pltpu.VMEM((nh, HEAD_DIM, HEAD_DIM), F32), pltpu.VMEM((seqs, CHUNK, 3 * d), F32),
                        pltpu.VMEM((seqs, CHUNK, d), F32), pltpu.VMEM((seqs, CHUNK, 2 * d), F32)],
        compiler_params=_params("arbitrary"),
    )(p3, p3, projab.reshape(seqs, t, AB_PAD), cw, alog, dtb, onw)
    return o2.reshape(n, d), st_all, conv_y, dinv_all


def _gdn_bwd_all(projm, projab, conv_y, cw, alog, dtb, onw, st_all, dinv_all, do2, seqs, name):
    n = projm.shape[0]
    t = n // seqs
    nc = t // CHUNK
    d = D_MODEL
    per_halo = CHUNK // HALO
    nh = seqs * N_HEADS

    def body(p_ref, halo_ref, ab_ref, y_ref, cw_ref, alog_ref, dtb_ref, onw_ref, st_all_ref, dinv_ref, do2_ref,
             dp_ref, dab_ref, dcw_ref, dalog_ref, ddtb_ref, donw_ref,
             dst_sc, dhalo_sc, c_sc, beta_sc, g_sc, dc_sc, dbeta_sc, dg_sc):
        first = pl.program_id(0) == 0
        first_chunk = pl.program_id(0) == nc - 1

        @pl.when(first)
        def _():
            dst_sc[...] = jnp.zeros_like(dst_sc)
            dhalo_sc[...] = jnp.zeros_like(dhalo_sc)

        gates_vjps = []
        for s in range(seqs):
            def act(at, s=s):
                c_sc[s, :, at()] = jax.nn.silu(y_ref[s, :, at()])

            _lane_blocks(3 * d, act)
            (beta_sc[s], g_sc[s]), gates_vjp = jax.vjp(
                _gd_gates, ab_ref[s, :, 0:N_HEADS], ab_ref[s, :, N_HEADS:2 * N_HEADS], alog_ref[...], dtb_ref[...])
            gates_vjps.append(gates_vjp)

        dinv = dinv_ref[0]
        _, vjp = jax.vjp(
            lambda *a: _GD_HEADS_AGAIN(*a, dinv), st_all_ref[0], _stack_all(c_sc), _stack_all(c_sc, N_HEADS),
            _stack_all(c_sc, 2 * N_HEADS), _stack_all(beta_sc), _stack_all(g_sc), _stack_all(g_sc, N_HEADS),
            _stack_all(p_ref, 3 * N_HEADS).astype(F32), onw_ref[...])
        dst_sc[...], dq, dk, dv, dbeta, dg, ddiff, dgate, donw = vjp((_stack_all(do2_ref).astype(F32), dst_sc[...]))
        _unstack_all(dc_sc, dq)
        _unstack_all(dc_sc, dk, N_HEADS)
        _unstack_all(dc_sc, dv, 2 * N_HEADS)
        _unstack_all(dbeta_sc, dbeta)
        _unstack_all(dg_sc, dg)
        _unstack_all(dg_sc, ddiff, N_HEADS)
        _unstack_all(dp_ref, dgate, 3 * N_HEADS)

        dalog, ddtb = None, None
        for s in range(seqs):
            def conv_bwd(at, s=s):
                sl = at()
                dxp, dcw = _gd_conv_bwd(_gd_xp(halo_ref.at[s], p_ref.at[s], sl, first_chunk), cw_ref[:, sl],
                                        y_ref[s, :, sl], dc_sc[s, :, sl])
                dqkv = jnp.concatenate([dxp[HALO:CHUNK], dxp[CHUNK:HALO + CHUNK] + dhalo_sc[s, :, sl]], axis=0)
                dp_ref[s, :, sl] = dqkv.astype(dp_ref.dtype)
                dhalo_sc[s, :, sl] = dxp[0:HALO]

                if s > 0:
                    dcw_ref[:, sl] += dcw
                    return

                @pl.when(first)
                def _():
                    dcw_ref[:, sl] = dcw

                @pl.when(jnp.logical_not(first))
                def _():
                    dcw_ref[:, sl] += dcw

            _lane_blocks(3 * d, conv_bwd)
            da, db, dalog_s, ddtb_s = gates_vjps[s]((dbeta_sc[s], dg_sc[s]))
            dab_ref[s] = jnp.concatenate(
                [da, db, jnp.zeros((CHUNK, AB_PAD - 2 * N_HEADS), F32)], axis=1).astype(dab_ref.dtype)
            dalog = dalog_s if dalog is None else dalog + dalog_s
            ddtb = ddtb_s if ddtb is None else ddtb + ddtb_s

        @pl.when(first)
        def _():
            dalog_ref[...] = dalog
            ddtb_ref[...] = ddtb
            donw_ref[...] = donw

        @pl.when(jnp.logical_not(first))
        def _():
            dalog_ref[...] += dalog
            ddtb_ref[...] += ddtb
            donw_ref[...] += donw

    back = lambda c: nc - 1 - c
    rows = lambda c: (0, back(c), 0)
    const = lambda c: (0, 0)
    per_chunk = lambda c: (back(c), 0, 0, 0)
    small = [pl.BlockSpec((CONV_K, 3 * d), const), pl.BlockSpec((1, N_HEADS), const),
             pl.BlockSpec((1, N_HEADS), const), pl.BlockSpec((1, HEAD_DIM), const)]
    p3 = projm.reshape(seqs, t, 4 * d)
    dp, dab, dcw, dalog, ddtb, donw = pl.pallas_call(
        body, name=name, grid=(nc,),
        in_specs=[pl.BlockSpec((seqs, CHUNK, 4 * d), rows),
                  pl.BlockSpec((seqs, HALO, 3 * d), lambda c: (0, jnp.maximum(back(c) * per_halo - 1, 0), 0)),
                  pl.BlockSpec((seqs, CHUNK, AB_PAD), rows), pl.BlockSpec((seqs, CHUNK, 3 * d), rows)] + small + [
                  pl.BlockSpec((1, nh, HEAD_DIM, HEAD_DIM), per_chunk), pl.BlockSpec((1, nh, CHUNK, CHUNK), per_chunk),
                  pl.BlockSpec((seqs, CHUNK, d), rows)],
        out_specs=[pl.BlockSpec((seqs, CHUNK, 4 * d), rows), pl.BlockSpec((seqs, CHUNK, AB_PAD), rows)] + small,
        out_shape=[jax.ShapeDtypeStruct((seqs, t, 4 * d), BF16), jax.ShapeDtypeStruct((seqs, t, AB_PAD), BF16),
                   jax.ShapeDtypeStruct((CONV_K, 3 * d), F32), jax.ShapeDtypeStruct((1, N_HEADS), F32),
                   jax.ShapeDtypeStruct((1, N_HEADS), F32), jax.ShapeDtypeStruct((1, HEAD_DIM), F32)],
        scratch_shapes=[pltpu.VMEM((nh, HEAD_DIM, HEAD_DIM), F32), pltpu.VMEM((seqs, HALO, 3 * d), F32),
                        pltpu.VMEM((seqs, CHUNK, 3 * d), F32), pltpu.VMEM((seqs, CHUNK, d), F32),
                        pltpu.VMEM((seqs, CHUNK, 2 * d), F32), pltpu.VMEM((seqs, CHUNK, 3 * d), F32),
                        pltpu.VMEM((seqs, CHUNK, d), F32), pltpu.VMEM((seqs, CHUNK, 2 * d), F32)],
        compiler_params=_params("arbitrary"),
    )(p3, p3, projab.reshape(seqs, t, AB_PAD), conv_y, cw, alog, dtb, onw, st_all, dinv_all,
      do2.reshape(seqs, t, d))
    return dp.reshape(n, 4 * d), dab.reshape(n, AB_PAD), dcw, dalog, ddtb, donw


def _hgrn_fwd_all(proj, lb, gw, seqs, name):
    n = proj.shape[0]
    t = n // seqs
    nc = t // CHUNK
    d = D_MODEL
    nh = seqs * N_HEADS
    sums, masks = _hg_level_sums(), _hg_level_masks()

    def body(p_ref, lb_ref, gw_ref, sums_wide_ref, sums_once_ref, masks_ref, o2_ref, o_ref, st_all_ref,
             st_sc, q_sc, k_sc, v_sc, e_sc):
        @pl.when(pl.program_id(0) == 0)
        def _():
            st_sc[...] = jnp.zeros_like(st_sc)

        sums_refs = (sums_wide_ref, sums_once_ref)
        for s in range(seqs):
            _lane_blocks(d, functools.partial(_hg_pre_block, p_ref.at[s], lb_ref, sums_refs, q_sc.at[s], k_sc.at[s],
                                              v_sc.at[s], e_sc.at[s]))
        st_all_ref[0] = st_sc[...]
        for s in range(seqs):
            one, mine = pl.ds(s, 1), pl.ds(s * N_HEADS, N_HEADS)
            o, st_sc[mine] = _HG_HEADS(st_sc[mine], *[_stack_all(r.at[one]) for r in (q_sc, k_sc, v_sc, e_sc)],
                                       masks_ref[...])
            _unstack_all(o_ref.at[one], o)

            def post(rows, s=s):
                gate = p_ref[s, rows, 3 * d:4 * d].astype(F32)
                o2_ref[s, rows, :] = _hg_post(o_ref[s, rows, :], gate, gw_ref[...]).astype(o2_ref.dtype)

            _row_blocks(CHUNK, post)

    rows = lambda c: (0, c, 0)
    vec = pl.BlockSpec((1, d), lambda c: (0, 0))
    act = pl.BlockSpec((seqs, CHUNK, d), rows)
    o2, o, st_all = pl.pallas_call(
        body, name=name, grid=(nc,),
        in_specs=[pl.BlockSpec((seqs, CHUNK, 4 * d), rows), vec, vec]
        + [pl.BlockSpec(m.shape, lambda c: (0, 0)) for m in sums] + [pl.BlockSpec(masks.shape, lambda c: (0, 0, 0))],
        out_specs=[act, act, pl.BlockSpec((1, nh, HEAD_DIM, HEAD_DIM), lambda c: (c, 0, 0, 0))],
        out_shape=[jax.ShapeDtypeStruct((seqs, t, d), BF16), jax.ShapeDtypeStruct((seqs, t, d), F32),
                   jax.ShapeDtypeStruct((nc, nh, HEAD_DIM, HEAD_DIM), F32)],
        scratch_shapes=[pltpu.VMEM((nh, HEAD_DIM, HEAD_DIM), F32)] + [pltpu.VMEM((seqs, CHUNK, d), F32)] * 3
        + [pltpu.VMEM((seqs, sums[0].shape[0], d), F32)],
        compiler_params=_params("arbitrary"),
    )(proj.reshape(seqs, t, 4 * d), lb, gw, *sums, masks)
    return o2.reshape(n, d), o, st_all


def _hgrn_bwd_all(proj, lb, gw, st_all, o, do2, seqs, name):
    n = proj.shape[0]
    t = n // seqs
    nc = t // CHUNK
    d = D_MODEL
    nh = seqs * N_HEADS
    sums, masks = _hg_level_sums(), _hg_level_masks()

    def body(p_ref, lb_ref, gw_ref, sums_wide_ref, sums_once_ref, masks_ref, st_all_ref, o_ref, do2_ref,
             dp_ref, dlb_ref, dgw_ref,
             dst_sc, q_sc, k_sc, v_sc, e_sc, do_sc, dq_sc, dk_sc, dv_sc, de_sc, dgw_sc):
        first = pl.program_id(0) == 0

        @pl.when(first)
        def _():
            dst_sc[...] = jnp.zeros_like(dst_sc)

        sums_refs = (sums_wide_ref, sums_once_ref)
        dgw_sc[...] = jnp.zeros_like(dgw_sc)
        for s in range(seqs):
            _lane_blocks(d, functools.partial(_hg_pre_block, p_ref.at[s], lb_ref, sums_refs, q_sc.at[s], k_sc.at[s],
                                              v_sc.at[s], e_sc.at[s]))

            def post_bwd(rows, s=s):
                _, vjp = jax.vjp(_hg_post, o_ref[s, rows, :], p_ref[s, rows, 3 * d:4 * d].astype(F32), gw_ref[...])
                do_sc[s, rows, :], dgate, dgw = vjp(do2_ref[s, rows, :].astype(F32))
                dp_ref[s, rows, 3 * d:4 * d] = dgate.astype(dp_ref.dtype)
                dgw_sc[...] += dgw

            _row_blocks(CHUNK, post_bwd)

        level_masks = masks_ref[...]
        _, vjp = jax.vjp(lambda *a: _HG_HEADS(*a, level_masks), st_all_ref[0],
                         *[_stack_all(r) for r in (q_sc, k_sc, v_sc, e_sc)])
        grads = vjp((_stack_all(do_sc), dst_sc[...]))
        dst_sc[...] = grads[0]
        for r, val in zip((dq_sc, dk_sc, dv_sc, de_sc), grads[1:]):
            _unstack_all(r, val)

        for s in range(seqs):
            def pre_bwd(at, s=s):
                sl = at()
                level_sums = (sums_wide_ref[...], sums_once_ref[...])
                _, vjp = jax.vjp(lambda qraw, f, lb: _hg_pre(qraw, f, lb, level_sums), p_ref[s, :, sl].astype(F32),
                                 p_ref[s, :, at(d)].astype(F32), lb_ref[:, sl])
                dqraw, df, dlb = vjp((dq_sc[s, :, sl], dk_sc[s, :, sl], de_sc[s, :, sl]))
                dp_ref[s, :, sl] = dqraw.astype(dp_ref.dtype)
                dp_ref[s, :, at(d)] = df.astype(dp_ref.dtype)
                dp_ref[s, :, at(2 * d)] = dv_sc[s, :, sl].astype(dp_ref.dtype)
                if s > 0:
                    dlb_ref[:, sl] += dlb
                    return

                @pl.when(first)
                def _():
                    dlb_ref[:, sl] = dlb

                @pl.when(jnp.logical_not(first))
                def _():
                    dlb_ref[:, sl] += dlb

            _lane_blocks(d, pre_bwd)

        @pl.when(first)
        def _():
            dgw_ref[...] = dgw_sc[...]

        @pl.when(jnp.logical_not(first))
        def _():
            dgw_ref[...] += dgw_sc[...]

    rows = lambda c: (0, nc - 1 - c, 0)
    vec = pl.BlockSpec((1, d), lambda c: (0, 0))
    act = pl.BlockSpec((seqs, CHUNK, d), rows)
    wide = pl.BlockSpec((seqs, CHUNK, 4 * d), rows)
    e_rows = sums[0].shape[0]
    dp, dlb, dgw = pl.pallas_call(
        body, name=name, grid=(nc,),
        in_specs=[wide, vec, vec] + [pl.BlockSpec(m.shape, lambda c: (0, 0)) for m in sums] + [
                  pl.BlockSpec(masks.shape, lambda c: (0, 0, 0)),
                  pl.BlockSpec((1, nh, HEAD_DIM, HEAD_DIM), lambda c: (nc - 1 - c, 0, 0, 0)), act, act],
        out_specs=[wide, vec, vec],
        out_shape=[jax.ShapeDtypeStruct((seqs, t, 4 * d), BF16), jax.ShapeDtypeStruct((1, d), F32),
                   jax.ShapeDtypeStruct((1, d), F32)],
        scratch_shapes=[pltpu.VMEM((nh, HEAD_DIM, HEAD_DIM), F32)]
        + [pltpu.VMEM((seqs, CHUNK, d), F32)] * 3 + [pltpu.VMEM((seqs, e_rows, d), F32)]
        + [pltpu.VMEM((seqs, CHUNK, d), F32)] * 4 + [pltpu.VMEM((seqs, e_rows, d), F32), pltpu.VMEM((1, d), F32)],
        compiler_params=_params("arbitrary"),
    )(proj.reshape(seqs, t, 4 * d), lb, gw, *sums, masks, st_all, o, do2.reshape(seqs, t, d))
    return dp.reshape(n, 4 * d), dlb, dgw


def _adam_update(w, g, m, v):
    b1c = 1.0 - ADAM_B1 ** ADAM_STEP
    b2c = 1.0 - ADAM_B2 ** ADAM_STEP
    m_new = ADAM_B1 * m + (1.0 - ADAM_B1) * g
    v_new = ADAM_B2 * v + (1.0 - ADAM_B2) * (g * g)
    delta = -ADAM_LR * ((m_new / b1c) / (jnp.sqrt(v_new / b2c) + ADAM_EPS) + ADAM_WD * w)
    return delta, m_new, v_new


def _adamw(w, g, m, v, name, tr=256):
    r, c = w.shape
    tr = _tile(r, tr)

    def body(w_ref, g_ref, m_ref, v_ref, d_ref, mo_ref, vo_ref):
        d_ref[...], mo_ref[...], vo_ref[...] = _adam_update(w_ref[...], g_ref[...], m_ref[...], v_ref[...])

    blk = pl.BlockSpec((tr, c), lambda i: (i, 0))
    return pl.pallas_call(
        body, name=name, grid=(r // tr,),
        in_specs=[blk] * 4, out_specs=[blk] * 3,
        out_shape=[jax.ShapeDtypeStruct((r, c), F32)] * 3,
        compiler_params=_params("arbitrary"),
    )(w, g, m, v)


def _adamw_slots(w, slot_bufs, m, v, name, tr=256):
    nl, r, c = w.shape
    tr = _tile(r, tr)

    def body(*refs):
        w_ref = refs[0]
        g_refs = refs[1:1 + nl]
        m_ref, v_ref, go_ref, d_ref, mo_ref, vo_ref = refs[1 + nl:]
        for k in range(nl):
            @pl.when(pl.program_id(0) == k)
            def _(k=k):
                g = g_refs[k][0].astype(F32)
                for s in range(1, N_DEV):
                    g = g + g_refs[k][s].astype(F32)
                go_ref[0] = g

        d_ref[0], mo_ref[0], vo_ref[0] = _adam_update(w_ref[0], go_ref[0], m_ref[0], v_ref[0])

    blk = pl.BlockSpec((1, tr, c), lambda l, i: (l, i, 0))
    g_specs = [pl.BlockSpec((N_DEV, tr, c), lambda l, i, k=k: (0, jnp.where(l == k, i, 0), 0)) for k in range(nl)]
    return pl.pallas_call(
        body, name=name, grid=(nl, r // tr),
        in_specs=[blk] + g_specs + [blk, blk], out_specs=[blk] * 4,
        out_shape=[jax.ShapeDtypeStruct((nl, r, c), F32)] * 4,
        compiler_params=_params("arbitrary", "arbitrary"),
    )(w, *slot_bufs, m, v)


def _adamw_windows(w, lo_bufs, hi_bufs, end_bufs, me, m, v, name, tr=256):
    nl, r, c = w.shape
    wl, wh = lo_bufs[0].shape[2], hi_bufs[0].shape[2]
    width, step = wl + wh, c - wl
    assert step >= 0 and (N_DEV - 1) * step + c <= width and N_DEV == 8
    tr = _tile(r, tr)

    def body(*refs):
        me_ref, w_ref = refs[0], refs[1]
        lo_refs, hi_refs, end_refs = refs[2:2 + nl], refs[2 + nl:2 + 2 * nl], refs[2 + 2 * nl:2 + 3 * nl]
        m_ref, v_ref, go_ref, d_ref, mo_ref, vo_ref = refs[2 + 3 * nl:]
        for k in range(nl):
            @pl.when(pl.program_id(0) == k)
            def _(k=k):
                last = me_ref[0] == N_DEV - 1
                hi_of = lambda s: jnp.where(last, end_refs[k][s].astype(F32), hi_refs[k][s].astype(F32))
                lo, hi = lo_refs[k][0].astype(F32), hi_of(0)
                for s in range(1, N_DEV):
                    lo, hi = lo + lo_refs[k][s].astype(F32), hi + hi_of(s)
                g = jnp.concatenate([lo, hi], axis=1)
                for bit in range(3):
                    moved = pltpu.roll(g, width - (step << bit), axis=1)
                    g = jnp.where((me_ref[0] >> bit) & 1 == 1, moved, g)
                go_ref[0] = g[:, :c]

        d_ref[0], mo_ref[0], vo_ref[0] = _adam_update(w_ref[0], go_ref[0], m_ref[0], v_ref[0])

    blk = pl.BlockSpec((1, tr, c), lambda l, i: (l, i, 0))
    g_specs = [pl.BlockSpec((N_DEV, tr, cols), lambda l, i, k=k: (0, jnp.where(l == k, i, 0), 0))
               for cols in (wl, wh, wh) for k in range(nl)]
    return pl.pallas_call(
        body, name=name, grid=(nl, r // tr),
        in_specs=[pl.BlockSpec(memory_space=pltpu.SMEM), blk] + g_specs + [blk, blk], out_specs=[blk] * 4,
        out_shape=[jax.ShapeDtypeStruct((nl, r, c), F32)] * 4,
        compiler_params=_params("arbitrary", "arbitrary"),
    )(me, w, *lo_bufs, *hi_bufs, *end_bufs, m, v)


def _window(w, me, width, step, name, tr=256):
    r, c = w.shape
    assert (N_DEV - 1) * step + c <= width and N_DEV == 8
    tr = _tile(r, tr)

    def body(me_ref, w_ref, out_ref, wide):
        wide[...] = jnp.zeros_like(wide)
        wide[:, 0:c] = w_ref[...]
        g = wide[...]
        for bit in range(3):
            moved = pltpu.roll(g, step << bit, axis=1)
            g = jnp.where((me_ref[0] >> bit) & 1 == 1, moved, g)
        out_ref[...] = g.astype(out_ref.dtype)

    return pl.pallas_call(
        body, name=name, grid=(r // tr,),
        in_specs=[pl.BlockSpec(memory_space=pltpu.SMEM), pl.BlockSpec((tr, c), lambda i: (i, 0))],
        out_specs=pl.BlockSpec((tr, width), lambda i: (i, 0)),
        out_shape=jax.ShapeDtypeStruct((r, width), BF16),
        scratch_shapes=[pltpu.VMEM((tr, width), F32)],
        compiler_params=_params("arbitrary"),
    )(me, w)


def _unshard_windows(win, c, name, tr=256):
    nd, r, w = win.shape
    wl = w - AB_PAD
    step = c - wl
    assert 0 <= step and nd * step <= AB_PAD
    tr = _tile(r, tr)

    def body(win_ref, main_ref, tail_ref):
        lane = lax.broadcasted_iota(jnp.int32, (tr, AB_PAD), 1)

        def past(s):
            return win_ref[s, :, wl:w].astype(F32)

        for s in range(nd):
            first = win_ref[s, :, 0:AB_PAD].astype(F32)
            if s > 0:
                first = jnp.where(lane < s * step, past(s - 1), first)
            main_ref[:, s * wl:s * wl + AB_PAD] = first.astype(main_ref.dtype)
            main_ref[:, s * wl + AB_PAD:(s + 1) * wl] = win_ref[s, :, AB_PAD:wl]
        tail_ref[...] = jnp.where(lane < nd * step, past(nd - 1), 0.0).astype(tail_ref.dtype)

    return pl.pallas_call(
        body, name=name, grid=(r // tr,),
        in_specs=[pl.BlockSpec((nd, tr, w), lambda i: (0, i, 0))],
        out_specs=[pl.BlockSpec((tr, nd * wl), lambda i: (i, 0)), pl.BlockSpec((tr, AB_PAD), lambda i: (i, 0))],
        out_shape=[jax.ShapeDtypeStruct((r, nd * wl), win.dtype), jax.ShapeDtypeStruct((r, AB_PAD), win.dtype)],
        compiler_params=_params("arbitrary"),
    )(win)


def _mesh_pos():
    return lax.axis_index("x"), lax.axis_index("y"), lax.axis_index("c")


def _flip(pos, p):
    x, y, c = pos
    return ((1 - x) if p & 4 else x, (1 - y) if p & 2 else y, (1 - c) if p & 1 else c)


def _lin(pos):
    return 4 * pos[0] + 2 * pos[1] + pos[2]


_HBM = pl.BlockSpec(memory_space=pltpu.HBM)
_SEM = pl.BlockSpec(memory_space=pltpu.SEMAPHORE)
_DATAFLOW = pltpu.SideEffectType.DATAFLOW_SIDE_EFFECTING


class _Item:
    def __init__(self, src, land_shape, src_pick, dst_pick, peers=tuple(range(1, N_DEV))):
        self.src, self.land_shape, self.src_pick, self.dst_pick = src, land_shape, src_pick, dst_pick
        self.peers = peers


def _distinct(arrays):
    found, where = [], []
    for a in arrays:
        hits = [k for k, f in enumerate(found) if f is a]
        where.append(hits[0] if hits else len(found))
        if not hits:
            found.append(a)
    return found, where


def _remote_copies(items, src, land, send_sem, recv_sem, me, arriving):
    me_i = _lin(me)
    out = []
    for it, s_ref, l_ref in zip(items, src, land):
        for p in it.peers:
            peer = _flip(me, p)
            out.append(pltpu.make_async_remote_copy(
                src_ref=it.src_pick(s_ref, _lin(peer)),
                dst_ref=it.dst_pick(l_ref, _lin(peer) if arriving else me_i),
                send_sem=send_sem, recv_sem=recv_sem, device_id=peer, device_id_type=pl.DeviceIdType.MESH))
    return out


def _own_copies(items, src, land, sem, me):
    me_i = _lin(me)
    return [pltpu.make_async_copy(it.src_pick(s_ref, me_i), it.dst_pick(l_ref, me_i), sem)
            for it, s_ref, l_ref in zip(items, src, land)]


def _exchange_start(groups, name):
    items = [it for g in groups for it in g]
    n, ng = len(items), len(groups)
    first = [sum(len(g) for g in groups[:gi]) for gi in range(ng)]
    arrays, where = _distinct([it.src for it in items])
    nu = len(arrays)

    def body(*refs):
        src, land = [refs[k] for k in where], refs[nu:nu + n]
        send_sems, recv_sems = refs[nu + n:nu + n + ng], refs[nu + n + ng:nu + n + 2 * ng]
        token = refs[2 * (nu + n) + 2 * ng]
        me = _mesh_pos()
        for gi, g in enumerate(groups):
            sl = slice(first[gi], first[gi] + len(g))
            for cp in _remote_copies(g, src[sl], land[sl], send_sems[gi], recv_sems[gi], me, arriving=False):
                cp.start()
            for cp in _own_copies(g, src[sl], land[sl], recv_sems[gi], me):
                cp.start()
        token[...] = jnp.zeros_like(token)

    srcs = [pltpu.with_memory_space_constraint(a, pltpu.HBM) for a in arrays]
    lands = [pltpu.with_memory_space_constraint(lax.empty(it.land_shape, it.src.dtype), pltpu.HBM) for it in items]
    res = pl.pallas_call(
        body, name=name,
        out_shape=([pltpu.SemaphoreType.DMA(())] * (2 * ng)
                   + [pltpu.HBM(a.shape, a.dtype) for a in arrays]
                   + [pltpu.HBM(it.land_shape, it.src.dtype) for it in items]
                   + [jax.ShapeDtypeStruct((8, 128), F32)]),
        in_specs=[_HBM] * (nu + n),
        out_specs=[_SEM] * (2 * ng) + [_HBM] * (nu + n) + [pl.BlockSpec(memory_space=pltpu.VMEM)],
        input_output_aliases={i: 2 * ng + i for i in range(nu + n)},
        compiler_params=pltpu.CompilerParams(has_side_effects=_DATAFLOW),
    )(*srcs, *lands)
    send_sems, recv_sems = res[0:ng], res[ng:2 * ng]
    src_thru, land_thru = [res[2 * ng + k] for k in where], res[2 * ng + nu:2 * ng + nu + n]
    handles = []
    for gi, g in enumerate(groups):
        sl = slice(first[gi], first[gi] + len(g))
        handles.append((g, src_thru[sl], land_thru[sl], send_sems[gi], recv_sems[gi]))
    return handles, res[-1]


def _exchange_wait(handle, after, name):
    items, src_thru, land_thru, send_sem, recv_sem = handle
    k = len(items)
    arrays, where = _distinct(src_thru)
    nu = len(arrays)
    afters = list(after) if isinstance(after, (list, tuple)) else [after]

    def body(*refs):
        src, land = [refs[u] for u in where], refs[nu:nu + k]
        send_ref, recv_ref = refs[nu + k], refs[nu + k + 1]
        for cp in _remote_copies(items, src, land, send_ref, recv_ref, _mesh_pos(), arriving=True):
            cp.wait_send()
            cp.wait_recv()
        for cp in _own_copies(items, src, land, recv_ref, _mesh_pos()):
            cp.wait()

    res = pl.pallas_call(
        body, name=name,
        out_shape=([pltpu.HBM(s.shape, s.dtype) for s in arrays] + [pltpu.HBM(l.shape, l.dtype) for l in land_thru]),
        in_specs=[_HBM] * (nu + k) + [_SEM, _SEM] + [pl.BlockSpec(memory_space=pl.ANY)] * len(afters),
        out_specs=[_HBM] * (nu + k),
        input_output_aliases={i: i for i in range(nu + k)},
        compiler_params=pltpu.CompilerParams(has_side_effects=_DATAFLOW),
    )(*arrays, *land_thru, send_sem, recv_sem, *afters)
    return res[nu:nu + k]


SAME_CORE = (2, 4, 6)
SIBLING = 1


def _pass_on_start(buf, name):
    def body(buf_ref, send_sem, recv_sem, thru_ref):
        me = _mesh_pos()
        for p in SAME_CORE:
            slot = buf_ref.at[_lin(_flip(me, p))]
            pltpu.make_async_remote_copy(src_ref=slot, dst_ref=slot, send_sem=send_sem, recv_sem=recv_sem,
                                         device_id=_flip(me, SIBLING), device_id_type=pl.DeviceIdType.MESH).start()

    return pl.pallas_call(
        body, name=name,
        out_shape=[pltpu.SemaphoreType.DMA(()), pltpu.SemaphoreType.DMA(()), pltpu.HBM(buf.shape, buf.dtype)],
        in_specs=[_HBM], out_specs=[_SEM, _SEM, _HBM], input_output_aliases={0: 2},
        compiler_params=pltpu.CompilerParams(has_side_effects=_DATAFLOW),
    )(pltpu.with_memory_space_constraint(buf, pltpu.HBM))


def _pass_on_wait(handle, name):
    send_sem, recv_sem, thru = handle

    def body(buf_ref, send_ref, recv_ref, out_ref):
        me = _mesh_pos()
        sibling = _flip(me, SIBLING)
        for p in SAME_CORE:
            mine, theirs = buf_ref.at[_lin(_flip(me, p))], buf_ref.at[_lin(_flip(sibling, p))]
            cp = pltpu.make_async_remote_copy(src_ref=mine, dst_ref=theirs, send_sem=send_ref, recv_sem=recv_ref,
                                              device_id=sibling, device_id_type=pl.DeviceIdType.MESH)
            cp.wait_send()
            cp.wait_recv()

    return pl.pallas_call(
        body, name=name, out_shape=pltpu.HBM(thru.shape, thru.dtype),
        in_specs=[_HBM, _SEM, _SEM], out_specs=_HBM, input_output_aliases={0: 0},
        compiler_params=pltpu.CompilerParams(has_side_effects=_DATAFLOW),
    )(thru, send_sem, recv_sem)


def _whole(ref, i):
    return ref


def _slot(ref, i):
    return ref.at[i]


def _rows_of(r):
    return lambda ref, i: ref.at[pl.ds(pl.multiple_of(i * r, r), r), :]


def _cols_of(c):
    return lambda ref, i: ref.at[:, pl.ds(pl.multiple_of(i * c, c), c)]


def _all_reduce_small(buf, after, name):
    r, c = buf.shape

    def body(src_ref, after_ref, out_ref, all_ref, send_sems, recv_sems):
        me = _mesh_pos()
        me_i = _lin(me)
        all_ref[me_i] = src_ref[...]
        for p in range(1, N_DEV):
            peer = _flip(me, p)
            pltpu.make_async_remote_copy(
                src_ref=src_ref, dst_ref=all_ref.at[me_i], send_sem=send_sems.at[p - 1], recv_sem=recv_sems.at[p - 1],
                device_id=peer, device_id_type=pl.DeviceIdType.MESH).start()
        for p in range(1, N_DEV):
            peer = _flip(me, p)
            cp = pltpu.make_async_remote_copy(
                src_ref=src_ref, dst_ref=all_ref.at[_lin(peer)], send_sem=send_sems.at[p - 1],
                recv_sem=recv_sems.at[p - 1], device_id=peer, device_id_type=pl.DeviceIdType.MESH)
            cp.wait_recv()
            cp.wait_send()
        acc = all_ref[0]
        for s in range(1, N_DEV):
            acc = acc + all_ref[s]
        out_ref[...] = acc

    vm = pl.BlockSpec(memory_space=pltpu.VMEM)
    return pl.pallas_call(
        body, name=name, in_specs=[vm, pl.BlockSpec(memory_space=pl.ANY)], out_specs=vm,
        out_shape=jax.ShapeDtypeStruct((r, c), F32),
        scratch_shapes=[pltpu.VMEM((N_DEV, r, c), F32), pltpu.SemaphoreType.DMA((N_DEV - 1,)),
                        pltpu.SemaphoreType.DMA((N_DEV - 1,))],
        compiler_params=pltpu.CompilerParams(has_side_effects=True),
    )(buf, after)


def _unshard_cols(g):
    s, l, r, c = g.shape
    return jnp.transpose(g, (1, 2, 0, 3)).reshape(l, r, s * c)


def kernel(x, gdn_w_in, gdn_conv, gdn_a_log, gdn_dt_bias, gdn_onorm, gdn_w_out, hgrn_w_in, hgrn_lb_logits, hgrn_gnorm, hgrn_w_out, norm_mix, norm_mlp, mlp_w_up, mlp_w_down, norm_final, loss_target, m_gdn_w_in, m_gdn_conv, m_gdn_a_log, m_gdn_dt_bias, m_gdn_onorm, m_gdn_w_out, m_hgrn_w_in, m_hgrn_lb_logits, m_hgrn_gnorm, m_hgrn_w_out, m_norm_mix, m_norm_mlp, m_mlp_w_up, m_mlp_w_down, m_norm_final, v_gdn_w_in, v_gdn_conv, v_gdn_a_log, v_gdn_dt_bias, v_gdn_onorm, v_gdn_w_out, v_hgrn_w_in, v_hgrn_lb_logits, v_hgrn_gnorm, v_hgrn_w_out, v_norm_mix, v_norm_mlp, v_mlp_w_up, v_mlp_w_down, v_norm_final):
    seqs, seq_len, d = x.shape
    n = seqs * seq_len
    me_i = _lin(_mesh_pos())
    x2 = x.reshape(n, d)
    target = loss_target.reshape(n, d)
    n_gdn, n_hgrn = gdn_w_in.shape[0], hgrn_w_in.shape[0]

    r_out, r_down = gdn_w_out.shape[1], mlp_w_down.shape[1]
    c_gin, c_hin, c_up = gdn_w_in.shape[2], hgrn_w_in.shape[2], mlp_w_up.shape[2]

    def gathered(w, pick, land_shape, **kw):
        return _Item(w.astype(BF16), land_shape, _whole, pick, **kw)

    wl = GDN_MAIN // N_DEV

    me_1 = me_i.astype(jnp.int32).reshape(1)

    def next_tile(ref, i):
        return ref.at[:, pl.ds(pl.multiple_of(jnp.minimum(i + 1, N_DEV - 1) * wl, AB_PAD), AB_PAD)]

    groups = [[_Item(gdn_conv, (N_DEV,) + gdn_conv.shape, _whole, _slot),
               _Item(hgrn_gnorm, (N_DEV,) + hgrn_gnorm.shape, _whole, _slot)]]
    for i in range(DEPTH):
        j = i // 2
        if i % 2 == 0:
            direct = (SIBLING,) + SAME_CORE if i == 0 else tuple(range(1, N_DEV))
            win = _window(gdn_w_in[j], me_1, wl + AB_PAD, c_gin - wl, f"window_in_{i}")
            groups += [[_Item(win, (N_DEV, d, wl + AB_PAD), _whole, _slot, peers=direct)],
                       [gathered(gdn_w_out[j], _rows_of(r_out), (N_DEV * r_out, d))]]
        else:
            groups += [[gathered(hgrn_w_in[j], _cols_of(c_hin), (d, N_DEV * c_hin))],
                       [gathered(hgrn_w_out[j], _rows_of(r_out), (N_DEV * r_out, d))]]
        groups += [[gathered(mlp_w_up[i], _cols_of(c_up), (d, N_DEV * c_up))],
                   [gathered(mlp_w_down[i], _rows_of(r_down), (N_DEV * r_down, d))]]
    gather_handles, token = _exchange_start(groups, "gather_start")
    lbs = _lb_fwd(hgrn_lb_logits + token[0:1, 0:1], "lb_fwd")

    def arrived(k, after, name):
        return _exchange_wait(gather_handles[k], after, "gather_wait_" + name)

    saved = []
    w_in, w_ab, w_out, w_up, w_down = ([None] * DEPTH for _ in range(5))
    h = x2
    for i in range(DEPTH):
        j = i // 2
        if i == 0:
            g_conv, g_gnorm = arrived(0, h, "small")
            conv_full = _unshard_cols(g_conv)
            gnorm_full = jnp.transpose(g_gnorm, (1, 0, 2)).reshape(n_hgrn, d)
        if i == 0:
            y = _rms_fwd(h, norm_mix[0:1] + token[0:1, 0:1], "rms_mix_0")
        (w_in[i],) = arrived(1 + 4 * i, [y, lbs, conv_full, gnorm_full] if i == 0 else y, f"in_{i}")
        if i == 0:
            w_in[i] = _pass_on_wait(_pass_on_start(w_in[i], "pass_on_start_in_0"), "pass_on_wait_in_0")
        if i % 2 == 0:
            w_in[i], w_ab[i] = _unshard_windows(w_in[i], c_gin, f"gdn_w_in_{i}")
            projm = _mm(y, w_in[i], "nn", [BF16], f"gdn_proj_{i}")
            projab = _mm(y, w_ab[i], "nn", [F32], f"gdn_proj_ab_{i}")
            o2, st_all, conv_y, dinv_all = _gdn_fwd_all(projm, projab, conv_full[j], gdn_a_log[j:j + 1],
                                                    gdn_dt_bias[j:j + 1], gdn_onorm[j:j + 1], seqs, f"gdn_fwd_{i}")
            mix = (projm, projab, conv_y, st_all, dinv_all)
        else:
            proj = _mm(y, w_in[i], "nn", [BF16], f"hgrn_proj_{i}")
            o2, o_raw, st_all = _hgrn_fwd_all(proj, lbs[i:i + 1], gnorm_full[j:j + 1], seqs, f"hgrn_fwd_{i}")
            mix = (proj, o_raw, st_all)
        (w_out[i],) = arrived(2 + 4 * i, o2, f"out_{i}")
        h1, y2 = _mm_rows(o2, w_out[i], "nn", [F32, BF16], f"mix_out_{i}", epilogue=_ep_residual_norm, extras=(h,),
                     vectors=(norm_mlp[i:i + 1],))
        (w_up[i],) = arrived(3 + 4 * i, y2, f"up_{i}")
        u, act = _mm(y2, w_up[i], "nn", [BF16, BF16], f"mlp_up_{i}",
                     epilogue=lambda acc: (acc, jnp.square(jnp.maximum(acc, 0.0))))
        (w_down[i],) = arrived(4 + 4 * i, act, f"down_{i}")
        saved.append((h, y, mix, o2, h1, y2, u, act))
        if i + 1 < DEPTH:
            h, y = _mm_rows(act, w_down[i], "nn", [F32, BF16], f"mlp_down_{i}", epilogue=_ep_residual_norm, extras=(h1,),
                       vectors=(norm_mix[i + 1:i + 2],))
        else:
            h = _mm(act, w_down[i], "nn", [F32], f"mlp_down_{i}", epilogue=lambda acc, res: (res + acc,),
                    extras=(h1,))

    dh, dh_b, d_nf, sq = _loss_head(h, norm_final.reshape(1, d), target, "loss_head")

    d_nmix, d_nmlp = [None] * DEPTH, [None] * DEPTH
    d_conv, d_alog, d_dtb, d_onorm = [None] * n_gdn, [None] * n_gdn, [None] * n_gdn, [None] * n_gdn
    d_lb = [jnp.zeros((1, d), F32)] * DEPTH
    d_gnorm = [None] * n_hgrn
    mlp_handles, mix_handles = [None] * DEPTH, [None] * DEPTH
    token = None
    for i in reversed(range(DEPTH)):
        j = i // 2
        h_in, y, mix, o2, h1, y2, u, act = saved[i]
        g_down = _mm(act, dh_b, "tn", [BF16], f"g_down_{i}", after=token)
        du = _mm(dh_b, w_down[i], "nt", [BF16], f"d_u_{i}",
                 epilogue=lambda acc, uu: (acc * (2.0 * jnp.maximum(uu.astype(F32), 0.0)),), extras=(u,))
        g_up = _mm(y2, du, "tn", [BF16], f"g_up_{i}")
        mlp_handles[i], token = _exchange_start(
            [[_Item(g_down, (N_DEV, r_down, d), _rows_of(r_down), _slot)],
             [_Item(g_up, (N_DEV, d, c_up), _cols_of(c_up), _slot)]], f"scatter_start_mlp_{i}")
        dh1, dh1_b, d_nmlp[i] = _mm_rows(du, w_up[i], "nt", [F32, BF16], f"d_y2_{i}", epilogue=_ep_norm_bwd,
                                     extras=(h1, dh), vectors=(norm_mlp[i:i + 1],), n_sums=1, after=token)
        g_out = _mm(o2, dh1_b, "tn", [BF16], f"g_out_{i}")
        do2 = _mm(dh1_b, w_out[i], "nt", [BF16], f"d_o2_{i}")
        if i % 2 == 0:
            projm, projab, conv_y, st_all, dinv_all = mix
            dpm, dpab, d_conv[j], d_alog[j], d_dtb[j], d_onorm[j] = _gdn_bwd_all(
                projm, projab, conv_y, conv_full[j], gdn_a_log[j:j + 1], gdn_dt_bias[j:j + 1], gdn_onorm[j:j + 1],
                st_all, dinv_all, do2, seqs, f"gdn_bwd_{i}")
            g_main = _mm(y, dpm, "tn", [BF16], f"g_in_{i}")
            g_ab = _mm(y, dpab, "tn", [BF16], f"g_in_ab_{i}")
            in_items = [_Item(g_main, (N_DEV, d, wl), _cols_of(wl), _slot),
                        _Item(g_main, (N_DEV, d, AB_PAD), next_tile, _slot),
                        _Item(g_ab, (N_DEV, d, AB_PAD), _whole, _slot)]
            dp, dy_more = dpm, [(dpab, w_ab[i])]
        else:
            proj, o_raw, st_all = mix
            dp, d_lb[i], d_gnorm[j] = _hgrn_bwd_all(proj, lbs[i:i + 1], gnorm_full[j:j + 1], st_all, o_raw, do2,
                                               seqs, f"hgrn_bwd_{i}")
            g_in = _mm(y, dp, "tn", [BF16], f"g_in_{i}")
            in_items = [_Item(g_in, (N_DEV, d, c_hin), _cols_of(c_hin), _slot)]
            dy_more = []
        mix_handles[i], token = _exchange_start(
            [[_Item(g_out, (N_DEV, r_out, d), _rows_of(r_out), _slot)], in_items], f"scatter_start_mix_{i}")
        dh, dh_b, d_nmix[i] = _mm_rows(dp, w_in[i], "nt", [F32, BF16], f"d_y_{i}", epilogue=_ep_norm_bwd,
                                  extras=(h_in, dh1), vectors=(norm_mix[i:i + 1],), n_sums=1, after=token, more=dy_more)
        token = None
    grad_x = dh.reshape(x.shape)

    def landed(handles, k, layers, after, name):
        return [_exchange_wait(handles[i][k], after, f"scatter_wait_{name}_{i}")[0] for i in layers]

    every, even, odd = range(DEPTH), range(0, DEPTH, 2), range(1, DEPTH, 2)
    upd = {}
    upd["mlp_w_down"] = _adamw_slots(mlp_w_down, landed(mlp_handles, 0, every, dh, "down"), m_mlp_w_down,
                                     v_mlp_w_down, "adamw_mlp_w_down")
    upd["mlp_w_up"] = _adamw_slots(mlp_w_up, landed(mlp_handles, 1, every, upd["mlp_w_down"][1], "up"), m_mlp_w_up,
                                   v_mlp_w_up, "adamw_mlp_w_up")
    upd["hgrn_w_out"] = _adamw_slots(hgrn_w_out, landed(mix_handles, 0, odd, upd["mlp_w_up"][1], "out"),
                                     m_hgrn_w_out, v_hgrn_w_out, "adamw_hgrn_w_out")
    upd["hgrn_w_in"] = _adamw_slots(hgrn_w_in, landed(mix_handles, 1, odd, upd["hgrn_w_out"][1], "in"), m_hgrn_w_in,
                                    v_hgrn_w_in, "adamw_hgrn_w_in")

    dlb_rows = jnp.concatenate(d_lb, axis=0)
    tail = jnp.concatenate(
        [jnp.concatenate(d_onorm, axis=1), jnp.concatenate(d_alog, axis=1), jnp.concatenate(d_dtb, axis=1)], axis=1)
    tail = jnp.pad(tail, ((0, 0), (0, d - tail.shape[1])))
    conv_rows = jnp.stack(d_conv).reshape(-1, d)
    packed = jnp.concatenate(
        [jnp.concatenate(d_nmix, axis=0), jnp.concatenate(d_nmlp, axis=0), d_nf, sq, dlb_rows,
         jnp.concatenate(d_gnorm, axis=0), tail, conv_rows], axis=0)
    pad_rows = (-packed.shape[0]) % 8
    packed = jnp.pad(packed, ((0, pad_rows), (0, 0)))
    tot = _all_reduce_small(packed, upd["hgrn_w_in"][1], "reduce_small")

    upd["gdn_w_out"] = _adamw_slots(gdn_w_out, landed(mix_handles, 0, even, tot, "out"),
                                    m_gdn_w_out, v_gdn_w_out, "adamw_gdn_w_out")
    windows = [_exchange_wait(mix_handles[i][1], upd["gdn_w_out"][1], f"scatter_wait_in_{i}") for i in even]
    upd["gdn_w_in"] = _adamw_windows(gdn_w_in, *([win[k] for win in windows] for k in range(3)),
                                     me_1, m_gdn_w_in, v_gdn_w_in, "adamw_gdn_w_in")

    def update(name, w, g, m, v):
        shape = w.shape
        c = shape[-1]
        res = _adamw(w.reshape(-1, c), g.reshape(-1, c), m.reshape(-1, c), v.reshape(-1, c), "adamw_" + name)
        return [g.reshape(shape)] + [o.reshape(shape) for o in res]

    r0 = 0
    g_nmix = tot[r0:r0 + DEPTH]; r0 += DEPTH
    g_nmlp = tot[r0:r0 + DEPTH]; r0 += DEPTH
    g_nf = tot[r0]; r0 += 1
    loss = tot[r0, 0]; r0 += 1
    g_lb = _lb_bwd(hgrn_lb_logits, tot[r0:r0 + DEPTH], "lb_bwd"); r0 += DEPTH
    g_gnorm_full = tot[r0:r0 + n_hgrn]; r0 += n_hgrn
    t_row = tot[r0]; r0 += 1
    g_conv_full = tot[r0:r0 + n_gdn * CONV_K * 3].reshape(n_gdn, CONV_K, 3 * d)
    g_onorm = t_row[0:n_gdn * HEAD_DIM].reshape(n_gdn, HEAD_DIM)
    o1 = n_gdn * HEAD_DIM
    g_alog = t_row[o1:o1 + n_gdn * N_HEADS].reshape(n_gdn, N_HEADS)
    g_dtb = t_row[o1 + n_gdn * N_HEADS:o1 + 2 * n_gdn * N_HEADS].reshape(n_gdn, N_HEADS)
    c_gn, c_cv = hgrn_gnorm.shape[1], gdn_conv.shape[2]
    g_gnorm = lax.dynamic_slice_in_dim(g_gnorm_full, me_i * c_gn, c_gn, axis=1)
    g_conv = lax.dynamic_slice_in_dim(g_conv_full, me_i * c_cv, c_cv, axis=2)

    upd["gdn_conv"] = update("gdn_conv", gdn_conv, g_conv, m_gdn_conv, v_gdn_conv)
    upd["gdn_a_log"] = update("gdn_a_log", gdn_a_log, g_alog, m_gdn_a_log, v_gdn_a_log)
    upd["gdn_dt_bias"] = update("gdn_dt_bias", gdn_dt_bias, g_dtb, m_gdn_dt_bias, v_gdn_dt_bias)
    upd["gdn_onorm"] = update("gdn_onorm", gdn_onorm, g_onorm, m_gdn_onorm, v_gdn_onorm)
    upd["hgrn_lb_logits"] = update("hgrn_lb_logits", hgrn_lb_logits, g_lb, m_hgrn_lb_logits, v_hgrn_lb_logits)
    upd["hgrn_gnorm"] = update("hgrn_gnorm", hgrn_gnorm, g_gnorm, m_hgrn_gnorm, v_hgrn_gnorm)
    upd["norm_mix"] = update("norm_mix", norm_mix, g_nmix, m_norm_mix, v_norm_mix)
    upd["norm_mlp"] = update("norm_mlp", norm_mlp, g_nmlp, m_norm_mlp, v_norm_mlp)
    upd["norm_final"] = update("norm_final", norm_final, g_nf, m_norm_final, v_norm_final)

    order = ["gdn_w_in", "gdn_conv", "gdn_a_log", "gdn_dt_bias", "gdn_onorm", "gdn_w_out", "hgrn_w_in",
             "hgrn_lb_logits", "hgrn_gnorm", "hgrn_w_out", "norm_mix", "norm_mlp", "mlp_w_up", "mlp_w_down",
             "norm_final"]
    outs = [loss, grad_x]
    for k in range(4):
        outs += [upd[name][k] for name in order]
    return tuple(outs)
```

```python
import functools

import numpy as np
import jax
import jax.numpy as jnp
from jax import lax
from jax.experimental import pallas as pl
from jax.experimental.pallas import tpu as pltpu

F32 = jnp.float32
BF16 = jnp.bfloat16

D_MODEL = 1024
N_HEADS = 8
HEAD_DIM = 128
CHUNK = 64
CONV_K = 4
HALO = 16
EPS = 1e-6
DEPTH = 4
N_DEV = 8
GDN_MAIN = 4 * D_MODEL
AB_PAD = 128
LANE_BLOCK = 256
ROW_BLOCK = 16
BLOCK_UNROLL = 4

ADAM_LR = 0.001
ADAM_B1 = 0.9
ADAM_B2 = 0.999
ADAM_EPS = 1e-08
ADAM_WD = 0.01
ADAM_STEP = 10

VMEM_LIMIT = 56 * 1024 * 1024
MM_TILE = 1024
MM_ROWS_MAX = 2048
MM_VMEM_BUDGET = 40 * 1024 * 1024
MM_ROWS_TILE = 512
_DIMS = {
    "nn": (((1,), (0,)), ((), ())),
    "nt": (((1,), (1,)), ((), ())),
    "tn": (((0,), (0,)), ((), ())),
}


def _parts(x, n):
    if n == 1 and x.dtype == BF16:
        return [x]
    out = []
    r = x.astype(F32)
    for i in range(n):
        p = r.astype(BF16)
        out.append(p)
        if i + 1 < n:
            r = r - p.astype(F32)
    return out


def _dot_raw(a, b, mode, na, nb):
    ap, bp = _parts(a, na), _parts(b, nb)
    nmax = max(na, nb)
    pairs = [(i, j) for i in range(na) for j in range(nb) if i + j < nmax]
    ka = 0 if mode == "tn" else 1
    kb = 1 if mode == "nt" else 0
    xa = ap[0] if len(pairs) == 1 else jnp.concatenate([ap[i] for i, _ in pairs], axis=ka)
    xb = bp[0] if len(pairs) == 1 else jnp.concatenate([bp[j] for _, j in pairs], axis=kb)
    return lax.dot_general(xa, xb, _DIMS[mode], preferred_element_type=F32)


@functools.partial(jax.custom_vjp, nondiff_argnums=(2, 3, 4))
def _dot(a, b, mode, na, nb):
    return _dot_raw(a, b, mode, na, nb)


def _dot_fwd(a, b, mode, na, nb):
    return _dot_raw(a, b, mode, na, nb), (a, b)


def _dot_bwd(mode, na, nb, res, ct):
    a, b = res
    if mode == "nn":
        da = _dot_raw(ct, b, "nt", 1, 1)
        db = _dot_raw(a, ct, "tn", 1, 1)
    elif mode == "nt":
        da = _dot_raw(ct, b, "nn", 1, 1)
        db = _dot_raw(ct, a, "tn", 1, 1)
    else:
        da = _dot_raw(b, ct, "nt", 1, 1)
        db = _dot_raw(a, ct, "nn", 1, 1)
    return da.astype(a.dtype), db.astype(b.dtype)


_dot.defvjp(_dot_fwd, _dot_bwd)


N_EXACT = 3


@jax.custom_vjp
def _dot01(x, m_wide, m):
    return lax.dot_general(m_wide, jnp.concatenate(_parts(x, N_EXACT), axis=0), _DIMS["nn"], preferred_element_type=F32)


def _dot01_fwd(x, m_wide, m):
    return _dot01(x, m_wide, m), (m_wide, m)


def _dot01_bwd(res, ct):
    m_wide, m = res
    dx = lax.dot_general(m, ct.astype(BF16), _DIMS["tn"], preferred_element_type=F32)
    return dx, jnp.zeros_like(m_wide), jnp.zeros_like(m)


_dot01.defvjp(_dot01_fwd, _dot01_bwd)


def _thrice(m):
    return jnp.concatenate([m] * N_EXACT, axis=1).astype(BF16), m.astype(BF16)


def _iota2(shape, dim):
    return lax.broadcasted_iota(jnp.int32, shape, dim)


def _tril_f32(n):
    return (_iota2((n, n), 0) >= _iota2((n, n), 1)).astype(F32)


def _below_block(n, b):
    ri, ci = _iota2((n, n), 0) // b, _iota2((n, n), 1) // b
    return (ri == ci + 1) & (ri % 2 == 1)


def _half_inverses(L):
    n = L.shape[0]
    eye = (_iota2((n, n), 0) == _iota2((n, n), 1)).astype(F32)
    d = eye - jnp.where(_below_block(n, 1), L, 0.0)
    b = 2
    while 2 * b < n:
        e = jnp.where(_below_block(n, b), L, 0.0)
        d = d - _dot_raw(d, _dot_raw(e, d, "nn", 2, 2), "nn", 2, 2)
        b *= 2
    return d, jnp.where(_below_block(n, b), L, 0.0)


def _solve_with(d, e, rhs):
    y = _dot_raw(d, rhs, "nn", 2, 2)
    return y - _dot_raw(d, _dot_raw(e, y, "nn", 2, 2), "nn", 2, 2)


@jax.custom_vjp
def _solve_unit_lower(L, rhs, d):
    n = L.shape[0]
    return _solve_with(d, jnp.where(_below_block(n, n // 2), L, 0.0), rhs)


def _solve_fwd(L, rhs, d):
    n = L.shape[0]
    e = jnp.where(_below_block(n, n // 2), L, 0.0)
    sol = _solve_with(d, e, rhs)
    return sol, (d, e, sol)


def _solve_bwd(res, ct):
    d, e, sol = res
    y = _dot_raw(d, ct - _dot_raw(e, _dot_raw(d, ct, "tn", 2, 2), "tn", 2, 2), "tn", 2, 2)
    return -_dot_raw(y, sol, "nt", 2, 2), y, jnp.zeros_like(d)


_solve_unit_lower.defvjp(_solve_fwd, _solve_bwd)


def _softplus(x):
    return jnp.maximum(x, 0.0) + jnp.log1p(jnp.exp(-jnp.abs(x)))


def _rms(x, w):
    return x * lax.rsqrt(jnp.mean(x * x, axis=-1, keepdims=True) + EPS) * w


HG_LEVELS = (32, 16, 8, 4, 2, 1)


def _hg_level_sums():
    i = np.arange(CHUNK)[:, None]
    m = np.arange(CHUNK)[None, :]
    to_row = [(m <= i) & (m // b == i // b) for b in HG_LEVELS]
    to_col = [(m > i) & (m // b == i // b) for b in HG_LEVELS if b > 1]
    return _thrice(jnp.asarray(np.concatenate(to_row + to_col + [m <= i]), F32))


def _hg_level_masks():
    i = np.arange(CHUNK)[:, None]
    j = np.arange(CHUNK)[None, :]
    return jnp.asarray(np.stack([(i // b == j // b + 1) & ((i // b) % 2 == 1) for b in HG_LEVELS]), F32)


def _hg_pre(qraw, f, lb, sums):
    g = jnp.log(lb + (1.0 - lb) * jax.nn.sigmoid(f))
    k = (1.0 - lb) * jax.nn.sigmoid(-f)
    q = jax.nn.silu(qraw) * (HEAD_DIM ** -0.5)
    return q, k, _dot01(g, *sums)


def _hg_head(st, q, k, v, e, masks):
    nl = len(HG_LEVELS)
    eye = (_iota2((CHUNK, CHUNK), 0) == _iota2((CHUNK, CHUNK), 1)).astype(F32)
    a = eye * jnp.sum(q * k, axis=-1, keepdims=True)
    for l, b in enumerate(HG_LEVELS):
        rows = q * jnp.exp(e[l * CHUNK:(l + 1) * CHUNK])
        cols = k * jnp.exp(e[(nl + l) * CHUNK:(nl + l + 1) * CHUNK]) if b > 1 else k
        a = a + masks[l] * _dot(rows, cols, "nt", 1, 1)
    gc = e[(2 * nl - 1) * CHUNK:2 * nl * CHUNK]
    o = _dot(a, v, "nn", 1, 1) + _dot(q * jnp.exp(gc), st, "nt", 1, 1)
    g_last = gc[CHUNK - 1:CHUNK]
    st_new = st * jnp.exp(g_last) + _dot(v, k * jnp.exp(g_last - gc), "tn", 1, 1)
    return o, st_new


_HG_HEADS = jax.vmap(_hg_head, in_axes=(0, 0, 0, 0, 0, None))


def _hg_post(o, gate, gw):
    return _rms(o, gw) * jax.nn.silu(gate)


def _gd_conv(xp, cw):
    off = HALO - (CONV_K - 1)
    y = cw[0:1] * xp[off:off + CHUNK]
    for kk in range(1, CONV_K):
        y = y + cw[kk:kk + 1] * xp[off + kk:off + kk + CHUNK]
    return y


def _gd_conv_bwd(xp, cw, y, dc):
    off = HALO - (CONV_K - 1)
    sig = jax.nn.sigmoid(y)
    dy = dc * (sig * (1.0 + y * (1.0 - sig)))
    dxp, dcw = None, []
    for kk in range(CONV_K):
        moved = jnp.pad(dy, ((off + kk, HALO - off - kk), (0, 0)))
        term = cw[kk:kk + 1] * moved
        dxp = term if dxp is None else dxp + term
        dcw.append(jnp.sum(xp * moved, axis=0, keepdims=True))
    return dxp, jnp.concatenate(dcw, axis=0)


def _gd_gates(a, b, alog, dtb):
    beta = jax.nn.sigmoid(b)
    g = -jnp.exp(alog) * _softplus(a + dtb)
    expand = (_iota2((N_HEADS, D_MODEL), 1) // HEAD_DIM == _iota2((N_HEADS, D_MODEL), 0)).astype(F32)
    g_x = _dot(g, expand, "nn", 3, 1)
    after = (_iota2((CHUNK, D_MODEL), 0) > _iota2((CHUNK, D_MODEL), 1) % HEAD_DIM).astype(F32)
    sums = _dot01(jnp.concatenate([g_x, g_x * after], axis=1), *_thrice(_tril_f32(CHUNK)))
    return _dot(beta, expand, "nn", 3, 1), sums


def _gd_head(st, q, k, v, beta, gc, diff, gate, onw, dinv=None):
    q = q * lax.rsqrt(jnp.sum(q * q, axis=-1, keepdims=True) + EPS) * (HEAD_DIM ** -0.5)
    k = k * lax.rsqrt(jnp.sum(k * k, axis=-1, keepdims=True) + EPS)
    ri = _iota2((CHUNK, CHUNK), 0)
    ci = _iota2((CHUNK, CHUNK), 1)
    decay = jnp.exp(jnp.where(ri >= ci, diff[:, 0:CHUNK], -jnp.inf))
    kb = k * beta
    egc = jnp.exp(gc)
    L = jnp.where(ri > ci, _dot(kb, k, "nt", 1, 1) * decay, 0.0)
    made = dinv is None
    if made:
        dinv = _half_inverses(L)[0]
    sol = _solve_unit_lower(L, jnp.concatenate([v * beta, kb * egc], axis=1), dinv)
    u = sol[:, 0:HEAD_DIM]
    w = sol[:, HEAD_DIM:2 * HEAD_DIM]
    a_qk = jnp.where(ri >= ci, _dot(q, k, "nt", 1, 1) * decay, 0.0)
    g_last = gc[CHUNK - 1:CHUNK]
    v_new = u - _dot(w, st, "nt", 1, 1)
    o = _dot(q * egc, st, "nt", 1, 1) + _dot(a_qk, v_new, "nn", 1, 1)
    st_new = st * jnp.exp(g_last) + _dot(v_new, k * jnp.exp(g_last - gc), "tn", 1, 1)
    out = (_rms(o, onw) * jax.nn.silu(gate), st_new)
    return out + (dinv,) if made else out


def _params(*sem):
    return pltpu.CompilerParams(dimension_semantics=sem, vmem_limit_bytes=VMEM_LIMIT)


def _tile(n, pref):
    t = min(n, pref)
    assert n % t == 0, (n, pref)
    return t


def _mm_tiles(m, n, k, a_size, b_size, tile_sizes):
    tn = _tile(n, MM_TILE)

    def need(tm, tk):
        acc = 4 * tm * tn * (2 if tk < k else 1)
        return 2 * (tm * tk * a_size + tk * tn * b_size + tm * tn * sum(tile_sizes)) + acc

    tk = k
    while True:
        tm = _tile(m, MM_ROWS_MAX)
        while tm > 256 and need(tm, tk) > MM_VMEM_BUDGET:
            tm //= 2
        if need(tm, tk) <= MM_VMEM_BUDGET or tk <= 512:
            return tm, tn, tk
        tk //= 2


def _mm(a, b, mode, out_dtypes, name, epilogue=None, extras=(), after=None):
    if mode == "nn":
        (m, k), (k2, n) = a.shape, b.shape
    elif mode == "nt":
        (m, k), (n, k2) = a.shape, b.shape
    else:
        (k, m), (k2, n) = a.shape, b.shape
    assert k == k2, (a.shape, b.shape, mode)
    tm, tn, tk = _mm_tiles(m, n, k, a.dtype.itemsize, b.dtype.itemsize,
                           [e.dtype.itemsize for e in extras] + [jnp.dtype(dt).itemsize for dt in out_dtypes])
    nk = k // tk
    ne, no, nafter = len(extras), len(out_dtypes), int(after is not None)
    if epilogue is None:
        epilogue = lambda acc: (acc,)

    def body(*refs):
        a_ref, b_ref = refs[0], refs[1]
        ex = refs[2:2 + ne]
        outs = refs[2 + ne + nafter:2 + ne + nafter + no]
        part = lax.dot_general(a_ref[...].astype(BF16), b_ref[...].astype(BF16), _DIMS[mode],
                               preferred_element_type=F32)

        def finish(acc):
            for o_ref, val in zip(outs, epilogue(acc, *[e[...] for e in ex])):
                o_ref[...] = val.astype(o_ref.dtype)

        if nk == 1:
            finish(part)
        else:
            acc_ref = refs[-1]
            kk = pl.program_id(2)

            @pl.when(kk == 0)
            def _():
                acc_ref[...] = part

            @pl.when(kk > 0)
            def _():
                acc_ref[...] += part

            @pl.when(kk == nk - 1)
            def _():
                finish(acc_ref[...])

    if mode == "tn":
        a_spec = pl.BlockSpec((tk, tm), lambda i, j, kk: (kk, i))
    else:
        a_spec = pl.BlockSpec((tm, tk), lambda i, j, kk: (i, kk))
    if mode == "nt":
        b_spec = pl.BlockSpec((tn, tk), lambda i, j, kk: (j, kk))
    else:
        b_spec = pl.BlockSpec((tk, tn), lambda i, j, kk: (kk, j))
    o_spec = pl.BlockSpec((tm, tn), lambda i, j, kk: (i, j))
    res = pl.pallas_call(
        body,
        name=name,
        grid=(m // tm, n // tn, nk),
        in_specs=[a_spec, b_spec] + [o_spec] * ne + [pl.BlockSpec(memory_space=pl.ANY)] * nafter,
        out_specs=[o_spec] * no,
        out_shape=[jax.ShapeDtypeStruct((m, n), dt) for dt in out_dtypes],
        scratch_shapes=[pltpu.VMEM((tm, tn), F32)] if nk > 1 else [],
        compiler_params=_params("parallel", "parallel", "arbitrary"),
    )(a, b, *extras, *([after] if nafter else []))
    return res[0] if no == 1 else res


def _mm_rows(a, b, mode, out_dtypes, name, epilogue, extras=(), vectors=(), n_sums=0, after=None, more=()):
    assert mode in ("nn", "nt")
    (m, k), n = a.shape, (b.shape[1] if mode == "nn" else b.shape[0])
    tm = _tile(m, MM_ROWS_TILE)
    mt = m // tm
    pairs = [(a, b)] + list(more)
    np_ = 2 * len(pairs)
    ne, no, nafter = len(extras) + len(vectors), len(out_dtypes), int(after is not None)

    def body(*refs):
        ex = refs[np_:np_ + ne]
        outs = refs[np_ + ne + nafter:np_ + ne + nafter + no]
        sums = refs[np_ + ne + nafter + no:np_ + ne + nafter + no + n_sums]
        acc_ref = refs[-1]
        i = pl.program_id(0)

        @pl.when(i == 0)
        def _():
            acc_ref[1] = jnp.zeros((tm, n), F32)

        vals = epilogue(acc_ref[1 - i % 2], *[e[...] for e in ex])
        prods = [lax.dot_general(refs[p][...].astype(BF16), refs[p + 1][...].astype(BF16), _DIMS[mode],
                                 preferred_element_type=F32) for p in range(0, np_, 2)]
        acc = prods[0]
        for prod in prods[1:]:
            acc = acc + prod
        acc_ref[i % 2] = acc
        for o_ref, val in zip(outs, vals[:no]):
            o_ref[...] = val.astype(o_ref.dtype)
        for s_ref, val in zip(sums, vals[no:]):
            @pl.when(i <= 1)
            def _(s_ref=s_ref, val=val):
                s_ref[...] = val

            @pl.when(i > 1)
            def _(s_ref=s_ref, val=val):
                s_ref[...] += val

    ahead = lambda i: (jnp.minimum(i, mt - 1), 0)
    behind = lambda i: (jnp.maximum(i - 1, 0), 0)
    fixed = lambda i: (0, 0)
    row = pl.BlockSpec((tm, n), behind)
    vec = pl.BlockSpec((1, n), fixed)
    res = pl.pallas_call(
        body, name=name, grid=(mt + 1,),
        in_specs=([spec for pa, pb in pairs for spec in (pl.BlockSpec((tm, pa.shape[1]), ahead),
                                                         pl.BlockSpec(pb.shape, fixed))] + [row] * len(extras)
                  + [vec] * len(vectors) + [pl.BlockSpec(memory_space=pl.ANY)] * nafter),
        out_specs=[row] * no + [vec] * n_sums,
        out_shape=[jax.ShapeDtypeStruct((m, n), dt) for dt in out_dtypes] + [jax.ShapeDtypeStruct((1, n), F32)] * n_sums,
        scratch_shapes=[pltpu.VMEM((2, tm, n), F32)],
        compiler_params=_params("arbitrary"),
    )(*[x for pair in pairs for x in pair], *extras, *vectors, *([after] if nafter else []))
    return res[0] if no + n_sums == 1 else res


def _ep_residual_norm(acc, res, w):
    h = res + acc
    return h, _rms(h, w)


def _ep_norm_bwd(acc, x, dres, w):
    r = lax.rsqrt(jnp.mean(x * x, axis=-1, keepdims=True) + EPS)
    g = acc * w
    dx = dres + (r * g - x * (r * r * r * jnp.mean(g * x, axis=-1, keepdims=True)))
    return dx, dx, jnp.sum(acc * (x * r), axis=0, keepdims=True)


def _rms_fwd(x, w, name, tm=512):
    n, d = x.shape
    tm = _tile(n, tm)

    def body(x_ref, w_ref, y_ref):
        y_ref[...] = _rms(x_ref[...], w_ref[...]).astype(y_ref.dtype)

    return pl.pallas_call(
        body, name=name, grid=(n // tm,),
        in_specs=[pl.BlockSpec((tm, d), lambda i: (i, 0)), pl.BlockSpec((1, d), lambda i: (0, 0))],
        out_specs=pl.BlockSpec((tm, d), lambda i: (i, 0)),
        out_shape=jax.ShapeDtypeStruct((n, d), BF16),
        compiler_params=_params("arbitrary"),
    )(x, w)


def _loss_head(h, w, target, name, tm=512):
    n, d = h.shape
    tm = _tile(n, tm)

    def body(h_ref, w_ref, t_ref, dh_ref, dhb_ref, dw_ref, sq_ref):
        y, vjp = jax.vjp(_rms, h_ref[...], w_ref[...])
        err = y - t_ref[...]
        dh, dw = vjp(err * (1.0 / d))
        dh_ref[...] = dh
        dhb_ref[...] = dh.astype(dhb_ref.dtype)
        sq = jnp.sum(err * err, axis=0, keepdims=True)

        @pl.when(pl.program_id(0) == 0)
        def _():
            dw_ref[...] = dw
            sq_ref[...] = sq

        @pl.when(pl.program_id(0) > 0)
        def _():
            dw_ref[...] += dw
            sq_ref[...] += sq

        @pl.when(pl.program_id(0) == n // tm - 1)
        def _():
            total = jnp.sum(sq_ref[...], axis=1, keepdims=True) * (0.5 / d)
            sq_ref[...] = jnp.broadcast_to(total, sq_ref.shape)

    row = pl.BlockSpec((tm, d), lambda i: (i, 0))
    vec = pl.BlockSpec((1, d), lambda i: (0, 0))
    return pl.pallas_call(
        body, name=name, grid=(n // tm,),
        in_specs=[row, vec, row],
        out_specs=[row, row, vec, vec],
        out_shape=[jax.ShapeDtypeStruct((n, d), F32), jax.ShapeDtypeStruct((n, d), BF16),
                   jax.ShapeDtypeStruct((1, d), F32), jax.ShapeDtypeStruct((1, d), F32)],
        compiler_params=_params("arbitrary"),
    )(h, w, target)


def _lower_bounds(logits):
    sm = jax.nn.softmax(logits, axis=0)
    rows = [sm[0:1] * 0.0]
    for r in range(1, DEPTH):
        rows.append(rows[-1] + sm[r:r + 1])
    return jnp.concatenate(rows, axis=0)


def _lb_fwd(logits, name):
    def body(l_ref, o_ref):
        o_ref[...] = _lower_bounds(l_ref[...])

    return pl.pallas_call(body, name=name, out_shape=jax.ShapeDtypeStruct(logits.shape, F32))(logits)


def _lb_bwd(logits, dlb, name):
    def body(l_ref, d_ref, o_ref):
        _, vjp = jax.vjp(_lower_bounds, l_ref[...])
        (o_ref[...],) = vjp(d_ref[...])

    return pl.pallas_call(body, name=name, out_shape=jax.ShapeDtypeStruct(logits.shape, F32))(logits, dlb)


def _head_slice(h):
    return pl.ds(h * HEAD_DIM, HEAD_DIM)


_GD_HEADS = jax.vmap(_gd_head, in_axes=(0, 0, 0, 0, 0, 0, 0, 0, None))
_GD_HEADS_AGAIN = jax.vmap(_gd_head, in_axes=(0, 0, 0, 0, 0, 0, 0, 0, None, 0))


def _lane_blocks(width, block_body):
    def trip(j, carry):
        block_body(lambda base=0: pl.ds(pl.multiple_of(j * LANE_BLOCK + base, LANE_BLOCK), LANE_BLOCK))
        return carry

    lax.fori_loop(0, width // LANE_BLOCK, trip, 0, unroll=BLOCK_UNROLL)


def _row_blocks(rows, block_body):
    def trip(j, carry):
        block_body(pl.ds(pl.multiple_of(j * ROW_BLOCK, ROW_BLOCK), ROW_BLOCK))
        return carry

    lax.fori_loop(0, rows // ROW_BLOCK, trip, 0, unroll=BLOCK_UNROLL)


def _hg_pre_block(p_ref, lb_ref, sums_refs, q_sc, k_sc, v_sc, e_sc, at):
    sl = at()
    q_sc[:, sl], k_sc[:, sl], e_sc[:, sl] = _hg_pre(
        p_ref[:, sl].astype(F32), p_ref[:, at(D_MODEL)].astype(F32), lb_ref[:, sl], [r[...] for r in sums_refs])
    v_sc[:, sl] = p_ref[:, at(2 * D_MODEL)].astype(F32)


def _gd_xp(halo_ref, p_ref, sl, first_chunk):
    halo = jnp.where(first_chunk, 0.0, halo_ref[:, sl].astype(F32))
    return jnp.concatenate([halo, p_ref[:, sl].astype(F32)], axis=0)


def _stack_all(ref, first=0):
    return jnp.stack([ref[s, :, _head_slice(h + first)] for s in range(ref.shape[0]) for h in range(N_HEADS)])


def _unstack_all(ref, val, first=0):
    for s in range(ref.shape[0]):
        for h in range(N_HEADS):
            ref[s, :, _head_slice(h + first)] = val[s * N_HEADS + h].astype(ref.dtype)


def _gdn_fwd_all(projm, projab, cw, alog, dtb, onw, seqs, name):
    n = projm.shape[0]
    t = n // seqs
    nc = t // CHUNK
    d = D_MODEL
    per_halo = CHUNK // HALO
    nh = seqs * N_HEADS

    def body(p_ref, halo_ref, ab_ref, cw_ref, alog_ref, dtb_ref, onw_ref, o2_ref, st_all_ref, y_ref, dinv_ref,
             st_sc, c_sc, beta_sc, g_sc):
        first_chunk = pl.program_id(0) == 0

        @pl.when(first_chunk)
        def _():
            st_sc[...] = jnp.zeros_like(st_sc)

        for s in range(seqs):
            def conv(at, s=s):
                sl = at()
                y = _gd_conv(_gd_xp(halo_ref.at[s], p_ref.at[s], sl, first_chunk), cw_ref[:, sl])
                y_ref[s, :, sl] = y
                c_sc[s, :, sl] = jax.nn.silu(y)

            _lane_blocks(3 * d, conv)
            beta_sc[s], g_sc[s] = _gd_gates(ab_ref[s, :, 0:N_HEADS], ab_ref[s, :, N_HEADS:2 * N_HEADS],
                                            alog_ref[...], dtb_ref[...])
        st_all_ref[0] = st_sc[...]
        o2, st_sc[...], dinv_ref[0] = _GD_HEADS(
            st_sc[...], _stack_all(c_sc), _stack_all(c_sc, N_HEADS), _stack_all(c_sc, 2 * N_HEADS), _stack_all(beta_sc),
            _stack_all(g_sc), _stack_all(g_sc, N_HEADS), _stack_all(p_ref, 3 * N_HEADS).astype(F32), onw_ref[...])
        _unstack_all(o2_ref, o2)

    rows = lambda c: (0, c, 0)
    const = lambda c: (0, 0)
    per_chunk = lambda c: (c, 0, 0, 0)
    p3 = projm.reshape(seqs, t, 4 * d)
    o2, st_all, conv_y, dinv_all = pl.pallas_call(
        body, name=name, grid=(nc,),
        in_specs=[pl.BlockSpec((seqs, CHUNK, 4 * d), rows),
                  pl.BlockSpec((seqs, HALO, 3 * d), lambda c: (0, jnp.maximum(c * per_halo - 1, 0), 0)),
                  pl.BlockSpec((seqs, CHUNK, AB_PAD), rows),
                  pl.BlockSpec((CONV_K, 3 * d), const), pl.BlockSpec((1, N_HEADS), const),
                  pl.BlockSpec((1, N_HEADS), const), pl.BlockSpec((1, HEAD_DIM), const)],
        out_specs=[pl.BlockSpec((seqs, CHUNK, d), rows), pl.BlockSpec((1, nh, HEAD_DIM, HEAD_DIM), per_chunk),
                   pl.BlockSpec((seqs, CHUNK, 3 * d), rows), pl.BlockSpec((1, nh, CHUNK, CHUNK), per_chunk)],
        out_shape=[jax.ShapeDtypeStruct((seqs, t, d), BF16), jax.ShapeDtypeStruct((nc, nh, HEAD_DIM, HEAD_DIM), F32),
                   jax.ShapeDtypeStruct((seqs, t, 3 * d), F32), jax.ShapeDtypeStruct((nc, nh, CHUNK, CHUNK), F32)],
        scratch_shapes=[pltpu.VMEM((nh, HEAD_DIM, HEAD_DIM), F32), pltpu.VMEM((seqs, CHUNK, 3 * d), F32),
                        pltpu.VMEM((seqs, CHUNK, d), F32), pltpu.VMEM((seqs, CHUNK, 2 * d), F32)],
        compiler_params=_params("arbitrary"),
    )(p3, p3, projab.reshape(seqs, t, AB_PAD), cw, alog, dtb, onw)
    return o2.reshape(n, d), st_all, conv_y, dinv_all


def _gdn_bwd_all(projm, projab, conv_y, cw, alog, dtb, onw, st_all, dinv_all, do2, seqs, name):
    n = projm.shape[0]
    t = n // seqs
    nc = t // CHUNK
    d = D_MODEL
    per_halo = CHUNK // HALO
    nh = seqs * N_HEADS

    def body(p_ref, halo_ref, ab_ref, y_ref, cw_ref, alog_ref, dtb_ref, onw_ref, st_all_ref, dinv_ref, do2_ref,
             dp_ref, dab_ref, dcw_ref, dalog_ref, ddtb_ref, donw_ref,
             dst_sc, dhalo_sc, c_sc, beta_sc, g_sc, dc_sc, dbeta_sc, dg_sc):
        first = pl.program_id(0) == 0
        first_chunk = pl.program_id(0) == nc - 1

        @pl.when(first)
        def _():
            dst_sc[...] = jnp.zeros_like(dst_sc)
            dhalo_sc[...] = jnp.zeros_like(dhalo_sc)

        gates_vjps = []
        for s in range(seqs):
            def act(at, s=s):
                c_sc[s, :, at()] = jax.nn.silu(y_ref[s, :, at()])

            _lane_blocks(3 * d, act)
            (beta_sc[s], g_sc[s]), gates_vjp = jax.vjp(
                _gd_gates, ab_ref[s, :, 0:N_HEADS], ab_ref[s, :, N_HEADS:2 * N_HEADS], alog_ref[...], dtb_ref[...])
            gates_vjps.append(gates_vjp)

        dinv = dinv_ref[0]
        _, vjp = jax.vjp(
            lambda *a: _GD_HEADS_AGAIN(*a, dinv), st_all_ref[0], _stack_all(c_sc), _stack_all(c_sc, N_HEADS),
            _stack_all(c_sc, 2 * N_HEADS), _stack_all(beta_sc), _stack_all(g_sc), _stack_all(g_sc, N_HEADS),
            _stack_all(p_ref, 3 * N_HEADS).astype(F32), onw_ref[...])
        dst_sc[...], dq, dk, dv, dbeta, dg, ddiff, dgate, donw = vjp((_stack_all(do2_ref).astype(F32), dst_sc[...]))
        _unstack_all(dc_sc, dq)
        _unstack_all(dc_sc, dk, N_HEADS)
        _unstack_all(dc_sc, dv, 2 * N_HEADS)
        _unstack_all(dbeta_sc, dbeta)
        _unstack_all(dg_sc, dg)
        _unstack_all(dg_sc, ddiff, N_HEADS)
        _unstack_all(dp_ref, dgate, 3 * N_HEADS)

        dalog, ddtb = None, None
        for s in range(seqs):
            def conv_bwd(at, s=s):
                sl = at()
                dxp, dcw = _gd_conv_bwd(_gd_xp(halo_ref.at[s], p_ref.at[s], sl, first_chunk), cw_ref[:, sl],
                                        y_ref[s, :, sl], dc_sc[s, :, sl])
                dqkv = jnp.concatenate([dxp[HALO:CHUNK], dxp[CHUNK:HALO + CHUNK] + dhalo_sc[s, :, sl]], axis=0)
                dp_ref[s, :, sl] = dqkv.astype(dp_ref.dtype)
                dhalo_sc[s, :, sl] = dxp[0:HALO]

                if s > 0:
                    dcw_ref[:, sl] += dcw
                    return

                @pl.when(first)
                def _():
                    dcw_ref[:, sl] = dcw

                @pl.when(jnp.logical_not(first))
                def _():
                    dcw_ref[:, sl] += dcw

            _lane_blocks(3 * d, conv_bwd)
            da, db, dalog_s, ddtb_s = gates_vjps[s]((dbeta_sc[s], dg_sc[s]))
            dab_ref[s] = jnp.concatenate(
                [da, db, jnp.zeros((CHUNK, AB_PAD - 2 * N_HEADS), F32)], axis=1).astype(dab_ref.dtype)
            dalog = dalog_s if dalog is None else dalog + dalog_s
            ddtb = ddtb_s if ddtb is None else ddtb + ddtb_s

        @pl.when(first)
        def _():
            dalog_ref[...] = dalog
            ddtb_ref[...] = ddtb
            donw_ref[...] = donw

        @pl.when(jnp.logical_not(first))
        def _():
            dalog_ref[...] += dalog
            ddtb_ref[...] += ddtb
            donw_ref[...] += donw

    back = lambda c: nc - 1 - c
    rows = lambda c: (0, back(c), 0)
    const = lambda c: (0, 0)
    per_chunk = lambda c: (back(c), 0, 0, 0)
    small = [pl.BlockSpec((CONV_K, 3 * d), const), pl.BlockSpec((1, N_HEADS), const),
             pl.BlockSpec((1, N_HEADS), const), pl.BlockSpec((1, HEAD_DIM), const)]
    p3 = projm.reshape(seqs, t, 4 * d)
    dp, dab, dcw, dalog, ddtb, donw = pl.pallas_call(
        body, name=name, grid=(nc,),
        in_specs=[pl.BlockSpec((seqs, CHUNK, 4 * d), rows),
                  pl.BlockSpec((seqs, HALO, 3 * d), lambda c: (0, jnp.maximum(back(c) * per_halo - 1, 0), 0)),
                  pl.BlockSpec((seqs, CHUNK, AB_PAD), rows), pl.BlockSpec((seqs, CHUNK, 3 * d), rows)] + small + [
                  pl.BlockSpec((1, nh, HEAD_DIM, HEAD_DIM), per_chunk), pl.BlockSpec((1, nh, CHUNK, CHUNK), per_chunk),
                  pl.BlockSpec((seqs, CHUNK, d), rows)],
        out_specs=[pl.BlockSpec((seqs, CHUNK, 4 * d), rows), pl.BlockSpec((seqs, CHUNK, AB_PAD), rows)] + small,
        out_shape=[jax.ShapeDtypeStruct((seqs, t, 4 * d), BF16), jax.ShapeDtypeStruct((seqs, t, AB_PAD), BF16),
                   jax.ShapeDtypeStruct((CONV_K, 3 * d), F32), jax.ShapeDtypeStruct((1, N_HEADS), F32),
                   jax.ShapeDtypeStruct((1, N_HEADS), F32), jax.ShapeDtypeStruct((1, HEAD_DIM), F32)],
        scratch_shapes=[pltpu.VMEM((nh, HEAD_DIM, HEAD_DIM), F32), pltpu.VMEM((seqs, HALO, 3 * d), F32),
                        pltpu.VMEM((seqs, CHUNK, 3 * d), F32), pltpu.VMEM((seqs, CHUNK, d), F32),
                        pltpu.VMEM((seqs, CHUNK, 2 * d), F32), pltpu.VMEM((seqs, CHUNK, 3 * d), F32),
                        pltpu.VMEM((seqs, CHUNK, d), F32), pltpu.VMEM((seqs, CHUNK, 2 * d), F32)],
        compiler_params=_params("arbitrary"),
    )(p3, p3, projab.reshape(seqs, t, AB_PAD), conv_y, cw, alog, dtb, onw, st_all, dinv_all,
      do2.reshape(seqs, t, d))
    return dp.reshape(n, 4 * d), dab.reshape(n, AB_PAD), dcw, dalog, ddtb, donw


def _hgrn_fwd_all(proj, lb, gw, seqs, name):
    n = proj.shape[0]
    t = n // seqs
    nc = t // CHUNK
    d = D_MODEL
    nh = seqs * N_HEADS
    sums, masks = _hg_level_sums(), _hg_level_masks()

    def body(p_ref, lb_ref, gw_ref, sums_wide_ref, sums_once_ref, masks_ref, o2_ref, o_ref, st_all_ref,
             st_sc, q_sc, k_sc, v_sc, e_sc):
        @pl.when(pl.program_id(0) == 0)
        def _():
            st_sc[...] = jnp.zeros_like(st_sc)

        sums_refs = (sums_wide_ref, sums_once_ref)
        for s in range(seqs):
            _lane_blocks(d, functools.partial(_hg_pre_block, p_ref.at[s], lb_ref, sums_refs, q_sc.at[s], k_sc.at[s],
                                              v_sc.at[s], e_sc.at[s]))
        st_all_ref[0] = st_sc[...]
        for s in range(seqs):
            one, mine = pl.ds(s, 1), pl.ds(s * N_HEADS, N_HEADS)
            o, st_sc[mine] = _HG_HEADS(st_sc[mine], *[_stack_all(r.at[one]) for r in (q_sc, k_sc, v_sc, e_sc)],
                                       masks_ref[...])
            _unstack_all(o_ref.at[one], o)

            def post(rows, s=s):
                gate = p_ref[s, rows, 3 * d:4 * d].astype(F32)
                o2_ref[s, rows, :] = _hg_post(o_ref[s, rows, :], gate, gw_ref[...]).astype(o2_ref.dtype)

            _row_blocks(CHUNK, post)

    rows = lambda c: (0, c, 0)
    vec = pl.BlockSpec((1, d), lambda c: (0, 0))
    act = pl.BlockSpec((seqs, CHUNK, d), rows)
    o2, o, st_all = pl.pallas_call(
        body, name=name, grid=(nc,),
        in_specs=[pl.BlockSpec((seqs, CHUNK, 4 * d), rows), vec, vec]
        + [pl.BlockSpec(m.shape, lambda c: (0, 0)) for m in sums] + [pl.BlockSpec(masks.shape, lambda c: (0, 0, 0))],
        out_specs=[act, act, pl.BlockSpec((1, nh, HEAD_DIM, HEAD_DIM), lambda c: (c, 0, 0, 0))],
        out_shape=[jax.ShapeDtypeStruct((seqs, t, d), BF16), jax.ShapeDtypeStruct((seqs, t, d), F32),
                   jax.ShapeDtypeStruct((nc, nh, HEAD_DIM, HEAD_DIM), F32)],
        scratch_shapes=[pltpu.VMEM((nh, HEAD_DIM, HEAD_DIM), F32)] + [pltpu.VMEM((seqs, CHUNK, d), F32)] * 3
        + [pltpu.VMEM((seqs, sums[0].shape[0], d), F32)],
        compiler_params=_params("arbitrary"),
    )(proj.reshape(seqs, t, 4 * d), lb, gw, *sums, masks)
    return o2.reshape(n, d), o, st_all


def _hgrn_bwd_all(proj, lb, gw, st_all, o, do2, seqs, name):
    n = proj.shape[0]
    t = n // seqs
    nc = t // CHUNK
    d = D_MODEL
    nh = seqs * N_HEADS
    sums, masks = _hg_level_sums(), _hg_level_masks()

    def body(p_ref, lb_ref, gw_ref, sums_wide_ref, sums_once_ref, masks_ref, st_all_ref, o_ref, do2_ref,
             dp_ref, dlb_ref, dgw_ref,
             dst_sc, q_sc, k_sc, v_sc, e_sc, do_sc, dq_sc, dk_sc, dv_sc, de_sc, dgw_sc):
        first = pl.program_id(0) == 0

        @pl.when(first)
        def _():
            dst_sc[...] = jnp.zeros_like(dst_sc)

        sums_refs = (sums_wide_ref, sums_once_ref)
        dgw_sc[...] = jnp.zeros_like(dgw_sc)
        for s in range(seqs):
            _lane_blocks(d, functools.partial(_hg_pre_block, p_ref.at[s], lb_ref, sums_refs, q_sc.at[s], k_sc.at[s],
                                              v_sc.at[s], e_sc.at[s]))

            def post_bwd(rows, s=s):
                _, vjp = jax.vjp(_hg_post, o_ref[s, rows, :], p_ref[s, rows, 3 * d:4 * d].astype(F32), gw_ref[...])
                do_sc[s, rows, :], dgate, dgw = vjp(do2_ref[s, rows, :].astype(F32))
                dp_ref[s, rows, 3 * d:4 * d] = dgate.astype(dp_ref.dtype)
                dgw_sc[...] += dgw

            _row_blocks(CHUNK, post_bwd)

        level_masks = masks_ref[...]
        _, vjp = jax.vjp(lambda *a: _HG_HEADS(*a, level_masks), st_all_ref[0],
                         *[_stack_all(r) for r in (q_sc, k_sc, v_sc, e_sc)])
        grads = vjp((_stack_all(do_sc), dst_sc[...]))
        dst_sc[...] = grads[0]
        for r, val in zip((dq_sc, dk_sc, dv_sc, de_sc), grads[1:]):
            _unstack_all(r, val)

        for s in range(seqs):
            def pre_bwd(at, s=s):
                sl = at()
                level_sums = (sums_wide_ref[...], sums_once_ref[...])
                _, vjp = jax.vjp(lambda qraw, f, lb: _hg_pre(qraw, f, lb, level_sums), p_ref[s, :, sl].astype(F32),
                                 p_ref[s, :, at(d)].astype(F32), lb_ref[:, sl])
                dqraw, df, dlb = vjp((dq_sc[s, :, sl], dk_sc[s, :, sl], de_sc[s, :, sl]))
                dp_ref[s, :, sl] = dqraw.astype(dp_ref.dtype)
                dp_ref[s, :, at(d)] = df.astype(dp_ref.dtype)
                dp_ref[s, :, at(2 * d)] = dv_sc[s, :, sl].astype(dp_ref.dtype)
                if s > 0:
                    dlb_ref[:, sl] += dlb
                    return

                @pl.when(first)
                def _():
                    dlb_ref[:, sl] = dlb

                @pl.when(jnp.logical_not(first))
                def _():
                    dlb_ref[:, sl] += dlb

            _lane_blocks(d, pre_bwd)

        @pl.when(first)
        def _():
            dgw_ref[...] = dgw_sc[...]

        @pl.when(jnp.logical_not(first))
        def _():
            dgw_ref[...] += dgw_sc[...]

    rows = lambda c: (0, nc - 1 - c, 0)
    vec = pl.BlockSpec((1, d), lambda c: (0, 0))
    act = pl.BlockSpec((seqs, CHUNK, d), rows)
    wide = pl.BlockSpec((seqs, CHUNK, 4 * d), rows)
    e_rows = sums[0].shape[0]
    dp, dlb, dgw = pl.pallas_call(
        body, name=name, grid=(nc,),
        in_specs=[wide, vec, vec] + [pl.BlockSpec(m.shape, lambda c: (0, 0)) for m in sums] + [
                  pl.BlockSpec(masks.shape, lambda c: (0, 0, 0)),
                  pl.BlockSpec((1, nh, HEAD_DIM, HEAD_DIM), lambda c: (nc - 1 - c, 0, 0, 0)), act, act],
        out_specs=[wide, vec, vec],
        out_shape=[jax.ShapeDtypeStruct((seqs, t, 4 * d), BF16), jax.ShapeDtypeStruct((1, d), F32),
                   jax.ShapeDtypeStruct((1, d), F32)],
        scratch_shapes=[pltpu.VMEM((nh, HEAD_DIM, HEAD_DIM), F32)]
        + [pltpu.VMEM((seqs, CHUNK, d), F32)] * 3 + [pltpu.VMEM((seqs, e_rows, d), F32)]
        + [pltpu.VMEM((seqs, CHUNK, d), F32)] * 4 + [pltpu.VMEM((seqs, e_rows, d), F32), pltpu.VMEM((1, d), F32)],
        compiler_params=_params("arbitrary"),
    )(proj.reshape(seqs, t, 4 * d), lb, gw, *sums, masks, st_all, o, do2.reshape(seqs, t, d))
    return dp.reshape(n, 4 * d), dlb, dgw


def _adam_update(w, g, m, v):
    b1c = 1.0 - ADAM_B1 ** ADAM_STEP
    b2c = 1.0 - ADAM_B2 ** ADAM_STEP
    m_new = ADAM_B1 * m + (1.0 - ADAM_B1) * g
    v_new = ADAM_B2 * v + (1.0 - ADAM_B2) * (g * g)
    delta = -ADAM_LR * ((m_new / b1c) / (jnp.sqrt(v_new / b2c) + ADAM_EPS) + ADAM_WD * w)
    return delta, m_new, v_new


def _adamw(w, g, m, v, name, tr=256):
    r, c = w.shape
    tr = _tile(r, tr)

    def body(w_ref, g_ref, m_ref, v_ref, d_ref, mo_ref, vo_ref):
        d_ref[...], mo_ref[...], vo_ref[...] = _adam_update(w_ref[...], g_ref[...], m_ref[...], v_ref[...])

    blk = pl.BlockSpec((tr, c), lambda i: (i, 0))
    return pl.pallas_call(
        body, name=name, grid=(r // tr,),
        in_specs=[blk] * 4, out_specs=[blk] * 3,
        out_shape=[jax.ShapeDtypeStruct((r, c), F32)] * 3,
        compiler_params=_params("arbitrary"),
    )(w, g, m, v)


def _adamw_slots(w, slot_bufs, m, v, name, tr=256):
    nl, r, c = w.shape
    tr = _tile(r, tr)

    def body(*refs):
        w_ref = refs[0]
        g_refs = refs[1:1 + nl]
        m_ref, v_ref, go_ref, d_ref, mo_ref, vo_ref = refs[1 + nl:]
        for k in range(nl):
            @pl.when(pl.program_id(0) == k)
            def _(k=k):
                g = g_refs[k][0].astype(F32)
                for s in range(1, N_DEV):
                    g = g + g_refs[k][s].astype(F32)
                go_ref[0] = g

        d_ref[0], mo_ref[0], vo_ref[0] = _adam_update(w_ref[0], go_ref[0], m_ref[0], v_ref[0])

    blk = pl.BlockSpec((1, tr, c), lambda l, i: (l, i, 0))
    g_specs = [pl.BlockSpec((N_DEV, tr, c), lambda l, i, k=k: (0, jnp.where(l == k, i, 0), 0)) for k in range(nl)]
    return pl.pallas_call(
        body, name=name, grid=(nl, r // tr),
        in_specs=[blk] + g_specs + [blk, blk], out_specs=[blk] * 4,
        out_shape=[jax.ShapeDtypeStruct((nl, r, c), F32)] * 4,
        compiler_params=_params("arbitrary", "arbitrary"),
    )(w, *slot_bufs, m, v)


def _adamw_windows(w, lo_bufs, hi_bufs, end_bufs, me, m, v, name, tr=256):
    nl, r, c = w.shape
    wl, wh = lo_bufs[0].shape[2], hi_bufs[0].shape[2]
    width, step = wl + wh, c - wl
    assert step >= 0 and (N_DEV - 1) * step + c <= width and N_DEV == 8
    tr = _tile(r, tr)

    def body(*refs):
        me_ref, w_ref = refs[0], refs[1]
        lo_refs, hi_refs, end_refs = refs[2:2 + nl], refs[2 + nl:2 + 2 * nl], refs[2 + 2 * nl:2 + 3 * nl]
        m_ref, v_ref, go_ref, d_ref, mo_ref, vo_ref = refs[2 + 3 * nl:]
        for k in range(nl):
            @pl.when(pl.program_id(0) == k)
            def _(k=k):
                last = me_ref[0] == N_DEV - 1
                hi_of = lambda s: jnp.where(last, end_refs[k][s].astype(F32), hi_refs[k][s].astype(F32))
                lo, hi = lo_refs[k][0].astype(F32), hi_of(0)
                for s in range(1, N_DEV):
                    lo, hi = lo + lo_refs[k][s].astype(F32), hi + hi_of(s)
                g = jnp.concatenate([lo, hi], axis=1)
                for bit in range(3):
                    moved = pltpu.roll(g, width - (step << bit), axis=1)
                    g = jnp.where((me_ref[0] >> bit) & 1 == 1, moved, g)
                go_ref[0] = g[:, :c]

        d_ref[0], mo_ref[0], vo_ref[0] = _adam_update(w_ref[0], go_ref[0], m_ref[0], v_ref[0])

    blk = pl.BlockSpec((1, tr, c), lambda l, i: (l, i, 0))
    g_specs = [pl.BlockSpec((N_DEV, tr, cols), lambda l, i, k=k: (0, jnp.where(l == k, i, 0), 0))
               for cols in (wl, wh, wh) for k in range(nl)]
    return pl.pallas_call(
        body, name=name, grid=(nl, r // tr),
        in_specs=[pl.BlockSpec(memory_space=pltpu.SMEM), blk] + g_specs + [blk, blk], out_specs=[blk] * 4,
        out_shape=[jax.ShapeDtypeStruct((nl, r, c), F32)] * 4,
        compiler_params=_params("arbitrary", "arbitrary"),
    )(me, w, *lo_bufs, *hi_bufs, *end_bufs, m, v)


def _window(w, me, width, step, name, tr=256):
    r, c = w.shape
    assert (N_DEV - 1) * step + c <= width and N_DEV == 8
    tr = _tile(r, tr)

    def body(me_ref, w_ref, out_ref, wide):
        wide[...] = jnp.zeros_like(wide)
        wide[:, 0:c] = w_ref[...]
        g = wide[...]
        for bit in range(3):
            moved = pltpu.roll(g, step << bit, axis=1)
            g = jnp.where((me_ref[0] >> bit) & 1 == 1, moved, g)
        out_ref[...] = g.astype(out_ref.dtype)

    return pl.pallas_call(
        body, name=name, grid=(r // tr,),
        in_specs=[pl.BlockSpec(memory_space=pltpu.SMEM), pl.BlockSpec((tr, c), lambda i: (i, 0))],
        out_specs=pl.BlockSpec((tr, width), lambda i: (i, 0)),
        out_shape=jax.ShapeDtypeStruct((r, width), BF16),
        scratch_shapes=[pltpu.VMEM((tr, width), F32)],
        compiler_params=_params("arbitrary"),
    )(me, w)


def _unshard_windows(win, c, name, tr=256):
    nd, r, w = win.shape
    wl = w - AB_PAD
    step = c - wl
    assert 0 <= step and nd * step <= AB_PAD
    tr = _tile(r, tr)

    def body(win_ref, main_ref, tail_ref):
        lane = lax.broadcasted_iota(jnp.int32, (tr, AB_PAD), 1)

        def past(s):
            return win_ref[s, :, wl:w].astype(F32)

        for s in range(nd):
            first = win_ref[s, :, 0:AB_PAD].astype(F32)
            if s > 0:
                first = jnp.where(lane < s * step, past(s - 1), first)
            main_ref[:, s * wl:s * wl + AB_PAD] = first.astype(main_ref.dtype)
            main_ref[:, s * wl + AB_PAD:(s + 1) * wl] = win_ref[s, :, AB_PAD:wl]
        tail_ref[...] = jnp.where(lane < nd * step, past(nd - 1), 0.0).astype(tail_ref.dtype)

    return pl.pallas_call(
        body, name=name, grid=(r // tr,),
        in_specs=[pl.BlockSpec((nd, tr, w), lambda i: (0, i, 0))],
        out_specs=[pl.BlockSpec((tr, nd * wl), lambda i: (i, 0)), pl.BlockSpec((tr, AB_PAD), lambda i: (i, 0))],
        out_shape=[jax.ShapeDtypeStruct((r, nd * wl), win.dtype), jax.ShapeDtypeStruct((r, AB_PAD), win.dtype)],
        compiler_params=_params("arbitrary"),
    )(win)


def _mesh_pos():
    return lax.axis_index("x"), lax.axis_index("y"), lax.axis_index("c")


def _flip(pos, p):
    x, y, c = pos
    return ((1 - x) if p & 4 else x, (1 - y) if p & 2 else y, (1 - c) if p & 1 else c)


def _lin(pos):
    return 4 * pos[0] + 2 * pos[1] + pos[2]


_HBM = pl.BlockSpec(memory_space=pltpu.HBM)
_SEM = pl.BlockSpec(memory_space=pltpu.SEMAPHORE)
_DATAFLOW = pltpu.SideEffectType.DATAFLOW_SIDE_EFFECTING


class _Item:
    def __init__(self, src, land_shape, src_pick, dst_pick, peers=tuple(range(1, N_DEV))):
        self.src, self.land_shape, self.src_pick, self.dst_pick = src, land_shape, src_pick, dst_pick
        self.peers = peers


def _distinct(arrays):
    found, where = [], []
    for a in arrays:
        hits = [k for k, f in enumerate(found) if f is a]
        where.append(hits[0] if hits else len(found))
        if not hits:
            found.append(a)
    return found, where


def _remote_copies(items, src, land, send_sem, recv_sem, me, arriving):
    me_i = _lin(me)
    out = []
    for it, s_ref, l_ref in zip(items, src, land):
        for p in it.peers:
            peer = _flip(me, p)
            out.append(pltpu.make_async_remote_copy(
                src_ref=it.src_pick(s_ref, _lin(peer)),
                dst_ref=it.dst_pick(l_ref, _lin(peer) if arriving else me_i),
                send_sem=send_sem, recv_sem=recv_sem, device_id=peer, device_id_type=pl.DeviceIdType.MESH))
    return out


def _own_copies(items, src, land, sem, me):
    me_i = _lin(me)
    return [pltpu.make_async_copy(it.src_pick(s_ref, me_i), it.dst_pick(l_ref, me_i), sem)
            for it, s_ref, l_ref in zip(items, src, land)]


def _exchange_start(groups, name):
    items = [it for g in groups for it in g]
    n, ng = len(items), len(groups)
    first = [sum(len(g) for g in groups[:gi]) for gi in range(ng)]
    arrays, where = _distinct([it.src for it in items])
    nu = len(arrays)

    def body(*refs):
        src, land = [refs[k] for k in where], refs[nu:nu + n]
        send_sems, recv_sems = refs[nu + n:nu + n + ng], refs[nu + n + ng:nu + n + 2 * ng]
        token = refs[2 * (nu + n) + 2 * ng]
        me = _mesh_pos()
        for gi, g in enumerate(groups):
            sl = slice(first[gi], first[gi] + len(g))
            for cp in _remote_copies(g, src[sl], land[sl], send_sems[gi], recv_sems[gi], me, arriving=False):
                cp.start()
            for cp in _own_copies(g, src[sl], land[sl], recv_sems[gi], me):
                cp.start()
        token[...] = jnp.zeros_like(token)

    srcs = [pltpu.with_memory_space_constraint(a, pltpu.HBM) for a in arrays]
    lands = [pltpu.with_memory_space_constraint(lax.empty(it.land_shape, it.src.dtype), pltpu.HBM) for it in items]
    res = pl.pallas_call(
        body, name=name,
        out_shape=([pltpu.SemaphoreType.DMA(())] * (2 * ng)
                   + [pltpu.HBM(a.shape, a.dtype) for a in arrays]
                   + [pltpu.HBM(it.land_shape, it.src.dtype) for it in items]
                   + [jax.ShapeDtypeStruct((8, 128), F32)]),
        in_specs=[_HBM] * (nu + n),
        out_specs=[_SEM] * (2 * ng) + [_HBM] * (nu + n) + [pl.BlockSpec(memory_space=pltpu.VMEM)],
        input_output_aliases={i: 2 * ng + i for i in range(nu + n)},
        compiler_params=pltpu.CompilerParams(has_side_effects=_DATAFLOW),
    )(*srcs, *lands)
    send_sems, recv_sems = res[0:ng], res[ng:2 * ng]
    src_thru, land_thru = [res[2 * ng + k] for k in where], res[2 * ng + nu:2 * ng + nu + n]
    handles = []
    for gi, g in enumerate(groups):
        sl = slice(first[gi], first[gi] + len(g))
        handles.append((g, src_thru[sl], land_thru[sl], send_sems[gi], recv_sems[gi]))
    return handles, res[-1]


def _exchange_wait(handle, after, name):
    items, src_thru, land_thru, send_sem, recv_sem = handle
    k = len(items)
    arrays, where = _distinct(src_thru)
    nu = len(arrays)
    afters = list(after) if isinstance(after, (list, tuple)) else [after]

    def body(*refs):
        src, land = [refs[u] for u in where], refs[nu:nu + k]
        send_ref, recv_ref = refs[nu + k], refs[nu + k + 1]
        for cp in _remote_copies(items, src, land, send_ref, recv_ref, _mesh_pos(), arriving=True):
            cp.wait_send()
            cp.wait_recv()
        for cp in _own_copies(items, src, land, recv_ref, _mesh_pos()):
            cp.wait()

    res = pl.pallas_call(
        body, name=name,
        out_shape=([pltpu.HBM(s.shape, s.dtype) for s in arrays] + [pltpu.HBM(l.shape, l.dtype) for l in land_thru]),
        in_specs=[_HBM] * (nu + k) + [_SEM, _SEM] + [pl.BlockSpec(memory_space=pl.ANY)] * len(afters),
        out_specs=[_HBM] * (nu + k),
        input_output_aliases={i: i for i in range(nu + k)},
        compiler_params=pltpu.CompilerParams(has_side_effects=_DATAFLOW),
    )(*arrays, *land_thru, send_sem, recv_sem, *afters)
    return res[nu:nu + k]


SAME_CORE = (2, 4, 6)
SIBLING = 1


def _pass_on_start(buf, name):
    def body(buf_ref, send_sem, recv_sem, thru_ref):
        me = _mesh_pos()
        for p in SAME_CORE:
            slot = buf_ref.at[_lin(_flip(me, p))]
            pltpu.make_async_remote_copy(src_ref=slot, dst_ref=slot, send_sem=send_sem, recv_sem=recv_sem,
                                         device_id=_flip(me, SIBLING), device_id_type=pl.DeviceIdType.MESH).start()

    return pl.pallas_call(
        body, name=name,
        out_shape=[pltpu.SemaphoreType.DMA(()), pltpu.SemaphoreType.DMA(()), pltpu.HBM(buf.shape, buf.dtype)],
        in_specs=[_HBM], out_specs=[_SEM, _SEM, _HBM], input_output_aliases={0: 2},
        compiler_params=pltpu.CompilerParams(has_side_effects=_DATAFLOW),
    )(pltpu.with_memory_space_constraint(buf, pltpu.HBM))


def _pass_on_wait(handle, name):
    send_sem, recv_sem, thru = handle

    def body(buf_ref, send_ref, recv_ref, out_ref):
        me = _mesh_pos()
        sibling = _flip(me, SIBLING)
        for p in SAME_CORE:
            mine, theirs = buf_ref.at[_lin(_flip(me, p))], buf_ref.at[_lin(_flip(sibling, p))]
            cp = pltpu.make_async_remote_copy(src_ref=mine, dst_ref=theirs, send_sem=send_ref, recv_sem=recv_ref,
                                              device_id=sibling, device_id_type=pl.DeviceIdType.MESH)
            cp.wait_send()
            cp.wait_recv()

    return pl.pallas_call(
        body, name=name, out_shape=pltpu.HBM(thru.shape, thru.dtype),
        in_specs=[_HBM, _SEM, _SEM], out_specs=_HBM, input_output_aliases={0: 0},
        compiler_params=pltpu.CompilerParams(has_side_effects=_DATAFLOW),
    )(thru, send_sem, recv_sem)


def _whole(ref, i):
    return ref


def _slot(ref, i):
    return ref.at[i]


def _rows_of(r):
    return lambda ref, i: ref.at[pl.ds(pl.multiple_of(i * r, r), r), :]


def _cols_of(c):
    return lambda ref, i: ref.at[:, pl.ds(pl.multiple_of(i * c, c), c)]


def _all_reduce_small(buf, after, name):
    r, c = buf.shape

    def body(src_ref, after_ref, out_ref, all_ref, send_sems, recv_sems):
        me = _mesh_pos()
        me_i = _lin(me)
        all_ref[me_i] = src_ref[...]
        for p in range(1, N_DEV):
            peer = _flip(me, p)
            pltpu.make_async_remote_copy(
                src_ref=src_ref, dst_ref=all_ref.at[me_i], send_sem=send_sems.at[p - 1], recv_sem=recv_sems.at[p - 1],
                device_id=peer, device_id_type=pl.DeviceIdType.MESH).start()
        for p in range(1, N_DEV):
            peer = _flip(me, p)
            cp = pltpu.make_async_remote_copy(
                src_ref=src_ref, dst_ref=all_ref.at[_lin(peer)], send_sem=send_sems.at[p - 1],
                recv_sem=recv_sems.at[p - 1], device_id=peer, device_id_type=pl.DeviceIdType.MESH)
            cp.wait_recv()
            cp.wait_send()
        acc = all_ref[0]
        for s in range(1, N_DEV):
            acc = acc + all_ref[s]
        out_ref[...] = acc

    vm = pl.BlockSpec(memory_space=pltpu.VMEM)
    return pl.pallas_call(
        body, name=name, in_specs=[vm, pl.BlockSpec(memory_space=pl.ANY)], out_specs=vm,
        out_shape=jax.ShapeDtypeStruct((r, c), F32),
        scratch_shapes=[pltpu.VMEM((N_DEV, r, c), F32), pltpu.SemaphoreType.DMA((N_DEV - 1,)),
                        pltpu.SemaphoreType.DMA((N_DEV - 1,))],
        compiler_params=pltpu.CompilerParams(has_side_effects=True),
    )(buf, after)


def _unshard_cols(g):
    s, l, r, c = g.shape
    return jnp.transpose(g, (1, 2, 0, 3)).reshape(l, r, s * c)


def kernel(x, gdn_w_in, gdn_conv, gdn_a_log, gdn_dt_bias, gdn_onorm, gdn_w_out, hgrn_w_in, hgrn_lb_logits, hgrn_gnorm, hgrn_w_out, norm_mix, norm_mlp, mlp_w_up, mlp_w_down, norm_final, loss_target, m_gdn_w_in, m_gdn_conv, m_gdn_a_log, m_gdn_dt_bias, m_gdn_onorm, m_gdn_w_out, m_hgrn_w_in, m_hgrn_lb_logits, m_hgrn_gnorm, m_hgrn_w_out, m_norm_mix, m_norm_mlp, m_mlp_w_up, m_mlp_w_down, m_norm_final, v_gdn_w_in, v_gdn_conv, v_gdn_a_log, v_gdn_dt_bias, v_gdn_onorm, v_gdn_w_out, v_hgrn_w_in, v_hgrn_lb_logits, v_hgrn_gnorm, v_hgrn_w_out, v_norm_mix, v_norm_mlp, v_mlp_w_up, v_mlp_w_down, v_norm_final):
    seqs, seq_len, d = x.shape
    n = seqs * seq_len
    me_i = _lin(_mesh_pos())
    x2 = x.reshape(n, d)
    target = loss_target.reshape(n, d)
    n_gdn, n_hgrn = gdn_w_in.shape[0], hgrn_w_in.shape[0]

    r_out, r_down = gdn_w_out.shape[1], mlp_w_down.shape[1]
    c_gin, c_hin, c_up = gdn_w_in.shape[2], hgrn_w_in.shape[2], mlp_w_up.shape[2]

    def gathered(w, pick, land_shape, **kw):
        return _Item(w.astype(BF16), land_shape, _whole, pick, **kw)

    wl = GDN_MAIN // N_DEV

    me_1 = me_i.astype(jnp.int32).reshape(1)

    def next_tile(ref, i):
        return ref.at[:, pl.ds(pl.multiple_of(jnp.minimum(i + 1, N_DEV - 1) * wl, AB_PAD), AB_PAD)]

    def window_item(w, i, peers):
        win = _window(w, me_1, wl + AB_PAD, c_gin - wl, f"window_in_{i}")
        return _Item(win, (N_DEV, d, wl + AB_PAD), _whole, _slot, peers=peers)

    first_handles, first_token = _exchange_start(
        [[_Item(gdn_conv, (N_DEV,) + gdn_conv.shape, _whole, _slot),
          _Item(hgrn_gnorm, (N_DEV,) + hgrn_gnorm.shape, _whole, _slot)],
         [window_item(gdn_w_in[0], 0, (SIBLING,) + SAME_CORE)]], "gather_start_first")
    _, (l_gin, l_gout, l_hin, l_hout, l_up, l_down) = lax.optimization_barrier(
        (first_token, (gdn_w_in, gdn_w_out, hgrn_w_in, hgrn_w_out, mlp_w_up, mlp_w_down)))
    groups = []
    for i in range(DEPTH):
        j = i // 2
        if i % 2 == 0:
            if i > 0:
                groups += [[window_item(l_gin[j], i, tuple(range(1, N_DEV)))]]
            groups += [[gathered(l_gout[j], _rows_of(r_out), (N_DEV * r_out, d))]]
        else:
            groups += [[gathered(l_hin[j], _cols_of(c_hin), (d, N_DEV * c_hin))],
                       [gathered(l_hout[j], _rows_of(r_out), (N_DEV * r_out, d))]]
        groups += [[gathered(l_up[i], _cols_of(c_up), (d, N_DEV * c_up))],
                   [gathered(l_down[i], _rows_of(r_down), (N_DEV * r_down, d))]]
    rest_handles, token = _exchange_start(groups, "gather_start")
    gather_handles = first_handles + rest_handles
    lbs = _lb_fwd(hgrn_lb_logits + token[0:1, 0:1], "lb_fwd")

    def arrived(k, after, name):
        return _exchange_wait(gather_handles[k], after, "gather_wait_" + name)

    saved = []
    w_in, w_ab, w_out, w_up, w_down = ([None] * DEPTH for _ in range(5))
    h = x2
    for i in range(DEPTH):
        j = i // 2
        if i == 0:
            g_conv, g_gnorm = arrived(0, h, "small")
            conv_full = _unshard_cols(g_conv)
            gnorm_full = jnp.transpose(g_gnorm, (1, 0, 2)).reshape(n_hgrn, d)
        if i == 0:
            y = _rms_fwd(h, norm_mix[0:1] + token[0:1, 0:1], "rms_mix_0")
        (w_in[i],) = arrived(1 + 4 * i, [y, lbs, conv_full, gnorm_full] if i == 0 else y, f"in_{i}")
        if i == 0:
            w_in[i] = _pass_on_wait(_pass_on_start(w_in[i], "pass_on_start_in_0"), "pass_on_wait_in_0")
        if i % 2 == 0:
            w_in[i], w_ab[i] = _unshard_windows(w_in[i], c_gin, f"gdn_w_in_{i}")
            projm = _mm(y, w_in[i], "nn", [BF16], f"gdn_proj_{i}")
            projab = _mm(y, w_ab[i], "nn", [F32], f"gdn_proj_ab_{i}")
            o2, st_all, conv_y, dinv_all = _gdn_fwd_all(projm, projab, conv_full[j], gdn_a_log[j:j + 1],
                                                    gdn_dt_bias[j:j + 1], gdn_onorm[j:j + 1], seqs, f"gdn_fwd_{i}")
            mix = (projm, projab, conv_y, st_all, dinv_all)
        else:
            proj = _mm(y, w_in[i], "nn", [BF16], f"hgrn_proj_{i}")
            o2, o_raw, st_all = _hgrn_fwd_all(proj, lbs[i:i + 1], gnorm_full[j:j + 1], seqs, f"hgrn_fwd_{i}")
            mix = (proj, o_raw, st_all)
        (w_out[i],) = arrived(2 + 4 * i, o2, f"out_{i}")
        h1, y2 = _mm_rows(o2, w_out[i], "nn", [F32, BF16], f"mix_out_{i}", epilogue=_ep_residual_norm, extras=(h,),
                     vectors=(norm_mlp[i:i + 1],))
        (w_up[i],) = arrived(3 + 4 * i, y2, f"up_{i}")
        u, act = _mm(y2, w_up[i], "nn", [BF16, BF16], f"mlp_up_{i}",
                     epilogue=lambda acc: (acc, jnp.square(jnp.maximum(acc, 0.0))))
        (w_down[i],) = arrived(4 + 4 * i, act, f"down_{i}")
        saved.append((h, y, mix, o2, h1, y2, u, act))
        if i + 1 < DEPTH:
            h, y = _mm_rows(act, w_down[i], "nn", [F32, BF16], f"mlp_down_{i}", epilogue=_ep_residual_norm, extras=(h1,),
                       vectors=(norm_mix[i + 1:i + 2],))
        else:
            h = _mm(act, w_down[i], "nn", [F32], f"mlp_down_{i}", epilogue=lambda acc, res: (res + acc,),
                    extras=(h1,))

    dh, dh_b, d_nf, sq = _loss_head(h, norm_final.reshape(1, d), target, "loss_head")

    d_nmix, d_nmlp = [None] * DEPTH, [None] * DEPTH
    d_conv, d_alog, d_dtb, d_onorm = [None] * n_gdn, [None] * n_gdn, [None] * n_gdn, [None] * n_gdn
    d_lb = [jnp.zeros((1, d), F32)] * DEPTH
    d_gnorm = [None] * n_hgrn
    mlp_handles, mix_handles = [None] * DEPTH, [None] * DEPTH
    token = None
    for i in reversed(range(DEPTH)):
        j = i // 2
        h_in, y, mix, o2, h1, y2, u, act = saved[i]
        g_down = _mm(act, dh_b, "tn", [BF16], f"g_down_{i}", after=token)
        du = _mm(dh_b, w_down[i], "nt", [BF16], f"d_u_{i}",
                 epilogue=lambda acc, uu: (acc * (2.0 * jnp.maximum(uu.astype(F32), 0.0)),), extras=(u,))
        g_up = _mm(y2, du, "tn", [BF16], f"g_up_{i}")
        mlp_handles[i], token = _exchange_start(
            [[_Item(g_down, (N_DEV, r_down, d), _rows_of(r_down), _slot)],
             [_Item(g_up, (N_DEV, d, c_up), _cols_of(c_up), _slot)]], f"scatter_start_mlp_{i}")
        dh1, dh1_b, d_nmlp[i] = _mm_rows(du, w_up[i], "nt", [F32, BF16], f"d_y2_{i}", epilogue=_ep_norm_bwd,
                                     extras=(h1, dh), vectors=(norm_mlp[i:i + 1],), n_sums=1, after=token)
        g_out = _mm(o2, dh1_b, "tn", [BF16], f"g_out_{i}")
        do2 = _mm(dh1_b, w_out[i], "nt", [BF16], f"d_o2_{i}")
        if i % 2 == 0:
            projm, projab, conv_y, st_all, dinv_all = mix
            dpm, dpab, d_conv[j], d_alog[j], d_dtb[j], d_onorm[j] = _gdn_bwd_all(
                projm, projab, conv_y, conv_full[j], gdn_a_log[j:j + 1], gdn_dt_bias[j:j + 1], gdn_onorm[j:j + 1],
                st_all, dinv_all, do2, seqs, f"gdn_bwd_{i}")
            g_main = _mm(y, dpm, "tn", [BF16], f"g_in_{i}")
            g_ab = _mm(y, dpab, "tn", [BF16], f"g_in_ab_{i}")
            in_items = [_Item(g_main, (N_DEV, d, wl), _cols_of(wl), _slot),
                        _Item(g_main, (N_DEV, d, AB_PAD), next_tile, _slot),
                        _Item(g_ab, (N_DEV, d, AB_PAD), _whole, _slot)]
            dp, dy_more = dpm, [(dpab, w_ab[i])]
        else:
            proj, o_raw, st_all = mix
            dp, d_lb[i], d_gnorm[j] = _hgrn_bwd_all(proj, lbs[i:i + 1], gnorm_full[j:j + 1], st_all, o_raw, do2,
                                               seqs, f"hgrn_bwd_{i}")
            g_in = _mm(y, dp, "tn", [BF16], f"g_in_{i}")
            in_items = [_Item(g_in, (N_DEV, d, c_hin), _cols_of(c_hin), _slot)]
            dy_more = []
        mix_handles[i], token = _exchange_start(
            [[_Item(g_out, (N_DEV, r_out, d), _rows_of(r_out), _slot)], in_items], f"scatter_start_mix_{i}")
        dh, dh_b, d_nmix[i] = _mm_rows(dp, w_in[i], "nt", [F32, BF16], f"d_y_{i}", epilogue=_ep_norm_bwd,
                                  extras=(h_in, dh1), vectors=(norm_mix[i:i + 1],), n_sums=1, after=token, more=dy_more)
        token = None
    grad_x = dh.reshape(x.shape)

    def landed(handles, k, layers, after, name):
        return [_exchange_wait(handles[i][k], after, f"scatter_wait_{name}_{i}")[0] for i in layers]

    every, even, odd = range(DEPTH), range(0, DEPTH, 2), range(1, DEPTH, 2)
    upd = {}
    upd["mlp_w_down"] = _adamw_slots(mlp_w_down, landed(mlp_handles, 0, every, dh, "down"), m_mlp_w_down,
                                     v_mlp_w_down, "adamw_mlp_w_down")
    upd["mlp_w_up"] = _adamw_slots(mlp_w_up, landed(mlp_handles, 1, every, upd["mlp_w_down"][1], "up"), m_mlp_w_up,
                                   v_mlp_w_up, "adamw_mlp_w_up")
    upd["hgrn_w_out"] = _adamw_slots(hgrn_w_out, landed(mix_handles, 0, odd, upd["mlp_w_up"][1], "out"),
                                     m_hgrn_w_out, v_hgrn_w_out, "adamw_hgrn_w_out")
    upd["hgrn_w_in"] = _adamw_slots(hgrn_w_in, landed(mix_handles, 1, odd, upd["hgrn_w_out"][1], "in"), m_hgrn_w_in,
                                    v_hgrn_w_in, "adamw_hgrn_w_in")

    dlb_rows = jnp.concatenate(d_lb, axis=0)
    tail = jnp.concatenate(
        [jnp.concatenate(d_onorm, axis=1), jnp.concatenate(d_alog, axis=1), jnp.concatenate(d_dtb, axis=1)], axis=1)
    tail = jnp.pad(tail, ((0, 0), (0, d - tail.shape[1])))
    conv_rows = jnp.stack(d_conv).reshape(-1, d)
    packed = jnp.concatenate(
        [jnp.concatenate(d_nmix, axis=0), jnp.concatenate(d_nmlp, axis=0), d_nf, sq, dlb_rows,
         jnp.concatenate(d_gnorm, axis=0), tail, conv_rows], axis=0)
    pad_rows = (-packed.shape[0]) % 8
    packed = jnp.pad(packed, ((0, pad_rows), (0, 0)))
    tot = _all_reduce_small(packed, upd["hgrn_w_in"][1], "reduce_small")

    upd["gdn_w_out"] = _adamw_slots(gdn_w_out, landed(mix_handles, 0, even, tot, "out"),
                                    m_gdn_w_out, v_gdn_w_out, "adamw_gdn_w_out")
    windows = [_exchange_wait(mix_handles[i][1], upd["gdn_w_out"][1], f"scatter_wait_in_{i}") for i in even]
    upd["gdn_w_in"] = _adamw_windows(gdn_w_in, *([win[k] for win in windows] for k in range(3)),
                                     me_1, m_gdn_w_in, v_gdn_w_in, "adamw_gdn_w_in")

    def update(name, w, g, m, v):
        shape = w.shape
        c = shape[-1]
        res = _adamw(w.reshape(-1, c), g.reshape(-1, c), m.reshape(-1, c), v.reshape(-1, c), "adamw_" + name)
        return [g.reshape(shape)] + [o.reshape(shape) for o in res]

    r0 = 0
    g_nmix = tot[r0:r0 + DEPTH]; r0 += DEPTH
    g_nmlp = tot[r0:r0 + DEPTH]; r0 += DEPTH
    g_nf = tot[r0]; r0 += 1
    loss = tot[r0, 0]; r0 += 1
    g_lb = _lb_bwd(hgrn_lb_logits, tot[r0:r0 + DEPTH], "lb_bwd"); r0 += DEPTH
    g_gnorm_full = tot[r0:r0 + n_hgrn]; r0 += n_hgrn
    t_row = tot[r0]; r0 += 1
    g_conv_full = tot[r0:r0 + n_gdn * CONV_K * 3].reshape(n_gdn, CONV_K, 3 * d)
    g_onorm = t_row[0:n_gdn * HEAD_DIM].reshape(n_gdn, HEAD_DIM)
    o1 = n_gdn * HEAD_DIM
    g_alog = t_row[o1:o1 + n_gdn * N_HEADS].reshape(n_gdn, N_HEADS)
    g_dtb = t_row[o1 + n_gdn * N_HEADS:o1 + 2 * n_gdn * N_HEADS].reshape(n_gdn, N_HEADS)
    c_gn, c_cv = hgrn_gnorm.shape[1], gdn_conv.shape[2]
    g_gnorm = lax.dynamic_slice_in_dim(g_gnorm_full, me_i * c_gn, c_gn, axis=1)
    g_conv = lax.dynamic_slice_in_dim(g_conv_full, me_i * c_cv, c_cv, axis=2)

    upd["gdn_conv"] = update("gdn_conv", gdn_conv, g_conv, m_gdn_conv, v_gdn_conv)
    upd["gdn_a_log"] = update("gdn_a_log", gdn_a_log, g_alog, m_gdn_a_log, v_gdn_a_log)
    upd["gdn_dt_bias"] = update("gdn_dt_bias", gdn_dt_bias, g_dtb, m_gdn_dt_bias, v_gdn_dt_bias)
    upd["gdn_onorm"] = update("gdn_onorm", gdn_onorm, g_onorm, m_gdn_onorm, v_gdn_onorm)
    upd["hgrn_lb_logits"] = update("hgrn_lb_logits", hgrn_lb_logits, g_lb, m_hgrn_lb_logits, v_hgrn_lb_logits)
    upd["hgrn_gnorm"] = update("hgrn_gnorm", hgrn_gnorm, g_gnorm, m_hgrn_gnorm, v_hgrn_gnorm)
    upd["norm_mix"] = update("norm_mix", norm_mix, g_nmix, m_norm_mix, v_norm_mix)
    upd["norm_mlp"] = update("norm_mlp", norm_mlp, g_nmlp, m_norm_mlp, v_norm_mlp)
    upd["norm_final"] = update("norm_final", norm_final, g_nf, m_norm_final, v_norm_final)

    order = ["gdn_w_in", "gdn_conv", "gdn_a_log", "gdn_dt_bias", "gdn_onorm", "gdn_w_out", "hgrn_w_in",
             "hgrn_lb_logits", "hgrn_gnorm", "hgrn_w_out", "norm_mix", "norm_mlp", "mlp_w_up", "mlp_w_down",
             "norm_final"]
    outs = [loss, grad_x]
    for k in range(4):
        outs += [upd[name][k] for name in order]
    return tuple(outs)
```

```python
import functools

import numpy as np
import jax
import jax.numpy as jnp
from jax import lax
from jax.experimental import pallas as pl
from jax.experimental.pallas import tpu as pltpu

F32 = jnp.float32
BF16 = jnp.bfloat16

D_MODEL = 1024
N_HEADS = 8
HEAD_DIM = 128
CHUNK = 64
CONV_K = 4
HALO = 16
EPS = 1e-6
DEPTH = 4
N_DEV = 8
GDN_MAIN = 4 * D_MODEL
AB_PAD = 128
LANE_BLOCK = 256
ROW_BLOCK = 16
BLOCK_UNROLL = 4

ADAM_LR = 0.001
ADAM_B1 = 0.9
ADAM_B2 = 0.999
ADAM_EPS = 1e-08
ADAM_WD = 0.01
ADAM_STEP = 10

VMEM_LIMIT = 56 * 1024 * 1024
MM_TILE = 1024
MM_ROWS_MAX = 2048
MM_VMEM_BUDGET = 40 * 1024 * 1024
MM_ROWS_TILE = 512
_DIMS = {
    "nn": (((1,), (0,)), ((), ())),
    "nt": (((1,), (1,)), ((), ())),
    "tn": (((0,), (0,)), ((), ())),
}


def _parts(x, n):
    if n == 1 and x.dtype == BF16:
        return [x]
    out = []
    r = x.astype(F32)
    for i in range(n):
        p = r.astype(BF16)
        out.append(p)
        if i + 1 < n:
            r = r - p.astype(F32)
    return out


def _dot_raw(a, b, mode, na, nb):
    ap, bp = _parts(a, na), _parts(b, nb)
    nmax = max(na, nb)
    pairs = [(i, j) for i in range(na) for j in range(nb) if i + j < nmax]
    ka = 0 if mode == "tn" else 1
    kb = 1 if mode == "nt" else 0
    xa = ap[0] if len(pairs) == 1 else jnp.concatenate([ap[i] for i, _ in pairs], axis=ka)
    xb = bp[0] if len(pairs) == 1 else jnp.concatenate([bp[j] for _, j in pairs], axis=kb)
    return lax.dot_general(xa, xb, _DIMS[mode], preferred_element_type=F32)


@functools.partial(jax.custom_vjp, nondiff_argnums=(2, 3, 4))
def _dot(a, b, mode, na, nb):
    return _dot_raw(a, b, mode, na, nb)


def _dot_fwd(a, b, mode, na, nb):
    return _dot_raw(a, b, mode, na, nb), (a, b)


def _dot_bwd(mode, na, nb, res, ct):
    a, b = res
    if mode == "nn":
        da = _dot_raw(ct, b, "nt", 1, 1)
        db = _dot_raw(a, ct, "tn", 1, 1)
    elif mode == "nt":
        da = _dot_raw(ct, b, "nn", 1, 1)
        db = _dot_raw(ct, a, "tn", 1, 1)
    else:
        da = _dot_raw(b, ct, "nt", 1, 1)
        db = _dot_raw(a, ct, "nn", 1, 1)
    return da.astype(a.dtype), db.astype(b.dtype)


_dot.defvjp(_dot_fwd, _dot_bwd)


N_EXACT = 3


@jax.custom_vjp
def _dot01(x, m_wide, m):
    return lax.dot_general(m_wide, jnp.concatenate(_parts(x, N_EXACT), axis=0), _DIMS["nn"], preferred_element_type=F32)


def _dot01_fwd(x, m_wide, m):
    return _dot01(x, m_wide, m), (m_wide, m)


def _dot01_bwd(res, ct):
    m_wide, m = res
    dx = lax.dot_general(m, ct.astype(BF16), _DIMS["tn"], preferred_element_type=F32)
    return dx, jnp.zeros_like(m_wide), jnp.zeros_like(m)


_dot01.defvjp(_dot01_fwd, _dot01_bwd)


def _thrice(m):
    return jnp.concatenate([m] * N_EXACT, axis=1).astype(BF16), m.astype(BF16)


def _iota2(shape, dim):
    return lax.broadcasted_iota(jnp.int32, shape, dim)


def _tril_f32(n):
    return (_iota2((n, n), 0) >= _iota2((n, n), 1)).astype(F32)


def _below_block(n, b):
    ri, ci = _iota2((n, n), 0) // b, _iota2((n, n), 1) // b
    return (ri == ci + 1) & (ri % 2 == 1)


def _half_inverses(L):
    n = L.shape[0]
    eye = (_iota2((n, n), 0) == _iota2((n, n), 1)).astype(F32)
    d = eye - jnp.where(_below_block(n, 1), L, 0.0)
    b = 2
    while 2 * b < n:
        e = jnp.where(_below_block(n, b), L, 0.0)
        d = d - _dot_raw(d, _dot_raw(e, d, "nn", 2, 2), "nn", 2, 2)
        b *= 2
    return d, jnp.where(_below_block(n, b), L, 0.0)


def _solve_with(d, e, rhs):
    y = _dot_raw(d, rhs, "nn", 2, 2)
    return y - _dot_raw(d, _dot_raw(e, y, "nn", 2, 2), "nn", 2, 2)


@jax.custom_vjp
def _solve_unit_lower(L, rhs, d):
    n = L.shape[0]
    return _solve_with(d, jnp.where(_below_block(n, n // 2), L, 0.0), rhs)


def _solve_fwd(L, rhs, d):
    n = L.shape[0]
    e = jnp.where(_below_block(n, n // 2), L, 0.0)
    sol = _solve_with(d, e, rhs)
    return sol, (d, e, sol)


def _solve_bwd(res, ct):
    d, e, sol = res
    y = _dot_raw(d, ct - _dot_raw(e, _dot_raw(d, ct, "tn", 2, 2), "tn", 2, 2), "tn", 2, 2)
    return -_dot_raw(y, sol, "nt", 2, 2), y, jnp.zeros_like(d)


_solve_unit_lower.defvjp(_solve_fwd, _solve_bwd)


def _softplus(x):
    return jnp.maximum(x, 0.0) + jnp.log1p(jnp.exp(-jnp.abs(x)))


def _rms(x, w):
    return x * lax.rsqrt(jnp.mean(x * x, axis=-1, keepdims=True) + EPS) * w


HG_LEVELS = (32, 16, 8, 4, 2, 1)


def _hg_level_sums():
    i = np.arange(CHUNK)[:, None]
    m = np.arange(CHUNK)[None, :]
    to_row = [(m <= i) & (m // b == i // b) for b in HG_LEVELS]
    to_col = [(m > i) & (m // b == i // b) for b in HG_LEVELS if b > 1]
    return _thrice(jnp.asarray(np.concatenate(to_row + to_col + [m <= i]), F32))


def _hg_level_masks():
    i = np.arange(CHUNK)[:, None]
    j = np.arange(CHUNK)[None, :]
    return jnp.asarray(np.stack([(i // b == j // b + 1) & ((i // b) % 2 == 1) for b in HG_LEVELS]), F32)


def _hg_pre(qraw, f, lb, sums):
    g = jnp.log(lb + (1.0 - lb) * jax.nn.sigmoid(f))
    k = (1.0 - lb) * jax.nn.sigmoid(-f)
    q = jax.nn.silu(qraw) * (HEAD_DIM ** -0.5)
    return q, k, _dot01(g, *sums)


def _hg_head(st, q, k, v, e, masks):
    nl = len(HG_LEVELS)
    eye = (_iota2((CHUNK, CHUNK), 0) == _iota2((CHUNK, CHUNK), 1)).astype(F32)
    a = eye * jnp.sum(q * k, axis=-1, keepdims=True)
    for l, b in enumerate(HG_LEVELS):
        rows = q * jnp.exp(e[l * CHUNK:(l + 1) * CHUNK])
        cols = k * jnp.exp(e[(nl + l) * CHUNK:(nl + l + 1) * CHUNK]) if b > 1 else k
        a = a + masks[l] * _dot(rows, cols, "nt", 1, 1)
    gc = e[(2 * nl - 1) * CHUNK:2 * nl * CHUNK]
    o = _dot(a, v, "nn", 1, 1) + _dot(q * jnp.exp(gc), st, "nt", 1, 1)
    g_last = gc[CHUNK - 1:CHUNK]
    st_new = st * jnp.exp(g_last) + _dot(v, k * jnp.exp(g_last - gc), "tn", 1, 1)
    return o, st_new


_HG_HEADS = jax.vmap(_hg_head, in_axes=(0, 0, 0, 0, 0, None))


def _hg_post(o, gate, gw):
    return _rms(o, gw) * jax.nn.silu(gate)


def _gd_conv(xp, cw):
    off = HALO - (CONV_K - 1)
    y = cw[0:1] * xp[off:off + CHUNK]
    for kk in range(1, CONV_K):
        y = y + cw[kk:kk + 1] * xp[off + kk:off + kk + CHUNK]
    return y


def _gd_conv_bwd(xp, cw, y, dc):
    off = HALO - (CONV_K - 1)
    sig = jax.nn.sigmoid(y)
    dy = dc * (sig * (1.0 + y * (1.0 - sig)))
    dxp, dcw = None, []
    for kk in range(CONV_K):
        moved = jnp.pad(dy, ((off + kk, HALO - off - kk), (0, 0)))
        term = cw[kk:kk + 1] * moved
        dxp = term if dxp is None else dxp + term
        dcw.append(jnp.sum(xp * moved, axis=0, keepdims=True))
    return dxp, jnp.concatenate(dcw, axis=0)


def _gd_gates(a, b, alog, dtb):
    beta = jax.nn.sigmoid(b)
    g = -jnp.exp(alog) * _softplus(a + dtb)
    expand = (_iota2((N_HEADS, D_MODEL), 1) // HEAD_DIM == _iota2((N_HEADS, D_MODEL), 0)).astype(F32)
    g_x = _dot(g, expand, "nn", 3, 1)
    after = (_iota2((CHUNK, D_MODEL), 0) > _iota2((CHUNK, D_MODEL), 1) % HEAD_DIM).astype(F32)
    sums = _dot01(jnp.concatenate([g_x, g_x * after], axis=1), *_thrice(_tril_f32(CHUNK)))
    return _dot(beta, expand, "nn", 3, 1), sums


def _gd_head(st, q, k, v, beta, gc, diff, gate, onw, dinv=None):
    q = q * lax.rsqrt(jnp.sum(q * q, axis=-1, keepdims=True) + EPS) * (HEAD_DIM ** -0.5)
    k = k * lax.rsqrt(jnp.sum(k * k, axis=-1, keepdims=True) + EPS)
    ri = _iota2((CHUNK, CHUNK), 0)
    ci = _iota2((CHUNK, CHUNK), 1)
    decay = jnp.exp(jnp.where(ri >= ci, diff[:, 0:CHUNK], -jnp.inf))
    kb = k * beta
    egc = jnp.exp(gc)
    L = jnp.where(ri > ci, _dot(kb, k, "nt", 1, 1) * decay, 0.0)
    made = dinv is None
    if made:
        dinv = _half_inverses(L)[0]
    sol = _solve_unit_lower(L, jnp.concatenate([v * beta, kb * egc], axis=1), dinv)
    u = sol[:, 0:HEAD_DIM]
    w = sol[:, HEAD_DIM:2 * HEAD_DIM]
    a_qk = jnp.where(ri >= ci, _dot(q, k, "nt", 1, 1) * decay, 0.0)
    g_last = gc[CHUNK - 1:CHUNK]
    v_new = u - _dot(w, st, "nt", 1, 1)
    o = _dot(q * egc, st, "nt", 1, 1) + _dot(a_qk, v_new, "nn", 1, 1)
    st_new = st * jnp.exp(g_last) + _dot(v_new, k * jnp.exp(g_last - gc), "tn", 1, 1)
    out = (_rms(o, onw) * jax.nn.silu(gate), st_new)
    return out + (dinv,) if made else out


def _params(*sem):
    return pltpu.CompilerParams(dimension_semantics=sem, vmem_limit_bytes=VMEM_LIMIT)


def _tile(n, pref):
    t = min(n, pref)
    assert n % t == 0, (n, pref)
    return t


def _mm_tiles(m, n, k, a_size, b_size, tile_sizes):
    tn = _tile(n, MM_TILE)

    def need(tm, tk):
        acc = 4 * tm * tn * (2 if tk < k else 1)
        return 2 * (tm * tk * a_size + tk * tn * b_size + tm * tn * sum(tile_sizes)) + acc

    tk = k
    while True:
        tm = _tile(m, MM_ROWS_MAX)
        while tm > 256 and need(tm, tk) > MM_VMEM_BUDGET:
            tm //= 2
        if need(tm, tk) <= MM_VMEM_BUDGET or tk <= 512:
            return tm, tn, tk
        tk //= 2


def _mm(a, b, mode, out_dtypes, name, epilogue=None, extras=(), after=None):
    if mode == "nn":
        (m, k), (k2, n) = a.shape, b.shape
    elif mode == "nt":
        (m, k), (n, k2) = a.shape, b.shape
    else:
        (k, m), (k2, n) = a.shape, b.shape
    assert k == k2, (a.shape, b.shape, mode)
    tm, tn, tk = _mm_tiles(m, n, k, a.dtype.itemsize, b.dtype.itemsize,
                           [e.dtype.itemsize for e in extras] + [jnp.dtype(dt).itemsize for dt in out_dtypes])
    nk = k // tk
    ne, no, nafter = len(extras), len(out_dtypes), int(after is not None)
    if epilogue is None:
        epilogue = lambda acc: (acc,)

    def body(*refs):
        a_ref, b_ref = refs[0], refs[1]
        ex = refs[2:2 + ne]
        outs = refs[2 + ne + nafter:2 + ne + nafter + no]
        part = lax.dot_general(a_ref[...].astype(BF16), b_ref[...].astype(BF16), _DIMS[mode],
                               preferred_element_type=F32)

        def finish(acc):
            for o_ref, val in zip(outs, epilogue(acc, *[e[...] for e in ex])):
                o_ref[...] = val.astype(o_ref.dtype)

        if nk == 1:
            finish(part)
        else:
            acc_ref = refs[-1]
            kk = pl.program_id(2)

            @pl.when(kk == 0)
            def _():
                acc_ref[...] = part

            @pl.when(kk > 0)
            def _():
                acc_ref[...] += part

            @pl.when(kk == nk - 1)
            def _():
                finish(acc_ref[...])

    if mode == "tn":
        a_spec = pl.BlockSpec((tk, tm), lambda i, j, kk: (kk, i))
    else:
        a_spec = pl.BlockSpec((tm, tk), lambda i, j, kk: (i, kk))
    if mode == "nt":
        b_spec = pl.BlockSpec((tn, tk), lambda i, j, kk: (j, kk))
    else:
        b_spec = pl.BlockSpec((tk, tn), lambda i, j, kk: (kk, j))
    o_spec = pl.BlockSpec((tm, tn), lambda i, j, kk: (i, j))
    res = pl.pallas_call(
        body,
        name=name,
        grid=(m // tm, n // tn, nk),
        in_specs=[a_spec, b_spec] + [o_spec] * ne + [pl.BlockSpec(memory_space=pl.ANY)] * nafter,
        out_specs=[o_spec] * no,
        out_shape=[jax.ShapeDtypeStruct((m, n), dt) for dt in out_dtypes],
        scratch_shapes=[pltpu.VMEM((tm, tn), F32)] if nk > 1 else [],
        compiler_params=_params("parallel", "parallel", "arbitrary"),
    )(a, b, *extras, *([after] if nafter else []))
    return res[0] if no == 1 else res


def _mm_rows(a, b, mode, out_dtypes, name, epilogue, extras=(), vectors=(), n_sums=0, after=None, more=()):
    assert mode in ("nn", "nt")
    (m, k), n = a.shape, (b.shape[1] if mode == "nn" else b.shape[0])
    tm = _tile(m, MM_ROWS_TILE)
    mt = m // tm
    pairs = [(a, b)] + list(more)
    np_ = 2 * len(pairs)
    ne, no, nafter = len(extras) + len(vectors), len(out_dtypes), int(after is not None)

    def body(*refs):
        ex = refs[np_:np_ + ne]
        outs = refs[np_ + ne + nafter:np_ + ne + nafter + no]
        sums = refs[np_ + ne + nafter + no:np_ + ne + nafter + no + n_sums]
        acc_ref = refs[-1]
        i = pl.program_id(0)

        @pl.when(i == 0)
        def _():
            acc_ref[1] = jnp.zeros((tm, n), F32)

        vals = epilogue(acc_ref[1 - i % 2], *[e[...] for e in ex])
        prods = [lax.dot_general(refs[p][...].astype(BF16), refs[p + 1][...].astype(BF16), _DIMS[mode],
                                 preferred_element_type=F32) for p in range(0, np_, 2)]
        acc = prods[0]
        for prod in prods[1:]:
            acc = acc + prod
        acc_ref[i % 2] = acc
        for o_ref, val in zip(outs, vals[:no]):
            o_ref[...] = val.astype(o_ref.dtype)
        for s_ref, val in zip(sums, vals[no:]):
            @pl.when(i <= 1)
            def _(s_ref=s_ref, val=val):
                s_ref[...] = val

            @pl.when(i > 1)
            def _(s_ref=s_ref, val=val):
                s_ref[...] += val

    ahead = lambda i: (jnp.minimum(i, mt - 1), 0)
    behind = lambda i: (jnp.maximum(i - 1, 0), 0)
    fixed = lambda i: (0, 0)
    row = pl.BlockSpec((tm, n), behind)
    vec = pl.BlockSpec((1, n), fixed)
    res = pl.pallas_call(
        body, name=name, grid=(mt + 1,),
        in_specs=([spec for pa, pb in pairs for spec in (pl.BlockSpec((tm, pa.shape[1]), ahead),
                                                         pl.BlockSpec(pb.shape, fixed))] + [row] * len(extras)
                  + [vec] * len(vectors) + [pl.BlockSpec(memory_space=pl.ANY)] * nafter),
        out_specs=[row] * no + [vec] * n_sums,
        out_shape=[jax.ShapeDtypeStruct((m, n), dt) for dt in out_dtypes] + [jax.ShapeDtypeStruct((1, n), F32)] * n_sums,
        scratch_shapes=[pltpu.VMEM((2, tm, n), F32)],
        compiler_params=_params("arbitrary"),
    )(*[x for pair in pairs for x in pair], *extras, *vectors, *([after] if nafter else []))
    return res[0] if no + n_sums == 1 else res


def _ep_residual_norm(acc, res, w):
    h = res + acc
    return h, _rms(h, w)


def _ep_norm_bwd(acc, x, dres, w):
    r = lax.rsqrt(jnp.mean(x * x, axis=-1, keepdims=True) + EPS)
    g = acc * w
    dx = dres + (r * g - x * (r * r * r * jnp.mean(g * x, axis=-1, keepdims=True)))
    return dx, dx, jnp.sum(acc * (x * r), axis=0, keepdims=True)


def _rms_fwd(x, w, name, tm=512):
    n, d = x.shape
    tm = _tile(n, tm)

    def body(x_ref, w_ref, y_ref):
        y_ref[...] = _rms(x_ref[...], w_ref[...]).astype(y_ref.dtype)

    return pl.pallas_call(
        body, name=name, grid=(n // tm,),
        in_specs=[pl.BlockSpec((tm, d), lambda i: (i, 0)), pl.BlockSpec((1, d), lambda i: (0, 0))],
        out_specs=pl.BlockSpec((tm, d), lambda i: (i, 0)),
        out_shape=jax.ShapeDtypeStruct((n, d), BF16),
        compiler_params=_params("arbitrary"),
    )(x, w)


def _loss_head(h, w, target, name, tm=512):
    n, d = h.shape
    tm = _tile(n, tm)

    def body(h_ref, w_ref, t_ref, dh_ref, dhb_ref, dw_ref, sq_ref):
        y, vjp = jax.vjp(_rms, h_ref[...], w_ref[...])
        err = y - t_ref[...]
        dh, dw = vjp(err * (1.0 / d))
        dh_ref[...] = dh
        dhb_ref[...] = dh.astype(dhb_ref.dtype)
        sq = jnp.sum(err * err, axis=0, keepdims=True)

        @pl.when(pl.program_id(0) == 0)
        def _():
            dw_ref[...] = dw
            sq_ref[...] = sq

        @pl.when(pl.program_id(0) > 0)
        def _():
            dw_ref[...] += dw
            sq_ref[...] += sq

        @pl.when(pl.program_id(0) == n // tm - 1)
        def _():
            total = jnp.sum(sq_ref[...], axis=1, keepdims=True) * (0.5 / d)
            sq_ref[...] = jnp.broadcast_to(total, sq_ref.shape)

    row = pl.BlockSpec((tm, d), lambda i: (i, 0))
    vec = pl.BlockSpec((1, d), lambda i: (0, 0))
    return pl.pallas_call(
        body, name=name, grid=(n // tm,),
        in_specs=[row, vec, row],
        out_specs=[row, row, vec, vec],
        out_shape=[jax.ShapeDtypeStruct((n, d), F32), jax.ShapeDtypeStruct((n, d), BF16),
                   jax.ShapeDtypeStruct((1, d), F32), jax.ShapeDtypeStruct((1, d), F32)],
        compiler_params=_params("arbitrary"),
    )(h, w, target)


def _lower_bounds(logits):
    sm = jax.nn.softmax(logits, axis=0)
    rows = [sm[0:1] * 0.0]
    for r in range(1, DEPTH):
        rows.append(rows[-1] + sm[r:r + 1])
    return jnp.concatenate(rows, axis=0)


def _lb_fwd(logits, name):
    def body(l_ref, o_ref):
        o_ref[...] = _lower_bounds(l_ref[...])

    return pl.pallas_call(body, name=name, out_shape=jax.ShapeDtypeStruct(logits.shape, F32))(logits)


def _lb_bwd(logits, dlb, name):
    def body(l_ref, d_ref, o_ref):
        _, vjp = jax.vjp(_lower_bounds, l_ref[...])
        (o_ref[...],) = vjp(d_ref[...])

    return pl.pallas_call(body, name=name, out_shape=jax.ShapeDtypeStruct(logits.shape, F32))(logits, dlb)


def _head_slice(h):
    return pl.ds(h * HEAD_DIM, HEAD_DIM)


_GD_HEADS = jax.vmap(_gd_head, in_axes=(0, 0, 0, 0, 0, 0, 0, 0, None))
_GD_HEADS_AGAIN = jax.vmap(_gd_head, in_axes=(0, 0, 0, 0, 0, 0, 0, 0, None, 0))


def _lane_blocks(width, block_body):
    def trip(j, carry):
        block_body(lambda base=0: pl.ds(pl.multiple_of(j * LANE_BLOCK + base, LANE_BLOCK), LANE_BLOCK))
        return carry

    lax.fori_loop(0, width // LANE_BLOCK, trip, 0, unroll=BLOCK_UNROLL)


def _row_blocks(rows, block_body):
    def trip(j, carry):
        block_body(pl.ds(pl.multiple_of(j * ROW_BLOCK, ROW_BLOCK), ROW_BLOCK))
        return carry

    lax.fori_loop(0, rows // ROW_BLOCK, trip, 0, unroll=BLOCK_UNROLL)


def _hg_pre_block(p_ref, lb_ref, sums_refs, q_sc, k_sc, v_sc, e_sc, at):
    sl = at()
    q_sc[:, sl], k_sc[:, sl], e_sc[:, sl] = _hg_pre(
        p_ref[:, sl].astype(F32), p_ref[:, at(D_MODEL)].astype(F32), lb_ref[:, sl], [r[...] for r in sums_refs])
    v_sc[:, sl] = p_ref[:, at(2 * D_MODEL)].astype(F32)


def _gd_xp(halo_ref, p_ref, sl, first_chunk):
    halo = jnp.where(first_chunk, 0.0, halo_ref[:, sl].astype(F32))
    return jnp.concatenate([halo, p_ref[:, sl].astype(F32)], axis=0)


def _stack_all(ref, first=0):
    return jnp.stack([ref[s, :, _head_slice(h + first)] for s in range(ref.shape[0]) for h in range(N_HEADS)])


def _unstack_all(ref, val, first=0):
    for s in range(ref.shape[0]):
        for h in range(N_HEADS):
            ref[s, :, _head_slice(h + first)] = val[s * N_HEADS + h].astype(ref.dtype)


def _gdn_fwd_all(projm, projab, cw, alog, dtb, onw, seqs, name):
    n = projm.shape[0]
    t = n // seqs
    nc = t // CHUNK
    d = D_MODEL
    per_halo = CHUNK // HALO
    nh = seqs * N_HEADS

    def body(p_ref, halo_ref, ab_ref, cw_ref, alog_ref, dtb_ref, onw_ref, o2_ref, st_all_ref, y_ref, dinv_ref,
             st_sc, c_sc, beta_sc, g_sc):
        first_chunk = pl.program_id(0) == 0

        @pl.when(first_chunk)
        def _():
            st_sc[...] = jnp.zeros_like(st_sc)

        for s in range(seqs):
            def conv(at, s=s):
                sl = at()
                y = _gd_conv(_gd_xp(halo_ref.at[s], p_ref.at[s], sl, first_chunk), cw_ref[:, sl])
                y_ref[s, :, sl] = y
                c_sc[s, :, sl] = jax.nn.silu(y)

            _lane_blocks(3 * d, conv)
            beta_sc[s], g_sc[s] = _gd_gates(ab_ref[s, :, 0:N_HEADS], ab_ref[s, :, N_HEADS:2 * N_HEADS],
                                            alog_ref[...], dtb_ref[...])
        st_all_ref[0] = st_sc[...]
        o2, st_sc[...], dinv_ref[0] = _GD_HEADS(
            st_sc[...], _stack_all(c_sc), _stack_all(c_sc, N_HEADS), _stack_all(c_sc, 2 * N_HEADS), _stack_all(beta_sc),
            _stack_all(g_sc), _stack_all(g_sc, N_HEADS), _stack_all(p_ref, 3 * N_HEADS).astype(F32), onw_ref[...])
        _unstack_all(o2_ref, o2)

    rows = lambda c: (0, c, 0)
    const = lambda c: (0, 0)
    per_chunk = lambda c: (c, 0, 0, 0)
    p3 = projm.reshape(seqs, t, 4 * d)
    o2, st_all, conv_y, dinv_all = pl.pallas_call(
        body, name=name, grid=(nc,),
        in_specs=[pl.BlockSpec((seqs, CHUNK, 4 * d), rows),
                  pl.BlockSpec((seqs, HALO, 3 * d), lambda c: (0, jnp.maximum(c * per_halo - 1, 0), 0)),
                  pl.BlockSpec((seqs, CHUNK, AB_PAD), rows),
                  pl.BlockSpec((CONV_K, 3 * d), const), pl.BlockSpec((1, N_HEADS), const),
                  pl.BlockSpec((1, N_HEADS), const), pl.BlockSpec((1, HEAD_DIM), const)],
        out_specs=[pl.BlockSpec((seqs, CHUNK, d), rows), pl.BlockSpec((1, nh, HEAD_DIM, HEAD_DIM), per_chunk),
                   pl.BlockSpec((seqs, CHUNK, 3 * d), rows), pl.BlockSpec((1, nh, CHUNK, CHUNK), per_chunk)],
        out_shape=[jax.ShapeDtypeStruct((seqs, t, d), BF16), jax.ShapeDtypeStruct((nc, nh, HEAD_DIM, HEAD_DIM), F32),
                   jax.ShapeDtypeStruct((seqs, t, 3 * d), F32), jax.ShapeDtypeStruct((nc, nh, CHUNK, CHUNK), F32)],
        scratch_shapes=[pltpu.VMEM((nh, HEAD_DIM, HEAD_DIM), F32), pltpu.VMEM((seqs, CHUNK, 3 * d), F32),
                        pltpu.VMEM((seqs, CHUNK, d), F32), pltpu.VMEM((seqs, CHUNK, 2 * d), F32)],
        compiler_params=_params("arbitrary"),
    )(p3, p3, projab.reshape(seqs, t, AB_PAD), cw, alog, dtb, onw)
    return o2.reshape(n, d), st_all, conv_y, dinv_all


def _gdn_bwd_all(projm, projab, conv_y, cw, alog, dtb, onw, st_all, dinv_all, do2, seqs, name):
    n = projm.shape[0]
    t = n // seqs
    nc = t // CHUNK
    d = D_MODEL
    per_halo = CHUNK // HALO
    nh = seqs * N_HEADS

    def body(p_ref, halo_ref, ab_ref, y_ref, cw_ref, alog_ref, dtb_ref, onw_ref, st_all_ref, dinv_ref, do2_ref,
             dp_ref, dab_ref, dcw_ref, dalog_ref, ddtb_ref, donw_ref,
             dst_sc, dhalo_sc, c_sc, beta_sc, g_sc, dc_sc, dbeta_sc, dg_sc):
        first = pl.program_id(0) == 0
        first_chunk = pl.program_id(0) == nc - 1

        @pl.when(first)
        def _():
            dst_sc[...] = jnp.zeros_like(dst_sc)
            dhalo_sc[...] = jnp.zeros_like(dhalo_sc)

        gates_vjps = []
        for s in range(seqs):
            def act(at, s=s):
                c_sc[s, :, at()] = jax.nn.silu(y_ref[s, :, at()])

            _lane_blocks(3 * d, act)
            (beta_sc[s], g_sc[s]), gates_vjp = jax.vjp(
                _gd_gates, ab_ref[s, :, 0:N_HEADS], ab_ref[s, :, N_HEADS:2 * N_HEADS], alog_ref[...], dtb_ref[...])
            gates_vjps.append(gates_vjp)

        dinv = dinv_ref[0]
        _, vjp = jax.vjp(
            lambda *a: _GD_HEADS_AGAIN(*a, dinv), st_all_ref[0], _stack_all(c_sc), _stack_all(c_sc, N_HEADS),
            _stack_all(c_sc, 2 * N_HEADS), _stack_all(beta_sc), _stack_all(g_sc), _stack_all(g_sc, N_HEADS),
            _stack_all(p_ref, 3 * N_HEADS).astype(F32), onw_ref[...])
        dst_sc[...], dq, dk, dv, dbeta, dg, ddiff, dgate, donw = vjp((_stack_all(do2_ref).astype(F32), dst_sc[...]))
        _unstack_all(dc_sc, dq)
        _unstack_all(dc_sc, dk, N_HEADS)
        _unstack_all(dc_sc, dv, 2 * N_HEADS)
        _unstack_all(dbeta_sc, dbeta)
        _unstack_all(dg_sc, dg)
        _unstack_all(dg_sc, ddiff, N_HEADS)
        _unstack_all(dp_ref, dgate, 3 * N_HEADS)

        dalog, ddtb = None, None
        for s in range(seqs):
            def conv_bwd(at, s=s):
                sl = at()
                dxp, dcw = _gd_conv_bwd(_gd_xp(halo_ref.at[s], p_ref.at[s], sl, first_chunk), cw_ref[:, sl],
                                        y_ref[s, :, sl], dc_sc[s, :, sl])
                dqkv = jnp.concatenate([dxp[HALO:CHUNK], dxp[CHUNK:HALO + CHUNK] + dhalo_sc[s, :, sl]], axis=0)
                dp_ref[s, :, sl] = dqkv.astype(dp_ref.dtype)
                dhalo_sc[s, :, sl] = dxp[0:HALO]

                if s > 0:
                    dcw_ref[:, sl] += dcw
                    return

                @pl.when(first)
                def _():
                    dcw_ref[:, sl] = dcw

                @pl.when(jnp.logical_not(first))
                def _():
                    dcw_ref[:, sl] += dcw

            _lane_blocks(3 * d, conv_bwd)
            da, db, dalog_s, ddtb_s = gates_vjps[s]((dbeta_sc[s], dg_sc[s]))
            dab_ref[s] = jnp.concatenate(
                [da, db, jnp.zeros((CHUNK, AB_PAD - 2 * N_HEADS), F32)], axis=1).astype(dab_ref.dtype)
            dalog = dalog_s if dalog is None else dalog + dalog_s
            ddtb = ddtb_s if ddtb is None else ddtb + ddtb_s

        @pl.when(first)
        def _():
            dalog_ref[...] = dalog
            ddtb_ref[...] = ddtb
            donw_ref[...] = donw

        @pl.when(jnp.logical_not(first))
        def _():
            dalog_ref[...] += dalog
            ddtb_ref[...] += ddtb
            donw_ref[...] += donw

    back = lambda c: nc - 1 - c
    rows = lambda c: (0, back(c), 0)
    const = lambda c: (0, 0)
    per_chunk = lambda c: (back(c), 0, 0, 0)
    small = [pl.BlockSpec((CONV_K, 3 * d), const), pl.BlockSpec((1, N_HEADS), const),
             pl.BlockSpec((1, N_HEADS), const), pl.BlockSpec((1, HEAD_DIM), const)]
    p3 = projm.reshape(seqs, t, 4 * d)
    dp, dab, dcw, dalog, ddtb, donw = pl.pallas_call(
        body, name=name, grid=(nc,),
        in_specs=[pl.BlockSpec((seqs, CHUNK, 4 * d), rows),
                  pl.BlockSpec((seqs, HALO, 3 * d), lambda c: (0, jnp.maximum(back(c) * per_halo - 1, 0), 0)),
                  pl.BlockSpec((seqs, CHUNK, AB_PAD), rows), pl.BlockSpec((seqs, CHUNK, 3 * d), rows)] + small + [
                  pl.BlockSpec((1, nh, HEAD_DIM, HEAD_DIM), per_chunk), pl.BlockSpec((1, nh, CHUNK, CHUNK), per_chunk),
                  pl.BlockSpec((seqs, CHUNK, d), rows)],
        out_specs=[pl.BlockSpec((seqs, CHUNK, 4 * d), rows), pl.BlockSpec((seqs, CHUNK, AB_PAD), rows)] + small,
        out_shape=[jax.ShapeDtypeStruct((seqs, t, 4 * d), BF16), jax.ShapeDtypeStruct((seqs, t, AB_PAD), BF16),
                   jax.ShapeDtypeStruct((CONV_K, 3 * d), F32), jax.ShapeDtypeStruct((1, N_HEADS), F32),
                   jax.ShapeDtypeStruct((1, N_HEADS), F32), jax.ShapeDtypeStruct((1, HEAD_DIM), F32)],
        scratch_shapes=[pltpu.VMEM((nh, HEAD_DIM, HEAD_DIM), F32), pltpu.VMEM((seqs, HALO, 3 * d), F32),
                        pltpu.VMEM((seqs, CHUNK, 3 * d), F32), pltpu.VMEM((seqs, CHUNK, d), F32),
                        pltpu.VMEM((seqs, CHUNK, 2 * d), F32), pltpu.VMEM((seqs, CHUNK, 3 * d), F32),
                        pltpu.VMEM((seqs, CHUNK, d), F32), pltpu.VMEM((seqs, CHUNK, 2 * d), F32)],
        compiler_params=_params("arbitrary"),
    )(p3, p3, projab.reshape(seqs, t, AB_PAD), conv_y, cw, alog, dtb, onw, st_all, dinv_all,
      do2.reshape(seqs, t, d))
    return dp.reshape(n, 4 * d), dab.reshape(n, AB_PAD), dcw, dalog, ddtb, donw


def _hgrn_fwd_all(proj, lb, gw, seqs, name):
    n = proj.shape[0]
    t = n // seqs
    nc = t // CHUNK
    d = D_MODEL
    nh = seqs * N_HEADS
    sums, masks = _hg_level_sums(), _hg_level_masks()

    def body(p_ref, lb_ref, gw_ref, sums_wide_ref, sums_once_ref, masks_ref, o2_ref, o_ref, st_all_ref,
             st_sc, q_sc, k_sc, v_sc, e_sc):
        @pl.when(pl.program_id(0) == 0)
        def _():
            st_sc[...] = jnp.zeros_like(st_sc)

        sums_refs = (sums_wide_ref, sums_once_ref)
        for s in range(seqs):
            _lane_blocks(d, functools.partial(_hg_pre_block, p_ref.at[s], lb_ref, sums_refs, q_sc.at[s], k_sc.at[s],
                                              v_sc.at[s], e_sc.at[s]))
        st_all_ref[0] = st_sc[...]
        for s in range(seqs):
            one, mine = pl.ds(s, 1), pl.ds(s * N_HEADS, N_HEADS)
            o, st_sc[mine] = _HG_HEADS(st_sc[mine], *[_stack_all(r.at[one]) for r in (q_sc, k_sc, v_sc, e_sc)],
                                       masks_ref[...])
            _unstack_all(o_ref.at[one], o)

            def post(rows, s=s):
                gate = p_ref[s, rows, 3 * d:4 * d].astype(F32)
                o2_ref[s, rows, :] = _hg_post(o_ref[s, rows, :], gate, gw_ref[...]).astype(o2_ref.dtype)

            _row_blocks(CHUNK, post)

    rows = lambda c: (0, c, 0)
    vec = pl.BlockSpec((1, d), lambda c: (0, 0))
    act = pl.BlockSpec((seqs, CHUNK, d), rows)
    o2, o, st_all = pl.pallas_call(
        body, name=name, grid=(nc,),
        in_specs=[pl.BlockSpec((seqs, CHUNK, 4 * d), rows), vec, vec]
        + [pl.BlockSpec(m.shape, lambda c: (0, 0)) for m in sums] + [pl.BlockSpec(masks.shape, lambda c: (0, 0, 0))],
        out_specs=[act, act, pl.BlockSpec((1, nh, HEAD_DIM, HEAD_DIM), lambda c: (c, 0, 0, 0))],
        out_shape=[jax.ShapeDtypeStruct((seqs, t, d), BF16), jax.ShapeDtypeStruct((seqs, t, d), F32),
                   jax.ShapeDtypeStruct((nc, nh, HEAD_DIM, HEAD_DIM), F32)],
        scratch_shapes=[pltpu.VMEM((nh, HEAD_DIM, HEAD_DIM), F32)] + [pltpu.VMEM((seqs, CHUNK, d), F32)] * 3
        + [pltpu.VMEM((seqs, sums[0].shape[0], d), F32)],
        compiler_params=_params("arbitrary"),
    )(proj.reshape(seqs, t, 4 * d), lb, gw, *sums, masks)
    return o2.reshape(n, d), o, st_all


def _hgrn_bwd_all(proj, lb, gw, st_all, o, do2, seqs, name):
    n = proj.shape[0]
    t = n // seqs
    nc = t // CHUNK
    d = D_MODEL
    nh = seqs * N_HEADS
    sums, masks = _hg_level_sums(), _hg_level_masks()

    def body(p_ref, lb_ref, gw_ref, sums_wide_ref, sums_once_ref, masks_ref, st_all_ref, o_ref, do2_ref,
             dp_ref, dlb_ref, dgw_ref,
             dst_sc, q_sc, k_sc, v_sc, e_sc, do_sc, dq_sc, dk_sc, dv_sc, de_sc, dgw_sc):
        first = pl.program_id(0) == 0

        @pl.when(first)
        def _():
            dst_sc[...] = jnp.zeros_like(dst_sc)

        sums_refs = (sums_wide_ref, sums_once_ref)
        dgw_sc[...] = jnp.zeros_like(dgw_sc)
        for s in range(seqs):
            _lane_blocks(d, functools.partial(_hg_pre_block, p_ref.at[s], lb_ref, sums_refs, q_sc.at[s], k_sc.at[s],
                                              v_sc.at[s], e_sc.at[s]))

            def post_bwd(rows, s=s):
                _, vjp = jax.vjp(_hg_post, o_ref[s, rows, :], p_ref[s, rows, 3 * d:4 * d].astype(F32), gw_ref[...])
                do_sc[s, rows, :], dgate, dgw = vjp(do2_ref[s, rows, :].astype(F32))
                dp_ref[s, rows, 3 * d:4 * d] = dgate.astype(dp_ref.dtype)
                dgw_sc[...] += dgw

            _row_blocks(CHUNK, post_bwd)

        level_masks = masks_ref[...]
        _, vjp = jax.vjp(lambda *a: _HG_HEADS(*a, level_masks), st_all_ref[0],
                         *[_stack_all(r) for r in (q_sc, k_sc, v_sc, e_sc)])
        grads = vjp((_stack_all(do_sc), dst_sc[...]))
        dst_sc[...] = grads[0]
        for r, val in zip((dq_sc, dk_sc, dv_sc, de_sc), grads[1:]):
            _unstack_all(r, val)

        for s in range(seqs):
            def pre_bwd(at, s=s):
                sl = at()
                level_sums = (sums_wide_ref[...], sums_once_ref[...])
                _, vjp = jax.vjp(lambda qraw, f, lb: _hg_pre(qraw, f, lb, level_sums), p_ref[s, :, sl].astype(F32),
                                 p_ref[s, :, at(d)].astype(F32), lb_ref[:, sl])
                dqraw, df, dlb = vjp((dq_sc[s, :, sl], dk_sc[s, :, sl], de_sc[s, :, sl]))
                dp_ref[s, :, sl] = dqraw.astype(dp_ref.dtype)
                dp_ref[s, :, at(d)] = df.astype(dp_ref.dtype)
                dp_ref[s, :, at(2 * d)] = dv_sc[s, :, sl].astype(dp_ref.dtype)
                if s > 0:
                    dlb_ref[:, sl] += dlb
                    return

                @pl.when(first)
                def _():
                    dlb_ref[:, sl] = dlb

                @pl.when(jnp.logical_not(first))
                def _():
                    dlb_ref[:, sl] += dlb

            _lane_blocks(d, pre_bwd)

        @pl.when(first)
        def _():
            dgw_ref[...] = dgw_sc[...]

        @pl.when(jnp.logical_not(first))
        def _():
            dgw_ref[...] += dgw_sc[...]

    rows = lambda c: (0, nc - 1 - c, 0)
    vec = pl.BlockSpec((1, d), lambda c: (0, 0))
    act = pl.BlockSpec((seqs, CHUNK, d), rows)
    wide = pl.BlockSpec((seqs, CHUNK, 4 * d), rows)
    e_rows = sums[0].shape[0]
    dp, dlb, dgw = pl.pallas_call(
        body, name=name, grid=(nc,),
        in_specs=[wide, vec, vec] + [pl.BlockSpec(m.shape, lambda c: (0, 0)) for m in sums] + [
                  pl.BlockSpec(masks.shape, lambda c: (0, 0, 0)),
                  pl.BlockSpec((1, nh, HEAD_DIM, HEAD_DIM), lambda c: (nc - 1 - c, 0, 0, 0)), act, act],
        out_specs=[wide, vec, vec],
        out_shape=[jax.ShapeDtypeStruct((seqs, t, 4 * d), BF16), jax.ShapeDtypeStruct((1, d), F32),
                   jax.ShapeDtypeStruct((1, d), F32)],
        scratch_shapes=[pltpu.VMEM((nh, HEAD_DIM, HEAD_DIM), F32)]
        + [pltpu.VMEM((seqs, CHUNK, d), F32)] * 3 + [pltpu.VMEM((seqs, e_rows, d), F32)]
        + [pltpu.VMEM((seqs, CHUNK, d), F32)] * 4 + [pltpu.VMEM((seqs, e_rows, d), F32), pltpu.VMEM((1, d), F32)],
        compiler_params=_params("arbitrary"),
    )(proj.reshape(seqs, t, 4 * d), lb, gw, *sums, masks, st_all, o, do2.reshape(seqs, t, d))
    return dp.reshape(n, 4 * d), dlb, dgw


def _adam_update(w, g, m, v):
    b1c = 1.0 - ADAM_B1 ** ADAM_STEP
    b2c = 1.0 - ADAM_B2 ** ADAM_STEP
    m_new = ADAM_B1 * m + (1.0 - ADAM_B1) * g
    v_new = ADAM_B2 * v + (1.0 - ADAM_B2) * (g * g)
    delta = -ADAM_LR * ((m_new / b1c) / (jnp.sqrt(v_new / b2c) + ADAM_EPS) + ADAM_WD * w)
    return delta, m_new, v_new


def _adamw(w, g, m, v, name, tr=256):
    r, c = w.shape
    tr = _tile(r, tr)

    def body(w_ref, g_ref, m_ref, v_ref, d_ref, mo_ref, vo_ref):
        d_ref[...], mo_ref[...], vo_ref[...] = _adam_update(w_ref[...], g_ref[...], m_ref[...], v_ref[...])

    blk = pl.BlockSpec((tr, c), lambda i: (i, 0))
    return pl.pallas_call(
        body, name=name, grid=(r // tr,),
        in_specs=[blk] * 4, out_specs=[blk] * 3,
        out_shape=[jax.ShapeDtypeStruct((r, c), F32)] * 3,
        compiler_params=_params("arbitrary"),
    )(w, g, m, v)


def _adamw_slots(w, slot_bufs, m, v, name, tr=256):
    nl, r, c = w.shape
    tr = _tile(r, tr)

    def body(*refs):
        w_ref = refs[0]
        g_refs = refs[1:1 + nl]
        m_ref, v_ref, go_ref, d_ref, mo_ref, vo_ref = refs[1 + nl:]
        for k in range(nl):
            @pl.when(pl.program_id(0) == k)
            def _(k=k):
                g = g_refs[k][0].astype(F32)
                for s in range(1, N_DEV):
                    g = g + g_refs[k][s].astype(F32)
                go_ref[0] = g

        d_ref[0], mo_ref[0], vo_ref[0] = _adam_update(w_ref[0], go_ref[0], m_ref[0], v_ref[0])

    blk = pl.BlockSpec((1, tr, c), lambda l, i: (l, i, 0))
    g_specs = [pl.BlockSpec((N_DEV, tr, c), lambda l, i, k=k: (0, jnp.where(l == k, i, 0), 0)) for k in range(nl)]
    return pl.pallas_call(
        body, name=name, grid=(nl, r // tr),
        in_specs=[blk] + g_specs + [blk, blk], out_specs=[blk] * 4,
        out_shape=[jax.ShapeDtypeStruct((nl, r, c), F32)] * 4,
        compiler_params=_params("arbitrary", "arbitrary"),
    )(w, *slot_bufs, m, v)


def _adamw_windows(w, lo_bufs, hi_bufs, end_bufs, me, m, v, name, tr=256):
    nl, r, c = w.shape
    wl, wh = lo_bufs[0].shape[2], hi_bufs[0].shape[2]
    width, step = wl + wh, c - wl
    assert step >= 0 and (N_DEV - 1) * step + c <= width and N_DEV == 8
    tr = _tile(r, tr)

    def body(*refs):
        me_ref, w_ref = refs[0], refs[1]
        lo_refs, hi_refs, end_refs = refs[2:2 + nl], refs[2 + nl:2 + 2 * nl], refs[2 + 2 * nl:2 + 3 * nl]
        m_ref, v_ref, go_ref, d_ref, mo_ref, vo_ref = refs[2 + 3 * nl:]
        for k in range(nl):
            @pl.when(pl.program_id(0) == k)
            def _(k=k):
                last = me_ref[0] == N_DEV - 1
                hi_of = lambda s: jnp.where(last, end_refs[k][s].astype(F32), hi_refs[k][s].astype(F32))
                lo, hi = lo_refs[k][0].astype(F32), hi_of(0)
                for s in range(1, N_DEV):
                    lo, hi = lo + lo_refs[k][s].astype(F32), hi + hi_of(s)
                g = jnp.concatenate([lo, hi], axis=1)
                for bit in range(3):
                    moved = pltpu.roll(g, width - (step << bit), axis=1)
                    g = jnp.where((me_ref[0] >> bit) & 1 == 1, moved, g)
                go_ref[0] = g[:, :c]

        d_ref[0], mo_ref[0], vo_ref[0] = _adam_update(w_ref[0], go_ref[0], m_ref[0], v_ref[0])

    blk = pl.BlockSpec((1, tr, c), lambda l, i: (l, i, 0))
    g_specs = [pl.BlockSpec((N_DEV, tr, cols), lambda l, i, k=k: (0, jnp.where(l == k, i, 0), 0))
               for cols in (wl, wh, wh) for k in range(nl)]
    return pl.pallas_call(
        body, name=name, grid=(nl, r // tr),
        in_specs=[pl.BlockSpec(memory_space=pltpu.SMEM), blk] + g_specs + [blk, blk], out_specs=[blk] * 4,
        out_shape=[jax.ShapeDtypeStruct((nl, r, c), F32)] * 4,
        compiler_params=_params("arbitrary", "arbitrary"),
    )(me, w, *lo_bufs, *hi_bufs, *end_bufs, m, v)


def _window(w, me, width, step, name, tr=256):
    r, c = w.shape
    assert (N_DEV - 1) * step + c <= width and N_DEV == 8
    tr = _tile(r, tr)

    def body(me_ref, w_ref, out_ref, wide):
        wide[...] = jnp.zeros_like(wide)
        wide[:, 0:c] = w_ref[...]
        g = wide[...]
        for bit in range(3):
            moved = pltpu.roll(g, step << bit, axis=1)
            g = jnp.where((me_ref[0] >> bit) & 1 == 1, moved, g)
        out_ref[...] = g.astype(out_ref.dtype)

    return pl.pallas_call(
        body, name=name, grid=(r // tr,),
        in_specs=[pl.BlockSpec(memory_space=pltpu.SMEM), pl.BlockSpec((tr, c), lambda i: (i, 0))],
        out_specs=pl.BlockSpec((tr, width), lambda i: (i, 0)),
        out_shape=jax.ShapeDtypeStruct((r, width), BF16),
        scratch_shapes=[pltpu.VMEM((tr, width), F32)],
        compiler_params=_params("arbitrary"),
    )(me, w)


def _unshard_windows(win, c, name, tr=256):
    nd, r, w = win.shape
    wl = w - AB_PAD
    step = c - wl
    assert 0 <= step and nd * step <= AB_PAD
    tr = _tile(r, tr)

    def body(win_ref, main_ref, tail_ref):
        lane = lax.broadcasted_iota(jnp.int32, (tr, AB_PAD), 1)

        def past(s):
            return win_ref[s, :, wl:w].astype(F32)

        for s in range(nd):
            first = win_ref[s, :, 0:AB_PAD].astype(F32)
            if s > 0:
                first = jnp.where(lane < s * step, past(s - 1), first)
            main_ref[:, s * wl:s * wl + AB_PAD] = first.astype(main_ref.dtype)
            main_ref[:, s * wl + AB_PAD:(s + 1) * wl] = win_ref[s, :, AB_PAD:wl]
        tail_ref[...] = jnp.where(lane < nd * step, past(nd - 1), 0.0).astype(tail_ref.dtype)

    return pl.pallas_call(
        body, name=name, grid=(r // tr,),
        in_specs=[pl.BlockSpec((nd, tr, w), lambda i: (0, i, 0))],
        out_specs=[pl.BlockSpec((tr, nd * wl), lambda i: (i, 0)), pl.BlockSpec((tr, AB_PAD), lambda i: (i, 0))],
        out_shape=[jax.ShapeDtypeStruct((r, nd * wl), win.dtype), jax.ShapeDtypeStruct((r, AB_PAD), win.dtype)],
        compiler_params=_params("arbitrary"),
    )(win)


def _mesh_pos():
    return lax.axis_index("x"), lax.axis_index("y"), lax.axis_index("c")


def _flip(pos, p):
    x, y, c = pos
    return ((1 - x) if p & 4 else x, (1 - y) if p & 2 else y, (1 - c) if p & 1 else c)


def _lin(pos):
    return 4 * pos[0] + 2 * pos[1] + pos[2]


_HBM = pl.BlockSpec(memory_space=pltpu.HBM)
_SEM = pl.BlockSpec(memory_space=pltpu.SEMAPHORE)
_DATAFLOW = pltpu.SideEffectType.DATAFLOW_SIDE_EFFECTING


class _Item:
    def __init__(self, src, land_shape, src_pick, dst_pick, peers=tuple(range(1, N_DEV))):
        self.src, self.land_shape, self.src_pick, self.dst_pick = src, land_shape, src_pick, dst_pick
        self.peers = peers


def _distinct(arrays):
    found, where = [], []
    for a in arrays:
        hits = [k for k, f in enumerate(found) if f is a]
        where.append(hits[0] if hits else len(found))
        if not hits:
            found.append(a)
    return found, where


def _remote_copies(items, src, land, send_sem, recv_sem, me, arriving):
    me_i = _lin(me)
    out = []
    for it, s_ref, l_ref in zip(items, src, land):
        for p in it.peers:
            peer = _flip(me, p)
            out.append(pltpu.make_async_remote_copy(
                src_ref=it.src_pick(s_ref, _lin(peer)),
                dst_ref=it.dst_pick(l_ref, _lin(peer) if arriving else me_i),
                send_sem=send_sem, recv_sem=recv_sem, device_id=peer, device_id_type=pl.DeviceIdType.MESH))
    return out


def _own_copies(items, src, land, sem, me):
    me_i = _lin(me)
    return [pltpu.make_async_copy(it.src_pick(s_ref, me_i), it.dst_pick(l_ref, me_i), sem)
            for it, s_ref, l_ref in zip(items, src, land)]


def _exchange_start(groups, name):
    items = [it for g in groups for it in g]
    n, ng = len(items), len(groups)
    first = [sum(len(g) for g in groups[:gi]) for gi in range(ng)]
    arrays, where = _distinct([it.src for it in items])
    nu = len(arrays)

    def body(*refs):
        src, land = [refs[k] for k in where], refs[nu:nu + n]
        send_sems, recv_sems = refs[nu + n:nu + n + ng], refs[nu + n + ng:nu + n + 2 * ng]
        token = refs[2 * (nu + n) + 2 * ng]
        me = _mesh_pos()
        for gi, g in enumerate(groups):
            sl = slice(first[gi], first[gi] + len(g))
            for cp in _remote_copies(g, src[sl], land[sl], send_sems[gi], recv_sems[gi], me, arriving=False):
                cp.start()
            for cp in _own_copies(g, src[sl], land[sl], recv_sems[gi], me):
                cp.start(priority=1)
        token[...] = jnp.zeros_like(token)

    srcs = [pltpu.with_memory_space_constraint(a, pltpu.HBM) for a in arrays]
    lands = [pltpu.with_memory_space_constraint(lax.empty(it.land_shape, it.src.dtype), pltpu.HBM) for it in items]
    res = pl.pallas_call(
        body, name=name,
        out_shape=([pltpu.SemaphoreType.DMA(())] * (2 * ng)
                   + [pltpu.HBM(a.shape, a.dtype) for a in arrays]
                   + [pltpu.HBM(it.land_shape, it.src.dtype) for it in items]
                   + [jax.ShapeDtypeStruct((8, 128), F32)]),
        in_specs=[_HBM] * (nu + n),
        out_specs=[_SEM] * (2 * ng) + [_HBM] * (nu + n) + [pl.BlockSpec(memory_space=pltpu.VMEM)],
        input_output_aliases={i: 2 * ng + i for i in range(nu + n)},
        compiler_params=pltpu.CompilerParams(has_side_effects=_DATAFLOW),
    )(*srcs, *lands)
    send_sems, recv_sems = res[0:ng], res[ng:2 * ng]
    src_thru, land_thru = [res[2 * ng + k] for k in where], res[2 * ng + nu:2 * ng + nu + n]
    handles = []
    for gi, g in enumerate(groups):
        sl = slice(first[gi], first[gi] + len(g))
        handles.append((g, src_thru[sl], land_thru[sl], send_sems[gi], recv_sems[gi]))
    return handles, res[-1]


def _exchange_wait(handle, after, name):
    items, src_thru, land_thru, send_sem, recv_sem = handle
    k = len(items)
    arrays, where = _distinct(src_thru)
    nu = len(arrays)
    afters = list(after) if isinstance(after, (list, tuple)) else [after]

    def body(*refs):
        src, land = [refs[u] for u in where], refs[nu:nu + k]
        send_ref, recv_ref = refs[nu + k], refs[nu + k + 1]
        for cp in _remote_copies(items, src, land, send_ref, recv_ref, _mesh_pos(), arriving=True):
            cp.wait_send()
            cp.wait_recv()
        for cp in _own_copies(items, src, land, recv_ref, _mesh_pos()):
            cp.wait()

    res = pl.pallas_call(
        body, name=name,
        out_shape=([pltpu.HBM(s.shape, s.dtype) for s in arrays] + [pltpu.HBM(l.shape, l.dtype) for l in land_thru]),
        in_specs=[_HBM] * (nu + k) + [_SEM, _SEM] + [pl.BlockSpec(memory_space=pl.ANY)] * len(afters),
        out_specs=[_HBM] * (nu + k),
        input_output_aliases={i: i for i in range(nu + k)},
        compiler_params=pltpu.CompilerParams(has_side_effects=_DATAFLOW),
    )(*arrays, *land_thru, send_sem, recv_sem, *afters)
    return res[nu:nu + k]


SAME_CORE = (2, 4, 6)
SIBLING = 1


def _pass_on_start(buf, name):
    def body(buf_ref, send_sem, recv_sem, thru_ref):
        me = _mesh_pos()
        for p in SAME_CORE:
            slot = buf_ref.at[_lin(_flip(me, p))]
            pltpu.make_async_remote_copy(src_ref=slot, dst_ref=slot, send_sem=send_sem, recv_sem=recv_sem,
                                         device_id=_flip(me, SIBLING), device_id_type=pl.DeviceIdType.MESH).start()

    return pl.pallas_call(
        body, name=name,
        out_shape=[pltpu.SemaphoreType.DMA(()), pltpu.SemaphoreType.DMA(()), pltpu.HBM(buf.shape, buf.dtype)],
        in_specs=[_HBM], out_specs=[_SEM, _SEM, _HBM], input_output_aliases={0: 2},
        compiler_params=pltpu.CompilerParams(has_side_effects=_DATAFLOW),
    )(pltpu.with_memory_space_constraint(buf, pltpu.HBM))


def _pass_on_wait(handle, name):
    send_sem, recv_sem, thru = handle

    def body(buf_ref, send_ref, recv_ref, out_ref):
        me = _mesh_pos()
        sibling = _flip(me, SIBLING)
        for p in SAME_CORE:
            mine, theirs = buf_ref.at[_lin(_flip(me, p))], buf_ref.at[_lin(_flip(sibling, p))]
            cp = pltpu.make_async_remote_copy(src_ref=mine, dst_ref=theirs, send_sem=send_ref, recv_sem=recv_ref,
                                              device_id=sibling, device_id_type=pl.DeviceIdType.MESH)
            cp.wait_send()
            cp.wait_recv()

    return pl.pallas_call(
        body, name=name, out_shape=pltpu.HBM(thru.shape, thru.dtype),
        in_specs=[_HBM, _SEM, _SEM], out_specs=_HBM, input_output_aliases={0: 0},
        compiler_params=pltpu.CompilerParams(has_side_effects=_DATAFLOW),
    )(thru, send_sem, recv_sem)


def _whole(ref, i):
    return ref


def _slot(ref, i):
    return ref.at[i]


def _rows_of(r):
    return lambda ref, i: ref.at[pl.ds(pl.multiple_of(i * r, r), r), :]


def _cols_of(c):
    return lambda ref, i: ref.at[:, pl.ds(pl.multiple_of(i * c, c), c)]


def _all_reduce_small(buf, after, name):
    r, c = buf.shape

    def body(src_ref, after_ref, out_ref, all_ref, send_sems, recv_sems):
        me = _mesh_pos()
        me_i = _lin(me)
        all_ref[me_i] = src_ref[...]
        for p in range(1, N_DEV):
            peer = _flip(me, p)
            pltpu.make_async_remote_copy(
                src_ref=src_ref, dst_ref=all_ref.at[me_i], send_sem=send_sems.at[p - 1], recv_sem=recv_sems.at[p - 1],
                device_id=peer, device_id_type=pl.DeviceIdType.MESH).start()
        for p in range(1, N_DEV):
            peer = _flip(me, p)
            cp = pltpu.make_async_remote_copy(
                src_ref=src_ref, dst_ref=all_ref.at[_lin(peer)], send_sem=send_sems.at[p - 1],
                recv_sem=recv_sems.at[p - 1], device_id=peer, device_id_type=pl.DeviceIdType.MESH)
            cp.wait_recv()
            cp.wait_send()
        acc = all_ref[0]
        for s in range(1, N_DEV):
            acc = acc + all_ref[s]
        out_ref[...] = acc

    vm = pl.BlockSpec(memory_space=pltpu.VMEM)
    return pl.pallas_call(
        body, name=name, in_specs=[vm, pl.BlockSpec(memory_space=pl.ANY)], out_specs=vm,
        out_shape=jax.ShapeDtypeStruct((r, c), F32),
        scratch_shapes=[pltpu.VMEM((N_DEV, r, c), F32), pltpu.SemaphoreType.DMA((N_DEV - 1,)),
                        pltpu.SemaphoreType.DMA((N_DEV - 1,))],
        compiler_params=pltpu.CompilerParams(has_side_effects=True),
    )(buf, after)


def _unshard_cols(g):
    s, l, r, c = g.shape
    return jnp.transpose(g, (1, 2, 0, 3)).reshape(l, r, s * c)


def kernel(x, gdn_w_in, gdn_conv, gdn_a_log, gdn_dt_bias, gdn_onorm, gdn_w_out, hgrn_w_in, hgrn_lb_logits, hgrn_gnorm, hgrn_w_out, norm_mix, norm_mlp, mlp_w_up, mlp_w_down, norm_final, loss_target, m_gdn_w_in, m_gdn_conv, m_gdn_a_log, m_gdn_dt_bias, m_gdn_onorm, m_gdn_w_out, m_hgrn_w_in, m_hgrn_lb_logits, m_hgrn_gnorm, m_hgrn_w_out, m_norm_mix, m_norm_mlp, m_mlp_w_up, m_mlp_w_down, m_norm_final, v_gdn_w_in, v_gdn_conv, v_gdn_a_log, v_gdn_dt_bias, v_gdn_onorm, v_gdn_w_out, v_hgrn_w_in, v_hgrn_lb_logits, v_hgrn_gnorm, v_hgrn_w_out, v_norm_mix, v_norm_mlp, v_mlp_w_up, v_mlp_w_down, v_norm_final):
    seqs, seq_len, d = x.shape
    n = seqs * seq_len
    me_i = _lin(_mesh_pos())
    x2 = x.reshape(n, d)
    target = loss_target.reshape(n, d)
    n_gdn, n_hgrn = gdn_w_in.shape[0], hgrn_w_in.shape[0]

    r_out, r_down = gdn_w_out.shape[1], mlp_w_down.shape[1]
    c_gin, c_hin, c_up = gdn_w_in.shape[2], hgrn_w_in.shape[2], mlp_w_up.shape[2]

    def gathered(w, pick, land_shape, **kw):
        return _Item(w.astype(BF16), land_shape, _whole, pick, **kw)

    wl = GDN_MAIN // N_DEV

    me_1 = me_i.astype(jnp.int32).reshape(1)

    def next_tile(ref, i):
        return ref.at[:, pl.ds(pl.multiple_of(jnp.minimum(i + 1, N_DEV - 1) * wl, AB_PAD), AB_PAD)]

    groups = [[_Item(gdn_conv, (N_DEV,) + gdn_conv.shape, _whole, _slot),
               _Item(hgrn_gnorm, (N_DEV,) + hgrn_gnorm.shape, _whole, _slot)]]
    for i in range(DEPTH):
        j = i // 2
        if i % 2 == 0:
            direct = (SIBLING,) + SAME_CORE if i == 0 else tuple(range(1, N_DEV))
            win = _window(gdn_w_in[j], me_1, wl + AB_PAD, c_gin - wl, f"window_in_{i}")
            groups += [[_Item(win, (N_DEV, d, wl + AB_PAD), _whole, _slot, peers=direct)],
                       [gathered(gdn_w_out[j], _rows_of(r_out), (N_DEV * r_out, d))]]
        else:
            groups += [[gathered(hgrn_w_in[j], _cols_of(c_hin), (d, N_DEV * c_hin))],
                       [gathered(hgrn_w_out[j], _rows_of(r_out), (N_DEV * r_out, d))]]
        groups += [[gathered(mlp_w_up[i], _cols_of(c_up), (d, N_DEV * c_up))],
                   [gathered(mlp_w_down[i], _rows_of(r_down), (N_DEV * r_down, d))]]
    gather_handles, token = _exchange_start(groups, "gather_start")
    lbs = _lb_fwd(hgrn_lb_logits + token[0:1, 0:1], "lb_fwd")

    def arrived(k, after, name):
        return _exchange_wait(gather_handles[k], after, "gather_wait_" + name)

    saved = []
    w_in, w_ab, w_out, w_up, w_down = ([None] * DEPTH for _ in range(5))
    h = x2
    for i in range(DEPTH):
        j = i // 2
        if i == 0:
            g_conv, g_gnorm = arrived(0, h, "small")
            conv_full = _unshard_cols(g_conv)
            gnorm_full = jnp.transpose(g_gnorm, (1, 0, 2)).reshape(n_hgrn, d)
        if i == 0:
            y = _rms_fwd(h, norm_mix[0:1] + token[0:1, 0:1], "rms_mix_0")
        (w_in[i],) = arrived(1 + 4 * i, [y, lbs, conv_full, gnorm_full] if i == 0 else y, f"in_{i}")
        if i == 0:
            w_in[i] = _pass_on_wait(_pass_on_start(w_in[i], "pass_on_start_in_0"), "pass_on_wait_in_0")
        if i % 2 == 0:
            w_in[i], w_ab[i] = _unshard_windows(w_in[i], c_gin, f"gdn_w_in_{i}")
            projm = _mm(y, w_in[i], "nn", [BF16], f"gdn_proj_{i}")
            projab = _mm(y, w_ab[i], "nn", [F32], f"gdn_proj_ab_{i}")
            o2, st_all, conv_y, dinv_all = _gdn_fwd_all(projm, projab, conv_full[j], gdn_a_log[j:j + 1],
                                                    gdn_dt_bias[j:j + 1], gdn_onorm[j:j + 1], seqs, f"gdn_fwd_{i}")
            mix = (projm, projab, conv_y, st_all, dinv_all)
        else:
            proj = _mm(y, w_in[i], "nn", [BF16], f"hgrn_proj_{i}")
            o2, o_raw, st_all = _hgrn_fwd_all(proj, lbs[i:i + 1], gnorm_full[j:j + 1], seqs, f"hgrn_fwd_{i}")
            mix = (proj, o_raw, st_all)
        (w_out[i],) = arrived(2 + 4 * i, o2, f"out_{i}")
        h1, y2 = _mm_rows(o2, w_out[i], "nn", [F32, BF16], f"mix_out_{i}", epilogue=_ep_residual_norm, extras=(h,),
                     vectors=(norm_mlp[i:i + 1],))
        (w_up[i],) = arrived(3 + 4 * i, y2, f"up_{i}")
        u, act = _mm(y2, w_up[i], "nn", [BF16, BF16], f"mlp_up_{i}",
                     epilogue=lambda acc: (acc, jnp.square(jnp.maximum(acc, 0.0))))
        (w_down[i],) = arrived(4 + 4 * i, act, f"down_{i}")
        saved.append((h, y, mix, o2, h1, y2, u, act))
        if i + 1 < DEPTH:
            h, y = _mm_rows(act, w_down[i], "nn", [F32, BF16], f"mlp_down_{i}", epilogue=_ep_residual_norm, extras=(h1,),
                       vectors=(norm_mix[i + 1:i + 2],))
        else:
            h = _mm(act, w_down[i], "nn", [F32], f"mlp_down_{i}", epilogue=lambda acc, res: (res + acc,),
                    extras=(h1,))

    dh, dh_b, d_nf, sq = _loss_head(h, norm_final.reshape(1, d), target, "loss_head")

    d_nmix, d_nmlp = [None] * DEPTH, [None] * DEPTH
    d_conv, d_alog, d_dtb, d_onorm = [None] * n_gdn, [None] * n_gdn, [None] * n_gdn, [None] * n_gdn
    d_lb = [jnp.zeros((1, d), F32)] * DEPTH
    d_gnorm = [None] * n_hgrn
    mlp_handles, mix_handles = [None] * DEPTH, [None] * DEPTH
    token = None
    for i in reversed(range(DEPTH)):
        j = i // 2
        h_in, y, mix, o2, h1, y2, u, act = saved[i]
        g_down = _mm(act, dh_b, "tn", [BF16], f"g_down_{i}", after=token)
        du = _mm(dh_b, w_down[i], "nt", [BF16], f"d_u_{i}",
                 epilogue=lambda acc, uu: (acc * (2.0 * jnp.maximum(uu.astype(F32), 0.0)),), extras=(u,))
        g_up = _mm(y2, du, "tn", [BF16], f"g_up_{i}")
        mlp_handles[i], token = _exchange_start(
            [[_Item(g_down, (N_DEV, r_down, d), _rows_of(r_down), _slot)],
             [_Item(g_up, (N_DEV, d, c_up), _cols_of(c_up), _slot)]], f"scatter_start_mlp_{i}")
        dh1, dh1_b, d_nmlp[i] = _mm_rows(du, w_up[i], "nt", [F32, BF16], f"d_y2_{i}", epilogue=_ep_norm_bwd,
                                     extras=(h1, dh), vectors=(norm_mlp[i:i + 1],), n_sums=1, after=token)
        g_out = _mm(o2, dh1_b, "tn", [BF16], f"g_out_{i}")
        do2 = _mm(dh1_b, w_out[i], "nt", [BF16], f"d_o2_{i}")
        if i % 2 == 0:
            projm, projab, conv_y, st_all, dinv_all = mix
            dpm, dpab, d_conv[j], d_alog[j], d_dtb[j], d_onorm[j] = _gdn_bwd_all(
                projm, projab, conv_y, conv_full[j], gdn_a_log[j:j + 1], gdn_dt_bias[j:j + 1], gdn_onorm[j:j + 1],
                st_all, dinv_all, do2, seqs, f"gdn_bwd_{i}")
            g_main = _mm(y, dpm, "tn", [BF16], f"g_in_{i}")
            g_ab = _mm(y, dpab, "tn", [BF16], f"g_in_ab_{i}")
            in_items = [_Item(g_main, (N_DEV, d, wl), _cols_of(wl), _slot),
                        _Item(g_main, (N_DEV, d, AB_PAD), next_tile, _slot),
                        _Item(g_ab, (N_DEV, d, AB_PAD), _whole, _slot)]
            dp, dy_more = dpm, [(dpab, w_ab[i])]
        else:
            proj, o_raw, st_all = mix
            dp, d_lb[i], d_gnorm[j] = _hgrn_bwd_all(proj, lbs[i:i + 1], gnorm_full[j:j + 1], st_all, o_raw, do2,
                                               seqs, f"hgrn_bwd_{i}")
            g_in = _mm(y, dp, "tn", [BF16], f"g_in_{i}")
            in_items = [_Item(g_in, (N_DEV, d, c_hin), _cols_of(c_hin), _slot)]
            dy_more = []
        mix_handles[i], token = _exchange_start(
            [[_Item(g_out, (N_DEV, r_out, d), _rows_of(r_out), _slot)], in_items], f"scatter_start_mix_{i}")
        dh, dh_b, d_nmix[i] = _mm_rows(dp, w_in[i], "nt", [F32, BF16], f"d_y_{i}", epilogue=_ep_norm_bwd,
                                  extras=(h_in, dh1), vectors=(norm_mix[i:i + 1],), n_sums=1, after=token, more=dy_more)
        token = None
    grad_x = dh.reshape(x.shape)

    def landed(handles, k, layers, after, name):
        return [_exchange_wait(handles[i][k], after, f"scatter_wait_{name}_{i}")[0] for i in layers]

    every, even, odd = range(DEPTH), range(0, DEPTH, 2), range(1, DEPTH, 2)
    upd = {}
    upd["mlp_w_down"] = _adamw_slots(mlp_w_down, landed(mlp_handles, 0, every, dh, "down"), m_mlp_w_down,
                                     v_mlp_w_down, "adamw_mlp_w_down")
    upd["mlp_w_up"] = _adamw_slots(mlp_w_up, landed(mlp_handles, 1, every, upd["mlp_w_down"][1], "up"), m_mlp_w_up,
                                   v_mlp_w_up, "adamw_mlp_w_up")
    upd["hgrn_w_out"] = _adamw_slots(hgrn_w_out, landed(mix_handles, 0, odd, upd["mlp_w_up"][1], "out"),
                                     m_hgrn_w_out, v_hgrn_w_out, "adamw_hgrn_w_out")
    upd["hgrn_w_in"] = _adamw_slots(hgrn_w_in, landed(mix_handles, 1, odd, upd["hgrn_w_out"][1], "in"), m_hgrn_w_in,
                                    v_hgrn_w_in, "adamw_hgrn_w_in")

    dlb_rows = jnp.concatenate(d_lb, axis=0)
    tail = jnp.concatenate(
        [jnp.concatenate(d_onorm, axis=1), jnp.concatenate(d_alog, axis=1), jnp.concatenate(d_dtb, axis=1)], axis=1)
    tail = jnp.pad(tail, ((0, 0), (0, d - tail.shape[1])))
    conv_rows = jnp.stack(d_conv).reshape(-1, d)
    packed = jnp.concatenate(
        [jnp.concatenate(d_nmix, axis=0), jnp.concatenate(d_nmlp, axis=0), d_nf, sq, dlb_rows,
         jnp.concatenate(d_gnorm, axis=0), tail, conv_rows], axis=0)
    pad_rows = (-packed.shape[0]) % 8
    packed = jnp.pad(packed, ((0, pad_rows), (0, 0)))
    tot = _all_reduce_small(packed, upd["hgrn_w_in"][1], "reduce_small")

    upd["gdn_w_out"] = _adamw_slots(gdn_w_out, landed(mix_handles, 0, even, tot, "out"),
                                    m_gdn_w_out, v_gdn_w_out, "adamw_gdn_w_out")
    windows = [_exchange_wait(mix_handles[i][1], upd["gdn_w_out"][1], f"scatter_wait_in_{i}") for i in even]
    upd["gdn_w_in"] = _adamw_windows(gdn_w_in, *([win[k] for win in windows] for k in range(3)),
                                     me_1, m_gdn_w_in, v_gdn_w_in, "adamw_gdn_w_in")

    def update(name, w, g, m, v):
        shape = w.shape
        c = shape[-1]
        res = _adamw(w.reshape(-1, c), g.reshape(-1, c), m.reshape(-1, c), v.reshape(-1, c), "adamw_" + name)
        return [g.reshape(shape)] + [o.reshape(shape) for o in res]

    r0 = 0
    g_nmix = tot[r0:r0 + DEPTH]; r0 += DEPTH
    g_nmlp = tot[r0:r0 + DEPTH]; r0 += DEPTH
    g_nf = tot[r0]; r0 += 1
    loss = tot[r0, 0]; r0 += 1
    g_lb = _lb_bwd(hgrn_lb_logits, tot[r0:r0 + DEPTH], "lb_bwd"); r0 += DEPTH
    g_gnorm_full = tot[r0:r0 + n_hgrn]; r0 += n_hgrn
    t_row = tot[r0]; r0 += 1
    g_conv_full = tot[r0:r0 + n_gdn * CONV_K * 3].reshape(n_gdn, CONV_K, 3 * d)
    g_onorm = t_row[0:n_gdn * HEAD_DIM].reshape(n_gdn, HEAD_DIM)
    o1 = n_gdn * HEAD_DIM
    g_alog = t_row[o1:o1 + n_gdn * N_HEADS].reshape(n_gdn, N_HEADS)
    g_dtb = t_row[o1 + n_gdn * N_HEADS:o1 + 2 * n_gdn * N_HEADS].reshape(n_gdn, N_HEADS)
    c_gn, c_cv = hgrn_gnorm.shape[1], gdn_conv.shape[2]
    g_gnorm = lax.dynamic_slice_in_dim(g_gnorm_full, me_i * c_gn, c_gn, axis=1)
    g_conv = lax.dynamic_slice_in_dim(g_conv_full, me_i * c_cv, c_cv, axis=2)

    upd["gdn_conv"] = update("gdn_conv", gdn_conv, g_conv, m_gdn_conv, v_gdn_conv)
    upd["gdn_a_log"] = update("gdn_a_log", gdn_a_log, g_alog, m_gdn_a_log, v_gdn_a_log)
    upd["gdn_dt_bias"] = update("gdn_dt_bias", gdn_dt_bias, g_dtb, m_gdn_dt_bias, v_gdn_dt_bias)
    upd["gdn_onorm"] = update("gdn_onorm", gdn_onorm, g_onorm, m_gdn_onorm, v_gdn_onorm)
    upd["hgrn_lb_logits"] = update("hgrn_lb_logits", hgrn_lb_logits, g_lb, m_hgrn_lb_logits, v_hgrn_lb_logits)
    upd["hgrn_gnorm"] = update("hgrn_gnorm", hgrn_gnorm, g_gnorm, m_hgrn_gnorm, v_hgrn_gnorm)
    upd["norm_mix"] = update("norm_mix", norm_mix, g_nmix, m_norm_mix, v_norm_mix)
    upd["norm_mlp"] = update("norm_mlp", norm_mlp, g_nmlp, m_norm_mlp, v_norm_mlp)
    upd["norm_final"] = update("norm_final", norm_final, g_nf, m_norm_final, v_norm_final)

    order = ["gdn_w_in", "gdn_conv", "gdn_a_log", "gdn_dt_bias", "gdn_onorm", "gdn_w_out", "hgrn_w_in",
             "hgrn_lb_logits", "hgrn_gnorm", "hgrn_w_out", "norm_mix", "norm_mlp", "mlp_w_up", "mlp_w_down",
             "norm_final"]
    outs = [loss, grad_x]
    for k in range(4):
        outs += [upd[name][k] for name in order]
    return tuple(outs)
```

```python
import functools

import numpy as np
import jax
import jax.numpy as jnp
from jax import lax
from jax.experimental import pallas as pl
from jax.experimental.pallas import tpu as pltpu

F32 = jnp.float32
BF16 = jnp.bfloat16

D_MODEL = 1024
N_HEADS = 8
HEAD_DIM = 128
CHUNK = 64
CONV_K = 4
HALO = 16
EPS = 1e-6
DEPTH = 4
N_DEV = 8
GDN_MAIN = 4 * D_MODEL
AB_PAD = 128
LANE_BLOCK = 256
ROW_BLOCK = 16
BLOCK_UNROLL = 4

ADAM_LR = 0.001
ADAM_B1 = 0.9
ADAM_B2 = 0.999
ADAM_EPS = 1e-08
ADAM_WD = 0.01
ADAM_STEP = 10

VMEM_LIMIT = 56 * 1024 * 1024
MM_TILE = 1024
MM_ROWS_MAX = 2048
MM_VMEM_BUDGET = 40 * 1024 * 1024
MM_ROWS_TILE = 512
_DIMS = {
    "nn": (((1,), (0,)), ((), ())),
    "nt": (((1,), (1,)), ((), ())),
    "tn": (((0,), (0,)), ((), ())),
}


def _parts(x, n):
    if n == 1 and x.dtype == BF16:
        return [x]
    out = []
    r = x.astype(F32)
    for i in range(n):
        p = r.astype(BF16)
        out.append(p)
        if i + 1 < n:
            r = r - p.astype(F32)
    return out


def _dot_raw(a, b, mode, na, nb):
    ap, bp = _parts(a, na), _parts(b, nb)
    nmax = max(na, nb)
    pairs = [(i, j) for i in range(na) for j in range(nb) if i + j < nmax]
    ka = 0 if mode == "tn" else 1
    kb = 1 if mode == "nt" else 0
    xa = ap[0] if len(pairs) == 1 else jnp.concatenate([ap[i] for i, _ in pairs], axis=ka)
    xb = bp[0] if len(pairs) == 1 else jnp.concatenate([bp[j] for _, j in pairs], axis=kb)
    return lax.dot_general(xa, xb, _DIMS[mode], preferred_element_type=F32)


@functools.partial(jax.custom_vjp, nondiff_argnums=(2, 3, 4))
def _dot(a, b, mode, na, nb):
    return _dot_raw(a, b, mode, na, nb)


def _dot_fwd(a, b, mode, na, nb):
    return _dot_raw(a, b, mode, na, nb), (a, b)


def _dot_bwd(mode, na, nb, res, ct):
    a, b = res
    if mode == "nn":
        da = _dot_raw(ct, b, "nt", 1, 1)
        db = _dot_raw(a, ct, "tn", 1, 1)
    elif mode == "nt":
        da = _dot_raw(ct, b, "nn", 1, 1)
        db = _dot_raw(ct, a, "tn", 1, 1)
    else:
        da = _dot_raw(b, ct, "nt", 1, 1)
        db = _dot_raw(a, ct, "nn", 1, 1)
    return da.astype(a.dtype), db.astype(b.dtype)


_dot.defvjp(_dot_fwd, _dot_bwd)


N_EXACT = 3


@jax.custom_vjp
def _dot01(x, m_wide, m):
    return lax.dot_general(m_wide, jnp.concatenate(_parts(x, N_EXACT), axis=0), _DIMS["nn"], preferred_element_type=F32)


def _dot01_fwd(x, m_wide, m):
    return _dot01(x, m_wide, m), (m_wide, m)


def _dot01_bwd(res, ct):
    m_wide, m = res
    dx = lax.dot_general(m, ct.astype(BF16), _DIMS["tn"], preferred_element_type=F32)
    return dx, jnp.zeros_like(m_wide), jnp.zeros_like(m)


_dot01.defvjp(_dot01_fwd, _dot01_bwd)


def _thrice(m):
    return jnp.concatenate([m] * N_EXACT, axis=1).astype(BF16), m.astype(BF16)


def _iota2(shape, dim):
    return lax.broadcasted_iota(jnp.int32, shape, dim)


def _tril_f32(n):
    return (_iota2((n, n), 0) >= _iota2((n, n), 1)).astype(F32)


def _below_block(n, b):
    ri, ci = _iota2((n, n), 0) // b, _iota2((n, n), 1) // b
    return (ri == ci + 1) & (ri % 2 == 1)


def _half_inverses(L):
    n = L.shape[0]
    eye = (_iota2((n, n), 0) == _iota2((n, n), 1)).astype(F32)
    d = eye - jnp.where(_below_block(n, 1), L, 0.0)
    b = 2
    while 2 * b < n:
        e = jnp.where(_below_block(n, b), L, 0.0)
        d = d - _dot_raw(d, _dot_raw(e, d, "nn", 2, 2), "nn", 2, 2)
        b *= 2
    return d, jnp.where(_below_block(n, b), L, 0.0)


def _solve_with(d, e, rhs):
    y = _dot_raw(d, rhs, "nn", 2, 2)
    return y - _dot_raw(d, _dot_raw(e, y, "nn", 2, 2), "nn", 2, 2)


@jax.custom_vjp
def _solve_unit_lower(L, rhs, d):
    n = L.shape[0]
    return _solve_with(d, jnp.where(_below_block(n, n // 2), L, 0.0), rhs)


def _solve_fwd(L, rhs, d):
    n = L.shape[0]
    e = jnp.where(_below_block(n, n // 2), L, 0.0)
    sol = _solve_with(d, e, rhs)
    return sol, (d, e, sol)


def _solve_bwd(res, ct):
    d, e, sol = res
    y = _dot_raw(d, ct - _dot_raw(e, _dot_raw(d, ct, "tn", 2, 2), "tn", 2, 2), "tn", 2, 2)
    return -_dot_raw(y, sol, "nt", 2, 2), y, jnp.zeros_like(d)


_solve_unit_lower.defvjp(_solve_fwd, _solve_bwd)


def _softplus(x):
    return jnp.maximum(x, 0.0) + jnp.log1p(jnp.exp(-jnp.abs(x)))


def _rms(x, w):
    return x * lax.rsqrt(jnp.mean(x * x, axis=-1, keepdims=True) + EPS) * w


HG_LEVELS = (32, 16, 8, 4, 2, 1)


def _hg_level_sums():
    i = np.arange(CHUNK)[:, None]
    m = np.arange(CHUNK)[None, :]
    to_row = [(m <= i) & (m // b == i // b) for b in HG_LEVELS]
    to_col = [(m > i) & (m // b == i // b) for b in HG_LEVELS if b > 1]
    return _thrice(jnp.asarray(np.concatenate(to_row + to_col + [m <= i]), F32))


def _hg_level_masks():
    i = np.arange(CHUNK)[:, None]
    j = np.arange(CHUNK)[None, :]
    return jnp.asarray(np.stack([(i // b == j // b + 1) & ((i // b) % 2 == 1) for b in HG_LEVELS]), F32)


def _hg_pre(qraw, f, lb, sums):
    g = jnp.log(lb + (1.0 - lb) * jax.nn.sigmoid(f))
    k = (1.0 - lb) * jax.nn.sigmoid(-f)
    q = jax.nn.silu(qraw) * (HEAD_DIM ** -0.5)
    return q, k, _dot01(g, *sums)


def _hg_head(st, q, k, v, e, masks):
    nl = len(HG_LEVELS)
    eye = (_iota2((CHUNK, CHUNK), 0) == _iota2((CHUNK, CHUNK), 1)).astype(F32)
    a = eye * jnp.sum(q * k, axis=-1, keepdims=True)
    for l, b in enumerate(HG_LEVELS):
        rows = q * jnp.exp(e[l * CHUNK:(l + 1) * CHUNK])
        cols = k * jnp.exp(e[(nl + l) * CHUNK:(nl + l + 1) * CHUNK]) if b > 1 else k
        a = a + masks[l] * _dot(rows, cols, "nt", 1, 1)
    gc = e[(2 * nl - 1) * CHUNK:2 * nl * CHUNK]
    o = _dot(a, v, "nn", 1, 1) + _dot(q * jnp.exp(gc), st, "nt", 1, 1)
    g_last = gc[CHUNK - 1:CHUNK]
    st_new = st * jnp.exp(g_last) + _dot(v, k * jnp.exp(g_last - gc), "tn", 1, 1)
    return o, st_new


_HG_HEADS = jax.vmap(_hg_head, in_axes=(0, 0, 0, 0, 0, None))


def _hg_post(o, gate, gw):
    return _rms(o, gw) * jax.nn.silu(gate)


def _gd_conv(xp, cw):
    off = HALO - (CONV_K - 1)
    y = cw[0:1] * xp[off:off + CHUNK]
    for kk in range(1, CONV_K):
        y = y + cw[kk:kk + 1] * xp[off + kk:off + kk + CHUNK]
    return y


def _gd_conv_bwd(xp, cw, y, dc):
    off = HALO - (CONV_K - 1)
    sig = jax.nn.sigmoid(y)
    dy = dc * (sig * (1.0 + y * (1.0 - sig)))
    dxp, dcw = None, []
    for kk in range(CONV_K):
        moved = jnp.pad(dy, ((off + kk, HALO - off - kk), (0, 0)))
        term = cw[kk:kk + 1] * moved
        dxp = term if dxp is None else dxp + term
        dcw.append(jnp.sum(xp * moved, axis=0, keepdims=True))
    return dxp, jnp.concatenate(dcw, axis=0)


def _gd_gates(a, b, alog, dtb):
    beta = jax.nn.sigmoid(b)
    g = -jnp.exp(alog) * _softplus(a + dtb)
    expand = (_iota2((N_HEADS, D_MODEL), 1) // HEAD_DIM == _iota2((N_HEADS, D_MODEL), 0)).astype(F32)
    g_x = _dot(g, expand, "nn", 3, 1)
    after = (_iota2((CHUNK, D_MODEL), 0) > _iota2((CHUNK, D_MODEL), 1) % HEAD_DIM).astype(F32)
    sums = _dot01(jnp.concatenate([g_x, g_x * after], axis=1), *_thrice(_tril_f32(CHUNK)))
    return _dot(beta, expand, "nn", 3, 1), sums


def _gd_head(st, q, k, v, beta, gc, diff, gate, onw, dinv=None):
    q = q * lax.rsqrt(jnp.sum(q * q, axis=-1, keepdims=True) + EPS) * (HEAD_DIM ** -0.5)
    k = k * lax.rsqrt(jnp.sum(k * k, axis=-1, keepdims=True) + EPS)
    ri = _iota2((CHUNK, CHUNK), 0)
    ci = _iota2((CHUNK, CHUNK), 1)
    decay = jnp.exp(jnp.where(ri >= ci, diff[:, 0:CHUNK], -jnp.inf))
    kb = k * beta
    egc = jnp.exp(gc)
    L = jnp.where(ri > ci, _dot(kb, k, "nt", 1, 1) * decay, 0.0)
    made = dinv is None
    if made:
        dinv = _half_inverses(L)[0]
    sol = _solve_unit_lower(L, jnp.concatenate([v * beta, kb * egc], axis=1), dinv)
    u = sol[:, 0:HEAD_DIM]
    w = sol[:, HEAD_DIM:2 * HEAD_DIM]
    a_qk = jnp.where(ri >= ci, _dot(q, k, "nt", 1, 1) * decay, 0.0)
    g_last = gc[CHUNK - 1:CHUNK]
    v_new = u - _dot(w, st, "nt", 1, 1)
    o = _dot(q * egc, st, "nt", 1, 1) + _dot(a_qk, v_new, "nn", 1, 1)
    st_new = st * jnp.exp(g_last) + _dot(v_new, k * jnp.exp(g_last - gc), "tn", 1, 1)
    out = (_rms(o, onw) * jax.nn.silu(gate), st_new)
    return out + (dinv,) if made else out


def _params(*sem):
    return pltpu.CompilerParams(dimension_semantics=sem, vmem_limit_bytes=VMEM_LIMIT)


def _tile(n, pref):
    t = min(n, pref)
    assert n % t == 0, (n, pref)
    return t


def _mm_tiles(m, n, k, a_size, b_size, tile_sizes):
    tn = _tile(n, MM_TILE)

    def need(tm, tk):
        acc = 4 * tm * tn * (2 if tk < k else 1)
        return 2 * (tm * tk * a_size + tk * tn * b_size + tm * tn * sum(tile_sizes)) + acc

    tk = k
    while True:
        tm = _tile(m, MM_ROWS_MAX)
        while tm > 256 and need(tm, tk) > MM_VMEM_BUDGET:
            tm //= 2
        if need(tm, tk) <= MM_VMEM_BUDGET or tk <= 512:
            return tm, tn, tk
        tk //= 2


def _mm(a, b, mode, out_dtypes, name, epilogue=None, extras=(), after=None):
    if mode == "nn":
        (m, k), (k2, n) = a.shape, b.shape
    elif mode == "nt":
        (m, k), (n, k2) = a.shape, b.shape
    else:
        (k, m), (k2, n) = a.shape, b.shape
    assert k == k2, (a.shape, b.shape, mode)
    tm, tn, tk = _mm_tiles(m, n, k, a.dtype.itemsize, b.dtype.itemsize,
                           [e.dtype.itemsize for e in extras] + [jnp.dtype(dt).itemsize for dt in out_dtypes])
    nk = k // tk
    ne, no, nafter = len(extras), len(out_dtypes), int(after is not None)
    if epilogue is None:
        epilogue = lambda acc: (acc,)

    def body(*refs):
        a_ref, b_ref = refs[0], refs[1]
        ex = refs[2:2 + ne]
        outs = refs[2 + ne + nafter:2 + ne + nafter + no]
        part = lax.dot_general(a_ref[...].astype(BF16), b_ref[...].astype(BF16), _DIMS[mode],
                               preferred_element_type=F32)

        def finish(acc):
            for o_ref, val in zip(outs, epilogue(acc, *[e[...] for e in ex])):
                o_ref[...] = val.astype(o_ref.dtype)

        if nk == 1:
            finish(part)
        else:
            acc_ref = refs[-1]
            kk = pl.program_id(2)

            @pl.when(kk == 0)
            def _():
                acc_ref[...] = part

            @pl.when(kk > 0)
            def _():
                acc_ref[...] += part

            @pl.when(kk == nk - 1)
            def _():
                finish(acc_ref[...])

    if mode == "tn":
        a_spec = pl.BlockSpec((tk, tm), lambda i, j, kk: (kk, i))
    else:
        a_spec = pl.BlockSpec((tm, tk), lambda i, j, kk: (i, kk))
    if mode == "nt":
        b_spec = pl.BlockSpec((tn, tk), lambda i, j, kk: (j, kk))
    else:
        b_spec = pl.BlockSpec((tk, tn), lambda i, j, kk: (kk, j))
    o_spec = pl.BlockSpec((tm, tn), lambda i, j, kk: (i, j))
    res = pl.pallas_call(
        body,
        name=name,
        grid=(m // tm, n // tn, nk),
        in_specs=[a_spec, b_spec] + [o_spec] * ne + [pl.BlockSpec(memory_space=pl.ANY)] * nafter,
        out_specs=[o_spec] * no,
        out_shape=[jax.ShapeDtypeStruct((m, n), dt) for dt in out_dtypes],
        scratch_shapes=[pltpu.VMEM((tm, tn), F32)] if nk > 1 else [],
        compiler_params=_params("parallel", "parallel", "arbitrary"),
    )(a, b, *extras, *([after] if nafter else []))
    return res[0] if no == 1 else res


def _mm_rows(a, b, mode, out_dtypes, name, epilogue, extras=(), vectors=(), n_sums=0, after=None, more=()):
    assert mode in ("nn", "nt")
    (m, k), n = a.shape, (b.shape[1] if mode == "nn" else b.shape[0])
    tm = _tile(m, MM_ROWS_TILE)
    mt = m // tm
    pairs = [(a, b)] + list(more)
    np_ = 2 * len(pairs)
    ne, no, nafter = len(extras) + len(vectors), len(out_dtypes), int(after is not None)

    def body(*refs):
        ex = refs[np_:np_ + ne]
        outs = refs[np_ + ne + nafter:np_ + ne + nafter + no]
        sums = refs[np_ + ne + nafter + no:np_ + ne + nafter + no + n_sums]
        acc_ref = refs[-1]
        i = pl.program_id(0)

        @pl.when(i == 0)
        def _():
            acc_ref[1] = jnp.zeros((tm, n), F32)

        vals = epilogue(acc_ref[1 - i % 2], *[e[...] for e in ex])
        prods = [lax.dot_general(refs[p][...].astype(BF16), refs[p + 1][...].astype(BF16), _DIMS[mode],
                                 preferred_element_type=F32) for p in range(0, np_, 2)]
        acc = prods[0]
        for prod in prods[1:]:
            acc = acc + prod
        acc_ref[i % 2] = acc
        for o_ref, val in zip(outs, vals[:no]):
            o_ref[...] = val.astype(o_ref.dtype)
        for s_ref, val in zip(sums, vals[no:]):
            @pl.when(i <= 1)
            def _(s_ref=s_ref, val=val):
                s_ref[...] = val

            @pl.when(i > 1)
            def _(s_ref=s_ref, val=val):
                s_ref[...] += val

    ahead = lambda i: (jnp.minimum(i, mt - 1), 0)
    behind = lambda i: (jnp.maximum(i - 1, 0), 0)
    fixed = lambda i: (0, 0)
    row = pl.BlockSpec((tm, n), behind)
    vec = pl.BlockSpec((1, n), fixed)
    res = pl.pallas_call(
        body, name=name, grid=(mt + 1,),
        in_specs=([spec for pa, pb in pairs for spec in (pl.BlockSpec((tm, pa.shape[1]), ahead),
                                                         pl.BlockSpec(pb.shape, fixed))] + [row] * len(extras)
                  + [vec] * len(vectors) + [pl.BlockSpec(memory_space=pl.ANY)] * nafter),
        out_specs=[row] * no + [vec] * n_sums,
        out_shape=[jax.ShapeDtypeStruct((m, n), dt) for dt in out_dtypes] + [jax.ShapeDtypeStruct((1, n), F32)] * n_sums,
        scratch_shapes=[pltpu.VMEM((2, tm, n), F32)],
        compiler_params=_params("arbitrary"),
    )(*[x for pair in pairs for x in pair], *extras, *vectors, *([after] if nafter else []))
    return res[0] if no + n_sums == 1 else res


def _ep_residual_norm(acc, res, w):
    h = res + acc
    return h, _rms(h, w)


def _ep_norm_bwd(acc, x, dres, w):
    r = lax.rsqrt(jnp.mean(x * x, axis=-1, keepdims=True) + EPS)
    g = acc * w
    dx = dres + (r * g - x * (r * r * r * jnp.mean(g * x, axis=-1, keepdims=True)))
    return dx, dx, jnp.sum(acc * (x * r), axis=0, keepdims=True)


def _rms_fwd(x, w, name, tm=512):
    n, d = x.shape
    tm = _tile(n, tm)

    def body(x_ref, w_ref, y_ref):
        y_ref[...] = _rms(x_ref[...], w_ref[...]).astype(y_ref.dtype)

    return pl.pallas_call(
        body, name=name, grid=(n // tm,),
        in_specs=[pl.BlockSpec((tm, d), lambda i: (i, 0)), pl.BlockSpec((1, d), lambda i: (0, 0))],
        out_specs=pl.BlockSpec((tm, d), lambda i: (i, 0)),
        out_shape=jax.ShapeDtypeStruct((n, d), BF16),
        compiler_params=_params("arbitrary"),
    )(x, w)


def _loss_head(h, w, target, name, tm=512):
    n, d = h.shape
    tm = _tile(n, tm)

    def body(h_ref, w_ref, t_ref, dh_ref, dhb_ref, dw_ref, sq_ref):
        y, vjp = jax.vjp(_rms, h_ref[...], w_ref[...])
        err = y - t_ref[...]
        dh, dw = vjp(err * (1.0 / d))
        dh_ref[...] = dh
        dhb_ref[...] = dh.astype(dhb_ref.dtype)
        sq = jnp.sum(err * err, axis=0, keepdims=True)

        @pl.when(pl.program_id(0) == 0)
        def _():
            dw_ref[...] = dw
            sq_ref[...] = sq

        @pl.when(pl.program_id(0) > 0)
        def _():
            dw_ref[...] += dw
            sq_ref[...] += sq

        @pl.when(pl.program_id(0) == n // tm - 1)
        def _():
            total = jnp.sum(sq_ref[...], axis=1, keepdims=True) * (0.5 / d)
            sq_ref[...] = jnp.broadcast_to(total, sq_ref.shape)

    row = pl.BlockSpec((tm, d), lambda i: (i, 0))
    vec = pl.BlockSpec((1, d), lambda i: (0, 0))
    return pl.pallas_call(
        body, name=name, grid=(n // tm,),
        in_specs=[row, vec, row],
        out_specs=[row, row, vec, vec],
        out_shape=[jax.ShapeDtypeStruct((n, d), F32), jax.ShapeDtypeStruct((n, d), BF16),
                   jax.ShapeDtypeStruct((1, d), F32), jax.ShapeDtypeStruct((1, d), F32)],
        compiler_params=_params("arbitrary"),
    )(h, w, target)


def _lower_bounds(logits):
    sm = jax.nn.softmax(logits, axis=0)
    rows = [sm[0:1] * 0.0]
    for r in range(1, DEPTH):
        rows.append(rows[-1] + sm[r:r + 1])
    return jnp.concatenate(rows, axis=0)


def _lb_fwd(logits, name):
    def body(l_ref, o_ref):
        o_ref[...] = _lower_bounds(l_ref[...])

    return pl.pallas_call(body, name=name, out_shape=jax.ShapeDtypeStruct(logits.shape, F32))(logits)


def _lb_bwd(logits, dlb, name):
    def body(l_ref, d_ref, o_ref):
        _, vjp = jax.vjp(_lower_bounds, l_ref[...])
        (o_ref[...],) = vjp(d_ref[...])

    return pl.pallas_call(body, name=name, out_shape=jax.ShapeDtypeStruct(logits.shape, F32))(logits, dlb)


def _head_slice(h):
    return pl.ds(h * HEAD_DIM, HEAD_DIM)


_GD_HEADS = jax.vmap(_gd_head, in_axes=(0, 0, 0, 0, 0, 0, 0, 0, None))
_GD_HEADS_AGAIN = jax.vmap(_gd_head, in_axes=(0, 0, 0, 0, 0, 0, 0, 0, None, 0))


def _lane_blocks(width, block_body):
    def trip(j, carry):
        block_body(lambda base=0: pl.ds(pl.multiple_of(j * LANE_BLOCK + base, LANE_BLOCK), LANE_BLOCK))
        return carry

    lax.fori_loop(0, width // LANE_BLOCK, trip, 0, unroll=BLOCK_UNROLL)


def _row_blocks(rows, block_body):
    def trip(j, carry):
        block_body(pl.ds(pl.multiple_of(j * ROW_BLOCK, ROW_BLOCK), ROW_BLOCK))
        return carry

    lax.fori_loop(0, rows // ROW_BLOCK, trip, 0, unroll=BLOCK_UNROLL)


def _hg_pre_block(p_ref, lb_ref, sums_refs, q_sc, k_sc, v_sc, e_sc, at):
    sl = at()
    q_sc[:, sl], k_sc[:, sl], e_sc[:, sl] = _hg_pre(
        p_ref[:, sl].astype(F32), p_ref[:, at(D_MODEL)].astype(F32), lb_ref[:, sl], [r[...] for r in sums_refs])
    v_sc[:, sl] = p_ref[:, at(2 * D_MODEL)].astype(F32)


def _gd_xp(halo_ref, p_ref, sl, first_chunk):
    halo = jnp.where(first_chunk, 0.0, halo_ref[:, sl].astype(F32))
    return jnp.concatenate([halo, p_ref[:, sl].astype(F32)], axis=0)


def _stack_all(ref, first=0):
    return jnp.stack([ref[s, :, _head_slice(h + first)] for s in range(ref.shape[0]) for h in range(N_HEADS)])


def _unstack_all(ref, val, first=0):
    for s in range(ref.shape[0]):
        for h in range(N_HEADS):
            ref[s, :, _head_slice(h + first)] = val[s * N_HEADS + h].astype(ref.dtype)


def _gdn_fwd_all(projm, projab, cw, alog, dtb, onw, seqs, name):
    n = projm.shape[0]
    t = n // seqs
    nc = t // CHUNK
    d = D_MODEL
    per_halo = CHUNK // HALO
    nh = seqs * N_HEADS

    def body(p_ref, halo_ref, ab_ref, cw_ref, alog_ref, dtb_ref, onw_ref, o2_ref, st_all_ref, y_ref, dinv_ref,
             st_sc, c_sc, beta_sc, g_sc):
        first_chunk = pl.program_id(0) == 0

        @pl.when(first_chunk)
        def _():
            st_sc[...] = jnp.zeros_like(st_sc)

        for s in range(seqs):
            def conv(at, s=s):
                sl = at()
                y = _gd_conv(_gd_xp(halo_ref.at[s], p_ref.at[s], sl, first_chunk), cw_ref[:, sl])
                y_ref[s, :, sl] = y
                c_sc[s, :, sl] = jax.nn.silu(y)

            _lane_blocks(3 * d, conv)
            beta_sc[s], g_sc[s] = _gd_gates(ab_ref[s, :, 0:N_HEADS], ab_ref[s, :, N_HEADS:2 * N_HEADS],
                                            alog_ref[...], dtb_ref[...])
        st_all_ref[0] = st_sc[...]
        o2, st_sc[...], dinv_ref[0] = _GD_HEADS(
            st_sc[...], _stack_all(c_sc), _stack_all(c_sc, N_HEADS), _stack_all(c_sc, 2 * N_HEADS), _stack_all(beta_sc),
            _stack_all(g_sc), _stack_all(g_sc, N_HEADS), _stack_all(p_ref, 3 * N_HEADS).astype(F32), onw_ref[...])
        _unstack_all(o2_ref, o2)

    rows = lambda c: (0, c, 0)
    const = lambda c: (0, 0)
    per_chunk = lambda c: (c, 0, 0, 0)
    p3 = projm.reshape(seqs, t, 4 * d)
    o2, st_all, conv_y, dinv_all = pl.pallas_call(
        body, name=name, grid=(nc,),
        in_specs=[pl.BlockSpec((seqs, CHUNK, 4 * d), rows),
                  pl.BlockSpec((seqs, HALO, 3 * d), lambda c: (0, jnp.maximum(c * per_halo - 1, 0), 0)),
                  pl.BlockSpec((seqs, CHUNK, AB_PAD), rows),
                  pl.BlockSpec((CONV_K, 3 * d), const), pl.BlockSpec((1, N_HEADS), const),
                  pl.BlockSpec((1, N_HEADS), const), pl.BlockSpec((1, HEAD_DIM), const)],
        out_specs=[pl.BlockSpec((seqs, CHUNK, d), rows), pl.BlockSpec((1, nh, HEAD_DIM, HEAD_DIM), per_chunk),
                   pl.BlockSpec((seqs, CHUNK, 3 * d), rows), pl.BlockSpec((1, nh, CHUNK, CHUNK), per_chunk)],
        out_shape=[jax.ShapeDtypeStruct((seqs, t, d), BF16), jax.ShapeDtypeStruct((nc, nh, HEAD_DIM, HEAD_DIM), F32),
                   jax.ShapeDtypeStruct((seqs, t, 3 * d), F32), jax.ShapeDtypeStruct((nc, nh, CHUNK, CHUNK), F32)],
        scratch_shapes=[pltpu.VMEM((nh, HEAD_DIM, HEAD_DIM), F32), pltpu.VMEM((seqs, CHUNK, 3 * d), F32),
                        pltpu.VMEM((seqs, CHUNK, d), F32), pltpu.VMEM((seqs, CHUNK, 2 * d), F32)],
        compiler_params=_params("arbitrary"),
    )(p3, p3, projab.reshape(seqs, t, AB_PAD), cw, alog, dtb, onw)
    return o2.reshape(n, d), st_all, conv_y, dinv_all


def _gdn_bwd_all(projm, projab, conv_y, cw, alog, dtb, onw, st_all, dinv_all, do2, seqs, name):
    n = projm.shape[0]
    t = n // seqs
    nc = t // CHUNK
    d = D_MODEL
    per_halo = CHUNK // HALO
    nh = seqs * N_HEADS

    def body(p_ref, halo_ref, ab_ref, y_ref, cw_ref, alog_ref, dtb_ref, onw_ref, st_all_ref, dinv_ref, do2_ref,
             dp_ref, dab_ref, dcw_ref, dalog_ref, ddtb_ref, donw_ref,
             dst_sc, dhalo_sc, c_sc, beta_sc, g_sc, dc_sc, dbeta_sc, dg_sc):
        first = pl.program_id(0) == 0
        first_chunk = pl.program_id(0) == nc - 1

        @pl.when(first)
        def _():
            dst_sc[...] = jnp.zeros_like(dst_sc)
            dhalo_sc[...] = jnp.zeros_like(dhalo_sc)

        gates_vjps = []
        for s in range(seqs):
            def act(at, s=s):
                c_sc[s, :, at()] = jax.nn.silu(y_ref[s, :, at()])

            _lane_blocks(3 * d, act)
            (beta_sc[s], g_sc[s]), gates_vjp = jax.vjp(
                _gd_gates, ab_ref[s, :, 0:N_HEADS], ab_ref[s, :, N_HEADS:2 * N_HEADS], alog_ref[...], dtb_ref[...])
            gates_vjps.append(gates_vjp)

        dinv = dinv_ref[0]
        _, vjp = jax.vjp(
            lambda *a: _GD_HEADS_AGAIN(*a, dinv), st_all_ref[0], _stack_all(c_sc), _stack_all(c_sc, N_HEADS),
            _stack_all(c_sc, 2 * N_HEADS), _stack_all(beta_sc), _stack_all(g_sc), _stack_all(g_sc, N_HEADS),
            _stack_all(p_ref, 3 * N_HEADS).astype(F32), onw_ref[...])
        dst_sc[...], dq, dk, dv, dbeta, dg, ddiff, dgate, donw = vjp((_stack_all(do2_ref).astype(F32), dst_sc[...]))
        _unstack_all(dc_sc, dq)
        _unstack_all(dc_sc, dk, N_HEADS)
        _unstack_all(dc_sc, dv, 2 * N_HEADS)
        _unstack_all(dbeta_sc, dbeta)
        _unstack_all(dg_sc, dg)
        _unstack_all(dg_sc, ddiff, N_HEADS)
        _unstack_all(dp_ref, dgate, 3 * N_HEADS)

        dalog, ddtb = None, None
        for s in range(seqs):
            def conv_bwd(at, s=s):
                sl = at()
                dxp, dcw = _gd_conv_bwd(_gd_xp(halo_ref.at[s], p_ref.at[s], sl, first_chunk), cw_ref[:, sl],
                                        y_ref[s, :, sl], dc_sc[s, :, sl])
                dqkv = jnp.concatenate([dxp[HALO:CHUNK], dxp[CHUNK:HALO + CHUNK] + dhalo_sc[s, :, sl]], axis=0)
                dp_ref[s, :, sl] = dqkv.astype(dp_ref.dtype)
                dhalo_sc[s, :, sl] = dxp[0:HALO]

                if s > 0:
                    dcw_ref[:, sl] += dcw
                    return

                @pl.when(first)
                def _():
                    dcw_ref[:, sl] = dcw

                @pl.when(jnp.logical_not(first))
                def _():
                    dcw_ref[:, sl] += dcw

            _lane_blocks(3 * d, conv_bwd)
            da, db, dalog_s, ddtb_s = gates_vjps[s]((dbeta_sc[s], dg_sc[s]))
            dab_ref[s] = jnp.concatenate(
                [da, db, jnp.zeros((CHUNK, AB_PAD - 2 * N_HEADS), F32)], axis=1).astype(dab_ref.dtype)
            dalog = dalog_s if dalog is None else dalog + dalog_s
            ddtb = ddtb_s if ddtb is None else ddtb + ddtb_s

        @pl.when(first)
        def _():
            dalog_ref[...] = dalog
            ddtb_ref[...] = ddtb
            donw_ref[...] = donw

        @pl.when(jnp.logical_not(first))
        def _():
            dalog_ref[...] += dalog
            ddtb_ref[...] += ddtb
            donw_ref[...] += donw

    back = lambda c: nc - 1 - c
    rows = lambda c: (0, back(c), 0)
    const = lambda c: (0, 0)
    per_chunk = lambda c: (back(c), 0, 0, 0)
    small = [pl.BlockSpec((CONV_K, 3 * d), const), pl.BlockSpec((1, N_HEADS), const),
             pl.BlockSpec((1, N_HEADS), const), pl.BlockSpec((1, HEAD_DIM), const)]
    p3 = projm.reshape(seqs, t, 4 * d)
    dp, dab, dcw, dalog, ddtb, donw = pl.pallas_call(
        body, name=name, grid=(nc,),
        in_specs=[pl.BlockSpec((seqs, CHUNK, 4 * d), rows),
                  pl.BlockSpec((seqs, HALO, 3 * d), lambda c: (0, jnp.maximum(back(c) * per_halo - 1, 0), 0)),
                  pl.BlockSpec((seqs, CHUNK, AB_PAD), rows), pl.BlockSpec((seqs, CHUNK, 3 * d), rows)] + small + [
                  pl.BlockSpec((1, nh, HEAD_DIM, HEAD_DIM), per_chunk), pl.BlockSpec((1, nh, CHUNK, CHUNK), per_chunk),
                  pl.BlockSpec((seqs, CHUNK, d), rows)],
        out_specs=[pl.BlockSpec((seqs, CHUNK, 4 * d), rows), pl.BlockSpec((seqs, CHUNK, AB_PAD), rows)] + small,
        out_shape=[jax.ShapeDtypeStruct((seqs, t, 4 * d), BF16), jax.ShapeDtypeStruct((seqs, t, AB_PAD), BF16),
                   jax.ShapeDtypeStruct((CONV_K, 3 * d), F32), jax.ShapeDtypeStruct((1, N_HEADS), F32),
                   jax.ShapeDtypeStruct((1, N_HEADS), F32), jax.ShapeDtypeStruct((1, HEAD_DIM), F32)],
        scratch_shapes=[pltpu.VMEM((nh, HEAD_DIM, HEAD_DIM), F32), pltpu.VMEM((seqs, HALO, 3 * d), F32),
                        pltpu.VMEM((seqs, CHUNK, 3 * d), F32), pltpu.VMEM((seqs, CHUNK, d), F32),
                        pltpu.VMEM((seqs, CHUNK, 2 * d), F32), pltpu.VMEM((seqs, CHUNK, 3 * d), F32),
                        pltpu.VMEM((seqs, CHUNK, d), F32), pltpu.VMEM((seqs, CHUNK, 2 * d), F32)],
        compiler_params=_params("arbitrary"),
    )(p3, p3, projab.reshape(seqs, t, AB_PAD), conv_y, cw, alog, dtb, onw, st_all, dinv_all,
      do2.reshape(seqs, t, d))
    return dp.reshape(n, 4 * d), dab.reshape(n, AB_PAD), dcw, dalog, ddtb, donw


def _hgrn_fwd_all(proj, lb, gw, seqs, name):
    n = proj.shape[0]
    t = n // seqs
    nc = t // CHUNK
    d = D_MODEL
    nh = seqs * N_HEADS
    sums, masks = _hg_level_sums(), _hg_level_masks()

    def body(p_ref, lb_ref, gw_ref, sums_wide_ref, sums_once_ref, masks_ref, o2_ref, o_ref, st_all_ref,
             st_sc, q_sc, k_sc, v_sc, e_sc):
        @pl.when(pl.program_id(0) == 0)
        def _():
            st_sc[...] = jnp.zeros_like(st_sc)

        sums_refs = (sums_wide_ref, sums_once_ref)
        for s in range(seqs):
            _lane_blocks(d, functools.partial(_hg_pre_block, p_ref.at[s], lb_ref, sums_refs, q_sc.at[s], k_sc.at[s],
                                              v_sc.at[s], e_sc.at[s]))
        st_all_ref[0] = st_sc[...]
        for s in range(seqs):
            one, mine = pl.ds(s, 1), pl.ds(s * N_HEADS, N_HEADS)
            o, st_sc[mine] = _HG_HEADS(st_sc[mine], *[_stack_all(r.at[one]) for r in (q_sc, k_sc, v_sc, e_sc)],
                                       masks_ref[...])
            _unstack_all(o_ref.at[one], o)

            def post(rows, s=s):
                gate = p_ref[s, rows, 3 * d:4 * d].astype(F32)
                o2_ref[s, rows, :] = _hg_post(o_ref[s, rows, :], gate, gw_ref[...]).astype(o2_ref.dtype)

            _row_blocks(CHUNK, post)

    rows = lambda c: (0, c, 0)
    vec = pl.BlockSpec((1, d), lambda c: (0, 0))
    act = pl.BlockSpec((seqs, CHUNK, d), rows)
    o2, o, st_all = pl.pallas_call(
        body, name=name, grid=(nc,),
        in_specs=[pl.BlockSpec((seqs, CHUNK, 4 * d), rows), vec, vec]
        + [pl.BlockSpec(m.shape, lambda c: (0, 0)) for m in sums] + [pl.BlockSpec(masks.shape, lambda c: (0, 0, 0))],
        out_specs=[act, act, pl.BlockSpec((1, nh, HEAD_DIM, HEAD_DIM), lambda c: (c, 0, 0, 0))],
        out_shape=[jax.ShapeDtypeStruct((seqs, t, d), BF16), jax.ShapeDtypeStruct((seqs, t, d), F32),
                   jax.ShapeDtypeStruct((nc, nh, HEAD_DIM, HEAD_DIM), F32)],
        scratch_shapes=[pltpu.VMEM((nh, HEAD_DIM, HEAD_DIM), F32)] + [pltpu.VMEM((seqs, CHUNK, d), F32)] * 3
        + [pltpu.VMEM((seqs, sums[0].shape[0], d), F32)],
        compiler_params=_params("arbitrary"),
    )(proj.reshape(seqs, t, 4 * d), lb, gw, *sums, masks)
    return o2.reshape(n, d), o, st_all


def _hgrn_bwd_all(proj, lb, gw, st_all, o, do2, seqs, name):
    n = proj.shape[0]
    t = n // seqs
    nc = t // CHUNK
    d = D_MODEL
    nh = seqs * N_HEADS
    sums, masks = _hg_level_sums(), _hg_level_masks()

    def body(p_ref, lb_ref, gw_ref, sums_wide_ref, sums_once_ref, masks_ref, st_all_ref, o_ref, do2_ref,
             dp_ref, dlb_ref, dgw_ref,
             dst_sc, q_sc, k_sc, v_sc, e_sc, do_sc, dq_sc, dk_sc, dv_sc, de_sc, dgw_sc):
        first = pl.program_id(0) == 0

        @pl.when(first)
        def _():
            dst_sc[...] = jnp.zeros_like(dst_sc)

        sums_refs = (sums_wide_ref, sums_once_ref)
        dgw_sc[...] = jnp.zeros_like(dgw_sc)
        for s in range(seqs):
            _lane_blocks(d, functools.partial(_hg_pre_block, p_ref.at[s], lb_ref, sums_refs, q_sc.at[s], k_sc.at[s],
                                              v_sc.at[s], e_sc.at[s]))

            def post_bwd(rows, s=s):
                _, vjp = jax.vjp(_hg_post, o_ref[s, rows, :], p_ref[s, rows, 3 * d:4 * d].astype(F32), gw_ref[...])
                do_sc[s, rows, :], dgate, dgw = vjp(do2_ref[s, rows, :].astype(F32))
                dp_ref[s, rows, 3 * d:4 * d] = dgate.astype(dp_ref.dtype)
                dgw_sc[...] += dgw

            _row_blocks(CHUNK, post_bwd)

        level_masks = masks_ref[...]
        _, vjp = jax.vjp(lambda *a: _HG_HEADS(*a, level_masks), st_all_ref[0],
                         *[_stack_all(r) for r in (q_sc, k_sc, v_sc, e_sc)])
        grads = vjp((_stack_all(do_sc), dst_sc[...]))
        dst_sc[...] = grads[0]
        for r, val in zip((dq_sc, dk_sc, dv_sc, de_sc), grads[1:]):
            _unstack_all(r, val)

        for s in range(seqs):
            def pre_bwd(at, s=s):
                sl = at()
                level_sums = (sums_wide_ref[...], sums_once_ref[...])
                _, vjp = jax.vjp(lambda qraw, f, lb: _hg_pre(qraw, f, lb, level_sums), p_ref[s, :, sl].astype(F32),
                                 p_ref[s, :, at(d)].astype(F32), lb_ref[:, sl])
                dqraw, df, dlb = vjp((dq_sc[s, :, sl], dk_sc[s, :, sl], de_sc[s, :, sl]))
                dp_ref[s, :, sl] = dqraw.astype(dp_ref.dtype)
                dp_ref[s, :, at(d)] = df.astype(dp_ref.dtype)
                dp_ref[s, :, at(2 * d)] = dv_sc[s, :, sl].astype(dp_ref.dtype)
                if s > 0:
                    dlb_ref[:, sl] += dlb
                    return

                @pl.when(first)
                def _():
                    dlb_ref[:, sl] = dlb

                @pl.when(jnp.logical_not(first))
                def _():
                    dlb_ref[:, sl] += dlb

            _lane_blocks(d, pre_bwd)

        @pl.when(first)
        def _():
            dgw_ref[...] = dgw_sc[...]

        @pl.when(jnp.logical_not(first))
        def _():
            dgw_ref[...] += dgw_sc[...]

    rows = lambda c: (0, nc - 1 - c, 0)
    vec = pl.BlockSpec((1, d), lambda c: (0, 0))
    act = pl.BlockSpec((seqs, CHUNK, d), rows)
    wide = pl.BlockSpec((seqs, CHUNK, 4 * d), rows)
    e_rows = sums[0].shape[0]
    dp, dlb, dgw = pl.pallas_call(
        body, name=name, grid=(nc,),
        in_specs=[wide, vec, vec] + [pl.BlockSpec(m.shape, lambda c: (0, 0)) for m in sums] + [
                  pl.BlockSpec(masks.shape, lambda c: (0, 0, 0)),
                  pl.BlockSpec((1, nh, HEAD_DIM, HEAD_DIM), lambda c: (nc - 1 - c, 0, 0, 0)), act, act],
        out_specs=[wide, vec, vec],
        out_shape=[jax.ShapeDtypeStruct((seqs, t, 4 * d), BF16), jax.ShapeDtypeStruct((1, d), F32),
                   jax.ShapeDtypeStruct((1, d), F32)],
        scratch_shapes=[pltpu.VMEM((nh, HEAD_DIM, HEAD_DIM), F32)]
        + [pltpu.VMEM((seqs, CHUNK, d), F32)] * 3 + [pltpu.VMEM((seqs, e_rows, d), F32)]
        + [pltpu.VMEM((seqs, CHUNK, d), F32)] * 4 + [pltpu.VMEM((seqs, e_rows, d), F32), pltpu.VMEM((1, d), F32)],
        compiler_params=_params("arbitrary"),
    )(proj.reshape(seqs, t, 4 * d), lb, gw, *sums, masks, st_all, o, do2.reshape(seqs, t, d))
    return dp.reshape(n, 4 * d), dlb, dgw


def _adam_update(w, g, m, v):
    b1c = 1.0 - ADAM_B1 ** ADAM_STEP
    b2c = 1.0 - ADAM_B2 ** ADAM_STEP
    m_new = ADAM_B1 * m + (1.0 - ADAM_B1) * g
    v_new = ADAM_B2 * v + (1.0 - ADAM_B2) * (g * g)
    delta = -ADAM_LR * ((m_new / b1c) / (jnp.sqrt(v_new / b2c) + ADAM_EPS) + ADAM_WD * w)
    return delta, m_new, v_new


def _adamw(w, g, m, v, name, tr=256):
    r, c = w.shape
    tr = _tile(r, tr)

    def body(w_ref, g_ref, m_ref, v_ref, d_ref, mo_ref, vo_ref):
        d_ref[...], mo_ref[...], vo_ref[...] = _adam_update(w_ref[...], g_ref[...], m_ref[...], v_ref[...])

    blk = pl.BlockSpec((tr, c), lambda i: (i, 0))
    return pl.pallas_call(
        body, name=name, grid=(r // tr,),
        in_specs=[blk] * 4, out_specs=[blk] * 3,
        out_shape=[jax.ShapeDtypeStruct((r, c), F32)] * 3,
        compiler_params=_params("arbitrary"),
    )(w, g, m, v)


def _adamw_slots(w, slot_bufs, m, v, name, tr=256):
    nl, r, c = w.shape
    tr = _tile(r, tr)

    def body(*refs):
        w_ref = refs[0]
        g_refs = refs[1:1 + nl]
        m_ref, v_ref, go_ref, d_ref, mo_ref, vo_ref = refs[1 + nl:]
        for k in range(nl):
            @pl.when(pl.program_id(0) == k)
            def _(k=k):
                g = g_refs[k][0].astype(F32)
                for s in range(1, N_DEV):
                    g = g + g_refs[k][s].astype(F32)
                go_ref[0] = g

        d_ref[0], mo_ref[0], vo_ref[0] = _adam_update(w_ref[0], go_ref[0], m_ref[0], v_ref[0])

    blk = pl.BlockSpec((1, tr, c), lambda l, i: (l, i, 0))
    g_specs = [pl.BlockSpec((N_DEV, tr, c), lambda l, i, k=k: (0, jnp.where(l == k, i, 0), 0)) for k in range(nl)]
    return pl.pallas_call(
        body, name=name, grid=(nl, r // tr),
        in_specs=[blk] + g_specs + [blk, blk], out_specs=[blk] * 4,
        out_shape=[jax.ShapeDtypeStruct((nl, r, c), F32)] * 4,
        compiler_params=_params("arbitrary", "arbitrary"),
    )(w, *slot_bufs, m, v)


def _adamw_windows(w, lo_bufs, hi_bufs, end_bufs, me, m, v, name, tr=256):
    nl, r, c = w.shape
    wl, wh = lo_bufs[0].shape[2], hi_bufs[0].shape[2]
    width, step = wl + wh, c - wl
    assert step >= 0 and (N_DEV - 1) * step + c <= width and N_DEV == 8
    tr = _tile(r, tr)

    def body(*refs):
        me_ref, w_ref = refs[0], refs[1]
        lo_refs, hi_refs, end_refs = refs[2:2 + nl], refs[2 + nl:2 + 2 * nl], refs[2 + 2 * nl:2 + 3 * nl]
        m_ref, v_ref, go_ref, d_ref, mo_ref, vo_ref = refs[2 + 3 * nl:]
        for k in range(nl):
            @pl.when(pl.program_id(0) == k)
            def _(k=k):
                last = me_ref[0] == N_DEV - 1
                hi_of = lambda s: jnp.where(last, end_refs[k][s].astype(F32), hi_refs[k][s].astype(F32))
                lo, hi = lo_refs[k][0].astype(F32), hi_of(0)
                for s in range(1, N_DEV):
                    lo, hi = lo + lo_refs[k][s].astype(F32), hi + hi_of(s)
                g = jnp.concatenate([lo, hi], axis=1)
                for bit in range(3):
                    moved = pltpu.roll(g, width - (step << bit), axis=1)
                    g = jnp.where((me_ref[0] >> bit) & 1 == 1, moved, g)
                go_ref[0] = g[:, :c]

        d_ref[0], mo_ref[0], vo_ref[0] = _adam_update(w_ref[0], go_ref[0], m_ref[0], v_ref[0])

    blk = pl.BlockSpec((1, tr, c), lambda l, i: (l, i, 0))
    g_specs = [pl.BlockSpec((N_DEV, tr, cols), lambda l, i, k=k: (0, jnp.where(l == k, i, 0), 0))
               for cols in (wl, wh, wh) for k in range(nl)]
    return pl.pallas_call(
        body, name=name, grid=(nl, r // tr),
        in_specs=[pl.BlockSpec(memory_space=pltpu.SMEM), blk] + g_specs + [blk, blk], out_specs=[blk] * 4,
        out_shape=[jax.ShapeDtypeStruct((nl, r, c), F32)] * 4,
        compiler_params=_params("arbitrary", "arbitrary"),
    )(me, w, *lo_bufs, *hi_bufs, *end_bufs, m, v)


def _window(w, me, width, step, name, tr=256):
    r, c = w.shape
    assert (N_DEV - 1) * step + c <= width and N_DEV == 8
    tr = _tile(r, tr)

    def body(me_ref, w_ref, out_ref, wide):
        wide[...] = jnp.zeros_like(wide)
        wide[:, 0:c] = w_ref[...]
        g = wide[...]
        for bit in range(3):
            moved = pltpu.roll(g, step << bit, axis=1)
            g = jnp.where((me_ref[0] >> bit) & 1 == 1, moved, g)
        out_ref[...] = g.astype(out_ref.dtype)

    return pl.pallas_call(
        body, name=name, grid=(r // tr,),
        in_specs=[pl.BlockSpec(memory_space=pltpu.SMEM), pl.BlockSpec((tr, c), lambda i: (i, 0))],
        out_specs=pl.BlockSpec((tr, width), lambda i: (i, 0)),
        out_shape=jax.ShapeDtypeStruct((r, width), BF16),
        scratch_shapes=[pltpu.VMEM((tr, width), F32)],
        compiler_params=_params("arbitrary"),
    )(me, w)


def _unshard_windows(win, c, name, tr=256):
    nd, r, w = win.shape
    wl = w - AB_PAD
    step = c - wl
    assert 0 <= step and nd * step <= AB_PAD
    tr = _tile(r, tr)

    def body(win_ref, main_ref, tail_ref):
        lane = lax.broadcasted_iota(jnp.int32, (tr, AB_PAD), 1)

        def past(s):
            return win_ref[s, :, wl:w].astype(F32)

        for s in range(nd):
            first = win_ref[s, :, 0:AB_PAD].astype(F32)
            if s > 0:
                first = jnp.where(lane < s * step, past(s - 1), first)
            main_ref[:, s * wl:s * wl + AB_PAD] = first.astype(main_ref.dtype)
            main_ref[:, s * wl + AB_PAD:(s + 1) * wl] = win_ref[s, :, AB_PAD:wl]
        tail_ref[...] = jnp.where(lane < nd * step, past(nd - 1), 0.0).astype(tail_ref.dtype)

    return pl.pallas_call(
        body, name=name, grid=(r // tr,),
        in_specs=[pl.BlockSpec((nd, tr, w), lambda i: (0, i, 0))],
        out_specs=[pl.BlockSpec((tr, nd * wl), lambda i: (i, 0)), pl.BlockSpec((tr, AB_PAD), lambda i: (i, 0))],
        out_shape=[jax.ShapeDtypeStruct((r, nd * wl), win.dtype), jax.ShapeDtypeStruct((r, AB_PAD), win.dtype)],
        compiler_params=_params("arbitrary"),
    )(win)


def _mesh_pos():
    return lax.axis_index("x"), lax.axis_index("y"), lax.axis_index("c")


def _flip(pos, p):
    x, y, c = pos
    return ((1 - x) if p & 4 else x, (1 - y) if p & 2 else y, (1 - c) if p & 1 else c)


def _lin(pos):
    return 4 * pos[0] + 2 * pos[1] + pos[2]


_HBM = pl.BlockSpec(memory_space=pltpu.HBM)
_SEM = pl.BlockSpec(memory_space=pltpu.SEMAPHORE)
_DATAFLOW = pltpu.SideEffectType.DATAFLOW_SIDE_EFFECTING


class _Item:
    def __init__(self, src, land_shape, src_pick, dst_pick, peers=tuple(range(1, N_DEV))):
        self.src, self.land_shape, self.src_pick, self.dst_pick = src, land_shape, src_pick, dst_pick
        self.peers = peers


def _distinct(arrays):
    found, where = [], []
    for a in arrays:
        hits = [k for k, f in enumerate(found) if f is a]
        where.append(hits[0] if hits else len(found))
        if not hits:
            found.append(a)
    return found, where


def _remote_copies(items, src, land, send_sem, recv_sem, me, arriving):
    me_i = _lin(me)
    out = []
    for it, s_ref, l_ref in zip(items, src, land):
        for p in it.peers:
            peer = _flip(me, p)
            out.append(pltpu.make_async_remote_copy(
                src_ref=it.src_pick(s_ref, _lin(peer)),
                dst_ref=it.dst_pick(l_ref, _lin(peer) if arriving else me_i),
                send_sem=send_sem, recv_sem=recv_sem, device_id=peer, device_id_type=pl.DeviceIdType.MESH))
    return out


def _own_copies(items, src, land, sem, me):
    me_i = _lin(me)
    return [pltpu.make_async_copy(it.src_pick(s_ref, me_i), it.dst_pick(l_ref, me_i), sem)
            for it, s_ref, l_ref in zip(items, src, land)]


def _exchange_start(groups, name):
    items = [it for g in groups for it in g]
    n, ng = len(items), len(groups)
    first = [sum(len(g) for g in groups[:gi]) for gi in range(ng)]
    arrays, where = _distinct([it.src for it in items])
    nu = len(arrays)

    def body(*refs):
        src, land = [refs[k] for k in where], refs[nu:nu + n]
        send_sems, recv_sems = refs[nu + n:nu + n + ng], refs[nu + n + ng:nu + n + 2 * ng]
        token = refs[2 * (nu + n) + 2 * ng]
        me = _mesh_pos()
        for gi, g in enumerate(groups):
            sl = slice(first[gi], first[gi] + len(g))
            for cp in _remote_copies(g, src[sl], land[sl], send_sems[gi], recv_sems[gi], me, arriving=False):
                cp.start()
            for cp in _own_copies(g, src[sl], land[sl], recv_sems[gi], me):
                cp.start()
        token[...] = jnp.zeros_like(token)

    srcs = [pltpu.with_memory_space_constraint(a, pltpu.HBM) for a in arrays]
    lands = [pltpu.with_memory_space_constraint(lax.empty(it.land_shape, it.src.dtype), pltpu.HBM) for it in items]
    res = pl.pallas_call(
        body, name=name,
        out_shape=([pltpu.SemaphoreType.DMA(())] * (2 * ng)
                   + [pltpu.HBM(a.shape, a.dtype) for a in arrays]
                   + [pltpu.HBM(it.land_shape, it.src.dtype) for it in items]
                   + [jax.ShapeDtypeStruct((8, 128), F32)]),
        in_specs=[_HBM] * (nu + n),
        out_specs=[_SEM] * (2 * ng) + [_HBM] * (nu + n) + [pl.BlockSpec(memory_space=pltpu.VMEM)],
        input_output_aliases={i: 2 * ng + i for i in range(nu + n)},
        compiler_params=pltpu.CompilerParams(has_side_effects=_DATAFLOW),
    )(*srcs, *lands)
    send_sems, recv_sems = res[0:ng], res[ng:2 * ng]
    src_thru, land_thru = [res[2 * ng + k] for k in where], res[2 * ng + nu:2 * ng + nu + n]
    handles = []
    for gi, g in enumerate(groups):
        sl = slice(first[gi], first[gi] + len(g))
        handles.append((g, src_thru[sl], land_thru[sl], send_sems[gi], recv_sems[gi]))
    return handles, res[-1]


def _exchange_wait(handle, after, name):
    items, src_thru, land_thru, send_sem, recv_sem = handle
    k = len(items)
    arrays, where = _distinct(src_thru)
    nu = len(arrays)
    afters = list(after) if isinstance(after, (list, tuple)) else [after]

    def body(*refs):
        src, land = [refs[u] for u in where], refs[nu:nu + k]
        send_ref, recv_ref = refs[nu + k], refs[nu + k + 1]
        for cp in _remote_copies(items, src, land, send_ref, recv_ref, _mesh_pos(), arriving=True):
            cp.wait_send()
            cp.wait_recv()
        for cp in _own_copies(items, src, land, recv_ref, _mesh_pos()):
            cp.wait()

    res = pl.pallas_call(
        body, name=name,
        out_shape=([pltpu.HBM(s.shape, s.dtype) for s in arrays] + [pltpu.HBM(l.shape, l.dtype) for l in land_thru]),
        in_specs=[_HBM] * (nu + k) + [_SEM, _SEM] + [pl.BlockSpec(memory_space=pl.ANY)] * len(afters),
        out_specs=[_HBM] * (nu + k),
        input_output_aliases={i: i for i in range(nu + k)},
        compiler_params=pltpu.CompilerParams(has_side_effects=_DATAFLOW),
    )(*arrays, *land_thru, send_sem, recv_sem, *afters)
    return res[nu:nu + k]


SAME_CORE = (2, 4, 6)
SIBLING = 1


def _pass_on_start(buf, name):
    def body(buf_ref, send_sem, recv_sem, thru_ref):
        me = _mesh_pos()
        for p in SAME_CORE:
            slot = buf_ref.at[_lin(_flip(me, p))]
            pltpu.make_async_remote_copy(src_ref=slot, dst_ref=slot, send_sem=send_sem, recv_sem=recv_sem,
                                         device_id=_flip(me, SIBLING), device_id_type=pl.DeviceIdType.MESH).start()

    return pl.pallas_call(
        body, name=name,
        out_shape=[pltpu.SemaphoreType.DMA(()), pltpu.SemaphoreType.DMA(()), pltpu.HBM(buf.shape, buf.dtype)],
        in_specs=[_HBM], out_specs=[_SEM, _SEM, _HBM], input_output_aliases={0: 2},
        compiler_params=pltpu.CompilerParams(has_side_effects=_DATAFLOW),
    )(pltpu.with_memory_space_constraint(buf, pltpu.HBM))


def _pass_on_wait(handle, name):
    send_sem, recv_sem, thru = handle

    def body(buf_ref, send_ref, recv_ref, out_ref):
        me = _mesh_pos()
        sibling = _flip(me, SIBLING)
        for p in SAME_CORE:
            mine, theirs = buf_ref.at[_lin(_flip(me, p))], buf_ref.at[_lin(_flip(sibling, p))]
            cp = pltpu.make_async_remote_copy(src_ref=mine, dst_ref=theirs, send_sem=send_ref, recv_sem=recv_ref,
                                              device_id=sibling, device_id_type=pl.DeviceIdType.MESH)
            cp.wait_send()
            cp.wait_recv()

    return pl.pallas_call(
        body, name=name, out_shape=pltpu.HBM(thru.shape, thru.dtype),
        in_specs=[_HBM, _SEM, _SEM], out_specs=_HBM, input_output_aliases={0: 0},
        compiler_params=pltpu.CompilerParams(has_side_effects=_DATAFLOW),
    )(thru, send_sem, recv_sem)


def _whole(ref, i):
    return ref


def _slot(ref, i):
    return ref.at[i]


def _rows_of(r):
    return lambda ref, i: ref.at[pl.ds(pl.multiple_of(i * r, r), r), :]


def _cols_of(c):
    return lambda ref, i: ref.at[:, pl.ds(pl.multiple_of(i * c, c), c)]


def _all_reduce_small(buf, after, name):
    r, c = buf.shape

    def body(src_ref, after_ref, out_ref, all_ref, send_sems, recv_sems):
        me = _mesh_pos()
        me_i = _lin(me)
        all_ref[me_i] = src_ref[...]
        for p in range(1, N_DEV):
            peer = _flip(me, p)
            pltpu.make_async_remote_copy(
                src_ref=src_ref, dst_ref=all_ref.at[me_i], send_sem=send_sems.at[p - 1], recv_sem=recv_sems.at[p - 1],
                device_id=peer, device_id_type=pl.DeviceIdType.MESH).start()
        for p in range(1, N_DEV):
            peer = _flip(me, p)
            cp = pltpu.make_async_remote_copy(
                src_ref=src_ref, dst_ref=all_ref.at[_lin(peer)], send_sem=send_sems.at[p - 1],
                recv_sem=recv_sems.at[p - 1], device_id=peer, device_id_type=pl.DeviceIdType.MESH)
            cp.wait_recv()
            cp.wait_send()
        acc = all_ref[0]
        for s in range(1, N_DEV):
            acc = acc + all_ref[s]
        out_ref[...] = acc

    vm = pl.BlockSpec(memory_space=pltpu.VMEM)
    return pl.pallas_call(
        body, name=name, in_specs=[vm, pl.BlockSpec(memory_space=pl.ANY)], out_specs=vm,
        out_shape=jax.ShapeDtypeStruct((r, c), F32),
        scratch_shapes=[pltpu.VMEM((N_DEV, r, c), F32), pltpu.SemaphoreType.DMA((N_DEV - 1,)),
                        pltpu.SemaphoreType.DMA((N_DEV - 1,))],
        compiler_params=pltpu.CompilerParams(has_side_effects=True),
    )(buf, after)


def _unshard_cols(g):
    s, l, r, c = g.shape
    return jnp.transpose(g, (1, 2, 0, 3)).reshape(l, r, s * c)


def kernel(x, gdn_w_in, gdn_conv, gdn_a_log, gdn_dt_bias, gdn_onorm, gdn_w_out, hgrn_w_in, hgrn_lb_logits, hgrn_gnorm, hgrn_w_out, norm_mix, norm_mlp, mlp_w_up, mlp_w_down, norm_final, loss_target, m_gdn_w_in, m_gdn_conv, m_gdn_a_log, m_gdn_dt_bias, m_gdn_onorm, m_gdn_w_out, m_hgrn_w_in, m_hgrn_lb_logits, m_hgrn_gnorm, m_hgrn_w_out, m_norm_mix, m_norm_mlp, m_mlp_w_up, m_mlp_w_down, m_norm_final, v_gdn_w_in, v_gdn_conv, v_gdn_a_log, v_gdn_dt_bias, v_gdn_onorm, v_gdn_w_out, v_hgrn_w_in, v_hgrn_lb_logits, v_hgrn_gnorm, v_hgrn_w_out, v_norm_mix, v_norm_mlp, v_mlp_w_up, v_mlp_w_down, v_norm_final):
    seqs, seq_len, d = x.shape
    n = seqs * seq_len
    me_i = _lin(_mesh_pos())
    x2 = x.reshape(n, d)
    target = loss_target.reshape(n, d)
    n_gdn, n_hgrn = gdn_w_in.shape[0], hgrn_w_in.shape[0]

    r_out, r_down = gdn_w_out.shape[1], mlp_w_down.shape[1]
    c_gin, c_hin, c_up = gdn_w_in.shape[2], hgrn_w_in.shape[2], mlp_w_up.shape[2]

    def gathered(w, pick, land_shape, **kw):
        return _Item(w.astype(BF16), land_shape, _whole, pick, **kw)

    wl = GDN_MAIN // N_DEV

    me_1 = me_i.astype(jnp.int32).reshape(1)

    def next_tile(ref, i):
        return ref.at[:, pl.ds(pl.multiple_of(jnp.minimum(i + 1, N_DEV - 1) * wl, AB_PAD), AB_PAD)]

    groups = [[_Item(gdn_conv, (N_DEV,) + gdn_conv.shape, _whole, _slot),
               _Item(hgrn_gnorm, (N_DEV,) + hgrn_gnorm.shape, _whole, _slot)]]
    for i in range(DEPTH):
        j = i // 2
        if i % 2 == 0:
            direct = (SIBLING,) + SAME_CORE if i == 0 else tuple(range(1, N_DEV))
            win = _window(gdn_w_in[j], me_1, wl + AB_PAD, c_gin - wl, f"window_in_{i}")
            groups += [[_Item(win, (N_DEV, d, wl + AB_PAD), _whole, _slot, peers=direct)],
                       [gathered(gdn_w_out[j], _rows_of(r_out), (N_DEV * r_out, d))]]
        else:
            groups += [[gathered(hgrn_w_in[j], _cols_of(c_hin), (d, N_DEV * c_hin))],
                       [gathered(hgrn_w_out[j], _rows_of(r_out), (N_DEV * r_out, d))]]
        groups += [[gathered(mlp_w_up[i], _cols_of(c_up), (d, N_DEV * c_up))],
                   [gathered(mlp_w_down[i], _rows_of(r_down), (N_DEV * r_down, d))]]
    gather_handles, token = _exchange_start(groups, "gather_start")
    lbs = _lb_fwd(hgrn_lb_logits + token[0:1, 0:1], "lb_fwd")

    def arrived(k, after, name):
        return _exchange_wait(gather_handles[k], after, "gather_wait_" + name)

    saved = []
    w_in, w_ab, w_out, w_up, w_down = ([None] * DEPTH for _ in range(5))
    h = x2
    for i in range(DEPTH):
        j = i // 2
        if i == 0:
            g_conv, g_gnorm = arrived(0, h, "small")
            conv_full = _unshard_cols(g_conv)
            gnorm_full = jnp.transpose(g_gnorm, (1, 0, 2)).reshape(n_hgrn, d)
        if i == 0:
            y = _rms_fwd(h, norm_mix[0:1] + token[0:1, 0:1], "rms_mix_0")
        (w_in[i],) = arrived(1 + 4 * i, [y, lbs, conv_full, gnorm_full] if i == 0 else y, f"in_{i}")
        if i == 0:
            w_in[i] = _pass_on_wait(_pass_on_start(w_in[i], "pass_on_start_in_0"), "pass_on_wait_in_0")
        if i % 2 == 0:
            w_in[i], w_ab[i] = _unshard_windows(w_in[i], c_gin, f"gdn_w_in_{i}")
            projm = _mm(y, w_in[i], "nn", [BF16], f"gdn_proj_{i}")
            projab = _mm(y, w_ab[i], "nn", [F32], f"gdn_proj_ab_{i}")
            o2, st_all, conv_y, dinv_all = _gdn_fwd_all(projm, projab, conv_full[j], gdn_a_log[j:j + 1],
                                                    gdn_dt_bias[j:j + 1], gdn_onorm[j:j + 1], seqs, f"gdn_fwd_{i}")
            mix = (projm, projab, conv_y, st_all, dinv_all)
        else:
            proj = _mm(y, w_in[i], "nn", [BF16], f"hgrn_proj_{i}")
            o2, o_raw, st_all = _hgrn_fwd_all(proj, lbs[i:i + 1], gnorm_full[j:j + 1], seqs, f"hgrn_fwd_{i}")
            mix = (proj, o_raw, st_all)
        (w_out[i],) = arrived(2 + 4 * i, o2, f"out_{i}")
        h1, y2 = _mm_rows(o2, w_out[i], "nn", [F32, BF16], f"mix_out_{i}", epilogue=_ep_residual_norm, extras=(h,),
                     vectors=(norm_mlp[i:i + 1],))
        (w_up[i],) = arrived(3 + 4 * i, y2, f"up_{i}")
        u, act = _mm(y2, w_up[i], "nn", [BF16, BF16], f"mlp_up_{i}",
                     epilogue=lambda acc: (acc, jnp.square(jnp.maximum(acc, 0.0))))
        (w_down[i],) = arrived(4 + 4 * i, act, f"down_{i}")
        saved.append((h, y, mix, o2, h1, y2, u, act))
        if i + 1 < DEPTH:
            h, y = _mm_rows(act, w_down[i], "nn", [F32, BF16], f"mlp_down_{i}", epilogue=_ep_residual_norm, extras=(h1,),
                       vectors=(norm_mix[i + 1:i + 2],))
        else:
            h = _mm(act, w_down[i], "nn", [F32], f"mlp_down_{i}", epilogue=lambda acc, res: (res + acc,),
                    extras=(h1,))

    dh, dh_b, d_nf, sq = _loss_head(h, norm_final.reshape(1, d), target, "loss_head")

    d_nmix, d_nmlp = [None] * DEPTH, [None] * DEPTH
    d_conv, d_alog, d_dtb, d_onorm = [None] * n_gdn, [None] * n_gdn, [None] * n_gdn, [None] * n_gdn
    d_lb = [jnp.zeros((1, d), F32)] * DEPTH
    d_gnorm = [None] * n_hgrn
    mlp_handles, mix_handles = [None] * DEPTH, [None] * DEPTH
    token, pending = None, []
    for i in reversed(range(DEPTH)):
        j = i // 2
        h_in, y, mix, o2, h1, y2, u, act = saved[i]
        g_down = _mm(act, dh_b, "tn", [BF16], f"g_down_{i}", after=token)
        du = _mm(dh_b, w_down[i], "nt", [BF16], f"d_u_{i}",
                 epilogue=lambda acc, uu: (acc * (2.0 * jnp.maximum(uu.astype(F32), 0.0)),), extras=(u,))
        g_up = _mm(y2, du, "tn", [BF16], f"g_up_{i}")
        handles, token = _exchange_start(
            [[_Item(g_down, (N_DEV, r_down, d), _rows_of(r_down), _slot)],
             [_Item(g_up, (N_DEV, d, c_up), _cols_of(c_up), _slot)]] + pending, f"scatter_start_mlp_{i}")
        mlp_handles[i] = handles[:2]
        if pending:
            mix_handles[i + 1] = handles[2:]
        dh1, dh1_b, d_nmlp[i] = _mm_rows(du, w_up[i], "nt", [F32, BF16], f"d_y2_{i}", epilogue=_ep_norm_bwd,
                                     extras=(h1, dh), vectors=(norm_mlp[i:i + 1],), n_sums=1, after=token)
        g_out = _mm(o2, dh1_b, "tn", [BF16], f"g_out_{i}")
        do2 = _mm(dh1_b, w_out[i], "nt", [BF16], f"d_o2_{i}")
        if i % 2 == 0:
            projm, projab, conv_y, st_all, dinv_all = mix
            dpm, dpab, d_conv[j], d_alog[j], d_dtb[j], d_onorm[j] = _gdn_bwd_all(
                projm, projab, conv_y, conv_full[j], gdn_a_log[j:j + 1], gdn_dt_bias[j:j + 1], gdn_onorm[j:j + 1],
                st_all, dinv_all, do2, seqs, f"gdn_bwd_{i}")
            g_main = _mm(y, dpm, "tn", [BF16], f"g_in_{i}")
            g_ab = _mm(y, dpab, "tn", [BF16], f"g_in_ab_{i}")
            in_items = [_Item(g_main, (N_DEV, d, wl), _cols_of(wl), _slot),
                        _Item(g_main, (N_DEV, d, AB_PAD), next_tile, _slot),
                        _Item(g_ab, (N_DEV, d, AB_PAD), _whole, _slot)]
            dp, dy_more = dpm, [(dpab, w_ab[i])]
        else:
            proj, o_raw, st_all = mix
            dp, d_lb[i], d_gnorm[j] = _hgrn_bwd_all(proj, lbs[i:i + 1], gnorm_full[j:j + 1], st_all, o_raw, do2,
                                               seqs, f"hgrn_bwd_{i}")
            g_in = _mm(y, dp, "tn", [BF16], f"g_in_{i}")
            in_items = [_Item(g_in, (N_DEV, d, c_hin), _cols_of(c_hin), _slot)]
            dy_more = []
        pending, token = [[_Item(g_out, (N_DEV, r_out, d), _rows_of(r_out), _slot)], in_items], None
        if i == 0:
            mix_handles[i], token = _exchange_start(pending, f"scatter_start_mix_{i}")
        dh, dh_b, d_nmix[i] = _mm_rows(dp, w_in[i], "nt", [F32, BF16], f"d_y_{i}", epilogue=_ep_norm_bwd,
                                  extras=(h_in, dh1), vectors=(norm_mix[i:i + 1],), n_sums=1, after=token, more=dy_more)
        token = None
    grad_x = dh.reshape(x.shape)

    def landed(handles, k, layers, after, name):
        return [_exchange_wait(handles[i][k], after, f"scatter_wait_{name}_{i}")[0] for i in layers]

    every, even, odd = range(DEPTH), range(0, DEPTH, 2), range(1, DEPTH, 2)
    upd = {}
    upd["mlp_w_down"] = _adamw_slots(mlp_w_down, landed(mlp_handles, 0, every, dh, "down"), m_mlp_w_down,
                                     v_mlp_w_down, "adamw_mlp_w_down")
    upd["mlp_w_up"] = _adamw_slots(mlp_w_up, landed(mlp_handles, 1, every, upd["mlp_w_down"][1], "up"), m_mlp_w_up,
                                   v_mlp_w_up, "adamw_mlp_w_up")
    upd["hgrn_w_out"] = _adamw_slots(hgrn_w_out, landed(mix_handles, 0, odd, upd["mlp_w_up"][1], "out"),
                                     m_hgrn_w_out, v_hgrn_w_out, "adamw_hgrn_w_out")
    upd["hgrn_w_in"] = _adamw_slots(hgrn_w_in, landed(mix_handles, 1, odd, upd["hgrn_w_out"][1], "in"), m_hgrn_w_in,
                                    v_hgrn_w_in, "adamw_hgrn_w_in")

    dlb_rows = jnp.concatenate(d_lb, axis=0)
    tail = jnp.concatenate(
        [jnp.concatenate(d_onorm, axis=1), jnp.concatenate(d_alog, axis=1), jnp.concatenate(d_dtb, axis=1)], axis=1)
    tail = jnp.pad(tail, ((0, 0), (0, d - tail.shape[1])))
    conv_rows = jnp.stack(d_conv).reshape(-1, d)
    packed = jnp.concatenate(
        [jnp.concatenate(d_nmix, axis=0), jnp.concatenate(d_nmlp, axis=0), d_nf, sq, dlb_rows,
         jnp.concatenate(d_gnorm, axis=0), tail, conv_rows], axis=0)
    pad_rows = (-packed.shape[0]) % 8
    packed = jnp.pad(packed, ((0, pad_rows), (0, 0)))
    tot = _all_reduce_small(packed, upd["hgrn_w_in"][1], "reduce_small")

    upd["gdn_w_out"] = _adamw_slots(gdn_w_out, landed(mix_handles, 0, even, tot, "out"),
                                    m_gdn_w_out, v_gdn_w_out, "adamw_gdn_w_out")
    windows = [_exchange_wait(mix_handles[i][1], upd["gdn_w_out"][1], f"scatter_wait_in_{i}") for i in even]
    upd["gdn_w_in"] = _adamw_windows(gdn_w_in, *([win[k] for win in windows] for k in range(3)),
                                     me_1, m_gdn_w_in, v_gdn_w_in, "adamw_gdn_w_in")

    def update(name, w, g, m, v):
        shape = w.shape
        c = shape[-1]
        res = _adamw(w.reshape(-1, c), g.reshape(-1, c), m.reshape(-1, c), v.reshape(-1, c), "adamw_" + name)
        return [g.reshape(shape)] + [o.reshape(shape) for o in res]

    r0 = 0
    g_nmix = tot[r0:r0 + DEPTH]; r0 += DEPTH
    g_nmlp = tot[r0:r0 + DEPTH]; r0 += DEPTH
    g_nf = tot[r0]; r0 += 1
    loss = tot[r0, 0]; r0 += 1
    g_lb = _lb_bwd(hgrn_lb_logits, tot[r0:r0 + DEPTH], "lb_bwd"); r0 += DEPTH
    g_gnorm_full = tot[r0:r0 + n_hgrn]; r0 += n_hgrn
    t_row = tot[r0]; r0 += 1
    g_conv_full = tot[r0:r0 + n_gdn * CONV_K * 3].reshape(n_gdn, CONV_K, 3 * d)
    g_onorm = t_row[0:n_gdn * HEAD_DIM].reshape(n_gdn, HEAD_DIM)
    o1 = n_gdn * HEAD_DIM
    g_alog = t_row[o1:o1 + n_gdn * N_HEADS].reshape(n_gdn, N_HEADS)
    g_dtb = t_row[o1 + n_gdn * N_HEADS:o1 + 2 * n_gdn * N_HEADS].reshape(n_gdn, N_HEADS)
    c_gn, c_cv = hgrn_gnorm.shape[1], gdn_conv.shape[2]
    g_gnorm = lax.dynamic_slice_in_dim(g_gnorm_full, me_i * c_gn, c_gn, axis=1)
    g_conv = lax.dynamic_slice_in_dim(g_conv_full, me_i * c_cv, c_cv, axis=2)

    upd["gdn_conv"] = update("gdn_conv", gdn_conv, g_conv, m_gdn_conv, v_gdn_conv)
    upd["gdn_a_log"] = update("gdn_a_log", gdn_a_log, g_alog, m_gdn_a_log, v_gdn_a_log)
    upd["gdn_dt_bias"] = update("gdn_dt_bias", gdn_dt_bias, g_dtb, m_gdn_dt_bias, v_gdn_dt_bias)
    upd["gdn_onorm"] = update("gdn_onorm", gdn_onorm, g_onorm, m_gdn_onorm, v_gdn_onorm)
    upd["hgrn_lb_logits"] = update("hgrn_lb_logits", hgrn_lb_logits, g_lb, m_hgrn_lb_logits, v_hgrn_lb_logits)
    upd["hgrn_gnorm"] = update("hgrn_gnorm", hgrn_gnorm, g_gnorm, m_hgrn_gnorm, v_hgrn_gnorm)
    upd["norm_mix"] = update("norm_mix", norm_mix, g_nmix, m_norm_mix, v_norm_mix)
    upd["norm_mlp"] = update("norm_mlp", norm_mlp, g_nmlp, m_norm_mlp, v_norm_mlp)
    upd["norm_final"] = update("norm_final", norm_final, g_nf, m_norm_final, v_norm_final)

    order = ["gdn_w_in", "gdn_conv", "gdn_a_log", "gdn_dt_bias", "gdn_onorm", "gdn_w_out", "hgrn_w_in",
             "hgrn_lb_logits", "hgrn_gnorm", "hgrn_w_out", "norm_mix", "norm_mlp", "mlp_w_up", "mlp_w_down",
             "norm_final"]
    outs = [loss, grad_x]
    for k in range(4):
        outs += [upd[name][k] for name in order]
    return tuple(outs)
```
